```python
import jax, jax.numpy as jnp
from jax import lax
import numpy as np

D_MODEL = 1024
BATCH = 8
SEQ = 4096
DEPTH = 1

CHUNK = 64
MIX_WIDTH = D_MODEL
HGRN_WIDTH = MIX_WIDTH // 2
CONV_WIDTH = MIX_WIDTH - HGRN_WIDTH
HGRN_HEAD_DIM = 128
HGRN_HEADS = HGRN_WIDTH // HGRN_HEAD_DIM
CONV_K = 3
D_FF = 4 * D_MODEL
ALPHA = (2 * DEPTH) ** 0.25
BETA = (8 * DEPTH) ** -0.25
EPS = 1e-5
IN_COLS = 4 * HGRN_WIDTH + 3 * CONV_WIDTH
SPLITS = (HGRN_WIDTH, 2 * HGRN_WIDTH, 3 * HGRN_WIDTH, 4 * HGRN_WIDTH,
          4 * HGRN_WIDTH + CONV_WIDTH, 4 * HGRN_WIDTH + 2 * CONV_WIDTH)

kernel_name = "hybrid_hgrn2_shortconv_deepnorm_layer"


def layer_norm(x, g, b):
    xf = x.astype(jnp.float32)
    mu = jnp.mean(xf, axis=-1, keepdims=True)
    xc = xf - mu
    var = jnp.mean(jnp.square(xc), axis=-1, keepdims=True)
    y = xc * lax.rsqrt(var + EPS) * g.astype(jnp.float32) + b.astype(jnp.float32)
    return y.astype(x.dtype)


def hgrn2_chunkwise(q, k, v, g):
    bsz, seq, n_h, d_k = q.shape
    d_v = v.shape[-1]
    n_c = seq // CHUNK

    def to_chunks(a):
        return a.reshape(bsz, n_c, CHUNK, n_h, a.shape[-1]).transpose(1, 0, 3, 2, 4)

    q, k, v, g = to_chunks(q), to_chunks(k), to_chunks(v), to_chunks(g)
    b = jnp.cumsum(g, axis=-2)
    b_ref = b[..., CHUNK // 2:CHUNK // 2 + 1, :]
    b_last = b[..., -1:, :]
    causal = jnp.tril(jnp.ones((CHUNK, CHUNK), dtype=bool))
    scores = jnp.einsum('nbhck,nbhsk->nbhcs', q * jnp.exp(b - b_ref), k * jnp.exp(b_ref - b))
    scores = jnp.where(causal, scores, 0.0)
    o_intra = jnp.einsum('nbhcs,nbhsv->nbhcv', scores, v)
    q_inter = q * jnp.exp(b)
    k_state = k * jnp.exp(b_last - b)
    chunk_decay = jnp.exp(b_last[..., 0, :])

    def step(state, inp):
        q_c, k_c, v_c, d_c = inp
        o_c = jnp.einsum('bhck,bhkv->bhcv', q_c, state)
        state = d_c[..., None] * state + jnp.einsum('bhck,bhcv->bhkv', k_c, v_c)
        return state, o_c

    s0 = jnp.zeros((bsz, n_h, d_k, d_v), q.dtype)
    _, o_inter = lax.scan(step, s0, (q_inter, k_state, v, chunk_decay))
    o = o_intra + o_inter
    return o.transpose(1, 0, 3, 2, 4).reshape(bsz, seq, n_h, d_v)


def causal_depthwise_conv(z, w):
    rhs = w.astype(z.dtype).reshape(CONV_K, 1, z.shape[-1])
    return lax.conv_general_dilated(
        z, rhs, window_strides=(1,), padding=[(CONV_K - 1, 0)],
        dimension_numbers=('NWC', 'WIO', 'NWC'), feature_group_count=z.shape[-1])


def token_mixer(h, w_in, lower_bound, gate_norm_w, conv_w, w_out):
    bsz, seq, _ = h.shape
    proj = h @ w_in
    q, f_pre, i_in, o_gate, b_gate, c_gate, u = jnp.split(proj, SPLITS, axis=-1)

    f = lower_bound + (1.0 - lower_bound) * jax.nn.sigmoid(f_pre.astype(jnp.float32))
    log_f = jnp.log(f)
    k = 1.0 - f
    heads = lambda a: a.reshape(bsz, seq, HGRN_HEADS, HGRN_HEAD_DIM)
    o = hgrn2_chunkwise(heads(q.astype(jnp.float32)), heads(k),
                        heads(i_in.astype(jnp.float32)), heads(log_f))
    o = o * lax.rsqrt(jnp.mean(jnp.square(o), axis=-1, keepdims=True) + EPS)
    o = o.reshape(bsz, seq, HGRN_WIDTH) * gate_norm_w.astype(jnp.float32) \
        * jax.nn.silu(o_gate.astype(jnp.float32))
    o = o.astype(h.dtype)

    y = b_gate * causal_depthwise_conv(c_gate * u, conv_w)

    return jnp.concatenate([o, y], axis=-1) @ w_out


def squared_relu_mlp(h, w1, w2):
    return jnp.square(jax.nn.relu(h @ w1)) @ w2


def _fwd_setup_inputs(seed: int = 0) -> dict:
    key = jax.random.key(seed)
    ks = jax.random.split(key, 12)
    nrm = lambda k, shape: jax.random.normal(k, shape, jnp.float32)
    return {
        "x": nrm(ks[0], (BATCH, SEQ, D_MODEL)),
        "w_in": nrm(ks[1], (DEPTH, D_MODEL, IN_COLS)) * D_MODEL ** -0.5,
        "lb_logits": 0.1 * nrm(ks[2], (DEPTH + 1, HGRN_WIDTH)),
        "gate_norm_w": 1.0 + 0.02 * nrm(ks[3], (DEPTH, HGRN_WIDTH)),
        "conv_w": nrm(ks[4], (DEPTH, CONV_K, CONV_WIDTH)) * CONV_K ** -0.5,
        "w_out": nrm(ks[5], (DEPTH, MIX_WIDTH, D_MODEL)) * (MIX_WIDTH ** -0.5 * BETA),
        "ln1_g": 1.0 + 0.02 * nrm(ks[6], (DEPTH, D_MODEL)),
        "ln1_b": 0.02 * nrm(ks[7], (DEPTH, D_MODEL)),
        "w_ff1": nrm(ks[8], (DEPTH, D_MODEL, D_FF)) * D_MODEL ** -0.5,
        "w_ff2": nrm(ks[9], (DEPTH, D_FF, D_MODEL)) * (D_FF ** -0.5 * BETA),
        "ln2_g": 1.0 + 0.02 * nrm(ks[10], (DEPTH, D_MODEL)),
        "ln2_b": 0.02 * nrm(ks[11], (DEPTH, D_MODEL)),
    }


def _fwd_reference(x, w_in, lb_logits, gate_norm_w, conv_w, w_out, ln1_g, ln1_b,
              w_ff1, w_ff2, ln2_g, ln2_b):
    lower_bounds = jnp.cumsum(jax.nn.softmax(lb_logits.astype(jnp.float32), axis=0), axis=0)
    h = x
    for l in range(DEPTH):
        mix = token_mixer(h, w_in[l], lower_bounds[l], gate_norm_w[l], conv_w[l], w_out[l])
        h = layer_norm(ALPHA * h + mix, ln1_g[l], ln1_b[l])
        h = layer_norm(ALPHA * h + squared_relu_mlp(h, w_ff1[l], w_ff2[l]), ln2_g[l], ln2_b[l])
    return h


import jax as _jax
import jax.numpy as _jnp

TWIN_FORMAT = 'train_step'
FWD_PARAMS = ['x', 'w_in', 'lb_logits', 'gate_norm_w', 'conv_w', 'w_out', 'ln1_g', 'ln1_b', 'w_ff1', 'w_ff2', 'ln2_g', 'ln2_b']
TWIN_WEIGHTS = ['w_in', 'lb_logits', 'gate_norm_w', 'conv_w', 'w_out', 'ln1_g', 'ln1_b', 'w_ff1', 'w_ff2', 'ln2_g', 'ln2_b']
TWIN_DIFF_INPUT = 'x'
TWIN_INPUTS = ['x', 'w_in', 'lb_logits', 'gate_norm_w', 'conv_w', 'w_out', 'ln1_g', 'ln1_b', 'w_ff1', 'w_ff2', 'ln2_g', 'ln2_b', 'loss_target', 'm_w_in', 'm_lb_logits', 'm_gate_norm_w', 'm_conv_w', 'm_w_out', 'm_ln1_g', 'm_ln1_b', 'm_w_ff1', 'm_w_ff2', 'm_ln2_g', 'm_ln2_b', 'v_w_in', 'v_lb_logits', 'v_gate_norm_w', 'v_conv_w', 'v_w_out', 'v_ln1_g', 'v_ln1_b', 'v_w_ff1', 'v_w_ff2', 'v_ln2_g', 'v_ln2_b']
TWIN_OUTPUTS = ['loss', 'grad_x', 'grad_w_in', 'grad_lb_logits', 'grad_gate_norm_w', 'grad_conv_w', 'grad_w_out', 'grad_ln1_g', 'grad_ln1_b', 'grad_w_ff1', 'grad_w_ff2', 'grad_ln2_g', 'grad_ln2_b', 'delta_w_in', 'delta_lb_logits', 'delta_gate_norm_w', 'delta_conv_w', 'delta_w_out', 'delta_ln1_g', 'delta_ln1_b', 'delta_w_ff1', 'delta_w_ff2', 'delta_ln2_g', 'delta_ln2_b', 'new_m_w_in', 'new_m_lb_logits', 'new_m_gate_norm_w', 'new_m_conv_w', 'new_m_w_out', 'new_m_ln1_g', 'new_m_ln1_b', 'new_m_w_ff1', 'new_m_w_ff2', 'new_m_ln2_g', 'new_m_ln2_b', 'new_v_w_in', 'new_v_lb_logits', 'new_v_gate_norm_w', 'new_v_conv_w', 'new_v_w_out', 'new_v_ln1_g', 'new_v_ln1_b', 'new_v_w_ff1', 'new_v_w_ff2', 'new_v_ln2_g', 'new_v_ln2_b']
TWIN_LEAF_KINDS = {'loss': 'loss', 'grad_x': 'grad_x', 'grad_w_in': 'grad_w', 'grad_lb_logits': 'grad_w', 'grad_gate_norm_w': 'grad_w', 'grad_conv_w': 'grad_w', 'grad_w_out': 'grad_w', 'grad_ln1_g': 'grad_w', 'grad_ln1_b': 'grad_w', 'grad_w_ff1': 'grad_w', 'grad_w_ff2': 'grad_w', 'grad_ln2_g': 'grad_w', 'grad_ln2_b': 'grad_w', 'delta_w_in': 'delta_w', 'delta_lb_logits': 'delta_w', 'delta_gate_norm_w': 'delta_w', 'delta_conv_w': 'delta_w', 'delta_w_out': 'delta_w', 'delta_ln1_g': 'delta_w', 'delta_ln1_b': 'delta_w', 'delta_w_ff1': 'delta_w', 'delta_w_ff2': 'delta_w', 'delta_ln2_g': 'delta_w', 'delta_ln2_b': 'delta_w', 'new_m_w_in': 'new_m', 'new_m_lb_logits': 'new_m', 'new_m_gate_norm_w': 'new_m', 'new_m_conv_w': 'new_m', 'new_m_w_out': 'new_m', 'new_m_ln1_g': 'new_m', 'new_m_ln1_b': 'new_m', 'new_m_w_ff1': 'new_m', 'new_m_w_ff2': 'new_m', 'new_m_ln2_g': 'new_m', 'new_m_ln2_b': 'new_m', 'new_v_w_in': 'new_v', 'new_v_lb_logits': 'new_v', 'new_v_gate_norm_w': 'new_v', 'new_v_conv_w': 'new_v', 'new_v_w_out': 'new_v', 'new_v_ln1_g': 'new_v', 'new_v_ln1_b': 'new_v', 'new_v_w_ff1': 'new_v', 'new_v_w_ff2': 'new_v', 'new_v_ln2_g': 'new_v', 'new_v_ln2_b': 'new_v'}


def _forward(args):
    return _fwd_reference(*[args[k] for k in FWD_PARAMS])


def _output_shape():
    def fwd():
        inp = _fwd_setup_inputs(0)
        return _fwd_reference(*[inp[k] for k in FWD_PARAMS])
    out = _jax.eval_shape(fwd)
    return out.shape, out.dtype

N_MICROBATCH = 1
ADAM_LR = 0.001
ADAM_B1 = 0.9
ADAM_B2 = 0.999
ADAM_EPS = 1e-08
ADAM_WD = 0.01
ADAM_STEP = 10
PER_EXAMPLE_BATCH_AXIS = {'x': 0, 'loss_target': 0}
SHARED_INPUTS = []
_WEIGHT_DTYPES = {'w_in': _jnp.float32, 'lb_logits': _jnp.float32, 'gate_norm_w': _jnp.float32, 'conv_w': _jnp.float32, 'w_out': _jnp.float32, 'ln1_g': _jnp.float32, 'ln1_b': _jnp.float32, 'w_ff1': _jnp.float32, 'w_ff2': _jnp.float32, 'ln2_g': _jnp.float32, 'ln2_b': _jnp.float32}
MOMENT_SCALE = {'w_in': 6.937019e-02, 'lb_logits': 3.324422e-02, 'gate_norm_w': 5.164140e-02, 'conv_w': 8.836702e-02, 'w_out': 1.180207e-01, 'ln1_g': 8.272312e-01, 'ln1_b': 5.580274e-01, 'w_ff1': 5.369110e-02, 'w_ff2': 2.349894e-01, 'ln2_g': 3.208865e+01, 'ln2_b': 6.696459e+00}


def _to_microbatches(a, axis):
    t = _jnp.moveaxis(a, axis, 0)
    t = t.reshape((N_MICROBATCH, t.shape[0] // N_MICROBATCH) + t.shape[1:])
    return _jnp.moveaxis(t, 1, axis + 1)


def setup_inputs(seed: int = 0) -> dict:
    inp = _fwd_setup_inputs(seed)
    key = _jax.random.fold_in(_jax.random.key(seed), 7919)
    shape, _ = _output_shape()
    out = dict(inp)
    out["loss_target"] = _jax.random.normal(_jax.random.fold_in(key, 0), shape, _jnp.float32)
    for i, name in enumerate(TWIN_WEIGHTS):
        w = inp[name].astype(_jnp.float32)
        if MOMENT_SCALE is None:
            s = _jnp.sqrt(_jnp.mean(_jnp.square(w)) + 1e-30)
        else:
            s = MOMENT_SCALE[name]
        km, kv = _jax.random.split(_jax.random.fold_in(key, i + 1))
        out[name] = w
        out["m_" + name] = s * _jax.random.normal(km, w.shape, _jnp.float32)
        out["v_" + name] = (s * s) * _jax.random.uniform(kv, w.shape, _jnp.float32, 0.5, 1.5)
    if N_MICROBATCH > 1:
        for name, axis in PER_EXAMPLE_BATCH_AXIS.items():
            out[name] = _to_microbatches(out[name], axis)
    return {'x': out['x'], 'w_in': out['w_in'], 'lb_logits': out['lb_logits'], 'gate_norm_w': out['gate_norm_w'], 'conv_w': out['conv_w'], 'w_out': out['w_out'], 'ln1_g': out['ln1_g'], 'ln1_b': out['ln1_b'], 'w_ff1': out['w_ff1'], 'w_ff2': out['w_ff2'], 'ln2_g': out['ln2_g'], 'ln2_b': out['ln2_b'], 'loss_target': out['loss_target'], 'm_w_in': out['m_w_in'], 'm_lb_logits': out['m_lb_logits'], 'm_gate_norm_w': out['m_gate_norm_w'], 'm_conv_w': out['m_conv_w'], 'm_w_out': out['m_w_out'], 'm_ln1_g': out['m_ln1_g'], 'm_ln1_b': out['m_ln1_b'], 'm_w_ff1': out['m_w_ff1'], 'm_w_ff2': out['m_w_ff2'], 'm_ln2_g': out['m_ln2_g'], 'm_ln2_b': out['m_ln2_b'], 'v_w_in': out['v_w_in'], 'v_lb_logits': out['v_lb_logits'], 'v_gate_norm_w': out['v_gate_norm_w'], 'v_conv_w': out['v_conv_w'], 'v_w_out': out['v_w_out'], 'v_ln1_g': out['v_ln1_g'], 'v_ln1_b': out['v_ln1_b'], 'v_w_ff1': out['v_w_ff1'], 'v_w_ff2': out['v_w_ff2'], 'v_ln2_g': out['v_ln2_g'], 'v_ln2_b': out['v_ln2_b']}


def _loss(weights, diff, rest, loss_target):
    with _jax.named_scope("forward"):
        args = {**rest, TWIN_DIFF_INPUT: diff, **{k: w.astype(_WEIGHT_DTYPES[k]) for k, w in weights.items()}}
        y = _forward(args)
    with _jax.named_scope("loss_head"):
        err = _jnp.square(y.astype(_jnp.float32) - loss_target)
        return 0.5 * _jnp.sum(_jnp.mean(err, axis=-1)) if err.ndim else 0.5 * err


def _adamw(w, g, m, v):
    m = ADAM_B1 * m + (1.0 - ADAM_B1) * g
    v = ADAM_B2 * v + (1.0 - ADAM_B2) * _jnp.square(g)
    m_hat = m / (1.0 - ADAM_B1 ** ADAM_STEP)
    v_hat = v / (1.0 - ADAM_B2 ** ADAM_STEP)
    delta = -ADAM_LR * (m_hat / (_jnp.sqrt(v_hat) + ADAM_EPS) + ADAM_WD * w)
    return delta, m, v


def reference(x, w_in, lb_logits, gate_norm_w, conv_w, w_out, ln1_g, ln1_b, w_ff1, w_ff2, ln2_g, ln2_b, loss_target, m_w_in, m_lb_logits, m_gate_norm_w, m_conv_w, m_w_out, m_ln1_g, m_ln1_b, m_w_ff1, m_w_ff2, m_ln2_g, m_ln2_b, v_w_in, v_lb_logits, v_gate_norm_w, v_conv_w, v_w_out, v_ln1_g, v_ln1_b, v_w_ff1, v_w_ff2, v_ln2_g, v_ln2_b):
    given = dict(x=x, w_in=w_in, lb_logits=lb_logits, gate_norm_w=gate_norm_w, conv_w=conv_w, w_out=w_out, ln1_g=ln1_g, ln1_b=ln1_b, w_ff1=w_ff1, w_ff2=w_ff2, ln2_g=ln2_g, ln2_b=ln2_b, loss_target=loss_target, m_w_in=m_w_in, m_lb_logits=m_lb_logits, m_gate_norm_w=m_gate_norm_w, m_conv_w=m_conv_w, m_w_out=m_w_out, m_ln1_g=m_ln1_g, m_ln1_b=m_ln1_b, m_w_ff1=m_w_ff1, m_w_ff2=m_w_ff2, m_ln2_g=m_ln2_g, m_ln2_b=m_ln2_b, v_w_in=v_w_in, v_lb_logits=v_lb_logits, v_gate_norm_w=v_gate_norm_w, v_conv_w=v_conv_w, v_w_out=v_w_out, v_ln1_g=v_ln1_g, v_ln1_b=v_ln1_b, v_w_ff1=v_w_ff1, v_w_ff2=v_w_ff2, v_ln2_g=v_ln2_g, v_ln2_b=v_ln2_b)
    weights = {n: given[n] for n in TWIN_WEIGHTS}
    shared = {n: given[n] for n in SHARED_INPUTS}
    per_example = {n: given[n] for n in ['x']}
    grad_fn = _jax.value_and_grad(_loss, argnums=(0, 1))

    def one_microbatch(ex, loss_target):
        ex = dict(ex)
        diff = ex.pop(TWIN_DIFF_INPUT)
        return grad_fn(weights, diff, {**shared, **ex}, loss_target)

    if N_MICROBATCH == 1:
        loss, (grad_w, grad_x) = one_microbatch(per_example, given["loss_target"])
    else:
        def body(carry, xs):
            loss_sum, grad_sum = carry
            l_k, (gw_k, gx_k) = one_microbatch(xs[0], xs[1])
            with _jax.named_scope("update"):
                return (loss_sum + l_k, _jax.tree.map(_jnp.add, grad_sum, gw_k)), gx_k

        init = (_jnp.zeros((), _jnp.float32), _jax.tree.map(_jnp.zeros_like, weights))
        (loss, grad_w), grad_x = _jax.lax.scan(body, init, (per_example, given["loss_target"]))
    with _jax.named_scope("update"):
        delta_w, new_m, new_v = {}, {}, {}
        for n in TWIN_WEIGHTS:
            delta_w[n], new_m[n], new_v[n] = _adamw(weights[n], grad_w[n], given["m_" + n], given["v_" + n])
    return (loss, grad_x, *[grad_w[n] for n in TWIN_WEIGHTS], *[delta_w[n] for n in TWIN_WEIGHTS],
            *[new_m[n] for n in TWIN_WEIGHTS], *[new_v[n] for n in TWIN_WEIGHTS])
```

```python
import jax
import jax.numpy as jnp
from jax import lax
from jax.experimental import pallas as pl
from jax.experimental.pallas import tpu as pltpu

F32 = jnp.float32
BF16 = jnp.bfloat16
MXU_DTYPE = jnp.bfloat16

D_MODEL = 1024
HGRN_WIDTH = 512
HEAD_DIM = 128
N_HEADS = 4
CONV_WIDTH = 512
CHUNK = 64
D_FF = 4096
IN_COLS = 3584
ALPHA = 2.0 ** 0.25
EPS = 1e-5
N_CHIPS = 4
ADAM_LR, ADAM_B1, ADAM_B2, ADAM_EPS, ADAM_WD, ADAM_STEP = 0.001, 0.9, 0.999, 1e-08, 0.01, 10

LANES = 128
SUBLANES = 8
VMEM_LIMIT = 48 * 1024 * 1024

NN = (((1,), (0,)), ((), ()))
NT = (((1,), (1,)), ((), ()))
TN = (((0,), (0,)), ((), ()))
MESH = pl.DeviceIdType.MESH
ANY = pl.BlockSpec(memory_space=pl.ANY)


def _dot(a, b, dims):
    return lax.dot_general(a.astype(MXU_DTYPE), b.astype(MXU_DTYPE), dims, preferred_element_type=F32)


def _dot_exact(a, b):
    return lax.dot_general(a, b, NN, precision=lax.Precision.HIGHEST, preferred_element_type=F32)


def _params(*sem):
    return pltpu.CompilerParams(dimension_semantics=sem, vmem_limit_bytes=VMEM_LIMIT)


def _sigmoid(v):
    return 1.0 / (1.0 + jnp.exp(-v))


def _lower_bound(lbl):
    m = jnp.max(lbl, axis=0, keepdims=True)
    e = jnp.exp(lbl - m)
    s = e / jnp.sum(e, axis=0, keepdims=True)
    return s[0:1, :], s[1:2, :]


def _in_proj(x, w_in):
    t = x.shape[0]
    tm, tn = min(t, 512), 512

    def body(x_ref, w_ref, o_ref):
        o_ref[...] = _dot(x_ref[...], w_ref[...], NN)

    return pl.pallas_call(
        body, name="in_proj", grid=(t // tm, IN_COLS // tn),
        in_specs=[pl.BlockSpec((tm, D_MODEL), lambda i, j: (i, 0)), pl.BlockSpec((D_MODEL, tn), lambda i, j: (0, j))],
        out_specs=pl.BlockSpec((tm, tn), lambda i, j: (i, j)),
        out_shape=jax.ShapeDtypeStruct((t, IN_COLS), F32),
        compiler_params=_params("parallel", "parallel"),
    )(x, w_in)


def _gates(fp, lb):
    sig = _sigmoid(fp)
    f = lb + (1.0 - lb) * sig
    return sig, f, jnp.log(f), 1.0 - f


def _hgrn_fwd(proj, lb_logits):
    t = proj.shape[0]
    tb = min(t, 512)
    ncb = tb // CHUNK
    nblk = t // tb

    def body(q_ref, f_ref, v_ref, lbl_ref, o_ref, st_ref, s_scr):
        @pl.when(pl.program_id(1) == 0)
        def _():
            s_scr[...] = jnp.zeros_like(s_scr)

        lb, _ = _lower_bound(lbl_ref[...])
        row = lax.broadcasted_iota(jnp.int32, (CHUNK, CHUNK), 0)
        col = lax.broadcasted_iota(jnp.int32, (CHUNK, CHUNK), 1)
        causal = row >= col
        tri = causal.astype(F32)

        def chunk(c, carry):
            rows = pl.ds(pl.multiple_of(c * CHUNK, CHUNK), CHUNK)
            q, v = q_ref[rows, :], v_ref[rows, :]
            _, _, g, k = _gates(f_ref[rows, :], lb)
            b = _dot_exact(tri, g)
            b_ref, b_last = b[CHUNK // 2:CHUNK // 2 + 1, :], b[CHUNK - 1:CHUNK, :]
            scores = _dot(q * jnp.exp(b - b_ref), k * jnp.exp(b_ref - b), NT)
            scores = jnp.where(causal, scores, 0.0)
            o_intra = _dot(scores, v, NN)
            st = s_scr[...]
            st_ref[c] = st
            o_inter = _dot(q * jnp.exp(b), st, NT)
            s_scr[...] = jnp.exp(b_last) * st + _dot(v, k * jnp.exp(b_last - b), TN)
            o_ref[rows, :] = o_intra + o_inter
            return carry

        lax.fori_loop(0, ncb, chunk, 0)

    col_spec = lambda off: pl.BlockSpec((tb, HEAD_DIM), lambda h, i: (i, off + h))
    return pl.pallas_call(
        body, name="hgrn_fwd", grid=(N_HEADS, nblk),
        in_specs=[col_spec(0), col_spec(4), col_spec(8), pl.BlockSpec((2, HEAD_DIM), lambda h, i: (0, h))],
        out_specs=[pl.BlockSpec((tb, HEAD_DIM), lambda h, i: (i, h)),
                   pl.BlockSpec((ncb, None, HEAD_DIM, HEAD_DIM), lambda h, i: (i, h, 0, 0))],
        out_shape=[jax.ShapeDtypeStruct((t, HGRN_WIDTH), F32),
                   jax.ShapeDtypeStruct((t // CHUNK, N_HEADS, HEAD_DIM, HEAD_DIM), F32)],
        scratch_shapes=[pltpu.VMEM((HEAD_DIM, HEAD_DIM), F32)],
        compiler_params=_params("parallel", "arbitrary"),
    )(proj, proj, proj, lb_logits)


def _conv_taps(z, halo, zbuf, tb):
    zbuf[0:SUBLANES, :] = halo
    zbuf[SUBLANES:SUBLANES + tb, :] = z
    return zbuf[SUBLANES - 1:SUBLANES - 1 + tb, :], zbuf[SUBLANES - 2:SUBLANES - 2 + tb, :]


def _gate_fwd(proj, o, gate_norm_w, conv_w):
    t = proj.shape[0]
    tb = min(t, 512)
    hb = tb // SUBLANES

    def body(o_ref, og_ref, gnw_ref, b_ref, c_ref, u_ref, ch_ref, uh_ref, cw_ref, cat_ref, zbuf):
        i = pl.program_id(1)
        ov, og = o_ref[...], og_ref[...]
        on = ov * lax.rsqrt(jnp.mean(ov * ov, axis=-1, keepdims=True) + EPS)
        cat_ref[0] = (on * gnw_ref[...] * (og * _sigmoid(og))).astype(cat_ref.dtype)
        z = c_ref[...] * u_ref[...]
        halo = jnp.where(i > 0, ch_ref[...] * uh_ref[...], 0.0)
        z1, z2 = _conv_taps(z, halo, zbuf, tb)
        cw = cw_ref[...]
        yc = cw[2:3, :] * z + cw[1:2, :] * z1 + cw[0:1, :] * z2
        cat_ref[1] = (b_ref[...] * yc).astype(cat_ref.dtype)

    blk = lambda off: pl.BlockSpec((tb, LANES), lambda j, i: (i, off + j))
    prev = lambda off: pl.BlockSpec((SUBLANES, LANES), lambda j, i: (jnp.maximum(i * hb - 1, 0), off + j))
    vec = lambda r: pl.BlockSpec((r, LANES), lambda j, i: (0, j))
    return pl.pallas_call(
        body, name="gate_fwd", grid=(4, t // tb),
        in_specs=[blk(0), blk(12), vec(1), blk(16), blk(20), blk(24), prev(20), prev(24), vec(3)],
        out_specs=pl.BlockSpec((2, tb, LANES), lambda j, i: (0, i, j)),
        out_shape=jax.ShapeDtypeStruct((2, t, HGRN_WIDTH), BF16),
        scratch_shapes=[pltpu.VMEM((tb + SUBLANES, LANES), F32)],
        compiler_params=_params("parallel", "arbitrary"),
    )(o, proj, gate_norm_w, proj, proj, proj, proj, proj, conv_w)


def _out_ln1(cat, w_out, x, g1, b1):
    t = x.shape[0]
    tm = min(t, 256)

    def body(cat_ref, w_ref, x_ref, g_ref, b_ref, xhat_ref, h1_ref, rstd_ref, acc):
        k = pl.program_id(1)

        @pl.when(k == 0)
        def _():
            acc[...] = jnp.zeros_like(acc)

        acc[...] += _dot(cat_ref[...], w_ref[...], NN)

        @pl.when(k == 1)
        def _():
            pre = ALPHA * x_ref[...] + acc[...]
            xc = pre - jnp.mean(pre, axis=-1, keepdims=True)
            rstd = lax.rsqrt(jnp.mean(xc * xc, axis=-1, keepdims=True) + EPS)
            xhat = xc * rstd
            xhat_ref[...] = xhat
            h1_ref[...] = (xhat * g_ref[...] + b_ref[...]).astype(h1_ref.dtype)
            rstd_ref[...] = rstd

    row = pl.BlockSpec((tm, D_MODEL), lambda i, k: (i, 0))
    vec = pl.BlockSpec((1, D_MODEL), lambda i, k: (0, 0))
    return pl.pallas_call(
        body, name="out_ln1", grid=(t // tm, 2),
        in_specs=[pl.BlockSpec((None, tm, HGRN_WIDTH), lambda i, k: (k, i, 0)),
                  pl.BlockSpec((HGRN_WIDTH, D_MODEL), lambda i, k: (k, 0)), row, vec, vec],
        out_specs=[row, row, pl.BlockSpec((tm, 1), lambda i, k: (i, 0))],
        out_shape=[jax.ShapeDtypeStruct((t, D_MODEL), F32), jax.ShapeDtypeStruct((t, D_MODEL), BF16),
                   jax.ShapeDtypeStruct((t, 1), F32)],
        scratch_shapes=[pltpu.VMEM((tm, D_MODEL), F32)],
        compiler_params=_params("parallel", "arbitrary"),
    )(cat, w_out, x, g1, b1)


def _ff1(h1b, w_ff1):
    t = h1b.shape[0]
    tm, tn = min(t, 512), 1024

    def body(h_ref, w_ref, a_ref, r_ref):
        a = _dot(h_ref[...], w_ref[...], NN)
        a_ref[...] = a
        r_ref[...] = jnp.square(jnp.maximum(a, 0.0)).astype(r_ref.dtype)

    out = pl.BlockSpec((tm, tn), lambda i, j: (i, j))
    return pl.pallas_call(
        body, name="ff1", grid=(t // tm, D_FF // tn),
        in_specs=[pl.BlockSpec((tm, D_MODEL), lambda i, j: (i, 0)), pl.BlockSpec((D_MODEL, tn), lambda i, j: (0, j))],
        out_specs=[out, out],
        out_shape=[jax.ShapeDtypeStruct((t, D_FF), F32), jax.ShapeDtypeStruct((t, D_FF), BF16)],
        compiler_params=_params("parallel", "parallel"),
    )(h1b, w_ff1)


def _ln_bwd(dy, xhat, rstd, g):
    dxhat = dy * g
    m1 = jnp.mean(dxhat, axis=-1, keepdims=True)
    m2 = jnp.mean(dxhat * xhat, axis=-1, keepdims=True)
    return rstd * (dxhat - m1 - xhat * m2)


def _ff2_ln2_loss(r, w_ff2, xhat1, g1, b1, g2, b2, target):
    t = r.shape[0]
    tm, tk = min(t, 256), 1024
    nk = D_FF // tk

    def body(r_ref, w_ref, xh_ref, g1_ref, b1_ref, g2_ref, b2_ref, tg_ref,
             dpre_ref, dpreb_ref, dg_ref, db_ref, loss_ref, acc):
        i, k = pl.program_id(0), pl.program_id(1)

        @pl.when(k == 0)
        def _():
            acc[...] = jnp.zeros_like(acc)

        acc[...] += _dot(r_ref[...], w_ref[...], NN)

        @pl.when(k == nk - 1)
        def _():
            @pl.when(i == 0)
            def _():
                dg_ref[...] = jnp.zeros_like(dg_ref)
                db_ref[...] = jnp.zeros_like(db_ref)
                loss_ref[...] = jnp.zeros_like(loss_ref)

            h1 = xh_ref[...] * g1_ref[...] + b1_ref[...]
            pre = ALPHA * h1 + acc[...]
            xc = pre - jnp.mean(pre, axis=-1, keepdims=True)
            rstd = lax.rsqrt(jnp.mean(xc * xc, axis=-1, keepdims=True) + EPS)
            xhat = xc * rstd
            err = xhat * g2_ref[...] + b2_ref[...] - tg_ref[...]
            loss_ref[...] += 0.5 * jnp.sum(jnp.mean(err * err, axis=-1, keepdims=True))
            dy = err * (1.0 / D_MODEL)
            dg_ref[...] += jnp.sum(dy * xhat, axis=0, keepdims=True)
            db_ref[...] += jnp.sum(dy, axis=0, keepdims=True)
            dpre = _ln_bwd(dy, xhat, rstd, g2_ref[...])
            dpre_ref[...] = dpre
            dpreb_ref[...] = dpre.astype(dpreb_ref.dtype)

    row = pl.BlockSpec((tm, D_MODEL), lambda i, k: (i, 0))
    vec = pl.BlockSpec((1, D_MODEL), lambda i, k: (0, 0))
    return pl.pallas_call(
        body, name="ff2_ln2_loss", grid=(t // tm, nk),
        in_specs=[pl.BlockSpec((tm, tk), lambda i, k: (i, k)), pl.BlockSpec((tk, D_MODEL), lambda i, k: (k, 0)),
                  row, vec, vec, vec, vec, row],
        out_specs=[row, row, vec, vec, pl.BlockSpec((SUBLANES, LANES), lambda i, k: (0, 0))],
        out_shape=[jax.ShapeDtypeStruct((t, D_MODEL), F32), jax.ShapeDtypeStruct((t, D_MODEL), BF16),
                   jax.ShapeDtypeStruct((1, D_MODEL), F32), jax.ShapeDtypeStruct((1, D_MODEL), F32),
                   jax.ShapeDtypeStruct((SUBLANES, LANES), F32)],
        scratch_shapes=[pltpu.VMEM((tm, D_MODEL), F32)],
        compiler_params=_params("arbitrary", "arbitrary"),
    )(r, w_ff2, xhat1, g1, b1, g2, b2, target)


def _ff2_bwd(dpre2b, w_ff2, a):
    t = a.shape[0]
    tm, tn = min(t, 512), 1024

    def body(d_ref, w_ref, a_ref, da_ref):
        dr = _dot(d_ref[...], w_ref[...], NT)
        da_ref[...] = (dr * (2.0 * jnp.maximum(a_ref[...], 0.0))).astype(da_ref.dtype)

    return pl.pallas_call(
        body, name="ff2_bwd", grid=(t // tm, D_FF // tn),
        in_specs=[pl.BlockSpec((tm, D_MODEL), lambda i, j: (i, 0)), pl.BlockSpec((tn, D_MODEL), lambda i, j: (j, 0)),
                  pl.BlockSpec((tm, tn), lambda i, j: (i, j))],
        out_specs=pl.BlockSpec((tm, tn), lambda i, j: (i, j)),
        out_shape=jax.ShapeDtypeStruct((t, D_FF), BF16),
        compiler_params=_params("parallel", "parallel"),
    )(dpre2b, w_ff2, a)


def _ff1_bwd_ln1(da, w_ff1, dpre2, xhat1, rstd1, g1):
    t = da.shape[0]
    tm, tk = min(t, 256), 1024
    nk = D_FF // tk

    def body(da_ref, w_ref, dp2_ref, xh_ref, rs_ref, g_ref, dpre_ref, dpreb_ref, dg_ref, db_ref, acc):
        i, k = pl.program_id(0), pl.program_id(1)

        @pl.when(k == 0)
        def _():
            acc[...] = jnp.zeros_like(acc)

        acc[...] += _dot(da_ref[...], w_ref[...], NT)

        @pl.when(k == nk - 1)
        def _():
            @pl.when(i == 0)
            def _():
                dg_ref[...] = jnp.zeros_like(dg_ref)
                db_ref[...] = jnp.zeros_like(db_ref)

            dh1 = ALPHA * dp2_ref[...] + acc[...]
            xhat = xh_ref[...]
            dg_ref[...] += jnp.sum(dh1 * xhat, axis=0, keepdims=True)
            db_ref[...] += jnp.sum(dh1, axis=0, keepdims=True)
            dpre = _ln_bwd(dh1, xhat, rs_ref[...], g_ref[...])
            dpre_ref[...] = dpre
            dpreb_ref[...] = dpre.astype(dpreb_ref.dtype)

    row = pl.BlockSpec((tm, D_MODEL), lambda i, k: (i, 0))
    vec = pl.BlockSpec((1, D_MODEL), lambda i, k: (0, 0))
    return pl.pallas_call(
        body, name="ff1_bwd_ln1", grid=(t // tm, nk),
        in_specs=[pl.BlockSpec((tm, tk), lambda i, k: (i, k)), pl.BlockSpec((D_MODEL, tk), lambda i, k: (0, k)),
                  row, row, pl.BlockSpec((tm, 1), lambda i, k: (i, 0)), vec],
        out_specs=[row, row, vec, vec],
        out_shape=[jax.ShapeDtypeStruct((t, D_MODEL), F32), jax.ShapeDtypeStruct((t, D_MODEL), BF16),
                   jax.ShapeDtypeStruct((1, D_MODEL), F32), jax.ShapeDtypeStruct((1, D_MODEL), F32)],
        scratch_shapes=[pltpu.VMEM((tm, D_MODEL), F32)],
        compiler_params=_params("arbitrary", "arbitrary"),
    )(da, w_ff1, dpre2, xhat1, rstd1, g1)


def _out_bwd(dpre1b, w_out):
    t = dpre1b.shape[0]
    tm = min(t, 512)

    def body(d_ref, w_ref, o_ref):
        o_ref[...] = _dot(d_ref[...], w_ref[...], NT)

    return pl.pallas_call(
        body, name="out_bwd", grid=(t // tm,),
        in_specs=[pl.BlockSpec((tm, D_MODEL), lambda i: (i, 0)), pl.BlockSpec((D_MODEL, D_MODEL), lambda i: (0, 0))],
        out_specs=pl.BlockSpec((tm, D_MODEL), lambda i: (i, 0)),
        out_shape=jax.ShapeDtypeStruct((t, D_MODEL), F32),
        compiler_params=_params("parallel"),
    )(dpre1b, w_out)


def _gate_bwd(dcat, o, proj, gate_norm_w, conv_w):
    t = proj.shape[0]
    tb = min(t, 512)
    hb = tb // SUBLANES
    nblk = t // tb

    def body(do2_ref, dy_ref, dyn_ref, o_ref, og_ref, gnw_ref, b_ref, bn_ref, c_ref, u_ref, ch_ref, uh_ref, cw_ref,
             do_ref, dp_ref, dgnw_ref, dcw_ref, zbuf, dbuf):
        i = pl.program_id(1)

        @pl.when(i == 0)
        def _():
            dgnw_ref[...] = jnp.zeros_like(dgnw_ref)
            dcw_ref[...] = jnp.zeros_like(dcw_ref)

        ov, og, gnw, do2 = o_ref[...], og_ref[...], gnw_ref[...], do2_ref[...]
        rs = lax.rsqrt(jnp.mean(ov * ov, axis=-1, keepdims=True) + EPS)
        on = ov * rs
        sg = _sigmoid(og)
        sil = og * sg
        don = do2 * gnw * sil
        dgnw_ref[...] += jnp.sum(do2 * on * sil, axis=0, keepdims=True)
        dp_ref[0] = (do2 * on * gnw * (sg * (1.0 + og * (1.0 - sg)))).astype(dp_ref.dtype)
        do_ref[...] = rs * (don - on * jnp.mean(don * on, axis=-1, keepdims=True))

        bg, cg, u, dy = b_ref[...], c_ref[...], u_ref[...], dy_ref[...]
        z = cg * u
        halo = jnp.where(i > 0, ch_ref[...] * uh_ref[...], 0.0)
        z1, z2 = _conv_taps(z, halo, zbuf, tb)
        cw = cw_ref[...]
        yc = cw[2:3, :] * z + cw[1:2, :] * z1 + cw[0:1, :] * z2
        dyc = dy * bg
        dbuf[0:tb, :] = dyc
        dbuf[tb:tb + SUBLANES, :] = jnp.where(i < nblk - 1, dyn_ref[...] * bn_ref[...], 0.0)
        d1, d2 = dbuf[1:1 + tb, :], dbuf[2:2 + tb, :]
        dz = cw[2:3, :] * dyc + cw[1:2, :] * d1 + cw[0:1, :] * d2
        dp_ref[1] = (dy * yc).astype(dp_ref.dtype)
        dp_ref[2] = (dz * u).astype(dp_ref.dtype)
        dp_ref[3] = (dz * cg).astype(dp_ref.dtype)
        dcw_ref[0:1, :] += jnp.sum(dyc * z2, axis=0, keepdims=True)
        dcw_ref[1:2, :] += jnp.sum(dyc * z1, axis=0, keepdims=True)
        dcw_ref[2:3, :] += jnp.sum(dyc * z, axis=0, keepdims=True)

    blk = lambda off: pl.BlockSpec((tb, LANES), lambda j, i: (i, off + j))
    prev = lambda off: pl.BlockSpec((SUBLANES, LANES), lambda j, i: (jnp.maximum(i * hb - 1, 0), off + j))
    nxt = lambda off: pl.BlockSpec((SUBLANES, LANES), lambda j, i: (jnp.minimum((i + 1) * hb, t // SUBLANES - 1), off + j))
    vec = lambda r: pl.BlockSpec((r, LANES), lambda j, i: (0, j))
    return pl.pallas_call(
        body, name="gate_bwd", grid=(4, nblk),
        in_specs=[blk(0), blk(4), nxt(4), blk(0), blk(12), vec(1), blk(16), nxt(16), blk(20), blk(24), prev(20), prev(24),
                  vec(3)],
        out_specs=[blk(0), pl.BlockSpec((4, tb, LANES), lambda j, i: (0, i, j)), vec(1), vec(3)],
        out_shape=[jax.ShapeDtypeStruct((t, HGRN_WIDTH), F32), jax.ShapeDtypeStruct((4, t, HGRN_WIDTH), BF16),
                   jax.ShapeDtypeStruct((1, HGRN_WIDTH), F32), jax.ShapeDtypeStruct((3, CONV_WIDTH), F32)],
        scratch_shapes=[pltpu.VMEM((tb + SUBLANES, LANES), F32), pltpu.VMEM((tb + SUBLANES, LANES), F32)],
        compiler_params=_params("parallel", "arbitrary"),
    )(dcat, dcat, dcat, o, proj, gate_norm_w, proj, proj, proj, proj, proj, proj, conv_w)


def _hgrn_bwd(proj, do, states, lb_logits):
    t = proj.shape[0]
    tb = min(t, 512)
    ncb = tb // CHUNK
    nblk = t // tb

    def body(q_ref, f_ref, v_ref, do_ref, st_ref, lbl_ref, dp_ref, dlbl_ref, ds_scr, dlb_scr):
        i = pl.program_id(1)

        @pl.when(i == 0)
        def _():
            ds_scr[...] = jnp.zeros_like(ds_scr)
            dlb_scr[...] = jnp.zeros_like(dlb_scr)

        lb, s1 = _lower_bound(lbl_ref[...])
        row = lax.broadcasted_iota(jnp.int32, (CHUNK, CHUNK), 0)
        col = lax.broadcasted_iota(jnp.int32, (CHUNK, CHUNK), 1)
        causal = row >= col
        tri = causal.astype(F32)
        tri_rev = (row <= col).astype(F32)

        def chunk(n, carry):
            c = ncb - 1 - n
            rows = pl.ds(pl.multiple_of(c * CHUNK, CHUNK), CHUNK)
            q, v, do_c = q_ref[rows, :], v_ref[rows, :], do_ref[rows, :]
            sig, f, g, k = _gates(f_ref[rows, :], lb)
            b = _dot_exact(tri, g)
            b_ref, b_last = b[CHUNK // 2:CHUNK // 2 + 1, :], b[CHUNK - 1:CHUNK, :]
            e_q, e_k, e_i, e_s = jnp.exp(b - b_ref), jnp.exp(b_ref - b), jnp.exp(b), jnp.exp(b_last - b)
            dec = jnp.exp(b_last)
            qt, kt, qi, ks = q * e_q, k * e_k, q * e_i, k * e_s
            st, dst = st_ref[c], ds_scr[...]

            scores = jnp.where(causal, _dot(qt, kt, NT), 0.0)
            dscores = jnp.where(causal, _dot(do_c, v, NT), 0.0)
            dqt = _dot(dscores, kt, NN)
            dkt = _dot(dscores, qt, TN)
            dv = _dot(scores, do_c, TN) + _dot(ks, dst, NT)
            dqi = _dot(do_c, st, NN)
            dks = _dot(v, dst, NN)
            ddec = jnp.sum(dst * st, axis=0, keepdims=True)
            ds_scr[...] = dec * dst + _dot(do_c, qi, TN)

            dq = dqt * e_q + dqi * e_i
            dk = dkt * e_k + dks * e_s
            db = q * dq - k * dk
            db_last = jnp.sum(dks * ks, axis=0, keepdims=True) + ddec * dec
            dg = _dot_exact(tri_rev, db) + db_last
            df = dg / f - dk
            dlb_scr[...] += jnp.sum(df * (1.0 - sig), axis=0, keepdims=True)
            dp_ref[0, rows, :] = dq.astype(dp_ref.dtype)
            dp_ref[1, rows, :] = (df * (1.0 - lb) * sig * (1.0 - sig)).astype(dp_ref.dtype)
            dp_ref[2, rows, :] = dv.astype(dp_ref.dtype)
            return carry

        lax.fori_loop(0, ncb, chunk, 0)

        @pl.when(i == nblk - 1)
        def _():
            dlb = dlb_scr[...]
            dlbl_ref[0:1, :] = dlb * lb * (1.0 - lb)
            dlbl_ref[1:2, :] = -dlb * lb * s1

    col_spec = lambda off: pl.BlockSpec((tb, HEAD_DIM), lambda h, i: (nblk - 1 - i, off + h))
    return pl.pallas_call(
        body, name="hgrn_bwd", grid=(N_HEADS, nblk),
        in_specs=[col_spec(0), col_spec(4), col_spec(8), col_spec(0),
                  pl.BlockSpec((ncb, None, HEAD_DIM, HEAD_DIM), lambda h, i: (nblk - 1 - i, h, 0, 0)),
                  pl.BlockSpec((2, HEAD_DIM), lambda h, i: (0, h))],
        out_specs=[pl.BlockSpec((3, tb, HEAD_DIM), lambda h, i: (0, nblk - 1 - i, h)),
                   pl.BlockSpec((2, HEAD_DIM), lambda h, i: (0, h))],
        out_shape=[jax.ShapeDtypeStruct((3, t, HGRN_WIDTH), BF16), jax.ShapeDtypeStruct((2, HGRN_WIDTH), F32)],
        scratch_shapes=[pltpu.VMEM((HEAD_DIM, HEAD_DIM), F32), pltpu.VMEM((1, HEAD_DIM), F32)],
        compiler_params=_params("parallel", "arbitrary"),
    )(proj, proj, proj, do, states, lb_logits)


def _in_bwd(dph, dpg, w_in, dpre1):
    t = dpre1.shape[0]
    tm = min(t, 512)
    ng = IN_COLS // HGRN_WIDTH

    def body(dh_ref, dg_ref, w_ref, dp_ref, o_ref, acc):
        k = pl.program_id(1)

        @pl.when(k == 0)
        def _():
            acc[...] = jnp.zeros_like(acc)

        @pl.when(k < 3)
        def _():
            acc[...] += _dot(dh_ref[...], w_ref[...], NT)

        @pl.when(k >= 3)
        def _():
            acc[...] += _dot(dg_ref[...], w_ref[...], NT)

        @pl.when(k == ng - 1)
        def _():
            o_ref[...] = ALPHA * dp_ref[...] + acc[...]

    row = pl.BlockSpec((tm, D_MODEL), lambda i, k: (i, 0))
    return pl.pallas_call(
        body, name="in_bwd", grid=(t // tm, ng),
        in_specs=[pl.BlockSpec((None, tm, HGRN_WIDTH), lambda i, k: (jnp.minimum(k, 2), i, 0)),
                  pl.BlockSpec((None, tm, HGRN_WIDTH), lambda i, k: (jnp.maximum(k - 3, 0), i, 0)),
                  pl.BlockSpec((D_MODEL, HGRN_WIDTH), lambda i, k: (0, k)), row],
        out_specs=row,
        out_shape=jax.ShapeDtypeStruct((t, D_MODEL), F32),
        scratch_shapes=[pltpu.VMEM((tm, D_MODEL), F32)],
        compiler_params=_params("parallel", "arbitrary"),
    )(dph, dpg, w_in, dpre1)


def _mm_tn(name, operands, in_specs, out_spec, out_shape, grid, pick=None):
    def body(*refs):
        a_ref, b_refs, o_ref, acc = refs[0], refs[1:-2], refs[-2], refs[-1]
        k = pl.program_id(len(grid) - 1)

        @pl.when(k == 0)
        def _():
            acc[...] = jnp.zeros_like(acc)

        if pick is None:
            acc[...] += _dot(a_ref[...], b_refs[0][...], TN)
        else:
            for n, b_ref in enumerate(b_refs):
                @pl.when(pick(n))
                def _(b_ref=b_ref):
                    acc[...] += _dot(a_ref[...], b_ref[...], TN)

        @pl.when(k == grid[-1] - 1)
        def _():
            o_ref[...] = acc[...]

    return pl.pallas_call(
        body, name=name, grid=grid, in_specs=in_specs, out_specs=out_spec,
        out_shape=jax.ShapeDtypeStruct(out_shape, F32),
        scratch_shapes=[pltpu.VMEM(out_spec.block_shape, F32)],
        compiler_params=_params(*(["parallel"] * (len(grid) - 1) + ["arbitrary"])),
    )(*operands)


def _weight_grads(x, dph, dpg, cat, dpre1b, h1b, da, r, dpre2b):
    t = x.shape[0]
    tt = min(t, 512)
    nt = t // tt
    g_in = _mm_tn(
        "dw_in", (x, dph, dpg),
        [pl.BlockSpec((tt, D_MODEL), lambda j, k: (k, 0)),
         pl.BlockSpec((None, tt, HGRN_WIDTH), lambda j, k: (jnp.minimum(j, 2), k, 0)),
         pl.BlockSpec((None, tt, HGRN_WIDTH), lambda j, k: (jnp.maximum(j - 3, 0), k, 0))],
        pl.BlockSpec((D_MODEL, HGRN_WIDTH), lambda j, k: (0, j)), (D_MODEL, IN_COLS), (IN_COLS // HGRN_WIDTH, nt),
        pick=lambda n: (pl.program_id(0) < 3) if n == 0 else (pl.program_id(0) >= 3))
    g_out = _mm_tn(
        "dw_out", (cat, dpre1b),
        [pl.BlockSpec((None, tt, HGRN_WIDTH), lambda g, k: (g, k, 0)), pl.BlockSpec((tt, D_MODEL), lambda g, k: (k, 0))],
        pl.BlockSpec((HGRN_WIDTH, D_MODEL), lambda g, k: (g, 0)), (D_MODEL, D_MODEL), (2, nt))
    g_ff1 = _mm_tn(
        "dw_ff1", (h1b, da),
        [pl.BlockSpec((tt, D_MODEL), lambda j, k: (k, 0)), pl.BlockSpec((tt, 1024), lambda j, k: (k, j))],
        pl.BlockSpec((D_MODEL, 1024), lambda j, k: (0, j)), (D_MODEL, D_FF), (D_FF // 1024, nt))
    g_ff2 = _mm_tn(
        "dw_ff2", (r, dpre2b),
        [pl.BlockSpec((tt, 1024), lambda j, k: (k, j)), pl.BlockSpec((tt, D_MODEL), lambda j, k: (k, 0))],
        pl.BlockSpec((1024, D_MODEL), lambda j, k: (j, 0)), (D_FF, D_MODEL), (D_FF // 1024, nt))
    return g_in, g_out, g_ff1, g_ff2


def _place():
    x, y, c = lax.axis_index("x"), lax.axis_index("y"), lax.axis_index("c")
    return x, y, c, 2 * x + y


def _other_chips(x, y):
    return [(1 - x, y), (x, 1 - y), (1 - x, 1 - y)]


def _shard_views(refs, chip):
    r_in, r_out, r_ff1, r_ff2 = refs
    return [r_in.at[:, pl.ds(chip * (IN_COLS // N_CHIPS), IN_COLS // N_CHIPS)],
            r_out.at[pl.ds(chip * (D_MODEL // N_CHIPS), D_MODEL // N_CHIPS), :],
            r_ff1.at[:, pl.ds(chip * (D_FF // N_CHIPS), D_FF // N_CHIPS)],
            r_ff2.at[pl.ds(chip * (D_FF // N_CHIPS), D_FF // N_CHIPS), :]]


def _gather_weights(w_in, w_out, w_ff1, w_ff2, conv_w):
    n_t, n_p = 5, 3

    def body(s_in, s_out, s_ff1, s_ff2, s_cv, f_in, f_out, f_ff1, f_ff2, f_cv, send_sems, recv_sems, local_sems):
        x, y, c, me = _place()
        srcs = [s_in, s_out, s_ff1, s_ff2, s_cv]

        def dsts(chip):
            return _shard_views((f_in, f_out, f_ff1, f_ff2), chip) + [f_cv.at[chip]]

        local = [pltpu.make_async_copy(srcs[n], dsts(me)[n], local_sems.at[n]) for n in range(n_t)]
        for cp in local:
            cp.start()
        sends = []
        for j, (px, py) in enumerate(_other_chips(x, y)):
            for n in range(n_t):
                sends.append(pltpu.make_async_remote_copy(
                    src_ref=srcs[n], dst_ref=dsts(me)[n], send_sem=send_sems.at[n_p * n + j],
                    recv_sem=recv_sems.at[n_p * n + j], device_id=(px, py, c), device_id_type=MESH))
        for cp in sends:
            cp.start()
        for j, (px, py) in enumerate(_other_chips(x, y)):
            for n in range(n_t):
                pltpu.make_async_remote_copy(
                    src_ref=srcs[n], dst_ref=dsts(2 * px + py)[n], send_sem=send_sems.at[n_p * n + j],
                    recv_sem=recv_sems.at[n_p * n + j], device_id=(px, py, c), device_id_type=MESH).wait_recv()
        for cp in sends:
            cp.wait_send()
        for cp in local:
            cp.wait()

    return pl.pallas_call(
        body, name="gather_weights", in_specs=[ANY] * n_t, out_specs=[ANY] * n_t,
        out_shape=[jax.ShapeDtypeStruct((D_MODEL, IN_COLS), w_in.dtype), jax.ShapeDtypeStruct((D_MODEL, D_MODEL), w_out.dtype),
                   jax.ShapeDtypeStruct((D_MODEL, D_FF), w_ff1.dtype), jax.ShapeDtypeStruct((D_FF, D_MODEL), w_ff2.dtype),
                   jax.ShapeDtypeStruct((N_CHIPS,) + conv_w.shape, conv_w.dtype)],
        scratch_shapes=[pltpu.SemaphoreType.DMA((n_t * n_p,)), pltpu.SemaphoreType.DMA((n_t * n_p,)),
                        pltpu.SemaphoreType.DMA((n_t,))],
    )(w_in, w_out, w_ff1, w_ff2, conv_w)


def _half_views(refs, h):
    r_in, r_out, r_ff1, r_ff2 = refs
    return [r_in.at[pl.ds(h * (D_MODEL // 2), D_MODEL // 2), :], r_out.at[:, pl.ds(h * (D_MODEL // 2), D_MODEL // 2)],
            r_ff1.at[pl.ds(h * (D_MODEL // 2), D_MODEL // 2), :], r_ff2.at[:, pl.ds(h * (D_MODEL // 2), D_MODEL // 2)]]


HALF_SHAPES = [(D_MODEL // 2, IN_COLS), (D_MODEL, D_MODEL // 2), (D_MODEL // 2, D_FF), (D_FF, D_MODEL // 2)]
PIECE_SHAPES = [(D_MODEL // 2, IN_COLS // N_CHIPS), (D_MODEL // N_CHIPS, D_MODEL // 2),
                (D_MODEL // 2, D_FF // N_CHIPS), (D_FF // N_CHIPS, D_MODEL // 2)]
SHARD_SHAPES = [(D_MODEL, IN_COLS // N_CHIPS), (D_MODEL // N_CHIPS, D_MODEL), (D_MODEL, D_FF // N_CHIPS),
                (D_FF // N_CHIPS, D_MODEL)]


def _swap_halves(grads):
    def body(g_in, g_out, g_ff1, g_ff2, r_in, r_out, r_ff1, r_ff2, send_sems, recv_sems):
        x, y, c, _ = _place()
        srcs = _half_views((g_in, g_out, g_ff1, g_ff2), 1 - c)
        copies = [pltpu.make_async_remote_copy(
            src_ref=srcs[n], dst_ref=dst, send_sem=send_sems.at[n], recv_sem=recv_sems.at[n],
            device_id=(x, y, 1 - c), device_id_type=MESH) for n, dst in enumerate((r_in, r_out, r_ff1, r_ff2))]
        for cp in copies:
            cp.start()
        for cp in copies:
            cp.wait()

    return pl.pallas_call(
        body, name="swap_halves", in_specs=[ANY] * 4, out_specs=[ANY] * 4,
        out_shape=[jax.ShapeDtypeStruct(s, F32) for s in HALF_SHAPES],
        scratch_shapes=[pltpu.SemaphoreType.DMA((4,)), pltpu.SemaphoreType.DMA((4,))],
    )(*grads)


def _add_half(name, g, recv, core, rows_split):
    shape = recv.shape
    tr = min(shape[0], 128 if rows_split else 256)
    nb = shape[0] // tr

    def body(c_ref, g_ref, r_ref, o_ref):
        o_ref[...] = (g_ref[...] + r_ref[...]).astype(o_ref.dtype)

    g_map = (lambda i, c_ref: (c_ref[0] * nb + i, 0)) if rows_split else (lambda i, c_ref: (i, c_ref[0]))
    blk = pl.BlockSpec((tr, shape[1]), lambda i, c_ref: (i, 0))
    return pl.pallas_call(
        body, name=name,
        grid_spec=pltpu.PrefetchScalarGridSpec(
            num_scalar_prefetch=1, grid=(nb,),
            in_specs=[pl.BlockSpec((tr, shape[1]), g_map), blk], out_specs=blk),
        out_shape=jax.ShapeDtypeStruct(shape, BF16),
        compiler_params=_params("parallel"),
    )(core, g, recv)


def _piece_views(refs, chip):
    return _shard_views(refs, chip)


def _exchange_pieces(halves):
    n_t, n_p = 4, 3

    def body(p_in, p_out, p_ff1, p_ff2, r_in, r_out, r_ff1, r_ff2, send_sems, recv_sems, local_sems):
        x, y, c, me = _place()
        pieces = lambda chip: _piece_views((p_in, p_out, p_ff1, p_ff2), chip)
        slots = lambda chip: [r.at[chip] for r in (r_in, r_out, r_ff1, r_ff2)]
        local = [pltpu.make_async_copy(pieces(me)[n], slots(me)[n], local_sems.at[n]) for n in range(n_t)]
        for cp in local:
            cp.start()
        sends = []
        for j, (px, py) in enumerate(_other_chips(x, y)):
            for n in range(n_t):
                sends.append(pltpu.make_async_remote_copy(
                    src_ref=pieces(2 * px + py)[n], dst_ref=slots(me)[n], send_sem=send_sems.at[n_p * n + j],
                    recv_sem=recv_sems.at[n_p * n + j], device_id=(px, py, c), device_id_type=MESH))
        for cp in sends:
            cp.start()
        for j, (px, py) in enumerate(_other_chips(x, y)):
            for n in range(n_t):
                pltpu.make_async_remote_copy(
                    src_ref=pieces(me)[n], dst_ref=slots(2 * px + py)[n], send_sem=send_sems.at[n_p * n + j],
                    recv_sem=recv_sems.at[n_p * n + j], device_id=(px, py, c), device_id_type=MESH).wait_recv()
        for cp in sends:
            cp.wait_send()
        for cp in local:
            cp.wait()

    return pl.pallas_call(
        body, name="exchange_pieces", in_specs=[ANY] * n_t, out_specs=[ANY] * n_t,
        out_shape=[jax.ShapeDtypeStruct((N_CHIPS,) + s, BF16) for s in PIECE_SHAPES],
        scratch_shapes=[pltpu.SemaphoreType.DMA((n_t * n_p,)), pltpu.SemaphoreType.DMA((n_t * n_p,)),
                        pltpu.SemaphoreType.DMA((n_t,))],
    )(*halves)


def _sum_slots(name, slots):
    shape = slots.shape[1:]
    tr = min(shape[0], 256)

    def body(s_ref, o_ref):
        total = s_ref[0].astype(F32)
        for q in range(1, N_CHIPS):
            total = total + s_ref[q].astype(F32)
        o_ref[...] = total

    return pl.pallas_call(
        body, name=name, grid=(shape[0] // tr,),
        in_specs=[pl.BlockSpec((N_CHIPS, tr, shape[1]), lambda i: (0, i, 0))],
        out_specs=pl.BlockSpec((tr, shape[1]), lambda i: (i, 0)),
        out_shape=jax.ShapeDtypeStruct(shape, F32),
        compiler_params=_params("parallel"),
    )(slots)


def _join_halves(pieces):
    def body(p_in, p_out, p_ff1, p_ff2, g_in, g_out, g_ff1, g_ff2, send_sems, recv_sems, local_sems):
        x, y, c, _ = _place()
        srcs = (p_in, p_out, p_ff1, p_ff2)
        mine = _half_views_shard((g_in, g_out, g_ff1, g_ff2), c)
        theirs = _half_views_shard((g_in, g_out, g_ff1, g_ff2), 1 - c)
        local = [pltpu.make_async_copy(srcs[n], mine[n], local_sems.at[n]) for n in range(4)]
        for cp in local:
            cp.start()
        sends = [pltpu.make_async_remote_copy(
            src_ref=srcs[n], dst_ref=mine[n], send_sem=send_sems.at[n], recv_sem=recv_sems.at[n],
            device_id=(x, y, 1 - c), device_id_type=MESH) for n in range(4)]
        for cp in sends:
            cp.start()
        for n in range(4):
            pltpu.make_async_remote_copy(
                src_ref=srcs[n], dst_ref=theirs[n], send_sem=send_sems.at[n], recv_sem=recv_sems.at[n],
                device_id=(x, y, 1 - c), device_id_type=MESH).wait_recv()
        for cp in sends:
            cp.wait_send()
        for cp in local:
            cp.wait()

    return pl.pallas_call(
        body, name="join_halves", in_specs=[ANY] * 4, out_specs=[ANY] * 4,
        out_shape=[jax.ShapeDtypeStruct(s, F32) for s in SHARD_SHAPES],
        scratch_shapes=[pltpu.SemaphoreType.DMA((4,)), pltpu.SemaphoreType.DMA((4,)), pltpu.SemaphoreType.DMA((4,))],
    )(*pieces)


def _half_views_shard(refs, h):
    r_in, r_out, r_ff1, r_ff2 = refs
    half = D_MODEL // 2
    return [r_in.at[pl.ds(h * half, half), :], r_out.at[:, pl.ds(h * half, half)],
            r_ff1.at[pl.ds(h * half, half), :], r_ff2.at[:, pl.ds(h * half, half)]]


def _sum_small(pack):
    n_dev = 8

    def body(p_ref, o_ref, slots, send_sems, recv_sems):
        x, y, c, _ = _place()
        me = 4 * x + 2 * y + c
        slots[me] = p_ref[...]
        sends = []
        for m in range(1, n_dev):
            peer = ((1 - x) if m & 4 else x, (1 - y) if m & 2 else y, (1 - c) if m & 1 else c)
            sends.append(pltpu.make_async_remote_copy(
                src_ref=p_ref, dst_ref=slots.at[me], send_sem=send_sems.at[m - 1], recv_sem=recv_sems.at[m - 1],
                device_id=peer, device_id_type=MESH))
        for cp in sends:
            cp.start()
        for m in range(1, n_dev):
            peer = ((1 - x) if m & 4 else x, (1 - y) if m & 2 else y, (1 - c) if m & 1 else c)
            pltpu.make_async_remote_copy(
                src_ref=p_ref, dst_ref=slots.at[4 * peer[0] + 2 * peer[1] + peer[2]], send_sem=send_sems.at[m - 1],
                recv_sem=recv_sems.at[m - 1], device_id=peer, device_id_type=MESH).wait_recv()
        for cp in sends:
            cp.wait_send()
        total = slots[0]
        for d in range(1, n_dev):
            total = total + slots[d]
        o_ref[...] = total

    vm = pl.BlockSpec(memory_space=pltpu.VMEM)
    return pl.pallas_call(
        body, name="sum_small", in_specs=[vm], out_specs=vm,
        out_shape=jax.ShapeDtypeStruct(pack.shape, F32),
        scratch_shapes=[pltpu.VMEM((n_dev,) + pack.shape, F32), pltpu.SemaphoreType.DMA((n_dev - 1,)),
                        pltpu.SemaphoreType.DMA((n_dev - 1,))],
    )(pack)


def _adamw(name, w, g, m, v):
    rows, cols = w.shape
    tr = min(rows, 256)

    def body(w_ref, g_ref, m_ref, v_ref, d_ref, nm_ref, nv_ref):
        gv = g_ref[...]
        nm = ADAM_B1 * m_ref[...] + (1.0 - ADAM_B1) * gv
        nv = ADAM_B2 * v_ref[...] + (1.0 - ADAM_B2) * jnp.square(gv)
        m_hat = nm / (1.0 - ADAM_B1 ** ADAM_STEP)
        v_hat = nv / (1.0 - ADAM_B2 ** ADAM_STEP)
        d_ref[...] = -ADAM_LR * (m_hat / (jnp.sqrt(v_hat) + ADAM_EPS) + ADAM_WD * w_ref[...])
        nm_ref[...] = nm
        nv_ref[...] = nv

    blk = pl.BlockSpec((tr, cols), lambda i: (i, 0))
    return pl.pallas_call(
        body, name=name, grid=(rows // tr,), in_specs=[blk] * 4, out_specs=[blk] * 3,
        out_shape=[jax.ShapeDtypeStruct(w.shape, F32)] * 3,
        compiler_params=_params("parallel"),
    )(w, g, m, v)


def kernel(x, w_in, lb_logits, gate_norm_w, conv_w, w_out, ln1_g, ln1_b, w_ff1, w_ff2, ln2_g, ln2_b, loss_target, m_w_in, m_lb_logits, m_gate_norm_w, m_conv_w, m_w_out, m_ln1_g, m_ln1_b, m_w_ff1, m_w_ff2, m_ln2_g, m_ln2_b, v_w_in, v_lb_logits, v_gate_norm_w, v_conv_w, v_w_out, v_ln1_g, v_ln1_b, v_w_ff1, v_w_ff2, v_ln2_g, v_ln2_b):
    xs, tgt = x[0], loss_target[0]
    chip = 2 * lax.axis_index("x") + lax.axis_index("y")
    core = lax.axis_index("c").astype(jnp.int32).reshape(1)

    wb_in, wb_out, wb_ff1, wb_ff2, cv4 = _gather_weights(
        w_in[0].astype(BF16), w_out[0].astype(BF16), w_ff1[0].astype(BF16), w_ff2[0].astype(BF16), conv_w[0])
    conv_full = cv4.transpose(1, 0, 2).reshape(3, CONV_WIDTH)

    proj = _in_proj(xs, wb_in)
    o, states = _hgrn_fwd(proj, lb_logits)
    cat = _gate_fwd(proj, o, gate_norm_w, conv_full)
    xhat1, h1b, rstd1 = _out_ln1(cat, wb_out, xs, ln1_g, ln1_b)
    a, r = _ff1(h1b, wb_ff1)
    dpre2, dpre2b, g_ln2_g, g_ln2_b, loss8 = _ff2_ln2_loss(r, wb_ff2, xhat1, ln1_g, ln1_b, ln2_g, ln2_b, tgt)
    loss = lax.psum(loss8[0, 0], ("x", "y", "c"))

    da = _ff2_bwd(dpre2b, wb_ff2, a)
    dpre1, dpre1b, g_ln1_g, g_ln1_b = _ff1_bwd_ln1(da, wb_ff1, dpre2, xhat1, rstd1, ln1_g)
    dcat = _out_bwd(dpre1b, wb_out)
    do, dpg, g_gnw, g_conv = _gate_bwd(dcat, o, proj, gate_norm_w, conv_full)
    dph, g_lbl = _hgrn_bwd(proj, do, states, lb_logits)
    grad_x = _in_bwd(dph, dpg, wb_in, dpre1)
    grads = _weight_grads(xs, dph, dpg, cat, dpre1b, h1b, da, r, dpre2b)

    recv = _swap_halves(grads)
    names = ("w_in", "w_out", "w_ff1", "w_ff2")
    halves = [_add_half("add_half_" + n, g, rv, core, rs) for n, g, rv, rs in zip(names, grads, recv, (True, False, True, False))]
    slots = _exchange_pieces(halves)
    pieces = [_sum_slots("sum_slots_" + n, s) for n, s in zip(names, slots)]
    g_w_in, g_w_out, g_w_ff1, g_w_ff2 = _join_halves(pieces)

    pack = jnp.concatenate([
        g_ln1_g, g_ln1_b, g_ln2_g, g_ln2_b,
        jnp.concatenate([g_lbl[0:1], g_lbl[1:2]], axis=1),
        jnp.concatenate([g_gnw, g_conv[0:1]], axis=1),
        jnp.concatenate([g_conv[1:2], g_conv[2:3]], axis=1),
        jnp.zeros((1, D_MODEL), F32)], axis=0)
    tot = _sum_small(pack)
    half = D_MODEL // 2
    g_lb_logits = jnp.concatenate([tot[4:5, :half], tot[4:5, half:]], axis=0)
    g_gate_norm_w = tot[5:6, :half]
    g_conv_full = jnp.concatenate([tot[5:6, half:], tot[6:7, :half], tot[6:7, half:]], axis=0)
    g_conv_w = lax.dynamic_slice(g_conv_full, (0, chip * LANES), (3, LANES))

    d_in, nm_in, nv_in = _adamw("adamw_w_in", w_in[0], g_w_in, m_w_in[0], v_w_in[0])
    d_out, nm_out, nv_out = _adamw("adamw_w_out", w_out[0], g_w_out, m_w_out[0], v_w_out[0])
    d_ff1, nm_ff1, nv_ff1 = _adamw("adamw_w_ff1", w_ff1[0], g_w_ff1, m_w_ff1[0], v_w_ff1[0])
    d_ff2, nm_ff2, nv_ff2 = _adamw("adamw_w_ff2", w_ff2[0], g_w_ff2, m_w_ff2[0], v_w_ff2[0])

    def small_pack(lbl, gnw, cv, l1g, l1b, l2g, l2b):
        pad = jnp.zeros((1, D_MODEL - 3 * LANES), F32)
        return jnp.concatenate([
            l1g, l1b, l2g, l2b, jnp.concatenate([lbl[0:1], lbl[1:2]], axis=1),
            jnp.concatenate([gnw, jnp.zeros((1, half), F32)], axis=1),
            jnp.concatenate([cv[0:1], cv[1:2], cv[2:3], pad], axis=1), jnp.zeros((1, D_MODEL), F32)], axis=0)

    w_s = small_pack(lb_logits, gate_norm_w, conv_w[0], ln1_g, ln1_b, ln2_g, ln2_b)
    g_s = small_pack(g_lb_logits, g_gate_norm_w, g_conv_w, tot[0:1], tot[1:2], tot[2:3], tot[3:4])
    m_s = small_pack(m_lb_logits, m_gate_norm_w, m_conv_w[0], m_ln1_g, m_ln1_b, m_ln2_g, m_ln2_b)
    v_s = small_pack(v_lb_logits, v_gate_norm_w, v_conv_w[0], v_ln1_g, v_ln1_b, v_ln2_g, v_ln2_b)
    d_s, nm_s, nv_s = _adamw("adamw_small", w_s, g_s, m_s, v_s)

    def unpack(p):
        lbl = jnp.concatenate([p[4:5, :half], p[4:5, half:]], axis=0)
        cv = jnp.concatenate([p[6:7, 0:LANES], p[6:7, LANES:2 * LANES], p[6:7, 2 * LANES:3 * LANES]], axis=0)
        return dict(lb_logits=lbl, gate_norm_w=p[5:6, :half], conv_w=cv[None], ln1_g=p[0:1], ln1_b=p[1:2],
                    ln2_g=p[2:3], ln2_b=p[3:4])

    order = ("w_in", "lb_logits", "gate_norm_w", "conv_w", "w_out", "ln1_g", "ln1_b", "w_ff1", "w_ff2", "ln2_g", "ln2_b")
    grad = dict(unpack(g_s), w_in=g_w_in[None], w_out=g_w_out[None], w_ff1=g_w_ff1[None], w_ff2=g_w_ff2[None])
    delta = dict(unpack(d_s), w_in=d_in[None], w_out=d_out[None], w_ff1=d_ff1[None], w_ff2=d_ff2[None])
    new_m = dict(unpack(nm_s), w_in=nm_in[None], w_out=nm_out[None], w_ff1=nm_ff1[None], w_ff2=nm_ff2[None])
    new_v = dict(unpack(nv_s), w_in=nv_in[None], w_out=nv_out[None], w_ff1=nv_ff1[None], w_ff2=nv_ff2[None])
    return (loss, grad_x[None], *[grad[n] for n in order], *[delta[n] for n in order],
            *[new_m[n] for n in order], *[new_v[n] for n in order])
```

```python
import jax
import jax.numpy as jnp
from jax import lax
from jax.experimental import pallas as pl
from jax.experimental.pallas import tpu as pltpu

F32 = jnp.float32
BF16 = jnp.bfloat16
MXU_DTYPE = jnp.bfloat16

D_MODEL = 1024
HGRN_WIDTH = 512
HEAD_DIM = 128
N_HEADS = 4
CONV_WIDTH = 512
CHUNK = 64
D_FF = 4096
IN_COLS = 3584
ALPHA = 2.0 ** 0.25
EPS = 1e-5
N_CHIPS = 4
ADAM_LR, ADAM_B1, ADAM_B2, ADAM_EPS, ADAM_WD, ADAM_STEP = 0.001, 0.9, 0.999, 1e-08, 0.01, 10

LANES = 128
SUBLANES = 8
VMEM_LIMIT = 48 * 1024 * 1024

NN = (((1,), (0,)), ((), ()))
NT = (((1,), (1,)), ((), ()))
TN = (((0,), (0,)), ((), ()))
MESH = pl.DeviceIdType.MESH
ANY = pl.BlockSpec(memory_space=pl.ANY)


def _dot(a, b, dims):
    return lax.dot_general(a.astype(MXU_DTYPE), b.astype(MXU_DTYPE), dims, preferred_element_type=F32)


def _dot_exact(a, b):
    return lax.dot_general(a, b, NN, precision=lax.Precision.HIGHEST, preferred_element_type=F32)


def _params(*sem):
    return pltpu.CompilerParams(dimension_semantics=sem, vmem_limit_bytes=VMEM_LIMIT)


def _sigmoid(v):
    return 1.0 / (1.0 + jnp.exp(-v))


def _lower_bound(lbl):
    m = jnp.max(lbl, axis=0, keepdims=True)
    e = jnp.exp(lbl - m)
    s = e / jnp.sum(e, axis=0, keepdims=True)
    return s[0:1, :], s[1:2, :]


def _in_proj(x, w_in, after):
    t = x.shape[0]
    tm, tn = min(t, 512), 512

    def body(x_ref, w_ref, after_ref, o_ref):
        o_ref[...] = _dot(x_ref[...], w_ref[...], NN)

    return pl.pallas_call(
        body, name="in_proj", grid=(t // tm, IN_COLS // tn),
        in_specs=[pl.BlockSpec((tm, D_MODEL), lambda i, j: (i, 0)), pl.BlockSpec((D_MODEL, tn), lambda i, j: (0, j)), ANY],
        out_specs=pl.BlockSpec((tm, tn), lambda i, j: (i, j)),
        out_shape=jax.ShapeDtypeStruct((t, IN_COLS), F32),
        compiler_params=_params("parallel", "parallel"),
    )(x, w_in, after)


def _gates(fp, lb):
    sig = _sigmoid(fp)
    f = lb + (1.0 - lb) * sig
    return sig, f, jnp.log(f), 1.0 - f


def _hgrn_fwd(proj, lb_logits):
    t = proj.shape[0]
    tb = min(t, 512)
    ncb = tb // CHUNK
    nblk = t // tb

    def body(q_ref, f_ref, v_ref, lbl_ref, o_ref, st_ref, s_scr):
        @pl.when(pl.program_id(1) == 0)
        def _():
            s_scr[...] = jnp.zeros_like(s_scr)

        lb, _ = _lower_bound(lbl_ref[...])
        row = lax.broadcasted_iota(jnp.int32, (CHUNK, CHUNK), 0)
        col = lax.broadcasted_iota(jnp.int32, (CHUNK, CHUNK), 1)
        causal = row >= col
        tri = causal.astype(F32)

        def chunk(c, carry):
            rows = pl.ds(pl.multiple_of(c * CHUNK, CHUNK), CHUNK)
            q, v = q_ref[rows, :], v_ref[rows, :]
            _, _, g, k = _gates(f_ref[rows, :], lb)
            b = _dot_exact(tri, g)
            b_ref, b_last = b[CHUNK // 2:CHUNK // 2 + 1, :], b[CHUNK - 1:CHUNK, :]
            scores = _dot(q * jnp.exp(b - b_ref), k * jnp.exp(b_ref - b), NT)
            scores = jnp.where(causal, scores, 0.0)
            o_intra = _dot(scores, v, NN)
            st = s_scr[...]
            st_ref[c] = st
            o_inter = _dot(q * jnp.exp(b), st, NT)
            s_scr[...] = jnp.exp(b_last) * st + _dot(v, k * jnp.exp(b_last - b), TN)
            o_ref[rows, :] = o_intra + o_inter
            return carry

        lax.fori_loop(0, ncb, chunk, 0)

    col_spec = lambda off: pl.BlockSpec((tb, HEAD_DIM), lambda h, i: (i, off + h))
    return pl.pallas_call(
        body, name="hgrn_fwd", grid=(N_HEADS, nblk),
        in_specs=[col_spec(0), col_spec(4), col_spec(8), pl.BlockSpec((2, HEAD_DIM), lambda h, i: (0, h))],
        out_specs=[pl.BlockSpec((tb, HEAD_DIM), lambda h, i: (i, h)),
                   pl.BlockSpec((ncb, None, HEAD_DIM, HEAD_DIM), lambda h, i: (i, h, 0, 0))],
        out_shape=[jax.ShapeDtypeStruct((t, HGRN_WIDTH), F32),
                   jax.ShapeDtypeStruct((t // CHUNK, N_HEADS, HEAD_DIM, HEAD_DIM), F32)],
        scratch_shapes=[pltpu.VMEM((HEAD_DIM, HEAD_DIM), F32)],
        compiler_params=_params("parallel", "arbitrary"),
    )(proj, proj, proj, lb_logits)


def _conv_taps(z, halo, zbuf, tb):
    zbuf[0:SUBLANES, :] = halo
    zbuf[SUBLANES:SUBLANES + tb, :] = z
    return zbuf[SUBLANES - 1:SUBLANES - 1 + tb, :], zbuf[SUBLANES - 2:SUBLANES - 2 + tb, :]


def _gate_fwd(proj, o, gate_norm_w, conv_w):
    t = proj.shape[0]
    tb = min(t, 512)
    hb = tb // SUBLANES

    def body(o_ref, og_ref, gnw_ref, b_ref, c_ref, u_ref, ch_ref, uh_ref, cw_ref, cat_ref, zbuf):
        i = pl.program_id(1)
        ov, og = o_ref[...], og_ref[...]
        on = ov * lax.rsqrt(jnp.mean(ov * ov, axis=-1, keepdims=True) + EPS)
        cat_ref[0] = (on * gnw_ref[...] * (og * _sigmoid(og))).astype(cat_ref.dtype)
        z = c_ref[...] * u_ref[...]
        halo = jnp.where(i > 0, ch_ref[...] * uh_ref[...], 0.0)
        z1, z2 = _conv_taps(z, halo, zbuf, tb)
        cw = cw_ref[...]
        yc = cw[2:3, :] * z + cw[1:2, :] * z1 + cw[0:1, :] * z2
        cat_ref[1] = (b_ref[...] * yc).astype(cat_ref.dtype)

    blk = lambda off: pl.BlockSpec((tb, LANES), lambda j, i: (i, off + j))
    prev = lambda off: pl.BlockSpec((SUBLANES, LANES), lambda j, i: (jnp.maximum(i * hb - 1, 0), off + j))
    vec = lambda r: pl.BlockSpec((r, LANES), lambda j, i: (0, j))
    return pl.pallas_call(
        body, name="gate_fwd", grid=(4, t // tb),
        in_specs=[blk(0), blk(12), vec(1), blk(16), blk(20), blk(24), prev(20), prev(24), vec(3)],
        out_specs=pl.BlockSpec((2, tb, LANES), lambda j, i: (0, i, j)),
        out_shape=jax.ShapeDtypeStruct((2, t, HGRN_WIDTH), BF16),
        scratch_shapes=[pltpu.VMEM((tb + SUBLANES, LANES), F32)],
        compiler_params=_params("parallel", "arbitrary"),
    )(o, proj, gate_norm_w, proj, proj, proj, proj, proj, conv_w)


def _out_ln1(cat, w_out, x, g1, b1):
    t = x.shape[0]
    tm = min(t, 256)

    def body(cat_ref, w_ref, x_ref, g_ref, b_ref, xhat_ref, h1_ref, rstd_ref, acc):
        k = pl.program_id(1)

        @pl.when(k == 0)
        def _():
            acc[...] = jnp.zeros_like(acc)

        acc[...] += _dot(cat_ref[...], w_ref[...], NN)

        @pl.when(k == 1)
        def _():
            pre = ALPHA * x_ref[...] + acc[...]
            xc = pre - jnp.mean(pre, axis=-1, keepdims=True)
            rstd = lax.rsqrt(jnp.mean(xc * xc, axis=-1, keepdims=True) + EPS)
            xhat = xc * rstd
            xhat_ref[...] = xhat
            h1_ref[...] = (xhat * g_ref[...] + b_ref[...]).astype(h1_ref.dtype)
            rstd_ref[...] = rstd

    row = pl.BlockSpec((tm, D_MODEL), lambda i, k: (i, 0))
    vec = pl.BlockSpec((1, D_MODEL), lambda i, k: (0, 0))
    return pl.pallas_call(
        body, name="out_ln1", grid=(t // tm, 2),
        in_specs=[pl.BlockSpec((None, tm, HGRN_WIDTH), lambda i, k: (k, i, 0)),
                  pl.BlockSpec((HGRN_WIDTH, D_MODEL), lambda i, k: (k, 0)), row, vec, vec],
        out_specs=[row, row, pl.BlockSpec((tm, 1), lambda i, k: (i, 0))],
        out_shape=[jax.ShapeDtypeStruct((t, D_MODEL), F32), jax.ShapeDtypeStruct((t, D_MODEL), BF16),
                   jax.ShapeDtypeStruct((t, 1), F32)],
        scratch_shapes=[pltpu.VMEM((tm, D_MODEL), F32)],
        compiler_params=_params("parallel", "arbitrary"),
    )(cat, w_out, x, g1, b1)


def _ff1(h1b, w_ff1):
    t = h1b.shape[0]
    tm, tn = min(t, 512), 1024

    def body(h_ref, w_ref, a_ref, r_ref):
        a = _dot(h_ref[...], w_ref[...], NN)
        a_ref[...] = a
        r_ref[...] = jnp.square(jnp.maximum(a, 0.0)).astype(r_ref.dtype)

    out = pl.BlockSpec((tm, tn), lambda i, j: (i, j))
    return pl.pallas_call(
        body, name="ff1", grid=(t // tm, D_FF // tn),
        in_specs=[pl.BlockSpec((tm, D_MODEL), lambda i, j: (i, 0)), pl.BlockSpec((D_MODEL, tn), lambda i, j: (0, j))],
        out_specs=[out, out],
        out_shape=[jax.ShapeDtypeStruct((t, D_FF), F32), jax.ShapeDtypeStruct((t, D_FF), BF16)],
        compiler_params=_params("parallel", "parallel"),
    )(h1b, w_ff1)


def _ln_bwd(dy, xhat, rstd, g):
    dxhat = dy * g
    m1 = jnp.mean(dxhat, axis=-1, keepdims=True)
    m2 = jnp.mean(dxhat * xhat, axis=-1, keepdims=True)
    return rstd * (dxhat - m1 - xhat * m2)


def _ff2_ln2_loss(r, w_ff2, xhat1, g1, b1, g2, b2, target):
    t = r.shape[0]
    tm, tk = min(t, 256), 1024
    nk = D_FF // tk

    def body(r_ref, w_ref, xh_ref, g1_ref, b1_ref, g2_ref, b2_ref, tg_ref,
             dpre_ref, dpreb_ref, dg_ref, db_ref, loss_ref, acc):
        i, k = pl.program_id(0), pl.program_id(1)

        @pl.when(k == 0)
        def _():
            acc[...] = jnp.zeros_like(acc)

        acc[...] += _dot(r_ref[...], w_ref[...], NN)

        @pl.when(k == nk - 1)
        def _():
            @pl.when(i == 0)
            def _():
                dg_ref[...] = jnp.zeros_like(dg_ref)
                db_ref[...] = jnp.zeros_like(db_ref)
                loss_ref[...] = jnp.zeros_like(loss_ref)

            h1 = xh_ref[...] * g1_ref[...] + b1_ref[...]
            pre = ALPHA * h1 + acc[...]
            xc = pre - jnp.mean(pre, axis=-1, keepdims=True)
            rstd = lax.rsqrt(jnp.mean(xc * xc, axis=-1, keepdims=True) + EPS)
            xhat = xc * rstd
            err = xhat * g2_ref[...] + b2_ref[...] - tg_ref[...]
            loss_ref[...] += 0.5 * jnp.sum(jnp.mean(err * err, axis=-1, keepdims=True))
            dy = err * (1.0 / D_MODEL)
            dg_ref[...] += jnp.sum(dy * xhat, axis=0, keepdims=True)
            db_ref[...] += jnp.sum(dy, axis=0, keepdims=True)
            dpre = _ln_bwd(dy, xhat, rstd, g2_ref[...])
            dpre_ref[...] = dpre
            dpreb_ref[...] = dpre.astype(dpreb_ref.dtype)

    row = pl.BlockSpec((tm, D_MODEL), lambda i, k: (i, 0))
    vec = pl.BlockSpec((1, D_MODEL), lambda i, k: (0, 0))
    return pl.pallas_call(
        body, name="ff2_ln2_loss", grid=(t // tm, nk),
        in_specs=[pl.BlockSpec((tm, tk), lambda i, k: (i, k)), pl.BlockSpec((tk, D_MODEL), lambda i, k: (k, 0)),
                  row, vec, vec, vec, vec, row],
        out_specs=[row, row, vec, vec, pl.BlockSpec((SUBLANES, LANES), lambda i, k: (0, 0))],
        out_shape=[jax.ShapeDtypeStruct((t, D_MODEL), F32), jax.ShapeDtypeStruct((t, D_MODEL), BF16),
                   jax.ShapeDtypeStruct((1, D_MODEL), F32), jax.ShapeDtypeStruct((1, D_MODEL), F32),
                   jax.ShapeDtypeStruct((SUBLANES, LANES), F32)],
        scratch_shapes=[pltpu.VMEM((tm, D_MODEL), F32)],
        compiler_params=_params("arbitrary", "arbitrary"),
    )(r, w_ff2, xhat1, g1, b1, g2, b2, target)


def _ff2_bwd(dpre2b, w_ff2, a):
    t = a.shape[0]
    tm, tn = min(t, 512), 1024

    def body(d_ref, w_ref, a_ref, da_ref):
        dr = _dot(d_ref[...], w_ref[...], NT)
        da_ref[...] = (dr * (2.0 * jnp.maximum(a_ref[...], 0.0))).astype(da_ref.dtype)

    return pl.pallas_call(
        body, name="ff2_bwd", grid=(t // tm, D_FF // tn),
        in_specs=[pl.BlockSpec((tm, D_MODEL), lambda i, j: (i, 0)), pl.BlockSpec((tn, D_MODEL), lambda i, j: (j, 0)),
                  pl.BlockSpec((tm, tn), lambda i, j: (i, j))],
        out_specs=pl.BlockSpec((tm, tn), lambda i, j: (i, j)),
        out_shape=jax.ShapeDtypeStruct((t, D_FF), BF16),
        compiler_params=_params("parallel", "parallel"),
    )(dpre2b, w_ff2, a)


def _ff1_bwd_ln1(da, w_ff1, dpre2, xhat1, rstd1, g1):
    t = da.shape[0]
    tm, tk = min(t, 256), 1024
    nk = D_FF // tk

    def body(da_ref, w_ref, dp2_ref, xh_ref, rs_ref, g_ref, dpre_ref, dpreb_ref, dg_ref, db_ref, acc):
        i, k = pl.program_id(0), pl.program_id(1)

        @pl.when(k == 0)
        def _():
            acc[...] = jnp.zeros_like(acc)

        acc[...] += _dot(da_ref[...], w_ref[...], NT)

        @pl.when(k == nk - 1)
        def _():
            @pl.when(i == 0)
            def _():
                dg_ref[...] = jnp.zeros_like(dg_ref)
                db_ref[...] = jnp.zeros_like(db_ref)

            dh1 = ALPHA * dp2_ref[...] + acc[...]
            xhat = xh_ref[...]
            dg_ref[...] += jnp.sum(dh1 * xhat, axis=0, keepdims=True)
            db_ref[...] += jnp.sum(dh1, axis=0, keepdims=True)
            dpre = _ln_bwd(dh1, xhat, rs_ref[...], g_ref[...])
            dpre_ref[...] = dpre
            dpreb_ref[...] = dpre.astype(dpreb_ref.dtype)

    row = pl.BlockSpec((tm, D_MODEL), lambda i, k: (i, 0))
    vec = pl.BlockSpec((1, D_MODEL), lambda i, k: (0, 0))
    return pl.pallas_call(
        body, name="ff1_bwd_ln1", grid=(t // tm, nk),
        in_specs=[pl.BlockSpec((tm, tk), lambda i, k: (i, k)), pl.BlockSpec((D_MODEL, tk), lambda i, k: (0, k)),
                  row, row, pl.BlockSpec((tm, 1), lambda i, k: (i, 0)), vec],
        out_specs=[row, row, vec, vec],
        out_shape=[jax.ShapeDtypeStruct((t, D_MODEL), F32), jax.ShapeDtypeStruct((t, D_MODEL), BF16),
                   jax.ShapeDtypeStruct((1, D_MODEL), F32), jax.ShapeDtypeStruct((1, D_MODEL), F32)],
        scratch_shapes=[pltpu.VMEM((tm, D_MODEL), F32)],
        compiler_params=_params("arbitrary", "arbitrary"),
    )(da, w_ff1, dpre2, xhat1, rstd1, g1)


def _out_bwd(dpre1b, w_out):
    t = dpre1b.shape[0]
    tm = min(t, 512)

    def body(d_ref, w_ref, o_ref):
        o_ref[...] = _dot(d_ref[...], w_ref[...], NT)

    return pl.pallas_call(
        body, name="out_bwd", grid=(t // tm,),
        in_specs=[pl.BlockSpec((tm, D_MODEL), lambda i: (i, 0)), pl.BlockSpec((D_MODEL, D_MODEL), lambda i: (0, 0))],
        out_specs=pl.BlockSpec((tm, D_MODEL), lambda i: (i, 0)),
        out_shape=jax.ShapeDtypeStruct((t, D_MODEL), F32),
        compiler_params=_params("parallel"),
    )(dpre1b, w_out)


def _gate_bwd(dcat, o, proj, gate_norm_w, conv_w):
    t = proj.shape[0]
    tb = min(t, 512)
    hb = tb // SUBLANES
    nblk = t // tb

    def body(do2_ref, dy_ref, dyn_ref, o_ref, og_ref, gnw_ref, b_ref, bn_ref, c_ref, u_ref, ch_ref, uh_ref, cw_ref,
             do_ref, dp_ref, dgnw_ref, dcw_ref, zbuf, dbuf):
        i = pl.program_id(1)

        @pl.when(i == 0)
        def _():
            dgnw_ref[...] = jnp.zeros_like(dgnw_ref)
            dcw_ref[...] = jnp.zeros_like(dcw_ref)

        ov, og, gnw, do2 = o_ref[...], og_ref[...], gnw_ref[...], do2_ref[...]
        rs = lax.rsqrt(jnp.mean(ov * ov, axis=-1, keepdims=True) + EPS)
        on = ov * rs
        sg = _sigmoid(og)
        sil = og * sg
        don = do2 * gnw * sil
        dgnw_ref[...] += jnp.sum(do2 * on * sil, axis=0, keepdims=True)
        dp_ref[0] = (do2 * on * gnw * (sg * (1.0 + og * (1.0 - sg)))).astype(dp_ref.dtype)
        do_ref[...] = rs * (don - on * jnp.mean(don * on, axis=-1, keepdims=True))

        bg, cg, u, dy = b_ref[...], c_ref[...], u_ref[...], dy_ref[...]
        z = cg * u
        halo = jnp.where(i > 0, ch_ref[...] * uh_ref[...], 0.0)
        z1, z2 = _conv_taps(z, halo, zbuf, tb)
        cw = cw_ref[...]
        yc = cw[2:3, :] * z + cw[1:2, :] * z1 + cw[0:1, :] * z2
        dyc = dy * bg
        dbuf[0:tb, :] = dyc
        dbuf[tb:tb + SUBLANES, :] = jnp.where(i < nblk - 1, dyn_ref[...] * bn_ref[...], 0.0)
        d1, d2 = dbuf[1:1 + tb, :], dbuf[2:2 + tb, :]
        dz = cw[2:3, :] * dyc + cw[1:2, :] * d1 + cw[0:1, :] * d2
        dp_ref[1] = (dy * yc).astype(dp_ref.dtype)
        dp_ref[2] = (dz * u).astype(dp_ref.dtype)
        dp_ref[3] = (dz * cg).astype(dp_ref.dtype)
        dcw_ref[0:1, :] += jnp.sum(dyc * z2, axis=0, keepdims=True)
        dcw_ref[1:2, :] += jnp.sum(dyc * z1, axis=0, keepdims=True)
        dcw_ref[2:3, :] += jnp.sum(dyc * z, axis=0, keepdims=True)

    blk = lambda off: pl.BlockSpec((tb, LANES), lambda j, i: (i, off + j))
    prev = lambda off: pl.BlockSpec((SUBLANES, LANES), lambda j, i: (jnp.maximum(i * hb - 1, 0), off + j))
    nxt = lambda off: pl.BlockSpec((SUBLANES, LANES), lambda j, i: (jnp.minimum((i + 1) * hb, t // SUBLANES - 1), off + j))
    vec = lambda r: pl.BlockSpec((r, LANES), lambda j, i: (0, j))
    return pl.pallas_call(
        body, name="gate_bwd", grid=(4, nblk),
        in_specs=[blk(0), blk(4), nxt(4), blk(0), blk(12), vec(1), blk(16), nxt(16), blk(20), blk(24), prev(20), prev(24),
                  vec(3)],
        out_specs=[blk(0), pl.BlockSpec((4, tb, LANES), lambda j, i: (0, i, j)), vec(1), vec(3)],
        out_shape=[jax.ShapeDtypeStruct((t, HGRN_WIDTH), F32), jax.ShapeDtypeStruct((4, t, HGRN_WIDTH), BF16),
                   jax.ShapeDtypeStruct((1, HGRN_WIDTH), F32), jax.ShapeDtypeStruct((3, CONV_WIDTH), F32)],
        scratch_shapes=[pltpu.VMEM((tb + SUBLANES, LANES), F32), pltpu.VMEM((tb + SUBLANES, LANES), F32)],
        compiler_params=_params("parallel", "arbitrary"),
    )(dcat, dcat, dcat, o, proj, gate_norm_w, proj, proj, proj, proj, proj, proj, conv_w)


def _hgrn_bwd(proj, do, states, lb_logits):
    t = proj.shape[0]
    tb = min(t, 512)
    ncb = tb // CHUNK
    nblk = t // tb

    def body(q_ref, f_ref, v_ref, do_ref, st_ref, lbl_ref, dp_ref, dlbl_ref, ds_scr, dlb_scr):
        i = pl.program_id(1)

        @pl.when(i == 0)
        def _():
            ds_scr[...] = jnp.zeros_like(ds_scr)
            dlb_scr[...] = jnp.zeros_like(dlb_scr)

        lb, s1 = _lower_bound(lbl_ref[...])
        row = lax.broadcasted_iota(jnp.int32, (CHUNK, CHUNK), 0)
        col = lax.broadcasted_iota(jnp.int32, (CHUNK, CHUNK), 1)
        causal = row >= col
        tri = causal.astype(F32)
        tri_rev = (row <= col).astype(F32)

        def chunk(n, carry):
            c = ncb - 1 - n
            rows = pl.ds(pl.multiple_of(c * CHUNK, CHUNK), CHUNK)
            q, v, do_c = q_ref[rows, :], v_ref[rows, :], do_ref[rows, :]
            sig, f, g, k = _gates(f_ref[rows, :], lb)
            b = _dot_exact(tri, g)
            b_ref, b_last = b[CHUNK // 2:CHUNK // 2 + 1, :], b[CHUNK - 1:CHUNK, :]
            e_q, e_k, e_i, e_s = jnp.exp(b - b_ref), jnp.exp(b_ref - b), jnp.exp(b), jnp.exp(b_last - b)
            dec = jnp.exp(b_last)
            qt, kt, qi, ks = q * e_q, k * e_k, q * e_i, k * e_s
            st, dst = st_ref[c], ds_scr[...]

            scores = jnp.where(causal, _dot(qt, kt, NT), 0.0)
            dscores = jnp.where(causal, _dot(do_c, v, NT), 0.0)
            dqt = _dot(dscores, kt, NN)
            dkt = _dot(dscores, qt, TN)
            dv = _dot(scores, do_c, TN) + _dot(ks, dst, NT)
            dqi = _dot(do_c, st, NN)
            dks = _dot(v, dst, NN)
            ddec = jnp.sum(dst * st, axis=0, keepdims=True)
            ds_scr[...] = dec * dst + _dot(do_c, qi, TN)

            dq = dqt * e_q + dqi * e_i
            dk = dkt * e_k + dks * e_s
            db = q * dq - k * dk
            db_last = jnp.sum(dks * ks, axis=0, keepdims=True) + ddec * dec
            dg = _dot_exact(tri_rev, db) + db_last
            df = dg / f - dk
            dlb_scr[...] += jnp.sum(df * (1.0 - sig), axis=0, keepdims=True)
            dp_ref[0, rows, :] = dq.astype(dp_ref.dtype)
            dp_ref[1, rows, :] = (df * (1.0 - lb) * sig * (1.0 - sig)).astype(dp_ref.dtype)
            dp_ref[2, rows, :] = dv.astype(dp_ref.dtype)
            return carry

        lax.fori_loop(0, ncb, chunk, 0)

        @pl.when(i == nblk - 1)
        def _():
            dlb = dlb_scr[...]
            dlbl_ref[0:1, :] = dlb * lb * (1.0 - lb)
            dlbl_ref[1:2, :] = -dlb * lb * s1

    col_spec = lambda off: pl.BlockSpec((tb, HEAD_DIM), lambda h, i: (nblk - 1 - i, off + h))
    return pl.pallas_call(
        body, name="hgrn_bwd", grid=(N_HEADS, nblk),
        in_specs=[col_spec(0), col_spec(4), col_spec(8), col_spec(0),
                  pl.BlockSpec((ncb, None, HEAD_DIM, HEAD_DIM), lambda h, i: (nblk - 1 - i, h, 0, 0)),
                  pl.BlockSpec((2, HEAD_DIM), lambda h, i: (0, h))],
        out_specs=[pl.BlockSpec((3, tb, HEAD_DIM), lambda h, i: (0, nblk - 1 - i, h)),
                   pl.BlockSpec((2, HEAD_DIM), lambda h, i: (0, h))],
        out_shape=[jax.ShapeDtypeStruct((3, t, HGRN_WIDTH), BF16), jax.ShapeDtypeStruct((2, HGRN_WIDTH), F32)],
        scratch_shapes=[pltpu.VMEM((HEAD_DIM, HEAD_DIM), F32), pltpu.VMEM((1, HEAD_DIM), F32)],
        compiler_params=_params("parallel", "arbitrary"),
    )(proj, proj, proj, do, states, lb_logits)


def _in_bwd(dph, dpg, w_in, dpre1):
    t = dpre1.shape[0]
    tm = min(t, 512)
    ng = IN_COLS // HGRN_WIDTH

    def body(dh_ref, dg_ref, w_ref, dp_ref, o_ref, acc):
        k = pl.program_id(1)

        @pl.when(k == 0)
        def _():
            acc[...] = jnp.zeros_like(acc)

        @pl.when(k < 3)
        def _():
            acc[...] += _dot(dh_ref[...], w_ref[...], NT)

        @pl.when(k >= 3)
        def _():
            acc[...] += _dot(dg_ref[...], w_ref[...], NT)

        @pl.when(k == ng - 1)
        def _():
            o_ref[...] = ALPHA * dp_ref[...] + acc[...]

    row = pl.BlockSpec((tm, D_MODEL), lambda i, k: (i, 0))
    return pl.pallas_call(
        body, name="in_bwd", grid=(t // tm, ng),
        in_specs=[pl.BlockSpec((None, tm, HGRN_WIDTH), lambda i, k: (jnp.minimum(k, 2), i, 0)),
                  pl.BlockSpec((None, tm, HGRN_WIDTH), lambda i, k: (jnp.maximum(k - 3, 0), i, 0)),
                  pl.BlockSpec((D_MODEL, HGRN_WIDTH), lambda i, k: (0, k)), row],
        out_specs=row,
        out_shape=jax.ShapeDtypeStruct((t, D_MODEL), F32),
        scratch_shapes=[pltpu.VMEM((tm, D_MODEL), F32)],
        compiler_params=_params("parallel", "arbitrary"),
    )(dph, dpg, w_in, dpre1)


def _mm_tn(name, operands, in_specs, out_spec, out_shape, grid, pick=None):
    def body(*refs):
        a_ref, b_refs, o_ref, acc = refs[0], refs[1:-2], refs[-2], refs[-1]
        k = pl.program_id(len(grid) - 1)

        @pl.when(k == 0)
        def _():
            acc[...] = jnp.zeros_like(acc)

        if pick is None:
            acc[...] += _dot(a_ref[...], b_refs[0][...], TN)
        else:
            for n, b_ref in enumerate(b_refs):
                @pl.when(pick(n))
                def _(b_ref=b_ref):
                    acc[...] += _dot(a_ref[...], b_ref[...], TN)

        @pl.when(k == grid[-1] - 1)
        def _():
            o_ref[...] = acc[...]

    return pl.pallas_call(
        body, name=name, grid=grid, in_specs=in_specs, out_specs=out_spec,
        out_shape=jax.ShapeDtypeStruct(out_shape, F32),
        scratch_shapes=[pltpu.VMEM(out_spec.block_shape, F32)],
        compiler_params=_params(*(["parallel"] * (len(grid) - 1) + ["arbitrary"])),
    )(*operands)


def _weight_grads(x, dph, dpg, cat, dpre1b, h1b, da, r, dpre2b):
    t = x.shape[0]
    tt = min(t, 512)
    nt = t // tt
    g_in = _mm_tn(
        "dw_in", (x, dph, dpg),
        [pl.BlockSpec((tt, D_MODEL), lambda j, k: (k, 0)),
         pl.BlockSpec((None, tt, HGRN_WIDTH), lambda j, k: (jnp.minimum(j, 2), k, 0)),
         pl.BlockSpec((None, tt, HGRN_WIDTH), lambda j, k: (jnp.maximum(j - 3, 0), k, 0))],
        pl.BlockSpec((D_MODEL, HGRN_WIDTH), lambda j, k: (0, j)), (D_MODEL, IN_COLS), (IN_COLS // HGRN_WIDTH, nt),
        pick=lambda n: (pl.program_id(0) < 3) if n == 0 else (pl.program_id(0) >= 3))
    g_out = _mm_tn(
        "dw_out", (cat, dpre1b),
        [pl.BlockSpec((None, tt, HGRN_WIDTH), lambda g, k: (g, k, 0)), pl.BlockSpec((tt, D_MODEL), lambda g, k: (k, 0))],
        pl.BlockSpec((HGRN_WIDTH, D_MODEL), lambda g, k: (g, 0)), (D_MODEL, D_MODEL), (2, nt))
    g_ff1 = _mm_tn(
        "dw_ff1", (h1b, da),
        [pl.BlockSpec((tt, D_MODEL), lambda j, k: (k, 0)), pl.BlockSpec((tt, 1024), lambda j, k: (k, j))],
        pl.BlockSpec((D_MODEL, 1024), lambda j, k: (0, j)), (D_MODEL, D_FF), (D_FF // 1024, nt))
    g_ff2 = _mm_tn(
        "dw_ff2", (r, dpre2b),
        [pl.BlockSpec((tt, 1024), lambda j, k: (k, j)), pl.BlockSpec((tt, D_MODEL), lambda j, k: (k, 0))],
        pl.BlockSpec((1024, D_MODEL), lambda j, k: (j, 0)), (D_FF, D_MODEL), (D_FF // 1024, nt))
    return g_in, g_out, g_ff1, g_ff2


def _place():
    x, y, c = lax.axis_index("x"), lax.axis_index("y"), lax.axis_index("c")
    return x, y, c, 2 * x + y


def _other_chips(x, y):
    return [(1 - x, y), (x, 1 - y), (1 - x, 1 - y)]


def _shard_views(refs, chip):
    r_in, r_out, r_ff1, r_ff2 = refs
    return [r_in.at[:, pl.ds(chip * (IN_COLS // N_CHIPS), IN_COLS // N_CHIPS)],
            r_out.at[pl.ds(chip * (D_MODEL // N_CHIPS), D_MODEL // N_CHIPS), :],
            r_ff1.at[:, pl.ds(chip * (D_FF // N_CHIPS), D_FF // N_CHIPS)],
            r_ff2.at[pl.ds(chip * (D_FF // N_CHIPS), D_FF // N_CHIPS), :]]


def _place_shard(name, w, chip, cols_sharded, after=None):
    rows, cols = w.shape
    tr = min(rows, 256)
    nb = rows // tr
    full = (rows, cols * N_CHIPS) if cols_sharded else (rows * N_CHIPS, cols)
    out_map = (lambda i, s: (i, s[0])) if cols_sharded else (lambda i, s: (s[0] * nb + i, 0))

    def body(s_ref, w_ref, *rest):
        rest[-1][...] = w_ref[...].astype(rest[-1].dtype)

    extra = [] if after is None else [after]
    return pl.pallas_call(
        body, name=name,
        grid_spec=pltpu.PrefetchScalarGridSpec(
            num_scalar_prefetch=1, grid=(nb,),
            in_specs=[pl.BlockSpec((tr, cols), lambda i, s: (i, 0))] + [ANY] * len(extra),
            out_specs=pl.BlockSpec((tr, cols), out_map)),
        out_shape=jax.ShapeDtypeStruct(full, BF16),
        compiler_params=_params("parallel"),
    )(chip, w, *extra)


def _gather_w_in(w_in, conv_w):
    half, cs, n_p = D_MODEL // 2, IN_COLS // N_CHIPS, 3

    def body(w_alias, cv_ref, w_ref, cvf_ref, send_sems, recv_sems, local_sem):
        x, y, c, me = _place()
        sibling = (x, y, 1 - c)
        chips = _other_chips(x, y)
        blk = lambda chip, h: w_ref.at[pl.ds(h * half, half), pl.ds(chip * cs, cs)]

        def copy(k, src, dst, to):
            return pltpu.make_async_remote_copy(src_ref=src, dst_ref=dst, send_sem=send_sems.at[k],
                                                recv_sem=recv_sems.at[k], device_id=to, device_id_type=MESH)

        own_cv = pltpu.make_async_copy(cv_ref, cvf_ref.at[me], local_sem)
        own_cv.start()
        first = [copy(j, blk(me, c), blk(me, c), (px, py, c)) for j, (px, py) in enumerate(chips)]
        first += [copy(2 * n_p + j, cv_ref, cvf_ref.at[me], (px, py, c)) for j, (px, py) in enumerate(chips)]
        for cp in first:
            cp.start()
        passed = []
        for j, (px, py) in enumerate(chips):
            got = blk(2 * px + py, c)
            copy(j, got, got, (px, py, c)).wait_recv()
            passed.append(copy(n_p + j, got, got, sibling))
            passed[-1].start()
        for j, (px, py) in enumerate(chips):
            got = blk(2 * px + py, 1 - c)
            copy(n_p + j, got, got, sibling).wait_recv()
            copy(2 * n_p + j, cv_ref, cvf_ref.at[2 * px + py], (px, py, c)).wait_recv()
        for cp in first + passed:
            cp.wait_send()
        own_cv.wait()

    return pl.pallas_call(
        body, name="gather_w_in", in_specs=[ANY, ANY], out_specs=[ANY, ANY],
        out_shape=[jax.ShapeDtypeStruct(w_in.shape, w_in.dtype), jax.ShapeDtypeStruct((N_CHIPS,) + conv_w.shape, conv_w.dtype)],
        input_output_aliases={0: 0},
        scratch_shapes=[pltpu.SemaphoreType.DMA((3 * n_p,)), pltpu.SemaphoreType.DMA((3 * n_p,)), pltpu.SemaphoreType.DMA],
    )(w_in, conv_w)


HBM = pl.BlockSpec(memory_space=pltpu.HBM)
SEM = pl.BlockSpec(memory_space=pltpu.SEMAPHORE)
EFFECT = pltpu.SideEffectType.DATAFLOW_SIDE_EFFECTING


def _rest_views(refs, chip):
    r_out, r_ff1, r_ff2 = refs
    return [r_out.at[pl.ds(chip * (D_MODEL // N_CHIPS), D_MODEL // N_CHIPS), :],
            r_ff1.at[:, pl.ds(chip * (D_FF // N_CHIPS), D_FF // N_CHIPS)],
            r_ff2.at[pl.ds(chip * (D_FF // N_CHIPS), D_FF // N_CHIPS), :]]


def _gather_rest_start(w_out, w_ff1, w_ff2):
    arrays = (w_out, w_ff1, w_ff2)

    def body(o_ref, f1_ref, f2_ref, send_sems, recv_sems, o_thru, f1_thru, f2_thru, token):
        x, y, c, me = _place()
        mine = _rest_views((o_ref, f1_ref, f2_ref), me)
        for n in range(3):
            for j, (px, py) in enumerate(_other_chips(x, y)):
                pltpu.make_async_remote_copy(
                    src_ref=mine[n], dst_ref=mine[n], send_sem=send_sems.at[3 * n + j], recv_sem=recv_sems.at[3 * n + j],
                    device_id=(px, py, c), device_id_type=MESH).start()
        token[...] = jnp.zeros_like(token)

    return pl.pallas_call(
        body, name="gather_rest_start",
        out_shape=(pltpu.SemaphoreType.DMA((9,)), pltpu.SemaphoreType.DMA((9,)),
                   *[pltpu.HBM(a.shape, a.dtype) for a in arrays], jax.ShapeDtypeStruct((SUBLANES, LANES), F32)),
        in_specs=(HBM, HBM, HBM), out_specs=(SEM, SEM, HBM, HBM, HBM, pl.BlockSpec(memory_space=pltpu.VMEM)),
        input_output_aliases={0: 2, 1: 3, 2: 4},
        compiler_params=pltpu.CompilerParams(has_side_effects=EFFECT),
    )(*[pltpu.with_memory_space_constraint(a, pltpu.HBM) for a in arrays])


def _gather_rest_wait(send_sems, recv_sems, w_out, w_ff1, w_ff2, after):
    arrays = (w_out, w_ff1, w_ff2)

    def body(o_ref, f1_ref, f2_ref, send_sems, recv_sems, after_ref, o_out, f1_out, f2_out):
        x, y, c, me = _place()
        mine = _rest_views((o_ref, f1_ref, f2_ref), me)
        for n in range(3):
            for j, (px, py) in enumerate(_other_chips(x, y)):
                cp = pltpu.make_async_remote_copy(
                    src_ref=mine[n], dst_ref=_rest_views((o_ref, f1_ref, f2_ref), 2 * px + py)[n],
                    send_sem=send_sems.at[3 * n + j], recv_sem=recv_sems.at[3 * n + j],
                    device_id=(px, py, c), device_id_type=MESH)
                cp.wait_send()
                cp.wait_recv()

    return pl.pallas_call(
        body, name="gather_rest_wait", out_shape=tuple(pltpu.HBM(a.shape, a.dtype) for a in arrays),
        in_specs=(HBM, HBM, HBM, SEM, SEM, ANY), out_specs=(HBM, HBM, HBM), input_output_aliases={0: 0, 1: 1, 2: 2},
        compiler_params=pltpu.CompilerParams(has_side_effects=EFFECT),
    )(w_out, w_ff1, w_ff2, send_sems, recv_sems, after)


def _half_views(refs, h):
    r_in, r_out, r_ff1, r_ff2 = refs
    return [r_in.at[pl.ds(h * (D_MODEL // 2), D_MODEL // 2), :], r_out.at[:, pl.ds(h * (D_MODEL // 2), D_MODEL // 2)],
            r_ff1.at[pl.ds(h * (D_MODEL // 2), D_MODEL // 2), :], r_ff2.at[:, pl.ds(h * (D_MODEL // 2), D_MODEL // 2)]]


HALF_SHAPES = [(D_MODEL // 2, IN_COLS), (D_MODEL, D_MODEL // 2), (D_MODEL // 2, D_FF), (D_FF, D_MODEL // 2)]
PIECE_SHAPES = [(D_MODEL // 2, IN_COLS // N_CHIPS), (D_MODEL // N_CHIPS, D_MODEL // 2),
                (D_MODEL // 2, D_FF // N_CHIPS), (D_FF // N_CHIPS, D_MODEL // 2)]
SHARD_SHAPES = [(D_MODEL, IN_COLS // N_CHIPS), (D_MODEL // N_CHIPS, D_MODEL), (D_MODEL, D_FF // N_CHIPS),
                (D_FF // N_CHIPS, D_MODEL)]


def _swap_halves(grads):
    def body(g_in, g_out, g_ff1, g_ff2, r_in, r_out, r_ff1, r_ff2, send_sems, recv_sems):
        x, y, c, _ = _place()
        srcs = _half_views((g_in, g_out, g_ff1, g_ff2), 1 - c)
        copies = [pltpu.make_async_remote_copy(
            src_ref=srcs[n], dst_ref=dst, send_sem=send_sems.at[n], recv_sem=recv_sems.at[n],
            device_id=(x, y, 1 - c), device_id_type=MESH) for n, dst in enumerate((r_in, r_out, r_ff1, r_ff2))]
        for cp in copies:
            cp.start()
        for cp in copies:
            cp.wait()

    return pl.pallas_call(
        body, name="swap_halves", in_specs=[ANY] * 4, out_specs=[ANY] * 4,
        out_shape=[jax.ShapeDtypeStruct(s, F32) for s in HALF_SHAPES],
        scratch_shapes=[pltpu.SemaphoreType.DMA((4,)), pltpu.SemaphoreType.DMA((4,))],
    )(*grads)


def _add_half(name, g, recv, core, rows_split):
    shape = recv.shape
    tr = min(shape[0], 128 if rows_split else 256)
    nb = shape[0] // tr

    def body(c_ref, g_ref, r_ref, o_ref):
        o_ref[...] = (g_ref[...] + r_ref[...]).astype(o_ref.dtype)

    g_map = (lambda i, c_ref: (c_ref[0] * nb + i, 0)) if rows_split else (lambda i, c_ref: (i, c_ref[0]))
    blk = pl.BlockSpec((tr, shape[1]), lambda i, c_ref: (i, 0))
    return pl.pallas_call(
        body, name=name,
        grid_spec=pltpu.PrefetchScalarGridSpec(
            num_scalar_prefetch=1, grid=(nb,),
            in_specs=[pl.BlockSpec((tr, shape[1]), g_map), blk], out_specs=blk),
        out_shape=jax.ShapeDtypeStruct(shape, BF16),
        compiler_params=_params("parallel"),
    )(core, g, recv)


def _piece_views(refs, chip):
    return _shard_views(refs, chip)


def _exchange_pieces(halves):
    n_t, n_p = 4, 3

    def body(p_in, p_out, p_ff1, p_ff2, r_in, r_out, r_ff1, r_ff2, send_sems, recv_sems):
        x, y, c, me = _place()
        pieces = lambda chip: _piece_views((p_in, p_out, p_ff1, p_ff2), chip)
        slots = lambda j: [r.at[j] for r in (r_in, r_out, r_ff1, r_ff2)]
        copies = []
        for j, (px, py) in enumerate(_other_chips(x, y)):
            for n in range(n_t):
                copies.append(pltpu.make_async_remote_copy(
                    src_ref=pieces(2 * px + py)[n], dst_ref=slots(j)[n], send_sem=send_sems.at[n_p * n + j],
                    recv_sem=recv_sems.at[n_p * n + j], device_id=(px, py, c), device_id_type=MESH))
        for cp in copies:
            cp.start()
        for cp in copies:
            cp.wait()

    return pl.pallas_call(
        body, name="exchange_pieces", in_specs=[ANY] * n_t, out_specs=[ANY] * n_t,
        out_shape=[jax.ShapeDtypeStruct((n_p,) + s, BF16) for s in PIECE_SHAPES],
        scratch_shapes=[pltpu.SemaphoreType.DMA((n_t * n_p,)), pltpu.SemaphoreType.DMA((n_t * n_p,))],
    )(*halves)


def _sum_pieces(name, half, slots, place, rows_split):
    n_p, rows, cols = slots.shape
    tr = min(rows, 256)
    nb = rows // tr
    if rows_split:
        own_map = lambda i, s: (i, s[0])
        out_map = lambda i, s: (s[1] * nb + i, 0)
        shard = (2 * rows, cols)
    else:
        own_map = lambda i, s: (s[0] * nb + i, 0)
        out_map = lambda i, s: (i, s[1])
        shard = (rows, 2 * cols)

    def body(s_ref, own_ref, slot_ref, o_ref):
        total = own_ref[...].astype(F32)
        for j in range(n_p):
            total = total + slot_ref[j].astype(F32)
        o_ref[...] = total

    return pl.pallas_call(
        body, name=name,
        grid_spec=pltpu.PrefetchScalarGridSpec(
            num_scalar_prefetch=1, grid=(nb,),
            in_specs=[pl.BlockSpec((tr, cols), own_map), pl.BlockSpec((n_p, tr, cols), lambda i, s: (0, i, 0))],
            out_specs=pl.BlockSpec((tr, cols), out_map)),
        out_shape=jax.ShapeDtypeStruct(shard, F32),
        compiler_params=_params("parallel"),
    )(place, half, slots)


def _join_halves(shards):
    def body(a_in, a_out, a_ff1, a_ff2, g_in, g_out, g_ff1, g_ff2, send_sems, recv_sems):
        x, y, c, _ = _place()
        mine = _half_views_shard((g_in, g_out, g_ff1, g_ff2), c)
        theirs = _half_views_shard((g_in, g_out, g_ff1, g_ff2), 1 - c)
        sends = [pltpu.make_async_remote_copy(
            src_ref=mine[n], dst_ref=mine[n], send_sem=send_sems.at[n], recv_sem=recv_sems.at[n],
            device_id=(x, y, 1 - c), device_id_type=MESH) for n in range(4)]
        for cp in sends:
            cp.start()
        for n in range(4):
            pltpu.make_async_remote_copy(
                src_ref=mine[n], dst_ref=theirs[n], send_sem=send_sems.at[n], recv_sem=recv_sems.at[n],
                device_id=(x, y, 1 - c), device_id_type=MESH).wait_recv()
        for cp in sends:
            cp.wait_send()

    return pl.pallas_call(
        body, name="join_halves", in_specs=[ANY] * 4, out_specs=[ANY] * 4,
        out_shape=[jax.ShapeDtypeStruct(s, F32) for s in SHARD_SHAPES],
        input_output_aliases={0: 0, 1: 1, 2: 2, 3: 3},
        scratch_shapes=[pltpu.SemaphoreType.DMA((4,)), pltpu.SemaphoreType.DMA((4,))],
    )(*shards)


def _half_views_shard(refs, h):
    r_in, r_out, r_ff1, r_ff2 = refs
    half = D_MODEL // 2
    return [r_in.at[pl.ds(h * half, half), :], r_out.at[:, pl.ds(h * half, half)],
            r_ff1.at[pl.ds(h * half, half), :], r_ff2.at[:, pl.ds(h * half, half)]]


def _sum_small(pack):
    n_dev = 8

    def body(p_ref, o_ref, slots, send_sems, recv_sems):
        x, y, c, _ = _place()
        me = 4 * x + 2 * y + c
        slots[me] = p_ref[...]
        sends = []
        for m in range(1, n_dev):
            peer = ((1 - x) if m & 4 else x, (1 - y) if m & 2 else y, (1 - c) if m & 1 else c)
            sends.append(pltpu.make_async_remote_copy(
                src_ref=p_ref, dst_ref=slots.at[me], send_sem=send_sems.at[m - 1], recv_sem=recv_sems.at[m - 1],
                device_id=peer, device_id_type=MESH))
        for cp in sends:
            cp.start()
        for m in range(1, n_dev):
            peer = ((1 - x) if m & 4 else x, (1 - y) if m & 2 else y, (1 - c) if m & 1 else c)
            pltpu.make_async_remote_copy(
                src_ref=p_ref, dst_ref=slots.at[4 * peer[0] + 2 * peer[1] + peer[2]], send_sem=send_sems.at[m - 1],
                recv_sem=recv_sems.at[m - 1], device_id=peer, device_id_type=MESH).wait_recv()
        for cp in sends:
            cp.wait_send()
        total = slots[0]
        for d in range(1, n_dev):
            total = total + slots[d]
        o_ref[...] = total

    vm = pl.BlockSpec(memory_space=pltpu.VMEM)
    return pl.pallas_call(
        body, name="sum_small", in_specs=[vm], out_specs=vm,
        out_shape=jax.ShapeDtypeStruct(pack.shape, F32),
        scratch_shapes=[pltpu.VMEM((n_dev,) + pack.shape, F32), pltpu.SemaphoreType.DMA((n_dev - 1,)),
                        pltpu.SemaphoreType.DMA((n_dev - 1,))],
    )(pack)


def _adamw(name, w, g, m, v):
    rows, cols = w.shape
    tr = min(rows, 256)

    def body(w_ref, g_ref, m_ref, v_ref, d_ref, nm_ref, nv_ref):
        gv = g_ref[...]
        nm = ADAM_B1 * m_ref[...] + (1.0 - ADAM_B1) * gv
        nv = ADAM_B2 * v_ref[...] + (1.0 - ADAM_B2) * jnp.square(gv)
        m_hat = nm / (1.0 - ADAM_B1 ** ADAM_STEP)
        v_hat = nv / (1.0 - ADAM_B2 ** ADAM_STEP)
        d_ref[...] = -ADAM_LR * (m_hat / (jnp.sqrt(v_hat) + ADAM_EPS) + ADAM_WD * w_ref[...])
        nm_ref[...] = nm
        nv_ref[...] = nv

    blk = pl.BlockSpec((tr, cols), lambda i: (i, 0))
    return pl.pallas_call(
        body, name=name, grid=(rows // tr,), in_specs=[blk] * 4, out_specs=[blk] * 3,
        out_shape=[jax.ShapeDtypeStruct(w.shape, F32)] * 3,
        compiler_params=_params("parallel"),
    )(w, g, m, v)


def kernel(x, w_in, lb_logits, gate_norm_w, conv_w, w_out, ln1_g, ln1_b, w_ff1, w_ff2, ln2_g, ln2_b, loss_target, m_w_in, m_lb_logits, m_gate_norm_w, m_conv_w, m_w_out, m_ln1_g, m_ln1_b, m_w_ff1, m_w_ff2, m_ln2_g, m_ln2_b, v_w_in, v_lb_logits, v_gate_norm_w, v_conv_w, v_w_out, v_ln1_g, v_ln1_b, v_w_ff1, v_w_ff2, v_ln2_g, v_ln2_b):
    xs, tgt = x[0], loss_target[0]
    chip = 2 * lax.axis_index("x") + lax.axis_index("y")
    core = lax.axis_index("c").astype(jnp.int32).reshape(1)
    chip1 = chip.astype(jnp.int32).reshape(1)
    place = jnp.concatenate([chip1, core])

    wb_in, cv4 = _gather_w_in(_place_shard("place_w_in", w_in[0], chip1, True), conv_w[0])
    conv_full = cv4.transpose(1, 0, 2).reshape(3, CONV_WIDTH)
    send_sems, recv_sems, wb_out, wb_ff1, wb_ff2, token = _gather_rest_start(
        _place_shard("place_w_out", w_out[0], chip1, False, after=wb_in),
        _place_shard("place_w_ff1", w_ff1[0], chip1, True, after=wb_in),
        _place_shard("place_w_ff2", w_ff2[0], chip1, False, after=wb_in))

    proj = _in_proj(xs, wb_in, token)
    o, states = _hgrn_fwd(proj, lb_logits)
    cat = _gate_fwd(proj, o, gate_norm_w, conv_full)
    wb_out, wb_ff1, wb_ff2 = _gather_rest_wait(send_sems, recv_sems, wb_out, wb_ff1, wb_ff2, cat)
    xhat1, h1b, rstd1 = _out_ln1(cat, wb_out, xs, ln1_g, ln1_b)
    a, r = _ff1(h1b, wb_ff1)
    dpre2, dpre2b, g_ln2_g, g_ln2_b, loss8 = _ff2_ln2_loss(r, wb_ff2, xhat1, ln1_g, ln1_b, ln2_g, ln2_b, tgt)
    loss = lax.psum(loss8[0, 0], ("x", "y", "c"))

    da = _ff2_bwd(dpre2b, wb_ff2, a)
    dpre1, dpre1b, g_ln1_g, g_ln1_b = _ff1_bwd_ln1(da, wb_ff1, dpre2, xhat1, rstd1, ln1_g)
    dcat = _out_bwd(dpre1b, wb_out)
    do, dpg, g_gnw, g_conv = _gate_bwd(dcat, o, proj, gate_norm_w, conv_full)
    dph, g_lbl = _hgrn_bwd(proj, do, states, lb_logits)
    grad_x = _in_bwd(dph, dpg, wb_in, dpre1)
    grads = _weight_grads(xs, dph, dpg, cat, dpre1b, h1b, da, r, dpre2b)

    recv = _swap_halves(grads)
    names = ("w_in", "w_out", "w_ff1", "w_ff2")
    halves = [_add_half("add_half_" + n, g, rv, core, rs) for n, g, rv, rs in zip(names, grads, recv, (True, False, True, False))]
    slots = _exchange_pieces(halves)
    shards = [_sum_pieces("sum_pieces_" + n, h, s, place, rs)
              for n, h, s, rs in zip(names, halves, slots, (True, False, True, False))]
    g_w_in, g_w_out, g_w_ff1, g_w_ff2 = _join_halves(shards)

    pack = jnp.concatenate([
        g_ln1_g, g_ln1_b, g_ln2_g, g_ln2_b,
        jnp.concatenate([g_lbl[0:1], g_lbl[1:2]], axis=1),
        jnp.concatenate([g_gnw, g_conv[0:1]], axis=1),
        jnp.concatenate([g_conv[1:2], g_conv[2:3]], axis=1),
        jnp.zeros((1, D_MODEL), F32)], axis=0)
    tot = _sum_small(pack)
    half = D_MODEL // 2
    g_lb_logits = jnp.concatenate([tot[4:5, :half], tot[4:5, half:]], axis=0)
    g_gate_norm_w = tot[5:6, :half]
    g_conv_full = jnp.concatenate([tot[5:6, half:], tot[6:7, :half], tot[6:7, half:]], axis=0)
    g_conv_w = lax.dynamic_slice(g_conv_full, (0, chip * LANES), (3, LANES))

    d_in, nm_in, nv_in = _adamw("adamw_w_in", w_in[0], g_w_in, m_w_in[0], v_w_in[0])
    d_out, nm_out, nv_out = _adamw("adamw_w_out", w_out[0], g_w_out, m_w_out[0], v_w_out[0])
    d_ff1, nm_ff1, nv_ff1 = _adamw("adamw_w_ff1", w_ff1[0], g_w_ff1, m_w_ff1[0], v_w_ff1[0])
    d_ff2, nm_ff2, nv_ff2 = _adamw("adamw_w_ff2", w_ff2[0], g_w_ff2, m_w_ff2[0], v_w_ff2[0])

    def small_pack(lbl, gnw, cv, l1g, l1b, l2g, l2b):
        pad = jnp.zeros((1, D_MODEL - 3 * LANES), F32)
        return jnp.concatenate([
            l1g, l1b, l2g, l2b, jnp.concatenate([lbl[0:1], lbl[1:2]], axis=1),
            jnp.concatenate([gnw, jnp.zeros((1, half), F32)], axis=1),
            jnp.concatenate([cv[0:1], cv[1:2], cv[2:3], pad], axis=1), jnp.zeros((1, D_MODEL), F32)], axis=0)

    w_s = small_pack(lb_logits, gate_norm_w, conv_w[0], ln1_g, ln1_b, ln2_g, ln2_b)
    g_s = small_pack(g_lb_logits, g_gate_norm_w, g_conv_w, tot[0:1], tot[1:2], tot[2:3], tot[3:4])
    m_s = small_pack(m_lb_logits, m_gate_norm_w, m_conv_w[0], m_ln1_g, m_ln1_b, m_ln2_g, m_ln2_b)
    v_s = small_pack(v_lb_logits, v_gate_norm_w, v_conv_w[0], v_ln1_g, v_ln1_b, v_ln2_g, v_ln2_b)
    d_s, nm_s, nv_s = _adamw("adamw_small", w_s, g_s, m_s, v_s)

    def unpack(p):
        lbl = jnp.concatenate([p[4:5, :half], p[4:5, half:]], axis=0)
        cv = jnp.concatenate([p[6:7, 0:LANES], p[6:7, LANES:2 * LANES], p[6:7, 2 * LANES:3 * LANES]], axis=0)
        return dict(lb_logits=lbl, gate_norm_w=p[5:6, :half], conv_w=cv[None], ln1_g=p[0:1], ln1_b=p[1:2],
                    ln2_g=p[2:3], ln2_b=p[3:4])

    order = ("w_in", "lb_logits", "gate_norm_w", "conv_w", "w_out", "ln1_g", "ln1_b", "w_ff1", "w_ff2", "ln2_g", "ln2_b")
    grad = dict(unpack(g_s), w_in=g_w_in[None], w_out=g_w_out[None], w_ff1=g_w_ff1[None], w_ff2=g_w_ff2[None])
    delta = dict(unpack(d_s), w_in=d_in[None], w_out=d_out[None], w_ff1=d_ff1[None], w_ff2=d_ff2[None])
    new_m = dict(unpack(nm_s), w_in=nm_in[None], w_out=nm_out[None], w_ff1=nm_ff1[None], w_ff2=nm_ff2[None])
    new_v = dict(unpack(nv_s), w_in=nv_in[None], w_out=nv_out[None], w_ff1=nv_ff1[None], w_ff2=nv_ff2[None])
    return (loss, grad_x[None], *[grad[n] for n in order], *[delta[n] for n in order],
            *[new_m[n] for n in order], *[new_v[n] for n in order])
```

```python
import jax
import jax.numpy as jnp
from jax import lax
from jax.experimental import pallas as pl
from jax.experimental.pallas import tpu as pltpu

F32 = jnp.float32
BF16 = jnp.bfloat16
MXU_DTYPE = jnp.bfloat16

D_MODEL = 1024
HGRN_WIDTH = 512
HEAD_DIM = 128
N_HEADS = 4
CONV_WIDTH = 512
CHUNK = 64
D_FF = 4096
IN_COLS = 3584
GROUP = 512
N_GROUPS = IN_COLS // GROUP
ALPHA = 2.0 ** 0.25
EPS = 1e-5
N_CHIPS = 4
ADAM_LR, ADAM_B1, ADAM_B2, ADAM_EPS, ADAM_WD, ADAM_STEP = 0.001, 0.9, 0.999, 1e-08, 0.01, 10

LANES = 128
SUBLANES = 8
VMEM_LIMIT = 56 * 1024 * 1024
FF_BLOCK = 1024

NN = (((1,), (0,)), ((), ()))
NT = (((1,), (1,)), ((), ()))
TN = (((0,), (0,)), ((), ()))
MESH = pl.DeviceIdType.MESH
ANY = pl.BlockSpec(memory_space=pl.ANY)


def _dot(a, b, dims):
    return lax.dot_general(a.astype(MXU_DTYPE), b.astype(MXU_DTYPE), dims, preferred_element_type=F32)


def _dot_exact(a, b):
    return lax.dot_general(a, b, NN, precision=lax.Precision.HIGHEST, preferred_element_type=F32)


def _params(*sem):
    return pltpu.CompilerParams(dimension_semantics=sem, vmem_limit_bytes=VMEM_LIMIT)


def _resident(shape):
    return pl.BlockSpec(shape, lambda *_: (0,) * len(shape), pipeline_mode=pl.Buffered(1))


def _sigmoid(v):
    return 1.0 / (1.0 + jnp.exp(-v))


def _lower_bound(lbl):
    m = jnp.max(lbl, axis=0, keepdims=True)
    e = jnp.exp(lbl - m)
    s = e / jnp.sum(e, axis=0, keepdims=True)
    return s[0:1, :], s[1:2, :]


def _heads(v):
    return [v[:, h * HEAD_DIM:(h + 1) * HEAD_DIM] for h in range(N_HEADS)]


def _in_proj(x, w_in, after):
    t = x.shape[0]
    tm = min(t, 512)

    def body(x_ref, w_ref, after_ref, o_ref, xb_ref):
        xb = x_ref[...].astype(xb_ref.dtype)
        xb_ref[...] = xb
        for g in range(N_GROUPS):
            cols = slice(g * GROUP, (g + 1) * GROUP)
            o_ref[:, cols] = _dot(xb, w_ref[:, cols], NN)

    return pl.pallas_call(
        body, name="in_proj", grid=(t // tm,),
        in_specs=[pl.BlockSpec((tm, D_MODEL), lambda i: (i, 0)), _resident((D_MODEL, IN_COLS)), ANY],
        out_specs=[pl.BlockSpec((tm, IN_COLS), lambda i: (i, 0)), pl.BlockSpec((tm, D_MODEL), lambda i: (i, 0))],
        out_shape=[jax.ShapeDtypeStruct((t, IN_COLS), F32), jax.ShapeDtypeStruct((t, D_MODEL), BF16)],
        compiler_params=_params("parallel"),
    )(x, w_in, after)


def _gates(fp, lb):
    sig = _sigmoid(fp)
    f = lb + (1.0 - lb) * sig
    return sig, f, jnp.log(f), 1.0 - f


def _chunk_masks():
    row = lax.broadcasted_iota(jnp.int32, (CHUNK, CHUNK), 0)
    col = lax.broadcasted_iota(jnp.int32, (CHUNK, CHUNK), 1)
    return row >= col, row <= col


def _hgrn_fwd(proj, lb_logits):
    t = proj.shape[0]
    tb = min(t, 512)
    ncb = tb // CHUNK

    def body(q_ref, f_ref, v_ref, lbl_ref, o_ref, st_ref, s_scr):
        @pl.when(pl.program_id(0) == 0)
        def _():
            s_scr[...] = jnp.zeros_like(s_scr)

        lb, _ = _lower_bound(lbl_ref[...])
        causal, _ = _chunk_masks()
        tri = causal.astype(F32)

        def chunk(c, carry):
            rows = pl.ds(pl.multiple_of(c * CHUNK, CHUNK), CHUNK)
            q, v = q_ref[rows, :], v_ref[rows, :]
            _, _, g, k = _gates(f_ref[rows, :], lb)
            b = _dot_exact(tri, g)
            b_ref, b_last = b[CHUNK // 2:CHUNK // 2 + 1, :], b[CHUNK - 1:CHUNK, :]
            qt, kt, qi, ks = q * jnp.exp(b - b_ref), k * jnp.exp(b_ref - b), q * jnp.exp(b), k * jnp.exp(b_last - b)
            dec = jnp.exp(b_last)
            st = s_scr[...]
            st_ref[c] = st
            outs, news = [], []
            for qt_h, kt_h, qi_h, ks_h, v_h, st_h, dec_h in zip(*map(_heads, (qt, kt, qi, ks, v, st, dec))):
                scores = jnp.where(causal, _dot(qt_h, kt_h, NT), 0.0)
                outs.append(_dot(scores, v_h, NN) + _dot(qi_h, st_h, NT))
                news.append(dec_h * st_h + _dot(v_h, ks_h, TN))
            o_ref[rows, :] = jnp.concatenate(outs, axis=1)
            s_scr[...] = jnp.concatenate(news, axis=1)
            return carry

        lax.fori_loop(0, ncb, chunk, 0)

    grp = lambda g: pl.BlockSpec((tb, HGRN_WIDTH), lambda i: (i, g))
    return pl.pallas_call(
        body, name="hgrn_fwd", grid=(t // tb,),
        in_specs=[grp(0), grp(1), grp(2), pl.BlockSpec((2, HGRN_WIDTH), lambda i: (0, 0))],
        out_specs=[grp(0), pl.BlockSpec((ncb, HEAD_DIM, HGRN_WIDTH), lambda i: (i, 0, 0))],
        out_shape=[jax.ShapeDtypeStruct((t, HGRN_WIDTH), F32),
                   jax.ShapeDtypeStruct((t // CHUNK, HEAD_DIM, HGRN_WIDTH), F32)],
        scratch_shapes=[pltpu.VMEM((HEAD_DIM, HGRN_WIDTH), F32)],
        compiler_params=_params("arbitrary"),
    )(proj, proj, proj, lb_logits)


def _conv_taps(z, halo, zbuf, tb):
    zbuf[0:SUBLANES, :] = halo
    zbuf[SUBLANES:SUBLANES + tb, :] = z
    return zbuf[SUBLANES - 1:SUBLANES - 1 + tb, :], zbuf[SUBLANES - 2:SUBLANES - 2 + tb, :]


def _gate_fwd(proj, o, gate_norm_w, conv_w):
    t = proj.shape[0]
    tb = min(t, 512)
    hb = tb // SUBLANES

    def body(o_ref, og_ref, gnw_ref, b_ref, c_ref, u_ref, ch_ref, uh_ref, cw_ref, cat_ref, zbuf):
        i = pl.program_id(1)
        ov, og = o_ref[...], og_ref[...]
        on = ov * lax.rsqrt(jnp.mean(ov * ov, axis=-1, keepdims=True) + EPS)
        cat_ref[0] = (on * gnw_ref[...] * (og * _sigmoid(og))).astype(cat_ref.dtype)
        z = c_ref[...] * u_ref[...]
        halo = jnp.where(i > 0, ch_ref[...] * uh_ref[...], 0.0)
        z1, z2 = _conv_taps(z, halo, zbuf, tb)
        cw = cw_ref[...]
        yc = cw[2:3, :] * z + cw[1:2, :] * z1 + cw[0:1, :] * z2
        cat_ref[1] = (b_ref[...] * yc).astype(cat_ref.dtype)

    blk = lambda off: pl.BlockSpec((tb, LANES), lambda j, i: (i, off + j))
    prev = lambda off: pl.BlockSpec((SUBLANES, LANES), lambda j, i: (jnp.maximum(i * hb - 1, 0), off + j))
    vec = lambda r: pl.BlockSpec((r, LANES), lambda j, i: (0, j))
    return pl.pallas_call(
        body, name="gate_fwd", grid=(4, t // tb),
        in_specs=[blk(0), blk(12), vec(1), blk(16), blk(20), blk(24), prev(20), prev(24), vec(3)],
        out_specs=pl.BlockSpec((2, tb, LANES), lambda j, i: (0, i, j)),
        out_shape=jax.ShapeDtypeStruct((2, t, HGRN_WIDTH), BF16),
        scratch_shapes=[pltpu.VMEM((tb + SUBLANES, LANES), F32)],
        compiler_params=_params("parallel", "arbitrary"),
    )(o, proj, gate_norm_w, proj, proj, proj, proj, proj, conv_w)


def _out_ln1(cat, w_out, x, g1, b1):
    t = x.shape[0]
    tm = min(t, 512)

    def body(cat_ref, w_ref, x_ref, g_ref, b_ref, xhat_ref, h1_ref, rstd_ref):
        mix = _dot(cat_ref[0], w_ref[0:GROUP, :], NN) + _dot(cat_ref[1], w_ref[GROUP:2 * GROUP, :], NN)
        pre = ALPHA * x_ref[...] + mix
        xc = pre - jnp.mean(pre, axis=-1, keepdims=True)
        rstd = lax.rsqrt(jnp.mean(xc * xc, axis=-1, keepdims=True) + EPS)
        xhat = xc * rstd
        xhat_ref[...] = xhat
        h1_ref[...] = (xhat * g_ref[...] + b_ref[...]).astype(h1_ref.dtype)
        rstd_ref[...] = rstd

    row = pl.BlockSpec((tm, D_MODEL), lambda i: (i, 0))
    vec = pl.BlockSpec((1, D_MODEL), lambda i: (0, 0))
    return pl.pallas_call(
        body, name="out_ln1", grid=(t // tm,),
        in_specs=[pl.BlockSpec((2, tm, GROUP), lambda i: (0, i, 0)), _resident((D_MODEL, D_MODEL)), row, vec, vec],
        out_specs=[row, row, pl.BlockSpec((tm, 1), lambda i: (i, 0))],
        out_shape=[jax.ShapeDtypeStruct((t, D_MODEL), F32), jax.ShapeDtypeStruct((t, D_MODEL), BF16),
                   jax.ShapeDtypeStruct((t, 1), F32)],
        compiler_params=_params("parallel"),
    )(cat, w_out, x, g1, b1)


def _ln_bwd(dy, xhat, rstd, g):
    dxhat = dy * g
    m1 = jnp.mean(dxhat, axis=-1, keepdims=True)
    m2 = jnp.mean(dxhat * xhat, axis=-1, keepdims=True)
    return rstd * (dxhat - m1 - xhat * m2)


def _mlp_fwd(xhat1, g1, b1, w_ff1, w_ff2, g2, b2, target):
    t = xhat1.shape[0]
    tm = min(t, 256)

    def body(xh_ref, g1_ref, b1_ref, w1_ref, w2_ref, g2_ref, b2_ref, tg_ref,
             r_ref, dpre_ref, dpreb_ref, dg_ref, db_ref, loss_ref):
        @pl.when(pl.program_id(0) == 0)
        def _():
            dg_ref[...] = jnp.zeros_like(dg_ref)
            db_ref[...] = jnp.zeros_like(db_ref)
            loss_ref[...] = jnp.zeros_like(loss_ref)

        h1 = xh_ref[...] * g1_ref[...] + b1_ref[...]
        h1b = h1.astype(MXU_DTYPE)
        mlp = jnp.zeros((tm, D_MODEL), F32)
        for j in range(D_FF // FF_BLOCK):
            cols = slice(j * FF_BLOCK, (j + 1) * FF_BLOCK)
            r = jnp.square(jnp.maximum(_dot(h1b, w1_ref[:, cols], NN), 0.0)).astype(r_ref.dtype)
            r_ref[:, cols] = r
            mlp = mlp + _dot(r, w2_ref[cols, :], NN)
        pre = ALPHA * h1 + mlp
        xc = pre - jnp.mean(pre, axis=-1, keepdims=True)
        rstd = lax.rsqrt(jnp.mean(xc * xc, axis=-1, keepdims=True) + EPS)
        xhat = xc * rstd
        err = xhat * g2_ref[...] + b2_ref[...] - tg_ref[...]
        loss_ref[...] += 0.5 * jnp.sum(jnp.mean(err * err, axis=-1, keepdims=True))
        dy = err * (1.0 / D_MODEL)
        dg_ref[...] += jnp.sum(dy * xhat, axis=0, keepdims=True)
        db_ref[...] += jnp.sum(dy, axis=0, keepdims=True)
        dpre = _ln_bwd(dy, xhat, rstd, g2_ref[...])
        dpre_ref[...] = dpre
        dpreb_ref[...] = dpre.astype(dpreb_ref.dtype)

    row = pl.BlockSpec((tm, D_MODEL), lambda i: (i, 0))
    vec = pl.BlockSpec((1, D_MODEL), lambda i: (0, 0))
    return pl.pallas_call(
        body, name="mlp_fwd", grid=(t // tm,),
        in_specs=[row, vec, vec, _resident((D_MODEL, D_FF)), _resident((D_FF, D_MODEL)), vec, vec, row],
        out_specs=[pl.BlockSpec((tm, D_FF), lambda i: (i, 0)), row, row, vec, vec,
                   pl.BlockSpec((SUBLANES, LANES), lambda i: (0, 0))],
        out_shape=[jax.ShapeDtypeStruct((t, D_FF), BF16), jax.ShapeDtypeStruct((t, D_MODEL), F32),
                   jax.ShapeDtypeStruct((t, D_MODEL), BF16), jax.ShapeDtypeStruct((1, D_MODEL), F32),
                   jax.ShapeDtypeStruct((1, D_MODEL), F32), jax.ShapeDtypeStruct((SUBLANES, LANES), F32)],
        compiler_params=_params("arbitrary"),
    )(xhat1, g1, b1, w_ff1, w_ff2, g2, b2, target)


def _mlp_bwd(dpre2, r, w_ff1, w_ff2, xhat1, rstd1, g1):
    t = r.shape[0]
    tm = min(t, 256)

    def body(dp2_ref, r_ref, w1_ref, w2_ref, xh_ref, rs_ref, g_ref, da_ref, dpre_ref, dpreb_ref, dg_ref, db_ref):
        @pl.when(pl.program_id(0) == 0)
        def _():
            dg_ref[...] = jnp.zeros_like(dg_ref)
            db_ref[...] = jnp.zeros_like(db_ref)

        dp2 = dp2_ref[...]
        dp2b = dp2.astype(MXU_DTYPE)
        back = jnp.zeros((tm, D_MODEL), F32)
        for j in range(D_FF // FF_BLOCK):
            cols = slice(j * FF_BLOCK, (j + 1) * FF_BLOCK)
            dr = _dot(dp2b, w2_ref[cols, :], NT)
            da = (dr * (2.0 * jnp.sqrt(r_ref[:, cols].astype(F32)))).astype(da_ref.dtype)
            da_ref[:, cols] = da
            back = back + _dot(da, w1_ref[:, cols], NT)
        dh1 = ALPHA * dp2 + back
        xhat = xh_ref[...]
        dg_ref[...] += jnp.sum(dh1 * xhat, axis=0, keepdims=True)
        db_ref[...] += jnp.sum(dh1, axis=0, keepdims=True)
        dpre = _ln_bwd(dh1, xhat, rs_ref[...], g_ref[...])
        dpre_ref[...] = dpre
        dpreb_ref[...] = dpre.astype(dpreb_ref.dtype)

    row = pl.BlockSpec((tm, D_MODEL), lambda i: (i, 0))
    wide = pl.BlockSpec((tm, D_FF), lambda i: (i, 0))
    vec = pl.BlockSpec((1, D_MODEL), lambda i: (0, 0))
    return pl.pallas_call(
        body, name="mlp_bwd", grid=(t // tm,),
        in_specs=[row, wide, _resident((D_MODEL, D_FF)), _resident((D_FF, D_MODEL)), row,
                  pl.BlockSpec((tm, 1), lambda i: (i, 0)), vec],
        out_specs=[wide, row, row, vec, vec],
        out_shape=[jax.ShapeDtypeStruct((t, D_FF), BF16), jax.ShapeDtypeStruct((t, D_MODEL), F32),
                   jax.ShapeDtypeStruct((t, D_MODEL), BF16), jax.ShapeDtypeStruct((1, D_MODEL), F32),
                   jax.ShapeDtypeStruct((1, D_MODEL), F32)],
        compiler_params=_params("arbitrary"),
    )(dpre2, r, w_ff1, w_ff2, xhat1, rstd1, g1)


def _out_bwd(dpre1b, w_out):
    t = dpre1b.shape[0]
    tm = min(t, 512)

    def body(d_ref, w_ref, o_ref):
        o_ref[...] = _dot(d_ref[...], w_ref[...], NT)

    return pl.pallas_call(
        body, name="out_bwd", grid=(t // tm,),
        in_specs=[pl.BlockSpec((tm, D_MODEL), lambda i: (i, 0)), _resident((D_MODEL, D_MODEL))],
        out_specs=pl.BlockSpec((tm, D_MODEL), lambda i: (i, 0)),
        out_shape=jax.ShapeDtypeStruct((t, D_MODEL), F32),
        compiler_params=_params("parallel"),
    )(dpre1b, w_out)


def _gate_bwd(dcat, o, proj, gate_norm_w, conv_w):
    t = proj.shape[0]
    tb = min(t, 512)
    hb = tb // SUBLANES
    nblk = t // tb

    def body(do2_ref, dy_ref, dyn_ref, o_ref, og_ref, gnw_ref, b_ref, bn_ref, c_ref, u_ref, ch_ref, uh_ref, cw_ref,
             do_ref, dp_ref, dgnw_ref, dcw_ref, zbuf, dbuf):
        i = pl.program_id(1)

        @pl.when(i == 0)
        def _():
            dgnw_ref[...] = jnp.zeros_like(dgnw_ref)
            dcw_ref[...] = jnp.zeros_like(dcw_ref)

        ov, og, gnw, do2 = o_ref[...], og_ref[...], gnw_ref[...], do2_ref[...]
        rs = lax.rsqrt(jnp.mean(ov * ov, axis=-1, keepdims=True) + EPS)
        on = ov * rs
        sg = _sigmoid(og)
        sil = og * sg
        don = do2 * gnw * sil
        dgnw_ref[...] += jnp.sum(do2 * on * sil, axis=0, keepdims=True)
        dp_ref[0] = (do2 * on * gnw * (sg * (1.0 + og * (1.0 - sg)))).astype(dp_ref.dtype)
        do_ref[...] = rs * (don - on * jnp.mean(don * on, axis=-1, keepdims=True))

        bg, cg, u, dy = b_ref[...], c_ref[...], u_ref[...], dy_ref[...]
        z = cg * u
        halo = jnp.where(i > 0, ch_ref[...] * uh_ref[...], 0.0)
        z1, z2 = _conv_taps(z, halo, zbuf, tb)
        cw = cw_ref[...]
        yc = cw[2:3, :] * z + cw[1:2, :] * z1 + cw[0:1, :] * z2
        dyc = dy * bg
        dbuf[0:tb, :] = dyc
        dbuf[tb:tb + SUBLANES, :] = jnp.where(i < nblk - 1, dyn_ref[...] * bn_ref[...], 0.0)
        d1, d2 = dbuf[1:1 + tb, :], dbuf[2:2 + tb, :]
        dz = cw[2:3, :] * dyc + cw[1:2, :] * d1 + cw[0:1, :] * d2
        dp_ref[1] = (dy * yc).astype(dp_ref.dtype)
        dp_ref[2] = (dz * u).astype(dp_ref.dtype)
        dp_ref[3] = (dz * cg).astype(dp_ref.dtype)
        dcw_ref[0:1, :] += jnp.sum(dyc * z2, axis=0, keepdims=True)
        dcw_ref[1:2, :] += jnp.sum(dyc * z1, axis=0, keepdims=True)
        dcw_ref[2:3, :] += jnp.sum(dyc * z, axis=0, keepdims=True)

    blk = lambda off: pl.BlockSpec((tb, LANES), lambda j, i: (i, off + j))
    prev = lambda off: pl.BlockSpec((SUBLANES, LANES), lambda j, i: (jnp.maximum(i * hb - 1, 0), off + j))
    nxt = lambda off: pl.BlockSpec((SUBLANES, LANES), lambda j, i: (jnp.minimum((i + 1) * hb, t // SUBLANES - 1), off + j))
    vec = lambda r: pl.BlockSpec((r, LANES), lambda j, i: (0, j))
    return pl.pallas_call(
        body, name="gate_bwd", grid=(4, nblk),
        in_specs=[blk(0), blk(4), nxt(4), blk(0), blk(12), vec(1), blk(16), nxt(16), blk(20), blk(24), prev(20), prev(24),
                  vec(3)],
        out_specs=[blk(0), pl.BlockSpec((4, tb, LANES), lambda j, i: (0, i, j)), vec(1), vec(3)],
        out_shape=[jax.ShapeDtypeStruct((t, HGRN_WIDTH), F32), jax.ShapeDtypeStruct((4, t, HGRN_WIDTH), BF16),
                   jax.ShapeDtypeStruct((1, HGRN_WIDTH), F32), jax.ShapeDtypeStruct((3, CONV_WIDTH), F32)],
        scratch_shapes=[pltpu.VMEM((tb + SUBLANES, LANES), F32), pltpu.VMEM((tb + SUBLANES, LANES), F32)],
        compiler_params=_params("parallel", "arbitrary"),
    )(dcat, dcat, dcat, o, proj, gate_norm_w, proj, proj, proj, proj, proj, proj, conv_w)


def _hgrn_bwd(proj, do, states, lb_logits):
    t = proj.shape[0]
    tb = min(t, 512)
    ncb = tb // CHUNK
    nblk = t // tb

    def body(q_ref, f_ref, v_ref, do_ref, st_ref, lbl_ref, dp_ref, dlbl_ref, ds_scr, dlb_scr):
        i = pl.program_id(0)

        @pl.when(i == 0)
        def _():
            ds_scr[...] = jnp.zeros_like(ds_scr)
            dlb_scr[...] = jnp.zeros_like(dlb_scr)

        lb, s1 = _lower_bound(lbl_ref[...])
        causal, anti = _chunk_masks()
        tri, tri_rev = causal.astype(F32), anti.astype(F32)

        def chunk(n, carry):
            c = ncb - 1 - n
            rows = pl.ds(pl.multiple_of(c * CHUNK, CHUNK), CHUNK)
            q, v, do_c = q_ref[rows, :], v_ref[rows, :], do_ref[rows, :]
            sig, f, g, k = _gates(f_ref[rows, :], lb)
            b = _dot_exact(tri, g)
            b_ref, b_last = b[CHUNK // 2:CHUNK // 2 + 1, :], b[CHUNK - 1:CHUNK, :]
            e_q, e_k, e_i, e_s = jnp.exp(b - b_ref), jnp.exp(b_ref - b), jnp.exp(b), jnp.exp(b_last - b)
            dec = jnp.exp(b_last)
            qt, kt, qi, ks = q * e_q, k * e_k, q * e_i, k * e_s
            st, dst = st_ref[c], ds_scr[...]

            dqt, dkt, dv, dqi, dks, new_ds = [], [], [], [], [], []
            for qt_h, kt_h, qi_h, ks_h, v_h, do_h, st_h, dst_h, dec_h in zip(
                    *map(_heads, (qt, kt, qi, ks, v, do_c, st, dst, dec))):
                scores = jnp.where(causal, _dot(qt_h, kt_h, NT), 0.0)
                dscores = jnp.where(causal, _dot(do_h, v_h, NT), 0.0)
                dqt.append(_dot(dscores, kt_h, NN))
                dkt.append(_dot(dscores, qt_h, TN))
                dv.append(_dot(scores, do_h, TN) + _dot(ks_h, dst_h, NT))
                dqi.append(_dot(do_h, st_h, NN))
                dks.append(_dot(v_h, dst_h, NN))
                new_ds.append(dec_h * dst_h + _dot(do_h, qi_h, TN))
            dqt, dkt, dv, dqi, dks = (jnp.concatenate(p, axis=1) for p in (dqt, dkt, dv, dqi, dks))
            ddec = jnp.sum(dst * st, axis=0, keepdims=True)
            ds_scr[...] = jnp.concatenate(new_ds, axis=1)

            dq = dqt * e_q + dqi * e_i
            dk = dkt * e_k + dks * e_s
            db = q * dq - k * dk
            db_last = jnp.sum(dks * ks, axis=0, keepdims=True) + ddec * dec
            dg = _dot_exact(tri_rev, db) + db_last
            df = dg / f - dk
            dlb_scr[...] += jnp.sum(df * (1.0 - sig), axis=0, keepdims=True)
            dp_ref[0, rows, :] = dq.astype(dp_ref.dtype)
            dp_ref[1, rows, :] = (df * (1.0 - lb) * sig * (1.0 - sig)).astype(dp_ref.dtype)
            dp_ref[2, rows, :] = dv.astype(dp_ref.dtype)
            return carry

        lax.fori_loop(0, ncb, chunk, 0)

        @pl.when(i == nblk - 1)
        def _():
            dlb = dlb_scr[...]
            dlbl_ref[0:1, :] = dlb * lb * (1.0 - lb)
            dlbl_ref[1:2, :] = -dlb * lb * s1

    grp = lambda g: pl.BlockSpec((tb, HGRN_WIDTH), lambda i: (nblk - 1 - i, g))
    vec = pl.BlockSpec((2, HGRN_WIDTH), lambda i: (0, 0))
    return pl.pallas_call(
        body, name="hgrn_bwd", grid=(nblk,),
        in_specs=[grp(0), grp(1), grp(2), grp(0),
                  pl.BlockSpec((ncb, HEAD_DIM, HGRN_WIDTH), lambda i: (nblk - 1 - i, 0, 0)), vec],
        out_specs=[pl.BlockSpec((3, tb, HGRN_WIDTH), lambda i: (0, nblk - 1 - i, 0)), vec],
        out_shape=[jax.ShapeDtypeStruct((3, t, HGRN_WIDTH), BF16), jax.ShapeDtypeStruct((2, HGRN_WIDTH), F32)],
        scratch_shapes=[pltpu.VMEM((HEAD_DIM, HGRN_WIDTH), F32), pltpu.VMEM((1, HGRN_WIDTH), F32)],
        compiler_params=_params("arbitrary"),
    )(proj, proj, proj, do, states, lb_logits)


def _in_bwd(dph, dpg, w_in, dpre1):
    t = dpre1.shape[0]
    tm = min(t, 512)

    def body(dh_ref, dg_ref, w_ref, dp_ref, o_ref):
        acc = ALPHA * dp_ref[...]
        for g in range(N_GROUPS):
            part = dh_ref[g] if g < 3 else dg_ref[g - 3]
            acc = acc + _dot(part, w_ref[:, g * GROUP:(g + 1) * GROUP], NT)
        o_ref[...] = acc

    row = pl.BlockSpec((tm, D_MODEL), lambda i: (i, 0))
    return pl.pallas_call(
        body, name="in_bwd", grid=(t // tm,),
        in_specs=[pl.BlockSpec((3, tm, GROUP), lambda i: (0, i, 0)), pl.BlockSpec((4, tm, GROUP), lambda i: (0, i, 0)),
                  _resident((D_MODEL, IN_COLS)), row],
        out_specs=row,
        out_shape=jax.ShapeDtypeStruct((t, D_MODEL), F32),
        compiler_params=_params("parallel"),
    )(dph, dpg, w_in, dpre1)


def _grad_w(name, operands, in_specs, shape, tt, step):
    n_steps = operands[0].shape[-2] // tt

    def body(*refs):
        o_ref, acc, sem = refs[-3:]
        k = pl.program_id(0)

        @pl.when(k == 0)
        def _():
            acc[...] = jnp.zeros_like(acc)

        step(acc, *refs[:-3])

        @pl.when(k == n_steps - 1)
        def _():
            out = pltpu.make_async_copy(acc, o_ref, sem)
            out.start()
            out.wait()

    return pl.pallas_call(
        body, name=name, grid=(n_steps,), in_specs=in_specs, out_specs=ANY,
        out_shape=jax.ShapeDtypeStruct(shape, F32),
        scratch_shapes=[pltpu.VMEM(shape, F32), pltpu.SemaphoreType.DMA],
        compiler_params=_params("arbitrary"),
    )(*operands)


def _weight_grads(xb, dph, dpg, cat, dpre1b, h1b, da, r, dpre2b):
    t = xb.shape[0]
    tt = min(t, 512)
    tile = lambda width: pl.BlockSpec((tt, width), lambda k: (k, 0))
    groups = lambda n: pl.BlockSpec((n, tt, GROUP), lambda k: (0, k, 0))

    def step_in(acc, x_ref, dh_ref, dg_ref):
        xv = x_ref[...]
        for g in range(N_GROUPS):
            part = dh_ref[g] if g < 3 else dg_ref[g - 3]
            acc[:, g * GROUP:(g + 1) * GROUP] += _dot(xv, part, TN)

    def step_out(acc, cat_ref, d_ref):
        dv = d_ref[...]
        for g in range(2):
            acc[g * GROUP:(g + 1) * GROUP, :] += _dot(cat_ref[g], dv, TN)

    def step_ff1(acc, h_ref, da_ref):
        hv = h_ref[...]
        for j in range(D_FF // FF_BLOCK):
            cols = slice(j * FF_BLOCK, (j + 1) * FF_BLOCK)
            acc[:, cols] += _dot(hv, da_ref[:, cols], TN)

    def step_ff2(acc, r_ref, d_ref):
        dv = d_ref[...]
        for j in range(D_FF // FF_BLOCK):
            rows = slice(j * FF_BLOCK, (j + 1) * FF_BLOCK)
            acc[rows, :] += _dot(r_ref[:, rows], dv, TN)

    g_in = _grad_w("dw_in", (xb, dph, dpg), [tile(D_MODEL), groups(3), groups(4)], (D_MODEL, IN_COLS), tt, step_in)
    g_out = _grad_w("dw_out", (cat, dpre1b), [groups(2), tile(D_MODEL)], (D_MODEL, D_MODEL), tt, step_out)
    g_ff1 = _grad_w("dw_ff1", (h1b, da), [tile(D_MODEL), tile(D_FF)], (D_MODEL, D_FF), tt, step_ff1)
    g_ff2 = _grad_w("dw_ff2", (r, dpre2b), [tile(D_FF), tile(D_MODEL)], (D_FF, D_MODEL), tt, step_ff2)
    return g_in, g_out, g_ff1, g_ff2


def _place():
    x, y, c = lax.axis_index("x"), lax.axis_index("y"), lax.axis_index("c")
    return x, y, c, 2 * x + y


def _other_chips(x, y):
    return [(1 - x, y), (x, 1 - y), (1 - x, 1 - y)]


def _shard_views(refs, chip):
    r_in, r_out, r_ff1, r_ff2 = refs
    return [r_in.at[:, pl.ds(chip * (IN_COLS // N_CHIPS), IN_COLS // N_CHIPS)],
            r_out.at[pl.ds(chip * (D_MODEL // N_CHIPS), D_MODEL // N_CHIPS), :],
            r_ff1.at[:, pl.ds(chip * (D_FF // N_CHIPS), D_FF // N_CHIPS)],
            r_ff2.at[pl.ds(chip * (D_FF // N_CHIPS), D_FF // N_CHIPS), :]]


def _place_shard(name, w, chip, cols_sharded, after=None):
    rows, cols = w.shape
    tr = min(rows, 256)
    nb = rows // tr
    full = (rows, cols * N_CHIPS) if cols_sharded else (rows * N_CHIPS, cols)
    out_map = (lambda i, s: (i, s[0])) if cols_sharded else (lambda i, s: (s[0] * nb + i, 0))

    def body(s_ref, w_ref, *rest):
        rest[-1][...] = w_ref[...].astype(rest[-1].dtype)

    extra = [] if after is None else [after]
    return pl.pallas_call(
        body, name=name,
        grid_spec=pltpu.PrefetchScalarGridSpec(
            num_scalar_prefetch=1, grid=(nb,),
            in_specs=[pl.BlockSpec((tr, cols), lambda i, s: (i, 0))] + [ANY] * len(extra),
            out_specs=pl.BlockSpec((tr, cols), out_map)),
        out_shape=jax.ShapeDtypeStruct(full, BF16),
        compiler_params=_params("parallel"),
    )(chip, w, *extra)


def _gather_w_in(w_in, conv_w):
    half, cs, n_p = D_MODEL // 2, IN_COLS // N_CHIPS, 3

    def body(w_alias, cv_ref, w_ref, cvf_ref, send_sems, recv_sems, local_sem):
        x, y, c, me = _place()
        sibling = (x, y, 1 - c)
        chips = _other_chips(x, y)
        blk = lambda chip, h: w_ref.at[pl.ds(h * half, half), pl.ds(chip * cs, cs)]

        def copy(k, src, dst, to):
            return pltpu.make_async_remote_copy(src_ref=src, dst_ref=dst, send_sem=send_sems.at[k],
                                                recv_sem=recv_sems.at[k], device_id=to, device_id_type=MESH)

        own_cv = pltpu.make_async_copy(cv_ref, cvf_ref.at[me], local_sem)
        own_cv.start()
        first = [copy(j, blk(me, c), blk(me, c), (px, py, c)) for j, (px, py) in enumerate(chips)]
        first += [copy(2 * n_p + j, cv_ref, cvf_ref.at[me], (px, py, c)) for j, (px, py) in enumerate(chips)]
        for cp in first:
            cp.start()
        passed = []
        for j, (px, py) in enumerate(chips):
            got = blk(2 * px + py, c)
            copy(j, got, got, (px, py, c)).wait_recv()
            passed.append(copy(n_p + j, got, got, sibling))
            passed[-1].start()
        for j, (px, py) in enumerate(chips):
            got = blk(2 * px + py, 1 - c)
            copy(n_p + j, got, got, sibling).wait_recv()
            copy(2 * n_p + j, cv_ref, cvf_ref.at[2 * px + py], (px, py, c)).wait_recv()
        for cp in first + passed:
            cp.wait_send()
        own_cv.wait()

    return pl.pallas_call(
        body, name="gather_w_in", in_specs=[ANY, ANY], out_specs=[ANY, ANY],
        out_shape=[jax.ShapeDtypeStruct(w_in.shape, w_in.dtype), jax.ShapeDtypeStruct((N_CHIPS,) + conv_w.shape, conv_w.dtype)],
        input_output_aliases={0: 0},
        scratch_shapes=[pltpu.SemaphoreType.DMA((3 * n_p,)), pltpu.SemaphoreType.DMA((3 * n_p,)), pltpu.SemaphoreType.DMA],
    )(w_in, conv_w)


HBM = pl.BlockSpec(memory_space=pltpu.HBM)
SEM = pl.BlockSpec(memory_space=pltpu.SEMAPHORE)
EFFECT = pltpu.SideEffectType.DATAFLOW_SIDE_EFFECTING


def _rest_views(refs, chip):
    r_out, r_ff1, r_ff2 = refs
    return [r_out.at[pl.ds(chip * (D_MODEL // N_CHIPS), D_MODEL // N_CHIPS), :],
            r_ff1.at[:, pl.ds(chip * (D_FF // N_CHIPS), D_FF // N_CHIPS)],
            r_ff2.at[pl.ds(chip * (D_FF // N_CHIPS), D_FF // N_CHIPS), :]]


def _gather_rest_start(w_out, w_ff1, w_ff2):
    arrays = (w_out, w_ff1, w_ff2)

    def body(o_ref, f1_ref, f2_ref, send_sems, recv_sems, o_thru, f1_thru, f2_thru, token):
        x, y, c, me = _place()
        mine = _rest_views((o_ref, f1_ref, f2_ref), me)
        for n in range(3):
            for j, (px, py) in enumerate(_other_chips(x, y)):
                pltpu.make_async_remote_copy(
                    src_ref=mine[n], dst_ref=mine[n], send_sem=send_sems.at[3 * n + j], recv_sem=recv_sems.at[3 * n + j],
                    device_id=(px, py, c), device_id_type=MESH).start()
        token[...] = jnp.zeros_like(token)

    return pl.pallas_call(
        body, name="gather_rest_start",
        out_shape=(pltpu.SemaphoreType.DMA((9,)), pltpu.SemaphoreType.DMA((9,)),
                   *[pltpu.HBM(a.shape, a.dtype) for a in arrays], jax.ShapeDtypeStruct((SUBLANES, LANES), F32)),
        in_specs=(HBM, HBM, HBM), out_specs=(SEM, SEM, HBM, HBM, HBM, pl.BlockSpec(memory_space=pltpu.VMEM)),
        input_output_aliases={0: 2, 1: 3, 2: 4},
        compiler_params=pltpu.CompilerParams(has_side_effects=EFFECT),
    )(*[pltpu.with_memory_space_constraint(a, pltpu.HBM) for a in arrays])


def _gather_rest_wait(send_sems, recv_sems, w_out, w_ff1, w_ff2, after):
    arrays = (w_out, w_ff1, w_ff2)

    def body(o_ref, f1_ref, f2_ref, send_sems, recv_sems, after_ref, o_out, f1_out, f2_out):
        x, y, c, me = _place()
        mine = _rest_views((o_ref, f1_ref, f2_ref), me)
        for n in range(3):
            for j, (px, py) in enumerate(_other_chips(x, y)):
                cp = pltpu.make_async_remote_copy(
                    src_ref=mine[n], dst_ref=_rest_views((o_ref, f1_ref, f2_ref), 2 * px + py)[n],
                    send_sem=send_sems.at[3 * n + j], recv_sem=recv_sems.at[3 * n + j],
                    device_id=(px, py, c), device_id_type=MESH)
                cp.wait_send()
                cp.wait_recv()

    return pl.pallas_call(
        body, name="gather_rest_wait", out_shape=tuple(pltpu.HBM(a.shape, a.dtype) for a in arrays),
        in_specs=(HBM, HBM, HBM, SEM, SEM, ANY), out_specs=(HBM, HBM, HBM), input_output_aliases={0: 0, 1: 1, 2: 2},
        compiler_params=pltpu.CompilerParams(has_side_effects=EFFECT),
    )(w_out, w_ff1, w_ff2, send_sems, recv_sems, after)


def _half_views(refs, h):
    r_in, r_out, r_ff1, r_ff2 = refs
    return [r_in.at[pl.ds(h * (D_MODEL // 2), D_MODEL // 2), :], r_out.at[:, pl.ds(h * (D_MODEL // 2), D_MODEL // 2)],
            r_ff1.at[pl.ds(h * (D_MODEL // 2), D_MODEL // 2), :], r_ff2.at[:, pl.ds(h * (D_MODEL // 2), D_MODEL // 2)]]


HALF_SHAPES = [(D_MODEL // 2, IN_COLS), (D_MODEL, D_MODEL // 2), (D_MODEL // 2, D_FF), (D_FF, D_MODEL // 2)]
PIECE_SHAPES = [(D_MODEL // 2, IN_COLS // N_CHIPS), (D_MODEL // N_CHIPS, D_MODEL // 2),
                (D_MODEL // 2, D_FF // N_CHIPS), (D_FF // N_CHIPS, D_MODEL // 2)]
SHARD_SHAPES = [(D_MODEL, IN_COLS // N_CHIPS), (D_MODEL // N_CHIPS, D_MODEL), (D_MODEL, D_FF // N_CHIPS),
                (D_FF // N_CHIPS, D_MODEL)]


def _swap_halves(grads):
    def body(g_in, g_out, g_ff1, g_ff2, r_in, r_out, r_ff1, r_ff2, send_sems, recv_sems):
        x, y, c, _ = _place()
        srcs = _half_views((g_in, g_out, g_ff1, g_ff2), 1 - c)
        copies = [pltpu.make_async_remote_copy(
            src_ref=srcs[n], dst_ref=dst, send_sem=send_sems.at[n], recv_sem=recv_sems.at[n],
            device_id=(x, y, 1 - c), device_id_type=MESH) for n, dst in enumerate((r_in, r_out, r_ff1, r_ff2))]
        for cp in copies:
            cp.start()
        for cp in copies:
            cp.wait()

    return pl.pallas_call(
        body, name="swap_halves", in_specs=[ANY] * 4, out_specs=[ANY] * 4,
        out_shape=[jax.ShapeDtypeStruct(s, F32) for s in HALF_SHAPES],
        scratch_shapes=[pltpu.SemaphoreType.DMA((4,)), pltpu.SemaphoreType.DMA((4,))],
    )(*grads)


def _add_half(name, g, recv, core, rows_split):
    shape = recv.shape
    tr = min(shape[0], 128 if rows_split else 256)
    nb = shape[0] // tr

    def body(c_ref, g_ref, r_ref, o_ref):
        o_ref[...] = (g_ref[...] + r_ref[...]).astype(o_ref.dtype)

    g_map = (lambda i, c_ref: (c_ref[0] * nb + i, 0)) if rows_split else (lambda i, c_ref: (i, c_ref[0]))
    blk = pl.BlockSpec((tr, shape[1]), lambda i, c_ref: (i, 0))
    return pl.pallas_call(
        body, name=name,
        grid_spec=pltpu.PrefetchScalarGridSpec(
            num_scalar_prefetch=1, grid=(nb,),
            in_specs=[pl.BlockSpec((tr, shape[1]), g_map), blk], out_specs=blk),
        out_shape=jax.ShapeDtypeStruct(shape, BF16),
        compiler_params=_params("parallel"),
    )(core, g, recv)


def _exchange_pieces(halves):
    n_t, n_p = 4, 3

    def body(p_in, p_out, p_ff1, p_ff2, r_in, r_out, r_ff1, r_ff2, send_sems, recv_sems):
        x, y, c, me = _place()
        pieces = lambda chip: _shard_views((p_in, p_out, p_ff1, p_ff2), chip)
        slots = lambda j: [r.at[j] for r in (r_in, r_out, r_ff1, r_ff2)]
        copies = []
        for j, (px, py) in enumerate(_other_chips(x, y)):
            for n in range(n_t):
                copies.append(pltpu.make_async_remote_copy(
                    src_ref=pieces(2 * px + py)[n], dst_ref=slots(j)[n], send_sem=send_sems.at[n_p * n + j],
                    recv_sem=recv_sems.at[n_p * n + j], device_id=(px, py, c), device_id_type=MESH))
        for cp in copies:
            cp.start()
        for cp in copies:
            cp.wait()

    return pl.pallas_call(
        body, name="exchange_pieces", in_specs=[ANY] * n_t, out_specs=[ANY] * n_t,
        out_shape=[jax.ShapeDtypeStruct((n_p,) + s, BF16) for s in PIECE_SHAPES],
        scratch_shapes=[pltpu.SemaphoreType.DMA((n_t * n_p,)), pltpu.SemaphoreType.DMA((n_t * n_p,))],
    )(*halves)


def _sum_pieces(name, half, slots, place, rows_split):
    n_p, rows, cols = slots.shape
    tr = min(rows, 256)
    nb = rows // tr
    if rows_split:
        own_map = lambda i, s: (i, s[0])
        out_map = lambda i, s: (s[1] * nb + i, 0)
        shard = (2 * rows, cols)
    else:
        own_map = lambda i, s: (s[0] * nb + i, 0)
        out_map = lambda i, s: (i, s[1])
        shard = (rows, 2 * cols)

    def body(s_ref, own_ref, slot_ref, o_ref):
        total = own_ref[...].astype(F32)
        for j in range(n_p):
            total = total + slot_ref[j].astype(F32)
        o_ref[...] = total

    return pl.pallas_call(
        body, name=name,
        grid_spec=pltpu.PrefetchScalarGridSpec(
            num_scalar_prefetch=1, grid=(nb,),
            in_specs=[pl.BlockSpec((tr, cols), own_map), pl.BlockSpec((n_p, tr, cols), lambda i, s: (0, i, 0))],
            out_specs=pl.BlockSpec((tr, cols), out_map)),
        out_shape=jax.ShapeDtypeStruct(shard, F32),
        compiler_params=_params("parallel"),
    )(place, half, slots)


def _half_views_shard(refs, h):
    r_in, r_out, r_ff1, r_ff2 = refs
    half = D_MODEL // 2
    return [r_in.at[pl.ds(h * half, half), :], r_out.at[:, pl.ds(h * half, half)],
            r_ff1.at[pl.ds(h * half, half), :], r_ff2.at[:, pl.ds(h * half, half)]]


def _join_halves(shards):
    def body(a_in, a_out, a_ff1, a_ff2, g_in, g_out, g_ff1, g_ff2, send_sems, recv_sems):
        x, y, c, _ = _place()
        mine = _half_views_shard((g_in, g_out, g_ff1, g_ff2), c)
        theirs = _half_views_shard((g_in, g_out, g_ff1, g_ff2), 1 - c)
        sends = [pltpu.make_async_remote_copy(
            src_ref=mine[n], dst_ref=mine[n], send_sem=send_sems.at[n], recv_sem=recv_sems.at[n],
            device_id=(x, y, 1 - c), device_id_type=MESH) for n in range(4)]
        for cp in sends:
            cp.start()
        for n in range(4):
            pltpu.make_async_remote_copy(
                src_ref=mine[n], dst_ref=theirs[n], send_sem=send_sems.at[n], recv_sem=recv_sems.at[n],
                device_id=(x, y, 1 - c), device_id_type=MESH).wait_recv()
        for cp in sends:
            cp.wait_send()

    return pl.pallas_call(
        body, name="join_halves", in_specs=[ANY] * 4, out_specs=[ANY] * 4,
        out_shape=[jax.ShapeDtypeStruct(s, F32) for s in SHARD_SHAPES],
        input_output_aliases={0: 0, 1: 1, 2: 2, 3: 3},
        scratch_shapes=[pltpu.SemaphoreType.DMA((4,)), pltpu.SemaphoreType.DMA((4,))],
    )(*shards)


def _sum_small(pack):
    n_dev = 8

    def body(p_ref, o_ref, slots, send_sems, recv_sems):
        x, y, c, _ = _place()
        me = 4 * x + 2 * y + c
        slots[me] = p_ref[...]
        sends = []
        for m in range(1, n_dev):
            peer = ((1 - x) if m & 4 else x, (1 - y) if m & 2 else y, (1 - c) if m & 1 else c)
            sends.append(pltpu.make_async_remote_copy(
                src_ref=p_ref, dst_ref=slots.at[me], send_sem=send_sems.at[m - 1], recv_sem=recv_sems.at[m - 1],
                device_id=peer, device_id_type=MESH))
        for cp in sends:
            cp.start()
        for m in range(1, n_dev):
            peer = ((1 - x) if m & 4 else x, (1 - y) if m & 2 else y, (1 - c) if m & 1 else c)
            pltpu.make_async_remote_copy(
                src_ref=p_ref, dst_ref=slots.at[4 * peer[0] + 2 * peer[1] + peer[2]], send_sem=send_sems.at[m - 1],
                recv_sem=recv_sems.at[m - 1], device_id=peer, device_id_type=MESH).wait_recv()
        for cp in sends:
            cp.wait_send()
        total = slots[0]
        for d in range(1, n_dev):
            total = total + slots[d]
        o_ref[...] = total

    vm = pl.BlockSpec(memory_space=pltpu.VMEM)
    return pl.pallas_call(
        body, name="sum_small", in_specs=[vm], out_specs=vm,
        out_shape=jax.ShapeDtypeStruct(pack.shape, F32),
        scratch_shapes=[pltpu.VMEM((n_dev,) + pack.shape, F32), pltpu.SemaphoreType.DMA((n_dev - 1,)),
                        pltpu.SemaphoreType.DMA((n_dev - 1,))],
    )(pack)


def _adamw(name, w, g, m, v):
    rows, cols = w.shape
    tr = min(rows, 256)

    def body(w_ref, g_ref, m_ref, v_ref, d_ref, nm_ref, nv_ref):
        gv = g_ref[...]
        nm = ADAM_B1 * m_ref[...] + (1.0 - ADAM_B1) * gv
        nv = ADAM_B2 * v_ref[...] + (1.0 - ADAM_B2) * jnp.square(gv)
        m_hat = nm / (1.0 - ADAM_B1 ** ADAM_STEP)
        v_hat = nv / (1.0 - ADAM_B2 ** ADAM_STEP)
        d_ref[...] = -ADAM_LR * (m_hat / (jnp.sqrt(v_hat) + ADAM_EPS) + ADAM_WD * w_ref[...])
        nm_ref[...] = nm
        nv_ref[...] = nv

    blk = pl.BlockSpec((tr, cols), lambda i: (i, 0))
    return pl.pallas_call(
        body, name=name, grid=(rows // tr,), in_specs=[blk] * 4, out_specs=[blk] * 3,
        out_shape=[jax.ShapeDtypeStruct(w.shape, F32)] * 3,
        compiler_params=_params("parallel"),
    )(w, g, m, v)


def kernel(x, w_in, lb_logits, gate_norm_w, conv_w, w_out, ln1_g, ln1_b, w_ff1, w_ff2, ln2_g, ln2_b, loss_target, m_w_in, m_lb_logits, m_gate_norm_w, m_conv_w, m_w_out, m_ln1_g, m_ln1_b, m_w_ff1, m_w_ff2, m_ln2_g, m_ln2_b, v_w_in, v_lb_logits, v_gate_norm_w, v_conv_w, v_w_out, v_ln1_g, v_ln1_b, v_w_ff1, v_w_ff2, v_ln2_g, v_ln2_b):
    xs, tgt = x[0], loss_target[0]
    chip = 2 * lax.axis_index("x") + lax.axis_index("y")
    core = lax.axis_index("c").astype(jnp.int32).reshape(1)
    chip1 = chip.astype(jnp.int32).reshape(1)
    place = jnp.concatenate([chip1, core])

    wb_in, cv4 = _gather_w_in(_place_shard("place_w_in", w_in[0], chip1, True), conv_w[0])
    conv_full = cv4.transpose(1, 0, 2).reshape(3, CONV_WIDTH)
    send_sems, recv_sems, wb_out, wb_ff1, wb_ff2, token = _gather_rest_start(
        _place_shard("place_w_out", w_out[0], chip1, False, after=wb_in),
        _place_shard("place_w_ff1", w_ff1[0], chip1, True, after=wb_in),
        _place_shard("place_w_ff2", w_ff2[0], chip1, False, after=wb_in))

    proj, xb = _in_proj(xs, wb_in, token)
    o, states = _hgrn_fwd(proj, lb_logits)
    cat = _gate_fwd(proj, o, gate_norm_w, conv_full)
    wb_out, wb_ff1, wb_ff2 = _gather_rest_wait(send_sems, recv_sems, wb_out, wb_ff1, wb_ff2, cat)
    xhat1, h1b, rstd1 = _out_ln1(cat, wb_out, xs, ln1_g, ln1_b)
    r, dpre2, dpre2b, g_ln2_g, g_ln2_b, loss8 = _mlp_fwd(xhat1, ln1_g, ln1_b, wb_ff1, wb_ff2, ln2_g, ln2_b, tgt)
    loss = lax.psum(loss8[0, 0], ("x", "y", "c"))

    da, dpre1, dpre1b, g_ln1_g, g_ln1_b = _mlp_bwd(dpre2, r, wb_ff1, wb_ff2, xhat1, rstd1, ln1_g)
    dcat = _out_bwd(dpre1b, wb_out)
    do, dpg, g_gnw, g_conv = _gate_bwd(dcat, o, proj, gate_norm_w, conv_full)
    dph, g_lbl = _hgrn_bwd(proj, do, states, lb_logits)
    grad_x = _in_bwd(dph, dpg, wb_in, dpre1)
    grads = _weight_grads(xb, dph, dpg, cat, dpre1b, h1b, da, r, dpre2b)

    recv = _swap_halves(grads)
    names = ("w_in", "w_out", "w_ff1", "w_ff2")
    halves = [_add_half("add_half_" + n, g, rv, core, rs) for n, g, rv, rs in zip(names, grads, recv, (True, False, True, False))]
    slots = _exchange_pieces(halves)
    shards = [_sum_pieces("sum_pieces_" + n, h, s, place, rs)
              for n, h, s, rs in zip(names, halves, slots, (True, False, True, False))]
    g_w_in, g_w_out, g_w_ff1, g_w_ff2 = _join_halves(shards)

    pack = jnp.concatenate([
        g_ln1_g, g_ln1_b, g_ln2_g, g_ln2_b,
        jnp.concatenate([g_lbl[0:1], g_lbl[1:2]], axis=1),
        jnp.concatenate([g_gnw, g_conv[0:1]], axis=1),
        jnp.concatenate([g_conv[1:2], g_conv[2:3]], axis=1),
        jnp.zeros((1, D_MODEL), F32)], axis=0)
    tot = _sum_small(pack)
    half = D_MODEL // 2
    g_lb_logits = jnp.concatenate([tot[4:5, :half], tot[4:5, half:]], axis=0)
    g_gate_norm_w = tot[5:6, :half]
    g_conv_full = jnp.concatenate([tot[5:6, half:], tot[6:7, :half], tot[6:7, half:]], axis=0)
    g_conv_w = lax.dynamic_slice(g_conv_full, (0, chip * LANES), (3, LANES))

    d_in, nm_in, nv_in = _adamw("adamw_w_in", w_in[0], g_w_in, m_w_in[0], v_w_in[0])
    d_out, nm_out, nv_out = _adamw("adamw_w_out", w_out[0], g_w_out, m_w_out[0], v_w_out[0])
    d_ff1, nm_ff1, nv_ff1 = _adamw("adamw_w_ff1", w_ff1[0], g_w_ff1, m_w_ff1[0], v_w_ff1[0])
    d_ff2, nm_ff2, nv_ff2 = _adamw("adamw_w_ff2", w_ff2[0], g_w_ff2, m_w_ff2[0], v_w_ff2[0])

    def small_pack(lbl, gnw, cv, l1g, l1b, l2g, l2b):
        pad = jnp.zeros((1, D_MODEL - 3 * LANES), F32)
        return jnp.concatenate([
            l1g, l1b, l2g, l2b, jnp.concatenate([lbl[0:1], lbl[1:2]], axis=1),
            jnp.concatenate([gnw, jnp.zeros((1, half), F32)], axis=1),
            jnp.concatenate([cv[0:1], cv[1:2], cv[2:3], pad], axis=1), jnp.zeros((1, D_MODEL), F32)], axis=0)

    w_s = small_pack(lb_logits, gate_norm_w, conv_w[0], ln1_g, ln1_b, ln2_g, ln2_b)
    g_s = small_pack(g_lb_logits, g_gate_norm_w, g_conv_w, tot[0:1], tot[1:2], tot[2:3], tot[3:4])
    m_s = small_pack(m_lb_logits, m_gate_norm_w, m_conv_w[0], m_ln1_g, m_ln1_b, m_ln2_g, m_ln2_b)
    v_s = small_pack(v_lb_logits, v_gate_norm_w, v_conv_w[0], v_ln1_g, v_ln1_b, v_ln2_g, v_ln2_b)
    d_s, nm_s, nv_s = _adamw("adamw_small", w_s, g_s, m_s, v_s)

    def unpack(p):
        lbl = jnp.concatenate([p[4:5, :half], p[4:5, half:]], axis=0)
        cv = jnp.concatenate([p[6:7, 0:LANES], p[6:7, LANES:2 * LANES], p[6:7, 2 * LANES:3 * LANES]], axis=0)
        return dict(lb_logits=lbl, gate_norm_w=p[5:6, :half], conv_w=cv[None], ln1_g=p[0:1], ln1_b=p[1:2],
                    ln2_g=p[2:3], ln2_b=p[3:4])

    order = ("w_in", "lb_logits", "gate_norm_w", "conv_w", "w_out", "ln1_g", "ln1_b", "w_ff1", "w_ff2", "ln2_g", "ln2_b")
    grad = dict(unpack(g_s), w_in=g_w_in[None], w_out=g_w_out[None], w_ff1=g_w_ff1[None], w_ff2=g_w_ff2[None])
    delta = dict(unpack(d_s), w_in=d_in[None], w_out=d_out[None], w_ff1=d_ff1[None], w_ff2=d_ff2[None])
    new_m = dict(unpack(nm_s), w_in=nm_in[None], w_out=nm_out[None], w_ff1=nm_ff1[None], w_ff2=nm_ff2[None])
    new_v = dict(unpack(nv_s), w_in=nv_in[None], w_out=nv_out[None], w_ff1=nv_ff1[None], w_ff2=nv_ff2[None])
    return (loss, grad_x[None], *[grad[n] for n in order], *[delta[n] for n in order],
            *[new_m[n] for n in order], *[new_v[n] for n in order])
```

```python
import jax
import jax.numpy as jnp
from jax import lax
from jax.experimental import pallas as pl
from jax.experimental.pallas import tpu as pltpu

F32 = jnp.float32
BF16 = jnp.bfloat16
MXU_DTYPE = jnp.bfloat16

D_MODEL = 1024
HGRN_WIDTH = 512
HEAD_DIM = 128
N_HEADS = 4
CONV_WIDTH = 512
CHUNK = 64
D_FF = 4096
IN_COLS = 3584
GROUP = 512
N_GROUPS = IN_COLS // GROUP
ALPHA = 2.0 ** 0.25
EPS = 1e-5
N_CHIPS = 4
ADAM_LR, ADAM_B1, ADAM_B2, ADAM_EPS, ADAM_WD, ADAM_STEP = 0.001, 0.9, 0.999, 1e-08, 0.01, 10

LANES = 128
SUBLANES = 8
VMEM_LIMIT = 56 * 1024 * 1024
FF_BLOCK = 1024

NN = (((1,), (0,)), ((), ()))
NT = (((1,), (1,)), ((), ()))
TN = (((0,), (0,)), ((), ()))
MESH = pl.DeviceIdType.MESH
ANY = pl.BlockSpec(memory_space=pl.ANY)


def _dot(a, b, dims):
    return lax.dot_general(a.astype(MXU_DTYPE), b.astype(MXU_DTYPE), dims, preferred_element_type=F32)


def _dot_exact(a, b):
    return lax.dot_general(a, b, NN, precision=lax.Precision.HIGHEST, preferred_element_type=F32)


def _params(*sem):
    return pltpu.CompilerParams(dimension_semantics=sem, vmem_limit_bytes=VMEM_LIMIT)


def _resident(shape):
    return pl.BlockSpec(shape, lambda *_: (0,) * len(shape), pipeline_mode=pl.Buffered(1))


def _sigmoid(v):
    return 1.0 / (1.0 + jnp.exp(-v))


def _lower_bound(lbl):
    m = jnp.max(lbl, axis=0, keepdims=True)
    e = jnp.exp(lbl - m)
    s = e / jnp.sum(e, axis=0, keepdims=True)
    return s[0:1, :], s[1:2, :]


def _heads(v):
    return [v[:, h * HEAD_DIM:(h + 1) * HEAD_DIM] for h in range(N_HEADS)]


def _in_proj(x, w_in, after):
    t = x.shape[0]
    tm = min(t, 512)

    def body(x_ref, w_ref, after_ref, o_ref, xb_ref):
        xb = x_ref[...].astype(xb_ref.dtype)
        xb_ref[...] = xb
        for g in range(N_GROUPS):
            cols = slice(g * GROUP, (g + 1) * GROUP)
            o_ref[:, cols] = _dot(xb, w_ref[:, cols], NN)

    return pl.pallas_call(
        body, name="in_proj", grid=(t // tm,),
        in_specs=[pl.BlockSpec((tm, D_MODEL), lambda i: (i, 0)), _resident((D_MODEL, IN_COLS)), ANY],
        out_specs=[pl.BlockSpec((tm, IN_COLS), lambda i: (i, 0)), pl.BlockSpec((tm, D_MODEL), lambda i: (i, 0))],
        out_shape=[jax.ShapeDtypeStruct((t, IN_COLS), F32), jax.ShapeDtypeStruct((t, D_MODEL), BF16)],
        compiler_params=_params("parallel"),
    )(x, w_in, after)


def _gates(fp, lb):
    sig = _sigmoid(fp)
    f = lb + (1.0 - lb) * sig
    return sig, f, jnp.log(f), 1.0 - f


def _chunk_masks():
    row = lax.broadcasted_iota(jnp.int32, (CHUNK, CHUNK), 0)
    col = lax.broadcasted_iota(jnp.int32, (CHUNK, CHUNK), 1)
    return row >= col, row <= col


def _hgrn_fwd(proj, lb_logits):
    t = proj.shape[0]
    tb = min(t, 512)
    ncb = tb // CHUNK

    def body(q_ref, f_ref, v_ref, lbl_ref, o_ref, st_ref, s_scr):
        @pl.when(pl.program_id(0) == 0)
        def _():
            s_scr[...] = jnp.zeros_like(s_scr)

        lb, _ = _lower_bound(lbl_ref[...])
        causal, _ = _chunk_masks()
        tri = causal.astype(F32)

        def chunk(c, carry):
            rows = pl.ds(pl.multiple_of(c * CHUNK, CHUNK), CHUNK)
            q, v = q_ref[rows, :], v_ref[rows, :]
            _, _, g, k = _gates(f_ref[rows, :], lb)
            b = _dot_exact(tri, g)
            b_ref, b_last = b[CHUNK // 2:CHUNK // 2 + 1, :], b[CHUNK - 1:CHUNK, :]
            qt, kt, qi, ks = q * jnp.exp(b - b_ref), k * jnp.exp(b_ref - b), q * jnp.exp(b), k * jnp.exp(b_last - b)
            dec = jnp.exp(b_last)
            st = s_scr[...]
            st_ref[c] = st
            outs, news = [], []
            for qt_h, kt_h, qi_h, ks_h, v_h, st_h, dec_h in zip(*map(_heads, (qt, kt, qi, ks, v, st, dec))):
                scores = jnp.where(causal, _dot(qt_h, kt_h, NT), 0.0)
                outs.append(_dot(scores, v_h, NN) + _dot(qi_h, st_h, NT))
                news.append(dec_h * st_h + _dot(v_h, ks_h, TN))
            o_ref[rows, :] = jnp.concatenate(outs, axis=1)
            s_scr[...] = jnp.concatenate(news, axis=1)
            return carry

        lax.fori_loop(0, ncb, chunk, 0)

    grp = lambda g: pl.BlockSpec((tb, HGRN_WIDTH), lambda i: (i, g))
    return pl.pallas_call(
        body, name="hgrn_fwd", grid=(t // tb,),
        in_specs=[grp(0), grp(1), grp(2), pl.BlockSpec((2, HGRN_WIDTH), lambda i: (0, 0))],
        out_specs=[grp(0), pl.BlockSpec((ncb, HEAD_DIM, HGRN_WIDTH), lambda i: (i, 0, 0))],
        out_shape=[jax.ShapeDtypeStruct((t, HGRN_WIDTH), F32),
                   jax.ShapeDtypeStruct((t // CHUNK, HEAD_DIM, HGRN_WIDTH), F32)],
        scratch_shapes=[pltpu.VMEM((HEAD_DIM, HGRN_WIDTH), F32)],
        compiler_params=_params("arbitrary"),
    )(proj, proj, proj, lb_logits)


def _conv_taps(z, halo, zbuf, tb):
    zbuf[0:SUBLANES, :] = halo
    zbuf[SUBLANES:SUBLANES + tb, :] = z
    return zbuf[SUBLANES - 1:SUBLANES - 1 + tb, :], zbuf[SUBLANES - 2:SUBLANES - 2 + tb, :]


def _gate_fwd(proj, o, gate_norm_w, conv_w):
    t = proj.shape[0]
    tb = min(t, 512)
    hb = tb // SUBLANES

    def body(o_ref, og_ref, gnw_ref, b_ref, c_ref, u_ref, ch_ref, uh_ref, cw_ref, cat_ref, zbuf):
        i = pl.program_id(1)
        ov, og = o_ref[...], og_ref[...]
        on = ov * lax.rsqrt(jnp.mean(ov * ov, axis=-1, keepdims=True) + EPS)
        cat_ref[0] = (on * gnw_ref[...] * (og * _sigmoid(og))).astype(cat_ref.dtype)
        z = c_ref[...] * u_ref[...]
        halo = jnp.where(i > 0, ch_ref[...] * uh_ref[...], 0.0)
        z1, z2 = _conv_taps(z, halo, zbuf, tb)
        cw = cw_ref[...]
        yc = cw[2:3, :] * z + cw[1:2, :] * z1 + cw[0:1, :] * z2
        cat_ref[1] = (b_ref[...] * yc).astype(cat_ref.dtype)

    blk = lambda off: pl.BlockSpec((tb, LANES), lambda j, i: (i, off + j))
    prev = lambda off: pl.BlockSpec((SUBLANES, LANES), lambda j, i: (jnp.maximum(i * hb - 1, 0), off + j))
    vec = lambda r: pl.BlockSpec((r, LANES), lambda j, i: (0, j))
    return pl.pallas_call(
        body, name="gate_fwd", grid=(4, t // tb),
        in_specs=[blk(0), blk(12), vec(1), blk(16), blk(20), blk(24), prev(20), prev(24), vec(3)],
        out_specs=pl.BlockSpec((2, tb, LANES), lambda j, i: (0, i, j)),
        out_shape=jax.ShapeDtypeStruct((2, t, HGRN_WIDTH), BF16),
        scratch_shapes=[pltpu.VMEM((tb + SUBLANES, LANES), F32)],
        compiler_params=_params("parallel", "arbitrary"),
    )(o, proj, gate_norm_w, proj, proj, proj, proj, proj, conv_w)


def _out_ln1(cat, w_out, x, g1, b1):
    t = x.shape[0]
    tm = min(t, 512)

    def body(cat_ref, w_ref, x_ref, g_ref, b_ref, xhat_ref, h1_ref, rstd_ref):
        mix = _dot(cat_ref[0], w_ref[0:GROUP, :], NN) + _dot(cat_ref[1], w_ref[GROUP:2 * GROUP, :], NN)
        pre = ALPHA * x_ref[...] + mix
        xc = pre - jnp.mean(pre, axis=-1, keepdims=True)
        rstd = lax.rsqrt(jnp.mean(xc * xc, axis=-1, keepdims=True) + EPS)
        xhat = xc * rstd
        xhat_ref[...] = xhat
        h1_ref[...] = (xhat * g_ref[...] + b_ref[...]).astype(h1_ref.dtype)
        rstd_ref[...] = rstd

    row = pl.BlockSpec((tm, D_MODEL), lambda i: (i, 0))
    vec = pl.BlockSpec((1, D_MODEL), lambda i: (0, 0))
    return pl.pallas_call(
        body, name="out_ln1", grid=(t // tm,),
        in_specs=[pl.BlockSpec((2, tm, GROUP), lambda i: (0, i, 0)), _resident((D_MODEL, D_MODEL)), row, vec, vec],
        out_specs=[row, row, pl.BlockSpec((tm, 1), lambda i: (i, 0))],
        out_shape=[jax.ShapeDtypeStruct((t, D_MODEL), F32), jax.ShapeDtypeStruct((t, D_MODEL), BF16),
                   jax.ShapeDtypeStruct((t, 1), F32)],
        compiler_params=_params("parallel"),
    )(cat, w_out, x, g1, b1)


def _ln_bwd(dy, xhat, rstd, g):
    dxhat = dy * g
    m1 = jnp.mean(dxhat, axis=-1, keepdims=True)
    m2 = jnp.mean(dxhat * xhat, axis=-1, keepdims=True)
    return rstd * (dxhat - m1 - xhat * m2)


def _mlp_fwd(xhat1, g1, b1, w_ff1, w_ff2, g2, b2, target):
    t = xhat1.shape[0]
    tm = min(t, 256)

    def body(xh_ref, g1_ref, b1_ref, w1_ref, w2_ref, g2_ref, b2_ref, tg_ref,
             r_ref, dpre_ref, dpreb_ref, dg_ref, db_ref, loss_ref):
        @pl.when(pl.program_id(0) == 0)
        def _():
            dg_ref[...] = jnp.zeros_like(dg_ref)
            db_ref[...] = jnp.zeros_like(db_ref)
            loss_ref[...] = jnp.zeros_like(loss_ref)

        h1 = xh_ref[...] * g1_ref[...] + b1_ref[...]
        h1b = h1.astype(MXU_DTYPE)
        mlp = jnp.zeros((tm, D_MODEL), F32)
        for j in range(D_FF // FF_BLOCK):
            cols = slice(j * FF_BLOCK, (j + 1) * FF_BLOCK)
            r = jnp.square(jnp.maximum(_dot(h1b, w1_ref[:, cols], NN), 0.0)).astype(r_ref.dtype)
            r_ref[:, cols] = r
            mlp = mlp + _dot(r, w2_ref[cols, :], NN)
        pre = ALPHA * h1 + mlp
        xc = pre - jnp.mean(pre, axis=-1, keepdims=True)
        rstd = lax.rsqrt(jnp.mean(xc * xc, axis=-1, keepdims=True) + EPS)
        xhat = xc * rstd
        err = xhat * g2_ref[...] + b2_ref[...] - tg_ref[...]
        loss_ref[...] += 0.5 * jnp.sum(jnp.mean(err * err, axis=-1, keepdims=True))
        dy = err * (1.0 / D_MODEL)
        dg_ref[...] += jnp.sum(dy * xhat, axis=0, keepdims=True)
        db_ref[...] += jnp.sum(dy, axis=0, keepdims=True)
        dpre = _ln_bwd(dy, xhat, rstd, g2_ref[...])
        dpre_ref[...] = dpre
        dpreb_ref[...] = dpre.astype(dpreb_ref.dtype)

    row = pl.BlockSpec((tm, D_MODEL), lambda i: (i, 0))
    vec = pl.BlockSpec((1, D_MODEL), lambda i: (0, 0))
    return pl.pallas_call(
        body, name="mlp_fwd", grid=(t // tm,),
        in_specs=[row, vec, vec, _resident((D_MODEL, D_FF)), _resident((D_FF, D_MODEL)), vec, vec, row],
        out_specs=[pl.BlockSpec((tm, D_FF), lambda i: (i, 0)), row, row, vec, vec,
                   pl.BlockSpec((SUBLANES, LANES), lambda i: (0, 0))],
        out_shape=[jax.ShapeDtypeStruct((t, D_FF), BF16), jax.ShapeDtypeStruct((t, D_MODEL), F32),
                   jax.ShapeDtypeStruct((t, D_MODEL), BF16), jax.ShapeDtypeStruct((1, D_MODEL), F32),
                   jax.ShapeDtypeStruct((1, D_MODEL), F32), jax.ShapeDtypeStruct((SUBLANES, LANES), F32)],
        compiler_params=_params("arbitrary"),
    )(xhat1, g1, b1, w_ff1, w_ff2, g2, b2, target)


def _mlp_bwd(dpre2, r, w_ff1, w_ff2, xhat1, rstd1, g1):
    t = r.shape[0]
    tm = min(t, 256)

    def body(dp2_ref, r_ref, w1_ref, w2_ref, xh_ref, rs_ref, g_ref, da_ref, dpre_ref, dpreb_ref, dg_ref, db_ref):
        @pl.when(pl.program_id(0) == 0)
        def _():
            dg_ref[...] = jnp.zeros_like(dg_ref)
            db_ref[...] = jnp.zeros_like(db_ref)

        dp2 = dp2_ref[...]
        dp2b = dp2.astype(MXU_DTYPE)
        back = jnp.zeros((tm, D_MODEL), F32)
        for j in range(D_FF // FF_BLOCK):
            cols = slice(j * FF_BLOCK, (j + 1) * FF_BLOCK)
            dr = _dot(dp2b, w2_ref[cols, :], NT)
            da = (dr * (2.0 * jnp.sqrt(r_ref[:, cols].astype(F32)))).astype(da_ref.dtype)
            da_ref[:, cols] = da
            back = back + _dot(da, w1_ref[:, cols], NT)
        dh1 = ALPHA * dp2 + back
        xhat = xh_ref[...]
        dg_ref[...] += jnp.sum(dh1 * xhat, axis=0, keepdims=True)
        db_ref[...] += jnp.sum(dh1, axis=0, keepdims=True)
        dpre = _ln_bwd(dh1, xhat, rs_ref[...], g_ref[...])
        dpre_ref[...] = dpre
        dpreb_ref[...] = dpre.astype(dpreb_ref.dtype)

    row = pl.BlockSpec((tm, D_MODEL), lambda i: (i, 0))
    wide = pl.BlockSpec((tm, D_FF), lambda i: (i, 0))
    vec = pl.BlockSpec((1, D_MODEL), lambda i: (0, 0))
    return pl.pallas_call(
        body, name="mlp_bwd", grid=(t // tm,),
        in_specs=[row, wide, _resident((D_MODEL, D_FF)), _resident((D_FF, D_MODEL)), row,
                  pl.BlockSpec((tm, 1), lambda i: (i, 0)), vec],
        out_specs=[wide, row, row, vec, vec],
        out_shape=[jax.ShapeDtypeStruct((t, D_FF), BF16), jax.ShapeDtypeStruct((t, D_MODEL), F32),
                   jax.ShapeDtypeStruct((t, D_MODEL), BF16), jax.ShapeDtypeStruct((1, D_MODEL), F32),
                   jax.ShapeDtypeStruct((1, D_MODEL), F32)],
        compiler_params=_params("arbitrary"),
    )(dpre2, r, w_ff1, w_ff2, xhat1, rstd1, g1)


def _out_bwd(dpre1b, w_out):
    t = dpre1b.shape[0]
    tm = min(t, 512)

    def body(d_ref, w_ref, o_ref):
        o_ref[...] = _dot(d_ref[...], w_ref[...], NT)

    return pl.pallas_call(
        body, name="out_bwd", grid=(t // tm,),
        in_specs=[pl.BlockSpec((tm, D_MODEL), lambda i: (i, 0)), _resident((D_MODEL, D_MODEL))],
        out_specs=pl.BlockSpec((tm, D_MODEL), lambda i: (i, 0)),
        out_shape=jax.ShapeDtypeStruct((t, D_MODEL), F32),
        compiler_params=_params("parallel"),
    )(dpre1b, w_out)


def _gate_bwd(dcat, o, proj, gate_norm_w, conv_w, after):
    t = proj.shape[0]
    tb = min(t, 512)
    hb = tb // SUBLANES
    nblk = t // tb

    def body(do2_ref, dy_ref, dyn_ref, o_ref, og_ref, gnw_ref, b_ref, bn_ref, c_ref, u_ref, ch_ref, uh_ref, cw_ref,
             after_ref, do_ref, dp_ref, dgnw_ref, dcw_ref, zbuf, dbuf):
        i = pl.program_id(1)

        @pl.when(i == 0)
        def _():
            dgnw_ref[...] = jnp.zeros_like(dgnw_ref)
            dcw_ref[...] = jnp.zeros_like(dcw_ref)

        ov, og, gnw, do2 = o_ref[...], og_ref[...], gnw_ref[...], do2_ref[...]
        rs = lax.rsqrt(jnp.mean(ov * ov, axis=-1, keepdims=True) + EPS)
        on = ov * rs
        sg = _sigmoid(og)
        sil = og * sg
        don = do2 * gnw * sil
        dgnw_ref[...] += jnp.sum(do2 * on * sil, axis=0, keepdims=True)
        dp_ref[0] = (do2 * on * gnw * (sg * (1.0 + og * (1.0 - sg)))).astype(dp_ref.dtype)
        do_ref[...] = rs * (don - on * jnp.mean(don * on, axis=-1, keepdims=True))

        bg, cg, u, dy = b_ref[...], c_ref[...], u_ref[...], dy_ref[...]
        z = cg * u
        halo = jnp.where(i > 0, ch_ref[...] * uh_ref[...], 0.0)
        z1, z2 = _conv_taps(z, halo, zbuf, tb)
        cw = cw_ref[...]
        yc = cw[2:3, :] * z + cw[1:2, :] * z1 + cw[0:1, :] * z2
        dyc = dy * bg
        dbuf[0:tb, :] = dyc
        dbuf[tb:tb + SUBLANES, :] = jnp.where(i < nblk - 1, dyn_ref[...] * bn_ref[...], 0.0)
        d1, d2 = dbuf[1:1 + tb, :], dbuf[2:2 + tb, :]
        dz = cw[2:3, :] * dyc + cw[1:2, :] * d1 + cw[0:1, :] * d2
        dp_ref[1] = (dy * yc).astype(dp_ref.dtype)
        dp_ref[2] = (dz * u).astype(dp_ref.dtype)
        dp_ref[3] = (dz * cg).astype(dp_ref.dtype)
        dcw_ref[0:1, :] += jnp.sum(dyc * z2, axis=0, keepdims=True)
        dcw_ref[1:2, :] += jnp.sum(dyc * z1, axis=0, keepdims=True)
        dcw_ref[2:3, :] += jnp.sum(dyc * z, axis=0, keepdims=True)

    blk = lambda off: pl.BlockSpec((tb, LANES), lambda j, i: (i, off + j))
    prev = lambda off: pl.BlockSpec((SUBLANES, LANES), lambda j, i: (jnp.maximum(i * hb - 1, 0), off + j))
    nxt = lambda off: pl.BlockSpec((SUBLANES, LANES), lambda j, i: (jnp.minimum((i + 1) * hb, t // SUBLANES - 1), off + j))
    vec = lambda r: pl.BlockSpec((r, LANES), lambda j, i: (0, j))
    return pl.pallas_call(
        body, name="gate_bwd", grid=(4, nblk),
        in_specs=[blk(0), blk(4), nxt(4), blk(0), blk(12), vec(1), blk(16), nxt(16), blk(20), blk(24), prev(20), prev(24),
                  vec(3), ANY],
        out_specs=[blk(0), pl.BlockSpec((4, tb, LANES), lambda j, i: (0, i, j)), vec(1), vec(3)],
        out_shape=[jax.ShapeDtypeStruct((t, HGRN_WIDTH), F32), jax.ShapeDtypeStruct((4, t, HGRN_WIDTH), BF16),
                   jax.ShapeDtypeStruct((1, HGRN_WIDTH), F32), jax.ShapeDtypeStruct((3, CONV_WIDTH), F32)],
        scratch_shapes=[pltpu.VMEM((tb + SUBLANES, LANES), F32), pltpu.VMEM((tb + SUBLANES, LANES), F32)],
        compiler_params=_params("parallel", "arbitrary"),
    )(dcat, dcat, dcat, o, proj, gate_norm_w, proj, proj, proj, proj, proj, proj, conv_w, after)


def _hgrn_bwd(proj, do, states, lb_logits, after):
    t = proj.shape[0]
    tb = min(t, 512)
    ncb = tb // CHUNK
    nblk = t // tb

    def body(q_ref, f_ref, v_ref, do_ref, st_ref, lbl_ref, after_ref, dp_ref, dlbl_ref, ds_scr, dlb_scr):
        i = pl.program_id(0)

        @pl.when(i == 0)
        def _():
            ds_scr[...] = jnp.zeros_like(ds_scr)
            dlb_scr[...] = jnp.zeros_like(dlb_scr)

        lb, s1 = _lower_bound(lbl_ref[...])
        causal, anti = _chunk_masks()
        tri, tri_rev = causal.astype(F32), anti.astype(F32)

        def chunk(n, carry):
            c = ncb - 1 - n
            rows = pl.ds(pl.multiple_of(c * CHUNK, CHUNK), CHUNK)
            q, v, do_c = q_ref[rows, :], v_ref[rows, :], do_ref[rows, :]
            sig, f, g, k = _gates(f_ref[rows, :], lb)
            b = _dot_exact(tri, g)
            b_ref, b_last = b[CHUNK // 2:CHUNK // 2 + 1, :], b[CHUNK - 1:CHUNK, :]
            e_q, e_k, e_i, e_s = jnp.exp(b - b_ref), jnp.exp(b_ref - b), jnp.exp(b), jnp.exp(b_last - b)
            dec = jnp.exp(b_last)
            qt, kt, qi, ks = q * e_q, k * e_k, q * e_i, k * e_s
            st, dst = st_ref[c], ds_scr[...]

            dqt, dkt, dv, dqi, dks, new_ds = [], [], [], [], [], []
            for qt_h, kt_h, qi_h, ks_h, v_h, do_h, st_h, dst_h, dec_h in zip(
                    *map(_heads, (qt, kt, qi, ks, v, do_c, st, dst, dec))):
                scores = jnp.where(causal, _dot(qt_h, kt_h, NT), 0.0)
                dscores = jnp.where(causal, _dot(do_h, v_h, NT), 0.0)
                dqt.append(_dot(dscores, kt_h, NN))
                dkt.append(_dot(dscores, qt_h, TN))
                dv.append(_dot(scores, do_h, TN) + _dot(ks_h, dst_h, NT))
                dqi.append(_dot(do_h, st_h, NN))
                dks.append(_dot(v_h, dst_h, NN))
                new_ds.append(dec_h * dst_h + _dot(do_h, qi_h, TN))
            dqt, dkt, dv, dqi, dks = (jnp.concatenate(p, axis=1) for p in (dqt, dkt, dv, dqi, dks))
            ddec = jnp.sum(dst * st, axis=0, keepdims=True)
            ds_scr[...] = jnp.concatenate(new_ds, axis=1)

            dq = dqt * e_q + dqi * e_i
            dk = dkt * e_k + dks * e_s
            db = q * dq - k * dk
            db_last = jnp.sum(dks * ks, axis=0, keepdims=True) + ddec * dec
            dg = _dot_exact(tri_rev, db) + db_last
            df = dg / f - dk
            dlb_scr[...] += jnp.sum(df * (1.0 - sig), axis=0, keepdims=True)
            dp_ref[0, rows, :] = dq.astype(dp_ref.dtype)
            dp_ref[1, rows, :] = (df * (1.0 - lb) * sig * (1.0 - sig)).astype(dp_ref.dtype)
            dp_ref[2, rows, :] = dv.astype(dp_ref.dtype)
            return carry

        lax.fori_loop(0, ncb, chunk, 0)

        @pl.when(i == nblk - 1)
        def _():
            dlb = dlb_scr[...]
            dlbl_ref[0:1, :] = dlb * lb * (1.0 - lb)
            dlbl_ref[1:2, :] = -dlb * lb * s1

    grp = lambda g: pl.BlockSpec((tb, HGRN_WIDTH), lambda i: (nblk - 1 - i, g))
    vec = pl.BlockSpec((2, HGRN_WIDTH), lambda i: (0, 0))
    return pl.pallas_call(
        body, name="hgrn_bwd", grid=(nblk,),
        in_specs=[grp(0), grp(1), grp(2), grp(0),
                  pl.BlockSpec((ncb, HEAD_DIM, HGRN_WIDTH), lambda i: (nblk - 1 - i, 0, 0)), vec, ANY],
        out_specs=[pl.BlockSpec((3, tb, HGRN_WIDTH), lambda i: (0, nblk - 1 - i, 0)), vec],
        out_shape=[jax.ShapeDtypeStruct((3, t, HGRN_WIDTH), BF16), jax.ShapeDtypeStruct((2, HGRN_WIDTH), F32)],
        scratch_shapes=[pltpu.VMEM((HEAD_DIM, HGRN_WIDTH), F32), pltpu.VMEM((1, HGRN_WIDTH), F32)],
        compiler_params=_params("arbitrary"),
    )(proj, proj, proj, do, states, lb_logits, after)


def _in_bwd(dph, dpg, w_in, dpre1):
    t = dpre1.shape[0]
    tm = min(t, 512)

    def body(dh_ref, dg_ref, w_ref, dp_ref, o_ref):
        acc = ALPHA * dp_ref[...]
        for g in range(N_GROUPS):
            part = dh_ref[g] if g < 3 else dg_ref[g - 3]
            acc = acc + _dot(part, w_ref[:, g * GROUP:(g + 1) * GROUP], NT)
        o_ref[...] = acc

    row = pl.BlockSpec((tm, D_MODEL), lambda i: (i, 0))
    return pl.pallas_call(
        body, name="in_bwd", grid=(t // tm,),
        in_specs=[pl.BlockSpec((3, tm, GROUP), lambda i: (0, i, 0)), pl.BlockSpec((4, tm, GROUP), lambda i: (0, i, 0)),
                  _resident((D_MODEL, IN_COLS)), row],
        out_specs=row,
        out_shape=jax.ShapeDtypeStruct((t, D_MODEL), F32),
        compiler_params=_params("parallel"),
    )(dph, dpg, w_in, dpre1)


def _grad_w(name, operands, widths, shape, step, after=None):
    t = operands[0].shape[-2]
    tt = min(t, 512)
    n_in, n_steps = len(operands), t // tt
    in_specs = [pl.BlockSpec((tt, w), lambda k: (k, 0)) if a.ndim == 2 else
                pl.BlockSpec((a.shape[0], tt, w), lambda k: (0, k, 0)) for a, w in zip(operands, widths)]
    extra = [] if after is None else [after]

    def body(*refs):
        o_ref, acc, sem = refs[-3:]
        k = pl.program_id(0)

        @pl.when(k == 0)
        def _():
            acc[...] = jnp.zeros_like(acc)

        step(acc, *refs[:n_in])

        @pl.when(k == n_steps - 1)
        def _():
            out = pltpu.make_async_copy(acc, o_ref, sem)
            out.start()
            out.wait()

    return pl.pallas_call(
        body, name=name, grid=(n_steps,), in_specs=in_specs + [ANY] * len(extra), out_specs=ANY,
        out_shape=jax.ShapeDtypeStruct(shape, F32),
        scratch_shapes=[pltpu.VMEM(shape, F32), pltpu.SemaphoreType.DMA],
        compiler_params=_params("arbitrary"),
    )(*operands, *extra)


def _dw_in(xb, dph, dpg, after):
    def step(acc, x_ref, dh_ref, dg_ref):
        xv = x_ref[...]
        for g in range(N_GROUPS):
            part = dh_ref[g] if g < 3 else dg_ref[g - 3]
            acc[:, g * GROUP:(g + 1) * GROUP] += _dot(xv, part, TN)

    return _grad_w("dw_in", (xb, dph, dpg), (D_MODEL, GROUP, GROUP), (D_MODEL, IN_COLS), step, after)


def _dw_out(cat, dpre1b):
    def step(acc, cat_ref, d_ref):
        dv = d_ref[...]
        for g in range(2):
            acc[g * GROUP:(g + 1) * GROUP, :] += _dot(cat_ref[g], dv, TN)

    return _grad_w("dw_out", (cat, dpre1b), (GROUP, D_MODEL), (D_MODEL, D_MODEL), step)


def _dw_ff1(h1b, da):
    def step(acc, h_ref, da_ref):
        hv = h_ref[...]
        for j in range(D_FF // FF_BLOCK):
            cols = slice(j * FF_BLOCK, (j + 1) * FF_BLOCK)
            acc[:, cols] += _dot(hv, da_ref[:, cols], TN)

    return _grad_w("dw_ff1", (h1b, da), (D_MODEL, D_FF), (D_MODEL, D_FF), step)


def _dw_ff2(r, dpre2b):
    def step(acc, r_ref, d_ref):
        dv = d_ref[...]
        for j in range(D_FF // FF_BLOCK):
            rows = slice(j * FF_BLOCK, (j + 1) * FF_BLOCK)
            acc[rows, :] += _dot(r_ref[:, rows], dv, TN)

    return _grad_w("dw_ff2", (r, dpre2b), (D_FF, D_MODEL), (D_FF, D_MODEL), step)


def _place():
    x, y, c = lax.axis_index("x"), lax.axis_index("y"), lax.axis_index("c")
    return x, y, c, 2 * x + y


def _other_chips(x, y):
    return [(1 - x, y), (x, 1 - y), (1 - x, 1 - y)]


def _place_shard(name, w, chip, cols_sharded, after=None):
    rows, cols = w.shape
    tr = min(rows, 256)
    nb = rows // tr
    full = (rows, cols * N_CHIPS) if cols_sharded else (rows * N_CHIPS, cols)
    out_map = (lambda i, s: (i, s[0])) if cols_sharded else (lambda i, s: (s[0] * nb + i, 0))

    def body(s_ref, w_ref, *rest):
        rest[-1][...] = w_ref[...].astype(rest[-1].dtype)

    extra = [] if after is None else [after]
    return pl.pallas_call(
        body, name=name,
        grid_spec=pltpu.PrefetchScalarGridSpec(
            num_scalar_prefetch=1, grid=(nb,),
            in_specs=[pl.BlockSpec((tr, cols), lambda i, s: (i, 0))] + [ANY] * len(extra),
            out_specs=pl.BlockSpec((tr, cols), out_map)),
        out_shape=jax.ShapeDtypeStruct(full, BF16),
        compiler_params=_params("parallel"),
    )(chip, w, *extra)


def _gather_w_in(w_in, conv_w):
    half, cs, n_p = D_MODEL // 2, IN_COLS // N_CHIPS, 3

    def body(w_alias, cv_ref, w_ref, cvf_ref, send_sems, recv_sems, local_sem):
        x, y, c, me = _place()
        sibling = (x, y, 1 - c)
        chips = _other_chips(x, y)
        blk = lambda chip, h: w_ref.at[pl.ds(h * half, half), pl.ds(chip * cs, cs)]

        def copy(k, src, dst, to):
            return pltpu.make_async_remote_copy(src_ref=src, dst_ref=dst, send_sem=send_sems.at[k],
                                                recv_sem=recv_sems.at[k], device_id=to, device_id_type=MESH)

        own_cv = pltpu.make_async_copy(cv_ref, cvf_ref.at[me], local_sem)
        own_cv.start()
        first = [copy(j, blk(me, c), blk(me, c), (px, py, c)) for j, (px, py) in enumerate(chips)]
        first += [copy(2 * n_p + j, cv_ref, cvf_ref.at[me], (px, py, c)) for j, (px, py) in enumerate(chips)]
        for cp in first:
            cp.start()
        passed = []
        for j, (px, py) in enumerate(chips):
            got = blk(2 * px + py, c)
            copy(j, got, got, (px, py, c)).wait_recv()
            passed.append(copy(n_p + j, got, got, sibling))
            passed[-1].start()
        for j, (px, py) in enumerate(chips):
            got = blk(2 * px + py, 1 - c)
            copy(n_p + j, got, got, sibling).wait_recv()
            copy(2 * n_p + j, cv_ref, cvf_ref.at[2 * px + py], (px, py, c)).wait_recv()
        for cp in first + passed:
            cp.wait_send()
        own_cv.wait()

    return pl.pallas_call(
        body, name="gather_w_in", in_specs=[ANY, ANY], out_specs=[ANY, ANY],
        out_shape=[jax.ShapeDtypeStruct(w_in.shape, w_in.dtype), jax.ShapeDtypeStruct((N_CHIPS,) + conv_w.shape, conv_w.dtype)],
        input_output_aliases={0: 0},
        scratch_shapes=[pltpu.SemaphoreType.DMA((3 * n_p,)), pltpu.SemaphoreType.DMA((3 * n_p,)), pltpu.SemaphoreType.DMA],
    )(w_in, conv_w)


HBM = pl.BlockSpec(memory_space=pltpu.HBM)
SEM = pl.BlockSpec(memory_space=pltpu.SEMAPHORE)
EFFECT = pltpu.SideEffectType.DATAFLOW_SIDE_EFFECTING


class _Split:
    def __init__(self, name, arrays, plan):
        n, n_copies = len(arrays), plan.count
        self.name, self.plan, self.n = name, plan, n

        def body(*refs):
            send_sems, recv_sems, token = refs[n], refs[n + 1], refs[-1]
            for k, (src, dst, to) in enumerate(plan(refs[:n])):
                pltpu.make_async_remote_copy(src_ref=src, dst_ref=dst, send_sem=send_sems.at[k], recv_sem=recv_sems.at[k],
                                             device_id=to, device_id_type=MESH).start()
            token[...] = jnp.zeros_like(token)

        outs = pl.pallas_call(
            body, name=name + "_start",
            out_shape=(pltpu.SemaphoreType.DMA((n_copies,)), pltpu.SemaphoreType.DMA((n_copies,)),
                       *[pltpu.HBM(a.shape, a.dtype) for a in arrays], jax.ShapeDtypeStruct((SUBLANES, LANES), F32)),
            in_specs=(HBM,) * n, out_specs=(SEM, SEM) + (HBM,) * n + (pl.BlockSpec(memory_space=pltpu.VMEM),),
            input_output_aliases={i: 2 + i for i in range(n)},
            compiler_params=pltpu.CompilerParams(has_side_effects=EFFECT),
        )(*[pltpu.with_memory_space_constraint(a, pltpu.HBM) for a in arrays])
        self.sems, self.arrays, self.token = outs[:2], outs[2:2 + n], outs[-1]

    def wait(self, after):
        n, plan = self.n, self.plan

        def body(*refs):
            send_sems, recv_sems = refs[n], refs[n + 1]
            for k, (src, dst, to) in enumerate(plan(refs[:n])):
                cp = pltpu.make_async_remote_copy(src_ref=src, dst_ref=dst, send_sem=send_sems.at[k],
                                                  recv_sem=recv_sems.at[k], device_id=to, device_id_type=MESH)
                cp.wait_send()
                cp.wait_recv()

        return pl.pallas_call(
            body, name=self.name + "_wait", out_shape=tuple(pltpu.HBM(a.shape, a.dtype) for a in self.arrays),
            in_specs=(HBM,) * n + (SEM, SEM, ANY), out_specs=(HBM,) * n, input_output_aliases={i: i for i in range(n)},
            compiler_params=pltpu.CompilerParams(has_side_effects=EFFECT),
        )(*self.arrays, *self.sems, after)


COLS_SHARDED = (True, False, True, False)
HALF_SHAPES = [(D_MODEL // 2, IN_COLS), (D_MODEL, D_MODEL // 2), (D_MODEL // 2, D_FF), (D_FF, D_MODEL // 2)]
PIECE_SHAPES = [(D_MODEL // 2, IN_COLS // N_CHIPS), (D_MODEL // N_CHIPS, D_MODEL // 2),
                (D_MODEL // 2, D_FF // N_CHIPS), (D_FF // N_CHIPS, D_MODEL // 2)]


def _shard_view(kind, ref, chip):
    if COLS_SHARDED[kind]:
        n = ref.shape[1] // N_CHIPS
        return ref.at[:, pl.ds(chip * n, n)]
    n = ref.shape[0] // N_CHIPS
    return ref.at[pl.ds(chip * n, n), :]


def _half_view(kind, ref, h):
    if COLS_SHARDED[kind]:
        n = ref.shape[0] // 2
        return ref.at[pl.ds(h * n, n), :]
    n = ref.shape[1] // 2
    return ref.at[:, pl.ds(h * n, n)]


def _plan(count):
    def mark(fn):
        fn.count = count
        return fn
    return mark


def _gather_rest(w_out, w_ff1, w_ff2):
    @_plan(9)
    def plan(refs):
        x, y, c, me = _place()
        mine = [_shard_view(kind, ref, me) for kind, ref in zip((1, 2, 3), refs)]
        return [(v, v, (px, py, c)) for v in mine for px, py in _other_chips(x, y)]

    return _Split("gather_rest", (w_out, w_ff1, w_ff2), plan)


def _swap_halves(kinds, grads):
    @_plan(len(kinds))
    def plan(refs):
        x, y, c, _ = _place()
        return [(_half_view(kind, g, 1 - c), land, (x, y, 1 - c))
                for kind, g, land in zip(kinds, refs[:len(kinds)], refs[len(kinds):])]

    lands = [lax.empty(HALF_SHAPES[kind], F32) for kind in kinds]
    return _Split("swap_halves_" + "".join(map(str, kinds)), (*grads, *lands), plan)


def _add_half(name, g, recv, core, rows_split):
    shape = recv.shape
    tr = min(shape[0], 128 if rows_split else 256)
    nb = shape[0] // tr

    def body(c_ref, g_ref, r_ref, o_ref):
        o_ref[...] = (g_ref[...] + r_ref[...]).astype(o_ref.dtype)

    g_map = (lambda i, c_ref: (c_ref[0] * nb + i, 0)) if rows_split else (lambda i, c_ref: (i, c_ref[0]))
    blk = pl.BlockSpec((tr, shape[1]), lambda i, c_ref: (i, 0))
    return pl.pallas_call(
        body, name=name,
        grid_spec=pltpu.PrefetchScalarGridSpec(
            num_scalar_prefetch=1, grid=(nb,),
            in_specs=[pl.BlockSpec((tr, shape[1]), g_map), blk], out_specs=blk),
        out_shape=jax.ShapeDtypeStruct(shape, BF16),
        compiler_params=_params("parallel"),
    )(core, g, recv)


def _exchange_pieces(kinds, halves):
    n_p = N_CHIPS - 1

    @_plan(n_p * len(kinds))
    def plan(refs):
        x, y, c, _ = _place()
        return [(_shard_view(kind, half, 2 * px + py), land.at[j], (px, py, c))
                for j, (px, py) in enumerate(_other_chips(x, y))
                for kind, half, land in zip(kinds, refs[:len(kinds)], refs[len(kinds):])]

    lands = [lax.empty((n_p,) + PIECE_SHAPES[kind], BF16) for kind in kinds]
    return _Split("exchange_pieces_" + "".join(map(str, kinds)), (*halves, *lands), plan)


def _sum_pieces(name, half, slots, place, rows_split):
    n_p, rows, cols = slots.shape
    tr = min(rows, 256)
    nb = rows // tr
    if rows_split:
        own_map = lambda i, s: (i, s[0])
        out_map = lambda i, s: (s[1] * nb + i, 0)
        shard = (2 * rows, cols)
    else:
        own_map = lambda i, s: (s[0] * nb + i, 0)
        out_map = lambda i, s: (i, s[1])
        shard = (rows, 2 * cols)

    def body(s_ref, own_ref, slot_ref, o_ref):
        total = own_ref[...].astype(F32)
        for j in range(n_p):
            total = total + slot_ref[j].astype(F32)
        o_ref[...] = total

    return pl.pallas_call(
        body, name=name,
        grid_spec=pltpu.PrefetchScalarGridSpec(
            num_scalar_prefetch=1, grid=(nb,),
            in_specs=[pl.BlockSpec((tr, cols), own_map), pl.BlockSpec((n_p, tr, cols), lambda i, s: (0, i, 0))],
            out_specs=pl.BlockSpec((tr, cols), out_map)),
        out_shape=jax.ShapeDtypeStruct(shard, F32),
        compiler_params=_params("parallel"),
    )(place, half, slots)


def _join_halves(kinds, shards):
    @_plan(len(kinds))
    def plan(refs):
        x, y, c, _ = _place()
        return [(_half_view(kind, g, c), _half_view(kind, g, c), (x, y, 1 - c)) for kind, g in zip(kinds, refs)]

    return _Split("join_halves_" + "".join(map(str, kinds)), tuple(shards), plan)


def _sum_small(pack, after):
    n_dev = 8

    def body(p_ref, after_ref, o_ref, slots, send_sems, recv_sems):
        x, y, c, _ = _place()
        me = 4 * x + 2 * y + c
        slots[me] = p_ref[...]
        sends = []
        for m in range(1, n_dev):
            peer = ((1 - x) if m & 4 else x, (1 - y) if m & 2 else y, (1 - c) if m & 1 else c)
            sends.append(pltpu.make_async_remote_copy(
                src_ref=p_ref, dst_ref=slots.at[me], send_sem=send_sems.at[m - 1], recv_sem=recv_sems.at[m - 1],
                device_id=peer, device_id_type=MESH))
        for cp in sends:
            cp.start()
        for m in range(1, n_dev):
            peer = ((1 - x) if m & 4 else x, (1 - y) if m & 2 else y, (1 - c) if m & 1 else c)
            pltpu.make_async_remote_copy(
                src_ref=p_ref, dst_ref=slots.at[4 * peer[0] + 2 * peer[1] + peer[2]], send_sem=send_sems.at[m - 1],
                recv_sem=recv_sems.at[m - 1], device_id=peer, device_id_type=MESH).wait_recv()
        for cp in sends:
            cp.wait_send()
        total = slots[0]
        for d in range(1, n_dev):
            total = total + slots[d]
        o_ref[...] = total

    vm = pl.BlockSpec(memory_space=pltpu.VMEM)
    return pl.pallas_call(
        body, name="sum_small", in_specs=[vm, ANY], out_specs=vm,
        out_shape=jax.ShapeDtypeStruct(pack.shape, F32),
        scratch_shapes=[pltpu.VMEM((n_dev,) + pack.shape, F32), pltpu.SemaphoreType.DMA((n_dev - 1,)),
                        pltpu.SemaphoreType.DMA((n_dev - 1,))],
    )(pack, after)


def _adamw(name, w, g, m, v, after=None):
    rows, cols = w.shape
    tr = min(rows, 256)
    extra = [] if after is None else [after]

    def body(w_ref, g_ref, m_ref, v_ref, *rest):
        d_ref, nm_ref, nv_ref = rest[-3:]
        gv = g_ref[...]
        nm = ADAM_B1 * m_ref[...] + (1.0 - ADAM_B1) * gv
        nv = ADAM_B2 * v_ref[...] + (1.0 - ADAM_B2) * jnp.square(gv)
        m_hat = nm / (1.0 - ADAM_B1 ** ADAM_STEP)
        v_hat = nv / (1.0 - ADAM_B2 ** ADAM_STEP)
        d_ref[...] = -ADAM_LR * (m_hat / (jnp.sqrt(v_hat) + ADAM_EPS) + ADAM_WD * w_ref[...])
        nm_ref[...] = nm
        nv_ref[...] = nv

    blk = pl.BlockSpec((tr, cols), lambda i: (i, 0))
    return pl.pallas_call(
        body, name=name, grid=(rows // tr,), in_specs=[blk] * 4 + [ANY] * len(extra), out_specs=[blk] * 3,
        out_shape=[jax.ShapeDtypeStruct(w.shape, F32)] * 3,
        compiler_params=_params("parallel"),
    )(w, g, m, v, *extra)


def kernel(x, w_in, lb_logits, gate_norm_w, conv_w, w_out, ln1_g, ln1_b, w_ff1, w_ff2, ln2_g, ln2_b, loss_target, m_w_in, m_lb_logits, m_gate_norm_w, m_conv_w, m_w_out, m_ln1_g, m_ln1_b, m_w_ff1, m_w_ff2, m_ln2_g, m_ln2_b, v_w_in, v_lb_logits, v_gate_norm_w, v_conv_w, v_w_out, v_ln1_g, v_ln1_b, v_w_ff1, v_w_ff2, v_ln2_g, v_ln2_b):
    xs, tgt = x[0], loss_target[0]
    chip = 2 * lax.axis_index("x") + lax.axis_index("y")
    core = lax.axis_index("c").astype(jnp.int32).reshape(1)
    chip1 = chip.astype(jnp.int32).reshape(1)
    place = jnp.concatenate([chip1, core])

    wb_in, cv4 = _gather_w_in(_place_shard("place_w_in", w_in[0], chip1, True), conv_w[0])
    conv_full = cv4.transpose(1, 0, 2).reshape(3, CONV_WIDTH)
    rest = _gather_rest(
        _place_shard("place_w_out", w_out[0], chip1, False, after=wb_in),
        _place_shard("place_w_ff1", w_ff1[0], chip1, True, after=wb_in),
        _place_shard("place_w_ff2", w_ff2[0], chip1, False, after=wb_in))

    proj, xb = _in_proj(xs, wb_in, rest.token)
    o, states = _hgrn_fwd(proj, lb_logits)
    cat = _gate_fwd(proj, o, gate_norm_w, conv_full)
    wb_out, wb_ff1, wb_ff2 = rest.wait(cat)
    xhat1, h1b, rstd1 = _out_ln1(cat, wb_out, xs, ln1_g, ln1_b)
    r, dpre2, dpre2b, g_ln2_g, g_ln2_b, loss8 = _mlp_fwd(xhat1, ln1_g, ln1_b, wb_ff1, wb_ff2, ln2_g, ln2_b, tgt)

    names = ("w_in", "w_out", "w_ff1", "w_ff2")

    def add_halves(kinds, grads, lands):
        return [_add_half("add_half_" + names[k], g, ld, core, COLS_SHARDED[k]) for k, g, ld in zip(kinds, grads, lands)]

    def sum_pieces(kinds, halves, lands):
        return [_sum_pieces("sum_pieces_" + names[k], h, ld, place, COLS_SHARDED[k]) for k, h, ld in zip(kinds, halves, lands)]

    da, dpre1, dpre1b, g_ln1_g, g_ln1_b = _mlp_bwd(dpre2, r, wb_ff1, wb_ff2, xhat1, rstd1, ln1_g)
    dcat = _out_bwd(dpre1b, wb_out)
    early = (1, 2, 3)
    swap = _swap_halves(early, (_dw_out(cat, dpre1b), _dw_ff1(h1b, da), _dw_ff2(r, dpre2b)))
    do, dpg, g_gnw, g_conv = _gate_bwd(dcat, o, proj, gate_norm_w, conv_full, swap.token)
    swapped = swap.wait(do)
    exch = _exchange_pieces(early, add_halves(early, swapped[:3], swapped[3:]))
    dph, g_lbl = _hgrn_bwd(proj, do, states, lb_logits, exch.token)
    grad_x = _in_bwd(dph, dpg, wb_in, dpre1)
    exchanged = exch.wait(grad_x)
    join = _join_halves(early, sum_pieces(early, exchanged[:3], exchanged[3:]))
    g_in_local = _dw_in(xb, dph, dpg, join.token)
    g_w_out, g_w_ff1, g_w_ff2 = join.wait(g_in_local)

    late = (0,)
    swap = _swap_halves(late, (g_in_local,))
    d_ff1, nm_ff1, nv_ff1 = _adamw("adamw_w_ff1", w_ff1[0], g_w_ff1, m_w_ff1[0], v_w_ff1[0], swap.token)
    swapped = swap.wait(d_ff1)
    exch = _exchange_pieces(late, add_halves(late, swapped[:1], swapped[1:]))
    d_ff2, nm_ff2, nv_ff2 = _adamw("adamw_w_ff2", w_ff2[0], g_w_ff2, m_w_ff2[0], v_w_ff2[0], exch.token)
    d_out, nm_out, nv_out = _adamw("adamw_w_out", w_out[0], g_w_out, m_w_out[0], v_w_out[0], d_ff2)

    pack = jnp.concatenate([
        g_ln1_g, g_ln1_b, g_ln2_g, g_ln2_b,
        jnp.concatenate([g_lbl[0:1], g_lbl[1:2]], axis=1),
        jnp.concatenate([g_gnw, g_conv[0:1]], axis=1),
        jnp.concatenate([g_conv[1:2], g_conv[2:3]], axis=1),
        jnp.concatenate([loss8[0:1], jnp.zeros((1, D_MODEL - LANES), F32)], axis=1)], axis=0)
    tot = _sum_small(pack, d_out)
    loss = tot[7, 0]
    half = D_MODEL // 2
    g_lb_logits = jnp.concatenate([tot[4:5, :half], tot[4:5, half:]], axis=0)
    g_gate_norm_w = tot[5:6, :half]
    g_conv_full = jnp.concatenate([tot[5:6, half:], tot[6:7, :half], tot[6:7, half:]], axis=0)
    g_conv_w = lax.dynamic_slice(g_conv_full, (0, chip * LANES), (3, LANES))

    exchanged = exch.wait(tot)
    join = _join_halves(late, sum_pieces(late, exchanged[:1], exchanged[1:]))

    def small_pack(lbl, gnw, cv, l1g, l1b, l2g, l2b):
        pad = jnp.zeros((1, D_MODEL - 3 * LANES), F32)
        return jnp.concatenate([
            l1g, l1b, l2g, l2b, jnp.concatenate([lbl[0:1], lbl[1:2]], axis=1),
            jnp.concatenate([gnw, jnp.zeros((1, half), F32)], axis=1),
            jnp.concatenate([cv[0:1], cv[1:2], cv[2:3], pad], axis=1), jnp.zeros((1, D_MODEL), F32)], axis=0)

    w_s = small_pack(lb_logits, gate_norm_w, conv_w[0], ln1_g, ln1_b, ln2_g, ln2_b)
    g_s = small_pack(g_lb_logits, g_gate_norm_w, g_conv_w, tot[0:1], tot[1:2], tot[2:3], tot[3:4])
    m_s = small_pack(m_lb_logits, m_gate_norm_w, m_conv_w[0], m_ln1_g, m_ln1_b, m_ln2_g, m_ln2_b)
    v_s = small_pack(v_lb_logits, v_gate_norm_w, v_conv_w[0], v_ln1_g, v_ln1_b, v_ln2_g, v_ln2_b)
    d_s, nm_s, nv_s = _adamw("adamw_small", w_s, g_s, m_s, v_s, join.token)
    g_w_in, = join.wait(d_s)
    d_in, nm_in, nv_in = _adamw("adamw_w_in", w_in[0], g_w_in, m_w_in[0], v_w_in[0])

    def unpack(p):
        lbl = jnp.concatenate([p[4:5, :half], p[4:5, half:]], axis=0)
        cv = jnp.concatenate([p[6:7, 0:LANES], p[6:7, LANES:2 * LANES], p[6:7, 2 * LANES:3 * LANES]], axis=0)
        return dict(lb_logits=lbl, gate_norm_w=p[5:6, :half], conv_w=cv[None], ln1_g=p[0:1], ln1_b=p[1:2],
                    ln2_g=p[2:3], ln2_b=p[3:4])

    order = ("w_in", "lb_logits", "gate_norm_w", "conv_w", "w_out", "ln1_g", "ln1_b", "w_ff1", "w_ff2", "ln2_g", "ln2_b")
    grad = dict(unpack(g_s), w_in=g_w_in[None], w_out=g_w_out[None], w_ff1=g_w_ff1[None], w_ff2=g_w_ff2[None])
    delta = dict(unpack(d_s), w_in=d_in[None], w_out=d_out[None], w_ff1=d_ff1[None], w_ff2=d_ff2[None])
    new_m = dict(unpack(nm_s), w_in=nm_in[None], w_out=nm_out[None], w_ff1=nm_ff1[None], w_ff2=nm_ff2[None])
    new_v = dict(unpack(nv_s), w_in=nv_in[None], w_out=nv_out[None], w_ff1=nv_ff1[None], w_ff2=nv_ff2[None])
    return (loss, grad_x[None], *[grad[n] for n in order], *[delta[n] for n in order],
            *[new_m[n] for n in order], *[new_v[n] for n in order])
```

```python
import jax
import jax.numpy as jnp
from jax import lax
from jax.experimental import pallas as pl
from jax.experimental.pallas import tpu as pltpu

F32 = jnp.float32
BF16 = jnp.bfloat16
MXU_DTYPE = jnp.bfloat16

D_MODEL = 1024
HGRN_WIDTH = 512
HEAD_DIM = 128
N_HEADS = 4
CONV_WIDTH = 512
CHUNK = 64
D_FF = 4096
IN_COLS = 3584
GROUP = 512
N_GROUPS = IN_COLS // GROUP
ALPHA = 2.0 ** 0.25
EPS = 1e-5
N_CHIPS = 4
ADAM_LR, ADAM_B1, ADAM_B2, ADAM_EPS, ADAM_WD, ADAM_STEP = 0.001, 0.9, 0.999, 1e-08, 0.01, 10

LANES = 128
SUBLANES = 8
VMEM_LIMIT = 56 * 1024 * 1024
FF_BLOCK = 1024

NN = (((1,), (0,)), ((), ()))
NT = (((1,), (1,)), ((), ()))
TN = (((0,), (0,)), ((), ()))
MESH = pl.DeviceIdType.MESH
ANY = pl.BlockSpec(memory_space=pl.ANY)


def _dot(a, b, dims):
    return lax.dot_general(a.astype(MXU_DTYPE), b.astype(MXU_DTYPE), dims, preferred_element_type=F32)


def _dot_exact(a, b):
    return lax.dot_general(a, b, NN, precision=lax.Precision.HIGHEST, preferred_element_type=F32)


def _params(*sem):
    return pltpu.CompilerParams(dimension_semantics=sem, vmem_limit_bytes=VMEM_LIMIT)


def _resident(shape):
    return pl.BlockSpec(shape, lambda *_: (0,) * len(shape), pipeline_mode=pl.Buffered(1))


def _sigmoid(v):
    return 1.0 / (1.0 + jnp.exp(-v))


def _lower_bound(lbl):
    m = jnp.max(lbl, axis=0, keepdims=True)
    e = jnp.exp(lbl - m)
    s = e / jnp.sum(e, axis=0, keepdims=True)
    return s[0:1, :], s[1:2, :]


def _heads(v):
    return [v[:, h * HEAD_DIM:(h + 1) * HEAD_DIM] for h in range(N_HEADS)]


def _in_proj(x, w_in, after):
    t = x.shape[0]
    tm = min(t, 512)

    def body(x_ref, w_ref, after_ref, o_ref, xb_ref):
        xb = x_ref[...].astype(xb_ref.dtype)
        xb_ref[...] = xb
        for g in range(N_GROUPS):
            cols = slice(g * GROUP, (g + 1) * GROUP)
            o_ref[:, cols] = _dot(xb, w_ref[:, cols], NN)

    return pl.pallas_call(
        body, name="in_proj", grid=(t // tm,),
        in_specs=[pl.BlockSpec((tm, D_MODEL), lambda i: (i, 0)), _resident((D_MODEL, IN_COLS)), ANY],
        out_specs=[pl.BlockSpec((tm, IN_COLS), lambda i: (i, 0)), pl.BlockSpec((tm, D_MODEL), lambda i: (i, 0))],
        out_shape=[jax.ShapeDtypeStruct((t, IN_COLS), F32), jax.ShapeDtypeStruct((t, D_MODEL), BF16)],
        compiler_params=_params("parallel"),
    )(x, w_in, after)


def _gates(fp, lb):
    sig = _sigmoid(fp)
    f = lb + (1.0 - lb) * sig
    return sig, f, jnp.log(f), 1.0 - f


def _chunk_masks():
    row = lax.broadcasted_iota(jnp.int32, (CHUNK, CHUNK), 0)
    col = lax.broadcasted_iota(jnp.int32, (CHUNK, CHUNK), 1)
    return row >= col, row <= col


def _hgrn_fwd(proj, lb_logits):
    t = proj.shape[0]
    tb = min(t, 512)
    ncb = tb // CHUNK

    def body(q_ref, f_ref, v_ref, lbl_ref, o_ref, st_ref, s_scr):
        @pl.when(pl.program_id(0) == 0)
        def _():
            s_scr[...] = jnp.zeros_like(s_scr)

        lb, _ = _lower_bound(lbl_ref[...])
        causal, _ = _chunk_masks()
        tri = causal.astype(F32)

        def chunk(c, carry):
            rows = pl.ds(pl.multiple_of(c * CHUNK, CHUNK), CHUNK)
            q, v = q_ref[rows, :], v_ref[rows, :]
            _, _, g, k = _gates(f_ref[rows, :], lb)
            b = _dot_exact(tri, g)
            b_ref, b_last = b[CHUNK // 2:CHUNK // 2 + 1, :], b[CHUNK - 1:CHUNK, :]
            qt, kt, qi, ks = q * jnp.exp(b - b_ref), k * jnp.exp(b_ref - b), q * jnp.exp(b), k * jnp.exp(b_last - b)
            dec = jnp.exp(b_last)
            st = s_scr[...]
            st_ref[c] = st
            outs, news = [], []
            for qt_h, kt_h, qi_h, ks_h, v_h, st_h, dec_h in zip(*map(_heads, (qt, kt, qi, ks, v, st, dec))):
                scores = jnp.where(causal, _dot(qt_h, kt_h, NT), 0.0)
                outs.append(_dot(scores, v_h, NN) + _dot(qi_h, st_h, NT))
                news.append(dec_h * st_h + _dot(v_h, ks_h, TN))
            o_ref[rows, :] = jnp.concatenate(outs, axis=1)
            s_scr[...] = jnp.concatenate(news, axis=1)
            return carry

        lax.fori_loop(0, ncb, chunk, 0)

    grp = lambda g: pl.BlockSpec((tb, HGRN_WIDTH), lambda i: (i, g))
    return pl.pallas_call(
        body, name="hgrn_fwd", grid=(t // tb,),
        in_specs=[grp(0), grp(1), grp(2), pl.BlockSpec((2, HGRN_WIDTH), lambda i: (0, 0))],
        out_specs=[grp(0), pl.BlockSpec((ncb, HEAD_DIM, HGRN_WIDTH), lambda i: (i, 0, 0))],
        out_shape=[jax.ShapeDtypeStruct((t, HGRN_WIDTH), F32),
                   jax.ShapeDtypeStruct((t // CHUNK, HEAD_DIM, HGRN_WIDTH), F32)],
        scratch_shapes=[pltpu.VMEM((HEAD_DIM, HGRN_WIDTH), F32)],
        compiler_params=_params("arbitrary"),
    )(proj, proj, proj, lb_logits)


def _conv_taps(z, halo, zbuf, tb):
    zbuf[0:SUBLANES, :] = halo
    zbuf[SUBLANES:SUBLANES + tb, :] = z
    return zbuf[SUBLANES - 1:SUBLANES - 1 + tb, :], zbuf[SUBLANES - 2:SUBLANES - 2 + tb, :]


def _gate_fwd(proj, o, gate_norm_w, conv_w, after):
    t = proj.shape[0]
    tb = min(t, 512)
    hb = tb // SUBLANES

    def body(o_ref, og_ref, gnw_ref, b_ref, c_ref, u_ref, ch_ref, uh_ref, cw_ref, after_ref, cat_ref, zbuf):
        i = pl.program_id(1)
        ov, og = o_ref[...], og_ref[...]
        on = ov * lax.rsqrt(jnp.mean(ov * ov, axis=-1, keepdims=True) + EPS)
        cat_ref[0] = (on * gnw_ref[...] * (og * _sigmoid(og))).astype(cat_ref.dtype)
        z = c_ref[...] * u_ref[...]
        halo = jnp.where(i > 0, ch_ref[...] * uh_ref[...], 0.0)
        z1, z2 = _conv_taps(z, halo, zbuf, tb)
        cw = cw_ref[...]
        yc = cw[2:3, :] * z + cw[1:2, :] * z1 + cw[0:1, :] * z2
        cat_ref[1] = (b_ref[...] * yc).astype(cat_ref.dtype)

    blk = lambda off: pl.BlockSpec((tb, LANES), lambda j, i: (i, off + j))
    prev = lambda off: pl.BlockSpec((SUBLANES, LANES), lambda j, i: (jnp.maximum(i * hb - 1, 0), off + j))
    vec = lambda r: pl.BlockSpec((r, LANES), lambda j, i: (0, j))
    return pl.pallas_call(
        body, name="gate_fwd", grid=(4, t // tb),
        in_specs=[blk(0), blk(12), vec(1), blk(16), blk(20), blk(24), prev(20), prev(24), vec(3), ANY],
        out_specs=pl.BlockSpec((2, tb, LANES), lambda j, i: (0, i, j)),
        out_shape=jax.ShapeDtypeStruct((2, t, HGRN_WIDTH), BF16),
        scratch_shapes=[pltpu.VMEM((tb + SUBLANES, LANES), F32)],
        compiler_params=_params("parallel", "arbitrary"),
    )(o, proj, gate_norm_w, proj, proj, proj, proj, proj, conv_w, after)


def _out_ln1(cat, w_out, x, g1, b1):
    t = x.shape[0]
    tm = min(t, 512)

    def body(cat_ref, w_ref, x_ref, g_ref, b_ref, xhat_ref, h1_ref, rstd_ref):
        mix = _dot(cat_ref[0], w_ref[0:GROUP, :], NN) + _dot(cat_ref[1], w_ref[GROUP:2 * GROUP, :], NN)
        pre = ALPHA * x_ref[...] + mix
        xc = pre - jnp.mean(pre, axis=-1, keepdims=True)
        rstd = lax.rsqrt(jnp.mean(xc * xc, axis=-1, keepdims=True) + EPS)
        xhat = xc * rstd
        xhat_ref[...] = xhat
        h1_ref[...] = (xhat * g_ref[...] + b_ref[...]).astype(h1_ref.dtype)
        rstd_ref[...] = rstd

    row = pl.BlockSpec((tm, D_MODEL), lambda i: (i, 0))
    vec = pl.BlockSpec((1, D_MODEL), lambda i: (0, 0))
    return pl.pallas_call(
        body, name="out_ln1", grid=(t // tm,),
        in_specs=[pl.BlockSpec((2, tm, GROUP), lambda i: (0, i, 0)), _resident((D_MODEL, D_MODEL)), row, vec, vec],
        out_specs=[row, row, pl.BlockSpec((tm, 1), lambda i: (i, 0))],
        out_shape=[jax.ShapeDtypeStruct((t, D_MODEL), F32), jax.ShapeDtypeStruct((t, D_MODEL), BF16),
                   jax.ShapeDtypeStruct((t, 1), F32)],
        compiler_params=_params("parallel"),
    )(cat, w_out, x, g1, b1)


def _ln_bwd(dy, xhat, rstd, g):
    dxhat = dy * g
    m1 = jnp.mean(dxhat, axis=-1, keepdims=True)
    m2 = jnp.mean(dxhat * xhat, axis=-1, keepdims=True)
    return rstd * (dxhat - m1 - xhat * m2)


def _mlp_fwd(xhat1, g1, b1, w_ff1, w_ff2, g2, b2, target):
    t = xhat1.shape[0]
    tm = min(t, 256)

    def body(xh_ref, g1_ref, b1_ref, w1_ref, w2_ref, g2_ref, b2_ref, tg_ref,
             r_ref, dpre_ref, dpreb_ref, dg_ref, db_ref, loss_ref):
        @pl.when(pl.program_id(0) == 0)
        def _():
            dg_ref[...] = jnp.zeros_like(dg_ref)
            db_ref[...] = jnp.zeros_like(db_ref)
            loss_ref[...] = jnp.zeros_like(loss_ref)

        h1 = xh_ref[...] * g1_ref[...] + b1_ref[...]
        h1b = h1.astype(MXU_DTYPE)
        mlp = jnp.zeros((tm, D_MODEL), F32)
        for j in range(D_FF // FF_BLOCK):
            cols = slice(j * FF_BLOCK, (j + 1) * FF_BLOCK)
            r = jnp.square(jnp.maximum(_dot(h1b, w1_ref[:, cols], NN), 0.0)).astype(r_ref.dtype)
            r_ref[:, cols] = r
            mlp = mlp + _dot(r, w2_ref[cols, :], NN)
        pre = ALPHA * h1 + mlp
        xc = pre - jnp.mean(pre, axis=-1, keepdims=True)
        rstd = lax.rsqrt(jnp.mean(xc * xc, axis=-1, keepdims=True) + EPS)
        xhat = xc * rstd
        err = xhat * g2_ref[...] + b2_ref[...] - tg_ref[...]
        loss_ref[...] += 0.5 * jnp.sum(jnp.mean(err * err, axis=-1, keepdims=True))
        dy = err * (1.0 / D_MODEL)
        dg_ref[...] += jnp.sum(dy * xhat, axis=0, keepdims=True)
        db_ref[...] += jnp.sum(dy, axis=0, keepdims=True)
        dpre = _ln_bwd(dy, xhat, rstd, g2_ref[...])
        dpre_ref[...] = dpre
        dpreb_ref[...] = dpre.astype(dpreb_ref.dtype)

    row = pl.BlockSpec((tm, D_MODEL), lambda i: (i, 0))
    vec = pl.BlockSpec((1, D_MODEL), lambda i: (0, 0))
    return pl.pallas_call(
        body, name="mlp_fwd", grid=(t // tm,),
        in_specs=[row, vec, vec, _resident((D_MODEL, D_FF)), _resident((D_FF, D_MODEL)), vec, vec, row],
        out_specs=[pl.BlockSpec((tm, D_FF), lambda i: (i, 0)), row, row, vec, vec,
                   pl.BlockSpec((SUBLANES, LANES), lambda i: (0, 0))],
        out_shape=[jax.ShapeDtypeStruct((t, D_FF), BF16), jax.ShapeDtypeStruct((t, D_MODEL), F32),
                   jax.ShapeDtypeStruct((t, D_MODEL), BF16), jax.ShapeDtypeStruct((1, D_MODEL), F32),
                   jax.ShapeDtypeStruct((1, D_MODEL), F32), jax.ShapeDtypeStruct((SUBLANES, LANES), F32)],
        compiler_params=_params("arbitrary"),
    )(xhat1, g1, b1, w_ff1, w_ff2, g2, b2, target)


def _mlp_bwd(dpre2, r, w_ff1, w_ff2, xhat1, rstd1, g1):
    t = r.shape[0]
    tm = min(t, 256)

    def body(dp2_ref, r_ref, w1_ref, w2_ref, xh_ref, rs_ref, g_ref, da_ref, dpre_ref, dpreb_ref, dg_ref, db_ref):
        @pl.when(pl.program_id(0) == 0)
        def _():
            dg_ref[...] = jnp.zeros_like(dg_ref)
            db_ref[...] = jnp.zeros_like(db_ref)

        dp2 = dp2_ref[...]
        dp2b = dp2.astype(MXU_DTYPE)
        back = jnp.zeros((tm, D_MODEL), F32)
        for j in range(D_FF // FF_BLOCK):
            cols = slice(j * FF_BLOCK, (j + 1) * FF_BLOCK)
            dr = _dot(dp2b, w2_ref[cols, :], NT)
            da = (dr * (2.0 * jnp.sqrt(r_ref[:, cols].astype(F32)))).astype(da_ref.dtype)
            da_ref[:, cols] = da
            back = back + _dot(da, w1_ref[:, cols], NT)
        dh1 = ALPHA * dp2 + back
        xhat = xh_ref[...]
        dg_ref[...] += jnp.sum(dh1 * xhat, axis=0, keepdims=True)
        db_ref[...] += jnp.sum(dh1, axis=0, keepdims=True)
        dpre = _ln_bwd(dh1, xhat, rs_ref[...], g_ref[...])
        dpre_ref[...] = dpre
        dpreb_ref[...] = dpre.astype(dpreb_ref.dtype)

    row = pl.BlockSpec((tm, D_MODEL), lambda i: (i, 0))
    wide = pl.BlockSpec((tm, D_FF), lambda i: (i, 0))
    vec = pl.BlockSpec((1, D_MODEL), lambda i: (0, 0))
    return pl.pallas_call(
        body, name="mlp_bwd", grid=(t // tm,),
        in_specs=[row, wide, _resident((D_MODEL, D_FF)), _resident((D_FF, D_MODEL)), row,
                  pl.BlockSpec((tm, 1), lambda i: (i, 0)), vec],
        out_specs=[wide, row, row, vec, vec],
        out_shape=[jax.ShapeDtypeStruct((t, D_FF), BF16), jax.ShapeDtypeStruct((t, D_MODEL), F32),
                   jax.ShapeDtypeStruct((t, D_MODEL), BF16), jax.ShapeDtypeStruct((1, D_MODEL), F32),
                   jax.ShapeDtypeStruct((1, D_MODEL), F32)],
        compiler_params=_params("arbitrary"),
    )(dpre2, r, w_ff1, w_ff2, xhat1, rstd1, g1)


def _out_bwd(dpre1b, w_out):
    t = dpre1b.shape[0]
    tm = min(t, 512)

    def body(d_ref, w_ref, o_ref):
        o_ref[...] = _dot(d_ref[...], w_ref[...], NT)

    return pl.pallas_call(
        body, name="out_bwd", grid=(t // tm,),
        in_specs=[pl.BlockSpec((tm, D_MODEL), lambda i: (i, 0)), _resident((D_MODEL, D_MODEL))],
        out_specs=pl.BlockSpec((tm, D_MODEL), lambda i: (i, 0)),
        out_shape=jax.ShapeDtypeStruct((t, D_MODEL), F32),
        compiler_params=_params("parallel"),
    )(dpre1b, w_out)


def _gate_bwd(dcat, o, proj, gate_norm_w, conv_w, after):
    t = proj.shape[0]
    tb = min(t, 512)
    hb = tb // SUBLANES
    nblk = t // tb

    def body(do2_ref, dy_ref, dyn_ref, o_ref, og_ref, gnw_ref, b_ref, bn_ref, c_ref, u_ref, ch_ref, uh_ref, cw_ref,
             after_ref, do_ref, dp_ref, dgnw_ref, dcw_ref, zbuf, dbuf):
        i = pl.program_id(1)

        @pl.when(i == 0)
        def _():
            dgnw_ref[...] = jnp.zeros_like(dgnw_ref)
            dcw_ref[...] = jnp.zeros_like(dcw_ref)

        ov, og, gnw, do2 = o_ref[...], og_ref[...], gnw_ref[...], do2_ref[...]
        rs = lax.rsqrt(jnp.mean(ov * ov, axis=-1, keepdims=True) + EPS)
        on = ov * rs
        sg = _sigmoid(og)
        sil = og * sg
        don = do2 * gnw * sil
        dgnw_ref[...] += jnp.sum(do2 * on * sil, axis=0, keepdims=True)
        dp_ref[0] = (do2 * on * gnw * (sg * (1.0 + og * (1.0 - sg)))).astype(dp_ref.dtype)
        do_ref[...] = rs * (don - on * jnp.mean(don * on, axis=-1, keepdims=True))

        bg, cg, u, dy = b_ref[...], c_ref[...], u_ref[...], dy_ref[...]
        z = cg * u
        halo = jnp.where(i > 0, ch_ref[...] * uh_ref[...], 0.0)
        z1, z2 = _conv_taps(z, halo, zbuf, tb)
        cw = cw_ref[...]
        yc = cw[2:3, :] * z + cw[1:2, :] * z1 + cw[0:1, :] * z2
        dyc = dy * bg
        dbuf[0:tb, :] = dyc
        dbuf[tb:tb + SUBLANES, :] = jnp.where(i < nblk - 1, dyn_ref[...] * bn_ref[...], 0.0)
        d1, d2 = dbuf[1:1 + tb, :], dbuf[2:2 + tb, :]
        dz = cw[2:3, :] * dyc + cw[1:2, :] * d1 + cw[0:1, :] * d2
        dp_ref[1] = (dy * yc).astype(dp_ref.dtype)
        dp_ref[2] = (dz * u).astype(dp_ref.dtype)
        dp_ref[3] = (dz * cg).astype(dp_ref.dtype)
        dcw_ref[0:1, :] += jnp.sum(dyc * z2, axis=0, keepdims=True)
        dcw_ref[1:2, :] += jnp.sum(dyc * z1, axis=0, keepdims=True)
        dcw_ref[2:3, :] += jnp.sum(dyc * z, axis=0, keepdims=True)

    blk = lambda off: pl.BlockSpec((tb, LANES), lambda j, i: (i, off + j))
    prev = lambda off: pl.BlockSpec((SUBLANES, LANES), lambda j, i: (jnp.maximum(i * hb - 1, 0), off + j))
    nxt = lambda off: pl.BlockSpec((SUBLANES, LANES), lambda j, i: (jnp.minimum((i + 1) * hb, t // SUBLANES - 1), off + j))
    vec = lambda r: pl.BlockSpec((r, LANES), lambda j, i: (0, j))
    return pl.pallas_call(
        body, name="gate_bwd", grid=(4, nblk),
        in_specs=[blk(0), blk(4), nxt(4), blk(0), blk(12), vec(1), blk(16), nxt(16), blk(20), blk(24), prev(20), prev(24),
                  vec(3), ANY],
        out_specs=[blk(0), pl.BlockSpec((4, tb, LANES), lambda j, i: (0, i, j)), vec(1), vec(3)],
        out_shape=[jax.ShapeDtypeStruct((t, HGRN_WIDTH), F32), jax.ShapeDtypeStruct((4, t, HGRN_WIDTH), BF16),
                   jax.ShapeDtypeStruct((1, HGRN_WIDTH), F32), jax.ShapeDtypeStruct((3, CONV_WIDTH), F32)],
        scratch_shapes=[pltpu.VMEM((tb + SUBLANES, LANES), F32), pltpu.VMEM((tb + SUBLANES, LANES), F32)],
        compiler_params=_params("parallel", "arbitrary"),
    )(dcat, dcat, dcat, o, proj, gate_norm_w, proj, proj, proj, proj, proj, proj, conv_w, after)


def _hgrn_bwd(proj, do, states, lb_logits, after):
    t = proj.shape[0]
    tb = min(t, 512)
    ncb = tb // CHUNK
    nblk = t // tb

    def body(q_ref, f_ref, v_ref, do_ref, st_ref, lbl_ref, after_ref, dp_ref, dlbl_ref, ds_scr, dlb_scr):
        i = pl.program_id(0)

        @pl.when(i == 0)
        def _():
            ds_scr[...] = jnp.zeros_like(ds_scr)
            dlb_scr[...] = jnp.zeros_like(dlb_scr)

        lb, s1 = _lower_bound(lbl_ref[...])
        causal, anti = _chunk_masks()
        tri, tri_rev = causal.astype(F32), anti.astype(F32)

        def chunk(n, carry):
            c = ncb - 1 - n
            rows = pl.ds(pl.multiple_of(c * CHUNK, CHUNK), CHUNK)
            q, v, do_c = q_ref[rows, :], v_ref[rows, :], do_ref[rows, :]
            sig, f, g, k = _gates(f_ref[rows, :], lb)
            b = _dot_exact(tri, g)
            b_ref, b_last = b[CHUNK // 2:CHUNK // 2 + 1, :], b[CHUNK - 1:CHUNK, :]
            e_q, e_k, e_i, e_s = jnp.exp(b - b_ref), jnp.exp(b_ref - b), jnp.exp(b), jnp.exp(b_last - b)
            dec = jnp.exp(b_last)
            qt, kt, qi, ks = q * e_q, k * e_k, q * e_i, k * e_s
            st, dst = st_ref[c], ds_scr[...]

            dqt, dkt, dv, dqi, dks, new_ds = [], [], [], [], [], []
            for qt_h, kt_h, qi_h, ks_h, v_h, do_h, st_h, dst_h, dec_h in zip(
                    *map(_heads, (qt, kt, qi, ks, v, do_c, st, dst, dec))):
                scores = jnp.where(causal, _dot(qt_h, kt_h, NT), 0.0)
                dscores = jnp.where(causal, _dot(do_h, v_h, NT), 0.0)
                dqt.append(_dot(dscores, kt_h, NN))
                dkt.append(_dot(dscores, qt_h, TN))
                dv.append(_dot(scores, do_h, TN) + _dot(ks_h, dst_h, NT))
                dqi.append(_dot(do_h, st_h, NN))
                dks.append(_dot(v_h, dst_h, NN))
                new_ds.append(dec_h * dst_h + _dot(do_h, qi_h, TN))
            dqt, dkt, dv, dqi, dks = (jnp.concatenate(p, axis=1) for p in (dqt, dkt, dv, dqi, dks))
            ddec = jnp.sum(dst * st, axis=0, keepdims=True)
            ds_scr[...] = jnp.concatenate(new_ds, axis=1)

            dq = dqt * e_q + dqi * e_i
            dk = dkt * e_k + dks * e_s
            db = q * dq - k * dk
            db_last = jnp.sum(dks * ks, axis=0, keepdims=True) + ddec * dec
            dg = _dot_exact(tri_rev, db) + db_last
            df = dg / f - dk
            dlb_scr[...] += jnp.sum(df * (1.0 - sig), axis=0, keepdims=True)
            dp_ref[0, rows, :] = dq.astype(dp_ref.dtype)
            dp_ref[1, rows, :] = (df * (1.0 - lb) * sig * (1.0 - sig)).astype(dp_ref.dtype)
            dp_ref[2, rows, :] = dv.astype(dp_ref.dtype)
            return carry

        lax.fori_loop(0, ncb, chunk, 0)

        @pl.when(i == nblk - 1)
        def _():
            dlb = dlb_scr[...]
            dlbl_ref[0:1, :] = dlb * lb * (1.0 - lb)
            dlbl_ref[1:2, :] = -dlb * lb * s1

    grp = lambda g: pl.BlockSpec((tb, HGRN_WIDTH), lambda i: (nblk - 1 - i, g))
    vec = pl.BlockSpec((2, HGRN_WIDTH), lambda i: (0, 0))
    return pl.pallas_call(
        body, name="hgrn_bwd", grid=(nblk,),
        in_specs=[grp(0), grp(1), grp(2), grp(0),
                  pl.BlockSpec((ncb, HEAD_DIM, HGRN_WIDTH), lambda i: (nblk - 1 - i, 0, 0)), vec, ANY],
        out_specs=[pl.BlockSpec((3, tb, HGRN_WIDTH), lambda i: (0, nblk - 1 - i, 0)), vec],
        out_shape=[jax.ShapeDtypeStruct((3, t, HGRN_WIDTH), BF16), jax.ShapeDtypeStruct((2, HGRN_WIDTH), F32)],
        scratch_shapes=[pltpu.VMEM((HEAD_DIM, HGRN_WIDTH), F32), pltpu.VMEM((1, HGRN_WIDTH), F32)],
        compiler_params=_params("arbitrary"),
    )(proj, proj, proj, do, states, lb_logits, after)


def _in_bwd(dph, dpg, w_in, dpre1):
    t = dpre1.shape[0]
    tm = min(t, 512)

    def body(dh_ref, dg_ref, w_ref, dp_ref, o_ref):
        acc = ALPHA * dp_ref[...]
        for g in range(N_GROUPS):
            part = dh_ref[g] if g < 3 else dg_ref[g - 3]
            acc = acc + _dot(part, w_ref[:, g * GROUP:(g + 1) * GROUP], NT)
        o_ref[...] = acc

    row = pl.BlockSpec((tm, D_MODEL), lambda i: (i, 0))
    return pl.pallas_call(
        body, name="in_bwd", grid=(t // tm,),
        in_specs=[pl.BlockSpec((3, tm, GROUP), lambda i: (0, i, 0)), pl.BlockSpec((4, tm, GROUP), lambda i: (0, i, 0)),
                  _resident((D_MODEL, IN_COLS)), row],
        out_specs=row,
        out_shape=jax.ShapeDtypeStruct((t, D_MODEL), F32),
        compiler_params=_params("parallel"),
    )(dph, dpg, w_in, dpre1)


def _grad_w(name, operands, widths, shape, step, after=None):
    t = operands[0].shape[-2]
    tt = min(t, 512)
    n_in, n_steps = len(operands), t // tt
    in_specs = [pl.BlockSpec((tt, w), lambda k: (k, 0)) if a.ndim == 2 else
                pl.BlockSpec((a.shape[0], tt, w), lambda k: (0, k, 0)) for a, w in zip(operands, widths)]
    extra = [] if after is None else [after]

    def body(*refs):
        o_ref, acc, sem = refs[-3:]
        k = pl.program_id(0)

        @pl.when(k == 0)
        def _():
            acc[...] = jnp.zeros_like(acc)

        step(acc, *refs[:n_in])

        @pl.when(k == n_steps - 1)
        def _():
            out = pltpu.make_async_copy(acc, o_ref, sem)
            out.start()
            out.wait()

    return pl.pallas_call(
        body, name=name, grid=(n_steps,), in_specs=in_specs + [ANY] * len(extra), out_specs=ANY,
        out_shape=jax.ShapeDtypeStruct(shape, F32),
        scratch_shapes=[pltpu.VMEM(shape, F32), pltpu.SemaphoreType.DMA],
        compiler_params=_params("arbitrary"),
    )(*operands, *extra)


def _dw_in(xb, dph, dpg, after):
    def step(acc, x_ref, dh_ref, dg_ref):
        xv = x_ref[...]
        for g in range(N_GROUPS):
            part = dh_ref[g] if g < 3 else dg_ref[g - 3]
            acc[:, g * GROUP:(g + 1) * GROUP] += _dot(xv, part, TN)

    return _grad_w("dw_in", (xb, dph, dpg), (D_MODEL, GROUP, GROUP), (D_MODEL, IN_COLS), step, after)


def _dw_out(cat, dpre1b):
    def step(acc, cat_ref, d_ref):
        dv = d_ref[...]
        for g in range(2):
            acc[g * GROUP:(g + 1) * GROUP, :] += _dot(cat_ref[g], dv, TN)

    return _grad_w("dw_out", (cat, dpre1b), (GROUP, D_MODEL), (D_MODEL, D_MODEL), step)


def _dw_ff1(h1b, da):
    def step(acc, h_ref, da_ref):
        hv = h_ref[...]
        for j in range(D_FF // FF_BLOCK):
            cols = slice(j * FF_BLOCK, (j + 1) * FF_BLOCK)
            acc[:, cols] += _dot(hv, da_ref[:, cols], TN)

    return _grad_w("dw_ff1", (h1b, da), (D_MODEL, D_FF), (D_MODEL, D_FF), step)


def _dw_ff2(r, dpre2b):
    def step(acc, r_ref, d_ref):
        dv = d_ref[...]
        for j in range(D_FF // FF_BLOCK):
            rows = slice(j * FF_BLOCK, (j + 1) * FF_BLOCK)
            acc[rows, :] += _dot(r_ref[:, rows], dv, TN)

    return _grad_w("dw_ff2", (r, dpre2b), (D_FF, D_MODEL), (D_FF, D_MODEL), step)


def _place():
    x, y, c = lax.axis_index("x"), lax.axis_index("y"), lax.axis_index("c")
    return x, y, c, 2 * x + y


def _other_chips(x, y):
    return [(1 - x, y), (x, 1 - y), (1 - x, 1 - y)]


def _place_shard(name, w, chip, cols_sharded, after=None):
    rows, cols = w.shape
    tr = min(rows, 256)
    nb = rows // tr
    full = (rows, cols * N_CHIPS) if cols_sharded else (rows * N_CHIPS, cols)
    out_map = (lambda i, s: (i, s[0])) if cols_sharded else (lambda i, s: (s[0] * nb + i, 0))

    def body(s_ref, w_ref, *rest):
        rest[-1][...] = w_ref[...].astype(rest[-1].dtype)

    extra = [] if after is None else [after]
    return pl.pallas_call(
        body, name=name,
        grid_spec=pltpu.PrefetchScalarGridSpec(
            num_scalar_prefetch=1, grid=(nb,),
            in_specs=[pl.BlockSpec((tr, cols), lambda i, s: (i, 0))] + [ANY] * len(extra),
            out_specs=pl.BlockSpec((tr, cols), out_map)),
        out_shape=jax.ShapeDtypeStruct(full, BF16),
        compiler_params=_params("parallel"),
    )(chip, w, *extra)


def _gather_w_in(w_in, conv_w):
    half, cs, n_p = D_MODEL // 2, IN_COLS // N_CHIPS, 3

    def body(w_alias, cv_ref, w_ref, cvf_ref, send_sems, recv_sems, local_sem):
        x, y, c, me = _place()
        sibling = (x, y, 1 - c)
        chips = _other_chips(x, y)
        blk = lambda chip, h: w_ref.at[pl.ds(h * half, half), pl.ds(chip * cs, cs)]

        def copy(k, src, dst, to):
            return pltpu.make_async_remote_copy(src_ref=src, dst_ref=dst, send_sem=send_sems.at[k],
                                                recv_sem=recv_sems.at[k], device_id=to, device_id_type=MESH)

        own_cv = pltpu.make_async_copy(cv_ref, cvf_ref.at[me], local_sem)
        own_cv.start()
        first = [copy(j, blk(me, c), blk(me, c), (px, py, c)) for j, (px, py) in enumerate(chips)]
        first += [copy(2 * n_p + j, cv_ref, cvf_ref.at[me], (px, py, c)) for j, (px, py) in enumerate(chips)]
        for cp in first:
            cp.start()
        passed = []
        for j, (px, py) in enumerate(chips):
            got = blk(2 * px + py, c)
            copy(j, got, got, (px, py, c)).wait_recv()
            passed.append(copy(n_p + j, got, got, sibling))
            passed[-1].start()
        for j, (px, py) in enumerate(chips):
            got = blk(2 * px + py, 1 - c)
            copy(n_p + j, got, got, sibling).wait_recv()
            copy(2 * n_p + j, cv_ref, cvf_ref.at[2 * px + py], (px, py, c)).wait_recv()
        for cp in first + passed:
            cp.wait_send()
        own_cv.wait()

    return pl.pallas_call(
        body, name="gather_w_in", in_specs=[ANY, ANY], out_specs=[ANY, ANY],
        out_shape=[jax.ShapeDtypeStruct(w_in.shape, w_in.dtype), jax.ShapeDtypeStruct((N_CHIPS,) + conv_w.shape, conv_w.dtype)],
        input_output_aliases={0: 0},
        scratch_shapes=[pltpu.SemaphoreType.DMA((3 * n_p,)), pltpu.SemaphoreType.DMA((3 * n_p,)), pltpu.SemaphoreType.DMA],
    )(w_in, conv_w)


HBM = pl.BlockSpec(memory_space=pltpu.HBM)
SEM = pl.BlockSpec(memory_space=pltpu.SEMAPHORE)
EFFECT = pltpu.SideEffectType.DATAFLOW_SIDE_EFFECTING


class _Split:
    def __init__(self, name, arrays, plan):
        n, n_copies = len(arrays), plan.count
        self.name, self.plan, self.n = name, plan, n

        def body(*refs):
            send_sems, recv_sems, token = refs[n], refs[n + 1], refs[-1]
            for k, (src, dst, to) in enumerate(plan(refs[:n])):
                pltpu.make_async_remote_copy(src_ref=src, dst_ref=dst, send_sem=send_sems.at[k], recv_sem=recv_sems.at[k],
                                             device_id=to, device_id_type=MESH).start()
            token[...] = jnp.zeros_like(token)

        outs = pl.pallas_call(
            body, name=name + "_start",
            out_shape=(pltpu.SemaphoreType.DMA((n_copies,)), pltpu.SemaphoreType.DMA((n_copies,)),
                       *[pltpu.HBM(a.shape, a.dtype) for a in arrays], jax.ShapeDtypeStruct((SUBLANES, LANES), F32)),
            in_specs=(HBM,) * n, out_specs=(SEM, SEM) + (HBM,) * n + (pl.BlockSpec(memory_space=pltpu.VMEM),),
            input_output_aliases={i: 2 + i for i in range(n)},
            compiler_params=pltpu.CompilerParams(has_side_effects=EFFECT),
        )(*[pltpu.with_memory_space_constraint(a, pltpu.HBM) for a in arrays])
        self.sems, self.arrays, self.token = outs[:2], outs[2:2 + n], outs[-1]

    def wait(self, after):
        n, plan = self.n, self.plan

        def body(*refs):
            send_sems, recv_sems = refs[n], refs[n + 1]
            for k, (src, dst, to) in enumerate(plan(refs[:n])):
                cp = pltpu.make_async_remote_copy(src_ref=src, dst_ref=dst, send_sem=send_sems.at[k],
                                                  recv_sem=recv_sems.at[k], device_id=to, device_id_type=MESH)
                cp.wait_send()
                cp.wait_recv()

        return pl.pallas_call(
            body, name=self.name + "_wait", out_shape=tuple(pltpu.HBM(a.shape, a.dtype) for a in self.arrays),
            in_specs=(HBM,) * n + (SEM, SEM, ANY), out_specs=(HBM,) * n, input_output_aliases={i: i for i in range(n)},
            compiler_params=pltpu.CompilerParams(has_side_effects=EFFECT),
        )(*self.arrays, *self.sems, after)


COLS_SHARDED = (True, False, True, False)
HALF_SHAPES = [(D_MODEL // 2, IN_COLS), (D_MODEL, D_MODEL // 2), (D_MODEL // 2, D_FF), (D_FF, D_MODEL // 2)]
PIECE_SHAPES = [(D_MODEL // 2, IN_COLS // N_CHIPS), (D_MODEL // N_CHIPS, D_MODEL // 2),
                (D_MODEL // 2, D_FF // N_CHIPS), (D_FF // N_CHIPS, D_MODEL // 2)]


def _shard_view(kind, ref, chip):
    if COLS_SHARDED[kind]:
        n = ref.shape[1] // N_CHIPS
        return ref.at[:, pl.ds(chip * n, n)]
    n = ref.shape[0] // N_CHIPS
    return ref.at[pl.ds(chip * n, n), :]


def _half_view(kind, ref, h):
    if COLS_SHARDED[kind]:
        n = ref.shape[0] // 2
        return ref.at[pl.ds(h * n, n), :]
    n = ref.shape[1] // 2
    return ref.at[:, pl.ds(h * n, n)]


def _plan(count):
    def mark(fn):
        fn.count = count
        return fn
    return mark


def _shard_half_view(kind, ref, chip, h):
    if COLS_SHARDED[kind]:
        m, n = ref.shape[0] // 2, ref.shape[1] // N_CHIPS
        return ref.at[pl.ds(h * m, m), pl.ds(chip * n, n)]
    m = ref.shape[0] // N_CHIPS // 2
    return ref.at[pl.ds((2 * chip + h) * m, m), :]


def _gather_rest(w_out, w_ff1, w_ff2):
    @_plan(9)
    def plan(refs):
        x, y, c, me = _place()
        mine = [_shard_half_view(kind, ref, me, c) for kind, ref in zip((1, 2, 3), refs)]
        return [(v, v, (px, py, c)) for v in mine for px, py in _other_chips(x, y)]

    return _Split("gather_rest", (w_out, w_ff1, w_ff2), plan)


def _pass_rest(w_out, w_ff1, w_ff2):
    @_plan(9)
    def plan(refs):
        x, y, c, _ = _place()
        got = [_shard_half_view(kind, ref, 2 * px + py, c) for kind, ref in zip((1, 2, 3), refs)
               for px, py in _other_chips(x, y)]
        return [(v, v, (x, y, 1 - c)) for v in got]

    return _Split("pass_rest", (w_out, w_ff1, w_ff2), plan)


def _swap_halves(kinds, grads):
    @_plan(len(kinds))
    def plan(refs):
        x, y, c, _ = _place()
        return [(_half_view(kind, g, 1 - c), land, (x, y, 1 - c))
                for kind, g, land in zip(kinds, refs[:len(kinds)], refs[len(kinds):])]

    lands = [lax.empty(HALF_SHAPES[kind], F32) for kind in kinds]
    return _Split("swap_halves_" + "".join(map(str, kinds)), (*grads, *lands), plan)


def _add_half(name, g, recv, core, rows_split):
    shape = recv.shape
    tr = min(shape[0], 128 if rows_split else 256)
    nb = shape[0] // tr

    def body(c_ref, g_ref, r_ref, o_ref):
        o_ref[...] = (g_ref[...] + r_ref[...]).astype(o_ref.dtype)

    g_map = (lambda i, c_ref: (c_ref[0] * nb + i, 0)) if rows_split else (lambda i, c_ref: (i, c_ref[0]))
    blk = pl.BlockSpec((tr, shape[1]), lambda i, c_ref: (i, 0))
    return pl.pallas_call(
        body, name=name,
        grid_spec=pltpu.PrefetchScalarGridSpec(
            num_scalar_prefetch=1, grid=(nb,),
            in_specs=[pl.BlockSpec((tr, shape[1]), g_map), blk], out_specs=blk),
        out_shape=jax.ShapeDtypeStruct(shape, BF16),
        compiler_params=_params("parallel"),
    )(core, g, recv)


def _exchange_pieces(kinds, halves):
    n_p = N_CHIPS - 1

    @_plan(n_p * len(kinds))
    def plan(refs):
        x, y, c, _ = _place()
        return [(_shard_view(kind, half, 2 * px + py), land.at[j], (px, py, c))
                for j, (px, py) in enumerate(_other_chips(x, y))
                for kind, half, land in zip(kinds, refs[:len(kinds)], refs[len(kinds):])]

    lands = [lax.empty((n_p,) + PIECE_SHAPES[kind], BF16) for kind in kinds]
    return _Split("exchange_pieces_" + "".join(map(str, kinds)), (*halves, *lands), plan)


def _sum_pieces(name, half, slots, place, rows_split):
    n_p, rows, cols = slots.shape
    tr = min(rows, 256)
    nb = rows // tr
    if rows_split:
        own_map = lambda i, s: (i, s[0])
        out_map = lambda i, s: (s[1] * nb + i, 0)
        shard = (2 * rows, cols)
    else:
        own_map = lambda i, s: (s[0] * nb + i, 0)
        out_map = lambda i, s: (i, s[1])
        shard = (rows, 2 * cols)

    def body(s_ref, own_ref, slot_ref, o_ref):
        total = own_ref[...].astype(F32)
        for j in range(n_p):
            total = total + slot_ref[j].astype(F32)
        o_ref[...] = total

    return pl.pallas_call(
        body, name=name,
        grid_spec=pltpu.PrefetchScalarGridSpec(
            num_scalar_prefetch=1, grid=(nb,),
            in_specs=[pl.BlockSpec((tr, cols), own_map), pl.BlockSpec((n_p, tr, cols), lambda i, s: (0, i, 0))],
            out_specs=pl.BlockSpec((tr, cols), out_map)),
        out_shape=jax.ShapeDtypeStruct(shard, F32),
        compiler_params=_params("parallel"),
    )(place, half, slots)


def _join_halves(kinds, shards):
    @_plan(len(kinds))
    def plan(refs):
        x, y, c, _ = _place()
        return [(_half_view(kind, g, c), _half_view(kind, g, c), (x, y, 1 - c)) for kind, g in zip(kinds, refs)]

    return _Split("join_halves_" + "".join(map(str, kinds)), tuple(shards), plan)


def _sum_small(pack, after):
    n_dev = 8

    def body(p_ref, after_ref, o_ref, slots, send_sems, recv_sems):
        x, y, c, _ = _place()
        me = 4 * x + 2 * y + c
        slots[me] = p_ref[...]
        sends = []
        for m in range(1, n_dev):
            peer = ((1 - x) if m & 4 else x, (1 - y) if m & 2 else y, (1 - c) if m & 1 else c)
            sends.append(pltpu.make_async_remote_copy(
                src_ref=p_ref, dst_ref=slots.at[me], send_sem=send_sems.at[m - 1], recv_sem=recv_sems.at[m - 1],
                device_id=peer, device_id_type=MESH))
        for cp in sends:
            cp.start()
        for m in range(1, n_dev):
            peer = ((1 - x) if m & 4 else x, (1 - y) if m & 2 else y, (1 - c) if m & 1 else c)
            pltpu.make_async_remote_copy(
                src_ref=p_ref, dst_ref=slots.at[4 * peer[0] + 2 * peer[1] + peer[2]], send_sem=send_sems.at[m - 1],
                recv_sem=recv_sems.at[m - 1], device_id=peer, device_id_type=MESH).wait_recv()
        for cp in sends:
            cp.wait_send()
        total = slots[0]
        for d in range(1, n_dev):
            total = total + slots[d]
        o_ref[...] = total

    vm = pl.BlockSpec(memory_space=pltpu.VMEM)
    return pl.pallas_call(
        body, name="sum_small", in_specs=[vm, ANY], out_specs=vm,
        out_shape=jax.ShapeDtypeStruct(pack.shape, F32),
        scratch_shapes=[pltpu.VMEM((n_dev,) + pack.shape, F32), pltpu.SemaphoreType.DMA((n_dev - 1,)),
                        pltpu.SemaphoreType.DMA((n_dev - 1,))],
    )(pack, after)


def _adamw(name, w, g, m, v, after=None):
    rows, cols = w.shape
    tr = min(rows, 256)
    extra = [] if after is None else [after]

    def body(w_ref, g_ref, m_ref, v_ref, *rest):
        d_ref, nm_ref, nv_ref = rest[-3:]
        gv = g_ref[...]
        nm = ADAM_B1 * m_ref[...] + (1.0 - ADAM_B1) * gv
        nv = ADAM_B2 * v_ref[...] + (1.0 - ADAM_B2) * jnp.square(gv)
        m_hat = nm / (1.0 - ADAM_B1 ** ADAM_STEP)
        v_hat = nv / (1.0 - ADAM_B2 ** ADAM_STEP)
        d_ref[...] = -ADAM_LR * (m_hat / (jnp.sqrt(v_hat) + ADAM_EPS) + ADAM_WD * w_ref[...])
        nm_ref[...] = nm
        nv_ref[...] = nv

    blk = pl.BlockSpec((tr, cols), lambda i: (i, 0))
    return pl.pallas_call(
        body, name=name, grid=(rows // tr,), in_specs=[blk] * 4 + [ANY] * len(extra), out_specs=[blk] * 3,
        out_shape=[jax.ShapeDtypeStruct(w.shape, F32)] * 3,
        compiler_params=_params("parallel"),
    )(w, g, m, v, *extra)


def kernel(x, w_in, lb_logits, gate_norm_w, conv_w, w_out, ln1_g, ln1_b, w_ff1, w_ff2, ln2_g, ln2_b, loss_target, m_w_in, m_lb_logits, m_gate_norm_w, m_conv_w, m_w_out, m_ln1_g, m_ln1_b, m_w_ff1, m_w_ff2, m_ln2_g, m_ln2_b, v_w_in, v_lb_logits, v_gate_norm_w, v_conv_w, v_w_out, v_ln1_g, v_ln1_b, v_w_ff1, v_w_ff2, v_ln2_g, v_ln2_b):
    xs, tgt = x[0], loss_target[0]
    chip = 2 * lax.axis_index("x") + lax.axis_index("y")
    core = lax.axis_index("c").astype(jnp.int32).reshape(1)
    chip1 = chip.astype(jnp.int32).reshape(1)
    place = jnp.concatenate([chip1, core])

    wb_in, cv4 = _gather_w_in(_place_shard("place_w_in", w_in[0], chip1, True), conv_w[0])
    conv_full = cv4.transpose(1, 0, 2).reshape(3, CONV_WIDTH)
    rest = _gather_rest(
        _place_shard("place_w_out", w_out[0], chip1, False, after=wb_in),
        _place_shard("place_w_ff1", w_ff1[0], chip1, True, after=wb_in),
        _place_shard("place_w_ff2", w_ff2[0], chip1, False, after=wb_in))

    proj, xb = _in_proj(xs, wb_in, rest.token)
    o, states = _hgrn_fwd(proj, lb_logits)
    passed = _pass_rest(*rest.wait(o))
    cat = _gate_fwd(proj, o, gate_norm_w, conv_full, passed.token)
    wb_out, wb_ff1, wb_ff2 = passed.wait(cat)
    xhat1, h1b, rstd1 = _out_ln1(cat, wb_out, xs, ln1_g, ln1_b)
    r, dpre2, dpre2b, g_ln2_g, g_ln2_b, loss8 = _mlp_fwd(xhat1, ln1_g, ln1_b, wb_ff1, wb_ff2, ln2_g, ln2_b, tgt)

    names = ("w_in", "w_out", "w_ff1", "w_ff2")

    def add_halves(kinds, grads, lands):
        return [_add_half("add_half_" + names[k], g, ld, core, COLS_SHARDED[k]) for k, g, ld in zip(kinds, grads, lands)]

    def sum_pieces(kinds, halves, lands):
        return [_sum_pieces("sum_pieces_" + names[k], h, ld, place, COLS_SHARDED[k]) for k, h, ld in zip(kinds, halves, lands)]

    da, dpre1, dpre1b, g_ln1_g, g_ln1_b = _mlp_bwd(dpre2, r, wb_ff1, wb_ff2, xhat1, rstd1, ln1_g)
    dcat = _out_bwd(dpre1b, wb_out)
    early = (1, 2, 3)
    swap = _swap_halves(early, (_dw_out(cat, dpre1b), _dw_ff1(h1b, da), _dw_ff2(r, dpre2b)))
    do, dpg, g_gnw, g_conv = _gate_bwd(dcat, o, proj, gate_norm_w, conv_full, swap.token)
    swapped = swap.wait(do)
    exch = _exchange_pieces(early, add_halves(early, swapped[:3], swapped[3:]))
    dph, g_lbl = _hgrn_bwd(proj, do, states, lb_logits, exch.token)
    grad_x = _in_bwd(dph, dpg, wb_in, dpre1)
    exchanged = exch.wait(grad_x)
    join = _join_halves(early, sum_pieces(early, exchanged[:3], exchanged[3:]))
    g_in_local = _dw_in(xb, dph, dpg, join.token)
    g_w_out, g_w_ff1, g_w_ff2 = join.wait(g_in_local)

    late = (0,)
    swap = _swap_halves(late, (g_in_local,))
    pack = jnp.concatenate([
        g_ln1_g, g_ln1_b, g_ln2_g, g_ln2_b,
        jnp.concatenate([g_lbl[0:1], g_lbl[1:2]], axis=1),
        jnp.concatenate([g_gnw, g_conv[0:1]], axis=1),
        jnp.concatenate([g_conv[1:2], g_conv[2:3]], axis=1),
        jnp.concatenate([loss8[0:1], jnp.zeros((1, D_MODEL - LANES), F32)], axis=1)], axis=0)
    tot = _sum_small(pack, swap.token)
    loss = tot[7, 0]
    half = D_MODEL // 2
    g_lb_logits = jnp.concatenate([tot[4:5, :half], tot[4:5, half:]], axis=0)
    g_gate_norm_w = tot[5:6, :half]
    g_conv_full = jnp.concatenate([tot[5:6, half:], tot[6:7, :half], tot[6:7, half:]], axis=0)
    g_conv_w = lax.dynamic_slice(g_conv_full, (0, chip * LANES), (3, LANES))
    swapped = swap.wait(tot)
    exch = _exchange_pieces(late, add_halves(late, swapped[:1], swapped[1:]))
    d_ff1, nm_ff1, nv_ff1 = _adamw("adamw_w_ff1", w_ff1[0], g_w_ff1, m_w_ff1[0], v_w_ff1[0], exch.token)
    d_ff2, nm_ff2, nv_ff2 = _adamw("adamw_w_ff2", w_ff2[0], g_w_ff2, m_w_ff2[0], v_w_ff2[0], d_ff1)
    d_out, nm_out, nv_out = _adamw("adamw_w_out", w_out[0], g_w_out, m_w_out[0], v_w_out[0], d_ff2)

    def small_pack(lbl, gnw, cv, l1g, l1b, l2g, l2b):
        pad = jnp.zeros((1, D_MODEL - 3 * LANES), F32)
        return jnp.concatenate([
            l1g, l1b, l2g, l2b, jnp.concatenate([lbl[0:1], lbl[1:2]], axis=1),
            jnp.concatenate([gnw, jnp.zeros((1, half), F32)], axis=1),
            jnp.concatenate([cv[0:1], cv[1:2], cv[2:3], pad], axis=1), jnp.zeros((1, D_MODEL), F32)], axis=0)

    w_s = small_pack(lb_logits, gate_norm_w, conv_w[0], ln1_g, ln1_b, ln2_g, ln2_b)
    g_s = small_pack(g_lb_logits, g_gate_norm_w, g_conv_w, tot[0:1], tot[1:2], tot[2:3], tot[3:4])
    m_s = small_pack(m_lb_logits, m_gate_norm_w, m_conv_w[0], m_ln1_g, m_ln1_b, m_ln2_g, m_ln2_b)
    v_s = small_pack(v_lb_logits, v_gate_norm_w, v_conv_w[0], v_ln1_g, v_ln1_b, v_ln2_g, v_ln2_b)
    d_s, nm_s, nv_s = _adamw("adamw_small", w_s, g_s, m_s, v_s, d_out)
    exchanged = exch.wait(d_s)
    join = _join_halves(late, sum_pieces(late, exchanged[:1], exchanged[1:]))
    g_w_in, = join.wait(join.token)
    d_in, nm_in, nv_in = _adamw("adamw_w_in", w_in[0], g_w_in, m_w_in[0], v_w_in[0])

    def unpack(p):
        lbl = jnp.concatenate([p[4:5, :half], p[4:5, half:]], axis=0)
        cv = jnp.concatenate([p[6:7, 0:LANES], p[6:7, LANES:2 * LANES], p[6:7, 2 * LANES:3 * LANES]], axis=0)
        return dict(lb_logits=lbl, gate_norm_w=p[5:6, :half], conv_w=cv[None], ln1_g=p[0:1], ln1_b=p[1:2],
                    ln2_g=p[2:3], ln2_b=p[3:4])

    order = ("w_in", "lb_logits", "gate_norm_w", "conv_w", "w_out", "ln1_g", "ln1_b", "w_ff1", "w_ff2", "ln2_g", "ln2_b")
    grad = dict(unpack(g_s), w_in=g_w_in[None], w_out=g_w_out[None], w_ff1=g_w_ff1[None], w_ff2=g_w_ff2[None])
    delta = dict(unpack(d_s), w_in=d_in[None], w_out=d_out[None], w_ff1=d_ff1[None], w_ff2=d_ff2[None])
    new_m = dict(unpack(nm_s), w_in=nm_in[None], w_out=nm_out[None], w_ff1=nm_ff1[None], w_ff2=nm_ff2[None])
    new_v = dict(unpack(nv_s), w_in=nv_in[None], w_out=nv_out[None], w_ff1=nv_ff1[None], w_ff2=nv_ff2[None])
    return (loss, grad_x[None], *[grad[n] for n in order], *[delta[n] for n in order],
            *[new_m[n] for n in order], *[new_v[n] for n in order])
```

```python
import jax
import jax.numpy as jnp
from jax import lax
from jax.experimental import pallas as pl
from jax.experimental.pallas import tpu as pltpu

F32 = jnp.float32
BF16 = jnp.bfloat16
MXU_DTYPE = jnp.bfloat16

D_MODEL = 1024
HGRN_WIDTH = 512
HEAD_DIM = 128
N_HEADS = 4
CONV_WIDTH = 512
CHUNK = 64
D_FF = 4096
IN_COLS = 3584
GROUP = 512
N_GROUPS = IN_COLS // GROUP
ALPHA = 2.0 ** 0.25
EPS = 1e-5
N_CHIPS = 4
ADAM_LR, ADAM_B1, ADAM_B2, ADAM_EPS, ADAM_WD, ADAM_STEP = 0.001, 0.9, 0.999, 1e-08, 0.01, 10

LANES = 128
SUBLANES = 8
VMEM_LIMIT = 56 * 1024 * 1024
FF_BLOCK = 1024
NN = (((1,), (0,)), ((), ()))
NT = (((1,), (1,)), ((), ()))
TN = (((0,), (0,)), ((), ()))
MESH = pl.DeviceIdType.MESH
ANY = pl.BlockSpec(memory_space=pl.ANY)


def _dot(a, b, dims):
    return lax.dot_general(a.astype(MXU_DTYPE), b.astype(MXU_DTYPE), dims, preferred_element_type=F32)


def _dot_exact(ones, v):
    ones = ones.astype(jnp.bfloat16)
    hi = v.astype(jnp.bfloat16)
    rest = v - hi.astype(F32)
    mid = rest.astype(jnp.bfloat16)
    low = (rest - mid.astype(F32)).astype(jnp.bfloat16)
    return sum(lax.dot_general(ones, part, NN, preferred_element_type=F32) for part in (hi, mid, low))


def _params(*sem):
    return pltpu.CompilerParams(dimension_semantics=sem, vmem_limit_bytes=VMEM_LIMIT)


def _resident(shape):
    return pl.BlockSpec(shape, lambda *_: (0,) * len(shape), pipeline_mode=pl.Buffered(1))


def _sigmoid(v):
    return 1.0 / (1.0 + jnp.exp(-v))


def _lower_bound(lbl):
    m = jnp.max(lbl, axis=0, keepdims=True)
    e = jnp.exp(lbl - m)
    s = e / jnp.sum(e, axis=0, keepdims=True)
    return s[0:1, :], s[1:2, :]


def _heads(v):
    return [v[:, h * HEAD_DIM:(h + 1) * HEAD_DIM] for h in range(N_HEADS)]


def _per_head(fn, *arrays):
    return jnp.concatenate([fn(*parts) for parts in zip(*map(_heads, arrays))], axis=1)


def _in_proj(x, w_in, after):
    t = x.shape[0]
    tm = min(t, 512)

    def body(x_ref, w_ref, after_ref, o_ref, xb_ref):
        xb = x_ref[...].astype(xb_ref.dtype)
        xb_ref[...] = xb
        for g in range(N_GROUPS):
            cols = slice(g * GROUP, (g + 1) * GROUP)
            o_ref[:, cols] = _dot(xb, w_ref[:, cols], NN)

    return pl.pallas_call(
        body, name="in_proj", grid=(t // tm,),
        in_specs=[pl.BlockSpec((tm, D_MODEL), lambda i: (i, 0)), _resident((D_MODEL, IN_COLS)), ANY],
        out_specs=[pl.BlockSpec((tm, IN_COLS), lambda i: (i, 0)), pl.BlockSpec((tm, D_MODEL), lambda i: (i, 0))],
        out_shape=[jax.ShapeDtypeStruct((t, IN_COLS), F32), jax.ShapeDtypeStruct((t, D_MODEL), BF16)],
        compiler_params=_params("parallel"),
    )(x, w_in, after)


def _gates(fp, lb):
    sig = _sigmoid(fp)
    f = lb + (1.0 - lb) * sig
    return sig, f, jnp.log(f), 1.0 - f


def _chunk_masks():
    row = lax.broadcasted_iota(jnp.int32, (CHUNK, CHUNK), 0)
    col = lax.broadcasted_iota(jnp.int32, (CHUNK, CHUNK), 1)
    return row >= col, row <= col


def _hgrn_fwd(proj, lb_logits):
    t = proj.shape[0]
    tb = min(t, 512)
    ncb = tb // CHUNK

    def body(q_ref, f_ref, v_ref, lbl_ref, o_ref, st_ref, s_scr):
        @pl.when(pl.program_id(0) == 0)
        def _():
            s_scr[...] = jnp.zeros_like(s_scr)

        lb, _ = _lower_bound(lbl_ref[...])
        causal, _ = _chunk_masks()

        every = range(ncb)
        rows = [slice(c * CHUNK, (c + 1) * CHUNK) for c in every]
        q, v = [q_ref[r, :] for r in rows], [v_ref[r, :] for r in rows]
        gates = [_gates(f_ref[r, :], lb) for r in rows]
        k = [gt[3] for gt in gates]
        b = [_dot_exact(causal, gt[2]) for gt in gates]
        mid, last = [x[CHUNK // 2:CHUNK // 2 + 1, :] for x in b], [x[CHUNK - 1:CHUNK, :] for x in b]
        qt = [q[c] * jnp.exp(b[c] - mid[c]) for c in every]
        kt = [k[c] * jnp.exp(mid[c] - b[c]) for c in every]
        qi = [q[c] * jnp.exp(b[c]) for c in every]
        ks = [k[c] * jnp.exp(last[c] - b[c]) for c in every]
        dec = [jnp.exp(x) for x in last]
        scores = [[jnp.where(causal, _dot(a, b_, NT), 0.0) for a, b_ in zip(_heads(qt[c]), _heads(kt[c]))] for c in every]
        intra = [[_dot(s, v_h, NN) for s, v_h in zip(scores[c], _heads(v[c]))] for c in every]
        update = [_per_head(lambda v_h, ks_h: _dot(v_h, ks_h, TN), v[c], ks[c]) for c in every]

        st = s_scr[...]
        states = []
        for c in every:
            states.append(st)
            st_ref[c] = st
            st = dec[c] * st + update[c]
        s_scr[...] = st

        o_ref[...] = jnp.concatenate(
            [jnp.concatenate([i_h + _dot(qi_h, st_h, NT) for i_h, qi_h, st_h in
                              zip(intra[c], _heads(qi[c]), _heads(states[c]))], axis=1) for c in every], axis=0)

    grp = lambda g: pl.BlockSpec((tb, HGRN_WIDTH), lambda i: (i, g))
    return pl.pallas_call(
        body, name="hgrn_fwd", grid=(t // tb,),
        in_specs=[grp(0), grp(1), grp(2), pl.BlockSpec((2, HGRN_WIDTH), lambda i: (0, 0))],
        out_specs=[grp(0), pl.BlockSpec((ncb, HEAD_DIM, HGRN_WIDTH), lambda i: (i, 0, 0))],
        out_shape=[jax.ShapeDtypeStruct((t, HGRN_WIDTH), F32),
                   jax.ShapeDtypeStruct((t // CHUNK, HEAD_DIM, HGRN_WIDTH), F32)],
        scratch_shapes=[pltpu.VMEM((HEAD_DIM, HGRN_WIDTH), F32)],
        compiler_params=_params("arbitrary"),
    )(proj, proj, proj, lb_logits)


def _conv_taps(z, halo, zbuf, tb):
    zbuf[0:SUBLANES, :] = halo
    zbuf[SUBLANES:SUBLANES + tb, :] = z
    return zbuf[SUBLANES - 1:SUBLANES - 1 + tb, :], zbuf[SUBLANES - 2:SUBLANES - 2 + tb, :]


def _gate_fwd(proj, o, gate_norm_w, conv_w, after):
    t = proj.shape[0]
    tb = min(t, 512)
    hb = tb // SUBLANES

    def body(o_ref, og_ref, gnw_ref, b_ref, c_ref, u_ref, ch_ref, uh_ref, cw_ref, after_ref, cat_ref, zbuf):
        i = pl.program_id(0)
        og = og_ref[...]
        on = _per_head(lambda o_h: o_h * lax.rsqrt(jnp.mean(o_h * o_h, axis=-1, keepdims=True) + EPS), o_ref[...])
        cat_ref[0] = (on * gnw_ref[...] * (og * _sigmoid(og))).astype(cat_ref.dtype)
        z = c_ref[...] * u_ref[...]
        halo = jnp.where(i > 0, ch_ref[...] * uh_ref[...], 0.0)
        z1, z2 = _conv_taps(z, halo, zbuf, tb)
        cw = cw_ref[...]
        yc = cw[2:3, :] * z + cw[1:2, :] * z1 + cw[0:1, :] * z2
        cat_ref[1] = (b_ref[...] * yc).astype(cat_ref.dtype)

    grp = lambda g: pl.BlockSpec((tb, GROUP), lambda i: (i, g))
    prev = lambda g: pl.BlockSpec((SUBLANES, GROUP), lambda i: (jnp.maximum(i * hb - 1, 0), g))
    vec = lambda r: pl.BlockSpec((r, GROUP), lambda i: (0, 0))
    return pl.pallas_call(
        body, name="gate_fwd", grid=(t // tb,),
        in_specs=[grp(0), grp(3), vec(1), grp(4), grp(5), grp(6), prev(5), prev(6), vec(3), ANY],
        out_specs=pl.BlockSpec((2, tb, GROUP), lambda i: (0, i, 0)),
        out_shape=jax.ShapeDtypeStruct((2, t, HGRN_WIDTH), BF16),
        scratch_shapes=[pltpu.VMEM((tb + SUBLANES, GROUP), F32)],
        compiler_params=_params("parallel"),
    )(o, proj, gate_norm_w, proj, proj, proj, proj, proj, conv_w, after)


def _out_ln1(cat, w_out, x, g1, b1):
    t = x.shape[0]
    tm = min(t, 512)

    def body(cat_ref, w_ref, x_ref, g_ref, b_ref, xhat_ref, h1_ref, rstd_ref):
        mix = _dot(cat_ref[0], w_ref[0:GROUP, :], NN) + _dot(cat_ref[1], w_ref[GROUP:2 * GROUP, :], NN)
        pre = ALPHA * x_ref[...] + mix
        xc = pre - jnp.mean(pre, axis=-1, keepdims=True)
        rstd = lax.rsqrt(jnp.mean(xc * xc, axis=-1, keepdims=True) + EPS)
        xhat = xc * rstd
        xhat_ref[...] = xhat
        h1_ref[...] = (xhat * g_ref[...] + b_ref[...]).astype(h1_ref.dtype)
        rstd_ref[...] = rstd

    row = pl.BlockSpec((tm, D_MODEL), lambda i: (i, 0))
    vec = pl.BlockSpec((1, D_MODEL), lambda i: (0, 0))
    return pl.pallas_call(
        body, name="out_ln1", grid=(t // tm,),
        in_specs=[pl.BlockSpec((2, tm, GROUP), lambda i: (0, i, 0)), _resident((D_MODEL, D_MODEL)), row, vec, vec],
        out_specs=[row, row, pl.BlockSpec((tm, 1), lambda i: (i, 0))],
        out_shape=[jax.ShapeDtypeStruct((t, D_MODEL), F32), jax.ShapeDtypeStruct((t, D_MODEL), BF16),
                   jax.ShapeDtypeStruct((t, 1), F32)],
        compiler_params=_params("parallel"),
    )(cat, w_out, x, g1, b1)


def _ln_bwd(dy, xhat, rstd, g):
    dxhat = dy * g
    m1 = jnp.mean(dxhat, axis=-1, keepdims=True)
    m2 = jnp.mean(dxhat * xhat, axis=-1, keepdims=True)
    return rstd * (dxhat - m1 - xhat * m2)


def _mlp_fwd(xhat1, g1, b1, w_ff1, w_ff2, g2, b2, target):
    t = xhat1.shape[0]
    tm = min(t, 256)

    def body(xh_ref, g1_ref, b1_ref, w1_ref, w2_ref, g2_ref, b2_ref, tg_ref,
             r_ref, dpre_ref, dpreb_ref, dg_ref, db_ref, loss_ref):
        @pl.when(pl.program_id(0) == 0)
        def _():
            dg_ref[...] = jnp.zeros_like(dg_ref)
            db_ref[...] = jnp.zeros_like(db_ref)
            loss_ref[...] = jnp.zeros_like(loss_ref)

        h1 = xh_ref[...] * g1_ref[...] + b1_ref[...]
        h1b = h1.astype(MXU_DTYPE)
        mlp = jnp.zeros((tm, D_MODEL), F32)
        for j in range(D_FF // FF_BLOCK):
            cols = slice(j * FF_BLOCK, (j + 1) * FF_BLOCK)
            r = jnp.square(jnp.maximum(_dot(h1b, w1_ref[:, cols], NN), 0.0)).astype(r_ref.dtype)
            r_ref[:, cols] = r
            mlp = mlp + _dot(r, w2_ref[cols, :], NN)
        pre = ALPHA * h1 + mlp
        xc = pre - jnp.mean(pre, axis=-1, keepdims=True)
        rstd = lax.rsqrt(jnp.mean(xc * xc, axis=-1, keepdims=True) + EPS)
        xhat = xc * rstd
        err = xhat * g2_ref[...] + b2_ref[...] - tg_ref[...]
        loss_ref[...] += 0.5 * jnp.sum(jnp.mean(err * err, axis=-1, keepdims=True))
        dy = err * (1.0 / D_MODEL)
        dg_ref[...] += jnp.sum(dy * xhat, axis=0, keepdims=True)
        db_ref[...] += jnp.sum(dy, axis=0, keepdims=True)
        dpre = _ln_bwd(dy, xhat, rstd, g2_ref[...])
        dpre_ref[...] = dpre
        dpreb_ref[...] = dpre.astype(dpreb_ref.dtype)

    row = pl.BlockSpec((tm, D_MODEL), lambda i: (i, 0))
    vec = pl.BlockSpec((1, D_MODEL), lambda i: (0, 0))
    return pl.pallas_call(
        body, name="mlp_fwd", grid=(t // tm,),
        in_specs=[row, vec, vec, _resident((D_MODEL, D_FF)), _resident((D_FF, D_MODEL)), vec, vec, row],
        out_specs=[pl.BlockSpec((tm, D_FF), lambda i: (i, 0)), row, row, vec, vec,
                   pl.BlockSpec((SUBLANES, LANES), lambda i: (0, 0))],
        out_shape=[jax.ShapeDtypeStruct((t, D_FF), BF16), jax.ShapeDtypeStruct((t, D_MODEL), F32),
                   jax.ShapeDtypeStruct((t, D_MODEL), BF16), jax.ShapeDtypeStruct((1, D_MODEL), F32),
                   jax.ShapeDtypeStruct((1, D_MODEL), F32), jax.ShapeDtypeStruct((SUBLANES, LANES), F32)],
        compiler_params=_params("arbitrary"),
    )(xhat1, g1, b1, w_ff1, w_ff2, g2, b2, target)


def _mlp_bwd(dpre2, r, w_ff1, w_ff2, xhat1, rstd1, g1):
    t = r.shape[0]
    tm = min(t, 256)

    def body(dp2_ref, r_ref, w1_ref, w2_ref, xh_ref, rs_ref, g_ref, da_ref, dpre_ref, dpreb_ref, dg_ref, db_ref):
        @pl.when(pl.program_id(0) == 0)
        def _():
            dg_ref[...] = jnp.zeros_like(dg_ref)
            db_ref[...] = jnp.zeros_like(db_ref)

        dp2 = dp2_ref[...]
        dp2b = dp2.astype(MXU_DTYPE)
        back = jnp.zeros((tm, D_MODEL), F32)
        for j in range(D_FF // FF_BLOCK):
            cols = slice(j * FF_BLOCK, (j + 1) * FF_BLOCK)
            dr = _dot(dp2b, w2_ref[cols, :], NT)
            da = (dr * (2.0 * jnp.sqrt(r_ref[:, cols].astype(F32)))).astype(da_ref.dtype)
            da_ref[:, cols] = da
            back = back + _dot(da, w1_ref[:, cols], NT)
        dh1 = ALPHA * dp2 + back
        xhat = xh_ref[...]
        dg_ref[...] += jnp.sum(dh1 * xhat, axis=0, keepdims=True)
        db_ref[...] += jnp.sum(dh1, axis=0, keepdims=True)
        dpre = _ln_bwd(dh1, xhat, rs_ref[...], g_ref[...])
        dpre_ref[...] = dpre
        dpreb_ref[...] = dpre.astype(dpreb_ref.dtype)

    row = pl.BlockSpec((tm, D_MODEL), lambda i: (i, 0))
    wide = pl.BlockSpec((tm, D_FF), lambda i: (i, 0))
    vec = pl.BlockSpec((1, D_MODEL), lambda i: (0, 0))
    return pl.pallas_call(
        body, name="mlp_bwd", grid=(t // tm,),
        in_specs=[row, wide, _resident((D_MODEL, D_FF)), _resident((D_FF, D_MODEL)), row,
                  pl.BlockSpec((tm, 1), lambda i: (i, 0)), vec],
        out_specs=[wide, row, row, vec, vec],
        out_shape=[jax.ShapeDtypeStruct((t, D_FF), BF16), jax.ShapeDtypeStruct((t, D_MODEL), F32),
                   jax.ShapeDtypeStruct((t, D_MODEL), BF16), jax.ShapeDtypeStruct((1, D_MODEL), F32),
                   jax.ShapeDtypeStruct((1, D_MODEL), F32)],
        compiler_params=_params("arbitrary"),
    )(dpre2, r, w_ff1, w_ff2, xhat1, rstd1, g1)


def _out_bwd(dpre1b, w_out):
    t = dpre1b.shape[0]
    tm = min(t, 512)

    def body(d_ref, w_ref, o_ref):
        o_ref[...] = _dot(d_ref[...], w_ref[...], NT)

    return pl.pallas_call(
        body, name="out_bwd", grid=(t // tm,),
        in_specs=[pl.BlockSpec((tm, D_MODEL), lambda i: (i, 0)), _resident((D_MODEL, D_MODEL))],
        out_specs=pl.BlockSpec((tm, D_MODEL), lambda i: (i, 0)),
        out_shape=jax.ShapeDtypeStruct((t, D_MODEL), F32),
        compiler_params=_params("parallel"),
    )(dpre1b, w_out)


def _gate_bwd(dcat, o, proj, gate_norm_w, conv_w, after):
    t = proj.shape[0]
    tb = min(t, 512)
    hb = tb // SUBLANES
    nblk = t // tb

    def body(do2_ref, dy_ref, dyn_ref, o_ref, og_ref, gnw_ref, b_ref, bn_ref, c_ref, u_ref, ch_ref, uh_ref, cw_ref,
             after_ref, do_ref, dp_ref, dgnw_ref, dcw_ref, zbuf, dbuf):
        i = pl.program_id(0)

        @pl.when(i == 0)
        def _():
            dgnw_ref[...] = jnp.zeros_like(dgnw_ref)
            dcw_ref[...] = jnp.zeros_like(dcw_ref)

        ov, og, gnw, do2 = o_ref[...], og_ref[...], gnw_ref[...], do2_ref[...]
        rs = _per_head(lambda o_h: jnp.broadcast_to(lax.rsqrt(jnp.mean(o_h * o_h, axis=-1, keepdims=True) + EPS),
                                                    o_h.shape), ov)
        on = ov * rs
        sg = _sigmoid(og)
        sil = og * sg
        don = do2 * gnw * sil
        dgnw_ref[...] += jnp.sum(do2 * on * sil, axis=0, keepdims=True)
        dp_ref[0] = (do2 * on * gnw * (sg * (1.0 + og * (1.0 - sg)))).astype(dp_ref.dtype)
        do_ref[...] = rs * (don - on * _per_head(
            lambda p_h: jnp.broadcast_to(jnp.mean(p_h, axis=-1, keepdims=True), p_h.shape), don * on))

        bg, cg, u, dy = b_ref[...], c_ref[...], u_ref[...], dy_ref[...]
        z = cg * u
        halo = jnp.where(i > 0, ch_ref[...] * uh_ref[...], 0.0)
        z1, z2 = _conv_taps(z, halo, zbuf, tb)
        cw = cw_ref[...]
        yc = cw[2:3, :] * z + cw[1:2, :] * z1 + cw[0:1, :] * z2
        dyc = dy * bg
        dbuf[0:tb, :] = dyc
        dbuf[tb:tb + SUBLANES, :] = jnp.where(i < nblk - 1, dyn_ref[...] * bn_ref[...], 0.0)
        d1, d2 = dbuf[1:1 + tb, :], dbuf[2:2 + tb, :]
        dz = cw[2:3, :] * dyc + cw[1:2, :] * d1 + cw[0:1, :] * d2
        dp_ref[1] = (dy * yc).astype(dp_ref.dtype)
        dp_ref[2] = (dz * u).astype(dp_ref.dtype)
        dp_ref[3] = (dz * cg).astype(dp_ref.dtype)
        dcw_ref[0:1, :] += jnp.sum(dyc * z2, axis=0, keepdims=True)
        dcw_ref[1:2, :] += jnp.sum(dyc * z1, axis=0, keepdims=True)
        dcw_ref[2:3, :] += jnp.sum(dyc * z, axis=0, keepdims=True)

    grp = lambda g: pl.BlockSpec((tb, GROUP), lambda i: (i, g))
    prev = lambda g: pl.BlockSpec((SUBLANES, GROUP), lambda i: (jnp.maximum(i * hb - 1, 0), g))
    nxt = lambda g: pl.BlockSpec((SUBLANES, GROUP), lambda i: (jnp.minimum((i + 1) * hb, t // SUBLANES - 1), g))
    vec = lambda r: pl.BlockSpec((r, GROUP), lambda i: (0, 0))
    return pl.pallas_call(
        body, name="gate_bwd", grid=(nblk,),
        in_specs=[grp(0), grp(1), nxt(1), grp(0), grp(3), vec(1), grp(4), nxt(4), grp(5), grp(6), prev(5), prev(6),
                  vec(3), ANY],
        out_specs=[grp(0), pl.BlockSpec((4, tb, GROUP), lambda i: (0, i, 0)), vec(1), vec(3)],
        out_shape=[jax.ShapeDtypeStruct((t, HGRN_WIDTH), F32), jax.ShapeDtypeStruct((4, t, HGRN_WIDTH), BF16),
                   jax.ShapeDtypeStruct((1, HGRN_WIDTH), F32), jax.ShapeDtypeStruct((3, CONV_WIDTH), F32)],
        scratch_shapes=[pltpu.VMEM((tb + SUBLANES, GROUP), F32), pltpu.VMEM((tb + SUBLANES, GROUP), F32)],
        compiler_params=_params("arbitrary"),
    )(dcat, dcat, dcat, o, proj, gate_norm_w, proj, proj, proj, proj, proj, proj, conv_w, after)


def _hgrn_bwd(proj, do, states, lb_logits, after):
    t = proj.shape[0]
    tb = min(t, 512)
    ncb = tb // CHUNK
    nblk = t // tb

    def body(q_ref, f_ref, v_ref, do_ref, st_ref, lbl_ref, after_ref, dp_ref, dlbl_ref, ds_scr, dlb_scr):
        i = pl.program_id(0)

        @pl.when(i == 0)
        def _():
            ds_scr[...] = jnp.zeros_like(ds_scr)
            dlb_scr[...] = jnp.zeros_like(dlb_scr)

        lb, s1 = _lower_bound(lbl_ref[...])
        causal, anti = _chunk_masks()
        every = range(ncb)
        rows = [slice(c * CHUNK, (c + 1) * CHUNK) for c in every]
        q, v, do = ([ref[r, :] for r in rows] for ref in (q_ref, v_ref, do_ref))
        st = [st_ref[c] for c in every]
        gates = [_gates(f_ref[r, :], lb) for r in rows]
        sig, f, k = ([gt[n] for gt in gates] for n in (0, 1, 3))
        b = [_dot_exact(causal, gt[2]) for gt in gates]
        mid, last = [x[CHUNK // 2:CHUNK // 2 + 1, :] for x in b], [x[CHUNK - 1:CHUNK, :] for x in b]
        e_q = [jnp.exp(b[c] - mid[c]) for c in every]
        e_k = [jnp.exp(mid[c] - b[c]) for c in every]
        e_i = [jnp.exp(x) for x in b]
        e_s = [jnp.exp(last[c] - b[c]) for c in every]
        dec = [jnp.exp(x) for x in last]
        qt, kt, qi, ks = ([a[c] * e[c] for c in every] for a, e in ((q, e_q), (k, e_k), (q, e_i), (k, e_s)))

        def masked(a, b_):
            return [[jnp.where(causal, _dot(a_h, b_h, NT), 0.0) for a_h, b_h in zip(_heads(a[c]), _heads(b_[c]))]
                    for c in every]

        def with_scores(s, other, dims):
            return [jnp.concatenate([_dot(s_h, o_h, dims) for s_h, o_h in zip(s[c], _heads(other[c]))], axis=1)
                    for c in every]

        def per_head(dims, a, b_):
            return [_per_head(lambda a_h, b_h: _dot(a_h, b_h, dims), a[c], b_[c]) for c in every]

        scores, dscores = masked(qt, kt), masked(do, v)
        dqt, dkt, dv_intra = with_scores(dscores, kt, NN), with_scores(dscores, qt, TN), with_scores(scores, do, TN)
        dqi, update = per_head(NN, do, st), per_head(TN, do, qi)

        dst = ds_scr[...]
        dsts = [None] * ncb
        for c in reversed(every):
            dsts[c] = dst
            dst = dec[c] * dst + update[c]
        ds_scr[...] = dst

        dv_state, dks = per_head(NT, ks, dsts), per_head(NN, v, dsts)
        ddec = [jnp.sum(dsts[c] * st[c], axis=0, keepdims=True) for c in every]
        dq = [dqt[c] * e_q[c] + dqi[c] * e_i[c] for c in every]
        dk = [dkt[c] * e_k[c] + dks[c] * e_s[c] for c in every]
        db = [q[c] * dq[c] - k[c] * dk[c] for c in every]
        db_last = [jnp.sum(dks[c] * ks[c], axis=0, keepdims=True) + ddec[c] * dec[c] for c in every]
        dg = [_dot_exact(anti, db[c]) + db_last[c] for c in every]
        df = [dg[c] / f[c] - dk[c] for c in every]
        dlb_scr[...] += sum(jnp.sum(df[c] * (1.0 - sig[c]), axis=0, keepdims=True) for c in every)
        dfp = [df[c] * (1.0 - lb) * sig[c] * (1.0 - sig[c]) for c in every]
        dv = [dv_intra[c] + dv_state[c] for c in every]
        for n, parts in enumerate((dq, dfp, dv)):
            dp_ref[n] = jnp.concatenate(parts, axis=0).astype(dp_ref.dtype)

        @pl.when(i == nblk - 1)
        def _():
            dlb = dlb_scr[...]
            dlbl_ref[0:1, :] = dlb * lb * (1.0 - lb)
            dlbl_ref[1:2, :] = -dlb * lb * s1

    grp = lambda g: pl.BlockSpec((tb, HGRN_WIDTH), lambda i: (nblk - 1 - i, g))
    vec = pl.BlockSpec((2, HGRN_WIDTH), lambda i: (0, 0))
    return pl.pallas_call(
        body, name="hgrn_bwd", grid=(nblk,),
        in_specs=[grp(0), grp(1), grp(2), grp(0),
                  pl.BlockSpec((ncb, HEAD_DIM, HGRN_WIDTH), lambda i: (nblk - 1 - i, 0, 0)), vec, ANY],
        out_specs=[pl.BlockSpec((3, tb, HGRN_WIDTH), lambda i: (0, nblk - 1 - i, 0)), vec],
        out_shape=[jax.ShapeDtypeStruct((3, t, HGRN_WIDTH), BF16), jax.ShapeDtypeStruct((2, HGRN_WIDTH), F32)],
        scratch_shapes=[pltpu.VMEM((HEAD_DIM, HGRN_WIDTH), F32), pltpu.VMEM((1, HGRN_WIDTH), F32)],
        compiler_params=_params("arbitrary"),
    )(proj, proj, proj, do, states, lb_logits, after)


def _in_bwd(dph, dpg, w_in, dpre1):
    t = dpre1.shape[0]
    tm = min(t, 512)

    def body(dh_ref, dg_ref, w_ref, dp_ref, o_ref):
        acc = ALPHA * dp_ref[...]
        for g in range(N_GROUPS):
            part = dh_ref[g] if g < 3 else dg_ref[g - 3]
            acc = acc + _dot(part, w_ref[:, g * GROUP:(g + 1) * GROUP], NT)
        o_ref[...] = acc

    row = pl.BlockSpec((tm, D_MODEL), lambda i: (i, 0))
    return pl.pallas_call(
        body, name="in_bwd", grid=(t // tm,),
        in_specs=[pl.BlockSpec((3, tm, GROUP), lambda i: (0, i, 0)), pl.BlockSpec((4, tm, GROUP), lambda i: (0, i, 0)),
                  _resident((D_MODEL, IN_COLS)), row],
        out_specs=row,
        out_shape=jax.ShapeDtypeStruct((t, D_MODEL), F32),
        compiler_params=_params("parallel"),
    )(dph, dpg, w_in, dpre1)


def _grad_w(name, operands, widths, shape, step, after=None):
    t = operands[0].shape[-2]
    tt = min(t, 512)
    n_in, n_steps = len(operands), t // tt
    in_specs = [pl.BlockSpec((tt, w), lambda k: (k, 0)) if a.ndim == 2 else
                pl.BlockSpec((a.shape[0], tt, w), lambda k: (0, k, 0)) for a, w in zip(operands, widths)]
    extra = [] if after is None else [after]

    def body(*refs):
        o_ref, acc, sem = refs[-3:]
        k = pl.program_id(0)

        @pl.when(k == 0)
        def _():
            acc[...] = jnp.zeros_like(acc)

        step(acc, *refs[:n_in])

        @pl.when(k == n_steps - 1)
        def _():
            out = pltpu.make_async_copy(acc, o_ref, sem)
            out.start()
            out.wait()

    return pl.pallas_call(
        body, name=name, grid=(n_steps,), in_specs=in_specs + [ANY] * len(extra), out_specs=ANY,
        out_shape=jax.ShapeDtypeStruct(shape, F32),
        scratch_shapes=[pltpu.VMEM(shape, F32), pltpu.SemaphoreType.DMA],
        compiler_params=_params("arbitrary"),
    )(*operands, *extra)


def _dw_in(xb, dph, dpg, after):
    def step(acc, x_ref, dh_ref, dg_ref):
        xv = x_ref[...]
        for g in range(N_GROUPS):
            part = dh_ref[g] if g < 3 else dg_ref[g - 3]
            acc[:, g * GROUP:(g + 1) * GROUP] += _dot(xv, part, TN)

    return _grad_w("dw_in", (xb, dph, dpg), (D_MODEL, GROUP, GROUP), (D_MODEL, IN_COLS), step, after)


def _dw_out(cat, dpre1b):
    def step(acc, cat_ref, d_ref):
        dv = d_ref[...]
        for g in range(2):
            acc[g * GROUP:(g + 1) * GROUP, :] += _dot(cat_ref[g], dv, TN)

    return _grad_w("dw_out", (cat, dpre1b), (GROUP, D_MODEL), (D_MODEL, D_MODEL), step)


def _dw_ff1(h1b, da):
    def step(acc, h_ref, da_ref):
        hv = h_ref[...]
        for j in range(D_FF // FF_BLOCK):
            cols = slice(j * FF_BLOCK, (j + 1) * FF_BLOCK)
            acc[:, cols] += _dot(hv, da_ref[:, cols], TN)

    return _grad_w("dw_ff1", (h1b, da), (D_MODEL, D_FF), (D_MODEL, D_FF), step)


def _dw_ff2(r, dpre2b):
    def step(acc, r_ref, d_ref):
        dv = d_ref[...]
        for j in range(D_FF // FF_BLOCK):
            rows = slice(j * FF_BLOCK, (j + 1) * FF_BLOCK)
            acc[rows, :] += _dot(r_ref[:, rows], dv, TN)

    return _grad_w("dw_ff2", (r, dpre2b), (D_FF, D_MODEL), (D_FF, D_MODEL), step)


def _place():
    x, y, c = lax.axis_index("x"), lax.axis_index("y"), lax.axis_index("c")
    return x, y, c, 2 * x + y


def _other_chips(x, y):
    return [(1 - x, y), (x, 1 - y), (1 - x, 1 - y)]


def _place_shard(name, w, chip, cols_sharded, after=None):
    rows, cols = w.shape
    tr = min(rows, 256)
    nb = rows // tr
    full = (rows, cols * N_CHIPS) if cols_sharded else (rows * N_CHIPS, cols)
    out_map = (lambda i, s: (i, s[0])) if cols_sharded else (lambda i, s: (s[0] * nb + i, 0))

    def body(s_ref, w_ref, *rest):
        rest[-1][...] = w_ref[...].astype(rest[-1].dtype)

    extra = [] if after is None else [after]
    return pl.pallas_call(
        body, name=name,
        grid_spec=pltpu.PrefetchScalarGridSpec(
            num_scalar_prefetch=1, grid=(nb,),
            in_specs=[pl.BlockSpec((tr, cols), lambda i, s: (i, 0))] + [ANY] * len(extra),
            out_specs=pl.BlockSpec((tr, cols), out_map)),
        out_shape=jax.ShapeDtypeStruct(full, BF16),
        compiler_params=_params("parallel"),
    )(chip, w, *extra)


def _gather_w_in(w_in, conv_w):
    half, cs, n_p = D_MODEL // 2, IN_COLS // N_CHIPS, 3

    def body(w_alias, cv_ref, w_ref, cvf_ref, send_sems, recv_sems, local_sem):
        x, y, c, me = _place()
        sibling = (x, y, 1 - c)
        chips = _other_chips(x, y)
        blk = lambda chip, h: w_ref.at[pl.ds(h * half, half), pl.ds(chip * cs, cs)]

        def copy(k, src, dst, to):
            return pltpu.make_async_remote_copy(src_ref=src, dst_ref=dst, send_sem=send_sems.at[k],
                                                recv_sem=recv_sems.at[k], device_id=to, device_id_type=MESH)

        own_cv = pltpu.make_async_copy(cv_ref, cvf_ref.at[me], local_sem)
        own_cv.start()
        first = [copy(j, blk(me, c), blk(me, c), (px, py, c)) for j, (px, py) in enumerate(chips)]
        first += [copy(2 * n_p + j, cv_ref, cvf_ref.at[me], (px, py, c)) for j, (px, py) in enumerate(chips)]
        for cp in first:
            cp.start()
        passed = []
        for j, (px, py) in enumerate(chips):
            got = blk(2 * px + py, c)
            copy(j, got, got, (px, py, c)).wait_recv()
            passed.append(copy(n_p + j, got, got, sibling))
            passed[-1].start()
        for j, (px, py) in enumerate(chips):
            got = blk(2 * px + py, 1 - c)
            copy(n_p + j, got, got, sibling).wait_recv()
            copy(2 * n_p + j, cv_ref, cvf_ref.at[2 * px + py], (px, py, c)).wait_recv()
        for cp in first + passed:
            cp.wait_send()
        own_cv.wait()

    return pl.pallas_call(
        body, name="gather_w_in", in_specs=[ANY, ANY], out_specs=[ANY, ANY],
        out_shape=[jax.ShapeDtypeStruct(w_in.shape, w_in.dtype), jax.ShapeDtypeStruct((N_CHIPS,) + conv_w.shape, conv_w.dtype)],
        input_output_aliases={0: 0},
        scratch_shapes=[pltpu.SemaphoreType.DMA((3 * n_p,)), pltpu.SemaphoreType.DMA((3 * n_p,)), pltpu.SemaphoreType.DMA],
    )(w_in, conv_w)


HBM = pl.BlockSpec(memory_space=pltpu.HBM)
SEM = pl.BlockSpec(memory_space=pltpu.SEMAPHORE)
EFFECT = pltpu.SideEffectType.DATAFLOW_SIDE_EFFECTING


class _Split:
    def __init__(self, name, arrays, plan):
        n, n_copies = len(arrays), plan.count
        self.name, self.plan, self.n = name, plan, n

        def body(*refs):
            send_sems, recv_sems, token = refs[n], refs[n + 1], refs[-1]
            for k, (src, dst, to) in enumerate(plan(refs[:n])):
                pltpu.make_async_remote_copy(src_ref=src, dst_ref=dst, send_sem=send_sems.at[k], recv_sem=recv_sems.at[k],
                                             device_id=to, device_id_type=MESH).start()
            token[...] = jnp.zeros_like(token)

        outs = pl.pallas_call(
            body, name=name + "_start",
            out_shape=(pltpu.SemaphoreType.DMA((n_copies,)), pltpu.SemaphoreType.DMA((n_copies,)),
                       *[pltpu.HBM(a.shape, a.dtype) for a in arrays], jax.ShapeDtypeStruct((SUBLANES, LANES), F32)),
            in_specs=(HBM,) * n, out_specs=(SEM, SEM) + (HBM,) * n + (pl.BlockSpec(memory_space=pltpu.VMEM),),
            input_output_aliases={i: 2 + i for i in range(n)},
            compiler_params=pltpu.CompilerParams(has_side_effects=EFFECT),
        )(*[pltpu.with_memory_space_constraint(a, pltpu.HBM) for a in arrays])
        self.sems, self.arrays, self.token = outs[:2], outs[2:2 + n], outs[-1]

    def wait(self, after):
        n, plan = self.n, self.plan

        def body(*refs):
            send_sems, recv_sems = refs[n], refs[n + 1]
            for k, (src, dst, to) in enumerate(plan(refs[:n])):
                cp = pltpu.make_async_remote_copy(src_ref=src, dst_ref=dst, send_sem=send_sems.at[k],
                                                  recv_sem=recv_sems.at[k], device_id=to, device_id_type=MESH)
                cp.wait_send()
                cp.wait_recv()

        return pl.pallas_call(
            body, name=self.name + "_wait", out_shape=tuple(pltpu.HBM(a.shape, a.dtype) for a in self.arrays),
            in_specs=(HBM,) * n + (SEM, SEM, ANY), out_specs=(HBM,) * n, input_output_aliases={i: i for i in range(n)},
            compiler_params=pltpu.CompilerParams(has_side_effects=EFFECT),
        )(*self.arrays, *self.sems, after)


COLS_SHARDED = (True, False, True, False)
HALF_SHAPES = [(D_MODEL // 2, IN_COLS), (D_MODEL, D_MODEL // 2), (D_MODEL // 2, D_FF), (D_FF, D_MODEL // 2)]
PIECE_SHAPES = [(D_MODEL // 2, IN_COLS // N_CHIPS), (D_MODEL // N_CHIPS, D_MODEL // 2),
                (D_MODEL // 2, D_FF // N_CHIPS), (D_FF // N_CHIPS, D_MODEL // 2)]


def _shard_view(kind, ref, chip):
    if COLS_SHARDED[kind]:
        n = ref.shape[1] // N_CHIPS
        return ref.at[:, pl.ds(chip * n, n)]
    n = ref.shape[0] // N_CHIPS
    return ref.at[pl.ds(chip * n, n), :]


def _half_view(kind, ref, h):
    if COLS_SHARDED[kind]:
        n = ref.shape[0] // 2
        return ref.at[pl.ds(h * n, n), :]
    n = ref.shape[1] // 2
    return ref.at[:, pl.ds(h * n, n)]


def _plan(count):
    def mark(fn):
        fn.count = count
        return fn
    return mark


def _shard_half_view(kind, ref, chip, h):
    if COLS_SHARDED[kind]:
        m, n = ref.shape[0] // 2, ref.shape[1] // N_CHIPS
        return ref.at[pl.ds(h * m, m), pl.ds(chip * n, n)]
    m = ref.shape[0] // N_CHIPS // 2
    return ref.at[pl.ds((2 * chip + h) * m, m), :]


def _gather_rest(w_out, w_ff1, w_ff2):
    @_plan(9)
    def plan(refs):
        x, y, c, me = _place()
        mine = [_shard_half_view(kind, ref, me, c) for kind, ref in zip((1, 2, 3), refs)]
        return [(v, v, (px, py, c)) for v in mine for px, py in _other_chips(x, y)]

    return _Split("gather_rest", (w_out, w_ff1, w_ff2), plan)


def _pass_rest(w_out, w_ff1, w_ff2):
    @_plan(9)
    def plan(refs):
        x, y, c, _ = _place()
        got = [_shard_half_view(kind, ref, 2 * px + py, c) for kind, ref in zip((1, 2, 3), refs)
               for px, py in _other_chips(x, y)]
        return [(v, v, (x, y, 1 - c)) for v in got]

    return _Split("pass_rest", (w_out, w_ff1, w_ff2), plan)


def _swap_halves(kinds, grads):
    @_plan(len(kinds))
    def plan(refs):
        x, y, c, _ = _place()
        return [(_half_view(kind, g, 1 - c), land, (x, y, 1 - c))
                for kind, g, land in zip(kinds, refs[:len(kinds)], refs[len(kinds):])]

    lands = [lax.empty(HALF_SHAPES[kind], F32) for kind in kinds]
    return _Split("swap_halves_" + "".join(map(str, kinds)), (*grads, *lands), plan)


def _add_half(name, g, recv, core, rows_split):
    shape = recv.shape
    tr = min(shape[0], 128 if rows_split else 256)
    nb = shape[0] // tr

    def body(c_ref, g_ref, r_ref, o_ref):
        o_ref[...] = (g_ref[...] + r_ref[...]).astype(o_ref.dtype)

    g_map = (lambda i, c_ref: (c_ref[0] * nb + i, 0)) if rows_split else (lambda i, c_ref: (i, c_ref[0]))
    blk = pl.BlockSpec((tr, shape[1]), lambda i, c_ref: (i, 0))
    return pl.pallas_call(
        body, name=name,
        grid_spec=pltpu.PrefetchScalarGridSpec(
            num_scalar_prefetch=1, grid=(nb,),
            in_specs=[pl.BlockSpec((tr, shape[1]), g_map), blk], out_specs=blk),
        out_shape=jax.ShapeDtypeStruct(shape, BF16),
        compiler_params=_params("parallel"),
    )(core, g, recv)


def _exchange_pieces(kinds, halves):
    n_p = N_CHIPS - 1

    @_plan(n_p * len(kinds))
    def plan(refs):
        x, y, c, _ = _place()
        return [(_shard_view(kind, half, 2 * px + py), land.at[j], (px, py, c))
                for j, (px, py) in enumerate(_other_chips(x, y))
                for kind, half, land in zip(kinds, refs[:len(kinds)], refs[len(kinds):])]

    lands = [lax.empty((n_p,) + PIECE_SHAPES[kind], BF16) for kind in kinds]
    return _Split("exchange_pieces_" + "".join(map(str, kinds)), (*halves, *lands), plan)


def _sum_pieces(name, half, slots, place, rows_split):
    n_p, rows, cols = slots.shape
    tr = min(rows, 256)
    nb = rows // tr
    if rows_split:
        own_map = lambda i, s: (i, s[0])
        out_map = lambda i, s: (s[1] * nb + i, 0)
        shard = (2 * rows, cols)
    else:
        own_map = lambda i, s: (s[0] * nb + i, 0)
        out_map = lambda i, s: (i, s[1])
        shard = (rows, 2 * cols)

    def body(s_ref, own_ref, slot_ref, o_ref):
        total = own_ref[...].astype(F32)
        for j in range(n_p):
            total = total + slot_ref[j].astype(F32)
        o_ref[...] = total

    return pl.pallas_call(
        body, name=name,
        grid_spec=pltpu.PrefetchScalarGridSpec(
            num_scalar_prefetch=1, grid=(nb,),
            in_specs=[pl.BlockSpec((tr, cols), own_map), pl.BlockSpec((n_p, tr, cols), lambda i, s: (0, i, 0))],
            out_specs=pl.BlockSpec((tr, cols), out_map)),
        out_shape=jax.ShapeDtypeStruct(shard, F32),
        compiler_params=_params("parallel"),
    )(place, half, slots)


def _join_halves(kinds, shards):
    @_plan(len(kinds))
    def plan(refs):
        x, y, c, _ = _place()
        return [(_half_view(kind, g, c), _half_view(kind, g, c), (x, y, 1 - c)) for kind, g in zip(kinds, refs)]

    return _Split("join_halves_" + "".join(map(str, kinds)), tuple(shards), plan)


def _sum_small(pack, after):
    n_dev = 8

    def body(p_ref, after_ref, o_ref, slots, send_sems, recv_sems):
        x, y, c, _ = _place()
        me = 4 * x + 2 * y + c
        slots[me] = p_ref[...]
        sends = []
        for m in range(1, n_dev):
            peer = ((1 - x) if m & 4 else x, (1 - y) if m & 2 else y, (1 - c) if m & 1 else c)
            sends.append(pltpu.make_async_remote_copy(
                src_ref=p_ref, dst_ref=slots.at[me], send_sem=send_sems.at[m - 1], recv_sem=recv_sems.at[m - 1],
                device_id=peer, device_id_type=MESH))
        for cp in sends:
            cp.start()
        for m in range(1, n_dev):
            peer = ((1 - x) if m & 4 else x, (1 - y) if m & 2 else y, (1 - c) if m & 1 else c)
            pltpu.make_async_remote_copy(
                src_ref=p_ref, dst_ref=slots.at[4 * peer[0] + 2 * peer[1] + peer[2]], send_sem=send_sems.at[m - 1],
                recv_sem=recv_sems.at[m - 1], device_id=peer, device_id_type=MESH).wait_recv()
        for cp in sends:
            cp.wait_send()
        total = slots[0]
        for d in range(1, n_dev):
            total = total + slots[d]
        o_ref[...] = total

    vm = pl.BlockSpec(memory_space=pltpu.VMEM)
    return pl.pallas_call(
        body, name="sum_small", in_specs=[vm, ANY], out_specs=vm,
        out_shape=jax.ShapeDtypeStruct(pack.shape, F32),
        scratch_shapes=[pltpu.VMEM((n_dev,) + pack.shape, F32), pltpu.SemaphoreType.DMA((n_dev - 1,)),
                        pltpu.SemaphoreType.DMA((n_dev - 1,))],
    )(pack, after)


def _adamw(name, w, g, m, v, after=None):
    rows, cols = w.shape
    tr = min(rows, 256)
    extra = [] if after is None else [after]

    def body(w_ref, g_ref, m_ref, v_ref, *rest):
        d_ref, nm_ref, nv_ref = rest[-3:]
        gv = g_ref[...]
        nm = ADAM_B1 * m_ref[...] + (1.0 - ADAM_B1) * gv
        nv = ADAM_B2 * v_ref[...] + (1.0 - ADAM_B2) * jnp.square(gv)
        m_hat = nm / (1.0 - ADAM_B1 ** ADAM_STEP)
        v_hat = nv / (1.0 - ADAM_B2 ** ADAM_STEP)
        d_ref[...] = -ADAM_LR * (m_hat / (jnp.sqrt(v_hat) + ADAM_EPS) + ADAM_WD * w_ref[...])
        nm_ref[...] = nm
        nv_ref[...] = nv

    blk = pl.BlockSpec((tr, cols), lambda i: (i, 0))
    return pl.pallas_call(
        body, name=name, grid=(rows // tr,), in_specs=[blk] * 4 + [ANY] * len(extra), out_specs=[blk] * 3,
        out_shape=[jax.ShapeDtypeStruct(w.shape, F32)] * 3,
        compiler_params=_params("parallel"),
    )(w, g, m, v, *extra)


def kernel(x, w_in, lb_logits, gate_norm_w, conv_w, w_out, ln1_g, ln1_b, w_ff1, w_ff2, ln2_g, ln2_b, loss_target, m_w_in, m_lb_logits, m_gate_norm_w, m_conv_w, m_w_out, m_ln1_g, m_ln1_b, m_w_ff1, m_w_ff2, m_ln2_g, m_ln2_b, v_w_in, v_lb_logits, v_gate_norm_w, v_conv_w, v_w_out, v_ln1_g, v_ln1_b, v_w_ff1, v_w_ff2, v_ln2_g, v_ln2_b):
    xs, tgt = x[0], loss_target[0]
    chip = 2 * lax.axis_index("x") + lax.axis_index("y")
    core = lax.axis_index("c").astype(jnp.int32).reshape(1)
    chip1 = chip.astype(jnp.int32).reshape(1)
    place = jnp.concatenate([chip1, core])

    wb_in, cv4 = _gather_w_in(_place_shard("place_w_in", w_in[0], chip1, True), conv_w[0])
    conv_full = cv4.transpose(1, 0, 2).reshape(3, CONV_WIDTH)
    rest = _gather_rest(
        _place_shard("place_w_out", w_out[0], chip1, False, after=wb_in),
        _place_shard("place_w_ff1", w_ff1[0], chip1, True, after=wb_in),
        _place_shard("place_w_ff2", w_ff2[0], chip1, False, after=wb_in))

    proj, xb = _in_proj(xs, wb_in, rest.token)
    o, states = _hgrn_fwd(proj, lb_logits)
    passed = _pass_rest(*rest.wait(o))
    cat = _gate_fwd(proj, o, gate_norm_w, conv_full, passed.token)
    wb_out, wb_ff1, wb_ff2 = passed.wait(cat)
    xhat1, h1b, rstd1 = _out_ln1(cat, wb_out, xs, ln1_g, ln1_b)
    r, dpre2, dpre2b, g_ln2_g, g_ln2_b, loss8 = _mlp_fwd(xhat1, ln1_g, ln1_b, wb_ff1, wb_ff2, ln2_g, ln2_b, tgt)

    names = ("w_in", "w_out", "w_ff1", "w_ff2")

    def add_halves(kinds, grads, lands):
        return [_add_half("add_half_" + names[k], g, ld, core, COLS_SHARDED[k]) for k, g, ld in zip(kinds, grads, lands)]

    def sum_pieces(kinds, halves, lands):
        return [_sum_pieces("sum_pieces_" + names[k], h, ld, place, COLS_SHARDED[k]) for k, h, ld in zip(kinds, halves, lands)]

    da, dpre1, dpre1b, g_ln1_g, g_ln1_b = _mlp_bwd(dpre2, r, wb_ff1, wb_ff2, xhat1, rstd1, ln1_g)
    dcat = _out_bwd(dpre1b, wb_out)
    early = (1, 2, 3)
    swap = _swap_halves(early, (_dw_out(cat, dpre1b), _dw_ff1(h1b, da), _dw_ff2(r, dpre2b)))
    do, dpg, g_gnw, g_conv = _gate_bwd(dcat, o, proj, gate_norm_w, conv_full, swap.token)
    swapped = swap.wait(do)
    exch = _exchange_pieces(early, add_halves(early, swapped[:3], swapped[3:]))
    dph, g_lbl = _hgrn_bwd(proj, do, states, lb_logits, exch.token)
    grad_x = _in_bwd(dph, dpg, wb_in, dpre1)
    exchanged = exch.wait(grad_x)
    join = _join_halves(early, sum_pieces(early, exchanged[:3], exchanged[3:]))
    g_in_local = _dw_in(xb, dph, dpg, join.token)
    g_w_out, g_w_ff1, g_w_ff2 = join.wait(g_in_local)

    late = (0,)
    swap = _swap_halves(late, (g_in_local,))
    pack = jnp.concatenate([
        g_ln1_g, g_ln1_b, g_ln2_g, g_ln2_b,
        jnp.concatenate([g_lbl[0:1], g_lbl[1:2]], axis=1),
        jnp.concatenate([g_gnw, g_conv[0:1]], axis=1),
        jnp.concatenate([g_conv[1:2], g_conv[2:3]], axis=1),
        jnp.concatenate([loss8[0:1], jnp.zeros((1, D_MODEL - LANES), F32)], axis=1)], axis=0)
    tot = _sum_small(pack, swap.token)
    loss = tot[7, 0]
    half = D_MODEL // 2
    g_lb_logits = jnp.concatenate([tot[4:5, :half], tot[4:5, half:]], axis=0)
    g_gate_norm_w = tot[5:6, :half]
    g_conv_full = jnp.concatenate([tot[5:6, half:], tot[6:7, :half], tot[6:7, half:]], axis=0)
    g_conv_w = lax.dynamic_slice(g_conv_full, (0, chip * LANES), (3, LANES))
    swapped = swap.wait(tot)
    exch = _exchange_pieces(late, add_halves(late, swapped[:1], swapped[1:]))
    d_ff1, nm_ff1, nv_ff1 = _adamw("adamw_w_ff1", w_ff1[0], g_w_ff1, m_w_ff1[0], v_w_ff1[0], exch.token)
    d_ff2, nm_ff2, nv_ff2 = _adamw("adamw_w_ff2", w_ff2[0], g_w_ff2, m_w_ff2[0], v_w_ff2[0], d_ff1)
    d_out, nm_out, nv_out = _adamw("adamw_w_out", w_out[0], g_w_out, m_w_out[0], v_w_out[0], d_ff2)

    def small_pack(lbl, gnw, cv, l1g, l1b, l2g, l2b):
        pad = jnp.zeros((1, D_MODEL - 3 * LANES), F32)
        return jnp.concatenate([
            l1g, l1b, l2g, l2b, jnp.concatenate([lbl[0:1], lbl[1:2]], axis=1),
            jnp.concatenate([gnw, jnp.zeros((1, half), F32)], axis=1),
            jnp.concatenate([cv[0:1], cv[1:2], cv[2:3], pad], axis=1), jnp.zeros((1, D_MODEL), F32)], axis=0)

    w_s = small_pack(lb_logits, gate_norm_w, conv_w[0], ln1_g, ln1_b, ln2_g, ln2_b)
    g_s = small_pack(g_lb_logits, g_gate_norm_w, g_conv_w, tot[0:1], tot[1:2], tot[2:3], tot[3:4])
    m_s = small_pack(m_lb_logits, m_gate_norm_w, m_conv_w[0], m_ln1_g, m_ln1_b, m_ln2_g, m_ln2_b)
    v_s = small_pack(v_lb_logits, v_gate_norm_w, v_conv_w[0], v_ln1_g, v_ln1_b, v_ln2_g, v_ln2_b)
    d_s, nm_s, nv_s = _adamw("adamw_small", w_s, g_s, m_s, v_s, d_out)
    exchanged = exch.wait(d_s)
    join = _join_halves(late, sum_pieces(late, exchanged[:1], exchanged[1:]))
    g_w_in, = join.wait(join.token)
    d_in, nm_in, nv_in = _adamw("adamw_w_in", w_in[0], g_w_in, m_w_in[0], v_w_in[0])

    def unpack(p):
        lbl = jnp.concatenate([p[4:5, :half], p[4:5, half:]], axis=0)
        cv = jnp.concatenate([p[6:7, 0:LANES], p[6:7, LANES:2 * LANES], p[6:7, 2 * LANES:3 * LANES]], axis=0)
        return dict(lb_logits=lbl, gate_norm_w=p[5:6, :half], conv_w=cv[None], ln1_g=p[0:1], ln1_b=p[1:2],
                    ln2_g=p[2:3], ln2_b=p[3:4])

    order = ("w_in", "lb_logits", "gate_norm_w", "conv_w", "w_out", "ln1_g", "ln1_b", "w_ff1", "w_ff2", "ln2_g", "ln2_b")
    grad = dict(unpack(g_s), w_in=g_w_in[None], w_out=g_w_out[None], w_ff1=g_w_ff1[None], w_ff2=g_w_ff2[None])
    delta = dict(unpack(d_s), w_in=d_in[None], w_out=d_out[None], w_ff1=d_ff1[None], w_ff2=d_ff2[None])
    new_m = dict(unpack(nm_s), w_in=nm_in[None], w_out=nm_out[None], w_ff1=nm_ff1[None], w_ff2=nm_ff2[None])
    new_v = dict(unpack(nv_s), w_in=nv_in[None], w_out=nv_out[None], w_ff1=nv_ff1[None], w_ff2=nv_ff2[None])
    return (loss, grad_x[None], *[grad[n] for n in order], *[delta[n] for n in order],
            *[new_m[n] for n in order], *[new_v[n] for n in order])
```

```python
import jax
import jax.numpy as jnp
from jax import lax
from jax.experimental import pallas as pl
from jax.experimental.pallas import tpu as pltpu

F32 = jnp.float32
BF16 = jnp.bfloat16
MXU_DTYPE = jnp.bfloat16

D_MODEL = 1024
HGRN_WIDTH = 512
HEAD_DIM = 128
N_HEADS = 4
CONV_WIDTH = 512
CHUNK = 64
D_FF = 4096
IN_COLS = 3584
GROUP = 512
N_GROUPS = IN_COLS // GROUP
ALPHA = 2.0 ** 0.25
EPS = 1e-5
N_CHIPS = 4
ADAM_LR, ADAM_B1, ADAM_B2, ADAM_EPS, ADAM_WD, ADAM_STEP = 0.001, 0.9, 0.999, 1e-08, 0.01, 10

LANES = 128
SUBLANES = 8
VMEM_LIMIT = 56 * 1024 * 1024
FF_BLOCK = 1024
N_FF = D_FF // FF_BLOCK
NN = (((1,), (0,)), ((), ()))
NT = (((1,), (1,)), ((), ()))
TN = (((0,), (0,)), ((), ()))
MESH = pl.DeviceIdType.MESH
ANY = pl.BlockSpec(memory_space=pl.ANY)


def _dot(a, b, dims):
    return lax.dot_general(a.astype(MXU_DTYPE), b.astype(MXU_DTYPE), dims, preferred_element_type=F32)


def _dot_exact(ones, v):
    ones = ones.astype(jnp.bfloat16)
    hi = v.astype(jnp.bfloat16)
    rest = v - hi.astype(F32)
    mid = rest.astype(jnp.bfloat16)
    low = (rest - mid.astype(F32)).astype(jnp.bfloat16)
    return sum(lax.dot_general(ones, part, NN, preferred_element_type=F32) for part in (hi, mid, low))


def _params(*sem):
    return pltpu.CompilerParams(dimension_semantics=sem, vmem_limit_bytes=VMEM_LIMIT)


def _resident(shape):
    return pl.BlockSpec(shape, lambda *_: (0,) * len(shape), pipeline_mode=pl.Buffered(1))


def _sigmoid(v):
    return 1.0 / (1.0 + jnp.exp(-v))


def _lower_bound(lbl):
    m = jnp.max(lbl, axis=0, keepdims=True)
    e = jnp.exp(lbl - m)
    s = e / jnp.sum(e, axis=0, keepdims=True)
    return s[0:1, :], s[1:2, :]


def _heads(v):
    return [v[:, h * HEAD_DIM:(h + 1) * HEAD_DIM] for h in range(N_HEADS)]


def _per_head(fn, *arrays):
    return jnp.concatenate([fn(*parts) for parts in zip(*map(_heads, arrays))], axis=1)


def _stream_in(first_step, hbm, vmem, sems, views):
    copies = [pltpu.make_async_copy(view(hbm), view(vmem), sems.at[n]) for n, view in enumerate(views)]

    @pl.when(first_step)
    def _():
        for cp in copies:
            cp.start()

    def ready(n):
        @pl.when(first_step)
        def _():
            copies[n].wait()

    return ready


def _col_blocks(width, n):
    return [lambda ref, j=j: ref.at[:, pl.ds(j * width, width)] for j in range(n)]


def _row_blocks(height, n):
    return [lambda ref, j=j: ref.at[pl.ds(j * height, height), :] for j in range(n)]


def _in_proj(x, w_in, after):
    t = x.shape[0]
    tm = min(t, 512)

    def body(x_ref, w_hbm, after_ref, o_ref, xb_ref, w_ref, sems):
        ready = _stream_in(pl.program_id(0) == 0, w_hbm, w_ref, sems, _col_blocks(GROUP, N_GROUPS))
        xb = x_ref[...].astype(xb_ref.dtype)
        xb_ref[...] = xb
        for g in range(N_GROUPS):
            ready(g)
            o_ref[g] = _dot(xb, w_ref[:, g * GROUP:(g + 1) * GROUP], NN)

    return pl.pallas_call(
        body, name="in_proj", grid=(t // tm,),
        in_specs=[pl.BlockSpec((tm, D_MODEL), lambda i: (i, 0)), ANY, ANY],
        out_specs=[pl.BlockSpec((N_GROUPS, tm, GROUP), lambda i: (0, i, 0)), pl.BlockSpec((tm, D_MODEL), lambda i: (i, 0))],
        out_shape=[jax.ShapeDtypeStruct((N_GROUPS, t, GROUP), F32), jax.ShapeDtypeStruct((t, D_MODEL), BF16)],
        scratch_shapes=[pltpu.VMEM((D_MODEL, IN_COLS), w_in.dtype), pltpu.SemaphoreType.DMA((N_GROUPS,))],
        compiler_params=_params("arbitrary"),
    )(x, w_in, after)


def _gates(fp, lb):
    sig = _sigmoid(fp)
    f = lb + (1.0 - lb) * sig
    return sig, f, jnp.log(f), 1.0 - f


def _chunk_masks():
    row = lax.broadcasted_iota(jnp.int32, (CHUNK, CHUNK), 0)
    col = lax.broadcasted_iota(jnp.int32, (CHUNK, CHUNK), 1)
    return row >= col, row <= col


def _hgrn_fwd(proj, lb_logits):
    t = proj.shape[1]
    tb = min(t, 512)
    ncb = tb // CHUNK

    def body(q_ref, f_ref, v_ref, lbl_ref, o_ref, st_ref, s_scr):
        @pl.when(pl.program_id(0) == 0)
        def _():
            s_scr[...] = jnp.zeros_like(s_scr)

        lb, _ = _lower_bound(lbl_ref[...])
        causal, _ = _chunk_masks()

        every = range(ncb)
        rows = [slice(c * CHUNK, (c + 1) * CHUNK) for c in every]
        q, v = [q_ref[r, :] for r in rows], [v_ref[r, :] for r in rows]
        gates = [_gates(f_ref[r, :], lb) for r in rows]
        k = [gt[3] for gt in gates]
        b = [_dot_exact(causal, gt[2]) for gt in gates]
        mid, last = [x[CHUNK // 2:CHUNK // 2 + 1, :] for x in b], [x[CHUNK - 1:CHUNK, :] for x in b]
        qt = [q[c] * jnp.exp(b[c] - mid[c]) for c in every]
        kt = [k[c] * jnp.exp(mid[c] - b[c]) for c in every]
        qi = [q[c] * jnp.exp(b[c]) for c in every]
        ks = [k[c] * jnp.exp(last[c] - b[c]) for c in every]
        dec = [jnp.exp(x) for x in last]
        scores = [[jnp.where(causal, _dot(a, b_, NT), 0.0) for a, b_ in zip(_heads(qt[c]), _heads(kt[c]))] for c in every]
        intra = [[_dot(s, v_h, NN) for s, v_h in zip(scores[c], _heads(v[c]))] for c in every]
        update = [_per_head(lambda v_h, ks_h: _dot(v_h, ks_h, TN), v[c], ks[c]) for c in every]

        st = s_scr[...]
        states = []
        for c in every:
            states.append(st)
            st_ref[c] = st
            st = dec[c] * st + update[c]
        s_scr[...] = st

        o_ref[...] = jnp.concatenate(
            [jnp.concatenate([i_h + _dot(qi_h, st_h, NT) for i_h, qi_h, st_h in
                              zip(intra[c], _heads(qi[c]), _heads(states[c]))], axis=1) for c in every], axis=0)

    grp = lambda g: pl.BlockSpec((None, tb, GROUP), lambda i: (g, i, 0))
    return pl.pallas_call(
        body, name="hgrn_fwd", grid=(t // tb,),
        in_specs=[grp(0), grp(1), grp(2), pl.BlockSpec((2, HGRN_WIDTH), lambda i: (0, 0))],
        out_specs=[pl.BlockSpec((tb, HGRN_WIDTH), lambda i: (i, 0)),
                   pl.BlockSpec((ncb, HEAD_DIM, HGRN_WIDTH), lambda i: (i, 0, 0))],
        out_shape=[jax.ShapeDtypeStruct((t, HGRN_WIDTH), F32),
                   jax.ShapeDtypeStruct((t // CHUNK, HEAD_DIM, HGRN_WIDTH), F32)],
        scratch_shapes=[pltpu.VMEM((HEAD_DIM, HGRN_WIDTH), F32)],
        compiler_params=_params("arbitrary"),
    )(proj, proj, proj, lb_logits)


def _conv_taps(z, halo, zbuf, tb):
    zbuf[0:SUBLANES, :] = halo
    zbuf[SUBLANES:SUBLANES + tb, :] = z
    return zbuf[SUBLANES - 1:SUBLANES - 1 + tb, :], zbuf[SUBLANES - 2:SUBLANES - 2 + tb, :]


def _gate_fwd(proj, o, gate_norm_w, conv_w, after):
    t = proj.shape[1]
    tb = min(t, 512)
    hb = tb // SUBLANES

    def body(o_ref, og_ref, gnw_ref, b_ref, c_ref, u_ref, ch_ref, uh_ref, cw_ref, after_ref, cat_ref, zbuf):
        i = pl.program_id(0)
        og = og_ref[...]
        on = _per_head(lambda o_h: o_h * lax.rsqrt(jnp.mean(o_h * o_h, axis=-1, keepdims=True) + EPS), o_ref[...])
        cat_ref[0] = (on * gnw_ref[...] * (og * _sigmoid(og))).astype(cat_ref.dtype)
        z = c_ref[...] * u_ref[...]
        halo = jnp.where(i > 0, ch_ref[...] * uh_ref[...], 0.0)
        z1, z2 = _conv_taps(z, halo, zbuf, tb)
        cw = cw_ref[...]
        yc = cw[2:3, :] * z + cw[1:2, :] * z1 + cw[0:1, :] * z2
        cat_ref[1] = (b_ref[...] * yc).astype(cat_ref.dtype)

    grp = lambda g: pl.BlockSpec((None, tb, GROUP), lambda i: (g, i, 0))
    prev = lambda g: pl.BlockSpec((None, SUBLANES, GROUP), lambda i: (g, jnp.maximum(i * hb - 1, 0), 0))
    vec = lambda r: pl.BlockSpec((r, GROUP), lambda i: (0, 0))
    return pl.pallas_call(
        body, name="gate_fwd", grid=(t // tb,),
        in_specs=[pl.BlockSpec((tb, GROUP), lambda i: (i, 0)), grp(3), vec(1), grp(4), grp(5), grp(6), prev(5), prev(6),
                  vec(3), ANY],
        out_specs=pl.BlockSpec((2, tb, GROUP), lambda i: (0, i, 0)),
        out_shape=jax.ShapeDtypeStruct((2, t, HGRN_WIDTH), BF16),
        scratch_shapes=[pltpu.VMEM((tb + SUBLANES, GROUP), F32)],
        compiler_params=_params("parallel"),
    )(o, proj, gate_norm_w, proj, proj, proj, proj, proj, conv_w, after)


def _out_ln1(cat, w_out, x, g1, b1, after):
    t = x.shape[0]
    tm = min(t, 512)

    def body(cat_ref, w_ref, x_ref, g_ref, b_ref, after_ref, xhat_ref, h1_ref, rstd_ref):
        mix = _dot(cat_ref[0], w_ref[0:GROUP, :], NN) + _dot(cat_ref[1], w_ref[GROUP:2 * GROUP, :], NN)
        pre = ALPHA * x_ref[...] + mix
        xc = pre - jnp.mean(pre, axis=-1, keepdims=True)
        rstd = lax.rsqrt(jnp.mean(xc * xc, axis=-1, keepdims=True) + EPS)
        xhat = xc * rstd
        xhat_ref[...] = xhat
        h1_ref[...] = (xhat * g_ref[...] + b_ref[...]).astype(h1_ref.dtype)
        rstd_ref[...] = rstd

    row = pl.BlockSpec((tm, D_MODEL), lambda i: (i, 0))
    vec = pl.BlockSpec((1, D_MODEL), lambda i: (0, 0))
    return pl.pallas_call(
        body, name="out_ln1", grid=(t // tm,),
        in_specs=[pl.BlockSpec((2, tm, GROUP), lambda i: (0, i, 0)), _resident((D_MODEL, D_MODEL)), row, vec, vec, ANY],
        out_specs=[row, row, pl.BlockSpec((tm, 1), lambda i: (i, 0))],
        out_shape=[jax.ShapeDtypeStruct((t, D_MODEL), F32), jax.ShapeDtypeStruct((t, D_MODEL), BF16),
                   jax.ShapeDtypeStruct((t, 1), F32)],
        compiler_params=_params("parallel"),
    )(cat, w_out, x, g1, b1, after)


def _ln_bwd(dy, xhat, rstd, g):
    dxhat = dy * g
    m1 = jnp.mean(dxhat, axis=-1, keepdims=True)
    m2 = jnp.mean(dxhat * xhat, axis=-1, keepdims=True)
    return rstd * (dxhat - m1 - xhat * m2)


def _mlp_fwd(xhat1, g1, b1, w_ff1, w_ff2, g2, b2, target):
    t = xhat1.shape[0]
    tm = min(t, 256)

    def body(xh_ref, g1_ref, b1_ref, w1_hbm, w2_hbm, g2_ref, b2_ref, tg_ref,
             r_ref, dpre_ref, dpreb_ref, dg_ref, db_ref, loss_ref, w1_ref, w2_ref, sems1, sems2):
        first = pl.program_id(0) == 0
        ready1 = _stream_in(first, w1_hbm, w1_ref, sems1, _col_blocks(FF_BLOCK, N_FF))
        ready2 = _stream_in(first, w2_hbm, w2_ref, sems2, _row_blocks(FF_BLOCK, N_FF))

        @pl.when(first)
        def _():
            dg_ref[...] = jnp.zeros_like(dg_ref)
            db_ref[...] = jnp.zeros_like(db_ref)
            loss_ref[...] = jnp.zeros_like(loss_ref)

        h1 = xh_ref[...] * g1_ref[...] + b1_ref[...]
        h1b = h1.astype(MXU_DTYPE)
        mlp = jnp.zeros((tm, D_MODEL), F32)
        for j in range(N_FF):
            cols = slice(j * FF_BLOCK, (j + 1) * FF_BLOCK)
            ready1(j)
            r = jnp.square(jnp.maximum(_dot(h1b, w1_ref[:, cols], NN), 0.0)).astype(r_ref.dtype)
            r_ref[:, cols] = r
            ready2(j)
            mlp = mlp + _dot(r, w2_ref[cols, :], NN)
        pre = ALPHA * h1 + mlp
        xc = pre - jnp.mean(pre, axis=-1, keepdims=True)
        rstd = lax.rsqrt(jnp.mean(xc * xc, axis=-1, keepdims=True) + EPS)
        xhat = xc * rstd
        err = xhat * g2_ref[...] + b2_ref[...] - tg_ref[...]
        loss_ref[...] += 0.5 * jnp.sum(jnp.mean(err * err, axis=-1, keepdims=True))
        dy = err * (1.0 / D_MODEL)
        dg_ref[...] += jnp.sum(dy * xhat, axis=0, keepdims=True)
        db_ref[...] += jnp.sum(dy, axis=0, keepdims=True)
        dpre = _ln_bwd(dy, xhat, rstd, g2_ref[...])
        dpre_ref[...] = dpre
        dpreb_ref[...] = dpre.astype(dpreb_ref.dtype)

    row = pl.BlockSpec((tm, D_MODEL), lambda i: (i, 0))
    vec = pl.BlockSpec((1, D_MODEL), lambda i: (0, 0))
    return pl.pallas_call(
        body, name="mlp_fwd", grid=(t // tm,),
        in_specs=[row, vec, vec, ANY, ANY, vec, vec, row],
        out_specs=[pl.BlockSpec((tm, D_FF), lambda i: (i, 0)), row, row, vec, vec,
                   pl.BlockSpec((SUBLANES, LANES), lambda i: (0, 0))],
        out_shape=[jax.ShapeDtypeStruct((t, D_FF), BF16), jax.ShapeDtypeStruct((t, D_MODEL), F32),
                   jax.ShapeDtypeStruct((t, D_MODEL), BF16), jax.ShapeDtypeStruct((1, D_MODEL), F32),
                   jax.ShapeDtypeStruct((1, D_MODEL), F32), jax.ShapeDtypeStruct((SUBLANES, LANES), F32)],
        scratch_shapes=[pltpu.VMEM(w_ff1.shape, w_ff1.dtype), pltpu.VMEM(w_ff2.shape, w_ff2.dtype),
                        pltpu.SemaphoreType.DMA((N_FF,)), pltpu.SemaphoreType.DMA((N_FF,))],
        compiler_params=_params("arbitrary"),
    )(xhat1, g1, b1, w_ff1, w_ff2, g2, b2, target)


def _mlp_bwd(dpre2, r, w_ff1, w_ff2, xhat1, rstd1, g1):
    t = r.shape[0]
    tm = min(t, 256)

    def body(dp2_ref, r_ref, w1_hbm, w2_hbm, xh_ref, rs_ref, g_ref, da_ref, dpre_ref, dpreb_ref, dg_ref, db_ref,
             w1_ref, w2_ref, sems1, sems2):
        first = pl.program_id(0) == 0
        ready2 = _stream_in(first, w2_hbm, w2_ref, sems2, _row_blocks(FF_BLOCK, N_FF))
        ready1 = _stream_in(first, w1_hbm, w1_ref, sems1, _col_blocks(FF_BLOCK, N_FF))

        @pl.when(first)
        def _():
            dg_ref[...] = jnp.zeros_like(dg_ref)
            db_ref[...] = jnp.zeros_like(db_ref)

        dp2 = dp2_ref[...]
        dp2b = dp2.astype(MXU_DTYPE)
        back = jnp.zeros((tm, D_MODEL), F32)
        for j in range(N_FF):
            cols = slice(j * FF_BLOCK, (j + 1) * FF_BLOCK)
            ready2(j)
            dr = _dot(dp2b, w2_ref[cols, :], NT)
            da = (dr * (2.0 * jnp.sqrt(r_ref[:, cols].astype(F32)))).astype(da_ref.dtype)
            da_ref[:, cols] = da
            ready1(j)
            back = back + _dot(da, w1_ref[:, cols], NT)
        dh1 = ALPHA * dp2 + back
        xhat = xh_ref[...]
        dg_ref[...] += jnp.sum(dh1 * xhat, axis=0, keepdims=True)
        db_ref[...] += jnp.sum(dh1, axis=0, keepdims=True)
        dpre = _ln_bwd(dh1, xhat, rs_ref[...], g_ref[...])
        dpre_ref[...] = dpre
        dpreb_ref[...] = dpre.astype(dpreb_ref.dtype)

    row = pl.BlockSpec((tm, D_MODEL), lambda i: (i, 0))
    wide = pl.BlockSpec((tm, D_FF), lambda i: (i, 0))
    vec = pl.BlockSpec((1, D_MODEL), lambda i: (0, 0))
    return pl.pallas_call(
        body, name="mlp_bwd", grid=(t // tm,),
        in_specs=[row, wide, ANY, ANY, row, pl.BlockSpec((tm, 1), lambda i: (i, 0)), vec],
        out_specs=[wide, row, row, vec, vec],
        out_shape=[jax.ShapeDtypeStruct((t, D_FF), BF16), jax.ShapeDtypeStruct((t, D_MODEL), F32),
                   jax.ShapeDtypeStruct((t, D_MODEL), BF16), jax.ShapeDtypeStruct((1, D_MODEL), F32),
                   jax.ShapeDtypeStruct((1, D_MODEL), F32)],
        scratch_shapes=[pltpu.VMEM(w_ff1.shape, w_ff1.dtype), pltpu.VMEM(w_ff2.shape, w_ff2.dtype),
                        pltpu.SemaphoreType.DMA((N_FF,)), pltpu.SemaphoreType.DMA((N_FF,))],
        compiler_params=_params("arbitrary"),
    )(dpre2, r, w_ff1, w_ff2, xhat1, rstd1, g1)


def _out_bwd(dpre1b, w_out):
    t = dpre1b.shape[0]
    tm = min(t, 512)

    def body(d_ref, w_ref, o_ref):
        o_ref[...] = _dot(d_ref[...], w_ref[...], NT)

    return pl.pallas_call(
        body, name="out_bwd", grid=(t // tm,),
        in_specs=[pl.BlockSpec((tm, D_MODEL), lambda i: (i, 0)), _resident((D_MODEL, D_MODEL))],
        out_specs=pl.BlockSpec((tm, D_MODEL), lambda i: (i, 0)),
        out_shape=jax.ShapeDtypeStruct((t, D_MODEL), F32),
        compiler_params=_params("parallel"),
    )(dpre1b, w_out)


def _gate_bwd(dcat, o, proj, gate_norm_w, conv_w, after):
    t = proj.shape[1]
    tb = min(t, 512)
    hb = tb // SUBLANES
    nblk = t // tb

    def body(do2_ref, dy_ref, dyn_ref, o_ref, og_ref, gnw_ref, b_ref, bn_ref, c_ref, u_ref, ch_ref, uh_ref, cw_ref,
             after_ref, do_ref, dp_ref, dgnw_ref, dcw_ref, zbuf, dbuf):
        i = pl.program_id(0)

        @pl.when(i == 0)
        def _():
            dgnw_ref[...] = jnp.zeros_like(dgnw_ref)
            dcw_ref[...] = jnp.zeros_like(dcw_ref)

        ov, og, gnw, do2 = o_ref[...], og_ref[...], gnw_ref[...], do2_ref[...]
        rs = _per_head(lambda o_h: jnp.broadcast_to(lax.rsqrt(jnp.mean(o_h * o_h, axis=-1, keepdims=True) + EPS),
                                                    o_h.shape), ov)
        on = ov * rs
        sg = _sigmoid(og)
        sil = og * sg
        don = do2 * gnw * sil
        dgnw_ref[...] += jnp.sum(do2 * on * sil, axis=0, keepdims=True)
        dp_ref[0] = (do2 * on * gnw * (sg * (1.0 + og * (1.0 - sg)))).astype(dp_ref.dtype)
        do_ref[...] = rs * (don - on * _per_head(
            lambda p_h: jnp.broadcast_to(jnp.mean(p_h, axis=-1, keepdims=True), p_h.shape), don * on))

        bg, cg, u, dy = b_ref[...], c_ref[...], u_ref[...], dy_ref[...]
        z = cg * u
        halo = jnp.where(i > 0, ch_ref[...] * uh_ref[...], 0.0)
        z1, z2 = _conv_taps(z, halo, zbuf, tb)
        cw = cw_ref[...]
        yc = cw[2:3, :] * z + cw[1:2, :] * z1 + cw[0:1, :] * z2
        dyc = dy * bg
        dbuf[0:tb, :] = dyc
        dbuf[tb:tb + SUBLANES, :] = jnp.where(i < nblk - 1, dyn_ref[...] * bn_ref[...], 0.0)
        d1, d2 = dbuf[1:1 + tb, :], dbuf[2:2 + tb, :]
        dz = cw[2:3, :] * dyc + cw[1:2, :] * d1 + cw[0:1, :] * d2
        dp_ref[1] = (dy * yc).astype(dp_ref.dtype)
        dp_ref[2] = (dz * u).astype(dp_ref.dtype)
        dp_ref[3] = (dz * cg).astype(dp_ref.dtype)
        dcw_ref[0:1, :] += jnp.sum(dyc * z2, axis=0, keepdims=True)
        dcw_ref[1:2, :] += jnp.sum(dyc * z1, axis=0, keepdims=True)
        dcw_ref[2:3, :] += jnp.sum(dyc * z, axis=0, keepdims=True)

    half = lambda g: pl.BlockSpec((tb, GROUP), lambda i: (i, g))
    grp = lambda g: pl.BlockSpec((None, tb, GROUP), lambda i: (g, i, 0))
    prev = lambda g: pl.BlockSpec((None, SUBLANES, GROUP), lambda i: (g, jnp.maximum(i * hb - 1, 0), 0))
    nxt_row = lambda i: jnp.minimum((i + 1) * hb, t // SUBLANES - 1)
    nxt = lambda g: pl.BlockSpec((None, SUBLANES, GROUP), lambda i: (g, nxt_row(i), 0))
    vec = lambda r: pl.BlockSpec((r, GROUP), lambda i: (0, 0))
    return pl.pallas_call(
        body, name="gate_bwd", grid=(nblk,),
        in_specs=[half(0), half(1), pl.BlockSpec((SUBLANES, GROUP), lambda i: (nxt_row(i), 1)), half(0), grp(3), vec(1),
                  grp(4), nxt(4), grp(5), grp(6), prev(5), prev(6), vec(3), ANY],
        out_specs=[half(0), pl.BlockSpec((4, tb, GROUP), lambda i: (0, i, 0)), vec(1), vec(3)],
        out_shape=[jax.ShapeDtypeStruct((t, HGRN_WIDTH), F32), jax.ShapeDtypeStruct((4, t, HGRN_WIDTH), BF16),
                   jax.ShapeDtypeStruct((1, HGRN_WIDTH), F32), jax.ShapeDtypeStruct((3, CONV_WIDTH), F32)],
        scratch_shapes=[pltpu.VMEM((tb + SUBLANES, GROUP), F32), pltpu.VMEM((tb + SUBLANES, GROUP), F32)],
        compiler_params=_params("arbitrary"),
    )(dcat, dcat, dcat, o, proj, gate_norm_w, proj, proj, proj, proj, proj, proj, conv_w, after)


def _hgrn_bwd(proj, do, states, lb_logits, after):
    t = proj.shape[1]
    tb = min(t, 512)
    ncb = tb // CHUNK
    nblk = t // tb

    def body(q_ref, f_ref, v_ref, do_ref, st_ref, lbl_ref, after_ref, dp_ref, dlbl_ref, ds_scr, dlb_scr):
        i = pl.program_id(0)

        @pl.when(i == 0)
        def _():
            ds_scr[...] = jnp.zeros_like(ds_scr)
            dlb_scr[...] = jnp.zeros_like(dlb_scr)

        lb, s1 = _lower_bound(lbl_ref[...])
        causal, anti = _chunk_masks()
        every = range(ncb)
        rows = [slice(c * CHUNK, (c + 1) * CHUNK) for c in every]
        q, v, do = ([ref[r, :] for r in rows] for ref in (q_ref, v_ref, do_ref))
        st = [st_ref[c] for c in every]
        gates = [_gates(f_ref[r, :], lb) for r in rows]
        sig, f, k = ([gt[n] for gt in gates] for n in (0, 1, 3))
        b = [_dot_exact(causal, gt[2]) for gt in gates]
        mid, last = [x[CHUNK // 2:CHUNK // 2 + 1, :] for x in b], [x[CHUNK - 1:CHUNK, :] for x in b]
        e_q = [jnp.exp(b[c] - mid[c]) for c in every]
        e_k = [jnp.exp(mid[c] - b[c]) for c in every]
        e_i = [jnp.exp(x) for x in b]
        e_s = [jnp.exp(last[c] - b[c]) for c in every]
        dec = [jnp.exp(x) for x in last]
        qt, kt, qi, ks = ([a[c] * e[c] for c in every] for a, e in ((q, e_q), (k, e_k), (q, e_i), (k, e_s)))

        def masked(a, b_):
            return [[jnp.where(causal, _dot(a_h, b_h, NT), 0.0) for a_h, b_h in zip(_heads(a[c]), _heads(b_[c]))]
                    for c in every]

        def with_scores(s, other, dims):
            return [jnp.concatenate([_dot(s_h, o_h, dims) for s_h, o_h in zip(s[c], _heads(other[c]))], axis=1)
                    for c in every]

        def per_head(dims, a, b_):
            return [_per_head(lambda a_h, b_h: _dot(a_h, b_h, dims), a[c], b_[c]) for c in every]

        scores, dscores = masked(qt, kt), masked(do, v)
        dqt, dkt, dv_intra = with_scores(dscores, kt, NN), with_scores(dscores, qt, TN), with_scores(scores, do, TN)
        dqi, update = per_head(NN, do, st), per_head(TN, do, qi)

        dst = ds_scr[...]
        dsts = [None] * ncb
        for c in reversed(every):
            dsts[c] = dst
            dst = dec[c] * dst + update[c]
        ds_scr[...] = dst

        dv_state, dks = per_head(NT, ks, dsts), per_head(NN, v, dsts)
        ddec = [jnp.sum(dsts[c] * st[c], axis=0, keepdims=True) for c in every]
        dq = [dqt[c] * e_q[c] + dqi[c] * e_i[c] for c in every]
        dk = [dkt[c] * e_k[c] + dks[c] * e_s[c] for c in every]
        db = [q[c] * dq[c] - k[c] * dk[c] for c in every]
        db_last = [jnp.sum(dks[c] * ks[c], axis=0, keepdims=True) + ddec[c] * dec[c] for c in every]
        dg = [_dot_exact(anti, db[c]) + db_last[c] for c in every]
        df = [dg[c] / f[c] - dk[c] for c in every]
        dlb_scr[...] += sum(jnp.sum(df[c] * (1.0 - sig[c]), axis=0, keepdims=True) for c in every)
        dfp = [df[c] * (1.0 - lb) * sig[c] * (1.0 - sig[c]) for c in every]
        dv = [dv_intra[c] + dv_state[c] for c in every]
        for n, parts in enumerate((dq, dfp, dv)):
            dp_ref[n] = jnp.concatenate(parts, axis=0).astype(dp_ref.dtype)

        @pl.when(i == nblk - 1)
        def _():
            dlb = dlb_scr[...]
            dlbl_ref[0:1, :] = dlb * lb * (1.0 - lb)
            dlbl_ref[1:2, :] = -dlb * lb * s1

    grp = lambda g: pl.BlockSpec((None, tb, GROUP), lambda i: (g, nblk - 1 - i, 0))
    vec = pl.BlockSpec((2, HGRN_WIDTH), lambda i: (0, 0))
    return pl.pallas_call(
        body, name="hgrn_bwd", grid=(nblk,),
        in_specs=[grp(0), grp(1), grp(2), pl.BlockSpec((tb, HGRN_WIDTH), lambda i: (nblk - 1 - i, 0)),
                  pl.BlockSpec((ncb, HEAD_DIM, HGRN_WIDTH), lambda i: (nblk - 1 - i, 0, 0)), vec, ANY],
        out_specs=[pl.BlockSpec((3, tb, HGRN_WIDTH), lambda i: (0, nblk - 1 - i, 0)), vec],
        out_shape=[jax.ShapeDtypeStruct((3, t, HGRN_WIDTH), BF16), jax.ShapeDtypeStruct((2, HGRN_WIDTH), F32)],
        scratch_shapes=[pltpu.VMEM((HEAD_DIM, HGRN_WIDTH), F32), pltpu.VMEM((1, HGRN_WIDTH), F32)],
        compiler_params=_params("arbitrary"),
    )(proj, proj, proj, do, states, lb_logits, after)


def _in_bwd(dph, dpg, w_in, dpre1):
    t = dpre1.shape[0]
    tm = min(t, 512)

    def body(dh_ref, dg_ref, w_hbm, dp_ref, o_ref, w_ref, sems):
        ready = _stream_in(pl.program_id(0) == 0, w_hbm, w_ref, sems, _col_blocks(GROUP, N_GROUPS))
        acc = ALPHA * dp_ref[...]
        for g in range(N_GROUPS):
            ready(g)
            part = dh_ref[g] if g < 3 else dg_ref[g - 3]
            acc = acc + _dot(part, w_ref[:, g * GROUP:(g + 1) * GROUP], NT)
        o_ref[...] = acc

    row = pl.BlockSpec((tm, D_MODEL), lambda i: (i, 0))
    return pl.pallas_call(
        body, name="in_bwd", grid=(t // tm,),
        in_specs=[pl.BlockSpec((3, tm, GROUP), lambda i: (0, i, 0)), pl.BlockSpec((4, tm, GROUP), lambda i: (0, i, 0)),
                  ANY, row],
        out_specs=row,
        out_shape=jax.ShapeDtypeStruct((t, D_MODEL), F32),
        scratch_shapes=[pltpu.VMEM((D_MODEL, IN_COLS), w_in.dtype), pltpu.SemaphoreType.DMA((N_GROUPS,))],
        compiler_params=_params("arbitrary"),
    )(dph, dpg, w_in, dpre1)


def _grad_w(name, operands, widths, shape, step, after=None):
    t = operands[0].shape[-2]
    tt = min(t, 512)
    n_in, n_steps = len(operands), t // tt
    in_specs = [pl.BlockSpec((tt, w), lambda k: (k, 0)) if a.ndim == 2 else
                pl.BlockSpec((a.shape[0], tt, w), lambda k: (0, k, 0)) for a, w in zip(operands, widths)]
    extra = [] if after is None else [after]

    def body(*refs):
        o_ref, acc, sem = refs[-3:]
        k = pl.program_id(0)

        @pl.when(k == 0)
        def _():
            acc[...] = jnp.zeros_like(acc)

        step(acc, *refs[:n_in])

        @pl.when(k == n_steps - 1)
        def _():
            out = pltpu.make_async_copy(acc, o_ref, sem)
            out.start()
            out.wait()

    return pl.pallas_call(
        body, name=name, grid=(n_steps,), in_specs=in_specs + [ANY] * len(extra), out_specs=ANY,
        out_shape=jax.ShapeDtypeStruct(shape, F32),
        scratch_shapes=[pltpu.VMEM(shape, F32), pltpu.SemaphoreType.DMA],
        compiler_params=_params("arbitrary"),
    )(*operands, *extra)


def _dw_in(xb, dph, dpg, after):
    def step(acc, x_ref, dh_ref, dg_ref):
        xv = x_ref[...]
        for g in range(N_GROUPS):
            part = dh_ref[g] if g < 3 else dg_ref[g - 3]
            acc[:, g * GROUP:(g + 1) * GROUP] += _dot(xv, part, TN)

    return _grad_w("dw_in", (xb, dph, dpg), (D_MODEL, GROUP, GROUP), (D_MODEL, IN_COLS), step, after)


def _dw_out(cat, dpre1b):
    def step(acc, cat_ref, d_ref):
        dv = d_ref[...]
        for g in range(2):
            acc[g * GROUP:(g + 1) * GROUP, :] += _dot(cat_ref[g], dv, TN)

    return _grad_w("dw_out", (cat, dpre1b), (GROUP, D_MODEL), (D_MODEL, D_MODEL), step)


def _dw_ff1(h1b, da):
    def step(acc, h_ref, da_ref):
        hv = h_ref[...]
        for j in range(D_FF // FF_BLOCK):
            cols = slice(j * FF_BLOCK, (j + 1) * FF_BLOCK)
            acc[:, cols] += _dot(hv, da_ref[:, cols], TN)

    return _grad_w("dw_ff1", (h1b, da), (D_MODEL, D_FF), (D_MODEL, D_FF), step)


def _dw_ff2(r, dpre2b):
    def step(acc, r_ref, d_ref):
        dv = d_ref[...]
        for j in range(D_FF // FF_BLOCK):
            rows = slice(j * FF_BLOCK, (j + 1) * FF_BLOCK)
            acc[rows, :] += _dot(r_ref[:, rows], dv, TN)

    return _grad_w("dw_ff2", (r, dpre2b), (D_FF, D_MODEL), (D_FF, D_MODEL), step)


def _place():
    x, y, c = lax.axis_index("x"), lax.axis_index("y"), lax.axis_index("c")
    return x, y, c, 2 * x + y


def _other_chips(x, y):
    return [(1 - x, y), (x, 1 - y), (1 - x, 1 - y)]


def _place_shard(name, w, chip, cols_sharded, after=None):
    rows, cols = w.shape
    tr = min(rows, 256)
    nb = rows // tr
    full = (rows, cols * N_CHIPS) if cols_sharded else (rows * N_CHIPS, cols)
    out_map = (lambda i, s: (i, s[0])) if cols_sharded else (lambda i, s: (s[0] * nb + i, 0))

    def body(s_ref, w_ref, *rest):
        rest[-1][...] = w_ref[...].astype(rest[-1].dtype)

    extra = [] if after is None else [after]
    return pl.pallas_call(
        body, name=name,
        grid_spec=pltpu.PrefetchScalarGridSpec(
            num_scalar_prefetch=1, grid=(nb,),
            in_specs=[pl.BlockSpec((tr, cols), lambda i, s: (i, 0))] + [ANY] * len(extra),
            out_specs=pl.BlockSpec((tr, cols), out_map)),
        out_shape=jax.ShapeDtypeStruct(full, BF16),
        compiler_params=_params("parallel"),
    )(chip, w, *extra)


def _gather_w_in(w_in, conv_w):
    half, cs, n_p = D_MODEL // 2, IN_COLS // N_CHIPS, 3

    def body(w_alias, cv_ref, w_ref, cvf_ref, send_sems, recv_sems, local_sem):
        x, y, c, me = _place()
        sibling = (x, y, 1 - c)
        chips = _other_chips(x, y)
        blk = lambda chip, h: w_ref.at[pl.ds(h * half, half), pl.ds(chip * cs, cs)]

        def copy(k, src, dst, to):
            return pltpu.make_async_remote_copy(src_ref=src, dst_ref=dst, send_sem=send_sems.at[k],
                                                recv_sem=recv_sems.at[k], device_id=to, device_id_type=MESH)

        own_cv = pltpu.make_async_copy(cv_ref, cvf_ref.at[me], local_sem)
        own_cv.start()
        first = [copy(j, blk(me, c), blk(me, c), (px, py, c)) for j, (px, py) in enumerate(chips)]
        first += [copy(2 * n_p + j, cv_ref, cvf_ref.at[me], (px, py, c)) for j, (px, py) in enumerate(chips)]
        for cp in first:
            cp.start()
        passed = []
        for j, (px, py) in enumerate(chips):
            got = blk(2 * px + py, c)
            copy(j, got, got, (px, py, c)).wait_recv()
            passed.append(copy(n_p + j, got, got, sibling))
            passed[-1].start()
        for j, (px, py) in enumerate(chips):
            got = blk(2 * px + py, 1 - c)
            copy(n_p + j, got, got, sibling).wait_recv()
            copy(2 * n_p + j, cv_ref, cvf_ref.at[2 * px + py], (px, py, c)).wait_recv()
        for cp in first + passed:
            cp.wait_send()
        own_cv.wait()

    return pl.pallas_call(
        body, name="gather_w_in", in_specs=[ANY, ANY], out_specs=[ANY, ANY],
        out_shape=[jax.ShapeDtypeStruct(w_in.shape, w_in.dtype), jax.ShapeDtypeStruct((N_CHIPS,) + conv_w.shape, conv_w.dtype)],
        input_output_aliases={0: 0},
        scratch_shapes=[pltpu.SemaphoreType.DMA((3 * n_p,)), pltpu.SemaphoreType.DMA((3 * n_p,)), pltpu.SemaphoreType.DMA],
    )(w_in, conv_w)


HBM = pl.BlockSpec(memory_space=pltpu.HBM)
SEM = pl.BlockSpec(memory_space=pltpu.SEMAPHORE)
EFFECT = pltpu.SideEffectType.DATAFLOW_SIDE_EFFECTING


class _Split:
    def __init__(self, name, arrays, plan):
        n, n_copies = len(arrays), plan.count
        self.name, self.plan, self.n = name, plan, n

        def body(*refs):
            send_sems, recv_sems, token = refs[n], refs[n + 1], refs[-1]
            for k, (src, dst, to) in enumerate(plan(refs[:n])):
                pltpu.make_async_remote_copy(src_ref=src, dst_ref=dst, send_sem=send_sems.at[k], recv_sem=recv_sems.at[k],
                                             device_id=to, device_id_type=MESH).start()
            token[...] = jnp.zeros_like(token)

        outs = pl.pallas_call(
            body, name=name + "_start",
            out_shape=(pltpu.SemaphoreType.DMA((n_copies,)), pltpu.SemaphoreType.DMA((n_copies,)),
                       *[pltpu.HBM(a.shape, a.dtype) for a in arrays], jax.ShapeDtypeStruct((SUBLANES, LANES), F32)),
            in_specs=(HBM,) * n, out_specs=(SEM, SEM) + (HBM,) * n + (pl.BlockSpec(memory_space=pltpu.VMEM),),
            input_output_aliases={i: 2 + i for i in range(n)},
            compiler_params=pltpu.CompilerParams(has_side_effects=EFFECT),
        )(*[pltpu.with_memory_space_constraint(a, pltpu.HBM) for a in arrays])
        self.sems, self.arrays, self.token = outs[:2], outs[2:2 + n], outs[-1]

    def wait(self, after):
        n, plan = self.n, self.plan

        def body(*refs):
            send_sems, recv_sems = refs[n], refs[n + 1]
            for k, (src, dst, to) in enumerate(plan(refs[:n])):
                cp = pltpu.make_async_remote_copy(src_ref=src, dst_ref=dst, send_sem=send_sems.at[k],
                                                  recv_sem=recv_sems.at[k], device_id=to, device_id_type=MESH)
                cp.wait_send()
                cp.wait_recv()

        return pl.pallas_call(
            body, name=self.name + "_wait", out_shape=tuple(pltpu.HBM(a.shape, a.dtype) for a in self.arrays),
            in_specs=(HBM,) * n + (SEM, SEM, ANY), out_specs=(HBM,) * n, input_output_aliases={i: i for i in range(n)},
            compiler_params=pltpu.CompilerParams(has_side_effects=EFFECT),
        )(*self.arrays, *self.sems, after)


COLS_SHARDED = (True, False, True, False)
HALF_SHAPES = [(D_MODEL // 2, IN_COLS), (D_MODEL, D_MODEL // 2), (D_MODEL // 2, D_FF), (D_FF, D_MODEL // 2)]
PIECE_SHAPES = [(D_MODEL // 2, IN_COLS // N_CHIPS), (D_MODEL // N_CHIPS, D_MODEL // 2),
                (D_MODEL // 2, D_FF // N_CHIPS), (D_FF // N_CHIPS, D_MODEL // 2)]


def _shard_view(kind, ref, chip):
    if COLS_SHARDED[kind]:
        n = ref.shape[1] // N_CHIPS
        return ref.at[:, pl.ds(chip * n, n)]
    n = ref.shape[0] // N_CHIPS
    return ref.at[pl.ds(chip * n, n), :]


def _half_view(kind, ref, h):
    if COLS_SHARDED[kind]:
        n = ref.shape[0] // 2
        return ref.at[pl.ds(h * n, n), :]
    n = ref.shape[1] // 2
    return ref.at[:, pl.ds(h * n, n)]


def _plan(count):
    def mark(fn):
        fn.count = count
        return fn
    return mark


def _shard_half_view(kind, ref, chip, h):
    if COLS_SHARDED[kind]:
        m, n = ref.shape[0] // 2, ref.shape[1] // N_CHIPS
        return ref.at[pl.ds(h * m, m), pl.ds(chip * n, n)]
    m = ref.shape[0] // N_CHIPS // 2
    return ref.at[pl.ds((2 * chip + h) * m, m), :]


def _gather_over_ici(kinds, weights):
    @_plan(3 * len(kinds))
    def plan(refs):
        x, y, c, me = _place()
        mine = [_shard_half_view(kind, ref, me, c) for kind, ref in zip(kinds, refs)]
        return [(v, v, (px, py, c)) for v in mine for px, py in _other_chips(x, y)]

    return _Split("gather_ici_" + "".join(map(str, kinds)), tuple(weights), plan)


def _gather_over_d2d(kinds, weights):
    @_plan(3 * len(kinds))
    def plan(refs):
        x, y, c, _ = _place()
        got = [_shard_half_view(kind, ref, 2 * px + py, c) for kind, ref in zip(kinds, refs)
               for px, py in _other_chips(x, y)]
        return [(v, v, (x, y, 1 - c)) for v in got]

    return _Split("gather_d2d_" + "".join(map(str, kinds)), tuple(weights), plan)


def _swap_halves(kinds, grads):
    @_plan(len(kinds))
    def plan(refs):
        x, y, c, _ = _place()
        return [(_half_view(kind, g, 1 - c), land, (x, y, 1 - c))
                for kind, g, land in zip(kinds, refs[:len(kinds)], refs[len(kinds):])]

    lands = [lax.empty(HALF_SHAPES[kind], F32) for kind in kinds]
    return _Split("swap_halves_" + "".join(map(str, kinds)), (*grads, *lands), plan)


def _add_half(name, g, recv, core, rows_split):
    shape = recv.shape
    tr = min(shape[0], 128 if rows_split else 256)
    nb = shape[0] // tr

    def body(c_ref, g_ref, r_ref, o_ref):
        o_ref[...] = (g_ref[...] + r_ref[...]).astype(o_ref.dtype)

    g_map = (lambda i, c_ref: (c_ref[0] * nb + i, 0)) if rows_split else (lambda i, c_ref: (i, c_ref[0]))
    blk = pl.BlockSpec((tr, shape[1]), lambda i, c_ref: (i, 0))
    return pl.pallas_call(
        body, name=name,
        grid_spec=pltpu.PrefetchScalarGridSpec(
            num_scalar_prefetch=1, grid=(nb,),
            in_specs=[pl.BlockSpec((tr, shape[1]), g_map), blk], out_specs=blk),
        out_shape=jax.ShapeDtypeStruct(shape, BF16),
        compiler_params=_params("parallel"),
    )(core, g, recv)


def _exchange_pieces(kinds, halves):
    n_p = N_CHIPS - 1

    @_plan(n_p * len(kinds))
    def plan(refs):
        x, y, c, _ = _place()
        return [(_shard_view(kind, half, 2 * px + py), land.at[j], (px, py, c))
                for j, (px, py) in enumerate(_other_chips(x, y))
                for kind, half, land in zip(kinds, refs[:len(kinds)], refs[len(kinds):])]

    lands = [lax.empty((n_p,) + PIECE_SHAPES[kind], BF16) for kind in kinds]
    return _Split("exchange_pieces_" + "".join(map(str, kinds)), (*halves, *lands), plan)


def _sum_pieces(name, half, slots, place, rows_split):
    n_p, rows, cols = slots.shape
    tr = min(rows, 256)
    nb = rows // tr
    if rows_split:
        own_map = lambda i, s: (i, s[0])
        out_map = lambda i, s: (s[1] * nb + i, 0)
        shard = (2 * rows, cols)
    else:
        own_map = lambda i, s: (s[0] * nb + i, 0)
        out_map = lambda i, s: (i, s[1])
        shard = (rows, 2 * cols)

    def body(s_ref, own_ref, slot_ref, o_ref):
        total = own_ref[...].astype(F32)
        for j in range(n_p):
            total = total + slot_ref[j].astype(F32)
        o_ref[...] = total

    return pl.pallas_call(
        body, name=name,
        grid_spec=pltpu.PrefetchScalarGridSpec(
            num_scalar_prefetch=1, grid=(nb,),
            in_specs=[pl.BlockSpec((tr, cols), own_map), pl.BlockSpec((n_p, tr, cols), lambda i, s: (0, i, 0))],
            out_specs=pl.BlockSpec((tr, cols), out_map)),
        out_shape=jax.ShapeDtypeStruct(shard, F32),
        compiler_params=_params("parallel"),
    )(place, half, slots)


def _join_halves(kinds, shards):
    @_plan(len(kinds))
    def plan(refs):
        x, y, c, _ = _place()
        return [(_half_view(kind, g, c), _half_view(kind, g, c), (x, y, 1 - c)) for kind, g in zip(kinds, refs)]

    return _Split("join_halves_" + "".join(map(str, kinds)), tuple(shards), plan)


def _sum_small(pack, after):
    n_dev = 8

    def body(p_ref, after_ref, o_ref, slots, send_sems, recv_sems):
        x, y, c, _ = _place()
        me = 4 * x + 2 * y + c
        slots[me] = p_ref[...]
        sends = []
        for m in range(1, n_dev):
            peer = ((1 - x) if m & 4 else x, (1 - y) if m & 2 else y, (1 - c) if m & 1 else c)
            sends.append(pltpu.make_async_remote_copy(
                src_ref=p_ref, dst_ref=slots.at[me], send_sem=send_sems.at[m - 1], recv_sem=recv_sems.at[m - 1],
                device_id=peer, device_id_type=MESH))
        for cp in sends:
            cp.start()
        for m in range(1, n_dev):
            peer = ((1 - x) if m & 4 else x, (1 - y) if m & 2 else y, (1 - c) if m & 1 else c)
            pltpu.make_async_remote_copy(
                src_ref=p_ref, dst_ref=slots.at[4 * peer[0] + 2 * peer[1] + peer[2]], send_sem=send_sems.at[m - 1],
                recv_sem=recv_sems.at[m - 1], device_id=peer, device_id_type=MESH).wait_recv()
        for cp in sends:
            cp.wait_send()
        total = slots[0]
        for d in range(1, n_dev):
            total = total + slots[d]
        o_ref[...] = total

    vm = pl.BlockSpec(memory_space=pltpu.VMEM)
    return pl.pallas_call(
        body, name="sum_small", in_specs=[vm, ANY], out_specs=vm,
        out_shape=jax.ShapeDtypeStruct(pack.shape, F32),
        scratch_shapes=[pltpu.VMEM((n_dev,) + pack.shape, F32), pltpu.SemaphoreType.DMA((n_dev - 1,)),
                        pltpu.SemaphoreType.DMA((n_dev - 1,))],
    )(pack, after)


def _adamw(name, w, g, m, v, after=None):
    rows, cols = w.shape
    tr = min(rows, 256)
    extra = [] if after is None else [after]

    def body(w_ref, g_ref, m_ref, v_ref, *rest):
        d_ref, nm_ref, nv_ref = rest[-3:]
        gv = g_ref[...]
        nm = ADAM_B1 * m_ref[...] + (1.0 - ADAM_B1) * gv
        nv = ADAM_B2 * v_ref[...] + (1.0 - ADAM_B2) * jnp.square(gv)
        m_hat = nm * (1.0 / (1.0 - ADAM_B1 ** ADAM_STEP))
        v_hat = nv * (1.0 / (1.0 - ADAM_B2 ** ADAM_STEP))
        d_ref[...] = -ADAM_LR * (m_hat / (jnp.sqrt(v_hat) + ADAM_EPS) + ADAM_WD * w_ref[...])
        nm_ref[...] = nm
        nv_ref[...] = nv

    blk = pl.BlockSpec((tr, cols), lambda i: (i, 0))
    return pl.pallas_call(
        body, name=name, grid=(rows // tr,), in_specs=[blk] * 4 + [ANY] * len(extra), out_specs=[blk] * 3,
        out_shape=[jax.ShapeDtypeStruct(w.shape, F32)] * 3,
        compiler_params=_params("parallel"),
    )(w, g, m, v, *extra)


def kernel(x, w_in, lb_logits, gate_norm_w, conv_w, w_out, ln1_g, ln1_b, w_ff1, w_ff2, ln2_g, ln2_b, loss_target, m_w_in, m_lb_logits, m_gate_norm_w, m_conv_w, m_w_out, m_ln1_g, m_ln1_b, m_w_ff1, m_w_ff2, m_ln2_g, m_ln2_b, v_w_in, v_lb_logits, v_gate_norm_w, v_conv_w, v_w_out, v_ln1_g, v_ln1_b, v_w_ff1, v_w_ff2, v_ln2_g, v_ln2_b):
    xs, tgt = x[0], loss_target[0]
    chip = 2 * lax.axis_index("x") + lax.axis_index("y")
    core = lax.axis_index("c").astype(jnp.int32).reshape(1)
    chip1 = chip.astype(jnp.int32).reshape(1)
    place = jnp.concatenate([chip1, core])

    wb_in, cv4 = _gather_w_in(_place_shard("place_w_in", w_in[0], chip1, True), conv_w[0])
    conv_full = cv4.transpose(1, 0, 2).reshape(3, CONV_WIDTH)
    ici_out = _gather_over_ici((1,), (_place_shard("place_w_out", w_out[0], chip1, False, after=wb_in),))
    ici_ff = _gather_over_ici((2, 3), (_place_shard("place_w_ff1", w_ff1[0], chip1, True, after=ici_out.token),
                                       _place_shard("place_w_ff2", w_ff2[0], chip1, False, after=ici_out.token)))

    proj, xb = _in_proj(xs, wb_in, ici_ff.token)
    o, states = _hgrn_fwd(proj, lb_logits)
    d2d_out = _gather_over_d2d((1,), ici_out.wait(o))
    cat = _gate_fwd(proj, o, gate_norm_w, conv_full, d2d_out.token)
    wb_out, = d2d_out.wait(cat)
    d2d_ff = _gather_over_d2d((2, 3), ici_ff.wait(cat))
    xhat1, h1b, rstd1 = _out_ln1(cat, wb_out, xs, ln1_g, ln1_b, d2d_ff.token)
    wb_ff1, wb_ff2 = d2d_ff.wait(xhat1)
    r, dpre2, dpre2b, g_ln2_g, g_ln2_b, loss8 = _mlp_fwd(xhat1, ln1_g, ln1_b, wb_ff1, wb_ff2, ln2_g, ln2_b, tgt)

    names = ("w_in", "w_out", "w_ff1", "w_ff2")

    def add_halves(kinds, grads, lands):
        return [_add_half("add_half_" + names[k], g, ld, core, COLS_SHARDED[k]) for k, g, ld in zip(kinds, grads, lands)]

    def sum_pieces(kinds, halves, lands):
        return [_sum_pieces("sum_pieces_" + names[k], h, ld, place, COLS_SHARDED[k]) for k, h, ld in zip(kinds, halves, lands)]

    da, dpre1, dpre1b, g_ln1_g, g_ln1_b = _mlp_bwd(dpre2, r, wb_ff1, wb_ff2, xhat1, rstd1, ln1_g)
    dcat = _out_bwd(dpre1b, wb_out)
    early = (1, 2, 3)
    swap = _swap_halves(early, (_dw_out(cat, dpre1b), _dw_ff1(h1b, da), _dw_ff2(r, dpre2b)))
    do, dpg, g_gnw, g_conv = _gate_bwd(dcat, o, proj, gate_norm_w, conv_full, swap.token)
    swapped = swap.wait(do)
    exch = _exchange_pieces(early, add_halves(early, swapped[:3], swapped[3:]))
    dph, g_lbl = _hgrn_bwd(proj, do, states, lb_logits, exch.token)
    grad_x = _in_bwd(dph, dpg, wb_in, dpre1)
    g_in_local = _dw_in(xb, dph, dpg, grad_x)

    late = (0,)
    swap = _swap_halves(late, (g_in_local,))
    exchanged = exch.wait(swap.token)
    join = _join_halves(early, sum_pieces(early, exchanged[:3], exchanged[3:]))
    pack = jnp.concatenate([
        g_ln1_g, g_ln1_b, g_ln2_g, g_ln2_b,
        jnp.concatenate([g_lbl[0:1], g_lbl[1:2]], axis=1),
        jnp.concatenate([g_gnw, g_conv[0:1]], axis=1),
        jnp.concatenate([g_conv[1:2], g_conv[2:3]], axis=1),
        jnp.concatenate([loss8[0:1], jnp.zeros((1, D_MODEL - LANES), F32)], axis=1)], axis=0)
    tot = _sum_small(pack, join.token)
    g_w_out, g_w_ff1, g_w_ff2 = join.wait(tot)
    loss = tot[7, 0]
    half = D_MODEL // 2
    g_lb_logits = jnp.concatenate([tot[4:5, :half], tot[4:5, half:]], axis=0)
    g_gate_norm_w = tot[5:6, :half]
    g_conv_full = jnp.concatenate([tot[5:6, half:], tot[6:7, :half], tot[6:7, half:]], axis=0)
    g_conv_w = lax.dynamic_slice(g_conv_full, (0, chip * LANES), (3, LANES))
    swapped = swap.wait(tot)
    exch = _exchange_pieces(late, add_halves(late, swapped[:1], swapped[1:]))
    d_ff1, nm_ff1, nv_ff1 = _adamw("adamw_w_ff1", w_ff1[0], g_w_ff1, m_w_ff1[0], v_w_ff1[0], exch.token)
    d_ff2, nm_ff2, nv_ff2 = _adamw("adamw_w_ff2", w_ff2[0], g_w_ff2, m_w_ff2[0], v_w_ff2[0], d_ff1)
    d_out, nm_out, nv_out = _adamw("adamw_w_out", w_out[0], g_w_out, m_w_out[0], v_w_out[0], d_ff2)

    def small_pack(lbl, gnw, cv, l1g, l1b, l2g, l2b):
        pad = jnp.zeros((1, D_MODEL - 3 * LANES), F32)
        return jnp.concatenate([
            l1g, l1b, l2g, l2b, jnp.concatenate([lbl[0:1], lbl[1:2]], axis=1),
            jnp.concatenate([gnw, jnp.zeros((1, half), F32)], axis=1),
            jnp.concatenate([cv[0:1], cv[1:2], cv[2:3], pad], axis=1), jnp.zeros((1, D_MODEL), F32)], axis=0)

    w_s = small_pack(lb_logits, gate_norm_w, conv_w[0], ln1_g, ln1_b, ln2_g, ln2_b)
    g_s = small_pack(g_lb_logits, g_gate_norm_w, g_conv_w, tot[0:1], tot[1:2], tot[2:3], tot[3:4])
    m_s = small_pack(m_lb_logits, m_gate_norm_w, m_conv_w[0], m_ln1_g, m_ln1_b, m_ln2_g, m_ln2_b)
    v_s = small_pack(v_lb_logits, v_gate_norm_w, v_conv_w[0], v_ln1_g, v_ln1_b, v_ln2_g, v_ln2_b)
    d_s, nm_s, nv_s = _adamw("adamw_small", w_s, g_s, m_s, v_s, d_out)
    exchanged = exch.wait(d_s)
    join = _join_halves(late, sum_pieces(late, exchanged[:1], exchanged[1:]))
    g_w_in, = join.wait(join.token)
    d_in, nm_in, nv_in = _adamw("adamw_w_in", w_in[0], g_w_in, m_w_in[0], v_w_in[0])

    def unpack(p):
        lbl = jnp.concatenate([p[4:5, :half], p[4:5, half:]], axis=0)
        cv = jnp.concatenate([p[6:7, 0:LANES], p[6:7, LANES:2 * LANES], p[6:7, 2 * LANES:3 * LANES]], axis=0)
        return dict(lb_logits=lbl, gate_norm_w=p[5:6, :half], conv_w=cv[None], ln1_g=p[0:1], ln1_b=p[1:2],
                    ln2_g=p[2:3], ln2_b=p[3:4])

    order = ("w_in", "lb_logits", "gate_norm_w", "conv_w", "w_out", "ln1_g", "ln1_b", "w_ff1", "w_ff2", "ln2_g", "ln2_b")
    grad = dict(unpack(g_s), w_in=g_w_in[None], w_out=g_w_out[None], w_ff1=g_w_ff1[None], w_ff2=g_w_ff2[None])
    delta = dict(unpack(d_s), w_in=d_in[None], w_out=d_out[None], w_ff1=d_ff1[None], w_ff2=d_ff2[None])
    new_m = dict(unpack(nm_s), w_in=nm_in[None], w_out=nm_out[None], w_ff1=nm_ff1[None], w_ff2=nm_ff2[None])
    new_v = dict(unpack(nv_s), w_in=nv_in[None], w_out=nv_out[None], w_ff1=nv_ff1[None], w_ff2=nv_ff2[None])
    return (loss, grad_x[None], *[grad[n] for n in order], *[delta[n] for n in order],
            *[new_m[n] for n in order], *[new_v[n] for n in order])
```

```python
import jax
import jax.numpy as jnp
from jax import lax
from jax.experimental import pallas as pl
from jax.experimental.pallas import tpu as pltpu

F32 = jnp.float32
BF16 = jnp.bfloat16
MXU_DTYPE = jnp.bfloat16

D_MODEL = 1024
HGRN_WIDTH = 512
HEAD_DIM = 128
N_HEADS = 4
CONV_WIDTH = 512
CHUNK = 64
D_FF = 4096
IN_COLS = 3584
GROUP = 512
N_GROUPS = IN_COLS // GROUP
ALPHA = 2.0 ** 0.25
EPS = 1e-5
N_CHIPS = 4
ADAM_LR, ADAM_B1, ADAM_B2, ADAM_EPS, ADAM_WD, ADAM_STEP = 0.001, 0.9, 0.999, 1e-08, 0.01, 10

LANES = 128
SUBLANES = 8
VMEM_LIMIT = 56 * 1024 * 1024
FF_BLOCK = 1024
N_FF = D_FF // FF_BLOCK
NN = (((1,), (0,)), ((), ()))
NT = (((1,), (1,)), ((), ()))
TN = (((0,), (0,)), ((), ()))
MESH = pl.DeviceIdType.MESH
ANY = pl.BlockSpec(memory_space=pl.ANY)


def _dot(a, b, dims):
    return lax.dot_general(a.astype(MXU_DTYPE), b.astype(MXU_DTYPE), dims, preferred_element_type=F32)


def _dot_exact(ones, v):
    ones = ones.astype(jnp.bfloat16)
    hi = v.astype(jnp.bfloat16)
    rest = v - hi.astype(F32)
    mid = rest.astype(jnp.bfloat16)
    low = (rest - mid.astype(F32)).astype(jnp.bfloat16)
    return sum(lax.dot_general(ones, part, NN, preferred_element_type=F32) for part in (hi, mid, low))


def _params(*sem):
    return pltpu.CompilerParams(dimension_semantics=sem, vmem_limit_bytes=VMEM_LIMIT)


def _resident(shape):
    return pl.BlockSpec(shape, lambda *_: (0,) * len(shape), pipeline_mode=pl.Buffered(1))


def _sigmoid(v):
    return 1.0 / (1.0 + jnp.exp(-v))


def _lower_bound(lbl):
    m = jnp.max(lbl, axis=0, keepdims=True)
    e = jnp.exp(lbl - m)
    s = e / jnp.sum(e, axis=0, keepdims=True)
    return s[0:1, :], s[1:2, :]


def _heads(v):
    return [v[:, h * HEAD_DIM:(h + 1) * HEAD_DIM] for h in range(N_HEADS)]


def _per_head(fn, *arrays):
    return jnp.concatenate([fn(*parts) for parts in zip(*map(_heads, arrays))], axis=1)


def _in_proj(x, w_in, after):
    t = x.shape[0]
    tm = min(t, 512)

    def body(x_ref, w_ref, after_ref, o_ref, xb_ref):
        xb = x_ref[...].astype(xb_ref.dtype)
        xb_ref[...] = xb
        for g in range(N_GROUPS):
            o_ref[g] = _dot(xb, w_ref[:, g * GROUP:(g + 1) * GROUP], NN)

    return pl.pallas_call(
        body, name="in_proj", grid=(t // tm,),
        in_specs=[pl.BlockSpec((tm, D_MODEL), lambda i: (i, 0)), _resident((D_MODEL, IN_COLS)), ANY],
        out_specs=[pl.BlockSpec((N_GROUPS, tm, GROUP), lambda i: (0, i, 0)), pl.BlockSpec((tm, D_MODEL), lambda i: (i, 0))],
        out_shape=[jax.ShapeDtypeStruct((N_GROUPS, t, GROUP), F32), jax.ShapeDtypeStruct((t, D_MODEL), BF16)],
        compiler_params=_params("parallel"),
    )(x, w_in, after)


def _gates(fp, lb):
    sig = _sigmoid(fp)
    f = lb + (1.0 - lb) * sig
    return sig, f, jnp.log(f), 1.0 - f


def _chunk_masks():
    row = lax.broadcasted_iota(jnp.int32, (CHUNK, CHUNK), 0)
    col = lax.broadcasted_iota(jnp.int32, (CHUNK, CHUNK), 1)
    return row >= col, row <= col


def _hgrn_fwd(proj, lb_logits):
    t = proj.shape[1]
    tb = min(t, 512)
    ncb = tb // CHUNK

    def body(q_ref, f_ref, v_ref, lbl_ref, o_ref, st_ref, s_scr):
        @pl.when(pl.program_id(0) == 0)
        def _():
            s_scr[...] = jnp.zeros_like(s_scr)

        lb, _ = _lower_bound(lbl_ref[...])
        causal, _ = _chunk_masks()

        every = range(ncb)
        rows = [slice(c * CHUNK, (c + 1) * CHUNK) for c in every]
        q, v = [q_ref[r, :] for r in rows], [v_ref[r, :] for r in rows]
        gates = [_gates(f_ref[r, :], lb) for r in rows]
        k = [gt[3] for gt in gates]
        b = [_dot_exact(causal, gt[2]) for gt in gates]
        mid, last = [x[CHUNK // 2:CHUNK // 2 + 1, :] for x in b], [x[CHUNK - 1:CHUNK, :] for x in b]
        qt = [q[c] * jnp.exp(b[c] - mid[c]) for c in every]
        kt = [k[c] * jnp.exp(mid[c] - b[c]) for c in every]
        qi = [q[c] * jnp.exp(b[c]) for c in every]
        ks = [k[c] * jnp.exp(last[c] - b[c]) for c in every]
        dec = [jnp.exp(x) for x in last]
        scores = [[jnp.where(causal, _dot(a, b_, NT), 0.0) for a, b_ in zip(_heads(qt[c]), _heads(kt[c]))] for c in every]
        intra = [[_dot(s, v_h, NN) for s, v_h in zip(scores[c], _heads(v[c]))] for c in every]
        update = [_per_head(lambda v_h, ks_h: _dot(v_h, ks_h, TN), v[c], ks[c]) for c in every]

        st = s_scr[...]
        states = []
        for c in every:
            states.append(st)
            st_ref[c] = st
            st = dec[c] * st + update[c]
        s_scr[...] = st

        o_ref[...] = jnp.concatenate(
            [jnp.concatenate([i_h + _dot(qi_h, st_h, NT) for i_h, qi_h, st_h in
                              zip(intra[c], _heads(qi[c]), _heads(states[c]))], axis=1) for c in every], axis=0)

    grp = lambda g: pl.BlockSpec((None, tb, GROUP), lambda i: (g, i, 0))
    return pl.pallas_call(
        body, name="hgrn_fwd", grid=(t // tb,),
        in_specs=[grp(0), grp(1), grp(2), pl.BlockSpec((2, HGRN_WIDTH), lambda i: (0, 0))],
        out_specs=[pl.BlockSpec((tb, HGRN_WIDTH), lambda i: (i, 0)),
                   pl.BlockSpec((ncb, HEAD_DIM, HGRN_WIDTH), lambda i: (i, 0, 0))],
        out_shape=[jax.ShapeDtypeStruct((t, HGRN_WIDTH), F32),
                   jax.ShapeDtypeStruct((t // CHUNK, HEAD_DIM, HGRN_WIDTH), F32)],
        scratch_shapes=[pltpu.VMEM((HEAD_DIM, HGRN_WIDTH), F32)],
        compiler_params=_params("arbitrary"),
    )(proj, proj, proj, lb_logits)


def _conv_taps(z, halo, zbuf, tb):
    zbuf[0:SUBLANES, :] = halo
    zbuf[SUBLANES:SUBLANES + tb, :] = z
    return zbuf[SUBLANES - 1:SUBLANES - 1 + tb, :], zbuf[SUBLANES - 2:SUBLANES - 2 + tb, :]


def _gate_fwd(proj, o, gate_norm_w, conv_w, after):
    t = proj.shape[1]
    tb = min(t, 512)
    hb = tb // SUBLANES

    def body(o_ref, og_ref, gnw_ref, b_ref, c_ref, u_ref, ch_ref, uh_ref, cw_ref, after_ref, cat_ref, zbuf):
        i = pl.program_id(0)
        og = og_ref[...]
        on = _per_head(lambda o_h: o_h * lax.rsqrt(jnp.mean(o_h * o_h, axis=-1, keepdims=True) + EPS), o_ref[...])
        cat_ref[0] = (on * gnw_ref[...] * (og * _sigmoid(og))).astype(cat_ref.dtype)
        z = c_ref[...] * u_ref[...]
        halo = jnp.where(i > 0, ch_ref[...] * uh_ref[...], 0.0)
        z1, z2 = _conv_taps(z, halo, zbuf, tb)
        cw = cw_ref[...]
        yc = cw[2:3, :] * z + cw[1:2, :] * z1 + cw[0:1, :] * z2
        cat_ref[1] = (b_ref[...] * yc).astype(cat_ref.dtype)

    grp = lambda g: pl.BlockSpec((None, tb, GROUP), lambda i: (g, i, 0))
    prev = lambda g: pl.BlockSpec((None, SUBLANES, GROUP), lambda i: (g, jnp.maximum(i * hb - 1, 0), 0))
    vec = lambda r: pl.BlockSpec((r, GROUP), lambda i: (0, 0))
    return pl.pallas_call(
        body, name="gate_fwd", grid=(t // tb,),
        in_specs=[pl.BlockSpec((tb, GROUP), lambda i: (i, 0)), grp(3), vec(1), grp(4), grp(5), grp(6), prev(5), prev(6),
                  vec(3), ANY],
        out_specs=pl.BlockSpec((2, tb, GROUP), lambda i: (0, i, 0)),
        out_shape=jax.ShapeDtypeStruct((2, t, HGRN_WIDTH), BF16),
        scratch_shapes=[pltpu.VMEM((tb + SUBLANES, GROUP), F32)],
        compiler_params=_params("parallel"),
    )(o, proj, gate_norm_w, proj, proj, proj, proj, proj, conv_w, after)


def _out_ln1(cat, w_out, x, g1, b1, after):
    t = x.shape[0]
    tm = min(t, 512)

    def body(cat_ref, w_ref, x_ref, g_ref, b_ref, after_ref, xhat_ref, h1_ref, rstd_ref):
        mix = _dot(cat_ref[0], w_ref[0:GROUP, :], NN) + _dot(cat_ref[1], w_ref[GROUP:2 * GROUP, :], NN)
        pre = ALPHA * x_ref[...] + mix
        xc = pre - jnp.mean(pre, axis=-1, keepdims=True)
        rstd = lax.rsqrt(jnp.mean(xc * xc, axis=-1, keepdims=True) + EPS)
        xhat = xc * rstd
        xhat_ref[...] = xhat
        h1_ref[...] = (xhat * g_ref[...] + b_ref[...]).astype(h1_ref.dtype)
        rstd_ref[...] = rstd

    row = pl.BlockSpec((tm, D_MODEL), lambda i: (i, 0))
    vec = pl.BlockSpec((1, D_MODEL), lambda i: (0, 0))
    return pl.pallas_call(
        body, name="out_ln1", grid=(t // tm,),
        in_specs=[pl.BlockSpec((2, tm, GROUP), lambda i: (0, i, 0)), _resident((D_MODEL, D_MODEL)), row, vec, vec, ANY],
        out_specs=[row, row, pl.BlockSpec((tm, 1), lambda i: (i, 0))],
        out_shape=[jax.ShapeDtypeStruct((t, D_MODEL), F32), jax.ShapeDtypeStruct((t, D_MODEL), BF16),
                   jax.ShapeDtypeStruct((t, 1), F32)],
        compiler_params=_params("parallel"),
    )(cat, w_out, x, g1, b1, after)


def _ln_bwd(dy, xhat, rstd, g):
    dxhat = dy * g
    m1 = jnp.mean(dxhat, axis=-1, keepdims=True)
    m2 = jnp.mean(dxhat * xhat, axis=-1, keepdims=True)
    return rstd * (dxhat - m1 - xhat * m2)


def _mlp_fwd(xhat1, g1, b1, w_ff1, w_ff2, g2, b2, target):
    t = xhat1.shape[0]
    tm = min(t, 256)

    def body(xh_ref, g1_ref, b1_ref, w1_ref, w2_ref, g2_ref, b2_ref, tg_ref,
             r_ref, dpre_ref, dpreb_ref, dg_ref, db_ref, loss_ref):
        @pl.when(pl.program_id(0) == 0)
        def _():
            dg_ref[...] = jnp.zeros_like(dg_ref)
            db_ref[...] = jnp.zeros_like(db_ref)
            loss_ref[...] = jnp.zeros_like(loss_ref)

        h1 = xh_ref[...] * g1_ref[...] + b1_ref[...]
        h1b = h1.astype(MXU_DTYPE)
        mlp = jnp.zeros((tm, D_MODEL), F32)
        for j in range(N_FF):
            cols = slice(j * FF_BLOCK, (j + 1) * FF_BLOCK)
            r = jnp.square(jnp.maximum(_dot(h1b, w1_ref[:, cols], NN), 0.0)).astype(r_ref.dtype)
            r_ref[:, cols] = r
            mlp = mlp + _dot(r, w2_ref[cols, :], NN)
        pre = ALPHA * h1 + mlp
        xc = pre - jnp.mean(pre, axis=-1, keepdims=True)
        rstd = lax.rsqrt(jnp.mean(xc * xc, axis=-1, keepdims=True) + EPS)
        xhat = xc * rstd
        err = xhat * g2_ref[...] + b2_ref[...] - tg_ref[...]
        loss_ref[...] += 0.5 * jnp.sum(jnp.mean(err * err, axis=-1, keepdims=True))
        dy = err * (1.0 / D_MODEL)
        dg_ref[...] += jnp.sum(dy * xhat, axis=0, keepdims=True)
        db_ref[...] += jnp.sum(dy, axis=0, keepdims=True)
        dpre = _ln_bwd(dy, xhat, rstd, g2_ref[...])
        dpre_ref[...] = dpre
        dpreb_ref[...] = dpre.astype(dpreb_ref.dtype)

    row = pl.BlockSpec((tm, D_MODEL), lambda i: (i, 0))
    vec = pl.BlockSpec((1, D_MODEL), lambda i: (0, 0))
    return pl.pallas_call(
        body, name="mlp_fwd", grid=(t // tm,),
        in_specs=[row, vec, vec, _resident((D_MODEL, D_FF)), _resident((D_FF, D_MODEL)), vec, vec, row],
        out_specs=[pl.BlockSpec((tm, D_FF), lambda i: (i, 0)), row, row, vec, vec,
                   pl.BlockSpec((SUBLANES, LANES), lambda i: (0, 0))],
        out_shape=[jax.ShapeDtypeStruct((t, D_FF), BF16), jax.ShapeDtypeStruct((t, D_MODEL), F32),
                   jax.ShapeDtypeStruct((t, D_MODEL), BF16), jax.ShapeDtypeStruct((1, D_MODEL), F32),
                   jax.ShapeDtypeStruct((1, D_MODEL), F32), jax.ShapeDtypeStruct((SUBLANES, LANES), F32)],
        compiler_params=_params("arbitrary"),
    )(xhat1, g1, b1, w_ff1, w_ff2, g2, b2, target)


def _mlp_bwd(dpre2, r, w_ff1, w_ff2, xhat1, rstd1, g1):
    t = r.shape[0]
    tm = min(t, 256)

    def body(dp2_ref, r_ref, w1_ref, w2_ref, xh_ref, rs_ref, g_ref, da_ref, dpre_ref, dpreb_ref, dg_ref, db_ref):
        @pl.when(pl.program_id(0) == 0)
        def _():
            dg_ref[...] = jnp.zeros_like(dg_ref)
            db_ref[...] = jnp.zeros_like(db_ref)

        dp2 = dp2_ref[...]
        dp2b = dp2.astype(MXU_DTYPE)
        back = jnp.zeros((tm, D_MODEL), F32)
        for j in range(N_FF):
            cols = slice(j * FF_BLOCK, (j + 1) * FF_BLOCK)
            dr = _dot(dp2b, w2_ref[cols, :], NT)
            da = (dr * (2.0 * jnp.sqrt(r_ref[:, cols].astype(F32)))).astype(da_ref.dtype)
            da_ref[:, cols] = da
            back = back + _dot(da, w1_ref[:, cols], NT)
        dh1 = ALPHA * dp2 + back
        xhat = xh_ref[...]
        dg_ref[...] += jnp.sum(dh1 * xhat, axis=0, keepdims=True)
        db_ref[...] += jnp.sum(dh1, axis=0, keepdims=True)
        dpre = _ln_bwd(dh1, xhat, rs_ref[...], g_ref[...])
        dpre_ref[...] = dpre
        dpreb_ref[...] = dpre.astype(dpreb_ref.dtype)

    row = pl.BlockSpec((tm, D_MODEL), lambda i: (i, 0))
    wide = pl.BlockSpec((tm, D_FF), lambda i: (i, 0))
    vec = pl.BlockSpec((1, D_MODEL), lambda i: (0, 0))
    return pl.pallas_call(
        body, name="mlp_bwd", grid=(t // tm,),
        in_specs=[row, wide, _resident((D_MODEL, D_FF)), _resident((D_FF, D_MODEL)), row,
                  pl.BlockSpec((tm, 1), lambda i: (i, 0)), vec],
        out_specs=[wide, row, row, vec, vec],
        out_shape=[jax.ShapeDtypeStruct((t, D_FF), BF16), jax.ShapeDtypeStruct((t, D_MODEL), F32),
                   jax.ShapeDtypeStruct((t, D_MODEL), BF16), jax.ShapeDtypeStruct((1, D_MODEL), F32),
                   jax.ShapeDtypeStruct((1, D_MODEL), F32)],
        compiler_params=_params("arbitrary"),
    )(dpre2, r, w_ff1, w_ff2, xhat1, rstd1, g1)


def _out_bwd(dpre1b, w_out):
    t = dpre1b.shape[0]
    tm = min(t, 512)

    def body(d_ref, w_ref, o_ref):
        o_ref[...] = _dot(d_ref[...], w_ref[...], NT)

    return pl.pallas_call(
        body, name="out_bwd", grid=(t // tm,),
        in_specs=[pl.BlockSpec((tm, D_MODEL), lambda i: (i, 0)), _resident((D_MODEL, D_MODEL))],
        out_specs=pl.BlockSpec((tm, D_MODEL), lambda i: (i, 0)),
        out_shape=jax.ShapeDtypeStruct((t, D_MODEL), F32),
        compiler_params=_params("parallel"),
    )(dpre1b, w_out)


def _gate_bwd(dcat, o, proj, gate_norm_w, conv_w, after):
    t = proj.shape[1]
    tb = min(t, 512)
    hb = tb // SUBLANES
    nblk = t // tb

    def body(do2_ref, dy_ref, dyn_ref, o_ref, og_ref, gnw_ref, b_ref, bn_ref, c_ref, u_ref, ch_ref, uh_ref, cw_ref,
             after_ref, do_ref, dp_ref, dgnw_ref, dcw_ref, zbuf, dbuf):
        i = pl.program_id(0)

        @pl.when(i == 0)
        def _():
            dgnw_ref[...] = jnp.zeros_like(dgnw_ref)
            dcw_ref[...] = jnp.zeros_like(dcw_ref)

        ov, og, gnw, do2 = o_ref[...], og_ref[...], gnw_ref[...], do2_ref[...]
        rs = _per_head(lambda o_h: jnp.broadcast_to(lax.rsqrt(jnp.mean(o_h * o_h, axis=-1, keepdims=True) + EPS),
                                                    o_h.shape), ov)
        on = ov * rs
        sg = _sigmoid(og)
        sil = og * sg
        don = do2 * gnw * sil
        dgnw_ref[...] += jnp.sum(do2 * on * sil, axis=0, keepdims=True)
        dp_ref[0] = (do2 * on * gnw * (sg * (1.0 + og * (1.0 - sg)))).astype(dp_ref.dtype)
        do_ref[...] = rs * (don - on * _per_head(
            lambda p_h: jnp.broadcast_to(jnp.mean(p_h, axis=-1, keepdims=True), p_h.shape), don * on))

        bg, cg, u, dy = b_ref[...], c_ref[...], u_ref[...], dy_ref[...]
        z = cg * u
        halo = jnp.where(i > 0, ch_ref[...] * uh_ref[...], 0.0)
        z1, z2 = _conv_taps(z, halo, zbuf, tb)
        cw = cw_ref[...]
        yc = cw[2:3, :] * z + cw[1:2, :] * z1 + cw[0:1, :] * z2
        dyc = dy * bg
        dbuf[0:tb, :] = dyc
        dbuf[tb:tb + SUBLANES, :] = jnp.where(i < nblk - 1, dyn_ref[...] * bn_ref[...], 0.0)
        d1, d2 = dbuf[1:1 + tb, :], dbuf[2:2 + tb, :]
        dz = cw[2:3, :] * dyc + cw[1:2, :] * d1 + cw[0:1, :] * d2
        dp_ref[1] = (dy * yc).astype(dp_ref.dtype)
        dp_ref[2] = (dz * u).astype(dp_ref.dtype)
        dp_ref[3] = (dz * cg).astype(dp_ref.dtype)
        dcw_ref[0:1, :] += jnp.sum(dyc * z2, axis=0, keepdims=True)
        dcw_ref[1:2, :] += jnp.sum(dyc * z1, axis=0, keepdims=True)
        dcw_ref[2:3, :] += jnp.sum(dyc * z, axis=0, keepdims=True)

    half = lambda g: pl.BlockSpec((tb, GROUP), lambda i: (i, g))
    grp = lambda g: pl.BlockSpec((None, tb, GROUP), lambda i: (g, i, 0))
    prev = lambda g: pl.BlockSpec((None, SUBLANES, GROUP), lambda i: (g, jnp.maximum(i * hb - 1, 0), 0))
    nxt_row = lambda i: jnp.minimum((i + 1) * hb, t // SUBLANES - 1)
    nxt = lambda g: pl.BlockSpec((None, SUBLANES, GROUP), lambda i: (g, nxt_row(i), 0))
    vec = lambda r: pl.BlockSpec((r, GROUP), lambda i: (0, 0))
    return pl.pallas_call(
        body, name="gate_bwd", grid=(nblk,),
        in_specs=[half(0), half(1), pl.BlockSpec((SUBLANES, GROUP), lambda i: (nxt_row(i), 1)), half(0), grp(3), vec(1),
                  grp(4), nxt(4), grp(5), grp(6), prev(5), prev(6), vec(3), ANY],
        out_specs=[half(0), pl.BlockSpec((4, tb, GROUP), lambda i: (0, i, 0)), vec(1), vec(3)],
        out_shape=[jax.ShapeDtypeStruct((t, HGRN_WIDTH), F32), jax.ShapeDtypeStruct((4, t, HGRN_WIDTH), BF16),
                   jax.ShapeDtypeStruct((1, HGRN_WIDTH), F32), jax.ShapeDtypeStruct((3, CONV_WIDTH), F32)],
        scratch_shapes=[pltpu.VMEM((tb + SUBLANES, GROUP), F32), pltpu.VMEM((tb + SUBLANES, GROUP), F32)],
        compiler_params=_params("arbitrary"),
    )(dcat, dcat, dcat, o, proj, gate_norm_w, proj, proj, proj, proj, proj, proj, conv_w, after)


def _hgrn_bwd(proj, do, states, lb_logits, after):
    t = proj.shape[1]
    tb = min(t, 512)
    ncb = tb // CHUNK
    nblk = t // tb

    def body(q_ref, f_ref, v_ref, do_ref, st_ref, lbl_ref, after_ref, dp_ref, dlbl_ref, ds_scr, dlb_scr):
        i = pl.program_id(0)

        @pl.when(i == 0)
        def _():
            ds_scr[...] = jnp.zeros_like(ds_scr)
            dlb_scr[...] = jnp.zeros_like(dlb_scr)

        lb, s1 = _lower_bound(lbl_ref[...])
        causal, anti = _chunk_masks()
        every = range(ncb)
        rows = [slice(c * CHUNK, (c + 1) * CHUNK) for c in every]
        q, v, do = ([ref[r, :] for r in rows] for ref in (q_ref, v_ref, do_ref))
        st = [st_ref[c] for c in every]
        gates = [_gates(f_ref[r, :], lb) for r in rows]
        sig, f, k = ([gt[n] for gt in gates] for n in (0, 1, 3))
        b = [_dot_exact(causal, gt[2]) for gt in gates]
        mid, last = [x[CHUNK // 2:CHUNK // 2 + 1, :] for x in b], [x[CHUNK - 1:CHUNK, :] for x in b]
        e_q = [jnp.exp(b[c] - mid[c]) for c in every]
        e_k = [jnp.exp(mid[c] - b[c]) for c in every]
        e_i = [jnp.exp(x) for x in b]
        e_s = [jnp.exp(last[c] - b[c]) for c in every]
        dec = [jnp.exp(x) for x in last]
        qt, kt, qi, ks = ([a[c] * e[c] for c in every] for a, e in ((q, e_q), (k, e_k), (q, e_i), (k, e_s)))

        def masked(a, b_):
            return [[jnp.where(causal, _dot(a_h, b_h, NT), 0.0) for a_h, b_h in zip(_heads(a[c]), _heads(b_[c]))]
                    for c in every]

        def with_scores(s, other, dims):
            return [jnp.concatenate([_dot(s_h, o_h, dims) for s_h, o_h in zip(s[c], _heads(other[c]))], axis=1)
                    for c in every]

        def per_head(dims, a, b_):
            return [_per_head(lambda a_h, b_h: _dot(a_h, b_h, dims), a[c], b_[c]) for c in every]

        scores, dscores = masked(qt, kt), masked(do, v)
        dqt, dkt, dv_intra = with_scores(dscores, kt, NN), with_scores(dscores, qt, TN), with_scores(scores, do, TN)
        dqi, update = per_head(NN, do, st), per_head(TN, do, qi)

        dst = ds_scr[...]
        dsts = [None] * ncb
        for c in reversed(every):
            dsts[c] = dst
            dst = dec[c] * dst + update[c]
        ds_scr[...] = dst

        dv_state, dks = per_head(NT, ks, dsts), per_head(NN, v, dsts)
        ddec = [jnp.sum(dsts[c] * st[c], axis=0, keepdims=True) for c in every]
        dq = [dqt[c] * e_q[c] + dqi[c] * e_i[c] for c in every]
        dk = [dkt[c] * e_k[c] + dks[c] * e_s[c] for c in every]
        db = [q[c] * dq[c] - k[c] * dk[c] for c in every]
        db_last = [jnp.sum(dks[c] * ks[c], axis=0, keepdims=True) + ddec[c] * dec[c] for c in every]
        dg = [_dot_exact(anti, db[c]) + db_last[c] for c in every]
        df = [dg[c] / f[c] - dk[c] for c in every]
        dlb_scr[...] += sum(jnp.sum(df[c] * (1.0 - sig[c]), axis=0, keepdims=True) for c in every)
        dfp = [df[c] * (1.0 - lb) * sig[c] * (1.0 - sig[c]) for c in every]
        dv = [dv_intra[c] + dv_state[c] for c in every]
        for n, parts in enumerate((dq, dfp, dv)):
            dp_ref[n] = jnp.concatenate(parts, axis=0).astype(dp_ref.dtype)

        @pl.when(i == nblk - 1)
        def _():
            dlb = dlb_scr[...]
            dlbl_ref[0:1, :] = dlb * lb * (1.0 - lb)
            dlbl_ref[1:2, :] = -dlb * lb * s1

    grp = lambda g: pl.BlockSpec((None, tb, GROUP), lambda i: (g, nblk - 1 - i, 0))
    vec = pl.BlockSpec((2, HGRN_WIDTH), lambda i: (0, 0))
    return pl.pallas_call(
        body, name="hgrn_bwd", grid=(nblk,),
        in_specs=[grp(0), grp(1), grp(2), pl.BlockSpec((tb, HGRN_WIDTH), lambda i: (nblk - 1 - i, 0)),
                  pl.BlockSpec((ncb, HEAD_DIM, HGRN_WIDTH), lambda i: (nblk - 1 - i, 0, 0)), vec, ANY],
        out_specs=[pl.BlockSpec((3, tb, HGRN_WIDTH), lambda i: (0, nblk - 1 - i, 0)), vec],
        out_shape=[jax.ShapeDtypeStruct((3, t, HGRN_WIDTH), BF16), jax.ShapeDtypeStruct((2, HGRN_WIDTH), F32)],
        scratch_shapes=[pltpu.VMEM((HEAD_DIM, HGRN_WIDTH), F32), pltpu.VMEM((1, HGRN_WIDTH), F32)],
        compiler_params=_params("arbitrary"),
    )(proj, proj, proj, do, states, lb_logits, after)


def _in_bwd(dph, dpg, w_in, dpre1):
    t = dpre1.shape[0]
    tm = min(t, 512)

    def body(dh_ref, dg_ref, w_ref, dp_ref, o_ref):
        acc = ALPHA * dp_ref[...]
        for g in range(N_GROUPS):
            part = dh_ref[g] if g < 3 else dg_ref[g - 3]
            acc = acc + _dot(part, w_ref[:, g * GROUP:(g + 1) * GROUP], NT)
        o_ref[...] = acc

    row = pl.BlockSpec((tm, D_MODEL), lambda i: (i, 0))
    return pl.pallas_call(
        body, name="in_bwd", grid=(t // tm,),
        in_specs=[pl.BlockSpec((3, tm, GROUP), lambda i: (0, i, 0)), pl.BlockSpec((4, tm, GROUP), lambda i: (0, i, 0)),
                  _resident((D_MODEL, IN_COLS)), row],
        out_specs=row,
        out_shape=jax.ShapeDtypeStruct((t, D_MODEL), F32),
        compiler_params=_params("parallel"),
    )(dph, dpg, w_in, dpre1)


def _grad_w(name, operands, widths, shape, step, after=None):
    t = operands[0].shape[-2]
    tt = min(t, 512)
    n_in, n_steps = len(operands), t // tt
    in_specs = [pl.BlockSpec((tt, w), lambda k: (k, 0)) if a.ndim == 2 else
                pl.BlockSpec((a.shape[0], tt, w), lambda k: (0, k, 0)) for a, w in zip(operands, widths)]
    extra = [] if after is None else [after]

    def body(*refs):
        o_ref, acc, sem = refs[-3:]
        k = pl.program_id(0)

        @pl.when(k == 0)
        def _():
            acc[...] = jnp.zeros_like(acc)

        step(acc, *refs[:n_in])

        @pl.when(k == n_steps - 1)
        def _():
            out = pltpu.make_async_copy(acc, o_ref, sem)
            out.start()
            out.wait()

    return pl.pallas_call(
        body, name=name, grid=(n_steps,), in_specs=in_specs + [ANY] * len(extra), out_specs=ANY,
        out_shape=jax.ShapeDtypeStruct(shape, F32),
        scratch_shapes=[pltpu.VMEM(shape, F32), pltpu.SemaphoreType.DMA],
        compiler_params=_params("arbitrary"),
    )(*operands, *extra)


def _dw_in(xb, dph, dpg, after):
    def step(acc, x_ref, dh_ref, dg_ref):
        xv = x_ref[...]
        for g in range(N_GROUPS):
            part = dh_ref[g] if g < 3 else dg_ref[g - 3]
            acc[:, g * GROUP:(g + 1) * GROUP] += _dot(xv, part, TN)

    return _grad_w("dw_in", (xb, dph, dpg), (D_MODEL, GROUP, GROUP), (D_MODEL, IN_COLS), step, after)


def _dw_out(cat, dpre1b):
    def step(acc, cat_ref, d_ref):
        dv = d_ref[...]
        for g in range(2):
            acc[g * GROUP:(g + 1) * GROUP, :] += _dot(cat_ref[g], dv, TN)

    return _grad_w("dw_out", (cat, dpre1b), (GROUP, D_MODEL), (D_MODEL, D_MODEL), step)


def _dw_ff1(h1b, da):
    def step(acc, h_ref, da_ref):
        hv = h_ref[...]
        for j in range(D_FF // FF_BLOCK):
            cols = slice(j * FF_BLOCK, (j + 1) * FF_BLOCK)
            acc[:, cols] += _dot(hv, da_ref[:, cols], TN)

    return _grad_w("dw_ff1", (h1b, da), (D_MODEL, D_FF), (D_MODEL, D_FF), step)


def _dw_ff2(r, dpre2b):
    def step(acc, r_ref, d_ref):
        dv = d_ref[...]
        for j in range(D_FF // FF_BLOCK):
            rows = slice(j * FF_BLOCK, (j + 1) * FF_BLOCK)
            acc[rows, :] += _dot(r_ref[:, rows], dv, TN)

    return _grad_w("dw_ff2", (r, dpre2b), (D_FF, D_MODEL), (D_FF, D_MODEL), step)


def _place():
    x, y, c = lax.axis_index("x"), lax.axis_index("y"), lax.axis_index("c")
    return x, y, c, 2 * x + y


def _other_chips(x, y):
    return [(1 - x, y), (x, 1 - y), (1 - x, 1 - y)]


def _place_shard(name, w, chip, cols_sharded, after=None):
    rows, cols = w.shape
    tr = min(rows, 256)
    nb = rows // tr
    full = (rows, cols * N_CHIPS) if cols_sharded else (rows * N_CHIPS, cols)
    out_map = (lambda i, s: (i, s[0])) if cols_sharded else (lambda i, s: (s[0] * nb + i, 0))

    def body(s_ref, w_ref, *rest):
        rest[-1][...] = w_ref[...].astype(rest[-1].dtype)

    extra = [] if after is None else [after]
    return pl.pallas_call(
        body, name=name,
        grid_spec=pltpu.PrefetchScalarGridSpec(
            num_scalar_prefetch=1, grid=(nb,),
            in_specs=[pl.BlockSpec((tr, cols), lambda i, s: (i, 0))] + [ANY] * len(extra),
            out_specs=pl.BlockSpec((tr, cols), out_map)),
        out_shape=jax.ShapeDtypeStruct(full, BF16),
        compiler_params=_params("parallel"),
    )(chip, w, *extra)


def _gather_w_in(w_in, conv_w):
    half, cs, n_p = D_MODEL // 2, IN_COLS // N_CHIPS, 3

    def body(w_alias, cv_ref, w_ref, cvf_ref, send_sems, recv_sems, local_sem):
        x, y, c, me = _place()
        sibling = (x, y, 1 - c)
        chips = _other_chips(x, y)
        blk = lambda chip, h: w_ref.at[pl.ds(h * half, half), pl.ds(chip * cs, cs)]

        def copy(k, src, dst, to):
            return pltpu.make_async_remote_copy(src_ref=src, dst_ref=dst, send_sem=send_sems.at[k],
                                                recv_sem=recv_sems.at[k], device_id=to, device_id_type=MESH)

        own_cv = pltpu.make_async_copy(cv_ref, cvf_ref.at[me], local_sem)
        own_cv.start()
        first = [copy(j, blk(me, c), blk(me, c), (px, py, c)) for j, (px, py) in enumerate(chips)]
        first += [copy(2 * n_p + j, cv_ref, cvf_ref.at[me], (px, py, c)) for j, (px, py) in enumerate(chips)]
        for cp in first:
            cp.start()
        passed = []
        for j, (px, py) in enumerate(chips):
            got = blk(2 * px + py, c)
            copy(j, got, got, (px, py, c)).wait_recv()
            passed.append(copy(n_p + j, got, got, sibling))
            passed[-1].start()
        for j, (px, py) in enumerate(chips):
            got = blk(2 * px + py, 1 - c)
            copy(n_p + j, got, got, sibling).wait_recv()
            copy(2 * n_p + j, cv_ref, cvf_ref.at[2 * px + py], (px, py, c)).wait_recv()
        for cp in first + passed:
            cp.wait_send()
        own_cv.wait()

    return pl.pallas_call(
        body, name="gather_w_in", in_specs=[ANY, ANY], out_specs=[ANY, ANY],
        out_shape=[jax.ShapeDtypeStruct(w_in.shape, w_in.dtype), jax.ShapeDtypeStruct((N_CHIPS,) + conv_w.shape, conv_w.dtype)],
        input_output_aliases={0: 0},
        scratch_shapes=[pltpu.SemaphoreType.DMA((3 * n_p,)), pltpu.SemaphoreType.DMA((3 * n_p,)), pltpu.SemaphoreType.DMA],
    )(w_in, conv_w)


HBM = pl.BlockSpec(memory_space=pltpu.HBM)
SEM = pl.BlockSpec(memory_space=pltpu.SEMAPHORE)
EFFECT = pltpu.SideEffectType.DATAFLOW_SIDE_EFFECTING


class _Split:
    def __init__(self, name, arrays, plan):
        n, n_copies = len(arrays), plan.count
        self.name, self.plan, self.n = name, plan, n

        def body(*refs):
            send_sems, recv_sems, token = refs[n], refs[n + 1], refs[-1]
            for k, (src, dst, to) in enumerate(plan(refs[:n])):
                pltpu.make_async_remote_copy(src_ref=src, dst_ref=dst, send_sem=send_sems.at[k], recv_sem=recv_sems.at[k],
                                             device_id=to, device_id_type=MESH).start()
            token[...] = jnp.zeros_like(token)

        outs = pl.pallas_call(
            body, name=name + "_start",
            out_shape=(pltpu.SemaphoreType.DMA((n_copies,)), pltpu.SemaphoreType.DMA((n_copies,)),
                       *[pltpu.HBM(a.shape, a.dtype) for a in arrays], jax.ShapeDtypeStruct((SUBLANES, LANES), F32)),
            in_specs=(HBM,) * n, out_specs=(SEM, SEM) + (HBM,) * n + (pl.BlockSpec(memory_space=pltpu.VMEM),),
            input_output_aliases={i: 2 + i for i in range(n)},
            compiler_params=pltpu.CompilerParams(has_side_effects=EFFECT),
        )(*[pltpu.with_memory_space_constraint(a, pltpu.HBM) for a in arrays])
        self.sems, self.arrays, self.token = outs[:2], outs[2:2 + n], outs[-1]

    def wait(self, after):
        n, plan = self.n, self.plan

        def body(*refs):
            send_sems, recv_sems = refs[n], refs[n + 1]
            for k, (src, dst, to) in enumerate(plan(refs[:n])):
                cp = pltpu.make_async_remote_copy(src_ref=src, dst_ref=dst, send_sem=send_sems.at[k],
                                                  recv_sem=recv_sems.at[k], device_id=to, device_id_type=MESH)
                cp.wait_send()
                cp.wait_recv()

        return pl.pallas_call(
            body, name=self.name + "_wait", out_shape=tuple(pltpu.HBM(a.shape, a.dtype) for a in self.arrays),
            in_specs=(HBM,) * n + (SEM, SEM, ANY), out_specs=(HBM,) * n, input_output_aliases={i: i for i in range(n)},
            compiler_params=pltpu.CompilerParams(has_side_effects=EFFECT),
        )(*self.arrays, *self.sems, after)


COLS_SHARDED = (True, False, True, False)
HALF_SHAPES = [(D_MODEL // 2, IN_COLS), (D_MODEL, D_MODEL // 2), (D_MODEL // 2, D_FF), (D_FF, D_MODEL // 2)]
PIECE_SHAPES = [(D_MODEL // 2, IN_COLS // N_CHIPS), (D_MODEL // N_CHIPS, D_MODEL // 2),
                (D_MODEL // 2, D_FF // N_CHIPS), (D_FF // N_CHIPS, D_MODEL // 2)]


def _shard_view(kind, ref, chip):
    if COLS_SHARDED[kind]:
        n = ref.shape[1] // N_CHIPS
        return ref.at[:, pl.ds(chip * n, n)]
    n = ref.shape[0] // N_CHIPS
    return ref.at[pl.ds(chip * n, n), :]


def _half_view(kind, ref, h):
    if COLS_SHARDED[kind]:
        n = ref.shape[0] // 2
        return ref.at[pl.ds(h * n, n), :]
    n = ref.shape[1] // 2
    return ref.at[:, pl.ds(h * n, n)]


def _plan(count):
    def mark(fn):
        fn.count = count
        return fn
    return mark


def _shard_half_view(kind, ref, chip, h):
    if COLS_SHARDED[kind]:
        m, n = ref.shape[0] // 2, ref.shape[1] // N_CHIPS
        return ref.at[pl.ds(h * m, m), pl.ds(chip * n, n)]
    m = ref.shape[0] // N_CHIPS // 2
    return ref.at[pl.ds((2 * chip + h) * m, m), :]


def _gather_over_ici(kinds, weights):
    @_plan(3 * len(kinds))
    def plan(refs):
        x, y, c, me = _place()
        mine = [_shard_half_view(kind, ref, me, c) for kind, ref in zip(kinds, refs)]
        return [(v, v, (px, py, c)) for v in mine for px, py in _other_chips(x, y)]

    return _Split("gather_ici_" + "".join(map(str, kinds)), tuple(weights), plan)


def _gather_over_d2d(kinds, weights):
    @_plan(3 * len(kinds))
    def plan(refs):
        x, y, c, _ = _place()
        got = [_shard_half_view(kind, ref, 2 * px + py, c) for kind, ref in zip(kinds, refs)
               for px, py in _other_chips(x, y)]
        return [(v, v, (x, y, 1 - c)) for v in got]

    return _Split("gather_d2d_" + "".join(map(str, kinds)), tuple(weights), plan)


def _swap_halves(kinds, grads):
    @_plan(len(kinds))
    def plan(refs):
        x, y, c, _ = _place()
        return [(_half_view(kind, g, 1 - c), land, (x, y, 1 - c))
                for kind, g, land in zip(kinds, refs[:len(kinds)], refs[len(kinds):])]

    lands = [lax.empty(HALF_SHAPES[kind], F32) for kind in kinds]
    return _Split("swap_halves_" + "".join(map(str, kinds)), (*grads, *lands), plan)


def _add_half(name, g, recv, core, rows_split):
    shape = recv.shape
    tr = min(shape[0], 128 if rows_split else 256)
    nb = shape[0] // tr

    def body(c_ref, g_ref, r_ref, o_ref):
        o_ref[...] = (g_ref[...] + r_ref[...]).astype(o_ref.dtype)

    g_map = (lambda i, c_ref: (c_ref[0] * nb + i, 0)) if rows_split else (lambda i, c_ref: (i, c_ref[0]))
    blk = pl.BlockSpec((tr, shape[1]), lambda i, c_ref: (i, 0))
    return pl.pallas_call(
        body, name=name,
        grid_spec=pltpu.PrefetchScalarGridSpec(
            num_scalar_prefetch=1, grid=(nb,),
            in_specs=[pl.BlockSpec((tr, shape[1]), g_map), blk], out_specs=blk),
        out_shape=jax.ShapeDtypeStruct(shape, BF16),
        compiler_params=_params("parallel"),
    )(core, g, recv)


def _exchange_pieces(kinds, halves):
    n_p = N_CHIPS - 1

    @_plan(n_p * len(kinds))
    def plan(refs):
        x, y, c, _ = _place()
        return [(_shard_view(kind, half, 2 * px + py), land.at[j], (px, py, c))
                for j, (px, py) in enumerate(_other_chips(x, y))
                for kind, half, land in zip(kinds, refs[:len(kinds)], refs[len(kinds):])]

    lands = [lax.empty((n_p,) + PIECE_SHAPES[kind], BF16) for kind in kinds]
    return _Split("exchange_pieces_" + "".join(map(str, kinds)), (*halves, *lands), plan)


def _sum_pieces(name, half, slots, place, rows_split):
    n_p, rows, cols = slots.shape
    tr = min(rows, 256)
    nb = rows // tr
    if rows_split:
        own_map = lambda i, s: (i, s[0])
        out_map = lambda i, s: (s[1] * nb + i, 0)
        shard = (2 * rows, cols)
    else:
        own_map = lambda i, s: (s[0] * nb + i, 0)
        out_map = lambda i, s: (i, s[1])
        shard = (rows, 2 * cols)

    def body(s_ref, own_ref, slot_ref, o_ref):
        total = own_ref[...].astype(F32)
        for j in range(n_p):
            total = total + slot_ref[j].astype(F32)
        o_ref[...] = total

    return pl.pallas_call(
        body, name=name,
        grid_spec=pltpu.PrefetchScalarGridSpec(
            num_scalar_prefetch=1, grid=(nb,),
            in_specs=[pl.BlockSpec((tr, cols), own_map), pl.BlockSpec((n_p, tr, cols), lambda i, s: (0, i, 0))],
            out_specs=pl.BlockSpec((tr, cols), out_map)),
        out_shape=jax.ShapeDtypeStruct(shard, F32),
        compiler_params=_params("parallel"),
    )(place, half, slots)


def _join_halves(kinds, shards):
    @_plan(len(kinds))
    def plan(refs):
        x, y, c, _ = _place()
        return [(_half_view(kind, g, c), _half_view(kind, g, c), (x, y, 1 - c)) for kind, g in zip(kinds, refs)]

    return _Split("join_halves_" + "".join(map(str, kinds)), tuple(shards), plan)


def _sum_small(pack, after):
    n_dev = 8

    def body(p_ref, after_ref, o_ref, slots, send_sems, recv_sems):
        x, y, c, _ = _place()
        me = 4 * x + 2 * y + c
        slots[me] = p_ref[...]
        sends = []
        for m in range(1, n_dev):
            peer = ((1 - x) if m & 4 else x, (1 - y) if m & 2 else y, (1 - c) if m & 1 else c)
            sends.append(pltpu.make_async_remote_copy(
                src_ref=p_ref, dst_ref=slots.at[me], send_sem=send_sems.at[m - 1], recv_sem=recv_sems.at[m - 1],
                device_id=peer, device_id_type=MESH))
        for cp in sends:
            cp.start()
        for m in range(1, n_dev):
            peer = ((1 - x) if m & 4 else x, (1 - y) if m & 2 else y, (1 - c) if m & 1 else c)
            pltpu.make_async_remote_copy(
                src_ref=p_ref, dst_ref=slots.at[4 * peer[0] + 2 * peer[1] + peer[2]], send_sem=send_sems.at[m - 1],
                recv_sem=recv_sems.at[m - 1], device_id=peer, device_id_type=MESH).wait_recv()
        for cp in sends:
            cp.wait_send()
        total = slots[0]
        for d in range(1, n_dev):
            total = total + slots[d]
        o_ref[...] = total

    vm = pl.BlockSpec(memory_space=pltpu.VMEM)
    return pl.pallas_call(
        body, name="sum_small", in_specs=[vm, ANY], out_specs=vm,
        out_shape=jax.ShapeDtypeStruct(pack.shape, F32),
        scratch_shapes=[pltpu.VMEM((n_dev,) + pack.shape, F32), pltpu.SemaphoreType.DMA((n_dev - 1,)),
                        pltpu.SemaphoreType.DMA((n_dev - 1,))],
    )(pack, after)


def _adamw(name, w, g, m, v, after=None):
    rows, cols = w.shape
    tr = min(rows, 256)
    extra = [] if after is None else [after]

    def body(w_ref, g_ref, m_ref, v_ref, *rest):
        d_ref, nm_ref, nv_ref = rest[-3:]
        gv = g_ref[...]
        nm = ADAM_B1 * m_ref[...] + (1.0 - ADAM_B1) * gv
        nv = ADAM_B2 * v_ref[...] + (1.0 - ADAM_B2) * jnp.square(gv)
        m_hat = nm * (1.0 / (1.0 - ADAM_B1 ** ADAM_STEP))
        v_hat = nv * (1.0 / (1.0 - ADAM_B2 ** ADAM_STEP))
        d_ref[...] = -ADAM_LR * (m_hat / (jnp.sqrt(v_hat) + ADAM_EPS) + ADAM_WD * w_ref[...])
        nm_ref[...] = nm
        nv_ref[...] = nv

    blk = pl.BlockSpec((tr, cols), lambda i: (i, 0))
    return pl.pallas_call(
        body, name=name, grid=(rows // tr,), in_specs=[blk] * 4 + [ANY] * len(extra), out_specs=[blk] * 3,
        out_shape=[jax.ShapeDtypeStruct(w.shape, F32)] * 3,
        compiler_params=_params("parallel"),
    )(w, g, m, v, *extra)


def kernel(x, w_in, lb_logits, gate_norm_w, conv_w, w_out, ln1_g, ln1_b, w_ff1, w_ff2, ln2_g, ln2_b, loss_target, m_w_in, m_lb_logits, m_gate_norm_w, m_conv_w, m_w_out, m_ln1_g, m_ln1_b, m_w_ff1, m_w_ff2, m_ln2_g, m_ln2_b, v_w_in, v_lb_logits, v_gate_norm_w, v_conv_w, v_w_out, v_ln1_g, v_ln1_b, v_w_ff1, v_w_ff2, v_ln2_g, v_ln2_b):
    xs, tgt = x[0], loss_target[0]
    chip = 2 * lax.axis_index("x") + lax.axis_index("y")
    core = lax.axis_index("c").astype(jnp.int32).reshape(1)
    chip1 = chip.astype(jnp.int32).reshape(1)
    place = jnp.concatenate([chip1, core])

    wb_in, cv4 = _gather_w_in(_place_shard("place_w_in", w_in[0], chip1, True), conv_w[0])
    conv_full = cv4.transpose(1, 0, 2).reshape(3, CONV_WIDTH)
    ici_out = _gather_over_ici((1,), (_place_shard("place_w_out", w_out[0], chip1, False, after=wb_in),))
    ici_ff = _gather_over_ici((2, 3), (_place_shard("place_w_ff1", w_ff1[0], chip1, True, after=ici_out.token),
                                       _place_shard("place_w_ff2", w_ff2[0], chip1, False, after=ici_out.token)))

    proj, xb = _in_proj(xs, wb_in, ici_ff.token)
    o, states = _hgrn_fwd(proj, lb_logits)
    d2d_out = _gather_over_d2d((1,), ici_out.wait(o))
    cat = _gate_fwd(proj, o, gate_norm_w, conv_full, d2d_out.token)
    wb_out, = d2d_out.wait(cat)
    d2d_ff = _gather_over_d2d((2, 3), ici_ff.wait(cat))
    xhat1, h1b, rstd1 = _out_ln1(cat, wb_out, xs, ln1_g, ln1_b, d2d_ff.token)
    wb_ff1, wb_ff2 = d2d_ff.wait(xhat1)
    r, dpre2, dpre2b, g_ln2_g, g_ln2_b, loss8 = _mlp_fwd(xhat1, ln1_g, ln1_b, wb_ff1, wb_ff2, ln2_g, ln2_b, tgt)

    names = ("w_in", "w_out", "w_ff1", "w_ff2")

    def add_halves(kinds, grads, lands):
        return [_add_half("add_half_" + names[k], g, ld, core, COLS_SHARDED[k]) for k, g, ld in zip(kinds, grads, lands)]

    def sum_pieces(kinds, halves, lands):
        return [_sum_pieces("sum_pieces_" + names[k], h, ld, place, COLS_SHARDED[k]) for k, h, ld in zip(kinds, halves, lands)]

    da, dpre1, dpre1b, g_ln1_g, g_ln1_b = _mlp_bwd(dpre2, r, wb_ff1, wb_ff2, xhat1, rstd1, ln1_g)
    dcat = _out_bwd(dpre1b, wb_out)
    early = (1, 2, 3)
    swap = _swap_halves(early, (_dw_out(cat, dpre1b), _dw_ff1(h1b, da), _dw_ff2(r, dpre2b)))
    do, dpg, g_gnw, g_conv = _gate_bwd(dcat, o, proj, gate_norm_w, conv_full, swap.token)
    swapped = swap.wait(do)
    exch = _exchange_pieces(early, add_halves(early, swapped[:3], swapped[3:]))
    dph, g_lbl = _hgrn_bwd(proj, do, states, lb_logits, exch.token)
    grad_x = _in_bwd(dph, dpg, wb_in, dpre1)
    g_in_local = _dw_in(xb, dph, dpg, grad_x)

    late = (0,)
    swap = _swap_halves(late, (g_in_local,))
    exchanged = exch.wait(swap.token)
    join = _join_halves(early, sum_pieces(early, exchanged[:3], exchanged[3:]))
    pack = jnp.concatenate([
        g_ln1_g, g_ln1_b, g_ln2_g, g_ln2_b,
        jnp.concatenate([g_lbl[0:1], g_lbl[1:2]], axis=1),
        jnp.concatenate([g_gnw, g_conv[0:1]], axis=1),
        jnp.concatenate([g_conv[1:2], g_conv[2:3]], axis=1),
        jnp.concatenate([loss8[0:1], jnp.zeros((1, D_MODEL - LANES), F32)], axis=1)], axis=0)
    tot = _sum_small(pack, join.token)
    g_w_out, g_w_ff1, g_w_ff2 = join.wait(tot)
    loss = tot[7, 0]
    half = D_MODEL // 2
    g_lb_logits = jnp.concatenate([tot[4:5, :half], tot[4:5, half:]], axis=0)
    g_gate_norm_w = tot[5:6, :half]
    g_conv_full = jnp.concatenate([tot[5:6, half:], tot[6:7, :half], tot[6:7, half:]], axis=0)
    g_conv_w = lax.dynamic_slice(g_conv_full, (0, chip * LANES), (3, LANES))
    swapped = swap.wait(tot)
    exch = _exchange_pieces(late, add_halves(late, swapped[:1], swapped[1:]))
    d_ff1, nm_ff1, nv_ff1 = _adamw("adamw_w_ff1", w_ff1[0], g_w_ff1, m_w_ff1[0], v_w_ff1[0], exch.token)
    d_ff2, nm_ff2, nv_ff2 = _adamw("adamw_w_ff2", w_ff2[0], g_w_ff2, m_w_ff2[0], v_w_ff2[0], d_ff1)
    d_out, nm_out, nv_out = _adamw("adamw_w_out", w_out[0], g_w_out, m_w_out[0], v_w_out[0], d_ff2)

    def small_pack(lbl, gnw, cv, l1g, l1b, l2g, l2b):
        pad = jnp.zeros((1, D_MODEL - 3 * LANES), F32)
        return jnp.concatenate([
            l1g, l1b, l2g, l2b, jnp.concatenate([lbl[0:1], lbl[1:2]], axis=1),
            jnp.concatenate([gnw, jnp.zeros((1, half), F32)], axis=1),
            jnp.concatenate([cv[0:1], cv[1:2], cv[2:3], pad], axis=1), jnp.zeros((1, D_MODEL), F32)], axis=0)

    w_s = small_pack(lb_logits, gate_norm_w, conv_w[0], ln1_g, ln1_b, ln2_g, ln2_b)
    g_s = small_pack(g_lb_logits, g_gate_norm_w, g_conv_w, tot[0:1], tot[1:2], tot[2:3], tot[3:4])
    m_s = small_pack(m_lb_logits, m_gate_norm_w, m_conv_w[0], m_ln1_g, m_ln1_b, m_ln2_g, m_ln2_b)
    v_s = small_pack(v_lb_logits, v_gate_norm_w, v_conv_w[0], v_ln1_g, v_ln1_b, v_ln2_g, v_ln2_b)
    d_s, nm_s, nv_s = _adamw("adamw_small", w_s, g_s, m_s, v_s, d_out)
    exchanged = exch.wait(d_s)
    join = _join_halves(late, sum_pieces(late, exchanged[:1], exchanged[1:]))
    g_w_in, = join.wait(join.token)
    d_in, nm_in, nv_in = _adamw("adamw_w_in", w_in[0], g_w_in, m_w_in[0], v_w_in[0])

    def unpack(p):
        lbl = jnp.concatenate([p[4:5, :half], p[4:5, half:]], axis=0)
        cv = jnp.concatenate([p[6:7, 0:LANES], p[6:7, LANES:2 * LANES], p[6:7, 2 * LANES:3 * LANES]], axis=0)
        return dict(lb_logits=lbl, gate_norm_w=p[5:6, :half], conv_w=cv[None], ln1_g=p[0:1], ln1_b=p[1:2],
                    ln2_g=p[2:3], ln2_b=p[3:4])

    order = ("w_in", "lb_logits", "gate_norm_w", "conv_w", "w_out", "ln1_g", "ln1_b", "w_ff1", "w_ff2", "ln2_g", "ln2_b")
    grad = dict(unpack(g_s), w_in=g_w_in[None], w_out=g_w_out[None], w_ff1=g_w_ff1[None], w_ff2=g_w_ff2[None])
    delta = dict(unpack(d_s), w_in=d_in[None], w_out=d_out[None], w_ff1=d_ff1[None], w_ff2=d_ff2[None])
    new_m = dict(unpack(nm_s), w_in=nm_in[None], w_out=nm_out[None], w_ff1=nm_ff1[None], w_ff2=nm_ff2[None])
    new_v = dict(unpack(nv_s), w_in=nv_in[None], w_out=nv_out[None], w_ff1=nv_ff1[None], w_ff2=nv_ff2[None])
    return (loss, grad_x[None], *[grad[n] for n in order], *[delta[n] for n in order],
            *[new_m[n] for n in order], *[new_v[n] for n in order])
```

```python
import jax
import jax.numpy as jnp
from jax import lax
from jax.experimental import pallas as pl
from jax.experimental.pallas import tpu as pltpu

F32 = jnp.float32
BF16 = jnp.bfloat16
MXU_DTYPE = jnp.bfloat16

D_MODEL = 1024
HGRN_WIDTH = 512
HEAD_DIM = 128
N_HEADS = 4
CONV_WIDTH = 512
CHUNK = 64
D_FF = 4096
IN_COLS = 3584
GROUP = 512
N_GROUPS = IN_COLS // GROUP
ALPHA = 2.0 ** 0.25
EPS = 1e-5
N_CHIPS = 4
ADAM_LR, ADAM_B1, ADAM_B2, ADAM_EPS, ADAM_WD, ADAM_STEP = 0.001, 0.9, 0.999, 1e-08, 0.01, 10

LANES = 128
SUBLANES = 8
VMEM_LIMIT = 56 * 1024 * 1024
FF_BLOCK = 1024
N_FF = D_FF // FF_BLOCK

NN = (((1,), (0,)), ((), ()))
NT = (((1,), (1,)), ((), ()))
TN = (((0,), (0,)), ((), ()))
MESH = pl.DeviceIdType.MESH
ANY = pl.BlockSpec(memory_space=pl.ANY)


def _dot(a, b, dims):
    return lax.dot_general(a.astype(MXU_DTYPE), b.astype(MXU_DTYPE), dims, preferred_element_type=F32)


def _dot_exact(ones, v):
    ones = ones.astype(jnp.bfloat16)
    hi = v.astype(jnp.bfloat16)
    rest = v - hi.astype(F32)
    mid = rest.astype(jnp.bfloat16)
    low = (rest - mid.astype(F32)).astype(jnp.bfloat16)
    return sum(lax.dot_general(ones, part, NN, preferred_element_type=F32) for part in (hi, mid, low))


def _params(*sem):
    return pltpu.CompilerParams(dimension_semantics=sem, vmem_limit_bytes=VMEM_LIMIT)


def _resident(shape):
    return pl.BlockSpec(shape, lambda *_: (0,) * len(shape), pipeline_mode=pl.Buffered(1))


def _sigmoid(v):
    return 1.0 / (1.0 + jnp.exp(-v))


def _lower_bound(lbl):
    m = jnp.max(lbl, axis=0, keepdims=True)
    e = jnp.exp(lbl - m)
    s = e / jnp.sum(e, axis=0, keepdims=True)
    return s[0:1, :], s[1:2, :]


def _heads(v):
    return [v[:, h * HEAD_DIM:(h + 1) * HEAD_DIM] for h in range(N_HEADS)]


def _per_head(fn, *arrays):
    return jnp.concatenate([fn(*parts) for parts in zip(*map(_heads, arrays))], axis=1)


def _in_proj(x, w_in, after):
    t = x.shape[0]
    tm = min(t, 512)

    def body(x_ref, w_ref, after_ref, o_ref, xb_ref):
        xb = x_ref[...].astype(xb_ref.dtype)
        xb_ref[...] = xb
        for g in range(N_GROUPS):
            o_ref[g] = _dot(xb, w_ref[:, g * GROUP:(g + 1) * GROUP], NN)

    return pl.pallas_call(
        body, name="in_proj", grid=(t // tm,),
        in_specs=[pl.BlockSpec((tm, D_MODEL), lambda i: (i, 0)), _resident((D_MODEL, IN_COLS)), ANY],
        out_specs=[pl.BlockSpec((N_GROUPS, tm, GROUP), lambda i: (0, i, 0)), pl.BlockSpec((tm, D_MODEL), lambda i: (i, 0))],
        out_shape=[jax.ShapeDtypeStruct((N_GROUPS, t, GROUP), F32), jax.ShapeDtypeStruct((t, D_MODEL), BF16)],
        compiler_params=_params("parallel"),
    )(x, w_in, after)


def _gates(fp, lb):
    sig = _sigmoid(fp)
    f = lb + (1.0 - lb) * sig
    return sig, f, jnp.log(f), 1.0 - f


def _chunk_masks():
    row = lax.broadcasted_iota(jnp.int32, (CHUNK, CHUNK), 0)
    col = lax.broadcasted_iota(jnp.int32, (CHUNK, CHUNK), 1)
    return row >= col, row <= col


def _hgrn_fwd(proj, lb_logits):
    t = proj.shape[1]
    tb = min(t, 512)
    ncb = tb // CHUNK

    def body(q_ref, f_ref, v_ref, lbl_ref, o_ref, st_ref, s_scr):
        @pl.when(pl.program_id(0) == 0)
        def _():
            s_scr[...] = jnp.zeros_like(s_scr)

        lb, _ = _lower_bound(lbl_ref[...])
        causal, _ = _chunk_masks()

        every = range(ncb)
        rows = [slice(c * CHUNK, (c + 1) * CHUNK) for c in every]
        q, v = [q_ref[r, :] for r in rows], [v_ref[r, :] for r in rows]
        gates = [_gates(f_ref[r, :], lb) for r in rows]
        k = [gt[3] for gt in gates]
        b = [_dot_exact(causal, gt[2]) for gt in gates]
        mid, last = [x[CHUNK // 2:CHUNK // 2 + 1, :] for x in b], [x[CHUNK - 1:CHUNK, :] for x in b]
        qt = [q[c] * jnp.exp(b[c] - mid[c]) for c in every]
        kt = [k[c] * jnp.exp(mid[c] - b[c]) for c in every]
        qi = [q[c] * jnp.exp(b[c]) for c in every]
        ks = [k[c] * jnp.exp(last[c] - b[c]) for c in every]
        dec = [jnp.exp(x) for x in last]
        scores = [[jnp.where(causal, _dot(a, b_, NT), 0.0) for a, b_ in zip(_heads(qt[c]), _heads(kt[c]))] for c in every]
        intra = [[_dot(s, v_h, NN) for s, v_h in zip(scores[c], _heads(v[c]))] for c in every]
        update = [_per_head(lambda v_h, ks_h: _dot(v_h, ks_h, TN), v[c], ks[c]) for c in every]

        st = s_scr[...]
        states = []
        for c in every:
            states.append(st)
            st_ref[c] = st
            st = dec[c] * st + update[c]
        s_scr[...] = st

        o_ref[...] = jnp.concatenate(
            [jnp.concatenate([i_h + _dot(qi_h, st_h, NT) for i_h, qi_h, st_h in
                              zip(intra[c], _heads(qi[c]), _heads(states[c]))], axis=1) for c in every], axis=0)

    grp = lambda g: pl.BlockSpec((None, tb, GROUP), lambda i: (g, i, 0))
    return pl.pallas_call(
        body, name="hgrn_fwd", grid=(t // tb,),
        in_specs=[grp(0), grp(1), grp(2), pl.BlockSpec((2, HGRN_WIDTH), lambda i: (0, 0))],
        out_specs=[pl.BlockSpec((tb, HGRN_WIDTH), lambda i: (i, 0)),
                   pl.BlockSpec((ncb, HEAD_DIM, HGRN_WIDTH), lambda i: (i, 0, 0))],
        out_shape=[jax.ShapeDtypeStruct((t, HGRN_WIDTH), F32),
                   jax.ShapeDtypeStruct((t // CHUNK, HEAD_DIM, HGRN_WIDTH), F32)],
        scratch_shapes=[pltpu.VMEM((HEAD_DIM, HGRN_WIDTH), F32)],
        compiler_params=_params("arbitrary"),
    )(proj, proj, proj, lb_logits)


def _conv_taps(z, halo, zbuf, tb):
    zbuf[0:SUBLANES, :] = halo
    zbuf[SUBLANES:SUBLANES + tb, :] = z
    return zbuf[SUBLANES - 1:SUBLANES - 1 + tb, :], zbuf[SUBLANES - 2:SUBLANES - 2 + tb, :]


def _gate_fwd(proj, o, gate_norm_w, conv_w, after):
    t = proj.shape[1]
    tb = min(t, 512)
    hb = tb // SUBLANES

    def body(o_ref, og_ref, gnw_ref, b_ref, c_ref, u_ref, ch_ref, uh_ref, cw_ref, after_ref, cat_ref, zbuf):
        i = pl.program_id(0)
        og = og_ref[...]
        on = _per_head(lambda o_h: o_h * lax.rsqrt(jnp.mean(o_h * o_h, axis=-1, keepdims=True) + EPS), o_ref[...])
        cat_ref[0] = (on * gnw_ref[...] * (og * _sigmoid(og))).astype(cat_ref.dtype)
        z = c_ref[...] * u_ref[...]
        halo = jnp.where(i > 0, ch_ref[...] * uh_ref[...], 0.0)
        z1, z2 = _conv_taps(z, halo, zbuf, tb)
        cw = cw_ref[...]
        yc = cw[2:3, :] * z + cw[1:2, :] * z1 + cw[0:1, :] * z2
        cat_ref[1] = (b_ref[...] * yc).astype(cat_ref.dtype)

    grp = lambda g: pl.BlockSpec((None, tb, GROUP), lambda i: (g, i, 0))
    prev = lambda g: pl.BlockSpec((None, SUBLANES, GROUP), lambda i: (g, jnp.maximum(i * hb - 1, 0), 0))
    vec = lambda r: pl.BlockSpec((r, GROUP), lambda i: (0, 0))
    return pl.pallas_call(
        body, name="gate_fwd", grid=(t // tb,),
        in_specs=[pl.BlockSpec((tb, GROUP), lambda i: (i, 0)), grp(3), vec(1), grp(4), grp(5), grp(6), prev(5), prev(6),
                  vec(3), ANY],
        out_specs=pl.BlockSpec((2, tb, GROUP), lambda i: (0, i, 0)),
        out_shape=jax.ShapeDtypeStruct((2, t, HGRN_WIDTH), BF16),
        scratch_shapes=[pltpu.VMEM((tb + SUBLANES, GROUP), F32)],
        compiler_params=_params("parallel"),
    )(o, proj, gate_norm_w, proj, proj, proj, proj, proj, conv_w, after)


def _out_ln1(cat, w_out, x, g1, b1, after):
    t = x.shape[0]
    tm = min(t, 512)

    def body(cat_ref, w_ref, x_ref, g_ref, b_ref, after_ref, xhat_ref, h1_ref, rstd_ref):
        mix = _dot(cat_ref[0], w_ref[0:GROUP, :], NN) + _dot(cat_ref[1], w_ref[GROUP:2 * GROUP, :], NN)
        pre = ALPHA * x_ref[...] + mix
        xc = pre - jnp.mean(pre, axis=-1, keepdims=True)
        rstd = lax.rsqrt(jnp.mean(xc * xc, axis=-1, keepdims=True) + EPS)
        xhat = xc * rstd
        xhat_ref[...] = xhat
        h1_ref[...] = (xhat * g_ref[...] + b_ref[...]).astype(h1_ref.dtype)
        rstd_ref[...] = rstd

    row = pl.BlockSpec((tm, D_MODEL), lambda i: (i, 0))
    vec = pl.BlockSpec((1, D_MODEL), lambda i: (0, 0))
    return pl.pallas_call(
        body, name="out_ln1", grid=(t // tm,),
        in_specs=[pl.BlockSpec((2, tm, GROUP), lambda i: (0, i, 0)), _resident((D_MODEL, D_MODEL)), row, vec, vec, ANY],
        out_specs=[row, row, pl.BlockSpec((tm, 1), lambda i: (i, 0))],
        out_shape=[jax.ShapeDtypeStruct((t, D_MODEL), F32), jax.ShapeDtypeStruct((t, D_MODEL), BF16),
                   jax.ShapeDtypeStruct((t, 1), F32)],
        compiler_params=_params("parallel"),
    )(cat, w_out, x, g1, b1, after)


def _ln_bwd(dy, xhat, rstd, g):
    dxhat = dy * g
    m1 = jnp.mean(dxhat, axis=-1, keepdims=True)
    m2 = jnp.mean(dxhat * xhat, axis=-1, keepdims=True)
    return rstd * (dxhat - m1 - xhat * m2)


def _mlp_fwd(xhat1, g1, b1, w_ff1, w_ff2, g2, b2, target):
    t = xhat1.shape[0]
    tm = min(t, 256)

    def body(xh_ref, g1_ref, b1_ref, w1_ref, w2_ref, g2_ref, b2_ref, tg_ref,
             r_ref, dpre_ref, dpreb_ref, dg_ref, db_ref, loss_ref):
        @pl.when(pl.program_id(0) == 0)
        def _():
            dg_ref[...] = jnp.zeros_like(dg_ref)
            db_ref[...] = jnp.zeros_like(db_ref)
            loss_ref[...] = jnp.zeros_like(loss_ref)

        h1 = xh_ref[...] * g1_ref[...] + b1_ref[...]
        h1b = h1.astype(MXU_DTYPE)
        mlp = jnp.zeros((tm, D_MODEL), F32)
        for j in range(N_FF):
            cols = slice(j * FF_BLOCK, (j + 1) * FF_BLOCK)
            r = jnp.square(jnp.maximum(_dot(h1b, w1_ref[:, cols], NN), 0.0)).astype(r_ref.dtype)
            r_ref[:, cols] = r
            mlp = mlp + _dot(r, w2_ref[cols, :], NN)
        pre = ALPHA * h1 + mlp
        xc = pre - jnp.mean(pre, axis=-1, keepdims=True)
        rstd = lax.rsqrt(jnp.mean(xc * xc, axis=-1, keepdims=True) + EPS)
        xhat = xc * rstd
        err = xhat * g2_ref[...] + b2_ref[...] - tg_ref[...]
        loss_ref[...] += 0.5 * jnp.sum(jnp.mean(err * err, axis=-1, keepdims=True))
        dy = err * (1.0 / D_MODEL)
        dg_ref[...] += jnp.sum(dy * xhat, axis=0, keepdims=True)
        db_ref[...] += jnp.sum(dy, axis=0, keepdims=True)
        dpre = _ln_bwd(dy, xhat, rstd, g2_ref[...])
        dpre_ref[...] = dpre
        dpreb_ref[...] = dpre.astype(dpreb_ref.dtype)

    row = pl.BlockSpec((tm, D_MODEL), lambda i: (i, 0))
    vec = pl.BlockSpec((1, D_MODEL), lambda i: (0, 0))
    return pl.pallas_call(
        body, name="mlp_fwd", grid=(t // tm,),
        in_specs=[row, vec, vec, _resident((D_MODEL, D_FF)), _resident((D_FF, D_MODEL)), vec, vec, row],
        out_specs=[pl.BlockSpec((tm, D_FF), lambda i: (i, 0)), row, row, vec, vec,
                   pl.BlockSpec((SUBLANES, LANES), lambda i: (0, 0))],
        out_shape=[jax.ShapeDtypeStruct((t, D_FF), BF16), jax.ShapeDtypeStruct((t, D_MODEL), F32),
                   jax.ShapeDtypeStruct((t, D_MODEL), BF16), jax.ShapeDtypeStruct((1, D_MODEL), F32),
                   jax.ShapeDtypeStruct((1, D_MODEL), F32), jax.ShapeDtypeStruct((SUBLANES, LANES), F32)],
        compiler_params=_params("arbitrary"),
    )(xhat1, g1, b1, w_ff1, w_ff2, g2, b2, target)


def _mlp_bwd(dpre2, r, w_ff1, w_ff2, xhat1, rstd1, g1):
    t = r.shape[0]
    tm = min(t, 256)

    def body(dp2_ref, r_ref, w1_ref, w2_ref, xh_ref, rs_ref, g_ref, da_ref, dpre_ref, dpreb_ref, dg_ref, db_ref):
        @pl.when(pl.program_id(0) == 0)
        def _():
            dg_ref[...] = jnp.zeros_like(dg_ref)
            db_ref[...] = jnp.zeros_like(db_ref)

        dp2 = dp2_ref[...]
        dp2b = dp2.astype(MXU_DTYPE)
        back = jnp.zeros((tm, D_MODEL), F32)
        for j in range(N_FF):
            cols = slice(j * FF_BLOCK, (j + 1) * FF_BLOCK)
            dr = _dot(dp2b, w2_ref[cols, :], NT)
            da = (dr * (2.0 * jnp.sqrt(r_ref[:, cols].astype(F32)))).astype(da_ref.dtype)
            da_ref[:, cols] = da
            back = back + _dot(da, w1_ref[:, cols], NT)
        dh1 = ALPHA * dp2 + back
        xhat = xh_ref[...]
        dg_ref[...] += jnp.sum(dh1 * xhat, axis=0, keepdims=True)
        db_ref[...] += jnp.sum(dh1, axis=0, keepdims=True)
        dpre = _ln_bwd(dh1, xhat, rs_ref[...], g_ref[...])
        dpre_ref[...] = dpre
        dpreb_ref[...] = dpre.astype(dpreb_ref.dtype)

    row = pl.BlockSpec((tm, D_MODEL), lambda i: (i, 0))
    wide = pl.BlockSpec((tm, D_FF), lambda i: (i, 0))
    vec = pl.BlockSpec((1, D_MODEL), lambda i: (0, 0))
    return pl.pallas_call(
        body, name="mlp_bwd", grid=(t // tm,),
        in_specs=[row, wide, _resident((D_MODEL, D_FF)), _resident((D_FF, D_MODEL)), row,
                  pl.BlockSpec((tm, 1), lambda i: (i, 0)), vec],
        out_specs=[wide, row, row, vec, vec],
        out_shape=[jax.ShapeDtypeStruct((t, D_FF), BF16), jax.ShapeDtypeStruct((t, D_MODEL), F32),
                   jax.ShapeDtypeStruct((t, D_MODEL), BF16), jax.ShapeDtypeStruct((1, D_MODEL), F32),
                   jax.ShapeDtypeStruct((1, D_MODEL), F32)],
        compiler_params=_params("arbitrary"),
    )(dpre2, r, w_ff1, w_ff2, xhat1, rstd1, g1)


def _out_bwd(dpre1b, w_out):
    t = dpre1b.shape[0]
    tm = min(t, 512)

    def body(d_ref, w_ref, o_ref):
        o_ref[...] = _dot(d_ref[...], w_ref[...], NT)

    return pl.pallas_call(
        body, name="out_bwd", grid=(t // tm,),
        in_specs=[pl.BlockSpec((tm, D_MODEL), lambda i: (i, 0)), _resident((D_MODEL, D_MODEL))],
        out_specs=pl.BlockSpec((tm, D_MODEL), lambda i: (i, 0)),
        out_shape=jax.ShapeDtypeStruct((t, D_MODEL), F32),
        compiler_params=_params("parallel"),
    )(dpre1b, w_out)


def _gate_bwd(dcat, o, proj, gate_norm_w, conv_w, after):
    t = proj.shape[1]
    tb = min(t, 512)
    hb = tb // SUBLANES
    nblk = t // tb

    def body(do2_ref, dy_ref, dyn_ref, o_ref, og_ref, gnw_ref, b_ref, bn_ref, c_ref, u_ref, ch_ref, uh_ref, cw_ref,
             after_ref, do_ref, dp_ref, dgnw_ref, dcw_ref, zbuf, dbuf):
        i = pl.program_id(0)

        @pl.when(i == 0)
        def _():
            dgnw_ref[...] = jnp.zeros_like(dgnw_ref)
            dcw_ref[...] = jnp.zeros_like(dcw_ref)

        ov, og, gnw, do2 = o_ref[...], og_ref[...], gnw_ref[...], do2_ref[...]
        rs = _per_head(lambda o_h: jnp.broadcast_to(lax.rsqrt(jnp.mean(o_h * o_h, axis=-1, keepdims=True) + EPS),
                                                    o_h.shape), ov)
        on = ov * rs
        sg = _sigmoid(og)
        sil = og * sg
        don = do2 * gnw * sil
        dgnw_ref[...] += jnp.sum(do2 * on * sil, axis=0, keepdims=True)
        dp_ref[0] = (do2 * on * gnw * (sg * (1.0 + og * (1.0 - sg)))).astype(dp_ref.dtype)
        do_ref[...] = rs * (don - on * _per_head(
            lambda p_h: jnp.broadcast_to(jnp.mean(p_h, axis=-1, keepdims=True), p_h.shape), don * on))

        bg, cg, u, dy = b_ref[...], c_ref[...], u_ref[...], dy_ref[...]
        z = cg * u
        halo = jnp.where(i > 0, ch_ref[...] * uh_ref[...], 0.0)
        z1, z2 = _conv_taps(z, halo, zbuf, tb)
        cw = cw_ref[...]
        yc = cw[2:3, :] * z + cw[1:2, :] * z1 + cw[0:1, :] * z2
        dyc = dy * bg
        dbuf[0:tb, :] = dyc
        dbuf[tb:tb + SUBLANES, :] = jnp.where(i < nblk - 1, dyn_ref[...] * bn_ref[...], 0.0)
        d1, d2 = dbuf[1:1 + tb, :], dbuf[2:2 + tb, :]
        dz = cw[2:3, :] * dyc + cw[1:2, :] * d1 + cw[0:1, :] * d2
        dp_ref[1] = (dy * yc).astype(dp_ref.dtype)
        dp_ref[2] = (dz * u).astype(dp_ref.dtype)
        dp_ref[3] = (dz * cg).astype(dp_ref.dtype)
        dcw_ref[0:1, :] += jnp.sum(dyc * z2, axis=0, keepdims=True)
        dcw_ref[1:2, :] += jnp.sum(dyc * z1, axis=0, keepdims=True)
        dcw_ref[2:3, :] += jnp.sum(dyc * z, axis=0, keepdims=True)

    half = lambda g: pl.BlockSpec((tb, GROUP), lambda i: (i, g))
    grp = lambda g: pl.BlockSpec((None, tb, GROUP), lambda i: (g, i, 0))
    prev = lambda g: pl.BlockSpec((None, SUBLANES, GROUP), lambda i: (g, jnp.maximum(i * hb - 1, 0), 0))
    nxt_row = lambda i: jnp.minimum((i + 1) * hb, t // SUBLANES - 1)
    nxt = lambda g: pl.BlockSpec((None, SUBLANES, GROUP), lambda i: (g, nxt_row(i), 0))
    vec = lambda r: pl.BlockSpec((r, GROUP), lambda i: (0, 0))
    return pl.pallas_call(
        body, name="gate_bwd", grid=(nblk,),
        in_specs=[half(0), half(1), pl.BlockSpec((SUBLANES, GROUP), lambda i: (nxt_row(i), 1)), half(0), grp(3), vec(1),
                  grp(4), nxt(4), grp(5), grp(6), prev(5), prev(6), vec(3), ANY],
        out_specs=[half(0), pl.BlockSpec((4, tb, GROUP), lambda i: (0, i, 0)), vec(1), vec(3)],
        out_shape=[jax.ShapeDtypeStruct((t, HGRN_WIDTH), F32), jax.ShapeDtypeStruct((4, t, HGRN_WIDTH), BF16),
                   jax.ShapeDtypeStruct((1, HGRN_WIDTH), F32), jax.ShapeDtypeStruct((3, CONV_WIDTH), F32)],
        scratch_shapes=[pltpu.VMEM((tb + SUBLANES, GROUP), F32), pltpu.VMEM((tb + SUBLANES, GROUP), F32)],
        compiler_params=_params("arbitrary"),
    )(dcat, dcat, dcat, o, proj, gate_norm_w, proj, proj, proj, proj, proj, proj, conv_w, after)


def _hgrn_bwd(proj, do, states, lb_logits, after):
    t = proj.shape[1]
    tb = min(t, 512)
    ncb = tb // CHUNK
    nblk = t // tb

    def body(q_ref, f_ref, v_ref, do_ref, st_ref, lbl_ref, after_ref, dp_ref, dlbl_ref, ds_scr, dlb_scr):
        i = pl.program_id(0)

        @pl.when(i == 0)
        def _():
            ds_scr[...] = jnp.zeros_like(ds_scr)
            dlb_scr[...] = jnp.zeros_like(dlb_scr)

        lb, s1 = _lower_bound(lbl_ref[...])
        causal, anti = _chunk_masks()
        every = range(ncb)
        rows = [slice(c * CHUNK, (c + 1) * CHUNK) for c in every]
        q, v, do = ([ref[r, :] for r in rows] for ref in (q_ref, v_ref, do_ref))
        st = [st_ref[c] for c in every]
        gates = [_gates(f_ref[r, :], lb) for r in rows]
        sig, f, k = ([gt[n] for gt in gates] for n in (0, 1, 3))
        b = [_dot_exact(causal, gt[2]) for gt in gates]
        mid, last = [x[CHUNK // 2:CHUNK // 2 + 1, :] for x in b], [x[CHUNK - 1:CHUNK, :] for x in b]
        e_q = [jnp.exp(b[c] - mid[c]) for c in every]
        e_k = [jnp.exp(mid[c] - b[c]) for c in every]
        e_i = [jnp.exp(x) for x in b]
        e_s = [jnp.exp(last[c] - b[c]) for c in every]
        dec = [jnp.exp(x) for x in last]
        qt, kt, qi, ks = ([a[c] * e[c] for c in every] for a, e in ((q, e_q), (k, e_k), (q, e_i), (k, e_s)))

        def masked(a, b_):
            return [[jnp.where(causal, _dot(a_h, b_h, NT), 0.0) for a_h, b_h in zip(_heads(a[c]), _heads(b_[c]))]
                    for c in every]

        def with_scores(s, other, dims):
            return [jnp.concatenate([_dot(s_h, o_h, dims) for s_h, o_h in zip(s[c], _heads(other[c]))], axis=1)
                    for c in every]

        def per_head(dims, a, b_):
            return [_per_head(lambda a_h, b_h: _dot(a_h, b_h, dims), a[c], b_[c]) for c in every]

        scores, dscores = masked(qt, kt), masked(do, v)
        dqt, dkt, dv_intra = with_scores(dscores, kt, NN), with_scores(dscores, qt, TN), with_scores(scores, do, TN)
        dqi, update = per_head(NN, do, st), per_head(TN, do, qi)

        dst = ds_scr[...]
        dsts = [None] * ncb
        for c in reversed(every):
            dsts[c] = dst
            dst = dec[c] * dst + update[c]
        ds_scr[...] = dst

        dv_state, dks = per_head(NT, ks, dsts), per_head(NN, v, dsts)
        ddec = [jnp.sum(dsts[c] * st[c], axis=0, keepdims=True) for c in every]
        dq = [dqt[c] * e_q[c] + dqi[c] * e_i[c] for c in every]
        dk = [dkt[c] * e_k[c] + dks[c] * e_s[c] for c in every]
        db = [q[c] * dq[c] - k[c] * dk[c] for c in every]
        db_last = [jnp.sum(dks[c] * ks[c], axis=0, keepdims=True) + ddec[c] * dec[c] for c in every]
        dg = [_dot_exact(anti, db[c]) + db_last[c] for c in every]
        df = [dg[c] / f[c] - dk[c] for c in every]
        dlb_scr[...] += sum(jnp.sum(df[c] * (1.0 - sig[c]), axis=0, keepdims=True) for c in every)
        dfp = [df[c] * (1.0 - lb) * sig[c] * (1.0 - sig[c]) for c in every]
        dv = [dv_intra[c] + dv_state[c] for c in every]
        for n, parts in enumerate((dq, dfp, dv)):
            dp_ref[n] = jnp.concatenate(parts, axis=0).astype(dp_ref.dtype)

        @pl.when(i == nblk - 1)
        def _():
            dlb = dlb_scr[...]
            dlbl_ref[0:1, :] = dlb * lb * (1.0 - lb)
            dlbl_ref[1:2, :] = -dlb * lb * s1

    grp = lambda g: pl.BlockSpec((None, tb, GROUP), lambda i: (g, nblk - 1 - i, 0))
    vec = pl.BlockSpec((2, HGRN_WIDTH), lambda i: (0, 0))
    return pl.pallas_call(
        body, name="hgrn_bwd", grid=(nblk,),
        in_specs=[grp(0), grp(1), grp(2), pl.BlockSpec((tb, HGRN_WIDTH), lambda i: (nblk - 1 - i, 0)),
                  pl.BlockSpec((ncb, HEAD_DIM, HGRN_WIDTH), lambda i: (nblk - 1 - i, 0, 0)), vec, ANY],
        out_specs=[pl.BlockSpec((3, tb, HGRN_WIDTH), lambda i: (0, nblk - 1 - i, 0)), vec],
        out_shape=[jax.ShapeDtypeStruct((3, t, HGRN_WIDTH), BF16), jax.ShapeDtypeStruct((2, HGRN_WIDTH), F32)],
        scratch_shapes=[pltpu.VMEM((HEAD_DIM, HGRN_WIDTH), F32), pltpu.VMEM((1, HGRN_WIDTH), F32)],
        compiler_params=_params("arbitrary"),
    )(proj, proj, proj, do, states, lb_logits, after)


def _in_bwd(dph, dpg, w_in, dpre1, after):
    t = dpre1.shape[0]
    tm = min(t, 512)

    def body(dh_ref, dg_ref, w_ref, dp_ref, after_ref, o_ref):
        acc = ALPHA * dp_ref[...]
        for g in range(N_GROUPS):
            part = dh_ref[g] if g < 3 else dg_ref[g - 3]
            acc = acc + _dot(part, w_ref[:, g * GROUP:(g + 1) * GROUP], NT)
        o_ref[...] = acc

    row = pl.BlockSpec((tm, D_MODEL), lambda i: (i, 0))
    return pl.pallas_call(
        body, name="in_bwd", grid=(t // tm,),
        in_specs=[pl.BlockSpec((3, tm, GROUP), lambda i: (0, i, 0)), pl.BlockSpec((4, tm, GROUP), lambda i: (0, i, 0)),
                  _resident((D_MODEL, IN_COLS)), row, ANY],
        out_specs=row,
        out_shape=jax.ShapeDtypeStruct((t, D_MODEL), F32),
        compiler_params=_params("parallel"),
    )(dph, dpg, w_in, dpre1, after)


def _grad_w(name, operands, widths, shape, step, after=None):
    t = operands[0].shape[-2]
    tt = min(t, 512)
    n_in, n_steps = len(operands), t // tt
    in_specs = [pl.BlockSpec((tt, w), lambda k: (k, 0)) if a.ndim == 2 else
                pl.BlockSpec((a.shape[0], tt, w), lambda k: (0, k, 0)) for a, w in zip(operands, widths)]
    extra = [] if after is None else [after]

    def body(*refs):
        o_ref, acc, sem = refs[-3:]
        k = pl.program_id(0)

        @pl.when(k == 0)
        def _():
            acc[...] = jnp.zeros_like(acc)

        step(acc, *refs[:n_in])

        @pl.when(k == n_steps - 1)
        def _():
            out = pltpu.make_async_copy(acc, o_ref, sem)
            out.start()
            out.wait()

    return pl.pallas_call(
        body, name=name, grid=(n_steps,), in_specs=in_specs + [ANY] * len(extra), out_specs=ANY,
        out_shape=jax.ShapeDtypeStruct(shape, F32),
        scratch_shapes=[pltpu.VMEM(shape, F32), pltpu.SemaphoreType.DMA],
        compiler_params=_params("arbitrary"),
    )(*operands, *extra)


def _dw_in(xb, dph, dpg, after):
    def step(acc, x_ref, dh_ref, dg_ref):
        xv = x_ref[...]
        for g in range(N_GROUPS):
            part = dh_ref[g] if g < 3 else dg_ref[g - 3]
            acc[:, g * GROUP:(g + 1) * GROUP] += _dot(xv, part, TN)

    return _grad_w("dw_in", (xb, dph, dpg), (D_MODEL, GROUP, GROUP), (D_MODEL, IN_COLS), step, after)


def _dw_out(cat, dpre1b):
    def step(acc, cat_ref, d_ref):
        dv = d_ref[...]
        for g in range(2):
            acc[g * GROUP:(g + 1) * GROUP, :] += _dot(cat_ref[g], dv, TN)

    return _grad_w("dw_out", (cat, dpre1b), (GROUP, D_MODEL), (D_MODEL, D_MODEL), step)


def _dw_ff1(h1b, da):
    def step(acc, h_ref, da_ref):
        hv = h_ref[...]
        for j in range(D_FF // FF_BLOCK):
            cols = slice(j * FF_BLOCK, (j + 1) * FF_BLOCK)
            acc[:, cols] += _dot(hv, da_ref[:, cols], TN)

    return _grad_w("dw_ff1", (h1b, da), (D_MODEL, D_FF), (D_MODEL, D_FF), step)


def _dw_ff2(r, dpre2b):
    def step(acc, r_ref, d_ref):
        dv = d_ref[...]
        for j in range(D_FF // FF_BLOCK):
            rows = slice(j * FF_BLOCK, (j + 1) * FF_BLOCK)
            acc[rows, :] += _dot(r_ref[:, rows], dv, TN)

    return _grad_w("dw_ff2", (r, dpre2b), (D_FF, D_MODEL), (D_FF, D_MODEL), step)


def _place():
    x, y, c = lax.axis_index("x"), lax.axis_index("y"), lax.axis_index("c")
    return x, y, c, 2 * x + y


def _other_chips(x, y):
    return [(1 - x, y), (x, 1 - y), (1 - x, 1 - y)]


def _place_shard(name, w, chip, cols_sharded, after=None):
    rows, cols = w.shape
    tr = min(rows, 256)
    nb = rows // tr
    full = (rows, cols * N_CHIPS) if cols_sharded else (rows * N_CHIPS, cols)
    out_map = (lambda i, s: (i, s[0])) if cols_sharded else (lambda i, s: (s[0] * nb + i, 0))

    def body(s_ref, w_ref, *rest):
        rest[-1][...] = w_ref[...].astype(rest[-1].dtype)

    extra = [] if after is None else [after]
    return pl.pallas_call(
        body, name=name,
        grid_spec=pltpu.PrefetchScalarGridSpec(
            num_scalar_prefetch=1, grid=(nb,),
            in_specs=[pl.BlockSpec((tr, cols), lambda i, s: (i, 0))] + [ANY] * len(extra),
            out_specs=pl.BlockSpec((tr, cols), out_map)),
        out_shape=jax.ShapeDtypeStruct(full, BF16),
        compiler_params=_params("parallel"),
    )(chip, w, *extra)


def _gather_w_in(w_in, conv_w):
    half, cs, n_p = D_MODEL // 2, IN_COLS // N_CHIPS, 3

    def body(w_alias, cv_ref, w_ref, cvf_ref, send_sems, recv_sems, local_sem):
        x, y, c, me = _place()
        sibling = (x, y, 1 - c)
        chips = _other_chips(x, y)
        blk = lambda chip, h: w_ref.at[pl.ds(h * half, half), pl.ds(chip * cs, cs)]

        def copy(k, src, dst, to):
            return pltpu.make_async_remote_copy(src_ref=src, dst_ref=dst, send_sem=send_sems.at[k],
                                                recv_sem=recv_sems.at[k], device_id=to, device_id_type=MESH)

        own_cv = pltpu.make_async_copy(cv_ref, cvf_ref.at[me], local_sem)
        own_cv.start()
        first = [copy(j, blk(me, c), blk(me, c), (px, py, c)) for j, (px, py) in enumerate(chips)]
        first += [copy(2 * n_p + j, cv_ref, cvf_ref.at[me], (px, py, c)) for j, (px, py) in enumerate(chips)]
        for cp in first:
            cp.start()
        passed = []
        for j, (px, py) in enumerate(chips):
            got = blk(2 * px + py, c)
            copy(j, got, got, (px, py, c)).wait_recv()
            passed.append(copy(n_p + j, got, got, sibling))
            passed[-1].start()
        for j, (px, py) in enumerate(chips):
            got = blk(2 * px + py, 1 - c)
            copy(n_p + j, got, got, sibling).wait_recv()
            copy(2 * n_p + j, cv_ref, cvf_ref.at[2 * px + py], (px, py, c)).wait_recv()
        for cp in first + passed:
            cp.wait_send()
        own_cv.wait()

    return pl.pallas_call(
        body, name="gather_w_in", in_specs=[ANY, ANY], out_specs=[ANY, ANY],
        out_shape=[jax.ShapeDtypeStruct(w_in.shape, w_in.dtype), jax.ShapeDtypeStruct((N_CHIPS,) + conv_w.shape, conv_w.dtype)],
        input_output_aliases={0: 0},
        scratch_shapes=[pltpu.SemaphoreType.DMA((3 * n_p,)), pltpu.SemaphoreType.DMA((3 * n_p,)), pltpu.SemaphoreType.DMA],
    )(w_in, conv_w)


HBM = pl.BlockSpec(memory_space=pltpu.HBM)
SEM = pl.BlockSpec(memory_space=pltpu.SEMAPHORE)
EFFECT = pltpu.SideEffectType.DATAFLOW_SIDE_EFFECTING


class _Split:
    def __init__(self, name, arrays, plan):
        n, n_copies = len(arrays), plan.count
        self.name, self.plan, self.n = name, plan, n

        def body(*refs):
            send_sems, recv_sems, token = refs[n], refs[n + 1], refs[-1]
            for k, (src, dst, to) in enumerate(plan(refs[:n])):
                pltpu.make_async_remote_copy(src_ref=src, dst_ref=dst, send_sem=send_sems.at[k], recv_sem=recv_sems.at[k],
                                             device_id=to, device_id_type=MESH).start()
            token[...] = jnp.zeros_like(token)

        outs = pl.pallas_call(
            body, name=name + "_start",
            out_shape=(pltpu.SemaphoreType.DMA((n_copies,)), pltpu.SemaphoreType.DMA((n_copies,)),
                       *[pltpu.HBM(a.shape, a.dtype) for a in arrays], jax.ShapeDtypeStruct((SUBLANES, LANES), F32)),
            in_specs=(HBM,) * n, out_specs=(SEM, SEM) + (HBM,) * n + (pl.BlockSpec(memory_space=pltpu.VMEM),),
            input_output_aliases={i: 2 + i for i in range(n)},
            compiler_params=pltpu.CompilerParams(has_side_effects=EFFECT),
        )(*[pltpu.with_memory_space_constraint(a, pltpu.HBM) for a in arrays])
        self.sems, self.arrays, self.token = outs[:2], outs[2:2 + n], outs[-1]

    def wait(self, after):
        n, plan = self.n, self.plan

        def body(*refs):
            send_sems, recv_sems = refs[n], refs[n + 1]
            for k, (src, dst, to) in enumerate(plan(refs[:n])):
                cp = pltpu.make_async_remote_copy(src_ref=src, dst_ref=dst, send_sem=send_sems.at[k],
                                                  recv_sem=recv_sems.at[k], device_id=to, device_id_type=MESH)
                cp.wait_send()
                cp.wait_recv()

        return pl.pallas_call(
            body, name=self.name + "_wait", out_shape=tuple(pltpu.HBM(a.shape, a.dtype) for a in self.arrays),
            in_specs=(HBM,) * n + (SEM, SEM, ANY), out_specs=(HBM,) * n, input_output_aliases={i: i for i in range(n)},
            compiler_params=pltpu.CompilerParams(has_side_effects=EFFECT),
        )(*self.arrays, *self.sems, after)


COLS_SHARDED = (True, False, True, False)
HALF_SHAPES = [(D_MODEL // 2, IN_COLS), (D_MODEL, D_MODEL // 2), (D_MODEL // 2, D_FF), (D_FF, D_MODEL // 2)]
PIECE_SHAPES = [(D_MODEL // 2, IN_COLS // N_CHIPS), (D_MODEL // N_CHIPS, D_MODEL // 2),
                (D_MODEL // 2, D_FF // N_CHIPS), (D_FF // N_CHIPS, D_MODEL // 2)]


def _shard_view(kind, ref, chip):
    if COLS_SHARDED[kind]:
        n = ref.shape[1] // N_CHIPS
        return ref.at[:, pl.ds(chip * n, n)]
    n = ref.shape[0] // N_CHIPS
    return ref.at[pl.ds(chip * n, n), :]


def _half_view(kind, ref, h):
    if COLS_SHARDED[kind]:
        n = ref.shape[0] // 2
        return ref.at[pl.ds(h * n, n), :]
    n = ref.shape[1] // 2
    return ref.at[:, pl.ds(h * n, n)]


def _plan(count):
    def mark(fn):
        fn.count = count
        return fn
    return mark


def _shard_half_view(kind, ref, chip, h):
    if COLS_SHARDED[kind]:
        m, n = ref.shape[0] // 2, ref.shape[1] // N_CHIPS
        return ref.at[pl.ds(h * m, m), pl.ds(chip * n, n)]
    m = ref.shape[0] // N_CHIPS // 2
    return ref.at[pl.ds((2 * chip + h) * m, m), :]


def _gather_over_ici(kinds, weights):
    @_plan(3 * len(kinds))
    def plan(refs):
        x, y, c, me = _place()
        mine = [_shard_half_view(kind, ref, me, c) for kind, ref in zip(kinds, refs)]
        return [(v, v, (px, py, c)) for v in mine for px, py in _other_chips(x, y)]

    return _Split("gather_ici_" + "".join(map(str, kinds)), tuple(weights), plan)


def _gather_over_d2d(kinds, weights):
    @_plan(3 * len(kinds))
    def plan(refs):
        x, y, c, _ = _place()
        got = [_shard_half_view(kind, ref, 2 * px + py, c) for kind, ref in zip(kinds, refs)
               for px, py in _other_chips(x, y)]
        return [(v, v, (x, y, 1 - c)) for v in got]

    return _Split("gather_d2d_" + "".join(map(str, kinds)), tuple(weights), plan)


def _swap_halves(kinds, grads):
    @_plan(len(kinds))
    def plan(refs):
        x, y, c, _ = _place()
        return [(_half_view(kind, g, 1 - c), land, (x, y, 1 - c))
                for kind, g, land in zip(kinds, refs[:len(kinds)], refs[len(kinds):])]

    lands = [lax.empty(HALF_SHAPES[kind], F32) for kind in kinds]
    return _Split("swap_halves_" + "".join(map(str, kinds)), (*grads, *lands), plan)


def _add_half(name, g, recv, core, rows_split):
    shape = recv.shape
    tr = min(shape[0], 128 if rows_split else 256)
    nb = shape[0] // tr

    def body(c_ref, g_ref, r_ref, o_ref):
        o_ref[...] = (g_ref[...] + r_ref[...]).astype(o_ref.dtype)

    g_map = (lambda i, c_ref: (c_ref[0] * nb + i, 0)) if rows_split else (lambda i, c_ref: (i, c_ref[0]))
    blk = pl.BlockSpec((tr, shape[1]), lambda i, c_ref: (i, 0))
    return pl.pallas_call(
        body, name=name,
        grid_spec=pltpu.PrefetchScalarGridSpec(
            num_scalar_prefetch=1, grid=(nb,),
            in_specs=[pl.BlockSpec((tr, shape[1]), g_map), blk], out_specs=blk),
        out_shape=jax.ShapeDtypeStruct(shape, BF16),
        compiler_params=_params("parallel"),
    )(core, g, recv)


def _exchange_pieces(kinds, halves):
    n_p = N_CHIPS - 1

    @_plan(n_p * len(kinds))
    def plan(refs):
        x, y, c, _ = _place()
        return [(_shard_view(kind, half, 2 * px + py), land.at[j], (px, py, c))
                for j, (px, py) in enumerate(_other_chips(x, y))
                for kind, half, land in zip(kinds, refs[:len(kinds)], refs[len(kinds):])]

    lands = [lax.empty((n_p,) + PIECE_SHAPES[kind], BF16) for kind in kinds]
    return _Split("exchange_pieces_" + "".join(map(str, kinds)), (*halves, *lands), plan)


def _sum_pieces(name, half, slots, place, rows_split):
    n_p, rows, cols = slots.shape
    tr = min(rows, 256)
    nb = rows // tr
    if rows_split:
        own_map = lambda i, s: (i, s[0])
        out_map = lambda i, s: (s[1] * nb + i, 0)
        shard = (2 * rows, cols)
    else:
        own_map = lambda i, s: (s[0] * nb + i, 0)
        out_map = lambda i, s: (i, s[1])
        shard = (rows, 2 * cols)

    def body(s_ref, own_ref, slot_ref, o_ref):
        total = own_ref[...].astype(F32)
        for j in range(n_p):
            total = total + slot_ref[j].astype(F32)
        o_ref[...] = total

    return pl.pallas_call(
        body, name=name,
        grid_spec=pltpu.PrefetchScalarGridSpec(
            num_scalar_prefetch=1, grid=(nb,),
            in_specs=[pl.BlockSpec((tr, cols), own_map), pl.BlockSpec((n_p, tr, cols), lambda i, s: (0, i, 0))],
            out_specs=pl.BlockSpec((tr, cols), out_map)),
        out_shape=jax.ShapeDtypeStruct(shard, F32),
        compiler_params=_params("parallel"),
    )(place, half, slots)


def _join_halves(kinds, shards):
    @_plan(len(kinds))
    def plan(refs):
        x, y, c, _ = _place()
        return [(_half_view(kind, g, c), _half_view(kind, g, c), (x, y, 1 - c)) for kind, g in zip(kinds, refs)]

    return _Split("join_halves_" + "".join(map(str, kinds)), tuple(shards), plan)


def _sum_small(pack, after):
    n_dev = 8

    def body(p_ref, after_ref, o_ref, slots, send_sems, recv_sems):
        x, y, c, _ = _place()
        me = 4 * x + 2 * y + c
        slots[me] = p_ref[...]
        sends = []
        for m in range(1, n_dev):
            peer = ((1 - x) if m & 4 else x, (1 - y) if m & 2 else y, (1 - c) if m & 1 else c)
            sends.append(pltpu.make_async_remote_copy(
                src_ref=p_ref, dst_ref=slots.at[me], send_sem=send_sems.at[m - 1], recv_sem=recv_sems.at[m - 1],
                device_id=peer, device_id_type=MESH))
        for cp in sends:
            cp.start()
        for m in range(1, n_dev):
            peer = ((1 - x) if m & 4 else x, (1 - y) if m & 2 else y, (1 - c) if m & 1 else c)
            pltpu.make_async_remote_copy(
                src_ref=p_ref, dst_ref=slots.at[4 * peer[0] + 2 * peer[1] + peer[2]], send_sem=send_sems.at[m - 1],
                recv_sem=recv_sems.at[m - 1], device_id=peer, device_id_type=MESH).wait_recv()
        for cp in sends:
            cp.wait_send()
        total = slots[0]
        for d in range(1, n_dev):
            total = total + slots[d]
        o_ref[...] = total

    vm = pl.BlockSpec(memory_space=pltpu.VMEM)
    return pl.pallas_call(
        body, name="sum_small", in_specs=[vm, ANY], out_specs=vm,
        out_shape=jax.ShapeDtypeStruct(pack.shape, F32),
        scratch_shapes=[pltpu.VMEM((n_dev,) + pack.shape, F32), pltpu.SemaphoreType.DMA((n_dev - 1,)),
                        pltpu.SemaphoreType.DMA((n_dev - 1,))],
    )(pack, after)


def _adamw(name, w, g, m, v, after=None):
    rows, cols = w.shape
    tr = min(rows, 256)
    extra = [] if after is None else [after]

    def body(w_ref, g_ref, m_ref, v_ref, *rest):
        d_ref, nm_ref, nv_ref = rest[-3:]
        gv = g_ref[...]
        nm = ADAM_B1 * m_ref[...] + (1.0 - ADAM_B1) * gv
        nv = ADAM_B2 * v_ref[...] + (1.0 - ADAM_B2) * jnp.square(gv)
        m_hat = nm * (1.0 / (1.0 - ADAM_B1 ** ADAM_STEP))
        v_hat = nv * (1.0 / (1.0 - ADAM_B2 ** ADAM_STEP))
        d_ref[...] = -ADAM_LR * (m_hat / (jnp.sqrt(v_hat) + ADAM_EPS) + ADAM_WD * w_ref[...])
        nm_ref[...] = nm
        nv_ref[...] = nv

    blk = pl.BlockSpec((tr, cols), lambda i: (i, 0))
    return pl.pallas_call(
        body, name=name, grid=(rows // tr,), in_specs=[blk] * 4 + [ANY] * len(extra), out_specs=[blk] * 3,
        out_shape=[jax.ShapeDtypeStruct(w.shape, F32)] * 3,
        compiler_params=_params("parallel"),
    )(w, g, m, v, *extra)


def kernel(x, w_in, lb_logits, gate_norm_w, conv_w, w_out, ln1_g, ln1_b, w_ff1, w_ff2, ln2_g, ln2_b, loss_target, m_w_in, m_lb_logits, m_gate_norm_w, m_conv_w, m_w_out, m_ln1_g, m_ln1_b, m_w_ff1, m_w_ff2, m_ln2_g, m_ln2_b, v_w_in, v_lb_logits, v_gate_norm_w, v_conv_w, v_w_out, v_ln1_g, v_ln1_b, v_w_ff1, v_w_ff2, v_ln2_g, v_ln2_b):
    xs, tgt = x[0], loss_target[0]
    chip = 2 * lax.axis_index("x") + lax.axis_index("y")
    core = lax.axis_index("c").astype(jnp.int32).reshape(1)
    chip1 = chip.astype(jnp.int32).reshape(1)
    place = jnp.concatenate([chip1, core])

    wb_in, cv4 = _gather_w_in(_place_shard("place_w_in", w_in[0], chip1, True), conv_w[0])
    conv_full = cv4.transpose(1, 0, 2).reshape(3, CONV_WIDTH)
    ici_out = _gather_over_ici((1,), (_place_shard("place_w_out", w_out[0], chip1, False, after=wb_in),))
    ici_ff = _gather_over_ici((2, 3), (_place_shard("place_w_ff1", w_ff1[0], chip1, True, after=ici_out.token),
                                       _place_shard("place_w_ff2", w_ff2[0], chip1, False, after=ici_out.token)))

    proj, xb = _in_proj(xs, wb_in, ici_ff.token)
    o, states = _hgrn_fwd(proj, lb_logits)
    d2d_out = _gather_over_d2d((1,), ici_out.wait(o))
    cat = _gate_fwd(proj, o, gate_norm_w, conv_full, d2d_out.token)
    wb_out, = d2d_out.wait(cat)
    d2d_ff = _gather_over_d2d((2, 3), ici_ff.wait(cat))
    xhat1, h1b, rstd1 = _out_ln1(cat, wb_out, xs, ln1_g, ln1_b, d2d_ff.token)
    wb_ff1, wb_ff2 = d2d_ff.wait(xhat1)
    r, dpre2, dpre2b, g_ln2_g, g_ln2_b, loss8 = _mlp_fwd(xhat1, ln1_g, ln1_b, wb_ff1, wb_ff2, ln2_g, ln2_b, tgt)

    names = ("w_in", "w_out", "w_ff1", "w_ff2")

    def add_halves(kinds, grads, lands):
        return [_add_half("add_half_" + names[k], g, ld, core, COLS_SHARDED[k]) for k, g, ld in zip(kinds, grads, lands)]

    def sum_pieces(kinds, halves, lands):
        return [_sum_pieces("sum_pieces_" + names[k], h, ld, place, COLS_SHARDED[k]) for k, h, ld in zip(kinds, halves, lands)]

    da, dpre1, dpre1b, g_ln1_g, g_ln1_b = _mlp_bwd(dpre2, r, wb_ff1, wb_ff2, xhat1, rstd1, ln1_g)
    dcat = _out_bwd(dpre1b, wb_out)
    early = (1, 2, 3)
    swap = _swap_halves(early, (_dw_out(cat, dpre1b), _dw_ff1(h1b, da), _dw_ff2(r, dpre2b)))
    do, dpg, g_gnw, g_conv = _gate_bwd(dcat, o, proj, gate_norm_w, conv_full, swap.token)
    swapped = swap.wait(do)
    exch = _exchange_pieces(early, add_halves(early, swapped[:3], swapped[3:]))
    dph, g_lbl = _hgrn_bwd(proj, do, states, lb_logits, exch.token)
    g_in_local = _dw_in(xb, dph, dpg, dph)

    late = (0,)
    swap = _swap_halves(late, (g_in_local,))
    grad_x = _in_bwd(dph, dpg, wb_in, dpre1, swap.token)
    exchanged = exch.wait(grad_x)
    join = _join_halves(early, sum_pieces(early, exchanged[:3], exchanged[3:]))
    pack = jnp.concatenate([
        g_ln1_g, g_ln1_b, g_ln2_g, g_ln2_b,
        jnp.concatenate([g_lbl[0:1], g_lbl[1:2]], axis=1),
        jnp.concatenate([g_gnw, g_conv[0:1]], axis=1),
        jnp.concatenate([g_conv[1:2], g_conv[2:3]], axis=1),
        jnp.concatenate([loss8[0:1], jnp.zeros((1, D_MODEL - LANES), F32)], axis=1)], axis=0)
    tot = _sum_small(pack, join.token)
    loss = tot[7, 0]
    half = D_MODEL // 2
    g_lb_logits = jnp.concatenate([tot[4:5, :half], tot[4:5, half:]], axis=0)
    g_gate_norm_w = tot[5:6, :half]
    g_conv_full = jnp.concatenate([tot[5:6, half:], tot[6:7, :half], tot[6:7, half:]], axis=0)
    g_conv_w = lax.dynamic_slice(g_conv_full, (0, chip * LANES), (3, LANES))
    swapped = swap.wait(tot)
    exch = _exchange_pieces(late, add_halves(late, swapped[:1], swapped[1:]))
    g_w_out, g_w_ff1, g_w_ff2 = join.wait(exch.token)
    d_ff1, nm_ff1, nv_ff1 = _adamw("adamw_w_ff1", w_ff1[0], g_w_ff1, m_w_ff1[0], v_w_ff1[0])
    d_ff2, nm_ff2, nv_ff2 = _adamw("adamw_w_ff2", w_ff2[0], g_w_ff2, m_w_ff2[0], v_w_ff2[0], d_ff1)
    d_out, nm_out, nv_out = _adamw("adamw_w_out", w_out[0], g_w_out, m_w_out[0], v_w_out[0], d_ff2)

    def small_pack(lbl, gnw, cv, l1g, l1b, l2g, l2b):
        pad = jnp.zeros((1, D_MODEL - 3 * LANES), F32)
        return jnp.concatenate([
            l1g, l1b, l2g, l2b, jnp.concatenate([lbl[0:1], lbl[1:2]], axis=1),
            jnp.concatenate([gnw, jnp.zeros((1, half), F32)], axis=1),
            jnp.concatenate([cv[0:1], cv[1:2], cv[2:3], pad], axis=1), jnp.zeros((1, D_MODEL), F32)], axis=0)

    w_s = small_pack(lb_logits, gate_norm_w, conv_w[0], ln1_g, ln1_b, ln2_g, ln2_b)
    g_s = small_pack(g_lb_logits, g_gate_norm_w, g_conv_w, tot[0:1], tot[1:2], tot[2:3], tot[3:4])
    m_s = small_pack(m_lb_logits, m_gate_norm_w, m_conv_w[0], m_ln1_g, m_ln1_b, m_ln2_g, m_ln2_b)
    v_s = small_pack(v_lb_logits, v_gate_norm_w, v_conv_w[0], v_ln1_g, v_ln1_b, v_ln2_g, v_ln2_b)
    d_s, nm_s, nv_s = _adamw("adamw_small", w_s, g_s, m_s, v_s, d_out)
    exchanged = exch.wait(d_s)
    join = _join_halves(late, sum_pieces(late, exchanged[:1], exchanged[1:]))
    g_w_in, = join.wait(join.token)
    d_in, nm_in, nv_in = _adamw("adamw_w_in", w_in[0], g_w_in, m_w_in[0], v_w_in[0])

    def unpack(p):
        lbl = jnp.concatenate([p[4:5, :half], p[4:5, half:]], axis=0)
        cv = jnp.concatenate([p[6:7, 0:LANES], p[6:7, LANES:2 * LANES], p[6:7, 2 * LANES:3 * LANES]], axis=0)
        return dict(lb_logits=lbl, gate_norm_w=p[5:6, :half], conv_w=cv[None], ln1_g=p[0:1], ln1_b=p[1:2],
                    ln2_g=p[2:3], ln2_b=p[3:4])

    order = ("w_in", "lb_logits", "gate_norm_w", "conv_w", "w_out", "ln1_g", "ln1_b", "w_ff1", "w_ff2", "ln2_g", "ln2_b")
    grad = dict(unpack(g_s), w_in=g_w_in[None], w_out=g_w_out[None], w_ff1=g_w_ff1[None], w_ff2=g_w_ff2[None])
    delta = dict(unpack(d_s), w_in=d_in[None], w_out=d_out[None], w_ff1=d_ff1[None], w_ff2=d_ff2[None])
    new_m = dict(unpack(nm_s), w_in=nm_in[None], w_out=nm_out[None], w_ff1=nm_ff1[None], w_ff2=nm_ff2[None])
    new_v = dict(unpack(nv_s), w_in=nv_in[None], w_out=nv_out[None], w_ff1=nv_ff1[None], w_ff2=nv_ff2[None])
    return (loss, grad_x[None], *[grad[n] for n in order], *[delta[n] for n in order],
            *[new_m[n] for n in order], *[new_v[n] for n in order])
```

```python
import jax
import jax.numpy as jnp
from jax import lax
from jax.experimental import pallas as pl
from jax.experimental.pallas import tpu as pltpu

F32 = jnp.float32
BF16 = jnp.bfloat16
MXU_DTYPE = jnp.bfloat16

D_MODEL = 1024
HGRN_WIDTH = 512
HEAD_DIM = 128
N_HEADS = 4
CONV_WIDTH = 512
CHUNK = 64
D_FF = 4096
IN_COLS = 3584
GROUP = 512
N_GROUPS = IN_COLS // GROUP
ALPHA = 2.0 ** 0.25
EPS = 1e-5
N_CHIPS = 4
ADAM_LR, ADAM_B1, ADAM_B2, ADAM_EPS, ADAM_WD, ADAM_STEP = 0.001, 0.9, 0.999, 1e-08, 0.01, 10

LANES = 128
SUBLANES = 8
VMEM_LIMIT = 56 * 1024 * 1024
FF_BLOCK = 1024
N_FF = D_FF // FF_BLOCK

NN = (((1,), (0,)), ((), ()))
NT = (((1,), (1,)), ((), ()))
TN = (((0,), (0,)), ((), ()))
MESH = pl.DeviceIdType.MESH
ANY = pl.BlockSpec(memory_space=pl.ANY)


def _dot(a, b, dims):
    return lax.dot_general(a.astype(MXU_DTYPE), b.astype(MXU_DTYPE), dims, preferred_element_type=F32)


def _dot_exact(ones, v):
    ones = ones.astype(jnp.bfloat16)
    hi = v.astype(jnp.bfloat16)
    rest = v - hi.astype(F32)
    mid = rest.astype(jnp.bfloat16)
    low = (rest - mid.astype(F32)).astype(jnp.bfloat16)
    return sum(lax.dot_general(ones, part, NN, preferred_element_type=F32) for part in (hi, mid, low))


def _params(*sem):
    return pltpu.CompilerParams(dimension_semantics=sem, vmem_limit_bytes=VMEM_LIMIT)


def _resident(shape):
    return pl.BlockSpec(shape, lambda *_: (0,) * len(shape), pipeline_mode=pl.Buffered(1))


def _sigmoid(v):
    return 1.0 / (1.0 + jnp.exp(-v))


def _lower_bound(lbl):
    m = jnp.max(lbl, axis=0, keepdims=True)
    e = jnp.exp(lbl - m)
    s = e / jnp.sum(e, axis=0, keepdims=True)
    return s[0:1, :], s[1:2, :]


def _heads(v):
    return [v[:, h * HEAD_DIM:(h + 1) * HEAD_DIM] for h in range(N_HEADS)]


def _per_head(fn, *arrays):
    return jnp.concatenate([fn(*parts) for parts in zip(*map(_heads, arrays))], axis=1)


def _in_proj(x, w_in, after):
    t = x.shape[0]
    tm = min(t, 512)

    def body(x_ref, w_ref, after_ref, o_ref, xb_ref):
        xb = x_ref[...].astype(xb_ref.dtype)
        xb_ref[...] = xb
        for g in range(N_GROUPS):
            o_ref[g] = _dot(xb, w_ref[:, g * GROUP:(g + 1) * GROUP], NN)

    return pl.pallas_call(
        body, name="in_proj", grid=(t // tm,),
        in_specs=[pl.BlockSpec((tm, D_MODEL), lambda i: (i, 0)), _resident((D_MODEL, IN_COLS)), ANY],
        out_specs=[pl.BlockSpec((N_GROUPS, tm, GROUP), lambda i: (0, i, 0)), pl.BlockSpec((tm, D_MODEL), lambda i: (i, 0))],
        out_shape=[jax.ShapeDtypeStruct((N_GROUPS, t, GROUP), F32), jax.ShapeDtypeStruct((t, D_MODEL), BF16)],
        compiler_params=_params("parallel"),
    )(x, w_in, after)


def _gates(fp, lb):
    sig = _sigmoid(fp)
    f = lb + (1.0 - lb) * sig
    return sig, f, jnp.log(f), 1.0 - f


def _chunk_masks():
    row = lax.broadcasted_iota(jnp.int32, (CHUNK, CHUNK), 0)
    col = lax.broadcasted_iota(jnp.int32, (CHUNK, CHUNK), 1)
    return row >= col, row <= col


def _hgrn_fwd(proj, lb_logits):
    t = proj.shape[1]
    tb = min(t, 512)
    ncb = tb // CHUNK

    def body(q_ref, f_ref, v_ref, lbl_ref, o_ref, st_ref, s_scr):
        @pl.when(pl.program_id(0) == 0)
        def _():
            s_scr[...] = jnp.zeros_like(s_scr)

        lb, _ = _lower_bound(lbl_ref[...])
        causal, _ = _chunk_masks()

        every = range(ncb)
        rows = [slice(c * CHUNK, (c + 1) * CHUNK) for c in every]
        q, v = [q_ref[r, :] for r in rows], [v_ref[r, :] for r in rows]
        gates = [_gates(f_ref[r, :], lb) for r in rows]
        k = [gt[3] for gt in gates]
        b = [_dot_exact(causal, gt[2]) for gt in gates]
        mid, last = [x[CHUNK // 2:CHUNK // 2 + 1, :] for x in b], [x[CHUNK - 1:CHUNK, :] for x in b]
        qt = [q[c] * jnp.exp(b[c] - mid[c]) for c in every]
        kt = [k[c] * jnp.exp(mid[c] - b[c]) for c in every]
        qi = [q[c] * jnp.exp(b[c]) for c in every]
        ks = [k[c] * jnp.exp(last[c] - b[c]) for c in every]
        dec = [jnp.exp(x) for x in last]
        scores = [[jnp.where(causal, _dot(a, b_, NT), 0.0) for a, b_ in zip(_heads(qt[c]), _heads(kt[c]))] for c in every]
        intra = [[_dot(s, v_h, NN) for s, v_h in zip(scores[c], _heads(v[c]))] for c in every]
        update = [_per_head(lambda v_h, ks_h: _dot(v_h, ks_h, TN), v[c], ks[c]) for c in every]

        st = s_scr[...]
        states = []
        for c in every:
            states.append(st)
            st_ref[c] = st
            st = dec[c] * st + update[c]
        s_scr[...] = st

        o_ref[...] = jnp.concatenate(
            [jnp.concatenate([i_h + _dot(qi_h, st_h, NT) for i_h, qi_h, st_h in
                              zip(intra[c], _heads(qi[c]), _heads(states[c]))], axis=1) for c in every], axis=0)

    grp = lambda g: pl.BlockSpec((None, tb, GROUP), lambda i: (g, i, 0))
    return pl.pallas_call(
        body, name="hgrn_fwd", grid=(t // tb,),
        in_specs=[grp(0), grp(1), grp(2), pl.BlockSpec((2, HGRN_WIDTH), lambda i: (0, 0))],
        out_specs=[pl.BlockSpec((tb, HGRN_WIDTH), lambda i: (i, 0)),
                   pl.BlockSpec((ncb, HEAD_DIM, HGRN_WIDTH), lambda i: (i, 0, 0))],
        out_shape=[jax.ShapeDtypeStruct((t, HGRN_WIDTH), F32),
                   jax.ShapeDtypeStruct((t // CHUNK, HEAD_DIM, HGRN_WIDTH), F32)],
        scratch_shapes=[pltpu.VMEM((HEAD_DIM, HGRN_WIDTH), F32)],
        compiler_params=_params("arbitrary"),
    )(proj, proj, proj, lb_logits)


def _conv_taps(z, halo, zbuf, tb):
    zbuf[0:SUBLANES, :] = halo
    zbuf[SUBLANES:SUBLANES + tb, :] = z
    return zbuf[SUBLANES - 1:SUBLANES - 1 + tb, :], zbuf[SUBLANES - 2:SUBLANES - 2 + tb, :]


def _gate_fwd(proj, o, gate_norm_w, conv_w, after):
    t = proj.shape[1]
    tb = min(t, 512)
    hb = tb // SUBLANES

    def body(o_ref, og_ref, gnw_ref, b_ref, c_ref, u_ref, ch_ref, uh_ref, cw_ref, after_ref, cat_ref, zbuf):
        i = pl.program_id(0)
        og = og_ref[...]
        on = _per_head(lambda o_h: o_h * lax.rsqrt(jnp.mean(o_h * o_h, axis=-1, keepdims=True) + EPS), o_ref[...])
        cat_ref[0] = (on * gnw_ref[...] * (og * _sigmoid(og))).astype(cat_ref.dtype)
        z = c_ref[...] * u_ref[...]
        halo = jnp.where(i > 0, ch_ref[...] * uh_ref[...], 0.0)
        z1, z2 = _conv_taps(z, halo, zbuf, tb)
        cw = cw_ref[...]
        yc = cw[2:3, :] * z + cw[1:2, :] * z1 + cw[0:1, :] * z2
        cat_ref[1] = (b_ref[...] * yc).astype(cat_ref.dtype)

    grp = lambda g: pl.BlockSpec((None, tb, GROUP), lambda i: (g, i, 0))
    prev = lambda g: pl.BlockSpec((None, SUBLANES, GROUP), lambda i: (g, jnp.maximum(i * hb - 1, 0), 0))
    vec = lambda r: pl.BlockSpec((r, GROUP), lambda i: (0, 0))
    return pl.pallas_call(
        body, name="gate_fwd", grid=(t // tb,),
        in_specs=[pl.BlockSpec((tb, GROUP), lambda i: (i, 0)), grp(3), vec(1), grp(4), grp(5), grp(6), prev(5), prev(6),
                  vec(3), ANY],
        out_specs=pl.BlockSpec((2, tb, GROUP), lambda i: (0, i, 0)),
        out_shape=jax.ShapeDtypeStruct((2, t, HGRN_WIDTH), BF16),
        scratch_shapes=[pltpu.VMEM((tb + SUBLANES, GROUP), F32)],
        compiler_params=_params("parallel"),
    )(o, proj, gate_norm_w, proj, proj, proj, proj, proj, conv_w, after)


def _out_ln1(cat, w_out, x, g1, b1, after):
    t = x.shape[0]
    tm = min(t, 512)

    def body(cat_ref, w_ref, x_ref, g_ref, b_ref, after_ref, xhat_ref, h1_ref, rstd_ref):
        mix = _dot(cat_ref[0], w_ref[0:GROUP, :], NN) + _dot(cat_ref[1], w_ref[GROUP:2 * GROUP, :], NN)
        pre = ALPHA * x_ref[...] + mix
        xc = pre - jnp.mean(pre, axis=-1, keepdims=True)
        rstd = lax.rsqrt(jnp.mean(xc * xc, axis=-1, keepdims=True) + EPS)
        xhat = xc * rstd
        xhat_ref[...] = xhat
        h1_ref[...] = (xhat * g_ref[...] + b_ref[...]).astype(h1_ref.dtype)
        rstd_ref[...] = rstd

    row = pl.BlockSpec((tm, D_MODEL), lambda i: (i, 0))
    vec = pl.BlockSpec((1, D_MODEL), lambda i: (0, 0))
    return pl.pallas_call(
        body, name="out_ln1", grid=(t // tm,),
        in_specs=[pl.BlockSpec((2, tm, GROUP), lambda i: (0, i, 0)), _resident((D_MODEL, D_MODEL)), row, vec, vec, ANY],
        out_specs=[row, row, pl.BlockSpec((tm, 1), lambda i: (i, 0))],
        out_shape=[jax.ShapeDtypeStruct((t, D_MODEL), F32), jax.ShapeDtypeStruct((t, D_MODEL), BF16),
                   jax.ShapeDtypeStruct((t, 1), F32)],
        compiler_params=_params("parallel"),
    )(cat, w_out, x, g1, b1, after)


def _ln_bwd(dy, xhat, rstd, g):
    dxhat = dy * g
    m1 = jnp.mean(dxhat, axis=-1, keepdims=True)
    m2 = jnp.mean(dxhat * xhat, axis=-1, keepdims=True)
    return rstd * (dxhat - m1 - xhat * m2)


def _mlp_fwd(xhat1, g1, b1, w_ff1, w_ff2, g2, b2, target):
    t = xhat1.shape[0]
    tm = min(t, 256)

    def body(xh_ref, g1_ref, b1_ref, w1_ref, w2_ref, g2_ref, b2_ref, tg_ref,
             r_ref, dpre_ref, dpreb_ref, dg_ref, db_ref, loss_ref):
        @pl.when(pl.program_id(0) == 0)
        def _():
            dg_ref[...] = jnp.zeros_like(dg_ref)
            db_ref[...] = jnp.zeros_like(db_ref)
            loss_ref[...] = jnp.zeros_like(loss_ref)

        h1 = xh_ref[...] * g1_ref[...] + b1_ref[...]
        h1b = h1.astype(MXU_DTYPE)
        mlp = jnp.zeros((tm, D_MODEL), F32)
        for j in range(N_FF):
            cols = slice(j * FF_BLOCK, (j + 1) * FF_BLOCK)
            r = jnp.square(jnp.maximum(_dot(h1b, w1_ref[:, cols], NN), 0.0)).astype(r_ref.dtype)
            r_ref[:, cols] = r
            mlp = mlp + _dot(r, w2_ref[cols, :], NN)
        pre = ALPHA * h1 + mlp
        xc = pre - jnp.mean(pre, axis=-1, keepdims=True)
        rstd = lax.rsqrt(jnp.mean(xc * xc, axis=-1, keepdims=True) + EPS)
        xhat = xc * rstd
        err = xhat * g2_ref[...] + b2_ref[...] - tg_ref[...]
        loss_ref[...] += 0.5 * jnp.sum(jnp.mean(err * err, axis=-1, keepdims=True))
        dy = err * (1.0 / D_MODEL)
        dg_ref[...] += jnp.sum(dy * xhat, axis=0, keepdims=True)
        db_ref[...] += jnp.sum(dy, axis=0, keepdims=True)
        dpre = _ln_bwd(dy, xhat, rstd, g2_ref[...])
        dpre_ref[...] = dpre
        dpreb_ref[...] = dpre.astype(dpreb_ref.dtype)

    row = pl.BlockSpec((tm, D_MODEL), lambda i: (i, 0))
    vec = pl.BlockSpec((1, D_MODEL), lambda i: (0, 0))
    return pl.pallas_call(
        body, name="mlp_fwd", grid=(t // tm,),
        in_specs=[row, vec, vec, _resident((D_MODEL, D_FF)), _resident((D_FF, D_MODEL)), vec, vec, row],
        out_specs=[pl.BlockSpec((tm, D_FF), lambda i: (i, 0)), row, row, vec, vec,
                   pl.BlockSpec((SUBLANES, LANES), lambda i: (0, 0))],
        out_shape=[jax.ShapeDtypeStruct((t, D_FF), BF16), jax.ShapeDtypeStruct((t, D_MODEL), F32),
                   jax.ShapeDtypeStruct((t, D_MODEL), BF16), jax.ShapeDtypeStruct((1, D_MODEL), F32),
                   jax.ShapeDtypeStruct((1, D_MODEL), F32), jax.ShapeDtypeStruct((SUBLANES, LANES), F32)],
        compiler_params=_params("arbitrary"),
    )(xhat1, g1, b1, w_ff1, w_ff2, g2, b2, target)


def _mlp_bwd(dpre2, r, w_ff1, w_ff2, xhat1, rstd1, g1):
    t = r.shape[0]
    tm = min(t, 256)

    def body(dp2_ref, r_ref, w1_ref, w2_ref, xh_ref, rs_ref, g_ref, da_ref, dpre_ref, dpreb_ref, dg_ref, db_ref):
        @pl.when(pl.program_id(0) == 0)
        def _():
            dg_ref[...] = jnp.zeros_like(dg_ref)
            db_ref[...] = jnp.zeros_like(db_ref)

        dp2 = dp2_ref[...]
        dp2b = dp2.astype(MXU_DTYPE)
        back = jnp.zeros((tm, D_MODEL), F32)
        for j in range(N_FF):
            cols = slice(j * FF_BLOCK, (j + 1) * FF_BLOCK)
            dr = _dot(dp2b, w2_ref[cols, :], NT)
            da = (dr * (2.0 * jnp.sqrt(r_ref[:, cols].astype(F32)))).astype(da_ref.dtype)
            da_ref[:, cols] = da
            back = back + _dot(da, w1_ref[:, cols], NT)
        dh1 = ALPHA * dp2 + back
        xhat = xh_ref[...]
        dg_ref[...] += jnp.sum(dh1 * xhat, axis=0, keepdims=True)
        db_ref[...] += jnp.sum(dh1, axis=0, keepdims=True)
        dpre = _ln_bwd(dh1, xhat, rs_ref[...], g_ref[...])
        dpre_ref[...] = dpre
        dpreb_ref[...] = dpre.astype(dpreb_ref.dtype)

    row = pl.BlockSpec((tm, D_MODEL), lambda i: (i, 0))
    wide = pl.BlockSpec((tm, D_FF), lambda i: (i, 0))
    vec = pl.BlockSpec((1, D_MODEL), lambda i: (0, 0))
    return pl.pallas_call(
        body, name="mlp_bwd", grid=(t // tm,),
        in_specs=[row, wide, _resident((D_MODEL, D_FF)), _resident((D_FF, D_MODEL)), row,
                  pl.BlockSpec((tm, 1), lambda i: (i, 0)), vec],
        out_specs=[wide, row, row, vec, vec],
        out_shape=[jax.ShapeDtypeStruct((t, D_FF), BF16), jax.ShapeDtypeStruct((t, D_MODEL), F32),
                   jax.ShapeDtypeStruct((t, D_MODEL), BF16), jax.ShapeDtypeStruct((1, D_MODEL), F32),
                   jax.ShapeDtypeStruct((1, D_MODEL), F32)],
        compiler_params=_params("arbitrary"),
    )(dpre2, r, w_ff1, w_ff2, xhat1, rstd1, g1)


def _out_bwd(dpre1b, w_out):
    t = dpre1b.shape[0]
    tm = min(t, 512)

    def body(d_ref, w_ref, o_ref):
        o_ref[...] = _dot(d_ref[...], w_ref[...], NT)

    return pl.pallas_call(
        body, name="out_bwd", grid=(t // tm,),
        in_specs=[pl.BlockSpec((tm, D_MODEL), lambda i: (i, 0)), _resident((D_MODEL, D_MODEL))],
        out_specs=pl.BlockSpec((tm, D_MODEL), lambda i: (i, 0)),
        out_shape=jax.ShapeDtypeStruct((t, D_MODEL), F32),
        compiler_params=_params("parallel"),
    )(dpre1b, w_out)


def _gate_bwd(dcat, o, proj, gate_norm_w, conv_w, after):
    t = proj.shape[1]
    tb = min(t, 512)
    hb = tb // SUBLANES
    nblk = t // tb

    def body(do2_ref, dy_ref, dyn_ref, o_ref, og_ref, gnw_ref, b_ref, bn_ref, c_ref, u_ref, ch_ref, uh_ref, cw_ref,
             after_ref, do_ref, dp_ref, dgnw_ref, dcw_ref, zbuf, dbuf):
        i = pl.program_id(0)

        @pl.when(i == 0)
        def _():
            dgnw_ref[...] = jnp.zeros_like(dgnw_ref)
            dcw_ref[...] = jnp.zeros_like(dcw_ref)

        ov, og, gnw, do2 = o_ref[...], og_ref[...], gnw_ref[...], do2_ref[...]
        rs = _per_head(lambda o_h: jnp.broadcast_to(lax.rsqrt(jnp.mean(o_h * o_h, axis=-1, keepdims=True) + EPS),
                                                    o_h.shape), ov)
        on = ov * rs
        sg = _sigmoid(og)
        sil = og * sg
        don = do2 * gnw * sil
        dgnw_ref[...] += jnp.sum(do2 * on * sil, axis=0, keepdims=True)
        dp_ref[0] = (do2 * on * gnw * (sg * (1.0 + og * (1.0 - sg)))).astype(dp_ref.dtype)
        do_ref[...] = rs * (don - on * _per_head(
            lambda p_h: jnp.broadcast_to(jnp.mean(p_h, axis=-1, keepdims=True), p_h.shape), don * on))

        bg, cg, u, dy = b_ref[...], c_ref[...], u_ref[...], dy_ref[...]
        z = cg * u
        halo = jnp.where(i > 0, ch_ref[...] * uh_ref[...], 0.0)
        z1, z2 = _conv_taps(z, halo, zbuf, tb)
        cw = cw_ref[...]
        yc = cw[2:3, :] * z + cw[1:2, :] * z1 + cw[0:1, :] * z2
        dyc = dy * bg
        dbuf[0:tb, :] = dyc
        dbuf[tb:tb + SUBLANES, :] = jnp.where(i < nblk - 1, dyn_ref[...] * bn_ref[...], 0.0)
        d1, d2 = dbuf[1:1 + tb, :], dbuf[2:2 + tb, :]
        dz = cw[2:3, :] * dyc + cw[1:2, :] * d1 + cw[0:1, :] * d2
        dp_ref[1] = (dy * yc).astype(dp_ref.dtype)
        dp_ref[2] = (dz * u).astype(dp_ref.dtype)
        dp_ref[3] = (dz * cg).astype(dp_ref.dtype)
        dcw_ref[0:1, :] += jnp.sum(dyc * z2, axis=0, keepdims=True)
        dcw_ref[1:2, :] += jnp.sum(dyc * z1, axis=0, keepdims=True)
        dcw_ref[2:3, :] += jnp.sum(dyc * z, axis=0, keepdims=True)

    half = lambda g: pl.BlockSpec((tb, GROUP), lambda i: (i, g))
    grp = lambda g: pl.BlockSpec((None, tb, GROUP), lambda i: (g, i, 0))
    prev = lambda g: pl.BlockSpec((None, SUBLANES, GROUP), lambda i: (g, jnp.maximum(i * hb - 1, 0), 0))
    nxt_row = lambda i: jnp.minimum((i + 1) * hb, t // SUBLANES - 1)
    nxt = lambda g: pl.BlockSpec((None, SUBLANES, GROUP), lambda i: (g, nxt_row(i), 0))
    vec = lambda r: pl.BlockSpec((r, GROUP), lambda i: (0, 0))
    return pl.pallas_call(
        body, name="gate_bwd", grid=(nblk,),
        in_specs=[half(0), half(1), pl.BlockSpec((SUBLANES, GROUP), lambda i: (nxt_row(i), 1)), half(0), grp(3), vec(1),
                  grp(4), nxt(4), grp(5), grp(6), prev(5), prev(6), vec(3), ANY],
        out_specs=[half(0), pl.BlockSpec((4, tb, GROUP), lambda i: (0, i, 0)), vec(1), vec(3)],
        out_shape=[jax.ShapeDtypeStruct((t, HGRN_WIDTH), F32), jax.ShapeDtypeStruct((4, t, HGRN_WIDTH), BF16),
                   jax.ShapeDtypeStruct((1, HGRN_WIDTH), F32), jax.ShapeDtypeStruct((3, CONV_WIDTH), F32)],
        scratch_shapes=[pltpu.VMEM((tb + SUBLANES, GROUP), F32), pltpu.VMEM((tb + SUBLANES, GROUP), F32)],
        compiler_params=_params("arbitrary"),
    )(dcat, dcat, dcat, o, proj, gate_norm_w, proj, proj, proj, proj, proj, proj, conv_w, after)


def _hgrn_bwd(proj, do, states, lb_logits, after):
    t = proj.shape[1]
    tb = min(t, 512)
    ncb = tb // CHUNK
    nblk = t // tb

    def body(q_ref, f_ref, v_ref, do_ref, st_ref, lbl_ref, after_ref, dp_ref, dlbl_ref, ds_scr, dlb_scr):
        i = pl.program_id(0)

        @pl.when(i == 0)
        def _():
            ds_scr[...] = jnp.zeros_like(ds_scr)
            dlb_scr[...] = jnp.zeros_like(dlb_scr)

        lb, s1 = _lower_bound(lbl_ref[...])
        causal, anti = _chunk_masks()
        every = range(ncb)
        rows = [slice(c * CHUNK, (c + 1) * CHUNK) for c in every]
        q, v, do = ([ref[r, :] for r in rows] for ref in (q_ref, v_ref, do_ref))
        st = [st_ref[c] for c in every]
        gates = [_gates(f_ref[r, :], lb) for r in rows]
        sig, f, k = ([gt[n] for gt in gates] for n in (0, 1, 3))
        b = [_dot_exact(causal, gt[2]) for gt in gates]
        mid, last = [x[CHUNK // 2:CHUNK // 2 + 1, :] for x in b], [x[CHUNK - 1:CHUNK, :] for x in b]
        e_q = [jnp.exp(b[c] - mid[c]) for c in every]
        e_k = [jnp.exp(mid[c] - b[c]) for c in every]
        e_i = [jnp.exp(x) for x in b]
        e_s = [jnp.exp(last[c] - b[c]) for c in every]
        dec = [jnp.exp(x) for x in last]
        qt, kt, qi, ks = ([a[c] * e[c] for c in every] for a, e in ((q, e_q), (k, e_k), (q, e_i), (k, e_s)))

        def masked(a, b_):
            return [[jnp.where(causal, _dot(a_h, b_h, NT), 0.0) for a_h, b_h in zip(_heads(a[c]), _heads(b_[c]))]
                    for c in every]

        def with_scores(s, other, dims):
            return [jnp.concatenate([_dot(s_h, o_h, dims) for s_h, o_h in zip(s[c], _heads(other[c]))], axis=1)
                    for c in every]

        def per_head(dims, a, b_):
            return [_per_head(lambda a_h, b_h: _dot(a_h, b_h, dims), a[c], b_[c]) for c in every]

        scores, dscores = masked(qt, kt), masked(do, v)
        dqt, dkt, dv_intra = with_scores(dscores, kt, NN), with_scores(dscores, qt, TN), with_scores(scores, do, TN)
        dqi, update = per_head(NN, do, st), per_head(TN, do, qi)

        dst = ds_scr[...]
        dsts = [None] * ncb
        for c in reversed(every):
            dsts[c] = dst
            dst = dec[c] * dst + update[c]
        ds_scr[...] = dst

        dv_state, dks = per_head(NT, ks, dsts), per_head(NN, v, dsts)
        ddec = [jnp.sum(dsts[c] * st[c], axis=0, keepdims=True) for c in every]
        dq = [dqt[c] * e_q[c] + dqi[c] * e_i[c] for c in every]
        dk = [dkt[c] * e_k[c] + dks[c] * e_s[c] for c in every]
        db = [q[c] * dq[c] - k[c] * dk[c] for c in every]
        db_last = [jnp.sum(dks[c] * ks[c], axis=0, keepdims=True) + ddec[c] * dec[c] for c in every]
        dg = [_dot_exact(anti, db[c]) + db_last[c] for c in every]
        df = [dg[c] / f[c] - dk[c] for c in every]
        dlb_scr[...] += sum(jnp.sum(df[c] * (1.0 - sig[c]), axis=0, keepdims=True) for c in every)
        dfp = [df[c] * (1.0 - lb) * sig[c] * (1.0 - sig[c]) for c in every]
        dv = [dv_intra[c] + dv_state[c] for c in every]
        for n, parts in enumerate((dq, dfp, dv)):
            dp_ref[n] = jnp.concatenate(parts, axis=0).astype(dp_ref.dtype)

        @pl.when(i == nblk - 1)
        def _():
            dlb = dlb_scr[...]
            dlbl_ref[0:1, :] = dlb * lb * (1.0 - lb)
            dlbl_ref[1:2, :] = -dlb * lb * s1

    grp = lambda g: pl.BlockSpec((None, tb, GROUP), lambda i: (g, nblk - 1 - i, 0))
    vec = pl.BlockSpec((2, HGRN_WIDTH), lambda i: (0, 0))
    return pl.pallas_call(
        body, name="hgrn_bwd", grid=(nblk,),
        in_specs=[grp(0), grp(1), grp(2), pl.BlockSpec((tb, HGRN_WIDTH), lambda i: (nblk - 1 - i, 0)),
                  pl.BlockSpec((ncb, HEAD_DIM, HGRN_WIDTH), lambda i: (nblk - 1 - i, 0, 0)), vec, ANY],
        out_specs=[pl.BlockSpec((3, tb, HGRN_WIDTH), lambda i: (0, nblk - 1 - i, 0)), vec],
        out_shape=[jax.ShapeDtypeStruct((3, t, HGRN_WIDTH), BF16), jax.ShapeDtypeStruct((2, HGRN_WIDTH), F32)],
        scratch_shapes=[pltpu.VMEM((HEAD_DIM, HGRN_WIDTH), F32), pltpu.VMEM((1, HGRN_WIDTH), F32)],
        compiler_params=_params("arbitrary"),
    )(proj, proj, proj, do, states, lb_logits, after)


def _in_bwd(dph, dpg, w_in, dpre1, after):
    t = dpre1.shape[0]
    tm = min(t, 512)

    def body(dh_ref, dg_ref, w_ref, dp_ref, after_ref, o_ref):
        acc = ALPHA * dp_ref[...]
        for g in range(N_GROUPS):
            part = dh_ref[g] if g < 3 else dg_ref[g - 3]
            acc = acc + _dot(part, w_ref[:, g * GROUP:(g + 1) * GROUP], NT)
        o_ref[...] = acc

    row = pl.BlockSpec((tm, D_MODEL), lambda i: (i, 0))
    return pl.pallas_call(
        body, name="in_bwd", grid=(t // tm,),
        in_specs=[pl.BlockSpec((3, tm, GROUP), lambda i: (0, i, 0)), pl.BlockSpec((4, tm, GROUP), lambda i: (0, i, 0)),
                  _resident((D_MODEL, IN_COLS)), row, ANY],
        out_specs=row,
        out_shape=jax.ShapeDtypeStruct((t, D_MODEL), F32),
        compiler_params=_params("parallel"),
    )(dph, dpg, w_in, dpre1, after)


def _grad_w(name, operands, widths, shape, step, after=None):
    t = operands[0].shape[-2]
    tt = min(t, 512)
    n_in, n_steps = len(operands), t // tt
    in_specs = [pl.BlockSpec((tt, w), lambda k: (k, 0)) if a.ndim == 2 else
                pl.BlockSpec((a.shape[0], tt, w), lambda k: (0, k, 0)) for a, w in zip(operands, widths)]
    extra = [] if after is None else [after]

    def body(*refs):
        o_ref, acc, sem = refs[-3:]
        k = pl.program_id(0)

        @pl.when(k == 0)
        def _():
            acc[...] = jnp.zeros_like(acc)

        step(acc, *refs[:n_in])

        @pl.when(k == n_steps - 1)
        def _():
            out = pltpu.make_async_copy(acc, o_ref, sem)
            out.start()
            out.wait()

    return pl.pallas_call(
        body, name=name, grid=(n_steps,), in_specs=in_specs + [ANY] * len(extra), out_specs=ANY,
        out_shape=jax.ShapeDtypeStruct(shape, F32),
        scratch_shapes=[pltpu.VMEM(shape, F32), pltpu.SemaphoreType.DMA],
        compiler_params=_params("arbitrary"),
    )(*operands, *extra)


def _dw_in(xb, dph, dpg, after):
    def step(acc, x_ref, dh_ref, dg_ref):
        xv = x_ref[...]
        for g in range(N_GROUPS):
            part = dh_ref[g] if g < 3 else dg_ref[g - 3]
            acc[:, g * GROUP:(g + 1) * GROUP] += _dot(xv, part, TN)

    return _grad_w("dw_in", (xb, dph, dpg), (D_MODEL, GROUP, GROUP), (D_MODEL, IN_COLS), step, after)


def _dw_out(cat, dpre1b):
    def step(acc, cat_ref, d_ref):
        dv = d_ref[...]
        for g in range(2):
            acc[g * GROUP:(g + 1) * GROUP, :] += _dot(cat_ref[g], dv, TN)

    return _grad_w("dw_out", (cat, dpre1b), (GROUP, D_MODEL), (D_MODEL, D_MODEL), step)


def _dw_ff1(h1b, da):
    def step(acc, h_ref, da_ref):
        hv = h_ref[...]
        for j in range(D_FF // FF_BLOCK):
            cols = slice(j * FF_BLOCK, (j + 1) * FF_BLOCK)
            acc[:, cols] += _dot(hv, da_ref[:, cols], TN)

    return _grad_w("dw_ff1", (h1b, da), (D_MODEL, D_FF), (D_MODEL, D_FF), step)


def _dw_ff2(r, dpre2b):
    def step(acc, r_ref, d_ref):
        dv = d_ref[...]
        for j in range(D_FF // FF_BLOCK):
            rows = slice(j * FF_BLOCK, (j + 1) * FF_BLOCK)
            acc[rows, :] += _dot(r_ref[:, rows], dv, TN)

    return _grad_w("dw_ff2", (r, dpre2b), (D_FF, D_MODEL), (D_FF, D_MODEL), step)


def _place():
    x, y, c = lax.axis_index("x"), lax.axis_index("y"), lax.axis_index("c")
    return x, y, c, 2 * x + y


def _other_chips(x, y):
    return [(1 - x, y), (x, 1 - y), (1 - x, 1 - y)]


def _place_shard(name, w, chip, cols_sharded, after=None):
    rows, cols = w.shape
    tr = min(rows, 256)
    nb = rows // tr
    full = (rows, cols * N_CHIPS) if cols_sharded else (rows * N_CHIPS, cols)
    out_map = (lambda i, s: (i, s[0])) if cols_sharded else (lambda i, s: (s[0] * nb + i, 0))

    def body(s_ref, w_ref, *rest):
        rest[-1][...] = w_ref[...].astype(rest[-1].dtype)

    extra = [] if after is None else [after]
    return pl.pallas_call(
        body, name=name,
        grid_spec=pltpu.PrefetchScalarGridSpec(
            num_scalar_prefetch=1, grid=(nb,),
            in_specs=[pl.BlockSpec((tr, cols), lambda i, s: (i, 0))] + [ANY] * len(extra),
            out_specs=pl.BlockSpec((tr, cols), out_map)),
        out_shape=jax.ShapeDtypeStruct(full, BF16),
        compiler_params=_params("parallel"),
    )(chip, w, *extra)


def _gather_w_in(w_in, conv_w):
    half, cs, n_p = D_MODEL // 2, IN_COLS // N_CHIPS, 3

    def body(w_alias, cv_ref, w_ref, cvf_ref, send_sems, recv_sems, local_sem):
        x, y, c, me = _place()
        sibling = (x, y, 1 - c)
        chips = _other_chips(x, y)
        blk = lambda chip, h: w_ref.at[pl.ds(h * half, half), pl.ds(chip * cs, cs)]

        def copy(k, src, dst, to):
            return pltpu.make_async_remote_copy(src_ref=src, dst_ref=dst, send_sem=send_sems.at[k],
                                                recv_sem=recv_sems.at[k], device_id=to, device_id_type=MESH)

        own_cv = pltpu.make_async_copy(cv_ref, cvf_ref.at[me], local_sem)
        own_cv.start()
        first = [copy(j, blk(me, c), blk(me, c), (px, py, c)) for j, (px, py) in enumerate(chips)]
        first += [copy(2 * n_p + j, cv_ref, cvf_ref.at[me], (px, py, c)) for j, (px, py) in enumerate(chips)]
        for cp in first:
            cp.start()
        passed = []
        for j, (px, py) in enumerate(chips):
            got = blk(2 * px + py, c)
            copy(j, got, got, (px, py, c)).wait_recv()
            passed.append(copy(n_p + j, got, got, sibling))
            passed[-1].start()
        for j, (px, py) in enumerate(chips):
            got = blk(2 * px + py, 1 - c)
            copy(n_p + j, got, got, sibling).wait_recv()
            copy(2 * n_p + j, cv_ref, cvf_ref.at[2 * px + py], (px, py, c)).wait_recv()
        for cp in first + passed:
            cp.wait_send()
        own_cv.wait()

    return pl.pallas_call(
        body, name="gather_w_in", in_specs=[ANY, ANY], out_specs=[ANY, ANY],
        out_shape=[jax.ShapeDtypeStruct(w_in.shape, w_in.dtype), jax.ShapeDtypeStruct((N_CHIPS,) + conv_w.shape, conv_w.dtype)],
        input_output_aliases={0: 0},
        scratch_shapes=[pltpu.SemaphoreType.DMA((3 * n_p,)), pltpu.SemaphoreType.DMA((3 * n_p,)), pltpu.SemaphoreType.DMA],
    )(w_in, conv_w)


HBM = pl.BlockSpec(memory_space=pltpu.HBM)
SEM = pl.BlockSpec(memory_space=pltpu.SEMAPHORE)
EFFECT = pltpu.SideEffectType.DATAFLOW_SIDE_EFFECTING


class _Split:
    def __init__(self, name, arrays, plan):
        n, n_copies = len(arrays), plan.count
        self.name, self.plan, self.n = name, plan, n

        def body(*refs):
            send_sems, recv_sems, token = refs[n], refs[n + 1], refs[-1]
            for k, (src, dst, to) in enumerate(plan(refs[:n])):
                pltpu.make_async_remote_copy(src_ref=src, dst_ref=dst, send_sem=send_sems.at[k], recv_sem=recv_sems.at[k],
                                             device_id=to, device_id_type=MESH).start()
            token[...] = jnp.zeros_like(token)

        outs = pl.pallas_call(
            body, name=name + "_start",
            out_shape=(pltpu.SemaphoreType.DMA((n_copies,)), pltpu.SemaphoreType.DMA((n_copies,)),
                       *[pltpu.HBM(a.shape, a.dtype) for a in arrays], jax.ShapeDtypeStruct((SUBLANES, LANES), F32)),
            in_specs=(HBM,) * n, out_specs=(SEM, SEM) + (HBM,) * n + (pl.BlockSpec(memory_space=pltpu.VMEM),),
            input_output_aliases={i: 2 + i for i in range(n)},
            compiler_params=pltpu.CompilerParams(has_side_effects=EFFECT),
        )(*[pltpu.with_memory_space_constraint(a, pltpu.HBM) for a in arrays])
        self.sems, self.arrays, self.token = outs[:2], outs[2:2 + n], outs[-1]

    def wait(self, after):
        n, plan = self.n, self.plan

        def body(*refs):
            send_sems, recv_sems = refs[n], refs[n + 1]
            for k, (src, dst, to) in enumerate(plan(refs[:n])):
                cp = pltpu.make_async_remote_copy(src_ref=src, dst_ref=dst, send_sem=send_sems.at[k],
                                                  recv_sem=recv_sems.at[k], device_id=to, device_id_type=MESH)
                cp.wait_send()
                cp.wait_recv()

        return pl.pallas_call(
            body, name=self.name + "_wait", out_shape=tuple(pltpu.HBM(a.shape, a.dtype) for a in self.arrays),
            in_specs=(HBM,) * n + (SEM, SEM, ANY), out_specs=(HBM,) * n, input_output_aliases={i: i for i in range(n)},
            compiler_params=pltpu.CompilerParams(has_side_effects=EFFECT),
        )(*self.arrays, *self.sems, after)


COLS_SHARDED = (True, False, True, False)
HALF_SHAPES = [(D_MODEL // 2, IN_COLS), (D_MODEL, D_MODEL // 2), (D_MODEL // 2, D_FF), (D_FF, D_MODEL // 2)]
PIECE_SHAPES = [(D_MODEL // 2, IN_COLS // N_CHIPS), (D_MODEL // N_CHIPS, D_MODEL // 2),
                (D_MODEL // 2, D_FF // N_CHIPS), (D_FF // N_CHIPS, D_MODEL // 2)]


def _shard_view(kind, ref, chip):
    if COLS_SHARDED[kind]:
        n = ref.shape[1] // N_CHIPS
        return ref.at[:, pl.ds(chip * n, n)]
    n = ref.shape[0] // N_CHIPS
    return ref.at[pl.ds(chip * n, n), :]


def _half_view(kind, ref, h):
    if COLS_SHARDED[kind]:
        n = ref.shape[0] // 2
        return ref.at[pl.ds(h * n, n), :]
    n = ref.shape[1] // 2
    return ref.at[:, pl.ds(h * n, n)]


def _plan(count):
    def mark(fn):
        fn.count = count
        return fn
    return mark


def _shard_half_view(kind, ref, chip, h):
    if COLS_SHARDED[kind]:
        m, n = ref.shape[0] // 2, ref.shape[1] // N_CHIPS
        return ref.at[pl.ds(h * m, m), pl.ds(chip * n, n)]
    m = ref.shape[0] // N_CHIPS // 2
    return ref.at[pl.ds((2 * chip + h) * m, m), :]


def _gather_over_ici(kinds, weights):
    @_plan(3 * len(kinds))
    def plan(refs):
        x, y, c, me = _place()
        mine = [_shard_half_view(kind, ref, me, c) for kind, ref in zip(kinds, refs)]
        return [(v, v, (px, py, c)) for v in mine for px, py in _other_chips(x, y)]

    return _Split("gather_ici_" + "".join(map(str, kinds)), tuple(weights), plan)


def _gather_over_d2d(kinds, weights):
    @_plan(3 * len(kinds))
    def plan(refs):
        x, y, c, _ = _place()
        got = [_shard_half_view(kind, ref, 2 * px + py, c) for kind, ref in zip(kinds, refs)
               for px, py in _other_chips(x, y)]
        return [(v, v, (x, y, 1 - c)) for v in got]

    return _Split("gather_d2d_" + "".join(map(str, kinds)), tuple(weights), plan)


def _swap_halves(kinds, grads):
    @_plan(len(kinds))
    def plan(refs):
        x, y, c, _ = _place()
        return [(_half_view(kind, g, 1 - c), land, (x, y, 1 - c))
                for kind, g, land in zip(kinds, refs[:len(kinds)], refs[len(kinds):])]

    lands = [lax.empty(HALF_SHAPES[kind], F32) for kind in kinds]
    return _Split("swap_halves_" + "".join(map(str, kinds)), (*grads, *lands), plan)


def _add_half(name, g, recv, core, rows_split):
    shape = recv.shape
    tr = min(shape[0], 128 if rows_split else 256)
    nb = shape[0] // tr

    def body(c_ref, g_ref, r_ref, o_ref):
        o_ref[...] = (g_ref[...] + r_ref[...]).astype(o_ref.dtype)

    g_map = (lambda i, c_ref: (c_ref[0] * nb + i, 0)) if rows_split else (lambda i, c_ref: (i, c_ref[0]))
    blk = pl.BlockSpec((tr, shape[1]), lambda i, c_ref: (i, 0))
    return pl.pallas_call(
        body, name=name,
        grid_spec=pltpu.PrefetchScalarGridSpec(
            num_scalar_prefetch=1, grid=(nb,),
            in_specs=[pl.BlockSpec((tr, shape[1]), g_map), blk], out_specs=blk),
        out_shape=jax.ShapeDtypeStruct(shape, BF16),
        compiler_params=_params("parallel"),
    )(core, g, recv)


def _exchange_pieces(kinds, halves):
    n_p = N_CHIPS - 1

    @_plan(n_p * len(kinds))
    def plan(refs):
        x, y, c, _ = _place()
        return [(_shard_view(kind, half, 2 * px + py), land.at[j], (px, py, c))
                for j, (px, py) in enumerate(_other_chips(x, y))
                for kind, half, land in zip(kinds, refs[:len(kinds)], refs[len(kinds):])]

    lands = [lax.empty((n_p,) + PIECE_SHAPES[kind], BF16) for kind in kinds]
    return _Split("exchange_pieces_" + "".join(map(str, kinds)), (*halves, *lands), plan)


def _sum_pieces(name, half, slots, place, rows_split):
    n_p, rows, cols = slots.shape
    tr = min(rows, 256)
    nb = rows // tr
    if rows_split:
        own_map = lambda i, s: (i, s[0])
        out_map = lambda i, s: (s[1] * nb + i, 0)
        shard = (2 * rows, cols)
    else:
        own_map = lambda i, s: (s[0] * nb + i, 0)
        out_map = lambda i, s: (i, s[1])
        shard = (rows, 2 * cols)

    def body(s_ref, own_ref, slot_ref, o_ref):
        total = own_ref[...].astype(F32)
        for j in range(n_p):
            total = total + slot_ref[j].astype(F32)
        o_ref[...] = total

    return pl.pallas_call(
        body, name=name,
        grid_spec=pltpu.PrefetchScalarGridSpec(
            num_scalar_prefetch=1, grid=(nb,),
            in_specs=[pl.BlockSpec((tr, cols), own_map), pl.BlockSpec((n_p, tr, cols), lambda i, s: (0, i, 0))],
            out_specs=pl.BlockSpec((tr, cols), out_map)),
        out_shape=jax.ShapeDtypeStruct(shard, F32),
        compiler_params=_params("parallel"),
    )(place, half, slots)


def _join_halves(kinds, shards):
    @_plan(len(kinds))
    def plan(refs):
        x, y, c, _ = _place()
        return [(_half_view(kind, g, c), _half_view(kind, g, c), (x, y, 1 - c)) for kind, g in zip(kinds, refs)]

    return _Split("join_halves_" + "".join(map(str, kinds)), tuple(shards), plan)


N_DEV = 8


def _share_small(pack, after):
    @_plan(N_DEV - 1)
    def plan(refs):
        x, y, c, _ = _place()
        me = 4 * x + 2 * y + c
        peers = [((1 - x) if m & 4 else x, (1 - y) if m & 2 else y, (1 - c) if m & 1 else c) for m in range(1, N_DEV)]
        return [(refs[0], refs[1].at[me], peer) for peer in peers]

    return _Split("share_small", (pack, lax.empty((N_DEV,) + pack.shape, F32), after), plan)


def _sum_shared(pack, land, device):
    def body(d_ref, p_ref, l_ref, o_ref):
        me = d_ref[0]
        total = jnp.where(me == 0, p_ref[...], l_ref[0])
        for d in range(1, N_DEV):
            total = total + jnp.where(me == d, p_ref[...], l_ref[d])
        o_ref[...] = total

    return pl.pallas_call(
        body, name="sum_shared",
        grid_spec=pltpu.PrefetchScalarGridSpec(
            num_scalar_prefetch=1, grid=(1,),
            in_specs=[pl.BlockSpec(pack.shape, lambda i, d: (0, 0)), pl.BlockSpec(land.shape, lambda i, d: (0, 0, 0))],
            out_specs=pl.BlockSpec(pack.shape, lambda i, d: (0, 0))),
        out_shape=jax.ShapeDtypeStruct(pack.shape, F32),
    )(device, pack, land)


def _adamw(name, w, g, m, v, after=None):
    rows, cols = w.shape
    tr = min(rows, 256)
    extra = [] if after is None else [after]

    def body(w_ref, g_ref, m_ref, v_ref, *rest):
        d_ref, nm_ref, nv_ref = rest[-3:]
        gv = g_ref[...]
        nm = ADAM_B1 * m_ref[...] + (1.0 - ADAM_B1) * gv
        nv = ADAM_B2 * v_ref[...] + (1.0 - ADAM_B2) * jnp.square(gv)
        m_hat = nm * (1.0 / (1.0 - ADAM_B1 ** ADAM_STEP))
        v_hat = nv * (1.0 / (1.0 - ADAM_B2 ** ADAM_STEP))
        d_ref[...] = -ADAM_LR * (m_hat / (jnp.sqrt(v_hat) + ADAM_EPS) + ADAM_WD * w_ref[...])
        nm_ref[...] = nm
        nv_ref[...] = nv

    blk = pl.BlockSpec((tr, cols), lambda i: (i, 0))
    return pl.pallas_call(
        body, name=name, grid=(rows // tr,), in_specs=[blk] * 4 + [ANY] * len(extra), out_specs=[blk] * 3,
        out_shape=[jax.ShapeDtypeStruct(w.shape, F32)] * 3,
        compiler_params=_params("parallel"),
    )(w, g, m, v, *extra)


def kernel(x, w_in, lb_logits, gate_norm_w, conv_w, w_out, ln1_g, ln1_b, w_ff1, w_ff2, ln2_g, ln2_b, loss_target, m_w_in, m_lb_logits, m_gate_norm_w, m_conv_w, m_w_out, m_ln1_g, m_ln1_b, m_w_ff1, m_w_ff2, m_ln2_g, m_ln2_b, v_w_in, v_lb_logits, v_gate_norm_w, v_conv_w, v_w_out, v_ln1_g, v_ln1_b, v_w_ff1, v_w_ff2, v_ln2_g, v_ln2_b):
    xs, tgt = x[0], loss_target[0]
    chip = 2 * lax.axis_index("x") + lax.axis_index("y")
    core = lax.axis_index("c").astype(jnp.int32).reshape(1)
    chip1 = chip.astype(jnp.int32).reshape(1)
    place = jnp.concatenate([chip1, core])

    wb_in, cv4 = _gather_w_in(_place_shard("place_w_in", w_in[0], chip1, True), conv_w[0])
    conv_full = cv4.transpose(1, 0, 2).reshape(3, CONV_WIDTH)
    ici_out = _gather_over_ici((1,), (_place_shard("place_w_out", w_out[0], chip1, False, after=wb_in),))
    ici_ff = _gather_over_ici((2, 3), (_place_shard("place_w_ff1", w_ff1[0], chip1, True, after=ici_out.token),
                                       _place_shard("place_w_ff2", w_ff2[0], chip1, False, after=ici_out.token)))

    proj, xb = _in_proj(xs, wb_in, ici_ff.token)
    o, states = _hgrn_fwd(proj, lb_logits)
    d2d_out = _gather_over_d2d((1,), ici_out.wait(o))
    cat = _gate_fwd(proj, o, gate_norm_w, conv_full, d2d_out.token)
    wb_out, = d2d_out.wait(cat)
    d2d_ff = _gather_over_d2d((2, 3), ici_ff.wait(cat))
    xhat1, h1b, rstd1 = _out_ln1(cat, wb_out, xs, ln1_g, ln1_b, d2d_ff.token)
    wb_ff1, wb_ff2 = d2d_ff.wait(xhat1)
    r, dpre2, dpre2b, g_ln2_g, g_ln2_b, loss8 = _mlp_fwd(xhat1, ln1_g, ln1_b, wb_ff1, wb_ff2, ln2_g, ln2_b, tgt)

    names = ("w_in", "w_out", "w_ff1", "w_ff2")

    def add_halves(kinds, grads, lands):
        return [_add_half("add_half_" + names[k], g, ld, core, COLS_SHARDED[k]) for k, g, ld in zip(kinds, grads, lands)]

    def sum_pieces(kinds, halves, lands):
        return [_sum_pieces("sum_pieces_" + names[k], h, ld, place, COLS_SHARDED[k]) for k, h, ld in zip(kinds, halves, lands)]

    da, dpre1, dpre1b, g_ln1_g, g_ln1_b = _mlp_bwd(dpre2, r, wb_ff1, wb_ff2, xhat1, rstd1, ln1_g)
    dcat = _out_bwd(dpre1b, wb_out)
    early = (1, 2, 3)
    swap = _swap_halves(early, (_dw_out(cat, dpre1b), _dw_ff1(h1b, da), _dw_ff2(r, dpre2b)))
    do, dpg, g_gnw, g_conv = _gate_bwd(dcat, o, proj, gate_norm_w, conv_full, swap.token)
    swapped = swap.wait(do)
    exch = _exchange_pieces(early, add_halves(early, swapped[:3], swapped[3:]))
    dph, g_lbl = _hgrn_bwd(proj, do, states, lb_logits, exch.token)
    g_in_local = _dw_in(xb, dph, dpg, dph)

    late = (0,)
    swap = _swap_halves(late, (g_in_local,))
    grad_x = _in_bwd(dph, dpg, wb_in, dpre1, swap.token)
    exchanged = exch.wait(grad_x)
    pack = jnp.concatenate([
        g_ln1_g, g_ln1_b, g_ln2_g, g_ln2_b,
        jnp.concatenate([g_lbl[0:1], g_lbl[1:2]], axis=1),
        jnp.concatenate([g_gnw, g_conv[0:1]], axis=1),
        jnp.concatenate([g_conv[1:2], g_conv[2:3]], axis=1),
        jnp.concatenate([loss8[0:1], jnp.zeros((1, D_MODEL - LANES), F32)], axis=1)], axis=0)
    small = _share_small(pack, exchanged[0])
    swapped = swap.wait(small.token)
    exch = _exchange_pieces(late, add_halves(late, swapped[:1], swapped[1:]))
    shared = small.wait(exch.token)
    join = _join_halves(early, sum_pieces(early, (shared[2], *exchanged[1:3]), exchanged[3:]))
    tot = _sum_shared(shared[0], shared[1], 2 * chip1 + core)
    loss = tot[7, 0]
    half = D_MODEL // 2
    g_lb_logits = jnp.concatenate([tot[4:5, :half], tot[4:5, half:]], axis=0)
    g_gate_norm_w = tot[5:6, :half]
    g_conv_full = jnp.concatenate([tot[5:6, half:], tot[6:7, :half], tot[6:7, half:]], axis=0)
    g_conv_w = lax.dynamic_slice(g_conv_full, (0, chip * LANES), (3, LANES))
    g_w_out, g_w_ff1, g_w_ff2 = join.wait(tot)
    d_ff1, nm_ff1, nv_ff1 = _adamw("adamw_w_ff1", w_ff1[0], g_w_ff1, m_w_ff1[0], v_w_ff1[0], exch.token)
    d_ff2, nm_ff2, nv_ff2 = _adamw("adamw_w_ff2", w_ff2[0], g_w_ff2, m_w_ff2[0], v_w_ff2[0], d_ff1)
    d_out, nm_out, nv_out = _adamw("adamw_w_out", w_out[0], g_w_out, m_w_out[0], v_w_out[0], d_ff2)

    def small_pack(lbl, gnw, cv, l1g, l1b, l2g, l2b):
        pad = jnp.zeros((1, D_MODEL - 3 * LANES), F32)
        return jnp.concatenate([
            l1g, l1b, l2g, l2b, jnp.concatenate([lbl[0:1], lbl[1:2]], axis=1),
            jnp.concatenate([gnw, jnp.zeros((1, half), F32)], axis=1),
            jnp.concatenate([cv[0:1], cv[1:2], cv[2:3], pad], axis=1), jnp.zeros((1, D_MODEL), F32)], axis=0)

    w_s = small_pack(lb_logits, gate_norm_w, conv_w[0], ln1_g, ln1_b, ln2_g, ln2_b)
    g_s = small_pack(g_lb_logits, g_gate_norm_w, g_conv_w, tot[0:1], tot[1:2], tot[2:3], tot[3:4])
    m_s = small_pack(m_lb_logits, m_gate_norm_w, m_conv_w[0], m_ln1_g, m_ln1_b, m_ln2_g, m_ln2_b)
    v_s = small_pack(v_lb_logits, v_gate_norm_w, v_conv_w[0], v_ln1_g, v_ln1_b, v_ln2_g, v_ln2_b)
    d_s, nm_s, nv_s = _adamw("adamw_small", w_s, g_s, m_s, v_s, d_out)
    exchanged = exch.wait(d_s)
    join = _join_halves(late, sum_pieces(late, exchanged[:1], exchanged[1:]))
    g_w_in, = join.wait(join.token)
    d_in, nm_in, nv_in = _adamw("adamw_w_in", w_in[0], g_w_in, m_w_in[0], v_w_in[0])

    def unpack(p):
        lbl = jnp.concatenate([p[4:5, :half], p[4:5, half:]], axis=0)
        cv = jnp.concatenate([p[6:7, 0:LANES], p[6:7, LANES:2 * LANES], p[6:7, 2 * LANES:3 * LANES]], axis=0)
        return dict(lb_logits=lbl, gate_norm_w=p[5:6, :half], conv_w=cv[None], ln1_g=p[0:1], ln1_b=p[1:2],
                    ln2_g=p[2:3], ln2_b=p[3:4])

    order = ("w_in", "lb_logits", "gate_norm_w", "conv_w", "w_out", "ln1_g", "ln1_b", "w_ff1", "w_ff2", "ln2_g", "ln2_b")
    grad = dict(unpack(g_s), w_in=g_w_in[None], w_out=g_w_out[None], w_ff1=g_w_ff1[None], w_ff2=g_w_ff2[None])
    delta = dict(unpack(d_s), w_in=d_in[None], w_out=d_out[None], w_ff1=d_ff1[None], w_ff2=d_ff2[None])
    new_m = dict(unpack(nm_s), w_in=nm_in[None], w_out=nm_out[None], w_ff1=nm_ff1[None], w_ff2=nm_ff2[None])
    new_v = dict(unpack(nv_s), w_in=nv_in[None], w_out=nv_out[None], w_ff1=nv_ff1[None], w_ff2=nv_ff2[None])
    return (loss, grad_x[None], *[grad[n] for n in order], *[delta[n] for n in order],
            *[new_m[n] for n in order], *[new_v[n] for n in order])
```

```python
import jax
import jax.numpy as jnp
from jax import lax
from jax.experimental import pallas as pl
from jax.experimental.pallas import tpu as pltpu

F32 = jnp.float32
BF16 = jnp.bfloat16
MXU_DTYPE = jnp.bfloat16

D_MODEL = 1024
HGRN_WIDTH = 512
HEAD_DIM = 128
N_HEADS = 4
CONV_WIDTH = 512
CHUNK = 64
D_FF = 4096
IN_COLS = 3584
GROUP = 512
N_GROUPS = IN_COLS // GROUP
ALPHA = 2.0 ** 0.25
EPS = 1e-5
N_CHIPS = 4
ADAM_LR, ADAM_B1, ADAM_B2, ADAM_EPS, ADAM_WD, ADAM_STEP = 0.001, 0.9, 0.999, 1e-08, 0.01, 10

LANES = 128
SUBLANES = 8
VMEM_LIMIT = 56 * 1024 * 1024
FF_BLOCK = 1024
N_FF = D_FF // FF_BLOCK

NN = (((1,), (0,)), ((), ()))
NT = (((1,), (1,)), ((), ()))
TN = (((0,), (0,)), ((), ()))
MESH = pl.DeviceIdType.MESH
ANY = pl.BlockSpec(memory_space=pl.ANY)


def _dot(a, b, dims):
    return lax.dot_general(a.astype(MXU_DTYPE), b.astype(MXU_DTYPE), dims, preferred_element_type=F32)


def _dot_exact(ones, v):
    ones = ones.astype(jnp.bfloat16)
    hi = v.astype(jnp.bfloat16)
    rest = v - hi.astype(F32)
    mid = rest.astype(jnp.bfloat16)
    low = (rest - mid.astype(F32)).astype(jnp.bfloat16)
    return sum(lax.dot_general(ones, part, NN, preferred_element_type=F32) for part in (hi, mid, low))


def _params(*sem):
    return pltpu.CompilerParams(dimension_semantics=sem, vmem_limit_bytes=VMEM_LIMIT)


def _resident(shape):
    return pl.BlockSpec(shape, lambda *_: (0,) * len(shape), pipeline_mode=pl.Buffered(1))


def _sigmoid(v):
    return 1.0 / (1.0 + jnp.exp(-v))


def _lower_bound(lbl):
    m = jnp.max(lbl, axis=0, keepdims=True)
    e = jnp.exp(lbl - m)
    s = e / jnp.sum(e, axis=0, keepdims=True)
    return s[0:1, :], s[1:2, :]


def _heads(v):
    return [v[:, h * HEAD_DIM:(h + 1) * HEAD_DIM] for h in range(N_HEADS)]


def _per_head(fn, *arrays):
    return jnp.concatenate([fn(*parts) for parts in zip(*map(_heads, arrays))], axis=1)


def _in_proj(x, w_in, after):
    t = x.shape[0]
    tm = min(t, 512)

    def body(x_ref, w_ref, after_ref, o_ref, xb_ref):
        xb = x_ref[...].astype(xb_ref.dtype)
        xb_ref[...] = xb
        for g in range(N_GROUPS):
            o_ref[g] = _dot(xb, w_ref[:, g * GROUP:(g + 1) * GROUP], NN)

    return pl.pallas_call(
        body, name="in_proj", grid=(t // tm,),
        in_specs=[pl.BlockSpec((tm, D_MODEL), lambda i: (i, 0)), _resident((D_MODEL, IN_COLS)), ANY],
        out_specs=[pl.BlockSpec((N_GROUPS, tm, GROUP), lambda i: (0, i, 0)), pl.BlockSpec((tm, D_MODEL), lambda i: (i, 0))],
        out_shape=[jax.ShapeDtypeStruct((N_GROUPS, t, GROUP), F32), jax.ShapeDtypeStruct((t, D_MODEL), BF16)],
        compiler_params=_params("parallel"),
    )(x, w_in, after)


def _gates(fp, lb):
    sig = _sigmoid(fp)
    f = lb + (1.0 - lb) * sig
    return sig, f, jnp.log(f), 1.0 - f


def _chunk_masks():
    row = lax.broadcasted_iota(jnp.int32, (CHUNK, CHUNK), 0)
    col = lax.broadcasted_iota(jnp.int32, (CHUNK, CHUNK), 1)
    return row >= col, row <= col


def _hgrn_fwd(proj, lb_logits):
    t = proj.shape[1]
    tb = min(t, 512)
    ncb = tb // CHUNK

    def body(q_ref, f_ref, v_ref, lbl_ref, o_ref, st_ref, s_scr):
        @pl.when(pl.program_id(0) == 0)
        def _():
            s_scr[...] = jnp.zeros_like(s_scr)

        lb, _ = _lower_bound(lbl_ref[...])
        causal, _ = _chunk_masks()

        every = range(ncb)
        rows = [slice(c * CHUNK, (c + 1) * CHUNK) for c in every]
        q, v = [q_ref[r, :] for r in rows], [v_ref[r, :] for r in rows]
        gates = [_gates(f_ref[r, :], lb) for r in rows]
        k = [gt[3] for gt in gates]
        b = [_dot_exact(causal, gt[2]) for gt in gates]
        mid, last = [x[CHUNK // 2:CHUNK // 2 + 1, :] for x in b], [x[CHUNK - 1:CHUNK, :] for x in b]
        qt = [q[c] * jnp.exp(b[c] - mid[c]) for c in every]
        kt = [k[c] * jnp.exp(mid[c] - b[c]) for c in every]
        qi = [q[c] * jnp.exp(b[c]) for c in every]
        ks = [k[c] * jnp.exp(last[c] - b[c]) for c in every]
        dec = [jnp.exp(x) for x in last]
        scores = [[jnp.where(causal, _dot(a, b_, NT), 0.0) for a, b_ in zip(_heads(qt[c]), _heads(kt[c]))] for c in every]
        intra = [[_dot(s, v_h, NN) for s, v_h in zip(scores[c], _heads(v[c]))] for c in every]
        update = [_per_head(lambda v_h, ks_h: _dot(v_h, ks_h, TN), v[c], ks[c]) for c in every]

        st = s_scr[...]
        states = []
        for c in every:
            states.append(st)
            st_ref[c] = st
            st = dec[c] * st + update[c]
        s_scr[...] = st

        o_ref[...] = jnp.concatenate(
            [jnp.concatenate([i_h + _dot(qi_h, st_h, NT) for i_h, qi_h, st_h in
                              zip(intra[c], _heads(qi[c]), _heads(states[c]))], axis=1) for c in every], axis=0)

    grp = lambda g: pl.BlockSpec((None, tb, GROUP), lambda i: (g, i, 0))
    return pl.pallas_call(
        body, name="hgrn_fwd", grid=(t // tb,),
        in_specs=[grp(0), grp(1), grp(2), pl.BlockSpec((2, HGRN_WIDTH), lambda i: (0, 0))],
        out_specs=[pl.BlockSpec((tb, HGRN_WIDTH), lambda i: (i, 0)),
                   pl.BlockSpec((ncb, HEAD_DIM, HGRN_WIDTH), lambda i: (i, 0, 0))],
        out_shape=[jax.ShapeDtypeStruct((t, HGRN_WIDTH), F32),
                   jax.ShapeDtypeStruct((t // CHUNK, HEAD_DIM, HGRN_WIDTH), F32)],
        scratch_shapes=[pltpu.VMEM((HEAD_DIM, HGRN_WIDTH), F32)],
        compiler_params=_params("arbitrary"),
    )(proj, proj, proj, lb_logits)


def _conv_taps(z, halo, zbuf, tb):
    zbuf[0:SUBLANES, :] = halo
    zbuf[SUBLANES:SUBLANES + tb, :] = z
    return zbuf[SUBLANES - 1:SUBLANES - 1 + tb, :], zbuf[SUBLANES - 2:SUBLANES - 2 + tb, :]


def _gate_fwd(proj, o, gate_norm_w, conv_w, after):
    t = proj.shape[1]
    tb = min(t, 512)
    hb = tb // SUBLANES

    def body(o_ref, og_ref, gnw_ref, b_ref, c_ref, u_ref, ch_ref, uh_ref, cw_ref, after_ref, cat_ref, zbuf):
        i = pl.program_id(0)
        og = og_ref[...]
        on = _per_head(lambda o_h: o_h * lax.rsqrt(jnp.mean(o_h * o_h, axis=-1, keepdims=True) + EPS), o_ref[...])
        cat_ref[0] = (on * gnw_ref[...] * (og * _sigmoid(og))).astype(cat_ref.dtype)
        z = c_ref[...] * u_ref[...]
        halo = jnp.where(i > 0, ch_ref[...] * uh_ref[...], 0.0)
        z1, z2 = _conv_taps(z, halo, zbuf, tb)
        cw = cw_ref[...]
        yc = cw[2:3, :] * z + cw[1:2, :] * z1 + cw[0:1, :] * z2
        cat_ref[1] = (b_ref[...] * yc).astype(cat_ref.dtype)

    grp = lambda g: pl.BlockSpec((None, tb, GROUP), lambda i: (g, i, 0))
    prev = lambda g: pl.BlockSpec((None, SUBLANES, GROUP), lambda i: (g, jnp.maximum(i * hb - 1, 0), 0))
    vec = lambda r: pl.BlockSpec((r, GROUP), lambda i: (0, 0))
    return pl.pallas_call(
        body, name="gate_fwd", grid=(t // tb,),
        in_specs=[pl.BlockSpec((tb, GROUP), lambda i: (i, 0)), grp(3), vec(1), grp(4), grp(5), grp(6), prev(5), prev(6),
                  vec(3), ANY],
        out_specs=pl.BlockSpec((2, tb, GROUP), lambda i: (0, i, 0)),
        out_shape=jax.ShapeDtypeStruct((2, t, HGRN_WIDTH), BF16),
        scratch_shapes=[pltpu.VMEM((tb + SUBLANES, GROUP), F32)],
        compiler_params=_params("parallel"),
    )(o, proj, gate_norm_w, proj, proj, proj, proj, proj, conv_w, after)


def _out_ln1(cat, w_out, x, g1, b1, after):
    t = x.shape[0]
    tm = min(t, 512)

    def body(cat_ref, w_ref, x_ref, g_ref, b_ref, after_ref, xhat_ref, h1_ref, rstd_ref):
        mix = _dot(cat_ref[0], w_ref[0:GROUP, :], NN) + _dot(cat_ref[1], w_ref[GROUP:2 * GROUP, :], NN)
        pre = ALPHA * x_ref[...] + mix
        xc = pre - jnp.mean(pre, axis=-1, keepdims=True)
        rstd = lax.rsqrt(jnp.mean(xc * xc, axis=-1, keepdims=True) + EPS)
        xhat = xc * rstd
        xhat_ref[...] = xhat
        h1_ref[...] = (xhat * g_ref[...] + b_ref[...]).astype(h1_ref.dtype)
        rstd_ref[...] = rstd

    row = pl.BlockSpec((tm, D_MODEL), lambda i: (i, 0))
    vec = pl.BlockSpec((1, D_MODEL), lambda i: (0, 0))
    return pl.pallas_call(
        body, name="out_ln1", grid=(t // tm,),
        in_specs=[pl.BlockSpec((2, tm, GROUP), lambda i: (0, i, 0)), _resident((D_MODEL, D_MODEL)), row, vec, vec, ANY],
        out_specs=[row, row, pl.BlockSpec((tm, 1), lambda i: (i, 0))],
        out_shape=[jax.ShapeDtypeStruct((t, D_MODEL), F32), jax.ShapeDtypeStruct((t, D_MODEL), BF16),
                   jax.ShapeDtypeStruct((t, 1), F32)],
        compiler_params=_params("parallel"),
    )(cat, w_out, x, g1, b1, after)


def _ln_bwd(dy, xhat, rstd, g):
    dxhat = dy * g
    m1 = jnp.mean(dxhat, axis=-1, keepdims=True)
    m2 = jnp.mean(dxhat * xhat, axis=-1, keepdims=True)
    return rstd * (dxhat - m1 - xhat * m2)


def _mlp_fwd(xhat1, g1, b1, w_ff1, w_ff2, g2, b2, target):
    t = xhat1.shape[0]
    tm = min(t, 256)

    def body(xh_ref, g1_ref, b1_ref, w1_ref, w2_ref, g2_ref, b2_ref, tg_ref,
             r_ref, dpre_ref, dpreb_ref, dg_ref, db_ref, loss_ref):
        @pl.when(pl.program_id(0) == 0)
        def _():
            dg_ref[...] = jnp.zeros_like(dg_ref)
            db_ref[...] = jnp.zeros_like(db_ref)
            loss_ref[...] = jnp.zeros_like(loss_ref)

        h1 = xh_ref[...] * g1_ref[...] + b1_ref[...]
        h1b = h1.astype(MXU_DTYPE)
        mlp = jnp.zeros((tm, D_MODEL), F32)
        for j in range(N_FF):
            cols = slice(j * FF_BLOCK, (j + 1) * FF_BLOCK)
            r = jnp.square(jnp.maximum(_dot(h1b, w1_ref[:, cols], NN), 0.0)).astype(r_ref.dtype)
            r_ref[:, cols] = r
            mlp = mlp + _dot(r, w2_ref[cols, :], NN)
        pre = ALPHA * h1 + mlp
        xc = pre - jnp.mean(pre, axis=-1, keepdims=True)
        rstd = lax.rsqrt(jnp.mean(xc * xc, axis=-1, keepdims=True) + EPS)
        xhat = xc * rstd
        err = xhat * g2_ref[...] + b2_ref[...] - tg_ref[...]
        loss_ref[...] += 0.5 * jnp.sum(jnp.mean(err * err, axis=-1, keepdims=True))
        dy = err * (1.0 / D_MODEL)
        dg_ref[...] += jnp.sum(dy * xhat, axis=0, keepdims=True)
        db_ref[...] += jnp.sum(dy, axis=0, keepdims=True)
        dpre = _ln_bwd(dy, xhat, rstd, g2_ref[...])
        dpre_ref[...] = dpre
        dpreb_ref[...] = dpre.astype(dpreb_ref.dtype)

    row = pl.BlockSpec((tm, D_MODEL), lambda i: (i, 0))
    vec = pl.BlockSpec((1, D_MODEL), lambda i: (0, 0))
    return pl.pallas_call(
        body, name="mlp_fwd", grid=(t // tm,),
        in_specs=[row, vec, vec, _resident((D_MODEL, D_FF)), _resident((D_FF, D_MODEL)), vec, vec, row],
        out_specs=[pl.BlockSpec((tm, D_FF), lambda i: (i, 0)), row, row, vec, vec,
                   pl.BlockSpec((SUBLANES, LANES), lambda i: (0, 0))],
        out_shape=[jax.ShapeDtypeStruct((t, D_FF), BF16), jax.ShapeDtypeStruct((t, D_MODEL), F32),
                   jax.ShapeDtypeStruct((t, D_MODEL), BF16), jax.ShapeDtypeStruct((1, D_MODEL), F32),
                   jax.ShapeDtypeStruct((1, D_MODEL), F32), jax.ShapeDtypeStruct((SUBLANES, LANES), F32)],
        compiler_params=_params("arbitrary"),
    )(xhat1, g1, b1, w_ff1, w_ff2, g2, b2, target)


def _mlp_bwd(dpre2, r, w_ff1, w_ff2, xhat1, rstd1, g1):
    t = r.shape[0]
    tm = min(t, 256)

    def body(dp2_ref, r_ref, w1_ref, w2_ref, xh_ref, rs_ref, g_ref, da_ref, dpre_ref, dpreb_ref, dg_ref, db_ref):
        @pl.when(pl.program_id(0) == 0)
        def _():
            dg_ref[...] = jnp.zeros_like(dg_ref)
            db_ref[...] = jnp.zeros_like(db_ref)

        dp2 = dp2_ref[...]
        dp2b = dp2.astype(MXU_DTYPE)
        back = jnp.zeros((tm, D_MODEL), F32)
        for j in range(N_FF):
            cols = slice(j * FF_BLOCK, (j + 1) * FF_BLOCK)
            dr = _dot(dp2b, w2_ref[cols, :], NT)
            da = (dr * (2.0 * jnp.sqrt(r_ref[:, cols].astype(F32)))).astype(da_ref.dtype)
            da_ref[:, cols] = da
            back = back + _dot(da, w1_ref[:, cols], NT)
        dh1 = ALPHA * dp2 + back
        xhat = xh_ref[...]
        dg_ref[...] += jnp.sum(dh1 * xhat, axis=0, keepdims=True)
        db_ref[...] += jnp.sum(dh1, axis=0, keepdims=True)
        dpre = _ln_bwd(dh1, xhat, rs_ref[...], g_ref[...])
        dpre_ref[...] = dpre
        dpreb_ref[...] = dpre.astype(dpreb_ref.dtype)

    row = pl.BlockSpec((tm, D_MODEL), lambda i: (i, 0))
    wide = pl.BlockSpec((tm, D_FF), lambda i: (i, 0))
    vec = pl.BlockSpec((1, D_MODEL), lambda i: (0, 0))
    return pl.pallas_call(
        body, name="mlp_bwd", grid=(t // tm,),
        in_specs=[row, wide, _resident((D_MODEL, D_FF)), _resident((D_FF, D_MODEL)), row,
                  pl.BlockSpec((tm, 1), lambda i: (i, 0)), vec],
        out_specs=[wide, row, row, vec, vec],
        out_shape=[jax.ShapeDtypeStruct((t, D_FF), BF16), jax.ShapeDtypeStruct((t, D_MODEL), F32),
                   jax.ShapeDtypeStruct((t, D_MODEL), BF16), jax.ShapeDtypeStruct((1, D_MODEL), F32),
                   jax.ShapeDtypeStruct((1, D_MODEL), F32)],
        compiler_params=_params("arbitrary"),
    )(dpre2, r, w_ff1, w_ff2, xhat1, rstd1, g1)


def _out_bwd(dpre1b, w_out):
    t = dpre1b.shape[0]
    tm = min(t, 512)

    def body(d_ref, w_ref, o_ref):
        o_ref[...] = _dot(d_ref[...], w_ref[...], NT)

    return pl.pallas_call(
        body, name="out_bwd", grid=(t // tm,),
        in_specs=[pl.BlockSpec((tm, D_MODEL), lambda i: (i, 0)), _resident((D_MODEL, D_MODEL))],
        out_specs=pl.BlockSpec((tm, D_MODEL), lambda i: (i, 0)),
        out_shape=jax.ShapeDtypeStruct((t, D_MODEL), F32),
        compiler_params=_params("parallel"),
    )(dpre1b, w_out)


def _gate_bwd(dcat, o, proj, gate_norm_w, conv_w, after):
    t = proj.shape[1]
    tb = min(t, 512)
    hb = tb // SUBLANES
    nblk = t // tb

    def body(do2_ref, dy_ref, dyn_ref, o_ref, og_ref, gnw_ref, b_ref, bn_ref, c_ref, u_ref, ch_ref, uh_ref, cw_ref,
             after_ref, do_ref, dp_ref, dgnw_ref, dcw_ref, zbuf, dbuf):
        i = pl.program_id(0)

        @pl.when(i == 0)
        def _():
            dgnw_ref[...] = jnp.zeros_like(dgnw_ref)
            dcw_ref[...] = jnp.zeros_like(dcw_ref)

        ov, og, gnw, do2 = o_ref[...], og_ref[...], gnw_ref[...], do2_ref[...]
        rs = _per_head(lambda o_h: jnp.broadcast_to(lax.rsqrt(jnp.mean(o_h * o_h, axis=-1, keepdims=True) + EPS),
                                                    o_h.shape), ov)
        on = ov * rs
        sg = _sigmoid(og)
        sil = og * sg
        don = do2 * gnw * sil
        dgnw_ref[...] += jnp.sum(do2 * on * sil, axis=0, keepdims=True)
        dp_ref[0] = (do2 * on * gnw * (sg * (1.0 + og * (1.0 - sg)))).astype(dp_ref.dtype)
        do_ref[...] = rs * (don - on * _per_head(
            lambda p_h: jnp.broadcast_to(jnp.mean(p_h, axis=-1, keepdims=True), p_h.shape), don * on))

        bg, cg, u, dy = b_ref[...], c_ref[...], u_ref[...], dy_ref[...]
        z = cg * u
        halo = jnp.where(i > 0, ch_ref[...] * uh_ref[...], 0.0)
        z1, z2 = _conv_taps(z, halo, zbuf, tb)
        cw = cw_ref[...]
        yc = cw[2:3, :] * z + cw[1:2, :] * z1 + cw[0:1, :] * z2
        dyc = dy * bg
        dbuf[0:tb, :] = dyc
        dbuf[tb:tb + SUBLANES, :] = jnp.where(i < nblk - 1, dyn_ref[...] * bn_ref[...], 0.0)
        d1, d2 = dbuf[1:1 + tb, :], dbuf[2:2 + tb, :]
        dz = cw[2:3, :] * dyc + cw[1:2, :] * d1 + cw[0:1, :] * d2
        dp_ref[1] = (dy * yc).astype(dp_ref.dtype)
        dp_ref[2] = (dz * u).astype(dp_ref.dtype)
        dp_ref[3] = (dz * cg).astype(dp_ref.dtype)
        dcw_ref[0:1, :] += jnp.sum(dyc * z2, axis=0, keepdims=True)
        dcw_ref[1:2, :] += jnp.sum(dyc * z1, axis=0, keepdims=True)
        dcw_ref[2:3, :] += jnp.sum(dyc * z, axis=0, keepdims=True)

    half = lambda g: pl.BlockSpec((tb, GROUP), lambda i: (i, g))
    grp = lambda g: pl.BlockSpec((None, tb, GROUP), lambda i: (g, i, 0))
    prev = lambda g: pl.BlockSpec((None, SUBLANES, GROUP), lambda i: (g, jnp.maximum(i * hb - 1, 0), 0))
    nxt_row = lambda i: jnp.minimum((i + 1) * hb, t // SUBLANES - 1)
    nxt = lambda g: pl.BlockSpec((None, SUBLANES, GROUP), lambda i: (g, nxt_row(i), 0))
    vec = lambda r: pl.BlockSpec((r, GROUP), lambda i: (0, 0))
    return pl.pallas_call(
        body, name="gate_bwd", grid=(nblk,),
        in_specs=[half(0), half(1), pl.BlockSpec((SUBLANES, GROUP), lambda i: (nxt_row(i), 1)), half(0), grp(3), vec(1),
                  grp(4), nxt(4), grp(5), grp(6), prev(5), prev(6), vec(3), ANY],
        out_specs=[half(0), pl.BlockSpec((4, tb, GROUP), lambda i: (0, i, 0)), vec(1), vec(3)],
        out_shape=[jax.ShapeDtypeStruct((t, HGRN_WIDTH), F32), jax.ShapeDtypeStruct((4, t, HGRN_WIDTH), BF16),
                   jax.ShapeDtypeStruct((1, HGRN_WIDTH), F32), jax.ShapeDtypeStruct((3, CONV_WIDTH), F32)],
        scratch_shapes=[pltpu.VMEM((tb + SUBLANES, GROUP), F32), pltpu.VMEM((tb + SUBLANES, GROUP), F32)],
        compiler_params=_params("arbitrary"),
    )(dcat, dcat, dcat, o, proj, gate_norm_w, proj, proj, proj, proj, proj, proj, conv_w, after)


def _hgrn_bwd(proj, do, states, lb_logits, after):
    t = proj.shape[1]
    tb = min(t, 512)
    ncb = tb // CHUNK
    nblk = t // tb

    def body(q_ref, f_ref, v_ref, do_ref, st_ref, lbl_ref, after_ref, dp_ref, dlbl_ref, ds_scr, dlb_scr):
        i = pl.program_id(0)

        @pl.when(i == 0)
        def _():
            ds_scr[...] = jnp.zeros_like(ds_scr)
            dlb_scr[...] = jnp.zeros_like(dlb_scr)

        lb, s1 = _lower_bound(lbl_ref[...])
        causal, anti = _chunk_masks()
        every = range(ncb)
        rows = [slice(c * CHUNK, (c + 1) * CHUNK) for c in every]
        q, v, do = ([ref[r, :] for r in rows] for ref in (q_ref, v_ref, do_ref))
        st = [st_ref[c] for c in every]
        gates = [_gates(f_ref[r, :], lb) for r in rows]
        sig, f, k = ([gt[n] for gt in gates] for n in (0, 1, 3))
        b = [_dot_exact(causal, gt[2]) for gt in gates]
        mid, last = [x[CHUNK // 2:CHUNK // 2 + 1, :] for x in b], [x[CHUNK - 1:CHUNK, :] for x in b]
        e_q = [jnp.exp(b[c] - mid[c]) for c in every]
        e_k = [jnp.exp(mid[c] - b[c]) for c in every]
        e_i = [jnp.exp(x) for x in b]
        e_s = [jnp.exp(last[c] - b[c]) for c in every]
        dec = [jnp.exp(x) for x in last]
        qt, kt, qi, ks = ([a[c] * e[c] for c in every] for a, e in ((q, e_q), (k, e_k), (q, e_i), (k, e_s)))

        def masked(a, b_):
            return [[jnp.where(causal, _dot(a_h, b_h, NT), 0.0) for a_h, b_h in zip(_heads(a[c]), _heads(b_[c]))]
                    for c in every]

        def with_scores(s, other, dims):
            return [jnp.concatenate([_dot(s_h, o_h, dims) for s_h, o_h in zip(s[c], _heads(other[c]))], axis=1)
                    for c in every]

        def per_head(dims, a, b_):
            return [_per_head(lambda a_h, b_h: _dot(a_h, b_h, dims), a[c], b_[c]) for c in every]

        scores, dscores = masked(qt, kt), masked(do, v)
        dqt, dkt, dv_intra = with_scores(dscores, kt, NN), with_scores(dscores, qt, TN), with_scores(scores, do, TN)
        dqi, update = per_head(NN, do, st), per_head(TN, do, qi)

        dst = ds_scr[...]
        dsts = [None] * ncb
        for c in reversed(every):
            dsts[c] = dst
            dst = dec[c] * dst + update[c]
        ds_scr[...] = dst

        dv_state, dks = per_head(NT, ks, dsts), per_head(NN, v, dsts)
        ddec = [jnp.sum(dsts[c] * st[c], axis=0, keepdims=True) for c in every]
        dq = [dqt[c] * e_q[c] + dqi[c] * e_i[c] for c in every]
        dk = [dkt[c] * e_k[c] + dks[c] * e_s[c] for c in every]
        db = [q[c] * dq[c] - k[c] * dk[c] for c in every]
        db_last = [jnp.sum(dks[c] * ks[c], axis=0, keepdims=True) + ddec[c] * dec[c] for c in every]
        dg = [_dot_exact(anti, db[c]) + db_last[c] for c in every]
        df = [dg[c] / f[c] - dk[c] for c in every]
        dlb_scr[...] += sum(jnp.sum(df[c] * (1.0 - sig[c]), axis=0, keepdims=True) for c in every)
        dfp = [df[c] * (1.0 - lb) * sig[c] * (1.0 - sig[c]) for c in every]
        dv = [dv_intra[c] + dv_state[c] for c in every]
        for n, parts in enumerate((dq, dfp, dv)):
            dp_ref[n] = jnp.concatenate(parts, axis=0).astype(dp_ref.dtype)

        @pl.when(i == nblk - 1)
        def _():
            dlb = dlb_scr[...]
            dlbl_ref[0:1, :] = dlb * lb * (1.0 - lb)
            dlbl_ref[1:2, :] = -dlb * lb * s1

    grp = lambda g: pl.BlockSpec((None, tb, GROUP), lambda i: (g, nblk - 1 - i, 0))
    vec = pl.BlockSpec((2, HGRN_WIDTH), lambda i: (0, 0))
    return pl.pallas_call(
        body, name="hgrn_bwd", grid=(nblk,),
        in_specs=[grp(0), grp(1), grp(2), pl.BlockSpec((tb, HGRN_WIDTH), lambda i: (nblk - 1 - i, 0)),
                  pl.BlockSpec((ncb, HEAD_DIM, HGRN_WIDTH), lambda i: (nblk - 1 - i, 0, 0)), vec, ANY],
        out_specs=[pl.BlockSpec((3, tb, HGRN_WIDTH), lambda i: (0, nblk - 1 - i, 0)), vec],
        out_shape=[jax.ShapeDtypeStruct((3, t, HGRN_WIDTH), BF16), jax.ShapeDtypeStruct((2, HGRN_WIDTH), F32)],
        scratch_shapes=[pltpu.VMEM((HEAD_DIM, HGRN_WIDTH), F32), pltpu.VMEM((1, HGRN_WIDTH), F32)],
        compiler_params=_params("arbitrary"),
    )(proj, proj, proj, do, states, lb_logits, after)


def _in_bwd(dph, dpg, w_in, dpre1, after):
    t = dpre1.shape[0]
    tm = min(t, 512)

    def body(dh_ref, dg_ref, w_ref, dp_ref, after_ref, o_ref):
        acc = ALPHA * dp_ref[...]
        for g in range(N_GROUPS):
            part = dh_ref[g] if g < 3 else dg_ref[g - 3]
            acc = acc + _dot(part, w_ref[:, g * GROUP:(g + 1) * GROUP], NT)
        o_ref[...] = acc

    row = pl.BlockSpec((tm, D_MODEL), lambda i: (i, 0))
    return pl.pallas_call(
        body, name="in_bwd", grid=(t // tm,),
        in_specs=[pl.BlockSpec((3, tm, GROUP), lambda i: (0, i, 0)), pl.BlockSpec((4, tm, GROUP), lambda i: (0, i, 0)),
                  _resident((D_MODEL, IN_COLS)), row, ANY],
        out_specs=row,
        out_shape=jax.ShapeDtypeStruct((t, D_MODEL), F32),
        compiler_params=_params("parallel"),
    )(dph, dpg, w_in, dpre1, after)


def _grad_w(name, operands, widths, shape, step, after=None):
    t = operands[0].shape[-2]
    tt = min(t, 1024)
    n_in, n_steps = len(operands), t // tt
    in_specs = [pl.BlockSpec((tt, w), lambda k: (k, 0)) if a.ndim == 2 else
                pl.BlockSpec((a.shape[0], tt, w), lambda k: (0, k, 0)) for a, w in zip(operands, widths)]
    extra = [] if after is None else [after]

    def body(*refs):
        o_ref, acc, sem = refs[-3:]
        k = pl.program_id(0)

        @pl.when(k == 0)
        def _():
            acc[...] = jnp.zeros_like(acc)

        step(acc, *refs[:n_in])

        @pl.when(k == n_steps - 1)
        def _():
            out = pltpu.make_async_copy(acc, o_ref, sem)
            out.start()
            out.wait()

    return pl.pallas_call(
        body, name=name, grid=(n_steps,), in_specs=in_specs + [ANY] * len(extra), out_specs=ANY,
        out_shape=jax.ShapeDtypeStruct(shape, F32),
        scratch_shapes=[pltpu.VMEM(shape, F32), pltpu.SemaphoreType.DMA],
        compiler_params=_params("arbitrary"),
    )(*operands, *extra)


def _dw_in(xb, dph, dpg, after):
    def step(acc, x_ref, dh_ref, dg_ref):
        xv = x_ref[...]
        for g in range(N_GROUPS):
            part = dh_ref[g] if g < 3 else dg_ref[g - 3]
            acc[:, g * GROUP:(g + 1) * GROUP] += _dot(xv, part, TN)

    return _grad_w("dw_in", (xb, dph, dpg), (D_MODEL, GROUP, GROUP), (D_MODEL, IN_COLS), step, after)


def _dw_out(cat, dpre1b):
    def step(acc, cat_ref, d_ref):
        dv = d_ref[...]
        for g in range(2):
            acc[g * GROUP:(g + 1) * GROUP, :] += _dot(cat_ref[g], dv, TN)

    return _grad_w("dw_out", (cat, dpre1b), (GROUP, D_MODEL), (D_MODEL, D_MODEL), step)


def _dw_ff1(h1b, da):
    def step(acc, h_ref, da_ref):
        hv = h_ref[...]
        for j in range(D_FF // FF_BLOCK):
            cols = slice(j * FF_BLOCK, (j + 1) * FF_BLOCK)
            acc[:, cols] += _dot(hv, da_ref[:, cols], TN)

    return _grad_w("dw_ff1", (h1b, da), (D_MODEL, D_FF), (D_MODEL, D_FF), step)


def _dw_ff2(r, dpre2b):
    def step(acc, r_ref, d_ref):
        dv = d_ref[...]
        for j in range(D_FF // FF_BLOCK):
            rows = slice(j * FF_BLOCK, (j + 1) * FF_BLOCK)
            acc[rows, :] += _dot(r_ref[:, rows], dv, TN)

    return _grad_w("dw_ff2", (r, dpre2b), (D_FF, D_MODEL), (D_FF, D_MODEL), step)


def _place():
    x, y, c = lax.axis_index("x"), lax.axis_index("y"), lax.axis_index("c")
    return x, y, c, 2 * x + y


def _other_chips(x, y):
    return [(1 - x, y), (x, 1 - y), (1 - x, 1 - y)]


def _place_shard(name, w, chip, cols_sharded, after=None):
    rows, cols = w.shape
    tr = min(rows, 256)
    nb = rows // tr
    full = (rows, cols * N_CHIPS) if cols_sharded else (rows * N_CHIPS, cols)
    out_map = (lambda i, s: (i, s[0])) if cols_sharded else (lambda i, s: (s[0] * nb + i, 0))

    def body(s_ref, w_ref, *rest):
        rest[-1][...] = w_ref[...].astype(rest[-1].dtype)

    extra = [] if after is None else [after]
    return pl.pallas_call(
        body, name=name,
        grid_spec=pltpu.PrefetchScalarGridSpec(
            num_scalar_prefetch=1, grid=(nb,),
            in_specs=[pl.BlockSpec((tr, cols), lambda i, s: (i, 0))] + [ANY] * len(extra),
            out_specs=pl.BlockSpec((tr, cols), out_map)),
        out_shape=jax.ShapeDtypeStruct(full, BF16),
        compiler_params=_params("parallel"),
    )(chip, w, *extra)


HBM = pl.BlockSpec(memory_space=pltpu.HBM)
SEM = pl.BlockSpec(memory_space=pltpu.SEMAPHORE)
EFFECT = pltpu.SideEffectType.DATAFLOW_SIDE_EFFECTING


class _Split:
    def __init__(self, name, arrays, plan):
        n, n_copies = len(arrays), plan.count
        self.name, self.plan, self.n = name, plan, n

        def body(*refs):
            send_sems, recv_sems, token = refs[n], refs[n + 1], refs[-1]
            for k, (src, dst, to) in enumerate(plan(refs[:n])):
                pltpu.make_async_remote_copy(src_ref=src, dst_ref=dst, send_sem=send_sems.at[k], recv_sem=recv_sems.at[k],
                                             device_id=to, device_id_type=MESH).start()
            token[...] = jnp.zeros_like(token)

        outs = pl.pallas_call(
            body, name=name + "_start",
            out_shape=(pltpu.SemaphoreType.DMA((n_copies,)), pltpu.SemaphoreType.DMA((n_copies,)),
                       *[pltpu.HBM(a.shape, a.dtype) for a in arrays], jax.ShapeDtypeStruct((SUBLANES, LANES), F32)),
            in_specs=(HBM,) * n, out_specs=(SEM, SEM) + (HBM,) * n + (pl.BlockSpec(memory_space=pltpu.VMEM),),
            input_output_aliases={i: 2 + i for i in range(n)},
            compiler_params=pltpu.CompilerParams(has_side_effects=EFFECT),
        )(*[pltpu.with_memory_space_constraint(a, pltpu.HBM) for a in arrays])
        self.sems, self.arrays, self.token = outs[:2], outs[2:2 + n], outs[-1]

    def wait(self, after):
        n, plan = self.n, self.plan

        def body(*refs):
            send_sems, recv_sems = refs[n], refs[n + 1]
            for k, (src, dst, to) in enumerate(plan(refs[:n])):
                cp = pltpu.make_async_remote_copy(src_ref=src, dst_ref=dst, send_sem=send_sems.at[k],
                                                  recv_sem=recv_sems.at[k], device_id=to, device_id_type=MESH)
                cp.wait_send()
                cp.wait_recv()

        return pl.pallas_call(
            body, name=self.name + "_wait", out_shape=tuple(pltpu.HBM(a.shape, a.dtype) for a in self.arrays),
            in_specs=(HBM,) * n + (SEM, SEM, ANY), out_specs=(HBM,) * n, input_output_aliases={i: i for i in range(n)},
            compiler_params=pltpu.CompilerParams(has_side_effects=EFFECT),
        )(*self.arrays, *self.sems, after)


COLS_SHARDED = (True, False, True, False)
HALF_SHAPES = [(D_MODEL // 2, IN_COLS), (D_MODEL, D_MODEL // 2), (D_MODEL // 2, D_FF), (D_FF, D_MODEL // 2)]
PIECE_SHAPES = [(D_MODEL // 2, IN_COLS // N_CHIPS), (D_MODEL // N_CHIPS, D_MODEL // 2),
                (D_MODEL // 2, D_FF // N_CHIPS), (D_FF // N_CHIPS, D_MODEL // 2)]


def _shard_view(kind, ref, chip):
    if COLS_SHARDED[kind]:
        n = ref.shape[1] // N_CHIPS
        return ref.at[:, pl.ds(chip * n, n)]
    n = ref.shape[0] // N_CHIPS
    return ref.at[pl.ds(chip * n, n), :]


def _half_view(kind, ref, h):
    if COLS_SHARDED[kind]:
        n = ref.shape[0] // 2
        return ref.at[pl.ds(h * n, n), :]
    n = ref.shape[1] // 2
    return ref.at[:, pl.ds(h * n, n)]


def _plan(count):
    def mark(fn):
        fn.count = count
        return fn
    return mark


def _shard_half_view(kind, ref, chip, h):
    if COLS_SHARDED[kind]:
        m, n = ref.shape[0] // 2, ref.shape[1] // N_CHIPS
        return ref.at[pl.ds(h * m, m), pl.ds(chip * n, n)]
    m = ref.shape[0] // N_CHIPS // 2
    return ref.at[pl.ds((2 * chip + h) * m, m), :]


def _gather_over_ici(kinds, weights):
    @_plan(3 * len(kinds))
    def plan(refs):
        x, y, c, me = _place()
        mine = [_shard_half_view(kind, ref, me, c) for kind, ref in zip(kinds, refs)]
        return [(v, v, (px, py, c)) for v in mine for px, py in _other_chips(x, y)]

    return _Split("gather_ici_" + "".join(map(str, kinds)), tuple(weights), plan)


def _gather_w_in_over_ici(w_in, conv4):
    @_plan(6)
    def plan(refs):
        x, y, c, me = _place()
        half, conv = _shard_half_view(0, refs[0], me, c), refs[1].at[me]
        return [(v, v, (px, py, c)) for v in (half, conv) for px, py in _other_chips(x, y)]

    return _Split("gather_w_in_ici", (w_in, conv4), plan)


def _gather_over_d2d(kinds, weights):
    @_plan(3 * len(kinds))
    def plan(refs):
        x, y, c, _ = _place()
        got = [_shard_half_view(kind, ref, 2 * px + py, c) for kind, ref in zip(kinds, refs)
               for px, py in _other_chips(x, y)]
        return [(v, v, (x, y, 1 - c)) for v in got]

    return _Split("gather_d2d_" + "".join(map(str, kinds)), tuple(weights), plan)


def _swap_halves(kinds, grads):
    @_plan(len(kinds))
    def plan(refs):
        x, y, c, _ = _place()
        return [(_half_view(kind, g, 1 - c), land, (x, y, 1 - c))
                for kind, g, land in zip(kinds, refs[:len(kinds)], refs[len(kinds):])]

    lands = [lax.empty(HALF_SHAPES[kind], F32) for kind in kinds]
    return _Split("swap_halves_" + "".join(map(str, kinds)), (*grads, *lands), plan)


def _add_half(name, g, recv, core, rows_split):
    shape = recv.shape
    tr = min(shape[0], 128 if rows_split else 256)
    nb = shape[0] // tr

    def body(c_ref, g_ref, r_ref, o_ref):
        o_ref[...] = (g_ref[...] + r_ref[...]).astype(o_ref.dtype)

    g_map = (lambda i, c_ref: (c_ref[0] * nb + i, 0)) if rows_split else (lambda i, c_ref: (i, c_ref[0]))
    blk = pl.BlockSpec((tr, shape[1]), lambda i, c_ref: (i, 0))
    return pl.pallas_call(
        body, name=name,
        grid_spec=pltpu.PrefetchScalarGridSpec(
            num_scalar_prefetch=1, grid=(nb,),
            in_specs=[pl.BlockSpec((tr, shape[1]), g_map), blk], out_specs=blk),
        out_shape=jax.ShapeDtypeStruct(shape, BF16),
        compiler_params=_params("parallel"),
    )(core, g, recv)


def _exchange_pieces(kinds, halves):
    n_p = N_CHIPS - 1

    @_plan(n_p * len(kinds))
    def plan(refs):
        x, y, c, _ = _place()
        return [(_shard_view(kind, half, 2 * px + py), land.at[j], (px, py, c))
                for j, (px, py) in enumerate(_other_chips(x, y))
                for kind, half, land in zip(kinds, refs[:len(kinds)], refs[len(kinds):])]

    lands = [lax.empty((n_p,) + PIECE_SHAPES[kind], BF16) for kind in kinds]
    return _Split("exchange_pieces_" + "".join(map(str, kinds)), (*halves, *lands), plan)


def _sum_pieces(name, half, slots, place, rows_split):
    n_p, rows, cols = slots.shape
    tr = min(rows, 256)
    nb = rows // tr
    if rows_split:
        own_map = lambda i, s: (i, s[0])
        out_map = lambda i, s: (s[1] * nb + i, 0)
        shard = (2 * rows, cols)
    else:
        own_map = lambda i, s: (s[0] * nb + i, 0)
        out_map = lambda i, s: (i, s[1])
        shard = (rows, 2 * cols)

    def body(s_ref, own_ref, slot_ref, o_ref):
        total = own_ref[...].astype(F32)
        for j in range(n_p):
            total = total + slot_ref[j].astype(F32)
        o_ref[...] = total

    return pl.pallas_call(
        body, name=name,
        grid_spec=pltpu.PrefetchScalarGridSpec(
            num_scalar_prefetch=1, grid=(nb,),
            in_specs=[pl.BlockSpec((tr, cols), own_map), pl.BlockSpec((n_p, tr, cols), lambda i, s: (0, i, 0))],
            out_specs=pl.BlockSpec((tr, cols), out_map)),
        out_shape=jax.ShapeDtypeStruct(shard, F32),
        compiler_params=_params("parallel"),
    )(place, half, slots)


def _join_halves(kinds, shards):
    @_plan(len(kinds))
    def plan(refs):
        x, y, c, _ = _place()
        return [(_half_view(kind, g, c), _half_view(kind, g, c), (x, y, 1 - c)) for kind, g in zip(kinds, refs)]

    return _Split("join_halves_" + "".join(map(str, kinds)), tuple(shards), plan)


N_DEV = 8


def _share_small(pack, after):
    @_plan(N_DEV - 1)
    def plan(refs):
        x, y, c, _ = _place()
        me = 4 * x + 2 * y + c
        peers = [((1 - x) if m & 4 else x, (1 - y) if m & 2 else y, (1 - c) if m & 1 else c) for m in range(1, N_DEV)]
        return [(refs[0], refs[1].at[me], peer) for peer in peers]

    return _Split("share_small", (pack, lax.empty((N_DEV,) + pack.shape, F32), after), plan)


def _sum_shared(pack, land, device):
    def body(d_ref, p_ref, l_ref, o_ref):
        me = d_ref[0]
        total = jnp.where(me == 0, p_ref[...], l_ref[0])
        for d in range(1, N_DEV):
            total = total + jnp.where(me == d, p_ref[...], l_ref[d])
        o_ref[...] = total

    return pl.pallas_call(
        body, name="sum_shared",
        grid_spec=pltpu.PrefetchScalarGridSpec(
            num_scalar_prefetch=1, grid=(1,),
            in_specs=[pl.BlockSpec(pack.shape, lambda i, d: (0, 0)), pl.BlockSpec(land.shape, lambda i, d: (0, 0, 0))],
            out_specs=pl.BlockSpec(pack.shape, lambda i, d: (0, 0))),
        out_shape=jax.ShapeDtypeStruct(pack.shape, F32),
    )(device, pack, land)


def _adamw(name, w, g, m, v, after=None):
    rows, cols = w.shape
    tr = min(rows, 256)
    extra = [] if after is None else [after]

    def body(w_ref, g_ref, m_ref, v_ref, *rest):
        d_ref, nm_ref, nv_ref = rest[-3:]
        gv = g_ref[...]
        nm = ADAM_B1 * m_ref[...] + (1.0 - ADAM_B1) * gv
        nv = ADAM_B2 * v_ref[...] + (1.0 - ADAM_B2) * jnp.square(gv)
        m_hat = nm * (1.0 / (1.0 - ADAM_B1 ** ADAM_STEP))
        v_hat = nv * (1.0 / (1.0 - ADAM_B2 ** ADAM_STEP))
        d_ref[...] = -ADAM_LR * (m_hat / (jnp.sqrt(v_hat) + ADAM_EPS) + ADAM_WD * w_ref[...])
        nm_ref[...] = nm
        nv_ref[...] = nv

    blk = pl.BlockSpec((tr, cols), lambda i: (i, 0))
    return pl.pallas_call(
        body, name=name, grid=(rows // tr,), in_specs=[blk] * 4 + [ANY] * len(extra), out_specs=[blk] * 3,
        out_shape=[jax.ShapeDtypeStruct(w.shape, F32)] * 3,
        compiler_params=_params("parallel"),
    )(w, g, m, v, *extra)


def kernel(x, w_in, lb_logits, gate_norm_w, conv_w, w_out, ln1_g, ln1_b, w_ff1, w_ff2, ln2_g, ln2_b, loss_target, m_w_in, m_lb_logits, m_gate_norm_w, m_conv_w, m_w_out, m_ln1_g, m_ln1_b, m_w_ff1, m_w_ff2, m_ln2_g, m_ln2_b, v_w_in, v_lb_logits, v_gate_norm_w, v_conv_w, v_w_out, v_ln1_g, v_ln1_b, v_w_ff1, v_w_ff2, v_ln2_g, v_ln2_b):
    xs, tgt = x[0], loss_target[0]
    chip = 2 * lax.axis_index("x") + lax.axis_index("y")
    core = lax.axis_index("c").astype(jnp.int32).reshape(1)
    chip1 = chip.astype(jnp.int32).reshape(1)
    place = jnp.concatenate([chip1, core])

    conv4 = lax.dynamic_update_slice(jnp.zeros((N_CHIPS,) + conv_w.shape[1:], F32), conv_w, (chip, 0, 0))
    ici_in = _gather_w_in_over_ici(_place_shard("place_w_in", w_in[0], chip1, True), conv4)
    ici_out = _gather_over_ici((1,), (_place_shard("place_w_out", w_out[0], chip1, False, after=ici_in.token),))
    ici_ff = _gather_over_ici((2, 3), (_place_shard("place_w_ff1", w_ff1[0], chip1, True, after=ici_in.token),
                                       _place_shard("place_w_ff2", w_ff2[0], chip1, False, after=ici_out.token)))
    wb_in, cv4 = ici_in.wait(ici_ff.token)
    d2d_in = _gather_over_d2d((0,), (wb_in,))
    wb_in, = d2d_in.wait(d2d_in.token)
    conv_full = cv4.transpose(1, 0, 2).reshape(3, CONV_WIDTH)

    proj, xb = _in_proj(xs, wb_in, ici_ff.token)
    o, states = _hgrn_fwd(proj, lb_logits)
    d2d_out = _gather_over_d2d((1,), ici_out.wait(o))
    cat = _gate_fwd(proj, o, gate_norm_w, conv_full, d2d_out.token)
    wb_out, = d2d_out.wait(cat)
    d2d_ff = _gather_over_d2d((2, 3), ici_ff.wait(cat))
    xhat1, h1b, rstd1 = _out_ln1(cat, wb_out, xs, ln1_g, ln1_b, d2d_ff.token)
    wb_ff1, wb_ff2 = d2d_ff.wait(xhat1)
    r, dpre2, dpre2b, g_ln2_g, g_ln2_b, loss8 = _mlp_fwd(xhat1, ln1_g, ln1_b, wb_ff1, wb_ff2, ln2_g, ln2_b, tgt)

    names = ("w_in", "w_out", "w_ff1", "w_ff2")

    def add_halves(kinds, grads, lands):
        return [_add_half("add_half_" + names[k], g, ld, core, COLS_SHARDED[k]) for k, g, ld in zip(kinds, grads, lands)]

    def sum_pieces(kinds, halves, lands):
        return [_sum_pieces("sum_pieces_" + names[k], h, ld, place, COLS_SHARDED[k]) for k, h, ld in zip(kinds, halves, lands)]

    da, dpre1, dpre1b, g_ln1_g, g_ln1_b = _mlp_bwd(dpre2, r, wb_ff1, wb_ff2, xhat1, rstd1, ln1_g)
    dcat = _out_bwd(dpre1b, wb_out)
    early = (1, 2, 3)
    swap = _swap_halves(early, (_dw_out(cat, dpre1b), _dw_ff1(h1b, da), _dw_ff2(r, dpre2b)))
    do, dpg, g_gnw, g_conv = _gate_bwd(dcat, o, proj, gate_norm_w, conv_full, swap.token)
    swapped = swap.wait(do)
    exch = _exchange_pieces(early, add_halves(early, swapped[:3], swapped[3:]))
    dph, g_lbl = _hgrn_bwd(proj, do, states, lb_logits, exch.token)
    g_in_local = _dw_in(xb, dph, dpg, dph)

    late = (0,)
    swap = _swap_halves(late, (g_in_local,))
    grad_x = _in_bwd(dph, dpg, wb_in, dpre1, swap.token)
    exchanged = exch.wait(grad_x)
    pack = jnp.concatenate([
        g_ln1_g, g_ln1_b, g_ln2_g, g_ln2_b,
        jnp.concatenate([g_lbl[0:1], g_lbl[1:2]], axis=1),
        jnp.concatenate([g_gnw, g_conv[0:1]], axis=1),
        jnp.concatenate([g_conv[1:2], g_conv[2:3]], axis=1),
        jnp.concatenate([loss8[0:1], jnp.zeros((1, D_MODEL - LANES), F32)], axis=1)], axis=0)
    small = _share_small(pack, exchanged[0])
    swapped = swap.wait(small.token)
    exch = _exchange_pieces(late, add_halves(late, swapped[:1], swapped[1:]))
    shared = small.wait(exch.token)
    join = _join_halves(early, sum_pieces(early, (shared[2], *exchanged[1:3]), exchanged[3:]))
    tot = _sum_shared(shared[0], shared[1], 2 * chip1 + core)
    loss = tot[7, 0]
    half = D_MODEL // 2
    g_lb_logits = jnp.concatenate([tot[4:5, :half], tot[4:5, half:]], axis=0)
    g_gate_norm_w = tot[5:6, :half]
    g_conv_full = jnp.concatenate([tot[5:6, half:], tot[6:7, :half], tot[6:7, half:]], axis=0)
    g_conv_w = lax.dynamic_slice(g_conv_full, (0, chip * LANES), (3, LANES))
    g_w_out, g_w_ff1, g_w_ff2 = join.wait(tot)
    d_ff1, nm_ff1, nv_ff1 = _adamw("adamw_w_ff1", w_ff1[0], g_w_ff1, m_w_ff1[0], v_w_ff1[0], exch.token)
    d_ff2, nm_ff2, nv_ff2 = _adamw("adamw_w_ff2", w_ff2[0], g_w_ff2, m_w_ff2[0], v_w_ff2[0], d_ff1)
    d_out, nm_out, nv_out = _adamw("adamw_w_out", w_out[0], g_w_out, m_w_out[0], v_w_out[0], d_ff2)

    def small_pack(lbl, gnw, cv, l1g, l1b, l2g, l2b):
        pad = jnp.zeros((1, D_MODEL - 3 * LANES), F32)
        return jnp.concatenate([
            l1g, l1b, l2g, l2b, jnp.concatenate([lbl[0:1], lbl[1:2]], axis=1),
            jnp.concatenate([gnw, jnp.zeros((1, half), F32)], axis=1),
            jnp.concatenate([cv[0:1], cv[1:2], cv[2:3], pad], axis=1), jnp.zeros((1, D_MODEL), F32)], axis=0)

    w_s = small_pack(lb_logits, gate_norm_w, conv_w[0], ln1_g, ln1_b, ln2_g, ln2_b)
    g_s = small_pack(g_lb_logits, g_gate_norm_w, g_conv_w, tot[0:1], tot[1:2], tot[2:3], tot[3:4])
    m_s = small_pack(m_lb_logits, m_gate_norm_w, m_conv_w[0], m_ln1_g, m_ln1_b, m_ln2_g, m_ln2_b)
    v_s = small_pack(v_lb_logits, v_gate_norm_w, v_conv_w[0], v_ln1_g, v_ln1_b, v_ln2_g, v_ln2_b)
    d_s, nm_s, nv_s = _adamw("adamw_small", w_s, g_s, m_s, v_s, d_out)
    exchanged = exch.wait(d_s)
    join = _join_halves(late, sum_pieces(late, exchanged[:1], exchanged[1:]))
    g_w_in, = join.wait(join.token)
    d_in, nm_in, nv_in = _adamw("adamw_w_in", w_in[0], g_w_in, m_w_in[0], v_w_in[0])

    def unpack(p):
        lbl = jnp.concatenate([p[4:5, :half], p[4:5, half:]], axis=0)
        cv = jnp.concatenate([p[6:7, 0:LANES], p[6:7, LANES:2 * LANES], p[6:7, 2 * LANES:3 * LANES]], axis=0)
        return dict(lb_logits=lbl, gate_norm_w=p[5:6, :half], conv_w=cv[None], ln1_g=p[0:1], ln1_b=p[1:2],
                    ln2_g=p[2:3], ln2_b=p[3:4])

    order = ("w_in", "lb_logits", "gate_norm_w", "conv_w", "w_out", "ln1_g", "ln1_b", "w_ff1", "w_ff2", "ln2_g", "ln2_b")
    grad = dict(unpack(g_s), w_in=g_w_in[None], w_out=g_w_out[None], w_ff1=g_w_ff1[None], w_ff2=g_w_ff2[None])
    delta = dict(unpack(d_s), w_in=d_in[None], w_out=d_out[None], w_ff1=d_ff1[None], w_ff2=d_ff2[None])
    new_m = dict(unpack(nm_s), w_in=nm_in[None], w_out=nm_out[None], w_ff1=nm_ff1[None], w_ff2=nm_ff2[None])
    new_v = dict(unpack(nv_s), w_in=nv_in[None], w_out=nv_out[None], w_ff1=nv_ff1[None], w_ff2=nv_ff2[None])
    return (loss, grad_x[None], *[grad[n] for n in order], *[delta[n] for n in order],
            *[new_m[n] for n in order], *[new_v[n] for n in order])
```

```python
import jax
import jax.numpy as jnp
from jax import lax
from jax.experimental import pallas as pl
from jax.experimental.pallas import tpu as pltpu

F32 = jnp.float32
BF16 = jnp.bfloat16
MXU_DTYPE = jnp.bfloat16

D_MODEL = 1024
HGRN_WIDTH = 512
HEAD_DIM = 128
N_HEADS = 4
CONV_WIDTH = 512
CHUNK = 64
D_FF = 4096
IN_COLS = 3584
GROUP = 512
N_GROUPS = IN_COLS // GROUP
ALPHA = 2.0 ** 0.25
EPS = 1e-5
N_CHIPS = 4
ADAM_LR, ADAM_B1, ADAM_B2, ADAM_EPS, ADAM_WD, ADAM_STEP = 0.001, 0.9, 0.999, 1e-08, 0.01, 10

LANES = 128
SUBLANES = 8
VMEM_LIMIT = 56 * 1024 * 1024
FF_BLOCK = 1024
N_FF = D_FF // FF_BLOCK

NN = (((1,), (0,)), ((), ()))
NT = (((1,), (1,)), ((), ()))
TN = (((0,), (0,)), ((), ()))
MESH = pl.DeviceIdType.MESH
ANY = pl.BlockSpec(memory_space=pl.ANY)


def _dot(a, b, dims):
    return lax.dot_general(a.astype(MXU_DTYPE), b.astype(MXU_DTYPE), dims, preferred_element_type=F32)


def _dot_exact(ones, v):
    ones = ones.astype(jnp.bfloat16)
    hi = v.astype(jnp.bfloat16)
    rest = v - hi.astype(F32)
    mid = rest.astype(jnp.bfloat16)
    low = (rest - mid.astype(F32)).astype(jnp.bfloat16)
    return sum(lax.dot_general(ones, part, NN, preferred_element_type=F32) for part in (hi, mid, low))


def _params(*sem):
    return pltpu.CompilerParams(dimension_semantics=sem, vmem_limit_bytes=VMEM_LIMIT)


def _resident(shape):
    return pl.BlockSpec(shape, lambda *_: (0,) * len(shape), pipeline_mode=pl.Buffered(1))


def _sigmoid(v):
    return 1.0 / (1.0 + jnp.exp(-v))


def _lower_bound(lbl):
    m = jnp.max(lbl, axis=0, keepdims=True)
    e = jnp.exp(lbl - m)
    s = e / jnp.sum(e, axis=0, keepdims=True)
    return s[0:1, :], s[1:2, :]


def _heads(v):
    return [v[:, h * HEAD_DIM:(h + 1) * HEAD_DIM] for h in range(N_HEADS)]


def _per_head(fn, *arrays):
    return jnp.concatenate([fn(*parts) for parts in zip(*map(_heads, arrays))], axis=1)


def _in_proj(x, w_in, after):
    t = x.shape[0]
    tm = min(t, 512)

    def body(x_ref, w_ref, after_ref, o_ref, xb_ref):
        xb = x_ref[...].astype(xb_ref.dtype)
        xb_ref[...] = xb
        for g in range(N_GROUPS):
            o_ref[g] = _dot(xb, w_ref[:, g * GROUP:(g + 1) * GROUP], NN)

    return pl.pallas_call(
        body, name="in_proj", grid=(t // tm,),
        in_specs=[pl.BlockSpec((tm, D_MODEL), lambda i: (i, 0)), _resident((D_MODEL, IN_COLS)), ANY],
        out_specs=[pl.BlockSpec((N_GROUPS, tm, GROUP), lambda i: (0, i, 0)), pl.BlockSpec((tm, D_MODEL), lambda i: (i, 0))],
        out_shape=[jax.ShapeDtypeStruct((N_GROUPS, t, GROUP), F32), jax.ShapeDtypeStruct((t, D_MODEL), BF16)],
        compiler_params=_params("parallel"),
    )(x, w_in, after)


def _gates(fp, lb):
    sig = _sigmoid(fp)
    f = lb + (1.0 - lb) * sig
    return sig, f, jnp.log(f), 1.0 - f


def _chunk_masks():
    row = lax.broadcasted_iota(jnp.int32, (CHUNK, CHUNK), 0)
    col = lax.broadcasted_iota(jnp.int32, (CHUNK, CHUNK), 1)
    return row >= col, row <= col


def _hgrn_fwd(proj, lb_logits):
    t = proj.shape[1]
    tb = min(t, 512)
    ncb = tb // CHUNK

    def body(q_ref, f_ref, v_ref, lbl_ref, o_ref, st_ref, s_scr):
        @pl.when(pl.program_id(0) == 0)
        def _():
            s_scr[...] = jnp.zeros_like(s_scr)

        lb, _ = _lower_bound(lbl_ref[...])
        causal, _ = _chunk_masks()

        every = range(ncb)
        rows = [slice(c * CHUNK, (c + 1) * CHUNK) for c in every]
        q, v = [q_ref[r, :] for r in rows], [v_ref[r, :] for r in rows]
        gates = [_gates(f_ref[r, :], lb) for r in rows]
        k = [gt[3] for gt in gates]
        b = [_dot_exact(causal, gt[2]) for gt in gates]
        mid, last = [x[CHUNK // 2:CHUNK // 2 + 1, :] for x in b], [x[CHUNK - 1:CHUNK, :] for x in b]
        qt = [q[c] * jnp.exp(b[c] - mid[c]) for c in every]
        kt = [k[c] * jnp.exp(mid[c] - b[c]) for c in every]
        qi = [q[c] * jnp.exp(b[c]) for c in every]
        ks = [k[c] * jnp.exp(last[c] - b[c]) for c in every]
        dec = [jnp.exp(x) for x in last]
        scores = [[jnp.where(causal, _dot(a, b_, NT), 0.0) for a, b_ in zip(_heads(qt[c]), _heads(kt[c]))] for c in every]
        intra = [[_dot(s, v_h, NN) for s, v_h in zip(scores[c], _heads(v[c]))] for c in every]
        update = [_per_head(lambda v_h, ks_h: _dot(v_h, ks_h, TN), v[c], ks[c]) for c in every]

        st = s_scr[...]
        states = []
        for c in every:
            states.append(st)
            st_ref[c] = st
            st = dec[c] * st + update[c]
        s_scr[...] = st

        o_ref[...] = jnp.concatenate(
            [jnp.concatenate([i_h + _dot(qi_h, st_h, NT) for i_h, qi_h, st_h in
                              zip(intra[c], _heads(qi[c]), _heads(states[c]))], axis=1) for c in every], axis=0)

    grp = lambda g: pl.BlockSpec((None, tb, GROUP), lambda i: (g, i, 0))
    return pl.pallas_call(
        body, name="hgrn_fwd", grid=(t // tb,),
        in_specs=[grp(0), grp(1), grp(2), pl.BlockSpec((2, HGRN_WIDTH), lambda i: (0, 0))],
        out_specs=[pl.BlockSpec((tb, HGRN_WIDTH), lambda i: (i, 0)),
                   pl.BlockSpec((ncb, HEAD_DIM, HGRN_WIDTH), lambda i: (i, 0, 0))],
        out_shape=[jax.ShapeDtypeStruct((t, HGRN_WIDTH), F32),
                   jax.ShapeDtypeStruct((t // CHUNK, HEAD_DIM, HGRN_WIDTH), F32)],
        scratch_shapes=[pltpu.VMEM((HEAD_DIM, HGRN_WIDTH), F32)],
        compiler_params=_params("arbitrary"),
    )(proj, proj, proj, lb_logits)


def _conv_taps(z, halo, zbuf, tb):
    zbuf[0:SUBLANES, :] = halo
    zbuf[SUBLANES:SUBLANES + tb, :] = z
    return zbuf[SUBLANES - 1:SUBLANES - 1 + tb, :], zbuf[SUBLANES - 2:SUBLANES - 2 + tb, :]


def _gate_fwd(proj, o, gate_norm_w, conv_w, after):
    t = proj.shape[1]
    tb = min(t, 512)
    hb = tb // SUBLANES

    def body(o_ref, og_ref, gnw_ref, b_ref, c_ref, u_ref, ch_ref, uh_ref, cw_ref, after_ref, cat_ref, zbuf):
        i = pl.program_id(0)
        og = og_ref[...]
        on = _per_head(lambda o_h: o_h * lax.rsqrt(jnp.mean(o_h * o_h, axis=-1, keepdims=True) + EPS), o_ref[...])
        cat_ref[0] = (on * gnw_ref[...] * (og * _sigmoid(og))).astype(cat_ref.dtype)
        z = c_ref[...] * u_ref[...]
        halo = jnp.where(i > 0, ch_ref[...] * uh_ref[...], 0.0)
        z1, z2 = _conv_taps(z, halo, zbuf, tb)
        cw = cw_ref[...]
        yc = cw[2:3, :] * z + cw[1:2, :] * z1 + cw[0:1, :] * z2
        cat_ref[1] = (b_ref[...] * yc).astype(cat_ref.dtype)

    grp = lambda g: pl.BlockSpec((None, tb, GROUP), lambda i: (g, i, 0))
    prev = lambda g: pl.BlockSpec((None, SUBLANES, GROUP), lambda i: (g, jnp.maximum(i * hb - 1, 0), 0))
    vec = lambda r: pl.BlockSpec((r, GROUP), lambda i: (0, 0))
    return pl.pallas_call(
        body, name="gate_fwd", grid=(t // tb,),
        in_specs=[pl.BlockSpec((tb, GROUP), lambda i: (i, 0)), grp(3), vec(1), grp(4), grp(5), grp(6), prev(5), prev(6),
                  vec(3), ANY],
        out_specs=pl.BlockSpec((2, tb, GROUP), lambda i: (0, i, 0)),
        out_shape=jax.ShapeDtypeStruct((2, t, HGRN_WIDTH), BF16),
        scratch_shapes=[pltpu.VMEM((tb + SUBLANES, GROUP), F32)],
        compiler_params=_params("parallel"),
    )(o, proj, gate_norm_w, proj, proj, proj, proj, proj, conv_w, after)


def _ln_bwd(dy, xhat, rstd, g):
    dxhat = dy * g
    m1 = jnp.mean(dxhat, axis=-1, keepdims=True)
    m2 = jnp.mean(dxhat * xhat, axis=-1, keepdims=True)
    return rstd * (dxhat - m1 - xhat * m2)


def _layer_norm(pre):
    xc = pre - jnp.mean(pre, axis=-1, keepdims=True)
    rstd = lax.rsqrt(jnp.mean(xc * xc, axis=-1, keepdims=True) + EPS)
    return xc * rstd, rstd


def _sublayers(cat, x, target, w_out, w_ff1, w_ff2, g1, b1, g2, b2):
    t = x.shape[0]
    tm = min(t, 256)

    def body(cat_ref, x_ref, tg_ref, wo_ref, w1_ref, w2_ref, g1_ref, b1_ref, g2_ref, b2_ref,
             h1_ref, r_ref, da_ref, dp2b_ref, dp1_ref, dp1b_ref, dcat_ref, dg1_ref, db1_ref, dg2_ref, db2_ref, loss_ref):
        @pl.when(pl.program_id(0) == 0)
        def _():
            for ref in (dg1_ref, db1_ref, dg2_ref, db2_ref, loss_ref):
                ref[...] = jnp.zeros_like(ref)

        mix = _dot(cat_ref[0], wo_ref[0:GROUP, :], NN) + _dot(cat_ref[1], wo_ref[GROUP:2 * GROUP, :], NN)
        xhat1, rstd1 = _layer_norm(ALPHA * x_ref[...] + mix)
        h1 = xhat1 * g1_ref[...] + b1_ref[...]
        h1b = h1.astype(h1_ref.dtype)
        h1_ref[...] = h1b
        mlp = jnp.zeros((tm, D_MODEL), F32)
        for j in range(N_FF):
            cols = slice(j * FF_BLOCK, (j + 1) * FF_BLOCK)
            r = jnp.square(jnp.maximum(_dot(h1b, w1_ref[:, cols], NN), 0.0)).astype(r_ref.dtype)
            r_ref[:, cols] = r
            mlp = mlp + _dot(r, w2_ref[cols, :], NN)
        xhat2, rstd2 = _layer_norm(ALPHA * h1 + mlp)
        err = xhat2 * g2_ref[...] + b2_ref[...] - tg_ref[...]
        loss_ref[...] += 0.5 * jnp.sum(jnp.mean(err * err, axis=-1, keepdims=True))
        dy = err * (1.0 / D_MODEL)
        dg2_ref[...] += jnp.sum(dy * xhat2, axis=0, keepdims=True)
        db2_ref[...] += jnp.sum(dy, axis=0, keepdims=True)
        dp2 = _ln_bwd(dy, xhat2, rstd2, g2_ref[...])
        dp2b = dp2.astype(dp2b_ref.dtype)
        dp2b_ref[...] = dp2b
        back = jnp.zeros((tm, D_MODEL), F32)
        for j in range(N_FF):
            cols = slice(j * FF_BLOCK, (j + 1) * FF_BLOCK)
            dr = _dot(dp2b, w2_ref[cols, :], NT)
            da = (dr * (2.0 * jnp.sqrt(r_ref[:, cols].astype(F32)))).astype(da_ref.dtype)
            da_ref[:, cols] = da
            back = back + _dot(da, w1_ref[:, cols], NT)
        dh1 = ALPHA * dp2 + back
        dg1_ref[...] += jnp.sum(dh1 * xhat1, axis=0, keepdims=True)
        db1_ref[...] += jnp.sum(dh1, axis=0, keepdims=True)
        dp1 = _ln_bwd(dh1, xhat1, rstd1, g1_ref[...])
        dp1b = dp1.astype(dp1b_ref.dtype)
        dp1_ref[...] = dp1
        dp1b_ref[...] = dp1b
        dcat_ref[...] = _dot(dp1b, wo_ref[...], NT)

    row = pl.BlockSpec((tm, D_MODEL), lambda i: (i, 0))
    wide = pl.BlockSpec((tm, D_FF), lambda i: (i, 0))
    vec = pl.BlockSpec((1, D_MODEL), lambda i: (0, 0))
    narrow = lambda dtype: jax.ShapeDtypeStruct((t, D_MODEL), dtype)
    return pl.pallas_call(
        body, name="sublayers", grid=(t // tm,),
        in_specs=[pl.BlockSpec((2, tm, GROUP), lambda i: (0, i, 0)), row, row, _resident((D_MODEL, D_MODEL)),
                  _resident((D_MODEL, D_FF)), _resident((D_FF, D_MODEL)), vec, vec, vec, vec],
        out_specs=[row, wide, wide, row, row, row, row, vec, vec, vec, vec,
                   pl.BlockSpec((SUBLANES, LANES), lambda i: (0, 0))],
        out_shape=[narrow(BF16), jax.ShapeDtypeStruct((t, D_FF), BF16), jax.ShapeDtypeStruct((t, D_FF), BF16),
                   narrow(BF16), narrow(F32), narrow(BF16), narrow(F32)]
                  + [jax.ShapeDtypeStruct((1, D_MODEL), F32)] * 4 + [jax.ShapeDtypeStruct((SUBLANES, LANES), F32)],
        compiler_params=_params("arbitrary"),
    )(cat, x, target, w_out, w_ff1, w_ff2, g1, b1, g2, b2)


def _gate_bwd(dcat, o, proj, gate_norm_w, conv_w, after):
    t = proj.shape[1]
    tb = min(t, 512)
    hb = tb // SUBLANES
    nblk = t // tb

    def body(do2_ref, dy_ref, dyn_ref, o_ref, og_ref, gnw_ref, b_ref, bn_ref, c_ref, u_ref, ch_ref, uh_ref, cw_ref,
             after_ref, do_ref, dp_ref, dgnw_ref, dcw_ref, zbuf, dbuf):
        i = pl.program_id(0)

        @pl.when(i == 0)
        def _():
            dgnw_ref[...] = jnp.zeros_like(dgnw_ref)
            dcw_ref[...] = jnp.zeros_like(dcw_ref)

        ov, og, gnw, do2 = o_ref[...], og_ref[...], gnw_ref[...], do2_ref[...]
        rs = _per_head(lambda o_h: jnp.broadcast_to(lax.rsqrt(jnp.mean(o_h * o_h, axis=-1, keepdims=True) + EPS),
                                                    o_h.shape), ov)
        on = ov * rs
        sg = _sigmoid(og)
        sil = og * sg
        don = do2 * gnw * sil
        dgnw_ref[...] += jnp.sum(do2 * on * sil, axis=0, keepdims=True)
        dp_ref[0] = (do2 * on * gnw * (sg * (1.0 + og * (1.0 - sg)))).astype(dp_ref.dtype)
        do_ref[...] = rs * (don - on * _per_head(
            lambda p_h: jnp.broadcast_to(jnp.mean(p_h, axis=-1, keepdims=True), p_h.shape), don * on))

        bg, cg, u, dy = b_ref[...], c_ref[...], u_ref[...], dy_ref[...]
        z = cg * u
        halo = jnp.where(i > 0, ch_ref[...] * uh_ref[...], 0.0)
        z1, z2 = _conv_taps(z, halo, zbuf, tb)
        cw = cw_ref[...]
        yc = cw[2:3, :] * z + cw[1:2, :] * z1 + cw[0:1, :] * z2
        dyc = dy * bg
        dbuf[0:tb, :] = dyc
        dbuf[tb:tb + SUBLANES, :] = jnp.where(i < nblk - 1, dyn_ref[...] * bn_ref[...], 0.0)
        d1, d2 = dbuf[1:1 + tb, :], dbuf[2:2 + tb, :]
        dz = cw[2:3, :] * dyc + cw[1:2, :] * d1 + cw[0:1, :] * d2
        dp_ref[1] = (dy * yc).astype(dp_ref.dtype)
        dp_ref[2] = (dz * u).astype(dp_ref.dtype)
        dp_ref[3] = (dz * cg).astype(dp_ref.dtype)
        dcw_ref[0:1, :] += jnp.sum(dyc * z2, axis=0, keepdims=True)
        dcw_ref[1:2, :] += jnp.sum(dyc * z1, axis=0, keepdims=True)
        dcw_ref[2:3, :] += jnp.sum(dyc * z, axis=0, keepdims=True)

    half = lambda g: pl.BlockSpec((tb, GROUP), lambda i: (i, g))
    grp = lambda g: pl.BlockSpec((None, tb, GROUP), lambda i: (g, i, 0))
    prev = lambda g: pl.BlockSpec((None, SUBLANES, GROUP), lambda i: (g, jnp.maximum(i * hb - 1, 0), 0))
    nxt_row = lambda i: jnp.minimum((i + 1) * hb, t // SUBLANES - 1)
    nxt = lambda g: pl.BlockSpec((None, SUBLANES, GROUP), lambda i: (g, nxt_row(i), 0))
    vec = lambda r: pl.BlockSpec((r, GROUP), lambda i: (0, 0))
    return pl.pallas_call(
        body, name="gate_bwd", grid=(nblk,),
        in_specs=[half(0), half(1), pl.BlockSpec((SUBLANES, GROUP), lambda i: (nxt_row(i), 1)), half(0), grp(3), vec(1),
                  grp(4), nxt(4), grp(5), grp(6), prev(5), prev(6), vec(3), ANY],
        out_specs=[half(0), pl.BlockSpec((4, tb, GROUP), lambda i: (0, i, 0)), vec(1), vec(3)],
        out_shape=[jax.ShapeDtypeStruct((t, HGRN_WIDTH), F32), jax.ShapeDtypeStruct((4, t, HGRN_WIDTH), BF16),
                   jax.ShapeDtypeStruct((1, HGRN_WIDTH), F32), jax.ShapeDtypeStruct((3, CONV_WIDTH), F32)],
        scratch_shapes=[pltpu.VMEM((tb + SUBLANES, GROUP), F32), pltpu.VMEM((tb + SUBLANES, GROUP), F32)],
        compiler_params=_params("arbitrary"),
    )(dcat, dcat, dcat, o, proj, gate_norm_w, proj, proj, proj, proj, proj, proj, conv_w, after)


def _hgrn_bwd(proj, do, states, lb_logits, after):
    t = proj.shape[1]
    tb = min(t, 512)
    ncb = tb // CHUNK
    nblk = t // tb

    def body(q_ref, f_ref, v_ref, do_ref, st_ref, lbl_ref, after_ref, dp_ref, dlbl_ref, ds_scr, dlb_scr):
        i = pl.program_id(0)

        @pl.when(i == 0)
        def _():
            ds_scr[...] = jnp.zeros_like(ds_scr)
            dlb_scr[...] = jnp.zeros_like(dlb_scr)

        lb, s1 = _lower_bound(lbl_ref[...])
        causal, anti = _chunk_masks()
        every = range(ncb)
        rows = [slice(c * CHUNK, (c + 1) * CHUNK) for c in every]
        q, v, do = ([ref[r, :] for r in rows] for ref in (q_ref, v_ref, do_ref))
        st = [st_ref[c] for c in every]
        gates = [_gates(f_ref[r, :], lb) for r in rows]
        sig, f, k = ([gt[n] for gt in gates] for n in (0, 1, 3))
        b = [_dot_exact(causal, gt[2]) for gt in gates]
        mid, last = [x[CHUNK // 2:CHUNK // 2 + 1, :] for x in b], [x[CHUNK - 1:CHUNK, :] for x in b]
        e_q = [jnp.exp(b[c] - mid[c]) for c in every]
        e_k = [jnp.exp(mid[c] - b[c]) for c in every]
        e_i = [jnp.exp(x) for x in b]
        e_s = [jnp.exp(last[c] - b[c]) for c in every]
        dec = [jnp.exp(x) for x in last]
        qt, kt, qi, ks = ([a[c] * e[c] for c in every] for a, e in ((q, e_q), (k, e_k), (q, e_i), (k, e_s)))

        def masked(a, b_):
            return [[jnp.where(causal, _dot(a_h, b_h, NT), 0.0) for a_h, b_h in zip(_heads(a[c]), _heads(b_[c]))]
                    for c in every]

        def with_scores(s, other, dims):
            return [jnp.concatenate([_dot(s_h, o_h, dims) for s_h, o_h in zip(s[c], _heads(other[c]))], axis=1)
                    for c in every]

        def per_head(dims, a, b_):
            return [_per_head(lambda a_h, b_h: _dot(a_h, b_h, dims), a[c], b_[c]) for c in every]

        scores, dscores = masked(qt, kt), masked(do, v)
        dqt, dkt, dv_intra = with_scores(dscores, kt, NN), with_scores(dscores, qt, TN), with_scores(scores, do, TN)
        dqi, update = per_head(NN, do, st), per_head(TN, do, qi)

        dst = ds_scr[...]
        dsts = [None] * ncb
        for c in reversed(every):
            dsts[c] = dst
            dst = dec[c] * dst + update[c]
        ds_scr[...] = dst

        dv_state, dks = per_head(NT, ks, dsts), per_head(NN, v, dsts)
        ddec = [jnp.sum(dsts[c] * st[c], axis=0, keepdims=True) for c in every]
        dq = [dqt[c] * e_q[c] + dqi[c] * e_i[c] for c in every]
        dk = [dkt[c] * e_k[c] + dks[c] * e_s[c] for c in every]
        db = [q[c] * dq[c] - k[c] * dk[c] for c in every]
        db_last = [jnp.sum(dks[c] * ks[c], axis=0, keepdims=True) + ddec[c] * dec[c] for c in every]
        dg = [_dot_exact(anti, db[c]) + db_last[c] for c in every]
        df = [dg[c] / f[c] - dk[c] for c in every]
        dlb_scr[...] += sum(jnp.sum(df[c] * (1.0 - sig[c]), axis=0, keepdims=True) for c in every)
        dfp = [df[c] * (1.0 - lb) * sig[c] * (1.0 - sig[c]) for c in every]
        dv = [dv_intra[c] + dv_state[c] for c in every]
        for n, parts in enumerate((dq, dfp, dv)):
            dp_ref[n] = jnp.concatenate(parts, axis=0).astype(dp_ref.dtype)

        @pl.when(i == nblk - 1)
        def _():
            dlb = dlb_scr[...]
            dlbl_ref[0:1, :] = dlb * lb * (1.0 - lb)
            dlbl_ref[1:2, :] = -dlb * lb * s1

    grp = lambda g: pl.BlockSpec((None, tb, GROUP), lambda i: (g, nblk - 1 - i, 0))
    vec = pl.BlockSpec((2, HGRN_WIDTH), lambda i: (0, 0))
    return pl.pallas_call(
        body, name="hgrn_bwd", grid=(nblk,),
        in_specs=[grp(0), grp(1), grp(2), pl.BlockSpec((tb, HGRN_WIDTH), lambda i: (nblk - 1 - i, 0)),
                  pl.BlockSpec((ncb, HEAD_DIM, HGRN_WIDTH), lambda i: (nblk - 1 - i, 0, 0)), vec, ANY],
        out_specs=[pl.BlockSpec((3, tb, HGRN_WIDTH), lambda i: (0, nblk - 1 - i, 0)), vec],
        out_shape=[jax.ShapeDtypeStruct((3, t, HGRN_WIDTH), BF16), jax.ShapeDtypeStruct((2, HGRN_WIDTH), F32)],
        scratch_shapes=[pltpu.VMEM((HEAD_DIM, HGRN_WIDTH), F32), pltpu.VMEM((1, HGRN_WIDTH), F32)],
        compiler_params=_params("arbitrary"),
    )(proj, proj, proj, do, states, lb_logits, after)


def _in_bwd(dph, dpg, w_in, dpre1, after):
    t = dpre1.shape[0]
    tm = min(t, 512)

    def body(dh_ref, dg_ref, w_ref, dp_ref, after_ref, o_ref):
        acc = ALPHA * dp_ref[...]
        for g in range(N_GROUPS):
            part = dh_ref[g] if g < 3 else dg_ref[g - 3]
            acc = acc + _dot(part, w_ref[:, g * GROUP:(g + 1) * GROUP], NT)
        o_ref[...] = acc

    row = pl.BlockSpec((tm, D_MODEL), lambda i: (i, 0))
    return pl.pallas_call(
        body, name="in_bwd", grid=(t // tm,),
        in_specs=[pl.BlockSpec((3, tm, GROUP), lambda i: (0, i, 0)), pl.BlockSpec((4, tm, GROUP), lambda i: (0, i, 0)),
                  _resident((D_MODEL, IN_COLS)), row, ANY],
        out_specs=row,
        out_shape=jax.ShapeDtypeStruct((t, D_MODEL), F32),
        compiler_params=_params("parallel"),
    )(dph, dpg, w_in, dpre1, after)


def _grad_w(name, operands, widths, shape, step, after=None):
    t = operands[0].shape[-2]
    tt = min(t, 1024)
    n_in, n_steps = len(operands), t // tt
    in_specs = [pl.BlockSpec((tt, w), lambda k: (k, 0)) if a.ndim == 2 else
                pl.BlockSpec((a.shape[0], tt, w), lambda k: (0, k, 0)) for a, w in zip(operands, widths)]
    extra = [] if after is None else [after]

    def body(*refs):
        o_ref, acc, sem = refs[-3:]
        k = pl.program_id(0)

        @pl.when(k == 0)
        def _():
            acc[...] = jnp.zeros_like(acc)

        step(acc, *refs[:n_in])

        @pl.when(k == n_steps - 1)
        def _():
            out = pltpu.make_async_copy(acc, o_ref, sem)
            out.start()
            out.wait()

    return pl.pallas_call(
        body, name=name, grid=(n_steps,), in_specs=in_specs + [ANY] * len(extra), out_specs=ANY,
        out_shape=jax.ShapeDtypeStruct(shape, F32),
        scratch_shapes=[pltpu.VMEM(shape, F32), pltpu.SemaphoreType.DMA],
        compiler_params=_params("arbitrary"),
    )(*operands, *extra)


def _dw_in(xb, dph, dpg, after):
    def step(acc, x_ref, dh_ref, dg_ref):
        xv = x_ref[...]
        for g in range(N_GROUPS):
            part = dh_ref[g] if g < 3 else dg_ref[g - 3]
            acc[:, g * GROUP:(g + 1) * GROUP] += _dot(xv, part, TN)

    return _grad_w("dw_in", (xb, dph, dpg), (D_MODEL, GROUP, GROUP), (D_MODEL, IN_COLS), step, after)


def _dw_out(cat, dpre1b):
    def step(acc, cat_ref, d_ref):
        dv = d_ref[...]
        for g in range(2):
            acc[g * GROUP:(g + 1) * GROUP, :] += _dot(cat_ref[g], dv, TN)

    return _grad_w("dw_out", (cat, dpre1b), (GROUP, D_MODEL), (D_MODEL, D_MODEL), step)


def _dw_ff1(h1b, da):
    def step(acc, h_ref, da_ref):
        hv = h_ref[...]
        for j in range(D_FF // FF_BLOCK):
            cols = slice(j * FF_BLOCK, (j + 1) * FF_BLOCK)
            acc[:, cols] += _dot(hv, da_ref[:, cols], TN)

    return _grad_w("dw_ff1", (h1b, da), (D_MODEL, D_FF), (D_MODEL, D_FF), step)


def _dw_ff2(r, dpre2b):
    def step(acc, r_ref, d_ref):
        dv = d_ref[...]
        for j in range(D_FF // FF_BLOCK):
            rows = slice(j * FF_BLOCK, (j + 1) * FF_BLOCK)
            acc[rows, :] += _dot(r_ref[:, rows], dv, TN)

    return _grad_w("dw_ff2", (r, dpre2b), (D_FF, D_MODEL), (D_FF, D_MODEL), step)


def _place():
    x, y, c = lax.axis_index("x"), lax.axis_index("y"), lax.axis_index("c")
    return x, y, c, 2 * x + y


def _other_chips(x, y):
    return [(1 - x, y), (x, 1 - y), (1 - x, 1 - y)]


def _place_shard(name, w, chip, cols_sharded, after=None):
    rows, cols = w.shape
    tr = min(rows, 256)
    nb = rows // tr
    full = (rows, cols * N_CHIPS) if cols_sharded else (rows * N_CHIPS, cols)
    out_map = (lambda i, s: (i, s[0])) if cols_sharded else (lambda i, s: (s[0] * nb + i, 0))

    def body(s_ref, w_ref, *rest):
        rest[-1][...] = w_ref[...].astype(rest[-1].dtype)

    extra = [] if after is None else [after]
    return pl.pallas_call(
        body, name=name,
        grid_spec=pltpu.PrefetchScalarGridSpec(
            num_scalar_prefetch=1, grid=(nb,),
            in_specs=[pl.BlockSpec((tr, cols), lambda i, s: (i, 0))] + [ANY] * len(extra),
            out_specs=pl.BlockSpec((tr, cols), out_map)),
        out_shape=jax.ShapeDtypeStruct(full, BF16),
        compiler_params=_params("parallel"),
    )(chip, w, *extra)


HBM = pl.BlockSpec(memory_space=pltpu.HBM)
SEM = pl.BlockSpec(memory_space=pltpu.SEMAPHORE)
EFFECT = pltpu.SideEffectType.DATAFLOW_SIDE_EFFECTING


class _Split:
    def __init__(self, name, arrays, plan):
        n, n_copies = len(arrays), plan.count
        self.name, self.plan, self.n = name, plan, n

        def body(*refs):
            send_sems, recv_sems, token = refs[n], refs[n + 1], refs[-1]
            for k, (src, dst, to) in enumerate(plan(refs[:n])):
                pltpu.make_async_remote_copy(src_ref=src, dst_ref=dst, send_sem=send_sems.at[k], recv_sem=recv_sems.at[k],
                                             device_id=to, device_id_type=MESH).start()
            token[...] = jnp.zeros_like(token)

        outs = pl.pallas_call(
            body, name=name + "_start",
            out_shape=(pltpu.SemaphoreType.DMA((n_copies,)), pltpu.SemaphoreType.DMA((n_copies,)),
                       *[pltpu.HBM(a.shape, a.dtype) for a in arrays], jax.ShapeDtypeStruct((SUBLANES, LANES), F32)),
            in_specs=(HBM,) * n, out_specs=(SEM, SEM) + (HBM,) * n + (pl.BlockSpec(memory_space=pltpu.VMEM),),
            input_output_aliases={i: 2 + i for i in range(n)},
            compiler_params=pltpu.CompilerParams(has_side_effects=EFFECT),
        )(*[pltpu.with_memory_space_constraint(a, pltpu.HBM) for a in arrays])
        self.sems, self.arrays, self.token = outs[:2], outs[2:2 + n], outs[-1]

    def wait(self, after):
        n, plan = self.n, self.plan

        def body(*refs):
            send_sems, recv_sems = refs[n], refs[n + 1]
            for k, (src, dst, to) in enumerate(plan(refs[:n])):
                cp = pltpu.make_async_remote_copy(src_ref=src, dst_ref=dst, send_sem=send_sems.at[k],
                                                  recv_sem=recv_sems.at[k], device_id=to, device_id_type=MESH)
                cp.wait_send()
                cp.wait_recv()

        return pl.pallas_call(
            body, name=self.name + "_wait", out_shape=tuple(pltpu.HBM(a.shape, a.dtype) for a in self.arrays),
            in_specs=(HBM,) * n + (SEM, SEM, ANY), out_specs=(HBM,) * n, input_output_aliases={i: i for i in range(n)},
            compiler_params=pltpu.CompilerParams(has_side_effects=EFFECT),
        )(*self.arrays, *self.sems, after)


COLS_SHARDED = (True, False, True, False)
HALF_SHAPES = [(D_MODEL // 2, IN_COLS), (D_MODEL, D_MODEL // 2), (D_MODEL // 2, D_FF), (D_FF, D_MODEL // 2)]
PIECE_SHAPES = [(D_MODEL // 2, IN_COLS // N_CHIPS), (D_MODEL // N_CHIPS, D_MODEL // 2),
                (D_MODEL // 2, D_FF // N_CHIPS), (D_FF // N_CHIPS, D_MODEL // 2)]


def _shard_view(kind, ref, chip):
    if COLS_SHARDED[kind]:
        n = ref.shape[1] // N_CHIPS
        return ref.at[:, pl.ds(chip * n, n)]
    n = ref.shape[0] // N_CHIPS
    return ref.at[pl.ds(chip * n, n), :]


def _half_view(kind, ref, h):
    if COLS_SHARDED[kind]:
        n = ref.shape[0] // 2
        return ref.at[pl.ds(h * n, n), :]
    n = ref.shape[1] // 2
    return ref.at[:, pl.ds(h * n, n)]


def _plan(count):
    def mark(fn):
        fn.count = count
        return fn
    return mark


def _shard_half_view(kind, ref, chip, h):
    if COLS_SHARDED[kind]:
        m, n = ref.shape[0] // 2, ref.shape[1] // N_CHIPS
        return ref.at[pl.ds(h * m, m), pl.ds(chip * n, n)]
    m = ref.shape[0] // N_CHIPS // 2
    return ref.at[pl.ds((2 * chip + h) * m, m), :]


def _gather_over_ici(kinds, weights):
    @_plan(3 * len(kinds))
    def plan(refs):
        x, y, c, me = _place()
        mine = [_shard_half_view(kind, ref, me, c) for kind, ref in zip(kinds, refs)]
        return [(v, v, (px, py, c)) for v in mine for px, py in _other_chips(x, y)]

    return _Split("gather_ici_" + "".join(map(str, kinds)), tuple(weights), plan)


def _gather_w_in_over_ici(w_in, conv4):
    @_plan(6)
    def plan(refs):
        x, y, c, me = _place()
        half, conv = _shard_half_view(0, refs[0], me, c), refs[1].at[me]
        return [(v, v, (px, py, c)) for v in (half, conv) for px, py in _other_chips(x, y)]

    return _Split("gather_w_in_ici", (w_in, conv4), plan)


def _gather_over_d2d(kinds, weights):
    @_plan(3 * len(kinds))
    def plan(refs):
        x, y, c, _ = _place()
        got = [_shard_half_view(kind, ref, 2 * px + py, c) for kind, ref in zip(kinds, refs)
               for px, py in _other_chips(x, y)]
        return [(v, v, (x, y, 1 - c)) for v in got]

    return _Split("gather_d2d_" + "".join(map(str, kinds)), tuple(weights), plan)


def _swap_halves(kinds, grads):
    @_plan(len(kinds))
    def plan(refs):
        x, y, c, _ = _place()
        return [(_half_view(kind, g, 1 - c), land, (x, y, 1 - c))
                for kind, g, land in zip(kinds, refs[:len(kinds)], refs[len(kinds):])]

    lands = [lax.empty(HALF_SHAPES[kind], F32) for kind in kinds]
    return _Split("swap_halves_" + "".join(map(str, kinds)), (*grads, *lands), plan)


def _add_half(name, g, recv, core, rows_split):
    shape = recv.shape
    tr = min(shape[0], 128 if rows_split else 256)
    nb = shape[0] // tr

    def body(c_ref, g_ref, r_ref, o_ref):
        o_ref[...] = (g_ref[...] + r_ref[...]).astype(o_ref.dtype)

    g_map = (lambda i, c_ref: (c_ref[0] * nb + i, 0)) if rows_split else (lambda i, c_ref: (i, c_ref[0]))
    blk = pl.BlockSpec((tr, shape[1]), lambda i, c_ref: (i, 0))
    return pl.pallas_call(
        body, name=name,
        grid_spec=pltpu.PrefetchScalarGridSpec(
            num_scalar_prefetch=1, grid=(nb,),
            in_specs=[pl.BlockSpec((tr, shape[1]), g_map), blk], out_specs=blk),
        out_shape=jax.ShapeDtypeStruct(shape, BF16),
        compiler_params=_params("parallel"),
    )(core, g, recv)


def _exchange_pieces(kinds, halves):
    n_p = N_CHIPS - 1

    @_plan(n_p * len(kinds))
    def plan(refs):
        x, y, c, _ = _place()
        return [(_shard_view(kind, half, 2 * px + py), land.at[j], (px, py, c))
                for j, (px, py) in enumerate(_other_chips(x, y))
                for kind, half, land in zip(kinds, refs[:len(kinds)], refs[len(kinds):])]

    lands = [lax.empty((n_p,) + PIECE_SHAPES[kind], BF16) for kind in kinds]
    return _Split("exchange_pieces_" + "".join(map(str, kinds)), (*halves, *lands), plan)


def _sum_pieces(name, half, slots, place, rows_split):
    n_p, rows, cols = slots.shape
    tr = min(rows, 256)
    nb = rows // tr
    if rows_split:
        own_map = lambda i, s: (i, s[0])
        out_map = lambda i, s: (s[1] * nb + i, 0)
        shard = (2 * rows, cols)
    else:
        own_map = lambda i, s: (s[0] * nb + i, 0)
        out_map = lambda i, s: (i, s[1])
        shard = (rows, 2 * cols)

    def body(s_ref, own_ref, slot_ref, o_ref):
        total = own_ref[...].astype(F32)
        for j in range(n_p):
            total = total + slot_ref[j].astype(F32)
        o_ref[...] = total

    return pl.pallas_call(
        body, name=name,
        grid_spec=pltpu.PrefetchScalarGridSpec(
            num_scalar_prefetch=1, grid=(nb,),
            in_specs=[pl.BlockSpec((tr, cols), own_map), pl.BlockSpec((n_p, tr, cols), lambda i, s: (0, i, 0))],
            out_specs=pl.BlockSpec((tr, cols), out_map)),
        out_shape=jax.ShapeDtypeStruct(shard, F32),
        compiler_params=_params("parallel"),
    )(place, half, slots)


def _join_halves(kinds, shards):
    @_plan(len(kinds))
    def plan(refs):
        x, y, c, _ = _place()
        return [(_half_view(kind, g, c), _half_view(kind, g, c), (x, y, 1 - c)) for kind, g in zip(kinds, refs)]

    return _Split("join_halves_" + "".join(map(str, kinds)), tuple(shards), plan)


N_DEV = 8


def _share_small(pack, after):
    @_plan(N_DEV - 1)
    def plan(refs):
        x, y, c, _ = _place()
        me = 4 * x + 2 * y + c
        peers = [((1 - x) if m & 4 else x, (1 - y) if m & 2 else y, (1 - c) if m & 1 else c) for m in range(1, N_DEV)]
        return [(refs[0], refs[1].at[me], peer) for peer in peers]

    return _Split("share_small", (pack, lax.empty((N_DEV,) + pack.shape, F32), after), plan)


def _sum_shared(pack, land, device):
    def body(d_ref, p_ref, l_ref, o_ref):
        me = d_ref[0]
        total = jnp.where(me == 0, p_ref[...], l_ref[0])
        for d in range(1, N_DEV):
            total = total + jnp.where(me == d, p_ref[...], l_ref[d])
        o_ref[...] = total

    return pl.pallas_call(
        body, name="sum_shared",
        grid_spec=pltpu.PrefetchScalarGridSpec(
            num_scalar_prefetch=1, grid=(1,),
            in_specs=[pl.BlockSpec(pack.shape, lambda i, d: (0, 0)), pl.BlockSpec(land.shape, lambda i, d: (0, 0, 0))],
            out_specs=pl.BlockSpec(pack.shape, lambda i, d: (0, 0))),
        out_shape=jax.ShapeDtypeStruct(pack.shape, F32),
    )(device, pack, land)


def _adamw(name, w, g, m, v, after=None):
    rows, cols = w.shape
    tr = min(rows, 256)
    extra = [] if after is None else [after]

    def body(w_ref, g_ref, m_ref, v_ref, *rest):
        d_ref, nm_ref, nv_ref = rest[-3:]
        gv = g_ref[...]
        nm = ADAM_B1 * m_ref[...] + (1.0 - ADAM_B1) * gv
        nv = ADAM_B2 * v_ref[...] + (1.0 - ADAM_B2) * jnp.square(gv)
        m_hat = nm * (1.0 / (1.0 - ADAM_B1 ** ADAM_STEP))
        v_hat = nv * (1.0 / (1.0 - ADAM_B2 ** ADAM_STEP))
        d_ref[...] = -ADAM_LR * (m_hat / (jnp.sqrt(v_hat) + ADAM_EPS) + ADAM_WD * w_ref[...])
        nm_ref[...] = nm
        nv_ref[...] = nv

    blk = pl.BlockSpec((tr, cols), lambda i: (i, 0))
    return pl.pallas_call(
        body, name=name, grid=(rows // tr,), in_specs=[blk] * 4 + [ANY] * len(extra), out_specs=[blk] * 3,
        out_shape=[jax.ShapeDtypeStruct(w.shape, F32)] * 3,
        compiler_params=_params("parallel"),
    )(w, g, m, v, *extra)


def kernel(x, w_in, lb_logits, gate_norm_w, conv_w, w_out, ln1_g, ln1_b, w_ff1, w_ff2, ln2_g, ln2_b, loss_target, m_w_in, m_lb_logits, m_gate_norm_w, m_conv_w, m_w_out, m_ln1_g, m_ln1_b, m_w_ff1, m_w_ff2, m_ln2_g, m_ln2_b, v_w_in, v_lb_logits, v_gate_norm_w, v_conv_w, v_w_out, v_ln1_g, v_ln1_b, v_w_ff1, v_w_ff2, v_ln2_g, v_ln2_b):
    xs, tgt = x[0], loss_target[0]
    chip = 2 * lax.axis_index("x") + lax.axis_index("y")
    core = lax.axis_index("c").astype(jnp.int32).reshape(1)
    chip1 = chip.astype(jnp.int32).reshape(1)
    place = jnp.concatenate([chip1, core])

    conv4 = lax.dynamic_update_slice(jnp.zeros((N_CHIPS,) + conv_w.shape[1:], F32), conv_w, (chip, 0, 0))
    ici_in = _gather_w_in_over_ici(_place_shard("place_w_in", w_in[0], chip1, True), conv4)
    ici_out = _gather_over_ici((1,), (_place_shard("place_w_out", w_out[0], chip1, False, after=ici_in.token),))
    ici_ff = _gather_over_ici((2, 3), (_place_shard("place_w_ff1", w_ff1[0], chip1, True, after=ici_in.token),
                                       _place_shard("place_w_ff2", w_ff2[0], chip1, False, after=ici_out.token)))
    wb_in, cv4 = ici_in.wait(ici_ff.token)
    d2d_in = _gather_over_d2d((0,), (wb_in,))
    wb_in, = d2d_in.wait(d2d_in.token)
    conv_full = cv4.transpose(1, 0, 2).reshape(3, CONV_WIDTH)

    proj, xb = _in_proj(xs, wb_in, ici_ff.token)
    o, states = _hgrn_fwd(proj, lb_logits)
    d2d_rest = _gather_over_d2d((1, 2, 3), (*ici_out.wait(o), *ici_ff.wait(o)))
    cat = _gate_fwd(proj, o, gate_norm_w, conv_full, d2d_rest.token)
    wb_out, wb_ff1, wb_ff2 = d2d_rest.wait(cat)

    (h1b, r, da, dpre2b, dpre1, dpre1b, dcat, g_ln1_g, g_ln1_b, g_ln2_g, g_ln2_b, loss8) = _sublayers(
        cat, xs, tgt, wb_out, wb_ff1, wb_ff2, ln1_g, ln1_b, ln2_g, ln2_b)

    names = ("w_in", "w_out", "w_ff1", "w_ff2")

    def add_halves(kinds, grads, lands):
        return [_add_half("add_half_" + names[k], g, ld, core, COLS_SHARDED[k]) for k, g, ld in zip(kinds, grads, lands)]

    def sum_pieces(kinds, halves, lands):
        return [_sum_pieces("sum_pieces_" + names[k], h, ld, place, COLS_SHARDED[k]) for k, h, ld in zip(kinds, halves, lands)]

    early = (1, 2, 3)
    swap = _swap_halves(early, (_dw_out(cat, dpre1b), _dw_ff1(h1b, da), _dw_ff2(r, dpre2b)))
    do, dpg, g_gnw, g_conv = _gate_bwd(dcat, o, proj, gate_norm_w, conv_full, swap.token)
    swapped = swap.wait(do)
    exch = _exchange_pieces(early, add_halves(early, swapped[:3], swapped[3:]))
    dph, g_lbl = _hgrn_bwd(proj, do, states, lb_logits, exch.token)
    g_in_local = _dw_in(xb, dph, dpg, dph)

    late = (0,)
    swap = _swap_halves(late, (g_in_local,))
    grad_x = _in_bwd(dph, dpg, wb_in, dpre1, swap.token)
    exchanged = exch.wait(grad_x)
    pack = jnp.concatenate([
        g_ln1_g, g_ln1_b, g_ln2_g, g_ln2_b,
        jnp.concatenate([g_lbl[0:1], g_lbl[1:2]], axis=1),
        jnp.concatenate([g_gnw, g_conv[0:1]], axis=1),
        jnp.concatenate([g_conv[1:2], g_conv[2:3]], axis=1),
        jnp.concatenate([loss8[0:1], jnp.zeros((1, D_MODEL - LANES), F32)], axis=1)], axis=0)
    small = _share_small(pack, exchanged[0])
    swapped = swap.wait(small.token)
    exch = _exchange_pieces(late, add_halves(late, swapped[:1], swapped[1:]))
    shared = small.wait(exch.token)
    join = _join_halves(early, sum_pieces(early, (shared[2], *exchanged[1:3]), exchanged[3:]))
    tot = _sum_shared(shared[0], shared[1], 2 * chip1 + core)
    loss = tot[7, 0]
    half = D_MODEL // 2
    g_lb_logits = jnp.concatenate([tot[4:5, :half], tot[4:5, half:]], axis=0)
    g_gate_norm_w = tot[5:6, :half]
    g_conv_full = jnp.concatenate([tot[5:6, half:], tot[6:7, :half], tot[6:7, half:]], axis=0)
    g_conv_w = lax.dynamic_slice(g_conv_full, (0, chip * LANES), (3, LANES))
    g_w_out, g_w_ff1, g_w_ff2 = join.wait(tot)
    d_ff1, nm_ff1, nv_ff1 = _adamw("adamw_w_ff1", w_ff1[0], g_w_ff1, m_w_ff1[0], v_w_ff1[0], exch.token)
    d_ff2, nm_ff2, nv_ff2 = _adamw("adamw_w_ff2", w_ff2[0], g_w_ff2, m_w_ff2[0], v_w_ff2[0], d_ff1)
    d_out, nm_out, nv_out = _adamw("adamw_w_out", w_out[0], g_w_out, m_w_out[0], v_w_out[0], d_ff2)

    def small_pack(lbl, gnw, cv, l1g, l1b, l2g, l2b):
        pad = jnp.zeros((1, D_MODEL - 3 * LANES), F32)
        return jnp.concatenate([
            l1g, l1b, l2g, l2b, jnp.concatenate([lbl[0:1], lbl[1:2]], axis=1),
            jnp.concatenate([gnw, jnp.zeros((1, half), F32)], axis=1),
            jnp.concatenate([cv[0:1], cv[1:2], cv[2:3], pad], axis=1), jnp.zeros((1, D_MODEL), F32)], axis=0)

    w_s = small_pack(lb_logits, gate_norm_w, conv_w[0], ln1_g, ln1_b, ln2_g, ln2_b)
    g_s = small_pack(g_lb_logits, g_gate_norm_w, g_conv_w, tot[0:1], tot[1:2], tot[2:3], tot[3:4])
    m_s = small_pack(m_lb_logits, m_gate_norm_w, m_conv_w[0], m_ln1_g, m_ln1_b, m_ln2_g, m_ln2_b)
    v_s = small_pack(v_lb_logits, v_gate_norm_w, v_conv_w[0], v_ln1_g, v_ln1_b, v_ln2_g, v_ln2_b)
    d_s, nm_s, nv_s = _adamw("adamw_small", w_s, g_s, m_s, v_s, d_out)
    exchanged = exch.wait(d_s)
    join = _join_halves(late, sum_pieces(late, exchanged[:1], exchanged[1:]))
    g_w_in, = join.wait(join.token)
    d_in, nm_in, nv_in = _adamw("adamw_w_in", w_in[0], g_w_in, m_w_in[0], v_w_in[0])

    def unpack(p):
        lbl = jnp.concatenate([p[4:5, :half], p[4:5, half:]], axis=0)
        cv = jnp.concatenate([p[6:7, 0:LANES], p[6:7, LANES:2 * LANES], p[6:7, 2 * LANES:3 * LANES]], axis=0)
        return dict(lb_logits=lbl, gate_norm_w=p[5:6, :half], conv_w=cv[None], ln1_g=p[0:1], ln1_b=p[1:2],
                    ln2_g=p[2:3], ln2_b=p[3:4])

    order = ("w_in", "lb_logits", "gate_norm_w", "conv_w", "w_out", "ln1_g", "ln1_b", "w_ff1", "w_ff2", "ln2_g", "ln2_b")
    grad = dict(unpack(g_s), w_in=g_w_in[None], w_out=g_w_out[None], w_ff1=g_w_ff1[None], w_ff2=g_w_ff2[None])
    delta = dict(unpack(d_s), w_in=d_in[None], w_out=d_out[None], w_ff1=d_ff1[None], w_ff2=d_ff2[None])
    new_m = dict(unpack(nm_s), w_in=nm_in[None], w_out=nm_out[None], w_ff1=nm_ff1[None], w_ff2=nm_ff2[None])
    new_v = dict(unpack(nv_s), w_in=nv_in[None], w_out=nv_out[None], w_ff1=nv_ff1[None], w_ff2=nv_ff2[None])
    return (loss, grad_x[None], *[grad[n] for n in order], *[delta[n] for n in order],
            *[new_m[n] for n in order], *[new_v[n] for n in order])
```

```python
import jax
import jax.numpy as jnp
from jax import lax
from jax.experimental import pallas as pl
from jax.experimental.pallas import tpu as pltpu

F32 = jnp.float32
BF16 = jnp.bfloat16
MXU_DTYPE = jnp.bfloat16

D_MODEL = 1024
HGRN_WIDTH = 512
HEAD_DIM = 128
N_HEADS = 4
CONV_WIDTH = 512
CHUNK = 64
D_FF = 4096
IN_COLS = 3584
GROUP = 512
N_GROUPS = IN_COLS // GROUP
ALPHA = 2.0 ** 0.25
EPS = 1e-5
N_CHIPS = 4
ADAM_LR, ADAM_B1, ADAM_B2, ADAM_EPS, ADAM_WD, ADAM_STEP = 0.001, 0.9, 0.999, 1e-08, 0.01, 10

LANES = 128
SUBLANES = 8
VMEM_LIMIT = 56 * 1024 * 1024
FF_BLOCK = 1024
N_FF = D_FF // FF_BLOCK

NN = (((1,), (0,)), ((), ()))
NT = (((1,), (1,)), ((), ()))
TN = (((0,), (0,)), ((), ()))
MESH = pl.DeviceIdType.MESH
ANY = pl.BlockSpec(memory_space=pl.ANY)


def _dot(a, b, dims):
    return lax.dot_general(a.astype(MXU_DTYPE), b.astype(MXU_DTYPE), dims, preferred_element_type=F32)


def _dot_exact(ones, v):
    ones = ones.astype(jnp.bfloat16)
    hi = v.astype(jnp.bfloat16)
    rest = v - hi.astype(F32)
    mid = rest.astype(jnp.bfloat16)
    low = (rest - mid.astype(F32)).astype(jnp.bfloat16)
    return sum(lax.dot_general(ones, part, NN, preferred_element_type=F32) for part in (hi, mid, low))


def _params(*sem):
    return pltpu.CompilerParams(dimension_semantics=sem, vmem_limit_bytes=VMEM_LIMIT)


def _resident(shape):
    return pl.BlockSpec(shape, lambda *_: (0,) * len(shape), pipeline_mode=pl.Buffered(1))


def _sigmoid(v):
    return 1.0 / (1.0 + jnp.exp(-v))


def _lower_bound(lbl):
    m = jnp.max(lbl, axis=0, keepdims=True)
    e = jnp.exp(lbl - m)
    s = e / jnp.sum(e, axis=0, keepdims=True)
    return s[0:1, :], s[1:2, :]


def _heads(v):
    return [v[:, h * HEAD_DIM:(h + 1) * HEAD_DIM] for h in range(N_HEADS)]


def _per_head(fn, *arrays):
    return jnp.concatenate([fn(*parts) for parts in zip(*map(_heads, arrays))], axis=1)


def _in_proj(x, w_in, after):
    t = x.shape[0]
    tm = min(t, 512)

    def body(x_ref, w_ref, after_ref, o_ref, xb_ref):
        xb = x_ref[...].astype(xb_ref.dtype)
        xb_ref[...] = xb
        for g in range(N_GROUPS):
            o_ref[g] = _dot(xb, w_ref[:, g * GROUP:(g + 1) * GROUP], NN)

    return pl.pallas_call(
        body, name="in_proj", grid=(t // tm,),
        in_specs=[pl.BlockSpec((tm, D_MODEL), lambda i: (i, 0)), _resident((D_MODEL, IN_COLS)), ANY],
        out_specs=[pl.BlockSpec((N_GROUPS, tm, GROUP), lambda i: (0, i, 0)), pl.BlockSpec((tm, D_MODEL), lambda i: (i, 0))],
        out_shape=[jax.ShapeDtypeStruct((N_GROUPS, t, GROUP), F32), jax.ShapeDtypeStruct((t, D_MODEL), BF16)],
        compiler_params=_params("parallel"),
    )(x, w_in, after)


def _gates(fp, lb):
    sig = _sigmoid(fp)
    f = lb + (1.0 - lb) * sig
    return sig, f, jnp.log(f), 1.0 - f


def _chunk_masks():
    row = lax.broadcasted_iota(jnp.int32, (CHUNK, CHUNK), 0)
    col = lax.broadcasted_iota(jnp.int32, (CHUNK, CHUNK), 1)
    return row >= col, row <= col


def _hgrn_fwd(proj, lb_logits, after):
    t = proj.shape[1]
    tb = min(t, 512)
    ncb = tb // CHUNK

    def body(q_ref, f_ref, v_ref, lbl_ref, after_ref, o_ref, st_ref, s_scr):
        @pl.when(pl.program_id(0) == 0)
        def _():
            s_scr[...] = jnp.zeros_like(s_scr)

        lb, _ = _lower_bound(lbl_ref[...])
        causal, _ = _chunk_masks()

        every = range(ncb)
        rows = [slice(c * CHUNK, (c + 1) * CHUNK) for c in every]
        q, v = [q_ref[r, :] for r in rows], [v_ref[r, :] for r in rows]
        gates = [_gates(f_ref[r, :], lb) for r in rows]
        k = [gt[3] for gt in gates]
        b = [_dot_exact(causal, gt[2]) for gt in gates]
        mid, last = [x[CHUNK // 2:CHUNK // 2 + 1, :] for x in b], [x[CHUNK - 1:CHUNK, :] for x in b]
        qt = [q[c] * jnp.exp(b[c] - mid[c]) for c in every]
        kt = [k[c] * jnp.exp(mid[c] - b[c]) for c in every]
        qi = [q[c] * jnp.exp(b[c]) for c in every]
        ks = [k[c] * jnp.exp(last[c] - b[c]) for c in every]
        dec = [jnp.exp(x) for x in last]
        scores = [[jnp.where(causal, _dot(a, b_, NT), 0.0) for a, b_ in zip(_heads(qt[c]), _heads(kt[c]))] for c in every]
        intra = [[_dot(s, v_h, NN) for s, v_h in zip(scores[c], _heads(v[c]))] for c in every]
        update = [_per_head(lambda v_h, ks_h: _dot(v_h, ks_h, TN), v[c], ks[c]) for c in every]

        st = s_scr[...]
        states = []
        for c in every:
            states.append(st)
            st_ref[c] = st
            st = dec[c] * st + update[c]
        s_scr[...] = st

        o_ref[...] = jnp.concatenate(
            [jnp.concatenate([i_h + _dot(qi_h, st_h, NT) for i_h, qi_h, st_h in
                              zip(intra[c], _heads(qi[c]), _heads(states[c]))], axis=1) for c in every], axis=0)

    grp = lambda g: pl.BlockSpec((None, tb, GROUP), lambda i: (g, i, 0))
    return pl.pallas_call(
        body, name="hgrn_fwd", grid=(t // tb,),
        in_specs=[grp(0), grp(1), grp(2), pl.BlockSpec((2, HGRN_WIDTH), lambda i: (0, 0)), ANY],
        out_specs=[pl.BlockSpec((tb, HGRN_WIDTH), lambda i: (i, 0)),
                   pl.BlockSpec((ncb, HEAD_DIM, HGRN_WIDTH), lambda i: (i, 0, 0))],
        out_shape=[jax.ShapeDtypeStruct((t, HGRN_WIDTH), F32),
                   jax.ShapeDtypeStruct((t // CHUNK, HEAD_DIM, HGRN_WIDTH), F32)],
        scratch_shapes=[pltpu.VMEM((HEAD_DIM, HGRN_WIDTH), F32)],
        compiler_params=_params("arbitrary"),
    )(proj, proj, proj, lb_logits, after)


def _conv_taps(z, halo, zbuf, tb):
    zbuf[0:SUBLANES, :] = halo
    zbuf[SUBLANES:SUBLANES + tb, :] = z
    return zbuf[SUBLANES - 1:SUBLANES - 1 + tb, :], zbuf[SUBLANES - 2:SUBLANES - 2 + tb, :]


def _gate_fwd(proj, o, gate_norm_w, conv_w, after):
    t = proj.shape[1]
    tb = min(t, 512)
    hb = tb // SUBLANES

    def body(o_ref, og_ref, gnw_ref, b_ref, c_ref, u_ref, ch_ref, uh_ref, cw_ref, after_ref, cat_ref, zbuf):
        i = pl.program_id(0)
        og = og_ref[...]
        on = _per_head(lambda o_h: o_h * lax.rsqrt(jnp.mean(o_h * o_h, axis=-1, keepdims=True) + EPS), o_ref[...])
        cat_ref[0] = (on * gnw_ref[...] * (og * _sigmoid(og))).astype(cat_ref.dtype)
        z = c_ref[...] * u_ref[...]
        halo = jnp.where(i > 0, ch_ref[...] * uh_ref[...], 0.0)
        z1, z2 = _conv_taps(z, halo, zbuf, tb)
        cw = cw_ref[...]
        yc = cw[2:3, :] * z + cw[1:2, :] * z1 + cw[0:1, :] * z2
        cat_ref[1] = (b_ref[...] * yc).astype(cat_ref.dtype)

    grp = lambda g: pl.BlockSpec((None, tb, GROUP), lambda i: (g, i, 0))
    prev = lambda g: pl.BlockSpec((None, SUBLANES, GROUP), lambda i: (g, jnp.maximum(i * hb - 1, 0), 0))
    vec = lambda r: pl.BlockSpec((r, GROUP), lambda i: (0, 0))
    return pl.pallas_call(
        body, name="gate_fwd", grid=(t // tb,),
        in_specs=[pl.BlockSpec((tb, GROUP), lambda i: (i, 0)), grp(3), vec(1), grp(4), grp(5), grp(6), prev(5), prev(6),
                  vec(3), ANY],
        out_specs=pl.BlockSpec((2, tb, GROUP), lambda i: (0, i, 0)),
        out_shape=jax.ShapeDtypeStruct((2, t, HGRN_WIDTH), BF16),
        scratch_shapes=[pltpu.VMEM((tb + SUBLANES, GROUP), F32)],
        compiler_params=_params("parallel"),
    )(o, proj, gate_norm_w, proj, proj, proj, proj, proj, conv_w, after)


def _ln_bwd(dy, xhat, rstd, g):
    dxhat = dy * g
    m1 = jnp.mean(dxhat, axis=-1, keepdims=True)
    m2 = jnp.mean(dxhat * xhat, axis=-1, keepdims=True)
    return rstd * (dxhat - m1 - xhat * m2)


def _layer_norm(pre):
    xc = pre - jnp.mean(pre, axis=-1, keepdims=True)
    rstd = lax.rsqrt(jnp.mean(xc * xc, axis=-1, keepdims=True) + EPS)
    return xc * rstd, rstd


def _sublayers(cat, x, target, w_out, w_ff1, w_ff2, g1, b1, g2, b2):
    t = x.shape[0]
    tm = min(t, 256)

    def body(cat_ref, x_ref, tg_ref, wo_ref, w1_ref, w2_ref, g1_ref, b1_ref, g2_ref, b2_ref,
             h1_ref, r_ref, da_ref, dp2b_ref, dp1_ref, dp1b_ref, dcat_ref, dg1_ref, db1_ref, dg2_ref, db2_ref, loss_ref):
        @pl.when(pl.program_id(0) == 0)
        def _():
            for ref in (dg1_ref, db1_ref, dg2_ref, db2_ref, loss_ref):
                ref[...] = jnp.zeros_like(ref)

        mix = _dot(cat_ref[0], wo_ref[0:GROUP, :], NN) + _dot(cat_ref[1], wo_ref[GROUP:2 * GROUP, :], NN)
        xhat1, rstd1 = _layer_norm(ALPHA * x_ref[...] + mix)
        h1 = xhat1 * g1_ref[...] + b1_ref[...]
        h1b = h1.astype(h1_ref.dtype)
        h1_ref[...] = h1b
        mlp = jnp.zeros((tm, D_MODEL), F32)
        for j in range(N_FF):
            cols = slice(j * FF_BLOCK, (j + 1) * FF_BLOCK)
            r = jnp.square(jnp.maximum(_dot(h1b, w1_ref[:, cols], NN), 0.0)).astype(r_ref.dtype)
            r_ref[:, cols] = r
            mlp = mlp + _dot(r, w2_ref[cols, :], NN)
        xhat2, rstd2 = _layer_norm(ALPHA * h1 + mlp)
        err = xhat2 * g2_ref[...] + b2_ref[...] - tg_ref[...]
        loss_ref[...] += 0.5 * jnp.sum(jnp.mean(err * err, axis=-1, keepdims=True))
        dy = err * (1.0 / D_MODEL)
        dg2_ref[...] += jnp.sum(dy * xhat2, axis=0, keepdims=True)
        db2_ref[...] += jnp.sum(dy, axis=0, keepdims=True)
        dp2 = _ln_bwd(dy, xhat2, rstd2, g2_ref[...])
        dp2b = dp2.astype(dp2b_ref.dtype)
        dp2b_ref[...] = dp2b
        back = jnp.zeros((tm, D_MODEL), F32)
        for j in range(N_FF):
            cols = slice(j * FF_BLOCK, (j + 1) * FF_BLOCK)
            dr = _dot(dp2b, w2_ref[cols, :], NT)
            da = (dr * (2.0 * jnp.sqrt(r_ref[:, cols].astype(F32)))).astype(da_ref.dtype)
            da_ref[:, cols] = da
            back = back + _dot(da, w1_ref[:, cols], NT)
        dh1 = ALPHA * dp2 + back
        dg1_ref[...] += jnp.sum(dh1 * xhat1, axis=0, keepdims=True)
        db1_ref[...] += jnp.sum(dh1, axis=0, keepdims=True)
        dp1 = _ln_bwd(dh1, xhat1, rstd1, g1_ref[...])
        dp1b = dp1.astype(dp1b_ref.dtype)
        dp1_ref[...] = dp1
        dp1b_ref[...] = dp1b
        dcat_ref[...] = _dot(dp1b, wo_ref[...], NT)

    row = pl.BlockSpec((tm, D_MODEL), lambda i: (i, 0))
    wide = pl.BlockSpec((tm, D_FF), lambda i: (i, 0))
    vec = pl.BlockSpec((1, D_MODEL), lambda i: (0, 0))
    narrow = lambda dtype: jax.ShapeDtypeStruct((t, D_MODEL), dtype)
    return pl.pallas_call(
        body, name="sublayers", grid=(t // tm,),
        in_specs=[pl.BlockSpec((2, tm, GROUP), lambda i: (0, i, 0)), row, row, _resident((D_MODEL, D_MODEL)),
                  _resident((D_MODEL, D_FF)), _resident((D_FF, D_MODEL)), vec, vec, vec, vec],
        out_specs=[row, wide, wide, row, row, row, row, vec, vec, vec, vec,
                   pl.BlockSpec((SUBLANES, LANES), lambda i: (0, 0))],
        out_shape=[narrow(BF16), jax.ShapeDtypeStruct((t, D_FF), BF16), jax.ShapeDtypeStruct((t, D_FF), BF16),
                   narrow(BF16), narrow(F32), narrow(BF16), narrow(F32)]
                  + [jax.ShapeDtypeStruct((1, D_MODEL), F32)] * 4 + [jax.ShapeDtypeStruct((SUBLANES, LANES), F32)],
        compiler_params=_params("arbitrary"),
    )(cat, x, target, w_out, w_ff1, w_ff2, g1, b1, g2, b2)


def _gate_bwd(dcat, o, proj, gate_norm_w, conv_w, after):
    t = proj.shape[1]
    tb = min(t, 512)
    hb = tb // SUBLANES
    nblk = t // tb

    def body(do2_ref, dy_ref, dyn_ref, o_ref, og_ref, gnw_ref, b_ref, bn_ref, c_ref, u_ref, ch_ref, uh_ref, cw_ref,
             after_ref, do_ref, dp_ref, dgnw_ref, dcw_ref, zbuf, dbuf):
        i = pl.program_id(0)

        @pl.when(i == 0)
        def _():
            dgnw_ref[...] = jnp.zeros_like(dgnw_ref)
            dcw_ref[...] = jnp.zeros_like(dcw_ref)

        ov, og, gnw, do2 = o_ref[...], og_ref[...], gnw_ref[...], do2_ref[...]
        rs = _per_head(lambda o_h: jnp.broadcast_to(lax.rsqrt(jnp.mean(o_h * o_h, axis=-1, keepdims=True) + EPS),
                                                    o_h.shape), ov)
        on = ov * rs
        sg = _sigmoid(og)
        sil = og * sg
        don = do2 * gnw * sil
        dgnw_ref[...] += jnp.sum(do2 * on * sil, axis=0, keepdims=True)
        dp_ref[0] = (do2 * on * gnw * (sg * (1.0 + og * (1.0 - sg)))).astype(dp_ref.dtype)
        do_ref[...] = rs * (don - on * _per_head(
            lambda p_h: jnp.broadcast_to(jnp.mean(p_h, axis=-1, keepdims=True), p_h.shape), don * on))

        bg, cg, u, dy = b_ref[...], c_ref[...], u_ref[...], dy_ref[...]
        z = cg * u
        halo = jnp.where(i > 0, ch_ref[...] * uh_ref[...], 0.0)
        z1, z2 = _conv_taps(z, halo, zbuf, tb)
        cw = cw_ref[...]
        yc = cw[2:3, :] * z + cw[1:2, :] * z1 + cw[0:1, :] * z2
        dyc = dy * bg
        dbuf[0:tb, :] = dyc
        dbuf[tb:tb + SUBLANES, :] = jnp.where(i < nblk - 1, dyn_ref[...] * bn_ref[...], 0.0)
        d1, d2 = dbuf[1:1 + tb, :], dbuf[2:2 + tb, :]
        dz = cw[2:3, :] * dyc + cw[1:2, :] * d1 + cw[0:1, :] * d2
        dp_ref[1] = (dy * yc).astype(dp_ref.dtype)
        dp_ref[2] = (dz * u).astype(dp_ref.dtype)
        dp_ref[3] = (dz * cg).astype(dp_ref.dtype)
        dcw_ref[0:1, :] += jnp.sum(dyc * z2, axis=0, keepdims=True)
        dcw_ref[1:2, :] += jnp.sum(dyc * z1, axis=0, keepdims=True)
        dcw_ref[2:3, :] += jnp.sum(dyc * z, axis=0, keepdims=True)

    half = lambda g: pl.BlockSpec((tb, GROUP), lambda i: (i, g))
    grp = lambda g: pl.BlockSpec((None, tb, GROUP), lambda i: (g, i, 0))
    prev = lambda g: pl.BlockSpec((None, SUBLANES, GROUP), lambda i: (g, jnp.maximum(i * hb - 1, 0), 0))
    nxt_row = lambda i: jnp.minimum((i + 1) * hb, t // SUBLANES - 1)
    nxt = lambda g: pl.BlockSpec((None, SUBLANES, GROUP), lambda i: (g, nxt_row(i), 0))
    vec = lambda r: pl.BlockSpec((r, GROUP), lambda i: (0, 0))
    return pl.pallas_call(
        body, name="gate_bwd", grid=(nblk,),
        in_specs=[half(0), half(1), pl.BlockSpec((SUBLANES, GROUP), lambda i: (nxt_row(i), 1)), half(0), grp(3), vec(1),
                  grp(4), nxt(4), grp(5), grp(6), prev(5), prev(6), vec(3), ANY],
        out_specs=[half(0), pl.BlockSpec((4, tb, GROUP), lambda i: (0, i, 0)), vec(1), vec(3)],
        out_shape=[jax.ShapeDtypeStruct((t, HGRN_WIDTH), F32), jax.ShapeDtypeStruct((4, t, HGRN_WIDTH), BF16),
                   jax.ShapeDtypeStruct((1, HGRN_WIDTH), F32), jax.ShapeDtypeStruct((3, CONV_WIDTH), F32)],
        scratch_shapes=[pltpu.VMEM((tb + SUBLANES, GROUP), F32), pltpu.VMEM((tb + SUBLANES, GROUP), F32)],
        compiler_params=_params("arbitrary"),
    )(dcat, dcat, dcat, o, proj, gate_norm_w, proj, proj, proj, proj, proj, proj, conv_w, after)


def _hgrn_bwd(proj, do, states, lb_logits, after):
    t = proj.shape[1]
    tb = min(t, 512)
    ncb = tb // CHUNK
    nblk = t // tb

    def body(q_ref, f_ref, v_ref, do_ref, st_ref, lbl_ref, after_ref, dp_ref, dlbl_ref, ds_scr, dlb_scr):
        i = pl.program_id(0)

        @pl.when(i == 0)
        def _():
            ds_scr[...] = jnp.zeros_like(ds_scr)
            dlb_scr[...] = jnp.zeros_like(dlb_scr)

        lb, s1 = _lower_bound(lbl_ref[...])
        causal, anti = _chunk_masks()
        every = range(ncb)
        rows = [slice(c * CHUNK, (c + 1) * CHUNK) for c in every]
        q, v, do = ([ref[r, :] for r in rows] for ref in (q_ref, v_ref, do_ref))
        st = [st_ref[c] for c in every]
        gates = [_gates(f_ref[r, :], lb) for r in rows]
        sig, f, k = ([gt[n] for gt in gates] for n in (0, 1, 3))
        b = [_dot_exact(causal, gt[2]) for gt in gates]
        mid, last = [x[CHUNK // 2:CHUNK // 2 + 1, :] for x in b], [x[CHUNK - 1:CHUNK, :] for x in b]
        e_q = [jnp.exp(b[c] - mid[c]) for c in every]
        e_k = [jnp.exp(mid[c] - b[c]) for c in every]
        e_i = [jnp.exp(x) for x in b]
        e_s = [jnp.exp(last[c] - b[c]) for c in every]
        dec = [jnp.exp(x) for x in last]
        qt, kt, qi, ks = ([a[c] * e[c] for c in every] for a, e in ((q, e_q), (k, e_k), (q, e_i), (k, e_s)))

        def masked(a, b_):
            return [[jnp.where(causal, _dot(a_h, b_h, NT), 0.0) for a_h, b_h in zip(_heads(a[c]), _heads(b_[c]))]
                    for c in every]

        def with_scores(s, other, dims):
            return [jnp.concatenate([_dot(s_h, o_h, dims) for s_h, o_h in zip(s[c], _heads(other[c]))], axis=1)
                    for c in every]

        def per_head(dims, a, b_):
            return [_per_head(lambda a_h, b_h: _dot(a_h, b_h, dims), a[c], b_[c]) for c in every]

        scores, dscores = masked(qt, kt), masked(do, v)
        dqt, dkt, dv_intra = with_scores(dscores, kt, NN), with_scores(dscores, qt, TN), with_scores(scores, do, TN)
        dqi, update = per_head(NN, do, st), per_head(TN, do, qi)

        dst = ds_scr[...]
        dsts = [None] * ncb
        for c in reversed(every):
            dsts[c] = dst
            dst = dec[c] * dst + update[c]
        ds_scr[...] = dst

        dv_state, dks = per_head(NT, ks, dsts), per_head(NN, v, dsts)
        ddec = [jnp.sum(dsts[c] * st[c], axis=0, keepdims=True) for c in every]
        dq = [dqt[c] * e_q[c] + dqi[c] * e_i[c] for c in every]
        dk = [dkt[c] * e_k[c] + dks[c] * e_s[c] for c in every]
        db = [q[c] * dq[c] - k[c] * dk[c] for c in every]
        db_last = [jnp.sum(dks[c] * ks[c], axis=0, keepdims=True) + ddec[c] * dec[c] for c in every]
        dg = [_dot_exact(anti, db[c]) + db_last[c] for c in every]
        df = [dg[c] / f[c] - dk[c] for c in every]
        dlb_scr[...] += sum(jnp.sum(df[c] * (1.0 - sig[c]), axis=0, keepdims=True) for c in every)
        dfp = [df[c] * (1.0 - lb) * sig[c] * (1.0 - sig[c]) for c in every]
        dv = [dv_intra[c] + dv_state[c] for c in every]
        for n, parts in enumerate((dq, dfp, dv)):
            dp_ref[n] = jnp.concatenate(parts, axis=0).astype(dp_ref.dtype)

        @pl.when(i == nblk - 1)
        def _():
            dlb = dlb_scr[...]
            dlbl_ref[0:1, :] = dlb * lb * (1.0 - lb)
            dlbl_ref[1:2, :] = -dlb * lb * s1

    grp = lambda g: pl.BlockSpec((None, tb, GROUP), lambda i: (g, nblk - 1 - i, 0))
    vec = pl.BlockSpec((2, HGRN_WIDTH), lambda i: (0, 0))
    return pl.pallas_call(
        body, name="hgrn_bwd", grid=(nblk,),
        in_specs=[grp(0), grp(1), grp(2), pl.BlockSpec((tb, HGRN_WIDTH), lambda i: (nblk - 1 - i, 0)),
                  pl.BlockSpec((ncb, HEAD_DIM, HGRN_WIDTH), lambda i: (nblk - 1 - i, 0, 0)), vec, ANY],
        out_specs=[pl.BlockSpec((3, tb, HGRN_WIDTH), lambda i: (0, nblk - 1 - i, 0)), vec],
        out_shape=[jax.ShapeDtypeStruct((3, t, HGRN_WIDTH), BF16), jax.ShapeDtypeStruct((2, HGRN_WIDTH), F32)],
        scratch_shapes=[pltpu.VMEM((HEAD_DIM, HGRN_WIDTH), F32), pltpu.VMEM((1, HGRN_WIDTH), F32)],
        compiler_params=_params("arbitrary"),
    )(proj, proj, proj, do, states, lb_logits, after)


def _in_bwd(dph, dpg, w_in, dpre1, after):
    t = dpre1.shape[0]
    tm = min(t, 512)

    def body(dh_ref, dg_ref, w_ref, dp_ref, after_ref, o_ref):
        acc = ALPHA * dp_ref[...]
        for g in range(N_GROUPS):
            part = dh_ref[g] if g < 3 else dg_ref[g - 3]
            acc = acc + _dot(part, w_ref[:, g * GROUP:(g + 1) * GROUP], NT)
        o_ref[...] = acc

    row = pl.BlockSpec((tm, D_MODEL), lambda i: (i, 0))
    return pl.pallas_call(
        body, name="in_bwd", grid=(t // tm,),
        in_specs=[pl.BlockSpec((3, tm, GROUP), lambda i: (0, i, 0)), pl.BlockSpec((4, tm, GROUP), lambda i: (0, i, 0)),
                  _resident((D_MODEL, IN_COLS)), row, ANY],
        out_specs=row,
        out_shape=jax.ShapeDtypeStruct((t, D_MODEL), F32),
        compiler_params=_params("parallel"),
    )(dph, dpg, w_in, dpre1, after)


def _grad_w(name, operands, widths, shape, step, after=None):
    t = operands[0].shape[-2]
    tt = min(t, 1024)
    n_in, n_steps = len(operands), t // tt
    in_specs = [pl.BlockSpec((tt, w), lambda k: (k, 0)) if a.ndim == 2 else
                pl.BlockSpec((a.shape[0], tt, w), lambda k: (0, k, 0)) for a, w in zip(operands, widths)]
    extra = [] if after is None else [after]

    def body(*refs):
        o_ref, acc, sem = refs[-3:]
        k = pl.program_id(0)

        @pl.when(k == 0)
        def _():
            acc[...] = jnp.zeros_like(acc)

        step(acc, *refs[:n_in])

        @pl.when(k == n_steps - 1)
        def _():
            out = pltpu.make_async_copy(acc, o_ref, sem)
            out.start()
            out.wait()

    return pl.pallas_call(
        body, name=name, grid=(n_steps,), in_specs=in_specs + [ANY] * len(extra), out_specs=ANY,
        out_shape=jax.ShapeDtypeStruct(shape, F32),
        scratch_shapes=[pltpu.VMEM(shape, F32), pltpu.SemaphoreType.DMA],
        compiler_params=_params("arbitrary"),
    )(*operands, *extra)


def _dw_in(xb, dph, dpg, after):
    def step(acc, x_ref, dh_ref, dg_ref):
        xv = x_ref[...]
        for g in range(N_GROUPS):
            part = dh_ref[g] if g < 3 else dg_ref[g - 3]
            acc[:, g * GROUP:(g + 1) * GROUP] += _dot(xv, part, TN)

    return _grad_w("dw_in", (xb, dph, dpg), (D_MODEL, GROUP, GROUP), (D_MODEL, IN_COLS), step, after)


def _dw_out(cat, dpre1b):
    def step(acc, cat_ref, d_ref):
        dv = d_ref[...]
        for g in range(2):
            acc[g * GROUP:(g + 1) * GROUP, :] += _dot(cat_ref[g], dv, TN)

    return _grad_w("dw_out", (cat, dpre1b), (GROUP, D_MODEL), (D_MODEL, D_MODEL), step)


def _dw_ff1(h1b, da):
    def step(acc, h_ref, da_ref):
        hv = h_ref[...]
        for j in range(D_FF // FF_BLOCK):
            cols = slice(j * FF_BLOCK, (j + 1) * FF_BLOCK)
            acc[:, cols] += _dot(hv, da_ref[:, cols], TN)

    return _grad_w("dw_ff1", (h1b, da), (D_MODEL, D_FF), (D_MODEL, D_FF), step)


def _dw_ff2(r, dpre2b):
    def step(acc, r_ref, d_ref):
        dv = d_ref[...]
        for j in range(D_FF // FF_BLOCK):
            rows = slice(j * FF_BLOCK, (j + 1) * FF_BLOCK)
            acc[rows, :] += _dot(r_ref[:, rows], dv, TN)

    return _grad_w("dw_ff2", (r, dpre2b), (D_FF, D_MODEL), (D_FF, D_MODEL), step)


def _place():
    x, y, c = lax.axis_index("x"), lax.axis_index("y"), lax.axis_index("c")
    return x, y, c, 2 * x + y


def _other_chips(x, y):
    return [(1 - x, y), (x, 1 - y), (1 - x, 1 - y)]


def _place_shard(name, w, chip, cols_sharded, after=None):
    rows, cols = w.shape
    tr = min(rows, 256)
    nb = rows // tr
    full = (rows, cols * N_CHIPS) if cols_sharded else (rows * N_CHIPS, cols)
    out_map = (lambda i, s: (i, s[0])) if cols_sharded else (lambda i, s: (s[0] * nb + i, 0))

    def body(s_ref, w_ref, *rest):
        rest[-1][...] = w_ref[...].astype(rest[-1].dtype)

    extra = [] if after is None else [after]
    return pl.pallas_call(
        body, name=name,
        grid_spec=pltpu.PrefetchScalarGridSpec(
            num_scalar_prefetch=1, grid=(nb,),
            in_specs=[pl.BlockSpec((tr, cols), lambda i, s: (i, 0))] + [ANY] * len(extra),
            out_specs=pl.BlockSpec((tr, cols), out_map)),
        out_shape=jax.ShapeDtypeStruct(full, BF16),
        compiler_params=_params("parallel"),
    )(chip, w, *extra)


HBM = pl.BlockSpec(memory_space=pltpu.HBM)
SEM = pl.BlockSpec(memory_space=pltpu.SEMAPHORE)
EFFECT = pltpu.SideEffectType.DATAFLOW_SIDE_EFFECTING


class _Split:
    def __init__(self, name, arrays, plan):
        n, n_copies = len(arrays), plan.count
        self.name, self.plan, self.n = name, plan, n

        def body(*refs):
            send_sems, recv_sems, token = refs[n], refs[n + 1], refs[-1]
            for k, (src, dst, to) in enumerate(plan(refs[:n])):
                pltpu.make_async_remote_copy(src_ref=src, dst_ref=dst, send_sem=send_sems.at[k], recv_sem=recv_sems.at[k],
                                             device_id=to, device_id_type=MESH).start()
            token[...] = jnp.zeros_like(token)

        outs = pl.pallas_call(
            body, name=name + "_start",
            out_shape=(pltpu.SemaphoreType.DMA((n_copies,)), pltpu.SemaphoreType.DMA((n_copies,)),
                       *[pltpu.HBM(a.shape, a.dtype) for a in arrays], jax.ShapeDtypeStruct((SUBLANES, LANES), F32)),
            in_specs=(HBM,) * n, out_specs=(SEM, SEM) + (HBM,) * n + (pl.BlockSpec(memory_space=pltpu.VMEM),),
            input_output_aliases={i: 2 + i for i in range(n)},
            compiler_params=pltpu.CompilerParams(has_side_effects=EFFECT),
        )(*[pltpu.with_memory_space_constraint(a, pltpu.HBM) for a in arrays])
        self.sems, self.arrays, self.token = outs[:2], outs[2:2 + n], outs[-1]

    def wait(self, after):
        n, plan = self.n, self.plan

        def body(*refs):
            send_sems, recv_sems = refs[n], refs[n + 1]
            for k, (src, dst, to) in enumerate(plan(refs[:n])):
                cp = pltpu.make_async_remote_copy(src_ref=src, dst_ref=dst, send_sem=send_sems.at[k],
                                                  recv_sem=recv_sems.at[k], device_id=to, device_id_type=MESH)
                cp.wait_send()
                cp.wait_recv()

        return pl.pallas_call(
            body, name=self.name + "_wait", out_shape=tuple(pltpu.HBM(a.shape, a.dtype) for a in self.arrays),
            in_specs=(HBM,) * n + (SEM, SEM, ANY), out_specs=(HBM,) * n, input_output_aliases={i: i for i in range(n)},
            compiler_params=pltpu.CompilerParams(has_side_effects=EFFECT),
        )(*self.arrays, *self.sems, after)


COLS_SHARDED = (True, False, True, False)
HALF_SHAPES = [(D_MODEL // 2, IN_COLS), (D_MODEL, D_MODEL // 2), (D_MODEL // 2, D_FF), (D_FF, D_MODEL // 2)]
PIECE_SHAPES = [(D_MODEL // 2, IN_COLS // N_CHIPS), (D_MODEL // N_CHIPS, D_MODEL // 2),
                (D_MODEL // 2, D_FF // N_CHIPS), (D_FF // N_CHIPS, D_MODEL // 2)]


def _shard_view(kind, ref, chip):
    if COLS_SHARDED[kind]:
        n = ref.shape[1] // N_CHIPS
        return ref.at[:, pl.ds(chip * n, n)]
    n = ref.shape[0] // N_CHIPS
    return ref.at[pl.ds(chip * n, n), :]


def _half_view(kind, ref, h):
    if COLS_SHARDED[kind]:
        n = ref.shape[0] // 2
        return ref.at[pl.ds(h * n, n), :]
    n = ref.shape[1] // 2
    return ref.at[:, pl.ds(h * n, n)]


def _plan(count):
    def mark(fn):
        fn.count = count
        return fn
    return mark


def _shard_rows_view(kind, ref, chip, part, n_parts):
    if COLS_SHARDED[kind]:
        m, n = ref.shape[0] // n_parts, ref.shape[1] // N_CHIPS
        return ref.at[pl.ds(part * m, m), pl.ds(chip * n, n)]
    m = ref.shape[0] // N_CHIPS // n_parts
    return ref.at[pl.ds((n_parts * chip + part) * m, m), :]


def _shard_half_view(kind, ref, chip, h):
    return _shard_rows_view(kind, ref, chip, h, 2)


def _gather_over_ici(kinds, weights):
    @_plan(2 * len(kinds))
    def plan(refs):
        x, y, c, me = _place()
        mine = [_shard_half_view(kind, ref, me, c) for kind, ref in zip(kinds, refs)]
        return [(v, v, to) for v in mine for to in ((1 - x, y, c), (x, 1 - y, c))]

    return _Split("gather_ici_" + "".join(map(str, kinds)), tuple(weights), plan)


def _relay_over_ici(kinds, weights):
    @_plan(2 * len(kinds))
    def plan(refs):
        x, y, c, _ = _place()
        x_nbr, y_nbr = 2 * (1 - x) + y, 2 * x + (1 - y)
        out = []
        for kind, ref in zip(kinds, refs):
            first, second = (_shard_rows_view(kind, ref, chip, 2 * c + q, 4) for q, chip in ((0, x_nbr), (1, y_nbr)))
            out += [(first, first, (x, 1 - y, c)), (second, second, (1 - x, y, c))]
        return out

    return _Split("relay_ici_" + "".join(map(str, kinds)), tuple(weights), plan)


def _gather_w_in_over_ici(w_in, conv4):
    @_plan(6)
    def plan(refs):
        x, y, c, me = _place()
        half, conv = _shard_half_view(0, refs[0], me, c), refs[1].at[me]
        return [(v, v, (px, py, c)) for v in (half, conv) for px, py in _other_chips(x, y)]

    return _Split("gather_w_in_ici", (w_in, conv4), plan)


def _gather_over_d2d(kinds, weights):
    @_plan(3 * len(kinds))
    def plan(refs):
        x, y, c, _ = _place()
        got = [_shard_half_view(kind, ref, 2 * px + py, c) for kind, ref in zip(kinds, refs)
               for px, py in _other_chips(x, y)]
        return [(v, v, (x, y, 1 - c)) for v in got]

    return _Split("gather_d2d_" + "".join(map(str, kinds)), tuple(weights), plan)


def _swap_halves(kinds, grads):
    @_plan(len(kinds))
    def plan(refs):
        x, y, c, _ = _place()
        return [(_half_view(kind, g, 1 - c), land, (x, y, 1 - c))
                for kind, g, land in zip(kinds, refs[:len(kinds)], refs[len(kinds):])]

    lands = [lax.empty(HALF_SHAPES[kind], F32) for kind in kinds]
    return _Split("swap_halves_" + "".join(map(str, kinds)), (*grads, *lands), plan)


def _add_half(name, g, recv, core, rows_split):
    shape = recv.shape
    tr = min(shape[0], 128 if rows_split else 256)
    nb = shape[0] // tr

    def body(c_ref, g_ref, r_ref, o_ref):
        o_ref[...] = (g_ref[...] + r_ref[...]).astype(o_ref.dtype)

    g_map = (lambda i, c_ref: (c_ref[0] * nb + i, 0)) if rows_split else (lambda i, c_ref: (i, c_ref[0]))
    blk = pl.BlockSpec((tr, shape[1]), lambda i, c_ref: (i, 0))
    return pl.pallas_call(
        body, name=name,
        grid_spec=pltpu.PrefetchScalarGridSpec(
            num_scalar_prefetch=1, grid=(nb,),
            in_specs=[pl.BlockSpec((tr, shape[1]), g_map), blk], out_specs=blk),
        out_shape=jax.ShapeDtypeStruct(shape, BF16),
        compiler_params=_params("parallel"),
    )(core, g, recv)


def _exchange_pieces(kinds, halves):
    n_p = N_CHIPS - 1

    @_plan(n_p * len(kinds))
    def plan(refs):
        x, y, c, _ = _place()
        return [(_shard_view(kind, half, 2 * px + py), land.at[j], (px, py, c))
                for j, (px, py) in enumerate(_other_chips(x, y))
                for kind, half, land in zip(kinds, refs[:len(kinds)], refs[len(kinds):])]

    lands = [lax.empty((n_p,) + PIECE_SHAPES[kind], BF16) for kind in kinds]
    return _Split("exchange_pieces_" + "".join(map(str, kinds)), (*halves, *lands), plan)


def _sum_pieces(name, half, slots, place, rows_split):
    n_p, rows, cols = slots.shape
    tr = min(rows, 256)
    nb = rows // tr
    if rows_split:
        own_map = lambda i, s: (i, s[0])
        out_map = lambda i, s: (s[1] * nb + i, 0)
        shard = (2 * rows, cols)
    else:
        own_map = lambda i, s: (s[0] * nb + i, 0)
        out_map = lambda i, s: (i, s[1])
        shard = (rows, 2 * cols)

    def body(s_ref, own_ref, slot_ref, o_ref):
        total = own_ref[...].astype(F32)
        for j in range(n_p):
            total = total + slot_ref[j].astype(F32)
        o_ref[...] = total

    return pl.pallas_call(
        body, name=name,
        grid_spec=pltpu.PrefetchScalarGridSpec(
            num_scalar_prefetch=1, grid=(nb,),
            in_specs=[pl.BlockSpec((tr, cols), own_map), pl.BlockSpec((n_p, tr, cols), lambda i, s: (0, i, 0))],
            out_specs=pl.BlockSpec((tr, cols), out_map)),
        out_shape=jax.ShapeDtypeStruct(shard, F32),
        compiler_params=_params("parallel"),
    )(place, half, slots)


def _join_halves(kinds, shards):
    @_plan(len(kinds))
    def plan(refs):
        x, y, c, _ = _place()
        return [(_half_view(kind, g, c), _half_view(kind, g, c), (x, y, 1 - c)) for kind, g in zip(kinds, refs)]

    return _Split("join_halves_" + "".join(map(str, kinds)), tuple(shards), plan)


N_DEV = 8


def _share_small(pack, after):
    @_plan(N_DEV - 1)
    def plan(refs):
        x, y, c, _ = _place()
        me = 4 * x + 2 * y + c
        peers = [((1 - x) if m & 4 else x, (1 - y) if m & 2 else y, (1 - c) if m & 1 else c) for m in range(1, N_DEV)]
        return [(refs[0], refs[1].at[me], peer) for peer in peers]

    return _Split("share_small", (pack, lax.empty((N_DEV,) + pack.shape, F32), after), plan)


def _sum_shared(pack, land, device):
    def body(d_ref, p_ref, l_ref, o_ref):
        me = d_ref[0]
        total = jnp.where(me == 0, p_ref[...], l_ref[0])
        for d in range(1, N_DEV):
            total = total + jnp.where(me == d, p_ref[...], l_ref[d])
        o_ref[...] = total

    return pl.pallas_call(
        body, name="sum_shared",
        grid_spec=pltpu.PrefetchScalarGridSpec(
            num_scalar_prefetch=1, grid=(1,),
            in_specs=[pl.BlockSpec(pack.shape, lambda i, d: (0, 0)), pl.BlockSpec(land.shape, lambda i, d: (0, 0, 0))],
            out_specs=pl.BlockSpec(pack.shape, lambda i, d: (0, 0))),
        out_shape=jax.ShapeDtypeStruct(pack.shape, F32),
    )(device, pack, land)


def _adamw(name, w, g, m, v, after=None):
    rows, cols = w.shape
    tr = min(rows, 256)
    extra = [] if after is None else [after]

    def body(w_ref, g_ref, m_ref, v_ref, *rest):
        d_ref, nm_ref, nv_ref = rest[-3:]
        gv = g_ref[...]
        nm = ADAM_B1 * m_ref[...] + (1.0 - ADAM_B1) * gv
        nv = ADAM_B2 * v_ref[...] + (1.0 - ADAM_B2) * jnp.square(gv)
        m_hat = nm * (1.0 / (1.0 - ADAM_B1 ** ADAM_STEP))
        v_hat = nv * (1.0 / (1.0 - ADAM_B2 ** ADAM_STEP))
        d_ref[...] = -ADAM_LR * (m_hat / (jnp.sqrt(v_hat) + ADAM_EPS) + ADAM_WD * w_ref[...])
        nm_ref[...] = nm
        nv_ref[...] = nv

    blk = pl.BlockSpec((tr, cols), lambda i: (i, 0))
    return pl.pallas_call(
        body, name=name, grid=(rows // tr,), in_specs=[blk] * 4 + [ANY] * len(extra), out_specs=[blk] * 3,
        out_shape=[jax.ShapeDtypeStruct(w.shape, F32)] * 3,
        compiler_params=_params("parallel"),
    )(w, g, m, v, *extra)


def kernel(x, w_in, lb_logits, gate_norm_w, conv_w, w_out, ln1_g, ln1_b, w_ff1, w_ff2, ln2_g, ln2_b, loss_target, m_w_in, m_lb_logits, m_gate_norm_w, m_conv_w, m_w_out, m_ln1_g, m_ln1_b, m_w_ff1, m_w_ff2, m_ln2_g, m_ln2_b, v_w_in, v_lb_logits, v_gate_norm_w, v_conv_w, v_w_out, v_ln1_g, v_ln1_b, v_w_ff1, v_w_ff2, v_ln2_g, v_ln2_b):
    xs, tgt = x[0], loss_target[0]
    chip = 2 * lax.axis_index("x") + lax.axis_index("y")
    core = lax.axis_index("c").astype(jnp.int32).reshape(1)
    chip1 = chip.astype(jnp.int32).reshape(1)
    place = jnp.concatenate([chip1, core])

    conv4 = lax.dynamic_update_slice(jnp.zeros((N_CHIPS,) + conv_w.shape[1:], F32), conv_w, (chip, 0, 0))
    ici_in = _gather_w_in_over_ici(_place_shard("place_w_in", w_in[0], chip1, True), conv4)
    rest = (1, 2, 3)
    ici_rest = _gather_over_ici(rest, (_place_shard("place_w_out", w_out[0], chip1, False, after=ici_in.token),
                                       _place_shard("place_w_ff1", w_ff1[0], chip1, True, after=ici_in.token),
                                       _place_shard("place_w_ff2", w_ff2[0], chip1, False, after=ici_in.token)))
    wb_in, cv4 = ici_in.wait(ici_rest.token)
    d2d_in = _gather_over_d2d((0,), (wb_in,))
    wb_in, = d2d_in.wait(d2d_in.token)
    conv_full = cv4.transpose(1, 0, 2).reshape(3, CONV_WIDTH)

    proj, xb = _in_proj(xs, wb_in, ici_rest.token)
    relay_rest = _relay_over_ici(rest, ici_rest.wait(proj))
    o, states = _hgrn_fwd(proj, lb_logits, relay_rest.token)
    d2d_rest = _gather_over_d2d(rest, relay_rest.wait(o))
    cat = _gate_fwd(proj, o, gate_norm_w, conv_full, d2d_rest.token)
    wb_out, wb_ff1, wb_ff2 = d2d_rest.wait(cat)

    (h1b, r, da, dpre2b, dpre1, dpre1b, dcat, g_ln1_g, g_ln1_b, g_ln2_g, g_ln2_b, loss8) = _sublayers(
        cat, xs, tgt, wb_out, wb_ff1, wb_ff2, ln1_g, ln1_b, ln2_g, ln2_b)

    names = ("w_in", "w_out", "w_ff1", "w_ff2")

    def add_halves(kinds, grads, lands):
        return [_add_half("add_half_" + names[k], g, ld, core, COLS_SHARDED[k]) for k, g, ld in zip(kinds, grads, lands)]

    def sum_pieces(kinds, halves, lands):
        return [_sum_pieces("sum_pieces_" + names[k], h, ld, place, COLS_SHARDED[k]) for k, h, ld in zip(kinds, halves, lands)]

    early = (1, 2, 3)
    swap = _swap_halves(early, (_dw_out(cat, dpre1b), _dw_ff1(h1b, da), _dw_ff2(r, dpre2b)))
    do, dpg, g_gnw, g_conv = _gate_bwd(dcat, o, proj, gate_norm_w, conv_full, swap.token)
    swapped = swap.wait(do)
    exch = _exchange_pieces(early, add_halves(early, swapped[:3], swapped[3:]))
    dph, g_lbl = _hgrn_bwd(proj, do, states, lb_logits, exch.token)
    g_in_local = _dw_in(xb, dph, dpg, dph)

    late = (0,)
    swap = _swap_halves(late, (g_in_local,))
    grad_x = _in_bwd(dph, dpg, wb_in, dpre1, swap.token)
    exchanged = exch.wait(grad_x)
    pack = jnp.concatenate([
        g_ln1_g, g_ln1_b, g_ln2_g, g_ln2_b,
        jnp.concatenate([g_lbl[0:1], g_lbl[1:2]], axis=1),
        jnp.concatenate([g_gnw, g_conv[0:1]], axis=1),
        jnp.concatenate([g_conv[1:2], g_conv[2:3]], axis=1),
        jnp.concatenate([loss8[0:1], jnp.zeros((1, D_MODEL - LANES), F32)], axis=1)], axis=0)
    small = _share_small(pack, exchanged[0])
    swapped = swap.wait(small.token)
    exch = _exchange_pieces(late, add_halves(late, swapped[:1], swapped[1:]))
    shared = small.wait(exch.token)
    join = _join_halves(early, sum_pieces(early, (shared[2], *exchanged[1:3]), exchanged[3:]))
    tot = _sum_shared(shared[0], shared[1], 2 * chip1 + core)
    loss = tot[7, 0]
    half = D_MODEL // 2
    g_lb_logits = jnp.concatenate([tot[4:5, :half], tot[4:5, half:]], axis=0)
    g_gate_norm_w = tot[5:6, :half]
    g_conv_full = jnp.concatenate([tot[5:6, half:], tot[6:7, :half], tot[6:7, half:]], axis=0)
    g_conv_w = lax.dynamic_slice(g_conv_full, (0, chip * LANES), (3, LANES))
    g_w_out, g_w_ff1, g_w_ff2 = join.wait(tot)
    d_ff1, nm_ff1, nv_ff1 = _adamw("adamw_w_ff1", w_ff1[0], g_w_ff1, m_w_ff1[0], v_w_ff1[0], exch.token)
    d_ff2, nm_ff2, nv_ff2 = _adamw("adamw_w_ff2", w_ff2[0], g_w_ff2, m_w_ff2[0], v_w_ff2[0], d_ff1)
    d_out, nm_out, nv_out = _adamw("adamw_w_out", w_out[0], g_w_out, m_w_out[0], v_w_out[0], d_ff2)

    def small_pack(lbl, gnw, cv, l1g, l1b, l2g, l2b):
        pad = jnp.zeros((1, D_MODEL - 3 * LANES), F32)
        return jnp.concatenate([
            l1g, l1b, l2g, l2b, jnp.concatenate([lbl[0:1], lbl[1:2]], axis=1),
            jnp.concatenate([gnw, jnp.zeros((1, half), F32)], axis=1),
            jnp.concatenate([cv[0:1], cv[1:2], cv[2:3], pad], axis=1), jnp.zeros((1, D_MODEL), F32)], axis=0)

    w_s = small_pack(lb_logits, gate_norm_w, conv_w[0], ln1_g, ln1_b, ln2_g, ln2_b)
    g_s = small_pack(g_lb_logits, g_gate_norm_w, g_conv_w, tot[0:1], tot[1:2], tot[2:3], tot[3:4])
    m_s = small_pack(m_lb_logits, m_gate_norm_w, m_conv_w[0], m_ln1_g, m_ln1_b, m_ln2_g, m_ln2_b)
    v_s = small_pack(v_lb_logits, v_gate_norm_w, v_conv_w[0], v_ln1_g, v_ln1_b, v_ln2_g, v_ln2_b)
    d_s, nm_s, nv_s = _adamw("adamw_small", w_s, g_s, m_s, v_s, d_out)
    exchanged = exch.wait(d_s)
    join = _join_halves(late, sum_pieces(late, exchanged[:1], exchanged[1:]))
    g_w_in, = join.wait(join.token)
    d_in, nm_in, nv_in = _adamw("adamw_w_in", w_in[0], g_w_in, m_w_in[0], v_w_in[0])

    def unpack(p):
        lbl = jnp.concatenate([p[4:5, :half], p[4:5, half:]], axis=0)
        cv = jnp.concatenate([p[6:7, 0:LANES], p[6:7, LANES:2 * LANES], p[6:7, 2 * LANES:3 * LANES]], axis=0)
        return dict(lb_logits=lbl, gate_norm_w=p[5:6, :half], conv_w=cv[None], ln1_g=p[0:1], ln1_b=p[1:2],
                    ln2_g=p[2:3], ln2_b=p[3:4])

    order = ("w_in", "lb_logits", "gate_norm_w", "conv_w", "w_out", "ln1_g", "ln1_b", "w_ff1", "w_ff2", "ln2_g", "ln2_b")
    grad = dict(unpack(g_s), w_in=g_w_in[None], w_out=g_w_out[None], w_ff1=g_w_ff1[None], w_ff2=g_w_ff2[None])
    delta = dict(unpack(d_s), w_in=d_in[None], w_out=d_out[None], w_ff1=d_ff1[None], w_ff2=d_ff2[None])
    new_m = dict(unpack(nm_s), w_in=nm_in[None], w_out=nm_out[None], w_ff1=nm_ff1[None], w_ff2=nm_ff2[None])
    new_v = dict(unpack(nv_s), w_in=nv_in[None], w_out=nv_out[None], w_ff1=nv_ff1[None], w_ff2=nv_ff2[None])
    return (loss, grad_x[None], *[grad[n] for n in order], *[delta[n] for n in order],
            *[new_m[n] for n in order], *[new_v[n] for n in order])
```

```python
import jax
import jax.numpy as jnp
from jax import lax
from jax.experimental import pallas as pl
from jax.experimental.pallas import tpu as pltpu

F32 = jnp.float32
BF16 = jnp.bfloat16
MXU_DTYPE = jnp.bfloat16

D_MODEL = 1024
HGRN_WIDTH = 512
HEAD_DIM = 128
N_HEADS = 4
CONV_WIDTH = 512
CHUNK = 64
D_FF = 4096
IN_COLS = 3584
GROUP = 512
N_GROUPS = IN_COLS // GROUP
ALPHA = 2.0 ** 0.25
EPS = 1e-5
N_CHIPS = 4
ADAM_LR, ADAM_B1, ADAM_B2, ADAM_EPS, ADAM_WD, ADAM_STEP = 0.001, 0.9, 0.999, 1e-08, 0.01, 10

LANES = 128
SUBLANES = 8
VMEM_LIMIT = 56 * 1024 * 1024
FF_BLOCK = 1024
N_FF = D_FF // FF_BLOCK

NN = (((1,), (0,)), ((), ()))
NT = (((1,), (1,)), ((), ()))
TN = (((0,), (0,)), ((), ()))
MESH = pl.DeviceIdType.MESH
ANY = pl.BlockSpec(memory_space=pl.ANY)


def _dot(a, b, dims):
    return lax.dot_general(a.astype(MXU_DTYPE), b.astype(MXU_DTYPE), dims, preferred_element_type=F32)


def _dot_exact(ones, v):
    ones = ones.astype(jnp.bfloat16)
    hi = v.astype(jnp.bfloat16)
    rest = v - hi.astype(F32)
    mid = rest.astype(jnp.bfloat16)
    low = (rest - mid.astype(F32)).astype(jnp.bfloat16)
    return sum(lax.dot_general(ones, part, NN, preferred_element_type=F32) for part in (hi, mid, low))


def _params(*sem):
    return pltpu.CompilerParams(dimension_semantics=sem, vmem_limit_bytes=VMEM_LIMIT)


def _resident(shape):
    return pl.BlockSpec(shape, lambda *_: (0,) * len(shape), pipeline_mode=pl.Buffered(1))


def _sigmoid(v):
    return 1.0 / (1.0 + jnp.exp(-v))


def _lower_bound(lbl):
    m = jnp.max(lbl, axis=0, keepdims=True)
    e = jnp.exp(lbl - m)
    s = e / jnp.sum(e, axis=0, keepdims=True)
    return s[0:1, :], s[1:2, :]


def _heads(v):
    return [v[:, h * HEAD_DIM:(h + 1) * HEAD_DIM] for h in range(N_HEADS)]


def _per_head(fn, *arrays):
    return jnp.concatenate([fn(*parts) for parts in zip(*map(_heads, arrays))], axis=1)


def _in_proj(x, w_in, after):
    t = x.shape[0]
    tm = min(t, 512)

    def body(x_ref, w_ref, after_ref, o_ref, xb_ref):
        xb = x_ref[...].astype(xb_ref.dtype)
        xb_ref[...] = xb
        for g in range(N_GROUPS):
            o_ref[g] = _dot(xb, w_ref[:, g * GROUP:(g + 1) * GROUP], NN)

    return pl.pallas_call(
        body, name="in_proj", grid=(t // tm,),
        in_specs=[pl.BlockSpec((tm, D_MODEL), lambda i: (i, 0)), _resident((D_MODEL, IN_COLS)), ANY],
        out_specs=[pl.BlockSpec((N_GROUPS, tm, GROUP), lambda i: (0, i, 0)), pl.BlockSpec((tm, D_MODEL), lambda i: (i, 0))],
        out_shape=[jax.ShapeDtypeStruct((N_GROUPS, t, GROUP), F32), jax.ShapeDtypeStruct((t, D_MODEL), BF16)],
        compiler_params=_params("parallel"),
    )(x, w_in, after)


def _gates(fp, lb):
    sig = _sigmoid(fp)
    f = lb + (1.0 - lb) * sig
    return sig, f, jnp.log(f), 1.0 - f


def _chunk_masks():
    row = lax.broadcasted_iota(jnp.int32, (CHUNK, CHUNK), 0)
    col = lax.broadcasted_iota(jnp.int32, (CHUNK, CHUNK), 1)
    return row >= col, row <= col


def _hgrn_fwd(proj, lb_logits, after):
    t = proj.shape[1]
    tb = min(t, 512)
    ncb = tb // CHUNK

    def body(q_ref, f_ref, v_ref, lbl_ref, after_ref, o_ref, st_ref, s_scr):
        @pl.when(pl.program_id(0) == 0)
        def _():
            s_scr[...] = jnp.zeros_like(s_scr)

        lb, _ = _lower_bound(lbl_ref[...])
        causal, _ = _chunk_masks()

        every = range(ncb)
        rows = [slice(c * CHUNK, (c + 1) * CHUNK) for c in every]
        q, v = [q_ref[r, :] for r in rows], [v_ref[r, :] for r in rows]
        gates = [_gates(f_ref[r, :], lb) for r in rows]
        k = [gt[3] for gt in gates]
        b = [_dot_exact(causal, gt[2]) for gt in gates]
        mid, last = [x[CHUNK // 2:CHUNK // 2 + 1, :] for x in b], [x[CHUNK - 1:CHUNK, :] for x in b]
        qt = [q[c] * jnp.exp(b[c] - mid[c]) for c in every]
        kt = [k[c] * jnp.exp(mid[c] - b[c]) for c in every]
        qi = [q[c] * jnp.exp(b[c]) for c in every]
        ks = [k[c] * jnp.exp(last[c] - b[c]) for c in every]
        dec = [jnp.exp(x) for x in last]
        scores = [[jnp.where(causal, _dot(a, b_, NT), 0.0) for a, b_ in zip(_heads(qt[c]), _heads(kt[c]))] for c in every]
        intra = [[_dot(s, v_h, NN) for s, v_h in zip(scores[c], _heads(v[c]))] for c in every]
        update = [_per_head(lambda v_h, ks_h: _dot(v_h, ks_h, TN), v[c], ks[c]) for c in every]

        st = s_scr[...]
        states = []
        for c in every:
            states.append(st)
            st_ref[c] = st
            st = dec[c] * st + update[c]
        s_scr[...] = st

        o_ref[...] = jnp.concatenate(
            [jnp.concatenate([i_h + _dot(qi_h, st_h, NT) for i_h, qi_h, st_h in
                              zip(intra[c], _heads(qi[c]), _heads(states[c]))], axis=1) for c in every], axis=0)

    grp = lambda g: pl.BlockSpec((None, tb, GROUP), lambda i: (g, i, 0))
    return pl.pallas_call(
        body, name="hgrn_fwd", grid=(t // tb,),
        in_specs=[grp(0), grp(1), grp(2), pl.BlockSpec((2, HGRN_WIDTH), lambda i: (0, 0)), ANY],
        out_specs=[pl.BlockSpec((tb, HGRN_WIDTH), lambda i: (i, 0)),
                   pl.BlockSpec((ncb, HEAD_DIM, HGRN_WIDTH), lambda i: (i, 0, 0))],
        out_shape=[jax.ShapeDtypeStruct((t, HGRN_WIDTH), F32),
                   jax.ShapeDtypeStruct((t // CHUNK, HEAD_DIM, HGRN_WIDTH), F32)],
        scratch_shapes=[pltpu.VMEM((HEAD_DIM, HGRN_WIDTH), F32)],
        compiler_params=_params("arbitrary"),
    )(proj, proj, proj, lb_logits, after)


def _conv_taps(z, halo, zbuf, tb):
    zbuf[0:SUBLANES, :] = halo
    zbuf[SUBLANES:SUBLANES + tb, :] = z
    return zbuf[SUBLANES - 1:SUBLANES - 1 + tb, :], zbuf[SUBLANES - 2:SUBLANES - 2 + tb, :]


def _gate_fwd(proj, o, gate_norm_w, conv_w, after):
    t = proj.shape[1]
    tb = min(t, 512)
    hb = tb // SUBLANES

    def body(o_ref, og_ref, gnw_ref, b_ref, c_ref, u_ref, ch_ref, uh_ref, cw_ref, after_ref, cat_ref, zbuf):
        i = pl.program_id(0)
        og = og_ref[...]
        on = _per_head(lambda o_h: o_h * lax.rsqrt(jnp.mean(o_h * o_h, axis=-1, keepdims=True) + EPS), o_ref[...])
        cat_ref[0] = (on * gnw_ref[...] * (og * _sigmoid(og))).astype(cat_ref.dtype)
        z = c_ref[...] * u_ref[...]
        halo = jnp.where(i > 0, ch_ref[...] * uh_ref[...], 0.0)
        z1, z2 = _conv_taps(z, halo, zbuf, tb)
        cw = cw_ref[...]
        yc = cw[2:3, :] * z + cw[1:2, :] * z1 + cw[0:1, :] * z2
        cat_ref[1] = (b_ref[...] * yc).astype(cat_ref.dtype)

    grp = lambda g: pl.BlockSpec((None, tb, GROUP), lambda i: (g, i, 0))
    prev = lambda g: pl.BlockSpec((None, SUBLANES, GROUP), lambda i: (g, jnp.maximum(i * hb - 1, 0), 0))
    vec = lambda r: pl.BlockSpec((r, GROUP), lambda i: (0, 0))
    return pl.pallas_call(
        body, name="gate_fwd", grid=(t // tb,),
        in_specs=[pl.BlockSpec((tb, GROUP), lambda i: (i, 0)), grp(3), vec(1), grp(4), grp(5), grp(6), prev(5), prev(6),
                  vec(3), ANY],
        out_specs=pl.BlockSpec((2, tb, GROUP), lambda i: (0, i, 0)),
        out_shape=jax.ShapeDtypeStruct((2, t, HGRN_WIDTH), BF16),
        scratch_shapes=[pltpu.VMEM((tb + SUBLANES, GROUP), F32)],
        compiler_params=_params("parallel"),
    )(o, proj, gate_norm_w, proj, proj, proj, proj, proj, conv_w, after)


def _ln_bwd(dy, xhat, rstd, g):
    dxhat = dy * g
    m1 = jnp.mean(dxhat, axis=-1, keepdims=True)
    m2 = jnp.mean(dxhat * xhat, axis=-1, keepdims=True)
    return rstd * (dxhat - m1 - xhat * m2)


def _layer_norm(pre):
    xc = pre - jnp.mean(pre, axis=-1, keepdims=True)
    rstd = lax.rsqrt(jnp.mean(xc * xc, axis=-1, keepdims=True) + EPS)
    return xc * rstd, rstd


def _sublayers(cat, x, target, w_out, w_ff1, w_ff2, g1, b1, g2, b2):
    t = x.shape[0]
    tm = min(t, 256)

    def body(cat_ref, x_ref, tg_ref, wo_ref, w1_ref, w2_ref, g1_ref, b1_ref, g2_ref, b2_ref,
             h1_ref, r_ref, da_ref, dp2b_ref, dp1_ref, dp1b_ref, dcat_ref, dg1_ref, db1_ref, dg2_ref, db2_ref, loss_ref):
        @pl.when(pl.program_id(0) == 0)
        def _():
            for ref in (dg1_ref, db1_ref, dg2_ref, db2_ref, loss_ref):
                ref[...] = jnp.zeros_like(ref)

        mix = _dot(cat_ref[0], wo_ref[0:GROUP, :], NN) + _dot(cat_ref[1], wo_ref[GROUP:2 * GROUP, :], NN)
        xhat1, rstd1 = _layer_norm(ALPHA * x_ref[...] + mix)
        h1 = xhat1 * g1_ref[...] + b1_ref[...]
        h1b = h1.astype(h1_ref.dtype)
        h1_ref[...] = h1b
        mlp = jnp.zeros((tm, D_MODEL), F32)
        for j in range(N_FF):
            cols = slice(j * FF_BLOCK, (j + 1) * FF_BLOCK)
            r = jnp.square(jnp.maximum(_dot(h1b, w1_ref[:, cols], NN), 0.0)).astype(r_ref.dtype)
            r_ref[:, cols] = r
            mlp = mlp + _dot(r, w2_ref[cols, :], NN)
        xhat2, rstd2 = _layer_norm(ALPHA * h1 + mlp)
        err = xhat2 * g2_ref[...] + b2_ref[...] - tg_ref[...]
        loss_ref[...] += 0.5 * jnp.sum(jnp.mean(err * err, axis=-1, keepdims=True))
        dy = err * (1.0 / D_MODEL)
        dg2_ref[...] += jnp.sum(dy * xhat2, axis=0, keepdims=True)
        db2_ref[...] += jnp.sum(dy, axis=0, keepdims=True)
        dp2 = _ln_bwd(dy, xhat2, rstd2, g2_ref[...])
        dp2b = dp2.astype(dp2b_ref.dtype)
        dp2b_ref[...] = dp2b
        back = jnp.zeros((tm, D_MODEL), F32)
        for j in range(N_FF):
            cols = slice(j * FF_BLOCK, (j + 1) * FF_BLOCK)
            dr = _dot(dp2b, w2_ref[cols, :], NT)
            da = (dr * (2.0 * jnp.sqrt(r_ref[:, cols].astype(F32)))).astype(da_ref.dtype)
            da_ref[:, cols] = da
            back = back + _dot(da, w1_ref[:, cols], NT)
        dh1 = ALPHA * dp2 + back
        dg1_ref[...] += jnp.sum(dh1 * xhat1, axis=0, keepdims=True)
        db1_ref[...] += jnp.sum(dh1, axis=0, keepdims=True)
        dp1 = _ln_bwd(dh1, xhat1, rstd1, g1_ref[...])
        dp1b = dp1.astype(dp1b_ref.dtype)
        dp1_ref[...] = dp1
        dp1b_ref[...] = dp1b
        dcat_ref[...] = _dot(dp1b, wo_ref[...], NT)

    row = pl.BlockSpec((tm, D_MODEL), lambda i: (i, 0))
    wide = pl.BlockSpec((tm, D_FF), lambda i: (i, 0))
    vec = pl.BlockSpec((1, D_MODEL), lambda i: (0, 0))
    narrow = lambda dtype: jax.ShapeDtypeStruct((t, D_MODEL), dtype)
    return pl.pallas_call(
        body, name="sublayers", grid=(t // tm,),
        in_specs=[pl.BlockSpec((2, tm, GROUP), lambda i: (0, i, 0)), row, row, _resident((D_MODEL, D_MODEL)),
                  _resident((D_MODEL, D_FF)), _resident((D_FF, D_MODEL)), vec, vec, vec, vec],
        out_specs=[row, wide, wide, row, row, row, row, vec, vec, vec, vec,
                   pl.BlockSpec((SUBLANES, LANES), lambda i: (0, 0))],
        out_shape=[narrow(BF16), jax.ShapeDtypeStruct((t, D_FF), BF16), jax.ShapeDtypeStruct((t, D_FF), BF16),
                   narrow(BF16), narrow(F32), narrow(BF16), narrow(F32)]
                  + [jax.ShapeDtypeStruct((1, D_MODEL), F32)] * 4 + [jax.ShapeDtypeStruct((SUBLANES, LANES), F32)],
        compiler_params=_params("arbitrary"),
    )(cat, x, target, w_out, w_ff1, w_ff2, g1, b1, g2, b2)


def _gate_bwd(dcat, o, proj, gate_norm_w, conv_w, after):
    t = proj.shape[1]
    tb = min(t, 512)
    hb = tb // SUBLANES
    nblk = t // tb

    def body(do2_ref, dy_ref, dyn_ref, o_ref, og_ref, gnw_ref, b_ref, bn_ref, c_ref, u_ref, ch_ref, uh_ref, cw_ref,
             after_ref, do_ref, dp_ref, dgnw_ref, dcw_ref, zbuf, dbuf):
        i = pl.program_id(0)

        @pl.when(i == 0)
        def _():
            dgnw_ref[...] = jnp.zeros_like(dgnw_ref)
            dcw_ref[...] = jnp.zeros_like(dcw_ref)

        ov, og, gnw, do2 = o_ref[...], og_ref[...], gnw_ref[...], do2_ref[...]
        rs = _per_head(lambda o_h: jnp.broadcast_to(lax.rsqrt(jnp.mean(o_h * o_h, axis=-1, keepdims=True) + EPS),
                                                    o_h.shape), ov)
        on = ov * rs
        sg = _sigmoid(og)
        sil = og * sg
        don = do2 * gnw * sil
        dgnw_ref[...] += jnp.sum(do2 * on * sil, axis=0, keepdims=True)
        dp_ref[0] = (do2 * on * gnw * (sg * (1.0 + og * (1.0 - sg)))).astype(dp_ref.dtype)
        do_ref[...] = rs * (don - on * _per_head(
            lambda p_h: jnp.broadcast_to(jnp.mean(p_h, axis=-1, keepdims=True), p_h.shape), don * on))

        bg, cg, u, dy = b_ref[...], c_ref[...], u_ref[...], dy_ref[...]
        z = cg * u
        halo = jnp.where(i > 0, ch_ref[...] * uh_ref[...], 0.0)
        z1, z2 = _conv_taps(z, halo, zbuf, tb)
        cw = cw_ref[...]
        yc = cw[2:3, :] * z + cw[1:2, :] * z1 + cw[0:1, :] * z2
        dyc = dy * bg
        dbuf[0:tb, :] = dyc
        dbuf[tb:tb + SUBLANES, :] = jnp.where(i < nblk - 1, dyn_ref[...] * bn_ref[...], 0.0)
        d1, d2 = dbuf[1:1 + tb, :], dbuf[2:2 + tb, :]
        dz = cw[2:3, :] * dyc + cw[1:2, :] * d1 + cw[0:1, :] * d2
        dp_ref[1] = (dy * yc).astype(dp_ref.dtype)
        dp_ref[2] = (dz * u).astype(dp_ref.dtype)
        dp_ref[3] = (dz * cg).astype(dp_ref.dtype)
        dcw_ref[0:1, :] += jnp.sum(dyc * z2, axis=0, keepdims=True)
        dcw_ref[1:2, :] += jnp.sum(dyc * z1, axis=0, keepdims=True)
        dcw_ref[2:3, :] += jnp.sum(dyc * z, axis=0, keepdims=True)

    half = lambda g: pl.BlockSpec((tb, GROUP), lambda i: (i, g))
    grp = lambda g: pl.BlockSpec((None, tb, GROUP), lambda i: (g, i, 0))
    prev = lambda g: pl.BlockSpec((None, SUBLANES, GROUP), lambda i: (g, jnp.maximum(i * hb - 1, 0), 0))
    nxt_row = lambda i: jnp.minimum((i + 1) * hb, t // SUBLANES - 1)
    nxt = lambda g: pl.BlockSpec((None, SUBLANES, GROUP), lambda i: (g, nxt_row(i), 0))
    vec = lambda r: pl.BlockSpec((r, GROUP), lambda i: (0, 0))
    return pl.pallas_call(
        body, name="gate_bwd", grid=(nblk,),
        in_specs=[half(0), half(1), pl.BlockSpec((SUBLANES, GROUP), lambda i: (nxt_row(i), 1)), half(0), grp(3), vec(1),
                  grp(4), nxt(4), grp(5), grp(6), prev(5), prev(6), vec(3), ANY],
        out_specs=[half(0), pl.BlockSpec((4, tb, GROUP), lambda i: (0, i, 0)), vec(1), vec(3)],
        out_shape=[jax.ShapeDtypeStruct((t, HGRN_WIDTH), F32), jax.ShapeDtypeStruct((4, t, HGRN_WIDTH), BF16),
                   jax.ShapeDtypeStruct((1, HGRN_WIDTH), F32), jax.ShapeDtypeStruct((3, CONV_WIDTH), F32)],
        scratch_shapes=[pltpu.VMEM((tb + SUBLANES, GROUP), F32), pltpu.VMEM((tb + SUBLANES, GROUP), F32)],
        compiler_params=_params("arbitrary"),
    )(dcat, dcat, dcat, o, proj, gate_norm_w, proj, proj, proj, proj, proj, proj, conv_w, after)


def _hgrn_bwd(proj, do, states, lb_logits, after):
    t = proj.shape[1]
    tb = min(t, 512)
    ncb = tb // CHUNK
    nblk = t // tb

    def body(q_ref, f_ref, v_ref, do_ref, st_ref, lbl_ref, after_ref, dp_ref, dlbl_ref, ds_scr, dlb_scr):
        i = pl.program_id(0)

        @pl.when(i == 0)
        def _():
            ds_scr[...] = jnp.zeros_like(ds_scr)
            dlb_scr[...] = jnp.zeros_like(dlb_scr)

        lb, s1 = _lower_bound(lbl_ref[...])
        causal, anti = _chunk_masks()
        every = range(ncb)
        rows = [slice(c * CHUNK, (c + 1) * CHUNK) for c in every]
        q, v, do = ([ref[r, :] for r in rows] for ref in (q_ref, v_ref, do_ref))
        st = [st_ref[c] for c in every]
        gates = [_gates(f_ref[r, :], lb) for r in rows]
        sig, f, k = ([gt[n] for gt in gates] for n in (0, 1, 3))
        b = [_dot_exact(causal, gt[2]) for gt in gates]
        mid, last = [x[CHUNK // 2:CHUNK // 2 + 1, :] for x in b], [x[CHUNK - 1:CHUNK, :] for x in b]
        e_q = [jnp.exp(b[c] - mid[c]) for c in every]
        e_k = [jnp.exp(mid[c] - b[c]) for c in every]
        e_i = [jnp.exp(x) for x in b]
        e_s = [jnp.exp(last[c] - b[c]) for c in every]
        dec = [jnp.exp(x) for x in last]
        qt, kt, qi, ks = ([a[c] * e[c] for c in every] for a, e in ((q, e_q), (k, e_k), (q, e_i), (k, e_s)))

        def masked(a, b_):
            return [[jnp.where(causal, _dot(a_h, b_h, NT), 0.0) for a_h, b_h in zip(_heads(a[c]), _heads(b_[c]))]
                    for c in every]

        def with_scores(s, other, dims):
            return [jnp.concatenate([_dot(s_h, o_h, dims) for s_h, o_h in zip(s[c], _heads(other[c]))], axis=1)
                    for c in every]

        def per_head(dims, a, b_):
            return [_per_head(lambda a_h, b_h: _dot(a_h, b_h, dims), a[c], b_[c]) for c in every]

        scores, dscores = masked(qt, kt), masked(do, v)
        dqt, dkt, dv_intra = with_scores(dscores, kt, NN), with_scores(dscores, qt, TN), with_scores(scores, do, TN)
        dqi, update = per_head(NN, do, st), per_head(TN, do, qi)

        dst = ds_scr[...]
        dsts = [None] * ncb
        for c in reversed(every):
            dsts[c] = dst
            dst = dec[c] * dst + update[c]
        ds_scr[...] = dst

        dv_state, dks = per_head(NT, ks, dsts), per_head(NN, v, dsts)
        ddec = [jnp.sum(dsts[c] * st[c], axis=0, keepdims=True) for c in every]
        dq = [dqt[c] * e_q[c] + dqi[c] * e_i[c] for c in every]
        dk = [dkt[c] * e_k[c] + dks[c] * e_s[c] for c in every]
        db = [q[c] * dq[c] - k[c] * dk[c] for c in every]
        db_last = [jnp.sum(dks[c] * ks[c], axis=0, keepdims=True) + ddec[c] * dec[c] for c in every]
        dg = [_dot_exact(anti, db[c]) + db_last[c] for c in every]
        df = [dg[c] / f[c] - dk[c] for c in every]
        dlb_scr[...] += sum(jnp.sum(df[c] * (1.0 - sig[c]), axis=0, keepdims=True) for c in every)
        dfp = [df[c] * (1.0 - lb) * sig[c] * (1.0 - sig[c]) for c in every]
        dv = [dv_intra[c] + dv_state[c] for c in every]
        for n, parts in enumerate((dq, dfp, dv)):
            dp_ref[n] = jnp.concatenate(parts, axis=0).astype(dp_ref.dtype)

        @pl.when(i == nblk - 1)
        def _():
            dlb = dlb_scr[...]
            dlbl_ref[0:1, :] = dlb * lb * (1.0 - lb)
            dlbl_ref[1:2, :] = -dlb * lb * s1

    grp = lambda g: pl.BlockSpec((None, tb, GROUP), lambda i: (g, nblk - 1 - i, 0))
    vec = pl.BlockSpec((2, HGRN_WIDTH), lambda i: (0, 0))
    return pl.pallas_call(
        body, name="hgrn_bwd", grid=(nblk,),
        in_specs=[grp(0), grp(1), grp(2), pl.BlockSpec((tb, HGRN_WIDTH), lambda i: (nblk - 1 - i, 0)),
                  pl.BlockSpec((ncb, HEAD_DIM, HGRN_WIDTH), lambda i: (nblk - 1 - i, 0, 0)), vec, ANY],
        out_specs=[pl.BlockSpec((3, tb, HGRN_WIDTH), lambda i: (0, nblk - 1 - i, 0)), vec],
        out_shape=[jax.ShapeDtypeStruct((3, t, HGRN_WIDTH), BF16), jax.ShapeDtypeStruct((2, HGRN_WIDTH), F32)],
        scratch_shapes=[pltpu.VMEM((HEAD_DIM, HGRN_WIDTH), F32), pltpu.VMEM((1, HGRN_WIDTH), F32)],
        compiler_params=_params("arbitrary"),
    )(proj, proj, proj, do, states, lb_logits, after)


def _in_bwd(dph, dpg, w_in, dpre1, after):
    t = dpre1.shape[0]
    tm = min(t, 512)

    def body(dh_ref, dg_ref, w_ref, dp_ref, after_ref, o_ref):
        acc = ALPHA * dp_ref[...]
        for g in range(N_GROUPS):
            part = dh_ref[g] if g < 3 else dg_ref[g - 3]
            acc = acc + _dot(part, w_ref[:, g * GROUP:(g + 1) * GROUP], NT)
        o_ref[...] = acc

    row = pl.BlockSpec((tm, D_MODEL), lambda i: (i, 0))
    return pl.pallas_call(
        body, name="in_bwd", grid=(t // tm,),
        in_specs=[pl.BlockSpec((3, tm, GROUP), lambda i: (0, i, 0)), pl.BlockSpec((4, tm, GROUP), lambda i: (0, i, 0)),
                  _resident((D_MODEL, IN_COLS)), row, ANY],
        out_specs=row,
        out_shape=jax.ShapeDtypeStruct((t, D_MODEL), F32),
        compiler_params=_params("parallel"),
    )(dph, dpg, w_in, dpre1, after)


def _grad_w(name, operands, widths, shape, step, after=None):
    t = operands[0].shape[-2]
    tt = min(t, 512)
    n_in, n_steps = len(operands), t // tt
    in_specs = [pl.BlockSpec((tt, w), lambda k: (k, 0)) if a.ndim == 2 else
                pl.BlockSpec((a.shape[0], tt, w), lambda k: (0, k, 0)) for a, w in zip(operands, widths)]
    extra = [] if after is None else [after]

    def body(*refs):
        o_ref, acc, narrow, sem = refs[-4:]
        k = pl.program_id(0)

        @pl.when(k == 0)
        def _():
            acc[...] = jnp.zeros_like(acc)

        step(acc, *refs[:n_in])

        @pl.when(k == n_steps - 1)
        def _():
            narrow[...] = acc[...].astype(narrow.dtype)
            out = pltpu.make_async_copy(narrow, o_ref, sem)
            out.start()
            out.wait()

    return pl.pallas_call(
        body, name=name, grid=(n_steps,), in_specs=in_specs + [ANY] * len(extra), out_specs=ANY,
        out_shape=jax.ShapeDtypeStruct(shape, BF16),
        scratch_shapes=[pltpu.VMEM(shape, F32), pltpu.VMEM(shape, BF16), pltpu.SemaphoreType.DMA],
        compiler_params=_params("arbitrary"),
    )(*operands, *extra)


def _dw_in(xb, dph, dpg, after):
    def step(acc, x_ref, dh_ref, dg_ref):
        xv = x_ref[...]
        for g in range(N_GROUPS):
            part = dh_ref[g] if g < 3 else dg_ref[g - 3]
            acc[:, g * GROUP:(g + 1) * GROUP] += _dot(xv, part, TN)

    return _grad_w("dw_in", (xb, dph, dpg), (D_MODEL, GROUP, GROUP), (D_MODEL, IN_COLS), step, after)


def _dw_out(cat, dpre1b):
    def step(acc, cat_ref, d_ref):
        dv = d_ref[...]
        for g in range(2):
            acc[g * GROUP:(g + 1) * GROUP, :] += _dot(cat_ref[g], dv, TN)

    return _grad_w("dw_out", (cat, dpre1b), (GROUP, D_MODEL), (D_MODEL, D_MODEL), step)


def _dw_ff1(h1b, da):
    def step(acc, h_ref, da_ref):
        hv = h_ref[...]
        for j in range(D_FF // FF_BLOCK):
            cols = slice(j * FF_BLOCK, (j + 1) * FF_BLOCK)
            acc[:, cols] += _dot(hv, da_ref[:, cols], TN)

    return _grad_w("dw_ff1", (h1b, da), (D_MODEL, D_FF), (D_MODEL, D_FF), step)


def _dw_ff2(r, dpre2b):
    def step(acc, r_ref, d_ref):
        dv = d_ref[...]
        for j in range(D_FF // FF_BLOCK):
            rows = slice(j * FF_BLOCK, (j + 1) * FF_BLOCK)
            acc[rows, :] += _dot(r_ref[:, rows], dv, TN)

    return _grad_w("dw_ff2", (r, dpre2b), (D_FF, D_MODEL), (D_FF, D_MODEL), step)


def _place():
    x, y, c = lax.axis_index("x"), lax.axis_index("y"), lax.axis_index("c")
    return x, y, c, 2 * x + y


def _other_chips(x, y):
    return [(1 - x, y), (x, 1 - y), (1 - x, 1 - y)]


def _place_shard(name, w, chip, cols_sharded, after=None):
    rows, cols = w.shape
    tr = min(rows, 256)
    nb = rows // tr
    full = (rows, cols * N_CHIPS) if cols_sharded else (rows * N_CHIPS, cols)
    out_map = (lambda i, s: (i, s[0])) if cols_sharded else (lambda i, s: (s[0] * nb + i, 0))

    def body(s_ref, w_ref, *rest):
        rest[-1][...] = w_ref[...].astype(rest[-1].dtype)

    extra = [] if after is None else [after]
    return pl.pallas_call(
        body, name=name,
        grid_spec=pltpu.PrefetchScalarGridSpec(
            num_scalar_prefetch=1, grid=(nb,),
            in_specs=[pl.BlockSpec((tr, cols), lambda i, s: (i, 0))] + [ANY] * len(extra),
            out_specs=pl.BlockSpec((tr, cols), out_map)),
        out_shape=jax.ShapeDtypeStruct(full, BF16),
        compiler_params=_params("parallel"),
    )(chip, w, *extra)


HBM = pl.BlockSpec(memory_space=pltpu.HBM)
SEM = pl.BlockSpec(memory_space=pltpu.SEMAPHORE)
EFFECT = pltpu.SideEffectType.DATAFLOW_SIDE_EFFECTING


class _Split:
    def __init__(self, name, arrays, plan):
        n, n_copies = len(arrays), plan.count
        self.name, self.plan, self.n = name, plan, n

        def body(*refs):
            send_sems, recv_sems, token = refs[n], refs[n + 1], refs[-1]
            for k, (src, dst, to) in enumerate(plan(refs[:n])):
                pltpu.make_async_remote_copy(src_ref=src, dst_ref=dst, send_sem=send_sems.at[k], recv_sem=recv_sems.at[k],
                                             device_id=to, device_id_type=MESH).start()
            token[...] = jnp.zeros_like(token)

        outs = pl.pallas_call(
            body, name=name + "_start",
            out_shape=(pltpu.SemaphoreType.DMA((n_copies,)), pltpu.SemaphoreType.DMA((n_copies,)),
                       *[pltpu.HBM(a.shape, a.dtype) for a in arrays], jax.ShapeDtypeStruct((SUBLANES, LANES), F32)),
            in_specs=(HBM,) * n, out_specs=(SEM, SEM) + (HBM,) * n + (pl.BlockSpec(memory_space=pltpu.VMEM),),
            input_output_aliases={i: 2 + i for i in range(n)},
            compiler_params=pltpu.CompilerParams(has_side_effects=EFFECT),
        )(*[pltpu.with_memory_space_constraint(a, pltpu.HBM) for a in arrays])
        self.sems, self.arrays, self.token = outs[:2], outs[2:2 + n], outs[-1]

    def wait(self, after):
        n, plan = self.n, self.plan

        def body(*refs):
            send_sems, recv_sems = refs[n], refs[n + 1]
            for k, (src, dst, to) in enumerate(plan(refs[:n])):
                cp = pltpu.make_async_remote_copy(src_ref=src, dst_ref=dst, send_sem=send_sems.at[k],
                                                  recv_sem=recv_sems.at[k], device_id=to, device_id_type=MESH)
                cp.wait_send()
                cp.wait_recv()

        return pl.pallas_call(
            body, name=self.name + "_wait", out_shape=tuple(pltpu.HBM(a.shape, a.dtype) for a in self.arrays),
            in_specs=(HBM,) * n + (SEM, SEM, ANY), out_specs=(HBM,) * n, input_output_aliases={i: i for i in range(n)},
            compiler_params=pltpu.CompilerParams(has_side_effects=EFFECT),
        )(*self.arrays, *self.sems, after)


COLS_SHARDED = (True, False, True, False)
HALF_SHAPES = [(D_MODEL // 2, IN_COLS), (D_MODEL, D_MODEL // 2), (D_MODEL // 2, D_FF), (D_FF, D_MODEL // 2)]
PIECE_SHAPES = [(D_MODEL // 2, IN_COLS // N_CHIPS), (D_MODEL // N_CHIPS, D_MODEL // 2),
                (D_MODEL // 2, D_FF // N_CHIPS), (D_FF // N_CHIPS, D_MODEL // 2)]


def _shard_view(kind, ref, chip):
    if COLS_SHARDED[kind]:
        n = ref.shape[1] // N_CHIPS
        return ref.at[:, pl.ds(chip * n, n)]
    n = ref.shape[0] // N_CHIPS
    return ref.at[pl.ds(chip * n, n), :]


def _half_view(kind, ref, h):
    if COLS_SHARDED[kind]:
        n = ref.shape[0] // 2
        return ref.at[pl.ds(h * n, n), :]
    n = ref.shape[1] // 2
    return ref.at[:, pl.ds(h * n, n)]


def _plan(count):
    def mark(fn):
        fn.count = count
        return fn
    return mark


def _shard_rows_view(kind, ref, chip, part, n_parts):
    if COLS_SHARDED[kind]:
        m, n = ref.shape[0] // n_parts, ref.shape[1] // N_CHIPS
        return ref.at[pl.ds(part * m, m), pl.ds(chip * n, n)]
    m = ref.shape[0] // N_CHIPS // n_parts
    return ref.at[pl.ds((n_parts * chip + part) * m, m), :]


def _shard_half_view(kind, ref, chip, h):
    return _shard_rows_view(kind, ref, chip, h, 2)


def _gather_over_ici(kinds, weights):
    @_plan(2 * len(kinds))
    def plan(refs):
        x, y, c, me = _place()
        mine = [_shard_half_view(kind, ref, me, c) for kind, ref in zip(kinds, refs)]
        return [(v, v, to) for v in mine for to in ((1 - x, y, c), (x, 1 - y, c))]

    return _Split("gather_ici_" + "".join(map(str, kinds)), tuple(weights), plan)


def _relay_over_ici(kinds, weights):
    @_plan(2 * len(kinds))
    def plan(refs):
        x, y, c, _ = _place()
        x_nbr, y_nbr = 2 * (1 - x) + y, 2 * x + (1 - y)
        out = []
        for kind, ref in zip(kinds, refs):
            first, second = (_shard_rows_view(kind, ref, chip, 2 * c + q, 4) for q, chip in ((0, x_nbr), (1, y_nbr)))
            out += [(first, first, (x, 1 - y, c)), (second, second, (1 - x, y, c))]
        return out

    return _Split("relay_ici_" + "".join(map(str, kinds)), tuple(weights), plan)


def _gather_w_in_over_ici(w_in, conv4):
    @_plan(6)
    def plan(refs):
        x, y, c, me = _place()
        half, conv = _shard_half_view(0, refs[0], me, c), refs[1].at[me]
        return [(v, v, (px, py, c)) for v in (half, conv) for px, py in _other_chips(x, y)]

    return _Split("gather_w_in_ici", (w_in, conv4), plan)


def _gather_over_d2d(kinds, weights):
    @_plan(3 * len(kinds))
    def plan(refs):
        x, y, c, _ = _place()
        got = [_shard_half_view(kind, ref, 2 * px + py, c) for kind, ref in zip(kinds, refs)
               for px, py in _other_chips(x, y)]
        return [(v, v, (x, y, 1 - c)) for v in got]

    return _Split("gather_d2d_" + "".join(map(str, kinds)), tuple(weights), plan)


def _swap_halves(kinds, grads):
    @_plan(len(kinds))
    def plan(refs):
        x, y, c, _ = _place()
        return [(_half_view(kind, g, 1 - c), land, (x, y, 1 - c))
                for kind, g, land in zip(kinds, refs[:len(kinds)], refs[len(kinds):])]

    lands = [lax.empty(HALF_SHAPES[kind], g.dtype) for kind, g in zip(kinds, grads)]
    return _Split("swap_halves_" + "".join(map(str, kinds)), (*grads, *lands), plan)


def _add_half(name, g, recv, core, rows_split):
    shape = recv.shape
    tr = min(shape[0], 128 if rows_split else 256)
    nb = shape[0] // tr

    def body(c_ref, g_ref, r_ref, o_ref):
        o_ref[...] = (g_ref[...].astype(F32) + r_ref[...].astype(F32)).astype(o_ref.dtype)

    g_map = (lambda i, c_ref: (c_ref[0] * nb + i, 0)) if rows_split else (lambda i, c_ref: (i, c_ref[0]))
    blk = pl.BlockSpec((tr, shape[1]), lambda i, c_ref: (i, 0))
    return pl.pallas_call(
        body, name=name,
        grid_spec=pltpu.PrefetchScalarGridSpec(
            num_scalar_prefetch=1, grid=(nb,),
            in_specs=[pl.BlockSpec((tr, shape[1]), g_map), blk], out_specs=blk),
        out_shape=jax.ShapeDtypeStruct(shape, BF16),
        compiler_params=_params("parallel"),
    )(core, g, recv)


def _exchange_pieces(kinds, halves, pack=None):
    n_p, n = N_CHIPS - 1, len(kinds)

    @_plan(n_p * n + (0 if pack is None else N_DEV - 1))
    def plan(refs):
        x, y, c, _ = _place()
        copies = []
        if pack is not None:
            me = 4 * x + 2 * y + c
            peers = [((1 - x) if m & 4 else x, (1 - y) if m & 2 else y, (1 - c) if m & 1 else c) for m in range(1, N_DEV)]
            copies += [(refs[2 * n], refs[2 * n + 1].at[me], peer) for peer in peers]
        return copies + [(_shard_view(kind, half, 2 * px + py), land.at[j], (px, py, c))
                         for j, (px, py) in enumerate(_other_chips(x, y))
                         for kind, half, land in zip(kinds, refs[:n], refs[n:2 * n])]

    lands = [lax.empty((n_p,) + PIECE_SHAPES[kind], BF16) for kind in kinds]
    small = () if pack is None else (pack, lax.empty((N_DEV,) + pack.shape, F32))
    return _Split("exchange_pieces_" + "".join(map(str, kinds)), (*halves, *lands, *small), plan)


def _sum_pieces(name, half, slots, place, rows_split, after):
    n_p, rows, cols = slots.shape
    tr = min(rows, 256)
    nb = rows // tr
    if rows_split:
        own_map = lambda i, s: (i, s[0])
        out_map = lambda i, s: (s[1] * nb + i, 0)
        shard = (2 * rows, cols)
    else:
        own_map = lambda i, s: (s[0] * nb + i, 0)
        out_map = lambda i, s: (i, s[1])
        shard = (rows, 2 * cols)

    def body(s_ref, own_ref, slot_ref, after_ref, o_ref):
        total = own_ref[...].astype(F32)
        for j in range(n_p):
            total = total + slot_ref[j].astype(F32)
        o_ref[...] = total

    return pl.pallas_call(
        body, name=name,
        grid_spec=pltpu.PrefetchScalarGridSpec(
            num_scalar_prefetch=1, grid=(nb,),
            in_specs=[pl.BlockSpec((tr, cols), own_map), pl.BlockSpec((n_p, tr, cols), lambda i, s: (0, i, 0)), ANY],
            out_specs=pl.BlockSpec((tr, cols), out_map)),
        out_shape=jax.ShapeDtypeStruct(shard, F32),
        compiler_params=_params("parallel"),
    )(place, half, slots, after)


def _join_halves(kinds, shards):
    @_plan(len(kinds))
    def plan(refs):
        x, y, c, _ = _place()
        return [(_half_view(kind, g, c), _half_view(kind, g, c), (x, y, 1 - c)) for kind, g in zip(kinds, refs)]

    return _Split("join_halves_" + "".join(map(str, kinds)), tuple(shards), plan)


N_DEV = 8


def _sum_shared(pack, land, device):
    def body(d_ref, p_ref, l_ref, o_ref):
        me = d_ref[0]
        total = jnp.where(me == 0, p_ref[...], l_ref[0])
        for d in range(1, N_DEV):
            total = total + jnp.where(me == d, p_ref[...], l_ref[d])
        o_ref[...] = total

    return pl.pallas_call(
        body, name="sum_shared",
        grid_spec=pltpu.PrefetchScalarGridSpec(
            num_scalar_prefetch=1, grid=(1,),
            in_specs=[pl.BlockSpec(pack.shape, lambda i, d: (0, 0)), pl.BlockSpec(land.shape, lambda i, d: (0, 0, 0))],
            out_specs=pl.BlockSpec(pack.shape, lambda i, d: (0, 0))),
        out_shape=jax.ShapeDtypeStruct(pack.shape, F32),
    )(device, pack, land)


def _adamw(name, w, g, m, v, after=None):
    rows, cols = w.shape
    tr = min(rows, 256)
    extra = [] if after is None else [after]

    def body(w_ref, g_ref, m_ref, v_ref, *rest):
        d_ref, nm_ref, nv_ref = rest[-3:]
        gv = g_ref[...]
        nm = ADAM_B1 * m_ref[...] + (1.0 - ADAM_B1) * gv
        nv = ADAM_B2 * v_ref[...] + (1.0 - ADAM_B2) * jnp.square(gv)
        m_hat = nm * (1.0 / (1.0 - ADAM_B1 ** ADAM_STEP))
        v_hat = nv * (1.0 / (1.0 - ADAM_B2 ** ADAM_STEP))
        d_ref[...] = -ADAM_LR * (m_hat / (jnp.sqrt(v_hat) + ADAM_EPS) + ADAM_WD * w_ref[...])
        nm_ref[...] = nm
        nv_ref[...] = nv

    blk = pl.BlockSpec((tr, cols), lambda i: (i, 0))
    return pl.pallas_call(
        body, name=name, grid=(rows // tr,), in_specs=[blk] * 4 + [ANY] * len(extra), out_specs=[blk] * 3,
        out_shape=[jax.ShapeDtypeStruct(w.shape, F32)] * 3,
        compiler_params=_params("parallel"),
    )(w, g, m, v, *extra)


def kernel(x, w_in, lb_logits, gate_norm_w, conv_w, w_out, ln1_g, ln1_b, w_ff1, w_ff2, ln2_g, ln2_b, loss_target, m_w_in, m_lb_logits, m_gate_norm_w, m_conv_w, m_w_out, m_ln1_g, m_ln1_b, m_w_ff1, m_w_ff2, m_ln2_g, m_ln2_b, v_w_in, v_lb_logits, v_gate_norm_w, v_conv_w, v_w_out, v_ln1_g, v_ln1_b, v_w_ff1, v_w_ff2, v_ln2_g, v_ln2_b):
    xs, tgt = x[0], loss_target[0]
    chip = 2 * lax.axis_index("x") + lax.axis_index("y")
    core = lax.axis_index("c").astype(jnp.int32).reshape(1)
    chip1 = chip.astype(jnp.int32).reshape(1)
    place = jnp.concatenate([chip1, core])

    conv4 = lax.dynamic_update_slice(jnp.zeros((N_CHIPS,) + conv_w.shape[1:], F32), conv_w, (chip, 0, 0))
    ici_in = _gather_w_in_over_ici(_place_shard("place_w_in", w_in[0], chip1, True), conv4)
    rest = (1, 2, 3)
    ici_rest = _gather_over_ici(rest, (_place_shard("place_w_out", w_out[0], chip1, False, after=ici_in.token),
                                       _place_shard("place_w_ff1", w_ff1[0], chip1, True, after=ici_in.token),
                                       _place_shard("place_w_ff2", w_ff2[0], chip1, False, after=ici_in.token)))
    wb_in, cv4 = ici_in.wait(ici_rest.token)
    d2d_in = _gather_over_d2d((0,), (wb_in,))
    wb_in, = d2d_in.wait(d2d_in.token)
    conv_full = cv4.transpose(1, 0, 2).reshape(3, CONV_WIDTH)

    proj, xb = _in_proj(xs, wb_in, ici_rest.token)
    relay_rest = _relay_over_ici(rest, ici_rest.wait(proj))
    o, states = _hgrn_fwd(proj, lb_logits, relay_rest.token)
    d2d_rest = _gather_over_d2d(rest, relay_rest.wait(o))
    cat = _gate_fwd(proj, o, gate_norm_w, conv_full, d2d_rest.token)
    wb_out, wb_ff1, wb_ff2 = d2d_rest.wait(cat)

    (h1b, r, da, dpre2b, dpre1, dpre1b, dcat, g_ln1_g, g_ln1_b, g_ln2_g, g_ln2_b, loss8) = _sublayers(
        cat, xs, tgt, wb_out, wb_ff1, wb_ff2, ln1_g, ln1_b, ln2_g, ln2_b)

    names = ("w_in", "w_out", "w_ff1", "w_ff2")

    def add_halves(kinds, grads, lands):
        return [_add_half("add_half_" + names[k], g, ld, core, COLS_SHARDED[k]) for k, g, ld in zip(kinds, grads, lands)]

    def sum_pieces(kinds, halves, lands, after):
        return [_sum_pieces("sum_pieces_" + names[k], h, ld, place, COLS_SHARDED[k], after)
                for k, h, ld in zip(kinds, halves, lands)]

    early = (1, 2, 3)
    swap = _swap_halves(early, (_dw_out(cat, dpre1b), _dw_ff1(h1b, da), _dw_ff2(r, dpre2b)))
    do, dpg, g_gnw, g_conv = _gate_bwd(dcat, o, proj, gate_norm_w, conv_full, swap.token)
    swapped = swap.wait(do)
    exch = _exchange_pieces(early, add_halves(early, swapped[:3], swapped[3:]))
    dph, g_lbl = _hgrn_bwd(proj, do, states, lb_logits, exch.token)
    g_in_local = _dw_in(xb, dph, dpg, dph)

    late = (0,)
    swap = _swap_halves(late, (g_in_local,))
    grad_x = _in_bwd(dph, dpg, wb_in, dpre1, swap.token)
    exchanged = exch.wait(grad_x)
    pack = jnp.concatenate([
        g_ln1_g, g_ln1_b, g_ln2_g, g_ln2_b,
        jnp.concatenate([g_lbl[0:1], g_lbl[1:2]], axis=1),
        jnp.concatenate([g_gnw, g_conv[0:1]], axis=1),
        jnp.concatenate([g_conv[1:2], g_conv[2:3]], axis=1),
        jnp.concatenate([loss8[0:1], jnp.zeros((1, D_MODEL - LANES), F32)], axis=1)], axis=0)
    swapped = swap.wait(exchanged[0])
    exch = _exchange_pieces(late, add_halves(late, swapped[:1], swapped[1:]), pack)
    join = _join_halves(early, sum_pieces(early, exchanged[:3], exchanged[3:], exch.token))
    g_w_out, g_w_ff1, g_w_ff2 = join.wait(join.token)
    d_ff1, nm_ff1, nv_ff1 = _adamw("adamw_w_ff1", w_ff1[0], g_w_ff1, m_w_ff1[0], v_w_ff1[0])
    d_ff2, nm_ff2, nv_ff2 = _adamw("adamw_w_ff2", w_ff2[0], g_w_ff2, m_w_ff2[0], v_w_ff2[0], d_ff1)
    d_out, nm_out, nv_out = _adamw("adamw_w_out", w_out[0], g_w_out, m_w_out[0], v_w_out[0], d_ff2)
    exchanged = exch.wait(d_out)
    tot = _sum_shared(exchanged[2], exchanged[3], 2 * chip1 + core)
    loss = tot[7, 0]
    half = D_MODEL // 2
    g_lb_logits = jnp.concatenate([tot[4:5, :half], tot[4:5, half:]], axis=0)
    g_gate_norm_w = tot[5:6, :half]
    g_conv_full = jnp.concatenate([tot[5:6, half:], tot[6:7, :half], tot[6:7, half:]], axis=0)
    g_conv_w = lax.dynamic_slice(g_conv_full, (0, chip * LANES), (3, LANES))
    join = _join_halves(late, sum_pieces(late, exchanged[:1], exchanged[1:2], tot))

    def small_pack(lbl, gnw, cv, l1g, l1b, l2g, l2b):
        pad = jnp.zeros((1, D_MODEL - 3 * LANES), F32)
        return jnp.concatenate([
            l1g, l1b, l2g, l2b, jnp.concatenate([lbl[0:1], lbl[1:2]], axis=1),
            jnp.concatenate([gnw, jnp.zeros((1, half), F32)], axis=1),
            jnp.concatenate([cv[0:1], cv[1:2], cv[2:3], pad], axis=1), jnp.zeros((1, D_MODEL), F32)], axis=0)

    w_s = small_pack(lb_logits, gate_norm_w, conv_w[0], ln1_g, ln1_b, ln2_g, ln2_b)
    g_s = small_pack(g_lb_logits, g_gate_norm_w, g_conv_w, tot[0:1], tot[1:2], tot[2:3], tot[3:4])
    m_s = small_pack(m_lb_logits, m_gate_norm_w, m_conv_w[0], m_ln1_g, m_ln1_b, m_ln2_g, m_ln2_b)
    v_s = small_pack(v_lb_logits, v_gate_norm_w, v_conv_w[0], v_ln1_g, v_ln1_b, v_ln2_g, v_ln2_b)
    d_s, nm_s, nv_s = _adamw("adamw_small", w_s, g_s, m_s, v_s, join.token)
    g_w_in, = join.wait(d_s)
    d_in, nm_in, nv_in = _adamw("adamw_w_in", w_in[0], g_w_in, m_w_in[0], v_w_in[0])

    def unpack(p):
        lbl = jnp.concatenate([p[4:5, :half], p[4:5, half:]], axis=0)
        cv = jnp.concatenate([p[6:7, 0:LANES], p[6:7, LANES:2 * LANES], p[6:7, 2 * LANES:3 * LANES]], axis=0)
        return dict(lb_logits=lbl, gate_norm_w=p[5:6, :half], conv_w=cv[None], ln1_g=p[0:1], ln1_b=p[1:2],
                    ln2_g=p[2:3], ln2_b=p[3:4])

    order = ("w_in", "lb_logits", "gate_norm_w", "conv_w", "w_out", "ln1_g", "ln1_b", "w_ff1", "w_ff2", "ln2_g", "ln2_b")
    grad = dict(unpack(g_s), w_in=g_w_in[None], w_out=g_w_out[None], w_ff1=g_w_ff1[None], w_ff2=g_w_ff2[None])
    delta = dict(unpack(d_s), w_in=d_in[None], w_out=d_out[None], w_ff1=d_ff1[None], w_ff2=d_ff2[None])
    new_m = dict(unpack(nm_s), w_in=nm_in[None], w_out=nm_out[None], w_ff1=nm_ff1[None], w_ff2=nm_ff2[None])
    new_v = dict(unpack(nv_s), w_in=nv_in[None], w_out=nv_out[None], w_ff1=nv_ff1[None], w_ff2=nv_ff2[None])
    return (loss, grad_x[None], *[grad[n] for n in order], *[delta[n] for n in order],
            *[new_m[n] for n in order], *[new_v[n] for n in order])
```

```python
import jax
import jax.numpy as jnp
from jax import lax
from jax.experimental import pallas as pl
from jax.experimental.pallas import tpu as pltpu

F32 = jnp.float32
BF16 = jnp.bfloat16
MXU_DTYPE = jnp.bfloat16

D_MODEL = 1024
HGRN_WIDTH = 512
HEAD_DIM = 128
N_HEADS = 4
CONV_WIDTH = 512
CHUNK = 64
D_FF = 4096
IN_COLS = 3584
GROUP = 512
N_GROUPS = IN_COLS // GROUP
ALPHA = 2.0 ** 0.25
EPS = 1e-5
N_CHIPS = 4
ADAM_LR, ADAM_B1, ADAM_B2, ADAM_EPS, ADAM_WD, ADAM_STEP = 0.001, 0.9, 0.999, 1e-08, 0.01, 10

LANES = 128
SUBLANES = 8
VMEM_LIMIT = 56 * 1024 * 1024
FF_BLOCK = 1024
N_FF = D_FF // FF_BLOCK

NN = (((1,), (0,)), ((), ()))
NT = (((1,), (1,)), ((), ()))
TN = (((0,), (0,)), ((), ()))
MESH = pl.DeviceIdType.MESH
ANY = pl.BlockSpec(memory_space=pl.ANY)


def _dot(a, b, dims):
    return lax.dot_general(a.astype(MXU_DTYPE), b.astype(MXU_DTYPE), dims, preferred_element_type=F32)


def _dot_exact(ones, v):
    ones = ones.astype(jnp.bfloat16)
    hi = v.astype(jnp.bfloat16)
    rest = v - hi.astype(F32)
    mid = rest.astype(jnp.bfloat16)
    low = (rest - mid.astype(F32)).astype(jnp.bfloat16)
    return sum(lax.dot_general(ones, part, NN, preferred_element_type=F32) for part in (hi, mid, low))


def _params(*sem):
    return pltpu.CompilerParams(dimension_semantics=sem, vmem_limit_bytes=VMEM_LIMIT)


def _resident(shape):
    return pl.BlockSpec(shape, lambda *_: (0,) * len(shape), pipeline_mode=pl.Buffered(1))


def _sigmoid(v):
    return 1.0 / (1.0 + jnp.exp(-v))


def _lower_bound(lbl):
    m = jnp.max(lbl, axis=0, keepdims=True)
    e = jnp.exp(lbl - m)
    s = e / jnp.sum(e, axis=0, keepdims=True)
    return s[0:1, :], s[1:2, :]


def _heads(v):
    return [v[:, h * HEAD_DIM:(h + 1) * HEAD_DIM] for h in range(N_HEADS)]


def _per_head(fn, *arrays):
    return jnp.concatenate([fn(*parts) for parts in zip(*map(_heads, arrays))], axis=1)


def _in_proj(x, w_in, after):
    t = x.shape[0]
    tm = min(t, 512)

    def body(x_ref, w_ref, after_ref, o_ref, xb_ref):
        xb = x_ref[...].astype(xb_ref.dtype)
        xb_ref[...] = xb
        for g in range(N_GROUPS):
            o_ref[g] = _dot(xb, w_ref[:, g * GROUP:(g + 1) * GROUP], NN)

    return pl.pallas_call(
        body, name="in_proj", grid=(t // tm,),
        in_specs=[pl.BlockSpec((tm, D_MODEL), lambda i: (i, 0)), _resident((D_MODEL, IN_COLS)), ANY],
        out_specs=[pl.BlockSpec((N_GROUPS, tm, GROUP), lambda i: (0, i, 0)), pl.BlockSpec((tm, D_MODEL), lambda i: (i, 0))],
        out_shape=[jax.ShapeDtypeStruct((N_GROUPS, t, GROUP), F32), jax.ShapeDtypeStruct((t, D_MODEL), BF16)],
        compiler_params=_params("parallel"),
    )(x, w_in, after)


def _gates(fp, lb):
    sig = _sigmoid(fp)
    f = lb + (1.0 - lb) * sig
    return sig, f, jnp.log(f), 1.0 - f


def _chunk_masks():
    row = lax.broadcasted_iota(jnp.int32, (CHUNK, CHUNK), 0)
    col = lax.broadcasted_iota(jnp.int32, (CHUNK, CHUNK), 1)
    return row >= col, row <= col


def _hgrn_fwd(proj, lb_logits, after):
    t = proj.shape[1]
    tb = min(t, 512)
    ncb = tb // CHUNK

    def body(q_ref, f_ref, v_ref, lbl_ref, after_ref, o_ref, st_ref, s_scr):
        @pl.when(pl.program_id(0) == 0)
        def _():
            s_scr[...] = jnp.zeros_like(s_scr)

        lb, _ = _lower_bound(lbl_ref[...])
        causal, _ = _chunk_masks()

        every = range(ncb)
        rows = [slice(c * CHUNK, (c + 1) * CHUNK) for c in every]
        q, v = [q_ref[r, :] for r in rows], [v_ref[r, :] for r in rows]
        gates = [_gates(f_ref[r, :], lb) for r in rows]
        k = [gt[3] for gt in gates]
        b = [_dot_exact(causal, gt[2]) for gt in gates]
        mid, last = [x[CHUNK // 2:CHUNK // 2 + 1, :] for x in b], [x[CHUNK - 1:CHUNK, :] for x in b]
        qt = [q[c] * jnp.exp(b[c] - mid[c]) for c in every]
        kt = [k[c] * jnp.exp(mid[c] - b[c]) for c in every]
        qi = [q[c] * jnp.exp(b[c]) for c in every]
        ks = [k[c] * jnp.exp(last[c] - b[c]) for c in every]
        dec = [jnp.exp(x) for x in last]
        scores = [[jnp.where(causal, _dot(a, b_, NT), 0.0) for a, b_ in zip(_heads(qt[c]), _heads(kt[c]))] for c in every]
        intra = [[_dot(s, v_h, NN) for s, v_h in zip(scores[c], _heads(v[c]))] for c in every]
        update = [_per_head(lambda v_h, ks_h: _dot(v_h, ks_h, TN), v[c], ks[c]) for c in every]

        st = s_scr[...]
        states = []
        for c in every:
            states.append(st)
            st_ref[c] = st
            st = dec[c] * st + update[c]
        s_scr[...] = st

        o_ref[...] = jnp.concatenate(
            [jnp.concatenate([i_h + _dot(qi_h, st_h, NT) for i_h, qi_h, st_h in
                              zip(intra[c], _heads(qi[c]), _heads(states[c]))], axis=1) for c in every], axis=0)

    grp = lambda g: pl.BlockSpec((None, tb, GROUP), lambda i: (g, i, 0))
    return pl.pallas_call(
        body, name="hgrn_fwd", grid=(t // tb,),
        in_specs=[grp(0), grp(1), grp(2), pl.BlockSpec((2, HGRN_WIDTH), lambda i: (0, 0)), ANY],
        out_specs=[pl.BlockSpec((tb, HGRN_WIDTH), lambda i: (i, 0)),
                   pl.BlockSpec((ncb, HEAD_DIM, HGRN_WIDTH), lambda i: (i, 0, 0))],
        out_shape=[jax.ShapeDtypeStruct((t, HGRN_WIDTH), F32),
                   jax.ShapeDtypeStruct((t // CHUNK, HEAD_DIM, HGRN_WIDTH), F32)],
        scratch_shapes=[pltpu.VMEM((HEAD_DIM, HGRN_WIDTH), F32)],
        compiler_params=_params("arbitrary"),
    )(proj, proj, proj, lb_logits, after)


def _conv_taps(z, halo, zbuf, tb):
    zbuf[0:SUBLANES, :] = halo
    zbuf[SUBLANES:SUBLANES + tb, :] = z
    return zbuf[SUBLANES - 1:SUBLANES - 1 + tb, :], zbuf[SUBLANES - 2:SUBLANES - 2 + tb, :]


def _gate_fwd(proj, o, gate_norm_w, conv_w, after):
    t = proj.shape[1]
    tb = min(t, 512)
    hb = tb // SUBLANES

    def body(o_ref, og_ref, gnw_ref, b_ref, c_ref, u_ref, ch_ref, uh_ref, cw_ref, after_ref, cat_ref, zbuf):
        i = pl.program_id(0)
        og = og_ref[...]
        on = _per_head(lambda o_h: o_h * lax.rsqrt(jnp.mean(o_h * o_h, axis=-1, keepdims=True) + EPS), o_ref[...])
        cat_ref[0] = (on * gnw_ref[...] * (og * _sigmoid(og))).astype(cat_ref.dtype)
        z = c_ref[...] * u_ref[...]
        halo = jnp.where(i > 0, ch_ref[...] * uh_ref[...], 0.0)
        z1, z2 = _conv_taps(z, halo, zbuf, tb)
        cw = cw_ref[...]
        yc = cw[2:3, :] * z + cw[1:2, :] * z1 + cw[0:1, :] * z2
        cat_ref[1] = (b_ref[...] * yc).astype(cat_ref.dtype)

    grp = lambda g: pl.BlockSpec((None, tb, GROUP), lambda i: (g, i, 0))
    prev = lambda g: pl.BlockSpec((None, SUBLANES, GROUP), lambda i: (g, jnp.maximum(i * hb - 1, 0), 0))
    vec = lambda r: pl.BlockSpec((r, GROUP), lambda i: (0, 0))
    return pl.pallas_call(
        body, name="gate_fwd", grid=(t // tb,),
        in_specs=[pl.BlockSpec((tb, GROUP), lambda i: (i, 0)), grp(3), vec(1), grp(4), grp(5), grp(6), prev(5), prev(6),
                  vec(3), ANY],
        out_specs=pl.BlockSpec((2, tb, GROUP), lambda i: (0, i, 0)),
        out_shape=jax.ShapeDtypeStruct((2, t, HGRN_WIDTH), BF16),
        scratch_shapes=[pltpu.VMEM((tb + SUBLANES, GROUP), F32)],
        compiler_params=_params("parallel"),
    )(o, proj, gate_norm_w, proj, proj, proj, proj, proj, conv_w, after)


def _ln_bwd(dy, xhat, rstd, g):
    dxhat = dy * g
    m1 = jnp.mean(dxhat, axis=-1, keepdims=True)
    m2 = jnp.mean(dxhat * xhat, axis=-1, keepdims=True)
    return rstd * (dxhat - m1 - xhat * m2)


def _layer_norm(pre):
    xc = pre - jnp.mean(pre, axis=-1, keepdims=True)
    rstd = lax.rsqrt(jnp.mean(xc * xc, axis=-1, keepdims=True) + EPS)
    return xc * rstd, rstd


def _sublayers(cat, x, target, w_out, w_ff1, w_ff2, g1, b1, g2, b2):
    t = x.shape[0]
    tm = min(t, 256)

    def body(cat_ref, x_ref, tg_ref, wo_ref, w1_ref, w2_ref, g1_ref, b1_ref, g2_ref, b2_ref,
             h1_ref, r_ref, da_ref, dp2b_ref, dp1_ref, dp1b_ref, dcat_ref, dg1_ref, db1_ref, dg2_ref, db2_ref, loss_ref):
        @pl.when(pl.program_id(0) == 0)
        def _():
            for ref in (dg1_ref, db1_ref, dg2_ref, db2_ref, loss_ref):
                ref[...] = jnp.zeros_like(ref)

        mix = _dot(cat_ref[0], wo_ref[0:GROUP, :], NN) + _dot(cat_ref[1], wo_ref[GROUP:2 * GROUP, :], NN)
        xhat1, rstd1 = _layer_norm(ALPHA * x_ref[...] + mix)
        h1 = xhat1 * g1_ref[...] + b1_ref[...]
        h1b = h1.astype(h1_ref.dtype)
        h1_ref[...] = h1b
        mlp = jnp.zeros((tm, D_MODEL), F32)
        for j in range(N_FF):
            cols = slice(j * FF_BLOCK, (j + 1) * FF_BLOCK)
            r = jnp.square(jnp.maximum(_dot(h1b, w1_ref[:, cols], NN), 0.0)).astype(r_ref.dtype)
            r_ref[:, cols] = r
            mlp = mlp + _dot(r, w2_ref[cols, :], NN)
        xhat2, rstd2 = _layer_norm(ALPHA * h1 + mlp)
        err = xhat2 * g2_ref[...] + b2_ref[...] - tg_ref[...]
        loss_ref[...] += 0.5 * jnp.sum(jnp.mean(err * err, axis=-1, keepdims=True))
        dy = err * (1.0 / D_MODEL)
        dg2_ref[...] += jnp.sum(dy * xhat2, axis=0, keepdims=True)
        db2_ref[...] += jnp.sum(dy, axis=0, keepdims=True)
        dp2 = _ln_bwd(dy, xhat2, rstd2, g2_ref[...])
        dp2b = dp2.astype(dp2b_ref.dtype)
        dp2b_ref[...] = dp2b
        back = jnp.zeros((tm, D_MODEL), F32)
        for j in range(N_FF):
            cols = slice(j * FF_BLOCK, (j + 1) * FF_BLOCK)
            dr = _dot(dp2b, w2_ref[cols, :], NT)
            da = (dr * (2.0 * jnp.sqrt(r_ref[:, cols].astype(F32)))).astype(da_ref.dtype)
            da_ref[:, cols] = da
            back = back + _dot(da, w1_ref[:, cols], NT)
        dh1 = ALPHA * dp2 + back
        dg1_ref[...] += jnp.sum(dh1 * xhat1, axis=0, keepdims=True)
        db1_ref[...] += jnp.sum(dh1, axis=0, keepdims=True)
        dp1 = _ln_bwd(dh1, xhat1, rstd1, g1_ref[...])
        dp1b = dp1.astype(dp1b_ref.dtype)
        dp1_ref[...] = dp1
        dp1b_ref[...] = dp1b
        dcat_ref[...] = _dot(dp1b, wo_ref[...], NT)

    row = pl.BlockSpec((tm, D_MODEL), lambda i: (i, 0))
    wide = pl.BlockSpec((tm, D_FF), lambda i: (i, 0))
    vec = pl.BlockSpec((1, D_MODEL), lambda i: (0, 0))
    narrow = lambda dtype: jax.ShapeDtypeStruct((t, D_MODEL), dtype)
    return pl.pallas_call(
        body, name="sublayers", grid=(t // tm,),
        in_specs=[pl.BlockSpec((2, tm, GROUP), lambda i: (0, i, 0)), row, row, _resident((D_MODEL, D_MODEL)),
                  _resident((D_MODEL, D_FF)), _resident((D_FF, D_MODEL)), vec, vec, vec, vec],
        out_specs=[row, wide, wide, row, row, row, row, vec, vec, vec, vec,
                   pl.BlockSpec((SUBLANES, LANES), lambda i: (0, 0))],
        out_shape=[narrow(BF16), jax.ShapeDtypeStruct((t, D_FF), BF16), jax.ShapeDtypeStruct((t, D_FF), BF16),
                   narrow(BF16), narrow(F32), narrow(BF16), narrow(F32)]
                  + [jax.ShapeDtypeStruct((1, D_MODEL), F32)] * 4 + [jax.ShapeDtypeStruct((SUBLANES, LANES), F32)],
        compiler_params=_params("arbitrary"),
    )(cat, x, target, w_out, w_ff1, w_ff2, g1, b1, g2, b2)


def _gate_bwd(dcat, o, proj, gate_norm_w, conv_w, after):
    t = proj.shape[1]
    tb = min(t, 512)
    hb = tb // SUBLANES
    nblk = t // tb

    def body(do2_ref, dy_ref, dyn_ref, o_ref, og_ref, gnw_ref, b_ref, bn_ref, c_ref, u_ref, ch_ref, uh_ref, cw_ref,
             after_ref, do_ref, dp_ref, dgnw_ref, dcw_ref, zbuf, dbuf):
        i = pl.program_id(0)

        @pl.when(i == 0)
        def _():
            dgnw_ref[...] = jnp.zeros_like(dgnw_ref)
            dcw_ref[...] = jnp.zeros_like(dcw_ref)

        ov, og, gnw, do2 = o_ref[...], og_ref[...], gnw_ref[...], do2_ref[...]
        rs = _per_head(lambda o_h: jnp.broadcast_to(lax.rsqrt(jnp.mean(o_h * o_h, axis=-1, keepdims=True) + EPS),
                                                    o_h.shape), ov)
        on = ov * rs
        sg = _sigmoid(og)
        sil = og * sg
        don = do2 * gnw * sil
        dgnw_ref[...] += jnp.sum(do2 * on * sil, axis=0, keepdims=True)
        dp_ref[0] = (do2 * on * gnw * (sg * (1.0 + og * (1.0 - sg)))).astype(dp_ref.dtype)
        do_ref[...] = rs * (don - on * _per_head(
            lambda p_h: jnp.broadcast_to(jnp.mean(p_h, axis=-1, keepdims=True), p_h.shape), don * on))

        bg, cg, u, dy = b_ref[...], c_ref[...], u_ref[...], dy_ref[...]
        z = cg * u
        halo = jnp.where(i > 0, ch_ref[...] * uh_ref[...], 0.0)
        z1, z2 = _conv_taps(z, halo, zbuf, tb)
        cw = cw_ref[...]
        yc = cw[2:3, :] * z + cw[1:2, :] * z1 + cw[0:1, :] * z2
        dyc = dy * bg
        dbuf[0:tb, :] = dyc
        dbuf[tb:tb + SUBLANES, :] = jnp.where(i < nblk - 1, dyn_ref[...] * bn_ref[...], 0.0)
        d1, d2 = dbuf[1:1 + tb, :], dbuf[2:2 + tb, :]
        dz = cw[2:3, :] * dyc + cw[1:2, :] * d1 + cw[0:1, :] * d2
        dp_ref[1] = (dy * yc).astype(dp_ref.dtype)
        dp_ref[2] = (dz * u).astype(dp_ref.dtype)
        dp_ref[3] = (dz * cg).astype(dp_ref.dtype)
        dcw_ref[0:1, :] += jnp.sum(dyc * z2, axis=0, keepdims=True)
        dcw_ref[1:2, :] += jnp.sum(dyc * z1, axis=0, keepdims=True)
        dcw_ref[2:3, :] += jnp.sum(dyc * z, axis=0, keepdims=True)

    half = lambda g: pl.BlockSpec((tb, GROUP), lambda i: (i, g))
    grp = lambda g: pl.BlockSpec((None, tb, GROUP), lambda i: (g, i, 0))
    prev = lambda g: pl.BlockSpec((None, SUBLANES, GROUP), lambda i: (g, jnp.maximum(i * hb - 1, 0), 0))
    nxt_row = lambda i: jnp.minimum((i + 1) * hb, t // SUBLANES - 1)
    nxt = lambda g: pl.BlockSpec((None, SUBLANES, GROUP), lambda i: (g, nxt_row(i), 0))
    vec = lambda r: pl.BlockSpec((r, GROUP), lambda i: (0, 0))
    return pl.pallas_call(
        body, name="gate_bwd", grid=(nblk,),
        in_specs=[half(0), half(1), pl.BlockSpec((SUBLANES, GROUP), lambda i: (nxt_row(i), 1)), half(0), grp(3), vec(1),
                  grp(4), nxt(4), grp(5), grp(6), prev(5), prev(6), vec(3), ANY],
        out_specs=[half(0), pl.BlockSpec((4, tb, GROUP), lambda i: (0, i, 0)), vec(1), vec(3)],
        out_shape=[jax.ShapeDtypeStruct((t, HGRN_WIDTH), F32), jax.ShapeDtypeStruct((4, t, HGRN_WIDTH), BF16),
                   jax.ShapeDtypeStruct((1, HGRN_WIDTH), F32), jax.ShapeDtypeStruct((3, CONV_WIDTH), F32)],
        scratch_shapes=[pltpu.VMEM((tb + SUBLANES, GROUP), F32), pltpu.VMEM((tb + SUBLANES, GROUP), F32)],
        compiler_params=_params("arbitrary"),
    )(dcat, dcat, dcat, o, proj, gate_norm_w, proj, proj, proj, proj, proj, proj, conv_w, after)


def _hgrn_bwd(proj, do, states, lb_logits, after):
    t = proj.shape[1]
    tb = min(t, 512)
    ncb = tb // CHUNK
    nblk = t // tb

    def body(q_ref, f_ref, v_ref, do_ref, st_ref, lbl_ref, after_ref, dp_ref, dlbl_ref, ds_scr, dlb_scr):
        i = pl.program_id(0)

        @pl.when(i == 0)
        def _():
            ds_scr[...] = jnp.zeros_like(ds_scr)
            dlb_scr[...] = jnp.zeros_like(dlb_scr)

        lb, s1 = _lower_bound(lbl_ref[...])
        causal, anti = _chunk_masks()
        every = range(ncb)
        rows = [slice(c * CHUNK, (c + 1) * CHUNK) for c in every]
        q, v, do = ([ref[r, :] for r in rows] for ref in (q_ref, v_ref, do_ref))
        st = [st_ref[c] for c in every]
        gates = [_gates(f_ref[r, :], lb) for r in rows]
        sig, f, k = ([gt[n] for gt in gates] for n in (0, 1, 3))
        b = [_dot_exact(causal, gt[2]) for gt in gates]
        mid, last = [x[CHUNK // 2:CHUNK // 2 + 1, :] for x in b], [x[CHUNK - 1:CHUNK, :] for x in b]
        e_q = [jnp.exp(b[c] - mid[c]) for c in every]
        e_k = [jnp.exp(mid[c] - b[c]) for c in every]
        e_i = [jnp.exp(x) for x in b]
        e_s = [jnp.exp(last[c] - b[c]) for c in every]
        dec = [jnp.exp(x) for x in last]
        qt, kt, qi, ks = ([a[c] * e[c] for c in every] for a, e in ((q, e_q), (k, e_k), (q, e_i), (k, e_s)))

        def masked(a, b_):
            return [[jnp.where(causal, _dot(a_h, b_h, NT), 0.0) for a_h, b_h in zip(_heads(a[c]), _heads(b_[c]))]
                    for c in every]

        def with_scores(s, other, dims):
            return [jnp.concatenate([_dot(s_h, o_h, dims) for s_h, o_h in zip(s[c], _heads(other[c]))], axis=1)
                    for c in every]

        def per_head(dims, a, b_):
            return [_per_head(lambda a_h, b_h: _dot(a_h, b_h, dims), a[c], b_[c]) for c in every]

        scores, dscores = masked(qt, kt), masked(do, v)
        dqt, dkt, dv_intra = with_scores(dscores, kt, NN), with_scores(dscores, qt, TN), with_scores(scores, do, TN)
        dqi, update = per_head(NN, do, st), per_head(TN, do, qi)

        dst = ds_scr[...]
        dsts = [None] * ncb
        for c in reversed(every):
            dsts[c] = dst
            dst = dec[c] * dst + update[c]
        ds_scr[...] = dst

        dv_state, dks = per_head(NT, ks, dsts), per_head(NN, v, dsts)
        ddec = [jnp.sum(dsts[c] * st[c], axis=0, keepdims=True) for c in every]
        dq = [dqt[c] * e_q[c] + dqi[c] * e_i[c] for c in every]
        dk = [dkt[c] * e_k[c] + dks[c] * e_s[c] for c in every]
        db = [q[c] * dq[c] - k[c] * dk[c] for c in every]
        db_last = [jnp.sum(dks[c] * ks[c], axis=0, keepdims=True) + ddec[c] * dec[c] for c in every]
        dg = [_dot_exact(anti, db[c]) + db_last[c] for c in every]
        df = [dg[c] / f[c] - dk[c] for c in every]
        dlb_scr[...] += sum(jnp.sum(df[c] * (1.0 - sig[c]), axis=0, keepdims=True) for c in every)
        dfp = [df[c] * (1.0 - lb) * sig[c] * (1.0 - sig[c]) for c in every]
        dv = [dv_intra[c] + dv_state[c] for c in every]
        for n, parts in enumerate((dq, dfp, dv)):
            dp_ref[n] = jnp.concatenate(parts, axis=0).astype(dp_ref.dtype)

        @pl.when(i == nblk - 1)
        def _():
            dlb = dlb_scr[...]
            dlbl_ref[0:1, :] = dlb * lb * (1.0 - lb)
            dlbl_ref[1:2, :] = -dlb * lb * s1

    grp = lambda g: pl.BlockSpec((None, tb, GROUP), lambda i: (g, nblk - 1 - i, 0))
    vec = pl.BlockSpec((2, HGRN_WIDTH), lambda i: (0, 0))
    return pl.pallas_call(
        body, name="hgrn_bwd", grid=(nblk,),
        in_specs=[grp(0), grp(1), grp(2), pl.BlockSpec((tb, HGRN_WIDTH), lambda i: (nblk - 1 - i, 0)),
                  pl.BlockSpec((ncb, HEAD_DIM, HGRN_WIDTH), lambda i: (nblk - 1 - i, 0, 0)), vec, ANY],
        out_specs=[pl.BlockSpec((3, tb, HGRN_WIDTH), lambda i: (0, nblk - 1 - i, 0)), vec],
        out_shape=[jax.ShapeDtypeStruct((3, t, HGRN_WIDTH), BF16), jax.ShapeDtypeStruct((2, HGRN_WIDTH), F32)],
        scratch_shapes=[pltpu.VMEM((HEAD_DIM, HGRN_WIDTH), F32), pltpu.VMEM((1, HGRN_WIDTH), F32)],
        compiler_params=_params("arbitrary"),
    )(proj, proj, proj, do, states, lb_logits, after)


def _in_bwd(dph, dpg, w_in, dpre1, after):
    t = dpre1.shape[0]
    tm = min(t, 512)

    def body(dh_ref, dg_ref, w_ref, dp_ref, after_ref, o_ref):
        acc = ALPHA * dp_ref[...]
        for g in range(N_GROUPS):
            part = dh_ref[g] if g < 3 else dg_ref[g - 3]
            acc = acc + _dot(part, w_ref[:, g * GROUP:(g + 1) * GROUP], NT)
        o_ref[...] = acc

    row = pl.BlockSpec((tm, D_MODEL), lambda i: (i, 0))
    return pl.pallas_call(
        body, name="in_bwd", grid=(t // tm,),
        in_specs=[pl.BlockSpec((3, tm, GROUP), lambda i: (0, i, 0)), pl.BlockSpec((4, tm, GROUP), lambda i: (0, i, 0)),
                  _resident((D_MODEL, IN_COLS)), row, ANY],
        out_specs=row,
        out_shape=jax.ShapeDtypeStruct((t, D_MODEL), F32),
        compiler_params=_params("parallel"),
    )(dph, dpg, w_in, dpre1, after)


def _grad_w(name, operands, widths, shape, step, after=None):
    t = operands[0].shape[-2]
    tt = min(t, 512)
    n_in, n_steps = len(operands), t // tt
    in_specs = [pl.BlockSpec((tt, w), lambda k: (k, 0)) if a.ndim == 2 else
                pl.BlockSpec((a.shape[0], tt, w), lambda k: (0, k, 0)) for a, w in zip(operands, widths)]
    extra = [] if after is None else [after]

    def body(*refs):
        o_ref, acc, narrow, sem = refs[-4:]
        k = pl.program_id(0)

        @pl.when(k == 0)
        def _():
            acc[...] = jnp.zeros_like(acc)

        step(acc, *refs[:n_in])

        @pl.when(k == n_steps - 1)
        def _():
            narrow[...] = acc[...].astype(narrow.dtype)
            out = pltpu.make_async_copy(narrow, o_ref, sem)
            out.start()
            out.wait()

    return pl.pallas_call(
        body, name=name, grid=(n_steps,), in_specs=in_specs + [ANY] * len(extra), out_specs=ANY,
        out_shape=jax.ShapeDtypeStruct(shape, BF16),
        scratch_shapes=[pltpu.VMEM(shape, F32), pltpu.VMEM(shape, BF16), pltpu.SemaphoreType.DMA],
        compiler_params=_params("arbitrary"),
    )(*operands, *extra)


def _dw_in(xb, dph, dpg, after):
    def step(acc, x_ref, dh_ref, dg_ref):
        xv = x_ref[...]
        for g in range(N_GROUPS):
            part = dh_ref[g] if g < 3 else dg_ref[g - 3]
            acc[:, g * GROUP:(g + 1) * GROUP] += _dot(xv, part, TN)

    return _grad_w("dw_in", (xb, dph, dpg), (D_MODEL, GROUP, GROUP), (D_MODEL, IN_COLS), step, after)


def _dw_out(cat, dpre1b):
    def step(acc, cat_ref, d_ref):
        dv = d_ref[...]
        for g in range(2):
            acc[g * GROUP:(g + 1) * GROUP, :] += _dot(cat_ref[g], dv, TN)

    return _grad_w("dw_out", (cat, dpre1b), (GROUP, D_MODEL), (D_MODEL, D_MODEL), step)


def _dw_ff1(h1b, da):
    def step(acc, h_ref, da_ref):
        hv = h_ref[...]
        for j in range(D_FF // FF_BLOCK):
            cols = slice(j * FF_BLOCK, (j + 1) * FF_BLOCK)
            acc[:, cols] += _dot(hv, da_ref[:, cols], TN)

    return _grad_w("dw_ff1", (h1b, da), (D_MODEL, D_FF), (D_MODEL, D_FF), step)


def _dw_ff2(r, dpre2b):
    def step(acc, r_ref, d_ref):
        dv = d_ref[...]
        for j in range(D_FF // FF_BLOCK):
            rows = slice(j * FF_BLOCK, (j + 1) * FF_BLOCK)
            acc[rows, :] += _dot(r_ref[:, rows], dv, TN)

    return _grad_w("dw_ff2", (r, dpre2b), (D_FF, D_MODEL), (D_FF, D_MODEL), step)


def _place():
    x, y, c = lax.axis_index("x"), lax.axis_index("y"), lax.axis_index("c")
    return x, y, c, 2 * x + y


def _other_chips(x, y):
    return [(1 - x, y), (x, 1 - y), (1 - x, 1 - y)]


def _place_shard(name, w, chip, cols_sharded, after=None):
    rows, cols = w.shape
    tr = min(rows, 256)
    nb = rows // tr
    full = (rows, cols * N_CHIPS) if cols_sharded else (rows * N_CHIPS, cols)
    out_map = (lambda i, s: (i, s[0])) if cols_sharded else (lambda i, s: (s[0] * nb + i, 0))

    def body(s_ref, w_ref, *rest):
        rest[-1][...] = w_ref[...].astype(rest[-1].dtype)

    extra = [] if after is None else [after]
    return pl.pallas_call(
        body, name=name,
        grid_spec=pltpu.PrefetchScalarGridSpec(
            num_scalar_prefetch=1, grid=(nb,),
            in_specs=[pl.BlockSpec((tr, cols), lambda i, s: (i, 0))] + [ANY] * len(extra),
            out_specs=pl.BlockSpec((tr, cols), out_map)),
        out_shape=jax.ShapeDtypeStruct(full, BF16),
        compiler_params=_params("parallel"),
    )(chip, w, *extra)


HBM = pl.BlockSpec(memory_space=pltpu.HBM)
SEM = pl.BlockSpec(memory_space=pltpu.SEMAPHORE)
EFFECT = pltpu.SideEffectType.DATAFLOW_SIDE_EFFECTING


class _Split:
    def __init__(self, name, arrays, plan):
        n, n_copies = len(arrays), plan.count
        self.name, self.plan, self.n = name, plan, n

        def body(*refs):
            send_sems, recv_sems, token = refs[n], refs[n + 1], refs[-1]
            for k, (src, dst, to) in enumerate(plan(refs[:n])):
                pltpu.make_async_remote_copy(src_ref=src, dst_ref=dst, send_sem=send_sems.at[k], recv_sem=recv_sems.at[k],
                                             device_id=to, device_id_type=MESH).start()
            token[...] = jnp.zeros_like(token)

        outs = pl.pallas_call(
            body, name=name + "_start",
            out_shape=(pltpu.SemaphoreType.DMA((n_copies,)), pltpu.SemaphoreType.DMA((n_copies,)),
                       *[pltpu.HBM(a.shape, a.dtype) for a in arrays], jax.ShapeDtypeStruct((SUBLANES, LANES), F32)),
            in_specs=(HBM,) * n, out_specs=(SEM, SEM) + (HBM,) * n + (pl.BlockSpec(memory_space=pltpu.VMEM),),
            input_output_aliases={i: 2 + i for i in range(n)},
            compiler_params=pltpu.CompilerParams(has_side_effects=EFFECT),
        )(*[pltpu.with_memory_space_constraint(a, pltpu.HBM) for a in arrays])
        self.sems, self.arrays, self.token = outs[:2], outs[2:2 + n], outs[-1]

    def wait(self, after):
        n, plan = self.n, self.plan

        def body(*refs):
            send_sems, recv_sems = refs[n], refs[n + 1]
            for k, (src, dst, to) in enumerate(plan(refs[:n])):
                cp = pltpu.make_async_remote_copy(src_ref=src, dst_ref=dst, send_sem=send_sems.at[k],
                                                  recv_sem=recv_sems.at[k], device_id=to, device_id_type=MESH)
                cp.wait_send()
                cp.wait_recv()

        return pl.pallas_call(
            body, name=self.name + "_wait", out_shape=tuple(pltpu.HBM(a.shape, a.dtype) for a in self.arrays),
            in_specs=(HBM,) * n + (SEM, SEM, ANY), out_specs=(HBM,) * n, input_output_aliases={i: i for i in range(n)},
            compiler_params=pltpu.CompilerParams(has_side_effects=EFFECT),
        )(*self.arrays, *self.sems, after)


COLS_SHARDED = (True, False, True, False)
HALF_SHAPES = [(D_MODEL // 2, IN_COLS), (D_MODEL, D_MODEL // 2), (D_MODEL // 2, D_FF), (D_FF, D_MODEL // 2)]
PIECE_SHAPES = [(D_MODEL // 2, IN_COLS // N_CHIPS), (D_MODEL // N_CHIPS, D_MODEL // 2),
                (D_MODEL // 2, D_FF // N_CHIPS), (D_FF // N_CHIPS, D_MODEL // 2)]


def _shard_view(kind, ref, chip):
    if COLS_SHARDED[kind]:
        n = ref.shape[1] // N_CHIPS
        return ref.at[:, pl.ds(chip * n, n)]
    n = ref.shape[0] // N_CHIPS
    return ref.at[pl.ds(chip * n, n), :]


def _half_view(kind, ref, h):
    if COLS_SHARDED[kind]:
        n = ref.shape[0] // 2
        return ref.at[pl.ds(h * n, n), :]
    n = ref.shape[1] // 2
    return ref.at[:, pl.ds(h * n, n)]


def _plan(count):
    def mark(fn):
        fn.count = count
        return fn
    return mark


def _shard_rows_view(kind, ref, chip, part, n_parts):
    if COLS_SHARDED[kind]:
        m, n = ref.shape[0] // n_parts, ref.shape[1] // N_CHIPS
        return ref.at[pl.ds(part * m, m), pl.ds(chip * n, n)]
    m = ref.shape[0] // N_CHIPS // n_parts
    return ref.at[pl.ds((n_parts * chip + part) * m, m), :]


def _shard_half_view(kind, ref, chip, h):
    return _shard_rows_view(kind, ref, chip, h, 2)


def _gather_over_ici(kinds, weights):
    @_plan(2 * len(kinds))
    def plan(refs):
        x, y, c, me = _place()
        mine = [_shard_half_view(kind, ref, me, c) for kind, ref in zip(kinds, refs)]
        return [(v, v, to) for v in mine for to in ((1 - x, y, c), (x, 1 - y, c))]

    return _Split("gather_ici_" + "".join(map(str, kinds)), tuple(weights), plan)


def _relay_over_ici(kinds, weights):
    @_plan(2 * len(kinds))
    def plan(refs):
        x, y, c, _ = _place()
        x_nbr, y_nbr = 2 * (1 - x) + y, 2 * x + (1 - y)
        out = []
        for kind, ref in zip(kinds, refs):
            first, second = (_shard_rows_view(kind, ref, chip, 2 * c + q, 4) for q, chip in ((0, x_nbr), (1, y_nbr)))
            out += [(first, first, (x, 1 - y, c)), (second, second, (1 - x, y, c))]
        return out

    return _Split("relay_ici_" + "".join(map(str, kinds)), tuple(weights), plan)


def _gather_w_in_over_ici(w_in, conv4):
    @_plan(6)
    def plan(refs):
        x, y, c, me = _place()
        half, conv = _shard_half_view(0, refs[0], me, c), refs[1].at[me]
        return [(v, v, (px, py, c)) for v in (half, conv) for px, py in _other_chips(x, y)]

    return _Split("gather_w_in_ici", (w_in, conv4), plan)


def _gather_over_d2d(kinds, weights):
    @_plan(3 * len(kinds))
    def plan(refs):
        x, y, c, _ = _place()
        got = [_shard_half_view(kind, ref, 2 * px + py, c) for kind, ref in zip(kinds, refs)
               for px, py in _other_chips(x, y)]
        return [(v, v, (x, y, 1 - c)) for v in got]

    return _Split("gather_d2d_" + "".join(map(str, kinds)), tuple(weights), plan)


def _swap_halves(kinds, grads):
    @_plan(len(kinds))
    def plan(refs):
        x, y, c, _ = _place()
        return [(_half_view(kind, g, 1 - c), land, (x, y, 1 - c))
                for kind, g, land in zip(kinds, refs[:len(kinds)], refs[len(kinds):])]

    lands = [lax.empty(HALF_SHAPES[kind], g.dtype) for kind, g in zip(kinds, grads)]
    return _Split("swap_halves_" + "".join(map(str, kinds)), (*grads, *lands), plan)


def _add_half(name, g, recv, core, rows_split):
    shape = recv.shape
    tr = min(shape[0], 128 if rows_split else 256)
    nb = shape[0] // tr

    def body(c_ref, g_ref, r_ref, o_ref):
        o_ref[...] = (g_ref[...].astype(F32) + r_ref[...].astype(F32)).astype(o_ref.dtype)

    g_map = (lambda i, c_ref: (c_ref[0] * nb + i, 0)) if rows_split else (lambda i, c_ref: (i, c_ref[0]))
    blk = pl.BlockSpec((tr, shape[1]), lambda i, c_ref: (i, 0))
    return pl.pallas_call(
        body, name=name,
        grid_spec=pltpu.PrefetchScalarGridSpec(
            num_scalar_prefetch=1, grid=(nb,),
            in_specs=[pl.BlockSpec((tr, shape[1]), g_map), blk], out_specs=blk),
        out_shape=jax.ShapeDtypeStruct(shape, BF16),
        compiler_params=_params("parallel"),
    )(core, g, recv)


def _exchange_pieces(kinds, halves, pack=None):
    n_p, n = N_CHIPS - 1, len(kinds)

    @_plan(n_p * n + (0 if pack is None else N_DEV - 1))
    def plan(refs):
        x, y, c, _ = _place()
        copies = []
        if pack is not None:
            me = 4 * x + 2 * y + c
            peers = [((1 - x) if m & 4 else x, (1 - y) if m & 2 else y, (1 - c) if m & 1 else c) for m in range(1, N_DEV)]
            copies += [(refs[2 * n], refs[2 * n + 1].at[me], peer) for peer in peers]
        return copies + [(_shard_view(kind, half, 2 * px + py), land.at[j], (px, py, c))
                         for j, (px, py) in enumerate(_other_chips(x, y))
                         for kind, half, land in zip(kinds, refs[:n], refs[n:2 * n])]

    lands = [lax.empty((n_p,) + PIECE_SHAPES[kind], BF16) for kind in kinds]
    small = () if pack is None else (pack, lax.empty((N_DEV,) + pack.shape, F32))
    return _Split("exchange_pieces_" + "".join(map(str, kinds)), (*halves, *lands, *small), plan)


def _sum_pieces(name, half, slots, place, rows_split, after):
    n_p, rows, cols = slots.shape
    tr = min(rows, 256)
    nb = rows // tr
    if rows_split:
        own_map = lambda i, s: (i, s[0])
        out_map = lambda i, s: (s[1] * nb + i, 0)
        shard = (2 * rows, cols)
    else:
        own_map = lambda i, s: (s[0] * nb + i, 0)
        out_map = lambda i, s: (i, s[1])
        shard = (rows, 2 * cols)

    def body(s_ref, own_ref, slot_ref, after_ref, o_ref):
        total = own_ref[...].astype(F32)
        for j in range(n_p):
            total = total + slot_ref[j].astype(F32)
        o_ref[...] = total

    return pl.pallas_call(
        body, name=name,
        grid_spec=pltpu.PrefetchScalarGridSpec(
            num_scalar_prefetch=1, grid=(nb,),
            in_specs=[pl.BlockSpec((tr, cols), own_map), pl.BlockSpec((n_p, tr, cols), lambda i, s: (0, i, 0)), ANY],
            out_specs=pl.BlockSpec((tr, cols), out_map)),
        out_shape=jax.ShapeDtypeStruct(shard, F32),
        compiler_params=_params("parallel"),
    )(place, half, slots, after)


def _join_halves(kinds, shards):
    @_plan(len(kinds))
    def plan(refs):
        x, y, c, _ = _place()
        return [(_half_view(kind, g, c), _half_view(kind, g, c), (x, y, 1 - c)) for kind, g in zip(kinds, refs)]

    return _Split("join_halves_" + "".join(map(str, kinds)), tuple(shards), plan)


N_DEV = 8


def _sum_shared(pack, land, device):
    def body(d_ref, p_ref, l_ref, o_ref):
        me = d_ref[0]
        total = jnp.where(me == 0, p_ref[...], l_ref[0])
        for d in range(1, N_DEV):
            total = total + jnp.where(me == d, p_ref[...], l_ref[d])
        o_ref[...] = total

    return pl.pallas_call(
        body, name="sum_shared",
        grid_spec=pltpu.PrefetchScalarGridSpec(
            num_scalar_prefetch=1, grid=(1,),
            in_specs=[pl.BlockSpec(pack.shape, lambda i, d: (0, 0)), pl.BlockSpec(land.shape, lambda i, d: (0, 0, 0))],
            out_specs=pl.BlockSpec(pack.shape, lambda i, d: (0, 0))),
        out_shape=jax.ShapeDtypeStruct(pack.shape, F32),
    )(device, pack, land)


def _adamw(name, w, g, m, v, after=None):
    rows, cols = w.shape
    tr = min(rows, 256)
    extra = [] if after is None else [after]

    def body(w_ref, g_ref, m_ref, v_ref, *rest):
        d_ref, nm_ref, nv_ref = rest[-3:]
        d_ref[...], nm_ref[...], nv_ref[...] = _adam_step(w_ref[...], g_ref[...], m_ref[...], v_ref[...])

    blk = pl.BlockSpec((tr, cols), lambda i: (i, 0))
    return pl.pallas_call(
        body, name=name, grid=(rows // tr,), in_specs=[blk] * 4 + [ANY] * len(extra), out_specs=[blk] * 3,
        out_shape=[jax.ShapeDtypeStruct(w.shape, F32)] * 3,
        compiler_params=_params("parallel"),
    )(w, g, m, v, *extra)


def _adam_step(w, g, m, v):
    nm = ADAM_B1 * m + (1.0 - ADAM_B1) * g
    nv = ADAM_B2 * v + (1.0 - ADAM_B2) * jnp.square(g)
    m_hat = nm * (1.0 / (1.0 - ADAM_B1 ** ADAM_STEP))
    v_hat = nv * (1.0 / (1.0 - ADAM_B2 ** ADAM_STEP))
    return -ADAM_LR * (m_hat / (jnp.sqrt(v_hat) + ADAM_EPS) + ADAM_WD * w), nm, nv


def _adamw_small(tot, chip, weights, ms, vs, after):
    n, half = len(weights), D_MODEL // 2

    def body(chip_ref, tot_ref, *refs):
        ins, outs = refs[:3 * n], refs[3 * n + 1:]
        tot = tot_ref[...]
        conv_all = jnp.concatenate([tot[5:6, half:], tot[6:7, :half], tot[6:7, half:]], axis=0)
        conv = sum(jnp.where(chip_ref[0] == s, conv_all[:, s * LANES:(s + 1) * LANES], 0.0) for s in range(N_CHIPS))
        grads = [jnp.concatenate([tot[4:5, :half], tot[4:5, half:]], axis=0), tot[5:6, :half], conv,
                 tot[0:1], tot[1:2], tot[2:3], tot[3:4]]
        for k, g in enumerate(grads):
            delta, nm, nv = _adam_step(ins[k][...], g, ins[n + k][...], ins[2 * n + k][...])
            outs[k][...], outs[n + k][...], outs[2 * n + k][...], outs[3 * n + k][...] = g, delta, nm, nv

    whole = lambda a: pl.BlockSpec(a.shape, lambda i, s: (0,) * a.ndim)
    arrays = (*weights, *ms, *vs)
    return pl.pallas_call(
        body, name="adamw_small",
        grid_spec=pltpu.PrefetchScalarGridSpec(
            num_scalar_prefetch=1, grid=(1,), in_specs=[whole(tot)] + [whole(a) for a in arrays] + [ANY],
            out_specs=[whole(a) for a in weights] * 4),
        out_shape=[jax.ShapeDtypeStruct(a.shape, F32) for a in weights] * 4,
    )(chip, tot, *arrays, after)


def kernel(x, w_in, lb_logits, gate_norm_w, conv_w, w_out, ln1_g, ln1_b, w_ff1, w_ff2, ln2_g, ln2_b, loss_target, m_w_in, m_lb_logits, m_gate_norm_w, m_conv_w, m_w_out, m_ln1_g, m_ln1_b, m_w_ff1, m_w_ff2, m_ln2_g, m_ln2_b, v_w_in, v_lb_logits, v_gate_norm_w, v_conv_w, v_w_out, v_ln1_g, v_ln1_b, v_w_ff1, v_w_ff2, v_ln2_g, v_ln2_b):
    xs, tgt = x[0], loss_target[0]
    chip = 2 * lax.axis_index("x") + lax.axis_index("y")
    core = lax.axis_index("c").astype(jnp.int32).reshape(1)
    chip1 = chip.astype(jnp.int32).reshape(1)
    place = jnp.concatenate([chip1, core])

    conv4 = lax.dynamic_update_slice(jnp.zeros((N_CHIPS,) + conv_w.shape[1:], F32), conv_w, (chip, 0, 0))
    ici_in = _gather_w_in_over_ici(_place_shard("place_w_in", w_in[0], chip1, True), conv4)
    rest = (1, 2, 3)
    ici_rest = _gather_over_ici(rest, (_place_shard("place_w_out", w_out[0], chip1, False, after=ici_in.token),
                                       _place_shard("place_w_ff1", w_ff1[0], chip1, True, after=ici_in.token),
                                       _place_shard("place_w_ff2", w_ff2[0], chip1, False, after=ici_in.token)))
    wb_in, cv4 = ici_in.wait(ici_rest.token)
    d2d_in = _gather_over_d2d((0,), (wb_in,))
    wb_in, = d2d_in.wait(d2d_in.token)
    conv_full = cv4.transpose(1, 0, 2).reshape(3, CONV_WIDTH)

    proj, xb = _in_proj(xs, wb_in, ici_rest.token)
    relay_rest = _relay_over_ici(rest, ici_rest.wait(proj))
    o, states = _hgrn_fwd(proj, lb_logits, relay_rest.token)
    d2d_rest = _gather_over_d2d(rest, relay_rest.wait(o))
    cat = _gate_fwd(proj, o, gate_norm_w, conv_full, d2d_rest.token)
    wb_out, wb_ff1, wb_ff2 = d2d_rest.wait(cat)

    (h1b, r, da, dpre2b, dpre1, dpre1b, dcat, g_ln1_g, g_ln1_b, g_ln2_g, g_ln2_b, loss8) = _sublayers(
        cat, xs, tgt, wb_out, wb_ff1, wb_ff2, ln1_g, ln1_b, ln2_g, ln2_b)

    names = ("w_in", "w_out", "w_ff1", "w_ff2")

    def add_halves(kinds, grads, lands):
        return [_add_half("add_half_" + names[k], g, ld, core, COLS_SHARDED[k]) for k, g, ld in zip(kinds, grads, lands)]

    def sum_pieces(kinds, halves, lands, after):
        return [_sum_pieces("sum_pieces_" + names[k], h, ld, place, COLS_SHARDED[k], after)
                for k, h, ld in zip(kinds, halves, lands)]

    early = (1, 2, 3)
    swap = _swap_halves(early, (_dw_out(cat, dpre1b), _dw_ff1(h1b, da), _dw_ff2(r, dpre2b)))
    do, dpg, g_gnw, g_conv = _gate_bwd(dcat, o, proj, gate_norm_w, conv_full, swap.token)
    swapped = swap.wait(do)
    exch = _exchange_pieces(early, add_halves(early, swapped[:3], swapped[3:]))
    dph, g_lbl = _hgrn_bwd(proj, do, states, lb_logits, exch.token)
    g_in_local = _dw_in(xb, dph, dpg, dph)

    late = (0,)
    swap = _swap_halves(late, (g_in_local,))
    grad_x = _in_bwd(dph, dpg, wb_in, dpre1, swap.token)
    exchanged = exch.wait(grad_x)
    pack = jnp.concatenate([
        g_ln1_g, g_ln1_b, g_ln2_g, g_ln2_b,
        jnp.concatenate([g_lbl[0:1], g_lbl[1:2]], axis=1),
        jnp.concatenate([g_gnw, g_conv[0:1]], axis=1),
        jnp.concatenate([g_conv[1:2], g_conv[2:3]], axis=1),
        jnp.concatenate([loss8[0:1], jnp.zeros((1, D_MODEL - LANES), F32)], axis=1)], axis=0)
    swapped = swap.wait(exchanged[0])
    exch = _exchange_pieces(late, add_halves(late, swapped[:1], swapped[1:]), pack)
    join = _join_halves(early, sum_pieces(early, exchanged[:3], exchanged[3:], exch.token))
    g_w_out, g_w_ff1, g_w_ff2 = join.wait(join.token)
    d_ff1, nm_ff1, nv_ff1 = _adamw("adamw_w_ff1", w_ff1[0], g_w_ff1, m_w_ff1[0], v_w_ff1[0])
    d_ff2, nm_ff2, nv_ff2 = _adamw("adamw_w_ff2", w_ff2[0], g_w_ff2, m_w_ff2[0], v_w_ff2[0], d_ff1)
    d_out, nm_out, nv_out = _adamw("adamw_w_out", w_out[0], g_w_out, m_w_out[0], v_w_out[0], d_ff2)
    exchanged = exch.wait(d_out)
    tot = _sum_shared(exchanged[2], exchanged[3], 2 * chip1 + core)
    loss = tot[7, 0]
    join = _join_halves(late, sum_pieces(late, exchanged[:1], exchanged[1:2], tot))
    small = ("lb_logits", "gate_norm_w", "conv_w", "ln1_g", "ln1_b", "ln2_g", "ln2_b")
    small_out = _adamw_small(
        tot, chip1, (lb_logits, gate_norm_w, conv_w[0], ln1_g, ln1_b, ln2_g, ln2_b),
        (m_lb_logits, m_gate_norm_w, m_conv_w[0], m_ln1_g, m_ln1_b, m_ln2_g, m_ln2_b),
        (v_lb_logits, v_gate_norm_w, v_conv_w[0], v_ln1_g, v_ln1_b, v_ln2_g, v_ln2_b), join.token)
    g_w_in, = join.wait(small_out[0])
    d_in, nm_in, nv_in = _adamw("adamw_w_in", w_in[0], g_w_in, m_w_in[0], v_w_in[0])

    def results(n_kind, large):
        out = dict(zip(small, small_out[n_kind * len(small):(n_kind + 1) * len(small)]))
        out["conv_w"] = out["conv_w"][None]
        out.update({name: a[None] for name, a in zip(("w_in", "w_out", "w_ff1", "w_ff2"), large)})
        return [out[name] for name in ("w_in", "lb_logits", "gate_norm_w", "conv_w", "w_out", "ln1_g", "ln1_b",
                                       "w_ff1", "w_ff2", "ln2_g", "ln2_b")]

    return (loss, grad_x[None], *results(0, (g_w_in, g_w_out, g_w_ff1, g_w_ff2)),
            *results(1, (d_in, d_out, d_ff1, d_ff2)), *results(2, (nm_in, nm_out, nm_ff1, nm_ff2)),
            *results(3, (nv_in, nv_out, nv_ff1, nv_ff2)))
```

```python
import jax
import jax.numpy as jnp
from jax import lax
from jax.experimental import pallas as pl
from jax.experimental.pallas import tpu as pltpu

F32 = jnp.float32
BF16 = jnp.bfloat16
MXU_DTYPE = jnp.bfloat16

D_MODEL = 1024
HGRN_WIDTH = 512
HEAD_DIM = 128
N_HEADS = 4
CONV_WIDTH = 512
CHUNK = 64
D_FF = 4096
IN_COLS = 3584
GROUP = 512
N_GROUPS = IN_COLS // GROUP
ALPHA = 2.0 ** 0.25
EPS = 1e-5
N_CHIPS = 4
ADAM_LR, ADAM_B1, ADAM_B2, ADAM_EPS, ADAM_WD, ADAM_STEP = 0.001, 0.9, 0.999, 1e-08, 0.01, 10

LANES = 128
SUBLANES = 8
VMEM_LIMIT = 56 * 1024 * 1024
FF_BLOCK = 1024
N_FF = D_FF // FF_BLOCK

NN = (((1,), (0,)), ((), ()))
NT = (((1,), (1,)), ((), ()))
TN = (((0,), (0,)), ((), ()))
MESH = pl.DeviceIdType.MESH
ANY = pl.BlockSpec(memory_space=pl.ANY)


def _dot(a, b, dims):
    return lax.dot_general(a.astype(MXU_DTYPE), b.astype(MXU_DTYPE), dims, preferred_element_type=F32)


def _dot_exact(ones, v):
    ones = ones.astype(jnp.bfloat16)
    hi = v.astype(jnp.bfloat16)
    rest = v - hi.astype(F32)
    mid = rest.astype(jnp.bfloat16)
    low = (rest - mid.astype(F32)).astype(jnp.bfloat16)
    return sum(lax.dot_general(ones, part, NN, preferred_element_type=F32) for part in (hi, mid, low))


def _params(*sem):
    return pltpu.CompilerParams(dimension_semantics=sem, vmem_limit_bytes=VMEM_LIMIT)


def _resident(shape):
    return pl.BlockSpec(shape, lambda *_: (0,) * len(shape), pipeline_mode=pl.Buffered(1))


def _sigmoid(v):
    return 1.0 / (1.0 + jnp.exp(-v))


def _lower_bound(lbl):
    m = jnp.max(lbl, axis=0, keepdims=True)
    e = jnp.exp(lbl - m)
    s = e / jnp.sum(e, axis=0, keepdims=True)
    return s[0:1, :], s[1:2, :]


def _heads(v):
    return [v[:, h * HEAD_DIM:(h + 1) * HEAD_DIM] for h in range(N_HEADS)]


def _per_head(fn, *arrays):
    return jnp.concatenate([fn(*parts) for parts in zip(*map(_heads, arrays))], axis=1)


def _in_proj(x, w_in, after):
    t = x.shape[0]
    tm = min(t, 512)

    def body(x_ref, w_ref, after_ref, o_ref, xb_ref):
        xb = x_ref[...].astype(xb_ref.dtype)
        xb_ref[...] = xb
        for g in range(N_GROUPS):
            o_ref[g] = _dot(xb, w_ref[:, g * GROUP:(g + 1) * GROUP], NN)

    return pl.pallas_call(
        body, name="in_proj", grid=(t // tm,),
        in_specs=[pl.BlockSpec((tm, D_MODEL), lambda i: (i, 0)), _resident((D_MODEL, IN_COLS)), ANY],
        out_specs=[pl.BlockSpec((N_GROUPS, tm, GROUP), lambda i: (0, i, 0)), pl.BlockSpec((tm, D_MODEL), lambda i: (i, 0))],
        out_shape=[jax.ShapeDtypeStruct((N_GROUPS, t, GROUP), F32), jax.ShapeDtypeStruct((t, D_MODEL), BF16)],
        compiler_params=_params("parallel"),
    )(x, w_in, after)


def _gates(fp, lb):
    sig = _sigmoid(fp)
    f = lb + (1.0 - lb) * sig
    return sig, f, jnp.log(f), 1.0 - f


def _chunk_masks():
    row = lax.broadcasted_iota(jnp.int32, (CHUNK, CHUNK), 0)
    col = lax.broadcasted_iota(jnp.int32, (CHUNK, CHUNK), 1)
    return row >= col, row <= col


def _hgrn_fwd(proj, lb_logits, after):
    t = proj.shape[1]
    tb = min(t, 512)
    ncb = tb // CHUNK

    def body(q_ref, f_ref, v_ref, lbl_ref, after_ref, o_ref, st_ref, s_scr):
        @pl.when(pl.program_id(0) == 0)
        def _():
            s_scr[...] = jnp.zeros_like(s_scr)

        lb, _ = _lower_bound(lbl_ref[...])
        causal, _ = _chunk_masks()

        every = range(ncb)
        rows = [slice(c * CHUNK, (c + 1) * CHUNK) for c in every]
        q, v = [q_ref[r, :] for r in rows], [v_ref[r, :] for r in rows]
        gates = [_gates(f_ref[r, :], lb) for r in rows]
        k = [gt[3] for gt in gates]
        b = [_dot_exact(causal, gt[2]) for gt in gates]
        mid, last = [x[CHUNK // 2:CHUNK // 2 + 1, :] for x in b], [x[CHUNK - 1:CHUNK, :] for x in b]
        qt = [q[c] * jnp.exp(b[c] - mid[c]) for c in every]
        kt = [k[c] * jnp.exp(mid[c] - b[c]) for c in every]
        qi = [q[c] * jnp.exp(b[c]) for c in every]
        ks = [k[c] * jnp.exp(last[c] - b[c]) for c in every]
        dec = [jnp.exp(x) for x in last]
        scores = [[jnp.where(causal, _dot(a, b_, NT), 0.0) for a, b_ in zip(_heads(qt[c]), _heads(kt[c]))] for c in every]
        intra = [[_dot(s, v_h, NN) for s, v_h in zip(scores[c], _heads(v[c]))] for c in every]
        update = [_per_head(lambda v_h, ks_h: _dot(v_h, ks_h, TN), v[c], ks[c]) for c in every]

        st = s_scr[...]
        states = []
        for c in every:
            states.append(st)
            st_ref[c] = st
            st = dec[c] * st + update[c]
        s_scr[...] = st

        o_ref[...] = jnp.concatenate(
            [jnp.concatenate([i_h + _dot(qi_h, st_h, NT) for i_h, qi_h, st_h in
                              zip(intra[c], _heads(qi[c]), _heads(states[c]))], axis=1) for c in every], axis=0)

    grp = lambda g: pl.BlockSpec((None, tb, GROUP), lambda i: (g, i, 0))
    return pl.pallas_call(
        body, name="hgrn_fwd", grid=(t // tb,),
        in_specs=[grp(0), grp(1), grp(2), pl.BlockSpec((2, HGRN_WIDTH), lambda i: (0, 0)), ANY],
        out_specs=[pl.BlockSpec((tb, HGRN_WIDTH), lambda i: (i, 0)),
                   pl.BlockSpec((ncb, HEAD_DIM, HGRN_WIDTH), lambda i: (i, 0, 0))],
        out_shape=[jax.ShapeDtypeStruct((t, HGRN_WIDTH), F32),
                   jax.ShapeDtypeStruct((t // CHUNK, HEAD_DIM, HGRN_WIDTH), F32)],
        scratch_shapes=[pltpu.VMEM((HEAD_DIM, HGRN_WIDTH), F32)],
        compiler_params=_params("arbitrary"),
    )(proj, proj, proj, lb_logits, after)


def _conv_taps(z, halo, zbuf, tb):
    zbuf[0:SUBLANES, :] = halo
    zbuf[SUBLANES:SUBLANES + tb, :] = z
    return zbuf[SUBLANES - 1:SUBLANES - 1 + tb, :], zbuf[SUBLANES - 2:SUBLANES - 2 + tb, :]


def _gate_fwd(proj, o, gate_norm_w, conv_w, after):
    t = proj.shape[1]
    tb = min(t, 512)
    hb = tb // SUBLANES

    def body(o_ref, og_ref, gnw_ref, b_ref, c_ref, u_ref, ch_ref, uh_ref, cw_ref, after_ref, cat_ref, zbuf):
        i = pl.program_id(0)
        og = og_ref[...]
        on = _per_head(lambda o_h: o_h * lax.rsqrt(jnp.mean(o_h * o_h, axis=-1, keepdims=True) + EPS), o_ref[...])
        cat_ref[0] = (on * gnw_ref[...] * (og * _sigmoid(og))).astype(cat_ref.dtype)
        z = c_ref[...] * u_ref[...]
        halo = jnp.where(i > 0, ch_ref[...] * uh_ref[...], 0.0)
        z1, z2 = _conv_taps(z, halo, zbuf, tb)
        cw = cw_ref[...]
        yc = cw[2:3, :] * z + cw[1:2, :] * z1 + cw[0:1, :] * z2
        cat_ref[1] = (b_ref[...] * yc).astype(cat_ref.dtype)

    grp = lambda g: pl.BlockSpec((None, tb, GROUP), lambda i: (g, i, 0))
    prev = lambda g: pl.BlockSpec((None, SUBLANES, GROUP), lambda i: (g, jnp.maximum(i * hb - 1, 0), 0))
    vec = lambda r: pl.BlockSpec((r, GROUP), lambda i: (0, 0))
    return pl.pallas_call(
        body, name="gate_fwd", grid=(t // tb,),
        in_specs=[pl.BlockSpec((tb, GROUP), lambda i: (i, 0)), grp(3), vec(1), grp(4), grp(5), grp(6), prev(5), prev(6),
                  vec(3), ANY],
        out_specs=pl.BlockSpec((2, tb, GROUP), lambda i: (0, i, 0)),
        out_shape=jax.ShapeDtypeStruct((2, t, HGRN_WIDTH), BF16),
        scratch_shapes=[pltpu.VMEM((tb + SUBLANES, GROUP), F32)],
        compiler_params=_params("parallel"),
    )(o, proj, gate_norm_w, proj, proj, proj, proj, proj, conv_w, after)


def _ln_bwd(dy, xhat, rstd, g):
    dxhat = dy * g
    m1 = jnp.mean(dxhat, axis=-1, keepdims=True)
    m2 = jnp.mean(dxhat * xhat, axis=-1, keepdims=True)
    return rstd * (dxhat - m1 - xhat * m2)


def _layer_norm(pre):
    xc = pre - jnp.mean(pre, axis=-1, keepdims=True)
    rstd = lax.rsqrt(jnp.mean(xc * xc, axis=-1, keepdims=True) + EPS)
    return xc * rstd, rstd


def _sublayers(cat, x, target, w_out, w_ff1, w_ff2, g1, b1, g2, b2):
    t = x.shape[0]
    tm = min(t, 256)

    def body(cat_ref, x_ref, tg_ref, wo_ref, w1_ref, w2_ref, g1_ref, b1_ref, g2_ref, b2_ref,
             h1_ref, r_ref, da_ref, dp2b_ref, dp1_ref, dp1b_ref, dcat_ref, dg1_ref, db1_ref, dg2_ref, db2_ref, loss_ref):
        @pl.when(pl.program_id(0) == 0)
        def _():
            for ref in (dg1_ref, db1_ref, dg2_ref, db2_ref, loss_ref):
                ref[...] = jnp.zeros_like(ref)

        mix = _dot(cat_ref[0], wo_ref[0:GROUP, :], NN) + _dot(cat_ref[1], wo_ref[GROUP:2 * GROUP, :], NN)
        xhat1, rstd1 = _layer_norm(ALPHA * x_ref[...] + mix)
        h1 = xhat1 * g1_ref[...] + b1_ref[...]
        h1b = h1.astype(h1_ref.dtype)
        h1_ref[...] = h1b
        mlp = jnp.zeros((tm, D_MODEL), F32)
        for j in range(N_FF):
            cols = slice(j * FF_BLOCK, (j + 1) * FF_BLOCK)
            r = jnp.square(jnp.maximum(_dot(h1b, w1_ref[:, cols], NN), 0.0)).astype(r_ref.dtype)
            r_ref[:, cols] = r
            mlp = mlp + _dot(r, w2_ref[cols, :], NN)
        xhat2, rstd2 = _layer_norm(ALPHA * h1 + mlp)
        err = xhat2 * g2_ref[...] + b2_ref[...] - tg_ref[...]
        loss_ref[...] += 0.5 * jnp.sum(jnp.mean(err * err, axis=-1, keepdims=True))
        dy = err * (1.0 / D_MODEL)
        dg2_ref[...] += jnp.sum(dy * xhat2, axis=0, keepdims=True)
        db2_ref[...] += jnp.sum(dy, axis=0, keepdims=True)
        dp2 = _ln_bwd(dy, xhat2, rstd2, g2_ref[...])
        dp2b = dp2.astype(dp2b_ref.dtype)
        dp2b_ref[...] = dp2b
        back = jnp.zeros((tm, D_MODEL), F32)
        for j in range(N_FF):
            cols = slice(j * FF_BLOCK, (j + 1) * FF_BLOCK)
            dr = _dot(dp2b, w2_ref[cols, :], NT)
            da = (dr * (2.0 * jnp.sqrt(r_ref[:, cols].astype(F32)))).astype(da_ref.dtype)
            da_ref[:, cols] = da
            back = back + _dot(da, w1_ref[:, cols], NT)
        dh1 = ALPHA * dp2 + back
        dg1_ref[...] += jnp.sum(dh1 * xhat1, axis=0, keepdims=True)
        db1_ref[...] += jnp.sum(dh1, axis=0, keepdims=True)
        dp1 = _ln_bwd(dh1, xhat1, rstd1, g1_ref[...])
        dp1b = dp1.astype(dp1b_ref.dtype)
        dp1_ref[...] = dp1
        dp1b_ref[...] = dp1b
        dcat_ref[...] = _dot(dp1b, wo_ref[...], NT)

    row = pl.BlockSpec((tm, D_MODEL), lambda i: (i, 0))
    wide = pl.BlockSpec((tm, D_FF), lambda i: (i, 0))
    vec = pl.BlockSpec((1, D_MODEL), lambda i: (0, 0))
    narrow = lambda dtype: jax.ShapeDtypeStruct((t, D_MODEL), dtype)
    return pl.pallas_call(
        body, name="sublayers", grid=(t // tm,),
        in_specs=[pl.BlockSpec((2, tm, GROUP), lambda i: (0, i, 0)), row, row, _resident((D_MODEL, D_MODEL)),
                  _resident((D_MODEL, D_FF)), _resident((D_FF, D_MODEL)), vec, vec, vec, vec],
        out_specs=[row, wide, wide, row, row, row, row, vec, vec, vec, vec,
                   pl.BlockSpec((SUBLANES, LANES), lambda i: (0, 0))],
        out_shape=[narrow(BF16), jax.ShapeDtypeStruct((t, D_FF), BF16), jax.ShapeDtypeStruct((t, D_FF), BF16),
                   narrow(BF16), narrow(F32), narrow(BF16), narrow(F32)]
                  + [jax.ShapeDtypeStruct((1, D_MODEL), F32)] * 4 + [jax.ShapeDtypeStruct((SUBLANES, LANES), F32)],
        compiler_params=_params("arbitrary"),
    )(cat, x, target, w_out, w_ff1, w_ff2, g1, b1, g2, b2)


def _gate_bwd(dcat, o, proj, gate_norm_w, conv_w, after):
    t = proj.shape[1]
    tb = min(t, 512)
    hb = tb // SUBLANES
    nblk = t // tb

    def body(do2_ref, dy_ref, dyn_ref, o_ref, og_ref, gnw_ref, b_ref, bn_ref, c_ref, u_ref, ch_ref, uh_ref, cw_ref,
             after_ref, do_ref, dp_ref, dgnw_ref, dcw_ref, zbuf, dbuf):
        i = pl.program_id(0)

        @pl.when(i == 0)
        def _():
            dgnw_ref[...] = jnp.zeros_like(dgnw_ref)
            dcw_ref[...] = jnp.zeros_like(dcw_ref)

        ov, og, gnw, do2 = o_ref[...], og_ref[...], gnw_ref[...], do2_ref[...]
        rs = _per_head(lambda o_h: jnp.broadcast_to(lax.rsqrt(jnp.mean(o_h * o_h, axis=-1, keepdims=True) + EPS),
                                                    o_h.shape), ov)
        on = ov * rs
        sg = _sigmoid(og)
        sil = og * sg
        don = do2 * gnw * sil
        dgnw_ref[...] += jnp.sum(do2 * on * sil, axis=0, keepdims=True)
        dp_ref[0] = (do2 * on * gnw * (sg * (1.0 + og * (1.0 - sg)))).astype(dp_ref.dtype)
        do_ref[...] = rs * (don - on * _per_head(
            lambda p_h: jnp.broadcast_to(jnp.mean(p_h, axis=-1, keepdims=True), p_h.shape), don * on))

        bg, cg, u, dy = b_ref[...], c_ref[...], u_ref[...], dy_ref[...]
        z = cg * u
        halo = jnp.where(i > 0, ch_ref[...] * uh_ref[...], 0.0)
        z1, z2 = _conv_taps(z, halo, zbuf, tb)
        cw = cw_ref[...]
        yc = cw[2:3, :] * z + cw[1:2, :] * z1 + cw[0:1, :] * z2
        dyc = dy * bg
        dbuf[0:tb, :] = dyc
        dbuf[tb:tb + SUBLANES, :] = jnp.where(i < nblk - 1, dyn_ref[...] * bn_ref[...], 0.0)
        d1, d2 = dbuf[1:1 + tb, :], dbuf[2:2 + tb, :]
        dz = cw[2:3, :] * dyc + cw[1:2, :] * d1 + cw[0:1, :] * d2
        dp_ref[1] = (dy * yc).astype(dp_ref.dtype)
        dp_ref[2] = (dz * u).astype(dp_ref.dtype)
        dp_ref[3] = (dz * cg).astype(dp_ref.dtype)
        dcw_ref[0:1, :] += jnp.sum(dyc * z2, axis=0, keepdims=True)
        dcw_ref[1:2, :] += jnp.sum(dyc * z1, axis=0, keepdims=True)
        dcw_ref[2:3, :] += jnp.sum(dyc * z, axis=0, keepdims=True)

    half = lambda g: pl.BlockSpec((tb, GROUP), lambda i: (i, g))
    grp = lambda g: pl.BlockSpec((None, tb, GROUP), lambda i: (g, i, 0))
    prev = lambda g: pl.BlockSpec((None, SUBLANES, GROUP), lambda i: (g, jnp.maximum(i * hb - 1, 0), 0))
    nxt_row = lambda i: jnp.minimum((i + 1) * hb, t // SUBLANES - 1)
    nxt = lambda g: pl.BlockSpec((None, SUBLANES, GROUP), lambda i: (g, nxt_row(i), 0))
    vec = lambda r: pl.BlockSpec((r, GROUP), lambda i: (0, 0))
    return pl.pallas_call(
        body, name="gate_bwd", grid=(nblk,),
        in_specs=[half(0), half(1), pl.BlockSpec((SUBLANES, GROUP), lambda i: (nxt_row(i), 1)), half(0), grp(3), vec(1),
                  grp(4), nxt(4), grp(5), grp(6), prev(5), prev(6), vec(3), ANY],
        out_specs=[half(0), pl.BlockSpec((4, tb, GROUP), lambda i: (0, i, 0)), vec(1), vec(3)],
        out_shape=[jax.ShapeDtypeStruct((t, HGRN_WIDTH), F32), jax.ShapeDtypeStruct((4, t, HGRN_WIDTH), BF16),
                   jax.ShapeDtypeStruct((1, HGRN_WIDTH), F32), jax.ShapeDtypeStruct((3, CONV_WIDTH), F32)],
        scratch_shapes=[pltpu.VMEM((tb + SUBLANES, GROUP), F32), pltpu.VMEM((tb + SUBLANES, GROUP), F32)],
        compiler_params=_params("arbitrary"),
    )(dcat, dcat, dcat, o, proj, gate_norm_w, proj, proj, proj, proj, proj, proj, conv_w, after)


def _hgrn_bwd(proj, do, states, lb_logits, after):
    t = proj.shape[1]
    tb = min(t, 512)
    ncb = tb // CHUNK
    nblk = t // tb

    def body(q_ref, f_ref, v_ref, do_ref, st_ref, lbl_ref, after_ref, dp_ref, dlbl_ref, ds_scr, dlb_scr):
        i = pl.program_id(0)

        @pl.when(i == 0)
        def _():
            ds_scr[...] = jnp.zeros_like(ds_scr)
            dlb_scr[...] = jnp.zeros_like(dlb_scr)

        lb, s1 = _lower_bound(lbl_ref[...])
        causal, anti = _chunk_masks()
        every = range(ncb)
        rows = [slice(c * CHUNK, (c + 1) * CHUNK) for c in every]
        q, v, do = ([ref[r, :] for r in rows] for ref in (q_ref, v_ref, do_ref))
        st = [st_ref[c] for c in every]
        gates = [_gates(f_ref[r, :], lb) for r in rows]
        sig, f, k = ([gt[n] for gt in gates] for n in (0, 1, 3))
        b = [_dot_exact(causal, gt[2]) for gt in gates]
        mid, last = [x[CHUNK // 2:CHUNK // 2 + 1, :] for x in b], [x[CHUNK - 1:CHUNK, :] for x in b]
        e_q = [jnp.exp(b[c] - mid[c]) for c in every]
        e_k = [jnp.exp(mid[c] - b[c]) for c in every]
        e_i = [jnp.exp(x) for x in b]
        e_s = [jnp.exp(last[c] - b[c]) for c in every]
        dec = [jnp.exp(x) for x in last]
        qt, kt, qi, ks = ([a[c] * e[c] for c in every] for a, e in ((q, e_q), (k, e_k), (q, e_i), (k, e_s)))

        def masked(a, b_):
            return [[jnp.where(causal, _dot(a_h, b_h, NT), 0.0) for a_h, b_h in zip(_heads(a[c]), _heads(b_[c]))]
                    for c in every]

        def with_scores(s, other, dims):
            return [jnp.concatenate([_dot(s_h, o_h, dims) for s_h, o_h in zip(s[c], _heads(other[c]))], axis=1)
                    for c in every]

        def per_head(dims, a, b_):
            return [_per_head(lambda a_h, b_h: _dot(a_h, b_h, dims), a[c], b_[c]) for c in every]

        scores, dscores = masked(qt, kt), masked(do, v)
        dqt, dkt, dv_intra = with_scores(dscores, kt, NN), with_scores(dscores, qt, TN), with_scores(scores, do, TN)
        dqi, update = per_head(NN, do, st), per_head(TN, do, qi)

        dst = ds_scr[...]
        dsts = [None] * ncb
        for c in reversed(every):
            dsts[c] = dst
            dst = dec[c] * dst + update[c]
        ds_scr[...] = dst

        dv_state, dks = per_head(NT, ks, dsts), per_head(NN, v, dsts)
        ddec = [jnp.sum(dsts[c] * st[c], axis=0, keepdims=True) for c in every]
        dq = [dqt[c] * e_q[c] + dqi[c] * e_i[c] for c in every]
        dk = [dkt[c] * e_k[c] + dks[c] * e_s[c] for c in every]
        db = [q[c] * dq[c] - k[c] * dk[c] for c in every]
        db_last = [jnp.sum(dks[c] * ks[c], axis=0, keepdims=True) + ddec[c] * dec[c] for c in every]
        dg = [_dot_exact(anti, db[c]) + db_last[c] for c in every]
        df = [dg[c] / f[c] - dk[c] for c in every]
        dlb_scr[...] += sum(jnp.sum(df[c] * (1.0 - sig[c]), axis=0, keepdims=True) for c in every)
        dfp = [df[c] * (1.0 - lb) * sig[c] * (1.0 - sig[c]) for c in every]
        dv = [dv_intra[c] + dv_state[c] for c in every]
        for n, parts in enumerate((dq, dfp, dv)):
            dp_ref[n] = jnp.concatenate(parts, axis=0).astype(dp_ref.dtype)

        @pl.when(i == nblk - 1)
        def _():
            dlb = dlb_scr[...]
            dlbl_ref[0:1, :] = dlb * lb * (1.0 - lb)
            dlbl_ref[1:2, :] = -dlb * lb * s1

    grp = lambda g: pl.BlockSpec((None, tb, GROUP), lambda i: (g, nblk - 1 - i, 0))
    vec = pl.BlockSpec((2, HGRN_WIDTH), lambda i: (0, 0))
    return pl.pallas_call(
        body, name="hgrn_bwd", grid=(nblk,),
        in_specs=[grp(0), grp(1), grp(2), pl.BlockSpec((tb, HGRN_WIDTH), lambda i: (nblk - 1 - i, 0)),
                  pl.BlockSpec((ncb, HEAD_DIM, HGRN_WIDTH), lambda i: (nblk - 1 - i, 0, 0)), vec, ANY],
        out_specs=[pl.BlockSpec((3, tb, HGRN_WIDTH), lambda i: (0, nblk - 1 - i, 0)), vec],
        out_shape=[jax.ShapeDtypeStruct((3, t, HGRN_WIDTH), BF16), jax.ShapeDtypeStruct((2, HGRN_WIDTH), F32)],
        scratch_shapes=[pltpu.VMEM((HEAD_DIM, HGRN_WIDTH), F32), pltpu.VMEM((1, HGRN_WIDTH), F32)],
        compiler_params=_params("arbitrary"),
    )(proj, proj, proj, do, states, lb_logits, after)


def _in_bwd(dph, dpg, w_in, dpre1, after):
    t = dpre1.shape[0]
    tm = min(t, 512)

    def body(dh_ref, dg_ref, w_ref, dp_ref, after_ref, o_ref):
        acc = ALPHA * dp_ref[...]
        for g in range(N_GROUPS):
            part = dh_ref[g] if g < 3 else dg_ref[g - 3]
            acc = acc + _dot(part, w_ref[:, g * GROUP:(g + 1) * GROUP], NT)
        o_ref[...] = acc

    row = pl.BlockSpec((tm, D_MODEL), lambda i: (i, 0))
    return pl.pallas_call(
        body, name="in_bwd", grid=(t // tm,),
        in_specs=[pl.BlockSpec((3, tm, GROUP), lambda i: (0, i, 0)), pl.BlockSpec((4, tm, GROUP), lambda i: (0, i, 0)),
                  _resident((D_MODEL, IN_COLS)), row, ANY],
        out_specs=row,
        out_shape=jax.ShapeDtypeStruct((t, D_MODEL), F32),
        compiler_params=_params("parallel"),
    )(dph, dpg, w_in, dpre1, after)


def _grad_w(name, operands, widths, shape, step, after=None):
    t = operands[0].shape[-2]
    tt = min(t, 512)
    n_in, n_steps = len(operands), t // tt
    in_specs = [pl.BlockSpec((tt, w), lambda k: (k, 0)) if a.ndim == 2 else
                pl.BlockSpec((a.shape[0], tt, w), lambda k: (0, k, 0)) for a, w in zip(operands, widths)]
    extra = [] if after is None else [after]

    def body(*refs):
        o_ref, acc, narrow, sem = refs[-4:]
        k = pl.program_id(0)

        @pl.when(k == 0)
        def _():
            acc[...] = jnp.zeros_like(acc)

        step(acc, *refs[:n_in])

        @pl.when(k == n_steps - 1)
        def _():
            narrow[...] = acc[...].astype(narrow.dtype)
            out = pltpu.make_async_copy(narrow, o_ref, sem)
            out.start()
            out.wait()

    return pl.pallas_call(
        body, name=name, grid=(n_steps,), in_specs=in_specs + [ANY] * len(extra), out_specs=ANY,
        out_shape=jax.ShapeDtypeStruct(shape, BF16),
        scratch_shapes=[pltpu.VMEM(shape, F32), pltpu.VMEM(shape, BF16), pltpu.SemaphoreType.DMA],
        compiler_params=_params("arbitrary"),
    )(*operands, *extra)


def _dw_in(xb, dph, dpg, after):
    def step(acc, x_ref, dh_ref, dg_ref):
        xv = x_ref[...]
        for g in range(N_GROUPS):
            part = dh_ref[g] if g < 3 else dg_ref[g - 3]
            acc[:, g * GROUP:(g + 1) * GROUP] += _dot(xv, part, TN)

    return _grad_w("dw_in", (xb, dph, dpg), (D_MODEL, GROUP, GROUP), (D_MODEL, IN_COLS), step, after)


def _dw_out(cat, dpre1b):
    def step(acc, cat_ref, d_ref):
        dv = d_ref[...]
        for g in range(2):
            acc[g * GROUP:(g + 1) * GROUP, :] += _dot(cat_ref[g], dv, TN)

    return _grad_w("dw_out", (cat, dpre1b), (GROUP, D_MODEL), (D_MODEL, D_MODEL), step)


def _dw_ff1(h1b, da):
    def step(acc, h_ref, da_ref):
        hv = h_ref[...]
        for j in range(D_FF // FF_BLOCK):
            cols = slice(j * FF_BLOCK, (j + 1) * FF_BLOCK)
            acc[:, cols] += _dot(hv, da_ref[:, cols], TN)

    return _grad_w("dw_ff1", (h1b, da), (D_MODEL, D_FF), (D_MODEL, D_FF), step)


def _dw_ff2(r, dpre2b):
    def step(acc, r_ref, d_ref):
        dv = d_ref[...]
        for j in range(D_FF // FF_BLOCK):
            rows = slice(j * FF_BLOCK, (j + 1) * FF_BLOCK)
            acc[rows, :] += _dot(r_ref[:, rows], dv, TN)

    return _grad_w("dw_ff2", (r, dpre2b), (D_FF, D_MODEL), (D_FF, D_MODEL), step)


def _place():
    x, y, c = lax.axis_index("x"), lax.axis_index("y"), lax.axis_index("c")
    return x, y, c, 2 * x + y


def _other_chips(x, y):
    return [(1 - x, y), (x, 1 - y), (1 - x, 1 - y)]


def _place_shard(name, w, chip, cols_sharded, after=None):
    rows, cols = w.shape
    tr = min(rows, 256)
    nb = rows // tr
    full = (rows, cols * N_CHIPS) if cols_sharded else (rows * N_CHIPS, cols)
    out_map = (lambda i, s: (i, s[0])) if cols_sharded else (lambda i, s: (s[0] * nb + i, 0))

    def body(s_ref, w_ref, *rest):
        rest[-1][...] = w_ref[...].astype(rest[-1].dtype)

    extra = [] if after is None else [after]
    return pl.pallas_call(
        body, name=name,
        grid_spec=pltpu.PrefetchScalarGridSpec(
            num_scalar_prefetch=1, grid=(nb,),
            in_specs=[pl.BlockSpec((tr, cols), lambda i, s: (i, 0))] + [ANY] * len(extra),
            out_specs=pl.BlockSpec((tr, cols), out_map)),
        out_shape=jax.ShapeDtypeStruct(full, BF16),
        compiler_params=_params("parallel"),
    )(chip, w, *extra)


HBM = pl.BlockSpec(memory_space=pltpu.HBM)
SEM = pl.BlockSpec(memory_space=pltpu.SEMAPHORE)
EFFECT = pltpu.SideEffectType.DATAFLOW_SIDE_EFFECTING


class _Split:
    def __init__(self, name, arrays, plan, others=()):
        n_own, arrays = len(arrays), (*arrays, *others)
        n, n_copies = len(arrays), plan.count
        self.name, self.plan, self.n = name, plan, n_own

        def body(*refs):
            send_sems, recv_sems, token = refs[n], refs[n + 1], refs[-1]
            for k, (src, dst, to) in enumerate(plan(refs[:n])):
                pltpu.make_async_remote_copy(src_ref=src, dst_ref=dst, send_sem=send_sems.at[k], recv_sem=recv_sems.at[k],
                                             device_id=to, device_id_type=MESH).start()
            token[...] = jnp.zeros_like(token)

        outs = pl.pallas_call(
            body, name=name + "_start",
            out_shape=(pltpu.SemaphoreType.DMA((n_copies,)), pltpu.SemaphoreType.DMA((n_copies,)),
                       *[pltpu.HBM(a.shape, a.dtype) for a in arrays], jax.ShapeDtypeStruct((SUBLANES, LANES), F32)),
            in_specs=(HBM,) * n, out_specs=(SEM, SEM) + (HBM,) * n + (pl.BlockSpec(memory_space=pltpu.VMEM),),
            input_output_aliases={i: 2 + i for i in range(n)},
            compiler_params=pltpu.CompilerParams(has_side_effects=EFFECT),
        )(*[pltpu.with_memory_space_constraint(a, pltpu.HBM) for a in arrays])
        self.sems, self.arrays, self.others, self.token = outs[:2], outs[2:2 + n_own], outs[2 + n_own:2 + n], outs[-1]

    def wait(self, after):
        n, plan = self.n, self.plan

        def body(*refs):
            send_sems, recv_sems = refs[n], refs[n + 1]
            for k, (src, dst, to) in enumerate(plan(refs[:n])):
                cp = pltpu.make_async_remote_copy(src_ref=src, dst_ref=dst, send_sem=send_sems.at[k],
                                                  recv_sem=recv_sems.at[k], device_id=to, device_id_type=MESH)
                cp.wait_send()
                cp.wait_recv()

        return pl.pallas_call(
            body, name=self.name + "_wait", out_shape=tuple(pltpu.HBM(a.shape, a.dtype) for a in self.arrays),
            in_specs=(HBM,) * n + (SEM, SEM, ANY), out_specs=(HBM,) * n, input_output_aliases={i: i for i in range(n)},
            compiler_params=pltpu.CompilerParams(has_side_effects=EFFECT),
        )(*self.arrays, *self.sems, after)


COLS_SHARDED = (True, False, True, False)
HALF_SHAPES = [(D_MODEL // 2, IN_COLS), (D_MODEL, D_MODEL // 2), (D_MODEL // 2, D_FF), (D_FF, D_MODEL // 2)]
PIECE_SHAPES = [(D_MODEL // 2, IN_COLS // N_CHIPS), (D_MODEL // N_CHIPS, D_MODEL // 2),
                (D_MODEL // 2, D_FF // N_CHIPS), (D_FF // N_CHIPS, D_MODEL // 2)]


def _shard_view(kind, ref, chip):
    if COLS_SHARDED[kind]:
        n = ref.shape[1] // N_CHIPS
        return ref.at[:, pl.ds(chip * n, n)]
    n = ref.shape[0] // N_CHIPS
    return ref.at[pl.ds(chip * n, n), :]


def _half_view(kind, ref, h):
    if COLS_SHARDED[kind]:
        n = ref.shape[0] // 2
        return ref.at[pl.ds(h * n, n), :]
    n = ref.shape[1] // 2
    return ref.at[:, pl.ds(h * n, n)]


def _plan(count):
    def mark(fn):
        fn.count = count
        return fn
    return mark


def _shard_rows_view(kind, ref, chip, part, n_parts):
    if COLS_SHARDED[kind]:
        m, n = ref.shape[0] // n_parts, ref.shape[1] // N_CHIPS
        return ref.at[pl.ds(part * m, m), pl.ds(chip * n, n)]
    m = ref.shape[0] // N_CHIPS // n_parts
    return ref.at[pl.ds((n_parts * chip + part) * m, m), :]


def _shard_half_view(kind, ref, chip, h):
    return _shard_rows_view(kind, ref, chip, h, 2)


def _gather_over_ici(kinds, weights):
    @_plan(2 * len(kinds))
    def plan(refs):
        x, y, c, me = _place()
        mine = [_shard_half_view(kind, ref, me, c) for kind, ref in zip(kinds, refs)]
        return [(v, v, to) for v in mine for to in ((1 - x, y, c), (x, 1 - y, c))]

    return _Split("gather_ici_" + "".join(map(str, kinds)), tuple(weights), plan)


def _relay_over_ici(kinds, weights, others=()):
    @_plan(2 * len(kinds))
    def plan(refs):
        x, y, c, _ = _place()
        x_nbr, y_nbr = 2 * (1 - x) + y, 2 * x + (1 - y)
        out = []
        for kind, ref in zip(kinds, refs):
            first, second = (_shard_rows_view(kind, ref, chip, 2 * c + q, 4) for q, chip in ((0, x_nbr), (1, y_nbr)))
            out += [(first, first, (x, 1 - y, c)), (second, second, (1 - x, y, c))]
        return out

    return _Split("relay_ici_" + "".join(map(str, kinds)), tuple(weights), plan, others)


def _gather_w_in_over_ici(w_in, conv4):
    @_plan(5)
    def plan(refs):
        x, y, c, me = _place()
        half, conv = _shard_half_view(0, refs[0], me, c), refs[1].at[me]
        return ([(half, half, to) for to in ((1 - x, y, c), (x, 1 - y, c))]
                + [(conv, conv, (px, py, c)) for px, py in _other_chips(x, y)])

    return _Split("gather_w_in_ici", (w_in, conv4), plan)


def _gather_over_d2d(kinds, weights):
    @_plan(3 * len(kinds))
    def plan(refs):
        x, y, c, _ = _place()
        got = [_shard_half_view(kind, ref, 2 * px + py, c) for kind, ref in zip(kinds, refs)
               for px, py in _other_chips(x, y)]
        return [(v, v, (x, y, 1 - c)) for v in got]

    return _Split("gather_d2d_" + "".join(map(str, kinds)), tuple(weights), plan)


def _swap_halves(kinds, grads):
    @_plan(len(kinds))
    def plan(refs):
        x, y, c, _ = _place()
        return [(_half_view(kind, g, 1 - c), land, (x, y, 1 - c))
                for kind, g, land in zip(kinds, refs[:len(kinds)], refs[len(kinds):])]

    lands = [lax.empty(HALF_SHAPES[kind], g.dtype) for kind, g in zip(kinds, grads)]
    return _Split("swap_halves_" + "".join(map(str, kinds)), (*grads, *lands), plan)


def _add_half(name, g, recv, core, rows_split):
    shape = recv.shape
    tr = min(shape[0], 128 if rows_split else 256)
    nb = shape[0] // tr

    def body(c_ref, g_ref, r_ref, o_ref):
        o_ref[...] = (g_ref[...].astype(F32) + r_ref[...].astype(F32)).astype(o_ref.dtype)

    g_map = (lambda i, c_ref: (c_ref[0] * nb + i, 0)) if rows_split else (lambda i, c_ref: (i, c_ref[0]))
    blk = pl.BlockSpec((tr, shape[1]), lambda i, c_ref: (i, 0))
    return pl.pallas_call(
        body, name=name,
        grid_spec=pltpu.PrefetchScalarGridSpec(
            num_scalar_prefetch=1, grid=(nb,),
            in_specs=[pl.BlockSpec((tr, shape[1]), g_map), blk], out_specs=blk),
        out_shape=jax.ShapeDtypeStruct(shape, BF16),
        compiler_params=_params("parallel"),
    )(core, g, recv)


def _exchange_pieces(kinds, halves, pack=None):
    n_p, n = N_CHIPS - 1, len(kinds)

    @_plan(n_p * n + (0 if pack is None else N_DEV - 1))
    def plan(refs):
        x, y, c, _ = _place()
        copies = []
        if pack is not None:
            me = 4 * x + 2 * y + c
            peers = [((1 - x) if m & 4 else x, (1 - y) if m & 2 else y, (1 - c) if m & 1 else c) for m in range(1, N_DEV)]
            copies += [(refs[2 * n], refs[2 * n + 1].at[me], peer) for peer in peers]
        return copies + [(_shard_view(kind, half, 2 * px + py), land.at[j], (px, py, c))
                         for j, (px, py) in enumerate(_other_chips(x, y))
                         for kind, half, land in zip(kinds, refs[:n], refs[n:2 * n])]

    lands = [lax.empty((n_p,) + PIECE_SHAPES[kind], BF16) for kind in kinds]
    small = () if pack is None else (pack, lax.empty((N_DEV,) + pack.shape, F32))
    return _Split("exchange_pieces_" + "".join(map(str, kinds)), (*halves, *lands, *small), plan)


def _sum_pieces(name, half, slots, place, rows_split, after):
    n_p, rows, cols = slots.shape
    tr = min(rows, 256)
    nb = rows // tr
    if rows_split:
        own_map = lambda i, s: (i, s[0])
        out_map = lambda i, s: (s[1] * nb + i, 0)
        shard = (2 * rows, cols)
    else:
        own_map = lambda i, s: (s[0] * nb + i, 0)
        out_map = lambda i, s: (i, s[1])
        shard = (rows, 2 * cols)

    def body(s_ref, own_ref, slot_ref, after_ref, o_ref):
        total = own_ref[...].astype(F32)
        for j in range(n_p):
            total = total + slot_ref[j].astype(F32)
        o_ref[...] = total

    return pl.pallas_call(
        body, name=name,
        grid_spec=pltpu.PrefetchScalarGridSpec(
            num_scalar_prefetch=1, grid=(nb,),
            in_specs=[pl.BlockSpec((tr, cols), own_map), pl.BlockSpec((n_p, tr, cols), lambda i, s: (0, i, 0)), ANY],
            out_specs=pl.BlockSpec((tr, cols), out_map)),
        out_shape=jax.ShapeDtypeStruct(shard, F32),
        compiler_params=_params("parallel"),
    )(place, half, slots, after)


def _join_halves(kinds, shards):
    @_plan(len(kinds))
    def plan(refs):
        x, y, c, _ = _place()
        return [(_half_view(kind, g, c), _half_view(kind, g, c), (x, y, 1 - c)) for kind, g in zip(kinds, refs)]

    return _Split("join_halves_" + "".join(map(str, kinds)), tuple(shards), plan)


N_DEV = 8


def _sum_shared(pack, land, device):
    def body(d_ref, p_ref, l_ref, o_ref):
        me = d_ref[0]
        total = jnp.where(me == 0, p_ref[...], l_ref[0])
        for d in range(1, N_DEV):
            total = total + jnp.where(me == d, p_ref[...], l_ref[d])
        o_ref[...] = total

    return pl.pallas_call(
        body, name="sum_shared",
        grid_spec=pltpu.PrefetchScalarGridSpec(
            num_scalar_prefetch=1, grid=(1,),
            in_specs=[pl.BlockSpec(pack.shape, lambda i, d: (0, 0)), pl.BlockSpec(land.shape, lambda i, d: (0, 0, 0))],
            out_specs=pl.BlockSpec(pack.shape, lambda i, d: (0, 0))),
        out_shape=jax.ShapeDtypeStruct(pack.shape, F32),
    )(device, pack, land)


def _adamw(name, w, g, m, v, after=None):
    rows, cols = w.shape
    tr = min(rows, 256)
    extra = [] if after is None else [after]

    def body(w_ref, g_ref, m_ref, v_ref, *rest):
        d_ref, nm_ref, nv_ref = rest[-3:]
        d_ref[...], nm_ref[...], nv_ref[...] = _adam_step(w_ref[...], g_ref[...], m_ref[...], v_ref[...])

    blk = pl.BlockSpec((tr, cols), lambda i: (i, 0))
    return pl.pallas_call(
        body, name=name, grid=(rows // tr,), in_specs=[blk] * 4 + [ANY] * len(extra), out_specs=[blk] * 3,
        out_shape=[jax.ShapeDtypeStruct(w.shape, F32)] * 3,
        compiler_params=_params("parallel"),
    )(w, g, m, v, *extra)


def _adam_step(w, g, m, v):
    nm = ADAM_B1 * m + (1.0 - ADAM_B1) * g
    nv = ADAM_B2 * v + (1.0 - ADAM_B2) * jnp.square(g)
    m_hat = nm * (1.0 / (1.0 - ADAM_B1 ** ADAM_STEP))
    v_hat = nv * (1.0 / (1.0 - ADAM_B2 ** ADAM_STEP))
    return -ADAM_LR * (m_hat / (jnp.sqrt(v_hat) + ADAM_EPS) + ADAM_WD * w), nm, nv


def _adamw_small(tot, chip, weights, ms, vs, after):
    n, half = len(weights), D_MODEL // 2

    def body(chip_ref, tot_ref, *refs):
        ins, outs = refs[:3 * n], refs[3 * n + 1:]
        tot = tot_ref[...]
        conv_all = jnp.concatenate([tot[5:6, half:], tot[6:7, :half], tot[6:7, half:]], axis=0)
        conv = sum(jnp.where(chip_ref[0] == s, conv_all[:, s * LANES:(s + 1) * LANES], 0.0) for s in range(N_CHIPS))
        grads = [jnp.concatenate([tot[4:5, :half], tot[4:5, half:]], axis=0), tot[5:6, :half], conv,
                 tot[0:1], tot[1:2], tot[2:3], tot[3:4]]
        for k, g in enumerate(grads):
            delta, nm, nv = _adam_step(ins[k][...], g, ins[n + k][...], ins[2 * n + k][...])
            outs[k][...], outs[n + k][...], outs[2 * n + k][...], outs[3 * n + k][...] = g, delta, nm, nv

    whole = lambda a: pl.BlockSpec(a.shape, lambda i, s: (0,) * a.ndim)
    arrays = (*weights, *ms, *vs)
    return pl.pallas_call(
        body, name="adamw_small",
        grid_spec=pltpu.PrefetchScalarGridSpec(
            num_scalar_prefetch=1, grid=(1,), in_specs=[whole(tot)] + [whole(a) for a in arrays] + [ANY],
            out_specs=[whole(a) for a in weights] * 4),
        out_shape=[jax.ShapeDtypeStruct(a.shape, F32) for a in weights] * 4,
    )(chip, tot, *arrays, after)


def kernel(x, w_in, lb_logits, gate_norm_w, conv_w, w_out, ln1_g, ln1_b, w_ff1, w_ff2, ln2_g, ln2_b, loss_target, m_w_in, m_lb_logits, m_gate_norm_w, m_conv_w, m_w_out, m_ln1_g, m_ln1_b, m_w_ff1, m_w_ff2, m_ln2_g, m_ln2_b, v_w_in, v_lb_logits, v_gate_norm_w, v_conv_w, v_w_out, v_ln1_g, v_ln1_b, v_w_ff1, v_w_ff2, v_ln2_g, v_ln2_b):
    xs, tgt = x[0], loss_target[0]
    chip = 2 * lax.axis_index("x") + lax.axis_index("y")
    core = lax.axis_index("c").astype(jnp.int32).reshape(1)
    chip1 = chip.astype(jnp.int32).reshape(1)
    place = jnp.concatenate([chip1, core])

    conv4 = lax.dynamic_update_slice(jnp.zeros((N_CHIPS,) + conv_w.shape[1:], F32), conv_w, (chip, 0, 0))
    ici_in = _gather_w_in_over_ici(_place_shard("place_w_in", w_in[0], chip1, True), conv4)
    rest = (1, 2, 3)
    placed = (_place_shard("place_w_out", w_out[0], chip1, False, after=ici_in.token),
              _place_shard("place_w_ff1", w_ff1[0], chip1, True, after=ici_in.token),
              _place_shard("place_w_ff2", w_ff2[0], chip1, False, after=ici_in.token))
    wb_in, cv4 = ici_in.wait(placed[2])
    relay_in = _relay_over_ici((0,), (wb_in,), placed)
    ici_rest = _gather_over_ici(rest, relay_in.others)
    wb_in, = relay_in.wait(ici_rest.token)
    d2d_in = _gather_over_d2d((0,), (wb_in,))
    wb_in, = d2d_in.wait(d2d_in.token)
    conv_full = cv4.transpose(1, 0, 2).reshape(3, CONV_WIDTH)

    proj, xb = _in_proj(xs, wb_in, ici_rest.token)
    relay_rest = _relay_over_ici(rest, ici_rest.wait(proj))
    o, states = _hgrn_fwd(proj, lb_logits, relay_rest.token)
    d2d_rest = _gather_over_d2d(rest, relay_rest.wait(o))
    cat = _gate_fwd(proj, o, gate_norm_w, conv_full, d2d_rest.token)
    wb_out, wb_ff1, wb_ff2 = d2d_rest.wait(cat)

    (h1b, r, da, dpre2b, dpre1, dpre1b, dcat, g_ln1_g, g_ln1_b, g_ln2_g, g_ln2_b, loss8) = _sublayers(
        cat, xs, tgt, wb_out, wb_ff1, wb_ff2, ln1_g, ln1_b, ln2_g, ln2_b)

    names = ("w_in", "w_out", "w_ff1", "w_ff2")

    def add_halves(kinds, grads, lands):
        return [_add_half("add_half_" + names[k], g, ld, core, COLS_SHARDED[k]) for k, g, ld in zip(kinds, grads, lands)]

    def sum_pieces(kinds, halves, lands, after):
        return [_sum_pieces("sum_pieces_" + names[k], h, ld, place, COLS_SHARDED[k], after)
                for k, h, ld in zip(kinds, halves, lands)]

    early = (1, 2, 3)
    swap = _swap_halves(early, (_dw_out(cat, dpre1b), _dw_ff1(h1b, da), _dw_ff2(r, dpre2b)))
    do, dpg, g_gnw, g_conv = _gate_bwd(dcat, o, proj, gate_norm_w, conv_full, swap.token)
    swapped = swap.wait(do)
    exch = _exchange_pieces(early, add_halves(early, swapped[:3], swapped[3:]))
    dph, g_lbl = _hgrn_bwd(proj, do, states, lb_logits, exch.token)
    g_in_local = _dw_in(xb, dph, dpg, dph)

    late = (0,)
    swap = _swap_halves(late, (g_in_local,))
    grad_x = _in_bwd(dph, dpg, wb_in, dpre1, swap.token)
    exchanged = exch.wait(grad_x)
    pack = jnp.concatenate([
        g_ln1_g, g_ln1_b, g_ln2_g, g_ln2_b,
        jnp.concatenate([g_lbl[0:1], g_lbl[1:2]], axis=1),
        jnp.concatenate([g_gnw, g_conv[0:1]], axis=1),
        jnp.concatenate([g_conv[1:2], g_conv[2:3]], axis=1),
        jnp.concatenate([loss8[0:1], jnp.zeros((1, D_MODEL - LANES), F32)], axis=1)], axis=0)
    swapped = swap.wait(exchanged[0])
    exch = _exchange_pieces(late, add_halves(late, swapped[:1], swapped[1:]), pack)
    join = _join_halves(early, sum_pieces(early, exchanged[:3], exchanged[3:], exch.token))
    g_w_out, g_w_ff1, g_w_ff2 = join.wait(join.token)
    d_ff1, nm_ff1, nv_ff1 = _adamw("adamw_w_ff1", w_ff1[0], g_w_ff1, m_w_ff1[0], v_w_ff1[0])
    d_ff2, nm_ff2, nv_ff2 = _adamw("adamw_w_ff2", w_ff2[0], g_w_ff2, m_w_ff2[0], v_w_ff2[0], d_ff1)
    d_out, nm_out, nv_out = _adamw("adamw_w_out", w_out[0], g_w_out, m_w_out[0], v_w_out[0], d_ff2)
    exchanged = exch.wait(d_out)
    tot = _sum_shared(exchanged[2], exchanged[3], 2 * chip1 + core)
    loss = tot[7, 0]
    join = _join_halves(late, sum_pieces(late, exchanged[:1], exchanged[1:2], tot))
    small = ("lb_logits", "gate_norm_w", "conv_w", "ln1_g", "ln1_b", "ln2_g", "ln2_b")
    small_out = _adamw_small(
        tot, chip1, (lb_logits, gate_norm_w, conv_w[0], ln1_g, ln1_b, ln2_g, ln2_b),
        (m_lb_logits, m_gate_norm_w, m_conv_w[0], m_ln1_g, m_ln1_b, m_ln2_g, m_ln2_b),
        (v_lb_logits, v_gate_norm_w, v_conv_w[0], v_ln1_g, v_ln1_b, v_ln2_g, v_ln2_b), join.token)
    g_w_in, = join.wait(small_out[0])
    d_in, nm_in, nv_in = _adamw("adamw_w_in", w_in[0], g_w_in, m_w_in[0], v_w_in[0])

    def results(n_kind, large):
        out = dict(zip(small, small_out[n_kind * len(small):(n_kind + 1) * len(small)]))
        out["conv_w"] = out["conv_w"][None]
        out.update({name: a[None] for name, a in zip(("w_in", "w_out", "w_ff1", "w_ff2"), large)})
        return [out[name] for name in ("w_in", "lb_logits", "gate_norm_w", "conv_w", "w_out", "ln1_g", "ln1_b",
                                       "w_ff1", "w_ff2", "ln2_g", "ln2_b")]

    return (loss, grad_x[None], *results(0, (g_w_in, g_w_out, g_w_ff1, g_w_ff2)),
            *results(1, (d_in, d_out, d_ff1, d_ff2)), *results(2, (nm_in, nm_out, nm_ff1, nm_ff2)),
            *results(3, (nv_in, nv_out, nv_ff1, nv_ff2)))
```

```python
import jax
import jax.numpy as jnp
from jax import lax
from jax.experimental import pallas as pl
from jax.experimental.pallas import tpu as pltpu

F32 = jnp.float32
BF16 = jnp.bfloat16
MXU_DTYPE = jnp.bfloat16

D_MODEL = 1024
HGRN_WIDTH = 512
HEAD_DIM = 128
N_HEADS = 4
CONV_WIDTH = 512
CHUNK = 64
D_FF = 4096
IN_COLS = 3584
GROUP = 512
N_GROUPS = IN_COLS // GROUP
ALPHA = 2.0 ** 0.25
EPS = 1e-5
N_CHIPS = 4
ADAM_LR, ADAM_B1, ADAM_B2, ADAM_EPS, ADAM_WD, ADAM_STEP = 0.001, 0.9, 0.999, 1e-08, 0.01, 10

LANES = 128
SUBLANES = 8
VMEM_LIMIT = 56 * 1024 * 1024
FF_BLOCK = 1024
N_FF = D_FF // FF_BLOCK

NN = (((1,), (0,)), ((), ()))
NT = (((1,), (1,)), ((), ()))
TN = (((0,), (0,)), ((), ()))
MESH = pl.DeviceIdType.MESH
ANY = pl.BlockSpec(memory_space=pl.ANY)


def _dot(a, b, dims):
    return lax.dot_general(a.astype(MXU_DTYPE), b.astype(MXU_DTYPE), dims, preferred_element_type=F32)


def _dot_exact(ones, v):
    ones = ones.astype(jnp.bfloat16)
    hi = v.astype(jnp.bfloat16)
    rest = v - hi.astype(F32)
    mid = rest.astype(jnp.bfloat16)
    low = (rest - mid.astype(F32)).astype(jnp.bfloat16)
    return sum(lax.dot_general(ones, part, NN, preferred_element_type=F32) for part in (hi, mid, low))


def _params(*sem):
    return pltpu.CompilerParams(dimension_semantics=sem, vmem_limit_bytes=VMEM_LIMIT)


def _resident(shape):
    return pl.BlockSpec(shape, lambda *_: (0,) * len(shape), pipeline_mode=pl.Buffered(1))


def _sigmoid(v):
    return 1.0 / (1.0 + jnp.exp(-v))


def _lower_bound(lbl):
    m = jnp.max(lbl, axis=0, keepdims=True)
    e = jnp.exp(lbl - m)
    s = e / jnp.sum(e, axis=0, keepdims=True)
    return s[0:1, :], s[1:2, :]


def _heads(v):
    return [v[:, h * HEAD_DIM:(h + 1) * HEAD_DIM] for h in range(N_HEADS)]


def _per_head(fn, *arrays):
    return jnp.concatenate([fn(*parts) for parts in zip(*map(_heads, arrays))], axis=1)


def _in_proj(x, w_in, after):
    t = x.shape[0]
    tm = min(t, 512)

    def body(x_ref, w_ref, after_ref, o_ref, xb_ref):
        xb = x_ref[...].astype(xb_ref.dtype)
        xb_ref[...] = xb
        for g in range(N_GROUPS):
            o_ref[g] = _dot(xb, w_ref[:, g * GROUP:(g + 1) * GROUP], NN)

    return pl.pallas_call(
        body, name="in_proj", grid=(t // tm,),
        in_specs=[pl.BlockSpec((tm, D_MODEL), lambda i: (i, 0)), _resident((D_MODEL, IN_COLS)), ANY],
        out_specs=[pl.BlockSpec((N_GROUPS, tm, GROUP), lambda i: (0, i, 0)), pl.BlockSpec((tm, D_MODEL), lambda i: (i, 0))],
        out_shape=[jax.ShapeDtypeStruct((N_GROUPS, t, GROUP), F32), jax.ShapeDtypeStruct((t, D_MODEL), BF16)],
        compiler_params=_params("parallel"),
    )(x, w_in, after)


def _gates(fp, lb):
    sig = _sigmoid(fp)
    f = lb + (1.0 - lb) * sig
    return sig, f, jnp.log(f), 1.0 - f


def _chunk_masks():
    row = lax.broadcasted_iota(jnp.int32, (CHUNK, CHUNK), 0)
    col = lax.broadcasted_iota(jnp.int32, (CHUNK, CHUNK), 1)
    return row >= col, row <= col


def _hgrn_fwd(proj, lb_logits, after):
    t = proj.shape[1]
    tb = min(t, 512)
    ncb = tb // CHUNK

    def body(q_ref, f_ref, v_ref, lbl_ref, after_ref, o_ref, st_ref, s_scr):
        @pl.when(pl.program_id(0) == 0)
        def _():
            s_scr[...] = jnp.zeros_like(s_scr)

        lb, _ = _lower_bound(lbl_ref[...])
        causal, _ = _chunk_masks()

        every = range(ncb)
        rows = [slice(c * CHUNK, (c + 1) * CHUNK) for c in every]
        q, v = [q_ref[r, :] for r in rows], [v_ref[r, :] for r in rows]
        gates = [_gates(f_ref[r, :], lb) for r in rows]
        k = [gt[3] for gt in gates]
        b = [_dot_exact(causal, gt[2]) for gt in gates]
        mid, last = [x[CHUNK // 2:CHUNK // 2 + 1, :] for x in b], [x[CHUNK - 1:CHUNK, :] for x in b]
        qt = [q[c] * jnp.exp(b[c] - mid[c]) for c in every]
        kt = [k[c] * jnp.exp(mid[c] - b[c]) for c in every]
        qi = [q[c] * jnp.exp(b[c]) for c in every]
        ks = [k[c] * jnp.exp(last[c] - b[c]) for c in every]
        dec = [jnp.exp(x) for x in last]
        scores = [[jnp.where(causal, _dot(a, b_, NT), 0.0) for a, b_ in zip(_heads(qt[c]), _heads(kt[c]))] for c in every]
        intra = [[_dot(s, v_h, NN) for s, v_h in zip(scores[c], _heads(v[c]))] for c in every]
        update = [_per_head(lambda v_h, ks_h: _dot(v_h, ks_h, TN), v[c], ks[c]) for c in every]

        st = s_scr[...]
        states = []
        for c in every:
            states.append(st)
            st_ref[c] = st
            st = dec[c] * st + update[c]
        s_scr[...] = st

        o_ref[...] = jnp.concatenate(
            [jnp.concatenate([i_h + _dot(qi_h, st_h, NT) for i_h, qi_h, st_h in
                              zip(intra[c], _heads(qi[c]), _heads(states[c]))], axis=1) for c in every], axis=0)

    grp = lambda g: pl.BlockSpec((None, tb, GROUP), lambda i: (g, i, 0))
    return pl.pallas_call(
        body, name="hgrn_fwd", grid=(t // tb,),
        in_specs=[grp(0), grp(1), grp(2), pl.BlockSpec((2, HGRN_WIDTH), lambda i: (0, 0)), ANY],
        out_specs=[pl.BlockSpec((tb, HGRN_WIDTH), lambda i: (i, 0)),
                   pl.BlockSpec((ncb, HEAD_DIM, HGRN_WIDTH), lambda i: (i, 0, 0))],
        out_shape=[jax.ShapeDtypeStruct((t, HGRN_WIDTH), F32),
                   jax.ShapeDtypeStruct((t // CHUNK, HEAD_DIM, HGRN_WIDTH), F32)],
        scratch_shapes=[pltpu.VMEM((HEAD_DIM, HGRN_WIDTH), F32)],
        compiler_params=_params("arbitrary"),
    )(proj, proj, proj, lb_logits, after)


def _conv_taps(z, halo, zbuf, tb):
    zbuf[0:SUBLANES, :] = halo
    zbuf[SUBLANES:SUBLANES + tb, :] = z
    return zbuf[SUBLANES - 1:SUBLANES - 1 + tb, :], zbuf[SUBLANES - 2:SUBLANES - 2 + tb, :]


def _gate_fwd(proj, o, gate_norm_w, conv_w, after):
    t = proj.shape[1]
    tb = min(t, 512)
    hb = tb // SUBLANES

    def body(o_ref, og_ref, gnw_ref, b_ref, c_ref, u_ref, ch_ref, uh_ref, cw_ref, after_ref, cat_ref, zbuf):
        i = pl.program_id(0)
        og = og_ref[...]
        on = _per_head(lambda o_h: o_h * lax.rsqrt(jnp.mean(o_h * o_h, axis=-1, keepdims=True) + EPS), o_ref[...])
        cat_ref[0] = (on * gnw_ref[...] * (og * _sigmoid(og))).astype(cat_ref.dtype)
        z = c_ref[...] * u_ref[...]
        halo = jnp.where(i > 0, ch_ref[...] * uh_ref[...], 0.0)
        z1, z2 = _conv_taps(z, halo, zbuf, tb)
        cw = cw_ref[...]
        yc = cw[2:3, :] * z + cw[1:2, :] * z1 + cw[0:1, :] * z2
        cat_ref[1] = (b_ref[...] * yc).astype(cat_ref.dtype)

    grp = lambda g: pl.BlockSpec((None, tb, GROUP), lambda i: (g, i, 0))
    prev = lambda g: pl.BlockSpec((None, SUBLANES, GROUP), lambda i: (g, jnp.maximum(i * hb - 1, 0), 0))
    vec = lambda r: pl.BlockSpec((r, GROUP), lambda i: (0, 0))
    return pl.pallas_call(
        body, name="gate_fwd", grid=(t // tb,),
        in_specs=[pl.BlockSpec((tb, GROUP), lambda i: (i, 0)), grp(3), vec(1), grp(4), grp(5), grp(6), prev(5), prev(6),
                  vec(3), ANY],
        out_specs=pl.BlockSpec((2, tb, GROUP), lambda i: (0, i, 0)),
        out_shape=jax.ShapeDtypeStruct((2, t, HGRN_WIDTH), BF16),
        scratch_shapes=[pltpu.VMEM((tb + SUBLANES, GROUP), F32)],
        compiler_params=_params("parallel"),
    )(o, proj, gate_norm_w, proj, proj, proj, proj, proj, conv_w, after)


def _ln_bwd(dy, xhat, rstd, g):
    dxhat = dy * g
    m1 = jnp.mean(dxhat, axis=-1, keepdims=True)
    m2 = jnp.mean(dxhat * xhat, axis=-1, keepdims=True)
    return rstd * (dxhat - m1 - xhat * m2)


def _layer_norm(pre):
    xc = pre - jnp.mean(pre, axis=-1, keepdims=True)
    rstd = lax.rsqrt(jnp.mean(xc * xc, axis=-1, keepdims=True) + EPS)
    return xc * rstd, rstd


def _sublayers(cat, x, target, w_out, w_ff1, w_ff2, g1, b1, g2, b2):
    t = x.shape[0]
    tm = min(t, 256)

    def body(cat_ref, x_ref, tg_ref, wo_ref, w1_ref, w2_ref, g1_ref, b1_ref, g2_ref, b2_ref,
             h1_ref, r_ref, da_ref, dp2b_ref, dp1_ref, dp1b_ref, dcat_ref, dg1_ref, db1_ref, dg2_ref, db2_ref, loss_ref):
        @pl.when(pl.program_id(0) == 0)
        def _():
            for ref in (dg1_ref, db1_ref, dg2_ref, db2_ref, loss_ref):
                ref[...] = jnp.zeros_like(ref)

        mix = _dot(cat_ref[0], wo_ref[0:GROUP, :], NN) + _dot(cat_ref[1], wo_ref[GROUP:2 * GROUP, :], NN)
        xhat1, rstd1 = _layer_norm(ALPHA * x_ref[...] + mix)
        h1 = xhat1 * g1_ref[...] + b1_ref[...]
        h1b = h1.astype(h1_ref.dtype)
        h1_ref[...] = h1b
        mlp = jnp.zeros((tm, D_MODEL), F32)
        for j in range(N_FF):
            cols = slice(j * FF_BLOCK, (j + 1) * FF_BLOCK)
            r = jnp.square(jnp.maximum(_dot(h1b, w1_ref[:, cols], NN), 0.0)).astype(r_ref.dtype)
            r_ref[:, cols] = r
            mlp = mlp + _dot(r, w2_ref[cols, :], NN)
        xhat2, rstd2 = _layer_norm(ALPHA * h1 + mlp)
        err = xhat2 * g2_ref[...] + b2_ref[...] - tg_ref[...]
        loss_ref[...] += 0.5 * jnp.sum(jnp.mean(err * err, axis=-1, keepdims=True))
        dy = err * (1.0 / D_MODEL)
        dg2_ref[...] += jnp.sum(dy * xhat2, axis=0, keepdims=True)
        db2_ref[...] += jnp.sum(dy, axis=0, keepdims=True)
        dp2 = _ln_bwd(dy, xhat2, rstd2, g2_ref[...])
        dp2b = dp2.astype(dp2b_ref.dtype)
        dp2b_ref[...] = dp2b
        back = jnp.zeros((tm, D_MODEL), F32)
        for j in range(N_FF):
            cols = slice(j * FF_BLOCK, (j + 1) * FF_BLOCK)
            dr = _dot(dp2b, w2_ref[cols, :], NT)
            da = (dr * (2.0 * jnp.sqrt(r_ref[:, cols].astype(F32)))).astype(da_ref.dtype)
            da_ref[:, cols] = da
            back = back + _dot(da, w1_ref[:, cols], NT)
        dh1 = ALPHA * dp2 + back
        dg1_ref[...] += jnp.sum(dh1 * xhat1, axis=0, keepdims=True)
        db1_ref[...] += jnp.sum(dh1, axis=0, keepdims=True)
        dp1 = _ln_bwd(dh1, xhat1, rstd1, g1_ref[...])
        dp1b = dp1.astype(dp1b_ref.dtype)
        dp1_ref[...] = dp1
        dp1b_ref[...] = dp1b
        dcat_ref[...] = _dot(dp1b, wo_ref[...], NT)

    row = pl.BlockSpec((tm, D_MODEL), lambda i: (i, 0))
    wide = pl.BlockSpec((tm, D_FF), lambda i: (i, 0))
    vec = pl.BlockSpec((1, D_MODEL), lambda i: (0, 0))
    narrow = lambda dtype: jax.ShapeDtypeStruct((t, D_MODEL), dtype)
    return pl.pallas_call(
        body, name="sublayers", grid=(t // tm,),
        in_specs=[pl.BlockSpec((2, tm, GROUP), lambda i: (0, i, 0)), row, row, _resident((D_MODEL, D_MODEL)),
                  _resident((D_MODEL, D_FF)), _resident((D_FF, D_MODEL)), vec, vec, vec, vec],
        out_specs=[row, wide, wide, row, row, row, row, vec, vec, vec, vec,
                   pl.BlockSpec((SUBLANES, LANES), lambda i: (0, 0))],
        out_shape=[narrow(BF16), jax.ShapeDtypeStruct((t, D_FF), BF16), jax.ShapeDtypeStruct((t, D_FF), BF16),
                   narrow(BF16), narrow(F32), narrow(BF16), narrow(F32)]
                  + [jax.ShapeDtypeStruct((1, D_MODEL), F32)] * 4 + [jax.ShapeDtypeStruct((SUBLANES, LANES), F32)],
        compiler_params=_params("arbitrary"),
    )(cat, x, target, w_out, w_ff1, w_ff2, g1, b1, g2, b2)


def _gate_bwd(dcat, o, proj, gate_norm_w, conv_w, after):
    t = proj.shape[1]
    tb = min(t, 512)
    hb = tb // SUBLANES
    nblk = t // tb

    def body(do2_ref, dy_ref, dyn_ref, o_ref, og_ref, gnw_ref, b_ref, bn_ref, c_ref, u_ref, ch_ref, uh_ref, cw_ref,
             after_ref, do_ref, dp_ref, dgnw_ref, dcw_ref, zbuf, dbuf):
        i = pl.program_id(0)

        @pl.when(i == 0)
        def _():
            dgnw_ref[...] = jnp.zeros_like(dgnw_ref)
            dcw_ref[...] = jnp.zeros_like(dcw_ref)

        ov, og, gnw, do2 = o_ref[...], og_ref[...], gnw_ref[...], do2_ref[...]
        rs = _per_head(lambda o_h: jnp.broadcast_to(lax.rsqrt(jnp.mean(o_h * o_h, axis=-1, keepdims=True) + EPS),
                                                    o_h.shape), ov)
        on = ov * rs
        sg = _sigmoid(og)
        sil = og * sg
        don = do2 * gnw * sil
        dgnw_ref[...] += jnp.sum(do2 * on * sil, axis=0, keepdims=True)
        dp_ref[0] = (do2 * on * gnw * (sg * (1.0 + og * (1.0 - sg)))).astype(dp_ref.dtype)
        do_ref[...] = rs * (don - on * _per_head(
            lambda p_h: jnp.broadcast_to(jnp.mean(p_h, axis=-1, keepdims=True), p_h.shape), don * on))

        bg, cg, u, dy = b_ref[...], c_ref[...], u_ref[...], dy_ref[...]
        z = cg * u
        halo = jnp.where(i > 0, ch_ref[...] * uh_ref[...], 0.0)
        z1, z2 = _conv_taps(z, halo, zbuf, tb)
        cw = cw_ref[...]
        yc = cw[2:3, :] * z + cw[1:2, :] * z1 + cw[0:1, :] * z2
        dyc = dy * bg
        dbuf[0:tb, :] = dyc
        dbuf[tb:tb + SUBLANES, :] = jnp.where(i < nblk - 1, dyn_ref[...] * bn_ref[...], 0.0)
        d1, d2 = dbuf[1:1 + tb, :], dbuf[2:2 + tb, :]
        dz = cw[2:3, :] * dyc + cw[1:2, :] * d1 + cw[0:1, :] * d2
        dp_ref[1] = (dy * yc).astype(dp_ref.dtype)
        dp_ref[2] = (dz * u).astype(dp_ref.dtype)
        dp_ref[3] = (dz * cg).astype(dp_ref.dtype)
        dcw_ref[0:1, :] += jnp.sum(dyc * z2, axis=0, keepdims=True)
        dcw_ref[1:2, :] += jnp.sum(dyc * z1, axis=0, keepdims=True)
        dcw_ref[2:3, :] += jnp.sum(dyc * z, axis=0, keepdims=True)

    half = lambda g: pl.BlockSpec((tb, GROUP), lambda i: (i, g))
    grp = lambda g: pl.BlockSpec((None, tb, GROUP), lambda i: (g, i, 0))
    prev = lambda g: pl.BlockSpec((None, SUBLANES, GROUP), lambda i: (g, jnp.maximum(i * hb - 1, 0), 0))
    nxt_row = lambda i: jnp.minimum((i + 1) * hb, t // SUBLANES - 1)
    nxt = lambda g: pl.BlockSpec((None, SUBLANES, GROUP), lambda i: (g, nxt_row(i), 0))
    vec = lambda r: pl.BlockSpec((r, GROUP), lambda i: (0, 0))
    return pl.pallas_call(
        body, name="gate_bwd", grid=(nblk,),
        in_specs=[half(0), half(1), pl.BlockSpec((SUBLANES, GROUP), lambda i: (nxt_row(i), 1)), half(0), grp(3), vec(1),
                  grp(4), nxt(4), grp(5), grp(6), prev(5), prev(6), vec(3), ANY],
        out_specs=[half(0), pl.BlockSpec((4, tb, GROUP), lambda i: (0, i, 0)), vec(1), vec(3)],
        out_shape=[jax.ShapeDtypeStruct((t, HGRN_WIDTH), F32), jax.ShapeDtypeStruct((4, t, HGRN_WIDTH), BF16),
                   jax.ShapeDtypeStruct((1, HGRN_WIDTH), F32), jax.ShapeDtypeStruct((3, CONV_WIDTH), F32)],
        scratch_shapes=[pltpu.VMEM((tb + SUBLANES, GROUP), F32), pltpu.VMEM((tb + SUBLANES, GROUP), F32)],
        compiler_params=_params("arbitrary"),
    )(dcat, dcat, dcat, o, proj, gate_norm_w, proj, proj, proj, proj, proj, proj, conv_w, after)


def _hgrn_bwd(proj, do, states, lb_logits, after):
    t = proj.shape[1]
    tb = min(t, 512)
    ncb = tb // CHUNK
    nblk = t // tb

    def body(q_ref, f_ref, v_ref, do_ref, st_ref, lbl_ref, after_ref, dp_ref, dlbl_ref, ds_scr, dlb_scr):
        i = pl.program_id(0)

        @pl.when(i == 0)
        def _():
            ds_scr[...] = jnp.zeros_like(ds_scr)
            dlb_scr[...] = jnp.zeros_like(dlb_scr)

        lb, s1 = _lower_bound(lbl_ref[...])
        causal, anti = _chunk_masks()
        every = range(ncb)
        rows = [slice(c * CHUNK, (c + 1) * CHUNK) for c in every]
        q, v, do = ([ref[r, :] for r in rows] for ref in (q_ref, v_ref, do_ref))
        st = [st_ref[c] for c in every]
        gates = [_gates(f_ref[r, :], lb) for r in rows]
        sig, f, k = ([gt[n] for gt in gates] for n in (0, 1, 3))
        b = [_dot_exact(causal, gt[2]) for gt in gates]
        mid, last = [x[CHUNK // 2:CHUNK // 2 + 1, :] for x in b], [x[CHUNK - 1:CHUNK, :] for x in b]
        e_q = [jnp.exp(b[c] - mid[c]) for c in every]
        e_k = [jnp.exp(mid[c] - b[c]) for c in every]
        e_i = [jnp.exp(x) for x in b]
        e_s = [jnp.exp(last[c] - b[c]) for c in every]
        dec = [jnp.exp(x) for x in last]
        qt, kt, qi, ks = ([a[c] * e[c] for c in every] for a, e in ((q, e_q), (k, e_k), (q, e_i), (k, e_s)))

        def masked(a, b_):
            return [[jnp.where(causal, _dot(a_h, b_h, NT), 0.0) for a_h, b_h in zip(_heads(a[c]), _heads(b_[c]))]
                    for c in every]

        def with_scores(s, other, dims):
            return [jnp.concatenate([_dot(s_h, o_h, dims) for s_h, o_h in zip(s[c], _heads(other[c]))], axis=1)
                    for c in every]

        def per_head(dims, a, b_):
            return [_per_head(lambda a_h, b_h: _dot(a_h, b_h, dims), a[c], b_[c]) for c in every]

        scores, dscores = masked(qt, kt), masked(do, v)
        dqt, dkt, dv_intra = with_scores(dscores, kt, NN), with_scores(dscores, qt, TN), with_scores(scores, do, TN)
        dqi, update = per_head(NN, do, st), per_head(TN, do, qi)

        dst = ds_scr[...]
        dsts = [None] * ncb
        for c in reversed(every):
            dsts[c] = dst
            dst = dec[c] * dst + update[c]
        ds_scr[...] = dst

        dv_state, dks = per_head(NT, ks, dsts), per_head(NN, v, dsts)
        ddec = [jnp.sum(dsts[c] * st[c], axis=0, keepdims=True) for c in every]
        dq = [dqt[c] * e_q[c] + dqi[c] * e_i[c] for c in every]
        dk = [dkt[c] * e_k[c] + dks[c] * e_s[c] for c in every]
        db = [q[c] * dq[c] - k[c] * dk[c] for c in every]
        db_last = [jnp.sum(dks[c] * ks[c], axis=0, keepdims=True) + ddec[c] * dec[c] for c in every]
        dg = [_dot_exact(anti, db[c]) + db_last[c] for c in every]
        df = [dg[c] / f[c] - dk[c] for c in every]
        dlb_scr[...] += sum(jnp.sum(df[c] * (1.0 - sig[c]), axis=0, keepdims=True) for c in every)
        dfp = [df[c] * (1.0 - lb) * sig[c] * (1.0 - sig[c]) for c in every]
        dv = [dv_intra[c] + dv_state[c] for c in every]
        for n, parts in enumerate((dq, dfp, dv)):
            dp_ref[n] = jnp.concatenate(parts, axis=0).astype(dp_ref.dtype)

        @pl.when(i == nblk - 1)
        def _():
            dlb = dlb_scr[...]
            dlbl_ref[0:1, :] = dlb * lb * (1.0 - lb)
            dlbl_ref[1:2, :] = -dlb * lb * s1

    grp = lambda g: pl.BlockSpec((None, tb, GROUP), lambda i: (g, nblk - 1 - i, 0))
    vec = pl.BlockSpec((2, HGRN_WIDTH), lambda i: (0, 0))
    return pl.pallas_call(
        body, name="hgrn_bwd", grid=(nblk,),
        in_specs=[grp(0), grp(1), grp(2), pl.BlockSpec((tb, HGRN_WIDTH), lambda i: (nblk - 1 - i, 0)),
                  pl.BlockSpec((ncb, HEAD_DIM, HGRN_WIDTH), lambda i: (nblk - 1 - i, 0, 0)), vec, ANY],
        out_specs=[pl.BlockSpec((3, tb, HGRN_WIDTH), lambda i: (0, nblk - 1 - i, 0)), vec],
        out_shape=[jax.ShapeDtypeStruct((3, t, HGRN_WIDTH), BF16), jax.ShapeDtypeStruct((2, HGRN_WIDTH), F32)],
        scratch_shapes=[pltpu.VMEM((HEAD_DIM, HGRN_WIDTH), F32), pltpu.VMEM((1, HGRN_WIDTH), F32)],
        compiler_params=_params("arbitrary"),
    )(proj, proj, proj, do, states, lb_logits, after)


def _in_bwd(dph, dpg, w_in, dpre1, after):
    t = dpre1.shape[0]
    tm = min(t, 512)

    def body(dh_ref, dg_ref, w_ref, dp_ref, after_ref, o_ref):
        acc = ALPHA * dp_ref[...]
        for g in range(N_GROUPS):
            part = dh_ref[g] if g < 3 else dg_ref[g - 3]
            acc = acc + _dot(part, w_ref[:, g * GROUP:(g + 1) * GROUP], NT)
        o_ref[...] = acc

    row = pl.BlockSpec((tm, D_MODEL), lambda i: (i, 0))
    return pl.pallas_call(
        body, name="in_bwd", grid=(t // tm,),
        in_specs=[pl.BlockSpec((3, tm, GROUP), lambda i: (0, i, 0)), pl.BlockSpec((4, tm, GROUP), lambda i: (0, i, 0)),
                  _resident((D_MODEL, IN_COLS)), row, ANY],
        out_specs=row,
        out_shape=jax.ShapeDtypeStruct((t, D_MODEL), F32),
        compiler_params=_params("parallel"),
    )(dph, dpg, w_in, dpre1, after)


def _grad_w(name, operands, widths, shape, step, after=None):
    t = operands[0].shape[-2]
    tt = min(t, 512)
    n_in, n_steps = len(operands), t // tt
    in_specs = [pl.BlockSpec((tt, w), lambda k: (k, 0)) if a.ndim == 2 else
                pl.BlockSpec((a.shape[0], tt, w), lambda k: (0, k, 0)) for a, w in zip(operands, widths)]
    extra = [] if after is None else [after]

    def body(*refs):
        o_ref, acc, narrow, sem = refs[-4:]
        k = pl.program_id(0)

        @pl.when(k == 0)
        def _():
            acc[...] = jnp.zeros_like(acc)

        step(acc, *refs[:n_in])

        @pl.when(k == n_steps - 1)
        def _():
            narrow[...] = acc[...].astype(narrow.dtype)
            out = pltpu.make_async_copy(narrow, o_ref, sem)
            out.start()
            out.wait()

    return pl.pallas_call(
        body, name=name, grid=(n_steps,), in_specs=in_specs + [ANY] * len(extra), out_specs=ANY,
        out_shape=jax.ShapeDtypeStruct(shape, BF16),
        scratch_shapes=[pltpu.VMEM(shape, F32), pltpu.VMEM(shape, BF16), pltpu.SemaphoreType.DMA],
        compiler_params=_params("arbitrary"),
    )(*operands, *extra)


def _dw_in(xb, dph, dpg, after):
    def step(acc, x_ref, dh_ref, dg_ref):
        xv = x_ref[...]
        for g in range(N_GROUPS):
            part = dh_ref[g] if g < 3 else dg_ref[g - 3]
            acc[:, g * GROUP:(g + 1) * GROUP] += _dot(xv, part, TN)

    return _grad_w("dw_in", (xb, dph, dpg), (D_MODEL, GROUP, GROUP), (D_MODEL, IN_COLS), step, after)


def _dw_out(cat, dpre1b):
    def step(acc, cat_ref, d_ref):
        dv = d_ref[...]
        for g in range(2):
            acc[g * GROUP:(g + 1) * GROUP, :] += _dot(cat_ref[g], dv, TN)

    return _grad_w("dw_out", (cat, dpre1b), (GROUP, D_MODEL), (D_MODEL, D_MODEL), step)


def _dw_ff1(h1b, da):
    def step(acc, h_ref, da_ref):
        hv = h_ref[...]
        for j in range(D_FF // FF_BLOCK):
            cols = slice(j * FF_BLOCK, (j + 1) * FF_BLOCK)
            acc[:, cols] += _dot(hv, da_ref[:, cols], TN)

    return _grad_w("dw_ff1", (h1b, da), (D_MODEL, D_FF), (D_MODEL, D_FF), step)


def _dw_ff2(r, dpre2b):
    def step(acc, r_ref, d_ref):
        dv = d_ref[...]
        for j in range(D_FF // FF_BLOCK):
            rows = slice(j * FF_BLOCK, (j + 1) * FF_BLOCK)
            acc[rows, :] += _dot(r_ref[:, rows], dv, TN)

    return _grad_w("dw_ff2", (r, dpre2b), (D_FF, D_MODEL), (D_FF, D_MODEL), step)


def _place():
    x, y, c = lax.axis_index("x"), lax.axis_index("y"), lax.axis_index("c")
    return x, y, c, 2 * x + y


def _other_chips(x, y):
    return [(1 - x, y), (x, 1 - y), (1 - x, 1 - y)]


def _place_shard(name, w, chip, cols_sharded, after=None):
    rows, cols = w.shape
    tr = min(rows, 256)
    nb = rows // tr
    full = (rows, cols * N_CHIPS) if cols_sharded else (rows * N_CHIPS, cols)
    out_map = (lambda i, s: (i, s[0])) if cols_sharded else (lambda i, s: (s[0] * nb + i, 0))

    def body(s_ref, w_ref, *rest):
        rest[-1][...] = w_ref[...].astype(rest[-1].dtype)

    extra = [] if after is None else [after]
    return pl.pallas_call(
        body, name=name,
        grid_spec=pltpu.PrefetchScalarGridSpec(
            num_scalar_prefetch=1, grid=(nb,),
            in_specs=[pl.BlockSpec((tr, cols), lambda i, s: (i, 0))] + [ANY] * len(extra),
            out_specs=pl.BlockSpec((tr, cols), out_map)),
        out_shape=jax.ShapeDtypeStruct(full, BF16),
        compiler_params=_params("parallel"),
    )(chip, w, *extra)


HBM = pl.BlockSpec(memory_space=pltpu.HBM)
SEM = pl.BlockSpec(memory_space=pltpu.SEMAPHORE)
EFFECT = pltpu.SideEffectType.DATAFLOW_SIDE_EFFECTING


SIBLING_BARRIER = 0


class _Split:
    def __init__(self, name, arrays, plan, others=(), sibling_only=False):
        n_own, arrays = len(arrays), (*arrays, *others)
        n, n_copies = len(arrays), plan.count
        self.name, self.plan, self.n = name, plan, n_own

        def body(*refs):
            if sibling_only:
                x, y, c, _ = _place()
                barrier = pltpu.get_barrier_semaphore()
                pl.semaphore_signal(barrier, inc=1, device_id=(x, y, 1 - c), device_id_type=MESH)
                pl.semaphore_wait(barrier, 1)
            send_sems, recv_sems, token = refs[n], refs[n + 1], refs[-1]
            for k, (src, dst, to) in enumerate(plan(refs[:n])):
                pltpu.make_async_remote_copy(src_ref=src, dst_ref=dst, send_sem=send_sems.at[k], recv_sem=recv_sems.at[k],
                                             device_id=to, device_id_type=MESH).start()
            token[...] = jnp.zeros_like(token)

        outs = pl.pallas_call(
            body, name=name + "_start",
            out_shape=(pltpu.SemaphoreType.DMA((n_copies,)), pltpu.SemaphoreType.DMA((n_copies,)),
                       *[pltpu.HBM(a.shape, a.dtype) for a in arrays], jax.ShapeDtypeStruct((SUBLANES, LANES), F32)),
            in_specs=(HBM,) * n, out_specs=(SEM, SEM) + (HBM,) * n + (pl.BlockSpec(memory_space=pltpu.VMEM),),
            input_output_aliases={i: 2 + i for i in range(n)},
            compiler_params=pltpu.CompilerParams(has_side_effects=EFFECT,
                                                 collective_id=SIBLING_BARRIER if sibling_only else None),
        )(*[pltpu.with_memory_space_constraint(a, pltpu.HBM) for a in arrays])
        self.sems, self.arrays, self.others, self.token = outs[:2], outs[2:2 + n_own], outs[2 + n_own:2 + n], outs[-1]

    def wait(self, after):
        n, plan = self.n, self.plan

        def body(*refs):
            send_sems, recv_sems = refs[n], refs[n + 1]
            for k, (src, dst, to) in enumerate(plan(refs[:n])):
                cp = pltpu.make_async_remote_copy(src_ref=src, dst_ref=dst, send_sem=send_sems.at[k],
                                                  recv_sem=recv_sems.at[k], device_id=to, device_id_type=MESH)
                cp.wait_send()
                cp.wait_recv()

        return pl.pallas_call(
            body, name=self.name + "_wait", out_shape=tuple(pltpu.HBM(a.shape, a.dtype) for a in self.arrays),
            in_specs=(HBM,) * n + (SEM, SEM, ANY), out_specs=(HBM,) * n, input_output_aliases={i: i for i in range(n)},
            compiler_params=pltpu.CompilerParams(has_side_effects=EFFECT),
        )(*self.arrays, *self.sems, after)


COLS_SHARDED = (True, False, True, False)
HALF_SHAPES = [(D_MODEL // 2, IN_COLS), (D_MODEL, D_MODEL // 2), (D_MODEL // 2, D_FF), (D_FF, D_MODEL // 2)]
PIECE_SHAPES = [(D_MODEL // 2, IN_COLS // N_CHIPS), (D_MODEL // N_CHIPS, D_MODEL // 2),
                (D_MODEL // 2, D_FF // N_CHIPS), (D_FF // N_CHIPS, D_MODEL // 2)]


def _shard_view(kind, ref, chip):
    if COLS_SHARDED[kind]:
        n = ref.shape[1] // N_CHIPS
        return ref.at[:, pl.ds(chip * n, n)]
    n = ref.shape[0] // N_CHIPS
    return ref.at[pl.ds(chip * n, n), :]


def _half_view(kind, ref, h):
    if COLS_SHARDED[kind]:
        n = ref.shape[0] // 2
        return ref.at[pl.ds(h * n, n), :]
    n = ref.shape[1] // 2
    return ref.at[:, pl.ds(h * n, n)]


def _plan(count):
    def mark(fn):
        fn.count = count
        return fn
    return mark


def _shard_rows_view(kind, ref, chip, part, n_parts):
    if COLS_SHARDED[kind]:
        m, n = ref.shape[0] // n_parts, ref.shape[1] // N_CHIPS
        return ref.at[pl.ds(part * m, m), pl.ds(chip * n, n)]
    m = ref.shape[0] // N_CHIPS // n_parts
    return ref.at[pl.ds((n_parts * chip + part) * m, m), :]


def _shard_half_view(kind, ref, chip, h):
    return _shard_rows_view(kind, ref, chip, h, 2)


def _gather_over_ici(kinds, weights):
    @_plan(2 * len(kinds))
    def plan(refs):
        x, y, c, me = _place()
        mine = [_shard_half_view(kind, ref, me, c) for kind, ref in zip(kinds, refs)]
        return [(v, v, to) for v in mine for to in ((1 - x, y, c), (x, 1 - y, c))]

    return _Split("gather_ici_" + "".join(map(str, kinds)), tuple(weights), plan)


def _relay_over_ici(kinds, weights, others=()):
    @_plan(2 * len(kinds))
    def plan(refs):
        x, y, c, _ = _place()
        x_nbr, y_nbr = 2 * (1 - x) + y, 2 * x + (1 - y)
        out = []
        for kind, ref in zip(kinds, refs):
            first, second = (_shard_rows_view(kind, ref, chip, 2 * c + q, 4) for q, chip in ((0, x_nbr), (1, y_nbr)))
            out += [(first, first, (x, 1 - y, c)), (second, second, (1 - x, y, c))]
        return out

    return _Split("relay_ici_" + "".join(map(str, kinds)), tuple(weights), plan, others)


def _gather_w_in_over_ici(w_in, conv4):
    @_plan(6)
    def plan(refs):
        x, y, c, me = _place()
        half, conv = _shard_half_view(0, refs[0], me, c), refs[1].at[me]
        return [(v, v, (px, py, c)) for v in (half, conv) for px, py in _other_chips(x, y)]

    return _Split("gather_w_in_ici", (w_in, conv4), plan)


def _gather_over_d2d(kinds, weights):
    @_plan(3 * len(kinds))
    def plan(refs):
        x, y, c, _ = _place()
        got = [_shard_half_view(kind, ref, 2 * px + py, c) for kind, ref in zip(kinds, refs)
               for px, py in _other_chips(x, y)]
        return [(v, v, (x, y, 1 - c)) for v in got]

    return _Split("gather_d2d_" + "".join(map(str, kinds)), tuple(weights), plan, sibling_only=True)


def _swap_halves(kinds, grads):
    @_plan(len(kinds))
    def plan(refs):
        x, y, c, _ = _place()
        return [(_half_view(kind, g, 1 - c), land, (x, y, 1 - c))
                for kind, g, land in zip(kinds, refs[:len(kinds)], refs[len(kinds):])]

    lands = [lax.empty(HALF_SHAPES[kind], g.dtype) for kind, g in zip(kinds, grads)]
    return _Split("swap_halves_" + "".join(map(str, kinds)), (*grads, *lands), plan, sibling_only=True)


def _add_half(name, g, recv, core, rows_split):
    shape = recv.shape
    tr = min(shape[0], 128 if rows_split else 256)
    nb = shape[0] // tr

    def body(c_ref, g_ref, r_ref, o_ref):
        o_ref[...] = (g_ref[...].astype(F32) + r_ref[...].astype(F32)).astype(o_ref.dtype)

    g_map = (lambda i, c_ref: (c_ref[0] * nb + i, 0)) if rows_split else (lambda i, c_ref: (i, c_ref[0]))
    blk = pl.BlockSpec((tr, shape[1]), lambda i, c_ref: (i, 0))
    return pl.pallas_call(
        body, name=name,
        grid_spec=pltpu.PrefetchScalarGridSpec(
            num_scalar_prefetch=1, grid=(nb,),
            in_specs=[pl.BlockSpec((tr, shape[1]), g_map), blk], out_specs=blk),
        out_shape=jax.ShapeDtypeStruct(shape, BF16),
        compiler_params=_params("parallel"),
    )(core, g, recv)


def _exchange_pieces(kinds, halves, pack=None):
    n_p, n = N_CHIPS - 1, len(kinds)

    @_plan(n_p * n + (0 if pack is None else N_DEV - 1))
    def plan(refs):
        x, y, c, _ = _place()
        copies = []
        if pack is not None:
            me = 4 * x + 2 * y + c
            peers = [((1 - x) if m & 4 else x, (1 - y) if m & 2 else y, (1 - c) if m & 1 else c) for m in range(1, N_DEV)]
            copies += [(refs[2 * n], refs[2 * n + 1].at[me], peer) for peer in peers]
        return copies + [(_shard_view(kind, half, 2 * px + py), land.at[j], (px, py, c))
                         for j, (px, py) in enumerate(_other_chips(x, y))
                         for kind, half, land in zip(kinds, refs[:n], refs[n:2 * n])]

    lands = [lax.empty((n_p,) + PIECE_SHAPES[kind], BF16) for kind in kinds]
    small = () if pack is None else (pack, lax.empty((N_DEV,) + pack.shape, F32))
    return _Split("exchange_pieces_" + "".join(map(str, kinds)), (*halves, *lands, *small), plan)


def _sum_pieces(name, half, slots, place, rows_split, after):
    n_p, rows, cols = slots.shape
    tr = min(rows, 256)
    nb = rows // tr
    if rows_split:
        own_map = lambda i, s: (i, s[0])
        out_map = lambda i, s: (s[1] * nb + i, 0)
        shard = (2 * rows, cols)
    else:
        own_map = lambda i, s: (s[0] * nb + i, 0)
        out_map = lambda i, s: (i, s[1])
        shard = (rows, 2 * cols)

    def body(s_ref, own_ref, slot_ref, after_ref, o_ref):
        total = own_ref[...].astype(F32)
        for j in range(n_p):
            total = total + slot_ref[j].astype(F32)
        o_ref[...] = total

    return pl.pallas_call(
        body, name=name,
        grid_spec=pltpu.PrefetchScalarGridSpec(
            num_scalar_prefetch=1, grid=(nb,),
            in_specs=[pl.BlockSpec((tr, cols), own_map), pl.BlockSpec((n_p, tr, cols), lambda i, s: (0, i, 0)), ANY],
            out_specs=pl.BlockSpec((tr, cols), out_map)),
        out_shape=jax.ShapeDtypeStruct(shard, F32),
        compiler_params=_params("parallel"),
    )(place, half, slots, after)


def _join_halves(kinds, shards):
    @_plan(len(kinds))
    def plan(refs):
        x, y, c, _ = _place()
        return [(_half_view(kind, g, c), _half_view(kind, g, c), (x, y, 1 - c)) for kind, g in zip(kinds, refs)]

    return _Split("join_halves_" + "".join(map(str, kinds)), tuple(shards), plan, sibling_only=True)


N_DEV = 8


def _sum_shared(pack, land, device):
    def body(d_ref, p_ref, l_ref, o_ref):
        me = d_ref[0]
        total = jnp.where(me == 0, p_ref[...], l_ref[0])
        for d in range(1, N_DEV):
            total = total + jnp.where(me == d, p_ref[...], l_ref[d])
        o_ref[...] = total

    return pl.pallas_call(
        body, name="sum_shared",
        grid_spec=pltpu.PrefetchScalarGridSpec(
            num_scalar_prefetch=1, grid=(1,),
            in_specs=[pl.BlockSpec(pack.shape, lambda i, d: (0, 0)), pl.BlockSpec(land.shape, lambda i, d: (0, 0, 0))],
            out_specs=pl.BlockSpec(pack.shape, lambda i, d: (0, 0))),
        out_shape=jax.ShapeDtypeStruct(pack.shape, F32),
    )(device, pack, land)


def _adamw(name, w, g, m, v, after=None):
    rows, cols = w.shape
    tr = min(rows, 256)
    extra = [] if after is None else [after]

    def body(w_ref, g_ref, m_ref, v_ref, *rest):
        d_ref, nm_ref, nv_ref = rest[-3:]
        d_ref[...], nm_ref[...], nv_ref[...] = _adam_step(w_ref[...], g_ref[...], m_ref[...], v_ref[...])

    blk = pl.BlockSpec((tr, cols), lambda i: (i, 0))
    return pl.pallas_call(
        body, name=name, grid=(rows // tr,), in_specs=[blk] * 4 + [ANY] * len(extra), out_specs=[blk] * 3,
        out_shape=[jax.ShapeDtypeStruct(w.shape, F32)] * 3,
        compiler_params=_params("parallel"),
    )(w, g, m, v, *extra)


def _adam_step(w, g, m, v):
    nm = ADAM_B1 * m + (1.0 - ADAM_B1) * g
    nv = ADAM_B2 * v + (1.0 - ADAM_B2) * jnp.square(g)
    m_hat = nm * (1.0 / (1.0 - ADAM_B1 ** ADAM_STEP))
    v_hat = nv * (1.0 / (1.0 - ADAM_B2 ** ADAM_STEP))
    return -ADAM_LR * (m_hat / (jnp.sqrt(v_hat) + ADAM_EPS) + ADAM_WD * w), nm, nv


def _adamw_small(tot, chip, weights, ms, vs, after):
    n, half = len(weights), D_MODEL // 2

    def body(chip_ref, tot_ref, *refs):
        ins, outs = refs[:3 * n], refs[3 * n + 1:]
        tot = tot_ref[...]
        conv_all = jnp.concatenate([tot[5:6, half:], tot[6:7, :half], tot[6:7, half:]], axis=0)
        conv = sum(jnp.where(chip_ref[0] == s, conv_all[:, s * LANES:(s + 1) * LANES], 0.0) for s in range(N_CHIPS))
        grads = [jnp.concatenate([tot[4:5, :half], tot[4:5, half:]], axis=0), tot[5:6, :half], conv,
                 tot[0:1], tot[1:2], tot[2:3], tot[3:4]]
        for k, g in enumerate(grads):
            delta, nm, nv = _adam_step(ins[k][...], g, ins[n + k][...], ins[2 * n + k][...])
            outs[k][...], outs[n + k][...], outs[2 * n + k][...], outs[3 * n + k][...] = g, delta, nm, nv

    whole = lambda a: pl.BlockSpec(a.shape, lambda i, s: (0,) * a.ndim)
    arrays = (*weights, *ms, *vs)
    return pl.pallas_call(
        body, name="adamw_small",
        grid_spec=pltpu.PrefetchScalarGridSpec(
            num_scalar_prefetch=1, grid=(1,), in_specs=[whole(tot)] + [whole(a) for a in arrays] + [ANY],
            out_specs=[whole(a) for a in weights] * 4),
        out_shape=[jax.ShapeDtypeStruct(a.shape, F32) for a in weights] * 4,
    )(chip, tot, *arrays, after)


def kernel(x, w_in, lb_logits, gate_norm_w, conv_w, w_out, ln1_g, ln1_b, w_ff1, w_ff2, ln2_g, ln2_b, loss_target, m_w_in, m_lb_logits, m_gate_norm_w, m_conv_w, m_w_out, m_ln1_g, m_ln1_b, m_w_ff1, m_w_ff2, m_ln2_g, m_ln2_b, v_w_in, v_lb_logits, v_gate_norm_w, v_conv_w, v_w_out, v_ln1_g, v_ln1_b, v_w_ff1, v_w_ff2, v_ln2_g, v_ln2_b):
    xs, tgt = x[0], loss_target[0]
    chip = 2 * lax.axis_index("x") + lax.axis_index("y")
    core = lax.axis_index("c").astype(jnp.int32).reshape(1)
    chip1 = chip.astype(jnp.int32).reshape(1)
    place = jnp.concatenate([chip1, core])

    conv4 = lax.dynamic_update_slice(jnp.zeros((N_CHIPS,) + conv_w.shape[1:], F32), conv_w, (chip, 0, 0))
    ici_in = _gather_w_in_over_ici(_place_shard("place_w_in", w_in[0], chip1, True), conv4)
    rest = (1, 2, 3)
    ici_rest = _gather_over_ici(rest, (_place_shard("place_w_out", w_out[0], chip1, False, after=ici_in.token),
                                       _place_shard("place_w_ff1", w_ff1[0], chip1, True, after=ici_in.token),
                                       _place_shard("place_w_ff2", w_ff2[0], chip1, False, after=ici_in.token)))
    wb_in, cv4 = ici_in.wait(ici_rest.token)
    d2d_in = _gather_over_d2d((0,), (wb_in,))
    wb_in, = d2d_in.wait(d2d_in.token)
    conv_full = cv4.transpose(1, 0, 2).reshape(3, CONV_WIDTH)

    proj, xb = _in_proj(xs, wb_in, ici_rest.token)
    relay_rest = _relay_over_ici(rest, ici_rest.wait(proj))
    o, states = _hgrn_fwd(proj, lb_logits, relay_rest.token)
    d2d_rest = _gather_over_d2d(rest, relay_rest.wait(o))
    cat = _gate_fwd(proj, o, gate_norm_w, conv_full, d2d_rest.token)
    wb_out, wb_ff1, wb_ff2 = d2d_rest.wait(cat)

    (h1b, r, da, dpre2b, dpre1, dpre1b, dcat, g_ln1_g, g_ln1_b, g_ln2_g, g_ln2_b, loss8) = _sublayers(
        cat, xs, tgt, wb_out, wb_ff1, wb_ff2, ln1_g, ln1_b, ln2_g, ln2_b)

    names = ("w_in", "w_out", "w_ff1", "w_ff2")

    def add_halves(kinds, grads, lands):
        return [_add_half("add_half_" + names[k], g, ld, core, COLS_SHARDED[k]) for k, g, ld in zip(kinds, grads, lands)]

    def sum_pieces(kinds, halves, lands, after):
        return [_sum_pieces("sum_pieces_" + names[k], h, ld, place, COLS_SHARDED[k], after)
                for k, h, ld in zip(kinds, halves, lands)]

    early = (1, 2, 3)
    swap = _swap_halves(early, (_dw_out(cat, dpre1b), _dw_ff1(h1b, da), _dw_ff2(r, dpre2b)))
    do, dpg, g_gnw, g_conv = _gate_bwd(dcat, o, proj, gate_norm_w, conv_full, swap.token)
    swapped = swap.wait(do)
    exch = _exchange_pieces(early, add_halves(early, swapped[:3], swapped[3:]))
    dph, g_lbl = _hgrn_bwd(proj, do, states, lb_logits, exch.token)
    g_in_local = _dw_in(xb, dph, dpg, dph)

    late = (0,)
    swap = _swap_halves(late, (g_in_local,))
    grad_x = _in_bwd(dph, dpg, wb_in, dpre1, swap.token)
    exchanged = exch.wait(grad_x)
    pack = jnp.concatenate([
        g_ln1_g, g_ln1_b, g_ln2_g, g_ln2_b,
        jnp.concatenate([g_lbl[0:1], g_lbl[1:2]], axis=1),
        jnp.concatenate([g_gnw, g_conv[0:1]], axis=1),
        jnp.concatenate([g_conv[1:2], g_conv[2:3]], axis=1),
        jnp.concatenate([loss8[0:1], jnp.zeros((1, D_MODEL - LANES), F32)], axis=1)], axis=0)
    swapped = swap.wait(exchanged[0])
    exch = _exchange_pieces(late, add_halves(late, swapped[:1], swapped[1:]), pack)
    join = _join_halves(early, sum_pieces(early, exchanged[:3], exchanged[3:], exch.token))
    g_w_out, g_w_ff1, g_w_ff2 = join.wait(join.token)
    d_ff1, nm_ff1, nv_ff1 = _adamw("adamw_w_ff1", w_ff1[0], g_w_ff1, m_w_ff1[0], v_w_ff1[0])
    d_ff2, nm_ff2, nv_ff2 = _adamw("adamw_w_ff2", w_ff2[0], g_w_ff2, m_w_ff2[0], v_w_ff2[0], d_ff1)
    d_out, nm_out, nv_out = _adamw("adamw_w_out", w_out[0], g_w_out, m_w_out[0], v_w_out[0], d_ff2)
    exchanged = exch.wait(d_out)
    tot = _sum_shared(exchanged[2], exchanged[3], 2 * chip1 + core)
    loss = tot[7, 0]
    join = _join_halves(late, sum_pieces(late, exchanged[:1], exchanged[1:2], tot))
    small = ("lb_logits", "gate_norm_w", "conv_w", "ln1_g", "ln1_b", "ln2_g", "ln2_b")
    small_out = _adamw_small(
        tot, chip1, (lb_logits, gate_norm_w, conv_w[0], ln1_g, ln1_b, ln2_g, ln2_b),
        (m_lb_logits, m_gate_norm_w, m_conv_w[0], m_ln1_g, m_ln1_b, m_ln2_g, m_ln2_b),
        (v_lb_logits, v_gate_norm_w, v_conv_w[0], v_ln1_g, v_ln1_b, v_ln2_g, v_ln2_b), join.token)
    g_w_in, = join.wait(small_out[0])
    d_in, nm_in, nv_in = _adamw("adamw_w_in", w_in[0], g_w_in, m_w_in[0], v_w_in[0])

    def results(n_kind, large):
        out = dict(zip(small, small_out[n_kind * len(small):(n_kind + 1) * len(small)]))
        out["conv_w"] = out["conv_w"][None]
        out.update({name: a[None] for name, a in zip(("w_in", "w_out", "w_ff1", "w_ff2"), large)})
        return [out[name] for name in ("w_in", "lb_logits", "gate_norm_w", "conv_w", "w_out", "ln1_g", "ln1_b",
                                       "w_ff1", "w_ff2", "ln2_g", "ln2_b")]

    return (loss, grad_x[None], *results(0, (g_w_in, g_w_out, g_w_ff1, g_w_ff2)),
            *results(1, (d_in, d_out, d_ff1, d_ff2)), *results(2, (nm_in, nm_out, nm_ff1, nm_ff2)),
            *results(3, (nv_in, nv_out, nv_ff1, nv_ff2)))
```

```python
import jax
import jax.numpy as jnp
from jax import lax
from jax.experimental import pallas as pl
from jax.experimental.pallas import tpu as pltpu

F32 = jnp.float32
BF16 = jnp.bfloat16
MXU_DTYPE = jnp.bfloat16

D_MODEL = 1024
HGRN_WIDTH = 512
HEAD_DIM = 128
N_HEADS = 4
CONV_WIDTH = 512
CHUNK = 64
D_FF = 4096
IN_COLS = 3584
GROUP = 512
N_GROUPS = IN_COLS // GROUP
ALPHA = 2.0 ** 0.25
EPS = 1e-5
N_CHIPS = 4
ADAM_LR, ADAM_B1, ADAM_B2, ADAM_EPS, ADAM_WD, ADAM_STEP = 0.001, 0.9, 0.999, 1e-08, 0.01, 10

LANES = 128
SUBLANES = 8
VMEM_LIMIT = 56 * 1024 * 1024
FF_BLOCK = 1024
N_FF = D_FF // FF_BLOCK

NN = (((1,), (0,)), ((), ()))
NT = (((1,), (1,)), ((), ()))
TN = (((0,), (0,)), ((), ()))
MESH = pl.DeviceIdType.MESH
ANY = pl.BlockSpec(memory_space=pl.ANY)


def _dot(a, b, dims):
    return lax.dot_general(a.astype(MXU_DTYPE), b.astype(MXU_DTYPE), dims, preferred_element_type=F32)


def _dot_exact(ones, v):
    ones = ones.astype(jnp.bfloat16)
    hi = v.astype(jnp.bfloat16)
    rest = v - hi.astype(F32)
    mid = rest.astype(jnp.bfloat16)
    low = (rest - mid.astype(F32)).astype(jnp.bfloat16)
    return sum(lax.dot_general(ones, part, NN, preferred_element_type=F32) for part in (hi, mid, low))


def _params(*sem):
    return pltpu.CompilerParams(dimension_semantics=sem, vmem_limit_bytes=VMEM_LIMIT)


def _resident(shape):
    return pl.BlockSpec(shape, lambda *_: (0,) * len(shape), pipeline_mode=pl.Buffered(1))


def _sigmoid(v):
    return 1.0 / (1.0 + jnp.exp(-v))


def _lower_bound(lbl):
    m = jnp.max(lbl, axis=0, keepdims=True)
    e = jnp.exp(lbl - m)
    s = e / jnp.sum(e, axis=0, keepdims=True)
    return s[0:1, :], s[1:2, :]


def _heads(v):
    return [v[:, h * HEAD_DIM:(h + 1) * HEAD_DIM] for h in range(N_HEADS)]


def _per_head(fn, *arrays):
    return jnp.concatenate([fn(*parts) for parts in zip(*map(_heads, arrays))], axis=1)


def _in_proj(x, w_in, after):
    t = x.shape[0]
    tm = min(t, 512)

    def body(x_ref, w_ref, after_ref, o_ref, xb_ref):
        xb = x_ref[...].astype(xb_ref.dtype)
        xb_ref[...] = xb
        for g in range(N_GROUPS):
            o_ref[g] = _dot(xb, w_ref[:, g * GROUP:(g + 1) * GROUP], NN)

    return pl.pallas_call(
        body, name="in_proj", grid=(t // tm,),
        in_specs=[pl.BlockSpec((tm, D_MODEL), lambda i: (i, 0)), _resident((D_MODEL, IN_COLS)), ANY],
        out_specs=[pl.BlockSpec((N_GROUPS, tm, GROUP), lambda i: (0, i, 0)), pl.BlockSpec((tm, D_MODEL), lambda i: (i, 0))],
        out_shape=[jax.ShapeDtypeStruct((N_GROUPS, t, GROUP), F32), jax.ShapeDtypeStruct((t, D_MODEL), BF16)],
        compiler_params=_params("parallel"),
    )(x, w_in, after)


def _gates(fp, lb):
    sig = _sigmoid(fp)
    f = lb + (1.0 - lb) * sig
    return sig, f, jnp.log(f), 1.0 - f


def _chunk_masks():
    row = lax.broadcasted_iota(jnp.int32, (CHUNK, CHUNK), 0)
    col = lax.broadcasted_iota(jnp.int32, (CHUNK, CHUNK), 1)
    return row >= col, row <= col


def _hgrn_fwd(proj, lb_logits, after):
    t = proj.shape[1]
    tb = min(t, 512)
    ncb = tb // CHUNK

    def body(q_ref, f_ref, v_ref, lbl_ref, after_ref, o_ref, st_ref, s_scr):
        @pl.when(pl.program_id(0) == 0)
        def _():
            s_scr[...] = jnp.zeros_like(s_scr)

        lb, _ = _lower_bound(lbl_ref[...])
        causal, _ = _chunk_masks()

        every = range(ncb)
        rows = [slice(c * CHUNK, (c + 1) * CHUNK) for c in every]
        q, v = [q_ref[r, :] for r in rows], [v_ref[r, :] for r in rows]
        gates = [_gates(f_ref[r, :], lb) for r in rows]
        k = [gt[3] for gt in gates]
        b = [_dot_exact(causal, gt[2]) for gt in gates]
        mid, last = [x[CHUNK // 2:CHUNK // 2 + 1, :] for x in b], [x[CHUNK - 1:CHUNK, :] for x in b]
        qt = [q[c] * jnp.exp(b[c] - mid[c]) for c in every]
        kt = [k[c] * jnp.exp(mid[c] - b[c]) for c in every]
        qi = [q[c] * jnp.exp(b[c]) for c in every]
        ks = [k[c] * jnp.exp(last[c] - b[c]) for c in every]
        dec = [jnp.exp(x) for x in last]
        scores = [[jnp.where(causal, _dot(a, b_, NT), 0.0) for a, b_ in zip(_heads(qt[c]), _heads(kt[c]))] for c in every]
        intra = [[_dot(s, v_h, NN) for s, v_h in zip(scores[c], _heads(v[c]))] for c in every]
        update = [_per_head(lambda v_h, ks_h: _dot(v_h, ks_h, TN), v[c], ks[c]) for c in every]

        st = s_scr[...]
        states = []
        for c in every:
            states.append(st)
            st_ref[c] = st
            st = dec[c] * st + update[c]
        s_scr[...] = st

        o_ref[...] = jnp.concatenate(
            [jnp.concatenate([i_h + _dot(qi_h, st_h, NT) for i_h, qi_h, st_h in
                              zip(intra[c], _heads(qi[c]), _heads(states[c]))], axis=1) for c in every], axis=0)

    grp = lambda g: pl.BlockSpec((None, tb, GROUP), lambda i: (g, i, 0))
    return pl.pallas_call(
        body, name="hgrn_fwd", grid=(t // tb,),
        in_specs=[grp(0), grp(1), grp(2), pl.BlockSpec((2, HGRN_WIDTH), lambda i: (0, 0)), ANY],
        out_specs=[pl.BlockSpec((tb, HGRN_WIDTH), lambda i: (i, 0)),
                   pl.BlockSpec((ncb, HEAD_DIM, HGRN_WIDTH), lambda i: (i, 0, 0))],
        out_shape=[jax.ShapeDtypeStruct((t, HGRN_WIDTH), F32),
                   jax.ShapeDtypeStruct((t // CHUNK, HEAD_DIM, HGRN_WIDTH), F32)],
        scratch_shapes=[pltpu.VMEM((HEAD_DIM, HGRN_WIDTH), F32)],
        compiler_params=_params("arbitrary"),
    )(proj, proj, proj, lb_logits, after)


def _conv_taps(z, halo, zbuf, tb):
    zbuf[0:SUBLANES, :] = halo
    zbuf[SUBLANES:SUBLANES + tb, :] = z
    return zbuf[SUBLANES - 1:SUBLANES - 1 + tb, :], zbuf[SUBLANES - 2:SUBLANES - 2 + tb, :]


def _gate_fwd(proj, o, gate_norm_w, conv_w, after):
    t = proj.shape[1]
    tb = min(t, 512)
    hb = tb // SUBLANES

    def body(o_ref, og_ref, gnw_ref, b_ref, c_ref, u_ref, ch_ref, uh_ref, cw_ref, after_ref, cat_ref, zbuf):
        i = pl.program_id(0)
        og = og_ref[...]
        on = _per_head(lambda o_h: o_h * lax.rsqrt(jnp.mean(o_h * o_h, axis=-1, keepdims=True) + EPS), o_ref[...])
        cat_ref[0] = (on * gnw_ref[...] * (og * _sigmoid(og))).astype(cat_ref.dtype)
        z = c_ref[...] * u_ref[...]
        halo = jnp.where(i > 0, ch_ref[...] * uh_ref[...], 0.0)
        z1, z2 = _conv_taps(z, halo, zbuf, tb)
        cw = cw_ref[...]
        yc = cw[2:3, :] * z + cw[1:2, :] * z1 + cw[0:1, :] * z2
        cat_ref[1] = (b_ref[...] * yc).astype(cat_ref.dtype)

    grp = lambda g: pl.BlockSpec((None, tb, GROUP), lambda i: (g, i, 0))
    prev = lambda g: pl.BlockSpec((None, SUBLANES, GROUP), lambda i: (g, jnp.maximum(i * hb - 1, 0), 0))
    vec = lambda r: pl.BlockSpec((r, GROUP), lambda i: (0, 0))
    return pl.pallas_call(
        body, name="gate_fwd", grid=(t // tb,),
        in_specs=[pl.BlockSpec((tb, GROUP), lambda i: (i, 0)), grp(3), vec(1), grp(4), grp(5), grp(6), prev(5), prev(6),
                  vec(3), ANY],
        out_specs=pl.BlockSpec((2, tb, GROUP), lambda i: (0, i, 0)),
        out_shape=jax.ShapeDtypeStruct((2, t, HGRN_WIDTH), BF16),
        scratch_shapes=[pltpu.VMEM((tb + SUBLANES, GROUP), F32)],
        compiler_params=_params("parallel"),
    )(o, proj, gate_norm_w, proj, proj, proj, proj, proj, conv_w, after)


def _ln_bwd(dy, xhat, rstd, g):
    dxhat = dy * g
    m1 = jnp.mean(dxhat, axis=-1, keepdims=True)
    m2 = jnp.mean(dxhat * xhat, axis=-1, keepdims=True)
    return rstd * (dxhat - m1 - xhat * m2)


def _layer_norm(pre):
    xc = pre - jnp.mean(pre, axis=-1, keepdims=True)
    rstd = lax.rsqrt(jnp.mean(xc * xc, axis=-1, keepdims=True) + EPS)
    return xc * rstd, rstd


def _sublayers(cat, x, target, w_out, w_ff1, w_ff2, g1, b1, g2, b2):
    t = x.shape[0]
    tm = min(t, 256)

    def body(cat_ref, x_ref, tg_ref, wo_ref, w1_ref, w2_ref, g1_ref, b1_ref, g2_ref, b2_ref,
             h1_ref, r_ref, da_ref, dp2b_ref, dp1_ref, dp1b_ref, dcat_ref, dg1_ref, db1_ref, dg2_ref, db2_ref, loss_ref):
        @pl.when(pl.program_id(0) == 0)
        def _():
            for ref in (dg1_ref, db1_ref, dg2_ref, db2_ref, loss_ref):
                ref[...] = jnp.zeros_like(ref)

        mix = _dot(cat_ref[0], wo_ref[0:GROUP, :], NN) + _dot(cat_ref[1], wo_ref[GROUP:2 * GROUP, :], NN)
        xhat1, rstd1 = _layer_norm(ALPHA * x_ref[...] + mix)
        h1 = xhat1 * g1_ref[...] + b1_ref[...]
        h1b = h1.astype(h1_ref.dtype)
        h1_ref[...] = h1b
        mlp = jnp.zeros((tm, D_MODEL), F32)
        for j in range(N_FF):
            cols = slice(j * FF_BLOCK, (j + 1) * FF_BLOCK)
            r = jnp.square(jnp.maximum(_dot(h1b, w1_ref[:, cols], NN), 0.0)).astype(r_ref.dtype)
            r_ref[:, cols] = r
            mlp = mlp + _dot(r, w2_ref[cols, :], NN)
        xhat2, rstd2 = _layer_norm(ALPHA * h1 + mlp)
        err = xhat2 * g2_ref[...] + b2_ref[...] - tg_ref[...]
        loss_ref[...] += 0.5 * jnp.sum(jnp.mean(err * err, axis=-1, keepdims=True))
        dy = err * (1.0 / D_MODEL)
        dg2_ref[...] += jnp.sum(dy * xhat2, axis=0, keepdims=True)
        db2_ref[...] += jnp.sum(dy, axis=0, keepdims=True)
        dp2 = _ln_bwd(dy, xhat2, rstd2, g2_ref[...])
        dp2b = dp2.astype(dp2b_ref.dtype)
        dp2b_ref[...] = dp2b
        back = jnp.zeros((tm, D_MODEL), F32)
        for j in range(N_FF):
            cols = slice(j * FF_BLOCK, (j + 1) * FF_BLOCK)
            dr = _dot(dp2b, w2_ref[cols, :], NT)
            da = (dr * (2.0 * jnp.sqrt(r_ref[:, cols].astype(F32)))).astype(da_ref.dtype)
            da_ref[:, cols] = da
            back = back + _dot(da, w1_ref[:, cols], NT)
        dh1 = ALPHA * dp2 + back
        dg1_ref[...] += jnp.sum(dh1 * xhat1, axis=0, keepdims=True)
        db1_ref[...] += jnp.sum(dh1, axis=0, keepdims=True)
        dp1 = _ln_bwd(dh1, xhat1, rstd1, g1_ref[...])
        dp1b = dp1.astype(dp1b_ref.dtype)
        dp1_ref[...] = dp1
        dp1b_ref[...] = dp1b
        dcat_ref[...] = _dot(dp1b, wo_ref[...], NT)

    row = pl.BlockSpec((tm, D_MODEL), lambda i: (i, 0))
    wide = pl.BlockSpec((tm, D_FF), lambda i: (i, 0))
    vec = pl.BlockSpec((1, D_MODEL), lambda i: (0, 0))
    narrow = lambda dtype: jax.ShapeDtypeStruct((t, D_MODEL), dtype)
    return pl.pallas_call(
        body, name="sublayers", grid=(t // tm,),
        in_specs=[pl.BlockSpec((2, tm, GROUP), lambda i: (0, i, 0)), row, row, _resident((D_MODEL, D_MODEL)),
                  _resident((D_MODEL, D_FF)), _resident((D_FF, D_MODEL)), vec, vec, vec, vec],
        out_specs=[row, wide, wide, row, row, row, row, vec, vec, vec, vec,
                   pl.BlockSpec((SUBLANES, LANES), lambda i: (0, 0))],
        out_shape=[narrow(BF16), jax.ShapeDtypeStruct((t, D_FF), BF16), jax.ShapeDtypeStruct((t, D_FF), BF16),
                   narrow(BF16), narrow(F32), narrow(BF16), narrow(F32)]
                  + [jax.ShapeDtypeStruct((1, D_MODEL), F32)] * 4 + [jax.ShapeDtypeStruct((SUBLANES, LANES), F32)],
        compiler_params=_params("arbitrary"),
    )(cat, x, target, w_out, w_ff1, w_ff2, g1, b1, g2, b2)


def _gate_bwd(dcat, o, proj, gate_norm_w, conv_w, after):
    t = proj.shape[1]
    tb = min(t, 512)
    hb = tb // SUBLANES
    nblk = t // tb

    def body(do2_ref, dy_ref, dyn_ref, o_ref, og_ref, gnw_ref, b_ref, bn_ref, c_ref, u_ref, ch_ref, uh_ref, cw_ref,
             after_ref, do_ref, dp_ref, dgnw_ref, dcw_ref, zbuf, dbuf):
        i = pl.program_id(0)

        @pl.when(i == 0)
        def _():
            dgnw_ref[...] = jnp.zeros_like(dgnw_ref)
            dcw_ref[...] = jnp.zeros_like(dcw_ref)

        ov, og, gnw, do2 = o_ref[...], og_ref[...], gnw_ref[...], do2_ref[...]
        rs = _per_head(lambda o_h: jnp.broadcast_to(lax.rsqrt(jnp.mean(o_h * o_h, axis=-1, keepdims=True) + EPS),
                                                    o_h.shape), ov)
        on = ov * rs
        sg = _sigmoid(og)
        sil = og * sg
        don = do2 * gnw * sil
        dgnw_ref[...] += jnp.sum(do2 * on * sil, axis=0, keepdims=True)
        dp_ref[0] = (do2 * on * gnw * (sg * (1.0 + og * (1.0 - sg)))).astype(dp_ref.dtype)
        do_ref[...] = rs * (don - on * _per_head(
            lambda p_h: jnp.broadcast_to(jnp.mean(p_h, axis=-1, keepdims=True), p_h.shape), don * on))

        bg, cg, u, dy = b_ref[...], c_ref[...], u_ref[...], dy_ref[...]
        z = cg * u
        halo = jnp.where(i > 0, ch_ref[...] * uh_ref[...], 0.0)
        z1, z2 = _conv_taps(z, halo, zbuf, tb)
        cw = cw_ref[...]
        yc = cw[2:3, :] * z + cw[1:2, :] * z1 + cw[0:1, :] * z2
        dyc = dy * bg
        dbuf[0:tb, :] = dyc
        dbuf[tb:tb + SUBLANES, :] = jnp.where(i < nblk - 1, dyn_ref[...] * bn_ref[...], 0.0)
        d1, d2 = dbuf[1:1 + tb, :], dbuf[2:2 + tb, :]
        dz = cw[2:3, :] * dyc + cw[1:2, :] * d1 + cw[0:1, :] * d2
        dp_ref[1] = (dy * yc).astype(dp_ref.dtype)
        dp_ref[2] = (dz * u).astype(dp_ref.dtype)
        dp_ref[3] = (dz * cg).astype(dp_ref.dtype)
        dcw_ref[0:1, :] += jnp.sum(dyc * z2, axis=0, keepdims=True)
        dcw_ref[1:2, :] += jnp.sum(dyc * z1, axis=0, keepdims=True)
        dcw_ref[2:3, :] += jnp.sum(dyc * z, axis=0, keepdims=True)

    half = lambda g: pl.BlockSpec((tb, GROUP), lambda i: (i, g))
    grp = lambda g: pl.BlockSpec((None, tb, GROUP), lambda i: (g, i, 0))
    prev = lambda g: pl.BlockSpec((None, SUBLANES, GROUP), lambda i: (g, jnp.maximum(i * hb - 1, 0), 0))
    nxt_row = lambda i: jnp.minimum((i + 1) * hb, t // SUBLANES - 1)
    nxt = lambda g: pl.BlockSpec((None, SUBLANES, GROUP), lambda i: (g, nxt_row(i), 0))
    vec = lambda r: pl.BlockSpec((r, GROUP), lambda i: (0, 0))
    return pl.pallas_call(
        body, name="gate_bwd", grid=(nblk,),
        in_specs=[half(0), half(1), pl.BlockSpec((SUBLANES, GROUP), lambda i: (nxt_row(i), 1)), half(0), grp(3), vec(1),
                  grp(4), nxt(4), grp(5), grp(6), prev(5), prev(6), vec(3), ANY],
        out_specs=[half(0), pl.BlockSpec((4, tb, GROUP), lambda i: (0, i, 0)), vec(1), vec(3)],
        out_shape=[jax.ShapeDtypeStruct((t, HGRN_WIDTH), F32), jax.ShapeDtypeStruct((4, t, HGRN_WIDTH), BF16),
                   jax.ShapeDtypeStruct((1, HGRN_WIDTH), F32), jax.ShapeDtypeStruct((3, CONV_WIDTH), F32)],
        scratch_shapes=[pltpu.VMEM((tb + SUBLANES, GROUP), F32), pltpu.VMEM((tb + SUBLANES, GROUP), F32)],
        compiler_params=_params("arbitrary"),
    )(dcat, dcat, dcat, o, proj, gate_norm_w, proj, proj, proj, proj, proj, proj, conv_w, after)


def _hgrn_bwd(proj, do, states, lb_logits, after):
    t = proj.shape[1]
    tb = min(t, 512)
    ncb = tb // CHUNK
    nblk = t // tb

    def body(q_ref, f_ref, v_ref, do_ref, st_ref, lbl_ref, after_ref, dp_ref, dlbl_ref, ds_scr, dlb_scr):
        i = pl.program_id(0)

        @pl.when(i == 0)
        def _():
            ds_scr[...] = jnp.zeros_like(ds_scr)
            dlb_scr[...] = jnp.zeros_like(dlb_scr)

        lb, s1 = _lower_bound(lbl_ref[...])
        causal, anti = _chunk_masks()
        every = range(ncb)
        rows = [slice(c * CHUNK, (c + 1) * CHUNK) for c in every]
        q, v, do = ([ref[r, :] for r in rows] for ref in (q_ref, v_ref, do_ref))
        st = [st_ref[c] for c in every]
        gates = [_gates(f_ref[r, :], lb) for r in rows]
        sig, f, k = ([gt[n] for gt in gates] for n in (0, 1, 3))
        b = [_dot_exact(causal, gt[2]) for gt in gates]
        mid, last = [x[CHUNK // 2:CHUNK // 2 + 1, :] for x in b], [x[CHUNK - 1:CHUNK, :] for x in b]
        e_q = [jnp.exp(b[c] - mid[c]) for c in every]
        e_k = [jnp.exp(mid[c] - b[c]) for c in every]
        e_i = [jnp.exp(x) for x in b]
        e_s = [jnp.exp(last[c] - b[c]) for c in every]
        dec = [jnp.exp(x) for x in last]
        qt, kt, qi, ks = ([a[c] * e[c] for c in every] for a, e in ((q, e_q), (k, e_k), (q, e_i), (k, e_s)))

        def masked(a, b_):
            return [[jnp.where(causal, _dot(a_h, b_h, NT), 0.0) for a_h, b_h in zip(_heads(a[c]), _heads(b_[c]))]
                    for c in every]

        def with_scores(s, other, dims):
            return [jnp.concatenate([_dot(s_h, o_h, dims) for s_h, o_h in zip(s[c], _heads(other[c]))], axis=1)
                    for c in every]

        def per_head(dims, a, b_):
            return [_per_head(lambda a_h, b_h: _dot(a_h, b_h, dims), a[c], b_[c]) for c in every]

        scores, dscores = masked(qt, kt), masked(do, v)
        dqt, dkt, dv_intra = with_scores(dscores, kt, NN), with_scores(dscores, qt, TN), with_scores(scores, do, TN)
        dqi, update = per_head(NN, do, st), per_head(TN, do, qi)

        dst = ds_scr[...]
        dsts = [None] * ncb
        for c in reversed(every):
            dsts[c] = dst
            dst = dec[c] * dst + update[c]
        ds_scr[...] = dst

        dv_state, dks = per_head(NT, ks, dsts), per_head(NN, v, dsts)
        ddec = [jnp.sum(dsts[c] * st[c], axis=0, keepdims=True) for c in every]
        dq = [dqt[c] * e_q[c] + dqi[c] * e_i[c] for c in every]
        dk = [dkt[c] * e_k[c] + dks[c] * e_s[c] for c in every]
        db = [q[c] * dq[c] - k[c] * dk[c] for c in every]
        db_last = [jnp.sum(dks[c] * ks[c], axis=0, keepdims=True) + ddec[c] * dec[c] for c in every]
        dg = [_dot_exact(anti, db[c]) + db_last[c] for c in every]
        df = [dg[c] / f[c] - dk[c] for c in every]
        dlb_scr[...] += sum(jnp.sum(df[c] * (1.0 - sig[c]), axis=0, keepdims=True) for c in every)
        dfp = [df[c] * (1.0 - lb) * sig[c] * (1.0 - sig[c]) for c in every]
        dv = [dv_intra[c] + dv_state[c] for c in every]
        for n, parts in enumerate((dq, dfp, dv)):
            dp_ref[n] = jnp.concatenate(parts, axis=0).astype(dp_ref.dtype)

        @pl.when(i == nblk - 1)
        def _():
            dlb = dlb_scr[...]
            dlbl_ref[0:1, :] = dlb * lb * (1.0 - lb)
            dlbl_ref[1:2, :] = -dlb * lb * s1

    grp = lambda g: pl.BlockSpec((None, tb, GROUP), lambda i: (g, nblk - 1 - i, 0))
    vec = pl.BlockSpec((2, HGRN_WIDTH), lambda i: (0, 0))
    return pl.pallas_call(
        body, name="hgrn_bwd", grid=(nblk,),
        in_specs=[grp(0), grp(1), grp(2), pl.BlockSpec((tb, HGRN_WIDTH), lambda i: (nblk - 1 - i, 0)),
                  pl.BlockSpec((ncb, HEAD_DIM, HGRN_WIDTH), lambda i: (nblk - 1 - i, 0, 0)), vec, ANY],
        out_specs=[pl.BlockSpec((3, tb, HGRN_WIDTH), lambda i: (0, nblk - 1 - i, 0)), vec],
        out_shape=[jax.ShapeDtypeStruct((3, t, HGRN_WIDTH), BF16), jax.ShapeDtypeStruct((2, HGRN_WIDTH), F32)],
        scratch_shapes=[pltpu.VMEM((HEAD_DIM, HGRN_WIDTH), F32), pltpu.VMEM((1, HGRN_WIDTH), F32)],
        compiler_params=_params("arbitrary"),
    )(proj, proj, proj, do, states, lb_logits, after)


def _in_bwd(dph, dpg, w_in, dpre1, after):
    t = dpre1.shape[0]
    tm = min(t, 512)

    def body(dh_ref, dg_ref, w_ref, dp_ref, after_ref, o_ref):
        acc = ALPHA * dp_ref[...]
        for g in range(N_GROUPS):
            part = dh_ref[g] if g < 3 else dg_ref[g - 3]
            acc = acc + _dot(part, w_ref[:, g * GROUP:(g + 1) * GROUP], NT)
        o_ref[...] = acc

    row = pl.BlockSpec((tm, D_MODEL), lambda i: (i, 0))
    return pl.pallas_call(
        body, name="in_bwd", grid=(t // tm,),
        in_specs=[pl.BlockSpec((3, tm, GROUP), lambda i: (0, i, 0)), pl.BlockSpec((4, tm, GROUP), lambda i: (0, i, 0)),
                  _resident((D_MODEL, IN_COLS)), row, ANY],
        out_specs=row,
        out_shape=jax.ShapeDtypeStruct((t, D_MODEL), F32),
        compiler_params=_params("parallel"),
    )(dph, dpg, w_in, dpre1, after)


def _grad_w(name, operands, widths, shape, step, after=None):
    t = operands[0].shape[-2]
    tt = min(t, 512)
    n_in, n_steps = len(operands), t // tt
    in_specs = [pl.BlockSpec((tt, w), lambda k: (k, 0)) if a.ndim == 2 else
                pl.BlockSpec((a.shape[0], tt, w), lambda k: (0, k, 0)) for a, w in zip(operands, widths)]
    extra = [] if after is None else [after]

    def body(*refs):
        o_ref, acc, narrow, sem = refs[-4:]
        k = pl.program_id(0)

        @pl.when(k == 0)
        def _():
            acc[...] = jnp.zeros_like(acc)

        step(acc, *refs[:n_in])

        @pl.when(k == n_steps - 1)
        def _():
            narrow[...] = acc[...].astype(narrow.dtype)
            out = pltpu.make_async_copy(narrow, o_ref, sem)
            out.start()
            out.wait()

    return pl.pallas_call(
        body, name=name, grid=(n_steps,), in_specs=in_specs + [ANY] * len(extra), out_specs=ANY,
        out_shape=jax.ShapeDtypeStruct(shape, BF16),
        scratch_shapes=[pltpu.VMEM(shape, F32), pltpu.VMEM(shape, BF16), pltpu.SemaphoreType.DMA],
        compiler_params=_params("arbitrary"),
    )(*operands, *extra)


def _dw_in(xb, dph, dpg, after):
    def step(acc, x_ref, dh_ref, dg_ref):
        xv = x_ref[...]
        for g in range(N_GROUPS):
            part = dh_ref[g] if g < 3 else dg_ref[g - 3]
            acc[:, g * GROUP:(g + 1) * GROUP] += _dot(xv, part, TN)

    return _grad_w("dw_in", (xb, dph, dpg), (D_MODEL, GROUP, GROUP), (D_MODEL, IN_COLS), step, after)


def _dw_out(cat, dpre1b):
    def step(acc, cat_ref, d_ref):
        dv = d_ref[...]
        for g in range(2):
            acc[g * GROUP:(g + 1) * GROUP, :] += _dot(cat_ref[g], dv, TN)

    return _grad_w("dw_out", (cat, dpre1b), (GROUP, D_MODEL), (D_MODEL, D_MODEL), step)


def _dw_ff1(h1b, da):
    def step(acc, h_ref, da_ref):
        hv = h_ref[...]
        for j in range(D_FF // FF_BLOCK):
            cols = slice(j * FF_BLOCK, (j + 1) * FF_BLOCK)
            acc[:, cols] += _dot(hv, da_ref[:, cols], TN)

    return _grad_w("dw_ff1", (h1b, da), (D_MODEL, D_FF), (D_MODEL, D_FF), step)


def _dw_ff2(r, dpre2b):
    def step(acc, r_ref, d_ref):
        dv = d_ref[...]
        for j in range(D_FF // FF_BLOCK):
            rows = slice(j * FF_BLOCK, (j + 1) * FF_BLOCK)
            acc[rows, :] += _dot(r_ref[:, rows], dv, TN)

    return _grad_w("dw_ff2", (r, dpre2b), (D_FF, D_MODEL), (D_FF, D_MODEL), step)


def _place():
    x, y, c = lax.axis_index("x"), lax.axis_index("y"), lax.axis_index("c")
    return x, y, c, 2 * x + y


def _other_chips(x, y):
    return [(1 - x, y), (x, 1 - y), (1 - x, 1 - y)]


def _place_shard(name, w, chip, cols_sharded, after=None):
    rows, cols = w.shape
    tr = min(rows, 256)
    nb = rows // tr
    full = (rows, cols * N_CHIPS) if cols_sharded else (rows * N_CHIPS, cols)
    out_map = (lambda i, s: (i, s[0])) if cols_sharded else (lambda i, s: (s[0] * nb + i, 0))

    def body(s_ref, w_ref, *rest):
        rest[-1][...] = w_ref[...].astype(rest[-1].dtype)

    extra = [] if after is None else [after]
    return pl.pallas_call(
        body, name=name,
        grid_spec=pltpu.PrefetchScalarGridSpec(
            num_scalar_prefetch=1, grid=(nb,),
            in_specs=[pl.BlockSpec((tr, cols), lambda i, s: (i, 0))] + [ANY] * len(extra),
            out_specs=pl.BlockSpec((tr, cols), out_map)),
        out_shape=jax.ShapeDtypeStruct(full, BF16),
        compiler_params=_params("parallel"),
    )(chip, w, *extra)


HBM = pl.BlockSpec(memory_space=pltpu.HBM)
SEM = pl.BlockSpec(memory_space=pltpu.SEMAPHORE)
EFFECT = pltpu.SideEffectType.DATAFLOW_SIDE_EFFECTING


PEER_SETS = {
    "sibling": (0, lambda x, y, c: [(x, y, 1 - c)]),
    "chips": (1, lambda x, y, c: [(1 - x, y, c), (x, 1 - y, c), (1 - x, 1 - y, c)]),
    "neighbours": (2, lambda x, y, c: [(1 - x, y, c), (x, 1 - y, c)]),
}


class _Split:
    def __init__(self, name, arrays, plan, others=(), peers=None):
        n_own, arrays = len(arrays), (*arrays, *others)
        n, n_copies = len(arrays), plan.count
        self.name, self.plan, self.n = name, plan, n_own
        barrier_id, peer_ids = PEER_SETS[peers] if peers else (None, None)

        def body(*refs):
            if peers:
                x, y, c, _ = _place()
                barrier = pltpu.get_barrier_semaphore()
                for peer in peer_ids(x, y, c):
                    pl.semaphore_signal(barrier, inc=1, device_id=peer, device_id_type=MESH)
                pl.semaphore_wait(barrier, len(peer_ids(0, 0, 0)))
            send_sems, recv_sems, token = refs[n], refs[n + 1], refs[-1]
            for k, (src, dst, to) in enumerate(plan(refs[:n])):
                pltpu.make_async_remote_copy(src_ref=src, dst_ref=dst, send_sem=send_sems.at[k], recv_sem=recv_sems.at[k],
                                             device_id=to, device_id_type=MESH).start()
            token[...] = jnp.zeros_like(token)

        outs = pl.pallas_call(
            body, name=name + "_start",
            out_shape=(pltpu.SemaphoreType.DMA((n_copies,)), pltpu.SemaphoreType.DMA((n_copies,)),
                       *[pltpu.HBM(a.shape, a.dtype) for a in arrays], jax.ShapeDtypeStruct((SUBLANES, LANES), F32)),
            in_specs=(HBM,) * n, out_specs=(SEM, SEM) + (HBM,) * n + (pl.BlockSpec(memory_space=pltpu.VMEM),),
            input_output_aliases={i: 2 + i for i in range(n)},
            compiler_params=pltpu.CompilerParams(has_side_effects=EFFECT, collective_id=barrier_id),
        )(*[pltpu.with_memory_space_constraint(a, pltpu.HBM) for a in arrays])
        self.sems, self.arrays, self.others, self.token = outs[:2], outs[2:2 + n_own], outs[2 + n_own:2 + n], outs[-1]

    def wait(self, after):
        n, plan = self.n, self.plan

        def body(*refs):
            send_sems, recv_sems = refs[n], refs[n + 1]
            for k, (src, dst, to) in enumerate(plan(refs[:n])):
                cp = pltpu.make_async_remote_copy(src_ref=src, dst_ref=dst, send_sem=send_sems.at[k],
                                                  recv_sem=recv_sems.at[k], device_id=to, device_id_type=MESH)
                cp.wait_send()
                cp.wait_recv()

        return pl.pallas_call(
            body, name=self.name + "_wait", out_shape=tuple(pltpu.HBM(a.shape, a.dtype) for a in self.arrays),
            in_specs=(HBM,) * n + (SEM, SEM, ANY), out_specs=(HBM,) * n, input_output_aliases={i: i for i in range(n)},
            compiler_params=pltpu.CompilerParams(has_side_effects=EFFECT),
        )(*self.arrays, *self.sems, after)


COLS_SHARDED = (True, False, True, False)
HALF_SHAPES = [(D_MODEL // 2, IN_COLS), (D_MODEL, D_MODEL // 2), (D_MODEL // 2, D_FF), (D_FF, D_MODEL // 2)]
PIECE_SHAPES = [(D_MODEL // 2, IN_COLS // N_CHIPS), (D_MODEL // N_CHIPS, D_MODEL // 2),
                (D_MODEL // 2, D_FF // N_CHIPS), (D_FF // N_CHIPS, D_MODEL // 2)]


def _shard_view(kind, ref, chip):
    if COLS_SHARDED[kind]:
        n = ref.shape[1] // N_CHIPS
        return ref.at[:, pl.ds(chip * n, n)]
    n = ref.shape[0] // N_CHIPS
    return ref.at[pl.ds(chip * n, n), :]


def _half_view(kind, ref, h):
    if COLS_SHARDED[kind]:
        n = ref.shape[0] // 2
        return ref.at[pl.ds(h * n, n), :]
    n = ref.shape[1] // 2
    return ref.at[:, pl.ds(h * n, n)]


def _plan(count):
    def mark(fn):
        fn.count = count
        return fn
    return mark


def _shard_rows_view(kind, ref, chip, part, n_parts):
    if COLS_SHARDED[kind]:
        m, n = ref.shape[0] // n_parts, ref.shape[1] // N_CHIPS
        return ref.at[pl.ds(part * m, m), pl.ds(chip * n, n)]
    m = ref.shape[0] // N_CHIPS // n_parts
    return ref.at[pl.ds((n_parts * chip + part) * m, m), :]


def _shard_half_view(kind, ref, chip, h):
    return _shard_rows_view(kind, ref, chip, h, 2)


def _gather_over_ici(kinds, weights):
    @_plan(2 * len(kinds))
    def plan(refs):
        x, y, c, me = _place()
        mine = [_shard_half_view(kind, ref, me, c) for kind, ref in zip(kinds, refs)]
        return [(v, v, to) for v in mine for to in ((1 - x, y, c), (x, 1 - y, c))]

    return _Split("gather_ici_" + "".join(map(str, kinds)), tuple(weights), plan, peers="neighbours")


def _relay_over_ici(kinds, weights, others=()):
    @_plan(2 * len(kinds))
    def plan(refs):
        x, y, c, _ = _place()
        x_nbr, y_nbr = 2 * (1 - x) + y, 2 * x + (1 - y)
        out = []
        for kind, ref in zip(kinds, refs):
            first, second = (_shard_rows_view(kind, ref, chip, 2 * c + q, 4) for q, chip in ((0, x_nbr), (1, y_nbr)))
            out += [(first, first, (x, 1 - y, c)), (second, second, (1 - x, y, c))]
        return out

    return _Split("relay_ici_" + "".join(map(str, kinds)), tuple(weights), plan, others, peers="neighbours")


def _gather_w_in_over_ici(w_in, conv4):
    @_plan(6)
    def plan(refs):
        x, y, c, me = _place()
        half, conv = _shard_half_view(0, refs[0], me, c), refs[1].at[me]
        return [(v, v, (px, py, c)) for v in (half, conv) for px, py in _other_chips(x, y)]

    return _Split("gather_w_in_ici", (w_in, conv4), plan, peers="chips")


def _gather_over_d2d(kinds, weights):
    @_plan(3 * len(kinds))
    def plan(refs):
        x, y, c, _ = _place()
        got = [_shard_half_view(kind, ref, 2 * px + py, c) for kind, ref in zip(kinds, refs)
               for px, py in _other_chips(x, y)]
        return [(v, v, (x, y, 1 - c)) for v in got]

    return _Split("gather_d2d_" + "".join(map(str, kinds)), tuple(weights), plan, peers="sibling")


def _swap_halves(kinds, grads):
    @_plan(len(kinds))
    def plan(refs):
        x, y, c, _ = _place()
        return [(_half_view(kind, g, 1 - c), land, (x, y, 1 - c))
                for kind, g, land in zip(kinds, refs[:len(kinds)], refs[len(kinds):])]

    lands = [lax.empty(HALF_SHAPES[kind], g.dtype) for kind, g in zip(kinds, grads)]
    return _Split("swap_halves_" + "".join(map(str, kinds)), (*grads, *lands), plan, peers="sibling")


def _add_half(name, g, recv, core, rows_split):
    shape = recv.shape
    tr = min(shape[0], 128 if rows_split else 256)
    nb = shape[0] // tr

    def body(c_ref, g_ref, r_ref, o_ref):
        o_ref[...] = (g_ref[...].astype(F32) + r_ref[...].astype(F32)).astype(o_ref.dtype)

    g_map = (lambda i, c_ref: (c_ref[0] * nb + i, 0)) if rows_split else (lambda i, c_ref: (i, c_ref[0]))
    blk = pl.BlockSpec((tr, shape[1]), lambda i, c_ref: (i, 0))
    return pl.pallas_call(
        body, name=name,
        grid_spec=pltpu.PrefetchScalarGridSpec(
            num_scalar_prefetch=1, grid=(nb,),
            in_specs=[pl.BlockSpec((tr, shape[1]), g_map), blk], out_specs=blk),
        out_shape=jax.ShapeDtypeStruct(shape, BF16),
        compiler_params=_params("parallel"),
    )(core, g, recv)


def _exchange_pieces(kinds, halves, pack=None):
    n_p, n = N_CHIPS - 1, len(kinds)

    @_plan(n_p * n + (0 if pack is None else N_DEV - 1))
    def plan(refs):
        x, y, c, _ = _place()
        copies = []
        if pack is not None:
            me = 4 * x + 2 * y + c
            peers = [((1 - x) if m & 4 else x, (1 - y) if m & 2 else y, (1 - c) if m & 1 else c) for m in range(1, N_DEV)]
            copies += [(refs[2 * n], refs[2 * n + 1].at[me], peer) for peer in peers]
        return copies + [(_shard_view(kind, half, 2 * px + py), land.at[j], (px, py, c))
                         for j, (px, py) in enumerate(_other_chips(x, y))
                         for kind, half, land in zip(kinds, refs[:n], refs[n:2 * n])]

    lands = [lax.empty((n_p,) + PIECE_SHAPES[kind], BF16) for kind in kinds]
    small = () if pack is None else (pack, lax.empty((N_DEV,) + pack.shape, F32))
    return _Split("exchange_pieces_" + "".join(map(str, kinds)), (*halves, *lands, *small), plan,
                  peers="chips" if pack is None else None)


def _sum_pieces(name, half, slots, place, rows_split, after):
    n_p, rows, cols = slots.shape
    tr = min(rows, 256)
    nb = rows // tr
    if rows_split:
        own_map = lambda i, s: (i, s[0])
        out_map = lambda i, s: (s[1] * nb + i, 0)
        shard = (2 * rows, cols)
    else:
        own_map = lambda i, s: (s[0] * nb + i, 0)
        out_map = lambda i, s: (i, s[1])
        shard = (rows, 2 * cols)

    def body(s_ref, own_ref, slot_ref, after_ref, o_ref):
        total = own_ref[...].astype(F32)
        for j in range(n_p):
            total = total + slot_ref[j].astype(F32)
        o_ref[...] = total

    return pl.pallas_call(
        body, name=name,
        grid_spec=pltpu.PrefetchScalarGridSpec(
            num_scalar_prefetch=1, grid=(nb,),
            in_specs=[pl.BlockSpec((tr, cols), own_map), pl.BlockSpec((n_p, tr, cols), lambda i, s: (0, i, 0)), ANY],
            out_specs=pl.BlockSpec((tr, cols), out_map)),
        out_shape=jax.ShapeDtypeStruct(shard, F32),
        compiler_params=_params("parallel"),
    )(place, half, slots, after)


def _join_halves(kinds, shards):
    @_plan(len(kinds))
    def plan(refs):
        x, y, c, _ = _place()
        return [(_half_view(kind, g, c), _half_view(kind, g, c), (x, y, 1 - c)) for kind, g in zip(kinds, refs)]

    return _Split("join_halves_" + "".join(map(str, kinds)), tuple(shards), plan, peers="sibling")


N_DEV = 8


def _sum_shared(pack, land, device):
    def body(d_ref, p_ref, l_ref, o_ref):
        me = d_ref[0]
        total = jnp.where(me == 0, p_ref[...], l_ref[0])
        for d in range(1, N_DEV):
            total = total + jnp.where(me == d, p_ref[...], l_ref[d])
        o_ref[...] = total

    return pl.pallas_call(
        body, name="sum_shared",
        grid_spec=pltpu.PrefetchScalarGridSpec(
            num_scalar_prefetch=1, grid=(1,),
            in_specs=[pl.BlockSpec(pack.shape, lambda i, d: (0, 0)), pl.BlockSpec(land.shape, lambda i, d: (0, 0, 0))],
            out_specs=pl.BlockSpec(pack.shape, lambda i, d: (0, 0))),
        out_shape=jax.ShapeDtypeStruct(pack.shape, F32),
    )(device, pack, land)


def _adamw(name, w, g, m, v, after=None):
    rows, cols = w.shape
    tr = min(rows, 256)
    extra = [] if after is None else [after]

    def body(w_ref, g_ref, m_ref, v_ref, *rest):
        d_ref, nm_ref, nv_ref = rest[-3:]
        d_ref[...], nm_ref[...], nv_ref[...] = _adam_step(w_ref[...], g_ref[...], m_ref[...], v_ref[...])

    blk = pl.BlockSpec((tr, cols), lambda i: (i, 0))
    return pl.pallas_call(
        body, name=name, grid=(rows // tr,), in_specs=[blk] * 4 + [ANY] * len(extra), out_specs=[blk] * 3,
        out_shape=[jax.ShapeDtypeStruct(w.shape, F32)] * 3,
        compiler_params=_params("parallel"),
    )(w, g, m, v, *extra)


def _adam_step(w, g, m, v):
    nm = ADAM_B1 * m + (1.0 - ADAM_B1) * g
    nv = ADAM_B2 * v + (1.0 - ADAM_B2) * jnp.square(g)
    m_hat = nm * (1.0 / (1.0 - ADAM_B1 ** ADAM_STEP))
    v_hat = nv * (1.0 / (1.0 - ADAM_B2 ** ADAM_STEP))
    return -ADAM_LR * (m_hat / (jnp.sqrt(v_hat) + ADAM_EPS) + ADAM_WD * w), nm, nv


def _adamw_small(tot, chip, weights, ms, vs, after):
    n, half = len(weights), D_MODEL // 2

    def body(chip_ref, tot_ref, *refs):
        ins, outs = refs[:3 * n], refs[3 * n + 1:]
        tot = tot_ref[...]
        conv_all = jnp.concatenate([tot[5:6, half:], tot[6:7, :half], tot[6:7, half:]], axis=0)
        conv = sum(jnp.where(chip_ref[0] == s, conv_all[:, s * LANES:(s + 1) * LANES], 0.0) for s in range(N_CHIPS))
        grads = [jnp.concatenate([tot[4:5, :half], tot[4:5, half:]], axis=0), tot[5:6, :half], conv,
                 tot[0:1], tot[1:2], tot[2:3], tot[3:4]]
        for k, g in enumerate(grads):
            delta, nm, nv = _adam_step(ins[k][...], g, ins[n + k][...], ins[2 * n + k][...])
            outs[k][...], outs[n + k][...], outs[2 * n + k][...], outs[3 * n + k][...] = g, delta, nm, nv

    whole = lambda a: pl.BlockSpec(a.shape, lambda i, s: (0,) * a.ndim)
    arrays = (*weights, *ms, *vs)
    return pl.pallas_call(
        body, name="adamw_small",
        grid_spec=pltpu.PrefetchScalarGridSpec(
            num_scalar_prefetch=1, grid=(1,), in_specs=[whole(tot)] + [whole(a) for a in arrays] + [ANY],
            out_specs=[whole(a) for a in weights] * 4),
        out_shape=[jax.ShapeDtypeStruct(a.shape, F32) for a in weights] * 4,
    )(chip, tot, *arrays, after)


def kernel(x, w_in, lb_logits, gate_norm_w, conv_w, w_out, ln1_g, ln1_b, w_ff1, w_ff2, ln2_g, ln2_b, loss_target, m_w_in, m_lb_logits, m_gate_norm_w, m_conv_w, m_w_out, m_ln1_g, m_ln1_b, m_w_ff1, m_w_ff2, m_ln2_g, m_ln2_b, v_w_in, v_lb_logits, v_gate_norm_w, v_conv_w, v_w_out, v_ln1_g, v_ln1_b, v_w_ff1, v_w_ff2, v_ln2_g, v_ln2_b):
    xs, tgt = x[0], loss_target[0]
    chip = 2 * lax.axis_index("x") + lax.axis_index("y")
    core = lax.axis_index("c").astype(jnp.int32).reshape(1)
    chip1 = chip.astype(jnp.int32).reshape(1)
    place = jnp.concatenate([chip1, core])

    conv4 = lax.dynamic_update_slice(jnp.zeros((N_CHIPS,) + conv_w.shape[1:], F32), conv_w, (chip, 0, 0))
    ici_in = _gather_w_in_over_ici(_place_shard("place_w_in", w_in[0], chip1, True), conv4)
    rest = (1, 2, 3)
    ici_rest = _gather_over_ici(rest, (_place_shard("place_w_out", w_out[0], chip1, False, after=ici_in.token),
                                       _place_shard("place_w_ff1", w_ff1[0], chip1, True, after=ici_in.token),
                                       _place_shard("place_w_ff2", w_ff2[0], chip1, False, after=ici_in.token)))
    wb_in, cv4 = ici_in.wait(ici_rest.token)
    d2d_in = _gather_over_d2d((0,), (wb_in,))
    wb_in, = d2d_in.wait(d2d_in.token)
    conv_full = cv4.transpose(1, 0, 2).reshape(3, CONV_WIDTH)

    proj, xb = _in_proj(xs, wb_in, ici_rest.token)
    relay_rest = _relay_over_ici(rest, ici_rest.wait(proj))
    o, states = _hgrn_fwd(proj, lb_logits, relay_rest.token)
    d2d_rest = _gather_over_d2d(rest, relay_rest.wait(o))
    cat = _gate_fwd(proj, o, gate_norm_w, conv_full, d2d_rest.token)
    wb_out, wb_ff1, wb_ff2 = d2d_rest.wait(cat)

    (h1b, r, da, dpre2b, dpre1, dpre1b, dcat, g_ln1_g, g_ln1_b, g_ln2_g, g_ln2_b, loss8) = _sublayers(
        cat, xs, tgt, wb_out, wb_ff1, wb_ff2, ln1_g, ln1_b, ln2_g, ln2_b)

    names = ("w_in", "w_out", "w_ff1", "w_ff2")

    def add_halves(kinds, grads, lands):
        return [_add_half("add_half_" + names[k], g, ld, core, COLS_SHARDED[k]) for k, g, ld in zip(kinds, grads, lands)]

    def sum_pieces(kinds, halves, lands, after):
        return [_sum_pieces("sum_pieces_" + names[k], h, ld, place, COLS_SHARDED[k], after)
                for k, h, ld in zip(kinds, halves, lands)]

    early = (1, 2, 3)
    swap = _swap_halves(early, (_dw_out(cat, dpre1b), _dw_ff1(h1b, da), _dw_ff2(r, dpre2b)))
    do, dpg, g_gnw, g_conv = _gate_bwd(dcat, o, proj, gate_norm_w, conv_full, swap.token)
    swapped = swap.wait(do)
    exch = _exchange_pieces(early, add_halves(early, swapped[:3], swapped[3:]))
    dph, g_lbl = _hgrn_bwd(proj, do, states, lb_logits, exch.token)
    g_in_local = _dw_in(xb, dph, dpg, dph)

    late = (0,)
    swap = _swap_halves(late, (g_in_local,))
    grad_x = _in_bwd(dph, dpg, wb_in, dpre1, swap.token)
    exchanged = exch.wait(grad_x)
    pack = jnp.concatenate([
        g_ln1_g, g_ln1_b, g_ln2_g, g_ln2_b,
        jnp.concatenate([g_lbl[0:1], g_lbl[1:2]], axis=1),
        jnp.concatenate([g_gnw, g_conv[0:1]], axis=1),
        jnp.concatenate([g_conv[1:2], g_conv[2:3]], axis=1),
        jnp.concatenate([loss8[0:1], jnp.zeros((1, D_MODEL - LANES), F32)], axis=1)], axis=0)
    swapped = swap.wait(exchanged[0])
    exch = _exchange_pieces(late, add_halves(late, swapped[:1], swapped[1:]), pack)
    join = _join_halves(early, sum_pieces(early, exchanged[:3], exchanged[3:], exch.token))
    g_w_out, g_w_ff1, g_w_ff2 = join.wait(join.token)
    d_ff1, nm_ff1, nv_ff1 = _adamw("adamw_w_ff1", w_ff1[0], g_w_ff1, m_w_ff1[0], v_w_ff1[0])
    d_ff2, nm_ff2, nv_ff2 = _adamw("adamw_w_ff2", w_ff2[0], g_w_ff2, m_w_ff2[0], v_w_ff2[0], d_ff1)
    d_out, nm_out, nv_out = _adamw("adamw_w_out", w_out[0], g_w_out, m_w_out[0], v_w_out[0], d_ff2)
    exchanged = exch.wait(d_out)
    tot = _sum_shared(exchanged[2], exchanged[3], 2 * chip1 + core)
    loss = tot[7, 0]
    join = _join_halves(late, sum_pieces(late, exchanged[:1], exchanged[1:2], tot))
    small = ("lb_logits", "gate_norm_w", "conv_w", "ln1_g", "ln1_b", "ln2_g", "ln2_b")
    small_out = _adamw_small(
        tot, chip1, (lb_logits, gate_norm_w, conv_w[0], ln1_g, ln1_b, ln2_g, ln2_b),
        (m_lb_logits, m_gate_norm_w, m_conv_w[0], m_ln1_g, m_ln1_b, m_ln2_g, m_ln2_b),
        (v_lb_logits, v_gate_norm_w, v_conv_w[0], v_ln1_g, v_ln1_b, v_ln2_g, v_ln2_b), join.token)
    g_w_in, = join.wait(small_out[0])
    d_in, nm_in, nv_in = _adamw("adamw_w_in", w_in[0], g_w_in, m_w_in[0], v_w_in[0])

    def results(n_kind, large):
        out = dict(zip(small, small_out[n_kind * len(small):(n_kind + 1) * len(small)]))
        out["conv_w"] = out["conv_w"][None]
        out.update({name: a[None] for name, a in zip(("w_in", "w_out", "w_ff1", "w_ff2"), large)})
        return [out[name] for name in ("w_in", "lb_logits", "gate_norm_w", "conv_w", "w_out", "ln1_g", "ln1_b",
                                       "w_ff1", "w_ff2", "ln2_g", "ln2_b")]

    return (loss, grad_x[None], *results(0, (g_w_in, g_w_out, g_w_ff1, g_w_ff2)),
            *results(1, (d_in, d_out, d_ff1, d_ff2)), *results(2, (nm_in, nm_out, nm_ff1, nm_ff2)),
            *results(3, (nv_in, nv_out, nv_ff1, nv_ff2)))
```

```python
import jax
import jax.numpy as jnp
from jax import lax
from jax.experimental import pallas as pl
from jax.experimental.pallas import tpu as pltpu

F32 = jnp.float32
BF16 = jnp.bfloat16
MXU_DTYPE = jnp.bfloat16

D_MODEL = 1024
HGRN_WIDTH = 512
HEAD_DIM = 128
N_HEADS = 4
CONV_WIDTH = 512
CHUNK = 64
D_FF = 4096
IN_COLS = 3584
GROUP = 512
N_GROUPS = IN_COLS // GROUP
ALPHA = 2.0 ** 0.25
EPS = 1e-5
N_CHIPS = 4
ADAM_LR, ADAM_B1, ADAM_B2, ADAM_EPS, ADAM_WD, ADAM_STEP = 0.001, 0.9, 0.999, 1e-08, 0.01, 10

LANES = 128
SUBLANES = 8
VMEM_LIMIT = 56 * 1024 * 1024
FF_BLOCK = 1024
N_FF = D_FF // FF_BLOCK
GATE_STRIP = 64

NN = (((1,), (0,)), ((), ()))
NT = (((1,), (1,)), ((), ()))
TN = (((0,), (0,)), ((), ()))
MESH = pl.DeviceIdType.MESH
ANY = pl.BlockSpec(memory_space=pl.ANY)


def _dot(a, b, dims):
    return lax.dot_general(a.astype(MXU_DTYPE), b.astype(MXU_DTYPE), dims, preferred_element_type=F32)


def _dot_exact(ones, v):
    ones = ones.astype(jnp.bfloat16)
    hi = v.astype(jnp.bfloat16)
    rest = v - hi.astype(F32)
    mid = rest.astype(jnp.bfloat16)
    low = (rest - mid.astype(F32)).astype(jnp.bfloat16)
    return sum(lax.dot_general(ones, part, NN, preferred_element_type=F32) for part in (hi, mid, low))


def _params(*sem):
    return pltpu.CompilerParams(dimension_semantics=sem, vmem_limit_bytes=VMEM_LIMIT)


def _resident(shape):
    return pl.BlockSpec(shape, lambda *_: (0,) * len(shape), pipeline_mode=pl.Buffered(1))


def _sigmoid(v):
    return 1.0 / (1.0 + jnp.exp(-v))


def _lower_bound(lbl):
    m = jnp.max(lbl, axis=0, keepdims=True)
    e = jnp.exp(lbl - m)
    s = e / jnp.sum(e, axis=0, keepdims=True)
    return s[0:1, :], s[1:2, :]


def _heads(v):
    return [v[:, h * HEAD_DIM:(h + 1) * HEAD_DIM] for h in range(N_HEADS)]


def _per_head(fn, *arrays):
    return jnp.concatenate([fn(*parts) for parts in zip(*map(_heads, arrays))], axis=1)


def _in_proj(x, w_in, after):
    t = x.shape[0]
    tm = min(t, 512)

    def body(x_ref, w_ref, after_ref, o_ref, xb_ref):
        xb = x_ref[...].astype(xb_ref.dtype)
        xb_ref[...] = xb
        for g in range(N_GROUPS):
            o_ref[g] = _dot(xb, w_ref[:, g * GROUP:(g + 1) * GROUP], NN)

    return pl.pallas_call(
        body, name="in_proj", grid=(t // tm,),
        in_specs=[pl.BlockSpec((tm, D_MODEL), lambda i: (i, 0)), _resident((D_MODEL, IN_COLS)), ANY],
        out_specs=[pl.BlockSpec((N_GROUPS, tm, GROUP), lambda i: (0, i, 0)), pl.BlockSpec((tm, D_MODEL), lambda i: (i, 0))],
        out_shape=[jax.ShapeDtypeStruct((N_GROUPS, t, GROUP), F32), jax.ShapeDtypeStruct((t, D_MODEL), BF16)],
        compiler_params=_params("parallel"),
    )(x, w_in, after)


def _gates(fp, lb):
    sig = _sigmoid(fp)
    f = lb + (1.0 - lb) * sig
    return sig, f, jnp.log(f), 1.0 - f


def _chunk_masks():
    row = lax.broadcasted_iota(jnp.int32, (CHUNK, CHUNK), 0)
    col = lax.broadcasted_iota(jnp.int32, (CHUNK, CHUNK), 1)
    return row >= col, row <= col


def _hgrn_fwd(proj, lb_logits, after):
    t = proj.shape[1]
    tb = min(t, 512)
    ncb = tb // CHUNK

    def body(q_ref, f_ref, v_ref, lbl_ref, after_ref, o_ref, st_ref, s_scr):
        @pl.when(pl.program_id(0) == 0)
        def _():
            s_scr[...] = jnp.zeros_like(s_scr)

        lb, _ = _lower_bound(lbl_ref[...])
        causal, _ = _chunk_masks()

        every = range(ncb)
        rows = [slice(c * CHUNK, (c + 1) * CHUNK) for c in every]
        q, v = [q_ref[r, :] for r in rows], [v_ref[r, :] for r in rows]
        gates = [_gates(f_ref[r, :], lb) for r in rows]
        k = [gt[3] for gt in gates]
        b = [_dot_exact(causal, gt[2]) for gt in gates]
        mid, last = [x[CHUNK // 2:CHUNK // 2 + 1, :] for x in b], [x[CHUNK - 1:CHUNK, :] for x in b]
        qt = [q[c] * jnp.exp(b[c] - mid[c]) for c in every]
        kt = [k[c] * jnp.exp(mid[c] - b[c]) for c in every]
        qi = [q[c] * jnp.exp(b[c]) for c in every]
        ks = [k[c] * jnp.exp(last[c] - b[c]) for c in every]
        dec = [jnp.exp(x) for x in last]
        scores = [[jnp.where(causal, _dot(a, b_, NT), 0.0) for a, b_ in zip(_heads(qt[c]), _heads(kt[c]))] for c in every]
        intra = [[_dot(s, v_h, NN) for s, v_h in zip(scores[c], _heads(v[c]))] for c in every]
        update = [_per_head(lambda v_h, ks_h: _dot(v_h, ks_h, TN), v[c], ks[c]) for c in every]

        st = s_scr[...]
        states = []
        for c in every:
            states.append(st)
            st_ref[c] = st
            st = dec[c] * st + update[c]
        s_scr[...] = st

        o_ref[...] = jnp.concatenate(
            [jnp.concatenate([i_h + _dot(qi_h, st_h, NT) for i_h, qi_h, st_h in
                              zip(intra[c], _heads(qi[c]), _heads(states[c]))], axis=1) for c in every], axis=0)

    grp = lambda g: pl.BlockSpec((None, tb, GROUP), lambda i: (g, i, 0))
    return pl.pallas_call(
        body, name="hgrn_fwd", grid=(t // tb,),
        in_specs=[grp(0), grp(1), grp(2), pl.BlockSpec((2, HGRN_WIDTH), lambda i: (0, 0)), ANY],
        out_specs=[pl.BlockSpec((tb, HGRN_WIDTH), lambda i: (i, 0)),
                   pl.BlockSpec((ncb, HEAD_DIM, HGRN_WIDTH), lambda i: (i, 0, 0))],
        out_shape=[jax.ShapeDtypeStruct((t, HGRN_WIDTH), F32),
                   jax.ShapeDtypeStruct((t // CHUNK, HEAD_DIM, HGRN_WIDTH), F32)],
        scratch_shapes=[pltpu.VMEM((HEAD_DIM, HGRN_WIDTH), F32)],
        compiler_params=_params("arbitrary"),
    )(proj, proj, proj, lb_logits, after)


def _gate_fwd(proj, o, gate_norm_w, conv_w, after):
    t = proj.shape[1]
    tb = min(t, 512)
    hb = tb // SUBLANES

    def body(o_ref, og_ref, gnw_ref, b_ref, c_ref, u_ref, ch_ref, uh_ref, cw_ref, after_ref, cat_ref, zbuf):
        i = pl.program_id(0)
        zbuf[0:SUBLANES, :] = jnp.where(i > 0, ch_ref[...] * uh_ref[...], 0.0)
        zbuf[SUBLANES:SUBLANES + tb, :] = c_ref[...] * u_ref[...]
        gnw, cw = gnw_ref[...], cw_ref[...]
        for s in range(tb // GATE_STRIP):
            rows = slice(s * GATE_STRIP, (s + 1) * GATE_STRIP)
            og = og_ref[rows, :]
            on = _per_head(lambda o_h: o_h * lax.rsqrt(jnp.mean(o_h * o_h, axis=-1, keepdims=True) + EPS), o_ref[rows, :])
            cat_ref[0, rows, :] = (on * gnw * (og * _sigmoid(og))).astype(cat_ref.dtype)
            at = lambda shift: zbuf[shift + s * GATE_STRIP:shift + (s + 1) * GATE_STRIP, :]
            yc = cw[2:3, :] * at(SUBLANES) + cw[1:2, :] * at(SUBLANES - 1) + cw[0:1, :] * at(SUBLANES - 2)
            cat_ref[1, rows, :] = (b_ref[rows, :] * yc).astype(cat_ref.dtype)

    grp = lambda g: pl.BlockSpec((None, tb, GROUP), lambda i: (g, i, 0))
    prev = lambda g: pl.BlockSpec((None, SUBLANES, GROUP), lambda i: (g, jnp.maximum(i * hb - 1, 0), 0))
    vec = lambda r: pl.BlockSpec((r, GROUP), lambda i: (0, 0))
    return pl.pallas_call(
        body, name="gate_fwd", grid=(t // tb,),
        in_specs=[pl.BlockSpec((tb, GROUP), lambda i: (i, 0)), grp(3), vec(1), grp(4), grp(5), grp(6), prev(5), prev(6),
                  vec(3), ANY],
        out_specs=pl.BlockSpec((2, tb, GROUP), lambda i: (0, i, 0)),
        out_shape=jax.ShapeDtypeStruct((2, t, HGRN_WIDTH), BF16),
        scratch_shapes=[pltpu.VMEM((tb + SUBLANES, GROUP), F32)],
        compiler_params=_params("parallel"),
    )(o, proj, gate_norm_w, proj, proj, proj, proj, proj, conv_w, after)


def _ln_bwd(dy, xhat, rstd, g):
    dxhat = dy * g
    m1 = jnp.mean(dxhat, axis=-1, keepdims=True)
    m2 = jnp.mean(dxhat * xhat, axis=-1, keepdims=True)
    return rstd * (dxhat - m1 - xhat * m2)


def _layer_norm(pre):
    xc = pre - jnp.mean(pre, axis=-1, keepdims=True)
    rstd = lax.rsqrt(jnp.mean(xc * xc, axis=-1, keepdims=True) + EPS)
    return xc * rstd, rstd


def _sublayers(cat, x, target, w_out, w_ff1, w_ff2, g1, b1, g2, b2):
    t = x.shape[0]
    tm = min(t, 256)

    def body(cat_ref, x_ref, tg_ref, wo_ref, w1_ref, w2_ref, g1_ref, b1_ref, g2_ref, b2_ref,
             h1_ref, r_ref, da_ref, dp2b_ref, dp1_ref, dp1b_ref, dcat_ref, dg1_ref, db1_ref, dg2_ref, db2_ref, loss_ref):
        @pl.when(pl.program_id(0) == 0)
        def _():
            for ref in (dg1_ref, db1_ref, dg2_ref, db2_ref, loss_ref):
                ref[...] = jnp.zeros_like(ref)

        mix = _dot(cat_ref[0], wo_ref[0:GROUP, :], NN) + _dot(cat_ref[1], wo_ref[GROUP:2 * GROUP, :], NN)
        xhat1, rstd1 = _layer_norm(ALPHA * x_ref[...] + mix)
        h1 = xhat1 * g1_ref[...] + b1_ref[...]
        h1b = h1.astype(h1_ref.dtype)
        h1_ref[...] = h1b
        mlp = jnp.zeros((tm, D_MODEL), F32)
        for j in range(N_FF):
            cols = slice(j * FF_BLOCK, (j + 1) * FF_BLOCK)
            r = jnp.square(jnp.maximum(_dot(h1b, w1_ref[:, cols], NN), 0.0)).astype(r_ref.dtype)
            r_ref[:, cols] = r
            mlp = mlp + _dot(r, w2_ref[cols, :], NN)
        xhat2, rstd2 = _layer_norm(ALPHA * h1 + mlp)
        err = xhat2 * g2_ref[...] + b2_ref[...] - tg_ref[...]
        loss_ref[...] += 0.5 * jnp.sum(jnp.mean(err * err, axis=-1, keepdims=True))
        dy = err * (1.0 / D_MODEL)
        dg2_ref[...] += jnp.sum(dy * xhat2, axis=0, keepdims=True)
        db2_ref[...] += jnp.sum(dy, axis=0, keepdims=True)
        dp2 = _ln_bwd(dy, xhat2, rstd2, g2_ref[...])
        dp2b = dp2.astype(dp2b_ref.dtype)
        dp2b_ref[...] = dp2b
        back = jnp.zeros((tm, D_MODEL), F32)
        for j in range(N_FF):
            cols = slice(j * FF_BLOCK, (j + 1) * FF_BLOCK)
            dr = _dot(dp2b, w2_ref[cols, :], NT)
            da = (dr * (2.0 * jnp.sqrt(r_ref[:, cols].astype(F32)))).astype(da_ref.dtype)
            da_ref[:, cols] = da
            back = back + _dot(da, w1_ref[:, cols], NT)
        dh1 = ALPHA * dp2 + back
        dg1_ref[...] += jnp.sum(dh1 * xhat1, axis=0, keepdims=True)
        db1_ref[...] += jnp.sum(dh1, axis=0, keepdims=True)
        dp1 = _ln_bwd(dh1, xhat1, rstd1, g1_ref[...])
        dp1b = dp1.astype(dp1b_ref.dtype)
        dp1_ref[...] = dp1
        dp1b_ref[...] = dp1b
        dcat_ref[...] = _dot(dp1b, wo_ref[...], NT)

    row = pl.BlockSpec((tm, D_MODEL), lambda i: (i, 0))
    wide = pl.BlockSpec((tm, D_FF), lambda i: (i, 0))
    vec = pl.BlockSpec((1, D_MODEL), lambda i: (0, 0))
    narrow = lambda dtype: jax.ShapeDtypeStruct((t, D_MODEL), dtype)
    return pl.pallas_call(
        body, name="sublayers", grid=(t // tm,),
        in_specs=[pl.BlockSpec((2, tm, GROUP), lambda i: (0, i, 0)), row, row, _resident((D_MODEL, D_MODEL)),
                  _resident((D_MODEL, D_FF)), _resident((D_FF, D_MODEL)), vec, vec, vec, vec],
        out_specs=[row, wide, wide, row, row, row, row, vec, vec, vec, vec,
                   pl.BlockSpec((SUBLANES, LANES), lambda i: (0, 0))],
        out_shape=[narrow(BF16), jax.ShapeDtypeStruct((t, D_FF), BF16), jax.ShapeDtypeStruct((t, D_FF), BF16),
                   narrow(BF16), narrow(F32), narrow(BF16), narrow(F32)]
                  + [jax.ShapeDtypeStruct((1, D_MODEL), F32)] * 4 + [jax.ShapeDtypeStruct((SUBLANES, LANES), F32)],
        compiler_params=_params("arbitrary"),
    )(cat, x, target, w_out, w_ff1, w_ff2, g1, b1, g2, b2)


def _gate_bwd(dcat, o, proj, gate_norm_w, conv_w, after):
    t = proj.shape[1]
    tb = min(t, 512)
    hb = tb // SUBLANES
    nblk = t // tb

    def body(do2_ref, dy_ref, dyn_ref, o_ref, og_ref, gnw_ref, b_ref, bn_ref, c_ref, u_ref, ch_ref, uh_ref, cw_ref,
             after_ref, do_ref, dp_ref, dgnw_ref, dcw_ref, zbuf, dbuf):
        i = pl.program_id(0)

        @pl.when(i == 0)
        def _():
            dgnw_ref[...] = jnp.zeros_like(dgnw_ref)
            dcw_ref[...] = jnp.zeros_like(dcw_ref)

        zbuf[0:SUBLANES, :] = jnp.where(i > 0, ch_ref[...] * uh_ref[...], 0.0)
        zbuf[SUBLANES:SUBLANES + tb, :] = c_ref[...] * u_ref[...]
        dbuf[0:tb, :] = dy_ref[...] * b_ref[...]
        dbuf[tb:tb + SUBLANES, :] = jnp.where(i < nblk - 1, dyn_ref[...] * bn_ref[...], 0.0)

        gnw, cw = gnw_ref[...], cw_ref[...]
        dgnw, dcw = jnp.zeros((GATE_STRIP, GROUP), F32), [jnp.zeros((GATE_STRIP, GROUP), F32) for _ in range(3)]
        for s in range(tb // GATE_STRIP):
            rows = slice(s * GATE_STRIP, (s + 1) * GATE_STRIP)
            ov, og, do2 = o_ref[rows, :], og_ref[rows, :], do2_ref[rows, :]
            rs = _per_head(lambda o_h: jnp.broadcast_to(
                lax.rsqrt(jnp.mean(o_h * o_h, axis=-1, keepdims=True) + EPS), o_h.shape), ov)
            on = ov * rs
            sg = _sigmoid(og)
            sil = og * sg
            don = do2 * gnw * sil
            dgnw = dgnw + do2 * on * sil
            dp_ref[0, rows, :] = (do2 * on * gnw * (sg * (1.0 + og * (1.0 - sg)))).astype(dp_ref.dtype)
            do_ref[rows, :] = rs * (don - on * _per_head(
                lambda p_h: jnp.broadcast_to(jnp.mean(p_h, axis=-1, keepdims=True), p_h.shape), don * on))

            at = lambda buf, shift: buf[shift + s * GATE_STRIP:shift + (s + 1) * GATE_STRIP, :]
            z, z1, z2 = at(zbuf, SUBLANES), at(zbuf, SUBLANES - 1), at(zbuf, SUBLANES - 2)
            dyc, d1, d2 = at(dbuf, 0), at(dbuf, 1), at(dbuf, 2)
            yc = cw[2:3, :] * z + cw[1:2, :] * z1 + cw[0:1, :] * z2
            dz = cw[2:3, :] * dyc + cw[1:2, :] * d1 + cw[0:1, :] * d2
            dp_ref[1, rows, :] = (dy_ref[rows, :] * yc).astype(dp_ref.dtype)
            dp_ref[2, rows, :] = (dz * u_ref[rows, :]).astype(dp_ref.dtype)
            dp_ref[3, rows, :] = (dz * c_ref[rows, :]).astype(dp_ref.dtype)
            dcw = [dcw[0] + dyc * z2, dcw[1] + dyc * z1, dcw[2] + dyc * z]
        dgnw_ref[...] += jnp.sum(dgnw, axis=0, keepdims=True)
        for j in range(3):
            dcw_ref[j:j + 1, :] += jnp.sum(dcw[j], axis=0, keepdims=True)

    half = lambda g: pl.BlockSpec((tb, GROUP), lambda i: (i, g))
    grp = lambda g: pl.BlockSpec((None, tb, GROUP), lambda i: (g, i, 0))
    prev = lambda g: pl.BlockSpec((None, SUBLANES, GROUP), lambda i: (g, jnp.maximum(i * hb - 1, 0), 0))
    nxt_row = lambda i: jnp.minimum((i + 1) * hb, t // SUBLANES - 1)
    nxt = lambda g: pl.BlockSpec((None, SUBLANES, GROUP), lambda i: (g, nxt_row(i), 0))
    vec = lambda r: pl.BlockSpec((r, GROUP), lambda i: (0, 0))
    return pl.pallas_call(
        body, name="gate_bwd", grid=(nblk,),
        in_specs=[half(0), half(1), pl.BlockSpec((SUBLANES, GROUP), lambda i: (nxt_row(i), 1)), half(0), grp(3), vec(1),
                  grp(4), nxt(4), grp(5), grp(6), prev(5), prev(6), vec(3), ANY],
        out_specs=[half(0), pl.BlockSpec((4, tb, GROUP), lambda i: (0, i, 0)), vec(1), vec(3)],
        out_shape=[jax.ShapeDtypeStruct((t, HGRN_WIDTH), F32), jax.ShapeDtypeStruct((4, t, HGRN_WIDTH), BF16),
                   jax.ShapeDtypeStruct((1, HGRN_WIDTH), F32), jax.ShapeDtypeStruct((3, CONV_WIDTH), F32)],
        scratch_shapes=[pltpu.VMEM((tb + SUBLANES, GROUP), F32), pltpu.VMEM((tb + SUBLANES, GROUP), F32)],
        compiler_params=_params("arbitrary"),
    )(dcat, dcat, dcat, o, proj, gate_norm_w, proj, proj, proj, proj, proj, proj, conv_w, after)


def _hgrn_bwd(proj, do, states, lb_logits, after):
    t = proj.shape[1]
    tb = min(t, 512)
    ncb = tb // CHUNK
    nblk = t // tb

    def body(q_ref, f_ref, v_ref, do_ref, st_ref, lbl_ref, after_ref, dp_ref, dlbl_ref, ds_scr, dlb_scr):
        i = pl.program_id(0)

        @pl.when(i == 0)
        def _():
            ds_scr[...] = jnp.zeros_like(ds_scr)
            dlb_scr[...] = jnp.zeros_like(dlb_scr)

        lb, s1 = _lower_bound(lbl_ref[...])
        causal, anti = _chunk_masks()
        every = range(ncb)
        rows = [slice(c * CHUNK, (c + 1) * CHUNK) for c in every]
        q, v, do = ([ref[r, :] for r in rows] for ref in (q_ref, v_ref, do_ref))
        st = [st_ref[c] for c in every]
        gates = [_gates(f_ref[r, :], lb) for r in rows]
        sig, f, k = ([gt[n] for gt in gates] for n in (0, 1, 3))
        b = [_dot_exact(causal, gt[2]) for gt in gates]
        mid, last = [x[CHUNK // 2:CHUNK // 2 + 1, :] for x in b], [x[CHUNK - 1:CHUNK, :] for x in b]
        e_q = [jnp.exp(b[c] - mid[c]) for c in every]
        e_k = [jnp.exp(mid[c] - b[c]) for c in every]
        e_i = [jnp.exp(x) for x in b]
        e_s = [jnp.exp(last[c] - b[c]) for c in every]
        dec = [jnp.exp(x) for x in last]
        qt, kt, qi, ks = ([a[c] * e[c] for c in every] for a, e in ((q, e_q), (k, e_k), (q, e_i), (k, e_s)))

        def masked(a, b_):
            return [[jnp.where(causal, _dot(a_h, b_h, NT), 0.0) for a_h, b_h in zip(_heads(a[c]), _heads(b_[c]))]
                    for c in every]

        def with_scores(s, other, dims):
            return [jnp.concatenate([_dot(s_h, o_h, dims) for s_h, o_h in zip(s[c], _heads(other[c]))], axis=1)
                    for c in every]

        def per_head(dims, a, b_):
            return [_per_head(lambda a_h, b_h: _dot(a_h, b_h, dims), a[c], b_[c]) for c in every]

        scores, dscores = masked(qt, kt), masked(do, v)
        dqt, dkt, dv_intra = with_scores(dscores, kt, NN), with_scores(dscores, qt, TN), with_scores(scores, do, TN)
        dqi, update = per_head(NN, do, st), per_head(TN, do, qi)

        dst = ds_scr[...]
        dsts = [None] * ncb
        for c in reversed(every):
            dsts[c] = dst
            dst = dec[c] * dst + update[c]
        ds_scr[...] = dst

        dv_state, dks = per_head(NT, ks, dsts), per_head(NN, v, dsts)
        ddec = [jnp.sum(dsts[c] * st[c], axis=0, keepdims=True) for c in every]
        dq = [dqt[c] * e_q[c] + dqi[c] * e_i[c] for c in every]
        dk = [dkt[c] * e_k[c] + dks[c] * e_s[c] for c in every]
        db = [q[c] * dq[c] - k[c] * dk[c] for c in every]
        db_last = [jnp.sum(dks[c] * ks[c], axis=0, keepdims=True) + ddec[c] * dec[c] for c in every]
        dg = [_dot_exact(anti, db[c]) + db_last[c] for c in every]
        df = [dg[c] / f[c] - dk[c] for c in every]
        dlb_scr[...] += sum(jnp.sum(df[c] * (1.0 - sig[c]), axis=0, keepdims=True) for c in every)
        dfp = [df[c] * (1.0 - lb) * sig[c] * (1.0 - sig[c]) for c in every]
        dv = [dv_intra[c] + dv_state[c] for c in every]
        for n, parts in enumerate((dq, dfp, dv)):
            dp_ref[n] = jnp.concatenate(parts, axis=0).astype(dp_ref.dtype)

        @pl.when(i == nblk - 1)
        def _():
            dlb = dlb_scr[...]
            dlbl_ref[0:1, :] = dlb * lb * (1.0 - lb)
            dlbl_ref[1:2, :] = -dlb * lb * s1

    grp = lambda g: pl.BlockSpec((None, tb, GROUP), lambda i: (g, nblk - 1 - i, 0))
    vec = pl.BlockSpec((2, HGRN_WIDTH), lambda i: (0, 0))
    return pl.pallas_call(
        body, name="hgrn_bwd", grid=(nblk,),
        in_specs=[grp(0), grp(1), grp(2), pl.BlockSpec((tb, HGRN_WIDTH), lambda i: (nblk - 1 - i, 0)),
                  pl.BlockSpec((ncb, HEAD_DIM, HGRN_WIDTH), lambda i: (nblk - 1 - i, 0, 0)), vec, ANY],
        out_specs=[pl.BlockSpec((3, tb, HGRN_WIDTH), lambda i: (0, nblk - 1 - i, 0)), vec],
        out_shape=[jax.ShapeDtypeStruct((3, t, HGRN_WIDTH), BF16), jax.ShapeDtypeStruct((2, HGRN_WIDTH), F32)],
        scratch_shapes=[pltpu.VMEM((HEAD_DIM, HGRN_WIDTH), F32), pltpu.VMEM((1, HGRN_WIDTH), F32)],
        compiler_params=_params("arbitrary"),
    )(proj, proj, proj, do, states, lb_logits, after)


def _in_bwd(dph, dpg, w_in, dpre1, after):
    t = dpre1.shape[0]
    tm = min(t, 512)

    def body(dh_ref, dg_ref, w_ref, dp_ref, after_ref, o_ref):
        acc = ALPHA * dp_ref[...]
        for g in range(N_GROUPS):
            part = dh_ref[g] if g < 3 else dg_ref[g - 3]
            acc = acc + _dot(part, w_ref[:, g * GROUP:(g + 1) * GROUP], NT)
        o_ref[...] = acc

    row = pl.BlockSpec((tm, D_MODEL), lambda i: (i, 0))
    return pl.pallas_call(
        body, name="in_bwd", grid=(t // tm,),
        in_specs=[pl.BlockSpec((3, tm, GROUP), lambda i: (0, i, 0)), pl.BlockSpec((4, tm, GROUP), lambda i: (0, i, 0)),
                  _resident((D_MODEL, IN_COLS)), row, ANY],
        out_specs=row,
        out_shape=jax.ShapeDtypeStruct((t, D_MODEL), F32),
        compiler_params=_params("parallel"),
    )(dph, dpg, w_in, dpre1, after)


def _grad_w(name, operands, widths, shape, step, after=None):
    t = operands[0].shape[-2]
    tt = min(t, 512)
    n_in, n_steps = len(operands), t // tt
    in_specs = [pl.BlockSpec((tt, w), lambda k: (k, 0)) if a.ndim == 2 else
                pl.BlockSpec((a.shape[0], tt, w), lambda k: (0, k, 0)) for a, w in zip(operands, widths)]
    extra = [] if after is None else [after]

    def body(*refs):
        o_ref, acc, narrow, sem = refs[-4:]
        k = pl.program_id(0)

        @pl.when(k == 0)
        def _():
            acc[...] = jnp.zeros_like(acc)

        step(acc, *refs[:n_in])

        @pl.when(k == n_steps - 1)
        def _():
            narrow[...] = acc[...].astype(narrow.dtype)
            out = pltpu.make_async_copy(narrow, o_ref, sem)
            out.start()
            out.wait()

    return pl.pallas_call(
        body, name=name, grid=(n_steps,), in_specs=in_specs + [ANY] * len(extra), out_specs=ANY,
        out_shape=jax.ShapeDtypeStruct(shape, BF16),
        scratch_shapes=[pltpu.VMEM(shape, F32), pltpu.VMEM(shape, BF16), pltpu.SemaphoreType.DMA],
        compiler_params=_params("arbitrary"),
    )(*operands, *extra)


def _dw_in(xb, dph, dpg, after):
    def step(acc, x_ref, dh_ref, dg_ref):
        xv = x_ref[...]
        for g in range(N_GROUPS):
            part = dh_ref[g] if g < 3 else dg_ref[g - 3]
            acc[:, g * GROUP:(g + 1) * GROUP] += _dot(xv, part, TN)

    return _grad_w("dw_in", (xb, dph, dpg), (D_MODEL, GROUP, GROUP), (D_MODEL, IN_COLS), step, after)


def _dw_out(cat, dpre1b):
    def step(acc, cat_ref, d_ref):
        dv = d_ref[...]
        for g in range(2):
            acc[g * GROUP:(g + 1) * GROUP, :] += _dot(cat_ref[g], dv, TN)

    return _grad_w("dw_out", (cat, dpre1b), (GROUP, D_MODEL), (D_MODEL, D_MODEL), step)


def _dw_ff1(h1b, da):
    def step(acc, h_ref, da_ref):
        hv = h_ref[...]
        for j in range(D_FF // FF_BLOCK):
            cols = slice(j * FF_BLOCK, (j + 1) * FF_BLOCK)
            acc[:, cols] += _dot(hv, da_ref[:, cols], TN)

    return _grad_w("dw_ff1", (h1b, da), (D_MODEL, D_FF), (D_MODEL, D_FF), step)


def _dw_ff2(r, dpre2b):
    def step(acc, r_ref, d_ref):
        dv = d_ref[...]
        for j in range(D_FF // FF_BLOCK):
            rows = slice(j * FF_BLOCK, (j + 1) * FF_BLOCK)
            acc[rows, :] += _dot(r_ref[:, rows], dv, TN)

    return _grad_w("dw_ff2", (r, dpre2b), (D_FF, D_MODEL), (D_FF, D_MODEL), step)


def _place():
    x, y, c = lax.axis_index("x"), lax.axis_index("y"), lax.axis_index("c")
    return x, y, c, 2 * x + y


def _other_chips(x, y):
    return [(1 - x, y), (x, 1 - y), (1 - x, 1 - y)]


def _place_shard(name, w, chip, cols_sharded, after=None):
    rows, cols = w.shape
    tr = min(rows, 256)
    nb = rows // tr
    full = (rows, cols * N_CHIPS) if cols_sharded else (rows * N_CHIPS, cols)
    out_map = (lambda i, s: (i, s[0])) if cols_sharded else (lambda i, s: (s[0] * nb + i, 0))

    def body(s_ref, w_ref, *rest):
        rest[-1][...] = w_ref[...].astype(rest[-1].dtype)

    extra = [] if after is None else [after]
    return pl.pallas_call(
        body, name=name,
        grid_spec=pltpu.PrefetchScalarGridSpec(
            num_scalar_prefetch=1, grid=(nb,),
            in_specs=[pl.BlockSpec((tr, cols), lambda i, s: (i, 0))] + [ANY] * len(extra),
            out_specs=pl.BlockSpec((tr, cols), out_map)),
        out_shape=jax.ShapeDtypeStruct(full, BF16),
        compiler_params=_params("parallel"),
    )(chip, w, *extra)


HBM = pl.BlockSpec(memory_space=pltpu.HBM)
SEM = pl.BlockSpec(memory_space=pltpu.SEMAPHORE)
EFFECT = pltpu.SideEffectType.DATAFLOW_SIDE_EFFECTING


PEER_SETS = {
    "sibling": (0, lambda x, y, c: [(x, y, 1 - c)]),
    "chips": (1, lambda x, y, c: [(1 - x, y, c), (x, 1 - y, c), (1 - x, 1 - y, c)]),
    "neighbours": (2, lambda x, y, c: [(1 - x, y, c), (x, 1 - y, c)]),
}


class _Split:
    def __init__(self, name, arrays, plan, others=(), peers=None):
        n_own, arrays = len(arrays), (*arrays, *others)
        n, n_copies = len(arrays), plan.count
        self.name, self.plan, self.n = name, plan, n_own
        barrier_id, peer_ids = PEER_SETS[peers] if peers else (None, None)

        def body(*refs):
            if peers:
                x, y, c, _ = _place()
                barrier = pltpu.get_barrier_semaphore()
                for peer in peer_ids(x, y, c):
                    pl.semaphore_signal(barrier, inc=1, device_id=peer, device_id_type=MESH)
                pl.semaphore_wait(barrier, len(peer_ids(0, 0, 0)))
            send_sems, recv_sems, token = refs[n], refs[n + 1], refs[-1]
            for k, (src, dst, to) in enumerate(plan(refs[:n])):
                pltpu.make_async_remote_copy(src_ref=src, dst_ref=dst, send_sem=send_sems.at[k], recv_sem=recv_sems.at[k],
                                             device_id=to, device_id_type=MESH).start()
            token[...] = jnp.zeros_like(token)

        outs = pl.pallas_call(
            body, name=name + "_start",
            out_shape=(pltpu.SemaphoreType.DMA((n_copies,)), pltpu.SemaphoreType.DMA((n_copies,)),
                       *[pltpu.HBM(a.shape, a.dtype) for a in arrays], jax.ShapeDtypeStruct((SUBLANES, LANES), F32)),
            in_specs=(HBM,) * n, out_specs=(SEM, SEM) + (HBM,) * n + (pl.BlockSpec(memory_space=pltpu.VMEM),),
            input_output_aliases={i: 2 + i for i in range(n)},
            compiler_params=pltpu.CompilerParams(has_side_effects=EFFECT, collective_id=barrier_id),
        )(*[pltpu.with_memory_space_constraint(a, pltpu.HBM) for a in arrays])
        self.sems, self.arrays, self.others, self.token = outs[:2], outs[2:2 + n_own], outs[2 + n_own:2 + n], outs[-1]

    def wait(self, after):
        n, plan = self.n, self.plan

        def body(*refs):
            send_sems, recv_sems = refs[n], refs[n + 1]
            for k, (src, dst, to) in enumerate(plan(refs[:n])):
                cp = pltpu.make_async_remote_copy(src_ref=src, dst_ref=dst, send_sem=send_sems.at[k],
                                                  recv_sem=recv_sems.at[k], device_id=to, device_id_type=MESH)
                cp.wait_send()
                cp.wait_recv()

        return pl.pallas_call(
            body, name=self.name + "_wait", out_shape=tuple(pltpu.HBM(a.shape, a.dtype) for a in self.arrays),
            in_specs=(HBM,) * n + (SEM, SEM, ANY), out_specs=(HBM,) * n, input_output_aliases={i: i for i in range(n)},
            compiler_params=pltpu.CompilerParams(has_side_effects=EFFECT),
        )(*self.arrays, *self.sems, after)


COLS_SHARDED = (True, False, True, False)
HALF_SHAPES = [(D_MODEL // 2, IN_COLS), (D_MODEL, D_MODEL // 2), (D_MODEL // 2, D_FF), (D_FF, D_MODEL // 2)]
PIECE_SHAPES = [(D_MODEL // 2, IN_COLS // N_CHIPS), (D_MODEL // N_CHIPS, D_MODEL // 2),
                (D_MODEL // 2, D_FF // N_CHIPS), (D_FF // N_CHIPS, D_MODEL // 2)]


def _shard_view(kind, ref, chip):
    if COLS_SHARDED[kind]:
        n = ref.shape[1] // N_CHIPS
        return ref.at[:, pl.ds(chip * n, n)]
    n = ref.shape[0] // N_CHIPS
    return ref.at[pl.ds(chip * n, n), :]


def _half_view(kind, ref, h):
    if COLS_SHARDED[kind]:
        n = ref.shape[0] // 2
        return ref.at[pl.ds(h * n, n), :]
    n = ref.shape[1] // 2
    return ref.at[:, pl.ds(h * n, n)]


def _plan(count):
    def mark(fn):
        fn.count = count
        return fn
    return mark


def _shard_rows_view(kind, ref, chip, part, n_parts):
    if COLS_SHARDED[kind]:
        m, n = ref.shape[0] // n_parts, ref.shape[1] // N_CHIPS
        return ref.at[pl.ds(part * m, m), pl.ds(chip * n, n)]
    m = ref.shape[0] // N_CHIPS // n_parts
    return ref.at[pl.ds((n_parts * chip + part) * m, m), :]


def _shard_half_view(kind, ref, chip, h):
    return _shard_rows_view(kind, ref, chip, h, 2)


def _gather_over_ici(kinds, weights):
    @_plan(2 * len(kinds))
    def plan(refs):
        x, y, c, me = _place()
        mine = [_shard_half_view(kind, ref, me, c) for kind, ref in zip(kinds, refs)]
        return [(v, v, to) for v in mine for to in ((1 - x, y, c), (x, 1 - y, c))]

    return _Split("gather_ici_" + "".join(map(str, kinds)), tuple(weights), plan, peers="neighbours")


def _relay_over_ici(kinds, weights, others=()):
    @_plan(2 * len(kinds))
    def plan(refs):
        x, y, c, _ = _place()
        x_nbr, y_nbr = 2 * (1 - x) + y, 2 * x + (1 - y)
        out = []
        for kind, ref in zip(kinds, refs):
            first, second = (_shard_rows_view(kind, ref, chip, 2 * c + q, 4) for q, chip in ((0, x_nbr), (1, y_nbr)))
            out += [(first, first, (x, 1 - y, c)), (second, second, (1 - x, y, c))]
        return out

    return _Split("relay_ici_" + "".join(map(str, kinds)), tuple(weights), plan, others, peers="neighbours")


def _gather_w_in_over_ici(w_in, conv4):
    @_plan(6)
    def plan(refs):
        x, y, c, me = _place()
        half, conv = _shard_half_view(0, refs[0], me, c), refs[1].at[me]
        return [(v, v, (px, py, c)) for v in (half, conv) for px, py in _other_chips(x, y)]

    return _Split("gather_w_in_ici", (w_in, conv4), plan, peers="chips")


def _gather_over_d2d(kinds, weights):
    @_plan(3 * len(kinds))
    def plan(refs):
        x, y, c, _ = _place()
        got = [_shard_half_view(kind, ref, 2 * px + py, c) for kind, ref in zip(kinds, refs)
               for px, py in _other_chips(x, y)]
        return [(v, v, (x, y, 1 - c)) for v in got]

    return _Split("gather_d2d_" + "".join(map(str, kinds)), tuple(weights), plan, peers="sibling")


def _swap_halves(kinds, grads):
    @_plan(len(kinds))
    def plan(refs):
        x, y, c, _ = _place()
        return [(_half_view(kind, g, 1 - c), land, (x, y, 1 - c))
                for kind, g, land in zip(kinds, refs[:len(kinds)], refs[len(kinds):])]

    lands = [lax.empty(HALF_SHAPES[kind], g.dtype) for kind, g in zip(kinds, grads)]
    return _Split("swap_halves_" + "".join(map(str, kinds)), (*grads, *lands), plan, peers="sibling")


def _add_half(name, g, recv, core, rows_split):
    shape = recv.shape
    tr = min(shape[0], 128 if rows_split else 256)
    nb = shape[0] // tr

    def body(c_ref, g_ref, r_ref, o_ref):
        o_ref[...] = (g_ref[...].astype(F32) + r_ref[...].astype(F32)).astype(o_ref.dtype)

    g_map = (lambda i, c_ref: (c_ref[0] * nb + i, 0)) if rows_split else (lambda i, c_ref: (i, c_ref[0]))
    blk = pl.BlockSpec((tr, shape[1]), lambda i, c_ref: (i, 0))
    return pl.pallas_call(
        body, name=name,
        grid_spec=pltpu.PrefetchScalarGridSpec(
            num_scalar_prefetch=1, grid=(nb,),
            in_specs=[pl.BlockSpec((tr, shape[1]), g_map), blk], out_specs=blk),
        out_shape=jax.ShapeDtypeStruct(shape, BF16),
        compiler_params=_params("parallel"),
    )(core, g, recv)


def _exchange_pieces(kinds, halves, pack=None):
    n_p, n = N_CHIPS - 1, len(kinds)

    @_plan(n_p * n + (0 if pack is None else N_DEV - 1))
    def plan(refs):
        x, y, c, _ = _place()
        copies = []
        if pack is not None:
            me = 4 * x + 2 * y + c
            peers = [((1 - x) if m & 4 else x, (1 - y) if m & 2 else y, (1 - c) if m & 1 else c) for m in range(1, N_DEV)]
            copies += [(refs[2 * n], refs[2 * n + 1].at[me], peer) for peer in peers]
        return copies + [(_shard_view(kind, half, 2 * px + py), land.at[j], (px, py, c))
                         for j, (px, py) in enumerate(_other_chips(x, y))
                         for kind, half, land in zip(kinds, refs[:n], refs[n:2 * n])]

    lands = [lax.empty((n_p,) + PIECE_SHAPES[kind], BF16) for kind in kinds]
    small = () if pack is None else (pack, lax.empty((N_DEV,) + pack.shape, F32))
    return _Split("exchange_pieces_" + "".join(map(str, kinds)), (*halves, *lands, *small), plan,
                  peers="chips" if pack is None else None)


def _sum_pieces(name, half, slots, place, rows_split, after):
    n_p, rows, cols = slots.shape
    tr = min(rows, 256)
    nb = rows // tr
    if rows_split:
        own_map = lambda i, s: (i, s[0])
        out_map = lambda i, s: (s[1] * nb + i, 0)
        shard = (2 * rows, cols)
    else:
        own_map = lambda i, s: (s[0] * nb + i, 0)
        out_map = lambda i, s: (i, s[1])
        shard = (rows, 2 * cols)

    def body(s_ref, own_ref, slot_ref, after_ref, o_ref):
        total = own_ref[...].astype(F32)
        for j in range(n_p):
            total = total + slot_ref[j].astype(F32)
        o_ref[...] = total

    return pl.pallas_call(
        body, name=name,
        grid_spec=pltpu.PrefetchScalarGridSpec(
            num_scalar_prefetch=1, grid=(nb,),
            in_specs=[pl.BlockSpec((tr, cols), own_map), pl.BlockSpec((n_p, tr, cols), lambda i, s: (0, i, 0)), ANY],
            out_specs=pl.BlockSpec((tr, cols), out_map)),
        out_shape=jax.ShapeDtypeStruct(shard, F32),
        compiler_params=_params("parallel"),
    )(place, half, slots, after)


def _join_halves(kinds, shards):
    @_plan(len(kinds))
    def plan(refs):
        x, y, c, _ = _place()
        return [(_half_view(kind, g, c), _half_view(kind, g, c), (x, y, 1 - c)) for kind, g in zip(kinds, refs)]

    return _Split("join_halves_" + "".join(map(str, kinds)), tuple(shards), plan, peers="sibling")


N_DEV = 8


def _sum_shared(pack, land, device):
    def body(d_ref, p_ref, l_ref, o_ref):
        me = d_ref[0]
        total = jnp.where(me == 0, p_ref[...], l_ref[0])
        for d in range(1, N_DEV):
            total = total + jnp.where(me == d, p_ref[...], l_ref[d])
        o_ref[...] = total

    return pl.pallas_call(
        body, name="sum_shared",
        grid_spec=pltpu.PrefetchScalarGridSpec(
            num_scalar_prefetch=1, grid=(1,),
            in_specs=[pl.BlockSpec(pack.shape, lambda i, d: (0, 0)), pl.BlockSpec(land.shape, lambda i, d: (0, 0, 0))],
            out_specs=pl.BlockSpec(pack.shape, lambda i, d: (0, 0))),
        out_shape=jax.ShapeDtypeStruct(pack.shape, F32),
    )(device, pack, land)


def _adamw(name, w, g, m, v, after=None):
    rows, cols = w.shape
    tr = min(rows, 256)
    extra = [] if after is None else [after]

    def body(w_ref, g_ref, m_ref, v_ref, *rest):
        d_ref, nm_ref, nv_ref = rest[-3:]
        d_ref[...], nm_ref[...], nv_ref[...] = _adam_step(w_ref[...], g_ref[...], m_ref[...], v_ref[...])

    blk = pl.BlockSpec((tr, cols), lambda i: (i, 0))
    return pl.pallas_call(
        body, name=name, grid=(rows // tr,), in_specs=[blk] * 4 + [ANY] * len(extra), out_specs=[blk] * 3,
        out_shape=[jax.ShapeDtypeStruct(w.shape, F32)] * 3,
        compiler_params=_params("parallel"),
    )(w, g, m, v, *extra)


def _adam_step(w, g, m, v):
    nm = ADAM_B1 * m + (1.0 - ADAM_B1) * g
    nv = ADAM_B2 * v + (1.0 - ADAM_B2) * jnp.square(g)
    m_hat = nm * (1.0 / (1.0 - ADAM_B1 ** ADAM_STEP))
    v_hat = nv * (1.0 / (1.0 - ADAM_B2 ** ADAM_STEP))
    return -ADAM_LR * (m_hat / (jnp.sqrt(v_hat) + ADAM_EPS) + ADAM_WD * w), nm, nv


def _adamw_small(tot, chip, weights, ms, vs, after):
    n, half = len(weights), D_MODEL // 2

    def body(chip_ref, tot_ref, *refs):
        ins, outs = refs[:3 * n], refs[3 * n + 1:]
        tot = tot_ref[...]
        conv_all = jnp.concatenate([tot[5:6, half:], tot[6:7, :half], tot[6:7, half:]], axis=0)
        conv = sum(jnp.where(chip_ref[0] == s, conv_all[:, s * LANES:(s + 1) * LANES], 0.0) for s in range(N_CHIPS))
        grads = [jnp.concatenate([tot[4:5, :half], tot[4:5, half:]], axis=0), tot[5:6, :half], conv,
                 tot[0:1], tot[1:2], tot[2:3], tot[3:4]]
        for k, g in enumerate(grads):
            delta, nm, nv = _adam_step(ins[k][...], g, ins[n + k][...], ins[2 * n + k][...])
            outs[k][...], outs[n + k][...], outs[2 * n + k][...], outs[3 * n + k][...] = g, delta, nm, nv

    whole = lambda a: pl.BlockSpec(a.shape, lambda i, s: (0,) * a.ndim)
    arrays = (*weights, *ms, *vs)
    return pl.pallas_call(
        body, name="adamw_small",
        grid_spec=pltpu.PrefetchScalarGridSpec(
            num_scalar_prefetch=1, grid=(1,), in_specs=[whole(tot)] + [whole(a) for a in arrays] + [ANY],
            out_specs=[whole(a) for a in weights] * 4),
        out_shape=[jax.ShapeDtypeStruct(a.shape, F32) for a in weights] * 4,
    )(chip, tot, *arrays, after)


def kernel(x, w_in, lb_logits, gate_norm_w, conv_w, w_out, ln1_g, ln1_b, w_ff1, w_ff2, ln2_g, ln2_b, loss_target, m_w_in, m_lb_logits, m_gate_norm_w, m_conv_w, m_w_out, m_ln1_g, m_ln1_b, m_w_ff1, m_w_ff2, m_ln2_g, m_ln2_b, v_w_in, v_lb_logits, v_gate_norm_w, v_conv_w, v_w_out, v_ln1_g, v_ln1_b, v_w_ff1, v_w_ff2, v_ln2_g, v_ln2_b):
    xs, tgt = x[0], loss_target[0]
    chip = 2 * lax.axis_index("x") + lax.axis_index("y")
    core = lax.axis_index("c").astype(jnp.int32).reshape(1)
    chip1 = chip.astype(jnp.int32).reshape(1)
    place = jnp.concatenate([chip1, core])

    conv4 = lax.dynamic_update_slice(jnp.zeros((N_CHIPS,) + conv_w.shape[1:], F32), conv_w, (chip, 0, 0))
    ici_in = _gather_w_in_over_ici(_place_shard("place_w_in", w_in[0], chip1, True), conv4)
    rest = (1, 2, 3)
    ici_rest = _gather_over_ici(rest, (_place_shard("place_w_out", w_out[0], chip1, False, after=ici_in.token),
                                       _place_shard("place_w_ff1", w_ff1[0], chip1, True, after=ici_in.token),
                                       _place_shard("place_w_ff2", w_ff2[0], chip1, False, after=ici_in.token)))
    wb_in, cv4 = ici_in.wait(ici_rest.token)
    d2d_in = _gather_over_d2d((0,), (wb_in,))
    wb_in, = d2d_in.wait(d2d_in.token)
    conv_full = cv4.transpose(1, 0, 2).reshape(3, CONV_WIDTH)

    proj, xb = _in_proj(xs, wb_in, ici_rest.token)
    relay_rest = _relay_over_ici(rest, ici_rest.wait(proj))
    o, states = _hgrn_fwd(proj, lb_logits, relay_rest.token)
    d2d_rest = _gather_over_d2d(rest, relay_rest.wait(o))
    cat = _gate_fwd(proj, o, gate_norm_w, conv_full, d2d_rest.token)
    wb_out, wb_ff1, wb_ff2 = d2d_rest.wait(cat)

    (h1b, r, da, dpre2b, dpre1, dpre1b, dcat, g_ln1_g, g_ln1_b, g_ln2_g, g_ln2_b, loss8) = _sublayers(
        cat, xs, tgt, wb_out, wb_ff1, wb_ff2, ln1_g, ln1_b, ln2_g, ln2_b)

    names = ("w_in", "w_out", "w_ff1", "w_ff2")

    def add_halves(kinds, grads, lands):
        return [_add_half("add_half_" + names[k], g, ld, core, COLS_SHARDED[k]) for k, g, ld in zip(kinds, grads, lands)]

    def sum_pieces(kinds, halves, lands, after):
        return [_sum_pieces("sum_pieces_" + names[k], h, ld, place, COLS_SHARDED[k], after)
                for k, h, ld in zip(kinds, halves, lands)]

    early = (1, 2, 3)
    swap = _swap_halves(early, (_dw_out(cat, dpre1b), _dw_ff1(h1b, da), _dw_ff2(r, dpre2b)))
    do, dpg, g_gnw, g_conv = _gate_bwd(dcat, o, proj, gate_norm_w, conv_full, swap.token)
    swapped = swap.wait(do)
    exch = _exchange_pieces(early, add_halves(early, swapped[:3], swapped[3:]))
    dph, g_lbl = _hgrn_bwd(proj, do, states, lb_logits, exch.token)
    g_in_local = _dw_in(xb, dph, dpg, dph)

    late = (0,)
    swap = _swap_halves(late, (g_in_local,))
    grad_x = _in_bwd(dph, dpg, wb_in, dpre1, swap.token)
    exchanged = exch.wait(grad_x)
    pack = jnp.concatenate([
        g_ln1_g, g_ln1_b, g_ln2_g, g_ln2_b,
        jnp.concatenate([g_lbl[0:1], g_lbl[1:2]], axis=1),
        jnp.concatenate([g_gnw, g_conv[0:1]], axis=1),
        jnp.concatenate([g_conv[1:2], g_conv[2:3]], axis=1),
        jnp.concatenate([loss8[0:1], jnp.zeros((1, D_MODEL - LANES), F32)], axis=1)], axis=0)
    swapped = swap.wait(exchanged[0])
    exch = _exchange_pieces(late, add_halves(late, swapped[:1], swapped[1:]), pack)
    join = _join_halves(early, sum_pieces(early, exchanged[:3], exchanged[3:], exch.token))
    g_w_out, g_w_ff1, g_w_ff2 = join.wait(join.token)
    d_ff1, nm_ff1, nv_ff1 = _adamw("adamw_w_ff1", w_ff1[0], g_w_ff1, m_w_ff1[0], v_w_ff1[0])
    d_ff2, nm_ff2, nv_ff2 = _adamw("adamw_w_ff2", w_ff2[0], g_w_ff2, m_w_ff2[0], v_w_ff2[0], d_ff1)
    d_out, nm_out, nv_out = _adamw("adamw_w_out", w_out[0], g_w_out, m_w_out[0], v_w_out[0], d_ff2)
    exchanged = exch.wait(d_out)
    tot = _sum_shared(exchanged[2], exchanged[3], 2 * chip1 + core)
    loss = tot[7, 0]
    join = _join_halves(late, sum_pieces(late, exchanged[:1], exchanged[1:2], tot))
    small = ("lb_logits", "gate_norm_w", "conv_w", "ln1_g", "ln1_b", "ln2_g", "ln2_b")
    small_out = _adamw_small(
        tot, chip1, (lb_logits, gate_norm_w, conv_w[0], ln1_g, ln1_b, ln2_g, ln2_b),
        (m_lb_logits, m_gate_norm_w, m_conv_w[0], m_ln1_g, m_ln1_b, m_ln2_g, m_ln2_b),
        (v_lb_logits, v_gate_norm_w, v_conv_w[0], v_ln1_g, v_ln1_b, v_ln2_g, v_ln2_b), join.token)
    g_w_in, = join.wait(small_out[0])
    d_in, nm_in, nv_in = _adamw("adamw_w_in", w_in[0], g_w_in, m_w_in[0], v_w_in[0])

    def results(n_kind, large):
        out = dict(zip(small, small_out[n_kind * len(small):(n_kind + 1) * len(small)]))
        out["conv_w"] = out["conv_w"][None]
        out.update({name: a[None] for name, a in zip(("w_in", "w_out", "w_ff1", "w_ff2"), large)})
        return [out[name] for name in ("w_in", "lb_logits", "gate_norm_w", "conv_w", "w_out", "ln1_g", "ln1_b",
                                       "w_ff1", "w_ff2", "ln2_g", "ln2_b")]

    return (loss, grad_x[None], *results(0, (g_w_in, g_w_out, g_w_ff1, g_w_ff2)),
            *results(1, (d_in, d_out, d_ff1, d_ff2)), *results(2, (nm_in, nm_out, nm_ff1, nm_ff2)),
            *results(3, (nv_in, nv_out, nv_ff1, nv_ff2)))
```

```python
import jax
import jax.numpy as jnp
from jax import lax
from jax.experimental import pallas as pl
from jax.experimental.pallas import tpu as pltpu

F32 = jnp.float32
BF16 = jnp.bfloat16
MXU_DTYPE = jnp.bfloat16

D_MODEL = 1024
HGRN_WIDTH = 512
HEAD_DIM = 128
N_HEADS = 4
CONV_WIDTH = 512
CHUNK = 64
D_FF = 4096
IN_COLS = 3584
GROUP = 512
N_GROUPS = IN_COLS // GROUP
ALPHA = 2.0 ** 0.25
EPS = 1e-5
N_CHIPS = 4
ADAM_LR, ADAM_B1, ADAM_B2, ADAM_EPS, ADAM_WD, ADAM_STEP = 0.001, 0.9, 0.999, 1e-08, 0.01, 10

LANES = 128
SUBLANES = 8
VMEM_LIMIT = 56 * 1024 * 1024
FF_BLOCK = 1024
N_FF = D_FF // FF_BLOCK
GATE_STRIP = 64

NN = (((1,), (0,)), ((), ()))
NT = (((1,), (1,)), ((), ()))
TN = (((0,), (0,)), ((), ()))
MESH = pl.DeviceIdType.MESH
ANY = pl.BlockSpec(memory_space=pl.ANY)


def _dot(a, b, dims):
    return lax.dot_general(a.astype(MXU_DTYPE), b.astype(MXU_DTYPE), dims, preferred_element_type=F32)


def _dot_exact(ones, v):
    ones = ones.astype(jnp.bfloat16)
    hi = v.astype(jnp.bfloat16)
    rest = v - hi.astype(F32)
    mid = rest.astype(jnp.bfloat16)
    low = (rest - mid.astype(F32)).astype(jnp.bfloat16)
    return sum(lax.dot_general(ones, part, NN, preferred_element_type=F32) for part in (hi, mid, low))


def _params(*sem):
    return pltpu.CompilerParams(dimension_semantics=sem, vmem_limit_bytes=VMEM_LIMIT)


def _resident(shape):
    return pl.BlockSpec(shape, lambda *_: (0,) * len(shape), pipeline_mode=pl.Buffered(1))


def _sigmoid(v):
    return 1.0 / (1.0 + jnp.exp(-v))


def _lower_bound(lbl):
    m = jnp.max(lbl, axis=0, keepdims=True)
    e = jnp.exp(lbl - m)
    s = e / jnp.sum(e, axis=0, keepdims=True)
    return s[0:1, :], s[1:2, :]


def _heads(v):
    return [v[:, h * HEAD_DIM:(h + 1) * HEAD_DIM] for h in range(N_HEADS)]


def _per_head(fn, *arrays):
    return jnp.concatenate([fn(*parts) for parts in zip(*map(_heads, arrays))], axis=1)


def _in_proj(x, w_in, conv_w, after):
    t = x.shape[0]
    tm = min(t, 512)

    def body(x_ref, w_ref, cw_ref, after_ref, o_ref, xb_ref, y_ref, zbuf):
        @pl.when(pl.program_id(0) == 0)
        def _():
            zbuf[tm:tm + SUBLANES, :] = jnp.zeros((SUBLANES, CONV_WIDTH), F32)

        xb = x_ref[...].astype(xb_ref.dtype)
        xb_ref[...] = xb
        for g in range(N_GROUPS):
            o_ref[g] = _dot(xb, w_ref[:, g * GROUP:(g + 1) * GROUP], NN)
        zbuf[0:SUBLANES, :] = zbuf[tm:tm + SUBLANES, :]
        zbuf[SUBLANES:SUBLANES + tm, :] = o_ref[5] * o_ref[6]
        cw = cw_ref[...]
        at = lambda shift: zbuf[shift:shift + tm, :]
        conv = cw[2:3, :] * at(SUBLANES) + cw[1:2, :] * at(SUBLANES - 1) + cw[0:1, :] * at(SUBLANES - 2)
        y_ref[...] = (o_ref[4] * conv).astype(y_ref.dtype)

    return pl.pallas_call(
        body, name="in_proj", grid=(t // tm,),
        in_specs=[pl.BlockSpec((tm, D_MODEL), lambda i: (i, 0)), _resident((D_MODEL, IN_COLS)),
                  pl.BlockSpec((3, CONV_WIDTH), lambda i: (0, 0)), ANY],
        out_specs=[pl.BlockSpec((N_GROUPS, tm, GROUP), lambda i: (0, i, 0)), pl.BlockSpec((tm, D_MODEL), lambda i: (i, 0)),
                   pl.BlockSpec((tm, CONV_WIDTH), lambda i: (i, 0))],
        out_shape=[jax.ShapeDtypeStruct((N_GROUPS, t, GROUP), F32), jax.ShapeDtypeStruct((t, D_MODEL), BF16),
                   jax.ShapeDtypeStruct((t, CONV_WIDTH), BF16)],
        scratch_shapes=[pltpu.VMEM((tm + SUBLANES, CONV_WIDTH), F32)],
        compiler_params=_params("arbitrary"),
    )(x, w_in, conv_w, after)


def _gates(fp, lb):
    sig = _sigmoid(fp)
    f = lb + (1.0 - lb) * sig
    return sig, f, jnp.log(f), 1.0 - f


def _chunk_masks():
    row = lax.broadcasted_iota(jnp.int32, (CHUNK, CHUNK), 0)
    col = lax.broadcasted_iota(jnp.int32, (CHUNK, CHUNK), 1)
    return row >= col, row <= col


def _hgrn_fwd(proj, lb_logits, after):
    t = proj.shape[1]
    tb = min(t, 512)
    ncb = tb // CHUNK

    def body(q_ref, f_ref, v_ref, lbl_ref, after_ref, o_ref, st_ref, s_scr):
        @pl.when(pl.program_id(0) == 0)
        def _():
            s_scr[...] = jnp.zeros_like(s_scr)

        lb, _ = _lower_bound(lbl_ref[...])
        causal, _ = _chunk_masks()

        every = range(ncb)
        rows = [slice(c * CHUNK, (c + 1) * CHUNK) for c in every]
        q, v = [q_ref[r, :] for r in rows], [v_ref[r, :] for r in rows]
        gates = [_gates(f_ref[r, :], lb) for r in rows]
        k = [gt[3] for gt in gates]
        b = [_dot_exact(causal, gt[2]) for gt in gates]
        mid, last = [x[CHUNK // 2:CHUNK // 2 + 1, :] for x in b], [x[CHUNK - 1:CHUNK, :] for x in b]
        qt = [q[c] * jnp.exp(b[c] - mid[c]) for c in every]
        kt = [k[c] * jnp.exp(mid[c] - b[c]) for c in every]
        qi = [q[c] * jnp.exp(b[c]) for c in every]
        ks = [k[c] * jnp.exp(last[c] - b[c]) for c in every]
        dec = [jnp.exp(x) for x in last]
        scores = [[jnp.where(causal, _dot(a, b_, NT), 0.0) for a, b_ in zip(_heads(qt[c]), _heads(kt[c]))] for c in every]
        intra = [[_dot(s, v_h, NN) for s, v_h in zip(scores[c], _heads(v[c]))] for c in every]
        update = [_per_head(lambda v_h, ks_h: _dot(v_h, ks_h, TN), v[c], ks[c]) for c in every]

        st = s_scr[...]
        states = []
        for c in every:
            states.append(st)
            st_ref[c] = st
            st = dec[c] * st + update[c]
        s_scr[...] = st

        o_ref[...] = jnp.concatenate(
            [jnp.concatenate([i_h + _dot(qi_h, st_h, NT) for i_h, qi_h, st_h in
                              zip(intra[c], _heads(qi[c]), _heads(states[c]))], axis=1) for c in every], axis=0)

    grp = lambda g: pl.BlockSpec((None, tb, GROUP), lambda i: (g, i, 0))
    return pl.pallas_call(
        body, name="hgrn_fwd", grid=(t // tb,),
        in_specs=[grp(0), grp(1), grp(2), pl.BlockSpec((2, HGRN_WIDTH), lambda i: (0, 0)), ANY],
        out_specs=[pl.BlockSpec((tb, HGRN_WIDTH), lambda i: (i, 0)),
                   pl.BlockSpec((ncb, HEAD_DIM, HGRN_WIDTH), lambda i: (i, 0, 0))],
        out_shape=[jax.ShapeDtypeStruct((t, HGRN_WIDTH), F32),
                   jax.ShapeDtypeStruct((t // CHUNK, HEAD_DIM, HGRN_WIDTH), F32)],
        scratch_shapes=[pltpu.VMEM((HEAD_DIM, HGRN_WIDTH), F32)],
        compiler_params=_params("arbitrary"),
    )(proj, proj, proj, lb_logits, after)


def _gate_fwd(proj, o, gate_norm_w, after):
    t = proj.shape[1]
    tb = min(t, 512)

    def body(o_ref, og_ref, gnw_ref, after_ref, out_ref):
        gnw = gnw_ref[...]
        for s in range(tb // GATE_STRIP):
            rows = slice(s * GATE_STRIP, (s + 1) * GATE_STRIP)
            og = og_ref[rows, :]
            on = _per_head(lambda o_h: o_h * lax.rsqrt(jnp.mean(o_h * o_h, axis=-1, keepdims=True) + EPS), o_ref[rows, :])
            out_ref[rows, :] = (on * gnw * (og * _sigmoid(og))).astype(out_ref.dtype)

    tile = pl.BlockSpec((tb, GROUP), lambda i: (i, 0))
    return pl.pallas_call(
        body, name="gate_fwd", grid=(t // tb,),
        in_specs=[tile, pl.BlockSpec((None, tb, GROUP), lambda i: (3, i, 0)), pl.BlockSpec((1, GROUP), lambda i: (0, 0)), ANY],
        out_specs=tile,
        out_shape=jax.ShapeDtypeStruct((t, HGRN_WIDTH), BF16),
        compiler_params=_params("parallel"),
    )(o, proj, gate_norm_w, after)


def _ln_bwd(dy, xhat, rstd, g):
    dxhat = dy * g
    m1 = jnp.mean(dxhat, axis=-1, keepdims=True)
    m2 = jnp.mean(dxhat * xhat, axis=-1, keepdims=True)
    return rstd * (dxhat - m1 - xhat * m2)


def _layer_norm(pre):
    xc = pre - jnp.mean(pre, axis=-1, keepdims=True)
    rstd = lax.rsqrt(jnp.mean(xc * xc, axis=-1, keepdims=True) + EPS)
    return xc * rstd, rstd


def _sublayers(cat_h, cat_c, x, target, w_out, w_ff1, w_ff2, g1, b1, g2, b2):
    t = x.shape[0]
    tm = min(t, 256)

    def body(ch_ref, cc_ref, x_ref, tg_ref, wo_ref, w1_ref, w2_ref, g1_ref, b1_ref, g2_ref, b2_ref,
             h1_ref, r_ref, da_ref, dp2b_ref, dp1_ref, dp1b_ref, dcat_ref, dg1_ref, db1_ref, dg2_ref, db2_ref, loss_ref):
        @pl.when(pl.program_id(0) == 0)
        def _():
            for ref in (dg1_ref, db1_ref, dg2_ref, db2_ref, loss_ref):
                ref[...] = jnp.zeros_like(ref)

        mix = _dot(ch_ref[...], wo_ref[0:GROUP, :], NN) + _dot(cc_ref[...], wo_ref[GROUP:2 * GROUP, :], NN)
        xhat1, rstd1 = _layer_norm(ALPHA * x_ref[...] + mix)
        h1 = xhat1 * g1_ref[...] + b1_ref[...]
        h1b = h1.astype(h1_ref.dtype)
        h1_ref[...] = h1b
        mlp = jnp.zeros((tm, D_MODEL), F32)
        for j in range(N_FF):
            cols = slice(j * FF_BLOCK, (j + 1) * FF_BLOCK)
            r = jnp.square(jnp.maximum(_dot(h1b, w1_ref[:, cols], NN), 0.0)).astype(r_ref.dtype)
            r_ref[:, cols] = r
            mlp = mlp + _dot(r, w2_ref[cols, :], NN)
        xhat2, rstd2 = _layer_norm(ALPHA * h1 + mlp)
        err = xhat2 * g2_ref[...] + b2_ref[...] - tg_ref[...]
        loss_ref[...] += 0.5 * jnp.sum(jnp.mean(err * err, axis=-1, keepdims=True))
        dy = err * (1.0 / D_MODEL)
        dg2_ref[...] += jnp.sum(dy * xhat2, axis=0, keepdims=True)
        db2_ref[...] += jnp.sum(dy, axis=0, keepdims=True)
        dp2 = _ln_bwd(dy, xhat2, rstd2, g2_ref[...])
        dp2b = dp2.astype(dp2b_ref.dtype)
        dp2b_ref[...] = dp2b
        back = jnp.zeros((tm, D_MODEL), F32)
        for j in range(N_FF):
            cols = slice(j * FF_BLOCK, (j + 1) * FF_BLOCK)
            dr = _dot(dp2b, w2_ref[cols, :], NT)
            da = (dr * (2.0 * jnp.sqrt(r_ref[:, cols].astype(F32)))).astype(da_ref.dtype)
            da_ref[:, cols] = da
            back = back + _dot(da, w1_ref[:, cols], NT)
        dh1 = ALPHA * dp2 + back
        dg1_ref[...] += jnp.sum(dh1 * xhat1, axis=0, keepdims=True)
        db1_ref[...] += jnp.sum(dh1, axis=0, keepdims=True)
        dp1 = _ln_bwd(dh1, xhat1, rstd1, g1_ref[...])
        dp1b = dp1.astype(dp1b_ref.dtype)
        dp1_ref[...] = dp1
        dp1b_ref[...] = dp1b
        dcat_ref[...] = _dot(dp1b, wo_ref[...], NT)

    row = pl.BlockSpec((tm, D_MODEL), lambda i: (i, 0))
    wide = pl.BlockSpec((tm, D_FF), lambda i: (i, 0))
    vec = pl.BlockSpec((1, D_MODEL), lambda i: (0, 0))
    narrow = lambda dtype: jax.ShapeDtypeStruct((t, D_MODEL), dtype)
    return pl.pallas_call(
        body, name="sublayers", grid=(t // tm,),
        in_specs=[pl.BlockSpec((tm, GROUP), lambda i: (i, 0)), pl.BlockSpec((tm, GROUP), lambda i: (i, 0)), row, row,
                  _resident((D_MODEL, D_MODEL)),
                  _resident((D_MODEL, D_FF)), _resident((D_FF, D_MODEL)), vec, vec, vec, vec],
        out_specs=[row, wide, wide, row, row, row, row, vec, vec, vec, vec,
                   pl.BlockSpec((SUBLANES, LANES), lambda i: (0, 0))],
        out_shape=[narrow(BF16), jax.ShapeDtypeStruct((t, D_FF), BF16), jax.ShapeDtypeStruct((t, D_FF), BF16),
                   narrow(BF16), narrow(F32), narrow(BF16), narrow(F32)]
                  + [jax.ShapeDtypeStruct((1, D_MODEL), F32)] * 4 + [jax.ShapeDtypeStruct((SUBLANES, LANES), F32)],
        compiler_params=_params("arbitrary"),
    )(cat_h, cat_c, x, target, w_out, w_ff1, w_ff2, g1, b1, g2, b2)


def _gate_bwd(dcat, o, proj, gate_norm_w, conv_w, after):
    t = proj.shape[1]
    tb = min(t, 512)
    hb = tb // SUBLANES
    nblk = t // tb

    def body(do2_ref, dy_ref, dyn_ref, o_ref, og_ref, gnw_ref, b_ref, bn_ref, c_ref, u_ref, ch_ref, uh_ref, cw_ref,
             after_ref, do_ref, dp_ref, dgnw_ref, dcw_ref, zbuf, dbuf):
        i = pl.program_id(0)

        @pl.when(i == 0)
        def _():
            dgnw_ref[...] = jnp.zeros_like(dgnw_ref)
            dcw_ref[...] = jnp.zeros_like(dcw_ref)

        zbuf[0:SUBLANES, :] = jnp.where(i > 0, ch_ref[...] * uh_ref[...], 0.0)
        zbuf[SUBLANES:SUBLANES + tb, :] = c_ref[...] * u_ref[...]
        dbuf[0:tb, :] = dy_ref[...] * b_ref[...]
        dbuf[tb:tb + SUBLANES, :] = jnp.where(i < nblk - 1, dyn_ref[...] * bn_ref[...], 0.0)

        gnw, cw = gnw_ref[...], cw_ref[...]
        dgnw, dcw = jnp.zeros((GATE_STRIP, GROUP), F32), [jnp.zeros((GATE_STRIP, GROUP), F32) for _ in range(3)]
        for s in range(tb // GATE_STRIP):
            rows = slice(s * GATE_STRIP, (s + 1) * GATE_STRIP)
            ov, og, do2 = o_ref[rows, :], og_ref[rows, :], do2_ref[rows, :]
            rs = _per_head(lambda o_h: jnp.broadcast_to(
                lax.rsqrt(jnp.mean(o_h * o_h, axis=-1, keepdims=True) + EPS), o_h.shape), ov)
            on = ov * rs
            sg = _sigmoid(og)
            sil = og * sg
            don = do2 * gnw * sil
            dgnw = dgnw + do2 * on * sil
            dp_ref[0, rows, :] = (do2 * on * gnw * (sg * (1.0 + og * (1.0 - sg)))).astype(dp_ref.dtype)
            do_ref[rows, :] = rs * (don - on * _per_head(
                lambda p_h: jnp.broadcast_to(jnp.mean(p_h, axis=-1, keepdims=True), p_h.shape), don * on))

            at = lambda buf, shift: buf[shift + s * GATE_STRIP:shift + (s + 1) * GATE_STRIP, :]
            z, z1, z2 = at(zbuf, SUBLANES), at(zbuf, SUBLANES - 1), at(zbuf, SUBLANES - 2)
            dyc, d1, d2 = at(dbuf, 0), at(dbuf, 1), at(dbuf, 2)
            yc = cw[2:3, :] * z + cw[1:2, :] * z1 + cw[0:1, :] * z2
            dz = cw[2:3, :] * dyc + cw[1:2, :] * d1 + cw[0:1, :] * d2
            dp_ref[1, rows, :] = (dy_ref[rows, :] * yc).astype(dp_ref.dtype)
            dp_ref[2, rows, :] = (dz * u_ref[rows, :]).astype(dp_ref.dtype)
            dp_ref[3, rows, :] = (dz * c_ref[rows, :]).astype(dp_ref.dtype)
            dcw = [dcw[0] + dyc * z2, dcw[1] + dyc * z1, dcw[2] + dyc * z]
        dgnw_ref[...] += jnp.sum(dgnw, axis=0, keepdims=True)
        for j in range(3):
            dcw_ref[j:j + 1, :] += jnp.sum(dcw[j], axis=0, keepdims=True)

    half = lambda g: pl.BlockSpec((tb, GROUP), lambda i: (i, g))
    grp = lambda g: pl.BlockSpec((None, tb, GROUP), lambda i: (g, i, 0))
    prev = lambda g: pl.BlockSpec((None, SUBLANES, GROUP), lambda i: (g, jnp.maximum(i * hb - 1, 0), 0))
    nxt_row = lambda i: jnp.minimum((i + 1) * hb, t // SUBLANES - 1)
    nxt = lambda g: pl.BlockSpec((None, SUBLANES, GROUP), lambda i: (g, nxt_row(i), 0))
    vec = lambda r: pl.BlockSpec((r, GROUP), lambda i: (0, 0))
    return pl.pallas_call(
        body, name="gate_bwd", grid=(nblk,),
        in_specs=[half(0), half(1), pl.BlockSpec((SUBLANES, GROUP), lambda i: (nxt_row(i), 1)), half(0), grp(3), vec(1),
                  grp(4), nxt(4), grp(5), grp(6), prev(5), prev(6), vec(3), ANY],
        out_specs=[half(0), pl.BlockSpec((4, tb, GROUP), lambda i: (0, i, 0)), vec(1), vec(3)],
        out_shape=[jax.ShapeDtypeStruct((t, HGRN_WIDTH), F32), jax.ShapeDtypeStruct((4, t, HGRN_WIDTH), BF16),
                   jax.ShapeDtypeStruct((1, HGRN_WIDTH), F32), jax.ShapeDtypeStruct((3, CONV_WIDTH), F32)],
        scratch_shapes=[pltpu.VMEM((tb + SUBLANES, GROUP), F32), pltpu.VMEM((tb + SUBLANES, GROUP), F32)],
        compiler_params=_params("arbitrary"),
    )(dcat, dcat, dcat, o, proj, gate_norm_w, proj, proj, proj, proj, proj, proj, conv_w, after)


def _hgrn_bwd(proj, do, states, lb_logits, after):
    t = proj.shape[1]
    tb = min(t, 512)
    ncb = tb // CHUNK
    nblk = t // tb

    def body(q_ref, f_ref, v_ref, do_ref, st_ref, lbl_ref, after_ref, dp_ref, dlbl_ref, ds_scr, dlb_scr):
        i = pl.program_id(0)

        @pl.when(i == 0)
        def _():
            ds_scr[...] = jnp.zeros_like(ds_scr)
            dlb_scr[...] = jnp.zeros_like(dlb_scr)

        lb, s1 = _lower_bound(lbl_ref[...])
        causal, anti = _chunk_masks()
        every = range(ncb)
        rows = [slice(c * CHUNK, (c + 1) * CHUNK) for c in every]
        q, v, do = ([ref[r, :] for r in rows] for ref in (q_ref, v_ref, do_ref))
        st = [st_ref[c] for c in every]
        gates = [_gates(f_ref[r, :], lb) for r in rows]
        sig, f, k = ([gt[n] for gt in gates] for n in (0, 1, 3))
        b = [_dot_exact(causal, gt[2]) for gt in gates]
        mid, last = [x[CHUNK // 2:CHUNK // 2 + 1, :] for x in b], [x[CHUNK - 1:CHUNK, :] for x in b]
        e_q = [jnp.exp(b[c] - mid[c]) for c in every]
        e_k = [jnp.exp(mid[c] - b[c]) for c in every]
        e_i = [jnp.exp(x) for x in b]
        e_s = [jnp.exp(last[c] - b[c]) for c in every]
        dec = [jnp.exp(x) for x in last]
        qt, kt, qi, ks = ([a[c] * e[c] for c in every] for a, e in ((q, e_q), (k, e_k), (q, e_i), (k, e_s)))

        def masked(a, b_):
            return [[jnp.where(causal, _dot(a_h, b_h, NT), 0.0) for a_h, b_h in zip(_heads(a[c]), _heads(b_[c]))]
                    for c in every]

        def with_scores(s, other, dims):
            return [jnp.concatenate([_dot(s_h, o_h, dims) for s_h, o_h in zip(s[c], _heads(other[c]))], axis=1)
                    for c in every]

        def per_head(dims, a, b_):
            return [_per_head(lambda a_h, b_h: _dot(a_h, b_h, dims), a[c], b_[c]) for c in every]

        scores, dscores = masked(qt, kt), masked(do, v)
        dqt, dkt, dv_intra = with_scores(dscores, kt, NN), with_scores(dscores, qt, TN), with_scores(scores, do, TN)
        dqi, update = per_head(NN, do, st), per_head(TN, do, qi)

        dst = ds_scr[...]
        dsts = [None] * ncb
        for c in reversed(every):
            dsts[c] = dst
            dst = dec[c] * dst + update[c]
        ds_scr[...] = dst

        dv_state, dks = per_head(NT, ks, dsts), per_head(NN, v, dsts)
        ddec = [jnp.sum(dsts[c] * st[c], axis=0, keepdims=True) for c in every]
        dq = [dqt[c] * e_q[c] + dqi[c] * e_i[c] for c in every]
        dk = [dkt[c] * e_k[c] + dks[c] * e_s[c] for c in every]
        db = [q[c] * dq[c] - k[c] * dk[c] for c in every]
        db_last = [jnp.sum(dks[c] * ks[c], axis=0, keepdims=True) + ddec[c] * dec[c] for c in every]
        dg = [_dot_exact(anti, db[c]) + db_last[c] for c in every]
        df = [dg[c] / f[c] - dk[c] for c in every]
        dlb_scr[...] += sum(jnp.sum(df[c] * (1.0 - sig[c]), axis=0, keepdims=True) for c in every)
        dfp = [df[c] * (1.0 - lb) * sig[c] * (1.0 - sig[c]) for c in every]
        dv = [dv_intra[c] + dv_state[c] for c in every]
        for n, parts in enumerate((dq, dfp, dv)):
            dp_ref[n] = jnp.concatenate(parts, axis=0).astype(dp_ref.dtype)

        @pl.when(i == nblk - 1)
        def _():
            dlb = dlb_scr[...]
            dlbl_ref[0:1, :] = dlb * lb * (1.0 - lb)
            dlbl_ref[1:2, :] = -dlb * lb * s1

    grp = lambda g: pl.BlockSpec((None, tb, GROUP), lambda i: (g, nblk - 1 - i, 0))
    vec = pl.BlockSpec((2, HGRN_WIDTH), lambda i: (0, 0))
    return pl.pallas_call(
        body, name="hgrn_bwd", grid=(nblk,),
        in_specs=[grp(0), grp(1), grp(2), pl.BlockSpec((tb, HGRN_WIDTH), lambda i: (nblk - 1 - i, 0)),
                  pl.BlockSpec((ncb, HEAD_DIM, HGRN_WIDTH), lambda i: (nblk - 1 - i, 0, 0)), vec, ANY],
        out_specs=[pl.BlockSpec((3, tb, HGRN_WIDTH), lambda i: (0, nblk - 1 - i, 0)), vec],
        out_shape=[jax.ShapeDtypeStruct((3, t, HGRN_WIDTH), BF16), jax.ShapeDtypeStruct((2, HGRN_WIDTH), F32)],
        scratch_shapes=[pltpu.VMEM((HEAD_DIM, HGRN_WIDTH), F32), pltpu.VMEM((1, HGRN_WIDTH), F32)],
        compiler_params=_params("arbitrary"),
    )(proj, proj, proj, do, states, lb_logits, after)


def _in_bwd(dph, dpg, w_in, dpre1, after):
    t = dpre1.shape[0]
    tm = min(t, 512)

    def body(dh_ref, dg_ref, w_ref, dp_ref, after_ref, o_ref):
        acc = ALPHA * dp_ref[...]
        for g in range(N_GROUPS):
            part = dh_ref[g] if g < 3 else dg_ref[g - 3]
            acc = acc + _dot(part, w_ref[:, g * GROUP:(g + 1) * GROUP], NT)
        o_ref[...] = acc

    row = pl.BlockSpec((tm, D_MODEL), lambda i: (i, 0))
    return pl.pallas_call(
        body, name="in_bwd", grid=(t // tm,),
        in_specs=[pl.BlockSpec((3, tm, GROUP), lambda i: (0, i, 0)), pl.BlockSpec((4, tm, GROUP), lambda i: (0, i, 0)),
                  _resident((D_MODEL, IN_COLS)), row, ANY],
        out_specs=row,
        out_shape=jax.ShapeDtypeStruct((t, D_MODEL), F32),
        compiler_params=_params("parallel"),
    )(dph, dpg, w_in, dpre1, after)


def _grad_w(name, operands, widths, shape, step, after=None):
    t = operands[0].shape[-2]
    tt = min(t, 512)
    n_in, n_steps = len(operands), t // tt
    in_specs = [pl.BlockSpec((tt, w), lambda k: (k, 0)) if a.ndim == 2 else
                pl.BlockSpec((a.shape[0], tt, w), lambda k: (0, k, 0)) for a, w in zip(operands, widths)]
    extra = [] if after is None else [after]

    def body(*refs):
        o_ref, acc, narrow, sem = refs[-4:]
        k = pl.program_id(0)

        @pl.when(k == 0)
        def _():
            acc[...] = jnp.zeros_like(acc)

        step(acc, *refs[:n_in])

        @pl.when(k == n_steps - 1)
        def _():
            narrow[...] = acc[...].astype(narrow.dtype)
            out = pltpu.make_async_copy(narrow, o_ref, sem)
            out.start()
            out.wait()

    return pl.pallas_call(
        body, name=name, grid=(n_steps,), in_specs=in_specs + [ANY] * len(extra), out_specs=ANY,
        out_shape=jax.ShapeDtypeStruct(shape, BF16),
        scratch_shapes=[pltpu.VMEM(shape, F32), pltpu.VMEM(shape, BF16), pltpu.SemaphoreType.DMA],
        compiler_params=_params("arbitrary"),
    )(*operands, *extra)


def _dw_in(xb, dph, dpg, after):
    def step(acc, x_ref, dh_ref, dg_ref):
        xv = x_ref[...]
        for g in range(N_GROUPS):
            part = dh_ref[g] if g < 3 else dg_ref[g - 3]
            acc[:, g * GROUP:(g + 1) * GROUP] += _dot(xv, part, TN)

    return _grad_w("dw_in", (xb, dph, dpg), (D_MODEL, GROUP, GROUP), (D_MODEL, IN_COLS), step, after)


def _dw_out(cat_h, cat_c, dpre1b):
    def step(acc, h_ref, c_ref, d_ref):
        dv = d_ref[...]
        acc[0:GROUP, :] += _dot(h_ref[...], dv, TN)
        acc[GROUP:2 * GROUP, :] += _dot(c_ref[...], dv, TN)

    return _grad_w("dw_out", (cat_h, cat_c, dpre1b), (GROUP, GROUP, D_MODEL), (D_MODEL, D_MODEL), step)


def _dw_ff1(h1b, da):
    def step(acc, h_ref, da_ref):
        hv = h_ref[...]
        for j in range(D_FF // FF_BLOCK):
            cols = slice(j * FF_BLOCK, (j + 1) * FF_BLOCK)
            acc[:, cols] += _dot(hv, da_ref[:, cols], TN)

    return _grad_w("dw_ff1", (h1b, da), (D_MODEL, D_FF), (D_MODEL, D_FF), step)


def _dw_ff2(r, dpre2b):
    def step(acc, r_ref, d_ref):
        dv = d_ref[...]
        for j in range(D_FF // FF_BLOCK):
            rows = slice(j * FF_BLOCK, (j + 1) * FF_BLOCK)
            acc[rows, :] += _dot(r_ref[:, rows], dv, TN)

    return _grad_w("dw_ff2", (r, dpre2b), (D_FF, D_MODEL), (D_FF, D_MODEL), step)


def _place():
    x, y, c = lax.axis_index("x"), lax.axis_index("y"), lax.axis_index("c")
    return x, y, c, 2 * x + y


def _other_chips(x, y):
    return [(1 - x, y), (x, 1 - y), (1 - x, 1 - y)]


def _place_shard(name, w, chip, cols_sharded, after=None):
    rows, cols = w.shape
    tr = min(rows, 256)
    nb = rows // tr
    full = (rows, cols * N_CHIPS) if cols_sharded else (rows * N_CHIPS, cols)
    out_map = (lambda i, s: (i, s[0])) if cols_sharded else (lambda i, s: (s[0] * nb + i, 0))

    def body(s_ref, w_ref, *rest):
        rest[-1][...] = w_ref[...].astype(rest[-1].dtype)

    extra = [] if after is None else [after]
    return pl.pallas_call(
        body, name=name,
        grid_spec=pltpu.PrefetchScalarGridSpec(
            num_scalar_prefetch=1, grid=(nb,),
            in_specs=[pl.BlockSpec((tr, cols), lambda i, s: (i, 0))] + [ANY] * len(extra),
            out_specs=pl.BlockSpec((tr, cols), out_map)),
        out_shape=jax.ShapeDtypeStruct(full, BF16),
        compiler_params=_params("parallel"),
    )(chip, w, *extra)


HBM = pl.BlockSpec(memory_space=pltpu.HBM)
SEM = pl.BlockSpec(memory_space=pltpu.SEMAPHORE)
EFFECT = pltpu.SideEffectType.DATAFLOW_SIDE_EFFECTING


PEER_SETS = {
    "sibling": (0, lambda x, y, c: [(x, y, 1 - c)]),
    "chips": (1, lambda x, y, c: [(1 - x, y, c), (x, 1 - y, c), (1 - x, 1 - y, c)]),
    "neighbours": (2, lambda x, y, c: [(1 - x, y, c), (x, 1 - y, c)]),
}


class _Split:
    def __init__(self, name, arrays, plan, others=(), peers=None):
        n_own, arrays = len(arrays), (*arrays, *others)
        n, n_copies = len(arrays), plan.count
        self.name, self.plan, self.n = name, plan, n_own
        barrier_id, peer_ids = PEER_SETS[peers] if peers else (None, None)

        def body(*refs):
            if peers:
                x, y, c, _ = _place()
                barrier = pltpu.get_barrier_semaphore()
                for peer in peer_ids(x, y, c):
                    pl.semaphore_signal(barrier, inc=1, device_id=peer, device_id_type=MESH)
                pl.semaphore_wait(barrier, len(peer_ids(0, 0, 0)))
            send_sems, recv_sems, token = refs[n], refs[n + 1], refs[-1]
            for k, (src, dst, to) in enumerate(plan(refs[:n])):
                pltpu.make_async_remote_copy(src_ref=src, dst_ref=dst, send_sem=send_sems.at[k], recv_sem=recv_sems.at[k],
                                             device_id=to, device_id_type=MESH).start()
            token[...] = jnp.zeros_like(token)

        outs = pl.pallas_call(
            body, name=name + "_start",
            out_shape=(pltpu.SemaphoreType.DMA((n_copies,)), pltpu.SemaphoreType.DMA((n_copies,)),
                       *[pltpu.HBM(a.shape, a.dtype) for a in arrays], jax.ShapeDtypeStruct((SUBLANES, LANES), F32)),
            in_specs=(HBM,) * n, out_specs=(SEM, SEM) + (HBM,) * n + (pl.BlockSpec(memory_space=pltpu.VMEM),),
            input_output_aliases={i: 2 + i for i in range(n)},
            compiler_params=pltpu.CompilerParams(has_side_effects=EFFECT, collective_id=barrier_id),
        )(*[pltpu.with_memory_space_constraint(a, pltpu.HBM) for a in arrays])
        self.sems, self.arrays, self.others, self.token = outs[:2], outs[2:2 + n_own], outs[2 + n_own:2 + n], outs[-1]

    def wait(self, after):
        n, plan = self.n, self.plan

        def body(*refs):
            send_sems, recv_sems = refs[n], refs[n + 1]
            for k, (src, dst, to) in enumerate(plan(refs[:n])):
                cp = pltpu.make_async_remote_copy(src_ref=src, dst_ref=dst, send_sem=send_sems.at[k],
                                                  recv_sem=recv_sems.at[k], device_id=to, device_id_type=MESH)
                cp.wait_send()
                cp.wait_recv()

        return pl.pallas_call(
            body, name=self.name + "_wait", out_shape=tuple(pltpu.HBM(a.shape, a.dtype) for a in self.arrays),
            in_specs=(HBM,) * n + (SEM, SEM, ANY), out_specs=(HBM,) * n, input_output_aliases={i: i for i in range(n)},
            compiler_params=pltpu.CompilerParams(has_side_effects=EFFECT),
        )(*self.arrays, *self.sems, after)


COLS_SHARDED = (True, False, True, False)
HALF_SHAPES = [(D_MODEL // 2, IN_COLS), (D_MODEL, D_MODEL // 2), (D_MODEL // 2, D_FF), (D_FF, D_MODEL // 2)]
PIECE_SHAPES = [(D_MODEL // 2, IN_COLS // N_CHIPS), (D_MODEL // N_CHIPS, D_MODEL // 2),
                (D_MODEL // 2, D_FF // N_CHIPS), (D_FF // N_CHIPS, D_MODEL // 2)]


def _shard_view(kind, ref, chip):
    if COLS_SHARDED[kind]:
        n = ref.shape[1] // N_CHIPS
        return ref.at[:, pl.ds(chip * n, n)]
    n = ref.shape[0] // N_CHIPS
    return ref.at[pl.ds(chip * n, n), :]


def _half_view(kind, ref, h):
    if COLS_SHARDED[kind]:
        n = ref.shape[0] // 2
        return ref.at[pl.ds(h * n, n), :]
    n = ref.shape[1] // 2
    return ref.at[:, pl.ds(h * n, n)]


def _plan(count):
    def mark(fn):
        fn.count = count
        return fn
    return mark


def _shard_rows_view(kind, ref, chip, part, n_parts):
    if COLS_SHARDED[kind]:
        m, n = ref.shape[0] // n_parts, ref.shape[1] // N_CHIPS
        return ref.at[pl.ds(part * m, m), pl.ds(chip * n, n)]
    m = ref.shape[0] // N_CHIPS // n_parts
    return ref.at[pl.ds((n_parts * chip + part) * m, m), :]


def _shard_half_view(kind, ref, chip, h):
    return _shard_rows_view(kind, ref, chip, h, 2)


def _gather_over_ici(kinds, weights):
    @_plan(2 * len(kinds))
    def plan(refs):
        x, y, c, me = _place()
        mine = [_shard_half_view(kind, ref, me, c) for kind, ref in zip(kinds, refs)]
        return [(v, v, to) for v in mine for to in ((1 - x, y, c), (x, 1 - y, c))]

    return _Split("gather_ici_" + "".join(map(str, kinds)), tuple(weights), plan, peers="neighbours")


def _relay_over_ici(kinds, weights, others=()):
    @_plan(2 * len(kinds))
    def plan(refs):
        x, y, c, _ = _place()
        x_nbr, y_nbr = 2 * (1 - x) + y, 2 * x + (1 - y)
        out = []
        for kind, ref in zip(kinds, refs):
            first, second = (_shard_rows_view(kind, ref, chip, 2 * c + q, 4) for q, chip in ((0, x_nbr), (1, y_nbr)))
            out += [(first, first, (x, 1 - y, c)), (second, second, (1 - x, y, c))]
        return out

    return _Split("relay_ici_" + "".join(map(str, kinds)), tuple(weights), plan, others, peers="neighbours")


def _gather_w_in_over_ici(w_in, conv4):
    @_plan(6)
    def plan(refs):
        x, y, c, me = _place()
        half, conv = _shard_half_view(0, refs[0], me, c), refs[1].at[me]
        return [(v, v, (px, py, c)) for v in (half, conv) for px, py in _other_chips(x, y)]

    return _Split("gather_w_in_ici", (w_in, conv4), plan, peers="chips")


def _gather_over_d2d(kinds, weights):
    @_plan(3 * len(kinds))
    def plan(refs):
        x, y, c, _ = _place()
        got = [_shard_half_view(kind, ref, 2 * px + py, c) for kind, ref in zip(kinds, refs)
               for px, py in _other_chips(x, y)]
        return [(v, v, (x, y, 1 - c)) for v in got]

    return _Split("gather_d2d_" + "".join(map(str, kinds)), tuple(weights), plan, peers="sibling")


def _swap_halves(kinds, grads):
    @_plan(len(kinds))
    def plan(refs):
        x, y, c, _ = _place()
        return [(_half_view(kind, g, 1 - c), land, (x, y, 1 - c))
                for kind, g, land in zip(kinds, refs[:len(kinds)], refs[len(kinds):])]

    lands = [lax.empty(HALF_SHAPES[kind], g.dtype) for kind, g in zip(kinds, grads)]
    return _Split("swap_halves_" + "".join(map(str, kinds)), (*grads, *lands), plan, peers="sibling")


def _add_half(name, g, recv, core, rows_split):
    shape = recv.shape
    tr = min(shape[0], 128 if rows_split else 256)
    nb = shape[0] // tr

    def body(c_ref, g_ref, r_ref, o_ref):
        o_ref[...] = (g_ref[...].astype(F32) + r_ref[...].astype(F32)).astype(o_ref.dtype)

    g_map = (lambda i, c_ref: (c_ref[0] * nb + i, 0)) if rows_split else (lambda i, c_ref: (i, c_ref[0]))
    blk = pl.BlockSpec((tr, shape[1]), lambda i, c_ref: (i, 0))
    return pl.pallas_call(
        body, name=name,
        grid_spec=pltpu.PrefetchScalarGridSpec(
            num_scalar_prefetch=1, grid=(nb,),
            in_specs=[pl.BlockSpec((tr, shape[1]), g_map), blk], out_specs=blk),
        out_shape=jax.ShapeDtypeStruct(shape, BF16),
        compiler_params=_params("parallel"),
    )(core, g, recv)


def _exchange_pieces(kinds, halves, pack=None):
    n_p, n = N_CHIPS - 1, len(kinds)

    @_plan(n_p * n + (0 if pack is None else N_DEV - 1))
    def plan(refs):
        x, y, c, _ = _place()
        copies = []
        if pack is not None:
            me = 4 * x + 2 * y + c
            peers = [((1 - x) if m & 4 else x, (1 - y) if m & 2 else y, (1 - c) if m & 1 else c) for m in range(1, N_DEV)]
            copies += [(refs[2 * n], refs[2 * n + 1].at[me], peer) for peer in peers]
        return copies + [(_shard_view(kind, half, 2 * px + py), land.at[j], (px, py, c))
                         for j, (px, py) in enumerate(_other_chips(x, y))
                         for kind, half, land in zip(kinds, refs[:n], refs[n:2 * n])]

    lands = [lax.empty((n_p,) + PIECE_SHAPES[kind], BF16) for kind in kinds]
    small = () if pack is None else (pack, lax.empty((N_DEV,) + pack.shape, F32))
    return _Split("exchange_pieces_" + "".join(map(str, kinds)), (*halves, *lands, *small), plan,
                  peers="chips" if pack is None else None)


def _sum_pieces(name, half, slots, place, rows_split, after):
    n_p, rows, cols = slots.shape
    tr = min(rows, 256)
    nb = rows // tr
    if rows_split:
        own_map = lambda i, s: (i, s[0])
        out_map = lambda i, s: (s[1] * nb + i, 0)
        shard = (2 * rows, cols)
    else:
        own_map = lambda i, s: (s[0] * nb + i, 0)
        out_map = lambda i, s: (i, s[1])
        shard = (rows, 2 * cols)

    def body(s_ref, own_ref, slot_ref, after_ref, o_ref):
        total = own_ref[...].astype(F32)
        for j in range(n_p):
            total = total + slot_ref[j].astype(F32)
        o_ref[...] = total

    return pl.pallas_call(
        body, name=name,
        grid_spec=pltpu.PrefetchScalarGridSpec(
            num_scalar_prefetch=1, grid=(nb,),
            in_specs=[pl.BlockSpec((tr, cols), own_map), pl.BlockSpec((n_p, tr, cols), lambda i, s: (0, i, 0)), ANY],
            out_specs=pl.BlockSpec((tr, cols), out_map)),
        out_shape=jax.ShapeDtypeStruct(shard, F32),
        compiler_params=_params("parallel"),
    )(place, half, slots, after)


def _join_halves(kinds, shards):
    @_plan(len(kinds))
    def plan(refs):
        x, y, c, _ = _place()
        return [(_half_view(kind, g, c), _half_view(kind, g, c), (x, y, 1 - c)) for kind, g in zip(kinds, refs)]

    return _Split("join_halves_" + "".join(map(str, kinds)), tuple(shards), plan, peers="sibling")


N_DEV = 8


def _sum_shared(pack, land, device):
    def body(d_ref, p_ref, l_ref, o_ref):
        me = d_ref[0]
        total = jnp.where(me == 0, p_ref[...], l_ref[0])
        for d in range(1, N_DEV):
            total = total + jnp.where(me == d, p_ref[...], l_ref[d])
        o_ref[...] = total

    return pl.pallas_call(
        body, name="sum_shared",
        grid_spec=pltpu.PrefetchScalarGridSpec(
            num_scalar_prefetch=1, grid=(1,),
            in_specs=[pl.BlockSpec(pack.shape, lambda i, d: (0, 0)), pl.BlockSpec(land.shape, lambda i, d: (0, 0, 0))],
            out_specs=pl.BlockSpec(pack.shape, lambda i, d: (0, 0))),
        out_shape=jax.ShapeDtypeStruct(pack.shape, F32),
    )(device, pack, land)


def _adamw(name, w, g, m, v, after=None):
    rows, cols = w.shape
    tr = min(rows, 256)
    extra = [] if after is None else [after]

    def body(w_ref, g_ref, m_ref, v_ref, *rest):
        d_ref, nm_ref, nv_ref = rest[-3:]
        d_ref[...], nm_ref[...], nv_ref[...] = _adam_step(w_ref[...], g_ref[...], m_ref[...], v_ref[...])

    blk = pl.BlockSpec((tr, cols), lambda i: (i, 0))
    return pl.pallas_call(
        body, name=name, grid=(rows // tr,), in_specs=[blk] * 4 + [ANY] * len(extra), out_specs=[blk] * 3,
        out_shape=[jax.ShapeDtypeStruct(w.shape, F32)] * 3,
        compiler_params=_params("parallel"),
    )(w, g, m, v, *extra)


def _adam_step(w, g, m, v):
    nm = ADAM_B1 * m + (1.0 - ADAM_B1) * g
    nv = ADAM_B2 * v + (1.0 - ADAM_B2) * jnp.square(g)
    m_hat = nm * (1.0 / (1.0 - ADAM_B1 ** ADAM_STEP))
    v_hat = nv * (1.0 / (1.0 - ADAM_B2 ** ADAM_STEP))
    return -ADAM_LR * (m_hat / (jnp.sqrt(v_hat) + ADAM_EPS) + ADAM_WD * w), nm, nv


def _adamw_small(tot, chip, weights, ms, vs, after):
    n, half = len(weights), D_MODEL // 2

    def body(chip_ref, tot_ref, *refs):
        ins, outs = refs[:3 * n], refs[3 * n + 1:]
        tot = tot_ref[...]
        conv_all = jnp.concatenate([tot[5:6, half:], tot[6:7, :half], tot[6:7, half:]], axis=0)
        conv = sum(jnp.where(chip_ref[0] == s, conv_all[:, s * LANES:(s + 1) * LANES], 0.0) for s in range(N_CHIPS))
        grads = [jnp.concatenate([tot[4:5, :half], tot[4:5, half:]], axis=0), tot[5:6, :half], conv,
                 tot[0:1], tot[1:2], tot[2:3], tot[3:4]]
        for k, g in enumerate(grads):
            delta, nm, nv = _adam_step(ins[k][...], g, ins[n + k][...], ins[2 * n + k][...])
            outs[k][...], outs[n + k][...], outs[2 * n + k][...], outs[3 * n + k][...] = g, delta, nm, nv

    whole = lambda a: pl.BlockSpec(a.shape, lambda i, s: (0,) * a.ndim)
    arrays = (*weights, *ms, *vs)
    return pl.pallas_call(
        body, name="adamw_small",
        grid_spec=pltpu.PrefetchScalarGridSpec(
            num_scalar_prefetch=1, grid=(1,), in_specs=[whole(tot)] + [whole(a) for a in arrays] + [ANY],
            out_specs=[whole(a) for a in weights] * 4),
        out_shape=[jax.ShapeDtypeStruct(a.shape, F32) for a in weights] * 4,
    )(chip, tot, *arrays, after)


def kernel(x, w_in, lb_logits, gate_norm_w, conv_w, w_out, ln1_g, ln1_b, w_ff1, w_ff2, ln2_g, ln2_b, loss_target, m_w_in, m_lb_logits, m_gate_norm_w, m_conv_w, m_w_out, m_ln1_g, m_ln1_b, m_w_ff1, m_w_ff2, m_ln2_g, m_ln2_b, v_w_in, v_lb_logits, v_gate_norm_w, v_conv_w, v_w_out, v_ln1_g, v_ln1_b, v_w_ff1, v_w_ff2, v_ln2_g, v_ln2_b):
    xs, tgt = x[0], loss_target[0]
    chip = 2 * lax.axis_index("x") + lax.axis_index("y")
    core = lax.axis_index("c").astype(jnp.int32).reshape(1)
    chip1 = chip.astype(jnp.int32).reshape(1)
    place = jnp.concatenate([chip1, core])

    conv4 = lax.dynamic_update_slice(jnp.zeros((N_CHIPS,) + conv_w.shape[1:], F32), conv_w, (chip, 0, 0))
    ici_in = _gather_w_in_over_ici(_place_shard("place_w_in", w_in[0], chip1, True), conv4)
    rest = (1, 2, 3)
    ici_rest = _gather_over_ici(rest, (_place_shard("place_w_out", w_out[0], chip1, False, after=ici_in.token),
                                       _place_shard("place_w_ff1", w_ff1[0], chip1, True, after=ici_in.token),
                                       _place_shard("place_w_ff2", w_ff2[0], chip1, False, after=ici_in.token)))
    wb_in, cv4 = ici_in.wait(ici_rest.token)
    d2d_in = _gather_over_d2d((0,), (wb_in,))
    wb_in, = d2d_in.wait(d2d_in.token)
    conv_full = cv4.transpose(1, 0, 2).reshape(3, CONV_WIDTH)

    proj, xb, cat_c = _in_proj(xs, wb_in, conv_full, ici_rest.token)
    relay_rest = _relay_over_ici(rest, ici_rest.wait(proj))
    o, states = _hgrn_fwd(proj, lb_logits, relay_rest.token)
    d2d_rest = _gather_over_d2d(rest, relay_rest.wait(o))
    cat_h = _gate_fwd(proj, o, gate_norm_w, d2d_rest.token)
    wb_out, wb_ff1, wb_ff2 = d2d_rest.wait(cat_h)

    (h1b, r, da, dpre2b, dpre1, dpre1b, dcat, g_ln1_g, g_ln1_b, g_ln2_g, g_ln2_b, loss8) = _sublayers(
        cat_h, cat_c, xs, tgt, wb_out, wb_ff1, wb_ff2, ln1_g, ln1_b, ln2_g, ln2_b)

    names = ("w_in", "w_out", "w_ff1", "w_ff2")

    def add_halves(kinds, grads, lands):
        return [_add_half("add_half_" + names[k], g, ld, core, COLS_SHARDED[k]) for k, g, ld in zip(kinds, grads, lands)]

    def sum_pieces(kinds, halves, lands, after):
        return [_sum_pieces("sum_pieces_" + names[k], h, ld, place, COLS_SHARDED[k], after)
                for k, h, ld in zip(kinds, halves, lands)]

    early = (1, 2, 3)
    swap = _swap_halves(early, (_dw_out(cat_h, cat_c, dpre1b), _dw_ff1(h1b, da), _dw_ff2(r, dpre2b)))
    do, dpg, g_gnw, g_conv = _gate_bwd(dcat, o, proj, gate_norm_w, conv_full, swap.token)
    swapped = swap.wait(do)
    exch = _exchange_pieces(early, add_halves(early, swapped[:3], swapped[3:]))
    dph, g_lbl = _hgrn_bwd(proj, do, states, lb_logits, exch.token)
    g_in_local = _dw_in(xb, dph, dpg, dph)

    late = (0,)
    swap = _swap_halves(late, (g_in_local,))
    grad_x = _in_bwd(dph, dpg, wb_in, dpre1, swap.token)
    exchanged = exch.wait(grad_x)
    pack = jnp.concatenate([
        g_ln1_g, g_ln1_b, g_ln2_g, g_ln2_b,
        jnp.concatenate([g_lbl[0:1], g_lbl[1:2]], axis=1),
        jnp.concatenate([g_gnw, g_conv[0:1]], axis=1),
        jnp.concatenate([g_conv[1:2], g_conv[2:3]], axis=1),
        jnp.concatenate([loss8[0:1], jnp.zeros((1, D_MODEL - LANES), F32)], axis=1)], axis=0)
    swapped = swap.wait(exchanged[0])
    exch = _exchange_pieces(late, add_halves(late, swapped[:1], swapped[1:]), pack)
    join = _join_halves(early, sum_pieces(early, exchanged[:3], exchanged[3:], exch.token))
    g_w_out, g_w_ff1, g_w_ff2 = join.wait(join.token)
    d_ff1, nm_ff1, nv_ff1 = _adamw("adamw_w_ff1", w_ff1[0], g_w_ff1, m_w_ff1[0], v_w_ff1[0])
    d_ff2, nm_ff2, nv_ff2 = _adamw("adamw_w_ff2", w_ff2[0], g_w_ff2, m_w_ff2[0], v_w_ff2[0], d_ff1)
    d_out, nm_out, nv_out = _adamw("adamw_w_out", w_out[0], g_w_out, m_w_out[0], v_w_out[0], d_ff2)
    exchanged = exch.wait(d_out)
    tot = _sum_shared(exchanged[2], exchanged[3], 2 * chip1 + core)
    loss = tot[7, 0]
    join = _join_halves(late, sum_pieces(late, exchanged[:1], exchanged[1:2], tot))
    small = ("lb_logits", "gate_norm_w", "conv_w", "ln1_g", "ln1_b", "ln2_g", "ln2_b")
    small_out = _adamw_small(
        tot, chip1, (lb_logits, gate_norm_w, conv_w[0], ln1_g, ln1_b, ln2_g, ln2_b),
        (m_lb_logits, m_gate_norm_w, m_conv_w[0], m_ln1_g, m_ln1_b, m_ln2_g, m_ln2_b),
        (v_lb_logits, v_gate_norm_w, v_conv_w[0], v_ln1_g, v_ln1_b, v_ln2_g, v_ln2_b), join.token)
    g_w_in, = join.wait(small_out[0])
    d_in, nm_in, nv_in = _adamw("adamw_w_in", w_in[0], g_w_in, m_w_in[0], v_w_in[0])

    def results(n_kind, large):
        out = dict(zip(small, small_out[n_kind * len(small):(n_kind + 1) * len(small)]))
        out["conv_w"] = out["conv_w"][None]
        out.update({name: a[None] for name, a in zip(("w_in", "w_out", "w_ff1", "w_ff2"), large)})
        return [out[name] for name in ("w_in", "lb_logits", "gate_norm_w", "conv_w", "w_out", "ln1_g", "ln1_b",
                                       "w_ff1", "w_ff2", "ln2_g", "ln2_b")]

    return (loss, grad_x[None], *results(0, (g_w_in, g_w_out, g_w_ff1, g_w_ff2)),
            *results(1, (d_in, d_out, d_ff1, d_ff2)), *results(2, (nm_in, nm_out, nm_ff1, nm_ff2)),
            *results(3, (nv_in, nv_out, nv_ff1, nv_ff2)))
```

```python
import jax
import jax.numpy as jnp
from jax import lax
from jax.experimental import pallas as pl
from jax.experimental.pallas import tpu as pltpu

F32 = jnp.float32
BF16 = jnp.bfloat16
MXU_DTYPE = jnp.bfloat16

D_MODEL = 1024
HGRN_WIDTH = 512
HEAD_DIM = 128
N_HEADS = 4
CONV_WIDTH = 512
CHUNK = 64
D_FF = 4096
IN_COLS = 3584
GROUP = 512
N_GROUPS = IN_COLS // GROUP
ALPHA = 2.0 ** 0.25
EPS = 1e-5
N_CHIPS = 4
ADAM_LR, ADAM_B1, ADAM_B2, ADAM_EPS, ADAM_WD, ADAM_STEP = 0.001, 0.9, 0.999, 1e-08, 0.01, 10

LANES = 128
SUBLANES = 8
VMEM_LIMIT = 56 * 1024 * 1024
FF_BLOCK = 1024
N_FF = D_FF // FF_BLOCK
GATE_STRIP = 64

NN = (((1,), (0,)), ((), ()))
NT = (((1,), (1,)), ((), ()))
TN = (((0,), (0,)), ((), ()))
MESH = pl.DeviceIdType.MESH
ANY = pl.BlockSpec(memory_space=pl.ANY)


def _dot(a, b, dims):
    return lax.dot_general(a.astype(MXU_DTYPE), b.astype(MXU_DTYPE), dims, preferred_element_type=F32)


def _dot_exact(ones, v):
    ones = ones.astype(jnp.bfloat16)
    hi = v.astype(jnp.bfloat16)
    rest = v - hi.astype(F32)
    mid = rest.astype(jnp.bfloat16)
    low = (rest - mid.astype(F32)).astype(jnp.bfloat16)
    return sum(lax.dot_general(ones, part, NN, preferred_element_type=F32) for part in (hi, mid, low))


def _params(*sem):
    return pltpu.CompilerParams(dimension_semantics=sem, vmem_limit_bytes=VMEM_LIMIT)


def _resident(shape):
    return pl.BlockSpec(shape, lambda *_: (0,) * len(shape), pipeline_mode=pl.Buffered(1))


def _sigmoid(v):
    return 1.0 / (1.0 + jnp.exp(-v))


def _lower_bound(lbl):
    m = jnp.max(lbl, axis=0, keepdims=True)
    e = jnp.exp(lbl - m)
    s = e / jnp.sum(e, axis=0, keepdims=True)
    return s[0:1, :], s[1:2, :]


def _heads(v):
    return [v[:, h * HEAD_DIM:(h + 1) * HEAD_DIM] for h in range(N_HEADS)]


def _per_head(fn, *arrays):
    return jnp.concatenate([fn(*parts) for parts in zip(*map(_heads, arrays))], axis=1)


def _in_proj(x, w_in, conv_w, after):
    t = x.shape[0]
    tm = min(t, 512)

    def body(x_ref, w_ref, cw_ref, after_ref, o_ref, xb_ref, y_ref, zbuf):
        @pl.when(pl.program_id(0) == 0)
        def _():
            zbuf[tm:tm + SUBLANES, :] = jnp.zeros((SUBLANES, CONV_WIDTH), F32)

        xb = x_ref[...].astype(xb_ref.dtype)
        xb_ref[...] = xb
        for g in range(N_GROUPS):
            o_ref[g] = _dot(xb, w_ref[:, g * GROUP:(g + 1) * GROUP], NN)
        zbuf[0:SUBLANES, :] = zbuf[tm:tm + SUBLANES, :]
        zbuf[SUBLANES:SUBLANES + tm, :] = o_ref[5] * o_ref[6]
        cw = cw_ref[...]
        at = lambda shift: zbuf[shift:shift + tm, :]
        conv = cw[2:3, :] * at(SUBLANES) + cw[1:2, :] * at(SUBLANES - 1) + cw[0:1, :] * at(SUBLANES - 2)
        y_ref[...] = (o_ref[4] * conv).astype(y_ref.dtype)

    return pl.pallas_call(
        body, name="in_proj", grid=(t // tm,),
        in_specs=[pl.BlockSpec((tm, D_MODEL), lambda i: (i, 0)), _resident((D_MODEL, IN_COLS)),
                  pl.BlockSpec((3, CONV_WIDTH), lambda i: (0, 0)), ANY],
        out_specs=[pl.BlockSpec((N_GROUPS, tm, GROUP), lambda i: (0, i, 0)), pl.BlockSpec((tm, D_MODEL), lambda i: (i, 0)),
                   pl.BlockSpec((tm, CONV_WIDTH), lambda i: (i, 0))],
        out_shape=[jax.ShapeDtypeStruct((N_GROUPS, t, GROUP), F32), jax.ShapeDtypeStruct((t, D_MODEL), BF16),
                   jax.ShapeDtypeStruct((t, CONV_WIDTH), BF16)],
        scratch_shapes=[pltpu.VMEM((tm + SUBLANES, CONV_WIDTH), F32)],
        compiler_params=_params("arbitrary"),
    )(x, w_in, conv_w, after)


def _gates(fp, lb):
    sig = _sigmoid(fp)
    f = lb + (1.0 - lb) * sig
    return sig, f, jnp.log(f), 1.0 - f


def _chunk_masks():
    row = lax.broadcasted_iota(jnp.int32, (CHUNK, CHUNK), 0)
    col = lax.broadcasted_iota(jnp.int32, (CHUNK, CHUNK), 1)
    return row >= col, row <= col


def _hgrn_fwd(proj, lb_logits, after):
    t = proj.shape[1]
    tb = min(t, 512)
    ncb = tb // CHUNK

    def body(q_ref, f_ref, v_ref, lbl_ref, after_ref, o_ref, st_ref, s_scr):
        @pl.when(pl.program_id(0) == 0)
        def _():
            s_scr[...] = jnp.zeros_like(s_scr)

        lb, _ = _lower_bound(lbl_ref[...])
        causal, _ = _chunk_masks()

        every = range(ncb)
        rows = [slice(c * CHUNK, (c + 1) * CHUNK) for c in every]
        q, v = [q_ref[r, :] for r in rows], [v_ref[r, :] for r in rows]
        gates = [_gates(f_ref[r, :], lb) for r in rows]
        k = [gt[3] for gt in gates]
        b = [_dot_exact(causal, gt[2]) for gt in gates]
        mid, last = [x[CHUNK // 2:CHUNK // 2 + 1, :] for x in b], [x[CHUNK - 1:CHUNK, :] for x in b]
        qt = [q[c] * jnp.exp(b[c] - mid[c]) for c in every]
        kt = [k[c] * jnp.exp(mid[c] - b[c]) for c in every]
        qi = [q[c] * jnp.exp(b[c]) for c in every]
        ks = [k[c] * jnp.exp(last[c] - b[c]) for c in every]
        dec = [jnp.exp(x) for x in last]
        scores = [[jnp.where(causal, _dot(a, b_, NT), 0.0) for a, b_ in zip(_heads(qt[c]), _heads(kt[c]))] for c in every]
        intra = [[_dot(s, v_h, NN) for s, v_h in zip(scores[c], _heads(v[c]))] for c in every]
        update = [_per_head(lambda v_h, ks_h: _dot(v_h, ks_h, TN), v[c], ks[c]) for c in every]

        st = s_scr[...]
        states = []
        for c in every:
            states.append(st)
            st_ref[c] = st
            st = dec[c] * st + update[c]
        s_scr[...] = st

        o_ref[...] = jnp.concatenate(
            [jnp.concatenate([i_h + _dot(qi_h, st_h, NT) for i_h, qi_h, st_h in
                              zip(intra[c], _heads(qi[c]), _heads(states[c]))], axis=1) for c in every], axis=0)

    grp = lambda g: pl.BlockSpec((None, tb, GROUP), lambda i: (g, i, 0))
    return pl.pallas_call(
        body, name="hgrn_fwd", grid=(t // tb,),
        in_specs=[grp(0), grp(1), grp(2), pl.BlockSpec((2, HGRN_WIDTH), lambda i: (0, 0)), ANY],
        out_specs=[pl.BlockSpec((tb, HGRN_WIDTH), lambda i: (i, 0)),
                   pl.BlockSpec((ncb, HEAD_DIM, HGRN_WIDTH), lambda i: (i, 0, 0))],
        out_shape=[jax.ShapeDtypeStruct((t, HGRN_WIDTH), F32),
                   jax.ShapeDtypeStruct((t // CHUNK, HEAD_DIM, HGRN_WIDTH), F32)],
        scratch_shapes=[pltpu.VMEM((HEAD_DIM, HGRN_WIDTH), F32)],
        compiler_params=_params("arbitrary"),
    )(proj, proj, proj, lb_logits, after)


def _gate_fwd(proj, o, gate_norm_w, after):
    t = proj.shape[1]
    tb = min(t, 512)

    def body(o_ref, og_ref, gnw_ref, after_ref, out_ref):
        gnw = gnw_ref[...]
        for s in range(tb // GATE_STRIP):
            rows = slice(s * GATE_STRIP, (s + 1) * GATE_STRIP)
            og = og_ref[rows, :]
            on = _per_head(lambda o_h: o_h * lax.rsqrt(jnp.mean(o_h * o_h, axis=-1, keepdims=True) + EPS), o_ref[rows, :])
            out_ref[rows, :] = (on * gnw * (og * _sigmoid(og))).astype(out_ref.dtype)

    tile = pl.BlockSpec((tb, GROUP), lambda i: (i, 0))
    return pl.pallas_call(
        body, name="gate_fwd", grid=(t // tb,),
        in_specs=[tile, pl.BlockSpec((None, tb, GROUP), lambda i: (3, i, 0)), pl.BlockSpec((1, GROUP), lambda i: (0, 0)), ANY],
        out_specs=tile,
        out_shape=jax.ShapeDtypeStruct((t, HGRN_WIDTH), BF16),
        compiler_params=_params("parallel"),
    )(o, proj, gate_norm_w, after)


def _ln_bwd(dy, xhat, rstd, g):
    dxhat = dy * g
    m1 = jnp.mean(dxhat, axis=-1, keepdims=True)
    m2 = jnp.mean(dxhat * xhat, axis=-1, keepdims=True)
    return rstd * (dxhat - m1 - xhat * m2)


def _layer_norm(pre):
    xc = pre - jnp.mean(pre, axis=-1, keepdims=True)
    rstd = lax.rsqrt(jnp.mean(xc * xc, axis=-1, keepdims=True) + EPS)
    return xc * rstd, rstd


def _sublayers(cat_h, cat_c, x, target, w_out, w_ff1, w_ff2, g1, b1, g2, b2):
    t = x.shape[0]
    tm = min(t, 256)

    def body(ch_ref, cc_ref, x_ref, tg_ref, wo_ref, w1_ref, w2_ref, g1_ref, b1_ref, g2_ref, b2_ref,
             h1_ref, r_ref, da_ref, dp2b_ref, dp1_ref, dp1b_ref, dcat_ref, dg1_ref, db1_ref, dg2_ref, db2_ref, loss_ref):
        @pl.when(pl.program_id(0) == 0)
        def _():
            for ref in (dg1_ref, db1_ref, dg2_ref, db2_ref, loss_ref):
                ref[...] = jnp.zeros_like(ref)

        mix = _dot(ch_ref[...], wo_ref[0:GROUP, :], NN) + _dot(cc_ref[...], wo_ref[GROUP:2 * GROUP, :], NN)
        xhat1, rstd1 = _layer_norm(ALPHA * x_ref[...] + mix)
        h1 = xhat1 * g1_ref[...] + b1_ref[...]
        h1b = h1.astype(h1_ref.dtype)
        h1_ref[...] = h1b
        mlp = jnp.zeros((tm, D_MODEL), F32)
        for j in range(N_FF):
            cols = slice(j * FF_BLOCK, (j + 1) * FF_BLOCK)
            r = jnp.square(jnp.maximum(_dot(h1b, w1_ref[:, cols], NN), 0.0)).astype(r_ref.dtype)
            r_ref[:, cols] = r
            mlp = mlp + _dot(r, w2_ref[cols, :], NN)
        xhat2, rstd2 = _layer_norm(ALPHA * h1 + mlp)
        err = xhat2 * g2_ref[...] + b2_ref[...] - tg_ref[...]
        loss_ref[...] += 0.5 * jnp.sum(jnp.mean(err * err, axis=-1, keepdims=True))
        dy = err * (1.0 / D_MODEL)
        dg2_ref[...] += jnp.sum(dy * xhat2, axis=0, keepdims=True)
        db2_ref[...] += jnp.sum(dy, axis=0, keepdims=True)
        dp2 = _ln_bwd(dy, xhat2, rstd2, g2_ref[...])
        dp2b = dp2.astype(dp2b_ref.dtype)
        dp2b_ref[...] = dp2b
        back = jnp.zeros((tm, D_MODEL), F32)
        for j in range(N_FF):
            cols = slice(j * FF_BLOCK, (j + 1) * FF_BLOCK)
            dr = _dot(dp2b, w2_ref[cols, :], NT)
            da = (dr * (2.0 * jnp.sqrt(r_ref[:, cols].astype(F32)))).astype(da_ref.dtype)
            da_ref[:, cols] = da
            back = back + _dot(da, w1_ref[:, cols], NT)
        dh1 = ALPHA * dp2 + back
        dg1_ref[...] += jnp.sum(dh1 * xhat1, axis=0, keepdims=True)
        db1_ref[...] += jnp.sum(dh1, axis=0, keepdims=True)
        dp1 = _ln_bwd(dh1, xhat1, rstd1, g1_ref[...])
        dp1b = dp1.astype(dp1b_ref.dtype)
        dp1_ref[...] = dp1
        dp1b_ref[...] = dp1b
        dcat_ref[...] = _dot(dp1b, wo_ref[...], NT)

    row = pl.BlockSpec((tm, D_MODEL), lambda i: (i, 0))
    wide = pl.BlockSpec((tm, D_FF), lambda i: (i, 0))
    vec = pl.BlockSpec((1, D_MODEL), lambda i: (0, 0))
    narrow = lambda dtype: jax.ShapeDtypeStruct((t, D_MODEL), dtype)
    return pl.pallas_call(
        body, name="sublayers", grid=(t // tm,),
        in_specs=[pl.BlockSpec((tm, GROUP), lambda i: (i, 0)), pl.BlockSpec((tm, GROUP), lambda i: (i, 0)), row, row,
                  _resident((D_MODEL, D_MODEL)),
                  _resident((D_MODEL, D_FF)), _resident((D_FF, D_MODEL)), vec, vec, vec, vec],
        out_specs=[row, wide, wide, row, row, row, row, vec, vec, vec, vec,
                   pl.BlockSpec((SUBLANES, LANES), lambda i: (0, 0))],
        out_shape=[narrow(BF16), jax.ShapeDtypeStruct((t, D_FF), BF16), jax.ShapeDtypeStruct((t, D_FF), BF16),
                   narrow(BF16), narrow(F32), narrow(BF16), narrow(F32)]
                  + [jax.ShapeDtypeStruct((1, D_MODEL), F32)] * 4 + [jax.ShapeDtypeStruct((SUBLANES, LANES), F32)],
        compiler_params=_params("arbitrary"),
    )(cat_h, cat_c, x, target, w_out, w_ff1, w_ff2, g1, b1, g2, b2)


def _hgrn_bwd(proj, do, states, lb_logits, after):
    t = proj.shape[1]
    tb = min(t, 512)
    ncb = tb // CHUNK
    nblk = t // tb

    def body(q_ref, f_ref, v_ref, do_ref, st_ref, lbl_ref, after_ref, dp_ref, dlbl_ref, ds_scr, dlb_scr):
        i = pl.program_id(0)

        @pl.when(i == 0)
        def _():
            ds_scr[...] = jnp.zeros_like(ds_scr)
            dlb_scr[...] = jnp.zeros_like(dlb_scr)

        lb, s1 = _lower_bound(lbl_ref[...])
        causal, anti = _chunk_masks()
        every = range(ncb)
        rows = [slice(c * CHUNK, (c + 1) * CHUNK) for c in every]
        q, v, do = ([ref[r, :] for r in rows] for ref in (q_ref, v_ref, do_ref))
        st = [st_ref[c] for c in every]
        gates = [_gates(f_ref[r, :], lb) for r in rows]
        sig, f, k = ([gt[n] for gt in gates] for n in (0, 1, 3))
        b = [_dot_exact(causal, gt[2]) for gt in gates]
        mid, last = [x[CHUNK // 2:CHUNK // 2 + 1, :] for x in b], [x[CHUNK - 1:CHUNK, :] for x in b]
        e_q = [jnp.exp(b[c] - mid[c]) for c in every]
        e_k = [jnp.exp(mid[c] - b[c]) for c in every]
        e_i = [jnp.exp(x) for x in b]
        e_s = [jnp.exp(last[c] - b[c]) for c in every]
        dec = [jnp.exp(x) for x in last]
        qt, kt, qi, ks = ([a[c] * e[c] for c in every] for a, e in ((q, e_q), (k, e_k), (q, e_i), (k, e_s)))

        def masked(a, b_):
            return [[jnp.where(causal, _dot(a_h, b_h, NT), 0.0) for a_h, b_h in zip(_heads(a[c]), _heads(b_[c]))]
                    for c in every]

        def with_scores(s, other, dims):
            return [jnp.concatenate([_dot(s_h, o_h, dims) for s_h, o_h in zip(s[c], _heads(other[c]))], axis=1)
                    for c in every]

        def per_head(dims, a, b_):
            return [_per_head(lambda a_h, b_h: _dot(a_h, b_h, dims), a[c], b_[c]) for c in every]

        scores, dscores = masked(qt, kt), masked(do, v)
        dqt, dkt, dv_intra = with_scores(dscores, kt, NN), with_scores(dscores, qt, TN), with_scores(scores, do, TN)
        dqi, update = per_head(NN, do, st), per_head(TN, do, qi)

        dst = ds_scr[...]
        dsts = [None] * ncb
        for c in reversed(every):
            dsts[c] = dst
            dst = dec[c] * dst + update[c]
        ds_scr[...] = dst

        dv_state, dks = per_head(NT, ks, dsts), per_head(NN, v, dsts)
        ddec = [jnp.sum(dsts[c] * st[c], axis=0, keepdims=True) for c in every]
        dq = [dqt[c] * e_q[c] + dqi[c] * e_i[c] for c in every]
        dk = [dkt[c] * e_k[c] + dks[c] * e_s[c] for c in every]
        db = [q[c] * dq[c] - k[c] * dk[c] for c in every]
        db_last = [jnp.sum(dks[c] * ks[c], axis=0, keepdims=True) + ddec[c] * dec[c] for c in every]
        dg = [_dot_exact(anti, db[c]) + db_last[c] for c in every]
        df = [dg[c] / f[c] - dk[c] for c in every]
        dlb_scr[...] += sum(jnp.sum(df[c] * (1.0 - sig[c]), axis=0, keepdims=True) for c in every)
        dfp = [df[c] * (1.0 - lb) * sig[c] * (1.0 - sig[c]) for c in every]
        dv = [dv_intra[c] + dv_state[c] for c in every]
        for n, parts in enumerate((dq, dfp, dv)):
            dp_ref[n] = jnp.concatenate(parts, axis=0).astype(dp_ref.dtype)

        @pl.when(i == nblk - 1)
        def _():
            dlb = dlb_scr[...]
            dlbl_ref[0:1, :] = dlb * lb * (1.0 - lb)
            dlbl_ref[1:2, :] = -dlb * lb * s1

    grp = lambda g: pl.BlockSpec((None, tb, GROUP), lambda i: (g, nblk - 1 - i, 0))
    vec = pl.BlockSpec((2, HGRN_WIDTH), lambda i: (0, 0))
    return pl.pallas_call(
        body, name="hgrn_bwd", grid=(nblk,),
        in_specs=[grp(0), grp(1), grp(2), pl.BlockSpec((tb, HGRN_WIDTH), lambda i: (nblk - 1 - i, 0)),
                  pl.BlockSpec((ncb, HEAD_DIM, HGRN_WIDTH), lambda i: (nblk - 1 - i, 0, 0)), vec, ANY],
        out_specs=[pl.BlockSpec((3, tb, HGRN_WIDTH), lambda i: (0, nblk - 1 - i, 0)), vec],
        out_shape=[jax.ShapeDtypeStruct((3, t, HGRN_WIDTH), BF16), jax.ShapeDtypeStruct((2, HGRN_WIDTH), F32)],
        scratch_shapes=[pltpu.VMEM((HEAD_DIM, HGRN_WIDTH), F32), pltpu.VMEM((1, HGRN_WIDTH), F32)],
        compiler_params=_params("arbitrary"),
    )(proj, proj, proj, do, states, lb_logits, after)


def _in_bwd(dph, dog, dpc, w_in, dpre1, after):
    t = dpre1.shape[0]
    tm = min(t, 512)

    def body(dh_ref, dog_ref, dc_ref, w_ref, dp_ref, after_ref, o_ref):
        acc = ALPHA * dp_ref[...]
        for g in range(N_GROUPS):
            part = dh_ref[g] if g < 3 else dog_ref[...] if g == 3 else dc_ref[g - 4]
            acc = acc + _dot(part, w_ref[:, g * GROUP:(g + 1) * GROUP], NT)
        o_ref[...] = acc

    row = pl.BlockSpec((tm, D_MODEL), lambda i: (i, 0))
    three = pl.BlockSpec((3, tm, GROUP), lambda i: (0, i, 0))
    return pl.pallas_call(
        body, name="in_bwd", grid=(t // tm,),
        in_specs=[three, pl.BlockSpec((tm, GROUP), lambda i: (i, 0)), three, _resident((D_MODEL, IN_COLS)), row, ANY],
        out_specs=row,
        out_shape=jax.ShapeDtypeStruct((t, D_MODEL), F32),
        compiler_params=_params("parallel"),
    )(dph, dog, dpc, w_in, dpre1, after)


GRAD_TILE = 512


class _Side:
    def __init__(self, operands, in_specs, out_shape, out_specs, scratch, init, begin):
        self.operands, self.in_specs, self.out_shape, self.out_specs = operands, in_specs, out_shape, out_specs
        self.scratch, self.init, self.begin = scratch, init, begin


def _grad_w(name, operands, widths, shape, step, after=None, side=None):
    t = operands[0].shape[-2]
    tt = min(t, GRAD_TILE)
    n_in, n_steps = len(operands), t // tt
    in_specs = [pl.BlockSpec((tt, w), lambda k: (k, 0)) if a.ndim == 2 else
                pl.BlockSpec((a.shape[0], tt, w), lambda k: (0, k, 0)) for a, w in zip(operands, widths)]
    extra = [] if after is None else [after]
    s_in, s_out = (len(side.operands), len(side.out_shape)) if side else (0, 0)
    first_out = n_in + s_in + len(extra)

    def body(*refs):
        o_ref, side_outs = refs[first_out], refs[first_out + 1:first_out + 1 + s_out]
        acc, narrow, sem = refs[first_out + 1 + s_out:first_out + 4 + s_out]
        k = pl.program_id(0)

        @pl.when(k == 0)
        def _():
            acc[...] = jnp.zeros_like(acc)
            if side:
                side.init(side_outs)

        tick = side.begin(k, n_steps, refs[n_in:n_in + s_in], side_outs, refs[first_out + 4 + s_out:]) if side else None
        step(acc, *refs[:n_in], tick or (lambda j: None))

        @pl.when(k == n_steps - 1)
        def _():
            narrow[...] = acc[...].astype(narrow.dtype)
            out = pltpu.make_async_copy(narrow, o_ref, sem)
            out.start()
            out.wait()

    outs = pl.pallas_call(
        body, name=name, grid=(n_steps,),
        in_specs=in_specs + (side.in_specs if side else []) + [ANY] * len(extra),
        out_specs=[ANY] + (side.out_specs if side else []),
        out_shape=[jax.ShapeDtypeStruct(shape, BF16)] + (side.out_shape if side else []),
        scratch_shapes=[pltpu.VMEM(shape, F32), pltpu.VMEM(shape, BF16), pltpu.SemaphoreType.DMA]
                       + (side.scratch if side else []),
        compiler_params=_params("arbitrary"),
    )(*operands, *(side.operands if side else ()), *extra)
    return outs if side else outs[0]


def _dw_in(xb, dph, dog, dpc, after):
    def step(acc, x_ref, dh_ref, dog_ref, dc_ref, tick):
        xv = x_ref[...]
        for g in range(N_GROUPS):
            part = dh_ref[g] if g < 3 else dog_ref[...] if g == 3 else dc_ref[g - 4]
            acc[:, g * GROUP:(g + 1) * GROUP] += _dot(xv, part, TN)

    return _grad_w("dw_in", (xb, dph, dog, dpc), (D_MODEL, GROUP, GROUP, GROUP), (D_MODEL, IN_COLS), step, after)


def _dw_out(cat_h, cat_c, dpre1b):
    def step(acc, h_ref, c_ref, d_ref, tick):
        dv = d_ref[...]
        acc[0:GROUP, :] += _dot(h_ref[...], dv, TN)
        acc[GROUP:2 * GROUP, :] += _dot(c_ref[...], dv, TN)

    return _grad_w("dw_out", (cat_h, cat_c, dpre1b), (GROUP, GROUP, D_MODEL), (D_MODEL, D_MODEL), step)


def _strips_of(j, tt):
    per_tick = tt // GATE_STRIP // N_FF
    return [slice(s * GATE_STRIP, (s + 1) * GATE_STRIP) for s in range(j * per_tick, (j + 1) * per_tick)]


def _dw_ff1(h1b, da, dcat, o, proj, gate_norm_w, after):
    t = h1b.shape[0]
    tt = min(t, GRAD_TILE)

    def step(acc, h_ref, da_ref, tick):
        hv = h_ref[...]
        for j in range(N_FF):
            cols = slice(j * FF_BLOCK, (j + 1) * FF_BLOCK)
            acc[:, cols] += _dot(hv, da_ref[:, cols], TN)
            tick(j)

    def init(outs):
        outs[2][...] = jnp.zeros_like(outs[2])

    def begin(k, n_steps, ins, outs, scratch):
        do2_ref, o_ref, og_ref, gnw_ref = ins
        do_ref, dog_ref, dgnw_ref = outs
        total = [jnp.zeros((GATE_STRIP, GROUP), F32)]

        def tick(j):
            gnw = gnw_ref[...]
            for rows in _strips_of(j, tt):
                ov, og, do2 = o_ref[rows, :], og_ref[rows, :], do2_ref[rows, :]
                rs = _per_head(lambda o_h: jnp.broadcast_to(
                    lax.rsqrt(jnp.mean(o_h * o_h, axis=-1, keepdims=True) + EPS), o_h.shape), ov)
                on = ov * rs
                sg = _sigmoid(og)
                sil = og * sg
                don = do2 * gnw * sil
                total[0] = total[0] + do2 * on * sil
                dog_ref[rows, :] = (do2 * on * gnw * (sg * (1.0 + og * (1.0 - sg)))).astype(dog_ref.dtype)
                do_ref[rows, :] = rs * (don - on * _per_head(
                    lambda p_h: jnp.broadcast_to(jnp.mean(p_h, axis=-1, keepdims=True), p_h.shape), don * on))
            if j == N_FF - 1:
                dgnw_ref[...] += jnp.sum(total[0], axis=0, keepdims=True)

        return tick

    tile = pl.BlockSpec((tt, GROUP), lambda k: (k, 0))
    vec = pl.BlockSpec((1, GROUP), lambda k: (0, 0))
    side = _Side(
        (dcat, o, proj, gate_norm_w), [tile, tile, pl.BlockSpec((None, tt, GROUP), lambda k: (3, k, 0)), vec],
        [jax.ShapeDtypeStruct((t, HGRN_WIDTH), F32), jax.ShapeDtypeStruct((t, HGRN_WIDTH), BF16),
         jax.ShapeDtypeStruct((1, HGRN_WIDTH), F32)], [tile, tile, vec], [], init, begin)
    return _grad_w("dw_ff1", (h1b, da), (D_MODEL, D_FF), (D_MODEL, D_FF), step, after, side)


def _dw_ff2(r, dpre2b, dcat, proj, conv_w):
    t = r.shape[0]
    tt = min(t, GRAD_TILE)
    hb = tt // SUBLANES

    def step(acc, r_ref, d_ref, tick):
        dv = d_ref[...]
        for j in range(N_FF):
            rows = slice(j * FF_BLOCK, (j + 1) * FF_BLOCK)
            acc[rows, :] += _dot(r_ref[:, rows], dv, TN)
            tick(j)

    def init(outs):
        outs[1][...] = jnp.zeros_like(outs[1])

    def begin(k, n_steps, ins, outs, scratch):
        dy_ref, dyn_ref, b_ref, bn_ref, c_ref, u_ref, ch_ref, uh_ref, cw_ref = ins
        dp_ref, dcw_ref = outs
        zbuf, dbuf = scratch
        zbuf[0:SUBLANES, :] = jnp.where(k > 0, ch_ref[...] * uh_ref[...], 0.0)
        zbuf[SUBLANES:SUBLANES + tt, :] = c_ref[...] * u_ref[...]
        dbuf[0:tt, :] = dy_ref[...] * b_ref[...]
        dbuf[tt:tt + SUBLANES, :] = jnp.where(k < n_steps - 1, dyn_ref[...] * bn_ref[...], 0.0)
        totals = [jnp.zeros((GATE_STRIP, GROUP), F32) for _ in range(3)]

        def tick(j):
            cw = cw_ref[...]
            for rows in _strips_of(j, tt):
                at = lambda buf, shift: buf[shift + rows.start:shift + rows.stop, :]
                z, z1, z2 = at(zbuf, SUBLANES), at(zbuf, SUBLANES - 1), at(zbuf, SUBLANES - 2)
                dyc, d1, d2 = at(dbuf, 0), at(dbuf, 1), at(dbuf, 2)
                yc = cw[2:3, :] * z + cw[1:2, :] * z1 + cw[0:1, :] * z2
                dz = cw[2:3, :] * dyc + cw[1:2, :] * d1 + cw[0:1, :] * d2
                dp_ref[0, rows, :] = (dy_ref[rows, :] * yc).astype(dp_ref.dtype)
                dp_ref[1, rows, :] = (dz * u_ref[rows, :]).astype(dp_ref.dtype)
                dp_ref[2, rows, :] = (dz * c_ref[rows, :]).astype(dp_ref.dtype)
                for n, tap in enumerate((z2, z1, z)):
                    totals[n] = totals[n] + dyc * tap
            if j == N_FF - 1:
                for n in range(3):
                    dcw_ref[n:n + 1, :] += jnp.sum(totals[n], axis=0, keepdims=True)

        return tick

    grp = lambda g: pl.BlockSpec((None, tt, GROUP), lambda k: (g, k, 0))
    prev = lambda g: pl.BlockSpec((None, SUBLANES, GROUP), lambda k: (g, jnp.maximum(k * hb - 1, 0), 0))
    nxt_row = lambda k: jnp.minimum((k + 1) * hb, t // SUBLANES - 1)
    nxt = lambda g: pl.BlockSpec((None, SUBLANES, GROUP), lambda k: (g, nxt_row(k), 0))
    whole = pl.BlockSpec((3, CONV_WIDTH), lambda k: (0, 0))
    side = _Side(
        (dcat, dcat, proj, proj, proj, proj, proj, proj, conv_w),
        [pl.BlockSpec((tt, GROUP), lambda k: (k, 1)), pl.BlockSpec((SUBLANES, GROUP), lambda k: (nxt_row(k), 1)),
         grp(4), nxt(4), grp(5), grp(6), prev(5), prev(6), whole],
        [jax.ShapeDtypeStruct((3, t, CONV_WIDTH), BF16), jax.ShapeDtypeStruct((3, CONV_WIDTH), F32)],
        [pl.BlockSpec((3, tt, GROUP), lambda k: (0, k, 0)), whole],
        [pltpu.VMEM((tt + SUBLANES, GROUP), F32), pltpu.VMEM((tt + SUBLANES, GROUP), F32)], init, begin)
    return _grad_w("dw_ff2", (r, dpre2b), (D_FF, D_MODEL), (D_FF, D_MODEL), step, side=side)


def _place():
    x, y, c = lax.axis_index("x"), lax.axis_index("y"), lax.axis_index("c")
    return x, y, c, 2 * x + y


def _other_chips(x, y):
    return [(1 - x, y), (x, 1 - y), (1 - x, 1 - y)]


def _place_shard(name, w, chip, cols_sharded, after=None):
    rows, cols = w.shape
    tr = min(rows, 256)
    nb = rows // tr
    full = (rows, cols * N_CHIPS) if cols_sharded else (rows * N_CHIPS, cols)
    out_map = (lambda i, s: (i, s[0])) if cols_sharded else (lambda i, s: (s[0] * nb + i, 0))

    def body(s_ref, w_ref, *rest):
        rest[-1][...] = w_ref[...].astype(rest[-1].dtype)

    extra = [] if after is None else [after]
    return pl.pallas_call(
        body, name=name,
        grid_spec=pltpu.PrefetchScalarGridSpec(
            num_scalar_prefetch=1, grid=(nb,),
            in_specs=[pl.BlockSpec((tr, cols), lambda i, s: (i, 0))] + [ANY] * len(extra),
            out_specs=pl.BlockSpec((tr, cols), out_map)),
        out_shape=jax.ShapeDtypeStruct(full, BF16),
        compiler_params=_params("parallel"),
    )(chip, w, *extra)


HBM = pl.BlockSpec(memory_space=pltpu.HBM)
SEM = pl.BlockSpec(memory_space=pltpu.SEMAPHORE)
EFFECT = pltpu.SideEffectType.DATAFLOW_SIDE_EFFECTING


PEER_SETS = {
    "sibling": (0, lambda x, y, c: [(x, y, 1 - c)]),
    "chips": (1, lambda x, y, c: [(1 - x, y, c), (x, 1 - y, c), (1 - x, 1 - y, c)]),
    "neighbours": (2, lambda x, y, c: [(1 - x, y, c), (x, 1 - y, c)]),
}


class _Split:
    def __init__(self, name, arrays, plan, others=(), peers=None):
        n_own, arrays = len(arrays), (*arrays, *others)
        n, n_copies = len(arrays), plan.count
        self.name, self.plan, self.n = name, plan, n_own
        barrier_id, peer_ids = PEER_SETS[peers] if peers else (None, None)

        def body(*refs):
            if peers:
                x, y, c, _ = _place()
                barrier = pltpu.get_barrier_semaphore()
                for peer in peer_ids(x, y, c):
                    pl.semaphore_signal(barrier, inc=1, device_id=peer, device_id_type=MESH)
                pl.semaphore_wait(barrier, len(peer_ids(0, 0, 0)))
            send_sems, recv_sems, token = refs[n], refs[n + 1], refs[-1]
            for k, (src, dst, to) in enumerate(plan(refs[:n])):
                pltpu.make_async_remote_copy(src_ref=src, dst_ref=dst, send_sem=send_sems.at[k], recv_sem=recv_sems.at[k],
                                             device_id=to, device_id_type=MESH).start()
            token[...] = jnp.zeros_like(token)

        outs = pl.pallas_call(
            body, name=name + "_start",
            out_shape=(pltpu.SemaphoreType.DMA((n_copies,)), pltpu.SemaphoreType.DMA((n_copies,)),
                       *[pltpu.HBM(a.shape, a.dtype) for a in arrays], jax.ShapeDtypeStruct((SUBLANES, LANES), F32)),
            in_specs=(HBM,) * n, out_specs=(SEM, SEM) + (HBM,) * n + (pl.BlockSpec(memory_space=pltpu.VMEM),),
            input_output_aliases={i: 2 + i for i in range(n)},
            compiler_params=pltpu.CompilerParams(has_side_effects=EFFECT, collective_id=barrier_id),
        )(*[pltpu.with_memory_space_constraint(a, pltpu.HBM) for a in arrays])
        self.sems, self.arrays, self.others, self.token = outs[:2], outs[2:2 + n_own], outs[2 + n_own:2 + n], outs[-1]

    def wait(self, after):
        n, plan = self.n, self.plan

        def body(*refs):
            send_sems, recv_sems = refs[n], refs[n + 1]
            for k, (src, dst, to) in enumerate(plan(refs[:n])):
                cp = pltpu.make_async_remote_copy(src_ref=src, dst_ref=dst, send_sem=send_sems.at[k],
                                                  recv_sem=recv_sems.at[k], device_id=to, device_id_type=MESH)
                cp.wait_send()
                cp.wait_recv()

        return pl.pallas_call(
            body, name=self.name + "_wait", out_shape=tuple(pltpu.HBM(a.shape, a.dtype) for a in self.arrays),
            in_specs=(HBM,) * n + (SEM, SEM, ANY), out_specs=(HBM,) * n, input_output_aliases={i: i for i in range(n)},
            compiler_params=pltpu.CompilerParams(has_side_effects=EFFECT),
        )(*self.arrays, *self.sems, after)


COLS_SHARDED = (True, False, True, False)
HALF_SHAPES = [(D_MODEL // 2, IN_COLS), (D_MODEL, D_MODEL // 2), (D_MODEL // 2, D_FF), (D_FF, D_MODEL // 2)]
PIECE_SHAPES = [(D_MODEL // 2, IN_COLS // N_CHIPS), (D_MODEL // N_CHIPS, D_MODEL // 2),
                (D_MODEL // 2, D_FF // N_CHIPS), (D_FF // N_CHIPS, D_MODEL // 2)]


def _shard_view(kind, ref, chip):
    if COLS_SHARDED[kind]:
        n = ref.shape[1] // N_CHIPS
        return ref.at[:, pl.ds(chip * n, n)]
    n = ref.shape[0] // N_CHIPS
    return ref.at[pl.ds(chip * n, n), :]


def _half_view(kind, ref, h):
    if COLS_SHARDED[kind]:
        n = ref.shape[0] // 2
        return ref.at[pl.ds(h * n, n), :]
    n = ref.shape[1] // 2
    return ref.at[:, pl.ds(h * n, n)]


def _plan(count):
    def mark(fn):
        fn.count = count
        return fn
    return mark


def _shard_rows_view(kind, ref, chip, part, n_parts):
    if COLS_SHARDED[kind]:
        m, n = ref.shape[0] // n_parts, ref.shape[1] // N_CHIPS
        return ref.at[pl.ds(part * m, m), pl.ds(chip * n, n)]
    m = ref.shape[0] // N_CHIPS // n_parts
    return ref.at[pl.ds((n_parts * chip + part) * m, m), :]


def _shard_half_view(kind, ref, chip, h):
    return _shard_rows_view(kind, ref, chip, h, 2)


def _gather_over_ici(kinds, weights):
    @_plan(2 * len(kinds))
    def plan(refs):
        x, y, c, me = _place()
        mine = [_shard_half_view(kind, ref, me, c) for kind, ref in zip(kinds, refs)]
        return [(v, v, to) for v in mine for to in ((1 - x, y, c), (x, 1 - y, c))]

    return _Split("gather_ici_" + "".join(map(str, kinds)), tuple(weights), plan, peers="neighbours")


def _relay_over_ici(kinds, weights, others=()):
    @_plan(2 * len(kinds))
    def plan(refs):
        x, y, c, _ = _place()
        x_nbr, y_nbr = 2 * (1 - x) + y, 2 * x + (1 - y)
        out = []
        for kind, ref in zip(kinds, refs):
            first, second = (_shard_rows_view(kind, ref, chip, 2 * c + q, 4) for q, chip in ((0, x_nbr), (1, y_nbr)))
            out += [(first, first, (x, 1 - y, c)), (second, second, (1 - x, y, c))]
        return out

    return _Split("relay_ici_" + "".join(map(str, kinds)), tuple(weights), plan, others, peers="neighbours")


def _gather_w_in_over_ici(w_in, conv4):
    @_plan(6)
    def plan(refs):
        x, y, c, me = _place()
        half, conv = _shard_half_view(0, refs[0], me, c), refs[1].at[me]
        return [(v, v, (px, py, c)) for v in (half, conv) for px, py in _other_chips(x, y)]

    return _Split("gather_w_in_ici", (w_in, conv4), plan, peers="chips")


def _gather_over_d2d(kinds, weights):
    @_plan(3 * len(kinds))
    def plan(refs):
        x, y, c, _ = _place()
        got = [_shard_half_view(kind, ref, 2 * px + py, c) for kind, ref in zip(kinds, refs)
               for px, py in _other_chips(x, y)]
        return [(v, v, (x, y, 1 - c)) for v in got]

    return _Split("gather_d2d_" + "".join(map(str, kinds)), tuple(weights), plan, peers="sibling")


def _swap_halves(kinds, grads):
    @_plan(len(kinds))
    def plan(refs):
        x, y, c, _ = _place()
        return [(_half_view(kind, g, 1 - c), land, (x, y, 1 - c))
                for kind, g, land in zip(kinds, refs[:len(kinds)], refs[len(kinds):])]

    lands = [lax.empty(HALF_SHAPES[kind], g.dtype) for kind, g in zip(kinds, grads)]
    return _Split("swap_halves_" + "".join(map(str, kinds)), (*grads, *lands), plan, peers="sibling")


def _add_half(name, g, recv, core, rows_split):
    shape = recv.shape
    tr = min(shape[0], 128 if rows_split else 256)
    nb = shape[0] // tr

    def body(c_ref, g_ref, r_ref, o_ref):
        o_ref[...] = (g_ref[...].astype(F32) + r_ref[...].astype(F32)).astype(o_ref.dtype)

    g_map = (lambda i, c_ref: (c_ref[0] * nb + i, 0)) if rows_split else (lambda i, c_ref: (i, c_ref[0]))
    blk = pl.BlockSpec((tr, shape[1]), lambda i, c_ref: (i, 0))
    return pl.pallas_call(
        body, name=name,
        grid_spec=pltpu.PrefetchScalarGridSpec(
            num_scalar_prefetch=1, grid=(nb,),
            in_specs=[pl.BlockSpec((tr, shape[1]), g_map), blk], out_specs=blk),
        out_shape=jax.ShapeDtypeStruct(shape, BF16),
        compiler_params=_params("parallel"),
    )(core, g, recv)


def _exchange_pieces(kinds, halves, pack=None):
    n_p, n = N_CHIPS - 1, len(kinds)

    @_plan(n_p * n + (0 if pack is None else N_DEV - 1))
    def plan(refs):
        x, y, c, _ = _place()
        copies = []
        if pack is not None:
            me = 4 * x + 2 * y + c
            peers = [((1 - x) if m & 4 else x, (1 - y) if m & 2 else y, (1 - c) if m & 1 else c) for m in range(1, N_DEV)]
            copies += [(refs[2 * n], refs[2 * n + 1].at[me], peer) for peer in peers]
        return copies + [(_shard_view(kind, half, 2 * px + py), land.at[j], (px, py, c))
                         for j, (px, py) in enumerate(_other_chips(x, y))
                         for kind, half, land in zip(kinds, refs[:n], refs[n:2 * n])]

    lands = [lax.empty((n_p,) + PIECE_SHAPES[kind], BF16) for kind in kinds]
    small = () if pack is None else (pack, lax.empty((N_DEV,) + pack.shape, F32))
    return _Split("exchange_pieces_" + "".join(map(str, kinds)), (*halves, *lands, *small), plan,
                  peers="chips" if pack is None else None)


def _sum_pieces(name, half, slots, place, rows_split, after):
    n_p, rows, cols = slots.shape
    tr = min(rows, 256)
    nb = rows // tr
    if rows_split:
        own_map = lambda i, s: (i, s[0])
        out_map = lambda i, s: (s[1] * nb + i, 0)
        shard = (2 * rows, cols)
    else:
        own_map = lambda i, s: (s[0] * nb + i, 0)
        out_map = lambda i, s: (i, s[1])
        shard = (rows, 2 * cols)

    def body(s_ref, own_ref, slot_ref, after_ref, o_ref):
        total = own_ref[...].astype(F32)
        for j in range(n_p):
            total = total + slot_ref[j].astype(F32)
        o_ref[...] = total

    return pl.pallas_call(
        body, name=name,
        grid_spec=pltpu.PrefetchScalarGridSpec(
            num_scalar_prefetch=1, grid=(nb,),
            in_specs=[pl.BlockSpec((tr, cols), own_map), pl.BlockSpec((n_p, tr, cols), lambda i, s: (0, i, 0)), ANY],
            out_specs=pl.BlockSpec((tr, cols), out_map)),
        out_shape=jax.ShapeDtypeStruct(shard, F32),
        compiler_params=_params("parallel"),
    )(place, half, slots, after)


def _join_halves(kinds, shards):
    @_plan(len(kinds))
    def plan(refs):
        x, y, c, _ = _place()
        return [(_half_view(kind, g, c), _half_view(kind, g, c), (x, y, 1 - c)) for kind, g in zip(kinds, refs)]

    return _Split("join_halves_" + "".join(map(str, kinds)), tuple(shards), plan, peers="sibling")


N_DEV = 8


def _sum_shared(pack, land, device):
    def body(d_ref, p_ref, l_ref, o_ref):
        me = d_ref[0]
        total = jnp.where(me == 0, p_ref[...], l_ref[0])
        for d in range(1, N_DEV):
            total = total + jnp.where(me == d, p_ref[...], l_ref[d])
        o_ref[...] = total

    return pl.pallas_call(
        body, name="sum_shared",
        grid_spec=pltpu.PrefetchScalarGridSpec(
            num_scalar_prefetch=1, grid=(1,),
            in_specs=[pl.BlockSpec(pack.shape, lambda i, d: (0, 0)), pl.BlockSpec(land.shape, lambda i, d: (0, 0, 0))],
            out_specs=pl.BlockSpec(pack.shape, lambda i, d: (0, 0))),
        out_shape=jax.ShapeDtypeStruct(pack.shape, F32),
    )(device, pack, land)


def _adamw(name, w, g, m, v, after=None):
    rows, cols = w.shape
    tr = min(rows, 256)
    extra = [] if after is None else [after]

    def body(w_ref, g_ref, m_ref, v_ref, *rest):
        d_ref, nm_ref, nv_ref = rest[-3:]
        d_ref[...], nm_ref[...], nv_ref[...] = _adam_step(w_ref[...], g_ref[...], m_ref[...], v_ref[...])

    blk = pl.BlockSpec((tr, cols), lambda i: (i, 0))
    return pl.pallas_call(
        body, name=name, grid=(rows // tr,), in_specs=[blk] * 4 + [ANY] * len(extra), out_specs=[blk] * 3,
        out_shape=[jax.ShapeDtypeStruct(w.shape, F32)] * 3,
        compiler_params=_params("parallel"),
    )(w, g, m, v, *extra)


def _adam_step(w, g, m, v):
    nm = ADAM_B1 * m + (1.0 - ADAM_B1) * g
    nv = ADAM_B2 * v + (1.0 - ADAM_B2) * jnp.square(g)
    m_hat = nm * (1.0 / (1.0 - ADAM_B1 ** ADAM_STEP))
    v_hat = nv * (1.0 / (1.0 - ADAM_B2 ** ADAM_STEP))
    return -ADAM_LR * (m_hat / (jnp.sqrt(v_hat) + ADAM_EPS) + ADAM_WD * w), nm, nv


def _adamw_small(tot, chip, weights, ms, vs, after):
    n, half = len(weights), D_MODEL // 2

    def body(chip_ref, tot_ref, *refs):
        ins, outs = refs[:3 * n], refs[3 * n + 1:]
        tot = tot_ref[...]
        conv_all = jnp.concatenate([tot[5:6, half:], tot[6:7, :half], tot[6:7, half:]], axis=0)
        conv = sum(jnp.where(chip_ref[0] == s, conv_all[:, s * LANES:(s + 1) * LANES], 0.0) for s in range(N_CHIPS))
        grads = [jnp.concatenate([tot[4:5, :half], tot[4:5, half:]], axis=0), tot[5:6, :half], conv,
                 tot[0:1], tot[1:2], tot[2:3], tot[3:4]]
        for k, g in enumerate(grads):
            delta, nm, nv = _adam_step(ins[k][...], g, ins[n + k][...], ins[2 * n + k][...])
            outs[k][...], outs[n + k][...], outs[2 * n + k][...], outs[3 * n + k][...] = g, delta, nm, nv

    whole = lambda a: pl.BlockSpec(a.shape, lambda i, s: (0,) * a.ndim)
    arrays = (*weights, *ms, *vs)
    return pl.pallas_call(
        body, name="adamw_small",
        grid_spec=pltpu.PrefetchScalarGridSpec(
            num_scalar_prefetch=1, grid=(1,), in_specs=[whole(tot)] + [whole(a) for a in arrays] + [ANY],
            out_specs=[whole(a) for a in weights] * 4),
        out_shape=[jax.ShapeDtypeStruct(a.shape, F32) for a in weights] * 4,
    )(chip, tot, *arrays, after)


def kernel(x, w_in, lb_logits, gate_norm_w, conv_w, w_out, ln1_g, ln1_b, w_ff1, w_ff2, ln2_g, ln2_b, loss_target, m_w_in, m_lb_logits, m_gate_norm_w, m_conv_w, m_w_out, m_ln1_g, m_ln1_b, m_w_ff1, m_w_ff2, m_ln2_g, m_ln2_b, v_w_in, v_lb_logits, v_gate_norm_w, v_conv_w, v_w_out, v_ln1_g, v_ln1_b, v_w_ff1, v_w_ff2, v_ln2_g, v_ln2_b):
    xs, tgt = x[0], loss_target[0]
    chip = 2 * lax.axis_index("x") + lax.axis_index("y")
    core = lax.axis_index("c").astype(jnp.int32).reshape(1)
    chip1 = chip.astype(jnp.int32).reshape(1)
    place = jnp.concatenate([chip1, core])

    conv4 = lax.dynamic_update_slice(jnp.zeros((N_CHIPS,) + conv_w.shape[1:], F32), conv_w, (chip, 0, 0))
    ici_in = _gather_w_in_over_ici(_place_shard("place_w_in", w_in[0], chip1, True), conv4)
    rest = (1, 2, 3)
    ici_rest = _gather_over_ici(rest, (_place_shard("place_w_out", w_out[0], chip1, False, after=ici_in.token),
                                       _place_shard("place_w_ff1", w_ff1[0], chip1, True, after=ici_in.token),
                                       _place_shard("place_w_ff2", w_ff2[0], chip1, False, after=ici_in.token)))
    wb_in, cv4 = ici_in.wait(ici_rest.token)
    d2d_in = _gather_over_d2d((0,), (wb_in,))
    wb_in, = d2d_in.wait(d2d_in.token)
    conv_full = cv4.transpose(1, 0, 2).reshape(3, CONV_WIDTH)

    proj, xb, cat_c = _in_proj(xs, wb_in, conv_full, ici_rest.token)
    relay_rest = _relay_over_ici(rest, ici_rest.wait(proj))
    o, states = _hgrn_fwd(proj, lb_logits, relay_rest.token)
    d2d_rest = _gather_over_d2d(rest, relay_rest.wait(o))
    cat_h = _gate_fwd(proj, o, gate_norm_w, d2d_rest.token)
    wb_out, wb_ff1, wb_ff2 = d2d_rest.wait(cat_h)

    (h1b, r, da, dpre2b, dpre1, dpre1b, dcat, g_ln1_g, g_ln1_b, g_ln2_g, g_ln2_b, loss8) = _sublayers(
        cat_h, cat_c, xs, tgt, wb_out, wb_ff1, wb_ff2, ln1_g, ln1_b, ln2_g, ln2_b)

    names = ("w_in", "w_out", "w_ff1", "w_ff2")

    def add_halves(kinds, grads, lands):
        return [_add_half("add_half_" + names[k], g, ld, core, COLS_SHARDED[k]) for k, g, ld in zip(kinds, grads, lands)]

    def sum_pieces(kinds, halves, lands, after):
        return [_sum_pieces("sum_pieces_" + names[k], h, ld, place, COLS_SHARDED[k], after)
                for k, h, ld in zip(kinds, halves, lands)]

    early = (1, 2, 3)
    g_out_local = _dw_out(cat_h, cat_c, dpre1b)
    g_ff2_local, dpc, g_conv = _dw_ff2(r, dpre2b, dcat, proj, conv_full)
    swap_a = _swap_halves((1, 3), (g_out_local, g_ff2_local))
    g_ff1_local, do, dog, g_gnw = _dw_ff1(h1b, da, dcat, o, proj, gate_norm_w, swap_a.token)
    swap_b = _swap_halves((2,), (g_ff1_local,))
    swapped_a = swap_a.wait(swap_b.token)
    halves_a = add_halves((1, 3), swapped_a[:2], swapped_a[2:])
    swapped_b = swap_b.wait(halves_a[1])
    halves = (halves_a[0], *add_halves((2,), swapped_b[:1], swapped_b[1:]), halves_a[1])
    exch = _exchange_pieces(early, halves)
    dph, g_lbl = _hgrn_bwd(proj, do, states, lb_logits, exch.token)
    g_in_local = _dw_in(xb, dph, dog, dpc, dph)

    late = (0,)
    swap = _swap_halves(late, (g_in_local,))
    grad_x = _in_bwd(dph, dog, dpc, wb_in, dpre1, swap.token)
    exchanged = exch.wait(grad_x)
    pack = jnp.concatenate([
        g_ln1_g, g_ln1_b, g_ln2_g, g_ln2_b,
        jnp.concatenate([g_lbl[0:1], g_lbl[1:2]], axis=1),
        jnp.concatenate([g_gnw, g_conv[0:1]], axis=1),
        jnp.concatenate([g_conv[1:2], g_conv[2:3]], axis=1),
        jnp.concatenate([loss8[0:1], jnp.zeros((1, D_MODEL - LANES), F32)], axis=1)], axis=0)
    swapped = swap.wait(exchanged[0])
    exch = _exchange_pieces(late, add_halves(late, swapped[:1], swapped[1:]), pack)
    join = _join_halves(early, sum_pieces(early, exchanged[:3], exchanged[3:], exch.token))
    g_w_out, g_w_ff1, g_w_ff2 = join.wait(join.token)
    d_ff1, nm_ff1, nv_ff1 = _adamw("adamw_w_ff1", w_ff1[0], g_w_ff1, m_w_ff1[0], v_w_ff1[0])
    d_ff2, nm_ff2, nv_ff2 = _adamw("adamw_w_ff2", w_ff2[0], g_w_ff2, m_w_ff2[0], v_w_ff2[0], d_ff1)
    d_out, nm_out, nv_out = _adamw("adamw_w_out", w_out[0], g_w_out, m_w_out[0], v_w_out[0], d_ff2)
    exchanged = exch.wait(d_out)
    tot = _sum_shared(exchanged[2], exchanged[3], 2 * chip1 + core)
    loss = tot[7, 0]
    join = _join_halves(late, sum_pieces(late, exchanged[:1], exchanged[1:2], tot))
    small = ("lb_logits", "gate_norm_w", "conv_w", "ln1_g", "ln1_b", "ln2_g", "ln2_b")
    small_out = _adamw_small(
        tot, chip1, (lb_logits, gate_norm_w, conv_w[0], ln1_g, ln1_b, ln2_g, ln2_b),
        (m_lb_logits, m_gate_norm_w, m_conv_w[0], m_ln1_g, m_ln1_b, m_ln2_g, m_ln2_b),
        (v_lb_logits, v_gate_norm_w, v_conv_w[0], v_ln1_g, v_ln1_b, v_ln2_g, v_ln2_b), join.token)
    g_w_in, = join.wait(small_out[0])
    d_in, nm_in, nv_in = _adamw("adamw_w_in", w_in[0], g_w_in, m_w_in[0], v_w_in[0])

    def results(n_kind, large):
        out = dict(zip(small, small_out[n_kind * len(small):(n_kind + 1) * len(small)]))
        out["conv_w"] = out["conv_w"][None]
        out.update({name: a[None] for name, a in zip(("w_in", "w_out", "w_ff1", "w_ff2"), large)})
        return [out[name] for name in ("w_in", "lb_logits", "gate_norm_w", "conv_w", "w_out", "ln1_g", "ln1_b",
                                       "w_ff1", "w_ff2", "ln2_g", "ln2_b")]

    return (loss, grad_x[None], *results(0, (g_w_in, g_w_out, g_w_ff1, g_w_ff2)),
            *results(1, (d_in, d_out, d_ff1, d_ff2)), *results(2, (nm_in, nm_out, nm_ff1, nm_ff2)),
            *results(3, (nv_in, nv_out, nv_ff1, nv_ff2)))
```

```python
import jax
import jax.numpy as jnp
from jax import lax
from jax.experimental import pallas as pl
from jax.experimental.pallas import tpu as pltpu

F32 = jnp.float32
BF16 = jnp.bfloat16
MXU_DTYPE = jnp.bfloat16

D_MODEL = 1024
HGRN_WIDTH = 512
HEAD_DIM = 128
N_HEADS = 4
CONV_WIDTH = 512
CHUNK = 64
D_FF = 4096
IN_COLS = 3584
GROUP = 512
N_GROUPS = IN_COLS // GROUP
ALPHA = 2.0 ** 0.25
EPS = 1e-5
N_CHIPS = 4
ADAM_LR, ADAM_B1, ADAM_B2, ADAM_EPS, ADAM_WD, ADAM_STEP = 0.001, 0.9, 0.999, 1e-08, 0.01, 10

LANES = 128
SUBLANES = 8
VMEM_LIMIT = 56 * 1024 * 1024
FF_BLOCK = 1024
N_FF = D_FF // FF_BLOCK
GATE_STRIP = 64

NN = (((1,), (0,)), ((), ()))
NT = (((1,), (1,)), ((), ()))
TN = (((0,), (0,)), ((), ()))
MESH = pl.DeviceIdType.MESH
ANY = pl.BlockSpec(memory_space=pl.ANY)


def _dot(a, b, dims):
    return lax.dot_general(a.astype(MXU_DTYPE), b.astype(MXU_DTYPE), dims, preferred_element_type=F32)


def _dot_exact(ones, v):
    ones = ones.astype(jnp.bfloat16)
    hi = v.astype(jnp.bfloat16)
    rest = v - hi.astype(F32)
    mid = rest.astype(jnp.bfloat16)
    low = (rest - mid.astype(F32)).astype(jnp.bfloat16)
    return sum(lax.dot_general(ones, part, NN, preferred_element_type=F32) for part in (hi, mid, low))


def _params(*sem):
    return pltpu.CompilerParams(dimension_semantics=sem, vmem_limit_bytes=VMEM_LIMIT)


def _resident(shape):
    return pl.BlockSpec(shape, lambda *_: (0,) * len(shape), pipeline_mode=pl.Buffered(1))


def _sigmoid(v):
    return 1.0 / (1.0 + jnp.exp(-v))


def _lower_bound(lbl):
    m = jnp.max(lbl, axis=0, keepdims=True)
    e = jnp.exp(lbl - m)
    s = e / jnp.sum(e, axis=0, keepdims=True)
    return s[0:1, :], s[1:2, :]


def _heads(v):
    return [v[:, h * HEAD_DIM:(h + 1) * HEAD_DIM] for h in range(N_HEADS)]


def _per_head(fn, *arrays):
    return jnp.concatenate([fn(*parts) for parts in zip(*map(_heads, arrays))], axis=1)


def _in_proj(x, w_in, conv_w, after):
    t = x.shape[0]
    tm = min(t, 512)

    def body(x_ref, w_ref, cw_ref, after_ref, o_ref, xb_ref, y_ref, zbuf):
        @pl.when(pl.program_id(0) == 0)
        def _():
            zbuf[tm:tm + SUBLANES, :] = jnp.zeros((SUBLANES, CONV_WIDTH), F32)

        xb = x_ref[...].astype(xb_ref.dtype)
        xb_ref[...] = xb
        for g in range(N_GROUPS):
            o_ref[g] = _dot(xb, w_ref[:, g * GROUP:(g + 1) * GROUP], NN)
        zbuf[0:SUBLANES, :] = zbuf[tm:tm + SUBLANES, :]
        zbuf[SUBLANES:SUBLANES + tm, :] = o_ref[5] * o_ref[6]
        cw = cw_ref[...]
        at = lambda shift: zbuf[shift:shift + tm, :]
        conv = cw[2:3, :] * at(SUBLANES) + cw[1:2, :] * at(SUBLANES - 1) + cw[0:1, :] * at(SUBLANES - 2)
        y_ref[...] = (o_ref[4] * conv).astype(y_ref.dtype)

    return pl.pallas_call(
        body, name="in_proj", grid=(t // tm,),
        in_specs=[pl.BlockSpec((tm, D_MODEL), lambda i: (i, 0)), _resident((D_MODEL, IN_COLS)),
                  pl.BlockSpec((3, CONV_WIDTH), lambda i: (0, 0)), ANY],
        out_specs=[pl.BlockSpec((N_GROUPS, tm, GROUP), lambda i: (0, i, 0)), pl.BlockSpec((tm, D_MODEL), lambda i: (i, 0)),
                   pl.BlockSpec((tm, CONV_WIDTH), lambda i: (i, 0))],
        out_shape=[jax.ShapeDtypeStruct((N_GROUPS, t, GROUP), F32), jax.ShapeDtypeStruct((t, D_MODEL), BF16),
                   jax.ShapeDtypeStruct((t, CONV_WIDTH), BF16)],
        scratch_shapes=[pltpu.VMEM((tm + SUBLANES, CONV_WIDTH), F32)],
        compiler_params=_params("arbitrary"),
    )(x, w_in, conv_w, after)


def _gates(fp, lb):
    sig = _sigmoid(fp)
    f = lb + (1.0 - lb) * sig
    return sig, f, jnp.log(f), 1.0 - f


def _chunk_masks():
    row = lax.broadcasted_iota(jnp.int32, (CHUNK, CHUNK), 0)
    col = lax.broadcasted_iota(jnp.int32, (CHUNK, CHUNK), 1)
    return row >= col, row <= col


def _hgrn_fwd(proj, lb_logits, after):
    t = proj.shape[1]
    tb = min(t, 512)
    ncb = tb // CHUNK

    def body(q_ref, f_ref, v_ref, lbl_ref, after_ref, o_ref, st_ref, s_scr):
        @pl.when(pl.program_id(0) == 0)
        def _():
            s_scr[...] = jnp.zeros_like(s_scr)

        lb, _ = _lower_bound(lbl_ref[...])
        causal, _ = _chunk_masks()

        every = range(ncb)
        rows = [slice(c * CHUNK, (c + 1) * CHUNK) for c in every]
        q, v = [q_ref[r, :] for r in rows], [v_ref[r, :] for r in rows]
        gates = [_gates(f_ref[r, :], lb) for r in rows]
        k = [gt[3] for gt in gates]
        b = [_dot_exact(causal, gt[2]) for gt in gates]
        mid, last = [x[CHUNK // 2:CHUNK // 2 + 1, :] for x in b], [x[CHUNK - 1:CHUNK, :] for x in b]
        qt = [q[c] * jnp.exp(b[c] - mid[c]) for c in every]
        kt = [k[c] * jnp.exp(mid[c] - b[c]) for c in every]
        qi = [q[c] * jnp.exp(b[c]) for c in every]
        ks = [k[c] * jnp.exp(last[c] - b[c]) for c in every]
        dec = [jnp.exp(x) for x in last]
        scores = [[jnp.where(causal, _dot(a, b_, NT), 0.0) for a, b_ in zip(_heads(qt[c]), _heads(kt[c]))] for c in every]
        intra = [[_dot(s, v_h, NN) for s, v_h in zip(scores[c], _heads(v[c]))] for c in every]
        update = [_per_head(lambda v_h, ks_h: _dot(v_h, ks_h, TN), v[c], ks[c]) for c in every]

        st = s_scr[...]
        states = []
        for c in every:
            states.append(st)
            st_ref[c] = st
            st = dec[c] * st + update[c]
        s_scr[...] = st

        o_ref[...] = jnp.concatenate(
            [jnp.concatenate([i_h + _dot(qi_h, st_h, NT) for i_h, qi_h, st_h in
                              zip(intra[c], _heads(qi[c]), _heads(states[c]))], axis=1) for c in every], axis=0)

    grp = lambda g: pl.BlockSpec((None, tb, GROUP), lambda i: (g, i, 0))
    return pl.pallas_call(
        body, name="hgrn_fwd", grid=(t // tb,),
        in_specs=[grp(0), grp(1), grp(2), pl.BlockSpec((2, HGRN_WIDTH), lambda i: (0, 0)), ANY],
        out_specs=[pl.BlockSpec((tb, HGRN_WIDTH), lambda i: (i, 0)),
                   pl.BlockSpec((ncb, HEAD_DIM, HGRN_WIDTH), lambda i: (i, 0, 0))],
        out_shape=[jax.ShapeDtypeStruct((t, HGRN_WIDTH), F32),
                   jax.ShapeDtypeStruct((t // CHUNK, HEAD_DIM, HGRN_WIDTH), F32)],
        scratch_shapes=[pltpu.VMEM((HEAD_DIM, HGRN_WIDTH), F32)],
        compiler_params=_params("arbitrary"),
    )(proj, proj, proj, lb_logits, after)


def _gate_fwd(proj, o, gate_norm_w, after):
    t = proj.shape[1]
    tb = min(t, 512)

    def body(o_ref, og_ref, gnw_ref, after_ref, out_ref):
        gnw = gnw_ref[...]
        for s in range(tb // GATE_STRIP):
            rows = slice(s * GATE_STRIP, (s + 1) * GATE_STRIP)
            og = og_ref[rows, :]
            on = _per_head(lambda o_h: o_h * lax.rsqrt(jnp.mean(o_h * o_h, axis=-1, keepdims=True) + EPS), o_ref[rows, :])
            out_ref[rows, :] = (on * gnw * (og * _sigmoid(og))).astype(out_ref.dtype)

    tile = pl.BlockSpec((tb, GROUP), lambda i: (i, 0))
    return pl.pallas_call(
        body, name="gate_fwd", grid=(t // tb,),
        in_specs=[tile, pl.BlockSpec((None, tb, GROUP), lambda i: (3, i, 0)), pl.BlockSpec((1, GROUP), lambda i: (0, 0)), ANY],
        out_specs=tile,
        out_shape=jax.ShapeDtypeStruct((t, HGRN_WIDTH), BF16),
        compiler_params=_params("parallel"),
    )(o, proj, gate_norm_w, after)


def _ln_bwd(dy, xhat, rstd, g):
    dxhat = dy * g
    m1 = jnp.mean(dxhat, axis=-1, keepdims=True)
    m2 = jnp.mean(dxhat * xhat, axis=-1, keepdims=True)
    return rstd * (dxhat - m1 - xhat * m2)


def _layer_norm(pre):
    xc = pre - jnp.mean(pre, axis=-1, keepdims=True)
    rstd = lax.rsqrt(jnp.mean(xc * xc, axis=-1, keepdims=True) + EPS)
    return xc * rstd, rstd


def _sublayers(cat_h, cat_c, x, target, w_out, w_ff1, w_ff2, g1, b1, g2, b2):
    t = x.shape[0]
    tm = min(t, 256)

    def body(ch_ref, cc_ref, x_ref, tg_ref, wo_ref, w1_ref, w2_ref, g1_ref, b1_ref, g2_ref, b2_ref,
             h1_ref, r_ref, da_ref, dp2b_ref, dp1_ref, dp1b_ref, dcat_ref, dg1_ref, db1_ref, dg2_ref, db2_ref, loss_ref):
        @pl.when(pl.program_id(0) == 0)
        def _():
            for ref in (dg1_ref, db1_ref, dg2_ref, db2_ref, loss_ref):
                ref[...] = jnp.zeros_like(ref)

        mix = _dot(ch_ref[...], wo_ref[0:GROUP, :], NN) + _dot(cc_ref[...], wo_ref[GROUP:2 * GROUP, :], NN)
        xhat1, rstd1 = _layer_norm(ALPHA * x_ref[...] + mix)
        h1 = xhat1 * g1_ref[...] + b1_ref[...]
        h1b = h1.astype(h1_ref.dtype)
        h1_ref[...] = h1b
        mlp = jnp.zeros((tm, D_MODEL), F32)
        for j in range(N_FF):
            cols = slice(j * FF_BLOCK, (j + 1) * FF_BLOCK)
            r = jnp.square(jnp.maximum(_dot(h1b, w1_ref[:, cols], NN), 0.0)).astype(r_ref.dtype)
            r_ref[:, cols] = r
            mlp = mlp + _dot(r, w2_ref[cols, :], NN)
        xhat2, rstd2 = _layer_norm(ALPHA * h1 + mlp)
        err = xhat2 * g2_ref[...] + b2_ref[...] - tg_ref[...]
        loss_ref[...] += 0.5 * jnp.sum(jnp.mean(err * err, axis=-1, keepdims=True))
        dy = err * (1.0 / D_MODEL)
        dg2_ref[...] += jnp.sum(dy * xhat2, axis=0, keepdims=True)
        db2_ref[...] += jnp.sum(dy, axis=0, keepdims=True)
        dp2 = _ln_bwd(dy, xhat2, rstd2, g2_ref[...])
        dp2b = dp2.astype(dp2b_ref.dtype)
        dp2b_ref[...] = dp2b
        back = jnp.zeros((tm, D_MODEL), F32)
        for j in range(N_FF):
            cols = slice(j * FF_BLOCK, (j + 1) * FF_BLOCK)
            dr = _dot(dp2b, w2_ref[cols, :], NT)
            da = (dr * (2.0 * jnp.sqrt(r_ref[:, cols].astype(F32)))).astype(da_ref.dtype)
            da_ref[:, cols] = da
            back = back + _dot(da, w1_ref[:, cols], NT)
        dh1 = ALPHA * dp2 + back
        dg1_ref[...] += jnp.sum(dh1 * xhat1, axis=0, keepdims=True)
        db1_ref[...] += jnp.sum(dh1, axis=0, keepdims=True)
        dp1 = _ln_bwd(dh1, xhat1, rstd1, g1_ref[...])
        dp1b = dp1.astype(dp1b_ref.dtype)
        dp1_ref[...] = dp1
        dp1b_ref[...] = dp1b
        dcat_ref[...] = _dot(dp1b, wo_ref[...], NT)

    row = pl.BlockSpec((tm, D_MODEL), lambda i: (i, 0))
    wide = pl.BlockSpec((tm, D_FF), lambda i: (i, 0))
    vec = pl.BlockSpec((1, D_MODEL), lambda i: (0, 0))
    narrow = lambda dtype: jax.ShapeDtypeStruct((t, D_MODEL), dtype)
    return pl.pallas_call(
        body, name="sublayers", grid=(t // tm,),
        in_specs=[pl.BlockSpec((tm, GROUP), lambda i: (i, 0)), pl.BlockSpec((tm, GROUP), lambda i: (i, 0)), row, row,
                  _resident((D_MODEL, D_MODEL)),
                  _resident((D_MODEL, D_FF)), _resident((D_FF, D_MODEL)), vec, vec, vec, vec],
        out_specs=[row, wide, wide, row, row, row, row, vec, vec, vec, vec,
                   pl.BlockSpec((SUBLANES, LANES), lambda i: (0, 0))],
        out_shape=[narrow(BF16), jax.ShapeDtypeStruct((t, D_FF), BF16), jax.ShapeDtypeStruct((t, D_FF), BF16),
                   narrow(BF16), narrow(F32), narrow(BF16), narrow(F32)]
                  + [jax.ShapeDtypeStruct((1, D_MODEL), F32)] * 4 + [jax.ShapeDtypeStruct((SUBLANES, LANES), F32)],
        compiler_params=_params("arbitrary"),
    )(cat_h, cat_c, x, target, w_out, w_ff1, w_ff2, g1, b1, g2, b2)


def _hgrn_bwd(proj, do, states, lb_logits, after):
    t = proj.shape[1]
    tb = min(t, 512)
    ncb = tb // CHUNK
    nblk = t // tb

    def body(q_ref, f_ref, v_ref, do_ref, st_ref, lbl_ref, after_ref, dp_ref, dlbl_ref, ds_scr, dlb_scr):
        i = pl.program_id(0)

        @pl.when(i == 0)
        def _():
            ds_scr[...] = jnp.zeros_like(ds_scr)
            dlb_scr[...] = jnp.zeros_like(dlb_scr)

        lb, s1 = _lower_bound(lbl_ref[...])
        causal, anti = _chunk_masks()
        every = range(ncb)
        rows = [slice(c * CHUNK, (c + 1) * CHUNK) for c in every]
        q, v, do = ([ref[r, :] for r in rows] for ref in (q_ref, v_ref, do_ref))
        st = [st_ref[c] for c in every]
        gates = [_gates(f_ref[r, :], lb) for r in rows]
        sig, f, k = ([gt[n] for gt in gates] for n in (0, 1, 3))
        b = [_dot_exact(causal, gt[2]) for gt in gates]
        mid, last = [x[CHUNK // 2:CHUNK // 2 + 1, :] for x in b], [x[CHUNK - 1:CHUNK, :] for x in b]
        e_q = [jnp.exp(b[c] - mid[c]) for c in every]
        e_k = [jnp.exp(mid[c] - b[c]) for c in every]
        e_i = [jnp.exp(x) for x in b]
        e_s = [jnp.exp(last[c] - b[c]) for c in every]
        dec = [jnp.exp(x) for x in last]
        qt, kt, qi, ks = ([a[c] * e[c] for c in every] for a, e in ((q, e_q), (k, e_k), (q, e_i), (k, e_s)))

        def masked(a, b_):
            return [[jnp.where(causal, _dot(a_h, b_h, NT), 0.0) for a_h, b_h in zip(_heads(a[c]), _heads(b_[c]))]
                    for c in every]

        def with_scores(s, other, dims):
            return [jnp.concatenate([_dot(s_h, o_h, dims) for s_h, o_h in zip(s[c], _heads(other[c]))], axis=1)
                    for c in every]

        def per_head(dims, a, b_):
            return [_per_head(lambda a_h, b_h: _dot(a_h, b_h, dims), a[c], b_[c]) for c in every]

        scores, dscores = masked(qt, kt), masked(do, v)
        dqt, dkt, dv_intra = with_scores(dscores, kt, NN), with_scores(dscores, qt, TN), with_scores(scores, do, TN)
        dqi, update = per_head(NN, do, st), per_head(TN, do, qi)

        dst = ds_scr[...]
        dsts = [None] * ncb
        for c in reversed(every):
            dsts[c] = dst
            dst = dec[c] * dst + update[c]
        ds_scr[...] = dst

        dv_state, dks = per_head(NT, ks, dsts), per_head(NN, v, dsts)
        ddec = [jnp.sum(dsts[c] * st[c], axis=0, keepdims=True) for c in every]
        dq = [dqt[c] * e_q[c] + dqi[c] * e_i[c] for c in every]
        dk = [dkt[c] * e_k[c] + dks[c] * e_s[c] for c in every]
        db = [q[c] * dq[c] - k[c] * dk[c] for c in every]
        db_last = [jnp.sum(dks[c] * ks[c], axis=0, keepdims=True) + ddec[c] * dec[c] for c in every]
        dg = [_dot_exact(anti, db[c]) + db_last[c] for c in every]
        df = [dg[c] / f[c] - dk[c] for c in every]
        dlb_scr[...] += sum(jnp.sum(df[c] * (1.0 - sig[c]), axis=0, keepdims=True) for c in every)
        dfp = [df[c] * (1.0 - lb) * sig[c] * (1.0 - sig[c]) for c in every]
        dv = [dv_intra[c] + dv_state[c] for c in every]
        for n, parts in enumerate((dq, dfp, dv)):
            dp_ref[n] = jnp.concatenate(parts, axis=0).astype(dp_ref.dtype)

        @pl.when(i == nblk - 1)
        def _():
            dlb = dlb_scr[...]
            dlbl_ref[0:1, :] = dlb * lb * (1.0 - lb)
            dlbl_ref[1:2, :] = -dlb * lb * s1

    grp = lambda g: pl.BlockSpec((None, tb, GROUP), lambda i: (g, nblk - 1 - i, 0))
    vec = pl.BlockSpec((2, HGRN_WIDTH), lambda i: (0, 0))
    return pl.pallas_call(
        body, name="hgrn_bwd", grid=(nblk,),
        in_specs=[grp(0), grp(1), grp(2), pl.BlockSpec((tb, HGRN_WIDTH), lambda i: (nblk - 1 - i, 0)),
                  pl.BlockSpec((ncb, HEAD_DIM, HGRN_WIDTH), lambda i: (nblk - 1 - i, 0, 0)), vec, ANY],
        out_specs=[pl.BlockSpec((3, tb, HGRN_WIDTH), lambda i: (0, nblk - 1 - i, 0)), vec],
        out_shape=[jax.ShapeDtypeStruct((3, t, HGRN_WIDTH), BF16), jax.ShapeDtypeStruct((2, HGRN_WIDTH), F32)],
        scratch_shapes=[pltpu.VMEM((HEAD_DIM, HGRN_WIDTH), F32), pltpu.VMEM((1, HGRN_WIDTH), F32)],
        compiler_params=_params("arbitrary"),
    )(proj, proj, proj, do, states, lb_logits, after)


def _in_bwd(dph, dog, dpc, w_in, dpre1, after):
    t = dpre1.shape[0]
    tm = min(t, 512)

    def body(dh_ref, dog_ref, dc_ref, w_ref, dp_ref, after_ref, o_ref):
        acc = ALPHA * dp_ref[...]
        for g in range(N_GROUPS):
            part = dh_ref[g] if g < 3 else dog_ref[...] if g == 3 else dc_ref[g - 4]
            acc = acc + _dot(part, w_ref[:, g * GROUP:(g + 1) * GROUP], NT)
        o_ref[...] = acc

    row = pl.BlockSpec((tm, D_MODEL), lambda i: (i, 0))
    three = pl.BlockSpec((3, tm, GROUP), lambda i: (0, i, 0))
    return pl.pallas_call(
        body, name="in_bwd", grid=(t // tm,),
        in_specs=[three, pl.BlockSpec((tm, GROUP), lambda i: (i, 0)), three, _resident((D_MODEL, IN_COLS)), row, ANY],
        out_specs=row,
        out_shape=jax.ShapeDtypeStruct((t, D_MODEL), F32),
        compiler_params=_params("parallel"),
    )(dph, dog, dpc, w_in, dpre1, after)


GRAD_TILE = 512
OUT_PARTS = 4


class _Side:
    def __init__(self, operands, in_specs, out_shape, out_specs, scratch, init, begin):
        self.operands, self.in_specs, self.out_shape, self.out_specs = operands, in_specs, out_shape, out_specs
        self.scratch, self.init, self.begin = scratch, init, begin


def _grad_w(name, operands, widths, shape, step, after=None, side=None):
    t = operands[0].shape[-2]
    tt = min(t, GRAD_TILE)
    n_in, n_steps = len(operands), t // tt
    in_specs = [pl.BlockSpec((tt, w), lambda k: (k, 0)) if a.ndim == 2 else
                pl.BlockSpec((a.shape[0], tt, w), lambda k: (0, k, 0)) for a, w in zip(operands, widths)]
    extra = [] if after is None else [after]
    s_in, s_out = (len(side.operands), len(side.out_shape)) if side else (0, 0)
    first_out = n_in + s_in + len(extra)

    def body(*refs):
        o_ref, side_outs = refs[first_out], refs[first_out + 1:first_out + 1 + s_out]
        acc, narrow, sem = refs[first_out + 1 + s_out:first_out + 4 + s_out]
        k = pl.program_id(0)

        @pl.when(k == 0)
        def _():
            acc[...] = jnp.zeros_like(acc)
            if side:
                side.init(side_outs)

        tick = side.begin(k, n_steps, refs[n_in:n_in + s_in], side_outs, refs[first_out + 4 + s_out:]) if side else None
        step(acc, *refs[:n_in], tick or (lambda j: None))

        @pl.when(k == n_steps - 1)
        def _():
            part = shape[0] // OUT_PARTS
            copies = []
            for p in range(OUT_PARTS):
                rows = pl.ds(p * part, part)
                narrow[rows, :] = acc[rows, :].astype(narrow.dtype)
                copies.append(pltpu.make_async_copy(narrow.at[rows, :], o_ref.at[rows, :], sem.at[p]))
                copies[-1].start()
            for cp in copies:
                cp.wait()

    outs = pl.pallas_call(
        body, name=name, grid=(n_steps,),
        in_specs=in_specs + (side.in_specs if side else []) + [ANY] * len(extra),
        out_specs=[ANY] + (side.out_specs if side else []),
        out_shape=[jax.ShapeDtypeStruct(shape, BF16)] + (side.out_shape if side else []),
        scratch_shapes=[pltpu.VMEM(shape, F32), pltpu.VMEM(shape, BF16), pltpu.SemaphoreType.DMA((OUT_PARTS,))]
                       + (side.scratch if side else []),
        compiler_params=_params("arbitrary"),
    )(*operands, *(side.operands if side else ()), *extra)
    return outs if side else outs[0]


def _dw_in(xb, dph, dog, dpc, after):
    def step(acc, x_ref, dh_ref, dog_ref, dc_ref, tick):
        xv = x_ref[...]
        for g in range(N_GROUPS):
            part = dh_ref[g] if g < 3 else dog_ref[...] if g == 3 else dc_ref[g - 4]
            acc[:, g * GROUP:(g + 1) * GROUP] += _dot(xv, part, TN)

    return _grad_w("dw_in", (xb, dph, dog, dpc), (D_MODEL, GROUP, GROUP, GROUP), (D_MODEL, IN_COLS), step, after)


def _dw_out(cat_h, cat_c, dpre1b):
    def step(acc, h_ref, c_ref, d_ref, tick):
        dv = d_ref[...]
        acc[0:GROUP, :] += _dot(h_ref[...], dv, TN)
        acc[GROUP:2 * GROUP, :] += _dot(c_ref[...], dv, TN)

    return _grad_w("dw_out", (cat_h, cat_c, dpre1b), (GROUP, GROUP, D_MODEL), (D_MODEL, D_MODEL), step)


def _strips_of(j, tt):
    per_tick = tt // GATE_STRIP // N_FF
    return [slice(s * GATE_STRIP, (s + 1) * GATE_STRIP) for s in range(j * per_tick, (j + 1) * per_tick)]


def _dw_ff1(h1b, da, dcat, o, proj, gate_norm_w, after):
    t = h1b.shape[0]
    tt = min(t, GRAD_TILE)

    def step(acc, h_ref, da_ref, tick):
        hv = h_ref[...]
        for j in range(N_FF):
            cols = slice(j * FF_BLOCK, (j + 1) * FF_BLOCK)
            acc[:, cols] += _dot(hv, da_ref[:, cols], TN)
            tick(j)

    def init(outs):
        outs[2][...] = jnp.zeros_like(outs[2])

    def begin(k, n_steps, ins, outs, scratch):
        do2_ref, o_ref, og_ref, gnw_ref = ins
        do_ref, dog_ref, dgnw_ref = outs
        total = [jnp.zeros((GATE_STRIP, GROUP), F32)]

        def tick(j):
            gnw = gnw_ref[...]
            for rows in _strips_of(j, tt):
                ov, og, do2 = o_ref[rows, :], og_ref[rows, :], do2_ref[rows, :]
                rs = _per_head(lambda o_h: jnp.broadcast_to(
                    lax.rsqrt(jnp.mean(o_h * o_h, axis=-1, keepdims=True) + EPS), o_h.shape), ov)
                on = ov * rs
                sg = _sigmoid(og)
                sil = og * sg
                don = do2 * gnw * sil
                total[0] = total[0] + do2 * on * sil
                dog_ref[rows, :] = (do2 * on * gnw * (sg * (1.0 + og * (1.0 - sg)))).astype(dog_ref.dtype)
                do_ref[rows, :] = rs * (don - on * _per_head(
                    lambda p_h: jnp.broadcast_to(jnp.mean(p_h, axis=-1, keepdims=True), p_h.shape), don * on))
            if j == N_FF - 1:
                dgnw_ref[...] += jnp.sum(total[0], axis=0, keepdims=True)

        return tick

    tile = pl.BlockSpec((tt, GROUP), lambda k: (k, 0))
    vec = pl.BlockSpec((1, GROUP), lambda k: (0, 0))
    side = _Side(
        (dcat, o, proj, gate_norm_w), [tile, tile, pl.BlockSpec((None, tt, GROUP), lambda k: (3, k, 0)), vec],
        [jax.ShapeDtypeStruct((t, HGRN_WIDTH), F32), jax.ShapeDtypeStruct((t, HGRN_WIDTH), BF16),
         jax.ShapeDtypeStruct((1, HGRN_WIDTH), F32)], [tile, tile, vec], [], init, begin)
    return _grad_w("dw_ff1", (h1b, da), (D_MODEL, D_FF), (D_MODEL, D_FF), step, after, side)


def _dw_ff2(r, dpre2b, dcat, proj, conv_w):
    t = r.shape[0]
    tt = min(t, GRAD_TILE)
    hb = tt // SUBLANES

    def step(acc, r_ref, d_ref, tick):
        dv = d_ref[...]
        for j in range(N_FF):
            rows = slice(j * FF_BLOCK, (j + 1) * FF_BLOCK)
            acc[rows, :] += _dot(r_ref[:, rows], dv, TN)
            tick(j)

    def init(outs):
        outs[1][...] = jnp.zeros_like(outs[1])

    def begin(k, n_steps, ins, outs, scratch):
        dy_ref, dyn_ref, b_ref, bn_ref, c_ref, u_ref, ch_ref, uh_ref, cw_ref = ins
        dp_ref, dcw_ref = outs
        zbuf, dbuf = scratch
        zbuf[0:SUBLANES, :] = jnp.where(k > 0, ch_ref[...] * uh_ref[...], 0.0)
        zbuf[SUBLANES:SUBLANES + tt, :] = c_ref[...] * u_ref[...]
        dbuf[0:tt, :] = dy_ref[...] * b_ref[...]
        dbuf[tt:tt + SUBLANES, :] = jnp.where(k < n_steps - 1, dyn_ref[...] * bn_ref[...], 0.0)
        totals = [jnp.zeros((GATE_STRIP, GROUP), F32) for _ in range(3)]

        def tick(j):
            cw = cw_ref[...]
            for rows in _strips_of(j, tt):
                at = lambda buf, shift: buf[shift + rows.start:shift + rows.stop, :]
                z, z1, z2 = at(zbuf, SUBLANES), at(zbuf, SUBLANES - 1), at(zbuf, SUBLANES - 2)
                dyc, d1, d2 = at(dbuf, 0), at(dbuf, 1), at(dbuf, 2)
                yc = cw[2:3, :] * z + cw[1:2, :] * z1 + cw[0:1, :] * z2
                dz = cw[2:3, :] * dyc + cw[1:2, :] * d1 + cw[0:1, :] * d2
                dp_ref[0, rows, :] = (dy_ref[rows, :] * yc).astype(dp_ref.dtype)
                dp_ref[1, rows, :] = (dz * u_ref[rows, :]).astype(dp_ref.dtype)
                dp_ref[2, rows, :] = (dz * c_ref[rows, :]).astype(dp_ref.dtype)
                for n, tap in enumerate((z2, z1, z)):
                    totals[n] = totals[n] + dyc * tap
            if j == N_FF - 1:
                for n in range(3):
                    dcw_ref[n:n + 1, :] += jnp.sum(totals[n], axis=0, keepdims=True)

        return tick

    grp = lambda g: pl.BlockSpec((None, tt, GROUP), lambda k: (g, k, 0))
    prev = lambda g: pl.BlockSpec((None, SUBLANES, GROUP), lambda k: (g, jnp.maximum(k * hb - 1, 0), 0))
    nxt_row = lambda k: jnp.minimum((k + 1) * hb, t // SUBLANES - 1)
    nxt = lambda g: pl.BlockSpec((None, SUBLANES, GROUP), lambda k: (g, nxt_row(k), 0))
    whole = pl.BlockSpec((3, CONV_WIDTH), lambda k: (0, 0))
    side = _Side(
        (dcat, dcat, proj, proj, proj, proj, proj, proj, conv_w),
        [pl.BlockSpec((tt, GROUP), lambda k: (k, 1)), pl.BlockSpec((SUBLANES, GROUP), lambda k: (nxt_row(k), 1)),
         grp(4), nxt(4), grp(5), grp(6), prev(5), prev(6), whole],
        [jax.ShapeDtypeStruct((3, t, CONV_WIDTH), BF16), jax.ShapeDtypeStruct((3, CONV_WIDTH), F32)],
        [pl.BlockSpec((3, tt, GROUP), lambda k: (0, k, 0)), whole],
        [pltpu.VMEM((tt + SUBLANES, GROUP), F32), pltpu.VMEM((tt + SUBLANES, GROUP), F32)], init, begin)
    return _grad_w("dw_ff2", (r, dpre2b), (D_FF, D_MODEL), (D_FF, D_MODEL), step, side=side)


def _place():
    x, y, c = lax.axis_index("x"), lax.axis_index("y"), lax.axis_index("c")
    return x, y, c, 2 * x + y


def _other_chips(x, y):
    return [(1 - x, y), (x, 1 - y), (1 - x, 1 - y)]


def _place_shard(name, w, chip, cols_sharded, after=None):
    rows, cols = w.shape
    tr = min(rows, 256)
    nb = rows // tr
    full = (rows, cols * N_CHIPS) if cols_sharded else (rows * N_CHIPS, cols)
    out_map = (lambda i, s: (i, s[0])) if cols_sharded else (lambda i, s: (s[0] * nb + i, 0))

    def body(s_ref, w_ref, *rest):
        rest[-1][...] = w_ref[...].astype(rest[-1].dtype)

    extra = [] if after is None else [after]
    return pl.pallas_call(
        body, name=name,
        grid_spec=pltpu.PrefetchScalarGridSpec(
            num_scalar_prefetch=1, grid=(nb,),
            in_specs=[pl.BlockSpec((tr, cols), lambda i, s: (i, 0))] + [ANY] * len(extra),
            out_specs=pl.BlockSpec((tr, cols), out_map)),
        out_shape=jax.ShapeDtypeStruct(full, BF16),
        compiler_params=_params("parallel"),
    )(chip, w, *extra)


HBM = pl.BlockSpec(memory_space=pltpu.HBM)
SEM = pl.BlockSpec(memory_space=pltpu.SEMAPHORE)
EFFECT = pltpu.SideEffectType.DATAFLOW_SIDE_EFFECTING


PEER_SETS = {
    "sibling": (0, lambda x, y, c: [(x, y, 1 - c)]),
    "chips": (1, lambda x, y, c: [(1 - x, y, c), (x, 1 - y, c), (1 - x, 1 - y, c)]),
    "neighbours": (2, lambda x, y, c: [(1 - x, y, c), (x, 1 - y, c)]),
}


class _Split:
    def __init__(self, name, arrays, plan, others=(), peers=None):
        n_own, arrays = len(arrays), (*arrays, *others)
        n, n_copies = len(arrays), plan.count
        self.name, self.plan, self.n = name, plan, n_own
        barrier_id, peer_ids = PEER_SETS[peers] if peers else (None, None)

        def body(*refs):
            if peers:
                x, y, c, _ = _place()
                barrier = pltpu.get_barrier_semaphore()
                for peer in peer_ids(x, y, c):
                    pl.semaphore_signal(barrier, inc=1, device_id=peer, device_id_type=MESH)
                pl.semaphore_wait(barrier, len(peer_ids(0, 0, 0)))
            send_sems, recv_sems, token = refs[n], refs[n + 1], refs[-1]
            for k, (src, dst, to) in enumerate(plan(refs[:n])):
                pltpu.make_async_remote_copy(src_ref=src, dst_ref=dst, send_sem=send_sems.at[k], recv_sem=recv_sems.at[k],
                                             device_id=to, device_id_type=MESH).start()
            token[...] = jnp.zeros_like(token)

        outs = pl.pallas_call(
            body, name=name + "_start",
            out_shape=(pltpu.SemaphoreType.DMA((n_copies,)), pltpu.SemaphoreType.DMA((n_copies,)),
                       *[pltpu.HBM(a.shape, a.dtype) for a in arrays], jax.ShapeDtypeStruct((SUBLANES, LANES), F32)),
            in_specs=(HBM,) * n, out_specs=(SEM, SEM) + (HBM,) * n + (pl.BlockSpec(memory_space=pltpu.VMEM),),
            input_output_aliases={i: 2 + i for i in range(n)},
            compiler_params=pltpu.CompilerParams(has_side_effects=EFFECT, collective_id=barrier_id),
        )(*[pltpu.with_memory_space_constraint(a, pltpu.HBM) for a in arrays])
        self.sems, self.arrays, self.others, self.token = outs[:2], outs[2:2 + n_own], outs[2 + n_own:2 + n], outs[-1]

    def wait(self, after):
        n, plan = self.n, self.plan

        def body(*refs):
            send_sems, recv_sems = refs[n], refs[n + 1]
            for k, (src, dst, to) in enumerate(plan(refs[:n])):
                cp = pltpu.make_async_remote_copy(src_ref=src, dst_ref=dst, send_sem=send_sems.at[k],
                                                  recv_sem=recv_sems.at[k], device_id=to, device_id_type=MESH)
                cp.wait_send()
                cp.wait_recv()

        return pl.pallas_call(
            body, name=self.name + "_wait", out_shape=tuple(pltpu.HBM(a.shape, a.dtype) for a in self.arrays),
            in_specs=(HBM,) * n + (SEM, SEM, ANY), out_specs=(HBM,) * n, input_output_aliases={i: i for i in range(n)},
            compiler_params=pltpu.CompilerParams(has_side_effects=EFFECT),
        )(*self.arrays, *self.sems, after)


COLS_SHARDED = (True, False, True, False)
HALF_SHAPES = [(D_MODEL // 2, IN_COLS), (D_MODEL, D_MODEL // 2), (D_MODEL // 2, D_FF), (D_FF, D_MODEL // 2)]
PIECE_SHAPES = [(D_MODEL // 2, IN_COLS // N_CHIPS), (D_MODEL // N_CHIPS, D_MODEL // 2),
                (D_MODEL // 2, D_FF // N_CHIPS), (D_FF // N_CHIPS, D_MODEL // 2)]


def _shard_view(kind, ref, chip):
    if COLS_SHARDED[kind]:
        n = ref.shape[1] // N_CHIPS
        return ref.at[:, pl.ds(chip * n, n)]
    n = ref.shape[0] // N_CHIPS
    return ref.at[pl.ds(chip * n, n), :]


def _half_view(kind, ref, h):
    if COLS_SHARDED[kind]:
        n = ref.shape[0] // 2
        return ref.at[pl.ds(h * n, n), :]
    n = ref.shape[1] // 2
    return ref.at[:, pl.ds(h * n, n)]


def _plan(count):
    def mark(fn):
        fn.count = count
        return fn
    return mark


def _shard_rows_view(kind, ref, chip, part, n_parts):
    if COLS_SHARDED[kind]:
        m, n = ref.shape[0] // n_parts, ref.shape[1] // N_CHIPS
        return ref.at[pl.ds(part * m, m), pl.ds(chip * n, n)]
    m = ref.shape[0] // N_CHIPS // n_parts
    return ref.at[pl.ds((n_parts * chip + part) * m, m), :]


def _shard_half_view(kind, ref, chip, h):
    return _shard_rows_view(kind, ref, chip, h, 2)


def _gather_over_ici(kinds, weights):
    @_plan(2 * len(kinds))
    def plan(refs):
        x, y, c, me = _place()
        mine = [_shard_half_view(kind, ref, me, c) for kind, ref in zip(kinds, refs)]
        return [(v, v, to) for v in mine for to in ((1 - x, y, c), (x, 1 - y, c))]

    return _Split("gather_ici_" + "".join(map(str, kinds)), tuple(weights), plan, peers="neighbours")


def _relay_over_ici(kinds, weights, others=()):
    @_plan(2 * len(kinds))
    def plan(refs):
        x, y, c, _ = _place()
        x_nbr, y_nbr = 2 * (1 - x) + y, 2 * x + (1 - y)
        out = []
        for kind, ref in zip(kinds, refs):
            first, second = (_shard_rows_view(kind, ref, chip, 2 * c + q, 4) for q, chip in ((0, x_nbr), (1, y_nbr)))
            out += [(first, first, (x, 1 - y, c)), (second, second, (1 - x, y, c))]
        return out

    return _Split("relay_ici_" + "".join(map(str, kinds)), tuple(weights), plan, others, peers="neighbours")


def _gather_w_in_over_ici(w_in, conv4):
    @_plan(6)
    def plan(refs):
        x, y, c, me = _place()
        half, conv = _shard_half_view(0, refs[0], me, c), refs[1].at[me]
        return [(v, v, (px, py, c)) for v in (half, conv) for px, py in _other_chips(x, y)]

    return _Split("gather_w_in_ici", (w_in, conv4), plan, peers="chips")


def _gather_over_d2d(kinds, weights):
    @_plan(3 * len(kinds))
    def plan(refs):
        x, y, c, _ = _place()
        got = [_shard_half_view(kind, ref, 2 * px + py, c) for kind, ref in zip(kinds, refs)
               for px, py in _other_chips(x, y)]
        return [(v, v, (x, y, 1 - c)) for v in got]

    return _Split("gather_d2d_" + "".join(map(str, kinds)), tuple(weights), plan, peers="sibling")


def _swap_halves(kinds, grads):
    @_plan(len(kinds))
    def plan(refs):
        x, y, c, _ = _place()
        return [(_half_view(kind, g, 1 - c), land, (x, y, 1 - c))
                for kind, g, land in zip(kinds, refs[:len(kinds)], refs[len(kinds):])]

    lands = [lax.empty(HALF_SHAPES[kind], g.dtype) for kind, g in zip(kinds, grads)]
    return _Split("swap_halves_" + "".join(map(str, kinds)), (*grads, *lands), plan, peers="sibling")


def _add_half(name, g, recv, core, rows_split):
    shape = recv.shape
    tr = min(shape[0], 128 if rows_split else 256)
    nb = shape[0] // tr

    def body(c_ref, g_ref, r_ref, o_ref):
        o_ref[...] = (g_ref[...].astype(F32) + r_ref[...].astype(F32)).astype(o_ref.dtype)

    g_map = (lambda i, c_ref: (c_ref[0] * nb + i, 0)) if rows_split else (lambda i, c_ref: (i, c_ref[0]))
    blk = pl.BlockSpec((tr, shape[1]), lambda i, c_ref: (i, 0))
    return pl.pallas_call(
        body, name=name,
        grid_spec=pltpu.PrefetchScalarGridSpec(
            num_scalar_prefetch=1, grid=(nb,),
            in_specs=[pl.BlockSpec((tr, shape[1]), g_map), blk], out_specs=blk),
        out_shape=jax.ShapeDtypeStruct(shape, BF16),
        compiler_params=_params("parallel"),
    )(core, g, recv)


def _exchange_pieces(kinds, halves, pack=None):
    n_p, n = N_CHIPS - 1, len(kinds)

    @_plan(n_p * n + (0 if pack is None else N_DEV - 1))
    def plan(refs):
        x, y, c, _ = _place()
        copies = []
        if pack is not None:
            me = 4 * x + 2 * y + c
            peers = [((1 - x) if m & 4 else x, (1 - y) if m & 2 else y, (1 - c) if m & 1 else c) for m in range(1, N_DEV)]
            copies += [(refs[2 * n], refs[2 * n + 1].at[me], peer) for peer in peers]
        return copies + [(_shard_view(kind, half, 2 * px + py), land.at[j], (px, py, c))
                         for j, (px, py) in enumerate(_other_chips(x, y))
                         for kind, half, land in zip(kinds, refs[:n], refs[n:2 * n])]

    lands = [lax.empty((n_p,) + PIECE_SHAPES[kind], BF16) for kind in kinds]
    small = () if pack is None else (pack, lax.empty((N_DEV,) + pack.shape, F32))
    return _Split("exchange_pieces_" + "".join(map(str, kinds)), (*halves, *lands, *small), plan,
                  peers="chips" if pack is None else None)


def _sum_pieces(name, half, slots, place, rows_split, after):
    n_p, rows, cols = slots.shape
    tr = min(rows, 256)
    nb = rows // tr
    if rows_split:
        own_map = lambda i, s: (i, s[0])
        out_map = lambda i, s: (s[1] * nb + i, 0)
        shard = (2 * rows, cols)
    else:
        own_map = lambda i, s: (s[0] * nb + i, 0)
        out_map = lambda i, s: (i, s[1])
        shard = (rows, 2 * cols)

    def body(s_ref, own_ref, slot_ref, after_ref, o_ref):
        total = own_ref[...].astype(F32)
        for j in range(n_p):
            total = total + slot_ref[j].astype(F32)
        o_ref[...] = total

    return pl.pallas_call(
        body, name=name,
        grid_spec=pltpu.PrefetchScalarGridSpec(
            num_scalar_prefetch=1, grid=(nb,),
            in_specs=[pl.BlockSpec((tr, cols), own_map), pl.BlockSpec((n_p, tr, cols), lambda i, s: (0, i, 0)), ANY],
            out_specs=pl.BlockSpec((tr, cols), out_map)),
        out_shape=jax.ShapeDtypeStruct(shard, F32),
        compiler_params=_params("parallel"),
    )(place, half, slots, after)


def _join_halves(kinds, shards):
    @_plan(len(kinds))
    def plan(refs):
        x, y, c, _ = _place()
        return [(_half_view(kind, g, c), _half_view(kind, g, c), (x, y, 1 - c)) for kind, g in zip(kinds, refs)]

    return _Split("join_halves_" + "".join(map(str, kinds)), tuple(shards), plan, peers="sibling")


N_DEV = 8


def _sum_shared(pack, land, device):
    def body(d_ref, p_ref, l_ref, o_ref):
        me = d_ref[0]
        total = jnp.where(me == 0, p_ref[...], l_ref[0])
        for d in range(1, N_DEV):
            total = total + jnp.where(me == d, p_ref[...], l_ref[d])
        o_ref[...] = total

    return pl.pallas_call(
        body, name="sum_shared",
        grid_spec=pltpu.PrefetchScalarGridSpec(
            num_scalar_prefetch=1, grid=(1,),
            in_specs=[pl.BlockSpec(pack.shape, lambda i, d: (0, 0)), pl.BlockSpec(land.shape, lambda i, d: (0, 0, 0))],
            out_specs=pl.BlockSpec(pack.shape, lambda i, d: (0, 0))),
        out_shape=jax.ShapeDtypeStruct(pack.shape, F32),
    )(device, pack, land)


def _adamw(name, w, g, m, v, after=None):
    rows, cols = w.shape
    tr = min(rows, 256)
    extra = [] if after is None else [after]

    def body(w_ref, g_ref, m_ref, v_ref, *rest):
        d_ref, nm_ref, nv_ref = rest[-3:]
        d_ref[...], nm_ref[...], nv_ref[...] = _adam_step(w_ref[...], g_ref[...], m_ref[...], v_ref[...])

    blk = pl.BlockSpec((tr, cols), lambda i: (i, 0))
    return pl.pallas_call(
        body, name=name, grid=(rows // tr,), in_specs=[blk] * 4 + [ANY] * len(extra), out_specs=[blk] * 3,
        out_shape=[jax.ShapeDtypeStruct(w.shape, F32)] * 3,
        compiler_params=_params("parallel"),
    )(w, g, m, v, *extra)


def _adam_step(w, g, m, v):
    nm = ADAM_B1 * m + (1.0 - ADAM_B1) * g
    nv = ADAM_B2 * v + (1.0 - ADAM_B2) * jnp.square(g)
    m_hat = nm * (1.0 / (1.0 - ADAM_B1 ** ADAM_STEP))
    v_hat = nv * (1.0 / (1.0 - ADAM_B2 ** ADAM_STEP))
    return -ADAM_LR * (m_hat / (jnp.sqrt(v_hat) + ADAM_EPS) + ADAM_WD * w), nm, nv


def _adamw_small(tot, chip, weights, ms, vs, after):
    n, half = len(weights), D_MODEL // 2

    def body(chip_ref, tot_ref, *refs):
        ins, outs = refs[:3 * n], refs[3 * n + 1:]
        tot = tot_ref[...]
        conv_all = jnp.concatenate([tot[5:6, half:], tot[6:7, :half], tot[6:7, half:]], axis=0)
        conv = sum(jnp.where(chip_ref[0] == s, conv_all[:, s * LANES:(s + 1) * LANES], 0.0) for s in range(N_CHIPS))
        grads = [jnp.concatenate([tot[4:5, :half], tot[4:5, half:]], axis=0), tot[5:6, :half], conv,
                 tot[0:1], tot[1:2], tot[2:3], tot[3:4]]
        for k, g in enumerate(grads):
            delta, nm, nv = _adam_step(ins[k][...], g, ins[n + k][...], ins[2 * n + k][...])
            outs[k][...], outs[n + k][...], outs[2 * n + k][...], outs[3 * n + k][...] = g, delta, nm, nv

    whole = lambda a: pl.BlockSpec(a.shape, lambda i, s: (0,) * a.ndim)
    arrays = (*weights, *ms, *vs)
    return pl.pallas_call(
        body, name="adamw_small",
        grid_spec=pltpu.PrefetchScalarGridSpec(
            num_scalar_prefetch=1, grid=(1,), in_specs=[whole(tot)] + [whole(a) for a in arrays] + [ANY],
            out_specs=[whole(a) for a in weights] * 4),
        out_shape=[jax.ShapeDtypeStruct(a.shape, F32) for a in weights] * 4,
    )(chip, tot, *arrays, after)


def kernel(x, w_in, lb_logits, gate_norm_w, conv_w, w_out, ln1_g, ln1_b, w_ff1, w_ff2, ln2_g, ln2_b, loss_target, m_w_in, m_lb_logits, m_gate_norm_w, m_conv_w, m_w_out, m_ln1_g, m_ln1_b, m_w_ff1, m_w_ff2, m_ln2_g, m_ln2_b, v_w_in, v_lb_logits, v_gate_norm_w, v_conv_w, v_w_out, v_ln1_g, v_ln1_b, v_w_ff1, v_w_ff2, v_ln2_g, v_ln2_b):
    xs, tgt = x[0], loss_target[0]
    chip = 2 * lax.axis_index("x") + lax.axis_index("y")
    core = lax.axis_index("c").astype(jnp.int32).reshape(1)
    chip1 = chip.astype(jnp.int32).reshape(1)
    place = jnp.concatenate([chip1, core])

    conv4 = lax.dynamic_update_slice(jnp.zeros((N_CHIPS,) + conv_w.shape[1:], F32), conv_w, (chip, 0, 0))
    ici_in = _gather_w_in_over_ici(_place_shard("place_w_in", w_in[0], chip1, True), conv4)
    rest = (1, 2, 3)
    ici_rest = _gather_over_ici(rest, (_place_shard("place_w_out", w_out[0], chip1, False, after=ici_in.token),
                                       _place_shard("place_w_ff1", w_ff1[0], chip1, True, after=ici_in.token),
                                       _place_shard("place_w_ff2", w_ff2[0], chip1, False, after=ici_in.token)))
    wb_in, cv4 = ici_in.wait(ici_rest.token)
    d2d_in = _gather_over_d2d((0,), (wb_in,))
    wb_in, = d2d_in.wait(d2d_in.token)
    conv_full = cv4.transpose(1, 0, 2).reshape(3, CONV_WIDTH)

    proj, xb, cat_c = _in_proj(xs, wb_in, conv_full, ici_rest.token)
    relay_rest = _relay_over_ici(rest, ici_rest.wait(proj))
    o, states = _hgrn_fwd(proj, lb_logits, relay_rest.token)
    d2d_rest = _gather_over_d2d(rest, relay_rest.wait(o))
    cat_h = _gate_fwd(proj, o, gate_norm_w, d2d_rest.token)
    wb_out, wb_ff1, wb_ff2 = d2d_rest.wait(cat_h)

    (h1b, r, da, dpre2b, dpre1, dpre1b, dcat, g_ln1_g, g_ln1_b, g_ln2_g, g_ln2_b, loss8) = _sublayers(
        cat_h, cat_c, xs, tgt, wb_out, wb_ff1, wb_ff2, ln1_g, ln1_b, ln2_g, ln2_b)

    names = ("w_in", "w_out", "w_ff1", "w_ff2")

    def add_halves(kinds, grads, lands):
        return [_add_half("add_half_" + names[k], g, ld, core, COLS_SHARDED[k]) for k, g, ld in zip(kinds, grads, lands)]

    def sum_pieces(kinds, halves, lands, after):
        return [_sum_pieces("sum_pieces_" + names[k], h, ld, place, COLS_SHARDED[k], after)
                for k, h, ld in zip(kinds, halves, lands)]

    early = (1, 2, 3)
    g_out_local = _dw_out(cat_h, cat_c, dpre1b)
    g_ff2_local, dpc, g_conv = _dw_ff2(r, dpre2b, dcat, proj, conv_full)
    swap_a = _swap_halves((1, 3), (g_out_local, g_ff2_local))
    g_ff1_local, do, dog, g_gnw = _dw_ff1(h1b, da, dcat, o, proj, gate_norm_w, swap_a.token)
    swap_b = _swap_halves((2,), (g_ff1_local,))
    swapped_a = swap_a.wait(swap_b.token)
    halves_a = add_halves((1, 3), swapped_a[:2], swapped_a[2:])
    swapped_b = swap_b.wait(halves_a[1])
    halves = (halves_a[0], *add_halves((2,), swapped_b[:1], swapped_b[1:]), halves_a[1])
    exch = _exchange_pieces(early, halves)
    dph, g_lbl = _hgrn_bwd(proj, do, states, lb_logits, exch.token)
    g_in_local = _dw_in(xb, dph, dog, dpc, dph)

    late = (0,)
    swap = _swap_halves(late, (g_in_local,))
    grad_x = _in_bwd(dph, dog, dpc, wb_in, dpre1, swap.token)
    exchanged = exch.wait(grad_x)
    pack = jnp.concatenate([
        g_ln1_g, g_ln1_b, g_ln2_g, g_ln2_b,
        jnp.concatenate([g_lbl[0:1], g_lbl[1:2]], axis=1),
        jnp.concatenate([g_gnw, g_conv[0:1]], axis=1),
        jnp.concatenate([g_conv[1:2], g_conv[2:3]], axis=1),
        jnp.concatenate([loss8[0:1], jnp.zeros((1, D_MODEL - LANES), F32)], axis=1)], axis=0)
    swapped = swap.wait(exchanged[0])
    exch = _exchange_pieces(late, add_halves(late, swapped[:1], swapped[1:]), pack)
    join = _join_halves(early, sum_pieces(early, exchanged[:3], exchanged[3:], exch.token))
    g_w_out, g_w_ff1, g_w_ff2 = join.wait(join.token)
    d_ff1, nm_ff1, nv_ff1 = _adamw("adamw_w_ff1", w_ff1[0], g_w_ff1, m_w_ff1[0], v_w_ff1[0])
    d_ff2, nm_ff2, nv_ff2 = _adamw("adamw_w_ff2", w_ff2[0], g_w_ff2, m_w_ff2[0], v_w_ff2[0], d_ff1)
    d_out, nm_out, nv_out = _adamw("adamw_w_out", w_out[0], g_w_out, m_w_out[0], v_w_out[0], d_ff2)
    exchanged = exch.wait(d_out)
    tot = _sum_shared(exchanged[2], exchanged[3], 2 * chip1 + core)
    loss = tot[7, 0]
    join = _join_halves(late, sum_pieces(late, exchanged[:1], exchanged[1:2], tot))
    small = ("lb_logits", "gate_norm_w", "conv_w", "ln1_g", "ln1_b", "ln2_g", "ln2_b")
    small_out = _adamw_small(
        tot, chip1, (lb_logits, gate_norm_w, conv_w[0], ln1_g, ln1_b, ln2_g, ln2_b),
        (m_lb_logits, m_gate_norm_w, m_conv_w[0], m_ln1_g, m_ln1_b, m_ln2_g, m_ln2_b),
        (v_lb_logits, v_gate_norm_w, v_conv_w[0], v_ln1_g, v_ln1_b, v_ln2_g, v_ln2_b), join.token)
    g_w_in, = join.wait(small_out[0])
    d_in, nm_in, nv_in = _adamw("adamw_w_in", w_in[0], g_w_in, m_w_in[0], v_w_in[0])

    def results(n_kind, large):
        out = dict(zip(small, small_out[n_kind * len(small):(n_kind + 1) * len(small)]))
        out["conv_w"] = out["conv_w"][None]
        out.update({name: a[None] for name, a in zip(("w_in", "w_out", "w_ff1", "w_ff2"), large)})
        return [out[name] for name in ("w_in", "lb_logits", "gate_norm_w", "conv_w", "w_out", "ln1_g", "ln1_b",
                                       "w_ff1", "w_ff2", "ln2_g", "ln2_b")]

    return (loss, grad_x[None], *results(0, (g_w_in, g_w_out, g_w_ff1, g_w_ff2)),
            *results(1, (d_in, d_out, d_ff1, d_ff2)), *results(2, (nm_in, nm_out, nm_ff1, nm_ff2)),
            *results(3, (nv_in, nv_out, nv_ff1, nv_ff2)))
```

```python
import jax
import jax.numpy as jnp
from jax import lax
from jax.experimental import pallas as pl
from jax.experimental.pallas import tpu as pltpu

F32 = jnp.float32
BF16 = jnp.bfloat16
MXU_DTYPE = jnp.bfloat16

D_MODEL = 1024
HGRN_WIDTH = 512
HEAD_DIM = 128
N_HEADS = 4
CONV_WIDTH = 512
CHUNK = 64
D_FF = 4096
IN_COLS = 3584
GROUP = 512
N_GROUPS = IN_COLS // GROUP
ALPHA = 2.0 ** 0.25
EPS = 1e-5
N_CHIPS = 4
ADAM_LR, ADAM_B1, ADAM_B2, ADAM_EPS, ADAM_WD, ADAM_STEP = 0.001, 0.9, 0.999, 1e-08, 0.01, 10

LANES = 128
SUBLANES = 8
VMEM_LIMIT = 56 * 1024 * 1024
FF_BLOCK = 1024
N_FF = D_FF // FF_BLOCK
GATE_STRIP = 64

NN = (((1,), (0,)), ((), ()))
NT = (((1,), (1,)), ((), ()))
TN = (((0,), (0,)), ((), ()))
MESH = pl.DeviceIdType.MESH
ANY = pl.BlockSpec(memory_space=pl.ANY)


def _dot(a, b, dims):
    return lax.dot_general(a.astype(MXU_DTYPE), b.astype(MXU_DTYPE), dims, preferred_element_type=F32)


def _dot_exact(ones, v):
    ones = ones.astype(jnp.bfloat16)
    hi = v.astype(jnp.bfloat16)
    rest = v - hi.astype(F32)
    mid = rest.astype(jnp.bfloat16)
    low = (rest - mid.astype(F32)).astype(jnp.bfloat16)
    return sum(lax.dot_general(ones, part, NN, preferred_element_type=F32) for part in (hi, mid, low))


def _params(*sem):
    return pltpu.CompilerParams(dimension_semantics=sem, vmem_limit_bytes=VMEM_LIMIT)


def _resident(shape):
    return pl.BlockSpec(shape, lambda *_: (0,) * len(shape), pipeline_mode=pl.Buffered(1))


def _sigmoid(v):
    return 1.0 / (1.0 + jnp.exp(-v))


def _lower_bound(lbl):
    m = jnp.max(lbl, axis=0, keepdims=True)
    e = jnp.exp(lbl - m)
    s = e / jnp.sum(e, axis=0, keepdims=True)
    return s[0:1, :], s[1:2, :]


def _heads(v):
    return [v[:, h * HEAD_DIM:(h + 1) * HEAD_DIM] for h in range(N_HEADS)]


def _per_head(fn, *arrays):
    return jnp.concatenate([fn(*parts) for parts in zip(*map(_heads, arrays))], axis=1)


def _in_proj(x, w_in, conv_w, after):
    t = x.shape[0]
    tm = min(t, 512)

    def body(x_ref, w_ref, cw_ref, after_ref, o_ref, xb_ref, y_ref, zbuf):
        @pl.when(pl.program_id(0) == 0)
        def _():
            zbuf[tm:tm + SUBLANES, :] = jnp.zeros((SUBLANES, CONV_WIDTH), F32)

        xb = x_ref[...].astype(xb_ref.dtype)
        xb_ref[...] = xb
        for g in range(N_GROUPS):
            o_ref[g] = _dot(xb, w_ref[:, g * GROUP:(g + 1) * GROUP], NN)
        zbuf[0:SUBLANES, :] = zbuf[tm:tm + SUBLANES, :]
        zbuf[SUBLANES:SUBLANES + tm, :] = o_ref[5] * o_ref[6]
        cw = cw_ref[...]
        at = lambda shift: zbuf[shift:shift + tm, :]
        conv = cw[2:3, :] * at(SUBLANES) + cw[1:2, :] * at(SUBLANES - 1) + cw[0:1, :] * at(SUBLANES - 2)
        y_ref[...] = (o_ref[4] * conv).astype(y_ref.dtype)

    return pl.pallas_call(
        body, name="in_proj", grid=(t // tm,),
        in_specs=[pl.BlockSpec((tm, D_MODEL), lambda i: (i, 0)), _resident((D_MODEL, IN_COLS)),
                  pl.BlockSpec((3, CONV_WIDTH), lambda i: (0, 0)), ANY],
        out_specs=[pl.BlockSpec((N_GROUPS, tm, GROUP), lambda i: (0, i, 0)), pl.BlockSpec((tm, D_MODEL), lambda i: (i, 0)),
                   pl.BlockSpec((tm, CONV_WIDTH), lambda i: (i, 0))],
        out_shape=[jax.ShapeDtypeStruct((N_GROUPS, t, GROUP), F32), jax.ShapeDtypeStruct((t, D_MODEL), BF16),
                   jax.ShapeDtypeStruct((t, CONV_WIDTH), BF16)],
        scratch_shapes=[pltpu.VMEM((tm + SUBLANES, CONV_WIDTH), F32)],
        compiler_params=_params("arbitrary"),
    )(x, w_in, conv_w, after)


def _gates(fp, lb):
    sig = _sigmoid(fp)
    f = lb + (1.0 - lb) * sig
    return sig, f, jnp.log(f), 1.0 - f


def _chunk_masks():
    row = lax.broadcasted_iota(jnp.int32, (CHUNK, CHUNK), 0)
    col = lax.broadcasted_iota(jnp.int32, (CHUNK, CHUNK), 1)
    return row >= col, row <= col


def _hgrn_fwd(proj, lb_logits, after):
    t = proj.shape[1]
    tb = min(t, 512)
    ncb = tb // CHUNK

    def body(q_ref, f_ref, v_ref, lbl_ref, after_ref, o_ref, st_ref, s_scr):
        @pl.when(pl.program_id(0) == 0)
        def _():
            s_scr[...] = jnp.zeros_like(s_scr)

        lb, _ = _lower_bound(lbl_ref[...])
        causal, _ = _chunk_masks()

        every = range(ncb)
        rows = [slice(c * CHUNK, (c + 1) * CHUNK) for c in every]
        q, v = [q_ref[r, :] for r in rows], [v_ref[r, :] for r in rows]
        gates = [_gates(f_ref[r, :], lb) for r in rows]
        k = [gt[3] for gt in gates]
        b = [_dot_exact(causal, gt[2]) for gt in gates]
        mid, last = [x[CHUNK // 2:CHUNK // 2 + 1, :] for x in b], [x[CHUNK - 1:CHUNK, :] for x in b]
        qt = [q[c] * jnp.exp(b[c] - mid[c]) for c in every]
        kt = [k[c] * jnp.exp(mid[c] - b[c]) for c in every]
        qi = [q[c] * jnp.exp(b[c]) for c in every]
        ks = [k[c] * jnp.exp(last[c] - b[c]) for c in every]
        dec = [jnp.exp(x) for x in last]
        scores = [[jnp.where(causal, _dot(a, b_, NT), 0.0) for a, b_ in zip(_heads(qt[c]), _heads(kt[c]))] for c in every]
        intra = [[_dot(s, v_h, NN) for s, v_h in zip(scores[c], _heads(v[c]))] for c in every]
        update = [_per_head(lambda v_h, ks_h: _dot(v_h, ks_h, TN), v[c], ks[c]) for c in every]

        st = s_scr[...]
        states = []
        for c in every:
            states.append(st)
            st_ref[c] = st
            st = dec[c] * st + update[c]
        s_scr[...] = st

        o_ref[...] = jnp.concatenate(
            [jnp.concatenate([i_h + _dot(qi_h, st_h, NT) for i_h, qi_h, st_h in
                              zip(intra[c], _heads(qi[c]), _heads(states[c]))], axis=1) for c in every], axis=0)

    grp = lambda g: pl.BlockSpec((None, tb, GROUP), lambda i: (g, i, 0))
    return pl.pallas_call(
        body, name="hgrn_fwd", grid=(t // tb,),
        in_specs=[grp(0), grp(1), grp(2), pl.BlockSpec((2, HGRN_WIDTH), lambda i: (0, 0)), ANY],
        out_specs=[pl.BlockSpec((tb, HGRN_WIDTH), lambda i: (i, 0)),
                   pl.BlockSpec((ncb, HEAD_DIM, HGRN_WIDTH), lambda i: (i, 0, 0))],
        out_shape=[jax.ShapeDtypeStruct((t, HGRN_WIDTH), F32),
                   jax.ShapeDtypeStruct((t // CHUNK, HEAD_DIM, HGRN_WIDTH), F32)],
        scratch_shapes=[pltpu.VMEM((HEAD_DIM, HGRN_WIDTH), F32)],
        compiler_params=_params("arbitrary"),
    )(proj, proj, proj, lb_logits, after)


def _gate_fwd(proj, o, gate_norm_w, after):
    t = proj.shape[1]
    tb = min(t, 512)

    def body(o_ref, og_ref, gnw_ref, after_ref, out_ref):
        gnw = gnw_ref[...]
        for s in range(tb // GATE_STRIP):
            rows = slice(s * GATE_STRIP, (s + 1) * GATE_STRIP)
            og = og_ref[rows, :]
            on = _per_head(lambda o_h: o_h * lax.rsqrt(jnp.mean(o_h * o_h, axis=-1, keepdims=True) + EPS), o_ref[rows, :])
            out_ref[rows, :] = (on * gnw * (og * _sigmoid(og))).astype(out_ref.dtype)

    tile = pl.BlockSpec((tb, GROUP), lambda i: (i, 0))
    return pl.pallas_call(
        body, name="gate_fwd", grid=(t // tb,),
        in_specs=[tile, pl.BlockSpec((None, tb, GROUP), lambda i: (3, i, 0)), pl.BlockSpec((1, GROUP), lambda i: (0, 0)), ANY],
        out_specs=tile,
        out_shape=jax.ShapeDtypeStruct((t, HGRN_WIDTH), BF16),
        compiler_params=_params("parallel"),
    )(o, proj, gate_norm_w, after)


def _ln_bwd(dy, xhat, rstd, g):
    dxhat = dy * g
    m1 = jnp.mean(dxhat, axis=-1, keepdims=True)
    m2 = jnp.mean(dxhat * xhat, axis=-1, keepdims=True)
    return rstd * (dxhat - m1 - xhat * m2)


def _layer_norm(pre):
    xc = pre - jnp.mean(pre, axis=-1, keepdims=True)
    rstd = lax.rsqrt(jnp.mean(xc * xc, axis=-1, keepdims=True) + EPS)
    return xc * rstd, rstd


def _sublayers(cat_h, cat_c, x, target, w_out, w_ff1, w_ff2, g1, b1, g2, b2):
    t = x.shape[0]
    tm = min(t, 256)

    def body(ch_ref, cc_ref, x_ref, tg_ref, wo_ref, w1_ref, w2_ref, g1_ref, b1_ref, g2_ref, b2_ref,
             h1_ref, r_ref, da_ref, dp2b_ref, dp1_ref, dp1b_ref, dcat_ref, dg1_ref, db1_ref, dg2_ref, db2_ref, loss_ref):
        @pl.when(pl.program_id(0) == 0)
        def _():
            for ref in (dg1_ref, db1_ref, dg2_ref, db2_ref, loss_ref):
                ref[...] = jnp.zeros_like(ref)

        mix = _dot(ch_ref[...], wo_ref[0:GROUP, :], NN) + _dot(cc_ref[...], wo_ref[GROUP:2 * GROUP, :], NN)
        xhat1, rstd1 = _layer_norm(ALPHA * x_ref[...] + mix)
        h1 = xhat1 * g1_ref[...] + b1_ref[...]
        h1b = h1.astype(h1_ref.dtype)
        h1_ref[...] = h1b
        mlp = jnp.zeros((tm, D_MODEL), F32)
        for j in range(N_FF):
            cols = slice(j * FF_BLOCK, (j + 1) * FF_BLOCK)
            r = jnp.square(jnp.maximum(_dot(h1b, w1_ref[:, cols], NN), 0.0)).astype(r_ref.dtype)
            r_ref[:, cols] = r
            mlp = mlp + _dot(r, w2_ref[cols, :], NN)
        xhat2, rstd2 = _layer_norm(ALPHA * h1 + mlp)
        err = xhat2 * g2_ref[...] + b2_ref[...] - tg_ref[...]
        loss_ref[...] += 0.5 * jnp.sum(jnp.mean(err * err, axis=-1, keepdims=True))
        dy = err * (1.0 / D_MODEL)
        dg2_ref[...] += jnp.sum(dy * xhat2, axis=0, keepdims=True)
        db2_ref[...] += jnp.sum(dy, axis=0, keepdims=True)
        dp2 = _ln_bwd(dy, xhat2, rstd2, g2_ref[...])
        dp2b = dp2.astype(dp2b_ref.dtype)
        dp2b_ref[...] = dp2b
        back = jnp.zeros((tm, D_MODEL), F32)
        for j in range(N_FF):
            cols = slice(j * FF_BLOCK, (j + 1) * FF_BLOCK)
            dr = _dot(dp2b, w2_ref[cols, :], NT)
            da = (dr * (2.0 * jnp.sqrt(r_ref[:, cols].astype(F32)))).astype(da_ref.dtype)
            da_ref[:, cols] = da
            back = back + _dot(da, w1_ref[:, cols], NT)
        dh1 = ALPHA * dp2 + back
        dg1_ref[...] += jnp.sum(dh1 * xhat1, axis=0, keepdims=True)
        db1_ref[...] += jnp.sum(dh1, axis=0, keepdims=True)
        dp1 = _ln_bwd(dh1, xhat1, rstd1, g1_ref[...])
        dp1b = dp1.astype(dp1b_ref.dtype)
        dp1_ref[...] = dp1
        dp1b_ref[...] = dp1b
        dcat_ref[...] = _dot(dp1b, wo_ref[...], NT)

    row = pl.BlockSpec((tm, D_MODEL), lambda i: (i, 0))
    wide = pl.BlockSpec((tm, D_FF), lambda i: (i, 0))
    vec = pl.BlockSpec((1, D_MODEL), lambda i: (0, 0))
    narrow = lambda dtype: jax.ShapeDtypeStruct((t, D_MODEL), dtype)
    return pl.pallas_call(
        body, name="sublayers", grid=(t // tm,),
        in_specs=[pl.BlockSpec((tm, GROUP), lambda i: (i, 0)), pl.BlockSpec((tm, GROUP), lambda i: (i, 0)), row, row,
                  _resident((D_MODEL, D_MODEL)),
                  _resident((D_MODEL, D_FF)), _resident((D_FF, D_MODEL)), vec, vec, vec, vec],
        out_specs=[row, wide, wide, row, row, row, row, vec, vec, vec, vec,
                   pl.BlockSpec((SUBLANES, LANES), lambda i: (0, 0))],
        out_shape=[narrow(BF16), jax.ShapeDtypeStruct((t, D_FF), BF16), jax.ShapeDtypeStruct((t, D_FF), BF16),
                   narrow(BF16), narrow(F32), narrow(BF16), narrow(F32)]
                  + [jax.ShapeDtypeStruct((1, D_MODEL), F32)] * 4 + [jax.ShapeDtypeStruct((SUBLANES, LANES), F32)],
        compiler_params=_params("arbitrary"),
    )(cat_h, cat_c, x, target, w_out, w_ff1, w_ff2, g1, b1, g2, b2)


def _hgrn_bwd(proj, do, states, lb_logits, after):
    t = proj.shape[1]
    tb = min(t, 512)
    ncb = tb // CHUNK
    nblk = t // tb

    def body(q_ref, f_ref, v_ref, do_ref, st_ref, lbl_ref, after_ref, dp_ref, dlbl_ref, ds_scr, dlb_scr):
        i = pl.program_id(0)

        @pl.when(i == 0)
        def _():
            ds_scr[...] = jnp.zeros_like(ds_scr)
            dlb_scr[...] = jnp.zeros_like(dlb_scr)

        lb, s1 = _lower_bound(lbl_ref[...])
        causal, anti = _chunk_masks()
        every = range(ncb)
        rows = [slice(c * CHUNK, (c + 1) * CHUNK) for c in every]
        q, v, do = ([ref[r, :] for r in rows] for ref in (q_ref, v_ref, do_ref))
        st = [st_ref[c] for c in every]
        gates = [_gates(f_ref[r, :], lb) for r in rows]
        sig, f, k = ([gt[n] for gt in gates] for n in (0, 1, 3))
        b = [_dot_exact(causal, gt[2]) for gt in gates]
        mid, last = [x[CHUNK // 2:CHUNK // 2 + 1, :] for x in b], [x[CHUNK - 1:CHUNK, :] for x in b]
        e_q = [jnp.exp(b[c] - mid[c]) for c in every]
        e_k = [jnp.exp(mid[c] - b[c]) for c in every]
        e_i = [jnp.exp(x) for x in b]
        e_s = [jnp.exp(last[c] - b[c]) for c in every]
        dec = [jnp.exp(x) for x in last]
        qt, kt, qi, ks = ([a[c] * e[c] for c in every] for a, e in ((q, e_q), (k, e_k), (q, e_i), (k, e_s)))

        def masked(a, b_):
            return [[jnp.where(causal, _dot(a_h, b_h, NT), 0.0) for a_h, b_h in zip(_heads(a[c]), _heads(b_[c]))]
                    for c in every]

        def with_scores(s, other, dims):
            return [jnp.concatenate([_dot(s_h, o_h, dims) for s_h, o_h in zip(s[c], _heads(other[c]))], axis=1)
                    for c in every]

        def per_head(dims, a, b_):
            return [_per_head(lambda a_h, b_h: _dot(a_h, b_h, dims), a[c], b_[c]) for c in every]

        scores, dscores = masked(qt, kt), masked(do, v)
        dqt, dkt, dv_intra = with_scores(dscores, kt, NN), with_scores(dscores, qt, TN), with_scores(scores, do, TN)
        dqi, update = per_head(NN, do, st), per_head(TN, do, qi)

        dst = ds_scr[...]
        dsts = [None] * ncb
        for c in reversed(every):
            dsts[c] = dst
            dst = dec[c] * dst + update[c]
        ds_scr[...] = dst

        dv_state, dks = per_head(NT, ks, dsts), per_head(NN, v, dsts)
        ddec = [jnp.sum(dsts[c] * st[c], axis=0, keepdims=True) for c in every]
        dq = [dqt[c] * e_q[c] + dqi[c] * e_i[c] for c in every]
        dk = [dkt[c] * e_k[c] + dks[c] * e_s[c] for c in every]
        db = [q[c] * dq[c] - k[c] * dk[c] for c in every]
        db_last = [jnp.sum(dks[c] * ks[c], axis=0, keepdims=True) + ddec[c] * dec[c] for c in every]
        dg = [_dot_exact(anti, db[c]) + db_last[c] for c in every]
        df = [dg[c] / f[c] - dk[c] for c in every]
        dlb_scr[...] += sum(jnp.sum(df[c] * (1.0 - sig[c]), axis=0, keepdims=True) for c in every)
        dfp = [df[c] * (1.0 - lb) * sig[c] * (1.0 - sig[c]) for c in every]
        dv = [dv_intra[c] + dv_state[c] for c in every]
        for n, parts in enumerate((dq, dfp, dv)):
            dp_ref[n] = jnp.concatenate(parts, axis=0).astype(dp_ref.dtype)

        @pl.when(i == nblk - 1)
        def _():
            dlb = dlb_scr[...]
            dlbl_ref[0:1, :] = dlb * lb * (1.0 - lb)
            dlbl_ref[1:2, :] = -dlb * lb * s1

    grp = lambda g: pl.BlockSpec((None, tb, GROUP), lambda i: (g, nblk - 1 - i, 0))
    vec = pl.BlockSpec((2, HGRN_WIDTH), lambda i: (0, 0))
    return pl.pallas_call(
        body, name="hgrn_bwd", grid=(nblk,),
        in_specs=[grp(0), grp(1), grp(2), pl.BlockSpec((tb, HGRN_WIDTH), lambda i: (nblk - 1 - i, 0)),
                  pl.BlockSpec((ncb, HEAD_DIM, HGRN_WIDTH), lambda i: (nblk - 1 - i, 0, 0)), vec, ANY],
        out_specs=[pl.BlockSpec((3, tb, HGRN_WIDTH), lambda i: (0, nblk - 1 - i, 0)), vec],
        out_shape=[jax.ShapeDtypeStruct((3, t, HGRN_WIDTH), BF16), jax.ShapeDtypeStruct((2, HGRN_WIDTH), F32)],
        scratch_shapes=[pltpu.VMEM((HEAD_DIM, HGRN_WIDTH), F32), pltpu.VMEM((1, HGRN_WIDTH), F32)],
        compiler_params=_params("arbitrary"),
    )(proj, proj, proj, do, states, lb_logits, after)


def _in_bwd(dph, dog, dpc, w_in, dpre1, after):
    t = dpre1.shape[0]
    tm = min(t, 512)

    def body(dh_ref, dog_ref, dc_ref, w_ref, dp_ref, after_ref, o_ref):
        acc = ALPHA * dp_ref[...]
        for g in range(N_GROUPS):
            part = dh_ref[g] if g < 3 else dog_ref[...] if g == 3 else dc_ref[g - 4]
            acc = acc + _dot(part, w_ref[:, g * GROUP:(g + 1) * GROUP], NT)
        o_ref[...] = acc

    row = pl.BlockSpec((tm, D_MODEL), lambda i: (i, 0))
    three = pl.BlockSpec((3, tm, GROUP), lambda i: (0, i, 0))
    return pl.pallas_call(
        body, name="in_bwd", grid=(t // tm,),
        in_specs=[three, pl.BlockSpec((tm, GROUP), lambda i: (i, 0)), three, _resident((D_MODEL, IN_COLS)), row, ANY],
        out_specs=row,
        out_shape=jax.ShapeDtypeStruct((t, D_MODEL), F32),
        compiler_params=_params("parallel"),
    )(dph, dog, dpc, w_in, dpre1, after)


GRAD_TILE = 512
OUT_PARTS = 4


class _Side:
    def __init__(self, operands, in_specs, out_shape, out_specs, scratch, init, begin):
        self.operands, self.in_specs, self.out_shape, self.out_specs = operands, in_specs, out_shape, out_specs
        self.scratch, self.init, self.begin = scratch, init, begin


def _grad_w(name, operands, widths, shape, step, after=None, side=None):
    t = operands[0].shape[-2]
    tt = min(t, GRAD_TILE)
    n_in, n_steps = len(operands), t // tt
    in_specs = [pl.BlockSpec((tt, w), lambda k: (k, 0)) if a.ndim == 2 else
                pl.BlockSpec((a.shape[0], tt, w), lambda k: (0, k, 0)) for a, w in zip(operands, widths)]
    extra = [] if after is None else [after]
    s_in, s_out = (len(side.operands), len(side.out_shape)) if side else (0, 0)
    first_out = n_in + s_in + len(extra)

    def body(*refs):
        o_ref, side_outs = refs[first_out], refs[first_out + 1:first_out + 1 + s_out]
        acc, narrow, sem = refs[first_out + 1 + s_out:first_out + 4 + s_out]
        k = pl.program_id(0)

        @pl.when(k == 0)
        def _():
            acc[...] = jnp.zeros_like(acc)
            if side:
                side.init(side_outs)

        tick = side.begin(k, n_steps, refs[n_in:n_in + s_in], side_outs, refs[first_out + 4 + s_out:]) if side else None
        step(acc, *refs[:n_in], tick or (lambda j: None))

        @pl.when(k == n_steps - 1)
        def _():
            part = shape[0] // OUT_PARTS
            copies = []
            for p in range(OUT_PARTS):
                rows = pl.ds(p * part, part)
                narrow[rows, :] = acc[rows, :].astype(narrow.dtype)
                copies.append(pltpu.make_async_copy(narrow.at[rows, :], o_ref.at[rows, :], sem.at[p]))
                copies[-1].start()
            for cp in copies:
                cp.wait()

    outs = pl.pallas_call(
        body, name=name, grid=(n_steps,),
        in_specs=in_specs + (side.in_specs if side else []) + [ANY] * len(extra),
        out_specs=[ANY] + (side.out_specs if side else []),
        out_shape=[jax.ShapeDtypeStruct(shape, BF16)] + (side.out_shape if side else []),
        scratch_shapes=[pltpu.VMEM(shape, F32), pltpu.VMEM(shape, BF16), pltpu.SemaphoreType.DMA((OUT_PARTS,))]
                       + (side.scratch if side else []),
        compiler_params=_params("arbitrary"),
    )(*operands, *(side.operands if side else ()), *extra)
    return outs if side else outs[0]


def _dw_in(xb, dph, dog, dpc, after):
    def step(acc, x_ref, dh_ref, dog_ref, dc_ref, tick):
        xv = x_ref[...]
        for g in range(N_GROUPS):
            part = dh_ref[g] if g < 3 else dog_ref[...] if g == 3 else dc_ref[g - 4]
            acc[:, g * GROUP:(g + 1) * GROUP] += _dot(xv, part, TN)

    return _grad_w("dw_in", (xb, dph, dog, dpc), (D_MODEL, GROUP, GROUP, GROUP), (D_MODEL, IN_COLS), step, after)


def _dw_out(cat_h, cat_c, dpre1b):
    def step(acc, h_ref, c_ref, d_ref, tick):
        dv = d_ref[...]
        acc[0:GROUP, :] += _dot(h_ref[...], dv, TN)
        acc[GROUP:2 * GROUP, :] += _dot(c_ref[...], dv, TN)

    return _grad_w("dw_out", (cat_h, cat_c, dpre1b), (GROUP, GROUP, D_MODEL), (D_MODEL, D_MODEL), step)


def _strips_of(j, tt):
    per_tick = tt // GATE_STRIP // N_FF
    return [slice(s * GATE_STRIP, (s + 1) * GATE_STRIP) for s in range(j * per_tick, (j + 1) * per_tick)]


def _dw_ff1(h1b, da, dcat, o, proj, gate_norm_w, after):
    t = h1b.shape[0]
    tt = min(t, GRAD_TILE)

    def step(acc, h_ref, da_ref, tick):
        hv = h_ref[...]
        for j in range(N_FF):
            cols = slice(j * FF_BLOCK, (j + 1) * FF_BLOCK)
            acc[:, cols] += _dot(hv, da_ref[:, cols], TN)
            tick(j)

    def init(outs):
        outs[2][...] = jnp.zeros_like(outs[2])

    def begin(k, n_steps, ins, outs, scratch):
        do2_ref, o_ref, og_ref, gnw_ref = ins
        do_ref, dog_ref, dgnw_ref = outs
        total = [jnp.zeros((GATE_STRIP, GROUP), F32)]

        def tick(j):
            gnw = gnw_ref[...]
            for rows in _strips_of(j, tt):
                ov, og, do2 = o_ref[rows, :], og_ref[rows, :], do2_ref[rows, :]
                rs = _per_head(lambda o_h: jnp.broadcast_to(
                    lax.rsqrt(jnp.mean(o_h * o_h, axis=-1, keepdims=True) + EPS), o_h.shape), ov)
                on = ov * rs
                sg = _sigmoid(og)
                sil = og * sg
                don = do2 * gnw * sil
                total[0] = total[0] + do2 * on * sil
                dog_ref[rows, :] = (do2 * on * gnw * (sg * (1.0 + og * (1.0 - sg)))).astype(dog_ref.dtype)
                do_ref[rows, :] = rs * (don - on * _per_head(
                    lambda p_h: jnp.broadcast_to(jnp.mean(p_h, axis=-1, keepdims=True), p_h.shape), don * on))
            if j == N_FF - 1:
                dgnw_ref[...] += jnp.sum(total[0], axis=0, keepdims=True)

        return tick

    tile = pl.BlockSpec((tt, GROUP), lambda k: (k, 0))
    vec = pl.BlockSpec((1, GROUP), lambda k: (0, 0))
    side = _Side(
        (dcat, o, proj, gate_norm_w), [tile, tile, pl.BlockSpec((None, tt, GROUP), lambda k: (3, k, 0)), vec],
        [jax.ShapeDtypeStruct((t, HGRN_WIDTH), F32), jax.ShapeDtypeStruct((t, HGRN_WIDTH), BF16),
         jax.ShapeDtypeStruct((1, HGRN_WIDTH), F32)], [tile, tile, vec], [], init, begin)
    return _grad_w("dw_ff1", (h1b, da), (D_MODEL, D_FF), (D_MODEL, D_FF), step, after, side)


def _dw_ff2(r, dpre2b, dcat, proj, conv_w):
    t = r.shape[0]
    tt = min(t, GRAD_TILE)
    hb = tt // SUBLANES

    def step(acc, r_ref, d_ref, tick):
        dv = d_ref[...]
        for j in range(N_FF):
            rows = slice(j * FF_BLOCK, (j + 1) * FF_BLOCK)
            acc[rows, :] += _dot(r_ref[:, rows], dv, TN)
            tick(j)

    def init(outs):
        outs[1][...] = jnp.zeros_like(outs[1])

    def begin(k, n_steps, ins, outs, scratch):
        dy_ref, dyn_ref, b_ref, bn_ref, c_ref, u_ref, ch_ref, uh_ref, cw_ref = ins
        dp_ref, dcw_ref = outs
        zbuf, dbuf = scratch
        zbuf[0:SUBLANES, :] = jnp.where(k > 0, ch_ref[...] * uh_ref[...], 0.0)
        zbuf[SUBLANES:SUBLANES + tt, :] = c_ref[...] * u_ref[...]
        dbuf[0:tt, :] = dy_ref[...] * b_ref[...]
        dbuf[tt:tt + SUBLANES, :] = jnp.where(k < n_steps - 1, dyn_ref[...] * bn_ref[...], 0.0)
        totals = [jnp.zeros((GATE_STRIP, GROUP), F32) for _ in range(3)]

        def tick(j):
            cw = cw_ref[...]
            for rows in _strips_of(j, tt):
                at = lambda buf, shift: buf[shift + rows.start:shift + rows.stop, :]
                z, z1, z2 = at(zbuf, SUBLANES), at(zbuf, SUBLANES - 1), at(zbuf, SUBLANES - 2)
                dyc, d1, d2 = at(dbuf, 0), at(dbuf, 1), at(dbuf, 2)
                yc = cw[2:3, :] * z + cw[1:2, :] * z1 + cw[0:1, :] * z2
                dz = cw[2:3, :] * dyc + cw[1:2, :] * d1 + cw[0:1, :] * d2
                dp_ref[0, rows, :] = (dy_ref[rows, :] * yc).astype(dp_ref.dtype)
                dp_ref[1, rows, :] = (dz * u_ref[rows, :]).astype(dp_ref.dtype)
                dp_ref[2, rows, :] = (dz * c_ref[rows, :]).astype(dp_ref.dtype)
                for n, tap in enumerate((z2, z1, z)):
                    totals[n] = totals[n] + dyc * tap
            if j == N_FF - 1:
                for n in range(3):
                    dcw_ref[n:n + 1, :] += jnp.sum(totals[n], axis=0, keepdims=True)

        return tick

    grp = lambda g: pl.BlockSpec((None, tt, GROUP), lambda k: (g, k, 0))
    prev = lambda g: pl.BlockSpec((None, SUBLANES, GROUP), lambda k: (g, jnp.maximum(k * hb - 1, 0), 0))
    nxt_row = lambda k: jnp.minimum((k + 1) * hb, t // SUBLANES - 1)
    nxt = lambda g: pl.BlockSpec((None, SUBLANES, GROUP), lambda k: (g, nxt_row(k), 0))
    whole = pl.BlockSpec((3, CONV_WIDTH), lambda k: (0, 0))
    side = _Side(
        (dcat, dcat, proj, proj, proj, proj, proj, proj, conv_w),
        [pl.BlockSpec((tt, GROUP), lambda k: (k, 1)), pl.BlockSpec((SUBLANES, GROUP), lambda k: (nxt_row(k), 1)),
         grp(4), nxt(4), grp(5), grp(6), prev(5), prev(6), whole],
        [jax.ShapeDtypeStruct((3, t, CONV_WIDTH), BF16), jax.ShapeDtypeStruct((3, CONV_WIDTH), F32)],
        [pl.BlockSpec((3, tt, GROUP), lambda k: (0, k, 0)), whole],
        [pltpu.VMEM((tt + SUBLANES, GROUP), F32), pltpu.VMEM((tt + SUBLANES, GROUP), F32)], init, begin)
    return _grad_w("dw_ff2", (r, dpre2b), (D_FF, D_MODEL), (D_FF, D_MODEL), step, side=side)


def _place():
    x, y, c = lax.axis_index("x"), lax.axis_index("y"), lax.axis_index("c")
    return x, y, c, 2 * x + y


def _other_chips(x, y):
    return [(1 - x, y), (x, 1 - y), (1 - x, 1 - y)]


def _place_shard(name, w, chip, cols_sharded, after=None):
    rows, cols = w.shape
    tr = min(rows, 256)
    nb = rows // tr
    full = (rows, cols * N_CHIPS) if cols_sharded else (rows * N_CHIPS, cols)
    out_map = (lambda i, s: (i, s[0])) if cols_sharded else (lambda i, s: (s[0] * nb + i, 0))

    def body(s_ref, w_ref, *rest):
        rest[-1][...] = w_ref[...].astype(rest[-1].dtype)

    extra = [] if after is None else [after]
    return pl.pallas_call(
        body, name=name,
        grid_spec=pltpu.PrefetchScalarGridSpec(
            num_scalar_prefetch=1, grid=(nb,),
            in_specs=[pl.BlockSpec((tr, cols), lambda i, s: (i, 0))] + [ANY] * len(extra),
            out_specs=pl.BlockSpec((tr, cols), out_map)),
        out_shape=jax.ShapeDtypeStruct(full, BF16),
        compiler_params=_params("parallel"),
    )(chip, w, *extra)


HBM = pl.BlockSpec(memory_space=pltpu.HBM)
SEM = pl.BlockSpec(memory_space=pltpu.SEMAPHORE)
EFFECT = pltpu.SideEffectType.DATAFLOW_SIDE_EFFECTING


PEER_SETS = {
    "sibling": (0, lambda x, y, c: [(x, y, 1 - c)]),
    "chips": (1, lambda x, y, c: [(1 - x, y, c), (x, 1 - y, c), (1 - x, 1 - y, c)]),
    "neighbours": (2, lambda x, y, c: [(1 - x, y, c), (x, 1 - y, c)]),
}


class _Split:
    def __init__(self, name, arrays, plan, others=(), peers=None):
        n_own, arrays = len(arrays), (*arrays, *others)
        n, n_copies = len(arrays), plan.count
        self.name, self.plan, self.n = name, plan, n_own
        barrier_id, peer_ids = PEER_SETS[peers] if peers else (None, None)

        def body(*refs):
            if peers:
                x, y, c, _ = _place()
                barrier = pltpu.get_barrier_semaphore()
                for peer in peer_ids(x, y, c):
                    pl.semaphore_signal(barrier, inc=1, device_id=peer, device_id_type=MESH)
                pl.semaphore_wait(barrier, len(peer_ids(0, 0, 0)))
            send_sems, recv_sems, token = refs[n], refs[n + 1], refs[-1]
            for k, (src, dst, to) in enumerate(plan(refs[:n])):
                pltpu.make_async_remote_copy(src_ref=src, dst_ref=dst, send_sem=send_sems.at[k], recv_sem=recv_sems.at[k],
                                             device_id=to, device_id_type=MESH).start()
            token[...] = jnp.zeros_like(token)

        outs = pl.pallas_call(
            body, name=name + "_start",
            out_shape=(pltpu.SemaphoreType.DMA((n_copies,)), pltpu.SemaphoreType.DMA((n_copies,)),
                       *[pltpu.HBM(a.shape, a.dtype) for a in arrays], jax.ShapeDtypeStruct((SUBLANES, LANES), F32)),
            in_specs=(HBM,) * n, out_specs=(SEM, SEM) + (HBM,) * n + (pl.BlockSpec(memory_space=pltpu.VMEM),),
            input_output_aliases={i: 2 + i for i in range(n)},
            compiler_params=pltpu.CompilerParams(has_side_effects=EFFECT, collective_id=barrier_id),
        )(*[pltpu.with_memory_space_constraint(a, pltpu.HBM) for a in arrays])
        self.sems, self.arrays, self.others, self.token = outs[:2], outs[2:2 + n_own], outs[2 + n_own:2 + n], outs[-1]

    def wait(self, after):
        n, plan = self.n, self.plan

        def body(*refs):
            send_sems, recv_sems = refs[n], refs[n + 1]
            for k, (src, dst, to) in enumerate(plan(refs[:n])):
                cp = pltpu.make_async_remote_copy(src_ref=src, dst_ref=dst, send_sem=send_sems.at[k],
                                                  recv_sem=recv_sems.at[k], device_id=to, device_id_type=MESH)
                cp.wait_send()
                cp.wait_recv()

        return pl.pallas_call(
            body, name=self.name + "_wait", out_shape=tuple(pltpu.HBM(a.shape, a.dtype) for a in self.arrays),
            in_specs=(HBM,) * n + (SEM, SEM, ANY), out_specs=(HBM,) * n, input_output_aliases={i: i for i in range(n)},
            compiler_params=pltpu.CompilerParams(has_side_effects=EFFECT),
        )(*self.arrays, *self.sems, after)


COLS_SHARDED = (True, False, True, False)
HALF_SHAPES = [(D_MODEL // 2, IN_COLS), (D_MODEL, D_MODEL // 2), (D_MODEL // 2, D_FF), (D_FF, D_MODEL // 2)]
PIECE_SHAPES = [(D_MODEL // 2, IN_COLS // N_CHIPS), (D_MODEL // N_CHIPS, D_MODEL // 2),
                (D_MODEL // 2, D_FF // N_CHIPS), (D_FF // N_CHIPS, D_MODEL // 2)]


def _shard_view(kind, ref, chip):
    if COLS_SHARDED[kind]:
        n = ref.shape[1] // N_CHIPS
        return ref.at[:, pl.ds(chip * n, n)]
    n = ref.shape[0] // N_CHIPS
    return ref.at[pl.ds(chip * n, n), :]


def _half_view(kind, ref, h):
    if COLS_SHARDED[kind]:
        n = ref.shape[0] // 2
        return ref.at[pl.ds(h * n, n), :]
    n = ref.shape[1] // 2
    return ref.at[:, pl.ds(h * n, n)]


def _plan(count):
    def mark(fn):
        fn.count = count
        return fn
    return mark


def _shard_rows_view(kind, ref, chip, part, n_parts):
    if COLS_SHARDED[kind]:
        m, n = ref.shape[0] // n_parts, ref.shape[1] // N_CHIPS
        return ref.at[pl.ds(part * m, m), pl.ds(chip * n, n)]
    m = ref.shape[0] // N_CHIPS // n_parts
    return ref.at[pl.ds((n_parts * chip + part) * m, m), :]


def _shard_half_view(kind, ref, chip, h):
    return _shard_rows_view(kind, ref, chip, h, 2)


def _gather_over_ici(kinds, weights):
    @_plan(2 * len(kinds))
    def plan(refs):
        x, y, c, me = _place()
        mine = [_shard_half_view(kind, ref, me, c) for kind, ref in zip(kinds, refs)]
        return [(v, v, to) for v in mine for to in ((1 - x, y, c), (x, 1 - y, c))]

    return _Split("gather_ici_" + "".join(map(str, kinds)), tuple(weights), plan, peers="neighbours")


def _relay_over_ici(kinds, weights, others=()):
    @_plan(2 * len(kinds))
    def plan(refs):
        x, y, c, _ = _place()
        x_nbr, y_nbr = 2 * (1 - x) + y, 2 * x + (1 - y)
        out = []
        for kind, ref in zip(kinds, refs):
            first, second = (_shard_rows_view(kind, ref, chip, 2 * c + q, 4) for q, chip in ((0, x_nbr), (1, y_nbr)))
            out += [(first, first, (x, 1 - y, c)), (second, second, (1 - x, y, c))]
        return out

    return _Split("relay_ici_" + "".join(map(str, kinds)), tuple(weights), plan, others, peers="neighbours")


def _gather_w_in_over_ici(w_in, conv4):
    @_plan(6)
    def plan(refs):
        x, y, c, me = _place()
        half, conv = _shard_half_view(0, refs[0], me, c), refs[1].at[me]
        return [(v, v, (px, py, c)) for v in (half, conv) for px, py in _other_chips(x, y)]

    return _Split("gather_w_in_ici", (w_in, conv4), plan, peers="chips")


def _gather_over_d2d(kinds, weights):
    @_plan(3 * len(kinds))
    def plan(refs):
        x, y, c, _ = _place()
        got = [_shard_half_view(kind, ref, 2 * px + py, c) for kind, ref in zip(kinds, refs)
               for px, py in _other_chips(x, y)]
        return [(v, v, (x, y, 1 - c)) for v in got]

    return _Split("gather_d2d_" + "".join(map(str, kinds)), tuple(weights), plan, peers="sibling")


def _swap_halves(kinds, grads):
    @_plan(len(kinds))
    def plan(refs):
        x, y, c, _ = _place()
        return [(_half_view(kind, g, 1 - c), land, (x, y, 1 - c))
                for kind, g, land in zip(kinds, refs[:len(kinds)], refs[len(kinds):])]

    lands = [lax.empty(HALF_SHAPES[kind], g.dtype) for kind, g in zip(kinds, grads)]
    return _Split("swap_halves_" + "".join(map(str, kinds)), (*grads, *lands), plan, peers="sibling")


def _block_rows(cols, elements):
    return 1 << ((elements // cols).bit_length() - 1)


def _add_half(name, g, recv, core, rows_split):
    shape = recv.shape
    tr = min(shape[0], _block_rows(shape[1], 1 << 20))
    nb = shape[0] // tr

    def body(c_ref, g_ref, r_ref, o_ref):
        o_ref[...] = (g_ref[...].astype(F32) + r_ref[...].astype(F32)).astype(o_ref.dtype)

    g_map = (lambda i, c_ref: (c_ref[0] * nb + i, 0)) if rows_split else (lambda i, c_ref: (i, c_ref[0]))
    blk = pl.BlockSpec((tr, shape[1]), lambda i, c_ref: (i, 0))
    return pl.pallas_call(
        body, name=name,
        grid_spec=pltpu.PrefetchScalarGridSpec(
            num_scalar_prefetch=1, grid=(nb,),
            in_specs=[pl.BlockSpec((tr, shape[1]), g_map), blk], out_specs=blk),
        out_shape=jax.ShapeDtypeStruct(shape, BF16),
        compiler_params=_params("parallel"),
    )(core, g, recv)


def _exchange_pieces(kinds, halves, pack=None):
    n_p, n = N_CHIPS - 1, len(kinds)

    @_plan(n_p * n + (0 if pack is None else N_DEV - 1))
    def plan(refs):
        x, y, c, _ = _place()
        copies = []
        if pack is not None:
            me = 4 * x + 2 * y + c
            peers = [((1 - x) if m & 4 else x, (1 - y) if m & 2 else y, (1 - c) if m & 1 else c) for m in range(1, N_DEV)]
            copies += [(refs[2 * n], refs[2 * n + 1].at[me], peer) for peer in peers]
        return copies + [(_shard_view(kind, half, 2 * px + py), land.at[j], (px, py, c))
                         for j, (px, py) in enumerate(_other_chips(x, y))
                         for kind, half, land in zip(kinds, refs[:n], refs[n:2 * n])]

    lands = [lax.empty((n_p,) + PIECE_SHAPES[kind], BF16) for kind in kinds]
    small = () if pack is None else (pack, lax.empty((N_DEV,) + pack.shape, F32))
    return _Split("exchange_pieces_" + "".join(map(str, kinds)), (*halves, *lands, *small), plan,
                  peers="chips" if pack is None else None)


def _sum_pieces(name, half, slots, place, rows_split, after):
    n_p, rows, cols = slots.shape
    tr = min(rows, _block_rows(cols, 1 << 19))
    nb = rows // tr
    if rows_split:
        own_map = lambda i, s: (i, s[0])
        out_map = lambda i, s: (s[1] * nb + i, 0)
        shard = (2 * rows, cols)
    else:
        own_map = lambda i, s: (s[0] * nb + i, 0)
        out_map = lambda i, s: (i, s[1])
        shard = (rows, 2 * cols)

    def body(s_ref, own_ref, slot_ref, after_ref, o_ref):
        total = own_ref[...].astype(F32)
        for j in range(n_p):
            total = total + slot_ref[j].astype(F32)
        o_ref[...] = total

    return pl.pallas_call(
        body, name=name,
        grid_spec=pltpu.PrefetchScalarGridSpec(
            num_scalar_prefetch=1, grid=(nb,),
            in_specs=[pl.BlockSpec((tr, cols), own_map), pl.BlockSpec((n_p, tr, cols), lambda i, s: (0, i, 0)), ANY],
            out_specs=pl.BlockSpec((tr, cols), out_map)),
        out_shape=jax.ShapeDtypeStruct(shard, F32),
        compiler_params=_params("parallel"),
    )(place, half, slots, after)


def _join_halves(kinds, shards):
    @_plan(len(kinds))
    def plan(refs):
        x, y, c, _ = _place()
        return [(_half_view(kind, g, c), _half_view(kind, g, c), (x, y, 1 - c)) for kind, g in zip(kinds, refs)]

    return _Split("join_halves_" + "".join(map(str, kinds)), tuple(shards), plan, peers="sibling")


N_DEV = 8


def _sum_shared(pack, land, device):
    def body(d_ref, p_ref, l_ref, o_ref):
        me = d_ref[0]
        total = jnp.where(me == 0, p_ref[...], l_ref[0])
        for d in range(1, N_DEV):
            total = total + jnp.where(me == d, p_ref[...], l_ref[d])
        o_ref[...] = total

    return pl.pallas_call(
        body, name="sum_shared",
        grid_spec=pltpu.PrefetchScalarGridSpec(
            num_scalar_prefetch=1, grid=(1,),
            in_specs=[pl.BlockSpec(pack.shape, lambda i, d: (0, 0)), pl.BlockSpec(land.shape, lambda i, d: (0, 0, 0))],
            out_specs=pl.BlockSpec(pack.shape, lambda i, d: (0, 0))),
        out_shape=jax.ShapeDtypeStruct(pack.shape, F32),
    )(device, pack, land)


def _adamw(name, w, g, m, v, after=None):
    rows, cols = w.shape
    tr = min(rows, 256)
    extra = [] if after is None else [after]

    def body(w_ref, g_ref, m_ref, v_ref, *rest):
        d_ref, nm_ref, nv_ref = rest[-3:]
        d_ref[...], nm_ref[...], nv_ref[...] = _adam_step(w_ref[...], g_ref[...], m_ref[...], v_ref[...])

    blk = pl.BlockSpec((tr, cols), lambda i: (i, 0))
    return pl.pallas_call(
        body, name=name, grid=(rows // tr,), in_specs=[blk] * 4 + [ANY] * len(extra), out_specs=[blk] * 3,
        out_shape=[jax.ShapeDtypeStruct(w.shape, F32)] * 3,
        compiler_params=_params("parallel"),
    )(w, g, m, v, *extra)


def _adam_step(w, g, m, v):
    nm = ADAM_B1 * m + (1.0 - ADAM_B1) * g
    nv = ADAM_B2 * v + (1.0 - ADAM_B2) * jnp.square(g)
    m_hat = nm * (1.0 / (1.0 - ADAM_B1 ** ADAM_STEP))
    v_hat = nv * (1.0 / (1.0 - ADAM_B2 ** ADAM_STEP))
    return -ADAM_LR * (m_hat / (jnp.sqrt(v_hat) + ADAM_EPS) + ADAM_WD * w), nm, nv


def _adamw_small(tot, chip, weights, ms, vs, after):
    n, half = len(weights), D_MODEL // 2

    def body(chip_ref, tot_ref, *refs):
        ins, outs = refs[:3 * n], refs[3 * n + 1:]
        tot = tot_ref[...]
        conv_all = jnp.concatenate([tot[5:6, half:], tot[6:7, :half], tot[6:7, half:]], axis=0)
        conv = sum(jnp.where(chip_ref[0] == s, conv_all[:, s * LANES:(s + 1) * LANES], 0.0) for s in range(N_CHIPS))
        grads = [jnp.concatenate([tot[4:5, :half], tot[4:5, half:]], axis=0), tot[5:6, :half], conv,
                 tot[0:1], tot[1:2], tot[2:3], tot[3:4]]
        for k, g in enumerate(grads):
            delta, nm, nv = _adam_step(ins[k][...], g, ins[n + k][...], ins[2 * n + k][...])
            outs[k][...], outs[n + k][...], outs[2 * n + k][...], outs[3 * n + k][...] = g, delta, nm, nv

    whole = lambda a: pl.BlockSpec(a.shape, lambda i, s: (0,) * a.ndim)
    arrays = (*weights, *ms, *vs)
    return pl.pallas_call(
        body, name="adamw_small",
        grid_spec=pltpu.PrefetchScalarGridSpec(
            num_scalar_prefetch=1, grid=(1,), in_specs=[whole(tot)] + [whole(a) for a in arrays] + [ANY],
            out_specs=[whole(a) for a in weights] * 4),
        out_shape=[jax.ShapeDtypeStruct(a.shape, F32) for a in weights] * 4,
    )(chip, tot, *arrays, after)


def kernel(x, w_in, lb_logits, gate_norm_w, conv_w, w_out, ln1_g, ln1_b, w_ff1, w_ff2, ln2_g, ln2_b, loss_target, m_w_in, m_lb_logits, m_gate_norm_w, m_conv_w, m_w_out, m_ln1_g, m_ln1_b, m_w_ff1, m_w_ff2, m_ln2_g, m_ln2_b, v_w_in, v_lb_logits, v_gate_norm_w, v_conv_w, v_w_out, v_ln1_g, v_ln1_b, v_w_ff1, v_w_ff2, v_ln2_g, v_ln2_b):
    xs, tgt = x[0], loss_target[0]
    chip = 2 * lax.axis_index("x") + lax.axis_index("y")
    core = lax.axis_index("c").astype(jnp.int32).reshape(1)
    chip1 = chip.astype(jnp.int32).reshape(1)
    place = jnp.concatenate([chip1, core])

    conv4 = lax.dynamic_update_slice(jnp.zeros((N_CHIPS,) + conv_w.shape[1:], F32), conv_w, (chip, 0, 0))
    ici_in = _gather_w_in_over_ici(_place_shard("place_w_in", w_in[0], chip1, True), conv4)
    rest = (1, 2, 3)
    ici_rest = _gather_over_ici(rest, (_place_shard("place_w_out", w_out[0], chip1, False, after=ici_in.token),
                                       _place_shard("place_w_ff1", w_ff1[0], chip1, True, after=ici_in.token),
                                       _place_shard("place_w_ff2", w_ff2[0], chip1, False, after=ici_in.token)))
    wb_in, cv4 = ici_in.wait(ici_rest.token)
    d2d_in = _gather_over_d2d((0,), (wb_in,))
    wb_in, = d2d_in.wait(d2d_in.token)
    conv_full = cv4.transpose(1, 0, 2).reshape(3, CONV_WIDTH)

    proj, xb, cat_c = _in_proj(xs, wb_in, conv_full, ici_rest.token)
    relay_rest = _relay_over_ici(rest, ici_rest.wait(proj))
    o, states = _hgrn_fwd(proj, lb_logits, relay_rest.token)
    d2d_rest = _gather_over_d2d(rest, relay_rest.wait(o))
    cat_h = _gate_fwd(proj, o, gate_norm_w, d2d_rest.token)
    wb_out, wb_ff1, wb_ff2 = d2d_rest.wait(cat_h)

    (h1b, r, da, dpre2b, dpre1, dpre1b, dcat, g_ln1_g, g_ln1_b, g_ln2_g, g_ln2_b, loss8) = _sublayers(
        cat_h, cat_c, xs, tgt, wb_out, wb_ff1, wb_ff2, ln1_g, ln1_b, ln2_g, ln2_b)

    names = ("w_in", "w_out", "w_ff1", "w_ff2")

    def add_halves(kinds, grads, lands):
        return [_add_half("add_half_" + names[k], g, ld, core, COLS_SHARDED[k]) for k, g, ld in zip(kinds, grads, lands)]

    def sum_pieces(kinds, halves, lands, after):
        return [_sum_pieces("sum_pieces_" + names[k], h, ld, place, COLS_SHARDED[k], after)
                for k, h, ld in zip(kinds, halves, lands)]

    early = (1, 2, 3)
    g_out_local = _dw_out(cat_h, cat_c, dpre1b)
    g_ff2_local, dpc, g_conv = _dw_ff2(r, dpre2b, dcat, proj, conv_full)
    swap_a = _swap_halves((1, 3), (g_out_local, g_ff2_local))
    g_ff1_local, do, dog, g_gnw = _dw_ff1(h1b, da, dcat, o, proj, gate_norm_w, swap_a.token)
    swap_b = _swap_halves((2,), (g_ff1_local,))
    swapped_a = swap_a.wait(swap_b.token)
    halves_a = add_halves((1, 3), swapped_a[:2], swapped_a[2:])
    swapped_b = swap_b.wait(halves_a[1])
    halves = (halves_a[0], *add_halves((2,), swapped_b[:1], swapped_b[1:]), halves_a[1])
    exch = _exchange_pieces(early, halves)
    dph, g_lbl = _hgrn_bwd(proj, do, states, lb_logits, exch.token)
    g_in_local = _dw_in(xb, dph, dog, dpc, dph)

    late = (0,)
    swap = _swap_halves(late, (g_in_local,))
    grad_x = _in_bwd(dph, dog, dpc, wb_in, dpre1, swap.token)
    exchanged = exch.wait(grad_x)
    pack = jnp.concatenate([
        g_ln1_g, g_ln1_b, g_ln2_g, g_ln2_b,
        jnp.concatenate([g_lbl[0:1], g_lbl[1:2]], axis=1),
        jnp.concatenate([g_gnw, g_conv[0:1]], axis=1),
        jnp.concatenate([g_conv[1:2], g_conv[2:3]], axis=1),
        jnp.concatenate([loss8[0:1], jnp.zeros((1, D_MODEL - LANES), F32)], axis=1)], axis=0)
    swapped = swap.wait(exchanged[0])
    exch = _exchange_pieces(late, add_halves(late, swapped[:1], swapped[1:]), pack)
    join = _join_halves(early, sum_pieces(early, exchanged[:3], exchanged[3:], exch.token))
    g_w_out, g_w_ff1, g_w_ff2 = join.wait(join.token)
    d_ff1, nm_ff1, nv_ff1 = _adamw("adamw_w_ff1", w_ff1[0], g_w_ff1, m_w_ff1[0], v_w_ff1[0])
    d_ff2, nm_ff2, nv_ff2 = _adamw("adamw_w_ff2", w_ff2[0], g_w_ff2, m_w_ff2[0], v_w_ff2[0], d_ff1)
    d_out, nm_out, nv_out = _adamw("adamw_w_out", w_out[0], g_w_out, m_w_out[0], v_w_out[0], d_ff2)
    exchanged = exch.wait(d_out)
    tot = _sum_shared(exchanged[2], exchanged[3], 2 * chip1 + core)
    loss = tot[7, 0]
    join = _join_halves(late, sum_pieces(late, exchanged[:1], exchanged[1:2], tot))
    small = ("lb_logits", "gate_norm_w", "conv_w", "ln1_g", "ln1_b", "ln2_g", "ln2_b")
    small_out = _adamw_small(
        tot, chip1, (lb_logits, gate_norm_w, conv_w[0], ln1_g, ln1_b, ln2_g, ln2_b),
        (m_lb_logits, m_gate_norm_w, m_conv_w[0], m_ln1_g, m_ln1_b, m_ln2_g, m_ln2_b),
        (v_lb_logits, v_gate_norm_w, v_conv_w[0], v_ln1_g, v_ln1_b, v_ln2_g, v_ln2_b), join.token)
    g_w_in, = join.wait(small_out[0])
    d_in, nm_in, nv_in = _adamw("adamw_w_in", w_in[0], g_w_in, m_w_in[0], v_w_in[0])

    def results(n_kind, large):
        out = dict(zip(small, small_out[n_kind * len(small):(n_kind + 1) * len(small)]))
        out["conv_w"] = out["conv_w"][None]
        out.update({name: a[None] for name, a in zip(("w_in", "w_out", "w_ff1", "w_ff2"), large)})
        return [out[name] for name in ("w_in", "lb_logits", "gate_norm_w", "conv_w", "w_out", "ln1_g", "ln1_b",
                                       "w_ff1", "w_ff2", "ln2_g", "ln2_b")]

    return (loss, grad_x[None], *results(0, (g_w_in, g_w_out, g_w_ff1, g_w_ff2)),
            *results(1, (d_in, d_out, d_ff1, d_ff2)), *results(2, (nm_in, nm_out, nm_ff1, nm_ff2)),
            *results(3, (nv_in, nv_out, nv_ff1, nv_ff2)))
```

```python
import jax
import jax.numpy as jnp
from jax import lax
from jax.experimental import pallas as pl
from jax.experimental.pallas import tpu as pltpu

F32 = jnp.float32
BF16 = jnp.bfloat16
MXU_DTYPE = jnp.bfloat16

D_MODEL = 1024
HGRN_WIDTH = 512
HEAD_DIM = 128
N_HEADS = 4
CONV_WIDTH = 512
CHUNK = 64
D_FF = 4096
IN_COLS = 3584
GROUP = 512
N_GROUPS = IN_COLS // GROUP
ALPHA = 2.0 ** 0.25
EPS = 1e-5
N_CHIPS = 4
ADAM_LR, ADAM_B1, ADAM_B2, ADAM_EPS, ADAM_WD, ADAM_STEP = 0.001, 0.9, 0.999, 1e-08, 0.01, 10

LANES = 128
SUBLANES = 8
VMEM_LIMIT = 56 * 1024 * 1024
FF_BLOCK = 1024
N_FF = D_FF // FF_BLOCK
GATE_STRIP = 64

NN = (((1,), (0,)), ((), ()))
NT = (((1,), (1,)), ((), ()))
TN = (((0,), (0,)), ((), ()))
MESH = pl.DeviceIdType.MESH
ANY = pl.BlockSpec(memory_space=pl.ANY)


def _dot(a, b, dims):
    return lax.dot_general(a.astype(MXU_DTYPE), b.astype(MXU_DTYPE), dims, preferred_element_type=F32)


def _dot_exact(ones, v):
    ones = ones.astype(jnp.bfloat16)
    hi = v.astype(jnp.bfloat16)
    rest = v - hi.astype(F32)
    mid = rest.astype(jnp.bfloat16)
    low = (rest - mid.astype(F32)).astype(jnp.bfloat16)
    return sum(lax.dot_general(ones, part, NN, preferred_element_type=F32) for part in (hi, mid, low))


def _params(*sem):
    return pltpu.CompilerParams(dimension_semantics=sem, vmem_limit_bytes=VMEM_LIMIT)


def _resident(shape):
    return pl.BlockSpec(shape, lambda *_: (0,) * len(shape), pipeline_mode=pl.Buffered(1))


def _sigmoid(v):
    return 1.0 / (1.0 + jnp.exp(-v))


def _lower_bound(lbl):
    m = jnp.max(lbl, axis=0, keepdims=True)
    e = jnp.exp(lbl - m)
    s = e / jnp.sum(e, axis=0, keepdims=True)
    return s[0:1, :], s[1:2, :]


def _heads(v):
    return [v[:, h * HEAD_DIM:(h + 1) * HEAD_DIM] for h in range(N_HEADS)]


def _per_head(fn, *arrays):
    return jnp.concatenate([fn(*parts) for parts in zip(*map(_heads, arrays))], axis=1)


def _in_proj(x, w_in, conv_w, after):
    t = x.shape[0]
    tm = min(t, 512)

    def body(x_ref, w_ref, cw_ref, after_ref, o_ref, bcu_ref, xb_ref, y_ref, zbuf):
        @pl.when(pl.program_id(0) == 0)
        def _():
            zbuf[tm:tm + SUBLANES, :] = jnp.zeros((SUBLANES, CONV_WIDTH), F32)

        xb = x_ref[...].astype(xb_ref.dtype)
        xb_ref[...] = xb
        group = lambda g: _dot(xb, w_ref[:, g * GROUP:(g + 1) * GROUP], NN)
        for g in range(4):
            o_ref[g] = group(g)
        b_gate, c_gate, u = group(4), group(5), group(6)
        for n, part in enumerate((b_gate, c_gate, u)):
            bcu_ref[n] = part.astype(bcu_ref.dtype)
        zbuf[0:SUBLANES, :] = zbuf[tm:tm + SUBLANES, :]
        zbuf[SUBLANES:SUBLANES + tm, :] = c_gate * u
        cw = cw_ref[...]
        at = lambda shift: zbuf[shift:shift + tm, :]
        conv = cw[2:3, :] * at(SUBLANES) + cw[1:2, :] * at(SUBLANES - 1) + cw[0:1, :] * at(SUBLANES - 2)
        y_ref[...] = (b_gate * conv).astype(y_ref.dtype)

    return pl.pallas_call(
        body, name="in_proj", grid=(t // tm,),
        in_specs=[pl.BlockSpec((tm, D_MODEL), lambda i: (i, 0)), _resident((D_MODEL, IN_COLS)),
                  pl.BlockSpec((3, CONV_WIDTH), lambda i: (0, 0)), ANY],
        out_specs=[pl.BlockSpec((4, tm, GROUP), lambda i: (0, i, 0)), pl.BlockSpec((3, tm, GROUP), lambda i: (0, i, 0)),
                   pl.BlockSpec((tm, D_MODEL), lambda i: (i, 0)), pl.BlockSpec((tm, CONV_WIDTH), lambda i: (i, 0))],
        out_shape=[jax.ShapeDtypeStruct((4, t, GROUP), F32), jax.ShapeDtypeStruct((3, t, GROUP), BF16),
                   jax.ShapeDtypeStruct((t, D_MODEL), BF16), jax.ShapeDtypeStruct((t, CONV_WIDTH), BF16)],
        scratch_shapes=[pltpu.VMEM((tm + SUBLANES, CONV_WIDTH), F32)],
        compiler_params=_params("arbitrary"),
    )(x, w_in, conv_w, after)


def _gates(fp, lb):
    sig = _sigmoid(fp)
    f = lb + (1.0 - lb) * sig
    return sig, f, jnp.log(f), 1.0 - f


def _chunk_masks():
    row = lax.broadcasted_iota(jnp.int32, (CHUNK, CHUNK), 0)
    col = lax.broadcasted_iota(jnp.int32, (CHUNK, CHUNK), 1)
    return row >= col, row <= col


def _hgrn_fwd(proj, lb_logits, after):
    t = proj.shape[1]
    tb = min(t, 512)
    ncb = tb // CHUNK

    def body(q_ref, f_ref, v_ref, lbl_ref, after_ref, o_ref, st_ref, s_scr):
        @pl.when(pl.program_id(0) == 0)
        def _():
            s_scr[...] = jnp.zeros_like(s_scr)

        lb, _ = _lower_bound(lbl_ref[...])
        causal, _ = _chunk_masks()

        every = range(ncb)
        rows = [slice(c * CHUNK, (c + 1) * CHUNK) for c in every]
        q, v = [q_ref[r, :] for r in rows], [v_ref[r, :] for r in rows]
        gates = [_gates(f_ref[r, :], lb) for r in rows]
        k = [gt[3] for gt in gates]
        b = [_dot_exact(causal, gt[2]) for gt in gates]
        mid, last = [x[CHUNK // 2:CHUNK // 2 + 1, :] for x in b], [x[CHUNK - 1:CHUNK, :] for x in b]
        qt = [q[c] * jnp.exp(b[c] - mid[c]) for c in every]
        kt = [k[c] * jnp.exp(mid[c] - b[c]) for c in every]
        qi = [q[c] * jnp.exp(b[c]) for c in every]
        ks = [k[c] * jnp.exp(last[c] - b[c]) for c in every]
        dec = [jnp.exp(x) for x in last]
        scores = [[jnp.where(causal, _dot(a, b_, NT), 0.0) for a, b_ in zip(_heads(qt[c]), _heads(kt[c]))] for c in every]
        intra = [[_dot(s, v_h, NN) for s, v_h in zip(scores[c], _heads(v[c]))] for c in every]
        update = [_per_head(lambda v_h, ks_h: _dot(v_h, ks_h, TN), v[c], ks[c]) for c in every]

        st = s_scr[...]
        states = []
        for c in every:
            states.append(st)
            st_ref[c] = st
            st = dec[c] * st + update[c]
        s_scr[...] = st

        o_ref[...] = jnp.concatenate(
            [jnp.concatenate([i_h + _dot(qi_h, st_h, NT) for i_h, qi_h, st_h in
                              zip(intra[c], _heads(qi[c]), _heads(states[c]))], axis=1) for c in every], axis=0)

    grp = lambda g: pl.BlockSpec((None, tb, GROUP), lambda i: (g, i, 0))
    return pl.pallas_call(
        body, name="hgrn_fwd", grid=(t // tb,),
        in_specs=[grp(0), grp(1), grp(2), pl.BlockSpec((2, HGRN_WIDTH), lambda i: (0, 0)), ANY],
        out_specs=[pl.BlockSpec((tb, HGRN_WIDTH), lambda i: (i, 0)),
                   pl.BlockSpec((ncb, HEAD_DIM, HGRN_WIDTH), lambda i: (i, 0, 0))],
        out_shape=[jax.ShapeDtypeStruct((t, HGRN_WIDTH), F32),
                   jax.ShapeDtypeStruct((t // CHUNK, HEAD_DIM, HGRN_WIDTH), F32)],
        scratch_shapes=[pltpu.VMEM((HEAD_DIM, HGRN_WIDTH), F32)],
        compiler_params=_params("arbitrary"),
    )(proj, proj, proj, lb_logits, after)


def _gate_fwd(proj, o, gate_norm_w, after):
    t = proj.shape[1]
    tb = min(t, 512)

    def body(o_ref, og_ref, gnw_ref, after_ref, out_ref):
        gnw = gnw_ref[...]
        for s in range(tb // GATE_STRIP):
            rows = slice(s * GATE_STRIP, (s + 1) * GATE_STRIP)
            og = og_ref[rows, :]
            on = _per_head(lambda o_h: o_h * lax.rsqrt(jnp.mean(o_h * o_h, axis=-1, keepdims=True) + EPS), o_ref[rows, :])
            out_ref[rows, :] = (on * gnw * (og * _sigmoid(og))).astype(out_ref.dtype)

    tile = pl.BlockSpec((tb, GROUP), lambda i: (i, 0))
    return pl.pallas_call(
        body, name="gate_fwd", grid=(t // tb,),
        in_specs=[tile, pl.BlockSpec((None, tb, GROUP), lambda i: (3, i, 0)), pl.BlockSpec((1, GROUP), lambda i: (0, 0)), ANY],
        out_specs=tile,
        out_shape=jax.ShapeDtypeStruct((t, HGRN_WIDTH), BF16),
        compiler_params=_params("parallel"),
    )(o, proj, gate_norm_w, after)


def _ln_bwd(dy, xhat, rstd, g):
    dxhat = dy * g
    m1 = jnp.mean(dxhat, axis=-1, keepdims=True)
    m2 = jnp.mean(dxhat * xhat, axis=-1, keepdims=True)
    return rstd * (dxhat - m1 - xhat * m2)


def _layer_norm(pre):
    xc = pre - jnp.mean(pre, axis=-1, keepdims=True)
    rstd = lax.rsqrt(jnp.mean(xc * xc, axis=-1, keepdims=True) + EPS)
    return xc * rstd, rstd


def _sublayers(cat_h, cat_c, x, target, w_out, w_ff1, w_ff2, g1, b1, g2, b2):
    t = x.shape[0]
    tm = min(t, 256)

    def body(ch_ref, cc_ref, x_ref, tg_ref, wo_ref, w1_ref, w2_ref, g1_ref, b1_ref, g2_ref, b2_ref,
             h1_ref, r_ref, da_ref, dp2b_ref, dp1_ref, dp1b_ref, dcat_ref, dg1_ref, db1_ref, dg2_ref, db2_ref, loss_ref):
        @pl.when(pl.program_id(0) == 0)
        def _():
            for ref in (dg1_ref, db1_ref, dg2_ref, db2_ref, loss_ref):
                ref[...] = jnp.zeros_like(ref)

        mix = _dot(ch_ref[...], wo_ref[0:GROUP, :], NN) + _dot(cc_ref[...], wo_ref[GROUP:2 * GROUP, :], NN)
        xhat1, rstd1 = _layer_norm(ALPHA * x_ref[...] + mix)
        h1 = xhat1 * g1_ref[...] + b1_ref[...]
        h1b = h1.astype(h1_ref.dtype)
        h1_ref[...] = h1b
        mlp = jnp.zeros((tm, D_MODEL), F32)
        for j in range(N_FF):
            cols = slice(j * FF_BLOCK, (j + 1) * FF_BLOCK)
            r = jnp.square(jnp.maximum(_dot(h1b, w1_ref[:, cols], NN), 0.0)).astype(r_ref.dtype)
            r_ref[:, cols] = r
            mlp = mlp + _dot(r, w2_ref[cols, :], NN)
        xhat2, rstd2 = _layer_norm(ALPHA * h1 + mlp)
        err = xhat2 * g2_ref[...] + b2_ref[...] - tg_ref[...]
        loss_ref[...] += 0.5 * jnp.sum(jnp.mean(err * err, axis=-1, keepdims=True))
        dy = err * (1.0 / D_MODEL)
        dg2_ref[...] += jnp.sum(dy * xhat2, axis=0, keepdims=True)
        db2_ref[...] += jnp.sum(dy, axis=0, keepdims=True)
        dp2 = _ln_bwd(dy, xhat2, rstd2, g2_ref[...])
        dp2b = dp2.astype(dp2b_ref.dtype)
        dp2b_ref[...] = dp2b
        back = jnp.zeros((tm, D_MODEL), F32)
        for j in range(N_FF):
            cols = slice(j * FF_BLOCK, (j + 1) * FF_BLOCK)
            dr = _dot(dp2b, w2_ref[cols, :], NT)
            da = (dr * (2.0 * jnp.sqrt(r_ref[:, cols].astype(F32)))).astype(da_ref.dtype)
            da_ref[:, cols] = da
            back = back + _dot(da, w1_ref[:, cols], NT)
        dh1 = ALPHA * dp2 + back
        dg1_ref[...] += jnp.sum(dh1 * xhat1, axis=0, keepdims=True)
        db1_ref[...] += jnp.sum(dh1, axis=0, keepdims=True)
        dp1 = _ln_bwd(dh1, xhat1, rstd1, g1_ref[...])
        dp1b = dp1.astype(dp1b_ref.dtype)
        dp1_ref[...] = dp1
        dp1b_ref[...] = dp1b
        dcat_ref[...] = _dot(dp1b, wo_ref[...], NT)

    row = pl.BlockSpec((tm, D_MODEL), lambda i: (i, 0))
    wide = pl.BlockSpec((tm, D_FF), lambda i: (i, 0))
    vec = pl.BlockSpec((1, D_MODEL), lambda i: (0, 0))
    narrow = lambda dtype: jax.ShapeDtypeStruct((t, D_MODEL), dtype)
    return pl.pallas_call(
        body, name="sublayers", grid=(t // tm,),
        in_specs=[pl.BlockSpec((tm, GROUP), lambda i: (i, 0)), pl.BlockSpec((tm, GROUP), lambda i: (i, 0)), row, row,
                  _resident((D_MODEL, D_MODEL)),
                  _resident((D_MODEL, D_FF)), _resident((D_FF, D_MODEL)), vec, vec, vec, vec],
        out_specs=[row, wide, wide, row, row, row, row, vec, vec, vec, vec,
                   pl.BlockSpec((SUBLANES, LANES), lambda i: (0, 0))],
        out_shape=[narrow(BF16), jax.ShapeDtypeStruct((t, D_FF), BF16), jax.ShapeDtypeStruct((t, D_FF), BF16),
                   narrow(BF16), narrow(F32), narrow(BF16), narrow(F32)]
                  + [jax.ShapeDtypeStruct((1, D_MODEL), F32)] * 4 + [jax.ShapeDtypeStruct((SUBLANES, LANES), F32)],
        compiler_params=_params("arbitrary"),
    )(cat_h, cat_c, x, target, w_out, w_ff1, w_ff2, g1, b1, g2, b2)


def _hgrn_bwd(proj, do, states, lb_logits, after):
    t = proj.shape[1]
    tb = min(t, 512)
    ncb = tb // CHUNK
    nblk = t // tb

    def body(q_ref, f_ref, v_ref, do_ref, st_ref, lbl_ref, after_ref, dp_ref, dlbl_ref, ds_scr, dlb_scr):
        i = pl.program_id(0)

        @pl.when(i == 0)
        def _():
            ds_scr[...] = jnp.zeros_like(ds_scr)
            dlb_scr[...] = jnp.zeros_like(dlb_scr)

        lb, s1 = _lower_bound(lbl_ref[...])
        causal, anti = _chunk_masks()
        every = range(ncb)
        rows = [slice(c * CHUNK, (c + 1) * CHUNK) for c in every]
        q, v, do = ([ref[r, :] for r in rows] for ref in (q_ref, v_ref, do_ref))
        st = [st_ref[c] for c in every]
        gates = [_gates(f_ref[r, :], lb) for r in rows]
        sig, f, k = ([gt[n] for gt in gates] for n in (0, 1, 3))
        b = [_dot_exact(causal, gt[2]) for gt in gates]
        mid, last = [x[CHUNK // 2:CHUNK // 2 + 1, :] for x in b], [x[CHUNK - 1:CHUNK, :] for x in b]
        e_q = [jnp.exp(b[c] - mid[c]) for c in every]
        e_k = [jnp.exp(mid[c] - b[c]) for c in every]
        e_i = [jnp.exp(x) for x in b]
        e_s = [jnp.exp(last[c] - b[c]) for c in every]
        dec = [jnp.exp(x) for x in last]
        qt, kt, qi, ks = ([a[c] * e[c] for c in every] for a, e in ((q, e_q), (k, e_k), (q, e_i), (k, e_s)))

        def masked(a, b_):
            return [[jnp.where(causal, _dot(a_h, b_h, NT), 0.0) for a_h, b_h in zip(_heads(a[c]), _heads(b_[c]))]
                    for c in every]

        def with_scores(s, other, dims):
            return [jnp.concatenate([_dot(s_h, o_h, dims) for s_h, o_h in zip(s[c], _heads(other[c]))], axis=1)
                    for c in every]

        def per_head(dims, a, b_):
            return [_per_head(lambda a_h, b_h: _dot(a_h, b_h, dims), a[c], b_[c]) for c in every]

        scores, dscores = masked(qt, kt), masked(do, v)
        dqt, dkt, dv_intra = with_scores(dscores, kt, NN), with_scores(dscores, qt, TN), with_scores(scores, do, TN)
        dqi, update = per_head(NN, do, st), per_head(TN, do, qi)

        dst = ds_scr[...]
        dsts = [None] * ncb
        for c in reversed(every):
            dsts[c] = dst
            dst = dec[c] * dst + update[c]
        ds_scr[...] = dst

        dv_state, dks = per_head(NT, ks, dsts), per_head(NN, v, dsts)
        ddec = [jnp.sum(dsts[c] * st[c], axis=0, keepdims=True) for c in every]
        dq = [dqt[c] * e_q[c] + dqi[c] * e_i[c] for c in every]
        dk = [dkt[c] * e_k[c] + dks[c] * e_s[c] for c in every]
        db = [q[c] * dq[c] - k[c] * dk[c] for c in every]
        db_last = [jnp.sum(dks[c] * ks[c], axis=0, keepdims=True) + ddec[c] * dec[c] for c in every]
        dg = [_dot_exact(anti, db[c]) + db_last[c] for c in every]
        df = [dg[c] / f[c] - dk[c] for c in every]
        dlb_scr[...] += sum(jnp.sum(df[c] * (1.0 - sig[c]), axis=0, keepdims=True) for c in every)
        dfp = [df[c] * (1.0 - lb) * sig[c] * (1.0 - sig[c]) for c in every]
        dv = [dv_intra[c] + dv_state[c] for c in every]
        for n, parts in enumerate((dq, dfp, dv)):
            dp_ref[n] = jnp.concatenate(parts, axis=0).astype(dp_ref.dtype)

        @pl.when(i == nblk - 1)
        def _():
            dlb = dlb_scr[...]
            dlbl_ref[0:1, :] = dlb * lb * (1.0 - lb)
            dlbl_ref[1:2, :] = -dlb * lb * s1

    grp = lambda g: pl.BlockSpec((None, tb, GROUP), lambda i: (g, nblk - 1 - i, 0))
    vec = pl.BlockSpec((2, HGRN_WIDTH), lambda i: (0, 0))
    return pl.pallas_call(
        body, name="hgrn_bwd", grid=(nblk,),
        in_specs=[grp(0), grp(1), grp(2), pl.BlockSpec((tb, HGRN_WIDTH), lambda i: (nblk - 1 - i, 0)),
                  pl.BlockSpec((ncb, HEAD_DIM, HGRN_WIDTH), lambda i: (nblk - 1 - i, 0, 0)), vec, ANY],
        out_specs=[pl.BlockSpec((3, tb, HGRN_WIDTH), lambda i: (0, nblk - 1 - i, 0)), vec],
        out_shape=[jax.ShapeDtypeStruct((3, t, HGRN_WIDTH), BF16), jax.ShapeDtypeStruct((2, HGRN_WIDTH), F32)],
        scratch_shapes=[pltpu.VMEM((HEAD_DIM, HGRN_WIDTH), F32), pltpu.VMEM((1, HGRN_WIDTH), F32)],
        compiler_params=_params("arbitrary"),
    )(proj, proj, proj, do, states, lb_logits, after)


def _in_bwd(dph, dog, dpc, w_in, dpre1, after):
    t = dpre1.shape[0]
    tm = min(t, 512)

    def body(dh_ref, dog_ref, dc_ref, w_ref, dp_ref, after_ref, o_ref):
        acc = ALPHA * dp_ref[...]
        for g in range(N_GROUPS):
            part = dh_ref[g] if g < 3 else dog_ref[...] if g == 3 else dc_ref[g - 4]
            acc = acc + _dot(part, w_ref[:, g * GROUP:(g + 1) * GROUP], NT)
        o_ref[...] = acc

    row = pl.BlockSpec((tm, D_MODEL), lambda i: (i, 0))
    three = pl.BlockSpec((3, tm, GROUP), lambda i: (0, i, 0))
    return pl.pallas_call(
        body, name="in_bwd", grid=(t // tm,),
        in_specs=[three, pl.BlockSpec((tm, GROUP), lambda i: (i, 0)), three, _resident((D_MODEL, IN_COLS)), row, ANY],
        out_specs=row,
        out_shape=jax.ShapeDtypeStruct((t, D_MODEL), F32),
        compiler_params=_params("parallel"),
    )(dph, dog, dpc, w_in, dpre1, after)


GRAD_TILE = 512
OUT_PARTS = 4


class _Side:
    def __init__(self, operands, in_specs, out_shape, out_specs, scratch, init, begin):
        self.operands, self.in_specs, self.out_shape, self.out_specs = operands, in_specs, out_shape, out_specs
        self.scratch, self.init, self.begin = scratch, init, begin


def _grad_w(name, operands, widths, shape, step, after=None, side=None):
    t = operands[0].shape[-2]
    tt = min(t, GRAD_TILE)
    n_in, n_steps = len(operands), t // tt
    in_specs = [pl.BlockSpec((tt, w), lambda k: (k, 0)) if a.ndim == 2 else
                pl.BlockSpec((a.shape[0], tt, w), lambda k: (0, k, 0)) for a, w in zip(operands, widths)]
    extra = [] if after is None else [after]
    s_in, s_out = (len(side.operands), len(side.out_shape)) if side else (0, 0)
    first_out = n_in + s_in + len(extra)

    def body(*refs):
        o_ref, side_outs = refs[first_out], refs[first_out + 1:first_out + 1 + s_out]
        acc, narrow, sem = refs[first_out + 1 + s_out:first_out + 4 + s_out]
        k = pl.program_id(0)

        @pl.when(k == 0)
        def _():
            acc[...] = jnp.zeros_like(acc)
            if side:
                side.init(side_outs)

        tick = side.begin(k, n_steps, refs[n_in:n_in + s_in], side_outs, refs[first_out + 4 + s_out:]) if side else None
        step(acc, *refs[:n_in], tick or (lambda j: None))

        @pl.when(k == n_steps - 1)
        def _():
            part = shape[0] // OUT_PARTS
            copies = []
            for p in range(OUT_PARTS):
                rows = pl.ds(p * part, part)
                narrow[rows, :] = acc[rows, :].astype(narrow.dtype)
                copies.append(pltpu.make_async_copy(narrow.at[rows, :], o_ref.at[rows, :], sem.at[p]))
                copies[-1].start()
            for cp in copies:
                cp.wait()

    outs = pl.pallas_call(
        body, name=name, grid=(n_steps,),
        in_specs=in_specs + (side.in_specs if side else []) + [ANY] * len(extra),
        out_specs=[ANY] + (side.out_specs if side else []),
        out_shape=[jax.ShapeDtypeStruct(shape, BF16)] + (side.out_shape if side else []),
        scratch_shapes=[pltpu.VMEM(shape, F32), pltpu.VMEM(shape, BF16), pltpu.SemaphoreType.DMA((OUT_PARTS,))]
                       + (side.scratch if side else []),
        compiler_params=_params("arbitrary"),
    )(*operands, *(side.operands if side else ()), *extra)
    return outs if side else outs[0]


def _dw_in(xb, dph, dog, dpc, after):
    def step(acc, x_ref, dh_ref, dog_ref, dc_ref, tick):
        xv = x_ref[...]
        for g in range(N_GROUPS):
            part = dh_ref[g] if g < 3 else dog_ref[...] if g == 3 else dc_ref[g - 4]
            acc[:, g * GROUP:(g + 1) * GROUP] += _dot(xv, part, TN)

    return _grad_w("dw_in", (xb, dph, dog, dpc), (D_MODEL, GROUP, GROUP, GROUP), (D_MODEL, IN_COLS), step, after)


def _dw_out(cat_h, cat_c, dpre1b):
    def step(acc, h_ref, c_ref, d_ref, tick):
        dv = d_ref[...]
        acc[0:GROUP, :] += _dot(h_ref[...], dv, TN)
        acc[GROUP:2 * GROUP, :] += _dot(c_ref[...], dv, TN)

    return _grad_w("dw_out", (cat_h, cat_c, dpre1b), (GROUP, GROUP, D_MODEL), (D_MODEL, D_MODEL), step)


def _strips_of(j, tt):
    per_tick = tt // GATE_STRIP // N_FF
    return [slice(s * GATE_STRIP, (s + 1) * GATE_STRIP) for s in range(j * per_tick, (j + 1) * per_tick)]


def _dw_ff1(h1b, da, dcat, o, proj, gate_norm_w, after):
    t = h1b.shape[0]
    tt = min(t, GRAD_TILE)

    def step(acc, h_ref, da_ref, tick):
        hv = h_ref[...]
        for j in range(N_FF):
            cols = slice(j * FF_BLOCK, (j + 1) * FF_BLOCK)
            acc[:, cols] += _dot(hv, da_ref[:, cols], TN)
            tick(j)

    def init(outs):
        outs[2][...] = jnp.zeros_like(outs[2])

    def begin(k, n_steps, ins, outs, scratch):
        do2_ref, o_ref, og_ref, gnw_ref = ins
        do_ref, dog_ref, dgnw_ref = outs
        total = [jnp.zeros((GATE_STRIP, GROUP), F32)]

        def tick(j):
            gnw = gnw_ref[...]
            for rows in _strips_of(j, tt):
                ov, og, do2 = o_ref[rows, :], og_ref[rows, :], do2_ref[rows, :]
                rs = _per_head(lambda o_h: jnp.broadcast_to(
                    lax.rsqrt(jnp.mean(o_h * o_h, axis=-1, keepdims=True) + EPS), o_h.shape), ov)
                on = ov * rs
                sg = _sigmoid(og)
                sil = og * sg
                don = do2 * gnw * sil
                total[0] = total[0] + do2 * on * sil
                dog_ref[rows, :] = (do2 * on * gnw * (sg * (1.0 + og * (1.0 - sg)))).astype(dog_ref.dtype)
                do_ref[rows, :] = rs * (don - on * _per_head(
                    lambda p_h: jnp.broadcast_to(jnp.mean(p_h, axis=-1, keepdims=True), p_h.shape), don * on))
            if j == N_FF - 1:
                dgnw_ref[...] += jnp.sum(total[0], axis=0, keepdims=True)

        return tick

    tile = pl.BlockSpec((tt, GROUP), lambda k: (k, 0))
    vec = pl.BlockSpec((1, GROUP), lambda k: (0, 0))
    side = _Side(
        (dcat, o, proj, gate_norm_w), [tile, tile, pl.BlockSpec((None, tt, GROUP), lambda k: (3, k, 0)), vec],
        [jax.ShapeDtypeStruct((t, HGRN_WIDTH), F32), jax.ShapeDtypeStruct((t, HGRN_WIDTH), BF16),
         jax.ShapeDtypeStruct((1, HGRN_WIDTH), F32)], [tile, tile, vec], [], init, begin)
    return _grad_w("dw_ff1", (h1b, da), (D_MODEL, D_FF), (D_MODEL, D_FF), step, after, side)


def _dw_ff2(r, dpre2b, dcat, bcu, conv_w):
    t = r.shape[0]
    tt = min(t, GRAD_TILE)
    hb = tt // SUBLANES
    halo = 2 * SUBLANES

    def step(acc, r_ref, d_ref, tick):
        dv = d_ref[...]
        for j in range(N_FF):
            rows = slice(j * FF_BLOCK, (j + 1) * FF_BLOCK)
            acc[rows, :] += _dot(r_ref[:, rows], dv, TN)
            tick(j)

    def init(outs):
        outs[1][...] = jnp.zeros_like(outs[1])

    def begin(k, n_steps, ins, outs, scratch):
        dy_ref, dyn_ref, b_ref, bn_ref, c_ref, u_ref, ch_ref, uh_ref, cw_ref = ins
        dp_ref, dcw_ref = outs
        zbuf, dbuf = scratch
        before = lambda ref: ref[SUBLANES:halo, :].astype(F32)
        zbuf[0:SUBLANES, :] = jnp.where(k > 0, before(ch_ref) * before(uh_ref), 0.0)
        zbuf[SUBLANES:SUBLANES + tt, :] = c_ref[...].astype(F32) * u_ref[...].astype(F32)
        dbuf[0:tt, :] = dy_ref[...] * b_ref[...].astype(F32)
        dbuf[tt:tt + SUBLANES, :] = jnp.where(k < n_steps - 1, dyn_ref[...] * bn_ref[0:SUBLANES, :].astype(F32), 0.0)
        totals = [jnp.zeros((GATE_STRIP, GROUP), F32) for _ in range(3)]

        def tick(j):
            cw = cw_ref[...]
            for rows in _strips_of(j, tt):
                at = lambda buf, shift: buf[shift + rows.start:shift + rows.stop, :]
                z, z1, z2 = at(zbuf, SUBLANES), at(zbuf, SUBLANES - 1), at(zbuf, SUBLANES - 2)
                dyc, d1, d2 = at(dbuf, 0), at(dbuf, 1), at(dbuf, 2)
                yc = cw[2:3, :] * z + cw[1:2, :] * z1 + cw[0:1, :] * z2
                dz = cw[2:3, :] * dyc + cw[1:2, :] * d1 + cw[0:1, :] * d2
                dp_ref[0, rows, :] = (dy_ref[rows, :] * yc).astype(dp_ref.dtype)
                dp_ref[1, rows, :] = (dz * u_ref[rows, :].astype(F32)).astype(dp_ref.dtype)
                dp_ref[2, rows, :] = (dz * c_ref[rows, :].astype(F32)).astype(dp_ref.dtype)
                for n, tap in enumerate((z2, z1, z)):
                    totals[n] = totals[n] + dyc * tap
            if j == N_FF - 1:
                for n in range(3):
                    dcw_ref[n:n + 1, :] += jnp.sum(totals[n], axis=0, keepdims=True)

        return tick

    grp = lambda g: pl.BlockSpec((None, tt, GROUP), lambda k: (g, k, 0))
    prev = lambda g: pl.BlockSpec((None, halo, GROUP), lambda k: (g, jnp.maximum(k * (tt // halo) - 1, 0), 0))
    nxt = lambda g: pl.BlockSpec((None, halo, GROUP), lambda k: (g, jnp.minimum((k + 1) * (tt // halo), t // halo - 1), 0))
    nxt_row = lambda k: jnp.minimum((k + 1) * hb, t // SUBLANES - 1)
    whole = pl.BlockSpec((3, CONV_WIDTH), lambda k: (0, 0))
    side = _Side(
        (dcat, dcat, bcu, bcu, bcu, bcu, bcu, bcu, conv_w),
        [pl.BlockSpec((tt, GROUP), lambda k: (k, 1)), pl.BlockSpec((SUBLANES, GROUP), lambda k: (nxt_row(k), 1)),
         grp(0), nxt(0), grp(1), grp(2), prev(1), prev(2), whole],
        [jax.ShapeDtypeStruct((3, t, CONV_WIDTH), BF16), jax.ShapeDtypeStruct((3, CONV_WIDTH), F32)],
        [pl.BlockSpec((3, tt, GROUP), lambda k: (0, k, 0)), whole],
        [pltpu.VMEM((tt + SUBLANES, GROUP), F32), pltpu.VMEM((tt + SUBLANES, GROUP), F32)], init, begin)
    return _grad_w("dw_ff2", (r, dpre2b), (D_FF, D_MODEL), (D_FF, D_MODEL), step, side=side)


def _place():
    x, y, c = lax.axis_index("x"), lax.axis_index("y"), lax.axis_index("c")
    return x, y, c, 2 * x + y


def _other_chips(x, y):
    return [(1 - x, y), (x, 1 - y), (1 - x, 1 - y)]


def _place_shard(name, w, chip, cols_sharded, after=None):
    rows, cols = w.shape
    tr = min(rows, 256)
    nb = rows // tr
    full = (rows, cols * N_CHIPS) if cols_sharded else (rows * N_CHIPS, cols)
    out_map = (lambda i, s: (i, s[0])) if cols_sharded else (lambda i, s: (s[0] * nb + i, 0))

    def body(s_ref, w_ref, *rest):
        rest[-1][...] = w_ref[...].astype(rest[-1].dtype)

    extra = [] if after is None else [after]
    return pl.pallas_call(
        body, name=name,
        grid_spec=pltpu.PrefetchScalarGridSpec(
            num_scalar_prefetch=1, grid=(nb,),
            in_specs=[pl.BlockSpec((tr, cols), lambda i, s: (i, 0))] + [ANY] * len(extra),
            out_specs=pl.BlockSpec((tr, cols), out_map)),
        out_shape=jax.ShapeDtypeStruct(full, BF16),
        compiler_params=_params("parallel"),
    )(chip, w, *extra)


HBM = pl.BlockSpec(memory_space=pltpu.HBM)
SEM = pl.BlockSpec(memory_space=pltpu.SEMAPHORE)
EFFECT = pltpu.SideEffectType.DATAFLOW_SIDE_EFFECTING


PEER_SETS = {
    "sibling": (0, lambda x, y, c: [(x, y, 1 - c)]),
    "chips": (1, lambda x, y, c: [(1 - x, y, c), (x, 1 - y, c), (1 - x, 1 - y, c)]),
    "neighbours": (2, lambda x, y, c: [(1 - x, y, c), (x, 1 - y, c)]),
}


class _Split:
    def __init__(self, name, arrays, plan, others=(), peers=None):
        n_own, arrays = len(arrays), (*arrays, *others)
        n, n_copies = len(arrays), plan.count
        self.name, self.plan, self.n = name, plan, n_own
        barrier_id, peer_ids = PEER_SETS[peers] if peers else (None, None)

        def body(*refs):
            if peers:
                x, y, c, _ = _place()
                barrier = pltpu.get_barrier_semaphore()
                for peer in peer_ids(x, y, c):
                    pl.semaphore_signal(barrier, inc=1, device_id=peer, device_id_type=MESH)
                pl.semaphore_wait(barrier, len(peer_ids(0, 0, 0)))
            send_sems, recv_sems, token = refs[n], refs[n + 1], refs[-1]
            for k, (src, dst, to) in enumerate(plan(refs[:n])):
                pltpu.make_async_remote_copy(src_ref=src, dst_ref=dst, send_sem=send_sems.at[k], recv_sem=recv_sems.at[k],
                                             device_id=to, device_id_type=MESH).start()
            token[...] = jnp.zeros_like(token)

        outs = pl.pallas_call(
            body, name=name + "_start",
            out_shape=(pltpu.SemaphoreType.DMA((n_copies,)), pltpu.SemaphoreType.DMA((n_copies,)),
                       *[pltpu.HBM(a.shape, a.dtype) for a in arrays], jax.ShapeDtypeStruct((SUBLANES, LANES), F32)),
            in_specs=(HBM,) * n, out_specs=(SEM, SEM) + (HBM,) * n + (pl.BlockSpec(memory_space=pltpu.VMEM),),
            input_output_aliases={i: 2 + i for i in range(n)},
            compiler_params=pltpu.CompilerParams(has_side_effects=EFFECT, collective_id=barrier_id),
        )(*[pltpu.with_memory_space_constraint(a, pltpu.HBM) for a in arrays])
        self.sems, self.arrays, self.others, self.token = outs[:2], outs[2:2 + n_own], outs[2 + n_own:2 + n], outs[-1]

    def wait(self, after):
        n, plan = self.n, self.plan

        def body(*refs):
            send_sems, recv_sems = refs[n], refs[n + 1]
            for k, (src, dst, to) in enumerate(plan(refs[:n])):
                cp = pltpu.make_async_remote_copy(src_ref=src, dst_ref=dst, send_sem=send_sems.at[k],
                                                  recv_sem=recv_sems.at[k], device_id=to, device_id_type=MESH)
                cp.wait_send()
                cp.wait_recv()

        return pl.pallas_call(
            body, name=self.name + "_wait", out_shape=tuple(pltpu.HBM(a.shape, a.dtype) for a in self.arrays),
            in_specs=(HBM,) * n + (SEM, SEM, ANY), out_specs=(HBM,) * n, input_output_aliases={i: i for i in range(n)},
            compiler_params=pltpu.CompilerParams(has_side_effects=EFFECT),
        )(*self.arrays, *self.sems, after)


COLS_SHARDED = (True, False, True, False)
HALF_SHAPES = [(D_MODEL // 2, IN_COLS), (D_MODEL, D_MODEL // 2), (D_MODEL // 2, D_FF), (D_FF, D_MODEL // 2)]
PIECE_SHAPES = [(D_MODEL // 2, IN_COLS // N_CHIPS), (D_MODEL // N_CHIPS, D_MODEL // 2),
                (D_MODEL // 2, D_FF // N_CHIPS), (D_FF // N_CHIPS, D_MODEL // 2)]


def _shard_view(kind, ref, chip):
    if COLS_SHARDED[kind]:
        n = ref.shape[1] // N_CHIPS
        return ref.at[:, pl.ds(chip * n, n)]
    n = ref.shape[0] // N_CHIPS
    return ref.at[pl.ds(chip * n, n), :]


def _half_view(kind, ref, h):
    if COLS_SHARDED[kind]:
        n = ref.shape[0] // 2
        return ref.at[pl.ds(h * n, n), :]
    n = ref.shape[1] // 2
    return ref.at[:, pl.ds(h * n, n)]


def _plan(count):
    def mark(fn):
        fn.count = count
        return fn
    return mark


def _shard_rows_view(kind, ref, chip, part, n_parts):
    if COLS_SHARDED[kind]:
        m, n = ref.shape[0] // n_parts, ref.shape[1] // N_CHIPS
        return ref.at[pl.ds(part * m, m), pl.ds(chip * n, n)]
    m = ref.shape[0] // N_CHIPS // n_parts
    return ref.at[pl.ds((n_parts * chip + part) * m, m), :]


def _shard_half_view(kind, ref, chip, h):
    return _shard_rows_view(kind, ref, chip, h, 2)


def _gather_over_ici(kinds, weights):
    @_plan(2 * len(kinds))
    def plan(refs):
        x, y, c, me = _place()
        mine = [_shard_half_view(kind, ref, me, c) for kind, ref in zip(kinds, refs)]
        return [(v, v, to) for v in mine for to in ((1 - x, y, c), (x, 1 - y, c))]

    return _Split("gather_ici_" + "".join(map(str, kinds)), tuple(weights), plan, peers="neighbours")


def _relay_over_ici(kinds, weights, others=()):
    @_plan(2 * len(kinds))
    def plan(refs):
        x, y, c, _ = _place()
        x_nbr, y_nbr = 2 * (1 - x) + y, 2 * x + (1 - y)
        out = []
        for kind, ref in zip(kinds, refs):
            first, second = (_shard_rows_view(kind, ref, chip, 2 * c + q, 4) for q, chip in ((0, x_nbr), (1, y_nbr)))
            out += [(first, first, (x, 1 - y, c)), (second, second, (1 - x, y, c))]
        return out

    return _Split("relay_ici_" + "".join(map(str, kinds)), tuple(weights), plan, others, peers="neighbours")


def _gather_w_in_over_ici(w_in, conv4):
    @_plan(6)
    def plan(refs):
        x, y, c, me = _place()
        half, conv = _shard_half_view(0, refs[0], me, c), refs[1].at[me]
        return [(v, v, (px, py, c)) for v in (half, conv) for px, py in _other_chips(x, y)]

    return _Split("gather_w_in_ici", (w_in, conv4), plan, peers="chips")


def _gather_over_d2d(kinds, weights):
    @_plan(3 * len(kinds))
    def plan(refs):
        x, y, c, _ = _place()
        got = [_shard_half_view(kind, ref, 2 * px + py, c) for kind, ref in zip(kinds, refs)
               for px, py in _other_chips(x, y)]
        return [(v, v, (x, y, 1 - c)) for v in got]

    return _Split("gather_d2d_" + "".join(map(str, kinds)), tuple(weights), plan, peers="sibling")


def _swap_halves(kinds, grads):
    @_plan(len(kinds))
    def plan(refs):
        x, y, c, _ = _place()
        return [(_half_view(kind, g, 1 - c), land, (x, y, 1 - c))
                for kind, g, land in zip(kinds, refs[:len(kinds)], refs[len(kinds):])]

    lands = [lax.empty(HALF_SHAPES[kind], g.dtype) for kind, g in zip(kinds, grads)]
    return _Split("swap_halves_" + "".join(map(str, kinds)), (*grads, *lands), plan, peers="sibling")


def _block_rows(cols, elements):
    return 1 << ((elements // cols).bit_length() - 1)


def _add_half(name, g, recv, core, rows_split):
    shape = recv.shape
    tr = min(shape[0], _block_rows(shape[1], 1 << 20))
    nb = shape[0] // tr

    def body(c_ref, g_ref, r_ref, o_ref):
        o_ref[...] = (g_ref[...].astype(F32) + r_ref[...].astype(F32)).astype(o_ref.dtype)

    g_map = (lambda i, c_ref: (c_ref[0] * nb + i, 0)) if rows_split else (lambda i, c_ref: (i, c_ref[0]))
    blk = pl.BlockSpec((tr, shape[1]), lambda i, c_ref: (i, 0))
    return pl.pallas_call(
        body, name=name,
        grid_spec=pltpu.PrefetchScalarGridSpec(
            num_scalar_prefetch=1, grid=(nb,),
            in_specs=[pl.BlockSpec((tr, shape[1]), g_map), blk], out_specs=blk),
        out_shape=jax.ShapeDtypeStruct(shape, BF16),
        compiler_params=_params("parallel"),
    )(core, g, recv)


def _exchange_pieces(kinds, halves, pack=None):
    n_p, n = N_CHIPS - 1, len(kinds)

    @_plan(n_p * n + (0 if pack is None else N_DEV - 1))
    def plan(refs):
        x, y, c, _ = _place()
        copies = []
        if pack is not None:
            me = 4 * x + 2 * y + c
            peers = [((1 - x) if m & 4 else x, (1 - y) if m & 2 else y, (1 - c) if m & 1 else c) for m in range(1, N_DEV)]
            copies += [(refs[2 * n], refs[2 * n + 1].at[me], peer) for peer in peers]
        return copies + [(_shard_view(kind, half, 2 * px + py), land.at[j], (px, py, c))
                         for j, (px, py) in enumerate(_other_chips(x, y))
                         for kind, half, land in zip(kinds, refs[:n], refs[n:2 * n])]

    lands = [lax.empty((n_p,) + PIECE_SHAPES[kind], BF16) for kind in kinds]
    small = () if pack is None else (pack, lax.empty((N_DEV,) + pack.shape, F32))
    return _Split("exchange_pieces_" + "".join(map(str, kinds)), (*halves, *lands, *small), plan,
                  peers="chips" if pack is None else None)


def _sum_pieces(name, half, slots, place, rows_split, after):
    n_p, rows, cols = slots.shape
    tr = min(rows, _block_rows(cols, 1 << 19))
    nb = rows // tr
    if rows_split:
        own_map = lambda i, s: (i, s[0])
        out_map = lambda i, s: (s[1] * nb + i, 0)
        shard = (2 * rows, cols)
    else:
        own_map = lambda i, s: (s[0] * nb + i, 0)
        out_map = lambda i, s: (i, s[1])
        shard = (rows, 2 * cols)

    def body(s_ref, own_ref, slot_ref, after_ref, o_ref):
        total = own_ref[...].astype(F32)
        for j in range(n_p):
            total = total + slot_ref[j].astype(F32)
        o_ref[...] = total

    return pl.pallas_call(
        body, name=name,
        grid_spec=pltpu.PrefetchScalarGridSpec(
            num_scalar_prefetch=1, grid=(nb,),
            in_specs=[pl.BlockSpec((tr, cols), own_map), pl.BlockSpec((n_p, tr, cols), lambda i, s: (0, i, 0)), ANY],
            out_specs=pl.BlockSpec((tr, cols), out_map)),
        out_shape=jax.ShapeDtypeStruct(shard, F32),
        compiler_params=_params("parallel"),
    )(place, half, slots, after)


def _join_halves(kinds, shards):
    @_plan(len(kinds))
    def plan(refs):
        x, y, c, _ = _place()
        return [(_half_view(kind, g, c), _half_view(kind, g, c), (x, y, 1 - c)) for kind, g in zip(kinds, refs)]

    return _Split("join_halves_" + "".join(map(str, kinds)), tuple(shards), plan, peers="sibling")


N_DEV = 8


def _sum_shared(pack, land, device):
    def body(d_ref, p_ref, l_ref, o_ref):
        me = d_ref[0]
        total = jnp.where(me == 0, p_ref[...], l_ref[0])
        for d in range(1, N_DEV):
            total = total + jnp.where(me == d, p_ref[...], l_ref[d])
        o_ref[...] = total

    return pl.pallas_call(
        body, name="sum_shared",
        grid_spec=pltpu.PrefetchScalarGridSpec(
            num_scalar_prefetch=1, grid=(1,),
            in_specs=[pl.BlockSpec(pack.shape, lambda i, d: (0, 0)), pl.BlockSpec(land.shape, lambda i, d: (0, 0, 0))],
            out_specs=pl.BlockSpec(pack.shape, lambda i, d: (0, 0))),
        out_shape=jax.ShapeDtypeStruct(pack.shape, F32),
    )(device, pack, land)


def _adamw(name, w, g, m, v, after=None):
    rows, cols = w.shape
    tr = min(rows, 256)
    extra = [] if after is None else [after]

    def body(w_ref, g_ref, m_ref, v_ref, *rest):
        d_ref, nm_ref, nv_ref = rest[-3:]
        d_ref[...], nm_ref[...], nv_ref[...] = _adam_step(w_ref[...], g_ref[...], m_ref[...], v_ref[...])

    blk = pl.BlockSpec((tr, cols), lambda i: (i, 0))
    return pl.pallas_call(
        body, name=name, grid=(rows // tr,), in_specs=[blk] * 4 + [ANY] * len(extra), out_specs=[blk] * 3,
        out_shape=[jax.ShapeDtypeStruct(w.shape, F32)] * 3,
        compiler_params=_params("parallel"),
    )(w, g, m, v, *extra)


def _adam_step(w, g, m, v):
    nm = ADAM_B1 * m + (1.0 - ADAM_B1) * g
    nv = ADAM_B2 * v + (1.0 - ADAM_B2) * jnp.square(g)
    m_hat = nm * (1.0 / (1.0 - ADAM_B1 ** ADAM_STEP))
    v_hat = nv * (1.0 / (1.0 - ADAM_B2 ** ADAM_STEP))
    return -ADAM_LR * (m_hat / (jnp.sqrt(v_hat) + ADAM_EPS) + ADAM_WD * w), nm, nv


def _adamw_small(tot, chip, weights, ms, vs, after):
    n, half = len(weights), D_MODEL // 2

    def body(chip_ref, tot_ref, *refs):
        ins, outs = refs[:3 * n], refs[3 * n + 1:]
        tot = tot_ref[...]
        conv_all = jnp.concatenate([tot[5:6, half:], tot[6:7, :half], tot[6:7, half:]], axis=0)
        conv = sum(jnp.where(chip_ref[0] == s, conv_all[:, s * LANES:(s + 1) * LANES], 0.0) for s in range(N_CHIPS))
        grads = [jnp.concatenate([tot[4:5, :half], tot[4:5, half:]], axis=0), tot[5:6, :half], conv,
                 tot[0:1], tot[1:2], tot[2:3], tot[3:4]]
        for k, g in enumerate(grads):
            delta, nm, nv = _adam_step(ins[k][...], g, ins[n + k][...], ins[2 * n + k][...])
            outs[k][...], outs[n + k][...], outs[2 * n + k][...], outs[3 * n + k][...] = g, delta, nm, nv

    whole = lambda a: pl.BlockSpec(a.shape, lambda i, s: (0,) * a.ndim)
    arrays = (*weights, *ms, *vs)
    return pl.pallas_call(
        body, name="adamw_small",
        grid_spec=pltpu.PrefetchScalarGridSpec(
            num_scalar_prefetch=1, grid=(1,), in_specs=[whole(tot)] + [whole(a) for a in arrays] + [ANY],
            out_specs=[whole(a) for a in weights] * 4),
        out_shape=[jax.ShapeDtypeStruct(a.shape, F32) for a in weights] * 4,
    )(chip, tot, *arrays, after)


def kernel(x, w_in, lb_logits, gate_norm_w, conv_w, w_out, ln1_g, ln1_b, w_ff1, w_ff2, ln2_g, ln2_b, loss_target, m_w_in, m_lb_logits, m_gate_norm_w, m_conv_w, m_w_out, m_ln1_g, m_ln1_b, m_w_ff1, m_w_ff2, m_ln2_g, m_ln2_b, v_w_in, v_lb_logits, v_gate_norm_w, v_conv_w, v_w_out, v_ln1_g, v_ln1_b, v_w_ff1, v_w_ff2, v_ln2_g, v_ln2_b):
    xs, tgt = x[0], loss_target[0]
    chip = 2 * lax.axis_index("x") + lax.axis_index("y")
    core = lax.axis_index("c").astype(jnp.int32).reshape(1)
    chip1 = chip.astype(jnp.int32).reshape(1)
    place = jnp.concatenate([chip1, core])

    conv4 = lax.dynamic_update_slice(jnp.zeros((N_CHIPS,) + conv_w.shape[1:], F32), conv_w, (chip, 0, 0))
    ici_in = _gather_w_in_over_ici(_place_shard("place_w_in", w_in[0], chip1, True), conv4)
    rest = (1, 2, 3)
    ici_rest = _gather_over_ici(rest, (_place_shard("place_w_out", w_out[0], chip1, False, after=ici_in.token),
                                       _place_shard("place_w_ff1", w_ff1[0], chip1, True, after=ici_in.token),
                                       _place_shard("place_w_ff2", w_ff2[0], chip1, False, after=ici_in.token)))
    wb_in, cv4 = ici_in.wait(ici_rest.token)
    d2d_in = _gather_over_d2d((0,), (wb_in,))
    wb_in, = d2d_in.wait(d2d_in.token)
    conv_full = cv4.transpose(1, 0, 2).reshape(3, CONV_WIDTH)

    proj, bcu, xb, cat_c = _in_proj(xs, wb_in, conv_full, ici_rest.token)
    relay_rest = _relay_over_ici(rest, ici_rest.wait(proj))
    o, states = _hgrn_fwd(proj, lb_logits, relay_rest.token)
    d2d_rest = _gather_over_d2d(rest, relay_rest.wait(o))
    cat_h = _gate_fwd(proj, o, gate_norm_w, d2d_rest.token)
    wb_out, wb_ff1, wb_ff2 = d2d_rest.wait(cat_h)

    (h1b, r, da, dpre2b, dpre1, dpre1b, dcat, g_ln1_g, g_ln1_b, g_ln2_g, g_ln2_b, loss8) = _sublayers(
        cat_h, cat_c, xs, tgt, wb_out, wb_ff1, wb_ff2, ln1_g, ln1_b, ln2_g, ln2_b)

    names = ("w_in", "w_out", "w_ff1", "w_ff2")

    def add_halves(kinds, grads, lands):
        return [_add_half("add_half_" + names[k], g, ld, core, COLS_SHARDED[k]) for k, g, ld in zip(kinds, grads, lands)]

    def sum_pieces(kinds, halves, lands, after):
        return [_sum_pieces("sum_pieces_" + names[k], h, ld, place, COLS_SHARDED[k], after)
                for k, h, ld in zip(kinds, halves, lands)]

    early = (1, 2, 3)
    g_out_local = _dw_out(cat_h, cat_c, dpre1b)
    g_ff2_local, dpc, g_conv = _dw_ff2(r, dpre2b, dcat, bcu, conv_full)
    swap_a = _swap_halves((1, 3), (g_out_local, g_ff2_local))
    g_ff1_local, do, dog, g_gnw = _dw_ff1(h1b, da, dcat, o, proj, gate_norm_w, swap_a.token)
    swap_b = _swap_halves((2,), (g_ff1_local,))
    swapped_a = swap_a.wait(swap_b.token)
    halves_a = add_halves((1, 3), swapped_a[:2], swapped_a[2:])
    swapped_b = swap_b.wait(halves_a[1])
    halves = (halves_a[0], *add_halves((2,), swapped_b[:1], swapped_b[1:]), halves_a[1])
    exch = _exchange_pieces(early, halves)
    dph, g_lbl = _hgrn_bwd(proj, do, states, lb_logits, exch.token)
    g_in_local = _dw_in(xb, dph, dog, dpc, dph)

    late = (0,)
    swap = _swap_halves(late, (g_in_local,))
    grad_x = _in_bwd(dph, dog, dpc, wb_in, dpre1, swap.token)
    exchanged = exch.wait(grad_x)
    pack = jnp.concatenate([
        g_ln1_g, g_ln1_b, g_ln2_g, g_ln2_b,
        jnp.concatenate([g_lbl[0:1], g_lbl[1:2]], axis=1),
        jnp.concatenate([g_gnw, g_conv[0:1]], axis=1),
        jnp.concatenate([g_conv[1:2], g_conv[2:3]], axis=1),
        jnp.concatenate([loss8[0:1], jnp.zeros((1, D_MODEL - LANES), F32)], axis=1)], axis=0)
    swapped = swap.wait(exchanged[0])
    exch = _exchange_pieces(late, add_halves(late, swapped[:1], swapped[1:]), pack)
    join = _join_halves(early, sum_pieces(early, exchanged[:3], exchanged[3:], exch.token))
    g_w_out, g_w_ff1, g_w_ff2 = join.wait(join.token)
    d_ff1, nm_ff1, nv_ff1 = _adamw("adamw_w_ff1", w_ff1[0], g_w_ff1, m_w_ff1[0], v_w_ff1[0])
    d_ff2, nm_ff2, nv_ff2 = _adamw("adamw_w_ff2", w_ff2[0], g_w_ff2, m_w_ff2[0], v_w_ff2[0], d_ff1)
    d_out, nm_out, nv_out = _adamw("adamw_w_out", w_out[0], g_w_out, m_w_out[0], v_w_out[0], d_ff2)
    exchanged = exch.wait(d_out)
    tot = _sum_shared(exchanged[2], exchanged[3], 2 * chip1 + core)
    loss = tot[7, 0]
    join = _join_halves(late, sum_pieces(late, exchanged[:1], exchanged[1:2], tot))
    small = ("lb_logits", "gate_norm_w", "conv_w", "ln1_g", "ln1_b", "ln2_g", "ln2_b")
    small_out = _adamw_small(
        tot, chip1, (lb_logits, gate_norm_w, conv_w[0], ln1_g, ln1_b, ln2_g, ln2_b),
        (m_lb_logits, m_gate_norm_w, m_conv_w[0], m_ln1_g, m_ln1_b, m_ln2_g, m_ln2_b),
        (v_lb_logits, v_gate_norm_w, v_conv_w[0], v_ln1_g, v_ln1_b, v_ln2_g, v_ln2_b), join.token)
    g_w_in, = join.wait(small_out[0])
    d_in, nm_in, nv_in = _adamw("adamw_w_in", w_in[0], g_w_in, m_w_in[0], v_w_in[0])

    def results(n_kind, large):
        out = dict(zip(small, small_out[n_kind * len(small):(n_kind + 1) * len(small)]))
        out["conv_w"] = out["conv_w"][None]
        out.update({name: a[None] for name, a in zip(("w_in", "w_out", "w_ff1", "w_ff2"), large)})
        return [out[name] for name in ("w_in", "lb_logits", "gate_norm_w", "conv_w", "w_out", "ln1_g", "ln1_b",
                                       "w_ff1", "w_ff2", "ln2_g", "ln2_b")]

    return (loss, grad_x[None], *results(0, (g_w_in, g_w_out, g_w_ff1, g_w_ff2)),
            *results(1, (d_in, d_out, d_ff1, d_ff2)), *results(2, (nm_in, nm_out, nm_ff1, nm_ff2)),
            *results(3, (nv_in, nv_out, nv_ff1, nv_ff2)))
```

```python
import jax
import jax.numpy as jnp
from jax import lax
from jax.experimental import pallas as pl
from jax.experimental.pallas import tpu as pltpu

F32 = jnp.float32
BF16 = jnp.bfloat16
MXU_DTYPE = jnp.bfloat16

D_MODEL = 1024
HGRN_WIDTH = 512
HEAD_DIM = 128
N_HEADS = 4
CONV_WIDTH = 512
CHUNK = 64
D_FF = 4096
IN_COLS = 3584
GROUP = 512
N_GROUPS = IN_COLS // GROUP
ALPHA = 2.0 ** 0.25
EPS = 1e-5
N_CHIPS = 4
ADAM_LR, ADAM_B1, ADAM_B2, ADAM_EPS, ADAM_WD, ADAM_STEP = 0.001, 0.9, 0.999, 1e-08, 0.01, 10

LANES = 128
SUBLANES = 8
VMEM_LIMIT = 56 * 1024 * 1024
FF_BLOCK = 1024
N_FF = D_FF // FF_BLOCK
GATE_STRIP = 64

NN = (((1,), (0,)), ((), ()))
NT = (((1,), (1,)), ((), ()))
TN = (((0,), (0,)), ((), ()))
MESH = pl.DeviceIdType.MESH
ANY = pl.BlockSpec(memory_space=pl.ANY)


def _dot(a, b, dims):
    return lax.dot_general(a.astype(MXU_DTYPE), b.astype(MXU_DTYPE), dims, preferred_element_type=F32)


def _dot_exact(ones, v):
    ones = ones.astype(jnp.bfloat16)
    hi = v.astype(jnp.bfloat16)
    rest = v - hi.astype(F32)
    mid = rest.astype(jnp.bfloat16)
    low = (rest - mid.astype(F32)).astype(jnp.bfloat16)
    return sum(lax.dot_general(ones, part, NN, preferred_element_type=F32) for part in (hi, mid, low))


def _params(*sem):
    return pltpu.CompilerParams(dimension_semantics=sem, vmem_limit_bytes=VMEM_LIMIT)


def _resident(shape):
    return pl.BlockSpec(shape, lambda *_: (0,) * len(shape), pipeline_mode=pl.Buffered(1))


def _sigmoid(v):
    return 1.0 / (1.0 + jnp.exp(-v))


def _lower_bound(lbl):
    m = jnp.max(lbl, axis=0, keepdims=True)
    e = jnp.exp(lbl - m)
    s = e / jnp.sum(e, axis=0, keepdims=True)
    return s[0:1, :], s[1:2, :]


def _heads(v):
    return [v[:, h * HEAD_DIM:(h + 1) * HEAD_DIM] for h in range(N_HEADS)]


def _per_head(fn, *arrays):
    return jnp.concatenate([fn(*parts) for parts in zip(*map(_heads, arrays))], axis=1)


def _in_proj(x, w_in, conv_w, after):
    t = x.shape[0]
    tm = min(t, 512)

    def body(x_ref, w_ref, cw_ref, after_ref, o_ref, bcu_ref, xb_ref, y_ref, zbuf):
        @pl.when(pl.program_id(0) == 0)
        def _():
            zbuf[tm:tm + SUBLANES, :] = jnp.zeros((SUBLANES, CONV_WIDTH), F32)

        xb = x_ref[...].astype(xb_ref.dtype)
        xb_ref[...] = xb
        group = lambda g: _dot(xb, w_ref[:, g * GROUP:(g + 1) * GROUP], NN)
        for g in range(4):
            o_ref[g] = group(g)
        b_gate, c_gate, u = group(4), group(5), group(6)
        for n, part in enumerate((b_gate, c_gate, u)):
            bcu_ref[n] = part.astype(bcu_ref.dtype)
        zbuf[0:SUBLANES, :] = zbuf[tm:tm + SUBLANES, :]
        zbuf[SUBLANES:SUBLANES + tm, :] = c_gate * u
        cw = cw_ref[...]
        at = lambda shift: zbuf[shift:shift + tm, :]
        conv = cw[2:3, :] * at(SUBLANES) + cw[1:2, :] * at(SUBLANES - 1) + cw[0:1, :] * at(SUBLANES - 2)
        y_ref[...] = (b_gate * conv).astype(y_ref.dtype)

    return pl.pallas_call(
        body, name="in_proj", grid=(t // tm,),
        in_specs=[pl.BlockSpec((tm, D_MODEL), lambda i: (i, 0)), _resident((D_MODEL, IN_COLS)),
                  pl.BlockSpec((3, CONV_WIDTH), lambda i: (0, 0)), ANY],
        out_specs=[pl.BlockSpec((4, tm, GROUP), lambda i: (0, i, 0)), pl.BlockSpec((3, tm, GROUP), lambda i: (0, i, 0)),
                   pl.BlockSpec((tm, D_MODEL), lambda i: (i, 0)), pl.BlockSpec((tm, CONV_WIDTH), lambda i: (i, 0))],
        out_shape=[jax.ShapeDtypeStruct((4, t, GROUP), F32), jax.ShapeDtypeStruct((3, t, GROUP), BF16),
                   jax.ShapeDtypeStruct((t, D_MODEL), BF16), jax.ShapeDtypeStruct((t, CONV_WIDTH), BF16)],
        scratch_shapes=[pltpu.VMEM((tm + SUBLANES, CONV_WIDTH), F32)],
        compiler_params=_params("arbitrary"),
    )(x, w_in, conv_w, after)


def _gates(fp, lb):
    sig = _sigmoid(fp)
    f = lb + (1.0 - lb) * sig
    return sig, f, jnp.log(f), 1.0 - f


def _chunk_masks():
    row = lax.broadcasted_iota(jnp.int32, (CHUNK, CHUNK), 0)
    col = lax.broadcasted_iota(jnp.int32, (CHUNK, CHUNK), 1)
    return row >= col, row <= col


def _hgrn_fwd(proj, lb_logits, after):
    t = proj.shape[1]
    tb = min(t, 512)
    ncb = tb // CHUNK

    def body(q_ref, f_ref, v_ref, lbl_ref, after_ref, o_ref, st_ref, s_scr):
        @pl.when(pl.program_id(0) == 0)
        def _():
            s_scr[...] = jnp.zeros_like(s_scr)

        lb, _ = _lower_bound(lbl_ref[...])
        causal, _ = _chunk_masks()

        every = range(ncb)
        rows = [slice(c * CHUNK, (c + 1) * CHUNK) for c in every]
        q, v = [q_ref[r, :] for r in rows], [v_ref[r, :] for r in rows]
        gates = [_gates(f_ref[r, :], lb) for r in rows]
        k = [gt[3] for gt in gates]
        b = [_dot_exact(causal, gt[2]) for gt in gates]
        mid, last = [x[CHUNK // 2:CHUNK // 2 + 1, :] for x in b], [x[CHUNK - 1:CHUNK, :] for x in b]
        qt = [q[c] * jnp.exp(b[c] - mid[c]) for c in every]
        kt = [k[c] * jnp.exp(mid[c] - b[c]) for c in every]
        qi = [q[c] * jnp.exp(b[c]) for c in every]
        ks = [k[c] * jnp.exp(last[c] - b[c]) for c in every]
        dec = [jnp.exp(x) for x in last]
        scores = [[jnp.where(causal, _dot(a, b_, NT), 0.0) for a, b_ in zip(_heads(qt[c]), _heads(kt[c]))] for c in every]
        intra = [[_dot(s, v_h, NN) for s, v_h in zip(scores[c], _heads(v[c]))] for c in every]
        update = [_per_head(lambda v_h, ks_h: _dot(v_h, ks_h, TN), v[c], ks[c]) for c in every]

        st = s_scr[...]
        states = []
        for c in every:
            states.append(st)
            st_ref[c] = st
            st = dec[c] * st + update[c]
        s_scr[...] = st

        o_ref[...] = jnp.concatenate(
            [jnp.concatenate([i_h + _dot(qi_h, st_h, NT) for i_h, qi_h, st_h in
                              zip(intra[c], _heads(qi[c]), _heads(states[c]))], axis=1) for c in every], axis=0)

    grp = lambda g: pl.BlockSpec((None, tb, GROUP), lambda i: (g, i, 0))
    return pl.pallas_call(
        body, name="hgrn_fwd", grid=(t // tb,),
        in_specs=[grp(0), grp(1), grp(2), pl.BlockSpec((2, HGRN_WIDTH), lambda i: (0, 0)), ANY],
        out_specs=[pl.BlockSpec((tb, HGRN_WIDTH), lambda i: (i, 0)),
                   pl.BlockSpec((ncb, HEAD_DIM, HGRN_WIDTH), lambda i: (i, 0, 0))],
        out_shape=[jax.ShapeDtypeStruct((t, HGRN_WIDTH), F32),
                   jax.ShapeDtypeStruct((t // CHUNK, HEAD_DIM, HGRN_WIDTH), F32)],
        scratch_shapes=[pltpu.VMEM((HEAD_DIM, HGRN_WIDTH), F32)],
        compiler_params=_params("arbitrary"),
    )(proj, proj, proj, lb_logits, after)


def _gate_fwd(proj, o, gate_norm_w, after):
    t = proj.shape[1]
    tb = min(t, 512)

    def body(o_ref, og_ref, gnw_ref, after_ref, out_ref):
        gnw = gnw_ref[...]
        for s in range(tb // GATE_STRIP):
            rows = slice(s * GATE_STRIP, (s + 1) * GATE_STRIP)
            og = og_ref[rows, :]
            on = _per_head(lambda o_h: o_h * lax.rsqrt(jnp.mean(o_h * o_h, axis=-1, keepdims=True) + EPS), o_ref[rows, :])
            out_ref[rows, :] = (on * gnw * (og * _sigmoid(og))).astype(out_ref.dtype)

    tile = pl.BlockSpec((tb, GROUP), lambda i: (i, 0))
    return pl.pallas_call(
        body, name="gate_fwd", grid=(t // tb,),
        in_specs=[tile, pl.BlockSpec((None, tb, GROUP), lambda i: (3, i, 0)), pl.BlockSpec((1, GROUP), lambda i: (0, 0)), ANY],
        out_specs=tile,
        out_shape=jax.ShapeDtypeStruct((t, HGRN_WIDTH), BF16),
        compiler_params=_params("parallel"),
    )(o, proj, gate_norm_w, after)


def _ln_bwd(dy, xhat, rstd, g):
    dxhat = dy * g
    m1 = jnp.mean(dxhat, axis=-1, keepdims=True)
    m2 = jnp.mean(dxhat * xhat, axis=-1, keepdims=True)
    return rstd * (dxhat - m1 - xhat * m2)


def _layer_norm(pre):
    xc = pre - jnp.mean(pre, axis=-1, keepdims=True)
    rstd = lax.rsqrt(jnp.mean(xc * xc, axis=-1, keepdims=True) + EPS)
    return xc * rstd, rstd


def _sublayers(cat_h, cat_c, x, target, w_out, w_ff1, w_ff2, g1, b1, g2, b2):
    t = x.shape[0]
    tm = min(t, 256)

    def body(ch_ref, cc_ref, x_ref, tg_ref, wo_ref, w1_ref, w2_ref, g1_ref, b1_ref, g2_ref, b2_ref,
             h1_ref, r_ref, da_ref, dp2b_ref, dp1_ref, dp1b_ref, dcat_ref, dg1_ref, db1_ref, dg2_ref, db2_ref, loss_ref):
        @pl.when(pl.program_id(0) == 0)
        def _():
            for ref in (dg1_ref, db1_ref, dg2_ref, db2_ref, loss_ref):
                ref[...] = jnp.zeros_like(ref)

        mix = _dot(ch_ref[...], wo_ref[0:GROUP, :], NN) + _dot(cc_ref[...], wo_ref[GROUP:2 * GROUP, :], NN)
        xhat1, rstd1 = _layer_norm(ALPHA * x_ref[...] + mix)
        h1 = xhat1 * g1_ref[...] + b1_ref[...]
        h1b = h1.astype(h1_ref.dtype)
        h1_ref[...] = h1b
        mlp = jnp.zeros((tm, D_MODEL), F32)
        for j in range(N_FF):
            cols = slice(j * FF_BLOCK, (j + 1) * FF_BLOCK)
            r = jnp.square(jnp.maximum(_dot(h1b, w1_ref[:, cols], NN), 0.0)).astype(r_ref.dtype)
            r_ref[:, cols] = r
            mlp = mlp + _dot(r, w2_ref[cols, :], NN)
        xhat2, rstd2 = _layer_norm(ALPHA * h1 + mlp)
        err = xhat2 * g2_ref[...] + b2_ref[...] - tg_ref[...]
        loss_ref[...] += 0.5 * jnp.sum(jnp.mean(err * err, axis=-1, keepdims=True))
        dy = err * (1.0 / D_MODEL)
        dg2_ref[...] += jnp.sum(dy * xhat2, axis=0, keepdims=True)
        db2_ref[...] += jnp.sum(dy, axis=0, keepdims=True)
        dp2 = _ln_bwd(dy, xhat2, rstd2, g2_ref[...])
        dp2b = dp2.astype(dp2b_ref.dtype)
        dp2b_ref[...] = dp2b
        back = jnp.zeros((tm, D_MODEL), F32)
        for j in range(N_FF):
            cols = slice(j * FF_BLOCK, (j + 1) * FF_BLOCK)
            dr = _dot(dp2b, w2_ref[cols, :], NT)
            da = (dr * (2.0 * jnp.sqrt(r_ref[:, cols].astype(F32)))).astype(da_ref.dtype)
            da_ref[:, cols] = da
            back = back + _dot(da, w1_ref[:, cols], NT)
        dh1 = ALPHA * dp2 + back
        dg1_ref[...] += jnp.sum(dh1 * xhat1, axis=0, keepdims=True)
        db1_ref[...] += jnp.sum(dh1, axis=0, keepdims=True)
        dp1 = _ln_bwd(dh1, xhat1, rstd1, g1_ref[...])
        dp1b = dp1.astype(dp1b_ref.dtype)
        dp1_ref[...] = dp1
        dp1b_ref[...] = dp1b
        dcat_ref[...] = _dot(dp1b, wo_ref[...], NT)

    row = pl.BlockSpec((tm, D_MODEL), lambda i: (i, 0))
    wide = pl.BlockSpec((tm, D_FF), lambda i: (i, 0))
    vec = pl.BlockSpec((1, D_MODEL), lambda i: (0, 0))
    narrow = lambda dtype: jax.ShapeDtypeStruct((t, D_MODEL), dtype)
    return pl.pallas_call(
        body, name="sublayers", grid=(t // tm,),
        in_specs=[pl.BlockSpec((tm, GROUP), lambda i: (i, 0)), pl.BlockSpec((tm, GROUP), lambda i: (i, 0)), row, row,
                  _resident((D_MODEL, D_MODEL)),
                  _resident((D_MODEL, D_FF)), _resident((D_FF, D_MODEL)), vec, vec, vec, vec],
        out_specs=[row, wide, wide, row, row, row, row, vec, vec, vec, vec,
                   pl.BlockSpec((SUBLANES, LANES), lambda i: (0, 0))],
        out_shape=[narrow(BF16), jax.ShapeDtypeStruct((t, D_FF), BF16), jax.ShapeDtypeStruct((t, D_FF), BF16),
                   narrow(BF16), narrow(F32), narrow(BF16), narrow(F32)]
                  + [jax.ShapeDtypeStruct((1, D_MODEL), F32)] * 4 + [jax.ShapeDtypeStruct((SUBLANES, LANES), F32)],
        compiler_params=_params("arbitrary"),
    )(cat_h, cat_c, x, target, w_out, w_ff1, w_ff2, g1, b1, g2, b2)


def _hgrn_bwd(proj, do, states, lb_logits, after):
    t = proj.shape[1]
    tb = min(t, 512)
    ncb = tb // CHUNK
    nblk = t // tb

    def body(q_ref, f_ref, v_ref, do_ref, st_ref, lbl_ref, after_ref, dp_ref, dlbl_ref, ds_scr, dlb_scr):
        i = pl.program_id(0)

        @pl.when(i == 0)
        def _():
            ds_scr[...] = jnp.zeros_like(ds_scr)
            dlb_scr[...] = jnp.zeros_like(dlb_scr)

        lb, s1 = _lower_bound(lbl_ref[...])
        causal, anti = _chunk_masks()
        every = range(ncb)
        rows = [slice(c * CHUNK, (c + 1) * CHUNK) for c in every]
        q, v, do = ([ref[r, :] for r in rows] for ref in (q_ref, v_ref, do_ref))
        st = [st_ref[c] for c in every]
        gates = [_gates(f_ref[r, :], lb) for r in rows]
        sig, f, k = ([gt[n] for gt in gates] for n in (0, 1, 3))
        b = [_dot_exact(causal, gt[2]) for gt in gates]
        mid, last = [x[CHUNK // 2:CHUNK // 2 + 1, :] for x in b], [x[CHUNK - 1:CHUNK, :] for x in b]
        e_q = [jnp.exp(b[c] - mid[c]) for c in every]
        e_k = [jnp.exp(mid[c] - b[c]) for c in every]
        e_i = [jnp.exp(x) for x in b]
        e_s = [jnp.exp(last[c] - b[c]) for c in every]
        dec = [jnp.exp(x) for x in last]
        qt, kt, qi, ks = ([a[c] * e[c] for c in every] for a, e in ((q, e_q), (k, e_k), (q, e_i), (k, e_s)))

        def masked(a, b_):
            return [[jnp.where(causal, _dot(a_h, b_h, NT), 0.0) for a_h, b_h in zip(_heads(a[c]), _heads(b_[c]))]
                    for c in every]

        def with_scores(s, other, dims):
            return [jnp.concatenate([_dot(s_h, o_h, dims) for s_h, o_h in zip(s[c], _heads(other[c]))], axis=1)
                    for c in every]

        def per_head(dims, a, b_):
            return [_per_head(lambda a_h, b_h: _dot(a_h, b_h, dims), a[c], b_[c]) for c in every]

        scores, dscores = masked(qt, kt), masked(do, v)
        dqt, dkt, dv_intra = with_scores(dscores, kt, NN), with_scores(dscores, qt, TN), with_scores(scores, do, TN)
        dqi, update = per_head(NN, do, st), per_head(TN, do, qi)

        dst = ds_scr[...]
        dsts = [None] * ncb
        for c in reversed(every):
            dsts[c] = dst
            dst = dec[c] * dst + update[c]
        ds_scr[...] = dst

        dv_state, dks = per_head(NT, ks, dsts), per_head(NN, v, dsts)
        ddec = [jnp.sum(dsts[c] * st[c], axis=0, keepdims=True) for c in every]
        dq = [dqt[c] * e_q[c] + dqi[c] * e_i[c] for c in every]
        dk = [dkt[c] * e_k[c] + dks[c] * e_s[c] for c in every]
        db = [q[c] * dq[c] - k[c] * dk[c] for c in every]
        db_last = [jnp.sum(dks[c] * ks[c], axis=0, keepdims=True) + ddec[c] * dec[c] for c in every]
        dg = [_dot_exact(anti, db[c]) + db_last[c] for c in every]
        df = [dg[c] / f[c] - dk[c] for c in every]
        dlb_scr[...] += sum(jnp.sum(df[c] * (1.0 - sig[c]), axis=0, keepdims=True) for c in every)
        dfp = [df[c] * (1.0 - lb) * sig[c] * (1.0 - sig[c]) for c in every]
        dv = [dv_intra[c] + dv_state[c] for c in every]
        for n, parts in enumerate((dq, dfp, dv)):
            dp_ref[n] = jnp.concatenate(parts, axis=0).astype(dp_ref.dtype)

        @pl.when(i == nblk - 1)
        def _():
            dlb = dlb_scr[...]
            dlbl_ref[0:1, :] = dlb * lb * (1.0 - lb)
            dlbl_ref[1:2, :] = -dlb * lb * s1

    grp = lambda g: pl.BlockSpec((None, tb, GROUP), lambda i: (g, nblk - 1 - i, 0))
    vec = pl.BlockSpec((2, HGRN_WIDTH), lambda i: (0, 0))
    return pl.pallas_call(
        body, name="hgrn_bwd", grid=(nblk,),
        in_specs=[grp(0), grp(1), grp(2), pl.BlockSpec((tb, HGRN_WIDTH), lambda i: (nblk - 1 - i, 0)),
                  pl.BlockSpec((ncb, HEAD_DIM, HGRN_WIDTH), lambda i: (nblk - 1 - i, 0, 0)), vec, ANY],
        out_specs=[pl.BlockSpec((3, tb, HGRN_WIDTH), lambda i: (0, nblk - 1 - i, 0)), vec],
        out_shape=[jax.ShapeDtypeStruct((3, t, HGRN_WIDTH), BF16), jax.ShapeDtypeStruct((2, HGRN_WIDTH), F32)],
        scratch_shapes=[pltpu.VMEM((HEAD_DIM, HGRN_WIDTH), F32), pltpu.VMEM((1, HGRN_WIDTH), F32)],
        compiler_params=_params("arbitrary"),
    )(proj, proj, proj, do, states, lb_logits, after)


def _in_bwd(dph, dog, dpc, w_in, dpre1, after):
    t = dpre1.shape[0]
    tm = min(t, 512)

    def body(dh_ref, dog_ref, dc_ref, w_ref, dp_ref, after_ref, o_ref):
        acc = ALPHA * dp_ref[...]
        for g in range(N_GROUPS):
            part = dh_ref[g] if g < 3 else dog_ref[...] if g == 3 else dc_ref[g - 4]
            acc = acc + _dot(part, w_ref[:, g * GROUP:(g + 1) * GROUP], NT)
        o_ref[...] = acc

    row = pl.BlockSpec((tm, D_MODEL), lambda i: (i, 0))
    three = pl.BlockSpec((3, tm, GROUP), lambda i: (0, i, 0))
    return pl.pallas_call(
        body, name="in_bwd", grid=(t // tm,),
        in_specs=[three, pl.BlockSpec((tm, GROUP), lambda i: (i, 0)), three, _resident((D_MODEL, IN_COLS)), row, ANY],
        out_specs=row,
        out_shape=jax.ShapeDtypeStruct((t, D_MODEL), F32),
        compiler_params=_params("parallel"),
    )(dph, dog, dpc, w_in, dpre1, after)


GRAD_TILE = 512
OUT_PARTS = 4


class _Side:
    def __init__(self, operands, in_specs, out_shape, out_specs, scratch, init, begin):
        self.operands, self.in_specs, self.out_shape, self.out_specs = operands, in_specs, out_shape, out_specs
        self.scratch, self.init, self.begin = scratch, init, begin


def _grad_w(name, operands, widths, shape, step, after=None, side=None):
    t = operands[0].shape[-2]
    tt = min(t, GRAD_TILE)
    n_in, n_steps = len(operands), t // tt
    in_specs = [pl.BlockSpec((tt, w), lambda k: (k, 0)) if a.ndim == 2 else
                pl.BlockSpec((a.shape[0], tt, w), lambda k: (0, k, 0)) for a, w in zip(operands, widths)]
    extra = [] if after is None else [after]
    s_in, s_out = (len(side.operands), len(side.out_shape)) if side else (0, 0)
    first_out = n_in + s_in + len(extra)

    def body(*refs):
        o_ref, side_outs = refs[first_out], refs[first_out + 1:first_out + 1 + s_out]
        acc, narrow, sem = refs[first_out + 1 + s_out:first_out + 4 + s_out]
        k = pl.program_id(0)

        @pl.when(k == 0)
        def _():
            acc[...] = jnp.zeros_like(acc)
            if side:
                side.init(side_outs)

        tick = side.begin(k, n_steps, refs[n_in:n_in + s_in], side_outs, refs[first_out + 4 + s_out:]) if side else None
        step(acc, *refs[:n_in], tick or (lambda j: None))

        @pl.when(k == n_steps - 1)
        def _():
            part = shape[0] // OUT_PARTS
            copies = []
            for p in range(OUT_PARTS):
                rows = pl.ds(p * part, part)
                narrow[rows, :] = acc[rows, :].astype(narrow.dtype)
                copies.append(pltpu.make_async_copy(narrow.at[rows, :], o_ref.at[rows, :], sem.at[p]))
                copies[-1].start()
            for cp in copies:
                cp.wait()

    outs = pl.pallas_call(
        body, name=name, grid=(n_steps,),
        in_specs=in_specs + (side.in_specs if side else []) + [ANY] * len(extra),
        out_specs=[ANY] + (side.out_specs if side else []),
        out_shape=[jax.ShapeDtypeStruct(shape, BF16)] + (side.out_shape if side else []),
        scratch_shapes=[pltpu.VMEM(shape, F32), pltpu.VMEM(shape, BF16), pltpu.SemaphoreType.DMA((OUT_PARTS,))]
                       + (side.scratch if side else []),
        compiler_params=_params("arbitrary"),
    )(*operands, *(side.operands if side else ()), *extra)
    return outs if side else outs[0]


def _dw_in(xb, dph, dog, dpc, after):
    def step(acc, x_ref, dh_ref, dog_ref, dc_ref, tick):
        xv = x_ref[...]
        for g in range(N_GROUPS):
            part = dh_ref[g] if g < 3 else dog_ref[...] if g == 3 else dc_ref[g - 4]
            acc[:, g * GROUP:(g + 1) * GROUP] += _dot(xv, part, TN)

    return _grad_w("dw_in", (xb, dph, dog, dpc), (D_MODEL, GROUP, GROUP, GROUP), (D_MODEL, IN_COLS), step, after)


def _dw_out(cat_h, cat_c, dpre1b):
    def step(acc, h_ref, c_ref, d_ref, tick):
        dv = d_ref[...]
        acc[0:GROUP, :] += _dot(h_ref[...], dv, TN)
        acc[GROUP:2 * GROUP, :] += _dot(c_ref[...], dv, TN)

    return _grad_w("dw_out", (cat_h, cat_c, dpre1b), (GROUP, GROUP, D_MODEL), (D_MODEL, D_MODEL), step)


def _strips_of(j, tt):
    per_tick = tt // GATE_STRIP // N_FF
    return [slice(s * GATE_STRIP, (s + 1) * GATE_STRIP) for s in range(j * per_tick, (j + 1) * per_tick)]


def _dw_ff1(h1b, da, dcat, o, proj, gate_norm_w, after):
    t = h1b.shape[0]
    tt = min(t, GRAD_TILE)

    def step(acc, h_ref, da_ref, tick):
        hv = h_ref[...]
        for j in range(N_FF):
            cols = slice(j * FF_BLOCK, (j + 1) * FF_BLOCK)
            acc[:, cols] += _dot(hv, da_ref[:, cols], TN)
            tick(j)

    def init(outs):
        outs[2][...] = jnp.zeros_like(outs[2])

    def begin(k, n_steps, ins, outs, scratch):
        do2_ref, o_ref, og_ref, gnw_ref = ins
        do_ref, dog_ref, dgnw_ref = outs
        total = [jnp.zeros((GATE_STRIP, GROUP), F32)]

        def tick(j):
            gnw = gnw_ref[...]
            for rows in _strips_of(j, tt):
                ov, og, do2 = o_ref[rows, :], og_ref[rows, :], do2_ref[rows, :]
                rs = _per_head(lambda o_h: jnp.broadcast_to(
                    lax.rsqrt(jnp.mean(o_h * o_h, axis=-1, keepdims=True) + EPS), o_h.shape), ov)
                on = ov * rs
                sg = _sigmoid(og)
                sil = og * sg
                don = do2 * gnw * sil
                total[0] = total[0] + do2 * on * sil
                dog_ref[rows, :] = (do2 * on * gnw * (sg * (1.0 + og * (1.0 - sg)))).astype(dog_ref.dtype)
                do_ref[rows, :] = rs * (don - on * _per_head(
                    lambda p_h: jnp.broadcast_to(jnp.mean(p_h, axis=-1, keepdims=True), p_h.shape), don * on))
            if j == N_FF - 1:
                dgnw_ref[...] += jnp.sum(total[0], axis=0, keepdims=True)

        return tick

    tile = pl.BlockSpec((tt, GROUP), lambda k: (k, 0))
    vec = pl.BlockSpec((1, GROUP), lambda k: (0, 0))
    side = _Side(
        (dcat, o, proj, gate_norm_w), [tile, tile, pl.BlockSpec((None, tt, GROUP), lambda k: (3, k, 0)), vec],
        [jax.ShapeDtypeStruct((t, HGRN_WIDTH), F32), jax.ShapeDtypeStruct((t, HGRN_WIDTH), BF16),
         jax.ShapeDtypeStruct((1, HGRN_WIDTH), F32)], [tile, tile, vec], [], init, begin)
    return _grad_w("dw_ff1", (h1b, da), (D_MODEL, D_FF), (D_MODEL, D_FF), step, after, side)


def _dw_ff2(r, dpre2b, dcat, bcu, conv_w):
    t = r.shape[0]
    tt = min(t, GRAD_TILE)
    hb = tt // SUBLANES
    halo = 2 * SUBLANES

    def step(acc, r_ref, d_ref, tick):
        dv = d_ref[...]
        for j in range(N_FF):
            rows = slice(j * FF_BLOCK, (j + 1) * FF_BLOCK)
            acc[rows, :] += _dot(r_ref[:, rows], dv, TN)
            tick(j)

    def init(outs):
        outs[1][...] = jnp.zeros_like(outs[1])

    def begin(k, n_steps, ins, outs, scratch):
        dy_ref, dyn_ref, b_ref, bn_ref, c_ref, u_ref, ch_ref, uh_ref, cw_ref = ins
        dp_ref, dcw_ref = outs
        zbuf, dbuf = scratch
        before = lambda ref: ref[SUBLANES:halo, :].astype(F32)
        zbuf[0:SUBLANES, :] = jnp.where(k > 0, before(ch_ref) * before(uh_ref), 0.0)
        zbuf[SUBLANES:SUBLANES + tt, :] = c_ref[...].astype(F32) * u_ref[...].astype(F32)
        dbuf[0:tt, :] = dy_ref[...] * b_ref[...].astype(F32)
        dbuf[tt:tt + SUBLANES, :] = jnp.where(k < n_steps - 1, dyn_ref[...] * bn_ref[0:SUBLANES, :].astype(F32), 0.0)
        totals = [jnp.zeros((GATE_STRIP, GROUP), F32) for _ in range(3)]

        def tick(j):
            cw = cw_ref[...]
            for rows in _strips_of(j, tt):
                at = lambda buf, shift: buf[shift + rows.start:shift + rows.stop, :]
                z, z1, z2 = at(zbuf, SUBLANES), at(zbuf, SUBLANES - 1), at(zbuf, SUBLANES - 2)
                dyc, d1, d2 = at(dbuf, 0), at(dbuf, 1), at(dbuf, 2)
                yc = cw[2:3, :] * z + cw[1:2, :] * z1 + cw[0:1, :] * z2
                dz = cw[2:3, :] * dyc + cw[1:2, :] * d1 + cw[0:1, :] * d2
                dp_ref[0, rows, :] = (dy_ref[rows, :] * yc).astype(dp_ref.dtype)
                dp_ref[1, rows, :] = (dz * u_ref[rows, :].astype(F32)).astype(dp_ref.dtype)
                dp_ref[2, rows, :] = (dz * c_ref[rows, :].astype(F32)).astype(dp_ref.dtype)
                for n, tap in enumerate((z2, z1, z)):
                    totals[n] = totals[n] + dyc * tap
            if j == N_FF - 1:
                for n in range(3):
                    dcw_ref[n:n + 1, :] += jnp.sum(totals[n], axis=0, keepdims=True)

        return tick

    grp = lambda g: pl.BlockSpec((None, tt, GROUP), lambda k: (g, k, 0))
    prev = lambda g: pl.BlockSpec((None, halo, GROUP), lambda k: (g, jnp.maximum(k * (tt // halo) - 1, 0), 0))
    nxt = lambda g: pl.BlockSpec((None, halo, GROUP), lambda k: (g, jnp.minimum((k + 1) * (tt // halo), t // halo - 1), 0))
    nxt_row = lambda k: jnp.minimum((k + 1) * hb, t // SUBLANES - 1)
    whole = pl.BlockSpec((3, CONV_WIDTH), lambda k: (0, 0))
    side = _Side(
        (dcat, dcat, bcu, bcu, bcu, bcu, bcu, bcu, conv_w),
        [pl.BlockSpec((tt, GROUP), lambda k: (k, 1)), pl.BlockSpec((SUBLANES, GROUP), lambda k: (nxt_row(k), 1)),
         grp(0), nxt(0), grp(1), grp(2), prev(1), prev(2), whole],
        [jax.ShapeDtypeStruct((3, t, CONV_WIDTH), BF16), jax.ShapeDtypeStruct((3, CONV_WIDTH), F32)],
        [pl.BlockSpec((3, tt, GROUP), lambda k: (0, k, 0)), whole],
        [pltpu.VMEM((tt + SUBLANES, GROUP), F32), pltpu.VMEM((tt + SUBLANES, GROUP), F32)], init, begin)
    return _grad_w("dw_ff2", (r, dpre2b), (D_FF, D_MODEL), (D_FF, D_MODEL), step, side=side)


def _place():
    x, y, c = lax.axis_index("x"), lax.axis_index("y"), lax.axis_index("c")
    return x, y, c, 2 * x + y


def _other_chips(x, y):
    return [(1 - x, y), (x, 1 - y), (1 - x, 1 - y)]


def _place_shard(name, w, chip, cols_sharded, after=None):
    rows, cols = w.shape
    tr = min(rows, 256)
    nb = rows // tr
    full = (rows, cols * N_CHIPS) if cols_sharded else (rows * N_CHIPS, cols)
    out_map = (lambda i, s: (i, s[0])) if cols_sharded else (lambda i, s: (s[0] * nb + i, 0))

    def body(s_ref, w_ref, *rest):
        rest[-1][...] = w_ref[...].astype(rest[-1].dtype)

    extra = [] if after is None else [after]
    return pl.pallas_call(
        body, name=name,
        grid_spec=pltpu.PrefetchScalarGridSpec(
            num_scalar_prefetch=1, grid=(nb,),
            in_specs=[pl.BlockSpec((tr, cols), lambda i, s: (i, 0))] + [ANY] * len(extra),
            out_specs=pl.BlockSpec((tr, cols), out_map)),
        out_shape=jax.ShapeDtypeStruct(full, BF16),
        compiler_params=_params("parallel"),
    )(chip, w, *extra)


HBM = pl.BlockSpec(memory_space=pltpu.HBM)
SEM = pl.BlockSpec(memory_space=pltpu.SEMAPHORE)
EFFECT = pltpu.SideEffectType.DATAFLOW_SIDE_EFFECTING


PEER_SETS = {
    "sibling": (0, lambda x, y, c: [(x, y, 1 - c)]),
    "chips": (1, lambda x, y, c: [(1 - x, y, c), (x, 1 - y, c), (1 - x, 1 - y, c)]),
    "neighbours": (2, lambda x, y, c: [(1 - x, y, c), (x, 1 - y, c)]),
}


class _Split:
    def __init__(self, name, arrays, plan, others=(), peers=None):
        n_own, arrays = len(arrays), (*arrays, *others)
        n, n_copies = len(arrays), plan.count
        self.name, self.plan, self.n = name, plan, n_own
        barrier_id, peer_ids = PEER_SETS[peers] if peers else (None, None)

        def body(*refs):
            if peers:
                x, y, c, _ = _place()
                barrier = pltpu.get_barrier_semaphore()
                for peer in peer_ids(x, y, c):
                    pl.semaphore_signal(barrier, inc=1, device_id=peer, device_id_type=MESH)
                pl.semaphore_wait(barrier, len(peer_ids(0, 0, 0)))
            send_sems, recv_sems, token = refs[n], refs[n + 1], refs[-1]
            for k, (src, dst, to) in enumerate(plan(refs[:n])):
                pltpu.make_async_remote_copy(src_ref=src, dst_ref=dst, send_sem=send_sems.at[k], recv_sem=recv_sems.at[k],
                                             device_id=to, device_id_type=MESH).start()
            token[...] = jnp.zeros_like(token)

        outs = pl.pallas_call(
            body, name=name + "_start",
            out_shape=(pltpu.SemaphoreType.DMA((n_copies,)), pltpu.SemaphoreType.DMA((n_copies,)),
                       *[pltpu.HBM(a.shape, a.dtype) for a in arrays], jax.ShapeDtypeStruct((SUBLANES, LANES), F32)),
            in_specs=(HBM,) * n, out_specs=(SEM, SEM) + (HBM,) * n + (pl.BlockSpec(memory_space=pltpu.VMEM),),
            input_output_aliases={i: 2 + i for i in range(n)},
            compiler_params=pltpu.CompilerParams(has_side_effects=EFFECT, collective_id=barrier_id),
        )(*[pltpu.with_memory_space_constraint(a, pltpu.HBM) for a in arrays])
        self.sems, self.arrays, self.others, self.token = outs[:2], outs[2:2 + n_own], outs[2 + n_own:2 + n], outs[-1]

    def wait(self, after):
        n, plan = self.n, self.plan

        def body(*refs):
            send_sems, recv_sems = refs[n], refs[n + 1]
            for k, (src, dst, to) in enumerate(plan(refs[:n])):
                cp = pltpu.make_async_remote_copy(src_ref=src, dst_ref=dst, send_sem=send_sems.at[k],
                                                  recv_sem=recv_sems.at[k], device_id=to, device_id_type=MESH)
                cp.wait_send()
                cp.wait_recv()

        return pl.pallas_call(
            body, name=self.name + "_wait", out_shape=tuple(pltpu.HBM(a.shape, a.dtype) for a in self.arrays),
            in_specs=(HBM,) * n + (SEM, SEM, ANY), out_specs=(HBM,) * n, input_output_aliases={i: i for i in range(n)},
            compiler_params=pltpu.CompilerParams(has_side_effects=EFFECT),
        )(*self.arrays, *self.sems, after)


COLS_SHARDED = (True, False, True, False)
HALF_SHAPES = [(D_MODEL // 2, IN_COLS), (D_MODEL, D_MODEL // 2), (D_MODEL // 2, D_FF), (D_FF, D_MODEL // 2)]
PIECE_SHAPES = [(D_MODEL // 2, IN_COLS // N_CHIPS), (D_MODEL // N_CHIPS, D_MODEL // 2),
                (D_MODEL // 2, D_FF // N_CHIPS), (D_FF // N_CHIPS, D_MODEL // 2)]


def _shard_view(kind, ref, chip):
    if COLS_SHARDED[kind]:
        n = ref.shape[1] // N_CHIPS
        return ref.at[:, pl.ds(chip * n, n)]
    n = ref.shape[0] // N_CHIPS
    return ref.at[pl.ds(chip * n, n), :]


def _half_view(kind, ref, h):
    if COLS_SHARDED[kind]:
        n = ref.shape[0] // 2
        return ref.at[pl.ds(h * n, n), :]
    n = ref.shape[1] // 2
    return ref.at[:, pl.ds(h * n, n)]


def _plan(count):
    def mark(fn):
        fn.count = count
        return fn
    return mark


def _shard_rows_view(kind, ref, chip, part, n_parts):
    if COLS_SHARDED[kind]:
        m, n = ref.shape[0] // n_parts, ref.shape[1] // N_CHIPS
        return ref.at[pl.ds(part * m, m), pl.ds(chip * n, n)]
    m = ref.shape[0] // N_CHIPS // n_parts
    return ref.at[pl.ds((n_parts * chip + part) * m, m), :]


def _shard_half_view(kind, ref, chip, h):
    return _shard_rows_view(kind, ref, chip, h, 2)


def _gather_over_ici(kinds, weights):
    @_plan(2 * len(kinds))
    def plan(refs):
        x, y, c, me = _place()
        mine = [_shard_half_view(kind, ref, me, c) for kind, ref in zip(kinds, refs)]
        return [(v, v, to) for v in mine for to in ((1 - x, y, c), (x, 1 - y, c))]

    return _Split("gather_ici_" + "".join(map(str, kinds)), tuple(weights), plan, peers="neighbours")


def _relay_over_ici(kinds, weights, others=()):
    @_plan(2 * len(kinds))
    def plan(refs):
        x, y, c, _ = _place()
        x_nbr, y_nbr = 2 * (1 - x) + y, 2 * x + (1 - y)
        out = []
        for kind, ref in zip(kinds, refs):
            first, second = (_shard_rows_view(kind, ref, chip, 2 * c + q, 4) for q, chip in ((0, x_nbr), (1, y_nbr)))
            out += [(first, first, (x, 1 - y, c)), (second, second, (1 - x, y, c))]
        return out

    return _Split("relay_ici_" + "".join(map(str, kinds)), tuple(weights), plan, others, peers="neighbours")


def _gather_w_in_over_ici(w_in, conv4):
    @_plan(6)
    def plan(refs):
        x, y, c, me = _place()
        half, conv = _shard_half_view(0, refs[0], me, c), refs[1].at[me]
        return [(v, v, (px, py, c)) for v in (half, conv) for px, py in _other_chips(x, y)]

    return _Split("gather_w_in_ici", (w_in, conv4), plan, peers="chips")


def _gather_over_d2d(kinds, weights):
    @_plan(3 * len(kinds))
    def plan(refs):
        x, y, c, _ = _place()
        got = [_shard_half_view(kind, ref, 2 * px + py, c) for kind, ref in zip(kinds, refs)
               for px, py in _other_chips(x, y)]
        return [(v, v, (x, y, 1 - c)) for v in got]

    return _Split("gather_d2d_" + "".join(map(str, kinds)), tuple(weights), plan, peers="sibling")


def _swap_halves(kinds, grads):
    @_plan(len(kinds))
    def plan(refs):
        x, y, c, _ = _place()
        return [(_half_view(kind, g, 1 - c), land, (x, y, 1 - c))
                for kind, g, land in zip(kinds, refs[:len(kinds)], refs[len(kinds):])]

    lands = [lax.empty(HALF_SHAPES[kind], g.dtype) for kind, g in zip(kinds, grads)]
    return _Split("swap_halves_" + "".join(map(str, kinds)), (*grads, *lands), plan, peers="sibling")


def _block_rows(cols, elements):
    return 1 << ((elements // cols).bit_length() - 1)


def _add_half(name, g, recv, core, rows_split):
    shape = recv.shape
    tr = min(shape[0], _block_rows(shape[1], 1 << 20))
    nb = shape[0] // tr

    def body(c_ref, g_ref, r_ref, o_ref):
        o_ref[...] = (g_ref[...].astype(F32) + r_ref[...].astype(F32)).astype(o_ref.dtype)

    g_map = (lambda i, c_ref: (c_ref[0] * nb + i, 0)) if rows_split else (lambda i, c_ref: (i, c_ref[0]))
    blk = pl.BlockSpec((tr, shape[1]), lambda i, c_ref: (i, 0))
    return pl.pallas_call(
        body, name=name,
        grid_spec=pltpu.PrefetchScalarGridSpec(
            num_scalar_prefetch=1, grid=(nb,),
            in_specs=[pl.BlockSpec((tr, shape[1]), g_map), blk], out_specs=blk),
        out_shape=jax.ShapeDtypeStruct(shape, BF16),
        compiler_params=_params("parallel"),
    )(core, g, recv)


def _exchange_pieces(kinds, halves, pack=None):
    n_p, n = N_CHIPS - 1, len(kinds)

    @_plan(n_p * n + (0 if pack is None else N_DEV - 1))
    def plan(refs):
        x, y, c, _ = _place()
        copies = []
        if pack is not None:
            me = 4 * x + 2 * y + c
            peers = [((1 - x) if m & 4 else x, (1 - y) if m & 2 else y, (1 - c) if m & 1 else c) for m in range(1, N_DEV)]
            copies += [(refs[2 * n], refs[2 * n + 1].at[me], peer) for peer in peers]
        return copies + [(_shard_view(kind, half, 2 * px + py), land.at[j], (px, py, c))
                         for j, (px, py) in enumerate(_other_chips(x, y))
                         for kind, half, land in zip(kinds, refs[:n], refs[n:2 * n])]

    lands = [lax.empty((n_p,) + PIECE_SHAPES[kind], BF16) for kind in kinds]
    small = () if pack is None else (pack, lax.empty((N_DEV,) + pack.shape, F32))
    return _Split("exchange_pieces_" + "".join(map(str, kinds)), (*halves, *lands, *small), plan,
                  peers="chips" if pack is None else None)


def _sum_pieces(name, half, slots, place, rows_split, after):
    n_p, rows, cols = slots.shape
    tr = min(rows, _block_rows(cols, 1 << 19))
    nb = rows // tr
    if rows_split:
        own_map = lambda i, s: (i, s[0])
        out_map = lambda i, s: (s[1] * nb + i, 0)
        shard = (2 * rows, cols)
    else:
        own_map = lambda i, s: (s[0] * nb + i, 0)
        out_map = lambda i, s: (i, s[1])
        shard = (rows, 2 * cols)

    def body(s_ref, own_ref, slot_ref, after_ref, o_ref):
        total = own_ref[...].astype(F32)
        for j in range(n_p):
            total = total + slot_ref[j].astype(F32)
        o_ref[...] = total

    return pl.pallas_call(
        body, name=name,
        grid_spec=pltpu.PrefetchScalarGridSpec(
            num_scalar_prefetch=1, grid=(nb,),
            in_specs=[pl.BlockSpec((tr, cols), own_map), pl.BlockSpec((n_p, tr, cols), lambda i, s: (0, i, 0)), ANY],
            out_specs=pl.BlockSpec((tr, cols), out_map)),
        out_shape=jax.ShapeDtypeStruct(shard, F32),
        compiler_params=_params("parallel"),
    )(place, half, slots, after)


def _join_halves(kinds, shards):
    @_plan(len(kinds))
    def plan(refs):
        x, y, c, _ = _place()
        return [(_half_view(kind, g, c), _half_view(kind, g, c), (x, y, 1 - c)) for kind, g in zip(kinds, refs)]

    return _Split("join_halves_" + "".join(map(str, kinds)), tuple(shards), plan, peers="sibling")


N_DEV = 8


def _sum_shared(pack, land, device):
    def body(d_ref, p_ref, l_ref, o_ref):
        me = d_ref[0]
        total = jnp.where(me == 0, p_ref[...], l_ref[0])
        for d in range(1, N_DEV):
            total = total + jnp.where(me == d, p_ref[...], l_ref[d])
        o_ref[...] = total

    return pl.pallas_call(
        body, name="sum_shared",
        grid_spec=pltpu.PrefetchScalarGridSpec(
            num_scalar_prefetch=1, grid=(1,),
            in_specs=[pl.BlockSpec(pack.shape, lambda i, d: (0, 0)), pl.BlockSpec(land.shape, lambda i, d: (0, 0, 0))],
            out_specs=pl.BlockSpec(pack.shape, lambda i, d: (0, 0))),
        out_shape=jax.ShapeDtypeStruct(pack.shape, F32),
    )(device, pack, land)


def _adamw(name, w, g, m, v, after=None):
    rows, cols = w.shape
    tr = min(rows, 256)
    extra = [] if after is None else [after]

    def body(w_ref, g_ref, m_ref, v_ref, *rest):
        go_ref, d_ref, nm_ref, nv_ref = rest[-4:]
        g = g_ref[...]
        go_ref[...] = g
        d_ref[...], nm_ref[...], nv_ref[...] = _adam_step(w_ref[...], g, m_ref[...], v_ref[...])

    blk = pl.BlockSpec((tr, cols), lambda i: (i, 0))
    return pl.pallas_call(
        body, name=name, grid=(rows // tr,), in_specs=[blk] * 4 + [ANY] * len(extra), out_specs=[blk] * 4,
        out_shape=[jax.ShapeDtypeStruct(w.shape, F32)] * 4,
        compiler_params=_params("parallel"),
    )(w, g, m, v, *extra)


def _adam_step(w, g, m, v):
    nm = ADAM_B1 * m + (1.0 - ADAM_B1) * g
    nv = ADAM_B2 * v + (1.0 - ADAM_B2) * jnp.square(g)
    m_hat = nm * (1.0 / (1.0 - ADAM_B1 ** ADAM_STEP))
    v_hat = nv * (1.0 / (1.0 - ADAM_B2 ** ADAM_STEP))
    return -ADAM_LR * (m_hat / (jnp.sqrt(v_hat) + ADAM_EPS) + ADAM_WD * w), nm, nv


def _adamw_small(tot, chip, weights, ms, vs, after):
    n, half = len(weights), D_MODEL // 2

    def body(chip_ref, tot_ref, *refs):
        ins, outs = refs[:3 * n], refs[3 * n + 1:]
        tot = tot_ref[...]
        conv_all = jnp.concatenate([tot[5:6, half:], tot[6:7, :half], tot[6:7, half:]], axis=0)
        conv = sum(jnp.where(chip_ref[0] == s, conv_all[:, s * LANES:(s + 1) * LANES], 0.0) for s in range(N_CHIPS))
        grads = [jnp.concatenate([tot[4:5, :half], tot[4:5, half:]], axis=0), tot[5:6, :half], conv,
                 tot[0:1], tot[1:2], tot[2:3], tot[3:4]]
        for k, g in enumerate(grads):
            delta, nm, nv = _adam_step(ins[k][...], g, ins[n + k][...], ins[2 * n + k][...])
            outs[k][...], outs[n + k][...], outs[2 * n + k][...], outs[3 * n + k][...] = g, delta, nm, nv
        outs[4 * n][...] = tot[7:8, 0:1]

    whole = lambda a: pl.BlockSpec(a.shape, lambda i, s: (0,) * a.ndim)
    arrays = (*weights, *ms, *vs)
    loss = jax.ShapeDtypeStruct((1, 1), F32)
    return pl.pallas_call(
        body, name="adamw_small",
        grid_spec=pltpu.PrefetchScalarGridSpec(
            num_scalar_prefetch=1, grid=(1,), in_specs=[whole(tot)] + [whole(a) for a in arrays] + [ANY],
            out_specs=[whole(a) for a in weights] * 4 + [whole(loss)]),
        out_shape=[jax.ShapeDtypeStruct(a.shape, F32) for a in weights] * 4 + [loss],
    )(chip, tot, *arrays, after)


def kernel(x, w_in, lb_logits, gate_norm_w, conv_w, w_out, ln1_g, ln1_b, w_ff1, w_ff2, ln2_g, ln2_b, loss_target, m_w_in, m_lb_logits, m_gate_norm_w, m_conv_w, m_w_out, m_ln1_g, m_ln1_b, m_w_ff1, m_w_ff2, m_ln2_g, m_ln2_b, v_w_in, v_lb_logits, v_gate_norm_w, v_conv_w, v_w_out, v_ln1_g, v_ln1_b, v_w_ff1, v_w_ff2, v_ln2_g, v_ln2_b):
    xs, tgt = x[0], loss_target[0]
    chip = 2 * lax.axis_index("x") + lax.axis_index("y")
    core = lax.axis_index("c").astype(jnp.int32).reshape(1)
    chip1 = chip.astype(jnp.int32).reshape(1)
    place = jnp.concatenate([chip1, core])

    conv4 = lax.dynamic_update_slice(jnp.zeros((N_CHIPS,) + conv_w.shape[1:], F32), conv_w, (chip, 0, 0))
    ici_in = _gather_w_in_over_ici(_place_shard("place_w_in", w_in[0], chip1, True), conv4)
    rest = (1, 2, 3)
    ici_rest = _gather_over_ici(rest, (_place_shard("place_w_out", w_out[0], chip1, False, after=ici_in.token),
                                       _place_shard("place_w_ff1", w_ff1[0], chip1, True, after=ici_in.token),
                                       _place_shard("place_w_ff2", w_ff2[0], chip1, False, after=ici_in.token)))
    wb_in, cv4 = ici_in.wait(ici_rest.token)
    d2d_in = _gather_over_d2d((0,), (wb_in,))
    wb_in, = d2d_in.wait(d2d_in.token)
    conv_full = cv4.transpose(1, 0, 2).reshape(3, CONV_WIDTH)

    proj, bcu, xb, cat_c = _in_proj(xs, wb_in, conv_full, ici_rest.token)
    relay_rest = _relay_over_ici(rest, ici_rest.wait(proj))
    o, states = _hgrn_fwd(proj, lb_logits, relay_rest.token)
    d2d_rest = _gather_over_d2d(rest, relay_rest.wait(o))
    cat_h = _gate_fwd(proj, o, gate_norm_w, d2d_rest.token)
    wb_out, wb_ff1, wb_ff2 = d2d_rest.wait(cat_h)

    (h1b, r, da, dpre2b, dpre1, dpre1b, dcat, g_ln1_g, g_ln1_b, g_ln2_g, g_ln2_b, loss8) = _sublayers(
        cat_h, cat_c, xs, tgt, wb_out, wb_ff1, wb_ff2, ln1_g, ln1_b, ln2_g, ln2_b)

    names = ("w_in", "w_out", "w_ff1", "w_ff2")

    def add_halves(kinds, grads, lands):
        return [_add_half("add_half_" + names[k], g, ld, core, COLS_SHARDED[k]) for k, g, ld in zip(kinds, grads, lands)]

    def sum_pieces(kinds, halves, lands, after):
        return [_sum_pieces("sum_pieces_" + names[k], h, ld, place, COLS_SHARDED[k], after)
                for k, h, ld in zip(kinds, halves, lands)]

    early = (1, 2, 3)
    g_out_local = _dw_out(cat_h, cat_c, dpre1b)
    g_ff2_local, dpc, g_conv = _dw_ff2(r, dpre2b, dcat, bcu, conv_full)
    swap_a = _swap_halves((1, 3), (g_out_local, g_ff2_local))
    g_ff1_local, do, dog, g_gnw = _dw_ff1(h1b, da, dcat, o, proj, gate_norm_w, swap_a.token)
    swap_b = _swap_halves((2,), (g_ff1_local,))
    swapped_a = swap_a.wait(swap_b.token)
    halves_a = add_halves((1, 3), swapped_a[:2], swapped_a[2:])
    swapped_b = swap_b.wait(halves_a[1])
    halves = (halves_a[0], *add_halves((2,), swapped_b[:1], swapped_b[1:]), halves_a[1])
    exch = _exchange_pieces(early, halves)
    dph, g_lbl = _hgrn_bwd(proj, do, states, lb_logits, exch.token)
    g_in_local = _dw_in(xb, dph, dog, dpc, dph)

    late = (0,)
    swap = _swap_halves(late, (g_in_local,))
    grad_x = _in_bwd(dph, dog, dpc, wb_in, dpre1, swap.token)
    exchanged = exch.wait(grad_x)
    pack = jnp.concatenate([
        g_ln1_g, g_ln1_b, g_ln2_g, g_ln2_b,
        jnp.concatenate([g_lbl[0:1], g_lbl[1:2]], axis=1),
        jnp.concatenate([g_gnw, g_conv[0:1]], axis=1),
        jnp.concatenate([g_conv[1:2], g_conv[2:3]], axis=1),
        jnp.concatenate([loss8[0:1], jnp.zeros((1, D_MODEL - LANES), F32)], axis=1)], axis=0)
    swapped = swap.wait(exchanged[0])
    exch = _exchange_pieces(late, add_halves(late, swapped[:1], swapped[1:]), pack)
    join_a = _join_halves((2,), sum_pieces((2,), exchanged[1:2], exchanged[4:5], exch.token))
    join_b = _join_halves((1, 3), sum_pieces((1, 3), exchanged[0:3:2], exchanged[3:6:2], join_a.token))
    g_w_ff1, = join_a.wait(join_b.token)
    g_w_ff1, d_ff1, nm_ff1, nv_ff1 = _adamw("adamw_w_ff1", w_ff1[0], g_w_ff1, m_w_ff1[0], v_w_ff1[0])
    g_w_out, g_w_ff2 = join_b.wait(d_ff1)
    g_w_ff2, d_ff2, nm_ff2, nv_ff2 = _adamw("adamw_w_ff2", w_ff2[0], g_w_ff2, m_w_ff2[0], v_w_ff2[0])
    g_w_out, d_out, nm_out, nv_out = _adamw("adamw_w_out", w_out[0], g_w_out, m_w_out[0], v_w_out[0], d_ff2)
    exchanged = exch.wait(d_out)
    tot = _sum_shared(exchanged[2], exchanged[3], 2 * chip1 + core)
    join = _join_halves(late, sum_pieces(late, exchanged[:1], exchanged[1:2], tot))
    small = ("lb_logits", "gate_norm_w", "conv_w", "ln1_g", "ln1_b", "ln2_g", "ln2_b")
    small_out = _adamw_small(
        tot, chip1, (lb_logits, gate_norm_w, conv_w[0], ln1_g, ln1_b, ln2_g, ln2_b),
        (m_lb_logits, m_gate_norm_w, m_conv_w[0], m_ln1_g, m_ln1_b, m_ln2_g, m_ln2_b),
        (v_lb_logits, v_gate_norm_w, v_conv_w[0], v_ln1_g, v_ln1_b, v_ln2_g, v_ln2_b), join.token)
    g_w_in, = join.wait(small_out[0])
    g_w_in, d_in, nm_in, nv_in = _adamw("adamw_w_in", w_in[0], g_w_in, m_w_in[0], v_w_in[0])
    loss = small_out[4 * len(small)][0, 0]

    def results(n_kind, large):
        out = dict(zip(small, small_out[n_kind * len(small):(n_kind + 1) * len(small)]))
        out["conv_w"] = out["conv_w"][None]
        out.update({name: a[None] for name, a in zip(("w_in", "w_out", "w_ff1", "w_ff2"), large)})
        return [out[name] for name in ("w_in", "lb_logits", "gate_norm_w", "conv_w", "w_out", "ln1_g", "ln1_b",
                                       "w_ff1", "w_ff2", "ln2_g", "ln2_b")]

    return (loss, grad_x[None], *results(0, (g_w_in, g_w_out, g_w_ff1, g_w_ff2)),
            *results(1, (d_in, d_out, d_ff1, d_ff2)), *results(2, (nm_in, nm_out, nm_ff1, nm_ff2)),
            *results(3, (nv_in, nv_out, nv_ff1, nv_ff2)))
```

```python
import jax
import jax.numpy as jnp
from jax import lax
from jax.experimental import pallas as pl
from jax.experimental.pallas import tpu as pltpu

F32 = jnp.float32
BF16 = jnp.bfloat16
MXU_DTYPE = jnp.bfloat16

D_MODEL = 1024
HGRN_WIDTH = 512
HEAD_DIM = 128
N_HEADS = 4
CONV_WIDTH = 512
CHUNK = 64
D_FF = 4096
IN_COLS = 3584
GROUP = 512
N_GROUPS = IN_COLS // GROUP
ALPHA = 2.0 ** 0.25
EPS = 1e-5
N_CHIPS = 4
ADAM_LR, ADAM_B1, ADAM_B2, ADAM_EPS, ADAM_WD, ADAM_STEP = 0.001, 0.9, 0.999, 1e-08, 0.01, 10

LANES = 128
SUBLANES = 8
VMEM_LIMIT = 56 * 1024 * 1024
FF_BLOCK = 1024
N_FF = D_FF // FF_BLOCK
GATE_STRIP = 64

NN = (((1,), (0,)), ((), ()))
NT = (((1,), (1,)), ((), ()))
TN = (((0,), (0,)), ((), ()))
MESH = pl.DeviceIdType.MESH
ANY = pl.BlockSpec(memory_space=pl.ANY)


def _dot(a, b, dims):
    return lax.dot_general(a.astype(MXU_DTYPE), b.astype(MXU_DTYPE), dims, preferred_element_type=F32)


def _dot_exact(ones, v):
    ones = ones.astype(jnp.bfloat16)
    hi = v.astype(jnp.bfloat16)
    rest = v - hi.astype(F32)
    mid = rest.astype(jnp.bfloat16)
    low = (rest - mid.astype(F32)).astype(jnp.bfloat16)
    return sum(lax.dot_general(ones, part, NN, preferred_element_type=F32) for part in (hi, mid, low))


def _params(*sem):
    return pltpu.CompilerParams(dimension_semantics=sem, vmem_limit_bytes=VMEM_LIMIT)


def _resident(shape):
    return pl.BlockSpec(shape, lambda *_: (0,) * len(shape), pipeline_mode=pl.Buffered(1))


def _sigmoid(v):
    return 1.0 / (1.0 + jnp.exp(-v))


def _lower_bound(lbl):
    m = jnp.max(lbl, axis=0, keepdims=True)
    e = jnp.exp(lbl - m)
    s = e / jnp.sum(e, axis=0, keepdims=True)
    return s[0:1, :], s[1:2, :]


def _heads(v):
    return [v[:, h * HEAD_DIM:(h + 1) * HEAD_DIM] for h in range(N_HEADS)]


def _per_head(fn, *arrays):
    return jnp.concatenate([fn(*parts) for parts in zip(*map(_heads, arrays))], axis=1)


def _in_proj(x, w_in, conv_w, after):
    t = x.shape[0]
    tm = min(t, 512)

    def body(x_ref, w_ref, cw_ref, after_ref, o_ref, bcu_ref, xb_ref, y_ref, zbuf):
        @pl.when(pl.program_id(0) == 0)
        def _():
            zbuf[tm:tm + SUBLANES, :] = jnp.zeros((SUBLANES, CONV_WIDTH), F32)

        xb = x_ref[...].astype(xb_ref.dtype)
        xb_ref[...] = xb
        group = lambda g: _dot(xb, w_ref[:, g * GROUP:(g + 1) * GROUP], NN)
        for g in range(4):
            o_ref[g] = group(g)
        b_gate, c_gate, u = group(4), group(5), group(6)
        for n, part in enumerate((b_gate, c_gate, u)):
            bcu_ref[n] = part.astype(bcu_ref.dtype)
        zbuf[0:SUBLANES, :] = zbuf[tm:tm + SUBLANES, :]
        zbuf[SUBLANES:SUBLANES + tm, :] = c_gate * u
        cw = cw_ref[...]
        at = lambda shift: zbuf[shift:shift + tm, :]
        conv = cw[2:3, :] * at(SUBLANES) + cw[1:2, :] * at(SUBLANES - 1) + cw[0:1, :] * at(SUBLANES - 2)
        y_ref[...] = (b_gate * conv).astype(y_ref.dtype)

    return pl.pallas_call(
        body, name="in_proj", grid=(t // tm,),
        in_specs=[pl.BlockSpec((tm, D_MODEL), lambda i: (i, 0)), _resident((D_MODEL, IN_COLS)),
                  pl.BlockSpec((3, CONV_WIDTH), lambda i: (0, 0)), ANY],
        out_specs=[pl.BlockSpec((4, tm, GROUP), lambda i: (0, i, 0)), pl.BlockSpec((3, tm, GROUP), lambda i: (0, i, 0)),
                   pl.BlockSpec((tm, D_MODEL), lambda i: (i, 0)), pl.BlockSpec((tm, CONV_WIDTH), lambda i: (i, 0))],
        out_shape=[jax.ShapeDtypeStruct((4, t, GROUP), F32), jax.ShapeDtypeStruct((3, t, GROUP), BF16),
                   jax.ShapeDtypeStruct((t, D_MODEL), BF16), jax.ShapeDtypeStruct((t, CONV_WIDTH), BF16)],
        scratch_shapes=[pltpu.VMEM((tm + SUBLANES, CONV_WIDTH), F32)],
        compiler_params=_params("arbitrary"),
    )(x, w_in, conv_w, after)


def _gates(fp, lb):
    sig = _sigmoid(fp)
    f = lb + (1.0 - lb) * sig
    return sig, f, jnp.log(f), 1.0 - f


def _chunk_masks():
    row = lax.broadcasted_iota(jnp.int32, (CHUNK, CHUNK), 0)
    col = lax.broadcasted_iota(jnp.int32, (CHUNK, CHUNK), 1)
    return row >= col, row <= col


def _hgrn_fwd(proj, lb_logits, after):
    t = proj.shape[1]
    tb = min(t, 512)
    ncb = tb // CHUNK

    def body(q_ref, f_ref, v_ref, lbl_ref, after_ref, o_ref, st_ref, s_scr):
        @pl.when(pl.program_id(0) == 0)
        def _():
            s_scr[...] = jnp.zeros_like(s_scr)

        lb, _ = _lower_bound(lbl_ref[...])
        causal, _ = _chunk_masks()

        every = range(ncb)
        rows = [slice(c * CHUNK, (c + 1) * CHUNK) for c in every]
        q, v = [q_ref[r, :] for r in rows], [v_ref[r, :] for r in rows]
        gates = [_gates(f_ref[r, :], lb) for r in rows]
        k = [gt[3] for gt in gates]
        b = [_dot_exact(causal, gt[2]) for gt in gates]
        mid, last = [x[CHUNK // 2:CHUNK // 2 + 1, :] for x in b], [x[CHUNK - 1:CHUNK, :] for x in b]
        qt = [q[c] * jnp.exp(b[c] - mid[c]) for c in every]
        kt = [k[c] * jnp.exp(mid[c] - b[c]) for c in every]
        qi = [q[c] * jnp.exp(b[c]) for c in every]
        ks = [k[c] * jnp.exp(last[c] - b[c]) for c in every]
        dec = [jnp.exp(x) for x in last]
        scores = [[jnp.where(causal, _dot(a, b_, NT), 0.0) for a, b_ in zip(_heads(qt[c]), _heads(kt[c]))] for c in every]
        intra = [[_dot(s, v_h, NN) for s, v_h in zip(scores[c], _heads(v[c]))] for c in every]
        update = [_per_head(lambda v_h, ks_h: _dot(v_h, ks_h, TN), v[c], ks[c]) for c in every]

        st = s_scr[...]
        states = []
        for c in every:
            states.append(st)
            st_ref[c] = st
            st = dec[c] * st + update[c]
        s_scr[...] = st

        o_ref[...] = jnp.concatenate(
            [jnp.concatenate([i_h + _dot(qi_h, st_h, NT) for i_h, qi_h, st_h in
                              zip(intra[c], _heads(qi[c]), _heads(states[c]))], axis=1) for c in every], axis=0)

    grp = lambda g: pl.BlockSpec((None, tb, GROUP), lambda i: (g, i, 0))
    return pl.pallas_call(
        body, name="hgrn_fwd", grid=(t // tb,),
        in_specs=[grp(0), grp(1), grp(2), pl.BlockSpec((2, HGRN_WIDTH), lambda i: (0, 0)), ANY],
        out_specs=[pl.BlockSpec((tb, HGRN_WIDTH), lambda i: (i, 0)),
                   pl.BlockSpec((ncb, HEAD_DIM, HGRN_WIDTH), lambda i: (i, 0, 0))],
        out_shape=[jax.ShapeDtypeStruct((t, HGRN_WIDTH), F32),
                   jax.ShapeDtypeStruct((t // CHUNK, HEAD_DIM, HGRN_WIDTH), F32)],
        scratch_shapes=[pltpu.VMEM((HEAD_DIM, HGRN_WIDTH), F32)],
        compiler_params=_params("arbitrary"),
    )(proj, proj, proj, lb_logits, after)


def _gate_fwd(proj, o, gate_norm_w, after):
    t = proj.shape[1]
    tb = min(t, 512)

    def body(o_ref, og_ref, gnw_ref, after_ref, out_ref):
        gnw = gnw_ref[...]
        for s in range(tb // GATE_STRIP):
            rows = slice(s * GATE_STRIP, (s + 1) * GATE_STRIP)
            og = og_ref[rows, :]
            on = _per_head(lambda o_h: o_h * lax.rsqrt(jnp.mean(o_h * o_h, axis=-1, keepdims=True) + EPS), o_ref[rows, :])
            out_ref[rows, :] = (on * gnw * (og * _sigmoid(og))).astype(out_ref.dtype)

    tile = pl.BlockSpec((tb, GROUP), lambda i: (i, 0))
    return pl.pallas_call(
        body, name="gate_fwd", grid=(t // tb,),
        in_specs=[tile, pl.BlockSpec((None, tb, GROUP), lambda i: (3, i, 0)), pl.BlockSpec((1, GROUP), lambda i: (0, 0)), ANY],
        out_specs=tile,
        out_shape=jax.ShapeDtypeStruct((t, HGRN_WIDTH), BF16),
        compiler_params=_params("parallel"),
    )(o, proj, gate_norm_w, after)


def _ln_bwd(dy, xhat, rstd, g):
    dxhat = dy * g
    m1 = jnp.mean(dxhat, axis=-1, keepdims=True)
    m2 = jnp.mean(dxhat * xhat, axis=-1, keepdims=True)
    return rstd * (dxhat - m1 - xhat * m2)


def _layer_norm(pre):
    xc = pre - jnp.mean(pre, axis=-1, keepdims=True)
    rstd = lax.rsqrt(jnp.mean(xc * xc, axis=-1, keepdims=True) + EPS)
    return xc * rstd, rstd


def _sublayers(cat_h, cat_c, x, target, w_out, w_ff1, w_ff2, g1, b1, g2, b2):
    t = x.shape[0]
    tm = min(t, 256)

    def body(ch_ref, cc_ref, x_ref, tg_ref, wo_ref, w1_ref, w2_ref, g1_ref, b1_ref, g2_ref, b2_ref,
             h1_ref, r_ref, da_ref, dp2b_ref, dp1_ref, dp1b_ref, dcat_ref, dg1_ref, db1_ref, dg2_ref, db2_ref, loss_ref):
        @pl.when(pl.program_id(0) == 0)
        def _():
            for ref in (dg1_ref, db1_ref, dg2_ref, db2_ref, loss_ref):
                ref[...] = jnp.zeros_like(ref)

        mix = _dot(ch_ref[...], wo_ref[0:GROUP, :], NN) + _dot(cc_ref[...], wo_ref[GROUP:2 * GROUP, :], NN)
        xhat1, rstd1 = _layer_norm(ALPHA * x_ref[...] + mix)
        h1 = xhat1 * g1_ref[...] + b1_ref[...]
        h1b = h1.astype(h1_ref.dtype)
        h1_ref[...] = h1b
        mlp = jnp.zeros((tm, D_MODEL), F32)
        for j in range(N_FF):
            cols = slice(j * FF_BLOCK, (j + 1) * FF_BLOCK)
            r = jnp.square(jnp.maximum(_dot(h1b, w1_ref[:, cols], NN), 0.0)).astype(r_ref.dtype)
            r_ref[:, cols] = r
            mlp = mlp + _dot(r, w2_ref[cols, :], NN)
        xhat2, rstd2 = _layer_norm(ALPHA * h1 + mlp)
        err = xhat2 * g2_ref[...] + b2_ref[...] - tg_ref[...]
        loss_ref[...] += 0.5 * jnp.sum(jnp.mean(err * err, axis=-1, keepdims=True))
        dy = err * (1.0 / D_MODEL)
        dg2_ref[...] += jnp.sum(dy * xhat2, axis=0, keepdims=True)
        db2_ref[...] += jnp.sum(dy, axis=0, keepdims=True)
        dp2 = _ln_bwd(dy, xhat2, rstd2, g2_ref[...])
        dp2b = dp2.astype(dp2b_ref.dtype)
        dp2b_ref[...] = dp2b
        back = jnp.zeros((tm, D_MODEL), F32)
        for j in range(N_FF):
            cols = slice(j * FF_BLOCK, (j + 1) * FF_BLOCK)
            dr = _dot(dp2b, w2_ref[cols, :], NT)
            da = (dr * (2.0 * jnp.sqrt(r_ref[:, cols].astype(F32)))).astype(da_ref.dtype)
            da_ref[:, cols] = da
            back = back + _dot(da, w1_ref[:, cols], NT)
        dh1 = ALPHA * dp2 + back
        dg1_ref[...] += jnp.sum(dh1 * xhat1, axis=0, keepdims=True)
        db1_ref[...] += jnp.sum(dh1, axis=0, keepdims=True)
        dp1 = _ln_bwd(dh1, xhat1, rstd1, g1_ref[...])
        dp1b = dp1.astype(dp1b_ref.dtype)
        dp1_ref[...] = dp1
        dp1b_ref[...] = dp1b
        dcat_ref[...] = _dot(dp1b, wo_ref[...], NT)

    row = pl.BlockSpec((tm, D_MODEL), lambda i: (i, 0))
    wide = pl.BlockSpec((tm, D_FF), lambda i: (i, 0))
    vec = pl.BlockSpec((1, D_MODEL), lambda i: (0, 0))
    narrow = lambda dtype: jax.ShapeDtypeStruct((t, D_MODEL), dtype)
    return pl.pallas_call(
        body, name="sublayers", grid=(t // tm,),
        in_specs=[pl.BlockSpec((tm, GROUP), lambda i: (i, 0)), pl.BlockSpec((tm, GROUP), lambda i: (i, 0)), row, row,
                  _resident((D_MODEL, D_MODEL)),
                  _resident((D_MODEL, D_FF)), _resident((D_FF, D_MODEL)), vec, vec, vec, vec],
        out_specs=[row, wide, wide, row, row, row, row, vec, vec, vec, vec,
                   pl.BlockSpec((SUBLANES, LANES), lambda i: (0, 0))],
        out_shape=[narrow(BF16), jax.ShapeDtypeStruct((t, D_FF), BF16), jax.ShapeDtypeStruct((t, D_FF), BF16),
                   narrow(BF16), narrow(F32), narrow(BF16), narrow(F32)]
                  + [jax.ShapeDtypeStruct((1, D_MODEL), F32)] * 4 + [jax.ShapeDtypeStruct((SUBLANES, LANES), F32)],
        compiler_params=_params("arbitrary"),
    )(cat_h, cat_c, x, target, w_out, w_ff1, w_ff2, g1, b1, g2, b2)


def _hgrn_bwd(proj, do, states, lb_logits, after):
    t = proj.shape[1]
    tb = min(t, 512)
    ncb = tb // CHUNK
    nblk = t // tb

    def body(q_ref, f_ref, v_ref, do_ref, st_ref, lbl_ref, after_ref, dp_ref, dlbl_ref, ds_scr, dlb_scr):
        i = pl.program_id(0)

        @pl.when(i == 0)
        def _():
            ds_scr[...] = jnp.zeros_like(ds_scr)
            dlb_scr[...] = jnp.zeros_like(dlb_scr)

        lb, s1 = _lower_bound(lbl_ref[...])
        causal, anti = _chunk_masks()
        every = range(ncb)
        rows = [slice(c * CHUNK, (c + 1) * CHUNK) for c in every]
        q, v, do = ([ref[r, :] for r in rows] for ref in (q_ref, v_ref, do_ref))
        st = [st_ref[c] for c in every]
        gates = [_gates(f_ref[r, :], lb) for r in rows]
        sig, f, k = ([gt[n] for gt in gates] for n in (0, 1, 3))
        b = [_dot_exact(causal, gt[2]) for gt in gates]
        mid, last = [x[CHUNK // 2:CHUNK // 2 + 1, :] for x in b], [x[CHUNK - 1:CHUNK, :] for x in b]
        e_q = [jnp.exp(b[c] - mid[c]) for c in every]
        e_k = [jnp.exp(mid[c] - b[c]) for c in every]
        e_i = [jnp.exp(x) for x in b]
        e_s = [jnp.exp(last[c] - b[c]) for c in every]
        dec = [jnp.exp(x) for x in last]
        qt, kt, qi, ks = ([a[c] * e[c] for c in every] for a, e in ((q, e_q), (k, e_k), (q, e_i), (k, e_s)))

        def masked(a, b_):
            return [[jnp.where(causal, _dot(a_h, b_h, NT), 0.0) for a_h, b_h in zip(_heads(a[c]), _heads(b_[c]))]
                    for c in every]

        def with_scores(s, other, dims):
            return [jnp.concatenate([_dot(s_h, o_h, dims) for s_h, o_h in zip(s[c], _heads(other[c]))], axis=1)
                    for c in every]

        def per_head(dims, a, b_):
            return [_per_head(lambda a_h, b_h: _dot(a_h, b_h, dims), a[c], b_[c]) for c in every]

        scores, dscores = masked(qt, kt), masked(do, v)
        dqt, dkt, dv_intra = with_scores(dscores, kt, NN), with_scores(dscores, qt, TN), with_scores(scores, do, TN)
        dqi, update = per_head(NN, do, st), per_head(TN, do, qi)

        dst = ds_scr[...]
        dsts = [None] * ncb
        for c in reversed(every):
            dsts[c] = dst
            dst = dec[c] * dst + update[c]
        ds_scr[...] = dst

        dv_state, dks = per_head(NT, ks, dsts), per_head(NN, v, dsts)
        ddec = [jnp.sum(dsts[c] * st[c], axis=0, keepdims=True) for c in every]
        dq = [dqt[c] * e_q[c] + dqi[c] * e_i[c] for c in every]
        dk = [dkt[c] * e_k[c] + dks[c] * e_s[c] for c in every]
        db = [q[c] * dq[c] - k[c] * dk[c] for c in every]
        db_last = [jnp.sum(dks[c] * ks[c], axis=0, keepdims=True) + ddec[c] * dec[c] for c in every]
        dg = [_dot_exact(anti, db[c]) + db_last[c] for c in every]
        df = [dg[c] / f[c] - dk[c] for c in every]
        dlb_scr[...] += sum(jnp.sum(df[c] * (1.0 - sig[c]), axis=0, keepdims=True) for c in every)
        dfp = [df[c] * (1.0 - lb) * sig[c] * (1.0 - sig[c]) for c in every]
        dv = [dv_intra[c] + dv_state[c] for c in every]
        for n, parts in enumerate((dq, dfp, dv)):
            dp_ref[n] = jnp.concatenate(parts, axis=0).astype(dp_ref.dtype)

        @pl.when(i == nblk - 1)
        def _():
            dlb = dlb_scr[...]
            dlbl_ref[0:1, :] = dlb * lb * (1.0 - lb)
            dlbl_ref[1:2, :] = -dlb * lb * s1

    grp = lambda g: pl.BlockSpec((None, tb, GROUP), lambda i: (g, nblk - 1 - i, 0))
    vec = pl.BlockSpec((2, HGRN_WIDTH), lambda i: (0, 0))
    return pl.pallas_call(
        body, name="hgrn_bwd", grid=(nblk,),
        in_specs=[grp(0), grp(1), grp(2), pl.BlockSpec((tb, HGRN_WIDTH), lambda i: (nblk - 1 - i, 0)),
                  pl.BlockSpec((ncb, HEAD_DIM, HGRN_WIDTH), lambda i: (nblk - 1 - i, 0, 0)), vec, ANY],
        out_specs=[pl.BlockSpec((3, tb, HGRN_WIDTH), lambda i: (0, nblk - 1 - i, 0)), vec],
        out_shape=[jax.ShapeDtypeStruct((3, t, HGRN_WIDTH), BF16), jax.ShapeDtypeStruct((2, HGRN_WIDTH), F32)],
        scratch_shapes=[pltpu.VMEM((HEAD_DIM, HGRN_WIDTH), F32), pltpu.VMEM((1, HGRN_WIDTH), F32)],
        compiler_params=_params("arbitrary"),
    )(proj, proj, proj, do, states, lb_logits, after)


GRAD_TILE = 512
OUT_PARTS = 4


class _Side:
    def __init__(self, operands, in_specs, out_shape, out_specs, scratch, init, begin):
        self.operands, self.in_specs, self.out_shape, self.out_specs = operands, in_specs, out_shape, out_specs
        self.scratch, self.init, self.begin = scratch, init, begin


def _grad_w(name, operands, widths, shape, step, after=None, side=None):
    t = operands[0].shape[-2]
    tt = min(t, GRAD_TILE)
    n_in, n_steps = len(operands), t // tt
    in_specs = [pl.BlockSpec((tt, w), lambda k: (k, 0)) if a.ndim == 2 else
                pl.BlockSpec((a.shape[0], tt, w), lambda k: (0, k, 0)) for a, w in zip(operands, widths)]
    extra = [] if after is None else [after]
    s_in, s_out = (len(side.operands), len(side.out_shape)) if side else (0, 0)
    first_out = n_in + s_in + len(extra)

    def body(*refs):
        o_ref, side_outs = refs[first_out], refs[first_out + 1:first_out + 1 + s_out]
        acc, narrow, sem = refs[first_out + 1 + s_out:first_out + 4 + s_out]
        k = pl.program_id(0)

        @pl.when(k == 0)
        def _():
            acc[...] = jnp.zeros_like(acc)
            if side:
                side.init(side_outs)

        tick = side.begin(k, n_steps, refs[n_in:n_in + s_in], side_outs, refs[first_out + 4 + s_out:]) if side else None
        step(acc, *refs[:n_in], tick or (lambda j: None))

        @pl.when(k == n_steps - 1)
        def _():
            part = shape[0] // OUT_PARTS
            copies = []
            for p in range(OUT_PARTS):
                rows = pl.ds(p * part, part)
                narrow[rows, :] = acc[rows, :].astype(narrow.dtype)
                copies.append(pltpu.make_async_copy(narrow.at[rows, :], o_ref.at[rows, :], sem.at[p]))
                copies[-1].start()
            for cp in copies:
                cp.wait()

    outs = pl.pallas_call(
        body, name=name, grid=(n_steps,),
        in_specs=in_specs + (side.in_specs if side else []) + [ANY] * len(extra),
        out_specs=[ANY] + (side.out_specs if side else []),
        out_shape=[jax.ShapeDtypeStruct(shape, BF16)] + (side.out_shape if side else []),
        scratch_shapes=[pltpu.VMEM(shape, F32), pltpu.VMEM(shape, BF16), pltpu.SemaphoreType.DMA((OUT_PARTS,))]
                       + (side.scratch if side else []),
        compiler_params=_params("arbitrary"),
    )(*operands, *(side.operands if side else ()), *extra)
    return outs if side else outs[0]


def _dw_in(xb, dph, dog, dpc, w_in, dpre1, after):
    t = xb.shape[0]

    def step(acc, x_ref, dh_ref, dog_ref, dc_ref, tick):
        xv = x_ref[...]
        for g in range(N_GROUPS):
            part = dh_ref[g] if g < 3 else dog_ref[...] if g == 3 else dc_ref[g - 4]
            acc[:, g * GROUP:(g + 1) * GROUP] += _dot(xv, part, TN)
            tick(g, part)

    def begin(k, n_steps, ins, outs, scratch):
        w_ref, dp_ref = ins
        total = [ALPHA * dp_ref[...]]

        def tick(g, part):
            total[0] = total[0] + _dot(part, w_ref[:, g * GROUP:(g + 1) * GROUP], NT)
            if g == N_GROUPS - 1:
                outs[0][...] = total[0]

        return tick

    row = pl.BlockSpec((min(t, GRAD_TILE), D_MODEL), lambda k: (k, 0))
    side = _Side((w_in, dpre1), [_resident((D_MODEL, IN_COLS)), row], [jax.ShapeDtypeStruct((t, D_MODEL), F32)], [row],
                 [], lambda outs: None, begin)
    return _grad_w("dw_in", (xb, dph, dog, dpc), (D_MODEL, GROUP, GROUP, GROUP), (D_MODEL, IN_COLS), step, after, side)


def _dw_out(cat_h, cat_c, dpre1b):
    def step(acc, h_ref, c_ref, d_ref, tick):
        dv = d_ref[...]
        acc[0:GROUP, :] += _dot(h_ref[...], dv, TN)
        acc[GROUP:2 * GROUP, :] += _dot(c_ref[...], dv, TN)

    return _grad_w("dw_out", (cat_h, cat_c, dpre1b), (GROUP, GROUP, D_MODEL), (D_MODEL, D_MODEL), step)


def _strips_of(j, tt):
    per_tick = tt // GATE_STRIP // N_FF
    return [slice(s * GATE_STRIP, (s + 1) * GATE_STRIP) for s in range(j * per_tick, (j + 1) * per_tick)]


def _dw_ff1(h1b, da, dcat, o, proj, gate_norm_w, after):
    t = h1b.shape[0]
    tt = min(t, GRAD_TILE)

    def step(acc, h_ref, da_ref, tick):
        hv = h_ref[...]
        for j in range(N_FF):
            cols = slice(j * FF_BLOCK, (j + 1) * FF_BLOCK)
            acc[:, cols] += _dot(hv, da_ref[:, cols], TN)
            tick(j)

    def init(outs):
        outs[2][...] = jnp.zeros_like(outs[2])

    def begin(k, n_steps, ins, outs, scratch):
        do2_ref, o_ref, og_ref, gnw_ref = ins
        do_ref, dog_ref, dgnw_ref = outs
        total = [jnp.zeros((GATE_STRIP, GROUP), F32)]

        def tick(j):
            gnw = gnw_ref[...]
            for rows in _strips_of(j, tt):
                ov, og, do2 = o_ref[rows, :], og_ref[rows, :], do2_ref[rows, :]
                rs = _per_head(lambda o_h: jnp.broadcast_to(
                    lax.rsqrt(jnp.mean(o_h * o_h, axis=-1, keepdims=True) + EPS), o_h.shape), ov)
                on = ov * rs
                sg = _sigmoid(og)
                sil = og * sg
                don = do2 * gnw * sil
                total[0] = total[0] + do2 * on * sil
                dog_ref[rows, :] = (do2 * on * gnw * (sg * (1.0 + og * (1.0 - sg)))).astype(dog_ref.dtype)
                do_ref[rows, :] = rs * (don - on * _per_head(
                    lambda p_h: jnp.broadcast_to(jnp.mean(p_h, axis=-1, keepdims=True), p_h.shape), don * on))
            if j == N_FF - 1:
                dgnw_ref[...] += jnp.sum(total[0], axis=0, keepdims=True)

        return tick

    tile = pl.BlockSpec((tt, GROUP), lambda k: (k, 0))
    vec = pl.BlockSpec((1, GROUP), lambda k: (0, 0))
    side = _Side(
        (dcat, o, proj, gate_norm_w), [tile, tile, pl.BlockSpec((None, tt, GROUP), lambda k: (3, k, 0)), vec],
        [jax.ShapeDtypeStruct((t, HGRN_WIDTH), F32), jax.ShapeDtypeStruct((t, HGRN_WIDTH), BF16),
         jax.ShapeDtypeStruct((1, HGRN_WIDTH), F32)], [tile, tile, vec], [], init, begin)
    return _grad_w("dw_ff1", (h1b, da), (D_MODEL, D_FF), (D_MODEL, D_FF), step, after, side)


def _dw_ff2(r, dpre2b, dcat, bcu, conv_w):
    t = r.shape[0]
    tt = min(t, GRAD_TILE)
    hb = tt // SUBLANES
    halo = 2 * SUBLANES

    def step(acc, r_ref, d_ref, tick):
        dv = d_ref[...]
        for j in range(N_FF):
            rows = slice(j * FF_BLOCK, (j + 1) * FF_BLOCK)
            acc[rows, :] += _dot(r_ref[:, rows], dv, TN)
            tick(j)

    def init(outs):
        outs[1][...] = jnp.zeros_like(outs[1])

    def begin(k, n_steps, ins, outs, scratch):
        dy_ref, dyn_ref, b_ref, bn_ref, c_ref, u_ref, ch_ref, uh_ref, cw_ref = ins
        dp_ref, dcw_ref = outs
        zbuf, dbuf = scratch
        before = lambda ref: ref[SUBLANES:halo, :].astype(F32)
        zbuf[0:SUBLANES, :] = jnp.where(k > 0, before(ch_ref) * before(uh_ref), 0.0)
        zbuf[SUBLANES:SUBLANES + tt, :] = c_ref[...].astype(F32) * u_ref[...].astype(F32)
        dbuf[0:tt, :] = dy_ref[...] * b_ref[...].astype(F32)
        dbuf[tt:tt + SUBLANES, :] = jnp.where(k < n_steps - 1, dyn_ref[...] * bn_ref[0:SUBLANES, :].astype(F32), 0.0)
        totals = [jnp.zeros((GATE_STRIP, GROUP), F32) for _ in range(3)]

        def tick(j):
            cw = cw_ref[...]
            for rows in _strips_of(j, tt):
                at = lambda buf, shift: buf[shift + rows.start:shift + rows.stop, :]
                z, z1, z2 = at(zbuf, SUBLANES), at(zbuf, SUBLANES - 1), at(zbuf, SUBLANES - 2)
                dyc, d1, d2 = at(dbuf, 0), at(dbuf, 1), at(dbuf, 2)
                yc = cw[2:3, :] * z + cw[1:2, :] * z1 + cw[0:1, :] * z2
                dz = cw[2:3, :] * dyc + cw[1:2, :] * d1 + cw[0:1, :] * d2
                dp_ref[0, rows, :] = (dy_ref[rows, :] * yc).astype(dp_ref.dtype)
                dp_ref[1, rows, :] = (dz * u_ref[rows, :].astype(F32)).astype(dp_ref.dtype)
                dp_ref[2, rows, :] = (dz * c_ref[rows, :].astype(F32)).astype(dp_ref.dtype)
                for n, tap in enumerate((z2, z1, z)):
                    totals[n] = totals[n] + dyc * tap
            if j == N_FF - 1:
                for n in range(3):
                    dcw_ref[n:n + 1, :] += jnp.sum(totals[n], axis=0, keepdims=True)

        return tick

    grp = lambda g: pl.BlockSpec((None, tt, GROUP), lambda k: (g, k, 0))
    prev = lambda g: pl.BlockSpec((None, halo, GROUP), lambda k: (g, jnp.maximum(k * (tt // halo) - 1, 0), 0))
    nxt = lambda g: pl.BlockSpec((None, halo, GROUP), lambda k: (g, jnp.minimum((k + 1) * (tt // halo), t // halo - 1), 0))
    nxt_row = lambda k: jnp.minimum((k + 1) * hb, t // SUBLANES - 1)
    whole = pl.BlockSpec((3, CONV_WIDTH), lambda k: (0, 0))
    side = _Side(
        (dcat, dcat, bcu, bcu, bcu, bcu, bcu, bcu, conv_w),
        [pl.BlockSpec((tt, GROUP), lambda k: (k, 1)), pl.BlockSpec((SUBLANES, GROUP), lambda k: (nxt_row(k), 1)),
         grp(0), nxt(0), grp(1), grp(2), prev(1), prev(2), whole],
        [jax.ShapeDtypeStruct((3, t, CONV_WIDTH), BF16), jax.ShapeDtypeStruct((3, CONV_WIDTH), F32)],
        [pl.BlockSpec((3, tt, GROUP), lambda k: (0, k, 0)), whole],
        [pltpu.VMEM((tt + SUBLANES, GROUP), F32), pltpu.VMEM((tt + SUBLANES, GROUP), F32)], init, begin)
    return _grad_w("dw_ff2", (r, dpre2b), (D_FF, D_MODEL), (D_FF, D_MODEL), step, side=side)


def _place():
    x, y, c = lax.axis_index("x"), lax.axis_index("y"), lax.axis_index("c")
    return x, y, c, 2 * x + y


def _other_chips(x, y):
    return [(1 - x, y), (x, 1 - y), (1 - x, 1 - y)]


def _place_shard(name, w, chip, cols_sharded, after=None):
    rows, cols = w.shape
    tr = min(rows, 256)
    nb = rows // tr
    full = (rows, cols * N_CHIPS) if cols_sharded else (rows * N_CHIPS, cols)
    out_map = (lambda i, s: (i, s[0])) if cols_sharded else (lambda i, s: (s[0] * nb + i, 0))

    def body(s_ref, w_ref, *rest):
        rest[-1][...] = w_ref[...].astype(rest[-1].dtype)

    extra = [] if after is None else [after]
    return pl.pallas_call(
        body, name=name,
        grid_spec=pltpu.PrefetchScalarGridSpec(
            num_scalar_prefetch=1, grid=(nb,),
            in_specs=[pl.BlockSpec((tr, cols), lambda i, s: (i, 0))] + [ANY] * len(extra),
            out_specs=pl.BlockSpec((tr, cols), out_map)),
        out_shape=jax.ShapeDtypeStruct(full, BF16),
        compiler_params=_params("parallel"),
    )(chip, w, *extra)


HBM = pl.BlockSpec(memory_space=pltpu.HBM)
SEM = pl.BlockSpec(memory_space=pltpu.SEMAPHORE)
EFFECT = pltpu.SideEffectType.DATAFLOW_SIDE_EFFECTING


PEER_SETS = {
    "sibling": (0, lambda x, y, c: [(x, y, 1 - c)]),
    "chips": (1, lambda x, y, c: [(1 - x, y, c), (x, 1 - y, c), (1 - x, 1 - y, c)]),
    "neighbours": (2, lambda x, y, c: [(1 - x, y, c), (x, 1 - y, c)]),
}


class _Split:
    def __init__(self, name, arrays, plan, others=(), peers=None):
        n_own, arrays = len(arrays), (*arrays, *others)
        n, n_copies = len(arrays), plan.count
        self.name, self.plan, self.n = name, plan, n_own
        barrier_id, peer_ids = PEER_SETS[peers] if peers else (None, None)

        def body(*refs):
            if peers:
                x, y, c, _ = _place()
                barrier = pltpu.get_barrier_semaphore()
                for peer in peer_ids(x, y, c):
                    pl.semaphore_signal(barrier, inc=1, device_id=peer, device_id_type=MESH)
                pl.semaphore_wait(barrier, len(peer_ids(0, 0, 0)))
            send_sems, recv_sems, token = refs[n], refs[n + 1], refs[-1]
            for k, (src, dst, to) in enumerate(plan(refs[:n])):
                pltpu.make_async_remote_copy(src_ref=src, dst_ref=dst, send_sem=send_sems.at[k], recv_sem=recv_sems.at[k],
                                             device_id=to, device_id_type=MESH).start()
            token[...] = jnp.zeros_like(token)

        outs = pl.pallas_call(
            body, name=name + "_start",
            out_shape=(pltpu.SemaphoreType.DMA((n_copies,)), pltpu.SemaphoreType.DMA((n_copies,)),
                       *[pltpu.HBM(a.shape, a.dtype) for a in arrays], jax.ShapeDtypeStruct((SUBLANES, LANES), F32)),
            in_specs=(HBM,) * n, out_specs=(SEM, SEM) + (HBM,) * n + (pl.BlockSpec(memory_space=pltpu.VMEM),),
            input_output_aliases={i: 2 + i for i in range(n)},
            compiler_params=pltpu.CompilerParams(has_side_effects=EFFECT, collective_id=barrier_id),
        )(*[pltpu.with_memory_space_constraint(a, pltpu.HBM) for a in arrays])
        self.sems, self.arrays, self.others, self.token = outs[:2], outs[2:2 + n_own], outs[2 + n_own:2 + n], outs[-1]

    def wait(self, after):
        n, plan = self.n, self.plan

        def body(*refs):
            send_sems, recv_sems = refs[n], refs[n + 1]
            for k, (src, dst, to) in enumerate(plan(refs[:n])):
                cp = pltpu.make_async_remote_copy(src_ref=src, dst_ref=dst, send_sem=send_sems.at[k],
                                                  recv_sem=recv_sems.at[k], device_id=to, device_id_type=MESH)
                cp.wait_send()
                cp.wait_recv()

        return pl.pallas_call(
            body, name=self.name + "_wait", out_shape=tuple(pltpu.HBM(a.shape, a.dtype) for a in self.arrays),
            in_specs=(HBM,) * n + (SEM, SEM, ANY), out_specs=(HBM,) * n, input_output_aliases={i: i for i in range(n)},
            compiler_params=pltpu.CompilerParams(has_side_effects=EFFECT),
        )(*self.arrays, *self.sems, after)


COLS_SHARDED = (True, False, True, False)
HALF_SHAPES = [(D_MODEL // 2, IN_COLS), (D_MODEL, D_MODEL // 2), (D_MODEL // 2, D_FF), (D_FF, D_MODEL // 2)]
PIECE_SHAPES = [(D_MODEL // 2, IN_COLS // N_CHIPS), (D_MODEL // N_CHIPS, D_MODEL // 2),
                (D_MODEL // 2, D_FF // N_CHIPS), (D_FF // N_CHIPS, D_MODEL // 2)]


def _shard_view(kind, ref, chip):
    if COLS_SHARDED[kind]:
        n = ref.shape[1] // N_CHIPS
        return ref.at[:, pl.ds(chip * n, n)]
    n = ref.shape[0] // N_CHIPS
    return ref.at[pl.ds(chip * n, n), :]


def _half_view(kind, ref, h):
    if COLS_SHARDED[kind]:
        n = ref.shape[0] // 2
        return ref.at[pl.ds(h * n, n), :]
    n = ref.shape[1] // 2
    return ref.at[:, pl.ds(h * n, n)]


def _plan(count):
    def mark(fn):
        fn.count = count
        return fn
    return mark


def _shard_rows_view(kind, ref, chip, part, n_parts):
    if COLS_SHARDED[kind]:
        m, n = ref.shape[0] // n_parts, ref.shape[1] // N_CHIPS
        return ref.at[pl.ds(part * m, m), pl.ds(chip * n, n)]
    m = ref.shape[0] // N_CHIPS // n_parts
    return ref.at[pl.ds((n_parts * chip + part) * m, m), :]


def _shard_half_view(kind, ref, chip, h):
    return _shard_rows_view(kind, ref, chip, h, 2)


def _gather_over_ici(kinds, weights):
    @_plan(2 * len(kinds))
    def plan(refs):
        x, y, c, me = _place()
        mine = [_shard_half_view(kind, ref, me, c) for kind, ref in zip(kinds, refs)]
        return [(v, v, to) for v in mine for to in ((1 - x, y, c), (x, 1 - y, c))]

    return _Split("gather_ici_" + "".join(map(str, kinds)), tuple(weights), plan, peers="neighbours")


def _relay_over_ici(kinds, weights, others=()):
    @_plan(2 * len(kinds))
    def plan(refs):
        x, y, c, _ = _place()
        x_nbr, y_nbr = 2 * (1 - x) + y, 2 * x + (1 - y)
        out = []
        for kind, ref in zip(kinds, refs):
            first, second = (_shard_rows_view(kind, ref, chip, 2 * c + q, 4) for q, chip in ((0, x_nbr), (1, y_nbr)))
            out += [(first, first, (x, 1 - y, c)), (second, second, (1 - x, y, c))]
        return out

    return _Split("relay_ici_" + "".join(map(str, kinds)), tuple(weights), plan, others, peers="neighbours")


def _gather_w_in_over_ici(w_in, conv4):
    @_plan(6)
    def plan(refs):
        x, y, c, me = _place()
        half, conv = _shard_half_view(0, refs[0], me, c), refs[1].at[me]
        return [(v, v, (px, py, c)) for v in (half, conv) for px, py in _other_chips(x, y)]

    return _Split("gather_w_in_ici", (w_in, conv4), plan, peers="chips")


def _gather_over_d2d(kinds, weights):
    @_plan(3 * len(kinds))
    def plan(refs):
        x, y, c, _ = _place()
        got = [_shard_half_view(kind, ref, 2 * px + py, c) for kind, ref in zip(kinds, refs)
               for px, py in _other_chips(x, y)]
        return [(v, v, (x, y, 1 - c)) for v in got]

    return _Split("gather_d2d_" + "".join(map(str, kinds)), tuple(weights), plan, peers="sibling")


def _swap_halves(kinds, grads):
    @_plan(len(kinds))
    def plan(refs):
        x, y, c, _ = _place()
        return [(_half_view(kind, g, 1 - c), land, (x, y, 1 - c))
                for kind, g, land in zip(kinds, refs[:len(kinds)], refs[len(kinds):])]

    lands = [lax.empty(HALF_SHAPES[kind], g.dtype) for kind, g in zip(kinds, grads)]
    return _Split("swap_halves_" + "".join(map(str, kinds)), (*grads, *lands), plan, peers="sibling")


def _block_rows(cols, elements):
    return 1 << ((elements // cols).bit_length() - 1)


def _add_half(name, g, recv, core, rows_split):
    shape = recv.shape
    tr = min(shape[0], _block_rows(shape[1], 1 << 20))
    nb = shape[0] // tr

    def body(c_ref, g_ref, r_ref, o_ref):
        o_ref[...] = (g_ref[...].astype(F32) + r_ref[...].astype(F32)).astype(o_ref.dtype)

    g_map = (lambda i, c_ref: (c_ref[0] * nb + i, 0)) if rows_split else (lambda i, c_ref: (i, c_ref[0]))
    blk = pl.BlockSpec((tr, shape[1]), lambda i, c_ref: (i, 0))
    return pl.pallas_call(
        body, name=name,
        grid_spec=pltpu.PrefetchScalarGridSpec(
            num_scalar_prefetch=1, grid=(nb,),
            in_specs=[pl.BlockSpec((tr, shape[1]), g_map), blk], out_specs=blk),
        out_shape=jax.ShapeDtypeStruct(shape, BF16),
        compiler_params=_params("parallel"),
    )(core, g, recv)


def _exchange_pieces(kinds, halves, pack=None):
    n_p, n = N_CHIPS - 1, len(kinds)

    @_plan(n_p * n + (0 if pack is None else N_DEV - 1))
    def plan(refs):
        x, y, c, _ = _place()
        copies = []
        if pack is not None:
            me = 4 * x + 2 * y + c
            peers = [((1 - x) if m & 4 else x, (1 - y) if m & 2 else y, (1 - c) if m & 1 else c) for m in range(1, N_DEV)]
            copies += [(refs[2 * n], refs[2 * n + 1].at[me], peer) for peer in peers]
        return copies + [(_shard_view(kind, half, 2 * px + py), land.at[j], (px, py, c))
                         for j, (px, py) in enumerate(_other_chips(x, y))
                         for kind, half, land in zip(kinds, refs[:n], refs[n:2 * n])]

    lands = [lax.empty((n_p,) + PIECE_SHAPES[kind], BF16) for kind in kinds]
    small = () if pack is None else (pack, lax.empty((N_DEV,) + pack.shape, F32))
    return _Split("exchange_pieces_" + "".join(map(str, kinds)), (*halves, *lands, *small), plan,
                  peers="chips" if pack is None else None)


def _sum_pieces(name, half, slots, place, rows_split, after):
    n_p, rows, cols = slots.shape
    tr = min(rows, _block_rows(cols, 1 << 19))
    nb = rows // tr
    if rows_split:
        own_map = lambda i, s: (i, s[0])
        out_map = lambda i, s: (s[1] * nb + i, 0)
        shard = (2 * rows, cols)
    else:
        own_map = lambda i, s: (s[0] * nb + i, 0)
        out_map = lambda i, s: (i, s[1])
        shard = (rows, 2 * cols)

    def body(s_ref, own_ref, slot_ref, after_ref, o_ref):
        total = own_ref[...].astype(F32)
        for j in range(n_p):
            total = total + slot_ref[j].astype(F32)
        o_ref[...] = total

    return pl.pallas_call(
        body, name=name,
        grid_spec=pltpu.PrefetchScalarGridSpec(
            num_scalar_prefetch=1, grid=(nb,),
            in_specs=[pl.BlockSpec((tr, cols), own_map), pl.BlockSpec((n_p, tr, cols), lambda i, s: (0, i, 0)), ANY],
            out_specs=pl.BlockSpec((tr, cols), out_map)),
        out_shape=jax.ShapeDtypeStruct(shard, F32),
        compiler_params=_params("parallel"),
    )(place, half, slots, after)


def _join_halves(kinds, shards):
    @_plan(len(kinds))
    def plan(refs):
        x, y, c, _ = _place()
        return [(_half_view(kind, g, c), _half_view(kind, g, c), (x, y, 1 - c)) for kind, g in zip(kinds, refs)]

    return _Split("join_halves_" + "".join(map(str, kinds)), tuple(shards), plan, peers="sibling")


N_DEV = 8


def _sum_shared(pack, land, device):
    def body(d_ref, p_ref, l_ref, o_ref):
        me = d_ref[0]
        total = jnp.where(me == 0, p_ref[...], l_ref[0])
        for d in range(1, N_DEV):
            total = total + jnp.where(me == d, p_ref[...], l_ref[d])
        o_ref[...] = total

    return pl.pallas_call(
        body, name="sum_shared",
        grid_spec=pltpu.PrefetchScalarGridSpec(
            num_scalar_prefetch=1, grid=(1,),
            in_specs=[pl.BlockSpec(pack.shape, lambda i, d: (0, 0)), pl.BlockSpec(land.shape, lambda i, d: (0, 0, 0))],
            out_specs=pl.BlockSpec(pack.shape, lambda i, d: (0, 0))),
        out_shape=jax.ShapeDtypeStruct(pack.shape, F32),
    )(device, pack, land)


def _adamw(name, w, g, m, v, after=None):
    rows, cols = w.shape
    tr = min(rows, 256)
    extra = [] if after is None else [after]

    def body(w_ref, g_ref, m_ref, v_ref, *rest):
        go_ref, d_ref, nm_ref, nv_ref = rest[-4:]
        g = g_ref[...]
        go_ref[...] = g
        d_ref[...], nm_ref[...], nv_ref[...] = _adam_step(w_ref[...], g, m_ref[...], v_ref[...])

    blk = pl.BlockSpec((tr, cols), lambda i: (i, 0))
    return pl.pallas_call(
        body, name=name, grid=(rows // tr,), in_specs=[blk] * 4 + [ANY] * len(extra), out_specs=[blk] * 4,
        out_shape=[jax.ShapeDtypeStruct(w.shape, F32)] * 4,
        compiler_params=_params("parallel"),
    )(w, g, m, v, *extra)


def _adam_step(w, g, m, v):
    nm = ADAM_B1 * m + (1.0 - ADAM_B1) * g
    nv = ADAM_B2 * v + (1.0 - ADAM_B2) * jnp.square(g)
    m_hat = nm * (1.0 / (1.0 - ADAM_B1 ** ADAM_STEP))
    v_hat = nv * (1.0 / (1.0 - ADAM_B2 ** ADAM_STEP))
    return -ADAM_LR * (m_hat / (jnp.sqrt(v_hat) + ADAM_EPS) + ADAM_WD * w), nm, nv


def _adamw_small(tot, chip, weights, ms, vs, after):
    n, half = len(weights), D_MODEL // 2

    def body(chip_ref, tot_ref, *refs):
        ins, outs = refs[:3 * n], refs[3 * n + 1:]
        tot = tot_ref[...]
        conv_all = jnp.concatenate([tot[5:6, half:], tot[6:7, :half], tot[6:7, half:]], axis=0)
        conv = sum(jnp.where(chip_ref[0] == s, conv_all[:, s * LANES:(s + 1) * LANES], 0.0) for s in range(N_CHIPS))
        grads = [jnp.concatenate([tot[4:5, :half], tot[4:5, half:]], axis=0), tot[5:6, :half], conv,
                 tot[0:1], tot[1:2], tot[2:3], tot[3:4]]
        for k, g in enumerate(grads):
            delta, nm, nv = _adam_step(ins[k][...], g, ins[n + k][...], ins[2 * n + k][...])
            outs[k][...], outs[n + k][...], outs[2 * n + k][...], outs[3 * n + k][...] = g, delta, nm, nv
        outs[4 * n][...] = tot[7:8, 0:1]

    whole = lambda a: pl.BlockSpec(a.shape, lambda i, s: (0,) * a.ndim)
    arrays = (*weights, *ms, *vs)
    loss = jax.ShapeDtypeStruct((1, 1), F32)
    return pl.pallas_call(
        body, name="adamw_small",
        grid_spec=pltpu.PrefetchScalarGridSpec(
            num_scalar_prefetch=1, grid=(1,), in_specs=[whole(tot)] + [whole(a) for a in arrays] + [ANY],
            out_specs=[whole(a) for a in weights] * 4 + [whole(loss)]),
        out_shape=[jax.ShapeDtypeStruct(a.shape, F32) for a in weights] * 4 + [loss],
    )(chip, tot, *arrays, after)


def kernel(x, w_in, lb_logits, gate_norm_w, conv_w, w_out, ln1_g, ln1_b, w_ff1, w_ff2, ln2_g, ln2_b, loss_target, m_w_in, m_lb_logits, m_gate_norm_w, m_conv_w, m_w_out, m_ln1_g, m_ln1_b, m_w_ff1, m_w_ff2, m_ln2_g, m_ln2_b, v_w_in, v_lb_logits, v_gate_norm_w, v_conv_w, v_w_out, v_ln1_g, v_ln1_b, v_w_ff1, v_w_ff2, v_ln2_g, v_ln2_b):
    xs, tgt = x[0], loss_target[0]
    chip = 2 * lax.axis_index("x") + lax.axis_index("y")
    core = lax.axis_index("c").astype(jnp.int32).reshape(1)
    chip1 = chip.astype(jnp.int32).reshape(1)
    place = jnp.concatenate([chip1, core])

    conv4 = lax.dynamic_update_slice(jnp.zeros((N_CHIPS,) + conv_w.shape[1:], F32), conv_w, (chip, 0, 0))
    ici_in = _gather_w_in_over_ici(_place_shard("place_w_in", w_in[0], chip1, True), conv4)
    rest = (1, 2, 3)
    ici_rest = _gather_over_ici(rest, (_place_shard("place_w_out", w_out[0], chip1, False, after=ici_in.token),
                                       _place_shard("place_w_ff1", w_ff1[0], chip1, True, after=ici_in.token),
                                       _place_shard("place_w_ff2", w_ff2[0], chip1, False, after=ici_in.token)))
    wb_in, cv4 = ici_in.wait(ici_rest.token)
    d2d_in = _gather_over_d2d((0,), (wb_in,))
    wb_in, = d2d_in.wait(d2d_in.token)
    conv_full = cv4.transpose(1, 0, 2).reshape(3, CONV_WIDTH)

    proj, bcu, xb, cat_c = _in_proj(xs, wb_in, conv_full, ici_rest.token)
    relay_rest = _relay_over_ici(rest, ici_rest.wait(proj))
    o, states = _hgrn_fwd(proj, lb_logits, relay_rest.token)
    d2d_rest = _gather_over_d2d(rest, relay_rest.wait(o))
    cat_h = _gate_fwd(proj, o, gate_norm_w, d2d_rest.token)
    wb_out, wb_ff1, wb_ff2 = d2d_rest.wait(cat_h)

    (h1b, r, da, dpre2b, dpre1, dpre1b, dcat, g_ln1_g, g_ln1_b, g_ln2_g, g_ln2_b, loss8) = _sublayers(
        cat_h, cat_c, xs, tgt, wb_out, wb_ff1, wb_ff2, ln1_g, ln1_b, ln2_g, ln2_b)

    names = ("w_in", "w_out", "w_ff1", "w_ff2")

    def add_halves(kinds, grads, lands):
        return [_add_half("add_half_" + names[k], g, ld, core, COLS_SHARDED[k]) for k, g, ld in zip(kinds, grads, lands)]

    def sum_pieces(kinds, halves, lands, after):
        return [_sum_pieces("sum_pieces_" + names[k], h, ld, place, COLS_SHARDED[k], after)
                for k, h, ld in zip(kinds, halves, lands)]

    early = (1, 2, 3)
    g_out_local = _dw_out(cat_h, cat_c, dpre1b)
    g_ff2_local, dpc, g_conv = _dw_ff2(r, dpre2b, dcat, bcu, conv_full)
    swap_a = _swap_halves((1, 3), (g_out_local, g_ff2_local))
    g_ff1_local, do, dog, g_gnw = _dw_ff1(h1b, da, dcat, o, proj, gate_norm_w, swap_a.token)
    swap_b = _swap_halves((2,), (g_ff1_local,))
    swapped_a = swap_a.wait(swap_b.token)
    halves_a = add_halves((1, 3), swapped_a[:2], swapped_a[2:])
    swapped_b = swap_b.wait(halves_a[1])
    halves = (halves_a[0], *add_halves((2,), swapped_b[:1], swapped_b[1:]), halves_a[1])
    exch = _exchange_pieces(early, halves)
    dph, g_lbl = _hgrn_bwd(proj, do, states, lb_logits, exch.token)
    g_in_local, grad_x = _dw_in(xb, dph, dog, dpc, wb_in, dpre1, dph)

    late = (0,)
    swap = _swap_halves(late, (g_in_local,))
    exchanged = exch.wait(swap.token)
    pack = jnp.concatenate([
        g_ln1_g, g_ln1_b, g_ln2_g, g_ln2_b,
        jnp.concatenate([g_lbl[0:1], g_lbl[1:2]], axis=1),
        jnp.concatenate([g_gnw, g_conv[0:1]], axis=1),
        jnp.concatenate([g_conv[1:2], g_conv[2:3]], axis=1),
        jnp.concatenate([loss8[0:1], jnp.zeros((1, D_MODEL - LANES), F32)], axis=1)], axis=0)
    join_a = _join_halves((2,), sum_pieces((2,), exchanged[1:2], exchanged[4:5], swap.token))
    swapped = swap.wait(join_a.token)
    exch = _exchange_pieces(late, add_halves(late, swapped[:1], swapped[1:]), pack)
    join_b = _join_halves((1, 3), sum_pieces((1, 3), exchanged[0:3:2], exchanged[3:6:2], exch.token))
    g_w_ff1, = join_a.wait(join_b.token)
    g_w_ff1, d_ff1, nm_ff1, nv_ff1 = _adamw("adamw_w_ff1", w_ff1[0], g_w_ff1, m_w_ff1[0], v_w_ff1[0])
    g_w_out, g_w_ff2 = join_b.wait(d_ff1)
    g_w_ff2, d_ff2, nm_ff2, nv_ff2 = _adamw("adamw_w_ff2", w_ff2[0], g_w_ff2, m_w_ff2[0], v_w_ff2[0])
    g_w_out, d_out, nm_out, nv_out = _adamw("adamw_w_out", w_out[0], g_w_out, m_w_out[0], v_w_out[0], d_ff2)
    exchanged = exch.wait(d_out)
    tot = _sum_shared(exchanged[2], exchanged[3], 2 * chip1 + core)
    join = _join_halves(late, sum_pieces(late, exchanged[:1], exchanged[1:2], tot))
    small = ("lb_logits", "gate_norm_w", "conv_w", "ln1_g", "ln1_b", "ln2_g", "ln2_b")
    small_out = _adamw_small(
        tot, chip1, (lb_logits, gate_norm_w, conv_w[0], ln1_g, ln1_b, ln2_g, ln2_b),
        (m_lb_logits, m_gate_norm_w, m_conv_w[0], m_ln1_g, m_ln1_b, m_ln2_g, m_ln2_b),
        (v_lb_logits, v_gate_norm_w, v_conv_w[0], v_ln1_g, v_ln1_b, v_ln2_g, v_ln2_b), join.token)
    g_w_in, = join.wait(small_out[0])
    g_w_in, d_in, nm_in, nv_in = _adamw("adamw_w_in", w_in[0], g_w_in, m_w_in[0], v_w_in[0])
    loss = small_out[4 * len(small)][0, 0]

    def results(n_kind, large):
        out = dict(zip(small, small_out[n_kind * len(small):(n_kind + 1) * len(small)]))
        out["conv_w"] = out["conv_w"][None]
        out.update({name: a[None] for name, a in zip(("w_in", "w_out", "w_ff1", "w_ff2"), large)})
        return [out[name] for name in ("w_in", "lb_logits", "gate_norm_w", "conv_w", "w_out", "ln1_g", "ln1_b",
                                       "w_ff1", "w_ff2", "ln2_g", "ln2_b")]

    return (loss, grad_x[None], *results(0, (g_w_in, g_w_out, g_w_ff1, g_w_ff2)),
            *results(1, (d_in, d_out, d_ff1, d_ff2)), *results(2, (nm_in, nm_out, nm_ff1, nm_ff2)),
            *results(3, (nv_in, nv_out, nv_ff1, nv_ff2)))
```

```python
import jax
import jax.numpy as jnp
from jax import lax
from jax.experimental import pallas as pl
from jax.experimental.pallas import tpu as pltpu

F32 = jnp.float32
BF16 = jnp.bfloat16
MXU_DTYPE = jnp.bfloat16

D_MODEL = 1024
HGRN_WIDTH = 512
HEAD_DIM = 128
N_HEADS = 4
CONV_WIDTH = 512
CHUNK = 64
D_FF = 4096
IN_COLS = 3584
GROUP = 512
N_GROUPS = IN_COLS // GROUP
ALPHA = 2.0 ** 0.25
EPS = 1e-5
N_CHIPS = 4
ADAM_LR, ADAM_B1, ADAM_B2, ADAM_EPS, ADAM_WD, ADAM_STEP = 0.001, 0.9, 0.999, 1e-08, 0.01, 10

LANES = 128
SUBLANES = 8
VMEM_LIMIT = 56 * 1024 * 1024
FF_BLOCK = 1024
N_FF = D_FF // FF_BLOCK
GATE_STRIP = 64

NN = (((1,), (0,)), ((), ()))
NT = (((1,), (1,)), ((), ()))
TN = (((0,), (0,)), ((), ()))
MESH = pl.DeviceIdType.MESH
ANY = pl.BlockSpec(memory_space=pl.ANY)


def _dot(a, b, dims):
    return lax.dot_general(a.astype(MXU_DTYPE), b.astype(MXU_DTYPE), dims, preferred_element_type=F32)


def _dot_exact(ones, v):
    ones = ones.astype(jnp.bfloat16)
    hi = v.astype(jnp.bfloat16)
    rest = v - hi.astype(F32)
    mid = rest.astype(jnp.bfloat16)
    low = (rest - mid.astype(F32)).astype(jnp.bfloat16)
    return sum(lax.dot_general(ones, part, NN, preferred_element_type=F32) for part in (hi, mid, low))


def _params(*sem):
    return pltpu.CompilerParams(dimension_semantics=sem, vmem_limit_bytes=VMEM_LIMIT)


def _resident(shape):
    return pl.BlockSpec(shape, lambda *_: (0,) * len(shape), pipeline_mode=pl.Buffered(1))


def _sigmoid(v):
    return 1.0 / (1.0 + jnp.exp(-v))


def _lower_bound(lbl):
    m = jnp.max(lbl, axis=0, keepdims=True)
    e = jnp.exp(lbl - m)
    s = e / jnp.sum(e, axis=0, keepdims=True)
    return s[0:1, :], s[1:2, :]


def _heads(v):
    return [v[:, h * HEAD_DIM:(h + 1) * HEAD_DIM] for h in range(N_HEADS)]


def _per_head(fn, *arrays):
    return jnp.concatenate([fn(*parts) for parts in zip(*map(_heads, arrays))], axis=1)


def _in_proj(x, w_in, conv_w, after):
    t = x.shape[0]
    tm = min(t, 512)

    def body(x_ref, w_ref, cw_ref, after_ref, o_ref, bcu_ref, xb_ref, y_ref, zbuf):
        @pl.when(pl.program_id(0) == 0)
        def _():
            zbuf[tm:tm + SUBLANES, :] = jnp.zeros((SUBLANES, CONV_WIDTH), F32)

        xb = x_ref[...].astype(xb_ref.dtype)
        xb_ref[...] = xb
        group = lambda g: _dot(xb, w_ref[:, g * GROUP:(g + 1) * GROUP], NN)
        for g in range(4):
            o_ref[g] = group(g)
        b_gate, c_gate, u = group(4), group(5), group(6)
        for n, part in enumerate((b_gate, c_gate, u)):
            bcu_ref[n] = part.astype(bcu_ref.dtype)
        zbuf[0:SUBLANES, :] = zbuf[tm:tm + SUBLANES, :]
        zbuf[SUBLANES:SUBLANES + tm, :] = c_gate * u
        cw = cw_ref[...]
        at = lambda shift: zbuf[shift:shift + tm, :]
        conv = cw[2:3, :] * at(SUBLANES) + cw[1:2, :] * at(SUBLANES - 1) + cw[0:1, :] * at(SUBLANES - 2)
        y_ref[...] = (b_gate * conv).astype(y_ref.dtype)

    return pl.pallas_call(
        body, name="in_proj", grid=(t // tm,),
        in_specs=[pl.BlockSpec((tm, D_MODEL), lambda i: (i, 0)), _resident((D_MODEL, IN_COLS)),
                  pl.BlockSpec((3, CONV_WIDTH), lambda i: (0, 0)), ANY],
        out_specs=[pl.BlockSpec((4, tm, GROUP), lambda i: (0, i, 0)), pl.BlockSpec((3, tm, GROUP), lambda i: (0, i, 0)),
                   pl.BlockSpec((tm, D_MODEL), lambda i: (i, 0)), pl.BlockSpec((tm, CONV_WIDTH), lambda i: (i, 0))],
        out_shape=[jax.ShapeDtypeStruct((4, t, GROUP), F32), jax.ShapeDtypeStruct((3, t, GROUP), BF16),
                   jax.ShapeDtypeStruct((t, D_MODEL), BF16), jax.ShapeDtypeStruct((t, CONV_WIDTH), BF16)],
        scratch_shapes=[pltpu.VMEM((tm + SUBLANES, CONV_WIDTH), F32)],
        compiler_params=_params("arbitrary"),
    )(x, w_in, conv_w, after)


def _gates(fp, lb):
    sig = _sigmoid(fp)
    f = lb + (1.0 - lb) * sig
    return sig, f, jnp.log(f), 1.0 - f


def _chunk_masks():
    row = lax.broadcasted_iota(jnp.int32, (CHUNK, CHUNK), 0)
    col = lax.broadcasted_iota(jnp.int32, (CHUNK, CHUNK), 1)
    return row >= col, row <= col


def _hgrn_fwd(proj, lb_logits, after):
    t = proj.shape[1]
    tb = min(t, 512)
    ncb = tb // CHUNK

    def body(q_ref, f_ref, v_ref, lbl_ref, after_ref, o_ref, st_ref, s_scr):
        @pl.when(pl.program_id(0) == 0)
        def _():
            s_scr[...] = jnp.zeros_like(s_scr)

        lb, _ = _lower_bound(lbl_ref[...])
        causal, _ = _chunk_masks()

        every = range(ncb)
        rows = [slice(c * CHUNK, (c + 1) * CHUNK) for c in every]
        q, v = [q_ref[r, :] for r in rows], [v_ref[r, :] for r in rows]
        gates = [_gates(f_ref[r, :], lb) for r in rows]
        k = [gt[3] for gt in gates]
        b = [_dot_exact(causal, gt[2]) for gt in gates]
        mid, last = [x[CHUNK // 2:CHUNK // 2 + 1, :] for x in b], [x[CHUNK - 1:CHUNK, :] for x in b]
        qt = [q[c] * jnp.exp(b[c] - mid[c]) for c in every]
        kt = [k[c] * jnp.exp(mid[c] - b[c]) for c in every]
        qi = [q[c] * jnp.exp(b[c]) for c in every]
        ks = [k[c] * jnp.exp(last[c] - b[c]) for c in every]
        dec = [jnp.exp(x) for x in last]
        scores = [[jnp.where(causal, _dot(a, b_, NT), 0.0) for a, b_ in zip(_heads(qt[c]), _heads(kt[c]))] for c in every]
        intra = [[_dot(s, v_h, NN) for s, v_h in zip(scores[c], _heads(v[c]))] for c in every]
        update = [_per_head(lambda v_h, ks_h: _dot(v_h, ks_h, TN), v[c], ks[c]) for c in every]

        st = s_scr[...]
        states = []
        for c in every:
            states.append(st)
            st_ref[c] = st
            st = dec[c] * st + update[c]
        s_scr[...] = st

        o_ref[...] = jnp.concatenate(
            [jnp.concatenate([i_h + _dot(qi_h, st_h, NT) for i_h, qi_h, st_h in
                              zip(intra[c], _heads(qi[c]), _heads(states[c]))], axis=1) for c in every], axis=0)

    grp = lambda g: pl.BlockSpec((None, tb, GROUP), lambda i: (g, i, 0))
    return pl.pallas_call(
        body, name="hgrn_fwd", grid=(t // tb,),
        in_specs=[grp(0), grp(1), grp(2), pl.BlockSpec((2, HGRN_WIDTH), lambda i: (0, 0)), ANY],
        out_specs=[pl.BlockSpec((tb, HGRN_WIDTH), lambda i: (i, 0)),
                   pl.BlockSpec((ncb, HEAD_DIM, HGRN_WIDTH), lambda i: (i, 0, 0))],
        out_shape=[jax.ShapeDtypeStruct((t, HGRN_WIDTH), F32),
                   jax.ShapeDtypeStruct((t // CHUNK, HEAD_DIM, HGRN_WIDTH), F32)],
        scratch_shapes=[pltpu.VMEM((HEAD_DIM, HGRN_WIDTH), F32)],
        compiler_params=_params("arbitrary"),
    )(proj, proj, proj, lb_logits, after)


def _gate_fwd(proj, o, gate_norm_w, after):
    t = proj.shape[1]
    tb = min(t, 512)

    def body(o_ref, og_ref, gnw_ref, after_ref, out_ref):
        gnw = gnw_ref[...]
        for s in range(tb // GATE_STRIP):
            rows = slice(s * GATE_STRIP, (s + 1) * GATE_STRIP)
            og = og_ref[rows, :]
            on = _per_head(lambda o_h: o_h * lax.rsqrt(jnp.mean(o_h * o_h, axis=-1, keepdims=True) + EPS), o_ref[rows, :])
            out_ref[rows, :] = (on * gnw * (og * _sigmoid(og))).astype(out_ref.dtype)

    tile = pl.BlockSpec((tb, GROUP), lambda i: (i, 0))
    return pl.pallas_call(
        body, name="gate_fwd", grid=(t // tb,),
        in_specs=[tile, pl.BlockSpec((None, tb, GROUP), lambda i: (3, i, 0)), pl.BlockSpec((1, GROUP), lambda i: (0, 0)), ANY],
        out_specs=tile,
        out_shape=jax.ShapeDtypeStruct((t, HGRN_WIDTH), BF16),
        compiler_params=_params("parallel"),
    )(o, proj, gate_norm_w, after)


def _ln_bwd(dy, xhat, rstd, g):
    dxhat = dy * g
    m1 = jnp.mean(dxhat, axis=-1, keepdims=True)
    m2 = jnp.mean(dxhat * xhat, axis=-1, keepdims=True)
    return rstd * (dxhat - m1 - xhat * m2)


def _layer_norm(pre):
    xc = pre - jnp.mean(pre, axis=-1, keepdims=True)
    rstd = lax.rsqrt(jnp.mean(xc * xc, axis=-1, keepdims=True) + EPS)
    return xc * rstd, rstd


def _sublayers(cat_h, cat_c, x, target, w_out, w_ff1, w_ff2, g1, b1, g2, b2):
    t = x.shape[0]
    tm = min(t, 256)

    def body(ch_ref, cc_ref, x_ref, tg_ref, wo_ref, w1_ref, w2_ref, g1_ref, b1_ref, g2_ref, b2_ref,
             h1_ref, r_ref, da_ref, dp2b_ref, dp1_ref, dcat_ref, dg1_ref, db1_ref, dg2_ref, db2_ref, loss_ref, gwo_ref,
             gwo_acc, gwo_narrow, sem):
        @pl.when(pl.program_id(0) == 0)
        def _():
            for ref in (dg1_ref, db1_ref, dg2_ref, db2_ref, loss_ref, gwo_acc):
                ref[...] = jnp.zeros_like(ref)

        mix = _dot(ch_ref[...], wo_ref[0:GROUP, :], NN) + _dot(cc_ref[...], wo_ref[GROUP:2 * GROUP, :], NN)
        xhat1, rstd1 = _layer_norm(ALPHA * x_ref[...] + mix)
        h1 = xhat1 * g1_ref[...] + b1_ref[...]
        h1b = h1.astype(h1_ref.dtype)
        h1_ref[...] = h1b
        mlp = jnp.zeros((tm, D_MODEL), F32)
        for j in range(N_FF):
            cols = slice(j * FF_BLOCK, (j + 1) * FF_BLOCK)
            r = jnp.square(jnp.maximum(_dot(h1b, w1_ref[:, cols], NN), 0.0)).astype(r_ref.dtype)
            r_ref[:, cols] = r
            mlp = mlp + _dot(r, w2_ref[cols, :], NN)
        xhat2, rstd2 = _layer_norm(ALPHA * h1 + mlp)
        err = xhat2 * g2_ref[...] + b2_ref[...] - tg_ref[...]
        loss_ref[...] += 0.5 * jnp.sum(jnp.mean(err * err, axis=-1, keepdims=True))
        dy = err * (1.0 / D_MODEL)
        dg2_ref[...] += jnp.sum(dy * xhat2, axis=0, keepdims=True)
        db2_ref[...] += jnp.sum(dy, axis=0, keepdims=True)
        dp2 = _ln_bwd(dy, xhat2, rstd2, g2_ref[...])
        dp2b = dp2.astype(dp2b_ref.dtype)
        dp2b_ref[...] = dp2b
        back = jnp.zeros((tm, D_MODEL), F32)
        for j in range(N_FF):
            cols = slice(j * FF_BLOCK, (j + 1) * FF_BLOCK)
            dr = _dot(dp2b, w2_ref[cols, :], NT)
            da = (dr * (2.0 * jnp.sqrt(r_ref[:, cols].astype(F32)))).astype(da_ref.dtype)
            da_ref[:, cols] = da
            back = back + _dot(da, w1_ref[:, cols], NT)
        dh1 = ALPHA * dp2 + back
        dg1_ref[...] += jnp.sum(dh1 * xhat1, axis=0, keepdims=True)
        db1_ref[...] += jnp.sum(dh1, axis=0, keepdims=True)
        dp1 = _ln_bwd(dh1, xhat1, rstd1, g1_ref[...])
        dp1b = dp1.astype(MXU_DTYPE)
        dp1_ref[...] = dp1
        dcat_ref[...] = _dot(dp1b, wo_ref[...], NT)
        gwo_acc[0:GROUP, :] += _dot(ch_ref[...], dp1b, TN)
        gwo_acc[GROUP:2 * GROUP, :] += _dot(cc_ref[...], dp1b, TN)

        @pl.when(pl.program_id(0) == pl.num_programs(0) - 1)
        def _():
            gwo_narrow[...] = gwo_acc[...].astype(gwo_narrow.dtype)
            copy = pltpu.make_async_copy(gwo_narrow, gwo_ref, sem.at[0])
            copy.start()
            copy.wait()

    row = pl.BlockSpec((tm, D_MODEL), lambda i: (i, 0))
    wide = pl.BlockSpec((tm, D_FF), lambda i: (i, 0))
    vec = pl.BlockSpec((1, D_MODEL), lambda i: (0, 0))
    narrow = lambda dtype: jax.ShapeDtypeStruct((t, D_MODEL), dtype)
    return pl.pallas_call(
        body, name="sublayers", grid=(t // tm,),
        in_specs=[pl.BlockSpec((tm, GROUP), lambda i: (i, 0)), pl.BlockSpec((tm, GROUP), lambda i: (i, 0)), row, row,
                  _resident((D_MODEL, D_MODEL)),
                  _resident((D_MODEL, D_FF)), _resident((D_FF, D_MODEL)), vec, vec, vec, vec],
        out_specs=[row, wide, wide, row, row, row, vec, vec, vec, vec,
                   pl.BlockSpec((SUBLANES, LANES), lambda i: (0, 0)), ANY],
        out_shape=[narrow(BF16), jax.ShapeDtypeStruct((t, D_FF), BF16), jax.ShapeDtypeStruct((t, D_FF), BF16),
                   narrow(BF16), narrow(F32), narrow(F32)]
                  + [jax.ShapeDtypeStruct((1, D_MODEL), F32)] * 4
                  + [jax.ShapeDtypeStruct((SUBLANES, LANES), F32), jax.ShapeDtypeStruct((D_MODEL, D_MODEL), BF16)],
        scratch_shapes=[pltpu.VMEM((D_MODEL, D_MODEL), F32), pltpu.VMEM((D_MODEL, D_MODEL), BF16),
                        pltpu.SemaphoreType.DMA((1,))],
        compiler_params=_params("arbitrary"),
    )(cat_h, cat_c, x, target, w_out, w_ff1, w_ff2, g1, b1, g2, b2)


def _hgrn_bwd(proj, do, states, lb_logits, after):
    t = proj.shape[1]
    tb = min(t, 512)
    ncb = tb // CHUNK
    nblk = t // tb

    def body(q_ref, f_ref, v_ref, do_ref, st_ref, lbl_ref, after_ref, dp_ref, dlbl_ref, ds_scr, dlb_scr):
        i = pl.program_id(0)

        @pl.when(i == 0)
        def _():
            ds_scr[...] = jnp.zeros_like(ds_scr)
            dlb_scr[...] = jnp.zeros_like(dlb_scr)

        lb, s1 = _lower_bound(lbl_ref[...])
        causal, anti = _chunk_masks()
        every = range(ncb)
        rows = [slice(c * CHUNK, (c + 1) * CHUNK) for c in every]
        q, v, do = ([ref[r, :] for r in rows] for ref in (q_ref, v_ref, do_ref))
        st = [st_ref[c] for c in every]
        gates = [_gates(f_ref[r, :], lb) for r in rows]
        sig, f, k = ([gt[n] for gt in gates] for n in (0, 1, 3))
        b = [_dot_exact(causal, gt[2]) for gt in gates]
        mid, last = [x[CHUNK // 2:CHUNK // 2 + 1, :] for x in b], [x[CHUNK - 1:CHUNK, :] for x in b]
        e_q = [jnp.exp(b[c] - mid[c]) for c in every]
        e_k = [jnp.exp(mid[c] - b[c]) for c in every]
        e_i = [jnp.exp(x) for x in b]
        e_s = [jnp.exp(last[c] - b[c]) for c in every]
        dec = [jnp.exp(x) for x in last]
        qt, kt, qi, ks = ([a[c] * e[c] for c in every] for a, e in ((q, e_q), (k, e_k), (q, e_i), (k, e_s)))

        def masked(a, b_):
            return [[jnp.where(causal, _dot(a_h, b_h, NT), 0.0) for a_h, b_h in zip(_heads(a[c]), _heads(b_[c]))]
                    for c in every]

        def with_scores(s, other, dims):
            return [jnp.concatenate([_dot(s_h, o_h, dims) for s_h, o_h in zip(s[c], _heads(other[c]))], axis=1)
                    for c in every]

        def per_head(dims, a, b_):
            return [_per_head(lambda a_h, b_h: _dot(a_h, b_h, dims), a[c], b_[c]) for c in every]

        scores, dscores = masked(qt, kt), masked(do, v)
        dqt, dkt, dv_intra = with_scores(dscores, kt, NN), with_scores(dscores, qt, TN), with_scores(scores, do, TN)
        dqi, update = per_head(NN, do, st), per_head(TN, do, qi)

        dst = ds_scr[...]
        dsts = [None] * ncb
        for c in reversed(every):
            dsts[c] = dst
            dst = dec[c] * dst + update[c]
        ds_scr[...] = dst

        dv_state, dks = per_head(NT, ks, dsts), per_head(NN, v, dsts)
        ddec = [jnp.sum(dsts[c] * st[c], axis=0, keepdims=True) for c in every]
        dq = [dqt[c] * e_q[c] + dqi[c] * e_i[c] for c in every]
        dk = [dkt[c] * e_k[c] + dks[c] * e_s[c] for c in every]
        db = [q[c] * dq[c] - k[c] * dk[c] for c in every]
        db_last = [jnp.sum(dks[c] * ks[c], axis=0, keepdims=True) + ddec[c] * dec[c] for c in every]
        dg = [_dot_exact(anti, db[c]) + db_last[c] for c in every]
        df = [dg[c] / f[c] - dk[c] for c in every]
        dlb_scr[...] += sum(jnp.sum(df[c] * (1.0 - sig[c]), axis=0, keepdims=True) for c in every)
        dfp = [df[c] * (1.0 - lb) * sig[c] * (1.0 - sig[c]) for c in every]
        dv = [dv_intra[c] + dv_state[c] for c in every]
        for n, parts in enumerate((dq, dfp, dv)):
            dp_ref[n] = jnp.concatenate(parts, axis=0).astype(dp_ref.dtype)

        @pl.when(i == nblk - 1)
        def _():
            dlb = dlb_scr[...]
            dlbl_ref[0:1, :] = dlb * lb * (1.0 - lb)
            dlbl_ref[1:2, :] = -dlb * lb * s1

    grp = lambda g: pl.BlockSpec((None, tb, GROUP), lambda i: (g, nblk - 1 - i, 0))
    vec = pl.BlockSpec((2, HGRN_WIDTH), lambda i: (0, 0))
    return pl.pallas_call(
        body, name="hgrn_bwd", grid=(nblk,),
        in_specs=[grp(0), grp(1), grp(2), pl.BlockSpec((tb, HGRN_WIDTH), lambda i: (nblk - 1 - i, 0)),
                  pl.BlockSpec((ncb, HEAD_DIM, HGRN_WIDTH), lambda i: (nblk - 1 - i, 0, 0)), vec, ANY],
        out_specs=[pl.BlockSpec((3, tb, HGRN_WIDTH), lambda i: (0, nblk - 1 - i, 0)), vec],
        out_shape=[jax.ShapeDtypeStruct((3, t, HGRN_WIDTH), BF16), jax.ShapeDtypeStruct((2, HGRN_WIDTH), F32)],
        scratch_shapes=[pltpu.VMEM((HEAD_DIM, HGRN_WIDTH), F32), pltpu.VMEM((1, HGRN_WIDTH), F32)],
        compiler_params=_params("arbitrary"),
    )(proj, proj, proj, do, states, lb_logits, after)


GRAD_TILE = 512
OUT_PARTS = 4


class _Side:
    def __init__(self, operands, in_specs, out_shape, out_specs, scratch, init, begin):
        self.operands, self.in_specs, self.out_shape, self.out_specs = operands, in_specs, out_shape, out_specs
        self.scratch, self.init, self.begin = scratch, init, begin


def _grad_w(name, operands, widths, shape, step, after=None, side=None):
    t = operands[0].shape[-2]
    tt = min(t, GRAD_TILE)
    n_in, n_steps = len(operands), t // tt
    in_specs = [pl.BlockSpec((tt, w), lambda k: (k, 0)) if a.ndim == 2 else
                pl.BlockSpec((a.shape[0], tt, w), lambda k: (0, k, 0)) for a, w in zip(operands, widths)]
    extra = [] if after is None else [after]
    s_in, s_out = (len(side.operands), len(side.out_shape)) if side else (0, 0)
    first_out = n_in + s_in + len(extra)

    def body(*refs):
        o_ref, side_outs = refs[first_out], refs[first_out + 1:first_out + 1 + s_out]
        acc, narrow, sem = refs[first_out + 1 + s_out:first_out + 4 + s_out]
        k = pl.program_id(0)

        @pl.when(k == 0)
        def _():
            acc[...] = jnp.zeros_like(acc)
            if side:
                side.init(side_outs)

        tick = side.begin(k, n_steps, refs[n_in:n_in + s_in], side_outs, refs[first_out + 4 + s_out:]) if side else None
        step(acc, *refs[:n_in], tick or (lambda j: None))

        @pl.when(k == n_steps - 1)
        def _():
            part = shape[0] // OUT_PARTS
            copies = []
            for p in range(OUT_PARTS):
                rows = pl.ds(p * part, part)
                narrow[rows, :] = acc[rows, :].astype(narrow.dtype)
                copies.append(pltpu.make_async_copy(narrow.at[rows, :], o_ref.at[rows, :], sem.at[p]))
                copies[-1].start()
            for cp in copies:
                cp.wait()

    outs = pl.pallas_call(
        body, name=name, grid=(n_steps,),
        in_specs=in_specs + (side.in_specs if side else []) + [ANY] * len(extra),
        out_specs=[ANY] + (side.out_specs if side else []),
        out_shape=[jax.ShapeDtypeStruct(shape, BF16)] + (side.out_shape if side else []),
        scratch_shapes=[pltpu.VMEM(shape, F32), pltpu.VMEM(shape, BF16), pltpu.SemaphoreType.DMA((OUT_PARTS,))]
                       + (side.scratch if side else []),
        compiler_params=_params("arbitrary"),
    )(*operands, *(side.operands if side else ()), *extra)
    return outs if side else outs[0]


def _dw_in(xb, dph, dog, dpc, w_in, dpre1, after):
    t = xb.shape[0]

    def step(acc, x_ref, dh_ref, dog_ref, dc_ref, tick):
        xv = x_ref[...]
        for g in range(N_GROUPS):
            part = dh_ref[g] if g < 3 else dog_ref[...] if g == 3 else dc_ref[g - 4]
            acc[:, g * GROUP:(g + 1) * GROUP] += _dot(xv, part, TN)
            tick(g, part)

    def begin(k, n_steps, ins, outs, scratch):
        w_ref, dp_ref = ins
        total = [ALPHA * dp_ref[...]]

        def tick(g, part):
            total[0] = total[0] + _dot(part, w_ref[:, g * GROUP:(g + 1) * GROUP], NT)
            if g == N_GROUPS - 1:
                outs[0][...] = total[0]

        return tick

    row = pl.BlockSpec((min(t, GRAD_TILE), D_MODEL), lambda k: (k, 0))
    side = _Side((w_in, dpre1), [_resident((D_MODEL, IN_COLS)), row], [jax.ShapeDtypeStruct((t, D_MODEL), F32)], [row],
                 [], lambda outs: None, begin)
    return _grad_w("dw_in", (xb, dph, dog, dpc), (D_MODEL, GROUP, GROUP, GROUP), (D_MODEL, IN_COLS), step, after, side)


def _strips_of(j, tt):
    per_tick = tt // GATE_STRIP // N_FF
    return [slice(s * GATE_STRIP, (s + 1) * GATE_STRIP) for s in range(j * per_tick, (j + 1) * per_tick)]


def _dw_ff1(h1b, da, dcat, o, proj, gate_norm_w, after):
    t = h1b.shape[0]
    tt = min(t, GRAD_TILE)

    def step(acc, h_ref, da_ref, tick):
        hv = h_ref[...]
        for j in range(N_FF):
            cols = slice(j * FF_BLOCK, (j + 1) * FF_BLOCK)
            acc[:, cols] += _dot(hv, da_ref[:, cols], TN)
            tick(j)

    def init(outs):
        outs[2][...] = jnp.zeros_like(outs[2])

    def begin(k, n_steps, ins, outs, scratch):
        do2_ref, o_ref, og_ref, gnw_ref = ins
        do_ref, dog_ref, dgnw_ref = outs
        total = [jnp.zeros((GATE_STRIP, GROUP), F32)]

        def tick(j):
            gnw = gnw_ref[...]
            for rows in _strips_of(j, tt):
                ov, og, do2 = o_ref[rows, :], og_ref[rows, :], do2_ref[rows, :]
                rs = _per_head(lambda o_h: jnp.broadcast_to(
                    lax.rsqrt(jnp.mean(o_h * o_h, axis=-1, keepdims=True) + EPS), o_h.shape), ov)
                on = ov * rs
                sg = _sigmoid(og)
                sil = og * sg
                don = do2 * gnw * sil
                total[0] = total[0] + do2 * on * sil
                dog_ref[rows, :] = (do2 * on * gnw * (sg * (1.0 + og * (1.0 - sg)))).astype(dog_ref.dtype)
                do_ref[rows, :] = rs * (don - on * _per_head(
                    lambda p_h: jnp.broadcast_to(jnp.mean(p_h, axis=-1, keepdims=True), p_h.shape), don * on))
            if j == N_FF - 1:
                dgnw_ref[...] += jnp.sum(total[0], axis=0, keepdims=True)

        return tick

    tile = pl.BlockSpec((tt, GROUP), lambda k: (k, 0))
    vec = pl.BlockSpec((1, GROUP), lambda k: (0, 0))
    side = _Side(
        (dcat, o, proj, gate_norm_w), [tile, tile, pl.BlockSpec((None, tt, GROUP), lambda k: (3, k, 0)), vec],
        [jax.ShapeDtypeStruct((t, HGRN_WIDTH), F32), jax.ShapeDtypeStruct((t, HGRN_WIDTH), BF16),
         jax.ShapeDtypeStruct((1, HGRN_WIDTH), F32)], [tile, tile, vec], [], init, begin)
    return _grad_w("dw_ff1", (h1b, da), (D_MODEL, D_FF), (D_MODEL, D_FF), step, after, side)


def _dw_ff2(r, dpre2b, dcat, bcu, conv_w):
    t = r.shape[0]
    tt = min(t, GRAD_TILE)
    hb = tt // SUBLANES
    halo = 2 * SUBLANES

    def step(acc, r_ref, d_ref, tick):
        dv = d_ref[...]
        for j in range(N_FF):
            rows = slice(j * FF_BLOCK, (j + 1) * FF_BLOCK)
            acc[rows, :] += _dot(r_ref[:, rows], dv, TN)
            tick(j)

    def init(outs):
        outs[1][...] = jnp.zeros_like(outs[1])

    def begin(k, n_steps, ins, outs, scratch):
        dy_ref, dyn_ref, b_ref, bn_ref, c_ref, u_ref, ch_ref, uh_ref, cw_ref = ins
        dp_ref, dcw_ref = outs
        zbuf, dbuf = scratch
        before = lambda ref: ref[SUBLANES:halo, :].astype(F32)
        zbuf[0:SUBLANES, :] = jnp.where(k > 0, before(ch_ref) * before(uh_ref), 0.0)
        zbuf[SUBLANES:SUBLANES + tt, :] = c_ref[...].astype(F32) * u_ref[...].astype(F32)
        dbuf[0:tt, :] = dy_ref[...] * b_ref[...].astype(F32)
        dbuf[tt:tt + SUBLANES, :] = jnp.where(k < n_steps - 1, dyn_ref[...] * bn_ref[0:SUBLANES, :].astype(F32), 0.0)
        totals = [jnp.zeros((GATE_STRIP, GROUP), F32) for _ in range(3)]

        def tick(j):
            cw = cw_ref[...]
            for rows in _strips_of(j, tt):
                at = lambda buf, shift: buf[shift + rows.start:shift + rows.stop, :]
                z, z1, z2 = at(zbuf, SUBLANES), at(zbuf, SUBLANES - 1), at(zbuf, SUBLANES - 2)
                dyc, d1, d2 = at(dbuf, 0), at(dbuf, 1), at(dbuf, 2)
                yc = cw[2:3, :] * z + cw[1:2, :] * z1 + cw[0:1, :] * z2
                dz = cw[2:3, :] * dyc + cw[1:2, :] * d1 + cw[0:1, :] * d2
                dp_ref[0, rows, :] = (dy_ref[rows, :] * yc).astype(dp_ref.dtype)
                dp_ref[1, rows, :] = (dz * u_ref[rows, :].astype(F32)).astype(dp_ref.dtype)
                dp_ref[2, rows, :] = (dz * c_ref[rows, :].astype(F32)).astype(dp_ref.dtype)
                for n, tap in enumerate((z2, z1, z)):
                    totals[n] = totals[n] + dyc * tap
            if j == N_FF - 1:
                for n in range(3):
                    dcw_ref[n:n + 1, :] += jnp.sum(totals[n], axis=0, keepdims=True)

        return tick

    grp = lambda g: pl.BlockSpec((None, tt, GROUP), lambda k: (g, k, 0))
    prev = lambda g: pl.BlockSpec((None, halo, GROUP), lambda k: (g, jnp.maximum(k * (tt // halo) - 1, 0), 0))
    nxt = lambda g: pl.BlockSpec((None, halo, GROUP), lambda k: (g, jnp.minimum((k + 1) * (tt // halo), t // halo - 1), 0))
    nxt_row = lambda k: jnp.minimum((k + 1) * hb, t // SUBLANES - 1)
    whole = pl.BlockSpec((3, CONV_WIDTH), lambda k: (0, 0))
    side = _Side(
        (dcat, dcat, bcu, bcu, bcu, bcu, bcu, bcu, conv_w),
        [pl.BlockSpec((tt, GROUP), lambda k: (k, 1)), pl.BlockSpec((SUBLANES, GROUP), lambda k: (nxt_row(k), 1)),
         grp(0), nxt(0), grp(1), grp(2), prev(1), prev(2), whole],
        [jax.ShapeDtypeStruct((3, t, CONV_WIDTH), BF16), jax.ShapeDtypeStruct((3, CONV_WIDTH), F32)],
        [pl.BlockSpec((3, tt, GROUP), lambda k: (0, k, 0)), whole],
        [pltpu.VMEM((tt + SUBLANES, GROUP), F32), pltpu.VMEM((tt + SUBLANES, GROUP), F32)], init, begin)
    return _grad_w("dw_ff2", (r, dpre2b), (D_FF, D_MODEL), (D_FF, D_MODEL), step, side=side)


def _place():
    x, y, c = lax.axis_index("x"), lax.axis_index("y"), lax.axis_index("c")
    return x, y, c, 2 * x + y


def _other_chips(x, y):
    return [(1 - x, y), (x, 1 - y), (1 - x, 1 - y)]


def _place_shard(name, w, chip, cols_sharded, after=None):
    rows, cols = w.shape
    tr = min(rows, 256)
    nb = rows // tr
    full = (rows, cols * N_CHIPS) if cols_sharded else (rows * N_CHIPS, cols)
    out_map = (lambda i, s: (i, s[0])) if cols_sharded else (lambda i, s: (s[0] * nb + i, 0))

    def body(s_ref, w_ref, *rest):
        rest[-1][...] = w_ref[...].astype(rest[-1].dtype)

    extra = [] if after is None else [after]
    return pl.pallas_call(
        body, name=name,
        grid_spec=pltpu.PrefetchScalarGridSpec(
            num_scalar_prefetch=1, grid=(nb,),
            in_specs=[pl.BlockSpec((tr, cols), lambda i, s: (i, 0))] + [ANY] * len(extra),
            out_specs=pl.BlockSpec((tr, cols), out_map)),
        out_shape=jax.ShapeDtypeStruct(full, BF16),
        compiler_params=_params("parallel"),
    )(chip, w, *extra)


HBM = pl.BlockSpec(memory_space=pltpu.HBM)
SEM = pl.BlockSpec(memory_space=pltpu.SEMAPHORE)
EFFECT = pltpu.SideEffectType.DATAFLOW_SIDE_EFFECTING


PEER_SETS = {
    "sibling": (0, lambda x, y, c: [(x, y, 1 - c)]),
    "chips": (1, lambda x, y, c: [(1 - x, y, c), (x, 1 - y, c), (1 - x, 1 - y, c)]),
    "neighbours": (2, lambda x, y, c: [(1 - x, y, c), (x, 1 - y, c)]),
}


class _Split:
    def __init__(self, name, arrays, plan, others=(), peers=None):
        n_own, arrays = len(arrays), (*arrays, *others)
        n, n_copies = len(arrays), plan.count
        self.name, self.plan, self.n = name, plan, n_own
        barrier_id, peer_ids = PEER_SETS[peers] if peers else (None, None)

        def body(*refs):
            if peers:
                x, y, c, _ = _place()
                barrier = pltpu.get_barrier_semaphore()
                for peer in peer_ids(x, y, c):
                    pl.semaphore_signal(barrier, inc=1, device_id=peer, device_id_type=MESH)
                pl.semaphore_wait(barrier, len(peer_ids(0, 0, 0)))
            send_sems, recv_sems, token = refs[n], refs[n + 1], refs[-1]
            for k, (src, dst, to) in enumerate(plan(refs[:n])):
                pltpu.make_async_remote_copy(src_ref=src, dst_ref=dst, send_sem=send_sems.at[k], recv_sem=recv_sems.at[k],
                                             device_id=to, device_id_type=MESH).start()
            token[...] = jnp.zeros_like(token)

        outs = pl.pallas_call(
            body, name=name + "_start",
            out_shape=(pltpu.SemaphoreType.DMA((n_copies,)), pltpu.SemaphoreType.DMA((n_copies,)),
                       *[pltpu.HBM(a.shape, a.dtype) for a in arrays], jax.ShapeDtypeStruct((SUBLANES, LANES), F32)),
            in_specs=(HBM,) * n, out_specs=(SEM, SEM) + (HBM,) * n + (pl.BlockSpec(memory_space=pltpu.VMEM),),
            input_output_aliases={i: 2 + i for i in range(n)},
            compiler_params=pltpu.CompilerParams(has_side_effects=EFFECT, collective_id=barrier_id),
        )(*[pltpu.with_memory_space_constraint(a, pltpu.HBM) for a in arrays])
        self.sems, self.arrays, self.others, self.token = outs[:2], outs[2:2 + n_own], outs[2 + n_own:2 + n], outs[-1]

    def wait(self, after):
        n, plan = self.n, self.plan

        def body(*refs):
            send_sems, recv_sems = refs[n], refs[n + 1]
            for k, (src, dst, to) in enumerate(plan(refs[:n])):
                cp = pltpu.make_async_remote_copy(src_ref=src, dst_ref=dst, send_sem=send_sems.at[k],
                                                  recv_sem=recv_sems.at[k], device_id=to, device_id_type=MESH)
                cp.wait_send()
                cp.wait_recv()

        return pl.pallas_call(
            body, name=self.name + "_wait", out_shape=tuple(pltpu.HBM(a.shape, a.dtype) for a in self.arrays),
            in_specs=(HBM,) * n + (SEM, SEM, ANY), out_specs=(HBM,) * n, input_output_aliases={i: i for i in range(n)},
            compiler_params=pltpu.CompilerParams(has_side_effects=EFFECT),
        )(*self.arrays, *self.sems, after)


COLS_SHARDED = (True, False, True, False)
HALF_SHAPES = [(D_MODEL // 2, IN_COLS), (D_MODEL, D_MODEL // 2), (D_MODEL // 2, D_FF), (D_FF, D_MODEL // 2)]
PIECE_SHAPES = [(D_MODEL // 2, IN_COLS // N_CHIPS), (D_MODEL // N_CHIPS, D_MODEL // 2),
                (D_MODEL // 2, D_FF // N_CHIPS), (D_FF // N_CHIPS, D_MODEL // 2)]


def _shard_view(kind, ref, chip):
    if COLS_SHARDED[kind]:
        n = ref.shape[1] // N_CHIPS
        return ref.at[:, pl.ds(chip * n, n)]
    n = ref.shape[0] // N_CHIPS
    return ref.at[pl.ds(chip * n, n), :]


def _half_view(kind, ref, h):
    if COLS_SHARDED[kind]:
        n = ref.shape[0] // 2
        return ref.at[pl.ds(h * n, n), :]
    n = ref.shape[1] // 2
    return ref.at[:, pl.ds(h * n, n)]


def _plan(count):
    def mark(fn):
        fn.count = count
        return fn
    return mark


def _shard_rows_view(kind, ref, chip, part, n_parts):
    if COLS_SHARDED[kind]:
        m, n = ref.shape[0] // n_parts, ref.shape[1] // N_CHIPS
        return ref.at[pl.ds(part * m, m), pl.ds(chip * n, n)]
    m = ref.shape[0] // N_CHIPS // n_parts
    return ref.at[pl.ds((n_parts * chip + part) * m, m), :]


def _shard_half_view(kind, ref, chip, h):
    return _shard_rows_view(kind, ref, chip, h, 2)


def _gather_over_ici(kinds, weights):
    @_plan(2 * len(kinds))
    def plan(refs):
        x, y, c, me = _place()
        mine = [_shard_half_view(kind, ref, me, c) for kind, ref in zip(kinds, refs)]
        return [(v, v, to) for v in mine for to in ((1 - x, y, c), (x, 1 - y, c))]

    return _Split("gather_ici_" + "".join(map(str, kinds)), tuple(weights), plan, peers="neighbours")


def _relay_over_ici(kinds, weights, others=()):
    @_plan(2 * len(kinds))
    def plan(refs):
        x, y, c, _ = _place()
        x_nbr, y_nbr = 2 * (1 - x) + y, 2 * x + (1 - y)
        out = []
        for kind, ref in zip(kinds, refs):
            first, second = (_shard_rows_view(kind, ref, chip, 2 * c + q, 4) for q, chip in ((0, x_nbr), (1, y_nbr)))
            out += [(first, first, (x, 1 - y, c)), (second, second, (1 - x, y, c))]
        return out

    return _Split("relay_ici_" + "".join(map(str, kinds)), tuple(weights), plan, others, peers="neighbours")


def _gather_w_in_over_ici(w_in, conv4):
    @_plan(6)
    def plan(refs):
        x, y, c, me = _place()
        half, conv = _shard_half_view(0, refs[0], me, c), refs[1].at[me]
        return [(v, v, (px, py, c)) for v in (half, conv) for px, py in _other_chips(x, y)]

    return _Split("gather_w_in_ici", (w_in, conv4), plan, peers="chips")


def _gather_over_d2d(kinds, weights):
    @_plan(3 * len(kinds))
    def plan(refs):
        x, y, c, _ = _place()
        got = [_shard_half_view(kind, ref, 2 * px + py, c) for kind, ref in zip(kinds, refs)
               for px, py in _other_chips(x, y)]
        return [(v, v, (x, y, 1 - c)) for v in got]

    return _Split("gather_d2d_" + "".join(map(str, kinds)), tuple(weights), plan, peers="sibling")


def _swap_halves(kinds, grads):
    @_plan(len(kinds))
    def plan(refs):
        x, y, c, _ = _place()
        return [(_half_view(kind, g, 1 - c), land, (x, y, 1 - c))
                for kind, g, land in zip(kinds, refs[:len(kinds)], refs[len(kinds):])]

    lands = [lax.empty(HALF_SHAPES[kind], g.dtype) for kind, g in zip(kinds, grads)]
    return _Split("swap_halves_" + "".join(map(str, kinds)), (*grads, *lands), plan, peers="sibling")


def _block_rows(cols, elements):
    return 1 << ((elements // cols).bit_length() - 1)


def _add_half(name, g, recv, core, rows_split):
    shape = recv.shape
    tr = min(shape[0], _block_rows(shape[1], 1 << 20))
    nb = shape[0] // tr

    def body(c_ref, g_ref, r_ref, o_ref):
        o_ref[...] = (g_ref[...].astype(F32) + r_ref[...].astype(F32)).astype(o_ref.dtype)

    g_map = (lambda i, c_ref: (c_ref[0] * nb + i, 0)) if rows_split else (lambda i, c_ref: (i, c_ref[0]))
    blk = pl.BlockSpec((tr, shape[1]), lambda i, c_ref: (i, 0))
    return pl.pallas_call(
        body, name=name,
        grid_spec=pltpu.PrefetchScalarGridSpec(
            num_scalar_prefetch=1, grid=(nb,),
            in_specs=[pl.BlockSpec((tr, shape[1]), g_map), blk], out_specs=blk),
        out_shape=jax.ShapeDtypeStruct(shape, BF16),
        compiler_params=_params("parallel"),
    )(core, g, recv)


def _exchange_pieces(kinds, halves, pack=None):
    n_p, n = N_CHIPS - 1, len(kinds)

    @_plan(n_p * n + (0 if pack is None else N_DEV - 1))
    def plan(refs):
        x, y, c, _ = _place()
        copies = []
        if pack is not None:
            me = 4 * x + 2 * y + c
            peers = [((1 - x) if m & 4 else x, (1 - y) if m & 2 else y, (1 - c) if m & 1 else c) for m in range(1, N_DEV)]
            copies += [(refs[2 * n], refs[2 * n + 1].at[me], peer) for peer in peers]
        return copies + [(_shard_view(kind, half, 2 * px + py), land.at[j], (px, py, c))
                         for j, (px, py) in enumerate(_other_chips(x, y))
                         for kind, half, land in zip(kinds, refs[:n], refs[n:2 * n])]

    lands = [lax.empty((n_p,) + PIECE_SHAPES[kind], BF16) for kind in kinds]
    small = () if pack is None else (pack, lax.empty((N_DEV,) + pack.shape, F32))
    return _Split("exchange_pieces_" + "".join(map(str, kinds)), (*halves, *lands, *small), plan,
                  peers="chips" if pack is None else None)


def _sum_pieces(name, half, slots, place, rows_split, after):
    n_p, rows, cols = slots.shape
    tr = min(rows, _block_rows(cols, 1 << 19))
    nb = rows // tr
    if rows_split:
        own_map = lambda i, s: (i, s[0])
        out_map = lambda i, s: (s[1] * nb + i, 0)
        shard = (2 * rows, cols)
    else:
        own_map = lambda i, s: (s[0] * nb + i, 0)
        out_map = lambda i, s: (i, s[1])
        shard = (rows, 2 * cols)

    def body(s_ref, own_ref, slot_ref, after_ref, o_ref):
        total = own_ref[...].astype(F32)
        for j in range(n_p):
            total = total + slot_ref[j].astype(F32)
        o_ref[...] = total

    return pl.pallas_call(
        body, name=name,
        grid_spec=pltpu.PrefetchScalarGridSpec(
            num_scalar_prefetch=1, grid=(nb,),
            in_specs=[pl.BlockSpec((tr, cols), own_map), pl.BlockSpec((n_p, tr, cols), lambda i, s: (0, i, 0)), ANY],
            out_specs=pl.BlockSpec((tr, cols), out_map)),
        out_shape=jax.ShapeDtypeStruct(shard, F32),
        compiler_params=_params("parallel"),
    )(place, half, slots, after)


def _join_halves(kinds, shards):
    @_plan(len(kinds))
    def plan(refs):
        x, y, c, _ = _place()
        return [(_half_view(kind, g, c), _half_view(kind, g, c), (x, y, 1 - c)) for kind, g in zip(kinds, refs)]

    return _Split("join_halves_" + "".join(map(str, kinds)), tuple(shards), plan, peers="sibling")


N_DEV = 8


def _sum_shared(pack, land, device):
    def body(d_ref, p_ref, l_ref, o_ref):
        me = d_ref[0]
        total = jnp.where(me == 0, p_ref[...], l_ref[0])
        for d in range(1, N_DEV):
            total = total + jnp.where(me == d, p_ref[...], l_ref[d])
        o_ref[...] = total

    return pl.pallas_call(
        body, name="sum_shared",
        grid_spec=pltpu.PrefetchScalarGridSpec(
            num_scalar_prefetch=1, grid=(1,),
            in_specs=[pl.BlockSpec(pack.shape, lambda i, d: (0, 0)), pl.BlockSpec(land.shape, lambda i, d: (0, 0, 0))],
            out_specs=pl.BlockSpec(pack.shape, lambda i, d: (0, 0))),
        out_shape=jax.ShapeDtypeStruct(pack.shape, F32),
    )(device, pack, land)


def _adamw(name, w, g, m, v, after=None):
    rows, cols = w.shape
    tr = min(rows, 256)
    extra = [] if after is None else [after]

    def body(w_ref, g_ref, m_ref, v_ref, *rest):
        go_ref, d_ref, nm_ref, nv_ref = rest[-4:]
        g = g_ref[...]
        go_ref[...] = g
        d_ref[...], nm_ref[...], nv_ref[...] = _adam_step(w_ref[...], g, m_ref[...], v_ref[...])

    blk = pl.BlockSpec((tr, cols), lambda i: (i, 0))
    return pl.pallas_call(
        body, name=name, grid=(rows // tr,), in_specs=[blk] * 4 + [ANY] * len(extra), out_specs=[blk] * 4,
        out_shape=[jax.ShapeDtypeStruct(w.shape, F32)] * 4,
        compiler_params=_params("parallel"),
    )(w, g, m, v, *extra)


def _adam_step(w, g, m, v):
    nm = ADAM_B1 * m + (1.0 - ADAM_B1) * g
    nv = ADAM_B2 * v + (1.0 - ADAM_B2) * jnp.square(g)
    m_hat = nm * (1.0 / (1.0 - ADAM_B1 ** ADAM_STEP))
    v_hat = nv * (1.0 / (1.0 - ADAM_B2 ** ADAM_STEP))
    return -ADAM_LR * (m_hat / (jnp.sqrt(v_hat) + ADAM_EPS) + ADAM_WD * w), nm, nv


def _adamw_small(tot, chip, weights, ms, vs, after):
    n, half = len(weights), D_MODEL // 2

    def body(chip_ref, tot_ref, *refs):
        ins, outs = refs[:3 * n], refs[3 * n + 1:]
        tot = tot_ref[...]
        conv_all = jnp.concatenate([tot[5:6, half:], tot[6:7, :half], tot[6:7, half:]], axis=0)
        conv = sum(jnp.where(chip_ref[0] == s, conv_all[:, s * LANES:(s + 1) * LANES], 0.0) for s in range(N_CHIPS))
        grads = [jnp.concatenate([tot[4:5, :half], tot[4:5, half:]], axis=0), tot[5:6, :half], conv,
                 tot[0:1], tot[1:2], tot[2:3], tot[3:4]]
        for k, g in enumerate(grads):
            delta, nm, nv = _adam_step(ins[k][...], g, ins[n + k][...], ins[2 * n + k][...])
            outs[k][...], outs[n + k][...], outs[2 * n + k][...], outs[3 * n + k][...] = g, delta, nm, nv
        outs[4 * n][...] = tot[7:8, 0:1]

    whole = lambda a: pl.BlockSpec(a.shape, lambda i, s: (0,) * a.ndim)
    arrays = (*weights, *ms, *vs)
    loss = jax.ShapeDtypeStruct((1, 1), F32)
    return pl.pallas_call(
        body, name="adamw_small",
        grid_spec=pltpu.PrefetchScalarGridSpec(
            num_scalar_prefetch=1, grid=(1,), in_specs=[whole(tot)] + [whole(a) for a in arrays] + [ANY],
            out_specs=[whole(a) for a in weights] * 4 + [whole(loss)]),
        out_shape=[jax.ShapeDtypeStruct(a.shape, F32) for a in weights] * 4 + [loss],
    )(chip, tot, *arrays, after)


def kernel(x, w_in, lb_logits, gate_norm_w, conv_w, w_out, ln1_g, ln1_b, w_ff1, w_ff2, ln2_g, ln2_b, loss_target, m_w_in, m_lb_logits, m_gate_norm_w, m_conv_w, m_w_out, m_ln1_g, m_ln1_b, m_w_ff1, m_w_ff2, m_ln2_g, m_ln2_b, v_w_in, v_lb_logits, v_gate_norm_w, v_conv_w, v_w_out, v_ln1_g, v_ln1_b, v_w_ff1, v_w_ff2, v_ln2_g, v_ln2_b):
    xs, tgt = x[0], loss_target[0]
    chip = 2 * lax.axis_index("x") + lax.axis_index("y")
    core = lax.axis_index("c").astype(jnp.int32).reshape(1)
    chip1 = chip.astype(jnp.int32).reshape(1)
    place = jnp.concatenate([chip1, core])

    conv4 = lax.dynamic_update_slice(jnp.zeros((N_CHIPS,) + conv_w.shape[1:], F32), conv_w, (chip, 0, 0))
    ici_in = _gather_w_in_over_ici(_place_shard("place_w_in", w_in[0], chip1, True), conv4)
    rest = (1, 2, 3)
    ici_rest = _gather_over_ici(rest, (_place_shard("place_w_out", w_out[0], chip1, False, after=ici_in.token),
                                       _place_shard("place_w_ff1", w_ff1[0], chip1, True, after=ici_in.token),
                                       _place_shard("place_w_ff2", w_ff2[0], chip1, False, after=ici_in.token)))
    wb_in, cv4 = ici_in.wait(ici_rest.token)
    d2d_in = _gather_over_d2d((0,), (wb_in,))
    wb_in, = d2d_in.wait(d2d_in.token)
    conv_full = cv4.transpose(1, 0, 2).reshape(3, CONV_WIDTH)

    proj, bcu, xb, cat_c = _in_proj(xs, wb_in, conv_full, ici_rest.token)
    relay_rest = _relay_over_ici(rest, ici_rest.wait(proj))
    o, states = _hgrn_fwd(proj, lb_logits, relay_rest.token)
    d2d_rest = _gather_over_d2d(rest, relay_rest.wait(o))
    cat_h = _gate_fwd(proj, o, gate_norm_w, d2d_rest.token)
    wb_out, wb_ff1, wb_ff2 = d2d_rest.wait(cat_h)

    (h1b, r, da, dpre2b, dpre1, dcat, g_ln1_g, g_ln1_b, g_ln2_g, g_ln2_b, loss8, g_out_local) = _sublayers(
        cat_h, cat_c, xs, tgt, wb_out, wb_ff1, wb_ff2, ln1_g, ln1_b, ln2_g, ln2_b)

    names = ("w_in", "w_out", "w_ff1", "w_ff2")

    def add_halves(kinds, grads, lands):
        return [_add_half("add_half_" + names[k], g, ld, core, COLS_SHARDED[k]) for k, g, ld in zip(kinds, grads, lands)]

    def sum_pieces(kinds, halves, lands, after):
        return [_sum_pieces("sum_pieces_" + names[k], h, ld, place, COLS_SHARDED[k], after)
                for k, h, ld in zip(kinds, halves, lands)]

    early = (1, 2, 3)
    g_ff2_local, dpc, g_conv = _dw_ff2(r, dpre2b, dcat, bcu, conv_full)
    swap_a = _swap_halves((1, 3), (g_out_local, g_ff2_local))
    g_ff1_local, do, dog, g_gnw = _dw_ff1(h1b, da, dcat, o, proj, gate_norm_w, swap_a.token)
    swap_b = _swap_halves((2,), (g_ff1_local,))
    swapped_a = swap_a.wait(swap_b.token)
    halves_a = add_halves((1, 3), swapped_a[:2], swapped_a[2:])
    swapped_b = swap_b.wait(halves_a[1])
    halves = (halves_a[0], *add_halves((2,), swapped_b[:1], swapped_b[1:]), halves_a[1])
    exch = _exchange_pieces(early, halves)
    dph, g_lbl = _hgrn_bwd(proj, do, states, lb_logits, exch.token)
    g_in_local, grad_x = _dw_in(xb, dph, dog, dpc, wb_in, dpre1, dph)

    late = (0,)
    swap = _swap_halves(late, (g_in_local,))
    exchanged = exch.wait(swap.token)
    pack = jnp.concatenate([
        g_ln1_g, g_ln1_b, g_ln2_g, g_ln2_b,
        jnp.concatenate([g_lbl[0:1], g_lbl[1:2]], axis=1),
        jnp.concatenate([g_gnw, g_conv[0:1]], axis=1),
        jnp.concatenate([g_conv[1:2], g_conv[2:3]], axis=1),
        jnp.concatenate([loss8[0:1], jnp.zeros((1, D_MODEL - LANES), F32)], axis=1)], axis=0)
    join_a = _join_halves((2,), sum_pieces((2,), exchanged[1:2], exchanged[4:5], swap.token))
    swapped = swap.wait(join_a.token)
    exch = _exchange_pieces(late, add_halves(late, swapped[:1], swapped[1:]), pack)
    join_b = _join_halves((1, 3), sum_pieces((1, 3), exchanged[0:3:2], exchanged[3:6:2], exch.token))
    g_w_ff1, = join_a.wait(join_b.token)
    g_w_ff1, d_ff1, nm_ff1, nv_ff1 = _adamw("adamw_w_ff1", w_ff1[0], g_w_ff1, m_w_ff1[0], v_w_ff1[0])
    g_w_out, g_w_ff2 = join_b.wait(d_ff1)
    g_w_ff2, d_ff2, nm_ff2, nv_ff2 = _adamw("adamw_w_ff2", w_ff2[0], g_w_ff2, m_w_ff2[0], v_w_ff2[0])
    g_w_out, d_out, nm_out, nv_out = _adamw("adamw_w_out", w_out[0], g_w_out, m_w_out[0], v_w_out[0], d_ff2)
    exchanged = exch.wait(d_out)
    tot = _sum_shared(exchanged[2], exchanged[3], 2 * chip1 + core)
    join = _join_halves(late, sum_pieces(late, exchanged[:1], exchanged[1:2], tot))
    small = ("lb_logits", "gate_norm_w", "conv_w", "ln1_g", "ln1_b", "ln2_g", "ln2_b")
    small_out = _adamw_small(
        tot, chip1, (lb_logits, gate_norm_w, conv_w[0], ln1_g, ln1_b, ln2_g, ln2_b),
        (m_lb_logits, m_gate_norm_w, m_conv_w[0], m_ln1_g, m_ln1_b, m_ln2_g, m_ln2_b),
        (v_lb_logits, v_gate_norm_w, v_conv_w[0], v_ln1_g, v_ln1_b, v_ln2_g, v_ln2_b), join.token)
    g_w_in, = join.wait(small_out[0])
    g_w_in, d_in, nm_in, nv_in = _adamw("adamw_w_in", w_in[0], g_w_in, m_w_in[0], v_w_in[0])
    loss = small_out[4 * len(small)][0, 0]

    def results(n_kind, large):
        out = dict(zip(small, small_out[n_kind * len(small):(n_kind + 1) * len(small)]))
        out["conv_w"] = out["conv_w"][None]
        out.update({name: a[None] for name, a in zip(("w_in", "w_out", "w_ff1", "w_ff2"), large)})
        return [out[name] for name in ("w_in", "lb_logits", "gate_norm_w", "conv_w", "w_out", "ln1_g", "ln1_b",
                                       "w_ff1", "w_ff2", "ln2_g", "ln2_b")]

    return (loss, grad_x[None], *results(0, (g_w_in, g_w_out, g_w_ff1, g_w_ff2)),
            *results(1, (d_in, d_out, d_ff1, d_ff2)), *results(2, (nm_in, nm_out, nm_ff1, nm_ff2)),
            *results(3, (nv_in, nv_out, nv_ff1, nv_ff2)))
```

```python
import jax
import jax.numpy as jnp
from jax import lax
from jax.experimental import pallas as pl
from jax.experimental.pallas import tpu as pltpu

F32 = jnp.float32
BF16 = jnp.bfloat16
MXU_DTYPE = jnp.bfloat16

D_MODEL = 1024
HGRN_WIDTH = 512
HEAD_DIM = 128
N_HEADS = 4
CONV_WIDTH = 512
CHUNK = 64
D_FF = 4096
IN_COLS = 3584
GROUP = 512
N_GROUPS = IN_COLS // GROUP
ALPHA = 2.0 ** 0.25
EPS = 1e-5
N_CHIPS = 4
ADAM_LR, ADAM_B1, ADAM_B2, ADAM_EPS, ADAM_WD, ADAM_STEP = 0.001, 0.9, 0.999, 1e-08, 0.01, 10

LANES = 128
SUBLANES = 8
VMEM_LIMIT = 56 * 1024 * 1024
FF_BLOCK = 1024
N_FF = D_FF // FF_BLOCK
GATE_STRIP = 64

NN = (((1,), (0,)), ((), ()))
NT = (((1,), (1,)), ((), ()))
TN = (((0,), (0,)), ((), ()))
MESH = pl.DeviceIdType.MESH
ANY = pl.BlockSpec(memory_space=pl.ANY)


def _dot(a, b, dims):
    return lax.dot_general(a.astype(MXU_DTYPE), b.astype(MXU_DTYPE), dims, preferred_element_type=F32)


def _dot_exact(ones, v):
    ones = ones.astype(jnp.bfloat16)
    hi = v.astype(jnp.bfloat16)
    rest = v - hi.astype(F32)
    mid = rest.astype(jnp.bfloat16)
    low = (rest - mid.astype(F32)).astype(jnp.bfloat16)
    return sum(lax.dot_general(ones, part, NN, preferred_element_type=F32) for part in (hi, mid, low))


def _params(*sem):
    return pltpu.CompilerParams(dimension_semantics=sem, vmem_limit_bytes=VMEM_LIMIT)


def _resident(shape):
    return pl.BlockSpec(shape, lambda *_: (0,) * len(shape), pipeline_mode=pl.Buffered(1))


def _sigmoid(v):
    return 1.0 / (1.0 + jnp.exp(-v))


def _lower_bound(lbl):
    m = jnp.max(lbl, axis=0, keepdims=True)
    e = jnp.exp(lbl - m)
    s = e / jnp.sum(e, axis=0, keepdims=True)
    return s[0:1, :], s[1:2, :]


def _heads(v):
    return [v[:, h * HEAD_DIM:(h + 1) * HEAD_DIM] for h in range(N_HEADS)]


def _per_head(fn, *arrays):
    return jnp.concatenate([fn(*parts) for parts in zip(*map(_heads, arrays))], axis=1)


def _in_proj(x, w_in, conv_w, after):
    t = x.shape[0]
    tm = min(t, 512)

    def body(x_ref, w_ref, cw_ref, after_ref, o_ref, bcu_ref, xb_ref, y_ref, zbuf):
        @pl.when(pl.program_id(0) == 0)
        def _():
            zbuf[tm:tm + SUBLANES, :] = jnp.zeros((SUBLANES, CONV_WIDTH), F32)

        xb = x_ref[...].astype(xb_ref.dtype)
        xb_ref[...] = xb
        group = lambda g: _dot(xb, w_ref[:, g * GROUP:(g + 1) * GROUP], NN)
        for g in range(4):
            o_ref[g] = group(g)
        b_gate, c_gate, u = group(4), group(5), group(6)
        for n, part in enumerate((b_gate, c_gate, u)):
            bcu_ref[n] = part.astype(bcu_ref.dtype)
        zbuf[0:SUBLANES, :] = zbuf[tm:tm + SUBLANES, :]
        zbuf[SUBLANES:SUBLANES + tm, :] = c_gate * u
        cw = cw_ref[...]
        at = lambda shift: zbuf[shift:shift + tm, :]
        conv = cw[2:3, :] * at(SUBLANES) + cw[1:2, :] * at(SUBLANES - 1) + cw[0:1, :] * at(SUBLANES - 2)
        y_ref[...] = (b_gate * conv).astype(y_ref.dtype)

    return pl.pallas_call(
        body, name="in_proj", grid=(t // tm,),
        in_specs=[pl.BlockSpec((tm, D_MODEL), lambda i: (i, 0)), _resident((D_MODEL, IN_COLS)),
                  pl.BlockSpec((3, CONV_WIDTH), lambda i: (0, 0)), ANY],
        out_specs=[pl.BlockSpec((4, tm, GROUP), lambda i: (0, i, 0)), pl.BlockSpec((3, tm, GROUP), lambda i: (0, i, 0)),
                   pl.BlockSpec((tm, D_MODEL), lambda i: (i, 0)), pl.BlockSpec((tm, CONV_WIDTH), lambda i: (i, 0))],
        out_shape=[jax.ShapeDtypeStruct((4, t, GROUP), F32), jax.ShapeDtypeStruct((3, t, GROUP), BF16),
                   jax.ShapeDtypeStruct((t, D_MODEL), BF16), jax.ShapeDtypeStruct((t, CONV_WIDTH), BF16)],
        scratch_shapes=[pltpu.VMEM((tm + SUBLANES, CONV_WIDTH), F32)],
        compiler_params=_params("arbitrary"),
    )(x, w_in, conv_w, after)


def _gates(fp, lb):
    sig = _sigmoid(fp)
    f = lb + (1.0 - lb) * sig
    return sig, f, jnp.log(f), 1.0 - f


def _chunk_masks():
    row = lax.broadcasted_iota(jnp.int32, (CHUNK, CHUNK), 0)
    col = lax.broadcasted_iota(jnp.int32, (CHUNK, CHUNK), 1)
    return row >= col, row <= col


def _hgrn_fwd(proj, lb_logits, after):
    t = proj.shape[1]
    tb = min(t, 512)
    ncb = tb // CHUNK

    def body(q_ref, f_ref, v_ref, lbl_ref, after_ref, o_ref, st_ref, s_scr):
        @pl.when(pl.program_id(0) == 0)
        def _():
            s_scr[...] = jnp.zeros_like(s_scr)

        lb, _ = _lower_bound(lbl_ref[...])
        causal, _ = _chunk_masks()

        every = range(ncb)
        rows = [slice(c * CHUNK, (c + 1) * CHUNK) for c in every]
        q, v = [q_ref[r, :] for r in rows], [v_ref[r, :] for r in rows]
        gates = [_gates(f_ref[r, :], lb) for r in rows]
        k = [gt[3] for gt in gates]
        b = [_dot_exact(causal, gt[2]) for gt in gates]
        mid, last = [x[CHUNK // 2:CHUNK // 2 + 1, :] for x in b], [x[CHUNK - 1:CHUNK, :] for x in b]
        qt = [q[c] * jnp.exp(b[c] - mid[c]) for c in every]
        kt = [k[c] * jnp.exp(mid[c] - b[c]) for c in every]
        qi = [q[c] * jnp.exp(b[c]) for c in every]
        ks = [k[c] * jnp.exp(last[c] - b[c]) for c in every]
        dec = [jnp.exp(x) for x in last]
        scores = [[jnp.where(causal, _dot(a, b_, NT), 0.0) for a, b_ in zip(_heads(qt[c]), _heads(kt[c]))] for c in every]
        intra = [[_dot(s, v_h, NN) for s, v_h in zip(scores[c], _heads(v[c]))] for c in every]
        update = [_per_head(lambda v_h, ks_h: _dot(v_h, ks_h, TN), v[c], ks[c]) for c in every]

        st = s_scr[...]
        states = []
        for c in every:
            states.append(st)
            st_ref[c] = st
            st = dec[c] * st + update[c]
        s_scr[...] = st

        o_ref[...] = jnp.concatenate(
            [jnp.concatenate([i_h + _dot(qi_h, st_h, NT) for i_h, qi_h, st_h in
                              zip(intra[c], _heads(qi[c]), _heads(states[c]))], axis=1) for c in every], axis=0)

    grp = lambda g: pl.BlockSpec((None, tb, GROUP), lambda i: (g, i, 0))
    return pl.pallas_call(
        body, name="hgrn_fwd", grid=(t // tb,),
        in_specs=[grp(0), grp(1), grp(2), pl.BlockSpec((2, HGRN_WIDTH), lambda i: (0, 0)), ANY],
        out_specs=[pl.BlockSpec((tb, HGRN_WIDTH), lambda i: (i, 0)),
                   pl.BlockSpec((ncb, HEAD_DIM, HGRN_WIDTH), lambda i: (i, 0, 0))],
        out_shape=[jax.ShapeDtypeStruct((t, HGRN_WIDTH), F32),
                   jax.ShapeDtypeStruct((t // CHUNK, HEAD_DIM, HGRN_WIDTH), F32)],
        scratch_shapes=[pltpu.VMEM((HEAD_DIM, HGRN_WIDTH), F32)],
        compiler_params=_params("arbitrary"),
    )(proj, proj, proj, lb_logits, after)


def _gate_fwd(proj, o, gate_norm_w, after):
    t = proj.shape[1]
    tb = min(t, 512)

    def body(o_ref, og_ref, gnw_ref, after_ref, out_ref):
        gnw = gnw_ref[...]
        for s in range(tb // GATE_STRIP):
            rows = slice(s * GATE_STRIP, (s + 1) * GATE_STRIP)
            og = og_ref[rows, :]
            on = _per_head(lambda o_h: o_h * lax.rsqrt(jnp.mean(o_h * o_h, axis=-1, keepdims=True) + EPS), o_ref[rows, :])
            out_ref[rows, :] = (on * gnw * (og * _sigmoid(og))).astype(out_ref.dtype)

    tile = pl.BlockSpec((tb, GROUP), lambda i: (i, 0))
    return pl.pallas_call(
        body, name="gate_fwd", grid=(t // tb,),
        in_specs=[tile, pl.BlockSpec((None, tb, GROUP), lambda i: (3, i, 0)), pl.BlockSpec((1, GROUP), lambda i: (0, 0)), ANY],
        out_specs=tile,
        out_shape=jax.ShapeDtypeStruct((t, HGRN_WIDTH), BF16),
        compiler_params=_params("parallel"),
    )(o, proj, gate_norm_w, after)


def _ln_bwd(dy, xhat, rstd, g):
    dxhat = dy * g
    m1 = jnp.mean(dxhat, axis=-1, keepdims=True)
    m2 = jnp.mean(dxhat * xhat, axis=-1, keepdims=True)
    return rstd * (dxhat - m1 - xhat * m2)


def _layer_norm(pre):
    xc = pre - jnp.mean(pre, axis=-1, keepdims=True)
    rstd = lax.rsqrt(jnp.mean(xc * xc, axis=-1, keepdims=True) + EPS)
    return xc * rstd, rstd


def _sublayers(cat_h, cat_c, x, target, w_out, w_ff1, w_ff2, g1, b1, g2, b2):
    t = x.shape[0]
    tm = min(t, 256)

    def body(ch_ref, cc_ref, x_ref, tg_ref, wo_ref, w1_ref, w2_ref, g1_ref, b1_ref, g2_ref, b2_ref,
             h1_ref, r_ref, da_ref, dp2b_ref, dp1_ref, dcat_ref, dg1_ref, db1_ref, dg2_ref, db2_ref, loss_ref, gwo_ref,
             gwo_acc, gwo_narrow, sem):
        @pl.when(pl.program_id(0) == 0)
        def _():
            for ref in (dg1_ref, db1_ref, dg2_ref, db2_ref, loss_ref, gwo_acc):
                ref[...] = jnp.zeros_like(ref)

        mix = _dot(ch_ref[...], wo_ref[0:GROUP, :], NN) + _dot(cc_ref[...], wo_ref[GROUP:2 * GROUP, :], NN)
        xhat1, rstd1 = _layer_norm(ALPHA * x_ref[...] + mix)
        h1 = xhat1 * g1_ref[...] + b1_ref[...]
        h1b = h1.astype(h1_ref.dtype)
        h1_ref[...] = h1b
        mlp = jnp.zeros((tm, D_MODEL), F32)
        for j in range(N_FF):
            cols = slice(j * FF_BLOCK, (j + 1) * FF_BLOCK)
            r = jnp.square(jnp.maximum(_dot(h1b, w1_ref[:, cols], NN), 0.0)).astype(r_ref.dtype)
            r_ref[:, cols] = r
            mlp = mlp + _dot(r, w2_ref[cols, :], NN)
        xhat2, rstd2 = _layer_norm(ALPHA * h1 + mlp)
        err = xhat2 * g2_ref[...] + b2_ref[...] - tg_ref[...]
        loss_ref[...] += 0.5 * jnp.sum(jnp.mean(err * err, axis=-1, keepdims=True))
        dy = err * (1.0 / D_MODEL)
        dg2_ref[...] += jnp.sum(dy * xhat2, axis=0, keepdims=True)
        db2_ref[...] += jnp.sum(dy, axis=0, keepdims=True)
        dp2 = _ln_bwd(dy, xhat2, rstd2, g2_ref[...])
        dp2b = dp2.astype(dp2b_ref.dtype)
        dp2b_ref[...] = dp2b
        back = jnp.zeros((tm, D_MODEL), F32)
        for j in range(N_FF):
            cols = slice(j * FF_BLOCK, (j + 1) * FF_BLOCK)
            dr = _dot(dp2b, w2_ref[cols, :], NT)
            da = (dr * (2.0 * jnp.sqrt(r_ref[:, cols].astype(F32)))).astype(da_ref.dtype)
            da_ref[:, cols] = da
            back = back + _dot(da, w1_ref[:, cols], NT)
        dh1 = ALPHA * dp2 + back
        dg1_ref[...] += jnp.sum(dh1 * xhat1, axis=0, keepdims=True)
        db1_ref[...] += jnp.sum(dh1, axis=0, keepdims=True)
        dp1 = _ln_bwd(dh1, xhat1, rstd1, g1_ref[...])
        dp1b = dp1.astype(MXU_DTYPE)
        dp1_ref[...] = dp1
        dcat_ref[...] = _dot(dp1b, wo_ref[...], NT)
        gwo_acc[0:GROUP, :] += _dot(ch_ref[...], dp1b, TN)
        gwo_acc[GROUP:2 * GROUP, :] += _dot(cc_ref[...], dp1b, TN)

        @pl.when(pl.program_id(0) == pl.num_programs(0) - 1)
        def _():
            gwo_narrow[...] = gwo_acc[...].astype(gwo_narrow.dtype)
            copy = pltpu.make_async_copy(gwo_narrow, gwo_ref, sem.at[0])
            copy.start()
            copy.wait()

    row = pl.BlockSpec((tm, D_MODEL), lambda i: (i, 0))
    wide = pl.BlockSpec((tm, D_FF), lambda i: (i, 0))
    vec = pl.BlockSpec((1, D_MODEL), lambda i: (0, 0))
    narrow = lambda dtype: jax.ShapeDtypeStruct((t, D_MODEL), dtype)
    return pl.pallas_call(
        body, name="sublayers", grid=(t // tm,),
        in_specs=[pl.BlockSpec((tm, GROUP), lambda i: (i, 0)), pl.BlockSpec((tm, GROUP), lambda i: (i, 0)), row, row,
                  _resident((D_MODEL, D_MODEL)),
                  _resident((D_MODEL, D_FF)), _resident((D_FF, D_MODEL)), vec, vec, vec, vec],
        out_specs=[row, wide, wide, row, row, row, vec, vec, vec, vec,
                   pl.BlockSpec((SUBLANES, LANES), lambda i: (0, 0)), ANY],
        out_shape=[narrow(BF16), jax.ShapeDtypeStruct((t, D_FF), BF16), jax.ShapeDtypeStruct((t, D_FF), BF16),
                   narrow(BF16), narrow(F32), narrow(F32)]
                  + [jax.ShapeDtypeStruct((1, D_MODEL), F32)] * 4
                  + [jax.ShapeDtypeStruct((SUBLANES, LANES), F32), jax.ShapeDtypeStruct((D_MODEL, D_MODEL), BF16)],
        scratch_shapes=[pltpu.VMEM((D_MODEL, D_MODEL), F32), pltpu.VMEM((D_MODEL, D_MODEL), BF16),
                        pltpu.SemaphoreType.DMA((1,))],
        compiler_params=_params("arbitrary"),
    )(cat_h, cat_c, x, target, w_out, w_ff1, w_ff2, g1, b1, g2, b2)


def _hgrn_bwd(proj, do, states, lb_logits, after):
    t = proj.shape[1]
    tb = min(t, 512)
    ncb = tb // CHUNK
    nblk = t // tb

    def body(q_ref, f_ref, v_ref, do_ref, st_ref, lbl_ref, after_ref, dp_ref, dlbl_ref, ds_scr, dlb_scr):
        i = pl.program_id(0)

        @pl.when(i == 0)
        def _():
            ds_scr[...] = jnp.zeros_like(ds_scr)
            dlb_scr[...] = jnp.zeros_like(dlb_scr)

        lb, s1 = _lower_bound(lbl_ref[...])
        causal, anti = _chunk_masks()
        every = range(ncb)
        rows = [slice(c * CHUNK, (c + 1) * CHUNK) for c in every]
        q, v, do = ([ref[r, :] for r in rows] for ref in (q_ref, v_ref, do_ref))
        st = [st_ref[c] for c in every]
        gates = [_gates(f_ref[r, :], lb) for r in rows]
        sig, f, k = ([gt[n] for gt in gates] for n in (0, 1, 3))
        b = [_dot_exact(causal, gt[2]) for gt in gates]
        mid, last = [x[CHUNK // 2:CHUNK // 2 + 1, :] for x in b], [x[CHUNK - 1:CHUNK, :] for x in b]
        e_q = [jnp.exp(b[c] - mid[c]) for c in every]
        e_k = [jnp.exp(mid[c] - b[c]) for c in every]
        e_i = [jnp.exp(x) for x in b]
        e_s = [jnp.exp(last[c] - b[c]) for c in every]
        dec = [jnp.exp(x) for x in last]
        qt, kt, qi, ks = ([a[c] * e[c] for c in every] for a, e in ((q, e_q), (k, e_k), (q, e_i), (k, e_s)))

        def masked(a, b_):
            return [[jnp.where(causal, _dot(a_h, b_h, NT), 0.0) for a_h, b_h in zip(_heads(a[c]), _heads(b_[c]))]
                    for c in every]

        def with_scores(s, other, dims):
            return [jnp.concatenate([_dot(s_h, o_h, dims) for s_h, o_h in zip(s[c], _heads(other[c]))], axis=1)
                    for c in every]

        def per_head(dims, a, b_):
            return [_per_head(lambda a_h, b_h: _dot(a_h, b_h, dims), a[c], b_[c]) for c in every]

        scores, dscores = masked(qt, kt), masked(do, v)
        dqt, dkt, dv_intra = with_scores(dscores, kt, NN), with_scores(dscores, qt, TN), with_scores(scores, do, TN)
        dqi, update = per_head(NN, do, st), per_head(TN, do, qi)

        dst = ds_scr[...]
        dsts = [None] * ncb
        for c in reversed(every):
            dsts[c] = dst
            dst = dec[c] * dst + update[c]
        ds_scr[...] = dst

        dv_state, dks = per_head(NT, ks, dsts), per_head(NN, v, dsts)
        ddec = [jnp.sum(dsts[c] * st[c], axis=0, keepdims=True) for c in every]
        dq = [dqt[c] * e_q[c] + dqi[c] * e_i[c] for c in every]
        dk = [dkt[c] * e_k[c] + dks[c] * e_s[c] for c in every]
        db = [q[c] * dq[c] - k[c] * dk[c] for c in every]
        db_last = [jnp.sum(dks[c] * ks[c], axis=0, keepdims=True) + ddec[c] * dec[c] for c in every]
        dg = [_dot_exact(anti, db[c]) + db_last[c] for c in every]
        df = [dg[c] / f[c] - dk[c] for c in every]
        dlb_scr[...] += sum(jnp.sum(df[c] * (1.0 - sig[c]), axis=0, keepdims=True) for c in every)
        dfp = [df[c] * (1.0 - lb) * sig[c] * (1.0 - sig[c]) for c in every]
        dv = [dv_intra[c] + dv_state[c] for c in every]
        for n, parts in enumerate((dq, dfp, dv)):
            dp_ref[n] = jnp.concatenate(parts, axis=0).astype(dp_ref.dtype)

        @pl.when(i == nblk - 1)
        def _():
            dlb = dlb_scr[...]
            dlbl_ref[0:1, :] = dlb * lb * (1.0 - lb)
            dlbl_ref[1:2, :] = -dlb * lb * s1

    grp = lambda g: pl.BlockSpec((None, tb, GROUP), lambda i: (g, nblk - 1 - i, 0))
    vec = pl.BlockSpec((2, HGRN_WIDTH), lambda i: (0, 0))
    return pl.pallas_call(
        body, name="hgrn_bwd", grid=(nblk,),
        in_specs=[grp(0), grp(1), grp(2), pl.BlockSpec((tb, HGRN_WIDTH), lambda i: (nblk - 1 - i, 0)),
                  pl.BlockSpec((ncb, HEAD_DIM, HGRN_WIDTH), lambda i: (nblk - 1 - i, 0, 0)), vec, ANY],
        out_specs=[pl.BlockSpec((3, tb, HGRN_WIDTH), lambda i: (0, nblk - 1 - i, 0)), vec],
        out_shape=[jax.ShapeDtypeStruct((3, t, HGRN_WIDTH), BF16), jax.ShapeDtypeStruct((2, HGRN_WIDTH), F32)],
        scratch_shapes=[pltpu.VMEM((HEAD_DIM, HGRN_WIDTH), F32), pltpu.VMEM((1, HGRN_WIDTH), F32)],
        compiler_params=_params("arbitrary"),
    )(proj, proj, proj, do, states, lb_logits, after)


GRAD_TILE = 512
OUT_PARTS = 4


class _Side:
    def __init__(self, operands, in_specs, out_shape, out_specs, scratch, init, begin):
        self.operands, self.in_specs, self.out_shape, self.out_specs = operands, in_specs, out_shape, out_specs
        self.scratch, self.init, self.begin = scratch, init, begin


def _grad_w(name, operands, widths, shape, step, after=None, side=None):
    t = operands[0].shape[-2]
    tt = min(t, GRAD_TILE)
    n_in, n_steps = len(operands), t // tt
    in_specs = [pl.BlockSpec((tt, w), lambda k: (k, 0)) if a.ndim == 2 else
                pl.BlockSpec((a.shape[0], tt, w), lambda k: (0, k, 0)) for a, w in zip(operands, widths)]
    extra = [] if after is None else [after]
    s_in, s_out = (len(side.operands), len(side.out_shape)) if side else (0, 0)
    first_out = n_in + s_in + len(extra)

    def body(*refs):
        o_ref, side_outs = refs[first_out], refs[first_out + 1:first_out + 1 + s_out]
        acc, narrow, sem = refs[first_out + 1 + s_out:first_out + 4 + s_out]
        k = pl.program_id(0)

        @pl.when(k == 0)
        def _():
            acc[...] = jnp.zeros_like(acc)
            if side:
                side.init(side_outs)

        tick = side.begin(k, n_steps, refs[n_in:n_in + s_in], side_outs, refs[first_out + 4 + s_out:]) if side else None
        step(acc, *refs[:n_in], tick or (lambda j: None))

        @pl.when(k == n_steps - 1)
        def _():
            part = shape[0] // OUT_PARTS
            copies = []
            for p in range(OUT_PARTS):
                rows = pl.ds(p * part, part)
                narrow[rows, :] = acc[rows, :].astype(narrow.dtype)
                copies.append(pltpu.make_async_copy(narrow.at[rows, :], o_ref.at[rows, :], sem.at[p]))
                copies[-1].start()
            for cp in copies:
                cp.wait()

    outs = pl.pallas_call(
        body, name=name, grid=(n_steps,),
        in_specs=in_specs + (side.in_specs if side else []) + [ANY] * len(extra),
        out_specs=[ANY] + (side.out_specs if side else []),
        out_shape=[jax.ShapeDtypeStruct(shape, BF16)] + (side.out_shape if side else []),
        scratch_shapes=[pltpu.VMEM(shape, F32), pltpu.VMEM(shape, BF16), pltpu.SemaphoreType.DMA((OUT_PARTS,))]
                       + (side.scratch if side else []),
        compiler_params=_params("arbitrary"),
    )(*operands, *(side.operands if side else ()), *extra)
    return outs if side else outs[0]


def _dw_in(xb, dph, dog, dpc, w_in, dpre1, after):
    t = xb.shape[0]

    def step(acc, x_ref, dh_ref, dog_ref, dc_ref, tick):
        xv = x_ref[...]
        for g in range(N_GROUPS):
            part = dh_ref[g] if g < 3 else dog_ref[...] if g == 3 else dc_ref[g - 4]
            acc[:, g * GROUP:(g + 1) * GROUP] += _dot(xv, part, TN)
            tick(g, part)

    def begin(k, n_steps, ins, outs, scratch):
        w_ref, dp_ref = ins
        total = [ALPHA * dp_ref[...]]

        def tick(g, part):
            total[0] = total[0] + _dot(part, w_ref[:, g * GROUP:(g + 1) * GROUP], NT)
            if g == N_GROUPS - 1:
                outs[0][...] = total[0]

        return tick

    row = pl.BlockSpec((min(t, GRAD_TILE), D_MODEL), lambda k: (k, 0))
    side = _Side((w_in, dpre1), [_resident((D_MODEL, IN_COLS)), row], [jax.ShapeDtypeStruct((t, D_MODEL), F32)], [row],
                 [], lambda outs: None, begin)
    return _grad_w("dw_in", (xb, dph, dog, dpc), (D_MODEL, GROUP, GROUP, GROUP), (D_MODEL, IN_COLS), step, after, side)


def _strips_of(j, tt):
    per_tick = tt // GATE_STRIP // N_FF
    return [slice(s * GATE_STRIP, (s + 1) * GATE_STRIP) for s in range(j * per_tick, (j + 1) * per_tick)]


def _dw_ff1(h1b, da, dcat, o, proj, gate_norm_w, after):
    t = h1b.shape[0]
    tt = min(t, GRAD_TILE)

    def step(acc, h_ref, da_ref, tick):
        hv = h_ref[...]
        for j in range(N_FF):
            cols = slice(j * FF_BLOCK, (j + 1) * FF_BLOCK)
            acc[:, cols] += _dot(hv, da_ref[:, cols], TN)
            tick(j)

    def init(outs):
        outs[2][...] = jnp.zeros_like(outs[2])

    def begin(k, n_steps, ins, outs, scratch):
        do2_ref, o_ref, og_ref, gnw_ref = ins
        do_ref, dog_ref, dgnw_ref = outs
        total = [jnp.zeros((GATE_STRIP, GROUP), F32)]

        def tick(j):
            gnw = gnw_ref[...]
            for rows in _strips_of(j, tt):
                ov, og, do2 = o_ref[rows, :], og_ref[rows, :], do2_ref[rows, :]
                rs = _per_head(lambda o_h: jnp.broadcast_to(
                    lax.rsqrt(jnp.mean(o_h * o_h, axis=-1, keepdims=True) + EPS), o_h.shape), ov)
                on = ov * rs
                sg = _sigmoid(og)
                sil = og * sg
                don = do2 * gnw * sil
                total[0] = total[0] + do2 * on * sil
                dog_ref[rows, :] = (do2 * on * gnw * (sg * (1.0 + og * (1.0 - sg)))).astype(dog_ref.dtype)
                do_ref[rows, :] = rs * (don - on * _per_head(
                    lambda p_h: jnp.broadcast_to(jnp.mean(p_h, axis=-1, keepdims=True), p_h.shape), don * on))
            if j == N_FF - 1:
                dgnw_ref[...] += jnp.sum(total[0], axis=0, keepdims=True)

        return tick

    tile = pl.BlockSpec((tt, GROUP), lambda k: (k, 0))
    vec = pl.BlockSpec((1, GROUP), lambda k: (0, 0))
    side = _Side(
        (dcat, o, proj, gate_norm_w), [tile, tile, pl.BlockSpec((None, tt, GROUP), lambda k: (3, k, 0)), vec],
        [jax.ShapeDtypeStruct((t, HGRN_WIDTH), F32), jax.ShapeDtypeStruct((t, HGRN_WIDTH), BF16),
         jax.ShapeDtypeStruct((1, HGRN_WIDTH), F32)], [tile, tile, vec], [], init, begin)
    return _grad_w("dw_ff1", (h1b, da), (D_MODEL, D_FF), (D_MODEL, D_FF), step, after, side)


def _dw_ff2(r, dpre2b, dcat, bcu, conv_w):
    t = r.shape[0]
    tt = min(t, GRAD_TILE)
    hb = tt // SUBLANES
    halo = 2 * SUBLANES

    def step(acc, r_ref, d_ref, tick):
        dv = d_ref[...]
        for j in range(N_FF):
            rows = slice(j * FF_BLOCK, (j + 1) * FF_BLOCK)
            acc[rows, :] += _dot(r_ref[:, rows], dv, TN)
            tick(j)

    def init(outs):
        outs[1][...] = jnp.zeros_like(outs[1])

    def begin(k, n_steps, ins, outs, scratch):
        dy_ref, dyn_ref, b_ref, bn_ref, c_ref, u_ref, ch_ref, uh_ref, cw_ref = ins
        dp_ref, dcw_ref = outs
        zbuf, dbuf = scratch
        before = lambda ref: ref[SUBLANES:halo, :].astype(F32)
        zbuf[0:SUBLANES, :] = jnp.where(k > 0, before(ch_ref) * before(uh_ref), 0.0)
        zbuf[SUBLANES:SUBLANES + tt, :] = c_ref[...].astype(F32) * u_ref[...].astype(F32)
        dbuf[0:tt, :] = dy_ref[...] * b_ref[...].astype(F32)
        dbuf[tt:tt + SUBLANES, :] = jnp.where(k < n_steps - 1, dyn_ref[...] * bn_ref[0:SUBLANES, :].astype(F32), 0.0)
        totals = [jnp.zeros((GATE_STRIP, GROUP), F32) for _ in range(3)]

        def tick(j):
            cw = cw_ref[...]
            for rows in _strips_of(j, tt):
                at = lambda buf, shift: buf[shift + rows.start:shift + rows.stop, :]
                z, z1, z2 = at(zbuf, SUBLANES), at(zbuf, SUBLANES - 1), at(zbuf, SUBLANES - 2)
                dyc, d1, d2 = at(dbuf, 0), at(dbuf, 1), at(dbuf, 2)
                yc = cw[2:3, :] * z + cw[1:2, :] * z1 + cw[0:1, :] * z2
                dz = cw[2:3, :] * dyc + cw[1:2, :] * d1 + cw[0:1, :] * d2
                dp_ref[0, rows, :] = (dy_ref[rows, :] * yc).astype(dp_ref.dtype)
                dp_ref[1, rows, :] = (dz * u_ref[rows, :].astype(F32)).astype(dp_ref.dtype)
                dp_ref[2, rows, :] = (dz * c_ref[rows, :].astype(F32)).astype(dp_ref.dtype)
                for n, tap in enumerate((z2, z1, z)):
                    totals[n] = totals[n] + dyc * tap
            if j == N_FF - 1:
                for n in range(3):
                    dcw_ref[n:n + 1, :] += jnp.sum(totals[n], axis=0, keepdims=True)

        return tick

    grp = lambda g: pl.BlockSpec((None, tt, GROUP), lambda k: (g, k, 0))
    prev = lambda g: pl.BlockSpec((None, halo, GROUP), lambda k: (g, jnp.maximum(k * (tt // halo) - 1, 0), 0))
    nxt = lambda g: pl.BlockSpec((None, halo, GROUP), lambda k: (g, jnp.minimum((k + 1) * (tt // halo), t // halo - 1), 0))
    nxt_row = lambda k: jnp.minimum((k + 1) * hb, t // SUBLANES - 1)
    whole = pl.BlockSpec((3, CONV_WIDTH), lambda k: (0, 0))
    side = _Side(
        (dcat, dcat, bcu, bcu, bcu, bcu, bcu, bcu, conv_w),
        [pl.BlockSpec((tt, GROUP), lambda k: (k, 1)), pl.BlockSpec((SUBLANES, GROUP), lambda k: (nxt_row(k), 1)),
         grp(0), nxt(0), grp(1), grp(2), prev(1), prev(2), whole],
        [jax.ShapeDtypeStruct((3, t, CONV_WIDTH), BF16), jax.ShapeDtypeStruct((3, CONV_WIDTH), F32)],
        [pl.BlockSpec((3, tt, GROUP), lambda k: (0, k, 0)), whole],
        [pltpu.VMEM((tt + SUBLANES, GROUP), F32), pltpu.VMEM((tt + SUBLANES, GROUP), F32)], init, begin)
    return _grad_w("dw_ff2", (r, dpre2b), (D_FF, D_MODEL), (D_FF, D_MODEL), step, side=side)


def _place():
    x, y, c = lax.axis_index("x"), lax.axis_index("y"), lax.axis_index("c")
    return x, y, c, 2 * x + y


def _other_chips(x, y):
    return [(1 - x, y), (x, 1 - y), (1 - x, 1 - y)]


def _place_shard(name, w, chip, cols_sharded, after=None):
    rows, cols = w.shape
    tr = min(rows, 256)
    nb = rows // tr
    full = (rows, cols * N_CHIPS) if cols_sharded else (rows * N_CHIPS, cols)
    out_map = (lambda i, s: (i, s[0])) if cols_sharded else (lambda i, s: (s[0] * nb + i, 0))

    def body(s_ref, w_ref, *rest):
        rest[-1][...] = w_ref[...].astype(rest[-1].dtype)

    extra = [] if after is None else [after]
    return pl.pallas_call(
        body, name=name,
        grid_spec=pltpu.PrefetchScalarGridSpec(
            num_scalar_prefetch=1, grid=(nb,),
            in_specs=[pl.BlockSpec((tr, cols), lambda i, s: (i, 0))] + [ANY] * len(extra),
            out_specs=pl.BlockSpec((tr, cols), out_map)),
        out_shape=jax.ShapeDtypeStruct(full, BF16),
        compiler_params=_params("parallel"),
    )(chip, w, *extra)


HBM = pl.BlockSpec(memory_space=pltpu.HBM)
SEM = pl.BlockSpec(memory_space=pltpu.SEMAPHORE)
EFFECT = pltpu.SideEffectType.DATAFLOW_SIDE_EFFECTING


PEER_SETS = {
    "sibling": (0, lambda x, y, c: [(x, y, 1 - c)]),
    "chips": (1, lambda x, y, c: [(1 - x, y, c), (x, 1 - y, c), (1 - x, 1 - y, c)]),
    "neighbours": (2, lambda x, y, c: [(1 - x, y, c), (x, 1 - y, c)]),
}


class _Split:
    def __init__(self, name, arrays, plan, others=(), peers=None):
        n_own, arrays = len(arrays), (*arrays, *others)
        n, n_copies = len(arrays), plan.count
        self.name, self.plan, self.n = name, plan, n_own
        barrier_id, peer_ids = PEER_SETS[peers] if peers else (None, None)

        def body(*refs):
            if peers:
                x, y, c, _ = _place()
                barrier = pltpu.get_barrier_semaphore()
                for peer in peer_ids(x, y, c):
                    pl.semaphore_signal(barrier, inc=1, device_id=peer, device_id_type=MESH)
                pl.semaphore_wait(barrier, len(peer_ids(0, 0, 0)))
            send_sems, recv_sems, token = refs[n], refs[n + 1], refs[-1]
            for k, (src, dst, to) in enumerate(plan(refs[:n])):
                pltpu.make_async_remote_copy(src_ref=src, dst_ref=dst, send_sem=send_sems.at[k], recv_sem=recv_sems.at[k],
                                             device_id=to, device_id_type=MESH).start()
            token[...] = jnp.zeros_like(token)

        outs = pl.pallas_call(
            body, name=name + "_start",
            out_shape=(pltpu.SemaphoreType.DMA((n_copies,)), pltpu.SemaphoreType.DMA((n_copies,)),
                       *[pltpu.HBM(a.shape, a.dtype) for a in arrays], jax.ShapeDtypeStruct((SUBLANES, LANES), F32)),
            in_specs=(HBM,) * n, out_specs=(SEM, SEM) + (HBM,) * n + (pl.BlockSpec(memory_space=pltpu.VMEM),),
            input_output_aliases={i: 2 + i for i in range(n)},
            compiler_params=pltpu.CompilerParams(has_side_effects=EFFECT, collective_id=barrier_id),
        )(*[pltpu.with_memory_space_constraint(a, pltpu.HBM) for a in arrays])
        self.sems, self.arrays, self.others, self.token = outs[:2], outs[2:2 + n_own], outs[2 + n_own:2 + n], outs[-1]

    def wait(self, after):
        n, plan = self.n, self.plan

        def body(*refs):
            send_sems, recv_sems = refs[n], refs[n + 1]
            for k, (src, dst, to) in enumerate(plan(refs[:n])):
                cp = pltpu.make_async_remote_copy(src_ref=src, dst_ref=dst, send_sem=send_sems.at[k],
                                                  recv_sem=recv_sems.at[k], device_id=to, device_id_type=MESH)
                cp.wait_send()
                cp.wait_recv()

        return pl.pallas_call(
            body, name=self.name + "_wait", out_shape=tuple(pltpu.HBM(a.shape, a.dtype) for a in self.arrays),
            in_specs=(HBM,) * n + (SEM, SEM, ANY), out_specs=(HBM,) * n, input_output_aliases={i: i for i in range(n)},
            compiler_params=pltpu.CompilerParams(has_side_effects=EFFECT),
        )(*self.arrays, *self.sems, after)


COLS_SHARDED = (True, False, True, False)
HALF_SHAPES = [(D_MODEL // 2, IN_COLS), (D_MODEL, D_MODEL // 2), (D_MODEL // 2, D_FF), (D_FF, D_MODEL // 2)]
PIECE_SHAPES = [(D_MODEL // 2, IN_COLS // N_CHIPS), (D_MODEL // N_CHIPS, D_MODEL // 2),
                (D_MODEL // 2, D_FF // N_CHIPS), (D_FF // N_CHIPS, D_MODEL // 2)]


def _shard_view(kind, ref, chip):
    if COLS_SHARDED[kind]:
        n = ref.shape[1] // N_CHIPS
        return ref.at[:, pl.ds(chip * n, n)]
    n = ref.shape[0] // N_CHIPS
    return ref.at[pl.ds(chip * n, n), :]


def _half_view(kind, ref, h):
    if COLS_SHARDED[kind]:
        n = ref.shape[0] // 2
        return ref.at[pl.ds(h * n, n), :]
    n = ref.shape[1] // 2
    return ref.at[:, pl.ds(h * n, n)]


def _plan(count):
    def mark(fn):
        fn.count = count
        return fn
    return mark


def _shard_rows_view(kind, ref, chip, part, n_parts):
    if COLS_SHARDED[kind]:
        m, n = ref.shape[0] // n_parts, ref.shape[1] // N_CHIPS
        return ref.at[pl.ds(part * m, m), pl.ds(chip * n, n)]
    m = ref.shape[0] // N_CHIPS // n_parts
    return ref.at[pl.ds((n_parts * chip + part) * m, m), :]


def _shard_half_view(kind, ref, chip, h):
    return _shard_rows_view(kind, ref, chip, h, 2)


def _gather_over_ici(kinds, weights):
    @_plan(2 * len(kinds))
    def plan(refs):
        x, y, c, me = _place()
        mine = [_shard_half_view(kind, ref, me, c) for kind, ref in zip(kinds, refs)]
        return [(v, v, to) for v in mine for to in ((1 - x, y, c), (x, 1 - y, c))]

    return _Split("gather_ici_" + "".join(map(str, kinds)), tuple(weights), plan, peers="neighbours")


def _relay_over_ici(kinds, weights, others=()):
    @_plan(2 * len(kinds))
    def plan(refs):
        x, y, c, _ = _place()
        x_nbr, y_nbr = 2 * (1 - x) + y, 2 * x + (1 - y)
        out = []
        for kind, ref in zip(kinds, refs):
            first, second = (_shard_rows_view(kind, ref, chip, 2 * c + q, 4) for q, chip in ((0, x_nbr), (1, y_nbr)))
            out += [(first, first, (x, 1 - y, c)), (second, second, (1 - x, y, c))]
        return out

    return _Split("relay_ici_" + "".join(map(str, kinds)), tuple(weights), plan, others, peers="neighbours")


def _gather_w_in_over_ici(w_in, conv4):
    @_plan(6)
    def plan(refs):
        x, y, c, me = _place()
        half, conv = _shard_half_view(0, refs[0], me, c), refs[1].at[me]
        return [(v, v, (px, py, c)) for v in (half, conv) for px, py in _other_chips(x, y)]

    return _Split("gather_w_in_ici", (w_in, conv4), plan, peers="chips")


def _gather_over_d2d(kinds, weights):
    @_plan(3 * len(kinds))
    def plan(refs):
        x, y, c, _ = _place()
        got = [_shard_half_view(kind, ref, 2 * px + py, c) for kind, ref in zip(kinds, refs)
               for px, py in _other_chips(x, y)]
        return [(v, v, (x, y, 1 - c)) for v in got]

    return _Split("gather_d2d_" + "".join(map(str, kinds)), tuple(weights), plan, peers="sibling")


def _swap_halves(kinds, grads):
    @_plan(len(kinds))
    def plan(refs):
        x, y, c, _ = _place()
        return [(_half_view(kind, g, 1 - c), land, (x, y, 1 - c))
                for kind, g, land in zip(kinds, refs[:len(kinds)], refs[len(kinds):])]

    lands = [lax.empty(HALF_SHAPES[kind], g.dtype) for kind, g in zip(kinds, grads)]
    return _Split("swap_halves_" + "".join(map(str, kinds)), (*grads, *lands), plan, peers="sibling")


def _block_rows(cols, elements):
    return 1 << ((elements // cols).bit_length() - 1)


def _add_half(name, g, recv, core, rows_split):
    shape = recv.shape
    tr = min(shape[0], _block_rows(shape[1], 1 << 20))
    nb = shape[0] // tr

    def body(c_ref, g_ref, r_ref, o_ref):
        o_ref[...] = (g_ref[...].astype(F32) + r_ref[...].astype(F32)).astype(o_ref.dtype)

    g_map = (lambda i, c_ref: (c_ref[0] * nb + i, 0)) if rows_split else (lambda i, c_ref: (i, c_ref[0]))
    blk = pl.BlockSpec((tr, shape[1]), lambda i, c_ref: (i, 0))
    return pl.pallas_call(
        body, name=name,
        grid_spec=pltpu.PrefetchScalarGridSpec(
            num_scalar_prefetch=1, grid=(nb,),
            in_specs=[pl.BlockSpec((tr, shape[1]), g_map), blk], out_specs=blk),
        out_shape=jax.ShapeDtypeStruct(shape, BF16),
        compiler_params=_params("parallel"),
    )(core, g, recv)


def _exchange_pieces(kinds, halves, pack=None):
    n_p, n = N_CHIPS - 1, len(kinds)

    @_plan(n_p * n + (0 if pack is None else N_DEV - 1))
    def plan(refs):
        x, y, c, _ = _place()
        copies = []
        if pack is not None:
            me = 4 * x + 2 * y + c
            peers = [((1 - x) if m & 4 else x, (1 - y) if m & 2 else y, (1 - c) if m & 1 else c) for m in range(1, N_DEV)]
            copies += [(refs[2 * n], refs[2 * n + 1].at[me], peer) for peer in peers]
        return copies + [(_shard_view(kind, half, 2 * px + py), land.at[j], (px, py, c))
                         for j, (px, py) in enumerate(_other_chips(x, y))
                         for kind, half, land in zip(kinds, refs[:n], refs[n:2 * n])]

    lands = [lax.empty((n_p,) + PIECE_SHAPES[kind], BF16) for kind in kinds]
    small = () if pack is None else (pack, lax.empty((N_DEV,) + pack.shape, F32))
    return _Split("exchange_pieces_" + "".join(map(str, kinds)), (*halves, *lands, *small), plan,
                  peers="chips" if pack is None else None)


def _sum_pieces(name, half, slots, place, rows_split, after):
    n_p, rows, cols = slots.shape
    tr = min(rows, _block_rows(cols, 1 << 19))
    nb = rows // tr
    if rows_split:
        own_map = lambda i, s: (i, s[0])
        out_map = lambda i, s: (s[1] * nb + i, 0)
        shard = (2 * rows, cols)
    else:
        own_map = lambda i, s: (s[0] * nb + i, 0)
        out_map = lambda i, s: (i, s[1])
        shard = (rows, 2 * cols)

    def body(s_ref, own_ref, slot_ref, after_ref, o_ref):
        total = own_ref[...].astype(F32)
        for j in range(n_p):
            total = total + slot_ref[j].astype(F32)
        o_ref[...] = total

    return pl.pallas_call(
        body, name=name,
        grid_spec=pltpu.PrefetchScalarGridSpec(
            num_scalar_prefetch=1, grid=(nb,),
            in_specs=[pl.BlockSpec((tr, cols), own_map), pl.BlockSpec((n_p, tr, cols), lambda i, s: (0, i, 0)), ANY],
            out_specs=pl.BlockSpec((tr, cols), out_map)),
        out_shape=jax.ShapeDtypeStruct(shard, F32),
        compiler_params=_params("parallel"),
    )(place, half, slots, after)


def _join_halves(kinds, shards):
    @_plan(len(kinds))
    def plan(refs):
        x, y, c, _ = _place()
        return [(_half_view(kind, g, c), _half_view(kind, g, c), (x, y, 1 - c)) for kind, g in zip(kinds, refs)]

    return _Split("join_halves_" + "".join(map(str, kinds)), tuple(shards), plan, peers="sibling")


N_DEV = 8


def _sum_shared(pack, land, device):
    def body(d_ref, p_ref, l_ref, o_ref):
        me = d_ref[0]
        total = jnp.where(me == 0, p_ref[...], l_ref[0])
        for d in range(1, N_DEV):
            total = total + jnp.where(me == d, p_ref[...], l_ref[d])
        o_ref[...] = total

    return pl.pallas_call(
        body, name="sum_shared",
        grid_spec=pltpu.PrefetchScalarGridSpec(
            num_scalar_prefetch=1, grid=(1,),
            in_specs=[pl.BlockSpec(pack.shape, lambda i, d: (0, 0)), pl.BlockSpec(land.shape, lambda i, d: (0, 0, 0))],
            out_specs=pl.BlockSpec(pack.shape, lambda i, d: (0, 0))),
        out_shape=jax.ShapeDtypeStruct(pack.shape, F32),
    )(device, pack, land)


def _adamw(name, w, g, m, v, after=None):
    rows, cols = w.shape
    tr = min(rows, 256)
    extra = [] if after is None else [after]

    def body(w_ref, g_ref, m_ref, v_ref, *rest):
        d_ref, nm_ref, nv_ref = rest[-3:]
        d_ref[...], nm_ref[...], nv_ref[...] = _adam_step(w_ref[...], g_ref[...], m_ref[...], v_ref[...])

    blk = pl.BlockSpec((tr, cols), lambda i: (i, 0))
    return pl.pallas_call(
        body, name=name, grid=(rows // tr,), in_specs=[blk] * 4 + [ANY] * len(extra), out_specs=[ANY] + [blk] * 3,
        out_shape=[jax.ShapeDtypeStruct(w.shape, F32)] * 4, input_output_aliases={1: 0},
        compiler_params=_params("parallel"),
    )(w, g, m, v, *extra)


def _adam_step(w, g, m, v):
    nm = ADAM_B1 * m + (1.0 - ADAM_B1) * g
    nv = ADAM_B2 * v + (1.0 - ADAM_B2) * jnp.square(g)
    m_hat = nm * (1.0 / (1.0 - ADAM_B1 ** ADAM_STEP))
    v_hat = nv * (1.0 / (1.0 - ADAM_B2 ** ADAM_STEP))
    return -ADAM_LR * (m_hat / (jnp.sqrt(v_hat) + ADAM_EPS) + ADAM_WD * w), nm, nv


def _adamw_small(tot, chip, weights, ms, vs, after):
    n, half = len(weights), D_MODEL // 2

    def body(chip_ref, tot_ref, *refs):
        ins, outs = refs[:3 * n], refs[3 * n + 1:]
        tot = tot_ref[...]
        conv_all = jnp.concatenate([tot[5:6, half:], tot[6:7, :half], tot[6:7, half:]], axis=0)
        conv = sum(jnp.where(chip_ref[0] == s, conv_all[:, s * LANES:(s + 1) * LANES], 0.0) for s in range(N_CHIPS))
        grads = [jnp.concatenate([tot[4:5, :half], tot[4:5, half:]], axis=0), tot[5:6, :half], conv,
                 tot[0:1], tot[1:2], tot[2:3], tot[3:4]]
        for k, g in enumerate(grads):
            delta, nm, nv = _adam_step(ins[k][...], g, ins[n + k][...], ins[2 * n + k][...])
            outs[k][...], outs[n + k][...], outs[2 * n + k][...], outs[3 * n + k][...] = g, delta, nm, nv
        outs[4 * n][...] = tot[7:8, 0:1]

    whole = lambda a: pl.BlockSpec(a.shape, lambda i, s: (0,) * a.ndim)
    arrays = (*weights, *ms, *vs)
    loss = jax.ShapeDtypeStruct((1, 1), F32)
    return pl.pallas_call(
        body, name="adamw_small",
        grid_spec=pltpu.PrefetchScalarGridSpec(
            num_scalar_prefetch=1, grid=(1,), in_specs=[whole(tot)] + [whole(a) for a in arrays] + [ANY],
            out_specs=[whole(a) for a in weights] * 4 + [whole(loss)]),
        out_shape=[jax.ShapeDtypeStruct(a.shape, F32) for a in weights] * 4 + [loss],
    )(chip, tot, *arrays, after)


def kernel(x, w_in, lb_logits, gate_norm_w, conv_w, w_out, ln1_g, ln1_b, w_ff1, w_ff2, ln2_g, ln2_b, loss_target, m_w_in, m_lb_logits, m_gate_norm_w, m_conv_w, m_w_out, m_ln1_g, m_ln1_b, m_w_ff1, m_w_ff2, m_ln2_g, m_ln2_b, v_w_in, v_lb_logits, v_gate_norm_w, v_conv_w, v_w_out, v_ln1_g, v_ln1_b, v_w_ff1, v_w_ff2, v_ln2_g, v_ln2_b):
    xs, tgt = x[0], loss_target[0]
    chip = 2 * lax.axis_index("x") + lax.axis_index("y")
    core = lax.axis_index("c").astype(jnp.int32).reshape(1)
    chip1 = chip.astype(jnp.int32).reshape(1)
    place = jnp.concatenate([chip1, core])

    conv4 = lax.dynamic_update_slice(jnp.zeros((N_CHIPS,) + conv_w.shape[1:], F32), conv_w, (chip, 0, 0))
    ici_in = _gather_w_in_over_ici(_place_shard("place_w_in", w_in[0], chip1, True), conv4)
    rest = (1, 2, 3)
    ici_rest = _gather_over_ici(rest, (_place_shard("place_w_out", w_out[0], chip1, False, after=ici_in.token),
                                       _place_shard("place_w_ff1", w_ff1[0], chip1, True, after=ici_in.token),
                                       _place_shard("place_w_ff2", w_ff2[0], chip1, False, after=ici_in.token)))
    wb_in, cv4 = ici_in.wait(ici_rest.token)
    d2d_in = _gather_over_d2d((0,), (wb_in,))
    wb_in, = d2d_in.wait(d2d_in.token)
    conv_full = cv4.transpose(1, 0, 2).reshape(3, CONV_WIDTH)

    proj, bcu, xb, cat_c = _in_proj(xs, wb_in, conv_full, ici_rest.token)
    relay_rest = _relay_over_ici(rest, ici_rest.wait(proj))
    o, states = _hgrn_fwd(proj, lb_logits, relay_rest.token)
    d2d_rest = _gather_over_d2d(rest, relay_rest.wait(o))
    cat_h = _gate_fwd(proj, o, gate_norm_w, d2d_rest.token)
    wb_out, wb_ff1, wb_ff2 = d2d_rest.wait(cat_h)

    (h1b, r, da, dpre2b, dpre1, dcat, g_ln1_g, g_ln1_b, g_ln2_g, g_ln2_b, loss8, g_out_local) = _sublayers(
        cat_h, cat_c, xs, tgt, wb_out, wb_ff1, wb_ff2, ln1_g, ln1_b, ln2_g, ln2_b)

    names = ("w_in", "w_out", "w_ff1", "w_ff2")

    def add_halves(kinds, grads, lands):
        return [_add_half("add_half_" + names[k], g, ld, core, COLS_SHARDED[k]) for k, g, ld in zip(kinds, grads, lands)]

    def sum_pieces(kinds, halves, lands, after):
        return [_sum_pieces("sum_pieces_" + names[k], h, ld, place, COLS_SHARDED[k], after)
                for k, h, ld in zip(kinds, halves, lands)]

    early = (1, 2, 3)
    g_ff2_local, dpc, g_conv = _dw_ff2(r, dpre2b, dcat, bcu, conv_full)
    swap_a = _swap_halves((1, 3), (g_out_local, g_ff2_local))
    g_ff1_local, do, dog, g_gnw = _dw_ff1(h1b, da, dcat, o, proj, gate_norm_w, swap_a.token)
    swap_b = _swap_halves((2,), (g_ff1_local,))
    swapped_a = swap_a.wait(swap_b.token)
    halves_a = add_halves((1, 3), swapped_a[:2], swapped_a[2:])
    swapped_b = swap_b.wait(halves_a[1])
    halves = (halves_a[0], *add_halves((2,), swapped_b[:1], swapped_b[1:]), halves_a[1])
    exch = _exchange_pieces(early, halves)
    dph, g_lbl = _hgrn_bwd(proj, do, states, lb_logits, exch.token)
    g_in_local, grad_x = _dw_in(xb, dph, dog, dpc, wb_in, dpre1, dph)

    late = (0,)
    swap = _swap_halves(late, (g_in_local,))
    exchanged = exch.wait(swap.token)
    pack = jnp.concatenate([
        g_ln1_g, g_ln1_b, g_ln2_g, g_ln2_b,
        jnp.concatenate([g_lbl[0:1], g_lbl[1:2]], axis=1),
        jnp.concatenate([g_gnw, g_conv[0:1]], axis=1),
        jnp.concatenate([g_conv[1:2], g_conv[2:3]], axis=1),
        jnp.concatenate([loss8[0:1], jnp.zeros((1, D_MODEL - LANES), F32)], axis=1)], axis=0)
    join_a = _join_halves((2,), sum_pieces((2,), exchanged[1:2], exchanged[4:5], swap.token))
    swapped = swap.wait(join_a.token)
    exch = _exchange_pieces(late, add_halves(late, swapped[:1], swapped[1:]), pack)
    join_b = _join_halves((1, 3), sum_pieces((1, 3), exchanged[0:3:2], exchanged[3:6:2], exch.token))
    g_w_ff1, = join_a.wait(join_b.token)
    g_w_ff1, d_ff1, nm_ff1, nv_ff1 = _adamw("adamw_w_ff1", w_ff1[0], g_w_ff1, m_w_ff1[0], v_w_ff1[0])
    g_w_out, g_w_ff2 = join_b.wait(d_ff1)
    g_w_ff2, d_ff2, nm_ff2, nv_ff2 = _adamw("adamw_w_ff2", w_ff2[0], g_w_ff2, m_w_ff2[0], v_w_ff2[0])
    g_w_out, d_out, nm_out, nv_out = _adamw("adamw_w_out", w_out[0], g_w_out, m_w_out[0], v_w_out[0], d_ff2)
    exchanged = exch.wait(d_out)
    tot = _sum_shared(exchanged[2], exchanged[3], 2 * chip1 + core)
    join = _join_halves(late, sum_pieces(late, exchanged[:1], exchanged[1:2], tot))
    small = ("lb_logits", "gate_norm_w", "conv_w", "ln1_g", "ln1_b", "ln2_g", "ln2_b")
    small_out = _adamw_small(
        tot, chip1, (lb_logits, gate_norm_w, conv_w[0], ln1_g, ln1_b, ln2_g, ln2_b),
        (m_lb_logits, m_gate_norm_w, m_conv_w[0], m_ln1_g, m_ln1_b, m_ln2_g, m_ln2_b),
        (v_lb_logits, v_gate_norm_w, v_conv_w[0], v_ln1_g, v_ln1_b, v_ln2_g, v_ln2_b), join.token)
    g_w_in, = join.wait(small_out[0])
    g_w_in, d_in, nm_in, nv_in = _adamw("adamw_w_in", w_in[0], g_w_in, m_w_in[0], v_w_in[0])
    loss = small_out[4 * len(small)][0, 0]

    def results(n_kind, large):
        out = dict(zip(small, small_out[n_kind * len(small):(n_kind + 1) * len(small)]))
        out["conv_w"] = out["conv_w"][None]
        out.update({name: a[None] for name, a in zip(("w_in", "w_out", "w_ff1", "w_ff2"), large)})
        return [out[name] for name in ("w_in", "lb_logits", "gate_norm_w", "conv_w", "w_out", "ln1_g", "ln1_b",
                                       "w_ff1", "w_ff2", "ln2_g", "ln2_b")]

    return (loss, grad_x[None], *results(0, (g_w_in, g_w_out, g_w_ff1, g_w_ff2)),
            *results(1, (d_in, d_out, d_ff1, d_ff2)), *results(2, (nm_in, nm_out, nm_ff1, nm_ff2)),
            *results(3, (nv_in, nv_out, nv_ff1, nv_ff2)))
```

```python
import jax
import jax.numpy as jnp
from jax import lax
from jax.experimental import pallas as pl
from jax.experimental.pallas import tpu as pltpu

F32 = jnp.float32
BF16 = jnp.bfloat16
MXU_DTYPE = jnp.bfloat16

D_MODEL = 1024
HGRN_WIDTH = 512
HEAD_DIM = 128
N_HEADS = 4
CONV_WIDTH = 512
CHUNK = 64
D_FF = 4096
IN_COLS = 3584
GROUP = 512
N_GROUPS = IN_COLS // GROUP
ALPHA = 2.0 ** 0.25
EPS = 1e-5
N_CHIPS = 4
ADAM_LR, ADAM_B1, ADAM_B2, ADAM_EPS, ADAM_WD, ADAM_STEP = 0.001, 0.9, 0.999, 1e-08, 0.01, 10

LANES = 128
SUBLANES = 8
VMEM_LIMIT = 56 * 1024 * 1024
FF_BLOCK = 1024
N_FF = D_FF // FF_BLOCK
GATE_STRIP = 64

NN = (((1,), (0,)), ((), ()))
NT = (((1,), (1,)), ((), ()))
TN = (((0,), (0,)), ((), ()))
MESH = pl.DeviceIdType.MESH
ANY = pl.BlockSpec(memory_space=pl.ANY)


def _dot(a, b, dims):
    return lax.dot_general(a.astype(MXU_DTYPE), b.astype(MXU_DTYPE), dims, preferred_element_type=F32)


def _dot_exact(ones, v):
    ones = ones.astype(jnp.bfloat16)
    hi = v.astype(jnp.bfloat16)
    rest = v - hi.astype(F32)
    mid = rest.astype(jnp.bfloat16)
    low = (rest - mid.astype(F32)).astype(jnp.bfloat16)
    return sum(lax.dot_general(ones, part, NN, preferred_element_type=F32) for part in (hi, mid, low))


def _params(*sem):
    return pltpu.CompilerParams(dimension_semantics=sem, vmem_limit_bytes=VMEM_LIMIT)


def _resident(shape):
    return pl.BlockSpec(shape, lambda *_: (0,) * len(shape), pipeline_mode=pl.Buffered(1))


def _sigmoid(v):
    return 1.0 / (1.0 + jnp.exp(-v))


def _lower_bound(lbl):
    m = jnp.max(lbl, axis=0, keepdims=True)
    e = jnp.exp(lbl - m)
    s = e / jnp.sum(e, axis=0, keepdims=True)
    return s[0:1, :], s[1:2, :]


def _heads(v):
    return [v[:, h * HEAD_DIM:(h + 1) * HEAD_DIM] for h in range(N_HEADS)]


def _per_head(fn, *arrays):
    return jnp.concatenate([fn(*parts) for parts in zip(*map(_heads, arrays))], axis=1)


def _in_proj(x, w_in, conv_w, after):
    t = x.shape[0]
    tm = min(t, 512)

    def body(x_ref, w_ref, cw_ref, after_ref, o_ref, bcu_ref, xb_ref, y_ref, zbuf):
        @pl.when(pl.program_id(0) == 0)
        def _():
            zbuf[tm:tm + SUBLANES, :] = jnp.zeros((SUBLANES, CONV_WIDTH), F32)

        xb = x_ref[...].astype(xb_ref.dtype)
        xb_ref[...] = xb
        group = lambda g: _dot(xb, w_ref[:, g * GROUP:(g + 1) * GROUP], NN)
        for g in range(4):
            o_ref[g] = group(g)
        b_gate, c_gate, u = group(4), group(5), group(6)
        for n, part in enumerate((b_gate, c_gate, u)):
            bcu_ref[n] = part.astype(bcu_ref.dtype)
        zbuf[0:SUBLANES, :] = zbuf[tm:tm + SUBLANES, :]
        zbuf[SUBLANES:SUBLANES + tm, :] = c_gate * u
        cw = cw_ref[...]
        at = lambda shift: zbuf[shift:shift + tm, :]
        conv = cw[2:3, :] * at(SUBLANES) + cw[1:2, :] * at(SUBLANES - 1) + cw[0:1, :] * at(SUBLANES - 2)
        y_ref[...] = (b_gate * conv).astype(y_ref.dtype)

    return pl.pallas_call(
        body, name="in_proj", grid=(t // tm,),
        in_specs=[pl.BlockSpec((tm, D_MODEL), lambda i: (i, 0)), _resident((D_MODEL, IN_COLS)),
                  pl.BlockSpec((3, CONV_WIDTH), lambda i: (0, 0)), ANY],
        out_specs=[pl.BlockSpec((4, tm, GROUP), lambda i: (0, i, 0)), pl.BlockSpec((3, tm, GROUP), lambda i: (0, i, 0)),
                   pl.BlockSpec((tm, D_MODEL), lambda i: (i, 0)), pl.BlockSpec((tm, CONV_WIDTH), lambda i: (i, 0))],
        out_shape=[jax.ShapeDtypeStruct((4, t, GROUP), F32), jax.ShapeDtypeStruct((3, t, GROUP), BF16),
                   jax.ShapeDtypeStruct((t, D_MODEL), BF16), jax.ShapeDtypeStruct((t, CONV_WIDTH), BF16)],
        scratch_shapes=[pltpu.VMEM((tm + SUBLANES, CONV_WIDTH), F32)],
        compiler_params=_params("arbitrary"),
    )(x, w_in, conv_w, after)


def _gates(fp, lb):
    sig = _sigmoid(fp)
    f = lb + (1.0 - lb) * sig
    return sig, f, jnp.log(f), 1.0 - f


def _chunk_masks():
    row = lax.broadcasted_iota(jnp.int32, (CHUNK, CHUNK), 0)
    col = lax.broadcasted_iota(jnp.int32, (CHUNK, CHUNK), 1)
    return row >= col, row <= col


def _hgrn_fwd(proj, lb_logits, after):
    t = proj.shape[1]
    tb = min(t, 512)
    ncb = tb // CHUNK

    def body(q_ref, f_ref, v_ref, lbl_ref, after_ref, o_ref, st_ref, s_scr):
        @pl.when(pl.program_id(0) == 0)
        def _():
            s_scr[...] = jnp.zeros_like(s_scr)

        lb, _ = _lower_bound(lbl_ref[...])
        causal, _ = _chunk_masks()

        every = range(ncb)
        rows = [slice(c * CHUNK, (c + 1) * CHUNK) for c in every]
        q, v = [q_ref[r, :] for r in rows], [v_ref[r, :] for r in rows]
        gates = [_gates(f_ref[r, :], lb) for r in rows]
        k = [gt[3] for gt in gates]
        b = [_dot_exact(causal, gt[2]) for gt in gates]
        mid, last = [x[CHUNK // 2:CHUNK // 2 + 1, :] for x in b], [x[CHUNK - 1:CHUNK, :] for x in b]
        qt = [q[c] * jnp.exp(b[c] - mid[c]) for c in every]
        kt = [k[c] * jnp.exp(mid[c] - b[c]) for c in every]
        qi = [q[c] * jnp.exp(b[c]) for c in every]
        ks = [k[c] * jnp.exp(last[c] - b[c]) for c in every]
        dec = [jnp.exp(x) for x in last]
        scores = [[jnp.where(causal, _dot(a, b_, NT), 0.0) for a, b_ in zip(_heads(qt[c]), _heads(kt[c]))] for c in every]
        intra = [[_dot(s, v_h, NN) for s, v_h in zip(scores[c], _heads(v[c]))] for c in every]
        update = [_per_head(lambda v_h, ks_h: _dot(v_h, ks_h, TN), v[c], ks[c]) for c in every]

        st = s_scr[...]
        states = []
        for c in every:
            states.append(st)
            st_ref[c] = st
            st = dec[c] * st + update[c]
        s_scr[...] = st

        o_ref[...] = jnp.concatenate(
            [jnp.concatenate([i_h + _dot(qi_h, st_h, NT) for i_h, qi_h, st_h in
                              zip(intra[c], _heads(qi[c]), _heads(states[c]))], axis=1) for c in every], axis=0)

    grp = lambda g: pl.BlockSpec((None, tb, GROUP), lambda i: (g, i, 0))
    return pl.pallas_call(
        body, name="hgrn_fwd", grid=(t // tb,),
        in_specs=[grp(0), grp(1), grp(2), pl.BlockSpec((2, HGRN_WIDTH), lambda i: (0, 0)), ANY],
        out_specs=[pl.BlockSpec((tb, HGRN_WIDTH), lambda i: (i, 0)),
                   pl.BlockSpec((ncb, HEAD_DIM, HGRN_WIDTH), lambda i: (i, 0, 0))],
        out_shape=[jax.ShapeDtypeStruct((t, HGRN_WIDTH), F32),
                   jax.ShapeDtypeStruct((t // CHUNK, HEAD_DIM, HGRN_WIDTH), F32)],
        scratch_shapes=[pltpu.VMEM((HEAD_DIM, HGRN_WIDTH), F32)],
        compiler_params=_params("arbitrary"),
    )(proj, proj, proj, lb_logits, after)


def _gate_fwd(proj, o, gate_norm_w, after):
    t = proj.shape[1]
    tb = min(t, 1024)

    def body(o_ref, og_ref, gnw_ref, after_ref, out_ref):
        gnw = gnw_ref[...]
        for s in range(tb // GATE_STRIP):
            rows = slice(s * GATE_STRIP, (s + 1) * GATE_STRIP)
            og = og_ref[rows, :]
            on = _per_head(lambda o_h: o_h * lax.rsqrt(jnp.mean(o_h * o_h, axis=-1, keepdims=True) + EPS), o_ref[rows, :])
            out_ref[rows, :] = (on * gnw * (og * _sigmoid(og))).astype(out_ref.dtype)

    tile = pl.BlockSpec((tb, GROUP), lambda i: (i, 0))
    return pl.pallas_call(
        body, name="gate_fwd", grid=(t // tb,),
        in_specs=[tile, pl.BlockSpec((None, tb, GROUP), lambda i: (3, i, 0)), pl.BlockSpec((1, GROUP), lambda i: (0, 0)), ANY],
        out_specs=tile,
        out_shape=jax.ShapeDtypeStruct((t, HGRN_WIDTH), BF16),
        compiler_params=_params("parallel"),
    )(o, proj, gate_norm_w, after)


def _ln_bwd(dy, xhat, rstd, g):
    dxhat = dy * g
    m1 = jnp.mean(dxhat, axis=-1, keepdims=True)
    m2 = jnp.mean(dxhat * xhat, axis=-1, keepdims=True)
    return rstd * (dxhat - m1 - xhat * m2)


def _layer_norm(pre):
    xc = pre - jnp.mean(pre, axis=-1, keepdims=True)
    rstd = lax.rsqrt(jnp.mean(xc * xc, axis=-1, keepdims=True) + EPS)
    return xc * rstd, rstd


def _sublayers(cat_h, cat_c, x, target, w_out, w_ff1, w_ff2, g1, b1, g2, b2):
    t = x.shape[0]
    tm = min(t, 256)

    def body(ch_ref, cc_ref, x_ref, tg_ref, wo_ref, w1_ref, w2_ref, g1_ref, b1_ref, g2_ref, b2_ref,
             h1_ref, r_ref, da_ref, dp2b_ref, dp1_ref, dcat_ref, dg1_ref, db1_ref, dg2_ref, db2_ref, loss_ref, gwo_ref,
             gwo_acc, gwo_narrow, sem):
        @pl.when(pl.program_id(0) == 0)
        def _():
            for ref in (dg1_ref, db1_ref, dg2_ref, db2_ref, loss_ref, gwo_acc):
                ref[...] = jnp.zeros_like(ref)

        mix = _dot(ch_ref[...], wo_ref[0:GROUP, :], NN) + _dot(cc_ref[...], wo_ref[GROUP:2 * GROUP, :], NN)
        xhat1, rstd1 = _layer_norm(ALPHA * x_ref[...] + mix)
        h1 = xhat1 * g1_ref[...] + b1_ref[...]
        h1b = h1.astype(h1_ref.dtype)
        h1_ref[...] = h1b
        mlp = jnp.zeros((tm, D_MODEL), F32)
        for j in range(N_FF):
            cols = slice(j * FF_BLOCK, (j + 1) * FF_BLOCK)
            r = jnp.square(jnp.maximum(_dot(h1b, w1_ref[:, cols], NN), 0.0)).astype(r_ref.dtype)
            r_ref[:, cols] = r
            mlp = mlp + _dot(r, w2_ref[cols, :], NN)
        xhat2, rstd2 = _layer_norm(ALPHA * h1 + mlp)
        err = xhat2 * g2_ref[...] + b2_ref[...] - tg_ref[...]
        loss_ref[...] += 0.5 * jnp.sum(jnp.mean(err * err, axis=-1, keepdims=True))
        dy = err * (1.0 / D_MODEL)
        dg2_ref[...] += jnp.sum(dy * xhat2, axis=0, keepdims=True)
        db2_ref[...] += jnp.sum(dy, axis=0, keepdims=True)
        dp2 = _ln_bwd(dy, xhat2, rstd2, g2_ref[...])
        dp2b = dp2.astype(dp2b_ref.dtype)
        dp2b_ref[...] = dp2b
        back = jnp.zeros((tm, D_MODEL), F32)
        for j in range(N_FF):
            cols = slice(j * FF_BLOCK, (j + 1) * FF_BLOCK)
            dr = _dot(dp2b, w2_ref[cols, :], NT)
            da = (dr * (2.0 * jnp.sqrt(r_ref[:, cols].astype(F32)))).astype(da_ref.dtype)
            da_ref[:, cols] = da
            back = back + _dot(da, w1_ref[:, cols], NT)
        dh1 = ALPHA * dp2 + back
        dg1_ref[...] += jnp.sum(dh1 * xhat1, axis=0, keepdims=True)
        db1_ref[...] += jnp.sum(dh1, axis=0, keepdims=True)
        dp1 = _ln_bwd(dh1, xhat1, rstd1, g1_ref[...])
        dp1b = dp1.astype(MXU_DTYPE)
        dp1_ref[...] = dp1
        dcat_ref[...] = _dot(dp1b, wo_ref[...], NT)
        gwo_acc[0:GROUP, :] += _dot(ch_ref[...], dp1b, TN)
        gwo_acc[GROUP:2 * GROUP, :] += _dot(cc_ref[...], dp1b, TN)

        @pl.when(pl.program_id(0) == pl.num_programs(0) - 1)
        def _():
            gwo_narrow[...] = gwo_acc[...].astype(gwo_narrow.dtype)
            copy = pltpu.make_async_copy(gwo_narrow, gwo_ref, sem.at[0])
            copy.start()
            copy.wait()

    row = pl.BlockSpec((tm, D_MODEL), lambda i: (i, 0))
    wide = pl.BlockSpec((tm, D_FF), lambda i: (i, 0))
    vec = pl.BlockSpec((1, D_MODEL), lambda i: (0, 0))
    narrow = lambda dtype: jax.ShapeDtypeStruct((t, D_MODEL), dtype)
    return pl.pallas_call(
        body, name="sublayers", grid=(t // tm,),
        in_specs=[pl.BlockSpec((tm, GROUP), lambda i: (i, 0)), pl.BlockSpec((tm, GROUP), lambda i: (i, 0)), row, row,
                  _resident((D_MODEL, D_MODEL)),
                  _resident((D_MODEL, D_FF)), _resident((D_FF, D_MODEL)), vec, vec, vec, vec],
        out_specs=[row, wide, wide, row, row, row, vec, vec, vec, vec,
                   pl.BlockSpec((SUBLANES, LANES), lambda i: (0, 0)), ANY],
        out_shape=[narrow(BF16), jax.ShapeDtypeStruct((t, D_FF), BF16), jax.ShapeDtypeStruct((t, D_FF), BF16),
                   narrow(BF16), narrow(F32), narrow(F32)]
                  + [jax.ShapeDtypeStruct((1, D_MODEL), F32)] * 4
                  + [jax.ShapeDtypeStruct((SUBLANES, LANES), F32), jax.ShapeDtypeStruct((D_MODEL, D_MODEL), BF16)],
        scratch_shapes=[pltpu.VMEM((D_MODEL, D_MODEL), F32), pltpu.VMEM((D_MODEL, D_MODEL), BF16),
                        pltpu.SemaphoreType.DMA((1,))],
        compiler_params=_params("arbitrary"),
    )(cat_h, cat_c, x, target, w_out, w_ff1, w_ff2, g1, b1, g2, b2)


def _hgrn_bwd(proj, do, states, lb_logits, after):
    t = proj.shape[1]
    tb = min(t, 512)
    ncb = tb // CHUNK
    nblk = t // tb

    def body(q_ref, f_ref, v_ref, do_ref, st_ref, lbl_ref, after_ref, dp_ref, dlbl_ref, ds_scr, dlb_scr):
        i = pl.program_id(0)

        @pl.when(i == 0)
        def _():
            ds_scr[...] = jnp.zeros_like(ds_scr)
            dlb_scr[...] = jnp.zeros_like(dlb_scr)

        lb, s1 = _lower_bound(lbl_ref[...])
        causal, anti = _chunk_masks()
        every = range(ncb)
        rows = [slice(c * CHUNK, (c + 1) * CHUNK) for c in every]
        q, v, do = ([ref[r, :] for r in rows] for ref in (q_ref, v_ref, do_ref))
        st = [st_ref[c] for c in every]
        gates = [_gates(f_ref[r, :], lb) for r in rows]
        sig, f, k = ([gt[n] for gt in gates] for n in (0, 1, 3))
        b = [_dot_exact(causal, gt[2]) for gt in gates]
        mid, last = [x[CHUNK // 2:CHUNK // 2 + 1, :] for x in b], [x[CHUNK - 1:CHUNK, :] for x in b]
        e_q = [jnp.exp(b[c] - mid[c]) for c in every]
        e_k = [jnp.exp(mid[c] - b[c]) for c in every]
        e_i = [jnp.exp(x) for x in b]
        e_s = [jnp.exp(last[c] - b[c]) for c in every]
        dec = [jnp.exp(x) for x in last]
        qt, kt, qi, ks = ([a[c] * e[c] for c in every] for a, e in ((q, e_q), (k, e_k), (q, e_i), (k, e_s)))

        def masked(a, b_):
            return [[jnp.where(causal, _dot(a_h, b_h, NT), 0.0) for a_h, b_h in zip(_heads(a[c]), _heads(b_[c]))]
                    for c in every]

        def with_scores(s, other, dims):
            return [jnp.concatenate([_dot(s_h, o_h, dims) for s_h, o_h in zip(s[c], _heads(other[c]))], axis=1)
                    for c in every]

        def per_head(dims, a, b_):
            return [_per_head(lambda a_h, b_h: _dot(a_h, b_h, dims), a[c], b_[c]) for c in every]

        scores, dscores = masked(qt, kt), masked(do, v)
        dqt, dkt, dv_intra = with_scores(dscores, kt, NN), with_scores(dscores, qt, TN), with_scores(scores, do, TN)
        dqi, update = per_head(NN, do, st), per_head(TN, do, qi)

        dst = ds_scr[...]
        dsts = [None] * ncb
        for c in reversed(every):
            dsts[c] = dst
            dst = dec[c] * dst + update[c]
        ds_scr[...] = dst

        dv_state, dks = per_head(NT, ks, dsts), per_head(NN, v, dsts)
        ddec = [jnp.sum(dsts[c] * st[c], axis=0, keepdims=True) for c in every]
        dq = [dqt[c] * e_q[c] + dqi[c] * e_i[c] for c in every]
        dk = [dkt[c] * e_k[c] + dks[c] * e_s[c] for c in every]
        db = [q[c] * dq[c] - k[c] * dk[c] for c in every]
        db_last = [jnp.sum(dks[c] * ks[c], axis=0, keepdims=True) + ddec[c] * dec[c] for c in every]
        dg = [_dot_exact(anti, db[c]) + db_last[c] for c in every]
        df = [dg[c] / f[c] - dk[c] for c in every]
        dlb_scr[...] += sum(jnp.sum(df[c] * (1.0 - sig[c]), axis=0, keepdims=True) for c in every)
        dfp = [df[c] * (1.0 - lb) * sig[c] * (1.0 - sig[c]) for c in every]
        dv = [dv_intra[c] + dv_state[c] for c in every]
        for n, parts in enumerate((dq, dfp, dv)):
            dp_ref[n] = jnp.concatenate(parts, axis=0).astype(dp_ref.dtype)

        @pl.when(i == nblk - 1)
        def _():
            dlb = dlb_scr[...]
            dlbl_ref[0:1, :] = dlb * lb * (1.0 - lb)
            dlbl_ref[1:2, :] = -dlb * lb * s1

    grp = lambda g: pl.BlockSpec((None, tb, GROUP), lambda i: (g, nblk - 1 - i, 0))
    vec = pl.BlockSpec((2, HGRN_WIDTH), lambda i: (0, 0))
    return pl.pallas_call(
        body, name="hgrn_bwd", grid=(nblk,),
        in_specs=[grp(0), grp(1), grp(2), pl.BlockSpec((tb, HGRN_WIDTH), lambda i: (nblk - 1 - i, 0)),
                  pl.BlockSpec((ncb, HEAD_DIM, HGRN_WIDTH), lambda i: (nblk - 1 - i, 0, 0)), vec, ANY],
        out_specs=[pl.BlockSpec((3, tb, HGRN_WIDTH), lambda i: (0, nblk - 1 - i, 0)), vec],
        out_shape=[jax.ShapeDtypeStruct((3, t, HGRN_WIDTH), BF16), jax.ShapeDtypeStruct((2, HGRN_WIDTH), F32)],
        scratch_shapes=[pltpu.VMEM((HEAD_DIM, HGRN_WIDTH), F32), pltpu.VMEM((1, HGRN_WIDTH), F32)],
        compiler_params=_params("arbitrary"),
    )(proj, proj, proj, do, states, lb_logits, after)


GRAD_TILE = 512
OUT_PARTS = 4


class _Side:
    def __init__(self, operands, in_specs, out_shape, out_specs, scratch, init, begin):
        self.operands, self.in_specs, self.out_shape, self.out_specs = operands, in_specs, out_shape, out_specs
        self.scratch, self.init, self.begin = scratch, init, begin


def _grad_w(name, operands, widths, shape, step, after=None, side=None):
    t = operands[0].shape[-2]
    tt = min(t, GRAD_TILE)
    n_in, n_steps = len(operands), t // tt
    in_specs = [pl.BlockSpec((tt, w), lambda k: (k, 0)) if a.ndim == 2 else
                pl.BlockSpec((a.shape[0], tt, w), lambda k: (0, k, 0)) for a, w in zip(operands, widths)]
    extra = [] if after is None else [after]
    s_in, s_out = (len(side.operands), len(side.out_shape)) if side else (0, 0)
    first_out = n_in + s_in + len(extra)

    def body(*refs):
        o_ref, side_outs = refs[first_out], refs[first_out + 1:first_out + 1 + s_out]
        acc, narrow, sem = refs[first_out + 1 + s_out:first_out + 4 + s_out]
        k = pl.program_id(0)

        @pl.when(k == 0)
        def _():
            acc[...] = jnp.zeros_like(acc)
            if side:
                side.init(side_outs)

        tick = side.begin(k, n_steps, refs[n_in:n_in + s_in], side_outs, refs[first_out + 4 + s_out:]) if side else None
        step(acc, *refs[:n_in], tick or (lambda j: None))

        @pl.when(k == n_steps - 1)
        def _():
            part = shape[0] // OUT_PARTS
            copies = []
            for p in range(OUT_PARTS):
                rows = pl.ds(p * part, part)
                narrow[rows, :] = acc[rows, :].astype(narrow.dtype)
                copies.append(pltpu.make_async_copy(narrow.at[rows, :], o_ref.at[rows, :], sem.at[p]))
                copies[-1].start()
            for cp in copies:
                cp.wait()

    outs = pl.pallas_call(
        body, name=name, grid=(n_steps,),
        in_specs=in_specs + (side.in_specs if side else []) + [ANY] * len(extra),
        out_specs=[ANY] + (side.out_specs if side else []),
        out_shape=[jax.ShapeDtypeStruct(shape, BF16)] + (side.out_shape if side else []),
        scratch_shapes=[pltpu.VMEM(shape, F32), pltpu.VMEM(shape, BF16), pltpu.SemaphoreType.DMA((OUT_PARTS,))]
                       + (side.scratch if side else []),
        compiler_params=_params("arbitrary"),
    )(*operands, *(side.operands if side else ()), *extra)
    return outs if side else outs[0]


def _dw_in(xb, dph, dog, dpc, w_in, dpre1, after):
    t = xb.shape[0]

    def step(acc, x_ref, dh_ref, dog_ref, dc_ref, tick):
        xv = x_ref[...]
        for g in range(N_GROUPS):
            part = dh_ref[g] if g < 3 else dog_ref[...] if g == 3 else dc_ref[g - 4]
            acc[:, g * GROUP:(g + 1) * GROUP] += _dot(xv, part, TN)
            tick(g, part)

    def begin(k, n_steps, ins, outs, scratch):
        w_ref, dp_ref = ins
        total = [ALPHA * dp_ref[...]]

        def tick(g, part):
            total[0] = total[0] + _dot(part, w_ref[:, g * GROUP:(g + 1) * GROUP], NT)
            if g == N_GROUPS - 1:
                outs[0][...] = total[0]

        return tick

    row = pl.BlockSpec((min(t, GRAD_TILE), D_MODEL), lambda k: (k, 0))
    side = _Side((w_in, dpre1), [_resident((D_MODEL, IN_COLS)), row], [jax.ShapeDtypeStruct((t, D_MODEL), F32)], [row],
                 [], lambda outs: None, begin)
    return _grad_w("dw_in", (xb, dph, dog, dpc), (D_MODEL, GROUP, GROUP, GROUP), (D_MODEL, IN_COLS), step, after, side)


def _strips_of(j, tt):
    per_tick = tt // GATE_STRIP // N_FF
    return [slice(s * GATE_STRIP, (s + 1) * GATE_STRIP) for s in range(j * per_tick, (j + 1) * per_tick)]


def _dw_ff1(h1b, da, dcat, o, proj, gate_norm_w, after):
    t = h1b.shape[0]
    tt = min(t, GRAD_TILE)

    def step(acc, h_ref, da_ref, tick):
        hv = h_ref[...]
        for j in range(N_FF):
            cols = slice(j * FF_BLOCK, (j + 1) * FF_BLOCK)
            acc[:, cols] += _dot(hv, da_ref[:, cols], TN)
            tick(j)

    def init(outs):
        outs[2][...] = jnp.zeros_like(outs[2])

    def begin(k, n_steps, ins, outs, scratch):
        do2_ref, o_ref, og_ref, gnw_ref = ins
        do_ref, dog_ref, dgnw_ref = outs
        total = [jnp.zeros((GATE_STRIP, GROUP), F32)]

        def tick(j):
            gnw = gnw_ref[...]
            for rows in _strips_of(j, tt):
                ov, og, do2 = o_ref[rows, :], og_ref[rows, :], do2_ref[rows, :]
                rs = _per_head(lambda o_h: jnp.broadcast_to(
                    lax.rsqrt(jnp.mean(o_h * o_h, axis=-1, keepdims=True) + EPS), o_h.shape), ov)
                on = ov * rs
                sg = _sigmoid(og)
                sil = og * sg
                don = do2 * gnw * sil
                total[0] = total[0] + do2 * on * sil
                dog_ref[rows, :] = (do2 * on * gnw * (sg * (1.0 + og * (1.0 - sg)))).astype(dog_ref.dtype)
                do_ref[rows, :] = rs * (don - on * _per_head(
                    lambda p_h: jnp.broadcast_to(jnp.mean(p_h, axis=-1, keepdims=True), p_h.shape), don * on))
            if j == N_FF - 1:
                dgnw_ref[...] += jnp.sum(total[0], axis=0, keepdims=True)

        return tick

    tile = pl.BlockSpec((tt, GROUP), lambda k: (k, 0))
    vec = pl.BlockSpec((1, GROUP), lambda k: (0, 0))
    side = _Side(
        (dcat, o, proj, gate_norm_w), [tile, tile, pl.BlockSpec((None, tt, GROUP), lambda k: (3, k, 0)), vec],
        [jax.ShapeDtypeStruct((t, HGRN_WIDTH), F32), jax.ShapeDtypeStruct((t, HGRN_WIDTH), BF16),
         jax.ShapeDtypeStruct((1, HGRN_WIDTH), F32)], [tile, tile, vec], [], init, begin)
    return _grad_w("dw_ff1", (h1b, da), (D_MODEL, D_FF), (D_MODEL, D_FF), step, after, side)


def _dw_ff2(r, dpre2b, dcat, bcu, conv_w):
    t = r.shape[0]
    tt = min(t, GRAD_TILE)
    hb = tt // SUBLANES
    halo = 2 * SUBLANES

    def step(acc, r_ref, d_ref, tick):
        dv = d_ref[...]
        for j in range(N_FF):
            rows = slice(j * FF_BLOCK, (j + 1) * FF_BLOCK)
            acc[rows, :] += _dot(r_ref[:, rows], dv, TN)
            tick(j)

    def init(outs):
        outs[1][...] = jnp.zeros_like(outs[1])

    def begin(k, n_steps, ins, outs, scratch):
        dy_ref, dyn_ref, b_ref, bn_ref, c_ref, u_ref, ch_ref, uh_ref, cw_ref = ins
        dp_ref, dcw_ref = outs
        zbuf, dbuf = scratch
        before = lambda ref: ref[SUBLANES:halo, :].astype(F32)
        zbuf[0:SUBLANES, :] = jnp.where(k > 0, before(ch_ref) * before(uh_ref), 0.0)
        zbuf[SUBLANES:SUBLANES + tt, :] = c_ref[...].astype(F32) * u_ref[...].astype(F32)
        dbuf[0:tt, :] = dy_ref[...] * b_ref[...].astype(F32)
        dbuf[tt:tt + SUBLANES, :] = jnp.where(k < n_steps - 1, dyn_ref[...] * bn_ref[0:SUBLANES, :].astype(F32), 0.0)
        totals = [jnp.zeros((GATE_STRIP, GROUP), F32) for _ in range(3)]

        def tick(j):
            cw = cw_ref[...]
            for rows in _strips_of(j, tt):
                at = lambda buf, shift: buf[shift + rows.start:shift + rows.stop, :]
                z, z1, z2 = at(zbuf, SUBLANES), at(zbuf, SUBLANES - 1), at(zbuf, SUBLANES - 2)
                dyc, d1, d2 = at(dbuf, 0), at(dbuf, 1), at(dbuf, 2)
                yc = cw[2:3, :] * z + cw[1:2, :] * z1 + cw[0:1, :] * z2
                dz = cw[2:3, :] * dyc + cw[1:2, :] * d1 + cw[0:1, :] * d2
                dp_ref[0, rows, :] = (dy_ref[rows, :] * yc).astype(dp_ref.dtype)
                dp_ref[1, rows, :] = (dz * u_ref[rows, :].astype(F32)).astype(dp_ref.dtype)
                dp_ref[2, rows, :] = (dz * c_ref[rows, :].astype(F32)).astype(dp_ref.dtype)
                for n, tap in enumerate((z2, z1, z)):
                    totals[n] = totals[n] + dyc * tap
            if j == N_FF - 1:
                for n in range(3):
                    dcw_ref[n:n + 1, :] += jnp.sum(totals[n], axis=0, keepdims=True)

        return tick

    grp = lambda g: pl.BlockSpec((None, tt, GROUP), lambda k: (g, k, 0))
    prev = lambda g: pl.BlockSpec((None, halo, GROUP), lambda k: (g, jnp.maximum(k * (tt // halo) - 1, 0), 0))
    nxt = lambda g: pl.BlockSpec((None, halo, GROUP), lambda k: (g, jnp.minimum((k + 1) * (tt // halo), t // halo - 1), 0))
    nxt_row = lambda k: jnp.minimum((k + 1) * hb, t // SUBLANES - 1)
    whole = pl.BlockSpec((3, CONV_WIDTH), lambda k: (0, 0))
    side = _Side(
        (dcat, dcat, bcu, bcu, bcu, bcu, bcu, bcu, conv_w),
        [pl.BlockSpec((tt, GROUP), lambda k: (k, 1)), pl.BlockSpec((SUBLANES, GROUP), lambda k: (nxt_row(k), 1)),
         grp(0), nxt(0), grp(1), grp(2), prev(1), prev(2), whole],
        [jax.ShapeDtypeStruct((3, t, CONV_WIDTH), BF16), jax.ShapeDtypeStruct((3, CONV_WIDTH), F32)],
        [pl.BlockSpec((3, tt, GROUP), lambda k: (0, k, 0)), whole],
        [pltpu.VMEM((tt + SUBLANES, GROUP), F32), pltpu.VMEM((tt + SUBLANES, GROUP), F32)], init, begin)
    return _grad_w("dw_ff2", (r, dpre2b), (D_FF, D_MODEL), (D_FF, D_MODEL), step, side=side)


def _place():
    x, y, c = lax.axis_index("x"), lax.axis_index("y"), lax.axis_index("c")
    return x, y, c, 2 * x + y


def _other_chips(x, y):
    return [(1 - x, y), (x, 1 - y), (1 - x, 1 - y)]


def _place_shard(name, w, chip, cols_sharded, after=None):
    rows, cols = w.shape
    tr = min(rows, 256)
    nb = rows // tr
    full = (rows, cols * N_CHIPS) if cols_sharded else (rows * N_CHIPS, cols)
    out_map = (lambda i, s: (i, s[0])) if cols_sharded else (lambda i, s: (s[0] * nb + i, 0))

    def body(s_ref, w_ref, *rest):
        rest[-1][...] = w_ref[...].astype(rest[-1].dtype)

    extra = [] if after is None else [after]
    return pl.pallas_call(
        body, name=name,
        grid_spec=pltpu.PrefetchScalarGridSpec(
            num_scalar_prefetch=1, grid=(nb,),
            in_specs=[pl.BlockSpec((tr, cols), lambda i, s: (i, 0))] + [ANY] * len(extra),
            out_specs=pl.BlockSpec((tr, cols), out_map)),
        out_shape=jax.ShapeDtypeStruct(full, BF16),
        compiler_params=_params("parallel"),
    )(chip, w, *extra)


HBM = pl.BlockSpec(memory_space=pltpu.HBM)
SEM = pl.BlockSpec(memory_space=pltpu.SEMAPHORE)
EFFECT = pltpu.SideEffectType.DATAFLOW_SIDE_EFFECTING


PEER_SETS = {
    "sibling": (0, lambda x, y, c: [(x, y, 1 - c)]),
    "chips": (1, lambda x, y, c: [(1 - x, y, c), (x, 1 - y, c), (1 - x, 1 - y, c)]),
    "neighbours": (2, lambda x, y, c: [(1 - x, y, c), (x, 1 - y, c)]),
}


class _Split:
    def __init__(self, name, arrays, plan, others=(), peers=None):
        n_own, arrays = len(arrays), (*arrays, *others)
        n, n_copies = len(arrays), plan.count
        self.name, self.plan, self.n = name, plan, n_own
        barrier_id, peer_ids = PEER_SETS[peers] if peers else (None, None)

        def body(*refs):
            if peers:
                x, y, c, _ = _place()
                barrier = pltpu.get_barrier_semaphore()
                for peer in peer_ids(x, y, c):
                    pl.semaphore_signal(barrier, inc=1, device_id=peer, device_id_type=MESH)
                pl.semaphore_wait(barrier, len(peer_ids(0, 0, 0)))
            send_sems, recv_sems, token = refs[n], refs[n + 1], refs[-1]
            for k, (src, dst, to) in enumerate(plan(refs[:n])):
                pltpu.make_async_remote_copy(src_ref=src, dst_ref=dst, send_sem=send_sems.at[k], recv_sem=recv_sems.at[k],
                                             device_id=to, device_id_type=MESH).start()
            token[...] = jnp.zeros_like(token)

        outs = pl.pallas_call(
            body, name=name + "_start",
            out_shape=(pltpu.SemaphoreType.DMA((n_copies,)), pltpu.SemaphoreType.DMA((n_copies,)),
                       *[pltpu.HBM(a.shape, a.dtype) for a in arrays], jax.ShapeDtypeStruct((SUBLANES, LANES), F32)),
            in_specs=(HBM,) * n, out_specs=(SEM, SEM) + (HBM,) * n + (pl.BlockSpec(memory_space=pltpu.VMEM),),
            input_output_aliases={i: 2 + i for i in range(n)},
            compiler_params=pltpu.CompilerParams(has_side_effects=EFFECT, collective_id=barrier_id),
        )(*[pltpu.with_memory_space_constraint(a, pltpu.HBM) for a in arrays])
        self.sems, self.arrays, self.others, self.token = outs[:2], outs[2:2 + n_own], outs[2 + n_own:2 + n], outs[-1]

    def wait(self, after):
        n, plan = self.n, self.plan

        def body(*refs):
            send_sems, recv_sems = refs[n], refs[n + 1]
            for k, (src, dst, to) in enumerate(plan(refs[:n])):
                cp = pltpu.make_async_remote_copy(src_ref=src, dst_ref=dst, send_sem=send_sems.at[k],
                                                  recv_sem=recv_sems.at[k], device_id=to, device_id_type=MESH)
                cp.wait_send()
                cp.wait_recv()

        return pl.pallas_call(
            body, name=self.name + "_wait", out_shape=tuple(pltpu.HBM(a.shape, a.dtype) for a in self.arrays),
            in_specs=(HBM,) * n + (SEM, SEM, ANY), out_specs=(HBM,) * n, input_output_aliases={i: i for i in range(n)},
            compiler_params=pltpu.CompilerParams(has_side_effects=EFFECT),
        )(*self.arrays, *self.sems, after)


COLS_SHARDED = (True, False, True, False)
HALF_SHAPES = [(D_MODEL // 2, IN_COLS), (D_MODEL, D_MODEL // 2), (D_MODEL // 2, D_FF), (D_FF, D_MODEL // 2)]
PIECE_SHAPES = [(D_MODEL // 2, IN_COLS // N_CHIPS), (D_MODEL // N_CHIPS, D_MODEL // 2),
                (D_MODEL // 2, D_FF // N_CHIPS), (D_FF // N_CHIPS, D_MODEL // 2)]


def _shard_view(kind, ref, chip):
    if COLS_SHARDED[kind]:
        n = ref.shape[1] // N_CHIPS
        return ref.at[:, pl.ds(chip * n, n)]
    n = ref.shape[0] // N_CHIPS
    return ref.at[pl.ds(chip * n, n), :]


def _half_view(kind, ref, h):
    if COLS_SHARDED[kind]:
        n = ref.shape[0] // 2
        return ref.at[pl.ds(h * n, n), :]
    n = ref.shape[1] // 2
    return ref.at[:, pl.ds(h * n, n)]


def _plan(count):
    def mark(fn):
        fn.count = count
        return fn
    return mark


def _shard_rows_view(kind, ref, chip, part, n_parts):
    if COLS_SHARDED[kind]:
        m, n = ref.shape[0] // n_parts, ref.shape[1] // N_CHIPS
        return ref.at[pl.ds(part * m, m), pl.ds(chip * n, n)]
    m = ref.shape[0] // N_CHIPS // n_parts
    return ref.at[pl.ds((n_parts * chip + part) * m, m), :]


def _shard_half_view(kind, ref, chip, h):
    return _shard_rows_view(kind, ref, chip, h, 2)


def _gather_over_ici(kinds, weights):
    @_plan(2 * len(kinds))
    def plan(refs):
        x, y, c, me = _place()
        mine = [_shard_half_view(kind, ref, me, c) for kind, ref in zip(kinds, refs)]
        return [(v, v, to) for v in mine for to in ((1 - x, y, c), (x, 1 - y, c))]

    return _Split("gather_ici_" + "".join(map(str, kinds)), tuple(weights), plan, peers="neighbours")


def _relay_over_ici(kinds, weights, others=()):
    @_plan(2 * len(kinds))
    def plan(refs):
        x, y, c, _ = _place()
        x_nbr, y_nbr = 2 * (1 - x) + y, 2 * x + (1 - y)
        out = []
        for kind, ref in zip(kinds, refs):
            first, second = (_shard_rows_view(kind, ref, chip, 2 * c + q, 4) for q, chip in ((0, x_nbr), (1, y_nbr)))
            out += [(first, first, (x, 1 - y, c)), (second, second, (1 - x, y, c))]
        return out

    return _Split("relay_ici_" + "".join(map(str, kinds)), tuple(weights), plan, others, peers="neighbours")


def _gather_w_in_over_ici(w_in, conv4):
    @_plan(6)
    def plan(refs):
        x, y, c, me = _place()
        half, conv = _shard_half_view(0, refs[0], me, c), refs[1].at[me]
        return [(v, v, (px, py, c)) for v in (half, conv) for px, py in _other_chips(x, y)]

    return _Split("gather_w_in_ici", (w_in, conv4), plan, peers="chips")


def _gather_over_d2d(kinds, weights):
    @_plan(3 * len(kinds))
    def plan(refs):
        x, y, c, _ = _place()
        got = [_shard_half_view(kind, ref, 2 * px + py, c) for kind, ref in zip(kinds, refs)
               for px, py in _other_chips(x, y)]
        return [(v, v, (x, y, 1 - c)) for v in got]

    return _Split("gather_d2d_" + "".join(map(str, kinds)), tuple(weights), plan, peers="sibling")


def _swap_halves(kinds, grads):
    @_plan(len(kinds))
    def plan(refs):
        x, y, c, _ = _place()
        return [(_half_view(kind, g, 1 - c), land, (x, y, 1 - c))
                for kind, g, land in zip(kinds, refs[:len(kinds)], refs[len(kinds):])]

    lands = [lax.empty(HALF_SHAPES[kind], g.dtype) for kind, g in zip(kinds, grads)]
    return _Split("swap_halves_" + "".join(map(str, kinds)), (*grads, *lands), plan, peers="sibling")


def _block_rows(cols, elements):
    return 1 << ((elements // cols).bit_length() - 1)


def _add_half(name, g, recv, core, rows_split):
    shape = recv.shape
    tr = min(shape[0], _block_rows(shape[1], 1 << 20))
    nb = shape[0] // tr

    def body(c_ref, g_ref, r_ref, o_ref):
        o_ref[...] = (g_ref[...].astype(F32) + r_ref[...].astype(F32)).astype(o_ref.dtype)

    g_map = (lambda i, c_ref: (c_ref[0] * nb + i, 0)) if rows_split else (lambda i, c_ref: (i, c_ref[0]))
    blk = pl.BlockSpec((tr, shape[1]), lambda i, c_ref: (i, 0))
    return pl.pallas_call(
        body, name=name,
        grid_spec=pltpu.PrefetchScalarGridSpec(
            num_scalar_prefetch=1, grid=(nb,),
            in_specs=[pl.BlockSpec((tr, shape[1]), g_map), blk], out_specs=blk),
        out_shape=jax.ShapeDtypeStruct(shape, BF16),
        compiler_params=_params("parallel"),
    )(core, g, recv)


def _exchange_pieces(kinds, halves, pack=None):
    n_p, n = N_CHIPS - 1, len(kinds)

    @_plan(n_p * n + (0 if pack is None else N_DEV - 1))
    def plan(refs):
        x, y, c, _ = _place()
        copies = []
        if pack is not None:
            me = 4 * x + 2 * y + c
            peers = [((1 - x) if m & 4 else x, (1 - y) if m & 2 else y, (1 - c) if m & 1 else c) for m in range(1, N_DEV)]
            copies += [(refs[2 * n], refs[2 * n + 1].at[me], peer) for peer in peers]
        return copies + [(_shard_view(kind, half, 2 * px + py), land.at[j], (px, py, c))
                         for j, (px, py) in enumerate(_other_chips(x, y))
                         for kind, half, land in zip(kinds, refs[:n], refs[n:2 * n])]

    lands = [lax.empty((n_p,) + PIECE_SHAPES[kind], BF16) for kind in kinds]
    small = () if pack is None else (pack, lax.empty((N_DEV,) + pack.shape, F32))
    return _Split("exchange_pieces_" + "".join(map(str, kinds)), (*halves, *lands, *small), plan,
                  peers="chips" if pack is None else None)


def _sum_pieces(name, half, slots, place, rows_split, after):
    n_p, rows, cols = slots.shape
    tr = min(rows, _block_rows(cols, 1 << 18))
    nb = rows // tr
    if rows_split:
        own_map = lambda i, s: (i, s[0])
        out_map = lambda i, s: (s[1] * nb + i, 0)
        shard = (2 * rows, cols)
    else:
        own_map = lambda i, s: (s[0] * nb + i, 0)
        out_map = lambda i, s: (i, s[1])
        shard = (rows, 2 * cols)

    def body(s_ref, own_ref, slot_ref, after_ref, o_ref):
        total = own_ref[...].astype(F32)
        for j in range(n_p):
            total = total + slot_ref[j].astype(F32)
        o_ref[...] = total

    return pl.pallas_call(
        body, name=name,
        grid_spec=pltpu.PrefetchScalarGridSpec(
            num_scalar_prefetch=1, grid=(nb,),
            in_specs=[pl.BlockSpec((tr, cols), own_map), pl.BlockSpec((n_p, tr, cols), lambda i, s: (0, i, 0)), ANY],
            out_specs=pl.BlockSpec((tr, cols), out_map)),
        out_shape=jax.ShapeDtypeStruct(shard, F32),
        compiler_params=_params("parallel"),
    )(place, half, slots, after)


def _join_halves(kinds, shards):
    @_plan(len(kinds))
    def plan(refs):
        x, y, c, _ = _place()
        return [(_half_view(kind, g, c), _half_view(kind, g, c), (x, y, 1 - c)) for kind, g in zip(kinds, refs)]

    return _Split("join_halves_" + "".join(map(str, kinds)), tuple(shards), plan, peers="sibling")


N_DEV = 8


def _sum_shared(pack, land, device):
    def body(d_ref, p_ref, l_ref, o_ref):
        me = d_ref[0]
        total = jnp.where(me == 0, p_ref[...], l_ref[0])
        for d in range(1, N_DEV):
            total = total + jnp.where(me == d, p_ref[...], l_ref[d])
        o_ref[...] = total

    return pl.pallas_call(
        body, name="sum_shared",
        grid_spec=pltpu.PrefetchScalarGridSpec(
            num_scalar_prefetch=1, grid=(1,),
            in_specs=[pl.BlockSpec(pack.shape, lambda i, d: (0, 0)), pl.BlockSpec(land.shape, lambda i, d: (0, 0, 0))],
            out_specs=pl.BlockSpec(pack.shape, lambda i, d: (0, 0))),
        out_shape=jax.ShapeDtypeStruct(pack.shape, F32),
    )(device, pack, land)


def _adamw(name, w, g, m, v, after=None):
    rows, cols = w.shape
    tr = min(rows, 256)
    extra = [] if after is None else [after]

    def body(w_ref, g_ref, m_ref, v_ref, *rest):
        go_ref, d_ref, nm_ref, nv_ref = rest[-4:]
        g = g_ref[...]
        go_ref[...] = g
        d_ref[...], nm_ref[...], nv_ref[...] = _adam_step(w_ref[...], g, m_ref[...], v_ref[...])

    blk = pl.BlockSpec((tr, cols), lambda i: (i, 0))
    return pl.pallas_call(
        body, name=name, grid=(rows // tr,), in_specs=[blk] * 4 + [ANY] * len(extra), out_specs=[blk] * 4,
        out_shape=[jax.ShapeDtypeStruct(w.shape, F32)] * 4,
        compiler_params=_params("parallel"),
    )(w, g, m, v, *extra)


def _adam_step(w, g, m, v):
    nm = ADAM_B1 * m + (1.0 - ADAM_B1) * g
    nv = ADAM_B2 * v + (1.0 - ADAM_B2) * jnp.square(g)
    m_hat = nm * (1.0 / (1.0 - ADAM_B1 ** ADAM_STEP))
    v_hat = nv * (1.0 / (1.0 - ADAM_B2 ** ADAM_STEP))
    return -ADAM_LR * (m_hat / (jnp.sqrt(v_hat) + ADAM_EPS) + ADAM_WD * w), nm, nv


def _adamw_small(tot, chip, weights, ms, vs, after):
    n, half = len(weights), D_MODEL // 2

    def body(chip_ref, tot_ref, *refs):
        ins, outs = refs[:3 * n], refs[3 * n + 1:]
        tot = tot_ref[...]
        conv_all = jnp.concatenate([tot[5:6, half:], tot[6:7, :half], tot[6:7, half:]], axis=0)
        conv = sum(jnp.where(chip_ref[0] == s, conv_all[:, s * LANES:(s + 1) * LANES], 0.0) for s in range(N_CHIPS))
        grads = [jnp.concatenate([tot[4:5, :half], tot[4:5, half:]], axis=0), tot[5:6, :half], conv,
                 tot[0:1], tot[1:2], tot[2:3], tot[3:4]]
        for k, g in enumerate(grads):
            delta, nm, nv = _adam_step(ins[k][...], g, ins[n + k][...], ins[2 * n + k][...])
            outs[k][...], outs[n + k][...], outs[2 * n + k][...], outs[3 * n + k][...] = g, delta, nm, nv
        outs[4 * n][...] = tot[7:8, 0:1]

    whole = lambda a: pl.BlockSpec(a.shape, lambda i, s: (0,) * a.ndim)
    arrays = (*weights, *ms, *vs)
    loss = jax.ShapeDtypeStruct((1, 1), F32)
    return pl.pallas_call(
        body, name="adamw_small",
        grid_spec=pltpu.PrefetchScalarGridSpec(
            num_scalar_prefetch=1, grid=(1,), in_specs=[whole(tot)] + [whole(a) for a in arrays] + [ANY],
            out_specs=[whole(a) for a in weights] * 4 + [whole(loss)]),
        out_shape=[jax.ShapeDtypeStruct(a.shape, F32) for a in weights] * 4 + [loss],
    )(chip, tot, *arrays, after)


def kernel(x, w_in, lb_logits, gate_norm_w, conv_w, w_out, ln1_g, ln1_b, w_ff1, w_ff2, ln2_g, ln2_b, loss_target, m_w_in, m_lb_logits, m_gate_norm_w, m_conv_w, m_w_out, m_ln1_g, m_ln1_b, m_w_ff1, m_w_ff2, m_ln2_g, m_ln2_b, v_w_in, v_lb_logits, v_gate_norm_w, v_conv_w, v_w_out, v_ln1_g, v_ln1_b, v_w_ff1, v_w_ff2, v_ln2_g, v_ln2_b):
    xs, tgt = x[0], loss_target[0]
    chip = 2 * lax.axis_index("x") + lax.axis_index("y")
    core = lax.axis_index("c").astype(jnp.int32).reshape(1)
    chip1 = chip.astype(jnp.int32).reshape(1)
    place = jnp.concatenate([chip1, core])

    conv4 = lax.dynamic_update_slice(jnp.zeros((N_CHIPS,) + conv_w.shape[1:], F32), conv_w, (chip, 0, 0))
    ici_in = _gather_w_in_over_ici(_place_shard("place_w_in", w_in[0], chip1, True), conv4)
    rest = (1, 2, 3)
    ici_rest = _gather_over_ici(rest, (_place_shard("place_w_out", w_out[0], chip1, False, after=ici_in.token),
                                       _place_shard("place_w_ff1", w_ff1[0], chip1, True, after=ici_in.token),
                                       _place_shard("place_w_ff2", w_ff2[0], chip1, False, after=ici_in.token)))
    wb_in, cv4 = ici_in.wait(ici_rest.token)
    d2d_in = _gather_over_d2d((0,), (wb_in,))
    wb_in, = d2d_in.wait(d2d_in.token)
    conv_full = cv4.transpose(1, 0, 2).reshape(3, CONV_WIDTH)

    proj, bcu, xb, cat_c = _in_proj(xs, wb_in, conv_full, ici_rest.token)
    relay_rest = _relay_over_ici(rest, ici_rest.wait(proj))
    o, states = _hgrn_fwd(proj, lb_logits, relay_rest.token)
    d2d_rest = _gather_over_d2d(rest, relay_rest.wait(o))
    cat_h = _gate_fwd(proj, o, gate_norm_w, d2d_rest.token)
    wb_out, wb_ff1, wb_ff2 = d2d_rest.wait(cat_h)

    (h1b, r, da, dpre2b, dpre1, dcat, g_ln1_g, g_ln1_b, g_ln2_g, g_ln2_b, loss8, g_out_local) = _sublayers(
        cat_h, cat_c, xs, tgt, wb_out, wb_ff1, wb_ff2, ln1_g, ln1_b, ln2_g, ln2_b)

    names = ("w_in", "w_out", "w_ff1", "w_ff2")

    def add_halves(kinds, grads, lands):
        return [_add_half("add_half_" + names[k], g, ld, core, COLS_SHARDED[k]) for k, g, ld in zip(kinds, grads, lands)]

    def sum_pieces(kinds, halves, lands, after):
        return [_sum_pieces("sum_pieces_" + names[k], h, ld, place, COLS_SHARDED[k], after)
                for k, h, ld in zip(kinds, halves, lands)]

    early = (1, 2, 3)
    g_ff2_local, dpc, g_conv = _dw_ff2(r, dpre2b, dcat, bcu, conv_full)
    swap_a = _swap_halves((1, 3), (g_out_local, g_ff2_local))
    g_ff1_local, do, dog, g_gnw = _dw_ff1(h1b, da, dcat, o, proj, gate_norm_w, swap_a.token)
    swap_b = _swap_halves((2,), (g_ff1_local,))
    swapped_a = swap_a.wait(swap_b.token)
    halves_a = add_halves((1, 3), swapped_a[:2], swapped_a[2:])
    swapped_b = swap_b.wait(halves_a[1])
    halves = (halves_a[0], *add_halves((2,), swapped_b[:1], swapped_b[1:]), halves_a[1])
    exch = _exchange_pieces(early, halves)
    dph, g_lbl = _hgrn_bwd(proj, do, states, lb_logits, exch.token)
    g_in_local, grad_x = _dw_in(xb, dph, dog, dpc, wb_in, dpre1, dph)

    late = (0,)
    swap = _swap_halves(late, (g_in_local,))
    exchanged = exch.wait(swap.token)
    pack = jnp.concatenate([
        g_ln1_g, g_ln1_b, g_ln2_g, g_ln2_b,
        jnp.concatenate([g_lbl[0:1], g_lbl[1:2]], axis=1),
        jnp.concatenate([g_gnw, g_conv[0:1]], axis=1),
        jnp.concatenate([g_conv[1:2], g_conv[2:3]], axis=1),
        jnp.concatenate([loss8[0:1], jnp.zeros((1, D_MODEL - LANES), F32)], axis=1)], axis=0)
    join_a = _join_halves((2,), sum_pieces((2,), exchanged[1:2], exchanged[4:5], swap.token))
    swapped = swap.wait(join_a.token)
    exch = _exchange_pieces(late, add_halves(late, swapped[:1], swapped[1:]), pack)
    join_b = _join_halves((1, 3), sum_pieces((1, 3), exchanged[0:3:2], exchanged[3:6:2], exch.token))
    g_w_ff1, = join_a.wait(join_b.token)
    g_w_ff1, d_ff1, nm_ff1, nv_ff1 = _adamw("adamw_w_ff1", w_ff1[0], g_w_ff1, m_w_ff1[0], v_w_ff1[0])
    g_w_out, g_w_ff2 = join_b.wait(d_ff1)
    g_w_ff2, d_ff2, nm_ff2, nv_ff2 = _adamw("adamw_w_ff2", w_ff2[0], g_w_ff2, m_w_ff2[0], v_w_ff2[0])
    g_w_out, d_out, nm_out, nv_out = _adamw("adamw_w_out", w_out[0], g_w_out, m_w_out[0], v_w_out[0], d_ff2)
    exchanged = exch.wait(d_out)
    tot = _sum_shared(exchanged[2], exchanged[3], 2 * chip1 + core)
    join = _join_halves(late, sum_pieces(late, exchanged[:1], exchanged[1:2], tot))
    small = ("lb_logits", "gate_norm_w", "conv_w", "ln1_g", "ln1_b", "ln2_g", "ln2_b")
    small_out = _adamw_small(
        tot, chip1, (lb_logits, gate_norm_w, conv_w[0], ln1_g, ln1_b, ln2_g, ln2_b),
        (m_lb_logits, m_gate_norm_w, m_conv_w[0], m_ln1_g, m_ln1_b, m_ln2_g, m_ln2_b),
        (v_lb_logits, v_gate_norm_w, v_conv_w[0], v_ln1_g, v_ln1_b, v_ln2_g, v_ln2_b), join.token)
    g_w_in, = join.wait(small_out[0])
    g_w_in, d_in, nm_in, nv_in = _adamw("adamw_w_in", w_in[0], g_w_in, m_w_in[0], v_w_in[0])
    loss = small_out[4 * len(small)][0, 0]

    def results(n_kind, large):
        out = dict(zip(small, small_out[n_kind * len(small):(n_kind + 1) * len(small)]))
        out["conv_w"] = out["conv_w"][None]
        out.update({name: a[None] for name, a in zip(("w_in", "w_out", "w_ff1", "w_ff2"), large)})
        return [out[name] for name in ("w_in", "lb_logits", "gate_norm_w", "conv_w", "w_out", "ln1_g", "ln1_b",
                                       "w_ff1", "w_ff2", "ln2_g", "ln2_b")]

    return (loss, grad_x[None], *results(0, (g_w_in, g_w_out, g_w_ff1, g_w_ff2)),
            *results(1, (d_in, d_out, d_ff1, d_ff2)), *results(2, (nm_in, nm_out, nm_ff1, nm_ff2)),
            *results(3, (nv_in, nv_out, nv_ff1, nv_ff2)))
```

```python
import jax
import jax.numpy as jnp
from jax import lax
from jax.experimental import pallas as pl
from jax.experimental.pallas import tpu as pltpu

F32 = jnp.float32
BF16 = jnp.bfloat16
MXU_DTYPE = jnp.bfloat16

D_MODEL = 1024
HGRN_WIDTH = 512
HEAD_DIM = 128
N_HEADS = 4
CONV_WIDTH = 512
CHUNK = 64
D_FF = 4096
IN_COLS = 3584
GROUP = 512
N_GROUPS = IN_COLS // GROUP
ALPHA = 2.0 ** 0.25
EPS = 1e-5
N_CHIPS = 4
ADAM_LR, ADAM_B1, ADAM_B2, ADAM_EPS, ADAM_WD, ADAM_STEP = 0.001, 0.9, 0.999, 1e-08, 0.01, 10

LANES = 128
SUBLANES = 8
VMEM_LIMIT = 56 * 1024 * 1024
FF_BLOCK = 1024
N_FF = D_FF // FF_BLOCK
GATE_STRIP = 64

NN = (((1,), (0,)), ((), ()))
NT = (((1,), (1,)), ((), ()))
TN = (((0,), (0,)), ((), ()))
MESH = pl.DeviceIdType.MESH
ANY = pl.BlockSpec(memory_space=pl.ANY)


def _dot(a, b, dims):
    return lax.dot_general(a.astype(MXU_DTYPE), b.astype(MXU_DTYPE), dims, preferred_element_type=F32)


def _dot_exact(ones, v):
    ones = ones.astype(jnp.bfloat16)
    hi = v.astype(jnp.bfloat16)
    rest = v - hi.astype(F32)
    mid = rest.astype(jnp.bfloat16)
    low = (rest - mid.astype(F32)).astype(jnp.bfloat16)
    return sum(lax.dot_general(ones, part, NN, preferred_element_type=F32) for part in (hi, mid, low))


def _params(*sem):
    return pltpu.CompilerParams(dimension_semantics=sem, vmem_limit_bytes=VMEM_LIMIT)


def _resident(shape):
    return pl.BlockSpec(shape, lambda *_: (0,) * len(shape), pipeline_mode=pl.Buffered(1))


def _sigmoid(v):
    return 1.0 / (1.0 + jnp.exp(-v))


def _lower_bound(lbl):
    m = jnp.max(lbl, axis=0, keepdims=True)
    e = jnp.exp(lbl - m)
    s = e / jnp.sum(e, axis=0, keepdims=True)
    return s[0:1, :], s[1:2, :]


def _heads(v):
    return [v[:, h * HEAD_DIM:(h + 1) * HEAD_DIM] for h in range(N_HEADS)]


def _per_head(fn, *arrays):
    return jnp.concatenate([fn(*parts) for parts in zip(*map(_heads, arrays))], axis=1)


def _in_proj(x, w_in, conv_w, after):
    t = x.shape[0]
    tm = min(t, 512)

    def body(x_ref, w_ref, cw_ref, after_ref, o_ref, bcu_ref, xb_ref, y_ref, zbuf):
        @pl.when(pl.program_id(0) == 0)
        def _():
            zbuf[tm:tm + SUBLANES, :] = jnp.zeros((SUBLANES, CONV_WIDTH), F32)

        xb = x_ref[...].astype(xb_ref.dtype)
        xb_ref[...] = xb
        group = lambda g: _dot(xb, w_ref[:, g * GROUP:(g + 1) * GROUP], NN)
        for g in range(4):
            o_ref[g] = group(g)
        b_gate, c_gate, u = group(4), group(5), group(6)
        for n, part in enumerate((b_gate, c_gate, u)):
            bcu_ref[n] = part.astype(bcu_ref.dtype)
        zbuf[0:SUBLANES, :] = zbuf[tm:tm + SUBLANES, :]
        zbuf[SUBLANES:SUBLANES + tm, :] = c_gate * u
        cw = cw_ref[...]
        at = lambda shift: zbuf[shift:shift + tm, :]
        conv = cw[2:3, :] * at(SUBLANES) + cw[1:2, :] * at(SUBLANES - 1) + cw[0:1, :] * at(SUBLANES - 2)
        y_ref[...] = (b_gate * conv).astype(y_ref.dtype)

    return pl.pallas_call(
        body, name="in_proj", grid=(t // tm,),
        in_specs=[pl.BlockSpec((tm, D_MODEL), lambda i: (i, 0)), _resident((D_MODEL, IN_COLS)),
                  pl.BlockSpec((3, CONV_WIDTH), lambda i: (0, 0)), ANY],
        out_specs=[pl.BlockSpec((4, tm, GROUP), lambda i: (0, i, 0)), pl.BlockSpec((3, tm, GROUP), lambda i: (0, i, 0)),
                   pl.BlockSpec((tm, D_MODEL), lambda i: (i, 0)), pl.BlockSpec((tm, CONV_WIDTH), lambda i: (i, 0))],
        out_shape=[jax.ShapeDtypeStruct((4, t, GROUP), F32), jax.ShapeDtypeStruct((3, t, GROUP), BF16),
                   jax.ShapeDtypeStruct((t, D_MODEL), BF16), jax.ShapeDtypeStruct((t, CONV_WIDTH), BF16)],
        scratch_shapes=[pltpu.VMEM((tm + SUBLANES, CONV_WIDTH), F32)],
        compiler_params=_params("arbitrary"),
    )(x, w_in, conv_w, after)


def _gates(fp, lb):
    sig = _sigmoid(fp)
    f = lb + (1.0 - lb) * sig
    return sig, f, jnp.log(f), 1.0 - f


def _chunk_masks():
    row = lax.broadcasted_iota(jnp.int32, (CHUNK, CHUNK), 0)
    col = lax.broadcasted_iota(jnp.int32, (CHUNK, CHUNK), 1)
    return row >= col, row <= col


def _hgrn_fwd(proj, lb_logits, after):
    t = proj.shape[1]
    tb = min(t, 512)
    ncb = tb // CHUNK

    def body(q_ref, f_ref, v_ref, lbl_ref, after_ref, o_ref, st_ref, s_scr):
        @pl.when(pl.program_id(0) == 0)
        def _():
            s_scr[...] = jnp.zeros_like(s_scr)

        lb, _ = _lower_bound(lbl_ref[...])
        causal, _ = _chunk_masks()

        every = range(ncb)
        rows = [slice(c * CHUNK, (c + 1) * CHUNK) for c in every]
        q, v = [q_ref[r, :] for r in rows], [v_ref[r, :] for r in rows]
        gates = [_gates(f_ref[r, :], lb) for r in rows]
        k = [gt[3] for gt in gates]
        b = [_dot_exact(causal, gt[2]) for gt in gates]
        mid, last = [x[CHUNK // 2:CHUNK // 2 + 1, :] for x in b], [x[CHUNK - 1:CHUNK, :] for x in b]
        qt = [q[c] * jnp.exp(b[c] - mid[c]) for c in every]
        kt = [k[c] * jnp.exp(mid[c] - b[c]) for c in every]
        qi = [q[c] * jnp.exp(b[c]) for c in every]
        ks = [k[c] * jnp.exp(last[c] - b[c]) for c in every]
        dec = [jnp.exp(x) for x in last]
        scores = [[jnp.where(causal, _dot(a, b_, NT), 0.0) for a, b_ in zip(_heads(qt[c]), _heads(kt[c]))] for c in every]
        intra = [[_dot(s, v_h, NN) for s, v_h in zip(scores[c], _heads(v[c]))] for c in every]
        update = [_per_head(lambda v_h, ks_h: _dot(v_h, ks_h, TN), v[c], ks[c]) for c in every]

        st = s_scr[...]
        states = []
        for c in every:
            states.append(st)
            st_ref[c] = st
            st = dec[c] * st + update[c]
        s_scr[...] = st

        o_ref[...] = jnp.concatenate(
            [jnp.concatenate([i_h + _dot(qi_h, st_h, NT) for i_h, qi_h, st_h in
                              zip(intra[c], _heads(qi[c]), _heads(states[c]))], axis=1) for c in every], axis=0)

    grp = lambda g: pl.BlockSpec((None, tb, GROUP), lambda i: (g, i, 0))
    return pl.pallas_call(
        body, name="hgrn_fwd", grid=(t // tb,),
        in_specs=[grp(0), grp(1), grp(2), pl.BlockSpec((2, HGRN_WIDTH), lambda i: (0, 0)), ANY],
        out_specs=[pl.BlockSpec((tb, HGRN_WIDTH), lambda i: (i, 0)),
                   pl.BlockSpec((ncb, HEAD_DIM, HGRN_WIDTH), lambda i: (i, 0, 0))],
        out_shape=[jax.ShapeDtypeStruct((t, HGRN_WIDTH), F32),
                   jax.ShapeDtypeStruct((t // CHUNK, HEAD_DIM, HGRN_WIDTH), F32)],
        scratch_shapes=[pltpu.VMEM((HEAD_DIM, HGRN_WIDTH), F32)],
        compiler_params=_params("arbitrary"),
    )(proj, proj, proj, lb_logits, after)


def _gate_fwd(proj, o, gate_norm_w, after):
    t = proj.shape[1]
    tb = min(t, 1024)

    def body(o_ref, og_ref, gnw_ref, after_ref, out_ref):
        gnw = gnw_ref[...]
        for s in range(tb // GATE_STRIP):
            rows = slice(s * GATE_STRIP, (s + 1) * GATE_STRIP)
            og = og_ref[rows, :]
            on = _per_head(lambda o_h: o_h * lax.rsqrt(jnp.mean(o_h * o_h, axis=-1, keepdims=True) + EPS), o_ref[rows, :])
            out_ref[rows, :] = (on * gnw * (og * _sigmoid(og))).astype(out_ref.dtype)

    tile = pl.BlockSpec((tb, GROUP), lambda i: (i, 0))
    return pl.pallas_call(
        body, name="gate_fwd", grid=(t // tb,),
        in_specs=[tile, pl.BlockSpec((None, tb, GROUP), lambda i: (3, i, 0)), pl.BlockSpec((1, GROUP), lambda i: (0, 0)), ANY],
        out_specs=tile,
        out_shape=jax.ShapeDtypeStruct((t, HGRN_WIDTH), BF16),
        compiler_params=_params("parallel"),
    )(o, proj, gate_norm_w, after)


def _ln_bwd(dy, xhat, rstd, g):
    dxhat = dy * g
    m1 = jnp.mean(dxhat, axis=-1, keepdims=True)
    m2 = jnp.mean(dxhat * xhat, axis=-1, keepdims=True)
    return rstd * (dxhat - m1 - xhat * m2)


def _layer_norm(pre):
    xc = pre - jnp.mean(pre, axis=-1, keepdims=True)
    rstd = lax.rsqrt(jnp.mean(xc * xc, axis=-1, keepdims=True) + EPS)
    return xc * rstd, rstd


def _sublayers(cat_h, cat_c, x, target, w_out, w_ff1, w_ff2, g1, b1, g2, b2):
    t = x.shape[0]
    tm = min(t, 256)

    def body(ch_ref, cc_ref, x_ref, tg_ref, wo_ref, w1_ref, w2_ref, g1_ref, b1_ref, g2_ref, b2_ref,
             h1_ref, r_ref, da_ref, dp2b_ref, dp1_ref, dcat_ref, dg1_ref, db1_ref, dg2_ref, db2_ref, loss_ref, gwo_ref,
             gwo_acc, gwo_narrow, sem):
        @pl.when(pl.program_id(0) == 0)
        def _():
            for ref in (dg1_ref, db1_ref, dg2_ref, db2_ref, loss_ref, gwo_acc):
                ref[...] = jnp.zeros_like(ref)

        mix = _dot(ch_ref[...], wo_ref[0:GROUP, :], NN) + _dot(cc_ref[...], wo_ref[GROUP:2 * GROUP, :], NN)
        xhat1, rstd1 = _layer_norm(ALPHA * x_ref[...] + mix)
        h1 = xhat1 * g1_ref[...] + b1_ref[...]
        h1b = h1.astype(h1_ref.dtype)
        h1_ref[...] = h1b
        mlp = jnp.zeros((tm, D_MODEL), F32)
        for j in range(N_FF):
            cols = slice(j * FF_BLOCK, (j + 1) * FF_BLOCK)
            r = jnp.square(jnp.maximum(_dot(h1b, w1_ref[:, cols], NN), 0.0)).astype(r_ref.dtype)
            r_ref[:, cols] = r
            mlp = mlp + _dot(r, w2_ref[cols, :], NN)
        xhat2, rstd2 = _layer_norm(ALPHA * h1 + mlp)
        err = xhat2 * g2_ref[...] + b2_ref[...] - tg_ref[...]
        loss_ref[...] += 0.5 * jnp.sum(jnp.mean(err * err, axis=-1, keepdims=True))
        dy = err * (1.0 / D_MODEL)
        dg2_ref[...] += jnp.sum(dy * xhat2, axis=0, keepdims=True)
        db2_ref[...] += jnp.sum(dy, axis=0, keepdims=True)
        dp2 = _ln_bwd(dy, xhat2, rstd2, g2_ref[...])
        dp2b = dp2.astype(dp2b_ref.dtype)
        dp2b_ref[...] = dp2b
        back = jnp.zeros((tm, D_MODEL), F32)
        for j in range(N_FF):
            cols = slice(j * FF_BLOCK, (j + 1) * FF_BLOCK)
            dr = _dot(dp2b, w2_ref[cols, :], NT)
            da = (dr * (2.0 * jnp.sqrt(r_ref[:, cols].astype(F32)))).astype(da_ref.dtype)
            da_ref[:, cols] = da
            back = back + _dot(da, w1_ref[:, cols], NT)
        dh1 = ALPHA * dp2 + back
        dg1_ref[...] += jnp.sum(dh1 * xhat1, axis=0, keepdims=True)
        db1_ref[...] += jnp.sum(dh1, axis=0, keepdims=True)
        dp1 = _ln_bwd(dh1, xhat1, rstd1, g1_ref[...])
        dp1b = dp1.astype(MXU_DTYPE)
        dp1_ref[...] = dp1
        dcat_ref[...] = _dot(dp1b, wo_ref[...], NT)
        gwo_acc[0:GROUP, :] += _dot(ch_ref[...], dp1b, TN)
        gwo_acc[GROUP:2 * GROUP, :] += _dot(cc_ref[...], dp1b, TN)

        @pl.when(pl.program_id(0) == pl.num_programs(0) - 1)
        def _():
            gwo_narrow[...] = gwo_acc[...].astype(gwo_narrow.dtype)
            copy = pltpu.make_async_copy(gwo_narrow, gwo_ref, sem.at[0])
            copy.start()
            copy.wait()

    row = pl.BlockSpec((tm, D_MODEL), lambda i: (i, 0))
    wide = pl.BlockSpec((tm, D_FF), lambda i: (i, 0))
    vec = pl.BlockSpec((1, D_MODEL), lambda i: (0, 0))
    narrow = lambda dtype: jax.ShapeDtypeStruct((t, D_MODEL), dtype)
    return pl.pallas_call(
        body, name="sublayers", grid=(t // tm,),
        in_specs=[pl.BlockSpec((tm, GROUP), lambda i: (i, 0)), pl.BlockSpec((tm, GROUP), lambda i: (i, 0)), row, row,
                  _resident((D_MODEL, D_MODEL)),
                  _resident((D_MODEL, D_FF)), _resident((D_FF, D_MODEL)), vec, vec, vec, vec],
        out_specs=[row, wide, wide, row, row, row, vec, vec, vec, vec,
                   pl.BlockSpec((SUBLANES, LANES), lambda i: (0, 0)), ANY],
        out_shape=[narrow(BF16), jax.ShapeDtypeStruct((t, D_FF), BF16), jax.ShapeDtypeStruct((t, D_FF), BF16),
                   narrow(BF16), narrow(F32), narrow(F32)]
                  + [jax.ShapeDtypeStruct((1, D_MODEL), F32)] * 4
                  + [jax.ShapeDtypeStruct((SUBLANES, LANES), F32), jax.ShapeDtypeStruct((D_MODEL, D_MODEL), BF16)],
        scratch_shapes=[pltpu.VMEM((D_MODEL, D_MODEL), F32), pltpu.VMEM((D_MODEL, D_MODEL), BF16),
                        pltpu.SemaphoreType.DMA((1,))],
        compiler_params=_params("arbitrary"),
    )(cat_h, cat_c, x, target, w_out, w_ff1, w_ff2, g1, b1, g2, b2)


def _hgrn_bwd(proj, do, states, lb_logits, after):
    t = proj.shape[1]
    tb = min(t, 512)
    ncb = tb // CHUNK
    nblk = t // tb

    def body(q_ref, f_ref, v_ref, do_ref, st_ref, lbl_ref, after_ref, dp_ref, dlbl_ref, ds_scr, dlb_scr):
        i = pl.program_id(0)

        @pl.when(i == 0)
        def _():
            ds_scr[...] = jnp.zeros_like(ds_scr)
            dlb_scr[...] = jnp.zeros_like(dlb_scr)

        lb, s1 = _lower_bound(lbl_ref[...])
        causal, anti = _chunk_masks()
        every = range(ncb)
        rows = [slice(c * CHUNK, (c + 1) * CHUNK) for c in every]
        q, v, do = ([ref[r, :] for r in rows] for ref in (q_ref, v_ref, do_ref))
        st = [st_ref[c] for c in every]
        gates = [_gates(f_ref[r, :], lb) for r in rows]
        sig, f, k = ([gt[n] for gt in gates] for n in (0, 1, 3))
        b = [_dot_exact(causal, gt[2]) for gt in gates]
        mid, last = [x[CHUNK // 2:CHUNK // 2 + 1, :] for x in b], [x[CHUNK - 1:CHUNK, :] for x in b]
        e_q = [jnp.exp(b[c] - mid[c]) for c in every]
        e_k = [jnp.exp(mid[c] - b[c]) for c in every]
        e_i = [jnp.exp(x) for x in b]
        e_s = [jnp.exp(last[c] - b[c]) for c in every]
        dec = [jnp.exp(x) for x in last]
        qt, kt, qi, ks = ([a[c] * e[c] for c in every] for a, e in ((q, e_q), (k, e_k), (q, e_i), (k, e_s)))

        def masked(a, b_):
            return [[jnp.where(causal, _dot(a_h, b_h, NT), 0.0) for a_h, b_h in zip(_heads(a[c]), _heads(b_[c]))]
                    for c in every]

        def with_scores(s, other, dims):
            return [jnp.concatenate([_dot(s_h, o_h, dims) for s_h, o_h in zip(s[c], _heads(other[c]))], axis=1)
                    for c in every]

        def per_head(dims, a, b_):
            return [_per_head(lambda a_h, b_h: _dot(a_h, b_h, dims), a[c], b_[c]) for c in every]

        scores, dscores = masked(qt, kt), masked(do, v)
        dqt, dkt, dv_intra = with_scores(dscores, kt, NN), with_scores(dscores, qt, TN), with_scores(scores, do, TN)
        dqi, update = per_head(NN, do, st), per_head(TN, do, qi)

        dst = ds_scr[...]
        dsts = [None] * ncb
        for c in reversed(every):
            dsts[c] = dst
            dst = dec[c] * dst + update[c]
        ds_scr[...] = dst

        dv_state, dks = per_head(NT, ks, dsts), per_head(NN, v, dsts)
        ddec = [jnp.sum(dsts[c] * st[c], axis=0, keepdims=True) for c in every]
        dq = [dqt[c] * e_q[c] + dqi[c] * e_i[c] for c in every]
        dk = [dkt[c] * e_k[c] + dks[c] * e_s[c] for c in every]
        db = [q[c] * dq[c] - k[c] * dk[c] for c in every]
        db_last = [jnp.sum(dks[c] * ks[c], axis=0, keepdims=True) + ddec[c] * dec[c] for c in every]
        dg = [_dot_exact(anti, db[c]) + db_last[c] for c in every]
        df = [dg[c] / f[c] - dk[c] for c in every]
        dlb_scr[...] += sum(jnp.sum(df[c] * (1.0 - sig[c]), axis=0, keepdims=True) for c in every)
        dfp = [df[c] * (1.0 - lb) * sig[c] * (1.0 - sig[c]) for c in every]
        dv = [dv_intra[c] + dv_state[c] for c in every]
        for n, parts in enumerate((dq, dfp, dv)):
            dp_ref[n] = jnp.concatenate(parts, axis=0).astype(dp_ref.dtype)

        @pl.when(i == nblk - 1)
        def _():
            dlb = dlb_scr[...]
            dlbl_ref[0:1, :] = dlb * lb * (1.0 - lb)
            dlbl_ref[1:2, :] = -dlb * lb * s1

    grp = lambda g: pl.BlockSpec((None, tb, GROUP), lambda i: (g, nblk - 1 - i, 0))
    vec = pl.BlockSpec((2, HGRN_WIDTH), lambda i: (0, 0))
    return pl.pallas_call(
        body, name="hgrn_bwd", grid=(nblk,),
        in_specs=[grp(0), grp(1), grp(2), pl.BlockSpec((tb, HGRN_WIDTH), lambda i: (nblk - 1 - i, 0)),
                  pl.BlockSpec((ncb, HEAD_DIM, HGRN_WIDTH), lambda i: (nblk - 1 - i, 0, 0)), vec, ANY],
        out_specs=[pl.BlockSpec((3, tb, HGRN_WIDTH), lambda i: (0, nblk - 1 - i, 0)), vec],
        out_shape=[jax.ShapeDtypeStruct((3, t, HGRN_WIDTH), BF16), jax.ShapeDtypeStruct((2, HGRN_WIDTH), F32)],
        scratch_shapes=[pltpu.VMEM((HEAD_DIM, HGRN_WIDTH), F32), pltpu.VMEM((1, HGRN_WIDTH), F32)],
        compiler_params=_params("arbitrary"),
    )(proj, proj, proj, do, states, lb_logits, after)


GRAD_TILE = 512
OUT_PARTS = 4


class _Side:
    def __init__(self, operands, in_specs, out_shape, out_specs, scratch, init, begin):
        self.operands, self.in_specs, self.out_shape, self.out_specs = operands, in_specs, out_shape, out_specs
        self.scratch, self.init, self.begin = scratch, init, begin


def _grad_w(name, operands, widths, shape, step, after=None, side=None):
    t = operands[0].shape[-2]
    tt = min(t, GRAD_TILE)
    n_in, n_steps = len(operands), t // tt
    in_specs = [pl.BlockSpec((tt, w), lambda k: (k, 0)) if a.ndim == 2 else
                pl.BlockSpec((a.shape[0], tt, w), lambda k: (0, k, 0)) for a, w in zip(operands, widths)]
    extra = [] if after is None else [after]
    s_in, s_out = (len(side.operands), len(side.out_shape)) if side else (0, 0)
    first_out = n_in + s_in + len(extra)

    def body(*refs):
        o_ref, side_outs = refs[first_out], refs[first_out + 1:first_out + 1 + s_out]
        acc, narrow, sem = refs[first_out + 1 + s_out:first_out + 4 + s_out]
        k = pl.program_id(0)

        @pl.when(k == 0)
        def _():
            acc[...] = jnp.zeros_like(acc)
            if side:
                side.init(side_outs)

        tick = side.begin(k, n_steps, refs[n_in:n_in + s_in], side_outs, refs[first_out + 4 + s_out:]) if side else None
        step(acc, *refs[:n_in], tick or (lambda j: None))

        @pl.when(k == n_steps - 1)
        def _():
            part = shape[0] // OUT_PARTS
            copies = []
            for p in range(OUT_PARTS):
                rows = pl.ds(p * part, part)
                narrow[rows, :] = acc[rows, :].astype(narrow.dtype)
                copies.append(pltpu.make_async_copy(narrow.at[rows, :], o_ref.at[rows, :], sem.at[p]))
                copies[-1].start()
            for cp in copies:
                cp.wait()

    outs = pl.pallas_call(
        body, name=name, grid=(n_steps,),
        in_specs=in_specs + (side.in_specs if side else []) + [ANY] * len(extra),
        out_specs=[ANY] + (side.out_specs if side else []),
        out_shape=[jax.ShapeDtypeStruct(shape, BF16)] + (side.out_shape if side else []),
        scratch_shapes=[pltpu.VMEM(shape, F32), pltpu.VMEM(shape, BF16), pltpu.SemaphoreType.DMA((OUT_PARTS,))]
                       + (side.scratch if side else []),
        compiler_params=_params("arbitrary"),
    )(*operands, *(side.operands if side else ()), *extra)
    return outs if side else outs[0]


def _dw_in(xb, dph, dog, dpc, w_in, dpre1, after):
    t = xb.shape[0]

    def step(acc, x_ref, dh_ref, dog_ref, dc_ref, tick):
        xv = x_ref[...]
        for g in range(N_GROUPS):
            part = dh_ref[g] if g < 3 else dog_ref[...] if g == 3 else dc_ref[g - 4]
            acc[:, g * GROUP:(g + 1) * GROUP] += _dot(xv, part, TN)
            tick(g, part)

    def begin(k, n_steps, ins, outs, scratch):
        w_ref, dp_ref = ins
        total = [ALPHA * dp_ref[...]]

        def tick(g, part):
            total[0] = total[0] + _dot(part, w_ref[:, g * GROUP:(g + 1) * GROUP], NT)
            if g == N_GROUPS - 1:
                outs[0][...] = total[0]

        return tick

    row = pl.BlockSpec((min(t, GRAD_TILE), D_MODEL), lambda k: (k, 0))
    side = _Side((w_in, dpre1), [_resident((D_MODEL, IN_COLS)), row], [jax.ShapeDtypeStruct((t, D_MODEL), F32)], [row],
                 [], lambda outs: None, begin)
    return _grad_w("dw_in", (xb, dph, dog, dpc), (D_MODEL, GROUP, GROUP, GROUP), (D_MODEL, IN_COLS), step, after, side)


def _strips_of(j, tt):
    per_tick = tt // GATE_STRIP // N_FF
    return [slice(s * GATE_STRIP, (s + 1) * GATE_STRIP) for s in range(j * per_tick, (j + 1) * per_tick)]


def _dw_ff1(h1b, da, dcat, o, proj, gate_norm_w, after):
    t = h1b.shape[0]
    tt = min(t, GRAD_TILE)

    def step(acc, h_ref, da_ref, tick):
        hv = h_ref[...]
        for j in range(N_FF):
            cols = slice(j * FF_BLOCK, (j + 1) * FF_BLOCK)
            acc[:, cols] += _dot(hv, da_ref[:, cols], TN)
            tick(j)

    def init(outs):
        outs[2][...] = jnp.zeros_like(outs[2])

    def begin(k, n_steps, ins, outs, scratch):
        do2_ref, o_ref, og_ref, gnw_ref = ins
        do_ref, dog_ref, dgnw_ref = outs
        total = [jnp.zeros((GATE_STRIP, GROUP), F32)]

        def tick(j):
            gnw = gnw_ref[...]
            for rows in _strips_of(j, tt):
                ov, og, do2 = o_ref[rows, :], og_ref[rows, :], do2_ref[rows, :]
                rs = _per_head(lambda o_h: jnp.broadcast_to(
                    lax.rsqrt(jnp.mean(o_h * o_h, axis=-1, keepdims=True) + EPS), o_h.shape), ov)
                on = ov * rs
                sg = _sigmoid(og)
                sil = og * sg
                don = do2 * gnw * sil
                total[0] = total[0] + do2 * on * sil
                dog_ref[rows, :] = (do2 * on * gnw * (sg * (1.0 + og * (1.0 - sg)))).astype(dog_ref.dtype)
                do_ref[rows, :] = rs * (don - on * _per_head(
                    lambda p_h: jnp.broadcast_to(jnp.mean(p_h, axis=-1, keepdims=True), p_h.shape), don * on))
            if j == N_FF - 1:
                dgnw_ref[...] += jnp.sum(total[0], axis=0, keepdims=True)

        return tick

    tile = pl.BlockSpec((tt, GROUP), lambda k: (k, 0))
    vec = pl.BlockSpec((1, GROUP), lambda k: (0, 0))
    side = _Side(
        (dcat, o, proj, gate_norm_w), [tile, tile, pl.BlockSpec((None, tt, GROUP), lambda k: (3, k, 0)), vec],
        [jax.ShapeDtypeStruct((t, HGRN_WIDTH), F32), jax.ShapeDtypeStruct((t, HGRN_WIDTH), BF16),
         jax.ShapeDtypeStruct((1, HGRN_WIDTH), F32)], [tile, tile, vec], [], init, begin)
    return _grad_w("dw_ff1", (h1b, da), (D_MODEL, D_FF), (D_MODEL, D_FF), step, after, side)


def _dw_ff2(r, dpre2b, dcat, bcu, conv_w):
    t = r.shape[0]
    tt = min(t, GRAD_TILE)
    hb = tt // SUBLANES
    halo = 2 * SUBLANES

    def step(acc, r_ref, d_ref, tick):
        dv = d_ref[...]
        for j in range(N_FF):
            rows = slice(j * FF_BLOCK, (j + 1) * FF_BLOCK)
            acc[rows, :] += _dot(r_ref[:, rows], dv, TN)
            tick(j)

    def init(outs):
        outs[1][...] = jnp.zeros_like(outs[1])

    def begin(k, n_steps, ins, outs, scratch):
        dy_ref, dyn_ref, b_ref, bn_ref, c_ref, u_ref, ch_ref, uh_ref, cw_ref = ins
        dp_ref, dcw_ref = outs
        zbuf, dbuf = scratch
        before = lambda ref: ref[SUBLANES:halo, :].astype(F32)
        zbuf[0:SUBLANES, :] = jnp.where(k > 0, before(ch_ref) * before(uh_ref), 0.0)
        zbuf[SUBLANES:SUBLANES + tt, :] = c_ref[...].astype(F32) * u_ref[...].astype(F32)
        dbuf[0:tt, :] = dy_ref[...] * b_ref[...].astype(F32)
        dbuf[tt:tt + SUBLANES, :] = jnp.where(k < n_steps - 1, dyn_ref[...] * bn_ref[0:SUBLANES, :].astype(F32), 0.0)
        totals = [jnp.zeros((GATE_STRIP, GROUP), F32) for _ in range(3)]

        def tick(j):
            cw = cw_ref[...]
            for rows in _strips_of(j, tt):
                at = lambda buf, shift: buf[shift + rows.start:shift + rows.stop, :]
                z, z1, z2 = at(zbuf, SUBLANES), at(zbuf, SUBLANES - 1), at(zbuf, SUBLANES - 2)
                dyc, d1, d2 = at(dbuf, 0), at(dbuf, 1), at(dbuf, 2)
                yc = cw[2:3, :] * z + cw[1:2, :] * z1 + cw[0:1, :] * z2
                dz = cw[2:3, :] * dyc + cw[1:2, :] * d1 + cw[0:1, :] * d2
                dp_ref[0, rows, :] = (dy_ref[rows, :] * yc).astype(dp_ref.dtype)
                dp_ref[1, rows, :] = (dz * u_ref[rows, :].astype(F32)).astype(dp_ref.dtype)
                dp_ref[2, rows, :] = (dz * c_ref[rows, :].astype(F32)).astype(dp_ref.dtype)
                for n, tap in enumerate((z2, z1, z)):
                    totals[n] = totals[n] + dyc * tap
            if j == N_FF - 1:
                for n in range(3):
                    dcw_ref[n:n + 1, :] += jnp.sum(totals[n], axis=0, keepdims=True)

        return tick

    grp = lambda g: pl.BlockSpec((None, tt, GROUP), lambda k: (g, k, 0))
    prev = lambda g: pl.BlockSpec((None, halo, GROUP), lambda k: (g, jnp.maximum(k * (tt // halo) - 1, 0), 0))
    nxt = lambda g: pl.BlockSpec((None, halo, GROUP), lambda k: (g, jnp.minimum((k + 1) * (tt // halo), t // halo - 1), 0))
    nxt_row = lambda k: jnp.minimum((k + 1) * hb, t // SUBLANES - 1)
    whole = pl.BlockSpec((3, CONV_WIDTH), lambda k: (0, 0))
    side = _Side(
        (dcat, dcat, bcu, bcu, bcu, bcu, bcu, bcu, conv_w),
        [pl.BlockSpec((tt, GROUP), lambda k: (k, 1)), pl.BlockSpec((SUBLANES, GROUP), lambda k: (nxt_row(k), 1)),
         grp(0), nxt(0), grp(1), grp(2), prev(1), prev(2), whole],
        [jax.ShapeDtypeStruct((3, t, CONV_WIDTH), BF16), jax.ShapeDtypeStruct((3, CONV_WIDTH), F32)],
        [pl.BlockSpec((3, tt, GROUP), lambda k: (0, k, 0)), whole],
        [pltpu.VMEM((tt + SUBLANES, GROUP), F32), pltpu.VMEM((tt + SUBLANES, GROUP), F32)], init, begin)
    return _grad_w("dw_ff2", (r, dpre2b), (D_FF, D_MODEL), (D_FF, D_MODEL), step, side=side)


def _place():
    x, y, c = lax.axis_index("x"), lax.axis_index("y"), lax.axis_index("c")
    return x, y, c, 2 * x + y


def _other_chips(x, y):
    return [(1 - x, y), (x, 1 - y), (1 - x, 1 - y)]


def _place_shard(name, w, chip, cols_sharded, after=None):
    rows, cols = w.shape
    tr = min(rows, 256)
    nb = rows // tr
    full = (rows, cols * N_CHIPS) if cols_sharded else (rows * N_CHIPS, cols)
    out_map = (lambda i, s: (i, s[0])) if cols_sharded else (lambda i, s: (s[0] * nb + i, 0))

    def body(s_ref, w_ref, *rest):
        rest[-1][...] = w_ref[...].astype(rest[-1].dtype)

    extra = [] if after is None else [after]
    return pl.pallas_call(
        body, name=name,
        grid_spec=pltpu.PrefetchScalarGridSpec(
            num_scalar_prefetch=1, grid=(nb,),
            in_specs=[pl.BlockSpec((tr, cols), lambda i, s: (i, 0))] + [ANY] * len(extra),
            out_specs=pl.BlockSpec((tr, cols), out_map)),
        out_shape=jax.ShapeDtypeStruct(full, BF16),
        compiler_params=_params("parallel"),
    )(chip, w, *extra)


HBM = pl.BlockSpec(memory_space=pltpu.HBM)
SEM = pl.BlockSpec(memory_space=pltpu.SEMAPHORE)
EFFECT = pltpu.SideEffectType.DATAFLOW_SIDE_EFFECTING


PEER_SETS = {
    "sibling": (0, lambda x, y, c: [(x, y, 1 - c)]),
    "chips": (1, lambda x, y, c: [(1 - x, y, c), (x, 1 - y, c), (1 - x, 1 - y, c)]),
    "neighbours": (2, lambda x, y, c: [(1 - x, y, c), (x, 1 - y, c)]),
}


class _Split:
    def __init__(self, name, arrays, plan, others=(), peers=None):
        n_own, arrays = len(arrays), (*arrays, *others)
        n, n_copies = len(arrays), plan.count
        self.name, self.plan, self.n = name, plan, n_own
        barrier_id, peer_ids = PEER_SETS[peers] if peers else (None, None)

        def body(*refs):
            if peers:
                x, y, c, _ = _place()
                barrier = pltpu.get_barrier_semaphore()
                for peer in peer_ids(x, y, c):
                    pl.semaphore_signal(barrier, inc=1, device_id=peer, device_id_type=MESH)
                pl.semaphore_wait(barrier, len(peer_ids(0, 0, 0)))
            send_sems, recv_sems, token = refs[n], refs[n + 1], refs[-1]
            for k, (src, dst, to) in enumerate(plan(refs[:n])):
                pltpu.make_async_remote_copy(src_ref=src, dst_ref=dst, send_sem=send_sems.at[k], recv_sem=recv_sems.at[k],
                                             device_id=to, device_id_type=MESH).start()
            token[...] = jnp.zeros_like(token)

        outs = pl.pallas_call(
            body, name=name + "_start",
            out_shape=(pltpu.SemaphoreType.DMA((n_copies,)), pltpu.SemaphoreType.DMA((n_copies,)),
                       *[pltpu.HBM(a.shape, a.dtype) for a in arrays], jax.ShapeDtypeStruct((SUBLANES, LANES), F32)),
            in_specs=(HBM,) * n, out_specs=(SEM, SEM) + (HBM,) * n + (pl.BlockSpec(memory_space=pltpu.VMEM),),
            input_output_aliases={i: 2 + i for i in range(n)},
            compiler_params=pltpu.CompilerParams(has_side_effects=EFFECT, collective_id=barrier_id),
        )(*[pltpu.with_memory_space_constraint(a, pltpu.HBM) for a in arrays])
        self.sems, self.arrays, self.others, self.token = outs[:2], outs[2:2 + n_own], outs[2 + n_own:2 + n], outs[-1]

    def wait(self, after):
        n, plan = self.n, self.plan

        def body(*refs):
            send_sems, recv_sems = refs[n], refs[n + 1]
            for k, (src, dst, to) in enumerate(plan(refs[:n])):
                cp = pltpu.make_async_remote_copy(src_ref=src, dst_ref=dst, send_sem=send_sems.at[k],
                                                  recv_sem=recv_sems.at[k], device_id=to, device_id_type=MESH)
                cp.wait_send()
                cp.wait_recv()

        return pl.pallas_call(
            body, name=self.name + "_wait", out_shape=tuple(pltpu.HBM(a.shape, a.dtype) for a in self.arrays),
            in_specs=(HBM,) * n + (SEM, SEM, ANY), out_specs=(HBM,) * n, input_output_aliases={i: i for i in range(n)},
            compiler_params=pltpu.CompilerParams(has_side_effects=EFFECT),
        )(*self.arrays, *self.sems, after)


COLS_SHARDED = (True, False, True, False)
HALF_SHAPES = [(D_MODEL // 2, IN_COLS), (D_MODEL, D_MODEL // 2), (D_MODEL // 2, D_FF), (D_FF, D_MODEL // 2)]
PIECE_SHAPES = [(D_MODEL // 2, IN_COLS // N_CHIPS), (D_MODEL // N_CHIPS, D_MODEL // 2),
                (D_MODEL // 2, D_FF // N_CHIPS), (D_FF // N_CHIPS, D_MODEL // 2)]


def _shard_view(kind, ref, chip):
    if COLS_SHARDED[kind]:
        n = ref.shape[1] // N_CHIPS
        return ref.at[:, pl.ds(chip * n, n)]
    n = ref.shape[0] // N_CHIPS
    return ref.at[pl.ds(chip * n, n), :]


def _half_view(kind, ref, h):
    if COLS_SHARDED[kind]:
        n = ref.shape[0] // 2
        return ref.at[pl.ds(h * n, n), :]
    n = ref.shape[1] // 2
    return ref.at[:, pl.ds(h * n, n)]


def _plan(count):
    def mark(fn):
        fn.count = count
        return fn
    return mark


def _shard_rows_view(kind, ref, chip, part, n_parts):
    if COLS_SHARDED[kind]:
        m, n = ref.shape[0] // n_parts, ref.shape[1] // N_CHIPS
        return ref.at[pl.ds(part * m, m), pl.ds(chip * n, n)]
    m = ref.shape[0] // N_CHIPS // n_parts
    return ref.at[pl.ds((n_parts * chip + part) * m, m), :]


def _shard_half_view(kind, ref, chip, h):
    return _shard_rows_view(kind, ref, chip, h, 2)


def _gather_over_ici(kinds, weights):
    @_plan(2 * len(kinds))
    def plan(refs):
        x, y, c, me = _place()
        mine = [_shard_half_view(kind, ref, me, c) for kind, ref in zip(kinds, refs)]
        return [(v, v, to) for v in mine for to in ((1 - x, y, c), (x, 1 - y, c))]

    return _Split("gather_ici_" + "".join(map(str, kinds)), tuple(weights), plan, peers="neighbours")


def _relay_over_ici(kinds, weights, others=()):
    @_plan(2 * len(kinds))
    def plan(refs):
        x, y, c, _ = _place()
        x_nbr, y_nbr = 2 * (1 - x) + y, 2 * x + (1 - y)
        out = []
        for kind, ref in zip(kinds, refs):
            first, second = (_shard_rows_view(kind, ref, chip, 2 * c + q, 4) for q, chip in ((0, x_nbr), (1, y_nbr)))
            out += [(first, first, (x, 1 - y, c)), (second, second, (1 - x, y, c))]
        return out

    return _Split("relay_ici_" + "".join(map(str, kinds)), tuple(weights), plan, others, peers="neighbours")


def _gather_w_in_over_ici(w_in, conv4):
    @_plan(6)
    def plan(refs):
        x, y, c, me = _place()
        half, conv = _shard_half_view(0, refs[0], me, c), refs[1].at[me]
        return [(v, v, (px, py, c)) for v in (half, conv) for px, py in _other_chips(x, y)]

    return _Split("gather_w_in_ici", (w_in, conv4), plan, peers="chips")


def _gather_over_d2d(kinds, weights):
    @_plan(3 * len(kinds))
    def plan(refs):
        x, y, c, _ = _place()
        got = [_shard_half_view(kind, ref, 2 * px + py, c) for kind, ref in zip(kinds, refs)
               for px, py in _other_chips(x, y)]
        return [(v, v, (x, y, 1 - c)) for v in got]

    return _Split("gather_d2d_" + "".join(map(str, kinds)), tuple(weights), plan, peers="sibling")


def _swap_halves(kinds, grads):
    @_plan(len(kinds))
    def plan(refs):
        x, y, c, _ = _place()
        return [(_half_view(kind, g, 1 - c), land, (x, y, 1 - c))
                for kind, g, land in zip(kinds, refs[:len(kinds)], refs[len(kinds):])]

    lands = [lax.empty(HALF_SHAPES[kind], g.dtype) for kind, g in zip(kinds, grads)]
    return _Split("swap_halves_" + "".join(map(str, kinds)), (*grads, *lands), plan, peers="sibling")


def _block_rows(cols, elements):
    return 1 << ((elements // cols).bit_length() - 1)


def _add_half(name, g, recv, core, rows_split):
    shape = recv.shape
    tr = min(shape[0], _block_rows(shape[1], 1 << 19))
    nb = shape[0] // tr

    def body(c_ref, g_ref, r_ref, o_ref):
        o_ref[...] = (g_ref[...].astype(F32) + r_ref[...].astype(F32)).astype(o_ref.dtype)

    g_map = (lambda i, c_ref: (c_ref[0] * nb + i, 0)) if rows_split else (lambda i, c_ref: (i, c_ref[0]))
    blk = pl.BlockSpec((tr, shape[1]), lambda i, c_ref: (i, 0))
    return pl.pallas_call(
        body, name=name,
        grid_spec=pltpu.PrefetchScalarGridSpec(
            num_scalar_prefetch=1, grid=(nb,),
            in_specs=[pl.BlockSpec((tr, shape[1]), g_map), blk], out_specs=blk),
        out_shape=jax.ShapeDtypeStruct(shape, BF16),
        compiler_params=_params("parallel"),
    )(core, g, recv)


def _exchange_pieces(kinds, halves, pack=None):
    n_p, n = N_CHIPS - 1, len(kinds)

    @_plan(n_p * n + (0 if pack is None else N_DEV - 1))
    def plan(refs):
        x, y, c, _ = _place()
        copies = []
        if pack is not None:
            me = 4 * x + 2 * y + c
            peers = [((1 - x) if m & 4 else x, (1 - y) if m & 2 else y, (1 - c) if m & 1 else c) for m in range(1, N_DEV)]
            copies += [(refs[2 * n], refs[2 * n + 1].at[me], peer) for peer in peers]
        return copies + [(_shard_view(kind, half, 2 * px + py), land.at[j], (px, py, c))
                         for j, (px, py) in enumerate(_other_chips(x, y))
                         for kind, half, land in zip(kinds, refs[:n], refs[n:2 * n])]

    lands = [lax.empty((n_p,) + PIECE_SHAPES[kind], BF16) for kind in kinds]
    small = () if pack is None else (pack, lax.empty((N_DEV,) + pack.shape, F32))
    return _Split("exchange_pieces_" + "".join(map(str, kinds)), (*halves, *lands, *small), plan,
                  peers="chips" if pack is None else None)


def _sum_pieces(name, half, slots, place, rows_split, after):
    n_p, rows, cols = slots.shape
    tr = min(rows, _block_rows(cols, 1 << 17))
    nb = rows // tr
    if rows_split:
        own_map = lambda i, s: (i, s[0])
        out_map = lambda i, s: (s[1] * nb + i, 0)
        shard = (2 * rows, cols)
    else:
        own_map = lambda i, s: (s[0] * nb + i, 0)
        out_map = lambda i, s: (i, s[1])
        shard = (rows, 2 * cols)

    def body(s_ref, own_ref, slot_ref, after_ref, o_ref):
        total = own_ref[...].astype(F32)
        for j in range(n_p):
            total = total + slot_ref[j].astype(F32)
        o_ref[...] = total

    return pl.pallas_call(
        body, name=name,
        grid_spec=pltpu.PrefetchScalarGridSpec(
            num_scalar_prefetch=1, grid=(nb,),
            in_specs=[pl.BlockSpec((tr, cols), own_map), pl.BlockSpec((n_p, tr, cols), lambda i, s: (0, i, 0)), ANY],
            out_specs=pl.BlockSpec((tr, cols), out_map)),
        out_shape=jax.ShapeDtypeStruct(shard, F32),
        compiler_params=_params("parallel"),
    )(place, half, slots, after)


def _join_halves(kinds, shards):
    @_plan(len(kinds))
    def plan(refs):
        x, y, c, _ = _place()
        return [(_half_view(kind, g, c), _half_view(kind, g, c), (x, y, 1 - c)) for kind, g in zip(kinds, refs)]

    return _Split("join_halves_" + "".join(map(str, kinds)), tuple(shards), plan, peers="sibling")


N_DEV = 8


def _sum_shared(pack, land, device):
    def body(d_ref, p_ref, l_ref, o_ref):
        me = d_ref[0]
        total = jnp.where(me == 0, p_ref[...], l_ref[0])
        for d in range(1, N_DEV):
            total = total + jnp.where(me == d, p_ref[...], l_ref[d])
        o_ref[...] = total

    return pl.pallas_call(
        body, name="sum_shared",
        grid_spec=pltpu.PrefetchScalarGridSpec(
            num_scalar_prefetch=1, grid=(1,),
            in_specs=[pl.BlockSpec(pack.shape, lambda i, d: (0, 0)), pl.BlockSpec(land.shape, lambda i, d: (0, 0, 0))],
            out_specs=pl.BlockSpec(pack.shape, lambda i, d: (0, 0))),
        out_shape=jax.ShapeDtypeStruct(pack.shape, F32),
    )(device, pack, land)


def _adamw(name, w, g, m, v, after=None):
    rows, cols = w.shape
    tr = min(rows, 256)
    extra = [] if after is None else [after]

    def body(w_ref, g_ref, m_ref, v_ref, *rest):
        go_ref, d_ref, nm_ref, nv_ref = rest[-4:]
        g = g_ref[...]
        go_ref[...] = g
        d_ref[...], nm_ref[...], nv_ref[...] = _adam_step(w_ref[...], g, m_ref[...], v_ref[...])

    blk = pl.BlockSpec((tr, cols), lambda i: (i, 0))
    return pl.pallas_call(
        body, name=name, grid=(rows // tr,), in_specs=[blk] * 4 + [ANY] * len(extra), out_specs=[blk] * 4,
        out_shape=[jax.ShapeDtypeStruct(w.shape, F32)] * 4,
        compiler_params=_params("parallel"),
    )(w, g, m, v, *extra)


def _adam_step(w, g, m, v):
    nm = ADAM_B1 * m + (1.0 - ADAM_B1) * g
    nv = ADAM_B2 * v + (1.0 - ADAM_B2) * jnp.square(g)
    m_hat = nm * (1.0 / (1.0 - ADAM_B1 ** ADAM_STEP))
    v_hat = nv * (1.0 / (1.0 - ADAM_B2 ** ADAM_STEP))
    return -ADAM_LR * (m_hat / (jnp.sqrt(v_hat) + ADAM_EPS) + ADAM_WD * w), nm, nv


def _adamw_small(tot, chip, weights, ms, vs, after):
    n, half = len(weights), D_MODEL // 2

    def body(chip_ref, tot_ref, *refs):
        ins, outs = refs[:3 * n], refs[3 * n + 1:]
        tot = tot_ref[...]
        conv_all = jnp.concatenate([tot[5:6, half:], tot[6:7, :half], tot[6:7, half:]], axis=0)
        conv = sum(jnp.where(chip_ref[0] == s, conv_all[:, s * LANES:(s + 1) * LANES], 0.0) for s in range(N_CHIPS))
        grads = [jnp.concatenate([tot[4:5, :half], tot[4:5, half:]], axis=0), tot[5:6, :half], conv,
                 tot[0:1], tot[1:2], tot[2:3], tot[3:4]]
        for k, g in enumerate(grads):
            delta, nm, nv = _adam_step(ins[k][...], g, ins[n + k][...], ins[2 * n + k][...])
            outs[k][...], outs[n + k][...], outs[2 * n + k][...], outs[3 * n + k][...] = g, delta, nm, nv
        outs[4 * n][...] = tot[7:8, 0:1]

    whole = lambda a: pl.BlockSpec(a.shape, lambda i, s: (0,) * a.ndim)
    arrays = (*weights, *ms, *vs)
    loss = jax.ShapeDtypeStruct((1, 1), F32)
    return pl.pallas_call(
        body, name="adamw_small",
        grid_spec=pltpu.PrefetchScalarGridSpec(
            num_scalar_prefetch=1, grid=(1,), in_specs=[whole(tot)] + [whole(a) for a in arrays] + [ANY],
            out_specs=[whole(a) for a in weights] * 4 + [whole(loss)]),
        out_shape=[jax.ShapeDtypeStruct(a.shape, F32) for a in weights] * 4 + [loss],
    )(chip, tot, *arrays, after)


def kernel(x, w_in, lb_logits, gate_norm_w, conv_w, w_out, ln1_g, ln1_b, w_ff1, w_ff2, ln2_g, ln2_b, loss_target, m_w_in, m_lb_logits, m_gate_norm_w, m_conv_w, m_w_out, m_ln1_g, m_ln1_b, m_w_ff1, m_w_ff2, m_ln2_g, m_ln2_b, v_w_in, v_lb_logits, v_gate_norm_w, v_conv_w, v_w_out, v_ln1_g, v_ln1_b, v_w_ff1, v_w_ff2, v_ln2_g, v_ln2_b):
    xs, tgt = x[0], loss_target[0]
    chip = 2 * lax.axis_index("x") + lax.axis_index("y")
    core = lax.axis_index("c").astype(jnp.int32).reshape(1)
    chip1 = chip.astype(jnp.int32).reshape(1)
    place = jnp.concatenate([chip1, core])

    conv4 = lax.dynamic_update_slice(jnp.zeros((N_CHIPS,) + conv_w.shape[1:], F32), conv_w, (chip, 0, 0))
    ici_in = _gather_w_in_over_ici(_place_shard("place_w_in", w_in[0], chip1, True), conv4)
    rest = (1, 2, 3)
    ici_rest = _gather_over_ici(rest, (_place_shard("place_w_out", w_out[0], chip1, False, after=ici_in.token),
                                       _place_shard("place_w_ff1", w_ff1[0], chip1, True, after=ici_in.token),
                                       _place_shard("place_w_ff2", w_ff2[0], chip1, False, after=ici_in.token)))
    wb_in, cv4 = ici_in.wait(ici_rest.token)
    d2d_in = _gather_over_d2d((0,), (wb_in,))
    wb_in, = d2d_in.wait(d2d_in.token)
    conv_full = cv4.transpose(1, 0, 2).reshape(3, CONV_WIDTH)

    proj, bcu, xb, cat_c = _in_proj(xs, wb_in, conv_full, ici_rest.token)
    relay_rest = _relay_over_ici(rest, ici_rest.wait(proj))
    o, states = _hgrn_fwd(proj, lb_logits, relay_rest.token)
    d2d_rest = _gather_over_d2d(rest, relay_rest.wait(o))
    cat_h = _gate_fwd(proj, o, gate_norm_w, d2d_rest.token)
    wb_out, wb_ff1, wb_ff2 = d2d_rest.wait(cat_h)

    (h1b, r, da, dpre2b, dpre1, dcat, g_ln1_g, g_ln1_b, g_ln2_g, g_ln2_b, loss8, g_out_local) = _sublayers(
        cat_h, cat_c, xs, tgt, wb_out, wb_ff1, wb_ff2, ln1_g, ln1_b, ln2_g, ln2_b)

    names = ("w_in", "w_out", "w_ff1", "w_ff2")

    def add_halves(kinds, grads, lands):
        return [_add_half("add_half_" + names[k], g, ld, core, COLS_SHARDED[k]) for k, g, ld in zip(kinds, grads, lands)]

    def sum_pieces(kinds, halves, lands, after):
        return [_sum_pieces("sum_pieces_" + names[k], h, ld, place, COLS_SHARDED[k], after)
                for k, h, ld in zip(kinds, halves, lands)]

    early = (1, 2, 3)
    g_ff2_local, dpc, g_conv = _dw_ff2(r, dpre2b, dcat, bcu, conv_full)
    swap_a = _swap_halves((1, 3), (g_out_local, g_ff2_local))
    g_ff1_local, do, dog, g_gnw = _dw_ff1(h1b, da, dcat, o, proj, gate_norm_w, swap_a.token)
    swap_b = _swap_halves((2,), (g_ff1_local,))
    swapped_a = swap_a.wait(swap_b.token)
    halves_a = add_halves((1, 3), swapped_a[:2], swapped_a[2:])
    swapped_b = swap_b.wait(halves_a[1])
    halves = (halves_a[0], *add_halves((2,), swapped_b[:1], swapped_b[1:]), halves_a[1])
    exch = _exchange_pieces(early, halves)
    dph, g_lbl = _hgrn_bwd(proj, do, states, lb_logits, exch.token)
    g_in_local, grad_x = _dw_in(xb, dph, dog, dpc, wb_in, dpre1, dph)

    late = (0,)
    swap = _swap_halves(late, (g_in_local,))
    exchanged = exch.wait(swap.token)
    pack = jnp.concatenate([
        g_ln1_g, g_ln1_b, g_ln2_g, g_ln2_b,
        jnp.concatenate([g_lbl[0:1], g_lbl[1:2]], axis=1),
        jnp.concatenate([g_gnw, g_conv[0:1]], axis=1),
        jnp.concatenate([g_conv[1:2], g_conv[2:3]], axis=1),
        jnp.concatenate([loss8[0:1], jnp.zeros((1, D_MODEL - LANES), F32)], axis=1)], axis=0)
    join_a = _join_halves((2,), sum_pieces((2,), exchanged[1:2], exchanged[4:5], swap.token))
    swapped = swap.wait(join_a.token)
    exch = _exchange_pieces(late, add_halves(late, swapped[:1], swapped[1:]), pack)
    join_b = _join_halves((1, 3), sum_pieces((1, 3), exchanged[0:3:2], exchanged[3:6:2], exch.token))
    g_w_ff1, = join_a.wait(join_b.token)
    g_w_ff1, d_ff1, nm_ff1, nv_ff1 = _adamw("adamw_w_ff1", w_ff1[0], g_w_ff1, m_w_ff1[0], v_w_ff1[0])
    g_w_out, g_w_ff2 = join_b.wait(d_ff1)
    g_w_ff2, d_ff2, nm_ff2, nv_ff2 = _adamw("adamw_w_ff2", w_ff2[0], g_w_ff2, m_w_ff2[0], v_w_ff2[0])
    g_w_out, d_out, nm_out, nv_out = _adamw("adamw_w_out", w_out[0], g_w_out, m_w_out[0], v_w_out[0], d_ff2)
    exchanged = exch.wait(d_out)
    tot = _sum_shared(exchanged[2], exchanged[3], 2 * chip1 + core)
    join = _join_halves(late, sum_pieces(late, exchanged[:1], exchanged[1:2], tot))
    small = ("lb_logits", "gate_norm_w", "conv_w", "ln1_g", "ln1_b", "ln2_g", "ln2_b")
    small_out = _adamw_small(
        tot, chip1, (lb_logits, gate_norm_w, conv_w[0], ln1_g, ln1_b, ln2_g, ln2_b),
        (m_lb_logits, m_gate_norm_w, m_conv_w[0], m_ln1_g, m_ln1_b, m_ln2_g, m_ln2_b),
        (v_lb_logits, v_gate_norm_w, v_conv_w[0], v_ln1_g, v_ln1_b, v_ln2_g, v_ln2_b), join.token)
    g_w_in, = join.wait(small_out[0])
    g_w_in, d_in, nm_in, nv_in = _adamw("adamw_w_in", w_in[0], g_w_in, m_w_in[0], v_w_in[0])
    loss = small_out[4 * len(small)][0, 0]

    def results(n_kind, large):
        out = dict(zip(small, small_out[n_kind * len(small):(n_kind + 1) * len(small)]))
        out["conv_w"] = out["conv_w"][None]
        out.update({name: a[None] for name, a in zip(("w_in", "w_out", "w_ff1", "w_ff2"), large)})
        return [out[name] for name in ("w_in", "lb_logits", "gate_norm_w", "conv_w", "w_out", "ln1_g", "ln1_b",
                                       "w_ff1", "w_ff2", "ln2_g", "ln2_b")]

    return (loss, grad_x[None], *results(0, (g_w_in, g_w_out, g_w_ff1, g_w_ff2)),
            *results(1, (d_in, d_out, d_ff1, d_ff2)), *results(2, (nm_in, nm_out, nm_ff1, nm_ff2)),
            *results(3, (nv_in, nv_out, nv_ff1, nv_ff2)))
```

```python
import jax
import jax.numpy as jnp
from jax import lax
from jax.experimental import pallas as pl
from jax.experimental.pallas import tpu as pltpu

F32 = jnp.float32
BF16 = jnp.bfloat16
MXU_DTYPE = jnp.bfloat16

D_MODEL = 1024
HGRN_WIDTH = 512
HEAD_DIM = 128
N_HEADS = 4
CONV_WIDTH = 512
CHUNK = 64
D_FF = 4096
IN_COLS = 3584
GROUP = 512
N_GROUPS = IN_COLS // GROUP
ALPHA = 2.0 ** 0.25
EPS = 1e-5
N_CHIPS = 4
ADAM_LR, ADAM_B1, ADAM_B2, ADAM_EPS, ADAM_WD, ADAM_STEP = 0.001, 0.9, 0.999, 1e-08, 0.01, 10

LANES = 128
SUBLANES = 8
VMEM_LIMIT = 56 * 1024 * 1024
FF_BLOCK = 1024
N_FF = D_FF // FF_BLOCK
GATE_STRIP = 64

NN = (((1,), (0,)), ((), ()))
NT = (((1,), (1,)), ((), ()))
TN = (((0,), (0,)), ((), ()))
MESH = pl.DeviceIdType.MESH
ANY = pl.BlockSpec(memory_space=pl.ANY)


def _dot(a, b, dims):
    return lax.dot_general(a.astype(MXU_DTYPE), b.astype(MXU_DTYPE), dims, preferred_element_type=F32)


def _dot_exact(ones, v):
    ones = ones.astype(jnp.bfloat16)
    hi = v.astype(jnp.bfloat16)
    rest = v - hi.astype(F32)
    mid = rest.astype(jnp.bfloat16)
    low = (rest - mid.astype(F32)).astype(jnp.bfloat16)
    return sum(lax.dot_general(ones, part, NN, preferred_element_type=F32) for part in (hi, mid, low))


def _params(*sem):
    return pltpu.CompilerParams(dimension_semantics=sem, vmem_limit_bytes=VMEM_LIMIT)


def _resident(shape):
    return pl.BlockSpec(shape, lambda *_: (0,) * len(shape), pipeline_mode=pl.Buffered(1))


def _sigmoid(v):
    return 1.0 / (1.0 + jnp.exp(-v))


def _lower_bound(lbl):
    m = jnp.max(lbl, axis=0, keepdims=True)
    e = jnp.exp(lbl - m)
    s = e / jnp.sum(e, axis=0, keepdims=True)
    return s[0:1, :], s[1:2, :]


def _heads(v):
    return [v[:, h * HEAD_DIM:(h + 1) * HEAD_DIM] for h in range(N_HEADS)]


def _per_head(fn, *arrays):
    return jnp.concatenate([fn(*parts) for parts in zip(*map(_heads, arrays))], axis=1)


def _in_proj(x, w_in, conv_w, after):
    t = x.shape[0]
    tm = min(t, 512)

    def body(x_ref, w_ref, cw_ref, after_ref, o_ref, ibcu_ref, xb_ref, y_ref, zbuf):
        @pl.when(pl.program_id(0) == 0)
        def _():
            zbuf[tm:tm + SUBLANES, :] = jnp.zeros((SUBLANES, CONV_WIDTH), F32)

        xb = x_ref[...].astype(xb_ref.dtype)
        xb_ref[...] = xb
        group = lambda g: _dot(xb, w_ref[:, g * GROUP:(g + 1) * GROUP], NN)
        for n, g in enumerate((0, 1, 3)):
            o_ref[n] = group(g)
        i_gate, b_gate, c_gate, u = group(2), group(4), group(5), group(6)
        for n, part in enumerate((i_gate, b_gate, c_gate, u)):
            ibcu_ref[n] = part.astype(ibcu_ref.dtype)
        zbuf[0:SUBLANES, :] = zbuf[tm:tm + SUBLANES, :]
        zbuf[SUBLANES:SUBLANES + tm, :] = c_gate * u
        cw = cw_ref[...]
        at = lambda shift: zbuf[shift:shift + tm, :]
        conv = cw[2:3, :] * at(SUBLANES) + cw[1:2, :] * at(SUBLANES - 1) + cw[0:1, :] * at(SUBLANES - 2)
        y_ref[...] = (b_gate * conv).astype(y_ref.dtype)

    return pl.pallas_call(
        body, name="in_proj", grid=(t // tm,),
        in_specs=[pl.BlockSpec((tm, D_MODEL), lambda i: (i, 0)), _resident((D_MODEL, IN_COLS)),
                  pl.BlockSpec((3, CONV_WIDTH), lambda i: (0, 0)), ANY],
        out_specs=[pl.BlockSpec((3, tm, GROUP), lambda i: (0, i, 0)), pl.BlockSpec((4, tm, GROUP), lambda i: (0, i, 0)),
                   pl.BlockSpec((tm, D_MODEL), lambda i: (i, 0)), pl.BlockSpec((tm, CONV_WIDTH), lambda i: (i, 0))],
        out_shape=[jax.ShapeDtypeStruct((3, t, GROUP), F32), jax.ShapeDtypeStruct((4, t, GROUP), BF16),
                   jax.ShapeDtypeStruct((t, D_MODEL), BF16), jax.ShapeDtypeStruct((t, CONV_WIDTH), BF16)],
        scratch_shapes=[pltpu.VMEM((tm + SUBLANES, CONV_WIDTH), F32)],
        compiler_params=_params("arbitrary"),
    )(x, w_in, conv_w, after)


def _gates(fp, lb):
    sig = _sigmoid(fp)
    f = lb + (1.0 - lb) * sig
    return sig, f, jnp.log(f), 1.0 - f


def _chunk_masks():
    row = lax.broadcasted_iota(jnp.int32, (CHUNK, CHUNK), 0)
    col = lax.broadcasted_iota(jnp.int32, (CHUNK, CHUNK), 1)
    return row >= col, row <= col


def _hgrn_fwd(proj, ibcu, lb_logits, after):
    t = proj.shape[1]
    tb = min(t, 512)
    ncb = tb // CHUNK

    def body(q_ref, f_ref, v_ref, lbl_ref, after_ref, o_ref, st_ref, s_scr):
        @pl.when(pl.program_id(0) == 0)
        def _():
            s_scr[...] = jnp.zeros_like(s_scr)

        lb, _ = _lower_bound(lbl_ref[...])
        causal, _ = _chunk_masks()

        every = range(ncb)
        rows = [slice(c * CHUNK, (c + 1) * CHUNK) for c in every]
        q, v = [q_ref[r, :] for r in rows], [v_ref[r, :] for r in rows]
        gates = [_gates(f_ref[r, :], lb) for r in rows]
        k = [gt[3] for gt in gates]
        b = [_dot_exact(causal, gt[2]) for gt in gates]
        mid, last = [x[CHUNK // 2:CHUNK // 2 + 1, :] for x in b], [x[CHUNK - 1:CHUNK, :] for x in b]
        qt = [q[c] * jnp.exp(b[c] - mid[c]) for c in every]
        kt = [k[c] * jnp.exp(mid[c] - b[c]) for c in every]
        qi = [q[c] * jnp.exp(b[c]) for c in every]
        ks = [k[c] * jnp.exp(last[c] - b[c]) for c in every]
        dec = [jnp.exp(x) for x in last]
        scores = [[jnp.where(causal, _dot(a, b_, NT), 0.0) for a, b_ in zip(_heads(qt[c]), _heads(kt[c]))] for c in every]
        intra = [[_dot(s, v_h, NN) for s, v_h in zip(scores[c], _heads(v[c]))] for c in every]
        update = [_per_head(lambda v_h, ks_h: _dot(v_h, ks_h, TN), v[c], ks[c]) for c in every]

        st = s_scr[...]
        states = []
        for c in every:
            states.append(st)
            st_ref[c] = st
            st = dec[c] * st + update[c]
        s_scr[...] = st

        o_ref[...] = jnp.concatenate(
            [jnp.concatenate([i_h + _dot(qi_h, st_h, NT) for i_h, qi_h, st_h in
                              zip(intra[c], _heads(qi[c]), _heads(states[c]))], axis=1) for c in every], axis=0)

    grp = lambda g: pl.BlockSpec((None, tb, GROUP), lambda i: (g, i, 0))
    return pl.pallas_call(
        body, name="hgrn_fwd", grid=(t // tb,),
        in_specs=[grp(0), grp(1), grp(0), pl.BlockSpec((2, HGRN_WIDTH), lambda i: (0, 0)), ANY],
        out_specs=[pl.BlockSpec((tb, HGRN_WIDTH), lambda i: (i, 0)),
                   pl.BlockSpec((ncb, HEAD_DIM, HGRN_WIDTH), lambda i: (i, 0, 0))],
        out_shape=[jax.ShapeDtypeStruct((t, HGRN_WIDTH), F32),
                   jax.ShapeDtypeStruct((t // CHUNK, HEAD_DIM, HGRN_WIDTH), F32)],
        scratch_shapes=[pltpu.VMEM((HEAD_DIM, HGRN_WIDTH), F32)],
        compiler_params=_params("arbitrary"),
    )(proj, proj, ibcu, lb_logits, after)


def _gate_fwd(proj, o, gate_norm_w, after):
    t = proj.shape[1]
    tb = min(t, 1024)

    def body(o_ref, og_ref, gnw_ref, after_ref, out_ref):
        gnw = gnw_ref[...]
        for s in range(tb // GATE_STRIP):
            rows = slice(s * GATE_STRIP, (s + 1) * GATE_STRIP)
            og = og_ref[rows, :]
            on = _per_head(lambda o_h: o_h * lax.rsqrt(jnp.mean(o_h * o_h, axis=-1, keepdims=True) + EPS), o_ref[rows, :])
            out_ref[rows, :] = (on * gnw * (og * _sigmoid(og))).astype(out_ref.dtype)

    tile = pl.BlockSpec((tb, GROUP), lambda i: (i, 0))
    return pl.pallas_call(
        body, name="gate_fwd", grid=(t // tb,),
        in_specs=[tile, pl.BlockSpec((None, tb, GROUP), lambda i: (2, i, 0)), pl.BlockSpec((1, GROUP), lambda i: (0, 0)), ANY],
        out_specs=tile,
        out_shape=jax.ShapeDtypeStruct((t, HGRN_WIDTH), BF16),
        compiler_params=_params("parallel"),
    )(o, proj, gate_norm_w, after)


def _ln_bwd(dy, xhat, rstd, g):
    dxhat = dy * g
    m1 = jnp.mean(dxhat, axis=-1, keepdims=True)
    m2 = jnp.mean(dxhat * xhat, axis=-1, keepdims=True)
    return rstd * (dxhat - m1 - xhat * m2)


def _layer_norm(pre):
    xc = pre - jnp.mean(pre, axis=-1, keepdims=True)
    rstd = lax.rsqrt(jnp.mean(xc * xc, axis=-1, keepdims=True) + EPS)
    return xc * rstd, rstd


def _sublayers(cat_h, cat_c, x, target, w_out, w_ff1, w_ff2, g1, b1, g2, b2):
    t = x.shape[0]
    tm = min(t, 256)

    def body(ch_ref, cc_ref, x_ref, tg_ref, wo_ref, w1_ref, w2_ref, g1_ref, b1_ref, g2_ref, b2_ref,
             h1_ref, r_ref, da_ref, dp2b_ref, dp1_ref, dcat_ref, dg1_ref, db1_ref, dg2_ref, db2_ref, loss_ref, gwo_ref,
             gwo_acc, gwo_narrow, sem):
        @pl.when(pl.program_id(0) == 0)
        def _():
            for ref in (dg1_ref, db1_ref, dg2_ref, db2_ref, loss_ref, gwo_acc):
                ref[...] = jnp.zeros_like(ref)

        mix = _dot(ch_ref[...], wo_ref[0:GROUP, :], NN) + _dot(cc_ref[...], wo_ref[GROUP:2 * GROUP, :], NN)
        xhat1, rstd1 = _layer_norm(ALPHA * x_ref[...] + mix)
        h1 = xhat1 * g1_ref[...] + b1_ref[...]
        h1b = h1.astype(h1_ref.dtype)
        h1_ref[...] = h1b
        mlp = jnp.zeros((tm, D_MODEL), F32)
        for j in range(N_FF):
            cols = slice(j * FF_BLOCK, (j + 1) * FF_BLOCK)
            r = jnp.square(jnp.maximum(_dot(h1b, w1_ref[:, cols], NN), 0.0)).astype(r_ref.dtype)
            r_ref[:, cols] = r
            mlp = mlp + _dot(r, w2_ref[cols, :], NN)
        xhat2, rstd2 = _layer_norm(ALPHA * h1 + mlp)
        err = xhat2 * g2_ref[...] + b2_ref[...] - tg_ref[...]
        loss_ref[...] += 0.5 * jnp.sum(jnp.mean(err * err, axis=-1, keepdims=True))
        dy = err * (1.0 / D_MODEL)
        dg2_ref[...] += jnp.sum(dy * xhat2, axis=0, keepdims=True)
        db2_ref[...] += jnp.sum(dy, axis=0, keepdims=True)
        dp2 = _ln_bwd(dy, xhat2, rstd2, g2_ref[...])
        dp2b = dp2.astype(dp2b_ref.dtype)
        dp2b_ref[...] = dp2b
        back = jnp.zeros((tm, D_MODEL), F32)
        for j in range(N_FF):
            cols = slice(j * FF_BLOCK, (j + 1) * FF_BLOCK)
            dr = _dot(dp2b, w2_ref[cols, :], NT)
            da = (dr * (2.0 * jnp.sqrt(r_ref[:, cols].astype(F32)))).astype(da_ref.dtype)
            da_ref[:, cols] = da
            back = back + _dot(da, w1_ref[:, cols], NT)
        dh1 = ALPHA * dp2 + back
        dg1_ref[...] += jnp.sum(dh1 * xhat1, axis=0, keepdims=True)
        db1_ref[...] += jnp.sum(dh1, axis=0, keepdims=True)
        dp1 = _ln_bwd(dh1, xhat1, rstd1, g1_ref[...])
        dp1b = dp1.astype(MXU_DTYPE)
        dp1_ref[...] = dp1
        dcat_ref[...] = _dot(dp1b, wo_ref[...], NT)
        gwo_acc[0:GROUP, :] += _dot(ch_ref[...], dp1b, TN)
        gwo_acc[GROUP:2 * GROUP, :] += _dot(cc_ref[...], dp1b, TN)

        @pl.when(pl.program_id(0) == pl.num_programs(0) - 1)
        def _():
            gwo_narrow[...] = gwo_acc[...].astype(gwo_narrow.dtype)
            copy = pltpu.make_async_copy(gwo_narrow, gwo_ref, sem.at[0])
            copy.start()
            copy.wait()

    row = pl.BlockSpec((tm, D_MODEL), lambda i: (i, 0))
    wide = pl.BlockSpec((tm, D_FF), lambda i: (i, 0))
    vec = pl.BlockSpec((1, D_MODEL), lambda i: (0, 0))
    narrow = lambda dtype: jax.ShapeDtypeStruct((t, D_MODEL), dtype)
    return pl.pallas_call(
        body, name="sublayers", grid=(t // tm,),
        in_specs=[pl.BlockSpec((tm, GROUP), lambda i: (i, 0)), pl.BlockSpec((tm, GROUP), lambda i: (i, 0)), row, row,
                  _resident((D_MODEL, D_MODEL)),
                  _resident((D_MODEL, D_FF)), _resident((D_FF, D_MODEL)), vec, vec, vec, vec],
        out_specs=[row, wide, wide, row, row, row, vec, vec, vec, vec,
                   pl.BlockSpec((SUBLANES, LANES), lambda i: (0, 0)), ANY],
        out_shape=[narrow(BF16), jax.ShapeDtypeStruct((t, D_FF), BF16), jax.ShapeDtypeStruct((t, D_FF), BF16),
                   narrow(BF16), narrow(F32), narrow(F32)]
                  + [jax.ShapeDtypeStruct((1, D_MODEL), F32)] * 4
                  + [jax.ShapeDtypeStruct((SUBLANES, LANES), F32), jax.ShapeDtypeStruct((D_MODEL, D_MODEL), BF16)],
        scratch_shapes=[pltpu.VMEM((D_MODEL, D_MODEL), F32), pltpu.VMEM((D_MODEL, D_MODEL), BF16),
                        pltpu.SemaphoreType.DMA((1,))],
        compiler_params=_params("arbitrary"),
    )(cat_h, cat_c, x, target, w_out, w_ff1, w_ff2, g1, b1, g2, b2)


def _hgrn_bwd(proj, ibcu, do, states, lb_logits, after):
    t = proj.shape[1]
    tb = min(t, 512)
    ncb = tb // CHUNK
    nblk = t // tb

    def body(q_ref, f_ref, v_ref, do_ref, st_ref, lbl_ref, after_ref, dp_ref, dlbl_ref, ds_scr, dlb_scr):
        i = pl.program_id(0)

        @pl.when(i == 0)
        def _():
            ds_scr[...] = jnp.zeros_like(ds_scr)
            dlb_scr[...] = jnp.zeros_like(dlb_scr)

        lb, s1 = _lower_bound(lbl_ref[...])
        causal, anti = _chunk_masks()
        every = range(ncb)
        rows = [slice(c * CHUNK, (c + 1) * CHUNK) for c in every]
        q, v, do = ([ref[r, :] for r in rows] for ref in (q_ref, v_ref, do_ref))
        st = [st_ref[c] for c in every]
        gates = [_gates(f_ref[r, :], lb) for r in rows]
        sig, f, k = ([gt[n] for gt in gates] for n in (0, 1, 3))
        b = [_dot_exact(causal, gt[2]) for gt in gates]
        mid, last = [x[CHUNK // 2:CHUNK // 2 + 1, :] for x in b], [x[CHUNK - 1:CHUNK, :] for x in b]
        e_q = [jnp.exp(b[c] - mid[c]) for c in every]
        e_k = [jnp.exp(mid[c] - b[c]) for c in every]
        e_i = [jnp.exp(x) for x in b]
        e_s = [jnp.exp(last[c] - b[c]) for c in every]
        dec = [jnp.exp(x) for x in last]
        qt, kt, qi, ks = ([a[c] * e[c] for c in every] for a, e in ((q, e_q), (k, e_k), (q, e_i), (k, e_s)))

        def masked(a, b_):
            return [[jnp.where(causal, _dot(a_h, b_h, NT), 0.0) for a_h, b_h in zip(_heads(a[c]), _heads(b_[c]))]
                    for c in every]

        def with_scores(s, other, dims):
            return [jnp.concatenate([_dot(s_h, o_h, dims) for s_h, o_h in zip(s[c], _heads(other[c]))], axis=1)
                    for c in every]

        def per_head(dims, a, b_):
            return [_per_head(lambda a_h, b_h: _dot(a_h, b_h, dims), a[c], b_[c]) for c in every]

        scores, dscores = masked(qt, kt), masked(do, v)
        dqt, dkt, dv_intra = with_scores(dscores, kt, NN), with_scores(dscores, qt, TN), with_scores(scores, do, TN)
        dqi, update = per_head(NN, do, st), per_head(TN, do, qi)

        dst = ds_scr[...]
        dsts = [None] * ncb
        for c in reversed(every):
            dsts[c] = dst
            dst = dec[c] * dst + update[c]
        ds_scr[...] = dst

        dv_state, dks = per_head(NT, ks, dsts), per_head(NN, v, dsts)
        ddec = [jnp.sum(dsts[c] * st[c], axis=0, keepdims=True) for c in every]
        dq = [dqt[c] * e_q[c] + dqi[c] * e_i[c] for c in every]
        dk = [dkt[c] * e_k[c] + dks[c] * e_s[c] for c in every]
        db = [q[c] * dq[c] - k[c] * dk[c] for c in every]
        db_last = [jnp.sum(dks[c] * ks[c], axis=0, keepdims=True) + ddec[c] * dec[c] for c in every]
        dg = [_dot_exact(anti, db[c]) + db_last[c] for c in every]
        df = [dg[c] / f[c] - dk[c] for c in every]
        dlb_scr[...] += sum(jnp.sum(df[c] * (1.0 - sig[c]), axis=0, keepdims=True) for c in every)
        dfp = [df[c] * (1.0 - lb) * sig[c] * (1.0 - sig[c]) for c in every]
        dv = [dv_intra[c] + dv_state[c] for c in every]
        for n, parts in enumerate((dq, dfp, dv)):
            dp_ref[n] = jnp.concatenate(parts, axis=0).astype(dp_ref.dtype)

        @pl.when(i == nblk - 1)
        def _():
            dlb = dlb_scr[...]
            dlbl_ref[0:1, :] = dlb * lb * (1.0 - lb)
            dlbl_ref[1:2, :] = -dlb * lb * s1

    grp = lambda g: pl.BlockSpec((None, tb, GROUP), lambda i: (g, nblk - 1 - i, 0))
    vec = pl.BlockSpec((2, HGRN_WIDTH), lambda i: (0, 0))
    return pl.pallas_call(
        body, name="hgrn_bwd", grid=(nblk,),
        in_specs=[grp(0), grp(1), grp(0), pl.BlockSpec((tb, HGRN_WIDTH), lambda i: (nblk - 1 - i, 0)),
                  pl.BlockSpec((ncb, HEAD_DIM, HGRN_WIDTH), lambda i: (nblk - 1 - i, 0, 0)), vec, ANY],
        out_specs=[pl.BlockSpec((3, tb, HGRN_WIDTH), lambda i: (0, nblk - 1 - i, 0)), vec],
        out_shape=[jax.ShapeDtypeStruct((3, t, HGRN_WIDTH), BF16), jax.ShapeDtypeStruct((2, HGRN_WIDTH), F32)],
        scratch_shapes=[pltpu.VMEM((HEAD_DIM, HGRN_WIDTH), F32), pltpu.VMEM((1, HGRN_WIDTH), F32)],
        compiler_params=_params("arbitrary"),
    )(proj, proj, ibcu, do, states, lb_logits, after)


GRAD_TILE = 512
OUT_PARTS = 4


class _Side:
    def __init__(self, operands, in_specs, out_shape, out_specs, scratch, init, begin):
        self.operands, self.in_specs, self.out_shape, self.out_specs = operands, in_specs, out_shape, out_specs
        self.scratch, self.init, self.begin = scratch, init, begin


def _grad_w(name, operands, widths, shape, step, after=None, side=None):
    t = operands[0].shape[-2]
    tt = min(t, GRAD_TILE)
    n_in, n_steps = len(operands), t // tt
    in_specs = [pl.BlockSpec((tt, w), lambda k: (k, 0)) if a.ndim == 2 else
                pl.BlockSpec((a.shape[0], tt, w), lambda k: (0, k, 0)) for a, w in zip(operands, widths)]
    extra = [] if after is None else [after]
    s_in, s_out = (len(side.operands), len(side.out_shape)) if side else (0, 0)
    first_out = n_in + s_in + len(extra)

    def body(*refs):
        o_ref, side_outs = refs[first_out], refs[first_out + 1:first_out + 1 + s_out]
        acc, narrow, sem = refs[first_out + 1 + s_out:first_out + 4 + s_out]
        k = pl.program_id(0)

        @pl.when(k == 0)
        def _():
            acc[...] = jnp.zeros_like(acc)
            if side:
                side.init(side_outs)

        tick = side.begin(k, n_steps, refs[n_in:n_in + s_in], side_outs, refs[first_out + 4 + s_out:]) if side else None
        step(acc, *refs[:n_in], tick or (lambda j: None))

        @pl.when(k == n_steps - 1)
        def _():
            part = shape[0] // OUT_PARTS
            copies = []
            for p in range(OUT_PARTS):
                rows = pl.ds(p * part, part)
                narrow[rows, :] = acc[rows, :].astype(narrow.dtype)
                copies.append(pltpu.make_async_copy(narrow.at[rows, :], o_ref.at[rows, :], sem.at[p]))
                copies[-1].start()
            for cp in copies:
                cp.wait()

    outs = pl.pallas_call(
        body, name=name, grid=(n_steps,),
        in_specs=in_specs + (side.in_specs if side else []) + [ANY] * len(extra),
        out_specs=[ANY] + (side.out_specs if side else []),
        out_shape=[jax.ShapeDtypeStruct(shape, BF16)] + (side.out_shape if side else []),
        scratch_shapes=[pltpu.VMEM(shape, F32), pltpu.VMEM(shape, BF16), pltpu.SemaphoreType.DMA((OUT_PARTS,))]
                       + (side.scratch if side else []),
        compiler_params=_params("arbitrary"),
    )(*operands, *(side.operands if side else ()), *extra)
    return outs if side else outs[0]


def _dw_in(xb, dph, dog, dpc, w_in, dpre1, after):
    t = xb.shape[0]

    def step(acc, x_ref, dh_ref, dog_ref, dc_ref, tick):
        xv = x_ref[...]
        for g in range(N_GROUPS):
            part = dh_ref[g] if g < 3 else dog_ref[...] if g == 3 else dc_ref[g - 4]
            acc[:, g * GROUP:(g + 1) * GROUP] += _dot(xv, part, TN)
            tick(g, part)

    def begin(k, n_steps, ins, outs, scratch):
        w_ref, dp_ref = ins
        total = [ALPHA * dp_ref[...]]

        def tick(g, part):
            total[0] = total[0] + _dot(part, w_ref[:, g * GROUP:(g + 1) * GROUP], NT)
            if g == N_GROUPS - 1:
                outs[0][...] = total[0]

        return tick

    row = pl.BlockSpec((min(t, GRAD_TILE), D_MODEL), lambda k: (k, 0))
    side = _Side((w_in, dpre1), [_resident((D_MODEL, IN_COLS)), row], [jax.ShapeDtypeStruct((t, D_MODEL), F32)], [row],
                 [], lambda outs: None, begin)
    return _grad_w("dw_in", (xb, dph, dog, dpc), (D_MODEL, GROUP, GROUP, GROUP), (D_MODEL, IN_COLS), step, after, side)


def _strips_of(j, tt):
    per_tick = tt // GATE_STRIP // N_FF
    return [slice(s * GATE_STRIP, (s + 1) * GATE_STRIP) for s in range(j * per_tick, (j + 1) * per_tick)]


def _dw_ff1(h1b, da, dcat, o, proj, gate_norm_w, after):
    t = h1b.shape[0]
    tt = min(t, GRAD_TILE)

    def step(acc, h_ref, da_ref, tick):
        hv = h_ref[...]
        for j in range(N_FF):
            cols = slice(j * FF_BLOCK, (j + 1) * FF_BLOCK)
            acc[:, cols] += _dot(hv, da_ref[:, cols], TN)
            tick(j)

    def init(outs):
        outs[2][...] = jnp.zeros_like(outs[2])

    def begin(k, n_steps, ins, outs, scratch):
        do2_ref, o_ref, og_ref, gnw_ref = ins
        do_ref, dog_ref, dgnw_ref = outs
        total = [jnp.zeros((GATE_STRIP, GROUP), F32)]

        def tick(j):
            gnw = gnw_ref[...]
            for rows in _strips_of(j, tt):
                ov, og, do2 = o_ref[rows, :], og_ref[rows, :], do2_ref[rows, :]
                rs = _per_head(lambda o_h: jnp.broadcast_to(
                    lax.rsqrt(jnp.mean(o_h * o_h, axis=-1, keepdims=True) + EPS), o_h.shape), ov)
                on = ov * rs
                sg = _sigmoid(og)
                sil = og * sg
                don = do2 * gnw * sil
                total[0] = total[0] + do2 * on * sil
                dog_ref[rows, :] = (do2 * on * gnw * (sg * (1.0 + og * (1.0 - sg)))).astype(dog_ref.dtype)
                do_ref[rows, :] = (rs * (don - on * _per_head(
                    lambda p_h: jnp.broadcast_to(jnp.mean(p_h, axis=-1, keepdims=True), p_h.shape), don * on))
                                   ).astype(do_ref.dtype)
            if j == N_FF - 1:
                dgnw_ref[...] += jnp.sum(total[0], axis=0, keepdims=True)

        return tick

    tile = pl.BlockSpec((tt, GROUP), lambda k: (k, 0))
    vec = pl.BlockSpec((1, GROUP), lambda k: (0, 0))
    side = _Side(
        (dcat, o, proj, gate_norm_w), [tile, tile, pl.BlockSpec((None, tt, GROUP), lambda k: (2, k, 0)), vec],
        [jax.ShapeDtypeStruct((t, HGRN_WIDTH), BF16), jax.ShapeDtypeStruct((t, HGRN_WIDTH), BF16),
         jax.ShapeDtypeStruct((1, HGRN_WIDTH), F32)], [tile, tile, vec], [], init, begin)
    return _grad_w("dw_ff1", (h1b, da), (D_MODEL, D_FF), (D_MODEL, D_FF), step, after, side)


def _dw_ff2(r, dpre2b, dcat, ibcu, conv_w):
    t = r.shape[0]
    tt = min(t, GRAD_TILE)
    hb = tt // SUBLANES
    halo = 2 * SUBLANES

    def step(acc, r_ref, d_ref, tick):
        dv = d_ref[...]
        for j in range(N_FF):
            rows = slice(j * FF_BLOCK, (j + 1) * FF_BLOCK)
            acc[rows, :] += _dot(r_ref[:, rows], dv, TN)
            tick(j)

    def init(outs):
        outs[1][...] = jnp.zeros_like(outs[1])

    def begin(k, n_steps, ins, outs, scratch):
        dy_ref, dyn_ref, b_ref, bn_ref, c_ref, u_ref, ch_ref, uh_ref, cw_ref = ins
        dp_ref, dcw_ref = outs
        zbuf, dbuf = scratch
        before = lambda ref: ref[SUBLANES:halo, :].astype(F32)
        zbuf[0:SUBLANES, :] = jnp.where(k > 0, before(ch_ref) * before(uh_ref), 0.0)
        zbuf[SUBLANES:SUBLANES + tt, :] = c_ref[...].astype(F32) * u_ref[...].astype(F32)
        dbuf[0:tt, :] = dy_ref[...] * b_ref[...].astype(F32)
        dbuf[tt:tt + SUBLANES, :] = jnp.where(k < n_steps - 1, dyn_ref[...] * bn_ref[0:SUBLANES, :].astype(F32), 0.0)
        totals = [jnp.zeros((GATE_STRIP, GROUP), F32) for _ in range(3)]

        def tick(j):
            cw = cw_ref[...]
            for rows in _strips_of(j, tt):
                at = lambda buf, shift: buf[shift + rows.start:shift + rows.stop, :]
                z, z1, z2 = at(zbuf, SUBLANES), at(zbuf, SUBLANES - 1), at(zbuf, SUBLANES - 2)
                dyc, d1, d2 = at(dbuf, 0), at(dbuf, 1), at(dbuf, 2)
                yc = cw[2:3, :] * z + cw[1:2, :] * z1 + cw[0:1, :] * z2
                dz = cw[2:3, :] * dyc + cw[1:2, :] * d1 + cw[0:1, :] * d2
                dp_ref[0, rows, :] = (dy_ref[rows, :] * yc).astype(dp_ref.dtype)
                dp_ref[1, rows, :] = (dz * u_ref[rows, :].astype(F32)).astype(dp_ref.dtype)
                dp_ref[2, rows, :] = (dz * c_ref[rows, :].astype(F32)).astype(dp_ref.dtype)
                for n, tap in enumerate((z2, z1, z)):
                    totals[n] = totals[n] + dyc * tap
            if j == N_FF - 1:
                for n in range(3):
                    dcw_ref[n:n + 1, :] += jnp.sum(totals[n], axis=0, keepdims=True)

        return tick

    grp = lambda g: pl.BlockSpec((None, tt, GROUP), lambda k: (g, k, 0))
    prev = lambda g: pl.BlockSpec((None, halo, GROUP), lambda k: (g, jnp.maximum(k * (tt // halo) - 1, 0), 0))
    nxt = lambda g: pl.BlockSpec((None, halo, GROUP), lambda k: (g, jnp.minimum((k + 1) * (tt // halo), t // halo - 1), 0))
    nxt_row = lambda k: jnp.minimum((k + 1) * hb, t // SUBLANES - 1)
    whole = pl.BlockSpec((3, CONV_WIDTH), lambda k: (0, 0))
    side = _Side(
        (dcat, dcat, ibcu, ibcu, ibcu, ibcu, ibcu, ibcu, conv_w),
        [pl.BlockSpec((tt, GROUP), lambda k: (k, 1)), pl.BlockSpec((SUBLANES, GROUP), lambda k: (nxt_row(k), 1)),
         grp(1), nxt(1), grp(2), grp(3), prev(2), prev(3), whole],
        [jax.ShapeDtypeStruct((3, t, CONV_WIDTH), BF16), jax.ShapeDtypeStruct((3, CONV_WIDTH), F32)],
        [pl.BlockSpec((3, tt, GROUP), lambda k: (0, k, 0)), whole],
        [pltpu.VMEM((tt + SUBLANES, GROUP), F32), pltpu.VMEM((tt + SUBLANES, GROUP), F32)], init, begin)
    return _grad_w("dw_ff2", (r, dpre2b), (D_FF, D_MODEL), (D_FF, D_MODEL), step, side=side)


def _place():
    x, y, c = lax.axis_index("x"), lax.axis_index("y"), lax.axis_index("c")
    return x, y, c, 2 * x + y


def _other_chips(x, y):
    return [(1 - x, y), (x, 1 - y), (1 - x, 1 - y)]


def _place_shard(name, w, chip, cols_sharded, after=None):
    rows, cols = w.shape
    tr = min(rows, 256)
    nb = rows // tr
    full = (rows, cols * N_CHIPS) if cols_sharded else (rows * N_CHIPS, cols)
    out_map = (lambda i, s: (i, s[0])) if cols_sharded else (lambda i, s: (s[0] * nb + i, 0))

    def body(s_ref, w_ref, *rest):
        rest[-1][...] = w_ref[...].astype(rest[-1].dtype)

    extra = [] if after is None else [after]
    return pl.pallas_call(
        body, name=name,
        grid_spec=pltpu.PrefetchScalarGridSpec(
            num_scalar_prefetch=1, grid=(nb,),
            in_specs=[pl.BlockSpec((tr, cols), lambda i, s: (i, 0))] + [ANY] * len(extra),
            out_specs=pl.BlockSpec((tr, cols), out_map)),
        out_shape=jax.ShapeDtypeStruct(full, BF16),
        compiler_params=_params("parallel"),
    )(chip, w, *extra)


HBM = pl.BlockSpec(memory_space=pltpu.HBM)
SEM = pl.BlockSpec(memory_space=pltpu.SEMAPHORE)
EFFECT = pltpu.SideEffectType.DATAFLOW_SIDE_EFFECTING


PEER_SETS = {
    "sibling": (0, lambda x, y, c: [(x, y, 1 - c)]),
    "chips": (1, lambda x, y, c: [(1 - x, y, c), (x, 1 - y, c), (1 - x, 1 - y, c)]),
    "neighbours": (2, lambda x, y, c: [(1 - x, y, c), (x, 1 - y, c)]),
}


class _Split:
    def __init__(self, name, arrays, plan, others=(), peers=None):
        n_own, arrays = len(arrays), (*arrays, *others)
        n, n_copies = len(arrays), plan.count
        self.name, self.plan, self.n = name, plan, n_own
        barrier_id, peer_ids = PEER_SETS[peers] if peers else (None, None)

        def body(*refs):
            if peers:
                x, y, c, _ = _place()
                barrier = pltpu.get_barrier_semaphore()
                for peer in peer_ids(x, y, c):
                    pl.semaphore_signal(barrier, inc=1, device_id=peer, device_id_type=MESH)
                pl.semaphore_wait(barrier, len(peer_ids(0, 0, 0)))
            send_sems, recv_sems, token = refs[n], refs[n + 1], refs[-1]
            for k, (src, dst, to) in enumerate(plan(refs[:n])):
                pltpu.make_async_remote_copy(src_ref=src, dst_ref=dst, send_sem=send_sems.at[k], recv_sem=recv_sems.at[k],
                                             device_id=to, device_id_type=MESH).start()
            token[...] = jnp.zeros_like(token)

        outs = pl.pallas_call(
            body, name=name + "_start",
            out_shape=(pltpu.SemaphoreType.DMA((n_copies,)), pltpu.SemaphoreType.DMA((n_copies,)),
                       *[pltpu.HBM(a.shape, a.dtype) for a in arrays], jax.ShapeDtypeStruct((SUBLANES, LANES), F32)),
            in_specs=(HBM,) * n, out_specs=(SEM, SEM) + (HBM,) * n + (pl.BlockSpec(memory_space=pltpu.VMEM),),
            input_output_aliases={i: 2 + i for i in range(n)},
            compiler_params=pltpu.CompilerParams(has_side_effects=EFFECT, collective_id=barrier_id),
        )(*[pltpu.with_memory_space_constraint(a, pltpu.HBM) for a in arrays])
        self.sems, self.arrays, self.others, self.token = outs[:2], outs[2:2 + n_own], outs[2 + n_own:2 + n], outs[-1]

    def wait(self, after):
        n, plan = self.n, self.plan

        def body(*refs):
            send_sems, recv_sems = refs[n], refs[n + 1]
            for k, (src, dst, to) in enumerate(plan(refs[:n])):
                cp = pltpu.make_async_remote_copy(src_ref=src, dst_ref=dst, send_sem=send_sems.at[k],
                                                  recv_sem=recv_sems.at[k], device_id=to, device_id_type=MESH)
                cp.wait_send()
                cp.wait_recv()

        return pl.pallas_call(
            body, name=self.name + "_wait", out_shape=tuple(pltpu.HBM(a.shape, a.dtype) for a in self.arrays),
            in_specs=(HBM,) * n + (SEM, SEM, ANY), out_specs=(HBM,) * n, input_output_aliases={i: i for i in range(n)},
            compiler_params=pltpu.CompilerParams(has_side_effects=EFFECT),
        )(*self.arrays, *self.sems, after)


COLS_SHARDED = (True, False, True, False)
HALF_SHAPES = [(D_MODEL // 2, IN_COLS), (D_MODEL, D_MODEL // 2), (D_MODEL // 2, D_FF), (D_FF, D_MODEL // 2)]
PIECE_SHAPES = [(D_MODEL // 2, IN_COLS // N_CHIPS), (D_MODEL // N_CHIPS, D_MODEL // 2),
                (D_MODEL // 2, D_FF // N_CHIPS), (D_FF // N_CHIPS, D_MODEL // 2)]


def _shard_view(kind, ref, chip):
    if COLS_SHARDED[kind]:
        n = ref.shape[1] // N_CHIPS
        return ref.at[:, pl.ds(chip * n, n)]
    n = ref.shape[0] // N_CHIPS
    return ref.at[pl.ds(chip * n, n), :]


def _half_view(kind, ref, h):
    if COLS_SHARDED[kind]:
        n = ref.shape[0] // 2
        return ref.at[pl.ds(h * n, n), :]
    n = ref.shape[1] // 2
    return ref.at[:, pl.ds(h * n, n)]


def _plan(count):
    def mark(fn):
        fn.count = count
        return fn
    return mark


def _shard_rows_view(kind, ref, chip, part, n_parts):
    if COLS_SHARDED[kind]:
        m, n = ref.shape[0] // n_parts, ref.shape[1] // N_CHIPS
        return ref.at[pl.ds(part * m, m), pl.ds(chip * n, n)]
    m = ref.shape[0] // N_CHIPS // n_parts
    return ref.at[pl.ds((n_parts * chip + part) * m, m), :]


def _shard_half_view(kind, ref, chip, h):
    return _shard_rows_view(kind, ref, chip, h, 2)


def _gather_over_ici(kinds, weights):
    @_plan(2 * len(kinds))
    def plan(refs):
        x, y, c, me = _place()
        mine = [_shard_half_view(kind, ref, me, c) for kind, ref in zip(kinds, refs)]
        return [(v, v, to) for v in mine for to in ((1 - x, y, c), (x, 1 - y, c))]

    return _Split("gather_ici_" + "".join(map(str, kinds)), tuple(weights), plan, peers="neighbours")


def _relay_over_ici(kinds, weights, others=()):
    @_plan(2 * len(kinds))
    def plan(refs):
        x, y, c, _ = _place()
        x_nbr, y_nbr = 2 * (1 - x) + y, 2 * x + (1 - y)
        out = []
        for kind, ref in zip(kinds, refs):
            first, second = (_shard_rows_view(kind, ref, chip, 2 * c + q, 4) for q, chip in ((0, x_nbr), (1, y_nbr)))
            out += [(first, first, (x, 1 - y, c)), (second, second, (1 - x, y, c))]
        return out

    return _Split("relay_ici_" + "".join(map(str, kinds)), tuple(weights), plan, others, peers="neighbours")


def _gather_w_in_over_ici(w_in, conv4):
    @_plan(6)
    def plan(refs):
        x, y, c, me = _place()
        half, conv = _shard_half_view(0, refs[0], me, c), refs[1].at[me]
        return [(v, v, (px, py, c)) for v in (half, conv) for px, py in _other_chips(x, y)]

    return _Split("gather_w_in_ici", (w_in, conv4), plan, peers="chips")


def _gather_over_d2d(kinds, weights):
    @_plan(3 * len(kinds))
    def plan(refs):
        x, y, c, _ = _place()
        got = [_shard_half_view(kind, ref, 2 * px + py, c) for kind, ref in zip(kinds, refs)
               for px, py in _other_chips(x, y)]
        return [(v, v, (x, y, 1 - c)) for v in got]

    return _Split("gather_d2d_" + "".join(map(str, kinds)), tuple(weights), plan, peers="sibling")


def _swap_halves(kinds, grads):
    @_plan(len(kinds))
    def plan(refs):
        x, y, c, _ = _place()
        return [(_half_view(kind, g, 1 - c), land, (x, y, 1 - c))
                for kind, g, land in zip(kinds, refs[:len(kinds)], refs[len(kinds):])]

    lands = [lax.empty(HALF_SHAPES[kind], g.dtype) for kind, g in zip(kinds, grads)]
    return _Split("swap_halves_" + "".join(map(str, kinds)), (*grads, *lands), plan, peers="sibling")


def _block_rows(cols, elements):
    return 1 << ((elements // cols).bit_length() - 1)


def _add_half(name, g, recv, core, rows_split):
    shape = recv.shape
    tr = min(shape[0], _block_rows(shape[1], 1 << 20))
    nb = shape[0] // tr

    def body(c_ref, g_ref, r_ref, o_ref):
        o_ref[...] = (g_ref[...].astype(F32) + r_ref[...].astype(F32)).astype(o_ref.dtype)

    g_map = (lambda i, c_ref: (c_ref[0] * nb + i, 0)) if rows_split else (lambda i, c_ref: (i, c_ref[0]))
    blk = pl.BlockSpec((tr, shape[1]), lambda i, c_ref: (i, 0))
    return pl.pallas_call(
        body, name=name,
        grid_spec=pltpu.PrefetchScalarGridSpec(
            num_scalar_prefetch=1, grid=(nb,),
            in_specs=[pl.BlockSpec((tr, shape[1]), g_map), blk], out_specs=blk),
        out_shape=jax.ShapeDtypeStruct(shape, BF16),
        compiler_params=_params("parallel"),
    )(core, g, recv)


def _exchange_pieces(kinds, halves, pack=None):
    n_p, n = N_CHIPS - 1, len(kinds)

    @_plan(n_p * n + (0 if pack is None else N_DEV - 1))
    def plan(refs):
        x, y, c, _ = _place()
        copies = []
        if pack is not None:
            me = 4 * x + 2 * y + c
            peers = [((1 - x) if m & 4 else x, (1 - y) if m & 2 else y, (1 - c) if m & 1 else c) for m in range(1, N_DEV)]
            copies += [(refs[2 * n], refs[2 * n + 1].at[me], peer) for peer in peers]
        return copies + [(_shard_view(kind, half, 2 * px + py), land.at[j], (px, py, c))
                         for j, (px, py) in enumerate(_other_chips(x, y))
                         for kind, half, land in zip(kinds, refs[:n], refs[n:2 * n])]

    lands = [lax.empty((n_p,) + PIECE_SHAPES[kind], BF16) for kind in kinds]
    small = () if pack is None else (pack, lax.empty((N_DEV,) + pack.shape, F32))
    return _Split("exchange_pieces_" + "".join(map(str, kinds)), (*halves, *lands, *small), plan,
                  peers="chips" if pack is None else None)


def _sum_pieces(name, half, slots, place, rows_split, after):
    n_p, rows, cols = slots.shape
    tr = min(rows, _block_rows(cols, 1 << 18))
    nb = rows // tr
    if rows_split:
        own_map = lambda i, s: (i, s[0])
        out_map = lambda i, s: (s[1] * nb + i, 0)
        shard = (2 * rows, cols)
    else:
        own_map = lambda i, s: (s[0] * nb + i, 0)
        out_map = lambda i, s: (i, s[1])
        shard = (rows, 2 * cols)

    def body(s_ref, own_ref, slot_ref, after_ref, o_ref):
        total = own_ref[...].astype(F32)
        for j in range(n_p):
            total = total + slot_ref[j].astype(F32)
        o_ref[...] = total

    return pl.pallas_call(
        body, name=name,
        grid_spec=pltpu.PrefetchScalarGridSpec(
            num_scalar_prefetch=1, grid=(nb,),
            in_specs=[pl.BlockSpec((tr, cols), own_map), pl.BlockSpec((n_p, tr, cols), lambda i, s: (0, i, 0)), ANY],
            out_specs=pl.BlockSpec((tr, cols), out_map)),
        out_shape=jax.ShapeDtypeStruct(shard, F32),
        compiler_params=_params("parallel"),
    )(place, half, slots, after)


def _join_halves(kinds, shards):
    @_plan(len(kinds))
    def plan(refs):
        x, y, c, _ = _place()
        return [(_half_view(kind, g, c), _half_view(kind, g, c), (x, y, 1 - c)) for kind, g in zip(kinds, refs)]

    return _Split("join_halves_" + "".join(map(str, kinds)), tuple(shards), plan, peers="sibling")


N_DEV = 8


def _sum_shared(pack, land, device):
    def body(d_ref, p_ref, l_ref, o_ref):
        me = d_ref[0]
        total = jnp.where(me == 0, p_ref[...], l_ref[0])
        for d in range(1, N_DEV):
            total = total + jnp.where(me == d, p_ref[...], l_ref[d])
        o_ref[...] = total

    return pl.pallas_call(
        body, name="sum_shared",
        grid_spec=pltpu.PrefetchScalarGridSpec(
            num_scalar_prefetch=1, grid=(1,),
            in_specs=[pl.BlockSpec(pack.shape, lambda i, d: (0, 0)), pl.BlockSpec(land.shape, lambda i, d: (0, 0, 0))],
            out_specs=pl.BlockSpec(pack.shape, lambda i, d: (0, 0))),
        out_shape=jax.ShapeDtypeStruct(pack.shape, F32),
    )(device, pack, land)


def _adamw(name, w, g, m, v, after=None):
    rows, cols = w.shape
    tr = min(rows, 256)
    extra = [] if after is None else [after]

    def body(w_ref, g_ref, m_ref, v_ref, *rest):
        go_ref, d_ref, nm_ref, nv_ref = rest[-4:]
        g = g_ref[...]
        go_ref[...] = g
        d_ref[...], nm_ref[...], nv_ref[...] = _adam_step(w_ref[...], g, m_ref[...], v_ref[...])

    blk = pl.BlockSpec((tr, cols), lambda i: (i, 0))
    return pl.pallas_call(
        body, name=name, grid=(rows // tr,), in_specs=[blk] * 4 + [ANY] * len(extra), out_specs=[blk] * 4,
        out_shape=[jax.ShapeDtypeStruct(w.shape, F32)] * 4,
        compiler_params=_params("parallel"),
    )(w, g, m, v, *extra)


def _adam_step(w, g, m, v):
    nm = ADAM_B1 * m + (1.0 - ADAM_B1) * g
    nv = ADAM_B2 * v + (1.0 - ADAM_B2) * jnp.square(g)
    m_hat = nm * (1.0 / (1.0 - ADAM_B1 ** ADAM_STEP))
    v_hat = nv * (1.0 / (1.0 - ADAM_B2 ** ADAM_STEP))
    return -ADAM_LR * (m_hat / (jnp.sqrt(v_hat) + ADAM_EPS) + ADAM_WD * w), nm, nv


def _adamw_small(tot, chip, weights, ms, vs, after):
    n, half = len(weights), D_MODEL // 2

    def body(chip_ref, tot_ref, *refs):
        ins, outs = refs[:3 * n], refs[3 * n + 1:]
        tot = tot_ref[...]
        conv_all = jnp.concatenate([tot[5:6, half:], tot[6:7, :half], tot[6:7, half:]], axis=0)
        conv = sum(jnp.where(chip_ref[0] == s, conv_all[:, s * LANES:(s + 1) * LANES], 0.0) for s in range(N_CHIPS))
        grads = [jnp.concatenate([tot[4:5, :half], tot[4:5, half:]], axis=0), tot[5:6, :half], conv,
                 tot[0:1], tot[1:2], tot[2:3], tot[3:4]]
        for k, g in enumerate(grads):
            delta, nm, nv = _adam_step(ins[k][...], g, ins[n + k][...], ins[2 * n + k][...])
            outs[k][...], outs[n + k][...], outs[2 * n + k][...], outs[3 * n + k][...] = g, delta, nm, nv
        outs[4 * n][...] = tot[7:8, 0:1]

    whole = lambda a: pl.BlockSpec(a.shape, lambda i, s: (0,) * a.ndim)
    arrays = (*weights, *ms, *vs)
    loss = jax.ShapeDtypeStruct((1, 1), F32)
    return pl.pallas_call(
        body, name="adamw_small",
        grid_spec=pltpu.PrefetchScalarGridSpec(
            num_scalar_prefetch=1, grid=(1,), in_specs=[whole(tot)] + [whole(a) for a in arrays] + [ANY],
            out_specs=[whole(a) for a in weights] * 4 + [whole(loss)]),
        out_shape=[jax.ShapeDtypeStruct(a.shape, F32) for a in weights] * 4 + [loss],
    )(chip, tot, *arrays, after)


def kernel(x, w_in, lb_logits, gate_norm_w, conv_w, w_out, ln1_g, ln1_b, w_ff1, w_ff2, ln2_g, ln2_b, loss_target, m_w_in, m_lb_logits, m_gate_norm_w, m_conv_w, m_w_out, m_ln1_g, m_ln1_b, m_w_ff1, m_w_ff2, m_ln2_g, m_ln2_b, v_w_in, v_lb_logits, v_gate_norm_w, v_conv_w, v_w_out, v_ln1_g, v_ln1_b, v_w_ff1, v_w_ff2, v_ln2_g, v_ln2_b):
    xs, tgt = x[0], loss_target[0]
    chip = 2 * lax.axis_index("x") + lax.axis_index("y")
    core = lax.axis_index("c").astype(jnp.int32).reshape(1)
    chip1 = chip.astype(jnp.int32).reshape(1)
    place = jnp.concatenate([chip1, core])

    conv4 = lax.dynamic_update_slice(jnp.zeros((N_CHIPS,) + conv_w.shape[1:], F32), conv_w, (chip, 0, 0))
    ici_in = _gather_w_in_over_ici(_place_shard("place_w_in", w_in[0], chip1, True), conv4)
    rest = (1, 2, 3)
    ici_rest = _gather_over_ici(rest, (_place_shard("place_w_out", w_out[0], chip1, False, after=ici_in.token),
                                       _place_shard("place_w_ff1", w_ff1[0], chip1, True, after=ici_in.token),
                                       _place_shard("place_w_ff2", w_ff2[0], chip1, False, after=ici_in.token)))
    wb_in, cv4 = ici_in.wait(ici_rest.token)
    d2d_in = _gather_over_d2d((0,), (wb_in,))
    wb_in, = d2d_in.wait(d2d_in.token)
    conv_full = cv4.transpose(1, 0, 2).reshape(3, CONV_WIDTH)

    proj, ibcu, xb, cat_c = _in_proj(xs, wb_in, conv_full, ici_rest.token)
    relay_rest = _relay_over_ici(rest, ici_rest.wait(proj))
    o, states = _hgrn_fwd(proj, ibcu, lb_logits, relay_rest.token)
    d2d_rest = _gather_over_d2d(rest, relay_rest.wait(o))
    cat_h = _gate_fwd(proj, o, gate_norm_w, d2d_rest.token)
    wb_out, wb_ff1, wb_ff2 = d2d_rest.wait(cat_h)

    (h1b, r, da, dpre2b, dpre1, dcat, g_ln1_g, g_ln1_b, g_ln2_g, g_ln2_b, loss8, g_out_local) = _sublayers(
        cat_h, cat_c, xs, tgt, wb_out, wb_ff1, wb_ff2, ln1_g, ln1_b, ln2_g, ln2_b)

    names = ("w_in", "w_out", "w_ff1", "w_ff2")

    def add_halves(kinds, grads, lands):
        return [_add_half("add_half_" + names[k], g, ld, core, COLS_SHARDED[k]) for k, g, ld in zip(kinds, grads, lands)]

    def sum_pieces(kinds, halves, lands, after):
        return [_sum_pieces("sum_pieces_" + names[k], h, ld, place, COLS_SHARDED[k], after)
                for k, h, ld in zip(kinds, halves, lands)]

    early = (1, 2, 3)
    g_ff2_local, dpc, g_conv = _dw_ff2(r, dpre2b, dcat, ibcu, conv_full)
    swap_a = _swap_halves((1, 3), (g_out_local, g_ff2_local))
    g_ff1_local, do, dog, g_gnw = _dw_ff1(h1b, da, dcat, o, proj, gate_norm_w, swap_a.token)
    swap_b = _swap_halves((2,), (g_ff1_local,))
    swapped_a = swap_a.wait(swap_b.token)
    halves_a = add_halves((1, 3), swapped_a[:2], swapped_a[2:])
    swapped_b = swap_b.wait(halves_a[1])
    halves = (halves_a[0], *add_halves((2,), swapped_b[:1], swapped_b[1:]), halves_a[1])
    exch = _exchange_pieces(early, halves)
    dph, g_lbl = _hgrn_bwd(proj, ibcu, do, states, lb_logits, exch.token)
    g_in_local, grad_x = _dw_in(xb, dph, dog, dpc, wb_in, dpre1, dph)

    late = (0,)
    swap = _swap_halves(late, (g_in_local,))
    exchanged = exch.wait(swap.token)
    pack = jnp.concatenate([
        g_ln1_g, g_ln1_b, g_ln2_g, g_ln2_b,
        jnp.concatenate([g_lbl[0:1], g_lbl[1:2]], axis=1),
        jnp.concatenate([g_gnw, g_conv[0:1]], axis=1),
        jnp.concatenate([g_conv[1:2], g_conv[2:3]], axis=1),
        jnp.concatenate([loss8[0:1], jnp.zeros((1, D_MODEL - LANES), F32)], axis=1)], axis=0)
    join_a = _join_halves((2,), sum_pieces((2,), exchanged[1:2], exchanged[4:5], swap.token))
    swapped = swap.wait(join_a.token)
    exch = _exchange_pieces(late, add_halves(late, swapped[:1], swapped[1:]), pack)
    join_b = _join_halves((1, 3), sum_pieces((1, 3), exchanged[0:3:2], exchanged[3:6:2], exch.token))
    g_w_ff1, = join_a.wait(join_b.token)
    g_w_ff1, d_ff1, nm_ff1, nv_ff1 = _adamw("adamw_w_ff1", w_ff1[0], g_w_ff1, m_w_ff1[0], v_w_ff1[0])
    g_w_out, g_w_ff2 = join_b.wait(d_ff1)
    g_w_ff2, d_ff2, nm_ff2, nv_ff2 = _adamw("adamw_w_ff2", w_ff2[0], g_w_ff2, m_w_ff2[0], v_w_ff2[0])
    g_w_out, d_out, nm_out, nv_out = _adamw("adamw_w_out", w_out[0], g_w_out, m_w_out[0], v_w_out[0], d_ff2)
    exchanged = exch.wait(d_out)
    tot = _sum_shared(exchanged[2], exchanged[3], 2 * chip1 + core)
    join = _join_halves(late, sum_pieces(late, exchanged[:1], exchanged[1:2], tot))
    small = ("lb_logits", "gate_norm_w", "conv_w", "ln1_g", "ln1_b", "ln2_g", "ln2_b")
    small_out = _adamw_small(
        tot, chip1, (lb_logits, gate_norm_w, conv_w[0], ln1_g, ln1_b, ln2_g, ln2_b),
        (m_lb_logits, m_gate_norm_w, m_conv_w[0], m_ln1_g, m_ln1_b, m_ln2_g, m_ln2_b),
        (v_lb_logits, v_gate_norm_w, v_conv_w[0], v_ln1_g, v_ln1_b, v_ln2_g, v_ln2_b), join.token)
    g_w_in, = join.wait(small_out[0])
    g_w_in, d_in, nm_in, nv_in = _adamw("adamw_w_in", w_in[0], g_w_in, m_w_in[0], v_w_in[0])
    loss = small_out[4 * len(small)][0, 0]

    def results(n_kind, large):
        out = dict(zip(small, small_out[n_kind * len(small):(n_kind + 1) * len(small)]))
        out["conv_w"] = out["conv_w"][None]
        out.update({name: a[None] for name, a in zip(("w_in", "w_out", "w_ff1", "w_ff2"), large)})
        return [out[name] for name in ("w_in", "lb_logits", "gate_norm_w", "conv_w", "w_out", "ln1_g", "ln1_b",
                                       "w_ff1", "w_ff2", "ln2_g", "ln2_b")]

    return (loss, grad_x[None], *results(0, (g_w_in, g_w_out, g_w_ff1, g_w_ff2)),
            *results(1, (d_in, d_out, d_ff1, d_ff2)), *results(2, (nm_in, nm_out, nm_ff1, nm_ff2)),
            *results(3, (nv_in, nv_out, nv_ff1, nv_ff2)))
```

```python
import jax
import jax.numpy as jnp
from jax import lax
from jax.experimental import pallas as pl
from jax.experimental.pallas import tpu as pltpu

F32 = jnp.float32
BF16 = jnp.bfloat16
MXU_DTYPE = jnp.bfloat16

D_MODEL = 1024
HGRN_WIDTH = 512
HEAD_DIM = 128
N_HEADS = 4
CONV_WIDTH = 512
CHUNK = 64
D_FF = 4096
IN_COLS = 3584
GROUP = 512
N_GROUPS = IN_COLS // GROUP
ALPHA = 2.0 ** 0.25
EPS = 1e-5
N_CHIPS = 4
ADAM_LR, ADAM_B1, ADAM_B2, ADAM_EPS, ADAM_WD, ADAM_STEP = 0.001, 0.9, 0.999, 1e-08, 0.01, 10

LANES = 128
SUBLANES = 8
VMEM_LIMIT = 56 * 1024 * 1024
FF_BLOCK = 1024
N_FF = D_FF // FF_BLOCK
GATE_STRIP = 64

NN = (((1,), (0,)), ((), ()))
NT = (((1,), (1,)), ((), ()))
TN = (((0,), (0,)), ((), ()))
MESH = pl.DeviceIdType.MESH
ANY = pl.BlockSpec(memory_space=pl.ANY)


def _dot(a, b, dims):
    return lax.dot_general(a.astype(MXU_DTYPE), b.astype(MXU_DTYPE), dims, preferred_element_type=F32)


def _dot_exact(ones, v):
    ones = ones.astype(jnp.bfloat16)
    hi = v.astype(jnp.bfloat16)
    rest = v - hi.astype(F32)
    mid = rest.astype(jnp.bfloat16)
    low = (rest - mid.astype(F32)).astype(jnp.bfloat16)
    return sum(lax.dot_general(ones, part, NN, preferred_element_type=F32) for part in (hi, mid, low))


def _params(*sem):
    return pltpu.CompilerParams(dimension_semantics=sem, vmem_limit_bytes=VMEM_LIMIT)


def _resident(shape):
    return pl.BlockSpec(shape, lambda *_: (0,) * len(shape), pipeline_mode=pl.Buffered(1))


def _sigmoid(v):
    return 1.0 / (1.0 + jnp.exp(-v))


def _lower_bound(lbl):
    m = jnp.max(lbl, axis=0, keepdims=True)
    e = jnp.exp(lbl - m)
    s = e / jnp.sum(e, axis=0, keepdims=True)
    return s[0:1, :], s[1:2, :]


def _heads(v):
    return [v[:, h * HEAD_DIM:(h + 1) * HEAD_DIM] for h in range(N_HEADS)]


def _per_head(fn, *arrays):
    return jnp.concatenate([fn(*parts) for parts in zip(*map(_heads, arrays))], axis=1)


def _in_proj(x, w_in, conv_w, after):
    t = x.shape[0]
    tm = min(t, 512)

    def body(x_ref, w_ref, cw_ref, after_ref, o_ref, bcu_ref, xb_ref, y_ref, zbuf):
        @pl.when(pl.program_id(0) == 0)
        def _():
            zbuf[tm:tm + SUBLANES, :] = jnp.zeros((SUBLANES, CONV_WIDTH), F32)

        xb = x_ref[...].astype(xb_ref.dtype)
        xb_ref[...] = xb
        group = lambda g: _dot(xb, w_ref[:, g * GROUP:(g + 1) * GROUP], NN)
        for g in range(4):
            o_ref[g] = group(g)
        b_gate, c_gate, u = group(4), group(5), group(6)
        for n, part in enumerate((b_gate, c_gate, u)):
            bcu_ref[n] = part.astype(bcu_ref.dtype)
        zbuf[0:SUBLANES, :] = zbuf[tm:tm + SUBLANES, :]
        zbuf[SUBLANES:SUBLANES + tm, :] = c_gate * u
        cw = cw_ref[...]
        at = lambda shift: zbuf[shift:shift + tm, :]
        conv = cw[2:3, :] * at(SUBLANES) + cw[1:2, :] * at(SUBLANES - 1) + cw[0:1, :] * at(SUBLANES - 2)
        y_ref[...] = (b_gate * conv).astype(y_ref.dtype)

    return pl.pallas_call(
        body, name="in_proj", grid=(t // tm,),
        in_specs=[pl.BlockSpec((tm, D_MODEL), lambda i: (i, 0)), _resident((D_MODEL, IN_COLS)),
                  pl.BlockSpec((3, CONV_WIDTH), lambda i: (0, 0)), ANY],
        out_specs=[pl.BlockSpec((4, tm, GROUP), lambda i: (0, i, 0)), pl.BlockSpec((3, tm, GROUP), lambda i: (0, i, 0)),
                   pl.BlockSpec((tm, D_MODEL), lambda i: (i, 0)), pl.BlockSpec((tm, CONV_WIDTH), lambda i: (i, 0))],
        out_shape=[jax.ShapeDtypeStruct((4, t, GROUP), F32), jax.ShapeDtypeStruct((3, t, GROUP), BF16),
                   jax.ShapeDtypeStruct((t, D_MODEL), BF16), jax.ShapeDtypeStruct((t, CONV_WIDTH), BF16)],
        scratch_shapes=[pltpu.VMEM((tm + SUBLANES, CONV_WIDTH), F32)],
        compiler_params=_params("arbitrary"),
    )(x, w_in, conv_w, after)


def _gates(fp, lb):
    sig = _sigmoid(fp)
    f = lb + (1.0 - lb) * sig
    return sig, f, jnp.log(f), 1.0 - f


def _chunk_masks():
    row = lax.broadcasted_iota(jnp.int32, (CHUNK, CHUNK), 0)
    col = lax.broadcasted_iota(jnp.int32, (CHUNK, CHUNK), 1)
    return row >= col, row <= col


def _hgrn_fwd(proj, lb_logits, after):
    t = proj.shape[1]
    tb = min(t, 512)
    ncb = tb // CHUNK

    def body(q_ref, f_ref, v_ref, lbl_ref, after_ref, o_ref, st_ref, s_scr):
        @pl.when(pl.program_id(0) == 0)
        def _():
            s_scr[...] = jnp.zeros_like(s_scr)

        lb, _ = _lower_bound(lbl_ref[...])
        causal, _ = _chunk_masks()

        every = range(ncb)
        rows = [slice(c * CHUNK, (c + 1) * CHUNK) for c in every]
        q, v = [q_ref[r, :] for r in rows], [v_ref[r, :] for r in rows]
        gates = [_gates(f_ref[r, :], lb) for r in rows]
        k = [gt[3] for gt in gates]
        b = [_dot_exact(causal, gt[2]) for gt in gates]
        mid, last = [x[CHUNK // 2:CHUNK // 2 + 1, :] for x in b], [x[CHUNK - 1:CHUNK, :] for x in b]
        qt = [q[c] * jnp.exp(b[c] - mid[c]) for c in every]
        kt = [k[c] * jnp.exp(mid[c] - b[c]) for c in every]
        qi = [q[c] * jnp.exp(b[c]) for c in every]
        ks = [k[c] * jnp.exp(last[c] - b[c]) for c in every]
        dec = [jnp.exp(x) for x in last]
        scores = [[jnp.where(causal, _dot(a, b_, NT), 0.0) for a, b_ in zip(_heads(qt[c]), _heads(kt[c]))] for c in every]
        intra = [[_dot(s, v_h, NN) for s, v_h in zip(scores[c], _heads(v[c]))] for c in every]
        update = [_per_head(lambda v_h, ks_h: _dot(v_h, ks_h, TN), v[c], ks[c]) for c in every]

        st = s_scr[...]
        states = []
        for c in every:
            states.append(st)
            st_ref[c] = st
            st = dec[c] * st + update[c]
        s_scr[...] = st

        o_ref[...] = jnp.concatenate(
            [jnp.concatenate([i_h + _dot(qi_h, st_h, NT) for i_h, qi_h, st_h in
                              zip(intra[c], _heads(qi[c]), _heads(states[c]))], axis=1) for c in every], axis=0)

    grp = lambda g: pl.BlockSpec((None, tb, GROUP), lambda i: (g, i, 0))
    return pl.pallas_call(
        body, name="hgrn_fwd", grid=(t // tb,),
        in_specs=[grp(0), grp(1), grp(2), pl.BlockSpec((2, HGRN_WIDTH), lambda i: (0, 0)), ANY],
        out_specs=[pl.BlockSpec((tb, HGRN_WIDTH), lambda i: (i, 0)),
                   pl.BlockSpec((ncb, HEAD_DIM, HGRN_WIDTH), lambda i: (i, 0, 0))],
        out_shape=[jax.ShapeDtypeStruct((t, HGRN_WIDTH), F32),
                   jax.ShapeDtypeStruct((t // CHUNK, HEAD_DIM, HGRN_WIDTH), F32)],
        scratch_shapes=[pltpu.VMEM((HEAD_DIM, HGRN_WIDTH), F32)],
        compiler_params=_params("arbitrary"),
    )(proj, proj, proj, lb_logits, after)


def _gate_fwd(proj, o, gate_norm_w, after):
    t = proj.shape[1]
    tb = min(t, 1024)

    def body(o_ref, og_ref, gnw_ref, after_ref, out_ref):
        gnw = gnw_ref[...]
        for s in range(tb // GATE_STRIP):
            rows = slice(s * GATE_STRIP, (s + 1) * GATE_STRIP)
            og = og_ref[rows, :]
            on = _per_head(lambda o_h: o_h * lax.rsqrt(jnp.mean(o_h * o_h, axis=-1, keepdims=True) + EPS), o_ref[rows, :])
            out_ref[rows, :] = (on * gnw * (og * _sigmoid(og))).astype(out_ref.dtype)

    tile = pl.BlockSpec((tb, GROUP), lambda i: (i, 0))
    return pl.pallas_call(
        body, name="gate_fwd", grid=(t // tb,),
        in_specs=[tile, pl.BlockSpec((None, tb, GROUP), lambda i: (3, i, 0)), pl.BlockSpec((1, GROUP), lambda i: (0, 0)), ANY],
        out_specs=tile,
        out_shape=jax.ShapeDtypeStruct((t, HGRN_WIDTH), BF16),
        compiler_params=_params("parallel"),
    )(o, proj, gate_norm_w, after)


def _ln_bwd(dy, xhat, rstd, g):
    dxhat = dy * g
    m1 = jnp.mean(dxhat, axis=-1, keepdims=True)
    m2 = jnp.mean(dxhat * xhat, axis=-1, keepdims=True)
    return rstd * (dxhat - m1 - xhat * m2)


def _layer_norm(pre):
    xc = pre - jnp.mean(pre, axis=-1, keepdims=True)
    rstd = lax.rsqrt(jnp.mean(xc * xc, axis=-1, keepdims=True) + EPS)
    return xc * rstd, rstd


def _sublayers(cat_h, cat_c, x, target, w_out, w_ff1, w_ff2, g1, b1, g2, b2):
    t = x.shape[0]
    tm = min(t, 256)

    def body(ch_ref, cc_ref, x_ref, tg_ref, wo_ref, w1_ref, w2_ref, g1_ref, b1_ref, g2_ref, b2_ref,
             h1_ref, r_ref, da_ref, dp2b_ref, dp1_ref, dcat_ref, dg1_ref, db1_ref, dg2_ref, db2_ref, loss_ref, gwo_ref,
             gwo_acc, gwo_narrow, sem):
        @pl.when(pl.program_id(0) == 0)
        def _():
            for ref in (dg1_ref, db1_ref, dg2_ref, db2_ref, loss_ref, gwo_acc):
                ref[...] = jnp.zeros_like(ref)

        mix = _dot(ch_ref[...], wo_ref[0:GROUP, :], NN) + _dot(cc_ref[...], wo_ref[GROUP:2 * GROUP, :], NN)
        xhat1, rstd1 = _layer_norm(ALPHA * x_ref[...] + mix)
        h1 = xhat1 * g1_ref[...] + b1_ref[...]
        h1b = h1.astype(h1_ref.dtype)
        h1_ref[...] = h1b
        mlp = jnp.zeros((tm, D_MODEL), F32)
        for j in range(N_FF):
            cols = slice(j * FF_BLOCK, (j + 1) * FF_BLOCK)
            r = jnp.square(jnp.maximum(_dot(h1b, w1_ref[:, cols], NN), 0.0)).astype(r_ref.dtype)
            r_ref[:, cols] = r
            mlp = mlp + _dot(r, w2_ref[cols, :], NN)
        xhat2, rstd2 = _layer_norm(ALPHA * h1 + mlp)
        err = xhat2 * g2_ref[...] + b2_ref[...] - tg_ref[...]
        loss_ref[...] += 0.5 * jnp.sum(jnp.mean(err * err, axis=-1, keepdims=True))
        dy = err * (1.0 / D_MODEL)
        dg2_ref[...] += jnp.sum(dy * xhat2, axis=0, keepdims=True)
        db2_ref[...] += jnp.sum(dy, axis=0, keepdims=True)
        dp2 = _ln_bwd(dy, xhat2, rstd2, g2_ref[...])
        dp2b = dp2.astype(dp2b_ref.dtype)
        dp2b_ref[...] = dp2b
        back = jnp.zeros((tm, D_MODEL), F32)
        for j in range(N_FF):
            cols = slice(j * FF_BLOCK, (j + 1) * FF_BLOCK)
            dr = _dot(dp2b, w2_ref[cols, :], NT)
            da = (dr * (2.0 * jnp.sqrt(r_ref[:, cols].astype(F32)))).astype(da_ref.dtype)
            da_ref[:, cols] = da
            back = back + _dot(da, w1_ref[:, cols], NT)
        dh1 = ALPHA * dp2 + back
        dg1_ref[...] += jnp.sum(dh1 * xhat1, axis=0, keepdims=True)
        db1_ref[...] += jnp.sum(dh1, axis=0, keepdims=True)
        dp1 = _ln_bwd(dh1, xhat1, rstd1, g1_ref[...])
        dp1b = dp1.astype(MXU_DTYPE)
        dp1_ref[...] = dp1
        dcat_ref[...] = _dot(dp1b, wo_ref[...], NT)
        gwo_acc[0:GROUP, :] += _dot(ch_ref[...], dp1b, TN)
        gwo_acc[GROUP:2 * GROUP, :] += _dot(cc_ref[...], dp1b, TN)

        @pl.when(pl.program_id(0) == pl.num_programs(0) - 1)
        def _():
            gwo_narrow[...] = gwo_acc[...].astype(gwo_narrow.dtype)
            copy = pltpu.make_async_copy(gwo_narrow, gwo_ref, sem.at[0])
            copy.start()
            copy.wait()

    row = pl.BlockSpec((tm, D_MODEL), lambda i: (i, 0))
    wide = pl.BlockSpec((tm, D_FF), lambda i: (i, 0))
    vec = pl.BlockSpec((1, D_MODEL), lambda i: (0, 0))
    narrow = lambda dtype: jax.ShapeDtypeStruct((t, D_MODEL), dtype)
    return pl.pallas_call(
        body, name="sublayers", grid=(t // tm,),
        in_specs=[pl.BlockSpec((tm, GROUP), lambda i: (i, 0)), pl.BlockSpec((tm, GROUP), lambda i: (i, 0)), row, row,
                  _resident((D_MODEL, D_MODEL)),
                  _resident((D_MODEL, D_FF)), _resident((D_FF, D_MODEL)), vec, vec, vec, vec],
        out_specs=[row, wide, wide, row, row, row, vec, vec, vec, vec,
                   pl.BlockSpec((SUBLANES, LANES), lambda i: (0, 0)), ANY],
        out_shape=[narrow(BF16), jax.ShapeDtypeStruct((t, D_FF), BF16), jax.ShapeDtypeStruct((t, D_FF), BF16),
                   narrow(BF16), narrow(F32), narrow(F32)]
                  + [jax.ShapeDtypeStruct((1, D_MODEL), F32)] * 4
                  + [jax.ShapeDtypeStruct((SUBLANES, LANES), F32), jax.ShapeDtypeStruct((D_MODEL, D_MODEL), BF16)],
        scratch_shapes=[pltpu.VMEM((D_MODEL, D_MODEL), F32), pltpu.VMEM((D_MODEL, D_MODEL), BF16),
                        pltpu.SemaphoreType.DMA((1,))],
        compiler_params=_params("arbitrary"),
    )(cat_h, cat_c, x, target, w_out, w_ff1, w_ff2, g1, b1, g2, b2)


def _hgrn_bwd(proj, do, states, lb_logits, after):
    t = proj.shape[1]
    tb = min(t, 512)
    ncb = tb // CHUNK
    nblk = t // tb

    def body(q_ref, f_ref, v_ref, do_ref, st_ref, lbl_ref, after_ref, dp_ref, dlbl_ref, ds_scr, dlb_scr):
        i = pl.program_id(0)

        @pl.when(i == 0)
        def _():
            ds_scr[...] = jnp.zeros_like(ds_scr)
            dlb_scr[...] = jnp.zeros_like(dlb_scr)

        lb, s1 = _lower_bound(lbl_ref[...])
        causal, anti = _chunk_masks()
        every = range(ncb)
        rows = [slice(c * CHUNK, (c + 1) * CHUNK) for c in every]
        q, v, do = ([ref[r, :] for r in rows] for ref in (q_ref, v_ref, do_ref))
        st = [st_ref[c] for c in every]
        gates = [_gates(f_ref[r, :], lb) for r in rows]
        sig, f, k = ([gt[n] for gt in gates] for n in (0, 1, 3))
        b = [_dot_exact(causal, gt[2]) for gt in gates]
        mid, last = [x[CHUNK // 2:CHUNK // 2 + 1, :] for x in b], [x[CHUNK - 1:CHUNK, :] for x in b]
        e_q = [jnp.exp(b[c] - mid[c]) for c in every]
        e_k = [jnp.exp(mid[c] - b[c]) for c in every]
        e_i = [jnp.exp(x) for x in b]
        e_s = [jnp.exp(last[c] - b[c]) for c in every]
        dec = [jnp.exp(x) for x in last]
        qt, kt, qi, ks = ([a[c] * e[c] for c in every] for a, e in ((q, e_q), (k, e_k), (q, e_i), (k, e_s)))

        def masked(a, b_):
            return [[jnp.where(causal, _dot(a_h, b_h, NT), 0.0) for a_h, b_h in zip(_heads(a[c]), _heads(b_[c]))]
                    for c in every]

        def with_scores(s, other, dims):
            return [jnp.concatenate([_dot(s_h, o_h, dims) for s_h, o_h in zip(s[c], _heads(other[c]))], axis=1)
                    for c in every]

        def per_head(dims, a, b_):
            return [_per_head(lambda a_h, b_h: _dot(a_h, b_h, dims), a[c], b_[c]) for c in every]

        scores, dscores = masked(qt, kt), masked(do, v)
        dqt, dkt, dv_intra = with_scores(dscores, kt, NN), with_scores(dscores, qt, TN), with_scores(scores, do, TN)
        dqi, update = per_head(NN, do, st), per_head(TN, do, qi)

        dst = ds_scr[...]
        dsts = [None] * ncb
        for c in reversed(every):
            dsts[c] = dst
            dst = dec[c] * dst + update[c]
        ds_scr[...] = dst

        dv_state, dks = per_head(NT, ks, dsts), per_head(NN, v, dsts)
        ddec = [jnp.sum(dsts[c] * st[c], axis=0, keepdims=True) for c in every]
        dq = [dqt[c] * e_q[c] + dqi[c] * e_i[c] for c in every]
        dk = [dkt[c] * e_k[c] + dks[c] * e_s[c] for c in every]
        db = [q[c] * dq[c] - k[c] * dk[c] for c in every]
        db_last = [jnp.sum(dks[c] * ks[c], axis=0, keepdims=True) + ddec[c] * dec[c] for c in every]
        dg = [_dot_exact(anti, db[c]) + db_last[c] for c in every]
        df = [dg[c] / f[c] - dk[c] for c in every]
        dlb_scr[...] += sum(jnp.sum(df[c] * (1.0 - sig[c]), axis=0, keepdims=True) for c in every)
        dfp = [df[c] * (1.0 - lb) * sig[c] * (1.0 - sig[c]) for c in every]
        dv = [dv_intra[c] + dv_state[c] for c in every]
        for n, parts in enumerate((dq, dfp, dv)):
            dp_ref[n] = jnp.concatenate(parts, axis=0).astype(dp_ref.dtype)

        @pl.when(i == nblk - 1)
        def _():
            dlb = dlb_scr[...]
            dlbl_ref[0:1, :] = dlb * lb * (1.0 - lb)
            dlbl_ref[1:2, :] = -dlb * lb * s1

    grp = lambda g: pl.BlockSpec((None, tb, GROUP), lambda i: (g, nblk - 1 - i, 0))
    vec = pl.BlockSpec((2, HGRN_WIDTH), lambda i: (0, 0))
    return pl.pallas_call(
        body, name="hgrn_bwd", grid=(nblk,),
        in_specs=[grp(0), grp(1), grp(2), pl.BlockSpec((tb, HGRN_WIDTH), lambda i: (nblk - 1 - i, 0)),
                  pl.BlockSpec((ncb, HEAD_DIM, HGRN_WIDTH), lambda i: (nblk - 1 - i, 0, 0)), vec, ANY],
        out_specs=[pl.BlockSpec((3, tb, HGRN_WIDTH), lambda i: (0, nblk - 1 - i, 0)), vec],
        out_shape=[jax.ShapeDtypeStruct((3, t, HGRN_WIDTH), BF16), jax.ShapeDtypeStruct((2, HGRN_WIDTH), F32)],
        scratch_shapes=[pltpu.VMEM((HEAD_DIM, HGRN_WIDTH), F32), pltpu.VMEM((1, HGRN_WIDTH), F32)],
        compiler_params=_params("arbitrary"),
    )(proj, proj, proj, do, states, lb_logits, after)


GRAD_TILE = 512
OUT_PARTS = 4


class _Side:
    def __init__(self, operands, in_specs, out_shape, out_specs, scratch, init, begin):
        self.operands, self.in_specs, self.out_shape, self.out_specs = operands, in_specs, out_shape, out_specs
        self.scratch, self.init, self.begin = scratch, init, begin


def _grad_w(name, operands, widths, shape, step, after=None, side=None):
    t = operands[0].shape[-2]
    tt = min(t, GRAD_TILE)
    n_in, n_steps = len(operands), t // tt
    in_specs = [pl.BlockSpec((tt, w), lambda k: (k, 0)) if a.ndim == 2 else
                pl.BlockSpec((a.shape[0], tt, w), lambda k: (0, k, 0)) for a, w in zip(operands, widths)]
    extra = [] if after is None else [after]
    s_in, s_out = (len(side.operands), len(side.out_shape)) if side else (0, 0)
    first_out = n_in + s_in + len(extra)

    def body(*refs):
        o_ref, side_outs = refs[first_out], refs[first_out + 1:first_out + 1 + s_out]
        acc, narrow, sem = refs[first_out + 1 + s_out:first_out + 4 + s_out]
        k = pl.program_id(0)

        @pl.when(k == 0)
        def _():
            acc[...] = jnp.zeros_like(acc)
            if side:
                side.init(side_outs)

        tick = side.begin(k, n_steps, refs[n_in:n_in + s_in], side_outs, refs[first_out + 4 + s_out:]) if side else None
        step(acc, *refs[:n_in], tick or (lambda j: None))

        @pl.when(k == n_steps - 1)
        def _():
            part = shape[0] // OUT_PARTS
            copies = []
            for p in range(OUT_PARTS):
                rows = pl.ds(p * part, part)
                narrow[rows, :] = acc[rows, :].astype(narrow.dtype)
                copies.append(pltpu.make_async_copy(narrow.at[rows, :], o_ref.at[rows, :], sem.at[p]))
                copies[-1].start()
            for cp in copies:
                cp.wait()

    outs = pl.pallas_call(
        body, name=name, grid=(n_steps,),
        in_specs=in_specs + (side.in_specs if side else []) + [ANY] * len(extra),
        out_specs=[ANY] + (side.out_specs if side else []),
        out_shape=[jax.ShapeDtypeStruct(shape, BF16)] + (side.out_shape if side else []),
        scratch_shapes=[pltpu.VMEM(shape, F32), pltpu.VMEM(shape, BF16), pltpu.SemaphoreType.DMA((OUT_PARTS,))]
                       + (side.scratch if side else []),
        compiler_params=_params("arbitrary"),
    )(*operands, *(side.operands if side else ()), *extra)
    return outs if side else outs[0]


def _dw_in(xb, dph, dog, dpc, w_in, dpre1, after):
    t = xb.shape[0]

    def step(acc, x_ref, dh_ref, dog_ref, dc_ref, tick):
        xv = x_ref[...]
        for g in range(N_GROUPS):
            part = dh_ref[g] if g < 3 else dog_ref[...] if g == 3 else dc_ref[g - 4]
            acc[:, g * GROUP:(g + 1) * GROUP] += _dot(xv, part, TN)
            tick(g, part)

    def begin(k, n_steps, ins, outs, scratch):
        w_ref, dp_ref = ins
        total = [ALPHA * dp_ref[...]]

        def tick(g, part):
            total[0] = total[0] + _dot(part, w_ref[:, g * GROUP:(g + 1) * GROUP], NT)
            if g == N_GROUPS - 1:
                outs[0][...] = total[0]

        return tick

    row = pl.BlockSpec((min(t, GRAD_TILE), D_MODEL), lambda k: (k, 0))
    side = _Side((w_in, dpre1), [_resident((D_MODEL, IN_COLS)), row], [jax.ShapeDtypeStruct((t, D_MODEL), F32)], [row],
                 [], lambda outs: None, begin)
    return _grad_w("dw_in", (xb, dph, dog, dpc), (D_MODEL, GROUP, GROUP, GROUP), (D_MODEL, IN_COLS), step, after, side)


def _strips_of(j, tt):
    per_tick = tt // GATE_STRIP // N_FF
    return [slice(s * GATE_STRIP, (s + 1) * GATE_STRIP) for s in range(j * per_tick, (j + 1) * per_tick)]


def _dw_ff1(h1b, da, dcat, o, proj, gate_norm_w, after):
    t = h1b.shape[0]
    tt = min(t, GRAD_TILE)

    def step(acc, h_ref, da_ref, tick):
        hv = h_ref[...]
        for j in range(N_FF):
            cols = slice(j * FF_BLOCK, (j + 1) * FF_BLOCK)
            acc[:, cols] += _dot(hv, da_ref[:, cols], TN)
            tick(j)

    def init(outs):
        outs[2][...] = jnp.zeros_like(outs[2])

    def begin(k, n_steps, ins, outs, scratch):
        do2_ref, o_ref, og_ref, gnw_ref = ins
        do_ref, dog_ref, dgnw_ref = outs
        total = [jnp.zeros((GATE_STRIP, GROUP), F32)]

        def tick(j):
            gnw = gnw_ref[...]
            for rows in _strips_of(j, tt):
                ov, og, do2 = o_ref[rows, :], og_ref[rows, :], do2_ref[rows, :]
                rs = _per_head(lambda o_h: jnp.broadcast_to(
                    lax.rsqrt(jnp.mean(o_h * o_h, axis=-1, keepdims=True) + EPS), o_h.shape), ov)
                on = ov * rs
                sg = _sigmoid(og)
                sil = og * sg
                don = do2 * gnw * sil
                total[0] = total[0] + do2 * on * sil
                dog_ref[rows, :] = (do2 * on * gnw * (sg * (1.0 + og * (1.0 - sg)))).astype(dog_ref.dtype)
                do_ref[rows, :] = rs * (don - on * _per_head(
                    lambda p_h: jnp.broadcast_to(jnp.mean(p_h, axis=-1, keepdims=True), p_h.shape), don * on))
            if j == N_FF - 1:
                dgnw_ref[...] += jnp.sum(total[0], axis=0, keepdims=True)

        return tick

    tile = pl.BlockSpec((tt, GROUP), lambda k: (k, 0))
    vec = pl.BlockSpec((1, GROUP), lambda k: (0, 0))
    side = _Side(
        (dcat, o, proj, gate_norm_w), [tile, tile, pl.BlockSpec((None, tt, GROUP), lambda k: (3, k, 0)), vec],
        [jax.ShapeDtypeStruct((t, HGRN_WIDTH), F32), jax.ShapeDtypeStruct((t, HGRN_WIDTH), BF16),
         jax.ShapeDtypeStruct((1, HGRN_WIDTH), F32)], [tile, tile, vec], [], init, begin)
    return _grad_w("dw_ff1", (h1b, da), (D_MODEL, D_FF), (D_MODEL, D_FF), step, after, side)


def _dw_ff2(r, dpre2b, dcat, bcu, conv_w):
    t = r.shape[0]
    tt = min(t, GRAD_TILE)
    hb = tt // SUBLANES
    halo = 2 * SUBLANES

    def step(acc, r_ref, d_ref, tick):
        dv = d_ref[...]
        for j in range(N_FF):
            rows = slice(j * FF_BLOCK, (j + 1) * FF_BLOCK)
            acc[rows, :] += _dot(r_ref[:, rows], dv, TN)
            tick(j)

    def init(outs):
        outs[1][...] = jnp.zeros_like(outs[1])

    def begin(k, n_steps, ins, outs, scratch):
        dy_ref, dyn_ref, b_ref, bn_ref, c_ref, u_ref, ch_ref, uh_ref, cw_ref = ins
        dp_ref, dcw_ref = outs
        zbuf, dbuf = scratch
        before = lambda ref: ref[SUBLANES:halo, :].astype(F32)
        zbuf[0:SUBLANES, :] = jnp.where(k > 0, before(ch_ref) * before(uh_ref), 0.0)
        zbuf[SUBLANES:SUBLANES + tt, :] = c_ref[...].astype(F32) * u_ref[...].astype(F32)
        dbuf[0:tt, :] = dy_ref[...] * b_ref[...].astype(F32)
        dbuf[tt:tt + SUBLANES, :] = jnp.where(k < n_steps - 1, dyn_ref[...] * bn_ref[0:SUBLANES, :].astype(F32), 0.0)
        totals = [jnp.zeros((GATE_STRIP, GROUP), F32) for _ in range(3)]

        def tick(j):
            cw = cw_ref[...]
            for rows in _strips_of(j, tt):
                at = lambda buf, shift: buf[shift + rows.start:shift + rows.stop, :]
                z, z1, z2 = at(zbuf, SUBLANES), at(zbuf, SUBLANES - 1), at(zbuf, SUBLANES - 2)
                dyc, d1, d2 = at(dbuf, 0), at(dbuf, 1), at(dbuf, 2)
                yc = cw[2:3, :] * z + cw[1:2, :] * z1 + cw[0:1, :] * z2
                dz = cw[2:3, :] * dyc + cw[1:2, :] * d1 + cw[0:1, :] * d2
                dp_ref[0, rows, :] = (dy_ref[rows, :] * yc).astype(dp_ref.dtype)
                dp_ref[1, rows, :] = (dz * u_ref[rows, :].astype(F32)).astype(dp_ref.dtype)
                dp_ref[2, rows, :] = (dz * c_ref[rows, :].astype(F32)).astype(dp_ref.dtype)
                for n, tap in enumerate((z2, z1, z)):
                    totals[n] = totals[n] + dyc * tap
            if j == N_FF - 1:
                for n in range(3):
                    dcw_ref[n:n + 1, :] += jnp.sum(totals[n], axis=0, keepdims=True)

        return tick

    grp = lambda g: pl.BlockSpec((None, tt, GROUP), lambda k: (g, k, 0))
    prev = lambda g: pl.BlockSpec((None, halo, GROUP), lambda k: (g, jnp.maximum(k * (tt // halo) - 1, 0), 0))
    nxt = lambda g: pl.BlockSpec((None, halo, GROUP), lambda k: (g, jnp.minimum((k + 1) * (tt // halo), t // halo - 1), 0))
    nxt_row = lambda k: jnp.minimum((k + 1) * hb, t // SUBLANES - 1)
    whole = pl.BlockSpec((3, CONV_WIDTH), lambda k: (0, 0))
    side = _Side(
        (dcat, dcat, bcu, bcu, bcu, bcu, bcu, bcu, conv_w),
        [pl.BlockSpec((tt, GROUP), lambda k: (k, 1)), pl.BlockSpec((SUBLANES, GROUP), lambda k: (nxt_row(k), 1)),
         grp(0), nxt(0), grp(1), grp(2), prev(1), prev(2), whole],
        [jax.ShapeDtypeStruct((3, t, CONV_WIDTH), BF16), jax.ShapeDtypeStruct((3, CONV_WIDTH), F32)],
        [pl.BlockSpec((3, tt, GROUP), lambda k: (0, k, 0)), whole],
        [pltpu.VMEM((tt + SUBLANES, GROUP), F32), pltpu.VMEM((tt + SUBLANES, GROUP), F32)], init, begin)
    return _grad_w("dw_ff2", (r, dpre2b), (D_FF, D_MODEL), (D_FF, D_MODEL), step, side=side)


def _place():
    x, y, c = lax.axis_index("x"), lax.axis_index("y"), lax.axis_index("c")
    return x, y, c, 2 * x + y


def _other_chips(x, y):
    return [(1 - x, y), (x, 1 - y), (1 - x, 1 - y)]


def _place_shard(name, w, chip, cols_sharded, after=None):
    rows, cols = w.shape
    tr = min(rows, 256)
    nb = rows // tr
    full = (rows, cols * N_CHIPS) if cols_sharded else (rows * N_CHIPS, cols)
    out_map = (lambda i, s: (i, s[0])) if cols_sharded else (lambda i, s: (s[0] * nb + i, 0))

    def body(s_ref, w_ref, *rest):
        rest[-1][...] = w_ref[...].astype(rest[-1].dtype)

    extra = [] if after is None else [after]
    return pl.pallas_call(
        body, name=name,
        grid_spec=pltpu.PrefetchScalarGridSpec(
            num_scalar_prefetch=1, grid=(nb,),
            in_specs=[pl.BlockSpec((tr, cols), lambda i, s: (i, 0))] + [ANY] * len(extra),
            out_specs=pl.BlockSpec((tr, cols), out_map)),
        out_shape=jax.ShapeDtypeStruct(full, BF16),
        compiler_params=_params("parallel"),
    )(chip, w, *extra)


HBM = pl.BlockSpec(memory_space=pltpu.HBM)
SEM = pl.BlockSpec(memory_space=pltpu.SEMAPHORE)
EFFECT = pltpu.SideEffectType.DATAFLOW_SIDE_EFFECTING


PEER_SETS = {
    "sibling": (0, lambda x, y, c: [(x, y, 1 - c)]),
    "chips": (1, lambda x, y, c: [(1 - x, y, c), (x, 1 - y, c), (1 - x, 1 - y, c)]),
    "neighbours": (2, lambda x, y, c: [(1 - x, y, c), (x, 1 - y, c)]),
}


class _Split:
    def __init__(self, name, arrays, plan, others=(), peers=None):
        n_own, arrays = len(arrays), (*arrays, *others)
        n, n_copies = len(arrays), plan.count
        self.name, self.plan, self.n = name, plan, n_own
        barrier_id, peer_ids = PEER_SETS[peers] if peers else (None, None)

        def body(*refs):
            if peers:
                x, y, c, _ = _place()
                barrier = pltpu.get_barrier_semaphore()
                for peer in peer_ids(x, y, c):
                    pl.semaphore_signal(barrier, inc=1, device_id=peer, device_id_type=MESH)
                pl.semaphore_wait(barrier, len(peer_ids(0, 0, 0)))
            send_sems, recv_sems, token = refs[n], refs[n + 1], refs[-1]
            for k, (src, dst, to) in enumerate(plan(refs[:n])):
                pltpu.make_async_remote_copy(src_ref=src, dst_ref=dst, send_sem=send_sems.at[k], recv_sem=recv_sems.at[k],
                                             device_id=to, device_id_type=MESH).start()
            token[...] = jnp.zeros_like(token)

        outs = pl.pallas_call(
            body, name=name + "_start",
            out_shape=(pltpu.SemaphoreType.DMA((n_copies,)), pltpu.SemaphoreType.DMA((n_copies,)),
                       *[pltpu.HBM(a.shape, a.dtype) for a in arrays], jax.ShapeDtypeStruct((SUBLANES, LANES), F32)),
            in_specs=(HBM,) * n, out_specs=(SEM, SEM) + (HBM,) * n + (pl.BlockSpec(memory_space=pltpu.VMEM),),
            input_output_aliases={i: 2 + i for i in range(n)},
            compiler_params=pltpu.CompilerParams(has_side_effects=EFFECT, collective_id=barrier_id),
        )(*[pltpu.with_memory_space_constraint(a, pltpu.HBM) for a in arrays])
        self.sems, self.arrays, self.others, self.token = outs[:2], outs[2:2 + n_own], outs[2 + n_own:2 + n], outs[-1]

    def wait(self, after):
        n, plan = self.n, self.plan

        def body(*refs):
            send_sems, recv_sems = refs[n], refs[n + 1]
            for k, (src, dst, to) in enumerate(plan(refs[:n])):
                cp = pltpu.make_async_remote_copy(src_ref=src, dst_ref=dst, send_sem=send_sems.at[k],
                                                  recv_sem=recv_sems.at[k], device_id=to, device_id_type=MESH)
                cp.wait_send()
                cp.wait_recv()

        return pl.pallas_call(
            body, name=self.name + "_wait", out_shape=tuple(pltpu.HBM(a.shape, a.dtype) for a in self.arrays),
            in_specs=(HBM,) * n + (SEM, SEM, ANY), out_specs=(HBM,) * n, input_output_aliases={i: i for i in range(n)},
            compiler_params=pltpu.CompilerParams(has_side_effects=EFFECT),
        )(*self.arrays, *self.sems, after)


COLS_SHARDED = (True, False, True, False)
HALF_SHAPES = [(D_MODEL // 2, IN_COLS), (D_MODEL, D_MODEL // 2), (D_MODEL // 2, D_FF), (D_FF, D_MODEL // 2)]
PIECE_SHAPES = [(D_MODEL // 2, IN_COLS // N_CHIPS), (D_MODEL // N_CHIPS, D_MODEL // 2),
                (D_MODEL // 2, D_FF // N_CHIPS), (D_FF // N_CHIPS, D_MODEL // 2)]


def _shard_view(kind, ref, chip):
    if COLS_SHARDED[kind]:
        n = ref.shape[1] // N_CHIPS
        return ref.at[:, pl.ds(chip * n, n)]
    n = ref.shape[0] // N_CHIPS
    return ref.at[pl.ds(chip * n, n), :]


def _half_view(kind, ref, h):
    if COLS_SHARDED[kind]:
        n = ref.shape[0] // 2
        return ref.at[pl.ds(h * n, n), :]
    n = ref.shape[1] // 2
    return ref.at[:, pl.ds(h * n, n)]


def _plan(count):
    def mark(fn):
        fn.count = count
        return fn
    return mark


def _shard_rows_view(kind, ref, chip, part, n_parts):
    if COLS_SHARDED[kind]:
        m, n = ref.shape[0] // n_parts, ref.shape[1] // N_CHIPS
        return ref.at[pl.ds(part * m, m), pl.ds(chip * n, n)]
    m = ref.shape[0] // N_CHIPS // n_parts
    return ref.at[pl.ds((n_parts * chip + part) * m, m), :]


def _shard_half_view(kind, ref, chip, h):
    return _shard_rows_view(kind, ref, chip, h, 2)


def _gather_over_ici(kinds, weights):
    @_plan(2 * len(kinds))
    def plan(refs):
        x, y, c, me = _place()
        mine = [_shard_half_view(kind, ref, me, c) for kind, ref in zip(kinds, refs)]
        return [(v, v, to) for v in mine for to in ((1 - x, y, c), (x, 1 - y, c))]

    return _Split("gather_ici_" + "".join(map(str, kinds)), tuple(weights), plan, peers="neighbours")


def _relay_over_ici(kinds, weights, others=()):
    @_plan(2 * len(kinds))
    def plan(refs):
        x, y, c, _ = _place()
        x_nbr, y_nbr = 2 * (1 - x) + y, 2 * x + (1 - y)
        out = []
        for kind, ref in zip(kinds, refs):
            first, second = (_shard_rows_view(kind, ref, chip, 2 * c + q, 4) for q, chip in ((0, x_nbr), (1, y_nbr)))
            out += [(first, first, (x, 1 - y, c)), (second, second, (1 - x, y, c))]
        return out

    return _Split("relay_ici_" + "".join(map(str, kinds)), tuple(weights), plan, others, peers="neighbours")


def _gather_w_in_over_ici(w_in, conv4):
    @_plan(6)
    def plan(refs):
        x, y, c, me = _place()
        half, conv = _shard_half_view(0, refs[0], me, c), refs[1].at[me]
        return [(v, v, (px, py, c)) for v in (half, conv) for px, py in _other_chips(x, y)]

    return _Split("gather_w_in_ici", (w_in, conv4), plan, peers="chips")


def _gather_over_d2d(kinds, weights):
    @_plan(3 * len(kinds))
    def plan(refs):
        x, y, c, _ = _place()
        got = [_shard_half_view(kind, ref, 2 * px + py, c) for kind, ref in zip(kinds, refs)
               for px, py in _other_chips(x, y)]
        return [(v, v, (x, y, 1 - c)) for v in got]

    return _Split("gather_d2d_" + "".join(map(str, kinds)), tuple(weights), plan, peers="sibling")


def _swap_halves(kinds, grads):
    @_plan(len(kinds))
    def plan(refs):
        x, y, c, _ = _place()
        return [(_half_view(kind, g, 1 - c), land, (x, y, 1 - c))
                for kind, g, land in zip(kinds, refs[:len(kinds)], refs[len(kinds):])]

    lands = [lax.empty(HALF_SHAPES[kind], g.dtype) for kind, g in zip(kinds, grads)]
    return _Split("swap_halves_" + "".join(map(str, kinds)), (*grads, *lands), plan, peers="sibling")


def _block_rows(cols, elements):
    return 1 << ((elements // cols).bit_length() - 1)


def _grid_steps(shapes, elements):
    rows, cols = max(shapes, key=lambda shape: shape[0] * shape[1])
    return rows // min(rows, _block_rows(cols, elements))


def _add_half(name, kinds, grads, recvs, core):
    n = len(kinds)
    shapes = [recv.shape for recv in recvs]
    nb = _grid_steps(shapes, 1 << 20)

    def body(c_ref, *refs):
        for g_ref, r_ref, o_ref in zip(refs[:n], refs[n:2 * n], refs[2 * n:]):
            o_ref[...] = (g_ref[...].astype(F32) + r_ref[...].astype(F32)).astype(o_ref.dtype)

    own = [pl.BlockSpec((rows // nb, cols), (lambda i, c_ref: (c_ref[0] * nb + i, 0)) if COLS_SHARDED[k] else
                        (lambda i, c_ref: (i, c_ref[0]))) for k, (rows, cols) in zip(kinds, shapes)]
    blocks = [pl.BlockSpec((rows // nb, cols), lambda i, c_ref: (i, 0)) for rows, cols in shapes]
    return pl.pallas_call(
        body, name=name,
        grid_spec=pltpu.PrefetchScalarGridSpec(
            num_scalar_prefetch=1, grid=(nb,), in_specs=own + blocks, out_specs=blocks),
        out_shape=[jax.ShapeDtypeStruct(shape, BF16) for shape in shapes],
        compiler_params=_params("parallel"),
    )(core, *grads, *recvs)


def _exchange_pieces(kinds, halves, pack=None):
    n_p, n = N_CHIPS - 1, len(kinds)

    @_plan(n_p * n + (0 if pack is None else N_DEV - 1))
    def plan(refs):
        x, y, c, _ = _place()
        copies = []
        if pack is not None:
            me = 4 * x + 2 * y + c
            peers = [((1 - x) if m & 4 else x, (1 - y) if m & 2 else y, (1 - c) if m & 1 else c) for m in range(1, N_DEV)]
            copies += [(refs[2 * n], refs[2 * n + 1].at[me], peer) for peer in peers]
        return copies + [(_shard_view(kind, half, 2 * px + py), land.at[j], (px, py, c))
                         for j, (px, py) in enumerate(_other_chips(x, y))
                         for kind, half, land in zip(kinds, refs[:n], refs[n:2 * n])]

    lands = [lax.empty((n_p,) + PIECE_SHAPES[kind], BF16) for kind in kinds]
    small = () if pack is None else (pack, lax.empty((N_DEV,) + pack.shape, F32))
    return _Split("exchange_pieces_" + "".join(map(str, kinds)), (*halves, *lands, *small), plan,
                  peers="chips" if pack is None else None)


def _sum_pieces(name, kinds, halves, slots, place, after):
    n, n_p = len(kinds), N_CHIPS - 1
    shapes = [slot.shape[1:] for slot in slots]
    nb = _grid_steps(shapes, 1 << 18)

    def body(s_ref, *refs):
        for own_ref, slot_ref, o_ref in zip(refs[:n], refs[n:2 * n], refs[2 * n + 1:]):
            total = own_ref[...].astype(F32)
            for j in range(n_p):
                total = total + slot_ref[j].astype(F32)
            o_ref[...] = total

    own, out, shards = [], [], []
    for k, (rows, cols) in zip(kinds, shapes):
        if COLS_SHARDED[k]:
            own_map, out_map, shard = (lambda i, s: (i, s[0])), (lambda i, s: (s[1] * nb + i, 0)), (2 * rows, cols)
        else:
            own_map, out_map, shard = (lambda i, s: (s[0] * nb + i, 0)), (lambda i, s: (i, s[1])), (rows, 2 * cols)
        own.append(pl.BlockSpec((rows // nb, cols), own_map))
        out.append(pl.BlockSpec((rows // nb, cols), out_map))
        shards.append(jax.ShapeDtypeStruct(shard, F32))
    landed = [pl.BlockSpec((n_p, rows // nb, cols), lambda i, s: (0, i, 0)) for rows, cols in shapes]
    return pl.pallas_call(
        body, name=name,
        grid_spec=pltpu.PrefetchScalarGridSpec(
            num_scalar_prefetch=1, grid=(nb,), in_specs=own + landed + [ANY], out_specs=out),
        out_shape=shards,
        compiler_params=_params("parallel"),
    )(place, *halves, *slots, after)


def _join_halves(kinds, shards):
    @_plan(len(kinds))
    def plan(refs):
        x, y, c, _ = _place()
        return [(_half_view(kind, g, c), _half_view(kind, g, c), (x, y, 1 - c)) for kind, g in zip(kinds, refs)]

    return _Split("join_halves_" + "".join(map(str, kinds)), tuple(shards), plan, peers="sibling")


N_DEV = 8


def _sum_shared(pack, land, device):
    def body(d_ref, p_ref, l_ref, o_ref):
        me = d_ref[0]
        total = jnp.where(me == 0, p_ref[...], l_ref[0])
        for d in range(1, N_DEV):
            total = total + jnp.where(me == d, p_ref[...], l_ref[d])
        o_ref[...] = total

    return pl.pallas_call(
        body, name="sum_shared",
        grid_spec=pltpu.PrefetchScalarGridSpec(
            num_scalar_prefetch=1, grid=(1,),
            in_specs=[pl.BlockSpec(pack.shape, lambda i, d: (0, 0)), pl.BlockSpec(land.shape, lambda i, d: (0, 0, 0))],
            out_specs=pl.BlockSpec(pack.shape, lambda i, d: (0, 0))),
        out_shape=jax.ShapeDtypeStruct(pack.shape, F32),
    )(device, pack, land)


def _adamw(name, weights, after=None):
    shapes = [w.shape for w, _, _, _ in weights]
    nb = _grid_steps(shapes, 1 << 18)
    extra = [] if after is None else [after]
    n_in = 4 * len(weights) + len(extra)

    def body(*refs):
        for k in range(len(weights)):
            w_ref, g_ref, m_ref, v_ref = refs[4 * k:4 * k + 4]
            go_ref, d_ref, nm_ref, nv_ref = refs[n_in + 4 * k:n_in + 4 * k + 4]
            g = g_ref[...]
            go_ref[...] = g
            d_ref[...], nm_ref[...], nv_ref[...] = _adam_step(w_ref[...], g, m_ref[...], v_ref[...])

    blocks = [pl.BlockSpec((rows // nb, cols), lambda i: (i, 0)) for rows, cols in shapes for _ in range(4)]
    outs = pl.pallas_call(
        body, name=name, grid=(nb,), in_specs=blocks + [ANY] * len(extra), out_specs=blocks,
        out_shape=[jax.ShapeDtypeStruct(shape, F32) for shape in shapes for _ in range(4)],
        compiler_params=_params("parallel"),
    )(*[a for group in weights for a in group], *extra)
    return [outs[4 * k:4 * k + 4] for k in range(len(weights))]


def _adam_step(w, g, m, v):
    nm = ADAM_B1 * m + (1.0 - ADAM_B1) * g
    nv = ADAM_B2 * v + (1.0 - ADAM_B2) * jnp.square(g)
    m_hat = nm * (1.0 / (1.0 - ADAM_B1 ** ADAM_STEP))
    v_hat = nv * (1.0 / (1.0 - ADAM_B2 ** ADAM_STEP))
    return -ADAM_LR * (m_hat / (jnp.sqrt(v_hat) + ADAM_EPS) + ADAM_WD * w), nm, nv


def _adamw_small(tot, chip, weights, ms, vs, after):
    n, half = len(weights), D_MODEL // 2

    def body(chip_ref, tot_ref, *refs):
        ins, outs = refs[:3 * n], refs[3 * n + 1:]
        tot = tot_ref[...]
        conv_all = jnp.concatenate([tot[5:6, half:], tot[6:7, :half], tot[6:7, half:]], axis=0)
        conv = sum(jnp.where(chip_ref[0] == s, conv_all[:, s * LANES:(s + 1) * LANES], 0.0) for s in range(N_CHIPS))
        grads = [jnp.concatenate([tot[4:5, :half], tot[4:5, half:]], axis=0), tot[5:6, :half], conv,
                 tot[0:1], tot[1:2], tot[2:3], tot[3:4]]
        for k, g in enumerate(grads):
            delta, nm, nv = _adam_step(ins[k][...], g, ins[n + k][...], ins[2 * n + k][...])
            outs[k][...], outs[n + k][...], outs[2 * n + k][...], outs[3 * n + k][...] = g, delta, nm, nv
        outs[4 * n][...] = tot[7:8, 0:1]

    whole = lambda a: pl.BlockSpec(a.shape, lambda i, s: (0,) * a.ndim)
    arrays = (*weights, *ms, *vs)
    loss = jax.ShapeDtypeStruct((1, 1), F32)
    return pl.pallas_call(
        body, name="adamw_small",
        grid_spec=pltpu.PrefetchScalarGridSpec(
            num_scalar_prefetch=1, grid=(1,), in_specs=[whole(tot)] + [whole(a) for a in arrays] + [ANY],
            out_specs=[whole(a) for a in weights] * 4 + [whole(loss)]),
        out_shape=[jax.ShapeDtypeStruct(a.shape, F32) for a in weights] * 4 + [loss],
    )(chip, tot, *arrays, after)


def kernel(x, w_in, lb_logits, gate_norm_w, conv_w, w_out, ln1_g, ln1_b, w_ff1, w_ff2, ln2_g, ln2_b, loss_target, m_w_in, m_lb_logits, m_gate_norm_w, m_conv_w, m_w_out, m_ln1_g, m_ln1_b, m_w_ff1, m_w_ff2, m_ln2_g, m_ln2_b, v_w_in, v_lb_logits, v_gate_norm_w, v_conv_w, v_w_out, v_ln1_g, v_ln1_b, v_w_ff1, v_w_ff2, v_ln2_g, v_ln2_b):
    xs, tgt = x[0], loss_target[0]
    chip = 2 * lax.axis_index("x") + lax.axis_index("y")
    core = lax.axis_index("c").astype(jnp.int32).reshape(1)
    chip1 = chip.astype(jnp.int32).reshape(1)
    place = jnp.concatenate([chip1, core])

    conv4 = lax.dynamic_update_slice(jnp.zeros((N_CHIPS,) + conv_w.shape[1:], F32), conv_w, (chip, 0, 0))
    ici_in = _gather_w_in_over_ici(_place_shard("place_w_in", w_in[0], chip1, True), conv4)
    rest = (1, 2, 3)
    ici_rest = _gather_over_ici(rest, (_place_shard("place_w_out", w_out[0], chip1, False, after=ici_in.token),
                                       _place_shard("place_w_ff1", w_ff1[0], chip1, True, after=ici_in.token),
                                       _place_shard("place_w_ff2", w_ff2[0], chip1, False, after=ici_in.token)))
    wb_in, cv4 = ici_in.wait(ici_rest.token)
    d2d_in = _gather_over_d2d((0,), (wb_in,))
    wb_in, = d2d_in.wait(d2d_in.token)
    conv_full = cv4.transpose(1, 0, 2).reshape(3, CONV_WIDTH)

    proj, bcu, xb, cat_c = _in_proj(xs, wb_in, conv_full, ici_rest.token)
    relay_rest = _relay_over_ici(rest, ici_rest.wait(proj))
    o, states = _hgrn_fwd(proj, lb_logits, relay_rest.token)
    d2d_rest = _gather_over_d2d(rest, relay_rest.wait(o))
    cat_h = _gate_fwd(proj, o, gate_norm_w, d2d_rest.token)
    wb_out, wb_ff1, wb_ff2 = d2d_rest.wait(cat_h)

    (h1b, r, da, dpre2b, dpre1, dcat, g_ln1_g, g_ln1_b, g_ln2_g, g_ln2_b, loss8, g_out_local) = _sublayers(
        cat_h, cat_c, xs, tgt, wb_out, wb_ff1, wb_ff2, ln1_g, ln1_b, ln2_g, ln2_b)

    names = ("w_in", "w_out", "w_ff1", "w_ff2")

    def named(prefix, kinds):
        return prefix + "".join("_" + names[k] for k in kinds)

    def add_halves(kinds, grads, lands):
        return _add_half(named("add_half", kinds), kinds, grads, lands, core)

    def sum_pieces(kinds, halves, lands, after):
        return _sum_pieces(named("sum_pieces", kinds), kinds, halves, lands, place, after)

    early = (1, 2, 3)
    g_ff2_local, dpc, g_conv = _dw_ff2(r, dpre2b, dcat, bcu, conv_full)
    swap_a = _swap_halves((1, 3), (g_out_local, g_ff2_local))
    g_ff1_local, do, dog, g_gnw = _dw_ff1(h1b, da, dcat, o, proj, gate_norm_w, swap_a.token)
    swap_b = _swap_halves((2,), (g_ff1_local,))
    swapped_a = swap_a.wait(swap_b.token)
    halves_a = add_halves((1, 3), swapped_a[:2], swapped_a[2:])
    swapped_b = swap_b.wait(halves_a[1])
    halves = (halves_a[0], *add_halves((2,), swapped_b[:1], swapped_b[1:]), halves_a[1])
    exch = _exchange_pieces(early, halves)
    dph, g_lbl = _hgrn_bwd(proj, do, states, lb_logits, exch.token)
    g_in_local, grad_x = _dw_in(xb, dph, dog, dpc, wb_in, dpre1, dph)

    late = (0,)
    swap = _swap_halves(late, (g_in_local,))
    exchanged = exch.wait(swap.token)
    pack = jnp.concatenate([
        g_ln1_g, g_ln1_b, g_ln2_g, g_ln2_b,
        jnp.concatenate([g_lbl[0:1], g_lbl[1:2]], axis=1),
        jnp.concatenate([g_gnw, g_conv[0:1]], axis=1),
        jnp.concatenate([g_conv[1:2], g_conv[2:3]], axis=1),
        jnp.concatenate([loss8[0:1], jnp.zeros((1, D_MODEL - LANES), F32)], axis=1)], axis=0)
    join_a = _join_halves((2,), sum_pieces((2,), exchanged[1:2], exchanged[4:5], swap.token))
    swapped = swap.wait(join_a.token)
    exch = _exchange_pieces(late, add_halves(late, swapped[:1], swapped[1:]), pack)
    join_b = _join_halves((1, 3), sum_pieces((1, 3), exchanged[0:3:2], exchanged[3:6:2], exch.token))
    g_w_ff1, = join_a.wait(join_b.token)
    (g_w_ff1, d_ff1, nm_ff1, nv_ff1), = _adamw("adamw_w_ff1", [(w_ff1[0], g_w_ff1, m_w_ff1[0], v_w_ff1[0])])
    g_w_out, g_w_ff2 = join_b.wait(d_ff1)
    (g_w_ff2, d_ff2, nm_ff2, nv_ff2), (g_w_out, d_out, nm_out, nv_out) = _adamw(
        "adamw_w_ff2_w_out", [(w_ff2[0], g_w_ff2, m_w_ff2[0], v_w_ff2[0]), (w_out[0], g_w_out, m_w_out[0], v_w_out[0])])
    exchanged = exch.wait(d_out)
    tot = _sum_shared(exchanged[2], exchanged[3], 2 * chip1 + core)
    join = _join_halves(late, sum_pieces(late, exchanged[:1], exchanged[1:2], tot))
    small = ("lb_logits", "gate_norm_w", "conv_w", "ln1_g", "ln1_b", "ln2_g", "ln2_b")
    small_out = _adamw_small(
        tot, chip1, (lb_logits, gate_norm_w, conv_w[0], ln1_g, ln1_b, ln2_g, ln2_b),
        (m_lb_logits, m_gate_norm_w, m_conv_w[0], m_ln1_g, m_ln1_b, m_ln2_g, m_ln2_b),
        (v_lb_logits, v_gate_norm_w, v_conv_w[0], v_ln1_g, v_ln1_b, v_ln2_g, v_ln2_b), join.token)
    g_w_in, = join.wait(small_out[0])
    (g_w_in, d_in, nm_in, nv_in), = _adamw("adamw_w_in", [(w_in[0], g_w_in, m_w_in[0], v_w_in[0])])
    loss = small_out[4 * len(small)][0, 0]

    def results(n_kind, large):
        out = dict(zip(small, small_out[n_kind * len(small):(n_kind + 1) * len(small)]))
        out["conv_w"] = out["conv_w"][None]
        out.update({name: a[None] for name, a in zip(("w_in", "w_out", "w_ff1", "w_ff2"), large)})
        return [out[name] for name in ("w_in", "lb_logits", "gate_norm_w", "conv_w", "w_out", "ln1_g", "ln1_b",
                                       "w_ff1", "w_ff2", "ln2_g", "ln2_b")]

    return (loss, grad_x[None], *results(0, (g_w_in, g_w_out, g_w_ff1, g_w_ff2)),
            *results(1, (d_in, d_out, d_ff1, d_ff2)), *results(2, (nm_in, nm_out, nm_ff1, nm_ff2)),
            *results(3, (nv_in, nv_out, nv_ff1, nv_ff2)))
```

```python
import jax
import jax.numpy as jnp
from jax import lax
from jax.experimental import pallas as pl
from jax.experimental.pallas import tpu as pltpu

F32 = jnp.float32
BF16 = jnp.bfloat16
MXU_DTYPE = jnp.bfloat16

D_MODEL = 1024
HGRN_WIDTH = 512
HEAD_DIM = 128
N_HEADS = 4
CONV_WIDTH = 512
CHUNK = 64
D_FF = 4096
IN_COLS = 3584
GROUP = 512
N_GROUPS = IN_COLS // GROUP
ALPHA = 2.0 ** 0.25
EPS = 1e-5
N_CHIPS = 4
ADAM_LR, ADAM_B1, ADAM_B2, ADAM_EPS, ADAM_WD, ADAM_STEP = 0.001, 0.9, 0.999, 1e-08, 0.01, 10

LANES = 128
SUBLANES = 8
VMEM_LIMIT = 56 * 1024 * 1024
FF_BLOCK = 1024
N_FF = D_FF // FF_BLOCK
GATE_STRIP = 64

NN = (((1,), (0,)), ((), ()))
NT = (((1,), (1,)), ((), ()))
TN = (((0,), (0,)), ((), ()))
MESH = pl.DeviceIdType.MESH
ANY = pl.BlockSpec(memory_space=pl.ANY)


def _dot(a, b, dims):
    return lax.dot_general(a.astype(MXU_DTYPE), b.astype(MXU_DTYPE), dims, preferred_element_type=F32)


def _dot_exact(ones, v):
    ones = ones.astype(jnp.bfloat16)
    hi = v.astype(jnp.bfloat16)
    rest = v - hi.astype(F32)
    mid = rest.astype(jnp.bfloat16)
    low = (rest - mid.astype(F32)).astype(jnp.bfloat16)
    return sum(lax.dot_general(ones, part, NN, preferred_element_type=F32) for part in (hi, mid, low))


def _params(*sem):
    return pltpu.CompilerParams(dimension_semantics=sem, vmem_limit_bytes=VMEM_LIMIT)


def _resident(shape):
    return pl.BlockSpec(shape, lambda *_: (0,) * len(shape), pipeline_mode=pl.Buffered(1))


def _sigmoid(v):
    return 1.0 / (1.0 + jnp.exp(-v))


def _lower_bound(lbl):
    m = jnp.max(lbl, axis=0, keepdims=True)
    e = jnp.exp(lbl - m)
    s = e / jnp.sum(e, axis=0, keepdims=True)
    return s[0:1, :], s[1:2, :]


def _heads(v):
    return [v[:, h * HEAD_DIM:(h + 1) * HEAD_DIM] for h in range(N_HEADS)]


def _per_head(fn, *arrays):
    return jnp.concatenate([fn(*parts) for parts in zip(*map(_heads, arrays))], axis=1)


def _in_proj(x, w_in, conv_w, after):
    t = x.shape[0]
    tm = min(t, 512)

    def body(x_ref, w_ref, cw_ref, after_ref, o_ref, bcu_ref, xb_ref, y_ref, zbuf):
        @pl.when(pl.program_id(0) == 0)
        def _():
            zbuf[tm:tm + SUBLANES, :] = jnp.zeros((SUBLANES, CONV_WIDTH), F32)

        xb = x_ref[...].astype(xb_ref.dtype)
        xb_ref[...] = xb
        group = lambda g: _dot(xb, w_ref[:, g * GROUP:(g + 1) * GROUP], NN)
        for g in range(4):
            o_ref[g] = group(g)
        b_gate, c_gate, u = group(4), group(5), group(6)
        for n, part in enumerate((b_gate, c_gate, u)):
            bcu_ref[n] = part.astype(bcu_ref.dtype)
        zbuf[0:SUBLANES, :] = zbuf[tm:tm + SUBLANES, :]
        zbuf[SUBLANES:SUBLANES + tm, :] = c_gate * u
        cw = cw_ref[...]
        at = lambda shift: zbuf[shift:shift + tm, :]
        conv = cw[2:3, :] * at(SUBLANES) + cw[1:2, :] * at(SUBLANES - 1) + cw[0:1, :] * at(SUBLANES - 2)
        y_ref[...] = (b_gate * conv).astype(y_ref.dtype)

    return pl.pallas_call(
        body, name="in_proj", grid=(t // tm,),
        in_specs=[pl.BlockSpec((tm, D_MODEL), lambda i: (i, 0)), _resident((D_MODEL, IN_COLS)),
                  pl.BlockSpec((3, CONV_WIDTH), lambda i: (0, 0)), ANY],
        out_specs=[pl.BlockSpec((4, tm, GROUP), lambda i: (0, i, 0)), pl.BlockSpec((3, tm, GROUP), lambda i: (0, i, 0)),
                   pl.BlockSpec((tm, D_MODEL), lambda i: (i, 0)), pl.BlockSpec((tm, CONV_WIDTH), lambda i: (i, 0))],
        out_shape=[jax.ShapeDtypeStruct((4, t, GROUP), F32), jax.ShapeDtypeStruct((3, t, GROUP), BF16),
                   jax.ShapeDtypeStruct((t, D_MODEL), BF16), jax.ShapeDtypeStruct((t, CONV_WIDTH), BF16)],
        scratch_shapes=[pltpu.VMEM((tm + SUBLANES, CONV_WIDTH), F32)],
        compiler_params=_params("arbitrary"),
    )(x, w_in, conv_w, after)


def _gates(fp, lb):
    sig = _sigmoid(fp)
    f = lb + (1.0 - lb) * sig
    return sig, f, jnp.log(f), 1.0 - f


def _chunk_masks():
    row = lax.broadcasted_iota(jnp.int32, (CHUNK, CHUNK), 0)
    col = lax.broadcasted_iota(jnp.int32, (CHUNK, CHUNK), 1)
    return row >= col, row <= col


def _hgrn_fwd(proj, lb_logits, after):
    t = proj.shape[1]
    tb = min(t, 512)
    ncb = tb // CHUNK

    def body(q_ref, f_ref, v_ref, lbl_ref, after_ref, o_ref, st_ref, s_scr):
        @pl.when(pl.program_id(0) == 0)
        def _():
            s_scr[...] = jnp.zeros_like(s_scr)

        lb, _ = _lower_bound(lbl_ref[...])
        causal, _ = _chunk_masks()

        every = range(ncb)
        rows = [slice(c * CHUNK, (c + 1) * CHUNK) for c in every]
        q, v = [q_ref[r, :] for r in rows], [v_ref[r, :] for r in rows]
        gates = [_gates(f_ref[r, :], lb) for r in rows]
        k = [gt[3] for gt in gates]
        b = [_dot_exact(causal, gt[2]) for gt in gates]
        mid, last = [x[CHUNK // 2:CHUNK // 2 + 1, :] for x in b], [x[CHUNK - 1:CHUNK, :] for x in b]
        qt = [q[c] * jnp.exp(b[c] - mid[c]) for c in every]
        kt = [k[c] * jnp.exp(mid[c] - b[c]) for c in every]
        qi = [q[c] * jnp.exp(b[c]) for c in every]
        ks = [k[c] * jnp.exp(last[c] - b[c]) for c in every]
        dec = [jnp.exp(x) for x in last]
        scores = [[jnp.where(causal, _dot(a, b_, NT), 0.0) for a, b_ in zip(_heads(qt[c]), _heads(kt[c]))] for c in every]
        intra = [[_dot(s, v_h, NN) for s, v_h in zip(scores[c], _heads(v[c]))] for c in every]
        update = [_per_head(lambda v_h, ks_h: _dot(v_h, ks_h, TN), v[c], ks[c]) for c in every]

        st = s_scr[...]
        states = []
        for c in every:
            states.append(st)
            st_ref[c] = st
            st = dec[c] * st + update[c]
        s_scr[...] = st

        o_ref[...] = jnp.concatenate(
            [jnp.concatenate([i_h + _dot(qi_h, st_h, NT) for i_h, qi_h, st_h in
                              zip(intra[c], _heads(qi[c]), _heads(states[c]))], axis=1) for c in every], axis=0)

    grp = lambda g: pl.BlockSpec((None, tb, GROUP), lambda i: (g, i, 0))
    return pl.pallas_call(
        body, name="hgrn_fwd", grid=(t // tb,),
        in_specs=[grp(0), grp(1), grp(2), pl.BlockSpec((2, HGRN_WIDTH), lambda i: (0, 0)), ANY],
        out_specs=[pl.BlockSpec((tb, HGRN_WIDTH), lambda i: (i, 0)),
                   pl.BlockSpec((ncb, HEAD_DIM, HGRN_WIDTH), lambda i: (i, 0, 0))],
        out_shape=[jax.ShapeDtypeStruct((t, HGRN_WIDTH), F32),
                   jax.ShapeDtypeStruct((t // CHUNK, HEAD_DIM, HGRN_WIDTH), F32)],
        scratch_shapes=[pltpu.VMEM((HEAD_DIM, HGRN_WIDTH), F32)],
        compiler_params=_params("arbitrary"),
    )(proj, proj, proj, lb_logits, after)


def _gate_fwd(proj, o, gate_norm_w, after):
    t = proj.shape[1]
    tb = min(t, 1024)

    def body(o_ref, og_ref, gnw_ref, after_ref, out_ref):
        gnw = gnw_ref[...]
        for s in range(tb // GATE_STRIP):
            rows = slice(s * GATE_STRIP, (s + 1) * GATE_STRIP)
            og = og_ref[rows, :]
            on = _per_head(lambda o_h: o_h * lax.rsqrt(jnp.mean(o_h * o_h, axis=-1, keepdims=True) + EPS), o_ref[rows, :])
            out_ref[rows, :] = (on * gnw * (og * _sigmoid(og))).astype(out_ref.dtype)

    tile = pl.BlockSpec((tb, GROUP), lambda i: (i, 0))
    return pl.pallas_call(
        body, name="gate_fwd", grid=(t // tb,),
        in_specs=[tile, pl.BlockSpec((None, tb, GROUP), lambda i: (3, i, 0)), pl.BlockSpec((1, GROUP), lambda i: (0, 0)), ANY],
        out_specs=tile,
        out_shape=jax.ShapeDtypeStruct((t, HGRN_WIDTH), BF16),
        compiler_params=_params("parallel"),
    )(o, proj, gate_norm_w, after)


def _ln_bwd(dy, xhat, rstd, g):
    dxhat = dy * g
    m1 = jnp.mean(dxhat, axis=-1, keepdims=True)
    m2 = jnp.mean(dxhat * xhat, axis=-1, keepdims=True)
    return rstd * (dxhat - m1 - xhat * m2)


def _layer_norm(pre):
    xc = pre - jnp.mean(pre, axis=-1, keepdims=True)
    rstd = lax.rsqrt(jnp.mean(xc * xc, axis=-1, keepdims=True) + EPS)
    return xc * rstd, rstd


def _sublayers(cat_h, cat_c, x, target, w_out, w_ff1, w_ff2, g1, b1, g2, b2):
    t = x.shape[0]
    tm = min(t, 256)

    def body(ch_ref, cc_ref, x_ref, tg_ref, wo_ref, w1_ref, w2_ref, g1_ref, b1_ref, g2_ref, b2_ref,
             h1_ref, r_ref, da_ref, dp2b_ref, dp1_ref, dcat_ref, dg1_ref, db1_ref, dg2_ref, db2_ref, loss_ref, gwo_ref,
             gwo_acc, gwo_narrow, sem):
        @pl.when(pl.program_id(0) == 0)
        def _():
            for ref in (dg1_ref, db1_ref, dg2_ref, db2_ref, loss_ref, gwo_acc):
                ref[...] = jnp.zeros_like(ref)

        mix = _dot(ch_ref[...], wo_ref[0:GROUP, :], NN) + _dot(cc_ref[...], wo_ref[GROUP:2 * GROUP, :], NN)
        xhat1, rstd1 = _layer_norm(ALPHA * x_ref[...] + mix)
        h1 = xhat1 * g1_ref[...] + b1_ref[...]
        h1b = h1.astype(h1_ref.dtype)
        h1_ref[...] = h1b
        mlp = jnp.zeros((tm, D_MODEL), F32)
        for j in range(N_FF):
            cols = slice(j * FF_BLOCK, (j + 1) * FF_BLOCK)
            r = jnp.square(jnp.maximum(_dot(h1b, w1_ref[:, cols], NN), 0.0)).astype(r_ref.dtype)
            r_ref[:, cols] = r
            mlp = mlp + _dot(r, w2_ref[cols, :], NN)
        xhat2, rstd2 = _layer_norm(ALPHA * h1 + mlp)
        err = xhat2 * g2_ref[...] + b2_ref[...] - tg_ref[...]
        loss_ref[...] += 0.5 * jnp.sum(jnp.mean(err * err, axis=-1, keepdims=True))
        dy = err * (1.0 / D_MODEL)
        dg2_ref[...] += jnp.sum(dy * xhat2, axis=0, keepdims=True)
        db2_ref[...] += jnp.sum(dy, axis=0, keepdims=True)
        dp2 = _ln_bwd(dy, xhat2, rstd2, g2_ref[...])
        dp2b = dp2.astype(dp2b_ref.dtype)
        dp2b_ref[...] = dp2b
        back = jnp.zeros((tm, D_MODEL), F32)
        for j in range(N_FF):
            cols = slice(j * FF_BLOCK, (j + 1) * FF_BLOCK)
            dr = _dot(dp2b, w2_ref[cols, :], NT)
            da = (dr * (2.0 * jnp.sqrt(r_ref[:, cols].astype(F32)))).astype(da_ref.dtype)
            da_ref[:, cols] = da
            back = back + _dot(da, w1_ref[:, cols], NT)
        dh1 = ALPHA * dp2 + back
        dg1_ref[...] += jnp.sum(dh1 * xhat1, axis=0, keepdims=True)
        db1_ref[...] += jnp.sum(dh1, axis=0, keepdims=True)
        dp1 = _ln_bwd(dh1, xhat1, rstd1, g1_ref[...])
        dp1b = dp1.astype(MXU_DTYPE)
        dp1_ref[...] = dp1
        dcat_ref[...] = _dot(dp1b, wo_ref[...], NT)
        gwo_acc[0:GROUP, :] += _dot(ch_ref[...], dp1b, TN)
        gwo_acc[GROUP:2 * GROUP, :] += _dot(cc_ref[...], dp1b, TN)

        @pl.when(pl.program_id(0) == pl.num_programs(0) - 1)
        def _():
            gwo_narrow[...] = gwo_acc[...].astype(gwo_narrow.dtype)
            copy = pltpu.make_async_copy(gwo_narrow, gwo_ref, sem.at[0])
            copy.start()
            copy.wait()

    row = pl.BlockSpec((tm, D_MODEL), lambda i: (i, 0))
    wide = pl.BlockSpec((tm, D_FF), lambda i: (i, 0))
    vec = pl.BlockSpec((1, D_MODEL), lambda i: (0, 0))
    narrow = lambda dtype: jax.ShapeDtypeStruct((t, D_MODEL), dtype)
    return pl.pallas_call(
        body, name="sublayers", grid=(t // tm,),
        in_specs=[pl.BlockSpec((tm, GROUP), lambda i: (i, 0)), pl.BlockSpec((tm, GROUP), lambda i: (i, 0)), row, row,
                  _resident((D_MODEL, D_MODEL)),
                  _resident((D_MODEL, D_FF)), _resident((D_FF, D_MODEL)), vec, vec, vec, vec],
        out_specs=[row, wide, wide, row, row, row, vec, vec, vec, vec,
                   pl.BlockSpec((SUBLANES, LANES), lambda i: (0, 0)), ANY],
        out_shape=[narrow(BF16), jax.ShapeDtypeStruct((t, D_FF), BF16), jax.ShapeDtypeStruct((t, D_FF), BF16),
                   narrow(BF16), narrow(F32), narrow(F32)]
                  + [jax.ShapeDtypeStruct((1, D_MODEL), F32)] * 4
                  + [jax.ShapeDtypeStruct((SUBLANES, LANES), F32), jax.ShapeDtypeStruct((D_MODEL, D_MODEL), BF16)],
        scratch_shapes=[pltpu.VMEM((D_MODEL, D_MODEL), F32), pltpu.VMEM((D_MODEL, D_MODEL), BF16),
                        pltpu.SemaphoreType.DMA((1,))],
        compiler_params=_params("arbitrary"),
    )(cat_h, cat_c, x, target, w_out, w_ff1, w_ff2, g1, b1, g2, b2)


def _hgrn_bwd(proj, do, states, lb_logits, after):
    t = proj.shape[1]
    tb = min(t, 512)
    ncb = tb // CHUNK
    nblk = t // tb

    def body(q_ref, f_ref, v_ref, do_ref, st_ref, lbl_ref, after_ref, dp_ref, dlbl_ref, ds_scr, dlb_scr):
        i = pl.program_id(0)

        @pl.when(i == 0)
        def _():
            ds_scr[...] = jnp.zeros_like(ds_scr)
            dlb_scr[...] = jnp.zeros_like(dlb_scr)

        lb, s1 = _lower_bound(lbl_ref[...])
        causal, anti = _chunk_masks()
        every = range(ncb)
        rows = [slice(c * CHUNK, (c + 1) * CHUNK) for c in every]
        q, v, do = ([ref[r, :] for r in rows] for ref in (q_ref, v_ref, do_ref))
        st = [st_ref[c] for c in every]
        gates = [_gates(f_ref[r, :], lb) for r in rows]
        sig, f, k = ([gt[n] for gt in gates] for n in (0, 1, 3))
        b = [_dot_exact(causal, gt[2]) for gt in gates]
        mid, last = [x[CHUNK // 2:CHUNK // 2 + 1, :] for x in b], [x[CHUNK - 1:CHUNK, :] for x in b]
        e_q = [jnp.exp(b[c] - mid[c]) for c in every]
        e_k = [jnp.exp(mid[c] - b[c]) for c in every]
        e_i = [jnp.exp(x) for x in b]
        e_s = [jnp.exp(last[c] - b[c]) for c in every]
        dec = [jnp.exp(x) for x in last]
        qt, kt, qi, ks = ([a[c] * e[c] for c in every] for a, e in ((q, e_q), (k, e_k), (q, e_i), (k, e_s)))

        def masked(a, b_):
            return [[jnp.where(causal, _dot(a_h, b_h, NT), 0.0) for a_h, b_h in zip(_heads(a[c]), _heads(b_[c]))]
                    for c in every]

        def with_scores(s, other, dims):
            return [jnp.concatenate([_dot(s_h, o_h, dims) for s_h, o_h in zip(s[c], _heads(other[c]))], axis=1)
                    for c in every]

        def per_head(dims, a, b_):
            return [_per_head(lambda a_h, b_h: _dot(a_h, b_h, dims), a[c], b_[c]) for c in every]

        scores, dscores = masked(qt, kt), masked(do, v)
        dqt, dkt, dv_intra = with_scores(dscores, kt, NN), with_scores(dscores, qt, TN), with_scores(scores, do, TN)
        dqi, update = per_head(NN, do, st), per_head(TN, do, qi)

        dst = ds_scr[...]
        dsts = [None] * ncb
        for c in reversed(every):
            dsts[c] = dst
            dst = dec[c] * dst + update[c]
        ds_scr[...] = dst

        dv_state, dks = per_head(NT, ks, dsts), per_head(NN, v, dsts)
        ddec = [jnp.sum(dsts[c] * st[c], axis=0, keepdims=True) for c in every]
        dq = [dqt[c] * e_q[c] + dqi[c] * e_i[c] for c in every]
        dk = [dkt[c] * e_k[c] + dks[c] * e_s[c] for c in every]
        db = [q[c] * dq[c] - k[c] * dk[c] for c in every]
        db_last = [jnp.sum(dks[c] * ks[c], axis=0, keepdims=True) + ddec[c] * dec[c] for c in every]
        dg = [_dot_exact(anti, db[c]) + db_last[c] for c in every]
        df = [dg[c] / f[c] - dk[c] for c in every]
        dlb_scr[...] += sum(jnp.sum(df[c] * (1.0 - sig[c]), axis=0, keepdims=True) for c in every)
        dfp = [df[c] * (1.0 - lb) * sig[c] * (1.0 - sig[c]) for c in every]
        dv = [dv_intra[c] + dv_state[c] for c in every]
        for n, parts in enumerate((dq, dfp, dv)):
            dp_ref[n] = jnp.concatenate(parts, axis=0).astype(dp_ref.dtype)

        @pl.when(i == nblk - 1)
        def _():
            dlb = dlb_scr[...]
            dlbl_ref[0:1, :] = dlb * lb * (1.0 - lb)
            dlbl_ref[1:2, :] = -dlb * lb * s1

    grp = lambda g: pl.BlockSpec((None, tb, GROUP), lambda i: (g, nblk - 1 - i, 0))
    vec = pl.BlockSpec((2, HGRN_WIDTH), lambda i: (0, 0))
    return pl.pallas_call(
        body, name="hgrn_bwd", grid=(nblk,),
        in_specs=[grp(0), grp(1), grp(2), pl.BlockSpec((tb, HGRN_WIDTH), lambda i: (nblk - 1 - i, 0)),
                  pl.BlockSpec((ncb, HEAD_DIM, HGRN_WIDTH), lambda i: (nblk - 1 - i, 0, 0)), vec, ANY],
        out_specs=[pl.BlockSpec((3, tb, HGRN_WIDTH), lambda i: (0, nblk - 1 - i, 0)), vec],
        out_shape=[jax.ShapeDtypeStruct((3, t, HGRN_WIDTH), BF16), jax.ShapeDtypeStruct((2, HGRN_WIDTH), F32)],
        scratch_shapes=[pltpu.VMEM((HEAD_DIM, HGRN_WIDTH), F32), pltpu.VMEM((1, HGRN_WIDTH), F32)],
        compiler_params=_params("arbitrary"),
    )(proj, proj, proj, do, states, lb_logits, after)


GRAD_TILE = 512
OUT_PARTS = 4


class _Side:
    def __init__(self, operands, in_specs, out_shape, out_specs, scratch, init, begin):
        self.operands, self.in_specs, self.out_shape, self.out_specs = operands, in_specs, out_shape, out_specs
        self.scratch, self.init, self.begin = scratch, init, begin


def _grad_w(name, operands, widths, shape, step, after=None, side=None):
    t = operands[0].shape[-2]
    tt = min(t, GRAD_TILE)
    n_in, n_steps = len(operands), t // tt
    in_specs = [pl.BlockSpec((tt, w), lambda k: (k, 0)) if a.ndim == 2 else
                pl.BlockSpec((a.shape[0], tt, w), lambda k: (0, k, 0)) for a, w in zip(operands, widths)]
    extra = [] if after is None else [after]
    s_in, s_out = (len(side.operands), len(side.out_shape)) if side else (0, 0)
    first_out = n_in + s_in + len(extra)

    def body(*refs):
        o_ref, side_outs = refs[first_out], refs[first_out + 1:first_out + 1 + s_out]
        acc, narrow, sem = refs[first_out + 1 + s_out:first_out + 4 + s_out]
        k = pl.program_id(0)

        @pl.when(k == 0)
        def _():
            acc[...] = jnp.zeros_like(acc)
            if side:
                side.init(side_outs)

        tick = side.begin(k, n_steps, refs[n_in:n_in + s_in], side_outs, refs[first_out + 4 + s_out:]) if side else None
        step(acc, *refs[:n_in], tick or (lambda j: None))

        @pl.when(k == n_steps - 1)
        def _():
            part = shape[0] // OUT_PARTS
            copies = []
            for p in range(OUT_PARTS):
                rows = pl.ds(p * part, part)
                narrow[rows, :] = acc[rows, :].astype(narrow.dtype)
                copies.append(pltpu.make_async_copy(narrow.at[rows, :], o_ref.at[rows, :], sem.at[p]))
                copies[-1].start()
            for cp in copies:
                cp.wait()

    outs = pl.pallas_call(
        body, name=name, grid=(n_steps,),
        in_specs=in_specs + (side.in_specs if side else []) + [ANY] * len(extra),
        out_specs=[ANY] + (side.out_specs if side else []),
        out_shape=[jax.ShapeDtypeStruct(shape, BF16)] + (side.out_shape if side else []),
        scratch_shapes=[pltpu.VMEM(shape, F32), pltpu.VMEM(shape, BF16), pltpu.SemaphoreType.DMA((OUT_PARTS,))]
                       + (side.scratch if side else []),
        compiler_params=_params("arbitrary"),
    )(*operands, *(side.operands if side else ()), *extra)
    return outs if side else outs[0]


def _dw_in(xb, dph, dog, dpc, w_in, dpre1, after):
    t = xb.shape[0]

    def step(acc, x_ref, dh_ref, dog_ref, dc_ref, tick):
        xv = x_ref[...]
        for g in range(N_GROUPS):
            part = dh_ref[g] if g < 3 else dog_ref[...] if g == 3 else dc_ref[g - 4]
            acc[:, g * GROUP:(g + 1) * GROUP] += _dot(xv, part, TN)
            tick(g, part)

    def begin(k, n_steps, ins, outs, scratch):
        w_ref, dp_ref = ins
        total = [ALPHA * dp_ref[...]]

        def tick(g, part):
            total[0] = total[0] + _dot(part, w_ref[:, g * GROUP:(g + 1) * GROUP], NT)
            if g == N_GROUPS - 1:
                outs[0][...] = total[0]

        return tick

    row = pl.BlockSpec((min(t, GRAD_TILE), D_MODEL), lambda k: (k, 0))
    side = _Side((w_in, dpre1), [_resident((D_MODEL, IN_COLS)), row], [jax.ShapeDtypeStruct((t, D_MODEL), F32)], [row],
                 [], lambda outs: None, begin)
    return _grad_w("dw_in", (xb, dph, dog, dpc), (D_MODEL, GROUP, GROUP, GROUP), (D_MODEL, IN_COLS), step, after, side)


def _strips_of(j, tt):
    per_tick = tt // GATE_STRIP // N_FF
    return [slice(s * GATE_STRIP, (s + 1) * GATE_STRIP) for s in range(j * per_tick, (j + 1) * per_tick)]


def _dw_ff1(h1b, da, dcat, o, proj, gate_norm_w, after):
    t = h1b.shape[0]
    tt = min(t, GRAD_TILE)

    def step(acc, h_ref, da_ref, tick):
        hv = h_ref[...]
        for j in range(N_FF):
            cols = slice(j * FF_BLOCK, (j + 1) * FF_BLOCK)
            acc[:, cols] += _dot(hv, da_ref[:, cols], TN)
            tick(j)

    def init(outs):
        outs[2][...] = jnp.zeros_like(outs[2])

    def begin(k, n_steps, ins, outs, scratch):
        do2_ref, o_ref, og_ref, gnw_ref = ins
        do_ref, dog_ref, dgnw_ref = outs
        total = [jnp.zeros((GATE_STRIP, GROUP), F32)]

        def tick(j):
            gnw = gnw_ref[...]
            for rows in _strips_of(j, tt):
                ov, og, do2 = o_ref[rows, :], og_ref[rows, :], do2_ref[rows, :]
                rs = _per_head(lambda o_h: jnp.broadcast_to(
                    lax.rsqrt(jnp.mean(o_h * o_h, axis=-1, keepdims=True) + EPS), o_h.shape), ov)
                on = ov * rs
                sg = _sigmoid(og)
                sil = og * sg
                don = do2 * gnw * sil
                total[0] = total[0] + do2 * on * sil
                dog_ref[rows, :] = (do2 * on * gnw * (sg * (1.0 + og * (1.0 - sg)))).astype(dog_ref.dtype)
                do_ref[rows, :] = rs * (don - on * _per_head(
                    lambda p_h: jnp.broadcast_to(jnp.mean(p_h, axis=-1, keepdims=True), p_h.shape), don * on))
            if j == N_FF - 1:
                dgnw_ref[...] += jnp.sum(total[0], axis=0, keepdims=True)

        return tick

    tile = pl.BlockSpec((tt, GROUP), lambda k: (k, 0))
    vec = pl.BlockSpec((1, GROUP), lambda k: (0, 0))
    side = _Side(
        (dcat, o, proj, gate_norm_w), [tile, tile, pl.BlockSpec((None, tt, GROUP), lambda k: (3, k, 0)), vec],
        [jax.ShapeDtypeStruct((t, HGRN_WIDTH), F32), jax.ShapeDtypeStruct((t, HGRN_WIDTH), BF16),
         jax.ShapeDtypeStruct((1, HGRN_WIDTH), F32)], [tile, tile, vec], [], init, begin)
    return _grad_w("dw_ff1", (h1b, da), (D_MODEL, D_FF), (D_MODEL, D_FF), step, after, side)


def _dw_ff2(r, dpre2b, dcat, bcu, conv_w):
    t = r.shape[0]
    tt = min(t, GRAD_TILE)
    hb = tt // SUBLANES
    halo = 2 * SUBLANES

    def step(acc, r_ref, d_ref, tick):
        dv = d_ref[...]
        for j in range(N_FF):
            rows = slice(j * FF_BLOCK, (j + 1) * FF_BLOCK)
            acc[rows, :] += _dot(r_ref[:, rows], dv, TN)
            tick(j)

    def init(outs):
        outs[1][...] = jnp.zeros_like(outs[1])

    def begin(k, n_steps, ins, outs, scratch):
        dy_ref, dyn_ref, b_ref, bn_ref, c_ref, u_ref, ch_ref, uh_ref, cw_ref = ins
        dp_ref, dcw_ref = outs
        zbuf, dbuf = scratch
        before = lambda ref: ref[SUBLANES:halo, :].astype(F32)
        zbuf[0:SUBLANES, :] = jnp.where(k > 0, before(ch_ref) * before(uh_ref), 0.0)
        zbuf[SUBLANES:SUBLANES + tt, :] = c_ref[...].astype(F32) * u_ref[...].astype(F32)
        dbuf[0:tt, :] = dy_ref[...] * b_ref[...].astype(F32)
        dbuf[tt:tt + SUBLANES, :] = jnp.where(k < n_steps - 1, dyn_ref[...] * bn_ref[0:SUBLANES, :].astype(F32), 0.0)
        totals = [jnp.zeros((GATE_STRIP, GROUP), F32) for _ in range(3)]

        def tick(j):
            cw = cw_ref[...]
            for rows in _strips_of(j, tt):
                at = lambda buf, shift: buf[shift + rows.start:shift + rows.stop, :]
                z, z1, z2 = at(zbuf, SUBLANES), at(zbuf, SUBLANES - 1), at(zbuf, SUBLANES - 2)
                dyc, d1, d2 = at(dbuf, 0), at(dbuf, 1), at(dbuf, 2)
                yc = cw[2:3, :] * z + cw[1:2, :] * z1 + cw[0:1, :] * z2
                dz = cw[2:3, :] * dyc + cw[1:2, :] * d1 + cw[0:1, :] * d2
                dp_ref[0, rows, :] = (dy_ref[rows, :] * yc).astype(dp_ref.dtype)
                dp_ref[1, rows, :] = (dz * u_ref[rows, :].astype(F32)).astype(dp_ref.dtype)
                dp_ref[2, rows, :] = (dz * c_ref[rows, :].astype(F32)).astype(dp_ref.dtype)
                for n, tap in enumerate((z2, z1, z)):
                    totals[n] = totals[n] + dyc * tap
            if j == N_FF - 1:
                for n in range(3):
                    dcw_ref[n:n + 1, :] += jnp.sum(totals[n], axis=0, keepdims=True)

        return tick

    grp = lambda g: pl.BlockSpec((None, tt, GROUP), lambda k: (g, k, 0))
    prev = lambda g: pl.BlockSpec((None, halo, GROUP), lambda k: (g, jnp.maximum(k * (tt // halo) - 1, 0), 0))
    nxt = lambda g: pl.BlockSpec((None, halo, GROUP), lambda k: (g, jnp.minimum((k + 1) * (tt // halo), t // halo - 1), 0))
    nxt_row = lambda k: jnp.minimum((k + 1) * hb, t // SUBLANES - 1)
    whole = pl.BlockSpec((3, CONV_WIDTH), lambda k: (0, 0))
    side = _Side(
        (dcat, dcat, bcu, bcu, bcu, bcu, bcu, bcu, conv_w),
        [pl.BlockSpec((tt, GROUP), lambda k: (k, 1)), pl.BlockSpec((SUBLANES, GROUP), lambda k: (nxt_row(k), 1)),
         grp(0), nxt(0), grp(1), grp(2), prev(1), prev(2), whole],
        [jax.ShapeDtypeStruct((3, t, CONV_WIDTH), BF16), jax.ShapeDtypeStruct((3, CONV_WIDTH), F32)],
        [pl.BlockSpec((3, tt, GROUP), lambda k: (0, k, 0)), whole],
        [pltpu.VMEM((tt + SUBLANES, GROUP), F32), pltpu.VMEM((tt + SUBLANES, GROUP), F32)], init, begin)
    return _grad_w("dw_ff2", (r, dpre2b), (D_FF, D_MODEL), (D_FF, D_MODEL), step, side=side)


def _place():
    x, y, c = lax.axis_index("x"), lax.axis_index("y"), lax.axis_index("c")
    return x, y, c, 2 * x + y


def _other_chips(x, y):
    return [(1 - x, y), (x, 1 - y), (1 - x, 1 - y)]


def _place_shard(name, w, chip, cols_sharded, after=None):
    rows, cols = w.shape
    tr = min(rows, 256)
    nb = rows // tr
    full = (rows, cols * N_CHIPS) if cols_sharded else (rows * N_CHIPS, cols)
    out_map = (lambda i, s: (i, s[0])) if cols_sharded else (lambda i, s: (s[0] * nb + i, 0))

    def body(s_ref, w_ref, *rest):
        rest[-1][...] = w_ref[...].astype(rest[-1].dtype)

    extra = [] if after is None else [after]
    return pl.pallas_call(
        body, name=name,
        grid_spec=pltpu.PrefetchScalarGridSpec(
            num_scalar_prefetch=1, grid=(nb,),
            in_specs=[pl.BlockSpec((tr, cols), lambda i, s: (i, 0))] + [ANY] * len(extra),
            out_specs=pl.BlockSpec((tr, cols), out_map)),
        out_shape=jax.ShapeDtypeStruct(full, BF16),
        compiler_params=_params("parallel"),
    )(chip, w, *extra)


HBM = pl.BlockSpec(memory_space=pltpu.HBM)
SEM = pl.BlockSpec(memory_space=pltpu.SEMAPHORE)
EFFECT = pltpu.SideEffectType.DATAFLOW_SIDE_EFFECTING


PEER_SETS = {
    "sibling": (0, lambda x, y, c: [(x, y, 1 - c)]),
    "chips": (1, lambda x, y, c: [(1 - x, y, c), (x, 1 - y, c), (1 - x, 1 - y, c)]),
    "neighbours": (2, lambda x, y, c: [(1 - x, y, c), (x, 1 - y, c)]),
}


class _Split:
    def __init__(self, name, arrays, plan, others=(), peers=None, prepare=None, sources=(), scratch=()):
        n_own, arrays = len(arrays), (*arrays, *others)
        n, n_copies, n_in = len(arrays), plan.count, len(arrays) + len(sources)
        self.name, self.plan, self.n = name, plan, n_own
        barrier_id, peer_ids = PEER_SETS[peers] if peers else (None, None)

        def body(*refs):
            send_sems, recv_sems, token = refs[n_in], refs[n_in + 1], refs[n_in + 2 + n]
            if peers:
                x, y, c, _ = _place()
                barrier = pltpu.get_barrier_semaphore()
                for peer in peer_ids(x, y, c):
                    pl.semaphore_signal(barrier, inc=1, device_id=peer, device_id_type=MESH)
            if prepare:
                prepare(refs[:n], refs[n:n_in], refs[n_in + 3 + n:])
            if peers:
                pl.semaphore_wait(barrier, len(peer_ids(0, 0, 0)))
            for k, (src, dst, to) in enumerate(plan(refs[:n])):
                pltpu.make_async_remote_copy(src_ref=src, dst_ref=dst, send_sem=send_sems.at[k], recv_sem=recv_sems.at[k],
                                             device_id=to, device_id_type=MESH).start()
            token[...] = jnp.zeros_like(token)

        outs = pl.pallas_call(
            body, name=name + "_start",
            out_shape=(pltpu.SemaphoreType.DMA((n_copies,)), pltpu.SemaphoreType.DMA((n_copies,)),
                       *[pltpu.HBM(a.shape, a.dtype) for a in arrays], jax.ShapeDtypeStruct((SUBLANES, LANES), F32)),
            in_specs=(HBM,) * n_in, out_specs=(SEM, SEM) + (HBM,) * n + (pl.BlockSpec(memory_space=pltpu.VMEM),),
            input_output_aliases={i: 2 + i for i in range(n)}, scratch_shapes=list(scratch),
            compiler_params=pltpu.CompilerParams(has_side_effects=EFFECT, collective_id=barrier_id),
        )(*[pltpu.with_memory_space_constraint(a, pltpu.HBM) for a in (*arrays, *sources)])
        self.sems, self.arrays, self.others, self.token = outs[:2], outs[2:2 + n_own], outs[2 + n_own:2 + n], outs[-1]

    def wait(self, after):
        n, plan = self.n, self.plan

        def body(*refs):
            send_sems, recv_sems = refs[n], refs[n + 1]
            for k, (src, dst, to) in enumerate(plan(refs[:n])):
                cp = pltpu.make_async_remote_copy(src_ref=src, dst_ref=dst, send_sem=send_sems.at[k],
                                                  recv_sem=recv_sems.at[k], device_id=to, device_id_type=MESH)
                cp.wait_send()
                cp.wait_recv()

        return pl.pallas_call(
            body, name=self.name + "_wait", out_shape=tuple(pltpu.HBM(a.shape, a.dtype) for a in self.arrays),
            in_specs=(HBM,) * n + (SEM, SEM, ANY), out_specs=(HBM,) * n, input_output_aliases={i: i for i in range(n)},
            compiler_params=pltpu.CompilerParams(has_side_effects=EFFECT),
        )(*self.arrays, *self.sems, after)


COLS_SHARDED = (True, False, True, False)
HALF_SHAPES = [(D_MODEL // 2, IN_COLS), (D_MODEL, D_MODEL // 2), (D_MODEL // 2, D_FF), (D_FF, D_MODEL // 2)]
PIECE_SHAPES = [(D_MODEL // 2, IN_COLS // N_CHIPS), (D_MODEL // N_CHIPS, D_MODEL // 2),
                (D_MODEL // 2, D_FF // N_CHIPS), (D_FF // N_CHIPS, D_MODEL // 2)]


def _shard_view(kind, ref, chip):
    if COLS_SHARDED[kind]:
        n = ref.shape[1] // N_CHIPS
        return ref.at[:, pl.ds(chip * n, n)]
    n = ref.shape[0] // N_CHIPS
    return ref.at[pl.ds(chip * n, n), :]


def _half_view(kind, ref, h):
    if COLS_SHARDED[kind]:
        n = ref.shape[0] // 2
        return ref.at[pl.ds(h * n, n), :]
    n = ref.shape[1] // 2
    return ref.at[:, pl.ds(h * n, n)]


def _plan(count):
    def mark(fn):
        fn.count = count
        return fn
    return mark


def _shard_rows_view(kind, ref, chip, part, n_parts):
    if COLS_SHARDED[kind]:
        m, n = ref.shape[0] // n_parts, ref.shape[1] // N_CHIPS
        return ref.at[pl.ds(part * m, m), pl.ds(chip * n, n)]
    m = ref.shape[0] // N_CHIPS // n_parts
    return ref.at[pl.ds((n_parts * chip + part) * m, m), :]


def _shard_half_view(kind, ref, chip, h):
    return _shard_rows_view(kind, ref, chip, h, 2)


def _gather_over_ici(kinds, weights):
    @_plan(2 * len(kinds))
    def plan(refs):
        x, y, c, me = _place()
        mine = [_shard_half_view(kind, ref, me, c) for kind, ref in zip(kinds, refs)]
        return [(v, v, to) for v in mine for to in ((1 - x, y, c), (x, 1 - y, c))]

    return _Split("gather_ici_" + "".join(map(str, kinds)), tuple(weights), plan, peers="neighbours")


def _relay_over_ici(kinds, weights, others=()):
    @_plan(2 * len(kinds))
    def plan(refs):
        x, y, c, _ = _place()
        x_nbr, y_nbr = 2 * (1 - x) + y, 2 * x + (1 - y)
        out = []
        for kind, ref in zip(kinds, refs):
            first, second = (_shard_rows_view(kind, ref, chip, 2 * c + q, 4) for q, chip in ((0, x_nbr), (1, y_nbr)))
            out += [(first, first, (x, 1 - y, c)), (second, second, (1 - x, y, c))]
        return out

    return _Split("relay_ici_" + "".join(map(str, kinds)), tuple(weights), plan, others, peers="neighbours")


def _gather_w_in_over_ici(shard, conv4):
    rows, cols = shard.shape

    @_plan(6)
    def plan(refs):
        x, y, c, me = _place()
        half, conv = _shard_half_view(0, refs[0], me, c), refs[1].at[me]
        return [(v, v, (px, py, c)) for v in (half, conv) for px, py in _other_chips(x, y)]

    def prepare(refs, sources, scratch):
        wide, narrow, sems = scratch
        _, _, _, me = _place()
        load = pltpu.make_async_copy(sources[0], wide, sems.at[0])
        load.start()
        load.wait()
        narrow[...] = wide[...].astype(narrow.dtype)
        store = pltpu.make_async_copy(narrow, _shard_view(0, refs[0], me), sems.at[1])
        store.start()
        store.wait()

    return _Split("gather_w_in_ici", (lax.empty((rows, cols * N_CHIPS), BF16), conv4), plan, peers="chips",
                  prepare=prepare, sources=(shard,),
                  scratch=(pltpu.VMEM((rows, cols), F32), pltpu.VMEM((rows, cols), BF16), pltpu.SemaphoreType.DMA((2,))))


def _gather_over_d2d(kinds, weights):
    @_plan(3 * len(kinds))
    def plan(refs):
        x, y, c, _ = _place()
        got = [_shard_half_view(kind, ref, 2 * px + py, c) for kind, ref in zip(kinds, refs)
               for px, py in _other_chips(x, y)]
        return [(v, v, (x, y, 1 - c)) for v in got]

    return _Split("gather_d2d_" + "".join(map(str, kinds)), tuple(weights), plan, peers="sibling")


def _swap_halves(kinds, grads):
    @_plan(len(kinds))
    def plan(refs):
        x, y, c, _ = _place()
        return [(_half_view(kind, g, 1 - c), land, (x, y, 1 - c))
                for kind, g, land in zip(kinds, refs[:len(kinds)], refs[len(kinds):])]

    lands = [lax.empty(HALF_SHAPES[kind], g.dtype) for kind, g in zip(kinds, grads)]
    return _Split("swap_halves_" + "".join(map(str, kinds)), (*grads, *lands), plan, peers="sibling")


def _block_rows(cols, elements):
    return 1 << ((elements // cols).bit_length() - 1)


def _grid_steps(shapes, elements):
    rows, cols = max(shapes, key=lambda shape: shape[0] * shape[1])
    return rows // min(rows, _block_rows(cols, elements))


def _add_half(name, kinds, grads, recvs, core):
    n = len(kinds)
    shapes = [recv.shape for recv in recvs]
    nb = _grid_steps(shapes, 1 << 20)

    def body(c_ref, *refs):
        for g_ref, r_ref, o_ref in zip(refs[:n], refs[n:2 * n], refs[2 * n:]):
            o_ref[...] = (g_ref[...].astype(F32) + r_ref[...].astype(F32)).astype(o_ref.dtype)

    own = [pl.BlockSpec((rows // nb, cols), (lambda i, c_ref: (c_ref[0] * nb + i, 0)) if COLS_SHARDED[k] else
                        (lambda i, c_ref: (i, c_ref[0]))) for k, (rows, cols) in zip(kinds, shapes)]
    blocks = [pl.BlockSpec((rows // nb, cols), lambda i, c_ref: (i, 0)) for rows, cols in shapes]
    return pl.pallas_call(
        body, name=name,
        grid_spec=pltpu.PrefetchScalarGridSpec(
            num_scalar_prefetch=1, grid=(nb,), in_specs=own + blocks, out_specs=blocks),
        out_shape=[jax.ShapeDtypeStruct(shape, BF16) for shape in shapes],
        compiler_params=_params("parallel"),
    )(core, *grads, *recvs)


def _exchange_pieces(kinds, halves, pack=None):
    n_p, n = N_CHIPS - 1, len(kinds)

    @_plan(n_p * n + (0 if pack is None else N_DEV - 1))
    def plan(refs):
        x, y, c, _ = _place()
        copies = []
        if pack is not None:
            me = 4 * x + 2 * y + c
            peers = [((1 - x) if m & 4 else x, (1 - y) if m & 2 else y, (1 - c) if m & 1 else c) for m in range(1, N_DEV)]
            copies += [(refs[2 * n], refs[2 * n + 1].at[me], peer) for peer in peers]
        return copies + [(_shard_view(kind, half, 2 * px + py), land.at[j], (px, py, c))
                         for j, (px, py) in enumerate(_other_chips(x, y))
                         for kind, half, land in zip(kinds, refs[:n], refs[n:2 * n])]

    lands = [lax.empty((n_p,) + PIECE_SHAPES[kind], BF16) for kind in kinds]
    small = () if pack is None else (pack, lax.empty((N_DEV,) + pack.shape, F32))
    return _Split("exchange_pieces_" + "".join(map(str, kinds)), (*halves, *lands, *small), plan,
                  peers="chips" if pack is None else None)


def _sum_pieces(name, kinds, halves, slots, place, after):
    n, n_p = len(kinds), N_CHIPS - 1
    shapes = [slot.shape[1:] for slot in slots]
    nb = _grid_steps(shapes, 1 << 18)

    def body(s_ref, *refs):
        for own_ref, slot_ref, o_ref in zip(refs[:n], refs[n:2 * n], refs[2 * n + 1:]):
            total = own_ref[...].astype(F32)
            for j in range(n_p):
                total = total + slot_ref[j].astype(F32)
            o_ref[...] = total

    own, out, shards = [], [], []
    for k, (rows, cols) in zip(kinds, shapes):
        if COLS_SHARDED[k]:
            own_map, out_map, shard = (lambda i, s: (i, s[0])), (lambda i, s: (s[1] * nb + i, 0)), (2 * rows, cols)
        else:
            own_map, out_map, shard = (lambda i, s: (s[0] * nb + i, 0)), (lambda i, s: (i, s[1])), (rows, 2 * cols)
        own.append(pl.BlockSpec((rows // nb, cols), own_map))
        out.append(pl.BlockSpec((rows // nb, cols), out_map))
        shards.append(jax.ShapeDtypeStruct(shard, F32))
    landed = [pl.BlockSpec((n_p, rows // nb, cols), lambda i, s: (0, i, 0)) for rows, cols in shapes]
    return pl.pallas_call(
        body, name=name,
        grid_spec=pltpu.PrefetchScalarGridSpec(
            num_scalar_prefetch=1, grid=(nb,), in_specs=own + landed + [ANY], out_specs=out),
        out_shape=shards,
        compiler_params=_params("parallel"),
    )(place, *halves, *slots, after)


def _join_halves(kinds, shards):
    @_plan(len(kinds))
    def plan(refs):
        x, y, c, _ = _place()
        return [(_half_view(kind, g, c), _half_view(kind, g, c), (x, y, 1 - c)) for kind, g in zip(kinds, refs)]

    return _Split("join_halves_" + "".join(map(str, kinds)), tuple(shards), plan, peers="sibling")


N_DEV = 8


def _sum_shared(pack, land, device):
    def body(d_ref, p_ref, l_ref, o_ref):
        me = d_ref[0]
        total = jnp.where(me == 0, p_ref[...], l_ref[0])
        for d in range(1, N_DEV):
            total = total + jnp.where(me == d, p_ref[...], l_ref[d])
        o_ref[...] = total

    return pl.pallas_call(
        body, name="sum_shared",
        grid_spec=pltpu.PrefetchScalarGridSpec(
            num_scalar_prefetch=1, grid=(1,),
            in_specs=[pl.BlockSpec(pack.shape, lambda i, d: (0, 0)), pl.BlockSpec(land.shape, lambda i, d: (0, 0, 0))],
            out_specs=pl.BlockSpec(pack.shape, lambda i, d: (0, 0))),
        out_shape=jax.ShapeDtypeStruct(pack.shape, F32),
    )(device, pack, land)


def _adamw(name, weights, after=None):
    shapes = [w.shape for w, _, _, _ in weights]
    nb = _grid_steps(shapes, 1 << 18)
    extra = [] if after is None else [after]
    n_in = 4 * len(weights) + len(extra)

    def body(*refs):
        for k in range(len(weights)):
            w_ref, g_ref, m_ref, v_ref = refs[4 * k:4 * k + 4]
            go_ref, d_ref, nm_ref, nv_ref = refs[n_in + 4 * k:n_in + 4 * k + 4]
            g = g_ref[...]
            go_ref[...] = g
            d_ref[...], nm_ref[...], nv_ref[...] = _adam_step(w_ref[...], g, m_ref[...], v_ref[...])

    blocks = [pl.BlockSpec((rows // nb, cols), lambda i: (i, 0)) for rows, cols in shapes for _ in range(4)]
    outs = pl.pallas_call(
        body, name=name, grid=(nb,), in_specs=blocks + [ANY] * len(extra), out_specs=blocks,
        out_shape=[jax.ShapeDtypeStruct(shape, F32) for shape in shapes for _ in range(4)],
        compiler_params=_params("parallel"),
    )(*[a for group in weights for a in group], *extra)
    return [outs[4 * k:4 * k + 4] for k in range(len(weights))]


def _adam_step(w, g, m, v):
    nm = ADAM_B1 * m + (1.0 - ADAM_B1) * g
    nv = ADAM_B2 * v + (1.0 - ADAM_B2) * jnp.square(g)
    m_hat = nm * (1.0 / (1.0 - ADAM_B1 ** ADAM_STEP))
    v_hat = nv * (1.0 / (1.0 - ADAM_B2 ** ADAM_STEP))
    return -ADAM_LR * (m_hat / (jnp.sqrt(v_hat) + ADAM_EPS) + ADAM_WD * w), nm, nv


def _adamw_small(tot, chip, weights, ms, vs, after):
    n, half = len(weights), D_MODEL // 2

    def body(chip_ref, tot_ref, *refs):
        ins, outs = refs[:3 * n], refs[3 * n + 1:]
        tot = tot_ref[...]
        conv_all = jnp.concatenate([tot[5:6, half:], tot[6:7, :half], tot[6:7, half:]], axis=0)
        conv = sum(jnp.where(chip_ref[0] == s, conv_all[:, s * LANES:(s + 1) * LANES], 0.0) for s in range(N_CHIPS))
        grads = [jnp.concatenate([tot[4:5, :half], tot[4:5, half:]], axis=0), tot[5:6, :half], conv,
                 tot[0:1], tot[1:2], tot[2:3], tot[3:4]]
        for k, g in enumerate(grads):
            delta, nm, nv = _adam_step(ins[k][...], g, ins[n + k][...], ins[2 * n + k][...])
            outs[k][...], outs[n + k][...], outs[2 * n + k][...], outs[3 * n + k][...] = g, delta, nm, nv
        outs[4 * n][...] = tot[7:8, 0:1]

    whole = lambda a: pl.BlockSpec(a.shape, lambda i, s: (0,) * a.ndim)
    arrays = (*weights, *ms, *vs)
    loss = jax.ShapeDtypeStruct((1, 1), F32)
    return pl.pallas_call(
        body, name="adamw_small",
        grid_spec=pltpu.PrefetchScalarGridSpec(
            num_scalar_prefetch=1, grid=(1,), in_specs=[whole(tot)] + [whole(a) for a in arrays] + [ANY],
            out_specs=[whole(a) for a in weights] * 4 + [whole(loss)]),
        out_shape=[jax.ShapeDtypeStruct(a.shape, F32) for a in weights] * 4 + [loss],
    )(chip, tot, *arrays, after)


def kernel(x, w_in, lb_logits, gate_norm_w, conv_w, w_out, ln1_g, ln1_b, w_ff1, w_ff2, ln2_g, ln2_b, loss_target, m_w_in, m_lb_logits, m_gate_norm_w, m_conv_w, m_w_out, m_ln1_g, m_ln1_b, m_w_ff1, m_w_ff2, m_ln2_g, m_ln2_b, v_w_in, v_lb_logits, v_gate_norm_w, v_conv_w, v_w_out, v_ln1_g, v_ln1_b, v_w_ff1, v_w_ff2, v_ln2_g, v_ln2_b):
    xs, tgt = x[0], loss_target[0]
    chip = 2 * lax.axis_index("x") + lax.axis_index("y")
    core = lax.axis_index("c").astype(jnp.int32).reshape(1)
    chip1 = chip.astype(jnp.int32).reshape(1)
    place = jnp.concatenate([chip1, core])

    conv4 = lax.dynamic_update_slice(jnp.zeros((N_CHIPS,) + conv_w.shape[1:], F32), conv_w, (chip, 0, 0))
    ici_in = _gather_w_in_over_ici(w_in[0], conv4)
    rest = (1, 2, 3)
    ici_rest = _gather_over_ici(rest, (_place_shard("place_w_out", w_out[0], chip1, False, after=ici_in.token),
                                       _place_shard("place_w_ff1", w_ff1[0], chip1, True, after=ici_in.token),
                                       _place_shard("place_w_ff2", w_ff2[0], chip1, False, after=ici_in.token)))
    wb_in, cv4 = ici_in.wait(ici_rest.token)
    d2d_in = _gather_over_d2d((0,), (wb_in,))
    wb_in, = d2d_in.wait(d2d_in.token)
    conv_full = cv4.transpose(1, 0, 2).reshape(3, CONV_WIDTH)

    proj, bcu, xb, cat_c = _in_proj(xs, wb_in, conv_full, ici_rest.token)
    relay_rest = _relay_over_ici(rest, ici_rest.wait(proj))
    o, states = _hgrn_fwd(proj, lb_logits, relay_rest.token)
    d2d_rest = _gather_over_d2d(rest, relay_rest.wait(o))
    cat_h = _gate_fwd(proj, o, gate_norm_w, d2d_rest.token)
    wb_out, wb_ff1, wb_ff2 = d2d_rest.wait(cat_h)

    (h1b, r, da, dpre2b, dpre1, dcat, g_ln1_g, g_ln1_b, g_ln2_g, g_ln2_b, loss8, g_out_local) = _sublayers(
        cat_h, cat_c, xs, tgt, wb_out, wb_ff1, wb_ff2, ln1_g, ln1_b, ln2_g, ln2_b)

    names = ("w_in", "w_out", "w_ff1", "w_ff2")

    def named(prefix, kinds):
        return prefix + "".join("_" + names[k] for k in kinds)

    def add_halves(kinds, grads, lands):
        return _add_half(named("add_half", kinds), kinds, grads, lands, core)

    def sum_pieces(kinds, halves, lands, after):
        return _sum_pieces(named("sum_pieces", kinds), kinds, halves, lands, place, after)

    early = (1, 2, 3)
    g_ff2_local, dpc, g_conv = _dw_ff2(r, dpre2b, dcat, bcu, conv_full)
    swap_a = _swap_halves((1, 3), (g_out_local, g_ff2_local))
    g_ff1_local, do, dog, g_gnw = _dw_ff1(h1b, da, dcat, o, proj, gate_norm_w, swap_a.token)
    swap_b = _swap_halves((2,), (g_ff1_local,))
    swapped_a = swap_a.wait(swap_b.token)
    halves_a = add_halves((1, 3), swapped_a[:2], swapped_a[2:])
    swapped_b = swap_b.wait(halves_a[1])
    halves = (halves_a[0], *add_halves((2,), swapped_b[:1], swapped_b[1:]), halves_a[1])
    exch = _exchange_pieces(early, halves)
    dph, g_lbl = _hgrn_bwd(proj, do, states, lb_logits, exch.token)
    g_in_local, grad_x = _dw_in(xb, dph, dog, dpc, wb_in, dpre1, dph)

    late = (0,)
    swap = _swap_halves(late, (g_in_local,))
    exchanged = exch.wait(swap.token)
    pack = jnp.concatenate([
        g_ln1_g, g_ln1_b, g_ln2_g, g_ln2_b,
        jnp.concatenate([g_lbl[0:1], g_lbl[1:2]], axis=1),
        jnp.concatenate([g_gnw, g_conv[0:1]], axis=1),
        jnp.concatenate([g_conv[1:2], g_conv[2:3]], axis=1),
        jnp.concatenate([loss8[0:1], jnp.zeros((1, D_MODEL - LANES), F32)], axis=1)], axis=0)
    join_a = _join_halves((2,), sum_pieces((2,), exchanged[1:2], exchanged[4:5], swap.token))
    swapped = swap.wait(join_a.token)
    exch = _exchange_pieces(late, add_halves(late, swapped[:1], swapped[1:]), pack)
    join_b = _join_halves((1, 3), sum_pieces((1, 3), exchanged[0:3:2], exchanged[3:6:2], exch.token))
    g_w_ff1, = join_a.wait(join_b.token)
    (g_w_ff1, d_ff1, nm_ff1, nv_ff1), = _adamw("adamw_w_ff1", [(w_ff1[0], g_w_ff1, m_w_ff1[0], v_w_ff1[0])])
    g_w_out, g_w_ff2 = join_b.wait(d_ff1)
    (g_w_ff2, d_ff2, nm_ff2, nv_ff2), (g_w_out, d_out, nm_out, nv_out) = _adamw(
        "adamw_w_ff2_w_out", [(w_ff2[0], g_w_ff2, m_w_ff2[0], v_w_ff2[0]), (w_out[0], g_w_out, m_w_out[0], v_w_out[0])])
    exchanged = exch.wait(d_out)
    tot = _sum_shared(exchanged[2], exchanged[3], 2 * chip1 + core)
    join = _join_halves(late, sum_pieces(late, exchanged[:1], exchanged[1:2], tot))
    small = ("lb_logits", "gate_norm_w", "conv_w", "ln1_g", "ln1_b", "ln2_g", "ln2_b")
    small_out = _adamw_small(
        tot, chip1, (lb_logits, gate_norm_w, conv_w[0], ln1_g, ln1_b, ln2_g, ln2_b),
        (m_lb_logits, m_gate_norm_w, m_conv_w[0], m_ln1_g, m_ln1_b, m_ln2_g, m_ln2_b),
        (v_lb_logits, v_gate_norm_w, v_conv_w[0], v_ln1_g, v_ln1_b, v_ln2_g, v_ln2_b), join.token)
    g_w_in, = join.wait(small_out[0])
    (g_w_in, d_in, nm_in, nv_in), = _adamw("adamw_w_in", [(w_in[0], g_w_in, m_w_in[0], v_w_in[0])])
    loss = small_out[4 * len(small)][0, 0]

    def results(n_kind, large):
        out = dict(zip(small, small_out[n_kind * len(small):(n_kind + 1) * len(small)]))
        out["conv_w"] = out["conv_w"][None]
        out.update({name: a[None] for name, a in zip(("w_in", "w_out", "w_ff1", "w_ff2"), large)})
        return [out[name] for name in ("w_in", "lb_logits", "gate_norm_w", "conv_w", "w_out", "ln1_g", "ln1_b",
                                       "w_ff1", "w_ff2", "ln2_g", "ln2_b")]

    return (loss, grad_x[None], *results(0, (g_w_in, g_w_out, g_w_ff1, g_w_ff2)),
            *results(1, (d_in, d_out, d_ff1, d_ff2)), *results(2, (nm_in, nm_out, nm_ff1, nm_ff2)),
            *results(3, (nv_in, nv_out, nv_ff1, nv_ff2)))
```

```python
import jax
import jax.numpy as jnp
from jax import lax
from jax.experimental import pallas as pl
from jax.experimental.pallas import tpu as pltpu

F32 = jnp.float32
BF16 = jnp.bfloat16
MXU_DTYPE = jnp.bfloat16

D_MODEL = 1024
HGRN_WIDTH = 512
HEAD_DIM = 128
N_HEADS = 4
CONV_WIDTH = 512
CHUNK = 64
D_FF = 4096
IN_COLS = 3584
GROUP = 512
N_GROUPS = IN_COLS // GROUP
ALPHA = 2.0 ** 0.25
EPS = 1e-5
N_CHIPS = 4
ADAM_LR, ADAM_B1, ADAM_B2, ADAM_EPS, ADAM_WD, ADAM_STEP = 0.001, 0.9, 0.999, 1e-08, 0.01, 10

LANES = 128
SUBLANES = 8
VMEM_LIMIT = 56 * 1024 * 1024
FF_BLOCK = 1024
N_FF = D_FF // FF_BLOCK
GATE_STRIP = 64

NN = (((1,), (0,)), ((), ()))
NT = (((1,), (1,)), ((), ()))
TN = (((0,), (0,)), ((), ()))
MESH = pl.DeviceIdType.MESH
ANY = pl.BlockSpec(memory_space=pl.ANY)


def _dot(a, b, dims):
    return lax.dot_general(a.astype(MXU_DTYPE), b.astype(MXU_DTYPE), dims, preferred_element_type=F32)


def _dot_exact(ones, v):
    ones = ones.astype(jnp.bfloat16)
    hi = v.astype(jnp.bfloat16)
    rest = v - hi.astype(F32)
    mid = rest.astype(jnp.bfloat16)
    low = (rest - mid.astype(F32)).astype(jnp.bfloat16)
    return sum(lax.dot_general(ones, part, NN, preferred_element_type=F32) for part in (hi, mid, low))


def _params(*sem):
    return pltpu.CompilerParams(dimension_semantics=sem, vmem_limit_bytes=VMEM_LIMIT)


def _resident(shape):
    return pl.BlockSpec(shape, lambda *_: (0,) * len(shape), pipeline_mode=pl.Buffered(1))


def _sigmoid(v):
    return 1.0 / (1.0 + jnp.exp(-v))


def _lower_bound(lbl):
    m = jnp.max(lbl, axis=0, keepdims=True)
    e = jnp.exp(lbl - m)
    s = e / jnp.sum(e, axis=0, keepdims=True)
    return s[0:1, :], s[1:2, :]


def _heads(v):
    return [v[:, h * HEAD_DIM:(h + 1) * HEAD_DIM] for h in range(N_HEADS)]


def _per_head(fn, *arrays):
    return jnp.concatenate([fn(*parts) for parts in zip(*map(_heads, arrays))], axis=1)


def _in_proj(x, w_in, conv_w, after):
    t = x.shape[0]
    tm = min(t, 512)

    def body(x_ref, w_ref, cw_ref, after_ref, o_ref, bcu_ref, xb_ref, y_ref, zbuf):
        @pl.when(pl.program_id(0) == 0)
        def _():
            zbuf[tm:tm + SUBLANES, :] = jnp.zeros((SUBLANES, CONV_WIDTH), F32)

        xb = x_ref[...].astype(xb_ref.dtype)
        xb_ref[...] = xb
        group = lambda g: _dot(xb, w_ref[:, g * GROUP:(g + 1) * GROUP], NN)
        for g in range(4):
            o_ref[g] = group(g)
        b_gate, c_gate, u = group(4), group(5), group(6)
        for n, part in enumerate((b_gate, c_gate, u)):
            bcu_ref[n] = part.astype(bcu_ref.dtype)
        zbuf[0:SUBLANES, :] = zbuf[tm:tm + SUBLANES, :]
        zbuf[SUBLANES:SUBLANES + tm, :] = c_gate * u
        cw = cw_ref[...]
        at = lambda shift: zbuf[shift:shift + tm, :]
        conv = cw[2:3, :] * at(SUBLANES) + cw[1:2, :] * at(SUBLANES - 1) + cw[0:1, :] * at(SUBLANES - 2)
        y_ref[...] = (b_gate * conv).astype(y_ref.dtype)

    return pl.pallas_call(
        body, name="in_proj", grid=(t // tm,),
        in_specs=[pl.BlockSpec((tm, D_MODEL), lambda i: (i, 0)), _resident((D_MODEL, IN_COLS)),
                  pl.BlockSpec((3, CONV_WIDTH), lambda i: (0, 0)), ANY],
        out_specs=[pl.BlockSpec((4, tm, GROUP), lambda i: (0, i, 0)), pl.BlockSpec((3, tm, GROUP), lambda i: (0, i, 0)),
                   pl.BlockSpec((tm, D_MODEL), lambda i: (i, 0)), pl.BlockSpec((tm, CONV_WIDTH), lambda i: (i, 0))],
        out_shape=[jax.ShapeDtypeStruct((4, t, GROUP), F32), jax.ShapeDtypeStruct((3, t, GROUP), BF16),
                   jax.ShapeDtypeStruct((t, D_MODEL), BF16), jax.ShapeDtypeStruct((t, CONV_WIDTH), BF16)],
        scratch_shapes=[pltpu.VMEM((tm + SUBLANES, CONV_WIDTH), F32)],
        compiler_params=_params("arbitrary"),
    )(x, w_in, conv_w, after)


def _gates(fp, lb):
    sig = _sigmoid(fp)
    f = lb + (1.0 - lb) * sig
    return sig, f, jnp.log(f), 1.0 - f


def _chunk_masks():
    row = lax.broadcasted_iota(jnp.int32, (CHUNK, CHUNK), 0)
    col = lax.broadcasted_iota(jnp.int32, (CHUNK, CHUNK), 1)
    return row >= col, row <= col


def _hgrn_fwd(proj, lb_logits, after):
    t = proj.shape[1]
    tb = min(t, 512)
    ncb = tb // CHUNK

    def body(q_ref, f_ref, v_ref, lbl_ref, after_ref, o_ref, st_ref, s_scr):
        @pl.when(pl.program_id(0) == 0)
        def _():
            s_scr[...] = jnp.zeros_like(s_scr)

        lb, _ = _lower_bound(lbl_ref[...])
        causal, _ = _chunk_masks()

        every = range(ncb)
        rows = [slice(c * CHUNK, (c + 1) * CHUNK) for c in every]
        q, v = [q_ref[r, :] for r in rows], [v_ref[r, :] for r in rows]
        gates = [_gates(f_ref[r, :], lb) for r in rows]
        k = [gt[3] for gt in gates]
        b = [_dot_exact(causal, gt[2]) for gt in gates]
        mid, last = [x[CHUNK // 2:CHUNK // 2 + 1, :] for x in b], [x[CHUNK - 1:CHUNK, :] for x in b]
        qt = [q[c] * jnp.exp(b[c] - mid[c]) for c in every]
        kt = [k[c] * jnp.exp(mid[c] - b[c]) for c in every]
        qi = [q[c] * jnp.exp(b[c]) for c in every]
        ks = [k[c] * jnp.exp(last[c] - b[c]) for c in every]
        dec = [jnp.exp(x) for x in last]
        scores = [[jnp.where(causal, _dot(a, b_, NT), 0.0) for a, b_ in zip(_heads(qt[c]), _heads(kt[c]))] for c in every]
        intra = [[_dot(s, v_h, NN) for s, v_h in zip(scores[c], _heads(v[c]))] for c in every]
        update = [_per_head(lambda v_h, ks_h: _dot(v_h, ks_h, TN), v[c], ks[c]) for c in every]

        st = s_scr[...]
        states = []
        for c in every:
            states.append(st)
            st_ref[c] = st
            st = dec[c] * st + update[c]
        s_scr[...] = st

        o_ref[...] = jnp.concatenate(
            [jnp.concatenate([i_h + _dot(qi_h, st_h, NT) for i_h, qi_h, st_h in
                              zip(intra[c], _heads(qi[c]), _heads(states[c]))], axis=1) for c in every], axis=0)

    grp = lambda g: pl.BlockSpec((None, tb, GROUP), lambda i: (g, i, 0))
    return pl.pallas_call(
        body, name="hgrn_fwd", grid=(t // tb,),
        in_specs=[grp(0), grp(1), grp(2), pl.BlockSpec((2, HGRN_WIDTH), lambda i: (0, 0)), ANY],
        out_specs=[pl.BlockSpec((tb, HGRN_WIDTH), lambda i: (i, 0)),
                   pl.BlockSpec((ncb, HEAD_DIM, HGRN_WIDTH), lambda i: (i, 0, 0))],
        out_shape=[jax.ShapeDtypeStruct((t, HGRN_WIDTH), F32),
                   jax.ShapeDtypeStruct((t // CHUNK, HEAD_DIM, HGRN_WIDTH), F32)],
        scratch_shapes=[pltpu.VMEM((HEAD_DIM, HGRN_WIDTH), F32)],
        compiler_params=_params("arbitrary"),
    )(proj, proj, proj, lb_logits, after)


def _gate_fwd(proj, o, gate_norm_w, after):
    t = proj.shape[1]
    tb = min(t, 1024)

    def body(o_ref, og_ref, gnw_ref, after_ref, out_ref):
        gnw = gnw_ref[...]
        for s in range(tb // GATE_STRIP):
            rows = slice(s * GATE_STRIP, (s + 1) * GATE_STRIP)
            og = og_ref[rows, :]
            on = _per_head(lambda o_h: o_h * lax.rsqrt(jnp.mean(o_h * o_h, axis=-1, keepdims=True) + EPS), o_ref[rows, :])
            out_ref[rows, :] = (on * gnw * (og * _sigmoid(og))).astype(out_ref.dtype)

    tile = pl.BlockSpec((tb, GROUP), lambda i: (i, 0))
    return pl.pallas_call(
        body, name="gate_fwd", grid=(t // tb,),
        in_specs=[tile, pl.BlockSpec((None, tb, GROUP), lambda i: (3, i, 0)), pl.BlockSpec((1, GROUP), lambda i: (0, 0)), ANY],
        out_specs=tile,
        out_shape=jax.ShapeDtypeStruct((t, HGRN_WIDTH), BF16),
        compiler_params=_params("parallel"),
    )(o, proj, gate_norm_w, after)


def _ln_bwd(dy, xhat, rstd, g):
    dxhat = dy * g
    m1 = jnp.mean(dxhat, axis=-1, keepdims=True)
    m2 = jnp.mean(dxhat * xhat, axis=-1, keepdims=True)
    return rstd * (dxhat - m1 - xhat * m2)


def _layer_norm(pre):
    xc = pre - jnp.mean(pre, axis=-1, keepdims=True)
    rstd = lax.rsqrt(jnp.mean(xc * xc, axis=-1, keepdims=True) + EPS)
    return xc * rstd, rstd


def _sublayers(cat_h, cat_c, x, target, w_out, w_ff1, w_ff2, g1, b1, g2, b2):
    t = x.shape[0]
    tm = min(t, 256)

    def body(ch_ref, cc_ref, x_ref, tg_ref, wo_ref, w1_ref, w2_ref, g1_ref, b1_ref, g2_ref, b2_ref,
             h1_ref, r_ref, da_ref, dp2b_ref, dp1_ref, dcat_ref, dg1_ref, db1_ref, dg2_ref, db2_ref, loss_ref, gwo_ref,
             gwo_acc, gwo_narrow, sem):
        @pl.when(pl.program_id(0) == 0)
        def _():
            for ref in (dg1_ref, db1_ref, dg2_ref, db2_ref, loss_ref, gwo_acc):
                ref[...] = jnp.zeros_like(ref)

        mix = _dot(ch_ref[...], wo_ref[0:GROUP, :], NN) + _dot(cc_ref[...], wo_ref[GROUP:2 * GROUP, :], NN)
        xhat1, rstd1 = _layer_norm(ALPHA * x_ref[...] + mix)
        h1 = xhat1 * g1_ref[...] + b1_ref[...]
        h1b = h1.astype(h1_ref.dtype)
        h1_ref[...] = h1b
        mlp = jnp.zeros((tm, D_MODEL), F32)
        for j in range(N_FF):
            cols = slice(j * FF_BLOCK, (j + 1) * FF_BLOCK)
            r = jnp.square(jnp.maximum(_dot(h1b, w1_ref[:, cols], NN), 0.0)).astype(r_ref.dtype)
            r_ref[:, cols] = r
            mlp = mlp + _dot(r, w2_ref[cols, :], NN)
        xhat2, rstd2 = _layer_norm(ALPHA * h1 + mlp)
        err = xhat2 * g2_ref[...] + b2_ref[...] - tg_ref[...]
        loss_ref[...] += 0.5 * jnp.sum(jnp.mean(err * err, axis=-1, keepdims=True))
        dy = err * (1.0 / D_MODEL)
        dg2_ref[...] += jnp.sum(dy * xhat2, axis=0, keepdims=True)
        db2_ref[...] += jnp.sum(dy, axis=0, keepdims=True)
        dp2 = _ln_bwd(dy, xhat2, rstd2, g2_ref[...])
        dp2b = dp2.astype(dp2b_ref.dtype)
        dp2b_ref[...] = dp2b
        back = jnp.zeros((tm, D_MODEL), F32)
        for j in range(N_FF):
            cols = slice(j * FF_BLOCK, (j + 1) * FF_BLOCK)
            dr = _dot(dp2b, w2_ref[cols, :], NT)
            da = (dr * (2.0 * jnp.sqrt(r_ref[:, cols].astype(F32)))).astype(da_ref.dtype)
            da_ref[:, cols] = da
            back = back + _dot(da, w1_ref[:, cols], NT)
        dh1 = ALPHA * dp2 + back
        dg1_ref[...] += jnp.sum(dh1 * xhat1, axis=0, keepdims=True)
        db1_ref[...] += jnp.sum(dh1, axis=0, keepdims=True)
        dp1 = _ln_bwd(dh1, xhat1, rstd1, g1_ref[...])
        dp1b = dp1.astype(MXU_DTYPE)
        dp1_ref[...] = dp1
        dcat_ref[...] = _dot(dp1b, wo_ref[...], NT)
        gwo_acc[0:GROUP, :] += _dot(ch_ref[...], dp1b, TN)
        gwo_acc[GROUP:2 * GROUP, :] += _dot(cc_ref[...], dp1b, TN)

        @pl.when(pl.program_id(0) == pl.num_programs(0) - 1)
        def _():
            gwo_narrow[...] = gwo_acc[...].astype(gwo_narrow.dtype)
            copy = pltpu.make_async_copy(gwo_narrow, gwo_ref, sem.at[0])
            copy.start()
            copy.wait()

    row = pl.BlockSpec((tm, D_MODEL), lambda i: (i, 0))
    wide = pl.BlockSpec((tm, D_FF), lambda i: (i, 0))
    vec = pl.BlockSpec((1, D_MODEL), lambda i: (0, 0))
    narrow = lambda dtype: jax.ShapeDtypeStruct((t, D_MODEL), dtype)
    return pl.pallas_call(
        body, name="sublayers", grid=(t // tm,),
        in_specs=[pl.BlockSpec((tm, GROUP), lambda i: (i, 0)), pl.BlockSpec((tm, GROUP), lambda i: (i, 0)), row, row,
                  _resident((D_MODEL, D_MODEL)),
                  _resident((D_MODEL, D_FF)), _resident((D_FF, D_MODEL)), vec, vec, vec, vec],
        out_specs=[row, wide, wide, row, row, row, vec, vec, vec, vec,
                   pl.BlockSpec((SUBLANES, LANES), lambda i: (0, 0)), ANY],
        out_shape=[narrow(BF16), jax.ShapeDtypeStruct((t, D_FF), BF16), jax.ShapeDtypeStruct((t, D_FF), BF16),
                   narrow(BF16), narrow(F32), narrow(F32)]
                  + [jax.ShapeDtypeStruct((1, D_MODEL), F32)] * 4
                  + [jax.ShapeDtypeStruct((SUBLANES, LANES), F32), jax.ShapeDtypeStruct((D_MODEL, D_MODEL), BF16)],
        scratch_shapes=[pltpu.VMEM((D_MODEL, D_MODEL), F32), pltpu.VMEM((D_MODEL, D_MODEL), BF16),
                        pltpu.SemaphoreType.DMA((1,))],
        compiler_params=_params("arbitrary"),
    )(cat_h, cat_c, x, target, w_out, w_ff1, w_ff2, g1, b1, g2, b2)


def _hgrn_bwd(proj, do, states, lb_logits, after):
    t = proj.shape[1]
    tb = min(t, 512)
    ncb = tb // CHUNK
    nblk = t // tb

    def body(q_ref, f_ref, v_ref, do_ref, st_ref, lbl_ref, after_ref, dp_ref, dlbl_ref, ds_scr, dlb_scr):
        i = pl.program_id(0)

        @pl.when(i == 0)
        def _():
            ds_scr[...] = jnp.zeros_like(ds_scr)
            dlb_scr[...] = jnp.zeros_like(dlb_scr)

        lb, s1 = _lower_bound(lbl_ref[...])
        causal, anti = _chunk_masks()
        every = range(ncb)
        rows = [slice(c * CHUNK, (c + 1) * CHUNK) for c in every]
        q, v, do = ([ref[r, :] for r in rows] for ref in (q_ref, v_ref, do_ref))
        st = [st_ref[c] for c in every]
        gates = [_gates(f_ref[r, :], lb) for r in rows]
        sig, f, k = ([gt[n] for gt in gates] for n in (0, 1, 3))
        b = [_dot_exact(causal, gt[2]) for gt in gates]
        mid, last = [x[CHUNK // 2:CHUNK // 2 + 1, :] for x in b], [x[CHUNK - 1:CHUNK, :] for x in b]
        e_q = [jnp.exp(b[c] - mid[c]) for c in every]
        e_k = [jnp.exp(mid[c] - b[c]) for c in every]
        e_i = [jnp.exp(x) for x in b]
        e_s = [jnp.exp(last[c] - b[c]) for c in every]
        dec = [jnp.exp(x) for x in last]
        qt, kt, qi, ks = ([a[c] * e[c] for c in every] for a, e in ((q, e_q), (k, e_k), (q, e_i), (k, e_s)))

        def masked(a, b_):
            return [[jnp.where(causal, _dot(a_h, b_h, NT), 0.0) for a_h, b_h in zip(_heads(a[c]), _heads(b_[c]))]
                    for c in every]

        def with_scores(s, other, dims):
            return [jnp.concatenate([_dot(s_h, o_h, dims) for s_h, o_h in zip(s[c], _heads(other[c]))], axis=1)
                    for c in every]

        def per_head(dims, a, b_):
            return [_per_head(lambda a_h, b_h: _dot(a_h, b_h, dims), a[c], b_[c]) for c in every]

        scores, dscores = masked(qt, kt), masked(do, v)
        dqt, dkt, dv_intra = with_scores(dscores, kt, NN), with_scores(dscores, qt, TN), with_scores(scores, do, TN)
        dqi, update = per_head(NN, do, st), per_head(TN, do, qi)

        dst = ds_scr[...]
        dsts = [None] * ncb
        for c in reversed(every):
            dsts[c] = dst
            dst = dec[c] * dst + update[c]
        ds_scr[...] = dst

        dv_state, dks = per_head(NT, ks, dsts), per_head(NN, v, dsts)
        ddec = [jnp.sum(dsts[c] * st[c], axis=0, keepdims=True) for c in every]
        dq = [dqt[c] * e_q[c] + dqi[c] * e_i[c] for c in every]
        dk = [dkt[c] * e_k[c] + dks[c] * e_s[c] for c in every]
        db = [q[c] * dq[c] - k[c] * dk[c] for c in every]
        db_last = [jnp.sum(dks[c] * ks[c], axis=0, keepdims=True) + ddec[c] * dec[c] for c in every]
        dg = [_dot_exact(anti, db[c]) + db_last[c] for c in every]
        df = [dg[c] / f[c] - dk[c] for c in every]
        dlb_scr[...] += sum(jnp.sum(df[c] * (1.0 - sig[c]), axis=0, keepdims=True) for c in every)
        dfp = [df[c] * (1.0 - lb) * sig[c] * (1.0 - sig[c]) for c in every]
        dv = [dv_intra[c] + dv_state[c] for c in every]
        for n, parts in enumerate((dq, dfp, dv)):
            dp_ref[n] = jnp.concatenate(parts, axis=0).astype(dp_ref.dtype)

        @pl.when(i == nblk - 1)
        def _():
            dlb = dlb_scr[...]
            dlbl_ref[0:1, :] = dlb * lb * (1.0 - lb)
            dlbl_ref[1:2, :] = -dlb * lb * s1

    grp = lambda g: pl.BlockSpec((None, tb, GROUP), lambda i: (g, nblk - 1 - i, 0))
    vec = pl.BlockSpec((2, HGRN_WIDTH), lambda i: (0, 0))
    return pl.pallas_call(
        body, name="hgrn_bwd", grid=(nblk,),
        in_specs=[grp(0), grp(1), grp(2), pl.BlockSpec((tb, HGRN_WIDTH), lambda i: (nblk - 1 - i, 0)),
                  pl.BlockSpec((ncb, HEAD_DIM, HGRN_WIDTH), lambda i: (nblk - 1 - i, 0, 0)), vec, ANY],
        out_specs=[pl.BlockSpec((3, tb, HGRN_WIDTH), lambda i: (0, nblk - 1 - i, 0)), vec],
        out_shape=[jax.ShapeDtypeStruct((3, t, HGRN_WIDTH), BF16), jax.ShapeDtypeStruct((2, HGRN_WIDTH), F32)],
        scratch_shapes=[pltpu.VMEM((HEAD_DIM, HGRN_WIDTH), F32), pltpu.VMEM((1, HGRN_WIDTH), F32)],
        compiler_params=_params("arbitrary"),
    )(proj, proj, proj, do, states, lb_logits, after)


GRAD_TILE = 512
OUT_PARTS = 4


class _Side:
    def __init__(self, operands, in_specs, out_shape, out_specs, scratch, init, begin):
        self.operands, self.in_specs, self.out_shape, self.out_specs = operands, in_specs, out_shape, out_specs
        self.scratch, self.init, self.begin = scratch, init, begin


def _grad_w(name, operands, widths, shape, step, after=None, side=None):
    t = operands[0].shape[-2]
    tt = min(t, GRAD_TILE)
    n_in, n_steps = len(operands), t // tt
    in_specs = [pl.BlockSpec((tt, w), lambda k: (k, 0)) if a.ndim == 2 else
                pl.BlockSpec((a.shape[0], tt, w), lambda k: (0, k, 0)) for a, w in zip(operands, widths)]
    extra = [] if after is None else [after]
    s_in, s_out = (len(side.operands), len(side.out_shape)) if side else (0, 0)
    first_out = n_in + s_in + len(extra)

    def body(*refs):
        o_ref, side_outs = refs[first_out], refs[first_out + 1:first_out + 1 + s_out]
        acc, narrow, sem = refs[first_out + 1 + s_out:first_out + 4 + s_out]
        k = pl.program_id(0)

        @pl.when(k == 0)
        def _():
            acc[...] = jnp.zeros_like(acc)
            if side:
                side.init(side_outs)

        tick = side.begin(k, n_steps, refs[n_in:n_in + s_in], side_outs, refs[first_out + 4 + s_out:]) if side else None
        step(acc, *refs[:n_in], tick or (lambda j: None))

        @pl.when(k == n_steps - 1)
        def _():
            part = shape[0] // OUT_PARTS
            copies = []
            for p in range(OUT_PARTS):
                rows = pl.ds(p * part, part)
                narrow[rows, :] = acc[rows, :].astype(narrow.dtype)
                copies.append(pltpu.make_async_copy(narrow.at[rows, :], o_ref.at[rows, :], sem.at[p]))
                copies[-1].start()
            for cp in copies:
                cp.wait()

    outs = pl.pallas_call(
        body, name=name, grid=(n_steps,),
        in_specs=in_specs + (side.in_specs if side else []) + [ANY] * len(extra),
        out_specs=[ANY] + (side.out_specs if side else []),
        out_shape=[jax.ShapeDtypeStruct(shape, BF16)] + (side.out_shape if side else []),
        scratch_shapes=[pltpu.VMEM(shape, F32), pltpu.VMEM(shape, BF16), pltpu.SemaphoreType.DMA((OUT_PARTS,))]
                       + (side.scratch if side else []),
        compiler_params=_params("arbitrary"),
    )(*operands, *(side.operands if side else ()), *extra)
    return outs if side else outs[0]


def _dw_in(xb, dph, dog, dpc, w_in, dpre1, after):
    t = xb.shape[0]

    def step(acc, x_ref, dh_ref, dog_ref, dc_ref, tick):
        xv = x_ref[...]
        for g in range(N_GROUPS):
            part = dh_ref[g] if g < 3 else dog_ref[...] if g == 3 else dc_ref[g - 4]
            acc[:, g * GROUP:(g + 1) * GROUP] += _dot(xv, part, TN)
            tick(g, part)

    def begin(k, n_steps, ins, outs, scratch):
        w_ref, dp_ref = ins
        total = [ALPHA * dp_ref[...]]

        def tick(g, part):
            total[0] = total[0] + _dot(part, w_ref[:, g * GROUP:(g + 1) * GROUP], NT)
            if g == N_GROUPS - 1:
                outs[0][...] = total[0]

        return tick

    row = pl.BlockSpec((min(t, GRAD_TILE), D_MODEL), lambda k: (k, 0))
    side = _Side((w_in, dpre1), [_resident((D_MODEL, IN_COLS)), row], [jax.ShapeDtypeStruct((t, D_MODEL), F32)], [row],
                 [], lambda outs: None, begin)
    return _grad_w("dw_in", (xb, dph, dog, dpc), (D_MODEL, GROUP, GROUP, GROUP), (D_MODEL, IN_COLS), step, after, side)


def _strips_of(j, tt):
    per_tick = tt // GATE_STRIP // N_FF
    return [slice(s * GATE_STRIP, (s + 1) * GATE_STRIP) for s in range(j * per_tick, (j + 1) * per_tick)]


def _dw_ff1(h1b, da, dcat, o, proj, gate_norm_w, after):
    t = h1b.shape[0]
    tt = min(t, GRAD_TILE)

    def step(acc, h_ref, da_ref, tick):
        hv = h_ref[...]
        for j in range(N_FF):
            cols = slice(j * FF_BLOCK, (j + 1) * FF_BLOCK)
            acc[:, cols] += _dot(hv, da_ref[:, cols], TN)
            tick(j)

    def init(outs):
        outs[2][...] = jnp.zeros_like(outs[2])

    def begin(k, n_steps, ins, outs, scratch):
        do2_ref, o_ref, og_ref, gnw_ref = ins
        do_ref, dog_ref, dgnw_ref = outs
        total = [jnp.zeros((GATE_STRIP, GROUP), F32)]

        def tick(j):
            gnw = gnw_ref[...]
            for rows in _strips_of(j, tt):
                ov, og, do2 = o_ref[rows, :], og_ref[rows, :], do2_ref[rows, :]
                rs = _per_head(lambda o_h: jnp.broadcast_to(
                    lax.rsqrt(jnp.mean(o_h * o_h, axis=-1, keepdims=True) + EPS), o_h.shape), ov)
                on = ov * rs
                sg = _sigmoid(og)
                sil = og * sg
                don = do2 * gnw * sil
                total[0] = total[0] + do2 * on * sil
                dog_ref[rows, :] = (do2 * on * gnw * (sg * (1.0 + og * (1.0 - sg)))).astype(dog_ref.dtype)
                do_ref[rows, :] = rs * (don - on * _per_head(
                    lambda p_h: jnp.broadcast_to(jnp.mean(p_h, axis=-1, keepdims=True), p_h.shape), don * on))
            if j == N_FF - 1:
                dgnw_ref[...] += jnp.sum(total[0], axis=0, keepdims=True)

        return tick

    tile = pl.BlockSpec((tt, GROUP), lambda k: (k, 0))
    vec = pl.BlockSpec((1, GROUP), lambda k: (0, 0))
    side = _Side(
        (dcat, o, proj, gate_norm_w), [tile, tile, pl.BlockSpec((None, tt, GROUP), lambda k: (3, k, 0)), vec],
        [jax.ShapeDtypeStruct((t, HGRN_WIDTH), F32), jax.ShapeDtypeStruct((t, HGRN_WIDTH), BF16),
         jax.ShapeDtypeStruct((1, HGRN_WIDTH), F32)], [tile, tile, vec], [], init, begin)
    return _grad_w("dw_ff1", (h1b, da), (D_MODEL, D_FF), (D_MODEL, D_FF), step, after, side)


def _dw_ff2(r, dpre2b, dcat, bcu, conv_w):
    t = r.shape[0]
    tt = min(t, GRAD_TILE)
    hb = tt // SUBLANES
    halo = 2 * SUBLANES

    def step(acc, r_ref, d_ref, tick):
        dv = d_ref[...]
        for j in range(N_FF):
            rows = slice(j * FF_BLOCK, (j + 1) * FF_BLOCK)
            acc[rows, :] += _dot(r_ref[:, rows], dv, TN)
            tick(j)

    def init(outs):
        outs[1][...] = jnp.zeros_like(outs[1])

    def begin(k, n_steps, ins, outs, scratch):
        dy_ref, dyn_ref, b_ref, bn_ref, c_ref, u_ref, ch_ref, uh_ref, cw_ref = ins
        dp_ref, dcw_ref = outs
        zbuf, dbuf = scratch
        before = lambda ref: ref[SUBLANES:halo, :].astype(F32)
        zbuf[0:SUBLANES, :] = jnp.where(k > 0, before(ch_ref) * before(uh_ref), 0.0)
        zbuf[SUBLANES:SUBLANES + tt, :] = c_ref[...].astype(F32) * u_ref[...].astype(F32)
        dbuf[0:tt, :] = dy_ref[...] * b_ref[...].astype(F32)
        dbuf[tt:tt + SUBLANES, :] = jnp.where(k < n_steps - 1, dyn_ref[...] * bn_ref[0:SUBLANES, :].astype(F32), 0.0)
        totals = [jnp.zeros((GATE_STRIP, GROUP), F32) for _ in range(3)]

        def tick(j):
            cw = cw_ref[...]
            for rows in _strips_of(j, tt):
                at = lambda buf, shift: buf[shift + rows.start:shift + rows.stop, :]
                z, z1, z2 = at(zbuf, SUBLANES), at(zbuf, SUBLANES - 1), at(zbuf, SUBLANES - 2)
                dyc, d1, d2 = at(dbuf, 0), at(dbuf, 1), at(dbuf, 2)
                yc = cw[2:3, :] * z + cw[1:2, :] * z1 + cw[0:1, :] * z2
                dz = cw[2:3, :] * dyc + cw[1:2, :] * d1 + cw[0:1, :] * d2
                dp_ref[0, rows, :] = (dy_ref[rows, :] * yc).astype(dp_ref.dtype)
                dp_ref[1, rows, :] = (dz * u_ref[rows, :].astype(F32)).astype(dp_ref.dtype)
                dp_ref[2, rows, :] = (dz * c_ref[rows, :].astype(F32)).astype(dp_ref.dtype)
                for n, tap in enumerate((z2, z1, z)):
                    totals[n] = totals[n] + dyc * tap
            if j == N_FF - 1:
                for n in range(3):
                    dcw_ref[n:n + 1, :] += jnp.sum(totals[n], axis=0, keepdims=True)

        return tick

    grp = lambda g: pl.BlockSpec((None, tt, GROUP), lambda k: (g, k, 0))
    prev = lambda g: pl.BlockSpec((None, halo, GROUP), lambda k: (g, jnp.maximum(k * (tt // halo) - 1, 0), 0))
    nxt = lambda g: pl.BlockSpec((None, halo, GROUP), lambda k: (g, jnp.minimum((k + 1) * (tt // halo), t // halo - 1), 0))
    nxt_row = lambda k: jnp.minimum((k + 1) * hb, t // SUBLANES - 1)
    whole = pl.BlockSpec((3, CONV_WIDTH), lambda k: (0, 0))
    side = _Side(
        (dcat, dcat, bcu, bcu, bcu, bcu, bcu, bcu, conv_w),
        [pl.BlockSpec((tt, GROUP), lambda k: (k, 1)), pl.BlockSpec((SUBLANES, GROUP), lambda k: (nxt_row(k), 1)),
         grp(0), nxt(0), grp(1), grp(2), prev(1), prev(2), whole],
        [jax.ShapeDtypeStruct((3, t, CONV_WIDTH), BF16), jax.ShapeDtypeStruct((3, CONV_WIDTH), F32)],
        [pl.BlockSpec((3, tt, GROUP), lambda k: (0, k, 0)), whole],
        [pltpu.VMEM((tt + SUBLANES, GROUP), F32), pltpu.VMEM((tt + SUBLANES, GROUP), F32)], init, begin)
    return _grad_w("dw_ff2", (r, dpre2b), (D_FF, D_MODEL), (D_FF, D_MODEL), step, side=side)


def _place():
    x, y, c = lax.axis_index("x"), lax.axis_index("y"), lax.axis_index("c")
    return x, y, c, 2 * x + y


def _other_chips(x, y):
    return [(1 - x, y), (x, 1 - y), (1 - x, 1 - y)]


def _place_shard(name, w, chip, cols_sharded, after=None):
    rows, cols = w.shape
    tr = min(rows, 256)
    nb = rows // tr
    full = (rows, cols * N_CHIPS) if cols_sharded else (rows * N_CHIPS, cols)
    out_map = (lambda i, s: (i, s[0])) if cols_sharded else (lambda i, s: (s[0] * nb + i, 0))

    def body(s_ref, w_ref, *rest):
        rest[-1][...] = w_ref[...].astype(rest[-1].dtype)

    extra = [] if after is None else [after]
    return pl.pallas_call(
        body, name=name,
        grid_spec=pltpu.PrefetchScalarGridSpec(
            num_scalar_prefetch=1, grid=(nb,),
            in_specs=[pl.BlockSpec((tr, cols), lambda i, s: (i, 0))] + [ANY] * len(extra),
            out_specs=pl.BlockSpec((tr, cols), out_map)),
        out_shape=jax.ShapeDtypeStruct(full, BF16),
        compiler_params=_params("parallel"),
    )(chip, w, *extra)


HBM = pl.BlockSpec(memory_space=pltpu.HBM)
SEM = pl.BlockSpec(memory_space=pltpu.SEMAPHORE)
EFFECT = pltpu.SideEffectType.DATAFLOW_SIDE_EFFECTING


PEER_SETS = {
    "sibling": (0, lambda x, y, c: [(x, y, 1 - c)]),
    "chips": (1, lambda x, y, c: [(1 - x, y, c), (x, 1 - y, c), (1 - x, 1 - y, c)]),
    "neighbours": (2, lambda x, y, c: [(1 - x, y, c), (x, 1 - y, c)]),
}


class _Split:
    def __init__(self, name, arrays, plan, others=(), peers=None, prepare=None, sources=(), scratch=()):
        n_own, arrays = len(arrays), (*arrays, *others)
        n, n_copies, n_in = len(arrays), plan.count, len(arrays) + len(sources)
        self.name, self.plan, self.n = name, plan, n_own
        barrier_id, peer_ids = PEER_SETS[peers] if peers else (None, None)

        def body(*refs):
            send_sems, recv_sems, token = refs[n_in], refs[n_in + 1], refs[n_in + 2 + n]
            if peers:
                x, y, c, _ = _place()
                barrier = pltpu.get_barrier_semaphore()
                for peer in peer_ids(x, y, c):
                    pl.semaphore_signal(barrier, inc=1, device_id=peer, device_id_type=MESH)
            if prepare:
                prepare(refs[:n], refs[n:n_in], refs[n_in + 3 + n:])
            if peers:
                pl.semaphore_wait(barrier, len(peer_ids(0, 0, 0)))
            for k, (src, dst, to) in enumerate(plan(refs[:n])):
                pltpu.make_async_remote_copy(src_ref=src, dst_ref=dst, send_sem=send_sems.at[k], recv_sem=recv_sems.at[k],
                                             device_id=to, device_id_type=MESH).start()
            token[...] = jnp.zeros_like(token)

        outs = pl.pallas_call(
            body, name=name + "_start",
            out_shape=(pltpu.SemaphoreType.DMA((n_copies,)), pltpu.SemaphoreType.DMA((n_copies,)),
                       *[pltpu.HBM(a.shape, a.dtype) for a in arrays], jax.ShapeDtypeStruct((SUBLANES, LANES), F32)),
            in_specs=(HBM,) * n_in, out_specs=(SEM, SEM) + (HBM,) * n + (pl.BlockSpec(memory_space=pltpu.VMEM),),
            input_output_aliases={i: 2 + i for i in range(n)}, scratch_shapes=list(scratch),
            compiler_params=pltpu.CompilerParams(has_side_effects=EFFECT, collective_id=barrier_id),
        )(*[pltpu.with_memory_space_constraint(a, pltpu.HBM) for a in (*arrays, *sources)])
        self.sems, self.arrays, self.others, self.token = outs[:2], outs[2:2 + n_own], outs[2 + n_own:2 + n], outs[-1]

    def wait(self, after):
        n, plan = self.n, self.plan

        def body(*refs):
            send_sems, recv_sems = refs[n], refs[n + 1]
            for k, (src, dst, to) in enumerate(plan(refs[:n])):
                cp = pltpu.make_async_remote_copy(src_ref=src, dst_ref=dst, send_sem=send_sems.at[k],
                                                  recv_sem=recv_sems.at[k], device_id=to, device_id_type=MESH)
                cp.wait_send()
                cp.wait_recv()

        return pl.pallas_call(
            body, name=self.name + "_wait", out_shape=tuple(pltpu.HBM(a.shape, a.dtype) for a in self.arrays),
            in_specs=(HBM,) * n + (SEM, SEM, ANY), out_specs=(HBM,) * n, input_output_aliases={i: i for i in range(n)},
            compiler_params=pltpu.CompilerParams(has_side_effects=EFFECT),
        )(*self.arrays, *self.sems, after)


COLS_SHARDED = (True, False, True, False)
HALF_SHAPES = [(D_MODEL // 2, IN_COLS), (D_MODEL, D_MODEL // 2), (D_MODEL // 2, D_FF), (D_FF, D_MODEL // 2)]
PIECE_SHAPES = [(D_MODEL // 2, IN_COLS // N_CHIPS), (D_MODEL // N_CHIPS, D_MODEL // 2),
                (D_MODEL // 2, D_FF // N_CHIPS), (D_FF // N_CHIPS, D_MODEL // 2)]


def _shard_view(kind, ref, chip):
    if COLS_SHARDED[kind]:
        n = ref.shape[1] // N_CHIPS
        return ref.at[:, pl.ds(chip * n, n)]
    n = ref.shape[0] // N_CHIPS
    return ref.at[pl.ds(chip * n, n), :]


def _half_view(kind, ref, h):
    if COLS_SHARDED[kind]:
        n = ref.shape[0] // 2
        return ref.at[pl.ds(h * n, n), :]
    n = ref.shape[1] // 2
    return ref.at[:, pl.ds(h * n, n)]


def _plan(count):
    def mark(fn):
        fn.count = count
        return fn
    return mark


def _shard_rows_view(kind, ref, chip, part, n_parts):
    if COLS_SHARDED[kind]:
        m, n = ref.shape[0] // n_parts, ref.shape[1] // N_CHIPS
        return ref.at[pl.ds(part * m, m), pl.ds(chip * n, n)]
    m = ref.shape[0] // N_CHIPS // n_parts
    return ref.at[pl.ds((n_parts * chip + part) * m, m), :]


def _shard_half_view(kind, ref, chip, h):
    return _shard_rows_view(kind, ref, chip, h, 2)


def _gather_over_ici(kinds, weights):
    @_plan(2 * len(kinds))
    def plan(refs):
        x, y, c, me = _place()
        mine = [_shard_half_view(kind, ref, me, c) for kind, ref in zip(kinds, refs)]
        return [(v, v, to) for v in mine for to in ((1 - x, y, c), (x, 1 - y, c))]

    return _Split("gather_ici_" + "".join(map(str, kinds)), tuple(weights), plan, peers="neighbours")


def _relay_over_ici(kinds, weights, others=()):
    @_plan(2 * len(kinds))
    def plan(refs):
        x, y, c, _ = _place()
        x_nbr, y_nbr = 2 * (1 - x) + y, 2 * x + (1 - y)
        out = []
        for kind, ref in zip(kinds, refs):
            first, second = (_shard_rows_view(kind, ref, chip, 2 * c + q, 4) for q, chip in ((0, x_nbr), (1, y_nbr)))
            out += [(first, first, (x, 1 - y, c)), (second, second, (1 - x, y, c))]
        return out

    return _Split("relay_ici_" + "".join(map(str, kinds)), tuple(weights), plan, others, peers="neighbours")


def _gather_w_in_over_ici(shard, conv4, others=()):
    rows, cols = shard.shape

    @_plan(6)
    def plan(refs):
        x, y, c, me = _place()
        half, conv = _shard_half_view(0, refs[0], me, c), refs[1].at[me]
        return [(v, v, (px, py, c)) for v in (half, conv) for px, py in _other_chips(x, y)]

    def prepare(refs, sources, scratch):
        wide, narrow, sems = scratch
        _, _, _, me = _place()
        load = pltpu.make_async_copy(sources[0], wide, sems.at[0])
        load.start()
        load.wait()
        narrow[...] = wide[...].astype(narrow.dtype)
        store = pltpu.make_async_copy(narrow, _shard_view(0, refs[0], me), sems.at[1])
        store.start()
        store.wait()

    return _Split("gather_w_in_ici", (lax.empty((rows, cols * N_CHIPS), BF16), conv4), plan, others, peers="chips",
                  prepare=prepare, sources=(shard,),
                  scratch=(pltpu.VMEM((rows, cols), F32), pltpu.VMEM((rows, cols), BF16), pltpu.SemaphoreType.DMA((2,))))


def _gather_over_d2d(kinds, weights):
    @_plan(3 * len(kinds))
    def plan(refs):
        x, y, c, _ = _place()
        got = [_shard_half_view(kind, ref, 2 * px + py, c) for kind, ref in zip(kinds, refs)
               for px, py in _other_chips(x, y)]
        return [(v, v, (x, y, 1 - c)) for v in got]

    return _Split("gather_d2d_" + "".join(map(str, kinds)), tuple(weights), plan, peers="sibling")


def _swap_halves(kinds, grads):
    @_plan(len(kinds))
    def plan(refs):
        x, y, c, _ = _place()
        return [(_half_view(kind, g, 1 - c), land, (x, y, 1 - c))
                for kind, g, land in zip(kinds, refs[:len(kinds)], refs[len(kinds):])]

    lands = [lax.empty(HALF_SHAPES[kind], g.dtype) for kind, g in zip(kinds, grads)]
    return _Split("swap_halves_" + "".join(map(str, kinds)), (*grads, *lands), plan, peers="sibling")


def _block_rows(cols, elements):
    return 1 << ((elements // cols).bit_length() - 1)


def _grid_steps(shapes, elements):
    rows, cols = max(shapes, key=lambda shape: shape[0] * shape[1])
    return rows // min(rows, _block_rows(cols, elements))


def _add_half(name, kinds, grads, recvs, core):
    n = len(kinds)
    shapes = [recv.shape for recv in recvs]
    nb = _grid_steps(shapes, 1 << 20)

    def body(c_ref, *refs):
        for g_ref, r_ref, o_ref in zip(refs[:n], refs[n:2 * n], refs[2 * n:]):
            o_ref[...] = (g_ref[...].astype(F32) + r_ref[...].astype(F32)).astype(o_ref.dtype)

    own = [pl.BlockSpec((rows // nb, cols), (lambda i, c_ref: (c_ref[0] * nb + i, 0)) if COLS_SHARDED[k] else
                        (lambda i, c_ref: (i, c_ref[0]))) for k, (rows, cols) in zip(kinds, shapes)]
    blocks = [pl.BlockSpec((rows // nb, cols), lambda i, c_ref: (i, 0)) for rows, cols in shapes]
    return pl.pallas_call(
        body, name=name,
        grid_spec=pltpu.PrefetchScalarGridSpec(
            num_scalar_prefetch=1, grid=(nb,), in_specs=own + blocks, out_specs=blocks),
        out_shape=[jax.ShapeDtypeStruct(shape, BF16) for shape in shapes],
        compiler_params=_params("parallel"),
    )(core, *grads, *recvs)


def _exchange_pieces(kinds, halves, pack=None):
    n_p, n = N_CHIPS - 1, len(kinds)

    @_plan(n_p * n + (0 if pack is None else N_DEV - 1))
    def plan(refs):
        x, y, c, _ = _place()
        copies = []
        if pack is not None:
            me = 4 * x + 2 * y + c
            peers = [((1 - x) if m & 4 else x, (1 - y) if m & 2 else y, (1 - c) if m & 1 else c) for m in range(1, N_DEV)]
            copies += [(refs[2 * n], refs[2 * n + 1].at[me], peer) for peer in peers]
        return copies + [(_shard_view(kind, half, 2 * px + py), land.at[j], (px, py, c))
                         for j, (px, py) in enumerate(_other_chips(x, y))
                         for kind, half, land in zip(kinds, refs[:n], refs[n:2 * n])]

    lands = [lax.empty((n_p,) + PIECE_SHAPES[kind], BF16) for kind in kinds]
    small = () if pack is None else (pack, lax.empty((N_DEV,) + pack.shape, F32))
    return _Split("exchange_pieces_" + "".join(map(str, kinds)), (*halves, *lands, *small), plan,
                  peers="chips" if pack is None else None)


def _sum_pieces(name, kinds, halves, slots, place, after):
    n, n_p = len(kinds), N_CHIPS - 1
    shapes = [slot.shape[1:] for slot in slots]
    nb = _grid_steps(shapes, 1 << 18)

    def body(s_ref, *refs):
        for own_ref, slot_ref, o_ref in zip(refs[:n], refs[n:2 * n], refs[2 * n + 1:]):
            total = own_ref[...].astype(F32)
            for j in range(n_p):
                total = total + slot_ref[j].astype(F32)
            o_ref[...] = total

    own, out, shards = [], [], []
    for k, (rows, cols) in zip(kinds, shapes):
        if COLS_SHARDED[k]:
            own_map, out_map, shard = (lambda i, s: (i, s[0])), (lambda i, s: (s[1] * nb + i, 0)), (2 * rows, cols)
        else:
            own_map, out_map, shard = (lambda i, s: (s[0] * nb + i, 0)), (lambda i, s: (i, s[1])), (rows, 2 * cols)
        own.append(pl.BlockSpec((rows // nb, cols), own_map))
        out.append(pl.BlockSpec((rows // nb, cols), out_map))
        shards.append(jax.ShapeDtypeStruct(shard, F32))
    landed = [pl.BlockSpec((n_p, rows // nb, cols), lambda i, s: (0, i, 0)) for rows, cols in shapes]
    return pl.pallas_call(
        body, name=name,
        grid_spec=pltpu.PrefetchScalarGridSpec(
            num_scalar_prefetch=1, grid=(nb,), in_specs=own + landed + [ANY], out_specs=out),
        out_shape=shards,
        compiler_params=_params("parallel"),
    )(place, *halves, *slots, after)


def _join_halves(kinds, shards):
    @_plan(len(kinds))
    def plan(refs):
        x, y, c, _ = _place()
        return [(_half_view(kind, g, c), _half_view(kind, g, c), (x, y, 1 - c)) for kind, g in zip(kinds, refs)]

    return _Split("join_halves_" + "".join(map(str, kinds)), tuple(shards), plan, peers="sibling")


N_DEV = 8


def _sum_shared(pack, land, device):
    def body(d_ref, p_ref, l_ref, o_ref):
        me = d_ref[0]
        total = jnp.where(me == 0, p_ref[...], l_ref[0])
        for d in range(1, N_DEV):
            total = total + jnp.where(me == d, p_ref[...], l_ref[d])
        o_ref[...] = total

    return pl.pallas_call(
        body, name="sum_shared",
        grid_spec=pltpu.PrefetchScalarGridSpec(
            num_scalar_prefetch=1, grid=(1,),
            in_specs=[pl.BlockSpec(pack.shape, lambda i, d: (0, 0)), pl.BlockSpec(land.shape, lambda i, d: (0, 0, 0))],
            out_specs=pl.BlockSpec(pack.shape, lambda i, d: (0, 0))),
        out_shape=jax.ShapeDtypeStruct(pack.shape, F32),
    )(device, pack, land)


def _adamw(name, weights, after=None):
    shapes = [w.shape for w, _, _, _ in weights]
    nb = _grid_steps(shapes, 1 << 18)
    extra = [] if after is None else [after]
    n_in = 4 * len(weights) + len(extra)

    def body(*refs):
        for k in range(len(weights)):
            w_ref, g_ref, m_ref, v_ref = refs[4 * k:4 * k + 4]
            go_ref, d_ref, nm_ref, nv_ref = refs[n_in + 4 * k:n_in + 4 * k + 4]
            g = g_ref[...]
            go_ref[...] = g
            d_ref[...], nm_ref[...], nv_ref[...] = _adam_step(w_ref[...], g, m_ref[...], v_ref[...])

    blocks = [pl.BlockSpec((rows // nb, cols), lambda i: (i, 0)) for rows, cols in shapes for _ in range(4)]
    outs = pl.pallas_call(
        body, name=name, grid=(nb,), in_specs=blocks + [ANY] * len(extra), out_specs=blocks,
        out_shape=[jax.ShapeDtypeStruct(shape, F32) for shape in shapes for _ in range(4)],
        compiler_params=_params("parallel"),
    )(*[a for group in weights for a in group], *extra)
    return [outs[4 * k:4 * k + 4] for k in range(len(weights))]


def _adam_step(w, g, m, v):
    nm = ADAM_B1 * m + (1.0 - ADAM_B1) * g
    nv = ADAM_B2 * v + (1.0 - ADAM_B2) * jnp.square(g)
    m_hat = nm * (1.0 / (1.0 - ADAM_B1 ** ADAM_STEP))
    v_hat = nv * (1.0 / (1.0 - ADAM_B2 ** ADAM_STEP))
    return -ADAM_LR * (m_hat / (jnp.sqrt(v_hat) + ADAM_EPS) + ADAM_WD * w), nm, nv


def _adamw_small(tot, chip, weights, ms, vs, after):
    n, half = len(weights), D_MODEL // 2

    def body(chip_ref, tot_ref, *refs):
        ins, outs = refs[:3 * n], refs[3 * n + 1:]
        tot = tot_ref[...]
        conv_all = jnp.concatenate([tot[5:6, half:], tot[6:7, :half], tot[6:7, half:]], axis=0)
        conv = sum(jnp.where(chip_ref[0] == s, conv_all[:, s * LANES:(s + 1) * LANES], 0.0) for s in range(N_CHIPS))
        grads = [jnp.concatenate([tot[4:5, :half], tot[4:5, half:]], axis=0), tot[5:6, :half], conv,
                 tot[0:1], tot[1:2], tot[2:3], tot[3:4]]
        for k, g in enumerate(grads):
            delta, nm, nv = _adam_step(ins[k][...], g, ins[n + k][...], ins[2 * n + k][...])
            outs[k][...], outs[n + k][...], outs[2 * n + k][...], outs[3 * n + k][...] = g, delta, nm, nv
        outs[4 * n][...] = tot[7:8, 0:1]

    whole = lambda a: pl.BlockSpec(a.shape, lambda i, s: (0,) * a.ndim)
    arrays = (*weights, *ms, *vs)
    loss = jax.ShapeDtypeStruct((1, 1), F32)
    return pl.pallas_call(
        body, name="adamw_small",
        grid_spec=pltpu.PrefetchScalarGridSpec(
            num_scalar_prefetch=1, grid=(1,), in_specs=[whole(tot)] + [whole(a) for a in arrays] + [ANY],
            out_specs=[whole(a) for a in weights] * 4 + [whole(loss)]),
        out_shape=[jax.ShapeDtypeStruct(a.shape, F32) for a in weights] * 4 + [loss],
    )(chip, tot, *arrays, after)


def kernel(x, w_in, lb_logits, gate_norm_w, conv_w, w_out, ln1_g, ln1_b, w_ff1, w_ff2, ln2_g, ln2_b, loss_target, m_w_in, m_lb_logits, m_gate_norm_w, m_conv_w, m_w_out, m_ln1_g, m_ln1_b, m_w_ff1, m_w_ff2, m_ln2_g, m_ln2_b, v_w_in, v_lb_logits, v_gate_norm_w, v_conv_w, v_w_out, v_ln1_g, v_ln1_b, v_w_ff1, v_w_ff2, v_ln2_g, v_ln2_b):
    xs, tgt = x[0], loss_target[0]
    chip = 2 * lax.axis_index("x") + lax.axis_index("y")
    core = lax.axis_index("c").astype(jnp.int32).reshape(1)
    chip1 = chip.astype(jnp.int32).reshape(1)
    place = jnp.concatenate([chip1, core])

    conv4 = lax.dynamic_update_slice(jnp.zeros((N_CHIPS,) + conv_w.shape[1:], F32), conv_w, (chip, 0, 0))
    ici_in = _gather_w_in_over_ici(w_in[0], conv4, (_place_shard("place_w_ff1", w_ff1[0], chip1, True),
                                                    _place_shard("place_w_ff2", w_ff2[0], chip1, False)))
    rest = (1, 2, 3)
    ici_rest = _gather_over_ici(rest, (_place_shard("place_w_out", w_out[0], chip1, False, after=ici_in.token),
                                       *ici_in.others))
    wb_in, cv4 = ici_in.wait(ici_rest.token)
    d2d_in = _gather_over_d2d((0,), (wb_in,))
    wb_in, = d2d_in.wait(d2d_in.token)
    conv_full = cv4.transpose(1, 0, 2).reshape(3, CONV_WIDTH)

    proj, bcu, xb, cat_c = _in_proj(xs, wb_in, conv_full, ici_rest.token)
    relay_rest = _relay_over_ici(rest, ici_rest.wait(proj))
    o, states = _hgrn_fwd(proj, lb_logits, relay_rest.token)
    d2d_rest = _gather_over_d2d(rest, relay_rest.wait(o))
    cat_h = _gate_fwd(proj, o, gate_norm_w, d2d_rest.token)
    wb_out, wb_ff1, wb_ff2 = d2d_rest.wait(cat_h)

    (h1b, r, da, dpre2b, dpre1, dcat, g_ln1_g, g_ln1_b, g_ln2_g, g_ln2_b, loss8, g_out_local) = _sublayers(
        cat_h, cat_c, xs, tgt, wb_out, wb_ff1, wb_ff2, ln1_g, ln1_b, ln2_g, ln2_b)

    names = ("w_in", "w_out", "w_ff1", "w_ff2")

    def named(prefix, kinds):
        return prefix + "".join("_" + names[k] for k in kinds)

    def add_halves(kinds, grads, lands):
        return _add_half(named("add_half", kinds), kinds, grads, lands, core)

    def sum_pieces(kinds, halves, lands, after):
        return _sum_pieces(named("sum_pieces", kinds), kinds, halves, lands, place, after)

    early = (1, 2, 3)
    g_ff2_local, dpc, g_conv = _dw_ff2(r, dpre2b, dcat, bcu, conv_full)
    swap_a = _swap_halves((1, 3), (g_out_local, g_ff2_local))
    g_ff1_local, do, dog, g_gnw = _dw_ff1(h1b, da, dcat, o, proj, gate_norm_w, swap_a.token)
    swap_b = _swap_halves((2,), (g_ff1_local,))
    swapped_a = swap_a.wait(swap_b.token)
    halves_a = add_halves((1, 3), swapped_a[:2], swapped_a[2:])
    swapped_b = swap_b.wait(halves_a[1])
    halves = (halves_a[0], *add_halves((2,), swapped_b[:1], swapped_b[1:]), halves_a[1])
    exch = _exchange_pieces(early, halves)
    dph, g_lbl = _hgrn_bwd(proj, do, states, lb_logits, exch.token)
    g_in_local, grad_x = _dw_in(xb, dph, dog, dpc, wb_in, dpre1, dph)

    late = (0,)
    swap = _swap_halves(late, (g_in_local,))
    exchanged = exch.wait(swap.token)
    pack = jnp.concatenate([
        g_ln1_g, g_ln1_b, g_ln2_g, g_ln2_b,
        jnp.concatenate([g_lbl[0:1], g_lbl[1:2]], axis=1),
        jnp.concatenate([g_gnw, g_conv[0:1]], axis=1),
        jnp.concatenate([g_conv[1:2], g_conv[2:3]], axis=1),
        jnp.concatenate([loss8[0:1], jnp.zeros((1, D_MODEL - LANES), F32)], axis=1)], axis=0)
    join_a = _join_halves((2,), sum_pieces((2,), exchanged[1:2], exchanged[4:5], swap.token))
    swapped = swap.wait(join_a.token)
    exch = _exchange_pieces(late, add_halves(late, swapped[:1], swapped[1:]), pack)
    join_b = _join_halves((1, 3), sum_pieces((1, 3), exchanged[0:3:2], exchanged[3:6:2], exch.token))
    g_w_ff1, = join_a.wait(join_b.token)
    (g_w_ff1, d_ff1, nm_ff1, nv_ff1), = _adamw("adamw_w_ff1", [(w_ff1[0], g_w_ff1, m_w_ff1[0], v_w_ff1[0])])
    g_w_out, g_w_ff2 = join_b.wait(d_ff1)
    (g_w_ff2, d_ff2, nm_ff2, nv_ff2), (g_w_out, d_out, nm_out, nv_out) = _adamw(
        "adamw_w_ff2_w_out", [(w_ff2[0], g_w_ff2, m_w_ff2[0], v_w_ff2[0]), (w_out[0], g_w_out, m_w_out[0], v_w_out[0])])
    exchanged = exch.wait(d_out)
    tot = _sum_shared(exchanged[2], exchanged[3], 2 * chip1 + core)
    join = _join_halves(late, sum_pieces(late, exchanged[:1], exchanged[1:2], tot))
    small = ("lb_logits", "gate_norm_w", "conv_w", "ln1_g", "ln1_b", "ln2_g", "ln2_b")
    small_out = _adamw_small(
        tot, chip1, (lb_logits, gate_norm_w, conv_w[0], ln1_g, ln1_b, ln2_g, ln2_b),
        (m_lb_logits, m_gate_norm_w, m_conv_w[0], m_ln1_g, m_ln1_b, m_ln2_g, m_ln2_b),
        (v_lb_logits, v_gate_norm_w, v_conv_w[0], v_ln1_g, v_ln1_b, v_ln2_g, v_ln2_b), join.token)
    g_w_in, = join.wait(small_out[0])
    (g_w_in, d_in, nm_in, nv_in), = _adamw("adamw_w_in", [(w_in[0], g_w_in, m_w_in[0], v_w_in[0])])
    loss = small_out[4 * len(small)][0, 0]

    def results(n_kind, large):
        out = dict(zip(small, small_out[n_kind * len(small):(n_kind + 1) * len(small)]))
        out["conv_w"] = out["conv_w"][None]
        out.update({name: a[None] for name, a in zip(("w_in", "w_out", "w_ff1", "w_ff2"), large)})
        return [out[name] for name in ("w_in", "lb_logits", "gate_norm_w", "conv_w", "w_out", "ln1_g", "ln1_b",
                                       "w_ff1", "w_ff2", "ln2_g", "ln2_b")]

    return (loss, grad_x[None], *results(0, (g_w_in, g_w_out, g_w_ff1, g_w_ff2)),
            *results(1, (d_in, d_out, d_ff1, d_ff2)), *results(2, (nm_in, nm_out, nm_ff1, nm_ff2)),
            *results(3, (nv_in, nv_out, nv_ff1, nv_ff2)))
```

```python
import jax
import jax.numpy as jnp
from jax import lax
from jax.experimental import pallas as pl
from jax.experimental.pallas import tpu as pltpu

F32 = jnp.float32
BF16 = jnp.bfloat16
MXU_DTYPE = jnp.bfloat16

D_MODEL = 1024
HGRN_WIDTH = 512
HEAD_DIM = 128
N_HEADS = 4
CONV_WIDTH = 512
CHUNK = 64
D_FF = 4096
IN_COLS = 3584
GROUP = 512
N_GROUPS = IN_COLS // GROUP
ALPHA = 2.0 ** 0.25
EPS = 1e-5
N_CHIPS = 4
ADAM_LR, ADAM_B1, ADAM_B2, ADAM_EPS, ADAM_WD, ADAM_STEP = 0.001, 0.9, 0.999, 1e-08, 0.01, 10

LANES = 128
SUBLANES = 8
VMEM_LIMIT = 56 * 1024 * 1024
FF_BLOCK = 1024
N_FF = D_FF // FF_BLOCK
GATE_STRIP = 64

NN = (((1,), (0,)), ((), ()))
NT = (((1,), (1,)), ((), ()))
TN = (((0,), (0,)), ((), ()))
MESH = pl.DeviceIdType.MESH
ANY = pl.BlockSpec(memory_space=pl.ANY)


def _dot(a, b, dims):
    return lax.dot_general(a.astype(MXU_DTYPE), b.astype(MXU_DTYPE), dims, preferred_element_type=F32)


def _dot_exact(ones, v):
    ones = ones.astype(jnp.bfloat16)
    hi = v.astype(jnp.bfloat16)
    rest = v - hi.astype(F32)
    mid = rest.astype(jnp.bfloat16)
    low = (rest - mid.astype(F32)).astype(jnp.bfloat16)
    return sum(lax.dot_general(ones, part, NN, preferred_element_type=F32) for part in (hi, mid, low))


def _params(*sem):
    return pltpu.CompilerParams(dimension_semantics=sem, vmem_limit_bytes=VMEM_LIMIT)


def _resident(shape):
    return pl.BlockSpec(shape, lambda *_: (0,) * len(shape), pipeline_mode=pl.Buffered(1))


def _sigmoid(v):
    return 1.0 / (1.0 + jnp.exp(-v))


def _lower_bound(lbl):
    m = jnp.max(lbl, axis=0, keepdims=True)
    e = jnp.exp(lbl - m)
    s = e / jnp.sum(e, axis=0, keepdims=True)
    return s[0:1, :], s[1:2, :]


def _heads(v):
    return [v[:, h * HEAD_DIM:(h + 1) * HEAD_DIM] for h in range(N_HEADS)]


def _per_head(fn, *arrays):
    return jnp.concatenate([fn(*parts) for parts in zip(*map(_heads, arrays))], axis=1)


def _in_proj(x, w_in, conv_w, after):
    t = x.shape[0]
    tm = min(t, 512)

    def body(x_ref, w_ref, cw_ref, after_ref, o_ref, bcu_ref, xb_ref, y_ref, zbuf):
        @pl.when(pl.program_id(0) == 0)
        def _():
            zbuf[tm:tm + SUBLANES, :] = jnp.zeros((SUBLANES, CONV_WIDTH), F32)

        xb = x_ref[...].astype(xb_ref.dtype)
        xb_ref[...] = xb
        group = lambda g: _dot(xb, w_ref[:, g * GROUP:(g + 1) * GROUP], NN)
        for g in range(4):
            o_ref[g] = group(g)
        b_gate, c_gate, u = group(4), group(5), group(6)
        for n, part in enumerate((b_gate, c_gate, u)):
            bcu_ref[n] = part.astype(bcu_ref.dtype)
        zbuf[0:SUBLANES, :] = zbuf[tm:tm + SUBLANES, :]
        zbuf[SUBLANES:SUBLANES + tm, :] = c_gate * u
        cw = cw_ref[...]
        at = lambda shift: zbuf[shift:shift + tm, :]
        conv = cw[2:3, :] * at(SUBLANES) + cw[1:2, :] * at(SUBLANES - 1) + cw[0:1, :] * at(SUBLANES - 2)
        y_ref[...] = (b_gate * conv).astype(y_ref.dtype)

    return pl.pallas_call(
        body, name="in_proj", grid=(t // tm,),
        in_specs=[pl.BlockSpec((tm, D_MODEL), lambda i: (i, 0)), _resident((D_MODEL, IN_COLS)),
                  pl.BlockSpec((3, CONV_WIDTH), lambda i: (0, 0)), ANY],
        out_specs=[pl.BlockSpec((4, tm, GROUP), lambda i: (0, i, 0)), pl.BlockSpec((3, tm, GROUP), lambda i: (0, i, 0)),
                   pl.BlockSpec((tm, D_MODEL), lambda i: (i, 0)), pl.BlockSpec((tm, CONV_WIDTH), lambda i: (i, 0))],
        out_shape=[jax.ShapeDtypeStruct((4, t, GROUP), F32), jax.ShapeDtypeStruct((3, t, GROUP), BF16),
                   jax.ShapeDtypeStruct((t, D_MODEL), BF16), jax.ShapeDtypeStruct((t, CONV_WIDTH), BF16)],
        scratch_shapes=[pltpu.VMEM((tm + SUBLANES, CONV_WIDTH), F32)],
        compiler_params=_params("arbitrary"),
    )(x, w_in, conv_w, after)


def _gates(fp, lb):
    sig = _sigmoid(fp)
    f = lb + (1.0 - lb) * sig
    return sig, f, jnp.log(f), 1.0 - f


def _chunk_masks():
    row = lax.broadcasted_iota(jnp.int32, (CHUNK, CHUNK), 0)
    col = lax.broadcasted_iota(jnp.int32, (CHUNK, CHUNK), 1)
    return row >= col, row <= col


def _hgrn_fwd(proj, lb_logits, after):
    t = proj.shape[1]
    tb = min(t, 512)
    ncb = tb // CHUNK

    def body(q_ref, f_ref, v_ref, lbl_ref, after_ref, o_ref, st_ref, s_scr):
        @pl.when(pl.program_id(0) == 0)
        def _():
            s_scr[...] = jnp.zeros_like(s_scr)

        lb, _ = _lower_bound(lbl_ref[...])
        causal, _ = _chunk_masks()

        every = range(ncb)
        rows = [slice(c * CHUNK, (c + 1) * CHUNK) for c in every]
        q, v = [q_ref[r, :] for r in rows], [v_ref[r, :] for r in rows]
        gates = [_gates(f_ref[r, :], lb) for r in rows]
        k = [gt[3] for gt in gates]
        b = [_dot_exact(causal, gt[2]) for gt in gates]
        mid, last = [x[CHUNK // 2:CHUNK // 2 + 1, :] for x in b], [x[CHUNK - 1:CHUNK, :] for x in b]
        qt = [q[c] * jnp.exp(b[c] - mid[c]) for c in every]
        kt = [k[c] * jnp.exp(mid[c] - b[c]) for c in every]
        qi = [q[c] * jnp.exp(b[c]) for c in every]
        ks = [k[c] * jnp.exp(last[c] - b[c]) for c in every]
        dec = [jnp.exp(x) for x in last]
        scores = [[jnp.where(causal, _dot(a, b_, NT), 0.0) for a, b_ in zip(_heads(qt[c]), _heads(kt[c]))] for c in every]
        intra = [[_dot(s, v_h, NN) for s, v_h in zip(scores[c], _heads(v[c]))] for c in every]
        update = [_per_head(lambda v_h, ks_h: _dot(v_h, ks_h, TN), v[c], ks[c]) for c in every]

        st = s_scr[...]
        states = []
        for c in every:
            states.append(st)
            st_ref[c] = st
            st = dec[c] * st + update[c]
        s_scr[...] = st

        o_ref[...] = jnp.concatenate(
            [jnp.concatenate([i_h + _dot(qi_h, st_h, NT) for i_h, qi_h, st_h in
                              zip(intra[c], _heads(qi[c]), _heads(states[c]))], axis=1) for c in every], axis=0)

    grp = lambda g: pl.BlockSpec((None, tb, GROUP), lambda i: (g, i, 0))
    return pl.pallas_call(
        body, name="hgrn_fwd", grid=(t // tb,),
        in_specs=[grp(0), grp(1), grp(2), pl.BlockSpec((2, HGRN_WIDTH), lambda i: (0, 0)), ANY],
        out_specs=[pl.BlockSpec((tb, HGRN_WIDTH), lambda i: (i, 0)),
                   pl.BlockSpec((ncb, HEAD_DIM, HGRN_WIDTH), lambda i: (i, 0, 0))],
        out_shape=[jax.ShapeDtypeStruct((t, HGRN_WIDTH), F32),
                   jax.ShapeDtypeStruct((t // CHUNK, HEAD_DIM, HGRN_WIDTH), F32)],
        scratch_shapes=[pltpu.VMEM((HEAD_DIM, HGRN_WIDTH), F32)],
        compiler_params=_params("arbitrary"),
    )(proj, proj, proj, lb_logits, after)


def _gate_fwd(proj, o, gate_norm_w, after):
    t = proj.shape[1]
    tb = min(t, 1024)

    def body(o_ref, og_ref, gnw_ref, after_ref, out_ref):
        gnw = gnw_ref[...]
        for s in range(tb // GATE_STRIP):
            rows = slice(s * GATE_STRIP, (s + 1) * GATE_STRIP)
            og = og_ref[rows, :]
            on = _per_head(lambda o_h: o_h * lax.rsqrt(jnp.mean(o_h * o_h, axis=-1, keepdims=True) + EPS), o_ref[rows, :])
            out_ref[rows, :] = (on * gnw * (og * _sigmoid(og))).astype(out_ref.dtype)

    tile = pl.BlockSpec((tb, GROUP), lambda i: (i, 0))
    return pl.pallas_call(
        body, name="gate_fwd", grid=(t // tb,),
        in_specs=[tile, pl.BlockSpec((None, tb, GROUP), lambda i: (3, i, 0)), pl.BlockSpec((1, GROUP), lambda i: (0, 0)), ANY],
        out_specs=tile,
        out_shape=jax.ShapeDtypeStruct((t, HGRN_WIDTH), BF16),
        compiler_params=_params("parallel"),
    )(o, proj, gate_norm_w, after)


def _ln_bwd(dy, xhat, rstd, g):
    dxhat = dy * g
    m1 = jnp.mean(dxhat, axis=-1, keepdims=True)
    m2 = jnp.mean(dxhat * xhat, axis=-1, keepdims=True)
    return rstd * (dxhat - m1 - xhat * m2)


def _layer_norm(pre):
    xc = pre - jnp.mean(pre, axis=-1, keepdims=True)
    rstd = lax.rsqrt(jnp.mean(xc * xc, axis=-1, keepdims=True) + EPS)
    return xc * rstd, rstd


def _sublayers(cat_h, cat_c, x, target, w_out, w_ff1, w_ff2, g1, b1, g2, b2):
    t = x.shape[0]
    tm = min(t, 256)

    def body(ch_ref, cc_ref, x_ref, tg_ref, wo_ref, w1_ref, w2_ref, g1_ref, b1_ref, g2_ref, b2_ref,
             h1_ref, r_ref, da_ref, dp2b_ref, dp1_ref, dcat_ref, dg1_ref, db1_ref, dg2_ref, db2_ref, loss_ref, gwo_ref,
             gwo_acc, gwo_narrow, sem):
        @pl.when(pl.program_id(0) == 0)
        def _():
            for ref in (dg1_ref, db1_ref, dg2_ref, db2_ref, loss_ref, gwo_acc):
                ref[...] = jnp.zeros_like(ref)

        mix = _dot(ch_ref[...], wo_ref[0:GROUP, :], NN) + _dot(cc_ref[...], wo_ref[GROUP:2 * GROUP, :], NN)
        xhat1, rstd1 = _layer_norm(ALPHA * x_ref[...] + mix)
        h1 = xhat1 * g1_ref[...] + b1_ref[...]
        h1b = h1.astype(h1_ref.dtype)
        h1_ref[...] = h1b
        mlp = jnp.zeros((tm, D_MODEL), F32)
        for j in range(N_FF):
            cols = slice(j * FF_BLOCK, (j + 1) * FF_BLOCK)
            r = jnp.square(jnp.maximum(_dot(h1b, w1_ref[:, cols], NN), 0.0)).astype(r_ref.dtype)
            r_ref[:, cols] = r
            mlp = mlp + _dot(r, w2_ref[cols, :], NN)
        xhat2, rstd2 = _layer_norm(ALPHA * h1 + mlp)
        err = xhat2 * g2_ref[...] + b2_ref[...] - tg_ref[...]
        loss_ref[...] += 0.5 * jnp.sum(jnp.mean(err * err, axis=-1, keepdims=True))
        dy = err * (1.0 / D_MODEL)
        dg2_ref[...] += jnp.sum(dy * xhat2, axis=0, keepdims=True)
        db2_ref[...] += jnp.sum(dy, axis=0, keepdims=True)
        dp2 = _ln_bwd(dy, xhat2, rstd2, g2_ref[...])
        dp2b = dp2.astype(dp2b_ref.dtype)
        dp2b_ref[...] = dp2b
        back = jnp.zeros((tm, D_MODEL), F32)
        for j in range(N_FF):
            cols = slice(j * FF_BLOCK, (j + 1) * FF_BLOCK)
            dr = _dot(dp2b, w2_ref[cols, :], NT)
            da = (dr * (2.0 * jnp.sqrt(r_ref[:, cols].astype(F32)))).astype(da_ref.dtype)
            da_ref[:, cols] = da
            back = back + _dot(da, w1_ref[:, cols], NT)
        dh1 = ALPHA * dp2 + back
        dg1_ref[...] += jnp.sum(dh1 * xhat1, axis=0, keepdims=True)
        db1_ref[...] += jnp.sum(dh1, axis=0, keepdims=True)
        dp1 = _ln_bwd(dh1, xhat1, rstd1, g1_ref[...])
        dp1b = dp1.astype(MXU_DTYPE)
        dp1_ref[...] = dp1
        dcat_ref[...] = _dot(dp1b, wo_ref[...], NT)
        gwo_acc[0:GROUP, :] += _dot(ch_ref[...], dp1b, TN)
        gwo_acc[GROUP:2 * GROUP, :] += _dot(cc_ref[...], dp1b, TN)

        @pl.when(pl.program_id(0) == pl.num_programs(0) - 1)
        def _():
            gwo_narrow[...] = gwo_acc[...].astype(gwo_narrow.dtype)
            copy = pltpu.make_async_copy(gwo_narrow, gwo_ref, sem.at[0])
            copy.start()
            copy.wait()

    row = pl.BlockSpec((tm, D_MODEL), lambda i: (i, 0))
    wide = pl.BlockSpec((tm, D_FF), lambda i: (i, 0))
    vec = pl.BlockSpec((1, D_MODEL), lambda i: (0, 0))
    narrow = lambda dtype: jax.ShapeDtypeStruct((t, D_MODEL), dtype)
    return pl.pallas_call(
        body, name="sublayers", grid=(t // tm,),
        in_specs=[pl.BlockSpec((tm, GROUP), lambda i: (i, 0)), pl.BlockSpec((tm, GROUP), lambda i: (i, 0)), row, row,
                  _resident((D_MODEL, D_MODEL)),
                  _resident((D_MODEL, D_FF)), _resident((D_FF, D_MODEL)), vec, vec, vec, vec],
        out_specs=[row, wide, wide, row, row, row, vec, vec, vec, vec,
                   pl.BlockSpec((SUBLANES, LANES), lambda i: (0, 0)), ANY],
        out_shape=[narrow(BF16), jax.ShapeDtypeStruct((t, D_FF), BF16), jax.ShapeDtypeStruct((t, D_FF), BF16),
                   narrow(BF16), narrow(F32), narrow(F32)]
                  + [jax.ShapeDtypeStruct((1, D_MODEL), F32)] * 4
                  + [jax.ShapeDtypeStruct((SUBLANES, LANES), F32), jax.ShapeDtypeStruct((D_MODEL, D_MODEL), BF16)],
        scratch_shapes=[pltpu.VMEM((D_MODEL, D_MODEL), F32), pltpu.VMEM((D_MODEL, D_MODEL), BF16),
                        pltpu.SemaphoreType.DMA((1,))],
        compiler_params=_params("arbitrary"),
    )(cat_h, cat_c, x, target, w_out, w_ff1, w_ff2, g1, b1, g2, b2)


def _hgrn_bwd(proj, do, states, lb_logits, after):
    t = proj.shape[1]
    tb = min(t, 512)
    ncb = tb // CHUNK
    nblk = t // tb

    def body(q_ref, f_ref, v_ref, do_ref, st_ref, lbl_ref, after_ref, dp_ref, dlbl_ref, ds_scr, dlb_scr):
        i = pl.program_id(0)

        @pl.when(i == 0)
        def _():
            ds_scr[...] = jnp.zeros_like(ds_scr)
            dlb_scr[...] = jnp.zeros_like(dlb_scr)

        lb, s1 = _lower_bound(lbl_ref[...])
        causal, anti = _chunk_masks()
        every = range(ncb)
        rows = [slice(c * CHUNK, (c + 1) * CHUNK) for c in every]
        q, v, do = ([ref[r, :] for r in rows] for ref in (q_ref, v_ref, do_ref))
        st = [st_ref[c] for c in every]
        gates = [_gates(f_ref[r, :], lb) for r in rows]
        sig, f, k = ([gt[n] for gt in gates] for n in (0, 1, 3))
        b = [_dot_exact(causal, gt[2]) for gt in gates]
        mid, last = [x[CHUNK // 2:CHUNK // 2 + 1, :] for x in b], [x[CHUNK - 1:CHUNK, :] for x in b]
        e_q = [jnp.exp(b[c] - mid[c]) for c in every]
        e_k = [jnp.exp(mid[c] - b[c]) for c in every]
        e_i = [jnp.exp(x) for x in b]
        e_s = [jnp.exp(last[c] - b[c]) for c in every]
        dec = [jnp.exp(x) for x in last]
        qt, kt, qi, ks = ([a[c] * e[c] for c in every] for a, e in ((q, e_q), (k, e_k), (q, e_i), (k, e_s)))

        def masked(a, b_):
            return [[jnp.where(causal, _dot(a_h, b_h, NT), 0.0) for a_h, b_h in zip(_heads(a[c]), _heads(b_[c]))]
                    for c in every]

        def with_scores(s, other, dims):
            return [jnp.concatenate([_dot(s_h, o_h, dims) for s_h, o_h in zip(s[c], _heads(other[c]))], axis=1)
                    for c in every]

        def per_head(dims, a, b_):
            return [_per_head(lambda a_h, b_h: _dot(a_h, b_h, dims), a[c], b_[c]) for c in every]

        scores, dscores = masked(qt, kt), masked(do, v)
        dqt, dkt, dv_intra = with_scores(dscores, kt, NN), with_scores(dscores, qt, TN), with_scores(scores, do, TN)
        dqi, update = per_head(NN, do, st), per_head(TN, do, qi)

        dst = ds_scr[...]
        dsts = [None] * ncb
        for c in reversed(every):
            dsts[c] = dst
            dst = dec[c] * dst + update[c]
        ds_scr[...] = dst

        dv_state, dks = per_head(NT, ks, dsts), per_head(NN, v, dsts)
        ddec = [jnp.sum(dsts[c] * st[c], axis=0, keepdims=True) for c in every]
        dq = [dqt[c] * e_q[c] + dqi[c] * e_i[c] for c in every]
        dk = [dkt[c] * e_k[c] + dks[c] * e_s[c] for c in every]
        db = [q[c] * dq[c] - k[c] * dk[c] for c in every]
        db_last = [jnp.sum(dks[c] * ks[c], axis=0, keepdims=True) + ddec[c] * dec[c] for c in every]
        dg = [_dot_exact(anti, db[c]) + db_last[c] for c in every]
        df = [dg[c] / f[c] - dk[c] for c in every]
        dlb_scr[...] += sum(jnp.sum(df[c] * (1.0 - sig[c]), axis=0, keepdims=True) for c in every)
        dfp = [df[c] * (1.0 - lb) * sig[c] * (1.0 - sig[c]) for c in every]
        dv = [dv_intra[c] + dv_state[c] for c in every]
        for n, parts in enumerate((dq, dfp, dv)):
            dp_ref[n] = jnp.concatenate(parts, axis=0).astype(dp_ref.dtype)

        @pl.when(i == nblk - 1)
        def _():
            dlb = dlb_scr[...]
            dlbl_ref[0:1, :] = dlb * lb * (1.0 - lb)
            dlbl_ref[1:2, :] = -dlb * lb * s1

    grp = lambda g: pl.BlockSpec((None, tb, GROUP), lambda i: (g, nblk - 1 - i, 0))
    vec = pl.BlockSpec((2, HGRN_WIDTH), lambda i: (0, 0))
    return pl.pallas_call(
        body, name="hgrn_bwd", grid=(nblk,),
        in_specs=[grp(0), grp(1), grp(2), pl.BlockSpec((tb, HGRN_WIDTH), lambda i: (nblk - 1 - i, 0)),
                  pl.BlockSpec((ncb, HEAD_DIM, HGRN_WIDTH), lambda i: (nblk - 1 - i, 0, 0)), vec, ANY],
        out_specs=[pl.BlockSpec((3, tb, HGRN_WIDTH), lambda i: (0, nblk - 1 - i, 0)), vec],
        out_shape=[jax.ShapeDtypeStruct((3, t, HGRN_WIDTH), BF16), jax.ShapeDtypeStruct((2, HGRN_WIDTH), F32)],
        scratch_shapes=[pltpu.VMEM((HEAD_DIM, HGRN_WIDTH), F32), pltpu.VMEM((1, HGRN_WIDTH), F32)],
        compiler_params=_params("arbitrary"),
    )(proj, proj, proj, do, states, lb_logits, after)


GRAD_TILE = 512
OUT_PARTS = 4


class _Side:
    def __init__(self, operands, in_specs, out_shape, out_specs, scratch, init, begin):
        self.operands, self.in_specs, self.out_shape, self.out_specs = operands, in_specs, out_shape, out_specs
        self.scratch, self.init, self.begin = scratch, init, begin


RING = 3


def _grad_w(name, operands, widths, shape, step, after=None, side=None, ringed=None):
    t = operands[0].shape[-2]
    tt = min(t, GRAD_TILE)
    n_in, n_steps = len(operands), t // tt
    in_specs = [ANY if n == ringed else pl.BlockSpec((tt, w), lambda k: (k, 0)) if a.ndim == 2 else
                pl.BlockSpec((a.shape[0], tt, w), lambda k: (0, k, 0)) for n, (a, w) in enumerate(zip(operands, widths))]
    extra = [] if after is None else [after]
    s_in, s_out = (len(side.operands), len(side.out_shape)) if side else (0, 0)
    first_out = n_in + s_in + len(extra)
    side_scratch = side.scratch if side else []
    ring_scratch = [] if ringed is None else [pltpu.VMEM((RING, tt, widths[ringed]), operands[ringed].dtype),
                                              pltpu.SemaphoreType.DMA((RING,))]

    def body(*refs):
        o_ref, side_outs = refs[first_out], refs[first_out + 1:first_out + 1 + s_out]
        acc, narrow, sem = refs[first_out + 1 + s_out:first_out + 4 + s_out]
        first_side = first_out + 4 + s_out
        k = pl.program_id(0)
        tiles = list(refs[:n_in])
        if ringed is not None:
            ring, ring_sems = refs[first_side + len(side_scratch):]

            def fetch(tile, slot):
                return pltpu.make_async_copy(refs[ringed].at[pl.ds(tile * tt, tt), :], ring.at[slot], ring_sems.at[slot])

            @pl.when(k == 0)
            def _():
                for first in range(min(RING - 1, n_steps)):
                    fetch(first, first).start()

            @pl.when(k + RING - 1 < n_steps)
            def _():
                fetch(k + RING - 1, lax.rem(k + RING - 1, RING)).start()

        @pl.when(k == 0)
        def _():
            acc[...] = jnp.zeros_like(acc)
            if side:
                side.init(side_outs)

        if ringed is not None:
            slot = lax.rem(k, RING)
            fetch(k, slot).wait()
            tiles[ringed] = ring.at[slot]

        tick = (side.begin(k, n_steps, refs[n_in:n_in + s_in], side_outs, refs[first_side:first_side + len(side_scratch)])
                if side else None)
        step(acc, *tiles, tick or (lambda j: None))

        @pl.when(k == n_steps - 1)
        def _():
            part = shape[0] // OUT_PARTS
            copies = []
            for p in range(OUT_PARTS):
                rows = pl.ds(p * part, part)
                narrow[rows, :] = acc[rows, :].astype(narrow.dtype)
                copies.append(pltpu.make_async_copy(narrow.at[rows, :], o_ref.at[rows, :], sem.at[p]))
                copies[-1].start()
            for cp in copies:
                cp.wait()

    outs = pl.pallas_call(
        body, name=name, grid=(n_steps,),
        in_specs=in_specs + (side.in_specs if side else []) + [ANY] * len(extra),
        out_specs=[ANY] + (side.out_specs if side else []),
        out_shape=[jax.ShapeDtypeStruct(shape, BF16)] + (side.out_shape if side else []),
        scratch_shapes=[pltpu.VMEM(shape, F32), pltpu.VMEM(shape, BF16), pltpu.SemaphoreType.DMA((OUT_PARTS,))]
                       + side_scratch + ring_scratch,
        compiler_params=_params("arbitrary"),
    )(*operands, *(side.operands if side else ()), *extra)
    return outs if side else outs[0]


def _dw_in(xb, dph, dog, dpc, w_in, dpre1, after):
    t = xb.shape[0]

    def step(acc, x_ref, dh_ref, dog_ref, dc_ref, tick):
        xv = x_ref[...]
        for g in range(N_GROUPS):
            part = dh_ref[g] if g < 3 else dog_ref[...] if g == 3 else dc_ref[g - 4]
            acc[:, g * GROUP:(g + 1) * GROUP] += _dot(xv, part, TN)
            tick(g, part)

    def begin(k, n_steps, ins, outs, scratch):
        w_ref, dp_ref = ins
        total = [ALPHA * dp_ref[...]]

        def tick(g, part):
            total[0] = total[0] + _dot(part, w_ref[:, g * GROUP:(g + 1) * GROUP], NT)
            if g == N_GROUPS - 1:
                outs[0][...] = total[0]

        return tick

    row = pl.BlockSpec((min(t, GRAD_TILE), D_MODEL), lambda k: (k, 0))
    side = _Side((w_in, dpre1), [_resident((D_MODEL, IN_COLS)), row], [jax.ShapeDtypeStruct((t, D_MODEL), F32)], [row],
                 [], lambda outs: None, begin)
    return _grad_w("dw_in", (xb, dph, dog, dpc), (D_MODEL, GROUP, GROUP, GROUP), (D_MODEL, IN_COLS), step, after, side)


def _strips_of(j, tt):
    per_tick = tt // GATE_STRIP // N_FF
    return [slice(s * GATE_STRIP, (s + 1) * GATE_STRIP) for s in range(j * per_tick, (j + 1) * per_tick)]


def _dw_ff1(h1b, da, dcat, o, proj, gate_norm_w, after):
    t = h1b.shape[0]
    tt = min(t, GRAD_TILE)

    def step(acc, h_ref, da_ref, tick):
        hv = h_ref[...]
        for j in range(N_FF):
            cols = slice(j * FF_BLOCK, (j + 1) * FF_BLOCK)
            acc[:, cols] += _dot(hv, da_ref[:, cols], TN)
            tick(j)

    def init(outs):
        outs[2][...] = jnp.zeros_like(outs[2])

    def begin(k, n_steps, ins, outs, scratch):
        do2_ref, o_ref, og_ref, gnw_ref = ins
        do_ref, dog_ref, dgnw_ref = outs
        total = [jnp.zeros((GATE_STRIP, GROUP), F32)]

        def tick(j):
            gnw = gnw_ref[...]
            for rows in _strips_of(j, tt):
                ov, og, do2 = o_ref[rows, :], og_ref[rows, :], do2_ref[rows, :]
                rs = _per_head(lambda o_h: jnp.broadcast_to(
                    lax.rsqrt(jnp.mean(o_h * o_h, axis=-1, keepdims=True) + EPS), o_h.shape), ov)
                on = ov * rs
                sg = _sigmoid(og)
                sil = og * sg
                don = do2 * gnw * sil
                total[0] = total[0] + do2 * on * sil
                dog_ref[rows, :] = (do2 * on * gnw * (sg * (1.0 + og * (1.0 - sg)))).astype(dog_ref.dtype)
                do_ref[rows, :] = rs * (don - on * _per_head(
                    lambda p_h: jnp.broadcast_to(jnp.mean(p_h, axis=-1, keepdims=True), p_h.shape), don * on))
            if j == N_FF - 1:
                dgnw_ref[...] += jnp.sum(total[0], axis=0, keepdims=True)

        return tick

    tile = pl.BlockSpec((tt, GROUP), lambda k: (k, 0))
    vec = pl.BlockSpec((1, GROUP), lambda k: (0, 0))
    side = _Side(
        (dcat, o, proj, gate_norm_w), [tile, tile, pl.BlockSpec((None, tt, GROUP), lambda k: (3, k, 0)), vec],
        [jax.ShapeDtypeStruct((t, HGRN_WIDTH), F32), jax.ShapeDtypeStruct((t, HGRN_WIDTH), BF16),
         jax.ShapeDtypeStruct((1, HGRN_WIDTH), F32)], [tile, tile, vec], [], init, begin)
    return _grad_w("dw_ff1", (h1b, da), (D_MODEL, D_FF), (D_MODEL, D_FF), step, after, side, ringed=1)


def _dw_ff2(r, dpre2b, dcat, bcu, conv_w):
    t = r.shape[0]
    tt = min(t, GRAD_TILE)
    hb = tt // SUBLANES
    halo = 2 * SUBLANES

    def step(acc, r_ref, d_ref, tick):
        dv = d_ref[...]
        for j in range(N_FF):
            rows = slice(j * FF_BLOCK, (j + 1) * FF_BLOCK)
            acc[rows, :] += _dot(r_ref[:, rows], dv, TN)
            tick(j)

    def init(outs):
        outs[1][...] = jnp.zeros_like(outs[1])

    def begin(k, n_steps, ins, outs, scratch):
        dy_ref, dyn_ref, b_ref, bn_ref, c_ref, u_ref, ch_ref, uh_ref, cw_ref = ins
        dp_ref, dcw_ref = outs
        zbuf, dbuf = scratch
        before = lambda ref: ref[SUBLANES:halo, :].astype(F32)
        zbuf[0:SUBLANES, :] = jnp.where(k > 0, before(ch_ref) * before(uh_ref), 0.0)
        zbuf[SUBLANES:SUBLANES + tt, :] = c_ref[...].astype(F32) * u_ref[...].astype(F32)
        dbuf[0:tt, :] = dy_ref[...] * b_ref[...].astype(F32)
        dbuf[tt:tt + SUBLANES, :] = jnp.where(k < n_steps - 1, dyn_ref[...] * bn_ref[0:SUBLANES, :].astype(F32), 0.0)
        totals = [jnp.zeros((GATE_STRIP, GROUP), F32) for _ in range(3)]

        def tick(j):
            cw = cw_ref[...]
            for rows in _strips_of(j, tt):
                at = lambda buf, shift: buf[shift + rows.start:shift + rows.stop, :]
                z, z1, z2 = at(zbuf, SUBLANES), at(zbuf, SUBLANES - 1), at(zbuf, SUBLANES - 2)
                dyc, d1, d2 = at(dbuf, 0), at(dbuf, 1), at(dbuf, 2)
                yc = cw[2:3, :] * z + cw[1:2, :] * z1 + cw[0:1, :] * z2
                dz = cw[2:3, :] * dyc + cw[1:2, :] * d1 + cw[0:1, :] * d2
                dp_ref[0, rows, :] = (dy_ref[rows, :] * yc).astype(dp_ref.dtype)
                dp_ref[1, rows, :] = (dz * u_ref[rows, :].astype(F32)).astype(dp_ref.dtype)
                dp_ref[2, rows, :] = (dz * c_ref[rows, :].astype(F32)).astype(dp_ref.dtype)
                for n, tap in enumerate((z2, z1, z)):
                    totals[n] = totals[n] + dyc * tap
            if j == N_FF - 1:
                for n in range(3):
                    dcw_ref[n:n + 1, :] += jnp.sum(totals[n], axis=0, keepdims=True)

        return tick

    grp = lambda g: pl.BlockSpec((None, tt, GROUP), lambda k: (g, k, 0))
    prev = lambda g: pl.BlockSpec((None, halo, GROUP), lambda k: (g, jnp.maximum(k * (tt // halo) - 1, 0), 0))
    nxt = lambda g: pl.BlockSpec((None, halo, GROUP), lambda k: (g, jnp.minimum((k + 1) * (tt // halo), t // halo - 1), 0))
    nxt_row = lambda k: jnp.minimum((k + 1) * hb, t // SUBLANES - 1)
    whole = pl.BlockSpec((3, CONV_WIDTH), lambda k: (0, 0))
    side = _Side(
        (dcat, dcat, bcu, bcu, bcu, bcu, bcu, bcu, conv_w),
        [pl.BlockSpec((tt, GROUP), lambda k: (k, 1)), pl.BlockSpec((SUBLANES, GROUP), lambda k: (nxt_row(k), 1)),
         grp(0), nxt(0), grp(1), grp(2), prev(1), prev(2), whole],
        [jax.ShapeDtypeStruct((3, t, CONV_WIDTH), BF16), jax.ShapeDtypeStruct((3, CONV_WIDTH), F32)],
        [pl.BlockSpec((3, tt, GROUP), lambda k: (0, k, 0)), whole],
        [pltpu.VMEM((tt + SUBLANES, GROUP), F32), pltpu.VMEM((tt + SUBLANES, GROUP), F32)], init, begin)
    return _grad_w("dw_ff2", (r, dpre2b), (D_FF, D_MODEL), (D_FF, D_MODEL), step, side=side, ringed=0)


def _place():
    x, y, c = lax.axis_index("x"), lax.axis_index("y"), lax.axis_index("c")
    return x, y, c, 2 * x + y


def _other_chips(x, y):
    return [(1 - x, y), (x, 1 - y), (1 - x, 1 - y)]


def _place_shard(name, w, chip, cols_sharded, after=None):
    rows, cols = w.shape
    tr = min(rows, 256)
    nb = rows // tr
    full = (rows, cols * N_CHIPS) if cols_sharded else (rows * N_CHIPS, cols)
    out_map = (lambda i, s: (i, s[0])) if cols_sharded else (lambda i, s: (s[0] * nb + i, 0))

    def body(s_ref, w_ref, *rest):
        rest[-1][...] = w_ref[...].astype(rest[-1].dtype)

    extra = [] if after is None else [after]
    return pl.pallas_call(
        body, name=name,
        grid_spec=pltpu.PrefetchScalarGridSpec(
            num_scalar_prefetch=1, grid=(nb,),
            in_specs=[pl.BlockSpec((tr, cols), lambda i, s: (i, 0))] + [ANY] * len(extra),
            out_specs=pl.BlockSpec((tr, cols), out_map)),
        out_shape=jax.ShapeDtypeStruct(full, BF16),
        compiler_params=_params("parallel"),
    )(chip, w, *extra)


HBM = pl.BlockSpec(memory_space=pltpu.HBM)
SEM = pl.BlockSpec(memory_space=pltpu.SEMAPHORE)
EFFECT = pltpu.SideEffectType.DATAFLOW_SIDE_EFFECTING


PEER_SETS = {
    "sibling": (0, lambda x, y, c: [(x, y, 1 - c)]),
    "chips": (1, lambda x, y, c: [(1 - x, y, c), (x, 1 - y, c), (1 - x, 1 - y, c)]),
    "neighbours": (2, lambda x, y, c: [(1 - x, y, c), (x, 1 - y, c)]),
}


class _Split:
    def __init__(self, name, arrays, plan, others=(), peers=None, prepare=None, sources=(), scratch=()):
        n_own, arrays = len(arrays), (*arrays, *others)
        n, n_copies, n_in = len(arrays), plan.count, len(arrays) + len(sources)
        self.name, self.plan, self.n = name, plan, n_own
        barrier_id, peer_ids = PEER_SETS[peers] if peers else (None, None)

        def body(*refs):
            send_sems, recv_sems, token = refs[n_in], refs[n_in + 1], refs[n_in + 2 + n]
            if peers:
                x, y, c, _ = _place()
                barrier = pltpu.get_barrier_semaphore()
                for peer in peer_ids(x, y, c):
                    pl.semaphore_signal(barrier, inc=1, device_id=peer, device_id_type=MESH)
            if prepare:
                prepare(refs[:n], refs[n:n_in], refs[n_in + 3 + n:])
            if peers:
                pl.semaphore_wait(barrier, len(peer_ids(0, 0, 0)))
            for k, (src, dst, to) in enumerate(plan(refs[:n])):
                pltpu.make_async_remote_copy(src_ref=src, dst_ref=dst, send_sem=send_sems.at[k], recv_sem=recv_sems.at[k],
                                             device_id=to, device_id_type=MESH).start()
            token[...] = jnp.zeros_like(token)

        outs = pl.pallas_call(
            body, name=name + "_start",
            out_shape=(pltpu.SemaphoreType.DMA((n_copies,)), pltpu.SemaphoreType.DMA((n_copies,)),
                       *[pltpu.HBM(a.shape, a.dtype) for a in arrays], jax.ShapeDtypeStruct((SUBLANES, LANES), F32)),
            in_specs=(HBM,) * n_in, out_specs=(SEM, SEM) + (HBM,) * n + (pl.BlockSpec(memory_space=pltpu.VMEM),),
            input_output_aliases={i: 2 + i for i in range(n)}, scratch_shapes=list(scratch),
            compiler_params=pltpu.CompilerParams(has_side_effects=EFFECT, collective_id=barrier_id),
        )(*[pltpu.with_memory_space_constraint(a, pltpu.HBM) for a in (*arrays, *sources)])
        self.sems, self.arrays, self.others, self.token = outs[:2], outs[2:2 + n_own], outs[2 + n_own:2 + n], outs[-1]

    def wait(self, after):
        n, plan = self.n, self.plan

        def body(*refs):
            send_sems, recv_sems = refs[n], refs[n + 1]
            for k, (src, dst, to) in enumerate(plan(refs[:n])):
                cp = pltpu.make_async_remote_copy(src_ref=src, dst_ref=dst, send_sem=send_sems.at[k],
                                                  recv_sem=recv_sems.at[k], device_id=to, device_id_type=MESH)
                cp.wait_send()
                cp.wait_recv()

        return pl.pallas_call(
            body, name=self.name + "_wait", out_shape=tuple(pltpu.HBM(a.shape, a.dtype) for a in self.arrays),
            in_specs=(HBM,) * n + (SEM, SEM, ANY), out_specs=(HBM,) * n, input_output_aliases={i: i for i in range(n)},
            compiler_params=pltpu.CompilerParams(has_side_effects=EFFECT),
        )(*self.arrays, *self.sems, after)


COLS_SHARDED = (True, False, True, False)
HALF_SHAPES = [(D_MODEL // 2, IN_COLS), (D_MODEL, D_MODEL // 2), (D_MODEL // 2, D_FF), (D_FF, D_MODEL // 2)]
PIECE_SHAPES = [(D_MODEL // 2, IN_COLS // N_CHIPS), (D_MODEL // N_CHIPS, D_MODEL // 2),
                (D_MODEL // 2, D_FF // N_CHIPS), (D_FF // N_CHIPS, D_MODEL // 2)]


def _shard_view(kind, ref, chip):
    if COLS_SHARDED[kind]:
        n = ref.shape[1] // N_CHIPS
        return ref.at[:, pl.ds(chip * n, n)]
    n = ref.shape[0] // N_CHIPS
    return ref.at[pl.ds(chip * n, n), :]


def _half_view(kind, ref, h):
    if COLS_SHARDED[kind]:
        n = ref.shape[0] // 2
        return ref.at[pl.ds(h * n, n), :]
    n = ref.shape[1] // 2
    return ref.at[:, pl.ds(h * n, n)]


def _plan(count):
    def mark(fn):
        fn.count = count
        return fn
    return mark


def _shard_rows_view(kind, ref, chip, part, n_parts):
    if COLS_SHARDED[kind]:
        m, n = ref.shape[0] // n_parts, ref.shape[1] // N_CHIPS
        return ref.at[pl.ds(part * m, m), pl.ds(chip * n, n)]
    m = ref.shape[0] // N_CHIPS // n_parts
    return ref.at[pl.ds((n_parts * chip + part) * m, m), :]


def _shard_half_view(kind, ref, chip, h):
    return _shard_rows_view(kind, ref, chip, h, 2)


def _gather_over_ici(kinds, weights):
    @_plan(2 * len(kinds))
    def plan(refs):
        x, y, c, me = _place()
        mine = [_shard_half_view(kind, ref, me, c) for kind, ref in zip(kinds, refs)]
        return [(v, v, to) for v in mine for to in ((1 - x, y, c), (x, 1 - y, c))]

    return _Split("gather_ici_" + "".join(map(str, kinds)), tuple(weights), plan, peers="neighbours")


def _relay_over_ici(kinds, weights, others=()):
    @_plan(2 * len(kinds))
    def plan(refs):
        x, y, c, _ = _place()
        x_nbr, y_nbr = 2 * (1 - x) + y, 2 * x + (1 - y)
        out = []
        for kind, ref in zip(kinds, refs):
            first, second = (_shard_rows_view(kind, ref, chip, 2 * c + q, 4) for q, chip in ((0, x_nbr), (1, y_nbr)))
            out += [(first, first, (x, 1 - y, c)), (second, second, (1 - x, y, c))]
        return out

    return _Split("relay_ici_" + "".join(map(str, kinds)), tuple(weights), plan, others, peers="neighbours")


def _gather_w_in_over_ici(shard, conv4):
    rows, cols = shard.shape

    @_plan(6)
    def plan(refs):
        x, y, c, me = _place()
        half, conv = _shard_half_view(0, refs[0], me, c), refs[1].at[me]
        return [(v, v, (px, py, c)) for v in (half, conv) for px, py in _other_chips(x, y)]

    def prepare(refs, sources, scratch):
        wide, narrow, sems = scratch
        _, _, _, me = _place()
        load = pltpu.make_async_copy(sources[0], wide, sems.at[0])
        load.start()
        load.wait()
        narrow[...] = wide[...].astype(narrow.dtype)
        store = pltpu.make_async_copy(narrow, _shard_view(0, refs[0], me), sems.at[1])
        store.start()
        store.wait()

    return _Split("gather_w_in_ici", (lax.empty((rows, cols * N_CHIPS), BF16), conv4), plan, peers="chips",
                  prepare=prepare, sources=(shard,),
                  scratch=(pltpu.VMEM((rows, cols), F32), pltpu.VMEM((rows, cols), BF16), pltpu.SemaphoreType.DMA((2,))))


def _gather_over_d2d(kinds, weights):
    @_plan(3 * len(kinds))
    def plan(refs):
        x, y, c, _ = _place()
        got = [_shard_half_view(kind, ref, 2 * px + py, c) for kind, ref in zip(kinds, refs)
               for px, py in _other_chips(x, y)]
        return [(v, v, (x, y, 1 - c)) for v in got]

    return _Split("gather_d2d_" + "".join(map(str, kinds)), tuple(weights), plan, peers="sibling")


def _swap_halves(kinds, grads):
    @_plan(len(kinds))
    def plan(refs):
        x, y, c, _ = _place()
        return [(_half_view(kind, g, 1 - c), land, (x, y, 1 - c))
                for kind, g, land in zip(kinds, refs[:len(kinds)], refs[len(kinds):])]

    lands = [lax.empty(HALF_SHAPES[kind], g.dtype) for kind, g in zip(kinds, grads)]
    return _Split("swap_halves_" + "".join(map(str, kinds)), (*grads, *lands), plan, peers="sibling")


def _block_rows(cols, elements):
    return 1 << ((elements // cols).bit_length() - 1)


def _grid_steps(shapes, elements):
    rows, cols = max(shapes, key=lambda shape: shape[0] * shape[1])
    return rows // min(rows, _block_rows(cols, elements))


def _add_half(name, kinds, grads, recvs, core):
    n = len(kinds)
    shapes = [recv.shape for recv in recvs]
    nb = _grid_steps(shapes, 1 << 20)

    def body(c_ref, *refs):
        for g_ref, r_ref, o_ref in zip(refs[:n], refs[n:2 * n], refs[2 * n:]):
            o_ref[...] = (g_ref[...].astype(F32) + r_ref[...].astype(F32)).astype(o_ref.dtype)

    own = [pl.BlockSpec((rows // nb, cols), (lambda i, c_ref: (c_ref[0] * nb + i, 0)) if COLS_SHARDED[k] else
                        (lambda i, c_ref: (i, c_ref[0]))) for k, (rows, cols) in zip(kinds, shapes)]
    blocks = [pl.BlockSpec((rows // nb, cols), lambda i, c_ref: (i, 0)) for rows, cols in shapes]
    return pl.pallas_call(
        body, name=name,
        grid_spec=pltpu.PrefetchScalarGridSpec(
            num_scalar_prefetch=1, grid=(nb,), in_specs=own + blocks, out_specs=blocks),
        out_shape=[jax.ShapeDtypeStruct(shape, BF16) for shape in shapes],
        compiler_params=_params("parallel"),
    )(core, *grads, *recvs)


def _exchange_pieces(kinds, halves, pack=None):
    n_p, n = N_CHIPS - 1, len(kinds)

    @_plan(n_p * n + (0 if pack is None else N_DEV - 1))
    def plan(refs):
        x, y, c, _ = _place()
        copies = []
        if pack is not None:
            me = 4 * x + 2 * y + c
            peers = [((1 - x) if m & 4 else x, (1 - y) if m & 2 else y, (1 - c) if m & 1 else c) for m in range(1, N_DEV)]
            copies += [(refs[2 * n], refs[2 * n + 1].at[me], peer) for peer in peers]
        return copies + [(_shard_view(kind, half, 2 * px + py), land.at[j], (px, py, c))
                         for j, (px, py) in enumerate(_other_chips(x, y))
                         for kind, half, land in zip(kinds, refs[:n], refs[n:2 * n])]

    lands = [lax.empty((n_p,) + PIECE_SHAPES[kind], BF16) for kind in kinds]
    small = () if pack is None else (pack, lax.empty((N_DEV,) + pack.shape, F32))
    return _Split("exchange_pieces_" + "".join(map(str, kinds)), (*halves, *lands, *small), plan,
                  peers="chips" if pack is None else None)


def _sum_pieces(name, kinds, halves, slots, place, after):
    n, n_p = len(kinds), N_CHIPS - 1
    shapes = [slot.shape[1:] for slot in slots]
    nb = _grid_steps(shapes, 1 << 18)

    def body(s_ref, *refs):
        for own_ref, slot_ref, o_ref in zip(refs[:n], refs[n:2 * n], refs[2 * n + 1:]):
            total = own_ref[...].astype(F32)
            for j in range(n_p):
                total = total + slot_ref[j].astype(F32)
            o_ref[...] = total

    own, out, shards = [], [], []
    for k, (rows, cols) in zip(kinds, shapes):
        if COLS_SHARDED[k]:
            own_map, out_map, shard = (lambda i, s: (i, s[0])), (lambda i, s: (s[1] * nb + i, 0)), (2 * rows, cols)
        else:
            own_map, out_map, shard = (lambda i, s: (s[0] * nb + i, 0)), (lambda i, s: (i, s[1])), (rows, 2 * cols)
        own.append(pl.BlockSpec((rows // nb, cols), own_map))
        out.append(pl.BlockSpec((rows // nb, cols), out_map))
        shards.append(jax.ShapeDtypeStruct(shard, F32))
    landed = [pl.BlockSpec((n_p, rows // nb, cols), lambda i, s: (0, i, 0)) for rows, cols in shapes]
    return pl.pallas_call(
        body, name=name,
        grid_spec=pltpu.PrefetchScalarGridSpec(
            num_scalar_prefetch=1, grid=(nb,), in_specs=own + landed + [ANY], out_specs=out),
        out_shape=shards,
        compiler_params=_params("parallel"),
    )(place, *halves, *slots, after)


def _join_halves(kinds, shards):
    @_plan(len(kinds))
    def plan(refs):
        x, y, c, _ = _place()
        return [(_half_view(kind, g, c), _half_view(kind, g, c), (x, y, 1 - c)) for kind, g in zip(kinds, refs)]

    return _Split("join_halves_" + "".join(map(str, kinds)), tuple(shards), plan, peers="sibling")


N_DEV = 8


def _sum_shared(pack, land, device):
    def body(d_ref, p_ref, l_ref, o_ref):
        me = d_ref[0]
        total = jnp.where(me == 0, p_ref[...], l_ref[0])
        for d in range(1, N_DEV):
            total = total + jnp.where(me == d, p_ref[...], l_ref[d])
        o_ref[...] = total

    return pl.pallas_call(
        body, name="sum_shared",
        grid_spec=pltpu.PrefetchScalarGridSpec(
            num_scalar_prefetch=1, grid=(1,),
            in_specs=[pl.BlockSpec(pack.shape, lambda i, d: (0, 0)), pl.BlockSpec(land.shape, lambda i, d: (0, 0, 0))],
            out_specs=pl.BlockSpec(pack.shape, lambda i, d: (0, 0))),
        out_shape=jax.ShapeDtypeStruct(pack.shape, F32),
    )(device, pack, land)


def _adamw(name, weights, after=None):
    shapes = [w.shape for w, _, _, _ in weights]
    nb = _grid_steps(shapes, 1 << 18)
    extra = [] if after is None else [after]
    n_in = 4 * len(weights) + len(extra)

    def body(*refs):
        for k in range(len(weights)):
            w_ref, g_ref, m_ref, v_ref = refs[4 * k:4 * k + 4]
            go_ref, d_ref, nm_ref, nv_ref = refs[n_in + 4 * k:n_in + 4 * k + 4]
            g = g_ref[...]
            go_ref[...] = g
            d_ref[...], nm_ref[...], nv_ref[...] = _adam_step(w_ref[...], g, m_ref[...], v_ref[...])

    blocks = [pl.BlockSpec((rows // nb, cols), lambda i: (i, 0)) for rows, cols in shapes for _ in range(4)]
    outs = pl.pallas_call(
        body, name=name, grid=(nb,), in_specs=blocks + [ANY] * len(extra), out_specs=blocks,
        out_shape=[jax.ShapeDtypeStruct(shape, F32) for shape in shapes for _ in range(4)],
        compiler_params=_params("parallel"),
    )(*[a for group in weights for a in group], *extra)
    return [outs[4 * k:4 * k + 4] for k in range(len(weights))]


def _adam_step(w, g, m, v):
    nm = ADAM_B1 * m + (1.0 - ADAM_B1) * g
    nv = ADAM_B2 * v + (1.0 - ADAM_B2) * jnp.square(g)
    m_hat = nm * (1.0 / (1.0 - ADAM_B1 ** ADAM_STEP))
    v_hat = nv * (1.0 / (1.0 - ADAM_B2 ** ADAM_STEP))
    return -ADAM_LR * (m_hat / (jnp.sqrt(v_hat) + ADAM_EPS) + ADAM_WD * w), nm, nv


def _adamw_small(tot, chip, weights, ms, vs, after):
    n, half = len(weights), D_MODEL // 2

    def body(chip_ref, tot_ref, *refs):
        ins, outs = refs[:3 * n], refs[3 * n + 1:]
        tot = tot_ref[...]
        conv_all = jnp.concatenate([tot[5:6, half:], tot[6:7, :half], tot[6:7, half:]], axis=0)
        conv = sum(jnp.where(chip_ref[0] == s, conv_all[:, s * LANES:(s + 1) * LANES], 0.0) for s in range(N_CHIPS))
        grads = [jnp.concatenate([tot[4:5, :half], tot[4:5, half:]], axis=0), tot[5:6, :half], conv,
                 tot[0:1], tot[1:2], tot[2:3], tot[3:4]]
        for k, g in enumerate(grads):
            delta, nm, nv = _adam_step(ins[k][...], g, ins[n + k][...], ins[2 * n + k][...])
            outs[k][...], outs[n + k][...], outs[2 * n + k][...], outs[3 * n + k][...] = g, delta, nm, nv
        outs[4 * n][...] = tot[7:8, 0:1]

    whole = lambda a: pl.BlockSpec(a.shape, lambda i, s: (0,) * a.ndim)
    arrays = (*weights, *ms, *vs)
    loss = jax.ShapeDtypeStruct((1, 1), F32)
    return pl.pallas_call(
        body, name="adamw_small",
        grid_spec=pltpu.PrefetchScalarGridSpec(
            num_scalar_prefetch=1, grid=(1,), in_specs=[whole(tot)] + [whole(a) for a in arrays] + [ANY],
            out_specs=[whole(a) for a in weights] * 4 + [whole(loss)]),
        out_shape=[jax.ShapeDtypeStruct(a.shape, F32) for a in weights] * 4 + [loss],
    )(chip, tot, *arrays, after)


def kernel(x, w_in, lb_logits, gate_norm_w, conv_w, w_out, ln1_g, ln1_b, w_ff1, w_ff2, ln2_g, ln2_b, loss_target, m_w_in, m_lb_logits, m_gate_norm_w, m_conv_w, m_w_out, m_ln1_g, m_ln1_b, m_w_ff1, m_w_ff2, m_ln2_g, m_ln2_b, v_w_in, v_lb_logits, v_gate_norm_w, v_conv_w, v_w_out, v_ln1_g, v_ln1_b, v_w_ff1, v_w_ff2, v_ln2_g, v_ln2_b):
    xs, tgt = x[0], loss_target[0]
    chip = 2 * lax.axis_index("x") + lax.axis_index("y")
    core = lax.axis_index("c").astype(jnp.int32).reshape(1)
    chip1 = chip.astype(jnp.int32).reshape(1)
    place = jnp.concatenate([chip1, core])

    conv4 = lax.dynamic_update_slice(jnp.zeros((N_CHIPS,) + conv_w.shape[1:], F32), conv_w, (chip, 0, 0))
    ici_in = _gather_w_in_over_ici(w_in[0], conv4)
    rest = (1, 2, 3)
    ici_rest = _gather_over_ici(rest, (_place_shard("place_w_out", w_out[0], chip1, False, after=ici_in.token),
                                       _place_shard("place_w_ff1", w_ff1[0], chip1, True, after=ici_in.token),
                                       _place_shard("place_w_ff2", w_ff2[0], chip1, False, after=ici_in.token)))
    wb_in, cv4 = ici_in.wait(ici_rest.token)
    d2d_in = _gather_over_d2d((0,), (wb_in,))
    wb_in, = d2d_in.wait(d2d_in.token)
    conv_full = cv4.transpose(1, 0, 2).reshape(3, CONV_WIDTH)

    proj, bcu, xb, cat_c = _in_proj(xs, wb_in, conv_full, ici_rest.token)
    relay_rest = _relay_over_ici(rest, ici_rest.wait(proj))
    o, states = _hgrn_fwd(proj, lb_logits, relay_rest.token)
    d2d_rest = _gather_over_d2d(rest, relay_rest.wait(o))
    cat_h = _gate_fwd(proj, o, gate_norm_w, d2d_rest.token)
    wb_out, wb_ff1, wb_ff2 = d2d_rest.wait(cat_h)

    (h1b, r, da, dpre2b, dpre1, dcat, g_ln1_g, g_ln1_b, g_ln2_g, g_ln2_b, loss8, g_out_local) = _sublayers(
        cat_h, cat_c, xs, tgt, wb_out, wb_ff1, wb_ff2, ln1_g, ln1_b, ln2_g, ln2_b)

    names = ("w_in", "w_out", "w_ff1", "w_ff2")

    def named(prefix, kinds):
        return prefix + "".join("_" + names[k] for k in kinds)

    def add_halves(kinds, grads, lands):
        return _add_half(named("add_half", kinds), kinds, grads, lands, core)

    def sum_pieces(kinds, halves, lands, after):
        return _sum_pieces(named("sum_pieces", kinds), kinds, halves, lands, place, after)

    early = (1, 2, 3)
    g_ff2_local, dpc, g_conv = _dw_ff2(r, dpre2b, dcat, bcu, conv_full)
    swap_a = _swap_halves((1, 3), (g_out_local, g_ff2_local))
    g_ff1_local, do, dog, g_gnw = _dw_ff1(h1b, da, dcat, o, proj, gate_norm_w, swap_a.token)
    swap_b = _swap_halves((2,), (g_ff1_local,))
    swapped_a = swap_a.wait(swap_b.token)
    halves_a = add_halves((1, 3), swapped_a[:2], swapped_a[2:])
    swapped_b = swap_b.wait(halves_a[1])
    halves = (halves_a[0], *add_halves((2,), swapped_b[:1], swapped_b[1:]), halves_a[1])
    exch = _exchange_pieces(early, halves)
    dph, g_lbl = _hgrn_bwd(proj, do, states, lb_logits, exch.token)
    g_in_local, grad_x = _dw_in(xb, dph, dog, dpc, wb_in, dpre1, dph)

    late = (0,)
    swap = _swap_halves(late, (g_in_local,))
    exchanged = exch.wait(swap.token)
    pack = jnp.concatenate([
        g_ln1_g, g_ln1_b, g_ln2_g, g_ln2_b,
        jnp.concatenate([g_lbl[0:1], g_lbl[1:2]], axis=1),
        jnp.concatenate([g_gnw, g_conv[0:1]], axis=1),
        jnp.concatenate([g_conv[1:2], g_conv[2:3]], axis=1),
        jnp.concatenate([loss8[0:1], jnp.zeros((1, D_MODEL - LANES), F32)], axis=1)], axis=0)
    join_a = _join_halves((2,), sum_pieces((2,), exchanged[1:2], exchanged[4:5], swap.token))
    swapped = swap.wait(join_a.token)
    exch = _exchange_pieces(late, add_halves(late, swapped[:1], swapped[1:]), pack)
    join_b = _join_halves((1, 3), sum_pieces((1, 3), exchanged[0:3:2], exchanged[3:6:2], exch.token))
    g_w_ff1, = join_a.wait(join_b.token)
    (g_w_ff1, d_ff1, nm_ff1, nv_ff1), = _adamw("adamw_w_ff1", [(w_ff1[0], g_w_ff1, m_w_ff1[0], v_w_ff1[0])])
    g_w_out, g_w_ff2 = join_b.wait(d_ff1)
    (g_w_ff2, d_ff2, nm_ff2, nv_ff2), (g_w_out, d_out, nm_out, nv_out) = _adamw(
        "adamw_w_ff2_w_out", [(w_ff2[0], g_w_ff2, m_w_ff2[0], v_w_ff2[0]), (w_out[0], g_w_out, m_w_out[0], v_w_out[0])])
    exchanged = exch.wait(d_out)
    tot = _sum_shared(exchanged[2], exchanged[3], 2 * chip1 + core)
    join = _join_halves(late, sum_pieces(late, exchanged[:1], exchanged[1:2], tot))
    small = ("lb_logits", "gate_norm_w", "conv_w", "ln1_g", "ln1_b", "ln2_g", "ln2_b")
    small_out = _adamw_small(
        tot, chip1, (lb_logits, gate_norm_w, conv_w[0], ln1_g, ln1_b, ln2_g, ln2_b),
        (m_lb_logits, m_gate_norm_w, m_conv_w[0], m_ln1_g, m_ln1_b, m_ln2_g, m_ln2_b),
        (v_lb_logits, v_gate_norm_w, v_conv_w[0], v_ln1_g, v_ln1_b, v_ln2_g, v_ln2_b), join.token)
    g_w_in, = join.wait(small_out[0])
    (g_w_in, d_in, nm_in, nv_in), = _adamw("adamw_w_in", [(w_in[0], g_w_in, m_w_in[0], v_w_in[0])])
    loss = small_out[4 * len(small)][0, 0]

    def results(n_kind, large):
        out = dict(zip(small, small_out[n_kind * len(small):(n_kind + 1) * len(small)]))
        out["conv_w"] = out["conv_w"][None]
        out.update({name: a[None] for name, a in zip(("w_in", "w_out", "w_ff1", "w_ff2"), large)})
        return [out[name] for name in ("w_in", "lb_logits", "gate_norm_w", "conv_w", "w_out", "ln1_g", "ln1_b",
                                       "w_ff1", "w_ff2", "ln2_g", "ln2_b")]

    return (loss, grad_x[None], *results(0, (g_w_in, g_w_out, g_w_ff1, g_w_ff2)),
            *results(1, (d_in, d_out, d_ff1, d_ff2)), *results(2, (nm_in, nm_out, nm_ff1, nm_ff2)),
            *results(3, (nv_in, nv_out, nv_ff1, nv_ff2)))
```

```python
import jax
import jax.numpy as jnp
from jax import lax
from jax.experimental import pallas as pl
from jax.experimental.pallas import tpu as pltpu

F32 = jnp.float32
BF16 = jnp.bfloat16
MXU_DTYPE = jnp.bfloat16

D_MODEL = 1024
HGRN_WIDTH = 512
HEAD_DIM = 128
N_HEADS = 4
CONV_WIDTH = 512
CHUNK = 64
D_FF = 4096
IN_COLS = 3584
GROUP = 512
N_GROUPS = IN_COLS // GROUP
ALPHA = 2.0 ** 0.25
EPS = 1e-5
N_CHIPS = 4
ADAM_LR, ADAM_B1, ADAM_B2, ADAM_EPS, ADAM_WD, ADAM_STEP = 0.001, 0.9, 0.999, 1e-08, 0.01, 10

LANES = 128
SUBLANES = 8
VMEM_LIMIT = 56 * 1024 * 1024
FF_BLOCK = 1024
N_FF = D_FF // FF_BLOCK
GATE_STRIP = 64

NN = (((1,), (0,)), ((), ()))
NT = (((1,), (1,)), ((), ()))
TN = (((0,), (0,)), ((), ()))
MESH = pl.DeviceIdType.MESH
ANY = pl.BlockSpec(memory_space=pl.ANY)


def _dot(a, b, dims):
    return lax.dot_general(a.astype(MXU_DTYPE), b.astype(MXU_DTYPE), dims, preferred_element_type=F32)


def _dot_exact(ones, v):
    ones = ones.astype(jnp.bfloat16)
    hi = v.astype(jnp.bfloat16)
    rest = v - hi.astype(F32)
    mid = rest.astype(jnp.bfloat16)
    low = (rest - mid.astype(F32)).astype(jnp.bfloat16)
    return sum(lax.dot_general(ones, part, NN, preferred_element_type=F32) for part in (hi, mid, low))


def _params(*sem):
    return pltpu.CompilerParams(dimension_semantics=sem, vmem_limit_bytes=VMEM_LIMIT)


def _resident(shape):
    return pl.BlockSpec(shape, lambda *_: (0,) * len(shape), pipeline_mode=pl.Buffered(1))


def _sigmoid(v):
    return 1.0 / (1.0 + jnp.exp(-v))


def _lower_bound(lbl):
    m = jnp.max(lbl, axis=0, keepdims=True)
    e = jnp.exp(lbl - m)
    s = e / jnp.sum(e, axis=0, keepdims=True)
    return s[0:1, :], s[1:2, :]


def _heads(v):
    return [v[:, h * HEAD_DIM:(h + 1) * HEAD_DIM] for h in range(N_HEADS)]


def _per_head(fn, *arrays):
    return jnp.concatenate([fn(*parts) for parts in zip(*map(_heads, arrays))], axis=1)


def _in_proj(x, w_in, conv_w, after):
    t = x.shape[0]
    tm = min(t, 512)

    def body(x_ref, w_ref, cw_ref, after_ref, o_ref, bcu_ref, xb_ref, y_ref, zbuf):
        @pl.when(pl.program_id(0) == 0)
        def _():
            zbuf[tm:tm + SUBLANES, :] = jnp.zeros((SUBLANES, CONV_WIDTH), F32)

        xb = x_ref[...].astype(xb_ref.dtype)
        xb_ref[...] = xb
        group = lambda g: _dot(xb, w_ref[:, g * GROUP:(g + 1) * GROUP], NN)
        for g in range(4):
            o_ref[g] = group(g)
        b_gate, c_gate, u = group(4), group(5), group(6)
        for n, part in enumerate((b_gate, c_gate, u)):
            bcu_ref[n] = part.astype(bcu_ref.dtype)
        zbuf[0:SUBLANES, :] = zbuf[tm:tm + SUBLANES, :]
        zbuf[SUBLANES:SUBLANES + tm, :] = c_gate * u
        cw = cw_ref[...]
        at = lambda shift: zbuf[shift:shift + tm, :]
        conv = cw[2:3, :] * at(SUBLANES) + cw[1:2, :] * at(SUBLANES - 1) + cw[0:1, :] * at(SUBLANES - 2)
        y_ref[...] = (b_gate * conv).astype(y_ref.dtype)

    return pl.pallas_call(
        body, name="in_proj", grid=(t // tm,),
        in_specs=[pl.BlockSpec((tm, D_MODEL), lambda i: (i, 0)), _resident((D_MODEL, IN_COLS)),
                  pl.BlockSpec((3, CONV_WIDTH), lambda i: (0, 0)), ANY],
        out_specs=[pl.BlockSpec((4, tm, GROUP), lambda i: (0, i, 0)), pl.BlockSpec((3, tm, GROUP), lambda i: (0, i, 0)),
                   pl.BlockSpec((tm, D_MODEL), lambda i: (i, 0)), pl.BlockSpec((tm, CONV_WIDTH), lambda i: (i, 0))],
        out_shape=[jax.ShapeDtypeStruct((4, t, GROUP), F32), jax.ShapeDtypeStruct((3, t, GROUP), BF16),
                   jax.ShapeDtypeStruct((t, D_MODEL), BF16), jax.ShapeDtypeStruct((t, CONV_WIDTH), BF16)],
        scratch_shapes=[pltpu.VMEM((tm + SUBLANES, CONV_WIDTH), F32)],
        compiler_params=_params("arbitrary"),
    )(x, w_in, conv_w, after)


def _gates(fp, lb):
    sig = _sigmoid(fp)
    f = lb + (1.0 - lb) * sig
    return sig, f, jnp.log(f), 1.0 - f


def _chunk_masks():
    row = lax.broadcasted_iota(jnp.int32, (CHUNK, CHUNK), 0)
    col = lax.broadcasted_iota(jnp.int32, (CHUNK, CHUNK), 1)
    return row >= col, row <= col


def _hgrn_fwd(proj, lb_logits, after):
    t = proj.shape[1]
    tb = min(t, 512)
    ncb = tb // CHUNK

    def body(q_ref, f_ref, v_ref, lbl_ref, after_ref, o_ref, st_ref, s_scr):
        @pl.when(pl.program_id(0) == 0)
        def _():
            s_scr[...] = jnp.zeros_like(s_scr)

        lb, _ = _lower_bound(lbl_ref[...])
        causal, _ = _chunk_masks()

        every = range(ncb)
        rows = [slice(c * CHUNK, (c + 1) * CHUNK) for c in every]
        q, v = [q_ref[r, :] for r in rows], [v_ref[r, :] for r in rows]
        gates = [_gates(f_ref[r, :], lb) for r in rows]
        k = [gt[3] for gt in gates]
        b = [_dot_exact(causal, gt[2]) for gt in gates]
        mid, last = [x[CHUNK // 2:CHUNK // 2 + 1, :] for x in b], [x[CHUNK - 1:CHUNK, :] for x in b]
        qt = [q[c] * jnp.exp(b[c] - mid[c]) for c in every]
        kt = [k[c] * jnp.exp(mid[c] - b[c]) for c in every]
        qi = [q[c] * jnp.exp(b[c]) for c in every]
        ks = [k[c] * jnp.exp(last[c] - b[c]) for c in every]
        dec = [jnp.exp(x) for x in last]
        scores = [[jnp.where(causal, _dot(a, b_, NT), 0.0) for a, b_ in zip(_heads(qt[c]), _heads(kt[c]))] for c in every]
        intra = [[_dot(s, v_h, NN) for s, v_h in zip(scores[c], _heads(v[c]))] for c in every]
        update = [_per_head(lambda v_h, ks_h: _dot(v_h, ks_h, TN), v[c], ks[c]) for c in every]

        st = s_scr[...]
        states = []
        for c in every:
            states.append(st)
            st_ref[c] = st
            st = dec[c] * st + update[c]
        s_scr[...] = st

        o_ref[...] = jnp.concatenate(
            [jnp.concatenate([i_h + _dot(qi_h, st_h, NT) for i_h, qi_h, st_h in
                              zip(intra[c], _heads(qi[c]), _heads(states[c]))], axis=1) for c in every], axis=0)

    grp = lambda g: pl.BlockSpec((None, tb, GROUP), lambda i: (g, i, 0))
    return pl.pallas_call(
        body, name="hgrn_fwd", grid=(t // tb,),
        in_specs=[grp(0), grp(1), grp(2), pl.BlockSpec((2, HGRN_WIDTH), lambda i: (0, 0)), ANY],
        out_specs=[pl.BlockSpec((tb, HGRN_WIDTH), lambda i: (i, 0)),
                   pl.BlockSpec((ncb, HEAD_DIM, HGRN_WIDTH), lambda i: (i, 0, 0))],
        out_shape=[jax.ShapeDtypeStruct((t, HGRN_WIDTH), F32),
                   jax.ShapeDtypeStruct((t // CHUNK, HEAD_DIM, HGRN_WIDTH), F32)],
        scratch_shapes=[pltpu.VMEM((HEAD_DIM, HGRN_WIDTH), F32)],
        compiler_params=_params("arbitrary"),
    )(proj, proj, proj, lb_logits, after)


def _gate_fwd(proj, o, gate_norm_w, after):
    t = proj.shape[1]
    tb = min(t, 1024)

    def body(o_ref, og_ref, gnw_ref, after_ref, out_ref):
        gnw = gnw_ref[...]
        for s in range(tb // GATE_STRIP):
            rows = slice(s * GATE_STRIP, (s + 1) * GATE_STRIP)
            og = og_ref[rows, :]
            on = _per_head(lambda o_h: o_h * lax.rsqrt(jnp.mean(o_h * o_h, axis=-1, keepdims=True) + EPS), o_ref[rows, :])
            out_ref[rows, :] = (on * gnw * (og * _sigmoid(og))).astype(out_ref.dtype)

    tile = pl.BlockSpec((tb, GROUP), lambda i: (i, 0))
    return pl.pallas_call(
        body, name="gate_fwd", grid=(t // tb,),
        in_specs=[tile, pl.BlockSpec((None, tb, GROUP), lambda i: (3, i, 0)), pl.BlockSpec((1, GROUP), lambda i: (0, 0)), ANY],
        out_specs=tile,
        out_shape=jax.ShapeDtypeStruct((t, HGRN_WIDTH), BF16),
        compiler_params=_params("parallel"),
    )(o, proj, gate_norm_w, after)


def _ln_bwd(dy, xhat, rstd, g):
    dxhat = dy * g
    m1 = jnp.mean(dxhat, axis=-1, keepdims=True)
    m2 = jnp.mean(dxhat * xhat, axis=-1, keepdims=True)
    return rstd * (dxhat - m1 - xhat * m2)


def _layer_norm(pre):
    xc = pre - jnp.mean(pre, axis=-1, keepdims=True)
    rstd = lax.rsqrt(jnp.mean(xc * xc, axis=-1, keepdims=True) + EPS)
    return xc * rstd, rstd


def _sublayers(cat_h, cat_c, x, target, w_out, w_ff1, w_ff2, g1, b1, g2, b2):
    t = x.shape[0]
    tm = min(t, 256)

    def body(ch_ref, cc_ref, x_ref, tg_ref, wo_ref, w1_ref, w2_ref, g1_ref, b1_ref, g2_ref, b2_ref,
             h1_ref, r_ref, da_ref, dp2b_ref, dp1_ref, dcat_ref, dg1_ref, db1_ref, dg2_ref, db2_ref, loss_ref, gwo_ref,
             gwo_acc, gwo_narrow, sem):
        @pl.when(pl.program_id(0) == 0)
        def _():
            for ref in (dg1_ref, db1_ref, dg2_ref, db2_ref, loss_ref, gwo_acc):
                ref[...] = jnp.zeros_like(ref)

        mix = _dot(ch_ref[...], wo_ref[0:GROUP, :], NN) + _dot(cc_ref[...], wo_ref[GROUP:2 * GROUP, :], NN)
        xhat1, rstd1 = _layer_norm(ALPHA * x_ref[...] + mix)
        h1 = xhat1 * g1_ref[...] + b1_ref[...]
        h1b = h1.astype(h1_ref.dtype)
        h1_ref[...] = h1b
        mlp = jnp.zeros((tm, D_MODEL), F32)
        for j in range(N_FF):
            cols = slice(j * FF_BLOCK, (j + 1) * FF_BLOCK)
            r = jnp.square(jnp.maximum(_dot(h1b, w1_ref[:, cols], NN), 0.0)).astype(r_ref.dtype)
            r_ref[:, cols] = r
            mlp = mlp + _dot(r, w2_ref[cols, :], NN)
        xhat2, rstd2 = _layer_norm(ALPHA * h1 + mlp)
        err = xhat2 * g2_ref[...] + b2_ref[...] - tg_ref[...]
        loss_ref[...] += 0.5 * jnp.sum(jnp.mean(err * err, axis=-1, keepdims=True))
        dy = err * (1.0 / D_MODEL)
        dg2_ref[...] += jnp.sum(dy * xhat2, axis=0, keepdims=True)
        db2_ref[...] += jnp.sum(dy, axis=0, keepdims=True)
        dp2 = _ln_bwd(dy, xhat2, rstd2, g2_ref[...])
        dp2b = dp2.astype(dp2b_ref.dtype)
        dp2b_ref[...] = dp2b
        back = jnp.zeros((tm, D_MODEL), F32)
        for j in range(N_FF):
            cols = slice(j * FF_BLOCK, (j + 1) * FF_BLOCK)
            dr = _dot(dp2b, w2_ref[cols, :], NT)
            da = (dr * (2.0 * jnp.sqrt(r_ref[:, cols].astype(F32)))).astype(da_ref.dtype)
            da_ref[:, cols] = da
            back = back + _dot(da, w1_ref[:, cols], NT)
        dh1 = ALPHA * dp2 + back
        dg1_ref[...] += jnp.sum(dh1 * xhat1, axis=0, keepdims=True)
        db1_ref[...] += jnp.sum(dh1, axis=0, keepdims=True)
        dp1 = _ln_bwd(dh1, xhat1, rstd1, g1_ref[...])
        dp1b = dp1.astype(MXU_DTYPE)
        dp1_ref[...] = dp1
        dcat_ref[...] = _dot(dp1b, wo_ref[...], NT)
        gwo_acc[0:GROUP, :] += _dot(ch_ref[...], dp1b, TN)
        gwo_acc[GROUP:2 * GROUP, :] += _dot(cc_ref[...], dp1b, TN)

        @pl.when(pl.program_id(0) == pl.num_programs(0) - 1)
        def _():
            gwo_narrow[...] = gwo_acc[...].astype(gwo_narrow.dtype)
            copy = pltpu.make_async_copy(gwo_narrow, gwo_ref, sem.at[0])
            copy.start()
            copy.wait()

    row = pl.BlockSpec((tm, D_MODEL), lambda i: (i, 0))
    wide = pl.BlockSpec((tm, D_FF), lambda i: (i, 0))
    vec = pl.BlockSpec((1, D_MODEL), lambda i: (0, 0))
    narrow = lambda dtype: jax.ShapeDtypeStruct((t, D_MODEL), dtype)
    return pl.pallas_call(
        body, name="sublayers", grid=(t // tm,),
        in_specs=[pl.BlockSpec((tm, GROUP), lambda i: (i, 0)), pl.BlockSpec((tm, GROUP), lambda i: (i, 0)), row, row,
                  _resident((D_MODEL, D_MODEL)),
                  _resident((D_MODEL, D_FF)), _resident((D_FF, D_MODEL)), vec, vec, vec, vec],
        out_specs=[row, wide, wide, row, row, row, vec, vec, vec, vec,
                   pl.BlockSpec((SUBLANES, LANES), lambda i: (0, 0)), ANY],
        out_shape=[narrow(BF16), jax.ShapeDtypeStruct((t, D_FF), BF16), jax.ShapeDtypeStruct((t, D_FF), BF16),
                   narrow(BF16), narrow(F32), narrow(F32)]
                  + [jax.ShapeDtypeStruct((1, D_MODEL), F32)] * 4
                  + [jax.ShapeDtypeStruct((SUBLANES, LANES), F32), jax.ShapeDtypeStruct((D_MODEL, D_MODEL), BF16)],
        scratch_shapes=[pltpu.VMEM((D_MODEL, D_MODEL), F32), pltpu.VMEM((D_MODEL, D_MODEL), BF16),
                        pltpu.SemaphoreType.DMA((1,))],
        compiler_params=_params("arbitrary"),
    )(cat_h, cat_c, x, target, w_out, w_ff1, w_ff2, g1, b1, g2, b2)


def _hgrn_bwd(proj, do, states, lb_logits, after):
    t = proj.shape[1]
    tb = min(t, 512)
    ncb = tb // CHUNK
    nblk = t // tb

    def body(proj_ref, do_ref, st_ref, lbl_ref, after_ref, dp_ref, dlbl_ref, ds_scr, dlb_scr, ring, ring_sems):
        i = pl.program_id(0)

        def fetch(step_index, slot):
            block = proj_ref.at[pl.ds(0, 3), pl.ds((nblk - 1 - step_index) * tb, tb), :]
            return pltpu.make_async_copy(block, ring.at[slot], ring_sems.at[slot])

        @pl.when(i == 0)
        def _():
            for first in range(min(RING - 1, nblk)):
                fetch(first, first).start()
            ds_scr[...] = jnp.zeros_like(ds_scr)
            dlb_scr[...] = jnp.zeros_like(dlb_scr)

        @pl.when(i + RING - 1 < nblk)
        def _():
            fetch(i + RING - 1, lax.rem(i + RING - 1, RING)).start()

        slot = lax.rem(i, RING)
        fetch(i, slot).wait()
        q_ref, f_ref, v_ref = (ring.at[slot, g] for g in range(3))

        lb, s1 = _lower_bound(lbl_ref[...])
        causal, anti = _chunk_masks()
        every = range(ncb)
        rows = [slice(c * CHUNK, (c + 1) * CHUNK) for c in every]
        q, v, do = ([ref[r, :] for r in rows] for ref in (q_ref, v_ref, do_ref))
        st = [st_ref[c] for c in every]
        gates = [_gates(f_ref[r, :], lb) for r in rows]
        sig, f, k = ([gt[n] for gt in gates] for n in (0, 1, 3))
        b = [_dot_exact(causal, gt[2]) for gt in gates]
        mid, last = [x[CHUNK // 2:CHUNK // 2 + 1, :] for x in b], [x[CHUNK - 1:CHUNK, :] for x in b]
        e_q = [jnp.exp(b[c] - mid[c]) for c in every]
        e_k = [jnp.exp(mid[c] - b[c]) for c in every]
        e_i = [jnp.exp(x) for x in b]
        e_s = [jnp.exp(last[c] - b[c]) for c in every]
        dec = [jnp.exp(x) for x in last]
        qt, kt, qi, ks = ([a[c] * e[c] for c in every] for a, e in ((q, e_q), (k, e_k), (q, e_i), (k, e_s)))

        def masked(a, b_):
            return [[jnp.where(causal, _dot(a_h, b_h, NT), 0.0) for a_h, b_h in zip(_heads(a[c]), _heads(b_[c]))]
                    for c in every]

        def with_scores(s, other, dims):
            return [jnp.concatenate([_dot(s_h, o_h, dims) for s_h, o_h in zip(s[c], _heads(other[c]))], axis=1)
                    for c in every]

        def per_head(dims, a, b_):
            return [_per_head(lambda a_h, b_h: _dot(a_h, b_h, dims), a[c], b_[c]) for c in every]

        scores, dscores = masked(qt, kt), masked(do, v)
        dqt, dkt, dv_intra = with_scores(dscores, kt, NN), with_scores(dscores, qt, TN), with_scores(scores, do, TN)
        dqi, update = per_head(NN, do, st), per_head(TN, do, qi)

        dst = ds_scr[...]
        dsts = [None] * ncb
        for c in reversed(every):
            dsts[c] = dst
            dst = dec[c] * dst + update[c]
        ds_scr[...] = dst

        dv_state, dks = per_head(NT, ks, dsts), per_head(NN, v, dsts)
        ddec = [jnp.sum(dsts[c] * st[c], axis=0, keepdims=True) for c in every]
        dq = [dqt[c] * e_q[c] + dqi[c] * e_i[c] for c in every]
        dk = [dkt[c] * e_k[c] + dks[c] * e_s[c] for c in every]
        db = [q[c] * dq[c] - k[c] * dk[c] for c in every]
        db_last = [jnp.sum(dks[c] * ks[c], axis=0, keepdims=True) + ddec[c] * dec[c] for c in every]
        dg = [_dot_exact(anti, db[c]) + db_last[c] for c in every]
        df = [dg[c] / f[c] - dk[c] for c in every]
        dlb_scr[...] += sum(jnp.sum(df[c] * (1.0 - sig[c]), axis=0, keepdims=True) for c in every)
        dfp = [df[c] * (1.0 - lb) * sig[c] * (1.0 - sig[c]) for c in every]
        dv = [dv_intra[c] + dv_state[c] for c in every]
        for n, parts in enumerate((dq, dfp, dv)):
            dp_ref[n] = jnp.concatenate(parts, axis=0).astype(dp_ref.dtype)

        @pl.when(i == nblk - 1)
        def _():
            dlb = dlb_scr[...]
            dlbl_ref[0:1, :] = dlb * lb * (1.0 - lb)
            dlbl_ref[1:2, :] = -dlb * lb * s1

    vec = pl.BlockSpec((2, HGRN_WIDTH), lambda i: (0, 0))
    return pl.pallas_call(
        body, name="hgrn_bwd", grid=(nblk,),
        in_specs=[ANY, pl.BlockSpec((tb, HGRN_WIDTH), lambda i: (nblk - 1 - i, 0)),
                  pl.BlockSpec((ncb, HEAD_DIM, HGRN_WIDTH), lambda i: (nblk - 1 - i, 0, 0)), vec, ANY],
        out_specs=[pl.BlockSpec((3, tb, HGRN_WIDTH), lambda i: (0, nblk - 1 - i, 0)), vec],
        out_shape=[jax.ShapeDtypeStruct((3, t, HGRN_WIDTH), BF16), jax.ShapeDtypeStruct((2, HGRN_WIDTH), F32)],
        scratch_shapes=[pltpu.VMEM((HEAD_DIM, HGRN_WIDTH), F32), pltpu.VMEM((1, HGRN_WIDTH), F32),
                        pltpu.VMEM((RING, 3, tb, GROUP), F32), pltpu.SemaphoreType.DMA((RING,))],
        compiler_params=_params("arbitrary"),
    )(proj, do, states, lb_logits, after)


GRAD_TILE = 512
OUT_PARTS = 4


class _Side:
    def __init__(self, operands, in_specs, out_shape, out_specs, scratch, init, begin):
        self.operands, self.in_specs, self.out_shape, self.out_specs = operands, in_specs, out_shape, out_specs
        self.scratch, self.init, self.begin = scratch, init, begin


RING = 3


def _grad_w(name, operands, widths, shape, step, after=None, side=None, ringed=None):
    t = operands[0].shape[-2]
    tt = min(t, GRAD_TILE)
    n_in, n_steps = len(operands), t // tt
    in_specs = [ANY if n == ringed else pl.BlockSpec((tt, w), lambda k: (k, 0)) if a.ndim == 2 else
                pl.BlockSpec((a.shape[0], tt, w), lambda k: (0, k, 0)) for n, (a, w) in enumerate(zip(operands, widths))]
    extra = [] if after is None else [after]
    s_in, s_out = (len(side.operands), len(side.out_shape)) if side else (0, 0)
    first_out = n_in + s_in + len(extra)
    side_scratch = side.scratch if side else []
    ring_scratch = [] if ringed is None else [pltpu.VMEM((RING, tt, widths[ringed]), operands[ringed].dtype),
                                              pltpu.SemaphoreType.DMA((RING,))]

    def body(*refs):
        o_ref, side_outs = refs[first_out], refs[first_out + 1:first_out + 1 + s_out]
        acc, narrow, sem = refs[first_out + 1 + s_out:first_out + 4 + s_out]
        first_side = first_out + 4 + s_out
        k = pl.program_id(0)
        tiles = list(refs[:n_in])
        if ringed is not None:
            ring, ring_sems = refs[first_side + len(side_scratch):]

            def fetch(tile, slot):
                return pltpu.make_async_copy(refs[ringed].at[pl.ds(tile * tt, tt), :], ring.at[slot], ring_sems.at[slot])

            @pl.when(k == 0)
            def _():
                for first in range(min(RING - 1, n_steps)):
                    fetch(first, first).start()

            @pl.when(k + RING - 1 < n_steps)
            def _():
                fetch(k + RING - 1, lax.rem(k + RING - 1, RING)).start()

        @pl.when(k == 0)
        def _():
            acc[...] = jnp.zeros_like(acc)
            if side:
                side.init(side_outs)

        if ringed is not None:
            slot = lax.rem(k, RING)
            fetch(k, slot).wait()
            tiles[ringed] = ring.at[slot]

        tick = (side.begin(k, n_steps, refs[n_in:n_in + s_in], side_outs, refs[first_side:first_side + len(side_scratch)])
                if side else None)
        step(acc, *tiles, tick or (lambda j: None))

        @pl.when(k == n_steps - 1)
        def _():
            part = shape[0] // OUT_PARTS
            copies = []
            for p in range(OUT_PARTS):
                rows = pl.ds(p * part, part)
                narrow[rows, :] = acc[rows, :].astype(narrow.dtype)
                copies.append(pltpu.make_async_copy(narrow.at[rows, :], o_ref.at[rows, :], sem.at[p]))
                copies[-1].start()
            for cp in copies:
                cp.wait()

    outs = pl.pallas_call(
        body, name=name, grid=(n_steps,),
        in_specs=in_specs + (side.in_specs if side else []) + [ANY] * len(extra),
        out_specs=[ANY] + (side.out_specs if side else []),
        out_shape=[jax.ShapeDtypeStruct(shape, BF16)] + (side.out_shape if side else []),
        scratch_shapes=[pltpu.VMEM(shape, F32), pltpu.VMEM(shape, BF16), pltpu.SemaphoreType.DMA((OUT_PARTS,))]
                       + side_scratch + ring_scratch,
        compiler_params=_params("arbitrary"),
    )(*operands, *(side.operands if side else ()), *extra)
    return outs if side else outs[0]


def _dw_in(xb, dph, dog, dpc, w_in, dpre1, after):
    t = xb.shape[0]

    def step(acc, x_ref, dh_ref, dog_ref, dc_ref, tick):
        xv = x_ref[...]
        for g in range(N_GROUPS):
            part = dh_ref[g] if g < 3 else dog_ref[...] if g == 3 else dc_ref[g - 4]
            acc[:, g * GROUP:(g + 1) * GROUP] += _dot(xv, part, TN)
            tick(g, part)

    def begin(k, n_steps, ins, outs, scratch):
        w_ref, dp_ref = ins
        total = [ALPHA * dp_ref[...]]

        def tick(g, part):
            total[0] = total[0] + _dot(part, w_ref[:, g * GROUP:(g + 1) * GROUP], NT)
            if g == N_GROUPS - 1:
                outs[0][...] = total[0]

        return tick

    row = pl.BlockSpec((min(t, GRAD_TILE), D_MODEL), lambda k: (k, 0))
    side = _Side((w_in, dpre1), [_resident((D_MODEL, IN_COLS)), row], [jax.ShapeDtypeStruct((t, D_MODEL), F32)], [row],
                 [], lambda outs: None, begin)
    return _grad_w("dw_in", (xb, dph, dog, dpc), (D_MODEL, GROUP, GROUP, GROUP), (D_MODEL, IN_COLS), step, after, side)


def _strips_of(j, tt):
    per_tick = tt // GATE_STRIP // N_FF
    return [slice(s * GATE_STRIP, (s + 1) * GATE_STRIP) for s in range(j * per_tick, (j + 1) * per_tick)]


def _dw_ff1(h1b, da, dcat, o, proj, gate_norm_w, after):
    t = h1b.shape[0]
    tt = min(t, GRAD_TILE)

    def step(acc, h_ref, da_ref, tick):
        hv = h_ref[...]
        for j in range(N_FF):
            cols = slice(j * FF_BLOCK, (j + 1) * FF_BLOCK)
            acc[:, cols] += _dot(hv, da_ref[:, cols], TN)
            tick(j)

    def init(outs):
        outs[2][...] = jnp.zeros_like(outs[2])

    def begin(k, n_steps, ins, outs, scratch):
        do2_ref, o_ref, og_ref, gnw_ref = ins
        do_ref, dog_ref, dgnw_ref = outs
        total = [jnp.zeros((GATE_STRIP, GROUP), F32)]

        def tick(j):
            gnw = gnw_ref[...]
            for rows in _strips_of(j, tt):
                ov, og, do2 = o_ref[rows, :], og_ref[rows, :], do2_ref[rows, :]
                rs = _per_head(lambda o_h: jnp.broadcast_to(
                    lax.rsqrt(jnp.mean(o_h * o_h, axis=-1, keepdims=True) + EPS), o_h.shape), ov)
                on = ov * rs
                sg = _sigmoid(og)
                sil = og * sg
                don = do2 * gnw * sil
                total[0] = total[0] + do2 * on * sil
                dog_ref[rows, :] = (do2 * on * gnw * (sg * (1.0 + og * (1.0 - sg)))).astype(dog_ref.dtype)
                do_ref[rows, :] = rs * (don - on * _per_head(
                    lambda p_h: jnp.broadcast_to(jnp.mean(p_h, axis=-1, keepdims=True), p_h.shape), don * on))
            if j == N_FF - 1:
                dgnw_ref[...] += jnp.sum(total[0], axis=0, keepdims=True)

        return tick

    tile = pl.BlockSpec((tt, GROUP), lambda k: (k, 0))
    vec = pl.BlockSpec((1, GROUP), lambda k: (0, 0))
    side = _Side(
        (dcat, o, proj, gate_norm_w), [tile, tile, pl.BlockSpec((None, tt, GROUP), lambda k: (3, k, 0)), vec],
        [jax.ShapeDtypeStruct((t, HGRN_WIDTH), F32), jax.ShapeDtypeStruct((t, HGRN_WIDTH), BF16),
         jax.ShapeDtypeStruct((1, HGRN_WIDTH), F32)], [tile, tile, vec], [], init, begin)
    return _grad_w("dw_ff1", (h1b, da), (D_MODEL, D_FF), (D_MODEL, D_FF), step, after, side, ringed=1)


def _dw_ff2(r, dpre2b, dcat, bcu, conv_w):
    t = r.shape[0]
    tt = min(t, GRAD_TILE)
    hb = tt // SUBLANES
    halo = 2 * SUBLANES

    def step(acc, r_ref, d_ref, tick):
        dv = d_ref[...]
        for j in range(N_FF):
            rows = slice(j * FF_BLOCK, (j + 1) * FF_BLOCK)
            acc[rows, :] += _dot(r_ref[:, rows], dv, TN)
            tick(j)

    def init(outs):
        outs[1][...] = jnp.zeros_like(outs[1])

    def begin(k, n_steps, ins, outs, scratch):
        dy_ref, dyn_ref, b_ref, bn_ref, c_ref, u_ref, ch_ref, uh_ref, cw_ref = ins
        dp_ref, dcw_ref = outs
        zbuf, dbuf = scratch
        before = lambda ref: ref[SUBLANES:halo, :].astype(F32)
        zbuf[0:SUBLANES, :] = jnp.where(k > 0, before(ch_ref) * before(uh_ref), 0.0)
        zbuf[SUBLANES:SUBLANES + tt, :] = c_ref[...].astype(F32) * u_ref[...].astype(F32)
        dbuf[0:tt, :] = dy_ref[...] * b_ref[...].astype(F32)
        dbuf[tt:tt + SUBLANES, :] = jnp.where(k < n_steps - 1, dyn_ref[...] * bn_ref[0:SUBLANES, :].astype(F32), 0.0)
        totals = [jnp.zeros((GATE_STRIP, GROUP), F32) for _ in range(3)]

        def tick(j):
            cw = cw_ref[...]
            for rows in _strips_of(j, tt):
                at = lambda buf, shift: buf[shift + rows.start:shift + rows.stop, :]
                z, z1, z2 = at(zbuf, SUBLANES), at(zbuf, SUBLANES - 1), at(zbuf, SUBLANES - 2)
                dyc, d1, d2 = at(dbuf, 0), at(dbuf, 1), at(dbuf, 2)
                yc = cw[2:3, :] * z + cw[1:2, :] * z1 + cw[0:1, :] * z2
                dz = cw[2:3, :] * dyc + cw[1:2, :] * d1 + cw[0:1, :] * d2
                dp_ref[0, rows, :] = (dy_ref[rows, :] * yc).astype(dp_ref.dtype)
                dp_ref[1, rows, :] = (dz * u_ref[rows, :].astype(F32)).astype(dp_ref.dtype)
                dp_ref[2, rows, :] = (dz * c_ref[rows, :].astype(F32)).astype(dp_ref.dtype)
                for n, tap in enumerate((z2, z1, z)):
                    totals[n] = totals[n] + dyc * tap
            if j == N_FF - 1:
                for n in range(3):
                    dcw_ref[n:n + 1, :] += jnp.sum(totals[n], axis=0, keepdims=True)

        return tick

    grp = lambda g: pl.BlockSpec((None, tt, GROUP), lambda k: (g, k, 0))
    prev = lambda g: pl.BlockSpec((None, halo, GROUP), lambda k: (g, jnp.maximum(k * (tt // halo) - 1, 0), 0))
    nxt = lambda g: pl.BlockSpec((None, halo, GROUP), lambda k: (g, jnp.minimum((k + 1) * (tt // halo), t // halo - 1), 0))
    nxt_row = lambda k: jnp.minimum((k + 1) * hb, t // SUBLANES - 1)
    whole = pl.BlockSpec((3, CONV_WIDTH), lambda k: (0, 0))
    side = _Side(
        (dcat, dcat, bcu, bcu, bcu, bcu, bcu, bcu, conv_w),
        [pl.BlockSpec((tt, GROUP), lambda k: (k, 1)), pl.BlockSpec((SUBLANES, GROUP), lambda k: (nxt_row(k), 1)),
         grp(0), nxt(0), grp(1), grp(2), prev(1), prev(2), whole],
        [jax.ShapeDtypeStruct((3, t, CONV_WIDTH), BF16), jax.ShapeDtypeStruct((3, CONV_WIDTH), F32)],
        [pl.BlockSpec((3, tt, GROUP), lambda k: (0, k, 0)), whole],
        [pltpu.VMEM((tt + SUBLANES, GROUP), F32), pltpu.VMEM((tt + SUBLANES, GROUP), F32)], init, begin)
    return _grad_w("dw_ff2", (r, dpre2b), (D_FF, D_MODEL), (D_FF, D_MODEL), step, side=side, ringed=0)


def _place():
    x, y, c = lax.axis_index("x"), lax.axis_index("y"), lax.axis_index("c")
    return x, y, c, 2 * x + y


def _other_chips(x, y):
    return [(1 - x, y), (x, 1 - y), (1 - x, 1 - y)]


def _place_shard(name, w, chip, cols_sharded, after=None):
    rows, cols = w.shape
    tr = min(rows, 256)
    nb = rows // tr
    full = (rows, cols * N_CHIPS) if cols_sharded else (rows * N_CHIPS, cols)
    out_map = (lambda i, s: (i, s[0])) if cols_sharded else (lambda i, s: (s[0] * nb + i, 0))

    def body(s_ref, w_ref, *rest):
        rest[-1][...] = w_ref[...].astype(rest[-1].dtype)

    extra = [] if after is None else [after]
    return pl.pallas_call(
        body, name=name,
        grid_spec=pltpu.PrefetchScalarGridSpec(
            num_scalar_prefetch=1, grid=(nb,),
            in_specs=[pl.BlockSpec((tr, cols), lambda i, s: (i, 0))] + [ANY] * len(extra),
            out_specs=pl.BlockSpec((tr, cols), out_map)),
        out_shape=jax.ShapeDtypeStruct(full, BF16),
        compiler_params=_params("parallel"),
    )(chip, w, *extra)


HBM = pl.BlockSpec(memory_space=pltpu.HBM)
SEM = pl.BlockSpec(memory_space=pltpu.SEMAPHORE)
EFFECT = pltpu.SideEffectType.DATAFLOW_SIDE_EFFECTING


PEER_SETS = {
    "sibling": (0, lambda x, y, c: [(x, y, 1 - c)]),
    "chips": (1, lambda x, y, c: [(1 - x, y, c), (x, 1 - y, c), (1 - x, 1 - y, c)]),
    "neighbours": (2, lambda x, y, c: [(1 - x, y, c), (x, 1 - y, c)]),
}


class _Split:
    def __init__(self, name, arrays, plan, others=(), peers=None, prepare=None, sources=(), scratch=()):
        n_own, arrays = len(arrays), (*arrays, *others)
        n, n_copies, n_in = len(arrays), plan.count, len(arrays) + len(sources)
        self.name, self.plan, self.n = name, plan, n_own
        barrier_id, peer_ids = PEER_SETS[peers] if peers else (None, None)

        def body(*refs):
            send_sems, recv_sems, token = refs[n_in], refs[n_in + 1], refs[n_in + 2 + n]
            if peers:
                x, y, c, _ = _place()
                barrier = pltpu.get_barrier_semaphore()
                for peer in peer_ids(x, y, c):
                    pl.semaphore_signal(barrier, inc=1, device_id=peer, device_id_type=MESH)
            if prepare:
                prepare(refs[:n], refs[n:n_in], refs[n_in + 3 + n:])
            if peers:
                pl.semaphore_wait(barrier, len(peer_ids(0, 0, 0)))
            for k, (src, dst, to) in enumerate(plan(refs[:n])):
                pltpu.make_async_remote_copy(src_ref=src, dst_ref=dst, send_sem=send_sems.at[k], recv_sem=recv_sems.at[k],
                                             device_id=to, device_id_type=MESH).start()
            token[...] = jnp.zeros_like(token)

        outs = pl.pallas_call(
            body, name=name + "_start",
            out_shape=(pltpu.SemaphoreType.DMA((n_copies,)), pltpu.SemaphoreType.DMA((n_copies,)),
                       *[pltpu.HBM(a.shape, a.dtype) for a in arrays], jax.ShapeDtypeStruct((SUBLANES, LANES), F32)),
            in_specs=(HBM,) * n_in, out_specs=(SEM, SEM) + (HBM,) * n + (pl.BlockSpec(memory_space=pltpu.VMEM),),
            input_output_aliases={i: 2 + i for i in range(n)}, scratch_shapes=list(scratch),
            compiler_params=pltpu.CompilerParams(has_side_effects=EFFECT, collective_id=barrier_id),
        )(*[pltpu.with_memory_space_constraint(a, pltpu.HBM) for a in (*arrays, *sources)])
        self.sems, self.arrays, self.others, self.token = outs[:2], outs[2:2 + n_own], outs[2 + n_own:2 + n], outs[-1]

    def wait(self, after):
        n, plan = self.n, self.plan

        def body(*refs):
            send_sems, recv_sems = refs[n], refs[n + 1]
            for k, (src, dst, to) in enumerate(plan(refs[:n])):
                cp = pltpu.make_async_remote_copy(src_ref=src, dst_ref=dst, send_sem=send_sems.at[k],
                                                  recv_sem=recv_sems.at[k], device_id=to, device_id_type=MESH)
                cp.wait_send()
                cp.wait_recv()

        return pl.pallas_call(
            body, name=self.name + "_wait", out_shape=tuple(pltpu.HBM(a.shape, a.dtype) for a in self.arrays),
            in_specs=(HBM,) * n + (SEM, SEM, ANY), out_specs=(HBM,) * n, input_output_aliases={i: i for i in range(n)},
            compiler_params=pltpu.CompilerParams(has_side_effects=EFFECT),
        )(*self.arrays, *self.sems, after)


COLS_SHARDED = (True, False, True, False)
HALF_SHAPES = [(D_MODEL // 2, IN_COLS), (D_MODEL, D_MODEL // 2), (D_MODEL // 2, D_FF), (D_FF, D_MODEL // 2)]
PIECE_SHAPES = [(D_MODEL // 2, IN_COLS // N_CHIPS), (D_MODEL // N_CHIPS, D_MODEL // 2),
                (D_MODEL // 2, D_FF // N_CHIPS), (D_FF // N_CHIPS, D_MODEL // 2)]


def _shard_view(kind, ref, chip):
    if COLS_SHARDED[kind]:
        n = ref.shape[1] // N_CHIPS
        return ref.at[:, pl.ds(chip * n, n)]
    n = ref.shape[0] // N_CHIPS
    return ref.at[pl.ds(chip * n, n), :]


def _half_view(kind, ref, h):
    if COLS_SHARDED[kind]:
        n = ref.shape[0] // 2
        return ref.at[pl.ds(h * n, n), :]
    n = ref.shape[1] // 2
    return ref.at[:, pl.ds(h * n, n)]


def _plan(count):
    def mark(fn):
        fn.count = count
        return fn
    return mark


def _shard_rows_view(kind, ref, chip, part, n_parts):
    if COLS_SHARDED[kind]:
        m, n = ref.shape[0] // n_parts, ref.shape[1] // N_CHIPS
        return ref.at[pl.ds(part * m, m), pl.ds(chip * n, n)]
    m = ref.shape[0] // N_CHIPS // n_parts
    return ref.at[pl.ds((n_parts * chip + part) * m, m), :]


def _shard_half_view(kind, ref, chip, h):
    return _shard_rows_view(kind, ref, chip, h, 2)


def _gather_over_ici(kinds, weights):
    @_plan(2 * len(kinds))
    def plan(refs):
        x, y, c, me = _place()
        mine = [_shard_half_view(kind, ref, me, c) for kind, ref in zip(kinds, refs)]
        return [(v, v, to) for v in mine for to in ((1 - x, y, c), (x, 1 - y, c))]

    return _Split("gather_ici_" + "".join(map(str, kinds)), tuple(weights), plan, peers="neighbours")


def _relay_over_ici(kinds, weights, others=()):
    @_plan(2 * len(kinds))
    def plan(refs):
        x, y, c, _ = _place()
        x_nbr, y_nbr = 2 * (1 - x) + y, 2 * x + (1 - y)
        out = []
        for kind, ref in zip(kinds, refs):
            first, second = (_shard_rows_view(kind, ref, chip, 2 * c + q, 4) for q, chip in ((0, x_nbr), (1, y_nbr)))
            out += [(first, first, (x, 1 - y, c)), (second, second, (1 - x, y, c))]
        return out

    return _Split("relay_ici_" + "".join(map(str, kinds)), tuple(weights), plan, others, peers="neighbours")


def _gather_w_in_over_ici(shard, conv4):
    rows, cols = shard.shape

    @_plan(6)
    def plan(refs):
        x, y, c, me = _place()
        half, conv = _shard_half_view(0, refs[0], me, c), refs[1].at[me]
        return [(v, v, (px, py, c)) for v in (half, conv) for px, py in _other_chips(x, y)]

    def prepare(refs, sources, scratch):
        wide, narrow, sems = scratch
        _, _, _, me = _place()
        load = pltpu.make_async_copy(sources[0], wide, sems.at[0])
        load.start()
        load.wait()
        narrow[...] = wide[...].astype(narrow.dtype)
        store = pltpu.make_async_copy(narrow, _shard_view(0, refs[0], me), sems.at[1])
        store.start()
        store.wait()

    return _Split("gather_w_in_ici", (lax.empty((rows, cols * N_CHIPS), BF16), conv4), plan, peers="chips",
                  prepare=prepare, sources=(shard,),
                  scratch=(pltpu.VMEM((rows, cols), F32), pltpu.VMEM((rows, cols), BF16), pltpu.SemaphoreType.DMA((2,))))


def _gather_over_d2d(kinds, weights):
    @_plan(3 * len(kinds))
    def plan(refs):
        x, y, c, _ = _place()
        got = [_shard_half_view(kind, ref, 2 * px + py, c) for kind, ref in zip(kinds, refs)
               for px, py in _other_chips(x, y)]
        return [(v, v, (x, y, 1 - c)) for v in got]

    return _Split("gather_d2d_" + "".join(map(str, kinds)), tuple(weights), plan, peers="sibling")


def _swap_halves(kinds, grads):
    @_plan(len(kinds))
    def plan(refs):
        x, y, c, _ = _place()
        return [(_half_view(kind, g, 1 - c), land, (x, y, 1 - c))
                for kind, g, land in zip(kinds, refs[:len(kinds)], refs[len(kinds):])]

    lands = [lax.empty(HALF_SHAPES[kind], g.dtype) for kind, g in zip(kinds, grads)]
    return _Split("swap_halves_" + "".join(map(str, kinds)), (*grads, *lands), plan, peers="sibling")


def _block_rows(cols, elements):
    return 1 << ((elements // cols).bit_length() - 1)


def _grid_steps(shapes, elements):
    rows, cols = max(shapes, key=lambda shape: shape[0] * shape[1])
    return rows // min(rows, _block_rows(cols, elements))


def _add_half(name, kinds, grads, recvs, core):
    n = len(kinds)
    shapes = [recv.shape for recv in recvs]
    nb = _grid_steps(shapes, 1 << 20)

    def body(c_ref, *refs):
        for g_ref, r_ref, o_ref in zip(refs[:n], refs[n:2 * n], refs[2 * n:]):
            o_ref[...] = (g_ref[...].astype(F32) + r_ref[...].astype(F32)).astype(o_ref.dtype)

    own = [pl.BlockSpec((rows // nb, cols), (lambda i, c_ref: (c_ref[0] * nb + i, 0)) if COLS_SHARDED[k] else
                        (lambda i, c_ref: (i, c_ref[0]))) for k, (rows, cols) in zip(kinds, shapes)]
    blocks = [pl.BlockSpec((rows // nb, cols), lambda i, c_ref: (i, 0)) for rows, cols in shapes]
    return pl.pallas_call(
        body, name=name,
        grid_spec=pltpu.PrefetchScalarGridSpec(
            num_scalar_prefetch=1, grid=(nb,), in_specs=own + blocks, out_specs=blocks),
        out_shape=[jax.ShapeDtypeStruct(shape, BF16) for shape in shapes],
        compiler_params=_params("parallel"),
    )(core, *grads, *recvs)


def _exchange_pieces(kinds, halves, pack=None):
    n_p, n = N_CHIPS - 1, len(kinds)

    @_plan(n_p * n + (0 if pack is None else N_DEV - 1))
    def plan(refs):
        x, y, c, _ = _place()
        copies = []
        if pack is not None:
            me = 4 * x + 2 * y + c
            peers = [((1 - x) if m & 4 else x, (1 - y) if m & 2 else y, (1 - c) if m & 1 else c) for m in range(1, N_DEV)]
            copies += [(refs[2 * n], refs[2 * n + 1].at[me], peer) for peer in peers]
        return copies + [(_shard_view(kind, half, 2 * px + py), land.at[j], (px, py, c))
                         for j, (px, py) in enumerate(_other_chips(x, y))
                         for kind, half, land in zip(kinds, refs[:n], refs[n:2 * n])]

    lands = [lax.empty((n_p,) + PIECE_SHAPES[kind], BF16) for kind in kinds]
    small = () if pack is None else (pack, lax.empty((N_DEV,) + pack.shape, F32))
    return _Split("exchange_pieces_" + "".join(map(str, kinds)), (*halves, *lands, *small), plan,
                  peers="chips" if pack is None else None)


def _sum_pieces(name, kinds, halves, slots, place, after):
    n, n_p = len(kinds), N_CHIPS - 1
    shapes = [slot.shape[1:] for slot in slots]
    nb = _grid_steps(shapes, 1 << 18)

    def body(s_ref, *refs):
        for own_ref, slot_ref, o_ref in zip(refs[:n], refs[n:2 * n], refs[2 * n + 1:]):
            total = own_ref[...].astype(F32)
            for j in range(n_p):
                total = total + slot_ref[j].astype(F32)
            o_ref[...] = total

    own, out, shards = [], [], []
    for k, (rows, cols) in zip(kinds, shapes):
        if COLS_SHARDED[k]:
            own_map, out_map, shard = (lambda i, s: (i, s[0])), (lambda i, s: (s[1] * nb + i, 0)), (2 * rows, cols)
        else:
            own_map, out_map, shard = (lambda i, s: (s[0] * nb + i, 0)), (lambda i, s: (i, s[1])), (rows, 2 * cols)
        own.append(pl.BlockSpec((rows // nb, cols), own_map))
        out.append(pl.BlockSpec((rows // nb, cols), out_map))
        shards.append(jax.ShapeDtypeStruct(shard, F32))
    landed = [pl.BlockSpec((n_p, rows // nb, cols), lambda i, s: (0, i, 0)) for rows, cols in shapes]
    return pl.pallas_call(
        body, name=name,
        grid_spec=pltpu.PrefetchScalarGridSpec(
            num_scalar_prefetch=1, grid=(nb,), in_specs=own + landed + [ANY], out_specs=out),
        out_shape=shards,
        compiler_params=_params("parallel"),
    )(place, *halves, *slots, after)


def _join_halves(kinds, shards):
    @_plan(len(kinds))
    def plan(refs):
        x, y, c, _ = _place()
        return [(_half_view(kind, g, c), _half_view(kind, g, c), (x, y, 1 - c)) for kind, g in zip(kinds, refs)]

    return _Split("join_halves_" + "".join(map(str, kinds)), tuple(shards), plan, peers="sibling")


N_DEV = 8


def _sum_shared(pack, land, device):
    def body(d_ref, p_ref, l_ref, o_ref):
        me = d_ref[0]
        total = jnp.where(me == 0, p_ref[...], l_ref[0])
        for d in range(1, N_DEV):
            total = total + jnp.where(me == d, p_ref[...], l_ref[d])
        o_ref[...] = total

    return pl.pallas_call(
        body, name="sum_shared",
        grid_spec=pltpu.PrefetchScalarGridSpec(
            num_scalar_prefetch=1, grid=(1,),
            in_specs=[pl.BlockSpec(pack.shape, lambda i, d: (0, 0)), pl.BlockSpec(land.shape, lambda i, d: (0, 0, 0))],
            out_specs=pl.BlockSpec(pack.shape, lambda i, d: (0, 0))),
        out_shape=jax.ShapeDtypeStruct(pack.shape, F32),
    )(device, pack, land)


def _adamw(name, weights, after=None):
    shapes = [w.shape for w, _, _, _ in weights]
    nb = _grid_steps(shapes, 1 << 18)
    extra = [] if after is None else [after]
    n_in = 4 * len(weights) + len(extra)

    def body(*refs):
        for k in range(len(weights)):
            w_ref, g_ref, m_ref, v_ref = refs[4 * k:4 * k + 4]
            go_ref, d_ref, nm_ref, nv_ref = refs[n_in + 4 * k:n_in + 4 * k + 4]
            g = g_ref[...]
            go_ref[...] = g
            d_ref[...], nm_ref[...], nv_ref[...] = _adam_step(w_ref[...], g, m_ref[...], v_ref[...])

    blocks = [pl.BlockSpec((rows // nb, cols), lambda i: (i, 0)) for rows, cols in shapes for _ in range(4)]
    outs = pl.pallas_call(
        body, name=name, grid=(nb,), in_specs=blocks + [ANY] * len(extra), out_specs=blocks,
        out_shape=[jax.ShapeDtypeStruct(shape, F32) for shape in shapes for _ in range(4)],
        compiler_params=_params("parallel"),
    )(*[a for group in weights for a in group], *extra)
    return [outs[4 * k:4 * k + 4] for k in range(len(weights))]


def _adam_step(w, g, m, v):
    nm = ADAM_B1 * m + (1.0 - ADAM_B1) * g
    nv = ADAM_B2 * v + (1.0 - ADAM_B2) * jnp.square(g)
    m_hat = nm * (1.0 / (1.0 - ADAM_B1 ** ADAM_STEP))
    v_hat = nv * (1.0 / (1.0 - ADAM_B2 ** ADAM_STEP))
    return -ADAM_LR * (m_hat / (jnp.sqrt(v_hat) + ADAM_EPS) + ADAM_WD * w), nm, nv


def _adamw_small(tot, chip, weights, ms, vs, after):
    n, half = len(weights), D_MODEL // 2

    def body(chip_ref, tot_ref, *refs):
        ins, outs = refs[:3 * n], refs[3 * n + 1:]
        tot = tot_ref[...]
        conv_all = jnp.concatenate([tot[5:6, half:], tot[6:7, :half], tot[6:7, half:]], axis=0)
        conv = sum(jnp.where(chip_ref[0] == s, conv_all[:, s * LANES:(s + 1) * LANES], 0.0) for s in range(N_CHIPS))
        grads = [jnp.concatenate([tot[4:5, :half], tot[4:5, half:]], axis=0), tot[5:6, :half], conv,
                 tot[0:1], tot[1:2], tot[2:3], tot[3:4]]
        for k, g in enumerate(grads):
            delta, nm, nv = _adam_step(ins[k][...], g, ins[n + k][...], ins[2 * n + k][...])
            outs[k][...], outs[n + k][...], outs[2 * n + k][...], outs[3 * n + k][...] = g, delta, nm, nv
        outs[4 * n][...] = tot[7:8, 0:1]

    whole = lambda a: pl.BlockSpec(a.shape, lambda i, s: (0,) * a.ndim)
    arrays = (*weights, *ms, *vs)
    loss = jax.ShapeDtypeStruct((1, 1), F32)
    return pl.pallas_call(
        body, name="adamw_small",
        grid_spec=pltpu.PrefetchScalarGridSpec(
            num_scalar_prefetch=1, grid=(1,), in_specs=[whole(tot)] + [whole(a) for a in arrays] + [ANY],
            out_specs=[whole(a) for a in weights] * 4 + [whole(loss)]),
        out_shape=[jax.ShapeDtypeStruct(a.shape, F32) for a in weights] * 4 + [loss],
    )(chip, tot, *arrays, after)


def kernel(x, w_in, lb_logits, gate_norm_w, conv_w, w_out, ln1_g, ln1_b, w_ff1, w_ff2, ln2_g, ln2_b, loss_target, m_w_in, m_lb_logits, m_gate_norm_w, m_conv_w, m_w_out, m_ln1_g, m_ln1_b, m_w_ff1, m_w_ff2, m_ln2_g, m_ln2_b, v_w_in, v_lb_logits, v_gate_norm_w, v_conv_w, v_w_out, v_ln1_g, v_ln1_b, v_w_ff1, v_w_ff2, v_ln2_g, v_ln2_b):
    xs, tgt = x[0], loss_target[0]
    chip = 2 * lax.axis_index("x") + lax.axis_index("y")
    core = lax.axis_index("c").astype(jnp.int32).reshape(1)
    chip1 = chip.astype(jnp.int32).reshape(1)
    place = jnp.concatenate([chip1, core])

    conv4 = lax.dynamic_update_slice(jnp.zeros((N_CHIPS,) + conv_w.shape[1:], F32), conv_w, (chip, 0, 0))
    ici_in = _gather_w_in_over_ici(w_in[0], conv4)
    rest = (1, 2, 3)
    ici_rest = _gather_over_ici(rest, (_place_shard("place_w_out", w_out[0], chip1, False, after=ici_in.token),
                                       _place_shard("place_w_ff1", w_ff1[0], chip1, True, after=ici_in.token),
                                       _place_shard("place_w_ff2", w_ff2[0], chip1, False, after=ici_in.token)))
    wb_in, cv4 = ici_in.wait(ici_rest.token)
    d2d_in = _gather_over_d2d((0,), (wb_in,))
    wb_in, = d2d_in.wait(d2d_in.token)
    conv_full = cv4.transpose(1, 0, 2).reshape(3, CONV_WIDTH)

    proj, bcu, xb, cat_c = _in_proj(xs, wb_in, conv_full, ici_rest.token)
    relay_rest = _relay_over_ici(rest, ici_rest.wait(proj))
    o, states = _hgrn_fwd(proj, lb_logits, relay_rest.token)
    d2d_rest = _gather_over_d2d(rest, relay_rest.wait(o))
    cat_h = _gate_fwd(proj, o, gate_norm_w, d2d_rest.token)
    wb_out, wb_ff1, wb_ff2 = d2d_rest.wait(cat_h)

    (h1b, r, da, dpre2b, dpre1, dcat, g_ln1_g, g_ln1_b, g_ln2_g, g_ln2_b, loss8, g_out_local) = _sublayers(
        cat_h, cat_c, xs, tgt, wb_out, wb_ff1, wb_ff2, ln1_g, ln1_b, ln2_g, ln2_b)

    names = ("w_in", "w_out", "w_ff1", "w_ff2")

    def named(prefix, kinds):
        return prefix + "".join("_" + names[k] for k in kinds)

    def add_halves(kinds, grads, lands):
        return _add_half(named("add_half", kinds), kinds, grads, lands, core)

    def sum_pieces(kinds, halves, lands, after):
        return _sum_pieces(named("sum_pieces", kinds), kinds, halves, lands, place, after)

    early = (1, 2, 3)
    g_ff2_local, dpc, g_conv = _dw_ff2(r, dpre2b, dcat, bcu, conv_full)
    swap_a = _swap_halves((1, 3), (g_out_local, g_ff2_local))
    g_ff1_local, do, dog, g_gnw = _dw_ff1(h1b, da, dcat, o, proj, gate_norm_w, swap_a.token)
    swap_b = _swap_halves((2,), (g_ff1_local,))
    swapped_a = swap_a.wait(swap_b.token)
    halves_a = add_halves((1, 3), swapped_a[:2], swapped_a[2:])
    swapped_b = swap_b.wait(halves_a[1])
    halves = (halves_a[0], *add_halves((2,), swapped_b[:1], swapped_b[1:]), halves_a[1])
    exch = _exchange_pieces(early, halves)
    dph, g_lbl = _hgrn_bwd(proj, do, states, lb_logits, exch.token)
    g_in_local, grad_x = _dw_in(xb, dph, dog, dpc, wb_in, dpre1, dph)

    late = (0,)
    swap = _swap_halves(late, (g_in_local,))
    exchanged = exch.wait(swap.token)
    pack = jnp.concatenate([
        g_ln1_g, g_ln1_b, g_ln2_g, g_ln2_b,
        jnp.concatenate([g_lbl[0:1], g_lbl[1:2]], axis=1),
        jnp.concatenate([g_gnw, g_conv[0:1]], axis=1),
        jnp.concatenate([g_conv[1:2], g_conv[2:3]], axis=1),
        jnp.concatenate([loss8[0:1], jnp.zeros((1, D_MODEL - LANES), F32)], axis=1)], axis=0)
    join_a = _join_halves((2,), sum_pieces((2,), exchanged[1:2], exchanged[4:5], swap.token))
    swapped = swap.wait(join_a.token)
    exch = _exchange_pieces(late, add_halves(late, swapped[:1], swapped[1:]), pack)
    join_b = _join_halves((1, 3), sum_pieces((1, 3), exchanged[0:3:2], exchanged[3:6:2], exch.token))
    g_w_ff1, = join_a.wait(join_b.token)
    (g_w_ff1, d_ff1, nm_ff1, nv_ff1), = _adamw("adamw_w_ff1", [(w_ff1[0], g_w_ff1, m_w_ff1[0], v_w_ff1[0])])
    g_w_out, g_w_ff2 = join_b.wait(d_ff1)
    (g_w_ff2, d_ff2, nm_ff2, nv_ff2), (g_w_out, d_out, nm_out, nv_out) = _adamw(
        "adamw_w_ff2_w_out", [(w_ff2[0], g_w_ff2, m_w_ff2[0], v_w_ff2[0]), (w_out[0], g_w_out, m_w_out[0], v_w_out[0])])
    exchanged = exch.wait(d_out)
    tot = _sum_shared(exchanged[2], exchanged[3], 2 * chip1 + core)
    join = _join_halves(late, sum_pieces(late, exchanged[:1], exchanged[1:2], tot))
    small = ("lb_logits", "gate_norm_w", "conv_w", "ln1_g", "ln1_b", "ln2_g", "ln2_b")
    small_out = _adamw_small(
        tot, chip1, (lb_logits, gate_norm_w, conv_w[0], ln1_g, ln1_b, ln2_g, ln2_b),
        (m_lb_logits, m_gate_norm_w, m_conv_w[0], m_ln1_g, m_ln1_b, m_ln2_g, m_ln2_b),
        (v_lb_logits, v_gate_norm_w, v_conv_w[0], v_ln1_g, v_ln1_b, v_ln2_g, v_ln2_b), join.token)
    g_w_in, = join.wait(small_out[0])
    (g_w_in, d_in, nm_in, nv_in), = _adamw("adamw_w_in", [(w_in[0], g_w_in, m_w_in[0], v_w_in[0])])
    loss = small_out[4 * len(small)][0, 0]

    def results(n_kind, large):
        out = dict(zip(small, small_out[n_kind * len(small):(n_kind + 1) * len(small)]))
        out["conv_w"] = out["conv_w"][None]
        out.update({name: a[None] for name, a in zip(("w_in", "w_out", "w_ff1", "w_ff2"), large)})
        return [out[name] for name in ("w_in", "lb_logits", "gate_norm_w", "conv_w", "w_out", "ln1_g", "ln1_b",
                                       "w_ff1", "w_ff2", "ln2_g", "ln2_b")]

    return (loss, grad_x[None], *results(0, (g_w_in, g_w_out, g_w_ff1, g_w_ff2)),
            *results(1, (d_in, d_out, d_ff1, d_ff2)), *results(2, (nm_in, nm_out, nm_ff1, nm_ff2)),
            *results(3, (nv_in, nv_out, nv_ff1, nv_ff2)))
```

```python
import jax
import jax.numpy as jnp
from jax import lax
from jax.experimental import pallas as pl
from jax.experimental.pallas import tpu as pltpu

F32 = jnp.float32
BF16 = jnp.bfloat16
MXU_DTYPE = jnp.bfloat16

D_MODEL = 1024
HGRN_WIDTH = 512
HEAD_DIM = 128
N_HEADS = 4
CONV_WIDTH = 512
CHUNK = 64
D_FF = 4096
IN_COLS = 3584
GROUP = 512
N_GROUPS = IN_COLS // GROUP
ALPHA = 2.0 ** 0.25
EPS = 1e-5
N_CHIPS = 4
ADAM_LR, ADAM_B1, ADAM_B2, ADAM_EPS, ADAM_WD, ADAM_STEP = 0.001, 0.9, 0.999, 1e-08, 0.01, 10

LANES = 128
SUBLANES = 8
VMEM_LIMIT = 56 * 1024 * 1024
FF_BLOCK = 1024
N_FF = D_FF // FF_BLOCK
GATE_STRIP = 64

NN = (((1,), (0,)), ((), ()))
NT = (((1,), (1,)), ((), ()))
TN = (((0,), (0,)), ((), ()))
MESH = pl.DeviceIdType.MESH
ANY = pl.BlockSpec(memory_space=pl.ANY)


def _dot(a, b, dims):
    return lax.dot_general(a.astype(MXU_DTYPE), b.astype(MXU_DTYPE), dims, preferred_element_type=F32)


def _dot_exact(ones, v):
    ones = ones.astype(jnp.bfloat16)
    hi = v.astype(jnp.bfloat16)
    rest = v - hi.astype(F32)
    mid = rest.astype(jnp.bfloat16)
    low = (rest - mid.astype(F32)).astype(jnp.bfloat16)
    return sum(lax.dot_general(ones, part, NN, preferred_element_type=F32) for part in (hi, mid, low))


def _params(*sem):
    return pltpu.CompilerParams(dimension_semantics=sem, vmem_limit_bytes=VMEM_LIMIT)


def _resident(shape):
    return pl.BlockSpec(shape, lambda *_: (0,) * len(shape), pipeline_mode=pl.Buffered(1))


def _sigmoid(v):
    return 1.0 / (1.0 + jnp.exp(-v))


def _lower_bound(lbl):
    m = jnp.max(lbl, axis=0, keepdims=True)
    e = jnp.exp(lbl - m)
    s = e / jnp.sum(e, axis=0, keepdims=True)
    return s[0:1, :], s[1:2, :]


def _heads(v):
    return [v[:, h * HEAD_DIM:(h + 1) * HEAD_DIM] for h in range(N_HEADS)]


def _per_head(fn, *arrays):
    return jnp.concatenate([fn(*parts) for parts in zip(*map(_heads, arrays))], axis=1)


def _in_proj(x, w_in, conv_w, after):
    t = x.shape[0]
    tm = min(t, 512)

    def body(x_ref, w_ref, cw_ref, after_ref, o_ref, bcu_ref, xb_ref, y_ref, zbuf):
        @pl.when(pl.program_id(0) == 0)
        def _():
            zbuf[tm:tm + SUBLANES, :] = jnp.zeros((SUBLANES, CONV_WIDTH), F32)

        xb = x_ref[...].astype(xb_ref.dtype)
        xb_ref[...] = xb
        group = lambda g: _dot(xb, w_ref[:, g * GROUP:(g + 1) * GROUP], NN)
        for g in range(4):
            o_ref[g] = group(g)
        b_gate, c_gate, u = group(4), group(5), group(6)
        for n, part in enumerate((b_gate, c_gate, u)):
            bcu_ref[n] = part.astype(bcu_ref.dtype)
        zbuf[0:SUBLANES, :] = zbuf[tm:tm + SUBLANES, :]
        zbuf[SUBLANES:SUBLANES + tm, :] = c_gate * u
        cw = cw_ref[...]
        at = lambda shift: zbuf[shift:shift + tm, :]
        conv = cw[2:3, :] * at(SUBLANES) + cw[1:2, :] * at(SUBLANES - 1) + cw[0:1, :] * at(SUBLANES - 2)
        y_ref[...] = (b_gate * conv).astype(y_ref.dtype)

    return pl.pallas_call(
        body, name="in_proj", grid=(t // tm,),
        in_specs=[pl.BlockSpec((tm, D_MODEL), lambda i: (i, 0)), _resident((D_MODEL, IN_COLS)),
                  pl.BlockSpec((3, CONV_WIDTH), lambda i: (0, 0)), ANY],
        out_specs=[pl.BlockSpec((4, tm, GROUP), lambda i: (0, i, 0)), pl.BlockSpec((3, tm, GROUP), lambda i: (0, i, 0)),
                   pl.BlockSpec((tm, D_MODEL), lambda i: (i, 0)), pl.BlockSpec((tm, CONV_WIDTH), lambda i: (i, 0))],
        out_shape=[jax.ShapeDtypeStruct((4, t, GROUP), F32), jax.ShapeDtypeStruct((3, t, GROUP), BF16),
                   jax.ShapeDtypeStruct((t, D_MODEL), BF16), jax.ShapeDtypeStruct((t, CONV_WIDTH), BF16)],
        scratch_shapes=[pltpu.VMEM((tm + SUBLANES, CONV_WIDTH), F32)],
        compiler_params=_params("arbitrary"),
    )(x, w_in, conv_w, after)


def _gates(fp, lb):
    sig = _sigmoid(fp)
    f = lb + (1.0 - lb) * sig
    return sig, f, jnp.log(f), 1.0 - f


def _chunk_masks():
    row = lax.broadcasted_iota(jnp.int32, (CHUNK, CHUNK), 0)
    col = lax.broadcasted_iota(jnp.int32, (CHUNK, CHUNK), 1)
    return row >= col, row <= col


def _hgrn_fwd(proj, lb_logits, after):
    t = proj.shape[1]
    tb = min(t, 512)
    ncb = tb // CHUNK

    def body(q_ref, f_ref, v_ref, lbl_ref, after_ref, o_ref, st_ref, s_scr):
        @pl.when(pl.program_id(0) == 0)
        def _():
            s_scr[...] = jnp.zeros_like(s_scr)

        lb, _ = _lower_bound(lbl_ref[...])
        causal, _ = _chunk_masks()

        every = range(ncb)
        rows = [slice(c * CHUNK, (c + 1) * CHUNK) for c in every]
        q, v = [q_ref[r, :] for r in rows], [v_ref[r, :] for r in rows]
        gates = [_gates(f_ref[r, :], lb) for r in rows]
        k = [gt[3] for gt in gates]
        b = [_dot_exact(causal, gt[2]) for gt in gates]
        mid, last = [x[CHUNK // 2:CHUNK // 2 + 1, :] for x in b], [x[CHUNK - 1:CHUNK, :] for x in b]
        qt = [q[c] * jnp.exp(b[c] - mid[c]) for c in every]
        kt = [k[c] * jnp.exp(mid[c] - b[c]) for c in every]
        qi = [q[c] * jnp.exp(b[c]) for c in every]
        ks = [k[c] * jnp.exp(last[c] - b[c]) for c in every]
        dec = [jnp.exp(x) for x in last]
        scores = [[jnp.where(causal, _dot(a, b_, NT), 0.0) for a, b_ in zip(_heads(qt[c]), _heads(kt[c]))] for c in every]
        intra = [[_dot(s, v_h, NN) for s, v_h in zip(scores[c], _heads(v[c]))] for c in every]
        update = [_per_head(lambda v_h, ks_h: _dot(v_h, ks_h, TN), v[c], ks[c]) for c in every]

        st = s_scr[...]
        states = []
        for c in every:
            states.append(st)
            st_ref[c] = st
            st = dec[c] * st + update[c]
        s_scr[...] = st

        o_ref[...] = jnp.concatenate(
            [jnp.concatenate([i_h + _dot(qi_h, st_h, NT) for i_h, qi_h, st_h in
                              zip(intra[c], _heads(qi[c]), _heads(states[c]))], axis=1) for c in every], axis=0)

    grp = lambda g: pl.BlockSpec((None, tb, GROUP), lambda i: (g, i, 0))
    return pl.pallas_call(
        body, name="hgrn_fwd", grid=(t // tb,),
        in_specs=[grp(0), grp(1), grp(2), pl.BlockSpec((2, HGRN_WIDTH), lambda i: (0, 0)), ANY],
        out_specs=[pl.BlockSpec((tb, HGRN_WIDTH), lambda i: (i, 0)),
                   pl.BlockSpec((ncb, HEAD_DIM, HGRN_WIDTH), lambda i: (i, 0, 0))],
        out_shape=[jax.ShapeDtypeStruct((t, HGRN_WIDTH), F32),
                   jax.ShapeDtypeStruct((t // CHUNK, HEAD_DIM, HGRN_WIDTH), F32)],
        scratch_shapes=[pltpu.VMEM((HEAD_DIM, HGRN_WIDTH), F32)],
        compiler_params=_params("arbitrary"),
    )(proj, proj, proj, lb_logits, after)


def _gate_fwd(proj, o, gate_norm_w, after):
    t = proj.shape[1]
    tb = min(t, 1024)

    def body(o_ref, og_ref, gnw_ref, after_ref, out_ref):
        gnw = gnw_ref[...]
        for s in range(tb // GATE_STRIP):
            rows = slice(s * GATE_STRIP, (s + 1) * GATE_STRIP)
            og = og_ref[rows, :]
            on = _per_head(lambda o_h: o_h * lax.rsqrt(jnp.mean(o_h * o_h, axis=-1, keepdims=True) + EPS), o_ref[rows, :])
            out_ref[rows, :] = (on * gnw * (og * _sigmoid(og))).astype(out_ref.dtype)

    tile = pl.BlockSpec((tb, GROUP), lambda i: (i, 0))
    return pl.pallas_call(
        body, name="gate_fwd", grid=(t // tb,),
        in_specs=[tile, pl.BlockSpec((None, tb, GROUP), lambda i: (3, i, 0)), pl.BlockSpec((1, GROUP), lambda i: (0, 0)), ANY],
        out_specs=tile,
        out_shape=jax.ShapeDtypeStruct((t, HGRN_WIDTH), BF16),
        compiler_params=_params("parallel"),
    )(o, proj, gate_norm_w, after)


def _ln_bwd(dy, xhat, rstd, g):
    dxhat = dy * g
    m1 = jnp.mean(dxhat, axis=-1, keepdims=True)
    m2 = jnp.mean(dxhat * xhat, axis=-1, keepdims=True)
    return rstd * (dxhat - m1 - xhat * m2)


def _layer_norm(pre):
    xc = pre - jnp.mean(pre, axis=-1, keepdims=True)
    rstd = lax.rsqrt(jnp.mean(xc * xc, axis=-1, keepdims=True) + EPS)
    return xc * rstd, rstd


def _sublayers(cat_h, cat_c, x, target, w_out, w_ff1, w_ff2, g1, b1, g2, b2):
    t = x.shape[0]
    tm = min(t, 256)

    def body(ch_ref, cc_ref, x_ref, tg_ref, wo_ref, w1_ref, w2_ref, g1_ref, b1_ref, g2_ref, b2_ref,
             h1_ref, r_ref, da_ref, dp2b_ref, dp1_ref, dcat_ref, dg1_ref, db1_ref, dg2_ref, db2_ref, loss_ref, gwo_ref,
             gwo_acc, gwo_narrow, sem):
        @pl.when(pl.program_id(0) == 0)
        def _():
            for ref in (dg1_ref, db1_ref, dg2_ref, db2_ref, loss_ref, gwo_acc):
                ref[...] = jnp.zeros_like(ref)

        mix = _dot(ch_ref[...], wo_ref[0:GROUP, :], NN) + _dot(cc_ref[...], wo_ref[GROUP:2 * GROUP, :], NN)
        xhat1, rstd1 = _layer_norm(ALPHA * x_ref[...] + mix)
        h1 = xhat1 * g1_ref[...] + b1_ref[...]
        h1b = h1.astype(h1_ref.dtype)
        h1_ref[...] = h1b
        mlp = jnp.zeros((tm, D_MODEL), F32)
        for j in range(N_FF):
            cols = slice(j * FF_BLOCK, (j + 1) * FF_BLOCK)
            r = jnp.square(jnp.maximum(_dot(h1b, w1_ref[:, cols], NN), 0.0)).astype(r_ref.dtype)
            r_ref[:, cols] = r
            mlp = mlp + _dot(r, w2_ref[cols, :], NN)
        xhat2, rstd2 = _layer_norm(ALPHA * h1 + mlp)
        err = xhat2 * g2_ref[...] + b2_ref[...] - tg_ref[...]
        loss_ref[...] += 0.5 * jnp.sum(jnp.mean(err * err, axis=-1, keepdims=True))
        dy = err * (1.0 / D_MODEL)
        dg2_ref[...] += jnp.sum(dy * xhat2, axis=0, keepdims=True)
        db2_ref[...] += jnp.sum(dy, axis=0, keepdims=True)
        dp2 = _ln_bwd(dy, xhat2, rstd2, g2_ref[...])
        dp2b = dp2.astype(dp2b_ref.dtype)
        dp2b_ref[...] = dp2b
        back = jnp.zeros((tm, D_MODEL), F32)
        for j in range(N_FF):
            cols = slice(j * FF_BLOCK, (j + 1) * FF_BLOCK)
            dr = _dot(dp2b, w2_ref[cols, :], NT)
            da = (dr * (2.0 * jnp.sqrt(r_ref[:, cols].astype(F32)))).astype(da_ref.dtype)
            da_ref[:, cols] = da
            back = back + _dot(da, w1_ref[:, cols], NT)
        dh1 = ALPHA * dp2 + back
        dg1_ref[...] += jnp.sum(dh1 * xhat1, axis=0, keepdims=True)
        db1_ref[...] += jnp.sum(dh1, axis=0, keepdims=True)
        dp1 = _ln_bwd(dh1, xhat1, rstd1, g1_ref[...])
        dp1b = dp1.astype(MXU_DTYPE)
        dp1_ref[...] = dp1
        dcat_ref[...] = _dot(dp1b, wo_ref[...], NT)
        gwo_acc[0:GROUP, :] += _dot(ch_ref[...], dp1b, TN)
        gwo_acc[GROUP:2 * GROUP, :] += _dot(cc_ref[...], dp1b, TN)

        @pl.when(pl.program_id(0) == pl.num_programs(0) - 1)
        def _():
            gwo_narrow[...] = gwo_acc[...].astype(gwo_narrow.dtype)
            copy = pltpu.make_async_copy(gwo_narrow, gwo_ref, sem.at[0])
            copy.start()
            copy.wait()

    row = pl.BlockSpec((tm, D_MODEL), lambda i: (i, 0))
    wide = pl.BlockSpec((tm, D_FF), lambda i: (i, 0))
    vec = pl.BlockSpec((1, D_MODEL), lambda i: (0, 0))
    narrow = lambda dtype: jax.ShapeDtypeStruct((t, D_MODEL), dtype)
    return pl.pallas_call(
        body, name="sublayers", grid=(t // tm,),
        in_specs=[pl.BlockSpec((tm, GROUP), lambda i: (i, 0)), pl.BlockSpec((tm, GROUP), lambda i: (i, 0)), row, row,
                  _resident((D_MODEL, D_MODEL)),
                  _resident((D_MODEL, D_FF)), _resident((D_FF, D_MODEL)), vec, vec, vec, vec],
        out_specs=[row, wide, wide, row, row, row, vec, vec, vec, vec,
                   pl.BlockSpec((SUBLANES, LANES), lambda i: (0, 0)), ANY],
        out_shape=[narrow(BF16), jax.ShapeDtypeStruct((t, D_FF), BF16), jax.ShapeDtypeStruct((t, D_FF), BF16),
                   narrow(BF16), narrow(F32), narrow(F32)]
                  + [jax.ShapeDtypeStruct((1, D_MODEL), F32)] * 4
                  + [jax.ShapeDtypeStruct((SUBLANES, LANES), F32), jax.ShapeDtypeStruct((D_MODEL, D_MODEL), BF16)],
        scratch_shapes=[pltpu.VMEM((D_MODEL, D_MODEL), F32), pltpu.VMEM((D_MODEL, D_MODEL), BF16),
                        pltpu.SemaphoreType.DMA((1,))],
        compiler_params=_params("arbitrary"),
    )(cat_h, cat_c, x, target, w_out, w_ff1, w_ff2, g1, b1, g2, b2)


def _hgrn_bwd(proj, do, states, lb_logits, after):
    t = proj.shape[1]
    tb = min(t, 512)
    ncb = tb // CHUNK
    nblk = t // tb

    def body(q_ref, f_ref, v_ref, do_ref, st_ref, lbl_ref, after_ref, dp_ref, dlbl_ref, ds_scr, dlb_scr):
        i = pl.program_id(0)

        @pl.when(i == 0)
        def _():
            ds_scr[...] = jnp.zeros_like(ds_scr)
            dlb_scr[...] = jnp.zeros_like(dlb_scr)

        lb, s1 = _lower_bound(lbl_ref[...])
        causal, anti = _chunk_masks()
        every = range(ncb)
        rows = [slice(c * CHUNK, (c + 1) * CHUNK) for c in every]
        q, v, do = ([ref[r, :] for r in rows] for ref in (q_ref, v_ref, do_ref))
        st = [st_ref[c] for c in every]
        gates = [_gates(f_ref[r, :], lb) for r in rows]
        sig, f, k = ([gt[n] for gt in gates] for n in (0, 1, 3))
        b = [_dot_exact(causal, gt[2]) for gt in gates]
        mid, last = [x[CHUNK // 2:CHUNK // 2 + 1, :] for x in b], [x[CHUNK - 1:CHUNK, :] for x in b]
        e_q = [jnp.exp(b[c] - mid[c]) for c in every]
        e_k = [jnp.exp(mid[c] - b[c]) for c in every]
        e_i = [jnp.exp(x) for x in b]
        e_s = [jnp.exp(last[c] - b[c]) for c in every]
        dec = [jnp.exp(x) for x in last]
        qt, kt, qi, ks = ([a[c] * e[c] for c in every] for a, e in ((q, e_q), (k, e_k), (q, e_i), (k, e_s)))

        def masked(a, b_):
            return [[jnp.where(causal, _dot(a_h, b_h, NT), 0.0) for a_h, b_h in zip(_heads(a[c]), _heads(b_[c]))]
                    for c in every]

        def with_scores(s, other, dims):
            return [jnp.concatenate([_dot(s_h, o_h, dims) for s_h, o_h in zip(s[c], _heads(other[c]))], axis=1)
                    for c in every]

        def per_head(dims, a, b_):
            return [_per_head(lambda a_h, b_h: _dot(a_h, b_h, dims), a[c], b_[c]) for c in every]

        scores, dscores = masked(qt, kt), masked(do, v)
        dqt, dkt, dv_intra = with_scores(dscores, kt, NN), with_scores(dscores, qt, TN), with_scores(scores, do, TN)
        dqi, update = per_head(NN, do, st), per_head(TN, do, qi)

        dst = ds_scr[...]
        dsts = [None] * ncb
        for c in reversed(every):
            dsts[c] = dst
            dst = dec[c] * dst + update[c]
        ds_scr[...] = dst

        dv_state, dks = per_head(NT, ks, dsts), per_head(NN, v, dsts)
        ddec = [jnp.sum(dsts[c] * st[c], axis=0, keepdims=True) for c in every]
        dq = [dqt[c] * e_q[c] + dqi[c] * e_i[c] for c in every]
        dk = [dkt[c] * e_k[c] + dks[c] * e_s[c] for c in every]
        db = [q[c] * dq[c] - k[c] * dk[c] for c in every]
        db_last = [jnp.sum(dks[c] * ks[c], axis=0, keepdims=True) + ddec[c] * dec[c] for c in every]
        dg = [_dot_exact(anti, db[c]) + db_last[c] for c in every]
        df = [dg[c] / f[c] - dk[c] for c in every]
        dlb_scr[...] += sum(jnp.sum(df[c] * (1.0 - sig[c]), axis=0, keepdims=True) for c in every)
        dfp = [df[c] * (1.0 - lb) * sig[c] * (1.0 - sig[c]) for c in every]
        dv = [dv_intra[c] + dv_state[c] for c in every]
        for n, parts in enumerate((dq, dfp, dv)):
            dp_ref[n] = jnp.concatenate(parts, axis=0).astype(dp_ref.dtype)

        @pl.when(i == nblk - 1)
        def _():
            dlb = dlb_scr[...]
            dlbl_ref[0:1, :] = dlb * lb * (1.0 - lb)
            dlbl_ref[1:2, :] = -dlb * lb * s1

    grp = lambda g: pl.BlockSpec((None, tb, GROUP), lambda i: (g, nblk - 1 - i, 0))
    vec = pl.BlockSpec((2, HGRN_WIDTH), lambda i: (0, 0))
    return pl.pallas_call(
        body, name="hgrn_bwd", grid=(nblk,),
        in_specs=[grp(0), grp(1), grp(2), pl.BlockSpec((tb, HGRN_WIDTH), lambda i: (nblk - 1 - i, 0)),
                  pl.BlockSpec((ncb, HEAD_DIM, HGRN_WIDTH), lambda i: (nblk - 1 - i, 0, 0)), vec, ANY],
        out_specs=[pl.BlockSpec((3, tb, HGRN_WIDTH), lambda i: (0, nblk - 1 - i, 0)), vec],
        out_shape=[jax.ShapeDtypeStruct((3, t, HGRN_WIDTH), BF16), jax.ShapeDtypeStruct((2, HGRN_WIDTH), F32)],
        scratch_shapes=[pltpu.VMEM((HEAD_DIM, HGRN_WIDTH), F32), pltpu.VMEM((1, HGRN_WIDTH), F32)],
        compiler_params=_params("arbitrary"),
    )(proj, proj, proj, do, states, lb_logits, after)


GRAD_TILE = 512
OUT_PARTS = 4


class _Side:
    def __init__(self, operands, in_specs, out_shape, out_specs, scratch, init, begin):
        self.operands, self.in_specs, self.out_shape, self.out_specs = operands, in_specs, out_shape, out_specs
        self.scratch, self.init, self.begin = scratch, init, begin


RING = 3


def _grad_w(name, operands, widths, shape, step, after=None, side=None, ringed=None):
    t = operands[0].shape[-2]
    tt = min(t, GRAD_TILE)
    n_in, n_steps = len(operands), t // tt
    in_specs = [ANY if n == ringed else pl.BlockSpec((tt, w), lambda k: (k, 0)) if a.ndim == 2 else
                pl.BlockSpec((a.shape[0], tt, w), lambda k: (0, k, 0)) for n, (a, w) in enumerate(zip(operands, widths))]
    extra = [] if after is None else [after]
    s_in, s_out = (len(side.operands), len(side.out_shape)) if side else (0, 0)
    first_out = n_in + s_in + len(extra)
    side_scratch = side.scratch if side else []
    ring_scratch = [] if ringed is None else [pltpu.VMEM((RING, tt, widths[ringed]), operands[ringed].dtype),
                                              pltpu.SemaphoreType.DMA((RING,))]

    def body(*refs):
        o_ref, side_outs = refs[first_out], refs[first_out + 1:first_out + 1 + s_out]
        acc, narrow, sem = refs[first_out + 1 + s_out:first_out + 4 + s_out]
        first_side = first_out + 4 + s_out
        k = pl.program_id(0)
        tiles = list(refs[:n_in])
        if ringed is not None:
            ring, ring_sems = refs[first_side + len(side_scratch):]

            def fetch(tile, slot):
                return pltpu.make_async_copy(refs[ringed].at[pl.ds(tile * tt, tt), :], ring.at[slot], ring_sems.at[slot])

            @pl.when(k == 0)
            def _():
                for first in range(min(RING - 1, n_steps)):
                    fetch(first, first).start()

            @pl.when(k + RING - 1 < n_steps)
            def _():
                fetch(k + RING - 1, lax.rem(k + RING - 1, RING)).start()

        @pl.when(k == 0)
        def _():
            acc[...] = jnp.zeros_like(acc)
            if side:
                side.init(side_outs)

        if ringed is not None:
            slot = lax.rem(k, RING)
            fetch(k, slot).wait()
            tiles[ringed] = ring.at[slot]

        tick = (side.begin(k, n_steps, refs[n_in:n_in + s_in], side_outs, refs[first_side:first_side + len(side_scratch)])
                if side else None)
        step(acc, *tiles, tick or (lambda j: None))

        @pl.when(k == n_steps - 1)
        def _():
            part = shape[0] // OUT_PARTS
            copies = []
            for p in range(OUT_PARTS):
                rows = pl.ds(p * part, part)
                narrow[rows, :] = acc[rows, :].astype(narrow.dtype)
                copies.append(pltpu.make_async_copy(narrow.at[rows, :], o_ref.at[rows, :], sem.at[p]))
                copies[-1].start()
            for cp in copies:
                cp.wait()

    outs = pl.pallas_call(
        body, name=name, grid=(n_steps,),
        in_specs=in_specs + (side.in_specs if side else []) + [ANY] * len(extra),
        out_specs=[ANY] + (side.out_specs if side else []),
        out_shape=[jax.ShapeDtypeStruct(shape, BF16)] + (side.out_shape if side else []),
        scratch_shapes=[pltpu.VMEM(shape, F32), pltpu.VMEM(shape, BF16), pltpu.SemaphoreType.DMA((OUT_PARTS,))]
                       + side_scratch + ring_scratch,
        compiler_params=_params("arbitrary"),
    )(*operands, *(side.operands if side else ()), *extra)
    return outs if side else outs[0]


def _dw_in(xb, dph, dog, dpc, w_in, dpre1, after):
    t = xb.shape[0]

    def step(acc, x_ref, dh_ref, dog_ref, dc_ref, tick):
        xv = x_ref[...]
        for g in range(N_GROUPS):
            part = dh_ref[g] if g < 3 else dog_ref[...] if g == 3 else dc_ref[g - 4]
            acc[:, g * GROUP:(g + 1) * GROUP] += _dot(xv, part, TN)
            tick(g, part)

    def begin(k, n_steps, ins, outs, scratch):
        w_ref, dp_ref = ins
        total = [ALPHA * dp_ref[...]]

        def tick(g, part):
            total[0] = total[0] + _dot(part, w_ref[:, g * GROUP:(g + 1) * GROUP], NT)
            if g == N_GROUPS - 1:
                outs[0][...] = total[0]

        return tick

    row = pl.BlockSpec((min(t, GRAD_TILE), D_MODEL), lambda k: (k, 0))
    side = _Side((w_in, dpre1), [_resident((D_MODEL, IN_COLS)), row], [jax.ShapeDtypeStruct((t, D_MODEL), F32)], [row],
                 [], lambda outs: None, begin)
    return _grad_w("dw_in", (xb, dph, dog, dpc), (D_MODEL, GROUP, GROUP, GROUP), (D_MODEL, IN_COLS), step, after, side)


def _strips_of(j, tt):
    per_tick = tt // GATE_STRIP // N_FF
    return [slice(s * GATE_STRIP, (s + 1) * GATE_STRIP) for s in range(j * per_tick, (j + 1) * per_tick)]


def _dw_ff1(h1b, da, dcat, o, proj, gate_norm_w, after):
    t = h1b.shape[0]
    tt = min(t, GRAD_TILE)

    def step(acc, h_ref, da_ref, tick):
        hv = h_ref[...]
        for j in range(N_FF):
            cols = slice(j * FF_BLOCK, (j + 1) * FF_BLOCK)
            acc[:, cols] += _dot(hv, da_ref[:, cols], TN)
            tick(j)

    def init(outs):
        outs[2][...] = jnp.zeros_like(outs[2])

    def begin(k, n_steps, ins, outs, scratch):
        do2_ref, o_ref, og_ref, gnw_ref = ins
        do_ref, dog_ref, dgnw_ref = outs
        total = [jnp.zeros((GATE_STRIP, GROUP), F32)]

        def tick(j):
            gnw = gnw_ref[...]
            for rows in _strips_of(j, tt):
                ov, og, do2 = o_ref[rows, :], og_ref[rows, :], do2_ref[rows, :]
                rs = _per_head(lambda o_h: jnp.broadcast_to(
                    lax.rsqrt(jnp.mean(o_h * o_h, axis=-1, keepdims=True) + EPS), o_h.shape), ov)
                on = ov * rs
                sg = _sigmoid(og)
                sil = og * sg
                don = do2 * gnw * sil
                total[0] = total[0] + do2 * on * sil
                dog_ref[rows, :] = (do2 * on * gnw * (sg * (1.0 + og * (1.0 - sg)))).astype(dog_ref.dtype)
                do_ref[rows, :] = rs * (don - on * _per_head(
                    lambda p_h: jnp.broadcast_to(jnp.mean(p_h, axis=-1, keepdims=True), p_h.shape), don * on))
            if j == N_FF - 1:
                dgnw_ref[...] += jnp.sum(total[0], axis=0, keepdims=True)

        return tick

    tile = pl.BlockSpec((tt, GROUP), lambda k: (k, 0))
    vec = pl.BlockSpec((1, GROUP), lambda k: (0, 0))
    side = _Side(
        (dcat, o, proj, gate_norm_w), [tile, tile, pl.BlockSpec((None, tt, GROUP), lambda k: (3, k, 0)), vec],
        [jax.ShapeDtypeStruct((t, HGRN_WIDTH), F32), jax.ShapeDtypeStruct((t, HGRN_WIDTH), BF16),
         jax.ShapeDtypeStruct((1, HGRN_WIDTH), F32)], [tile, tile, vec], [], init, begin)
    return _grad_w("dw_ff1", (h1b, da), (D_MODEL, D_FF), (D_MODEL, D_FF), step, after, side, ringed=1)


def _dw_ff2(r, dpre2b, dcat, bcu, conv_w):
    t = r.shape[0]
    tt = min(t, GRAD_TILE)
    hb = tt // SUBLANES
    halo = 2 * SUBLANES

    def step(acc, r_ref, d_ref, tick):
        dv = d_ref[...]
        for j in range(N_FF):
            rows = slice(j * FF_BLOCK, (j + 1) * FF_BLOCK)
            acc[rows, :] += _dot(r_ref[:, rows], dv, TN)
            tick(j)

    def init(outs):
        outs[1][...] = jnp.zeros_like(outs[1])

    def begin(k, n_steps, ins, outs, scratch):
        dy_ref, dyn_ref, b_ref, bn_ref, c_ref, u_ref, ch_ref, uh_ref, cw_ref = ins
        dp_ref, dcw_ref = outs
        zbuf, dbuf = scratch
        before = lambda ref: ref[SUBLANES:halo, :].astype(F32)
        zbuf[0:SUBLANES, :] = jnp.where(k > 0, before(ch_ref) * before(uh_ref), 0.0)
        zbuf[SUBLANES:SUBLANES + tt, :] = c_ref[...].astype(F32) * u_ref[...].astype(F32)
        dbuf[0:tt, :] = dy_ref[...] * b_ref[...].astype(F32)
        dbuf[tt:tt + SUBLANES, :] = jnp.where(k < n_steps - 1, dyn_ref[...] * bn_ref[0:SUBLANES, :].astype(F32), 0.0)
        totals = [jnp.zeros((GATE_STRIP, GROUP), F32) for _ in range(3)]

        def tick(j):
            cw = cw_ref[...]
            for rows in _strips_of(j, tt):
                at = lambda buf, shift: buf[shift + rows.start:shift + rows.stop, :]
                z, z1, z2 = at(zbuf, SUBLANES), at(zbuf, SUBLANES - 1), at(zbuf, SUBLANES - 2)
                dyc, d1, d2 = at(dbuf, 0), at(dbuf, 1), at(dbuf, 2)
                yc = cw[2:3, :] * z + cw[1:2, :] * z1 + cw[0:1, :] * z2
                dz = cw[2:3, :] * dyc + cw[1:2, :] * d1 + cw[0:1, :] * d2
                dp_ref[0, rows, :] = (dy_ref[rows, :] * yc).astype(dp_ref.dtype)
                dp_ref[1, rows, :] = (dz * u_ref[rows, :].astype(F32)).astype(dp_ref.dtype)
                dp_ref[2, rows, :] = (dz * c_ref[rows, :].astype(F32)).astype(dp_ref.dtype)
                for n, tap in enumerate((z2, z1, z)):
                    totals[n] = totals[n] + dyc * tap
            if j == N_FF - 1:
                for n in range(3):
                    dcw_ref[n:n + 1, :] += jnp.sum(totals[n], axis=0, keepdims=True)

        return tick

    grp = lambda g: pl.BlockSpec((None, tt, GROUP), lambda k: (g, k, 0))
    prev = lambda g: pl.BlockSpec((None, halo, GROUP), lambda k: (g, jnp.maximum(k * (tt // halo) - 1, 0), 0))
    nxt = lambda g: pl.BlockSpec((None, halo, GROUP), lambda k: (g, jnp.minimum((k + 1) * (tt // halo), t // halo - 1), 0))
    nxt_row = lambda k: jnp.minimum((k + 1) * hb, t // SUBLANES - 1)
    whole = pl.BlockSpec((3, CONV_WIDTH), lambda k: (0, 0))
    side = _Side(
        (dcat, dcat, bcu, bcu, bcu, bcu, bcu, bcu, conv_w),
        [pl.BlockSpec((tt, GROUP), lambda k: (k, 1)), pl.BlockSpec((SUBLANES, GROUP), lambda k: (nxt_row(k), 1)),
         grp(0), nxt(0), grp(1), grp(2), prev(1), prev(2), whole],
        [jax.ShapeDtypeStruct((3, t, CONV_WIDTH), BF16), jax.ShapeDtypeStruct((3, CONV_WIDTH), F32)],
        [pl.BlockSpec((3, tt, GROUP), lambda k: (0, k, 0)), whole],
        [pltpu.VMEM((tt + SUBLANES, GROUP), F32), pltpu.VMEM((tt + SUBLANES, GROUP), F32)], init, begin)
    return _grad_w("dw_ff2", (r, dpre2b), (D_FF, D_MODEL), (D_FF, D_MODEL), step, side=side, ringed=0)


def _place():
    x, y, c = lax.axis_index("x"), lax.axis_index("y"), lax.axis_index("c")
    return x, y, c, 2 * x + y


def _other_chips(x, y):
    return [(1 - x, y), (x, 1 - y), (1 - x, 1 - y)]


def _place_shard(name, w, chip, cols_sharded, after=None):
    rows, cols = w.shape
    tr = min(rows, 256)
    nb = rows // tr
    full = (rows, cols * N_CHIPS) if cols_sharded else (rows * N_CHIPS, cols)
    out_map = (lambda i, s: (i, s[0])) if cols_sharded else (lambda i, s: (s[0] * nb + i, 0))

    def body(s_ref, w_ref, *rest):
        rest[-1][...] = w_ref[...].astype(rest[-1].dtype)

    extra = [] if after is None else [after]
    return pl.pallas_call(
        body, name=name,
        grid_spec=pltpu.PrefetchScalarGridSpec(
            num_scalar_prefetch=1, grid=(nb,),
            in_specs=[pl.BlockSpec((tr, cols), lambda i, s: (i, 0))] + [ANY] * len(extra),
            out_specs=pl.BlockSpec((tr, cols), out_map)),
        out_shape=jax.ShapeDtypeStruct(full, BF16),
        compiler_params=_params("parallel"),
    )(chip, w, *extra)


HBM = pl.BlockSpec(memory_space=pltpu.HBM)
SEM = pl.BlockSpec(memory_space=pltpu.SEMAPHORE)
EFFECT = pltpu.SideEffectType.DATAFLOW_SIDE_EFFECTING


PEER_SETS = {
    "sibling": (0, lambda x, y, c: [(x, y, 1 - c)]),
    "chips": (1, lambda x, y, c: [(1 - x, y, c), (x, 1 - y, c), (1 - x, 1 - y, c)]),
    "neighbours": (2, lambda x, y, c: [(1 - x, y, c), (x, 1 - y, c)]),
}


class _Split:
    def __init__(self, name, arrays, plan, others=(), peers=None, prepare=None, sources=(), scratch=()):
        n_own, arrays = len(arrays), (*arrays, *others)
        n, n_copies, n_in = len(arrays), plan.count, len(arrays) + len(sources)
        self.name, self.plan, self.n = name, plan, n_own
        barrier_id, peer_ids = PEER_SETS[peers] if peers else (None, None)

        def body(*refs):
            send_sems, recv_sems, token = refs[n_in], refs[n_in + 1], refs[n_in + 2 + n]
            if peers:
                x, y, c, _ = _place()
                barrier = pltpu.get_barrier_semaphore()
                for peer in peer_ids(x, y, c):
                    pl.semaphore_signal(barrier, inc=1, device_id=peer, device_id_type=MESH)
            if prepare:
                prepare(refs[:n], refs[n:n_in], refs[n_in + 3 + n:])
            if peers:
                pl.semaphore_wait(barrier, len(peer_ids(0, 0, 0)))
            for k, (src, dst, to) in enumerate(plan(refs[:n])):
                pltpu.make_async_remote_copy(src_ref=src, dst_ref=dst, send_sem=send_sems.at[k], recv_sem=recv_sems.at[k],
                                             device_id=to, device_id_type=MESH).start()
            token[...] = jnp.zeros_like(token)

        outs = pl.pallas_call(
            body, name=name + "_start",
            out_shape=(pltpu.SemaphoreType.DMA((n_copies,)), pltpu.SemaphoreType.DMA((n_copies,)),
                       *[pltpu.HBM(a.shape, a.dtype) for a in arrays], jax.ShapeDtypeStruct((SUBLANES, LANES), F32)),
            in_specs=(HBM,) * n_in, out_specs=(SEM, SEM) + (HBM,) * n + (pl.BlockSpec(memory_space=pltpu.VMEM),),
            input_output_aliases={i: 2 + i for i in range(n)}, scratch_shapes=list(scratch),
            compiler_params=pltpu.CompilerParams(has_side_effects=EFFECT, collective_id=barrier_id),
        )(*[pltpu.with_memory_space_constraint(a, pltpu.HBM) for a in (*arrays, *sources)])
        self.sems, self.arrays, self.others, self.token = outs[:2], outs[2:2 + n_own], outs[2 + n_own:2 + n], outs[-1]
        self.waited = set()

    def wait(self, after, copies=None, part=""):
        n, plan = self.n, self.plan
        mine = set(range(plan.count) if copies is None else copies) - self.waited
        self.waited |= mine

        def body(*refs):
            send_sems, recv_sems = refs[n], refs[n + 1]
            for k, (src, dst, to) in enumerate(plan(refs[:n])):
                if k in mine:
                    cp = pltpu.make_async_remote_copy(src_ref=src, dst_ref=dst, send_sem=send_sems.at[k],
                                                      recv_sem=recv_sems.at[k], device_id=to, device_id_type=MESH)
                    cp.wait_send()
                    cp.wait_recv()

        self.arrays = pl.pallas_call(
            body, name=self.name + "_wait" + part, out_shape=tuple(pltpu.HBM(a.shape, a.dtype) for a in self.arrays),
            in_specs=(HBM,) * n + (SEM, SEM, ANY), out_specs=(HBM,) * n, input_output_aliases={i: i for i in range(n)},
            compiler_params=pltpu.CompilerParams(has_side_effects=EFFECT),
        )(*self.arrays, *self.sems, after)
        return self.arrays


COLS_SHARDED = (True, False, True, False)
HALF_SHAPES = [(D_MODEL // 2, IN_COLS), (D_MODEL, D_MODEL // 2), (D_MODEL // 2, D_FF), (D_FF, D_MODEL // 2)]
PIECE_SHAPES = [(D_MODEL // 2, IN_COLS // N_CHIPS), (D_MODEL // N_CHIPS, D_MODEL // 2),
                (D_MODEL // 2, D_FF // N_CHIPS), (D_FF // N_CHIPS, D_MODEL // 2)]


def _shard_view(kind, ref, chip):
    if COLS_SHARDED[kind]:
        n = ref.shape[1] // N_CHIPS
        return ref.at[:, pl.ds(chip * n, n)]
    n = ref.shape[0] // N_CHIPS
    return ref.at[pl.ds(chip * n, n), :]


def _half_view(kind, ref, h):
    if COLS_SHARDED[kind]:
        n = ref.shape[0] // 2
        return ref.at[pl.ds(h * n, n), :]
    n = ref.shape[1] // 2
    return ref.at[:, pl.ds(h * n, n)]


def _plan(count):
    def mark(fn):
        fn.count = count
        return fn
    return mark


def _shard_rows_view(kind, ref, chip, part, n_parts):
    if COLS_SHARDED[kind]:
        m, n = ref.shape[0] // n_parts, ref.shape[1] // N_CHIPS
        return ref.at[pl.ds(part * m, m), pl.ds(chip * n, n)]
    m = ref.shape[0] // N_CHIPS // n_parts
    return ref.at[pl.ds((n_parts * chip + part) * m, m), :]


def _shard_half_view(kind, ref, chip, h):
    return _shard_rows_view(kind, ref, chip, h, 2)


def _gather_over_ici(kinds, weights):
    @_plan(2 * len(kinds))
    def plan(refs):
        x, y, c, me = _place()
        mine = [_shard_half_view(kind, ref, me, c) for kind, ref in zip(kinds, refs)]
        return [(v, v, to) for v in mine for to in ((1 - x, y, c), (x, 1 - y, c))]

    return _Split("gather_ici_" + "".join(map(str, kinds)), tuple(weights), plan, peers="neighbours")


def _relay_over_ici(kinds, weights, others=()):
    @_plan(2 * len(kinds))
    def plan(refs):
        x, y, c, _ = _place()
        x_nbr, y_nbr = 2 * (1 - x) + y, 2 * x + (1 - y)
        out = []
        for kind, ref in zip(kinds, refs):
            first, second = (_shard_rows_view(kind, ref, chip, 2 * c + q, 4) for q, chip in ((0, x_nbr), (1, y_nbr)))
            out += [(first, first, (x, 1 - y, c)), (second, second, (1 - x, y, c))]
        return out

    return _Split("relay_ici_" + "".join(map(str, kinds)), tuple(weights), plan, others, peers="neighbours")


def _gather_w_in_over_ici(shard, conv4):
    rows, cols = shard.shape

    @_plan(6)
    def plan(refs):
        x, y, c, me = _place()
        half, conv = _shard_half_view(0, refs[0], me, c), refs[1].at[me]
        return [(v, v, (px, py, c)) for v in (half, conv) for px, py in _other_chips(x, y)]

    def prepare(refs, sources, scratch):
        wide, narrow, sems = scratch
        _, _, _, me = _place()
        load = pltpu.make_async_copy(sources[0], wide, sems.at[0])
        load.start()
        load.wait()
        narrow[...] = wide[...].astype(narrow.dtype)
        store = pltpu.make_async_copy(narrow, _shard_view(0, refs[0], me), sems.at[1])
        store.start()
        store.wait()

    return _Split("gather_w_in_ici", (lax.empty((rows, cols * N_CHIPS), BF16), conv4), plan, peers="chips",
                  prepare=prepare, sources=(shard,),
                  scratch=(pltpu.VMEM((rows, cols), F32), pltpu.VMEM((rows, cols), BF16), pltpu.SemaphoreType.DMA((2,))))


def _gather_over_d2d(kinds, weights):
    @_plan(3 * len(kinds))
    def plan(refs):
        x, y, c, _ = _place()
        got = [_shard_half_view(kind, ref, 2 * px + py, c) for kind, ref in zip(kinds, refs)
               for px, py in _other_chips(x, y)]
        return [(v, v, (x, y, 1 - c)) for v in got]

    return _Split("gather_d2d_" + "".join(map(str, kinds)), tuple(weights), plan, peers="sibling")


def _swap_halves(kinds, grads):
    @_plan(len(kinds))
    def plan(refs):
        x, y, c, _ = _place()
        return [(_half_view(kind, g, 1 - c), land, (x, y, 1 - c))
                for kind, g, land in zip(kinds, refs[:len(kinds)], refs[len(kinds):])]

    lands = [lax.empty(HALF_SHAPES[kind], g.dtype) for kind, g in zip(kinds, grads)]
    return _Split("swap_halves_" + "".join(map(str, kinds)), (*grads, *lands), plan, peers="sibling")


def _block_rows(cols, elements):
    return 1 << ((elements // cols).bit_length() - 1)


def _grid_steps(shapes, elements):
    rows, cols = max(shapes, key=lambda shape: shape[0] * shape[1])
    return rows // min(rows, _block_rows(cols, elements))


def _add_half(name, kinds, grads, recvs, core):
    n = len(kinds)
    shapes = [recv.shape for recv in recvs]
    nb = _grid_steps(shapes, 1 << 20)

    def body(c_ref, *refs):
        for g_ref, r_ref, o_ref in zip(refs[:n], refs[n:2 * n], refs[2 * n:]):
            o_ref[...] = (g_ref[...].astype(F32) + r_ref[...].astype(F32)).astype(o_ref.dtype)

    own = [pl.BlockSpec((rows // nb, cols), (lambda i, c_ref: (c_ref[0] * nb + i, 0)) if COLS_SHARDED[k] else
                        (lambda i, c_ref: (i, c_ref[0]))) for k, (rows, cols) in zip(kinds, shapes)]
    blocks = [pl.BlockSpec((rows // nb, cols), lambda i, c_ref: (i, 0)) for rows, cols in shapes]
    return pl.pallas_call(
        body, name=name,
        grid_spec=pltpu.PrefetchScalarGridSpec(
            num_scalar_prefetch=1, grid=(nb,), in_specs=own + blocks, out_specs=blocks),
        out_shape=[jax.ShapeDtypeStruct(shape, BF16) for shape in shapes],
        compiler_params=_params("parallel"),
    )(core, *grads, *recvs)


def _exchange_pieces(kinds, halves, pack=None):
    n_p, n = N_CHIPS - 1, len(kinds)

    @_plan(n_p * n + (0 if pack is None else N_DEV - 1))
    def plan(refs):
        x, y, c, _ = _place()
        copies = []
        if pack is not None:
            me = 4 * x + 2 * y + c
            peers = [((1 - x) if m & 4 else x, (1 - y) if m & 2 else y, (1 - c) if m & 1 else c) for m in range(1, N_DEV)]
            copies += [(refs[2 * n], refs[2 * n + 1].at[me], peer) for peer in peers]
        return copies + [(_shard_view(kind, half, 2 * px + py), land.at[j], (px, py, c))
                         for j, (px, py) in enumerate(_other_chips(x, y))
                         for kind, half, land in zip(kinds, refs[:n], refs[n:2 * n])]

    lands = [lax.empty((n_p,) + PIECE_SHAPES[kind], BF16) for kind in kinds]
    small = () if pack is None else (pack, lax.empty((N_DEV,) + pack.shape, F32))
    return _Split("exchange_pieces_" + "".join(map(str, kinds)), (*halves, *lands, *small), plan,
                  peers="chips" if pack is None else None)


def _sum_pieces(name, kinds, halves, slots, place, after):
    n, n_p = len(kinds), N_CHIPS - 1
    shapes = [slot.shape[1:] for slot in slots]
    nb = _grid_steps(shapes, 1 << 18)

    def body(s_ref, *refs):
        for own_ref, slot_ref, o_ref in zip(refs[:n], refs[n:2 * n], refs[2 * n + 1:]):
            total = own_ref[...].astype(F32)
            for j in range(n_p):
                total = total + slot_ref[j].astype(F32)
            o_ref[...] = total

    own, out, shards = [], [], []
    for k, (rows, cols) in zip(kinds, shapes):
        if COLS_SHARDED[k]:
            own_map, out_map, shard = (lambda i, s: (i, s[0])), (lambda i, s: (s[1] * nb + i, 0)), (2 * rows, cols)
        else:
            own_map, out_map, shard = (lambda i, s: (s[0] * nb + i, 0)), (lambda i, s: (i, s[1])), (rows, 2 * cols)
        own.append(pl.BlockSpec((rows // nb, cols), own_map))
        out.append(pl.BlockSpec((rows // nb, cols), out_map))
        shards.append(jax.ShapeDtypeStruct(shard, F32))
    landed = [pl.BlockSpec((n_p, rows // nb, cols), lambda i, s: (0, i, 0)) for rows, cols in shapes]
    return pl.pallas_call(
        body, name=name,
        grid_spec=pltpu.PrefetchScalarGridSpec(
            num_scalar_prefetch=1, grid=(nb,), in_specs=own + landed + [ANY], out_specs=out),
        out_shape=shards,
        compiler_params=_params("parallel"),
    )(place, *halves, *slots, after)


def _join_halves(kinds, shards):
    @_plan(len(kinds))
    def plan(refs):
        x, y, c, _ = _place()
        return [(_half_view(kind, g, c), _half_view(kind, g, c), (x, y, 1 - c)) for kind, g in zip(kinds, refs)]

    return _Split("join_halves_" + "".join(map(str, kinds)), tuple(shards), plan, peers="sibling")


N_DEV = 8


def _sum_shared(pack, land, device):
    def body(d_ref, p_ref, l_ref, o_ref):
        me = d_ref[0]
        total = jnp.where(me == 0, p_ref[...], l_ref[0])
        for d in range(1, N_DEV):
            total = total + jnp.where(me == d, p_ref[...], l_ref[d])
        o_ref[...] = total

    return pl.pallas_call(
        body, name="sum_shared",
        grid_spec=pltpu.PrefetchScalarGridSpec(
            num_scalar_prefetch=1, grid=(1,),
            in_specs=[pl.BlockSpec(pack.shape, lambda i, d: (0, 0)), pl.BlockSpec(land.shape, lambda i, d: (0, 0, 0))],
            out_specs=pl.BlockSpec(pack.shape, lambda i, d: (0, 0))),
        out_shape=jax.ShapeDtypeStruct(pack.shape, F32),
    )(device, pack, land)


def _adamw(name, weights, after=None):
    shapes = [w.shape for w, _, _, _ in weights]
    nb = _grid_steps(shapes, 1 << 18)
    extra = [] if after is None else [after]
    n_in = 4 * len(weights) + len(extra)

    def body(*refs):
        for k in range(len(weights)):
            w_ref, g_ref, m_ref, v_ref = refs[4 * k:4 * k + 4]
            go_ref, d_ref, nm_ref, nv_ref = refs[n_in + 4 * k:n_in + 4 * k + 4]
            g = g_ref[...]
            go_ref[...] = g
            d_ref[...], nm_ref[...], nv_ref[...] = _adam_step(w_ref[...], g, m_ref[...], v_ref[...])

    blocks = [pl.BlockSpec((rows // nb, cols), lambda i: (i, 0)) for rows, cols in shapes for _ in range(4)]
    outs = pl.pallas_call(
        body, name=name, grid=(nb,), in_specs=blocks + [ANY] * len(extra), out_specs=blocks,
        out_shape=[jax.ShapeDtypeStruct(shape, F32) for shape in shapes for _ in range(4)],
        compiler_params=_params("parallel"),
    )(*[a for group in weights for a in group], *extra)
    return [outs[4 * k:4 * k + 4] for k in range(len(weights))]


def _adam_step(w, g, m, v):
    nm = ADAM_B1 * m + (1.0 - ADAM_B1) * g
    nv = ADAM_B2 * v + (1.0 - ADAM_B2) * jnp.square(g)
    m_hat = nm * (1.0 / (1.0 - ADAM_B1 ** ADAM_STEP))
    v_hat = nv * (1.0 / (1.0 - ADAM_B2 ** ADAM_STEP))
    return -ADAM_LR * (m_hat / (jnp.sqrt(v_hat) + ADAM_EPS) + ADAM_WD * w), nm, nv


def _adamw_small(tot, chip, weights, ms, vs, after):
    n, half = len(weights), D_MODEL // 2

    def body(chip_ref, tot_ref, *refs):
        ins, outs = refs[:3 * n], refs[3 * n + 1:]
        tot = tot_ref[...]
        conv_all = jnp.concatenate([tot[5:6, half:], tot[6:7, :half], tot[6:7, half:]], axis=0)
        conv = sum(jnp.where(chip_ref[0] == s, conv_all[:, s * LANES:(s + 1) * LANES], 0.0) for s in range(N_CHIPS))
        grads = [jnp.concatenate([tot[4:5, :half], tot[4:5, half:]], axis=0), tot[5:6, :half], conv,
                 tot[0:1], tot[1:2], tot[2:3], tot[3:4]]
        for k, g in enumerate(grads):
            delta, nm, nv = _adam_step(ins[k][...], g, ins[n + k][...], ins[2 * n + k][...])
            outs[k][...], outs[n + k][...], outs[2 * n + k][...], outs[3 * n + k][...] = g, delta, nm, nv
        outs[4 * n][...] = tot[7:8, 0:1]

    whole = lambda a: pl.BlockSpec(a.shape, lambda i, s: (0,) * a.ndim)
    arrays = (*weights, *ms, *vs)
    loss = jax.ShapeDtypeStruct((1, 1), F32)
    return pl.pallas_call(
        body, name="adamw_small",
        grid_spec=pltpu.PrefetchScalarGridSpec(
            num_scalar_prefetch=1, grid=(1,), in_specs=[whole(tot)] + [whole(a) for a in arrays] + [ANY],
            out_specs=[whole(a) for a in weights] * 4 + [whole(loss)]),
        out_shape=[jax.ShapeDtypeStruct(a.shape, F32) for a in weights] * 4 + [loss],
    )(chip, tot, *arrays, after)


def kernel(x, w_in, lb_logits, gate_norm_w, conv_w, w_out, ln1_g, ln1_b, w_ff1, w_ff2, ln2_g, ln2_b, loss_target, m_w_in, m_lb_logits, m_gate_norm_w, m_conv_w, m_w_out, m_ln1_g, m_ln1_b, m_w_ff1, m_w_ff2, m_ln2_g, m_ln2_b, v_w_in, v_lb_logits, v_gate_norm_w, v_conv_w, v_w_out, v_ln1_g, v_ln1_b, v_w_ff1, v_w_ff2, v_ln2_g, v_ln2_b):
    xs, tgt = x[0], loss_target[0]
    chip = 2 * lax.axis_index("x") + lax.axis_index("y")
    core = lax.axis_index("c").astype(jnp.int32).reshape(1)
    chip1 = chip.astype(jnp.int32).reshape(1)
    place = jnp.concatenate([chip1, core])

    conv4 = lax.dynamic_update_slice(jnp.zeros((N_CHIPS,) + conv_w.shape[1:], F32), conv_w, (chip, 0, 0))
    ici_in = _gather_w_in_over_ici(w_in[0], conv4)
    rest = (1, 2, 3)
    ici_rest = _gather_over_ici(rest, (_place_shard("place_w_out", w_out[0], chip1, False, after=ici_in.token),
                                       _place_shard("place_w_ff1", w_ff1[0], chip1, True, after=ici_in.token),
                                       _place_shard("place_w_ff2", w_ff2[0], chip1, False, after=ici_in.token)))
    wb_in, cv4 = ici_in.wait(ici_rest.token)
    d2d_in = _gather_over_d2d((0,), (wb_in,))
    wb_in, = d2d_in.wait(d2d_in.token)
    conv_full = cv4.transpose(1, 0, 2).reshape(3, CONV_WIDTH)

    proj, bcu, xb, cat_c = _in_proj(xs, wb_in, conv_full, ici_rest.token)
    relay_rest = _relay_over_ici(rest, ici_rest.wait(proj))
    o, states = _hgrn_fwd(proj, lb_logits, relay_rest.token)
    d2d_rest = _gather_over_d2d(rest, relay_rest.wait(o))
    cat_h = _gate_fwd(proj, o, gate_norm_w, d2d_rest.token)
    wb_out, wb_ff1, wb_ff2 = d2d_rest.wait(cat_h)

    (h1b, r, da, dpre2b, dpre1, dcat, g_ln1_g, g_ln1_b, g_ln2_g, g_ln2_b, loss8, g_out_local) = _sublayers(
        cat_h, cat_c, xs, tgt, wb_out, wb_ff1, wb_ff2, ln1_g, ln1_b, ln2_g, ln2_b)

    names = ("w_in", "w_out", "w_ff1", "w_ff2")

    def named(prefix, kinds):
        return prefix + "".join("_" + names[k] for k in kinds)

    def add_halves(kinds, grads, lands):
        return _add_half(named("add_half", kinds), kinds, grads, lands, core)

    def sum_pieces(kinds, halves, lands, after):
        return _sum_pieces(named("sum_pieces", kinds), kinds, halves, lands, place, after)

    early = (1, 2, 3)
    g_ff2_local, dpc, g_conv = _dw_ff2(r, dpre2b, dcat, bcu, conv_full)
    swap_a = _swap_halves((1, 3), (g_out_local, g_ff2_local))
    g_ff1_local, do, dog, g_gnw = _dw_ff1(h1b, da, dcat, o, proj, gate_norm_w, swap_a.token)
    swap_b = _swap_halves((2,), (g_ff1_local,))
    swapped_a = swap_a.wait(swap_b.token)
    halves_a = add_halves((1, 3), swapped_a[:2], swapped_a[2:])
    swapped_b = swap_b.wait(halves_a[1])
    halves = (halves_a[0], *add_halves((2,), swapped_b[:1], swapped_b[1:]), halves_a[1])
    exch = _exchange_pieces(early, halves)
    dph, g_lbl = _hgrn_bwd(proj, do, states, lb_logits, exch.token)
    g_in_local, grad_x = _dw_in(xb, dph, dog, dpc, wb_in, dpre1, dph)

    late = (0,)
    swap = _swap_halves(late, (g_in_local,))
    exchanged = exch.wait(swap.token)
    pack = jnp.concatenate([
        g_ln1_g, g_ln1_b, g_ln2_g, g_ln2_b,
        jnp.concatenate([g_lbl[0:1], g_lbl[1:2]], axis=1),
        jnp.concatenate([g_gnw, g_conv[0:1]], axis=1),
        jnp.concatenate([g_conv[1:2], g_conv[2:3]], axis=1),
        jnp.concatenate([loss8[0:1], jnp.zeros((1, D_MODEL - LANES), F32)], axis=1)], axis=0)
    join_a = _join_halves((2,), sum_pieces((2,), exchanged[1:2], exchanged[4:5], swap.token))
    swapped = swap.wait(join_a.token)
    exch = _exchange_pieces(late, add_halves(late, swapped[:1], swapped[1:]), pack)
    join_b = _join_halves((1, 3), sum_pieces((1, 3), exchanged[0:3:2], exchanged[3:6:2], exch.token))
    g_w_ff1, = join_a.wait(join_b.token)
    (g_w_ff1, d_ff1, nm_ff1, nv_ff1), = _adamw("adamw_w_ff1", [(w_ff1[0], g_w_ff1, m_w_ff1[0], v_w_ff1[0])])
    g_w_out, g_w_ff2 = join_b.wait(d_ff1)
    (g_w_ff2, d_ff2, nm_ff2, nv_ff2), (g_w_out, d_out, nm_out, nv_out) = _adamw(
        "adamw_w_ff2_w_out", [(w_ff2[0], g_w_ff2, m_w_ff2[0], v_w_ff2[0]), (w_out[0], g_w_out, m_w_out[0], v_w_out[0])])
    shared = exch.wait(d_out, range(N_DEV - 1), "_pack")
    tot = _sum_shared(shared[2], shared[3], 2 * chip1 + core)
    exchanged = exch.wait(tot)
    join = _join_halves(late, sum_pieces(late, exchanged[:1], exchanged[1:2], tot))
    small = ("lb_logits", "gate_norm_w", "conv_w", "ln1_g", "ln1_b", "ln2_g", "ln2_b")
    small_out = _adamw_small(
        tot, chip1, (lb_logits, gate_norm_w, conv_w[0], ln1_g, ln1_b, ln2_g, ln2_b),
        (m_lb_logits, m_gate_norm_w, m_conv_w[0], m_ln1_g, m_ln1_b, m_ln2_g, m_ln2_b),
        (v_lb_logits, v_gate_norm_w, v_conv_w[0], v_ln1_g, v_ln1_b, v_ln2_g, v_ln2_b), join.token)
    g_w_in, = join.wait(small_out[0])
    (g_w_in, d_in, nm_in, nv_in), = _adamw("adamw_w_in", [(w_in[0], g_w_in, m_w_in[0], v_w_in[0])])
    loss = small_out[4 * len(small)][0, 0]

    def results(n_kind, large):
        out = dict(zip(small, small_out[n_kind * len(small):(n_kind + 1) * len(small)]))
        out["conv_w"] = out["conv_w"][None]
        out.update({name: a[None] for name, a in zip(("w_in", "w_out", "w_ff1", "w_ff2"), large)})
        return [out[name] for name in ("w_in", "lb_logits", "gate_norm_w", "conv_w", "w_out", "ln1_g", "ln1_b",
                                       "w_ff1", "w_ff2", "ln2_g", "ln2_b")]

    return (loss, grad_x[None], *results(0, (g_w_in, g_w_out, g_w_ff1, g_w_ff2)),
            *results(1, (d_in, d_out, d_ff1, d_ff2)), *results(2, (nm_in, nm_out, nm_ff1, nm_ff2)),
            *results(3, (nv_in, nv_out, nv_ff1, nv_ff2)))
```

```python
import jax
import jax.numpy as jnp
from jax import lax
from jax.experimental import pallas as pl
from jax.experimental.pallas import tpu as pltpu

F32 = jnp.float32
BF16 = jnp.bfloat16
MXU_DTYPE = jnp.bfloat16

D_MODEL = 1024
HGRN_WIDTH = 512
HEAD_DIM = 128
N_HEADS = 4
CONV_WIDTH = 512
CHUNK = 64
D_FF = 4096
IN_COLS = 3584
GROUP = 512
N_GROUPS = IN_COLS // GROUP
ALPHA = 2.0 ** 0.25
EPS = 1e-5
N_CHIPS = 4
ADAM_LR, ADAM_B1, ADAM_B2, ADAM_EPS, ADAM_WD, ADAM_STEP = 0.001, 0.9, 0.999, 1e-08, 0.01, 10

LANES = 128
SUBLANES = 8
VMEM_LIMIT = 56 * 1024 * 1024
FF_BLOCK = 1024
N_FF = D_FF // FF_BLOCK
GATE_STRIP = 64

NN = (((1,), (0,)), ((), ()))
NT = (((1,), (1,)), ((), ()))
TN = (((0,), (0,)), ((), ()))
MESH = pl.DeviceIdType.MESH
ANY = pl.BlockSpec(memory_space=pl.ANY)


def _dot(a, b, dims):
    return lax.dot_general(a.astype(MXU_DTYPE), b.astype(MXU_DTYPE), dims, preferred_element_type=F32)


def _dot_exact(ones, v):
    ones = ones.astype(jnp.bfloat16)
    hi = v.astype(jnp.bfloat16)
    rest = v - hi.astype(F32)
    mid = rest.astype(jnp.bfloat16)
    low = (rest - mid.astype(F32)).astype(jnp.bfloat16)
    return sum(lax.dot_general(ones, part, NN, preferred_element_type=F32) for part in (hi, mid, low))


def _params(*sem):
    return pltpu.CompilerParams(dimension_semantics=sem, vmem_limit_bytes=VMEM_LIMIT)


def _resident(shape):
    return pl.BlockSpec(shape, lambda *_: (0,) * len(shape), pipeline_mode=pl.Buffered(1))


def _sigmoid(v):
    return 1.0 / (1.0 + jnp.exp(-v))


def _lower_bound(lbl):
    m = jnp.max(lbl, axis=0, keepdims=True)
    e = jnp.exp(lbl - m)
    s = e / jnp.sum(e, axis=0, keepdims=True)
    return s[0:1, :], s[1:2, :]


def _heads(v):
    return [v[:, h * HEAD_DIM:(h + 1) * HEAD_DIM] for h in range(N_HEADS)]


def _per_head(fn, *arrays):
    return jnp.concatenate([fn(*parts) for parts in zip(*map(_heads, arrays))], axis=1)


def _in_proj(x, w_in, conv_w, after):
    t = x.shape[0]
    tm = min(t, 512)

    def body(x_ref, w_ref, cw_ref, after_ref, o_ref, bcu_ref, xb_ref, y_ref, zbuf):
        @pl.when(pl.program_id(0) == 0)
        def _():
            zbuf[tm:tm + SUBLANES, :] = jnp.zeros((SUBLANES, CONV_WIDTH), F32)

        xb = x_ref[...].astype(xb_ref.dtype)
        xb_ref[...] = xb
        group = lambda g: _dot(xb, w_ref[:, g * GROUP:(g + 1) * GROUP], NN)
        for g in range(4):
            o_ref[g] = group(g)
        b_gate, c_gate, u = group(4), group(5), group(6)
        for n, part in enumerate((b_gate, c_gate, u)):
            bcu_ref[n] = part.astype(bcu_ref.dtype)
        zbuf[0:SUBLANES, :] = zbuf[tm:tm + SUBLANES, :]
        zbuf[SUBLANES:SUBLANES + tm, :] = c_gate * u
        cw = cw_ref[...]
        at = lambda shift: zbuf[shift:shift + tm, :]
        conv = cw[2:3, :] * at(SUBLANES) + cw[1:2, :] * at(SUBLANES - 1) + cw[0:1, :] * at(SUBLANES - 2)
        y_ref[...] = (b_gate * conv).astype(y_ref.dtype)

    return pl.pallas_call(
        body, name="in_proj", grid=(t // tm,),
        in_specs=[pl.BlockSpec((tm, D_MODEL), lambda i: (i, 0)), _resident((D_MODEL, IN_COLS)),
                  pl.BlockSpec((3, CONV_WIDTH), lambda i: (0, 0)), ANY],
        out_specs=[pl.BlockSpec((4, tm, GROUP), lambda i: (0, i, 0)), pl.BlockSpec((3, tm, GROUP), lambda i: (0, i, 0)),
                   pl.BlockSpec((tm, D_MODEL), lambda i: (i, 0)), pl.BlockSpec((tm, CONV_WIDTH), lambda i: (i, 0))],
        out_shape=[jax.ShapeDtypeStruct((4, t, GROUP), F32), jax.ShapeDtypeStruct((3, t, GROUP), BF16),
                   jax.ShapeDtypeStruct((t, D_MODEL), BF16), jax.ShapeDtypeStruct((t, CONV_WIDTH), BF16)],
        scratch_shapes=[pltpu.VMEM((tm + SUBLANES, CONV_WIDTH), F32)],
        compiler_params=_params("arbitrary"),
    )(x, w_in, conv_w, after)


def _gates(fp, lb):
    sig = _sigmoid(fp)
    f = lb + (1.0 - lb) * sig
    return sig, f, jnp.log(f), 1.0 - f


def _chunk_masks():
    row = lax.broadcasted_iota(jnp.int32, (CHUNK, CHUNK), 0)
    col = lax.broadcasted_iota(jnp.int32, (CHUNK, CHUNK), 1)
    return row >= col, row <= col


def _hgrn_fwd(proj, lb_logits, after):
    t = proj.shape[1]
    tb = min(t, 512)
    ncb = tb // CHUNK

    def body(q_ref, f_ref, v_ref, lbl_ref, after_ref, o_ref, st_ref, s_scr):
        @pl.when(pl.program_id(0) == 0)
        def _():
            s_scr[...] = jnp.zeros_like(s_scr)

        lb, _ = _lower_bound(lbl_ref[...])
        causal, _ = _chunk_masks()

        every = range(ncb)
        rows = [slice(c * CHUNK, (c + 1) * CHUNK) for c in every]
        q, v = [q_ref[r, :] for r in rows], [v_ref[r, :] for r in rows]
        gates = [_gates(f_ref[r, :], lb) for r in rows]
        k = [gt[3] for gt in gates]
        b = [_dot_exact(causal, gt[2]) for gt in gates]
        mid, last = [x[CHUNK // 2:CHUNK // 2 + 1, :] for x in b], [x[CHUNK - 1:CHUNK, :] for x in b]
        qt = [q[c] * jnp.exp(b[c] - mid[c]) for c in every]
        kt = [k[c] * jnp.exp(mid[c] - b[c]) for c in every]
        qi = [q[c] * jnp.exp(b[c]) for c in every]
        ks = [k[c] * jnp.exp(last[c] - b[c]) for c in every]
        dec = [jnp.exp(x) for x in last]
        scores = [[jnp.where(causal, _dot(a, b_, NT), 0.0) for a, b_ in zip(_heads(qt[c]), _heads(kt[c]))] for c in every]
        intra = [[_dot(s, v_h, NN) for s, v_h in zip(scores[c], _heads(v[c]))] for c in every]
        update = [_per_head(lambda v_h, ks_h: _dot(v_h, ks_h, TN), v[c], ks[c]) for c in every]

        st = s_scr[...]
        states = []
        for c in every:
            states.append(st)
            st_ref[c] = st
            st = dec[c] * st + update[c]
        s_scr[...] = st

        o_ref[...] = jnp.concatenate(
            [jnp.concatenate([i_h + _dot(qi_h, st_h, NT) for i_h, qi_h, st_h in
                              zip(intra[c], _heads(qi[c]), _heads(states[c]))], axis=1) for c in every], axis=0)

    grp = lambda g: pl.BlockSpec((None, tb, GROUP), lambda i: (g, i, 0))
    return pl.pallas_call(
        body, name="hgrn_fwd", grid=(t // tb,),
        in_specs=[grp(0), grp(1), grp(2), pl.BlockSpec((2, HGRN_WIDTH), lambda i: (0, 0)), ANY],
        out_specs=[pl.BlockSpec((tb, HGRN_WIDTH), lambda i: (i, 0)),
                   pl.BlockSpec((ncb, HEAD_DIM, HGRN_WIDTH), lambda i: (i, 0, 0))],
        out_shape=[jax.ShapeDtypeStruct((t, HGRN_WIDTH), F32),
                   jax.ShapeDtypeStruct((t // CHUNK, HEAD_DIM, HGRN_WIDTH), F32)],
        scratch_shapes=[pltpu.VMEM((HEAD_DIM, HGRN_WIDTH), F32)],
        compiler_params=_params("arbitrary"),
    )(proj, proj, proj, lb_logits, after)


def _gate_fwd(proj, o, gate_norm_w, after):
    t = proj.shape[1]
    tb = min(t, 1024)

    def body(o_ref, og_ref, gnw_ref, after_ref, out_ref):
        gnw = gnw_ref[...]
        for s in range(tb // GATE_STRIP):
            rows = slice(s * GATE_STRIP, (s + 1) * GATE_STRIP)
            og = og_ref[rows, :]
            on = _per_head(lambda o_h: o_h * lax.rsqrt(jnp.mean(o_h * o_h, axis=-1, keepdims=True) + EPS), o_ref[rows, :])
            out_ref[rows, :] = (on * gnw * (og * _sigmoid(og))).astype(out_ref.dtype)

    tile = pl.BlockSpec((tb, GROUP), lambda i: (i, 0))
    return pl.pallas_call(
        body, name="gate_fwd", grid=(t // tb,),
        in_specs=[tile, pl.BlockSpec((None, tb, GROUP), lambda i: (3, i, 0)), pl.BlockSpec((1, GROUP), lambda i: (0, 0)), ANY],
        out_specs=tile,
        out_shape=jax.ShapeDtypeStruct((t, HGRN_WIDTH), BF16),
        compiler_params=_params("parallel"),
    )(o, proj, gate_norm_w, after)


def _ln_bwd(dy, xhat, rstd, g):
    dxhat = dy * g
    m1 = jnp.mean(dxhat, axis=-1, keepdims=True)
    m2 = jnp.mean(dxhat * xhat, axis=-1, keepdims=True)
    return rstd * (dxhat - m1 - xhat * m2)


def _layer_norm(pre):
    xc = pre - jnp.mean(pre, axis=-1, keepdims=True)
    rstd = lax.rsqrt(jnp.mean(xc * xc, axis=-1, keepdims=True) + EPS)
    return xc * rstd, rstd


def _sublayers(cat_h, cat_c, x, target, w_out, w_ff1, w_ff2, g1, b1, g2, b2):
    t = x.shape[0]
    tm = min(t, 256)

    def body(ch_ref, cc_ref, x_ref, tg_ref, wo_ref, w1_ref, w2_ref, g1_ref, b1_ref, g2_ref, b2_ref,
             h1_ref, r_ref, da_ref, dp2b_ref, dp1_ref, dcat_ref, dg1_ref, db1_ref, dg2_ref, db2_ref, loss_ref, gwo_ref,
             gwo_acc, gwo_narrow, sem):
        @pl.when(pl.program_id(0) == 0)
        def _():
            for ref in (dg1_ref, db1_ref, dg2_ref, db2_ref, loss_ref, gwo_acc):
                ref[...] = jnp.zeros_like(ref)

        mix = _dot(ch_ref[...], wo_ref[0:GROUP, :], NN) + _dot(cc_ref[...], wo_ref[GROUP:2 * GROUP, :], NN)
        xhat1, rstd1 = _layer_norm(ALPHA * x_ref[...] + mix)
        h1 = xhat1 * g1_ref[...] + b1_ref[...]
        h1b = h1.astype(h1_ref.dtype)
        h1_ref[...] = h1b
        mlp = jnp.zeros((tm, D_MODEL), F32)
        for j in range(N_FF):
            cols = slice(j * FF_BLOCK, (j + 1) * FF_BLOCK)
            r = jnp.square(jnp.maximum(_dot(h1b, w1_ref[:, cols], NN), 0.0)).astype(r_ref.dtype)
            r_ref[:, cols] = r
            mlp = mlp + _dot(r, w2_ref[cols, :], NN)
        xhat2, rstd2 = _layer_norm(ALPHA * h1 + mlp)
        err = xhat2 * g2_ref[...] + b2_ref[...] - tg_ref[...]
        loss_ref[...] += 0.5 * jnp.sum(jnp.mean(err * err, axis=-1, keepdims=True))
        dy = err * (1.0 / D_MODEL)
        dg2_ref[...] += jnp.sum(dy * xhat2, axis=0, keepdims=True)
        db2_ref[...] += jnp.sum(dy, axis=0, keepdims=True)
        dp2 = _ln_bwd(dy, xhat2, rstd2, g2_ref[...])
        dp2b = dp2.astype(dp2b_ref.dtype)
        dp2b_ref[...] = dp2b
        back = jnp.zeros((tm, D_MODEL), F32)
        for j in range(N_FF):
            cols = slice(j * FF_BLOCK, (j + 1) * FF_BLOCK)
            dr = _dot(dp2b, w2_ref[cols, :], NT)
            da = (dr * (2.0 * jnp.sqrt(r_ref[:, cols].astype(F32)))).astype(da_ref.dtype)
            da_ref[:, cols] = da
            back = back + _dot(da, w1_ref[:, cols], NT)
        dh1 = ALPHA * dp2 + back
        dg1_ref[...] += jnp.sum(dh1 * xhat1, axis=0, keepdims=True)
        db1_ref[...] += jnp.sum(dh1, axis=0, keepdims=True)
        dp1 = _ln_bwd(dh1, xhat1, rstd1, g1_ref[...])
        dp1b = dp1.astype(MXU_DTYPE)
        dp1_ref[...] = dp1
        dcat_ref[...] = _dot(dp1b, wo_ref[...], NT)
        gwo_acc[0:GROUP, :] += _dot(ch_ref[...], dp1b, TN)
        gwo_acc[GROUP:2 * GROUP, :] += _dot(cc_ref[...], dp1b, TN)

        @pl.when(pl.program_id(0) == pl.num_programs(0) - 1)
        def _():
            gwo_narrow[...] = gwo_acc[...].astype(gwo_narrow.dtype)
            copy = pltpu.make_async_copy(gwo_narrow, gwo_ref, sem.at[0])
            copy.start()
            copy.wait()

    row = pl.BlockSpec((tm, D_MODEL), lambda i: (i, 0))
    wide = pl.BlockSpec((tm, D_FF), lambda i: (i, 0))
    vec = pl.BlockSpec((1, D_MODEL), lambda i: (0, 0))
    narrow = lambda dtype: jax.ShapeDtypeStruct((t, D_MODEL), dtype)
    return pl.pallas_call(
        body, name="sublayers", grid=(t // tm,),
        in_specs=[pl.BlockSpec((tm, GROUP), lambda i: (i, 0)), pl.BlockSpec((tm, GROUP), lambda i: (i, 0)), row, row,
                  _resident((D_MODEL, D_MODEL)),
                  _resident((D_MODEL, D_FF)), _resident((D_FF, D_MODEL)), vec, vec, vec, vec],
        out_specs=[row, wide, wide, row, row, row, vec, vec, vec, vec,
                   pl.BlockSpec((SUBLANES, LANES), lambda i: (0, 0)), ANY],
        out_shape=[narrow(BF16), jax.ShapeDtypeStruct((t, D_FF), BF16), jax.ShapeDtypeStruct((t, D_FF), BF16),
                   narrow(BF16), narrow(F32), narrow(F32)]
                  + [jax.ShapeDtypeStruct((1, D_MODEL), F32)] * 4
                  + [jax.ShapeDtypeStruct((SUBLANES, LANES), F32), jax.ShapeDtypeStruct((D_MODEL, D_MODEL), BF16)],
        scratch_shapes=[pltpu.VMEM((D_MODEL, D_MODEL), F32), pltpu.VMEM((D_MODEL, D_MODEL), BF16),
                        pltpu.SemaphoreType.DMA((1,))],
        compiler_params=_params("arbitrary"),
    )(cat_h, cat_c, x, target, w_out, w_ff1, w_ff2, g1, b1, g2, b2)


def _hgrn_bwd(proj, do, states, lb_logits, after):
    t = proj.shape[1]
    tb = min(t, 512)
    ncb = tb // CHUNK
    nblk = t // tb

    def body(q_ref, f_ref, v_ref, do_ref, st_ref, lbl_ref, after_ref, dp_ref, dlbl_ref, ds_scr, dlb_scr):
        i = pl.program_id(0)

        @pl.when(i == 0)
        def _():
            ds_scr[...] = jnp.zeros_like(ds_scr)
            dlb_scr[...] = jnp.zeros_like(dlb_scr)

        lb, s1 = _lower_bound(lbl_ref[...])
        causal, anti = _chunk_masks()
        every = range(ncb)
        rows = [slice(c * CHUNK, (c + 1) * CHUNK) for c in every]
        q, v, do = ([ref[r, :] for r in rows] for ref in (q_ref, v_ref, do_ref))
        st = [st_ref[c] for c in every]
        gates = [_gates(f_ref[r, :], lb) for r in rows]
        sig, f, k = ([gt[n] for gt in gates] for n in (0, 1, 3))
        b = [_dot_exact(causal, gt[2]) for gt in gates]
        mid, last = [x[CHUNK // 2:CHUNK // 2 + 1, :] for x in b], [x[CHUNK - 1:CHUNK, :] for x in b]
        e_q = [jnp.exp(b[c] - mid[c]) for c in every]
        e_k = [jnp.exp(mid[c] - b[c]) for c in every]
        e_i = [jnp.exp(x) for x in b]
        e_s = [jnp.exp(last[c] - b[c]) for c in every]
        dec = [jnp.exp(x) for x in last]
        qt, kt, qi, ks = ([a[c] * e[c] for c in every] for a, e in ((q, e_q), (k, e_k), (q, e_i), (k, e_s)))

        def masked(a, b_):
            return [[jnp.where(causal, _dot(a_h, b_h, NT), 0.0) for a_h, b_h in zip(_heads(a[c]), _heads(b_[c]))]
                    for c in every]

        def with_scores(s, other, dims):
            return [jnp.concatenate([_dot(s_h, o_h, dims) for s_h, o_h in zip(s[c], _heads(other[c]))], axis=1)
                    for c in every]

        def per_head(dims, a, b_):
            return [_per_head(lambda a_h, b_h: _dot(a_h, b_h, dims), a[c], b_[c]) for c in every]

        scores, dscores = masked(qt, kt), masked(do, v)
        dqt, dkt, dv_intra = with_scores(dscores, kt, NN), with_scores(dscores, qt, TN), with_scores(scores, do, TN)
        dqi, update = per_head(NN, do, st), per_head(TN, do, qi)

        dst = ds_scr[...]
        dsts = [None] * ncb
        for c in reversed(every):
            dsts[c] = dst
            dst = dec[c] * dst + update[c]
        ds_scr[...] = dst

        dv_state, dks = per_head(NT, ks, dsts), per_head(NN, v, dsts)
        ddec = [jnp.sum(dsts[c] * st[c], axis=0, keepdims=True) for c in every]
        dq = [dqt[c] * e_q[c] + dqi[c] * e_i[c] for c in every]
        dk = [dkt[c] * e_k[c] + dks[c] * e_s[c] for c in every]
        db = [q[c] * dq[c] - k[c] * dk[c] for c in every]
        db_last = [jnp.sum(dks[c] * ks[c], axis=0, keepdims=True) + ddec[c] * dec[c] for c in every]
        dg = [_dot_exact(anti, db[c]) + db_last[c] for c in every]
        df = [dg[c] / f[c] - dk[c] for c in every]
        dlb_scr[...] += sum(jnp.sum(df[c] * (1.0 - sig[c]), axis=0, keepdims=True) for c in every)
        dfp = [df[c] * (1.0 - lb) * sig[c] * (1.0 - sig[c]) for c in every]
        dv = [dv_intra[c] + dv_state[c] for c in every]
        for n, parts in enumerate((dq, dfp, dv)):
            dp_ref[n] = jnp.concatenate(parts, axis=0).astype(dp_ref.dtype)

        @pl.when(i == nblk - 1)
        def _():
            dlb = dlb_scr[...]
            dlbl_ref[0:1, :] = dlb * lb * (1.0 - lb)
            dlbl_ref[1:2, :] = -dlb * lb * s1

    grp = lambda g: pl.BlockSpec((None, tb, GROUP), lambda i: (g, nblk - 1 - i, 0))
    vec = pl.BlockSpec((2, HGRN_WIDTH), lambda i: (0, 0))
    return pl.pallas_call(
        body, name="hgrn_bwd", grid=(nblk,),
        in_specs=[grp(0), grp(1), grp(2), pl.BlockSpec((tb, HGRN_WIDTH), lambda i: (nblk - 1 - i, 0)),
                  pl.BlockSpec((ncb, HEAD_DIM, HGRN_WIDTH), lambda i: (nblk - 1 - i, 0, 0)), vec, ANY],
        out_specs=[pl.BlockSpec((3, tb, HGRN_WIDTH), lambda i: (0, nblk - 1 - i, 0)), vec],
        out_shape=[jax.ShapeDtypeStruct((3, t, HGRN_WIDTH), BF16), jax.ShapeDtypeStruct((2, HGRN_WIDTH), F32)],
        scratch_shapes=[pltpu.VMEM((HEAD_DIM, HGRN_WIDTH), F32), pltpu.VMEM((1, HGRN_WIDTH), F32)],
        compiler_params=_params("arbitrary"),
    )(proj, proj, proj, do, states, lb_logits, after)


GRAD_TILE = 512
OUT_PARTS = 4


class _Side:
    def __init__(self, operands, in_specs, out_shape, out_specs, scratch, init, begin):
        self.operands, self.in_specs, self.out_shape, self.out_specs = operands, in_specs, out_shape, out_specs
        self.scratch, self.init, self.begin = scratch, init, begin


RING = 3


def _grad_w(name, operands, widths, shape, step, after=None, side=None, ringed=None):
    t = operands[0].shape[-2]
    tt = min(t, GRAD_TILE)
    n_in, n_steps = len(operands), t // tt
    in_specs = [ANY if n == ringed else pl.BlockSpec((tt, w), lambda k: (k, 0)) if a.ndim == 2 else
                pl.BlockSpec((a.shape[0], tt, w), lambda k: (0, k, 0)) for n, (a, w) in enumerate(zip(operands, widths))]
    extra = [] if after is None else [after]
    s_in, s_out = (len(side.operands), len(side.out_shape)) if side else (0, 0)
    first_out = n_in + s_in + len(extra)
    side_scratch = side.scratch if side else []
    ring_scratch = [] if ringed is None else [pltpu.VMEM((RING, tt, widths[ringed]), operands[ringed].dtype),
                                              pltpu.SemaphoreType.DMA((RING,))]

    def body(*refs):
        o_ref, side_outs = refs[first_out], refs[first_out + 1:first_out + 1 + s_out]
        acc, narrow, sem = refs[first_out + 1 + s_out:first_out + 4 + s_out]
        first_side = first_out + 4 + s_out
        k = pl.program_id(0)
        tiles = list(refs[:n_in])
        if ringed is not None:
            ring, ring_sems = refs[first_side + len(side_scratch):]

            def fetch(tile, slot):
                return pltpu.make_async_copy(refs[ringed].at[pl.ds(tile * tt, tt), :], ring.at[slot], ring_sems.at[slot])

            @pl.when(k == 0)
            def _():
                for first in range(min(RING - 1, n_steps)):
                    fetch(first, first).start()

            @pl.when(k + RING - 1 < n_steps)
            def _():
                fetch(k + RING - 1, lax.rem(k + RING - 1, RING)).start()

        @pl.when(k == 0)
        def _():
            acc[...] = jnp.zeros_like(acc)
            if side:
                side.init(side_outs)

        if ringed is not None:
            slot = lax.rem(k, RING)
            fetch(k, slot).wait()
            tiles[ringed] = ring.at[slot]

        tick = (side.begin(k, n_steps, refs[n_in:n_in + s_in], side_outs, refs[first_side:first_side + len(side_scratch)])
                if side else None)
        step(acc, *tiles, tick or (lambda j: None))

        @pl.when(k == n_steps - 1)
        def _():
            part = shape[0] // OUT_PARTS
            copies = []
            for p in range(OUT_PARTS):
                rows = pl.ds(p * part, part)
                narrow[rows, :] = acc[rows, :].astype(narrow.dtype)
                copies.append(pltpu.make_async_copy(narrow.at[rows, :], o_ref.at[rows, :], sem.at[p]))
                copies[-1].start()
            for cp in copies:
                cp.wait()

    outs = pl.pallas_call(
        body, name=name, grid=(n_steps,),
        in_specs=in_specs + (side.in_specs if side else []) + [ANY] * len(extra),
        out_specs=[ANY] + (side.out_specs if side else []),
        out_shape=[jax.ShapeDtypeStruct(shape, BF16)] + (side.out_shape if side else []),
        scratch_shapes=[pltpu.VMEM(shape, F32), pltpu.VMEM(shape, BF16), pltpu.SemaphoreType.DMA((OUT_PARTS,))]
                       + side_scratch + ring_scratch,
        compiler_params=_params("arbitrary"),
    )(*operands, *(side.operands if side else ()), *extra)
    return outs if side else outs[0]


def _dw_in(xb, dph, dog, dpc, w_in, dpre1, after):
    t = xb.shape[0]

    def step(acc, x_ref, dh_ref, dog_ref, dc_ref, tick):
        xv = x_ref[...]
        for g in range(N_GROUPS):
            part = dh_ref[g] if g < 3 else dog_ref[...] if g == 3 else dc_ref[g - 4]
            acc[:, g * GROUP:(g + 1) * GROUP] += _dot(xv, part, TN)
            tick(g, part)

    def begin(k, n_steps, ins, outs, scratch):
        w_ref, dp_ref = ins
        total = [ALPHA * dp_ref[...]]

        def tick(g, part):
            total[0] = total[0] + _dot(part, w_ref[:, g * GROUP:(g + 1) * GROUP], NT)
            if g == N_GROUPS - 1:
                outs[0][...] = total[0]

        return tick

    row = pl.BlockSpec((min(t, GRAD_TILE), D_MODEL), lambda k: (k, 0))
    side = _Side((w_in, dpre1), [_resident((D_MODEL, IN_COLS)), row], [jax.ShapeDtypeStruct((t, D_MODEL), F32)], [row],
                 [], lambda outs: None, begin)
    return _grad_w("dw_in", (xb, dph, dog, dpc), (D_MODEL, GROUP, GROUP, GROUP), (D_MODEL, IN_COLS), step, after, side)


def _strips_of(j, tt):
    per_tick = tt // GATE_STRIP // N_FF
    return [slice(s * GATE_STRIP, (s + 1) * GATE_STRIP) for s in range(j * per_tick, (j + 1) * per_tick)]


def _dw_ff1(h1b, da, dcat, o, proj, gate_norm_w, after):
    t = h1b.shape[0]
    tt = min(t, GRAD_TILE)

    def step(acc, h_ref, da_ref, tick):
        hv = h_ref[...]
        for j in range(N_FF):
            cols = slice(j * FF_BLOCK, (j + 1) * FF_BLOCK)
            acc[:, cols] += _dot(hv, da_ref[:, cols], TN)
            tick(j)

    def init(outs):
        outs[2][...] = jnp.zeros_like(outs[2])

    def begin(k, n_steps, ins, outs, scratch):
        do2_ref, o_ref, og_ref, gnw_ref = ins
        do_ref, dog_ref, dgnw_ref = outs
        total = [jnp.zeros((GATE_STRIP, GROUP), F32)]

        def tick(j):
            gnw = gnw_ref[...]
            for rows in _strips_of(j, tt):
                ov, og, do2 = o_ref[rows, :], og_ref[rows, :], do2_ref[rows, :]
                rs = _per_head(lambda o_h: jnp.broadcast_to(
                    lax.rsqrt(jnp.mean(o_h * o_h, axis=-1, keepdims=True) + EPS), o_h.shape), ov)
                on = ov * rs
                sg = _sigmoid(og)
                sil = og * sg
                don = do2 * gnw * sil
                total[0] = total[0] + do2 * on * sil
                dog_ref[rows, :] = (do2 * on * gnw * (sg * (1.0 + og * (1.0 - sg)))).astype(dog_ref.dtype)
                do_ref[rows, :] = rs * (don - on * _per_head(
                    lambda p_h: jnp.broadcast_to(jnp.mean(p_h, axis=-1, keepdims=True), p_h.shape), don * on))
            if j == N_FF - 1:
                dgnw_ref[...] += jnp.sum(total[0], axis=0, keepdims=True)

        return tick

    tile = pl.BlockSpec((tt, GROUP), lambda k: (k, 0))
    vec = pl.BlockSpec((1, GROUP), lambda k: (0, 0))
    side = _Side(
        (dcat, o, proj, gate_norm_w), [tile, tile, pl.BlockSpec((None, tt, GROUP), lambda k: (3, k, 0)), vec],
        [jax.ShapeDtypeStruct((t, HGRN_WIDTH), F32), jax.ShapeDtypeStruct((t, HGRN_WIDTH), BF16),
         jax.ShapeDtypeStruct((1, HGRN_WIDTH), F32)], [tile, tile, vec], [], init, begin)
    return _grad_w("dw_ff1", (h1b, da), (D_MODEL, D_FF), (D_MODEL, D_FF), step, after, side, ringed=1)


def _dw_ff2(r, dpre2b, dcat, bcu, conv_w):
    t = r.shape[0]
    tt = min(t, GRAD_TILE)
    hb = tt // SUBLANES
    halo = 2 * SUBLANES

    def step(acc, r_ref, d_ref, tick):
        dv = d_ref[...]
        for j in range(N_FF):
            rows = slice(j * FF_BLOCK, (j + 1) * FF_BLOCK)
            acc[rows, :] += _dot(r_ref[:, rows], dv, TN)
            tick(j)

    def init(outs):
        outs[1][...] = jnp.zeros_like(outs[1])

    def begin(k, n_steps, ins, outs, scratch):
        dy_ref, dyn_ref, b_ref, bn_ref, c_ref, u_ref, ch_ref, uh_ref, cw_ref = ins
        dp_ref, dcw_ref = outs
        zbuf, dbuf = scratch
        before = lambda ref: ref[SUBLANES:halo, :].astype(F32)
        zbuf[0:SUBLANES, :] = jnp.where(k > 0, before(ch_ref) * before(uh_ref), 0.0)
        zbuf[SUBLANES:SUBLANES + tt, :] = c_ref[...].astype(F32) * u_ref[...].astype(F32)
        dbuf[0:tt, :] = dy_ref[...] * b_ref[...].astype(F32)
        dbuf[tt:tt + SUBLANES, :] = jnp.where(k < n_steps - 1, dyn_ref[...] * bn_ref[0:SUBLANES, :].astype(F32), 0.0)
        totals = [jnp.zeros((GATE_STRIP, GROUP), F32) for _ in range(3)]

        def tick(j):
            cw = cw_ref[...]
            for rows in _strips_of(j, tt):
                at = lambda buf, shift: buf[shift + rows.start:shift + rows.stop, :]
                z, z1, z2 = at(zbuf, SUBLANES), at(zbuf, SUBLANES - 1), at(zbuf, SUBLANES - 2)
                dyc, d1, d2 = at(dbuf, 0), at(dbuf, 1), at(dbuf, 2)
                yc = cw[2:3, :] * z + cw[1:2, :] * z1 + cw[0:1, :] * z2
                dz = cw[2:3, :] * dyc + cw[1:2, :] * d1 + cw[0:1, :] * d2
                dp_ref[0, rows, :] = (dy_ref[rows, :] * yc).astype(dp_ref.dtype)
                dp_ref[1, rows, :] = (dz * u_ref[rows, :].astype(F32)).astype(dp_ref.dtype)
                dp_ref[2, rows, :] = (dz * c_ref[rows, :].astype(F32)).astype(dp_ref.dtype)
                for n, tap in enumerate((z2, z1, z)):
                    totals[n] = totals[n] + dyc * tap
            if j == N_FF - 1:
                for n in range(3):
                    dcw_ref[n:n + 1, :] += jnp.sum(totals[n], axis=0, keepdims=True)

        return tick

    grp = lambda g: pl.BlockSpec((None, tt, GROUP), lambda k: (g, k, 0))
    prev = lambda g: pl.BlockSpec((None, halo, GROUP), lambda k: (g, jnp.maximum(k * (tt // halo) - 1, 0), 0))
    nxt = lambda g: pl.BlockSpec((None, halo, GROUP), lambda k: (g, jnp.minimum((k + 1) * (tt // halo), t // halo - 1), 0))
    nxt_row = lambda k: jnp.minimum((k + 1) * hb, t // SUBLANES - 1)
    whole = pl.BlockSpec((3, CONV_WIDTH), lambda k: (0, 0))
    side = _Side(
        (dcat, dcat, bcu, bcu, bcu, bcu, bcu, bcu, conv_w),
        [pl.BlockSpec((tt, GROUP), lambda k: (k, 1)), pl.BlockSpec((SUBLANES, GROUP), lambda k: (nxt_row(k), 1)),
         grp(0), nxt(0), grp(1), grp(2), prev(1), prev(2), whole],
        [jax.ShapeDtypeStruct((3, t, CONV_WIDTH), BF16), jax.ShapeDtypeStruct((3, CONV_WIDTH), F32)],
        [pl.BlockSpec((3, tt, GROUP), lambda k: (0, k, 0)), whole],
        [pltpu.VMEM((tt + SUBLANES, GROUP), F32), pltpu.VMEM((tt + SUBLANES, GROUP), F32)], init, begin)
    return _grad_w("dw_ff2", (r, dpre2b), (D_FF, D_MODEL), (D_FF, D_MODEL), step, side=side, ringed=0)


def _place():
    x, y, c = lax.axis_index("x"), lax.axis_index("y"), lax.axis_index("c")
    return x, y, c, 2 * x + y


def _other_chips(x, y):
    return [(1 - x, y), (x, 1 - y), (1 - x, 1 - y)]


def _place_shard(name, w, chip, cols_sharded, after=None):
    rows, cols = w.shape
    tr = min(rows, 256)
    nb = rows // tr
    full = (rows, cols * N_CHIPS) if cols_sharded else (rows * N_CHIPS, cols)
    out_map = (lambda i, s: (i, s[0])) if cols_sharded else (lambda i, s: (s[0] * nb + i, 0))

    def body(s_ref, w_ref, *rest):
        rest[-1][...] = w_ref[...].astype(rest[-1].dtype)

    extra = [] if after is None else [after]
    return pl.pallas_call(
        body, name=name,
        grid_spec=pltpu.PrefetchScalarGridSpec(
            num_scalar_prefetch=1, grid=(nb,),
            in_specs=[pl.BlockSpec((tr, cols), lambda i, s: (i, 0))] + [ANY] * len(extra),
            out_specs=pl.BlockSpec((tr, cols), out_map)),
        out_shape=jax.ShapeDtypeStruct(full, BF16),
        compiler_params=_params("parallel"),
    )(chip, w, *extra)


HBM = pl.BlockSpec(memory_space=pltpu.HBM)
SEM = pl.BlockSpec(memory_space=pltpu.SEMAPHORE)
EFFECT = pltpu.SideEffectType.DATAFLOW_SIDE_EFFECTING


PEER_SETS = {
    "sibling": (0, lambda x, y, c: [(x, y, 1 - c)]),
    "chips": (1, lambda x, y, c: [(1 - x, y, c), (x, 1 - y, c), (1 - x, 1 - y, c)]),
    "neighbours": (2, lambda x, y, c: [(1 - x, y, c), (x, 1 - y, c)]),
}


class _Split:
    def __init__(self, name, arrays, plan, others=(), peers=None, prepare=None, sources=(), scratch=()):
        n_own, arrays = len(arrays), (*arrays, *others)
        n, n_copies, n_in = len(arrays), plan.count, len(arrays) + len(sources)
        self.name, self.plan, self.n = name, plan, n_own
        barrier_id, peer_ids = PEER_SETS[peers] if peers else (None, None)

        def body(*refs):
            send_sems, recv_sems, token = refs[n_in], refs[n_in + 1], refs[n_in + 2 + n]
            if peers:
                x, y, c, _ = _place()
                barrier = pltpu.get_barrier_semaphore()
                for peer in peer_ids(x, y, c):
                    pl.semaphore_signal(barrier, inc=1, device_id=peer, device_id_type=MESH)
            if prepare:
                prepare(refs[:n], refs[n:n_in], refs[n_in + 3 + n:])
            if peers:
                pl.semaphore_wait(barrier, len(peer_ids(0, 0, 0)))
            for k, (src, dst, to) in enumerate(plan(refs[:n])):
                pltpu.make_async_remote_copy(src_ref=src, dst_ref=dst, send_sem=send_sems.at[k], recv_sem=recv_sems.at[k],
                                             device_id=to, device_id_type=MESH).start()
            token[...] = jnp.zeros_like(token)

        outs = pl.pallas_call(
            body, name=name + "_start",
            out_shape=(pltpu.SemaphoreType.DMA((n_copies,)), pltpu.SemaphoreType.DMA((n_copies,)),
                       *[pltpu.HBM(a.shape, a.dtype) for a in arrays], jax.ShapeDtypeStruct((SUBLANES, LANES), F32)),
            in_specs=(HBM,) * n_in, out_specs=(SEM, SEM) + (HBM,) * n + (pl.BlockSpec(memory_space=pltpu.VMEM),),
            input_output_aliases={i: 2 + i for i in range(n)}, scratch_shapes=list(scratch),
            compiler_params=pltpu.CompilerParams(has_side_effects=EFFECT, collective_id=barrier_id),
        )(*[pltpu.with_memory_space_constraint(a, pltpu.HBM) for a in (*arrays, *sources)])
        self.sems, self.arrays, self.others, self.token = outs[:2], outs[2:2 + n_own], outs[2 + n_own:2 + n], outs[-1]
        self.waited = set()

    def wait(self, after, copies=None, part=""):
        n, plan = self.n, self.plan
        mine = set(range(plan.count) if copies is None else copies) - self.waited
        self.waited |= mine

        def body(*refs):
            send_sems, recv_sems = refs[n], refs[n + 1]
            for k, (src, dst, to) in enumerate(plan(refs[:n])):
                if k in mine:
                    cp = pltpu.make_async_remote_copy(src_ref=src, dst_ref=dst, send_sem=send_sems.at[k],
                                                      recv_sem=recv_sems.at[k], device_id=to, device_id_type=MESH)
                    cp.wait_send()
                    cp.wait_recv()

        self.arrays = pl.pallas_call(
            body, name=self.name + "_wait" + part, out_shape=tuple(pltpu.HBM(a.shape, a.dtype) for a in self.arrays),
            in_specs=(HBM,) * n + (SEM, SEM, ANY), out_specs=(HBM,) * n, input_output_aliases={i: i for i in range(n)},
            compiler_params=pltpu.CompilerParams(has_side_effects=EFFECT),
        )(*self.arrays, *self.sems, after)
        return self.arrays


COLS_SHARDED = (True, False, True, False)
HALF_SHAPES = [(D_MODEL // 2, IN_COLS), (D_MODEL, D_MODEL // 2), (D_MODEL // 2, D_FF), (D_FF, D_MODEL // 2)]
PIECE_SHAPES = [(D_MODEL // 2, IN_COLS // N_CHIPS), (D_MODEL // N_CHIPS, D_MODEL // 2),
                (D_MODEL // 2, D_FF // N_CHIPS), (D_FF // N_CHIPS, D_MODEL // 2)]


def _shard_view(kind, ref, chip):
    if COLS_SHARDED[kind]:
        n = ref.shape[1] // N_CHIPS
        return ref.at[:, pl.ds(chip * n, n)]
    n = ref.shape[0] // N_CHIPS
    return ref.at[pl.ds(chip * n, n), :]


def _half_view(kind, ref, h):
    if COLS_SHARDED[kind]:
        n = ref.shape[0] // 2
        return ref.at[pl.ds(h * n, n), :]
    n = ref.shape[1] // 2
    return ref.at[:, pl.ds(h * n, n)]


def _plan(count):
    def mark(fn):
        fn.count = count
        return fn
    return mark


def _shard_rows_view(kind, ref, chip, part, n_parts):
    if COLS_SHARDED[kind]:
        m, n = ref.shape[0] // n_parts, ref.shape[1] // N_CHIPS
        return ref.at[pl.ds(part * m, m), pl.ds(chip * n, n)]
    m = ref.shape[0] // N_CHIPS // n_parts
    return ref.at[pl.ds((n_parts * chip + part) * m, m), :]


def _shard_half_view(kind, ref, chip, h):
    return _shard_rows_view(kind, ref, chip, h, 2)


def _gather_over_ici(kinds, weights):
    @_plan(2 * len(kinds))
    def plan(refs):
        x, y, c, me = _place()
        mine = [_shard_half_view(kind, ref, me, c) for kind, ref in zip(kinds, refs)]
        return [(v, v, to) for v in mine for to in ((1 - x, y, c), (x, 1 - y, c))]

    return _Split("gather_ici_" + "".join(map(str, kinds)), tuple(weights), plan, peers="neighbours")


def _relay_over_ici(kinds, weights, others=()):
    @_plan(2 * len(kinds))
    def plan(refs):
        x, y, c, _ = _place()
        x_nbr, y_nbr = 2 * (1 - x) + y, 2 * x + (1 - y)
        out = []
        for kind, ref in zip(kinds, refs):
            first, second = (_shard_rows_view(kind, ref, chip, 2 * c + q, 4) for q, chip in ((0, x_nbr), (1, y_nbr)))
            out += [(first, first, (x, 1 - y, c)), (second, second, (1 - x, y, c))]
        return out

    return _Split("relay_ici_" + "".join(map(str, kinds)), tuple(weights), plan, others, peers="neighbours")


def _gather_w_in_over_ici(shard, conv4):
    rows, cols = shard.shape

    @_plan(6)
    def plan(refs):
        x, y, c, me = _place()
        half, conv = _shard_half_view(0, refs[0], me, c), refs[1].at[me]
        return [(v, v, (px, py, c)) for v in (half, conv) for px, py in _other_chips(x, y)]

    def prepare(refs, sources, scratch):
        wide, narrow, sems = scratch
        _, _, _, me = _place()
        load = pltpu.make_async_copy(sources[0], wide, sems.at[0])
        load.start()
        load.wait()
        narrow[...] = wide[...].astype(narrow.dtype)
        store = pltpu.make_async_copy(narrow, _shard_view(0, refs[0], me), sems.at[1])
        store.start()
        store.wait()

    return _Split("gather_w_in_ici", (lax.empty((rows, cols * N_CHIPS), BF16), conv4), plan, peers="chips",
                  prepare=prepare, sources=(shard,),
                  scratch=(pltpu.VMEM((rows, cols), F32), pltpu.VMEM((rows, cols), BF16), pltpu.SemaphoreType.DMA((2,))))


def _gather_over_d2d(kinds, weights, relations=(0, 1, 2), part=""):
    @_plan(len(relations) * len(kinds))
    def plan(refs):
        x, y, c, _ = _place()
        chips = [_other_chips(x, y)[n] for n in relations]
        got = [_shard_half_view(kind, ref, 2 * px + py, c) for kind, ref in zip(kinds, refs) for px, py in chips]
        return [(v, v, (x, y, 1 - c)) for v in got]

    return _Split("gather_d2d_" + "".join(map(str, kinds)) + part, tuple(weights), plan, peers="sibling")


def _swap_halves(kinds, grads):
    @_plan(len(kinds))
    def plan(refs):
        x, y, c, _ = _place()
        return [(_half_view(kind, g, 1 - c), land, (x, y, 1 - c))
                for kind, g, land in zip(kinds, refs[:len(kinds)], refs[len(kinds):])]

    lands = [lax.empty(HALF_SHAPES[kind], g.dtype) for kind, g in zip(kinds, grads)]
    return _Split("swap_halves_" + "".join(map(str, kinds)), (*grads, *lands), plan, peers="sibling")


def _block_rows(cols, elements):
    return 1 << ((elements // cols).bit_length() - 1)


def _grid_steps(shapes, elements):
    rows, cols = max(shapes, key=lambda shape: shape[0] * shape[1])
    return rows // min(rows, _block_rows(cols, elements))


def _add_half(name, kinds, grads, recvs, core):
    n = len(kinds)
    shapes = [recv.shape for recv in recvs]
    nb = _grid_steps(shapes, 1 << 20)

    def body(c_ref, *refs):
        for g_ref, r_ref, o_ref in zip(refs[:n], refs[n:2 * n], refs[2 * n:]):
            o_ref[...] = (g_ref[...].astype(F32) + r_ref[...].astype(F32)).astype(o_ref.dtype)

    own = [pl.BlockSpec((rows // nb, cols), (lambda i, c_ref: (c_ref[0] * nb + i, 0)) if COLS_SHARDED[k] else
                        (lambda i, c_ref: (i, c_ref[0]))) for k, (rows, cols) in zip(kinds, shapes)]
    blocks = [pl.BlockSpec((rows // nb, cols), lambda i, c_ref: (i, 0)) for rows, cols in shapes]
    return pl.pallas_call(
        body, name=name,
        grid_spec=pltpu.PrefetchScalarGridSpec(
            num_scalar_prefetch=1, grid=(nb,), in_specs=own + blocks, out_specs=blocks),
        out_shape=[jax.ShapeDtypeStruct(shape, BF16) for shape in shapes],
        compiler_params=_params("parallel"),
    )(core, *grads, *recvs)


def _exchange_pieces(kinds, halves, pack=None):
    n_p, n = N_CHIPS - 1, len(kinds)

    @_plan(n_p * n + (0 if pack is None else N_DEV - 1))
    def plan(refs):
        x, y, c, _ = _place()
        copies = []
        if pack is not None:
            me = 4 * x + 2 * y + c
            peers = [((1 - x) if m & 4 else x, (1 - y) if m & 2 else y, (1 - c) if m & 1 else c) for m in range(1, N_DEV)]
            copies += [(refs[2 * n], refs[2 * n + 1].at[me], peer) for peer in peers]
        return copies + [(_shard_view(kind, half, 2 * px + py), land.at[j], (px, py, c))
                         for j, (px, py) in enumerate(_other_chips(x, y))
                         for kind, half, land in zip(kinds, refs[:n], refs[n:2 * n])]

    lands = [lax.empty((n_p,) + PIECE_SHAPES[kind], BF16) for kind in kinds]
    small = () if pack is None else (pack, lax.empty((N_DEV,) + pack.shape, F32))
    return _Split("exchange_pieces_" + "".join(map(str, kinds)), (*halves, *lands, *small), plan,
                  peers="chips" if pack is None else None)


def _sum_pieces(name, kinds, halves, slots, place, after):
    n, n_p = len(kinds), N_CHIPS - 1
    shapes = [slot.shape[1:] for slot in slots]
    nb = _grid_steps(shapes, 1 << 18)

    def body(s_ref, *refs):
        for own_ref, slot_ref, o_ref in zip(refs[:n], refs[n:2 * n], refs[2 * n + 1:]):
            total = own_ref[...].astype(F32)
            for j in range(n_p):
                total = total + slot_ref[j].astype(F32)
            o_ref[...] = total

    own, out, shards = [], [], []
    for k, (rows, cols) in zip(kinds, shapes):
        if COLS_SHARDED[k]:
            own_map, out_map, shard = (lambda i, s: (i, s[0])), (lambda i, s: (s[1] * nb + i, 0)), (2 * rows, cols)
        else:
            own_map, out_map, shard = (lambda i, s: (s[0] * nb + i, 0)), (lambda i, s: (i, s[1])), (rows, 2 * cols)
        own.append(pl.BlockSpec((rows // nb, cols), own_map))
        out.append(pl.BlockSpec((rows // nb, cols), out_map))
        shards.append(jax.ShapeDtypeStruct(shard, F32))
    landed = [pl.BlockSpec((n_p, rows // nb, cols), lambda i, s: (0, i, 0)) for rows, cols in shapes]
    return pl.pallas_call(
        body, name=name,
        grid_spec=pltpu.PrefetchScalarGridSpec(
            num_scalar_prefetch=1, grid=(nb,), in_specs=own + landed + [ANY], out_specs=out),
        out_shape=shards,
        compiler_params=_params("parallel"),
    )(place, *halves, *slots, after)


def _join_halves(kinds, shards):
    @_plan(len(kinds))
    def plan(refs):
        x, y, c, _ = _place()
        return [(_half_view(kind, g, c), _half_view(kind, g, c), (x, y, 1 - c)) for kind, g in zip(kinds, refs)]

    return _Split("join_halves_" + "".join(map(str, kinds)), tuple(shards), plan, peers="sibling")


N_DEV = 8


def _sum_shared(pack, land, device):
    def body(d_ref, p_ref, l_ref, o_ref):
        me = d_ref[0]
        total = jnp.where(me == 0, p_ref[...], l_ref[0])
        for d in range(1, N_DEV):
            total = total + jnp.where(me == d, p_ref[...], l_ref[d])
        o_ref[...] = total

    return pl.pallas_call(
        body, name="sum_shared",
        grid_spec=pltpu.PrefetchScalarGridSpec(
            num_scalar_prefetch=1, grid=(1,),
            in_specs=[pl.BlockSpec(pack.shape, lambda i, d: (0, 0)), pl.BlockSpec(land.shape, lambda i, d: (0, 0, 0))],
            out_specs=pl.BlockSpec(pack.shape, lambda i, d: (0, 0))),
        out_shape=jax.ShapeDtypeStruct(pack.shape, F32),
    )(device, pack, land)


def _adamw(name, weights, after=None):
    shapes = [w.shape for w, _, _, _ in weights]
    nb = _grid_steps(shapes, 1 << 18)
    extra = [] if after is None else [after]
    n_in = 4 * len(weights) + len(extra)

    def body(*refs):
        for k in range(len(weights)):
            w_ref, g_ref, m_ref, v_ref = refs[4 * k:4 * k + 4]
            go_ref, d_ref, nm_ref, nv_ref = refs[n_in + 4 * k:n_in + 4 * k + 4]
            g = g_ref[...]
            go_ref[...] = g
            d_ref[...], nm_ref[...], nv_ref[...] = _adam_step(w_ref[...], g, m_ref[...], v_ref[...])

    blocks = [pl.BlockSpec((rows // nb, cols), lambda i: (i, 0)) for rows, cols in shapes for _ in range(4)]
    outs = pl.pallas_call(
        body, name=name, grid=(nb,), in_specs=blocks + [ANY] * len(extra), out_specs=blocks,
        out_shape=[jax.ShapeDtypeStruct(shape, F32) for shape in shapes for _ in range(4)],
        compiler_params=_params("parallel"),
    )(*[a for group in weights for a in group], *extra)
    return [outs[4 * k:4 * k + 4] for k in range(len(weights))]


def _adam_step(w, g, m, v):
    nm = ADAM_B1 * m + (1.0 - ADAM_B1) * g
    nv = ADAM_B2 * v + (1.0 - ADAM_B2) * jnp.square(g)
    m_hat = nm * (1.0 / (1.0 - ADAM_B1 ** ADAM_STEP))
    v_hat = nv * (1.0 / (1.0 - ADAM_B2 ** ADAM_STEP))
    return -ADAM_LR * (m_hat / (jnp.sqrt(v_hat) + ADAM_EPS) + ADAM_WD * w), nm, nv


def _adamw_small(tot, chip, weights, ms, vs, after):
    n, half = len(weights), D_MODEL // 2

    def body(chip_ref, tot_ref, *refs):
        ins, outs = refs[:3 * n], refs[3 * n + 1:]
        tot = tot_ref[...]
        conv_all = jnp.concatenate([tot[5:6, half:], tot[6:7, :half], tot[6:7, half:]], axis=0)
        conv = sum(jnp.where(chip_ref[0] == s, conv_all[:, s * LANES:(s + 1) * LANES], 0.0) for s in range(N_CHIPS))
        grads = [jnp.concatenate([tot[4:5, :half], tot[4:5, half:]], axis=0), tot[5:6, :half], conv,
                 tot[0:1], tot[1:2], tot[2:3], tot[3:4]]
        for k, g in enumerate(grads):
            delta, nm, nv = _adam_step(ins[k][...], g, ins[n + k][...], ins[2 * n + k][...])
            outs[k][...], outs[n + k][...], outs[2 * n + k][...], outs[3 * n + k][...] = g, delta, nm, nv
        outs[4 * n][...] = tot[7:8, 0:1]

    whole = lambda a: pl.BlockSpec(a.shape, lambda i, s: (0,) * a.ndim)
    arrays = (*weights, *ms, *vs)
    loss = jax.ShapeDtypeStruct((1, 1), F32)
    return pl.pallas_call(
        body, name="adamw_small",
        grid_spec=pltpu.PrefetchScalarGridSpec(
            num_scalar_prefetch=1, grid=(1,), in_specs=[whole(tot)] + [whole(a) for a in arrays] + [ANY],
            out_specs=[whole(a) for a in weights] * 4 + [whole(loss)]),
        out_shape=[jax.ShapeDtypeStruct(a.shape, F32) for a in weights] * 4 + [loss],
    )(chip, tot, *arrays, after)


def kernel(x, w_in, lb_logits, gate_norm_w, conv_w, w_out, ln1_g, ln1_b, w_ff1, w_ff2, ln2_g, ln2_b, loss_target, m_w_in, m_lb_logits, m_gate_norm_w, m_conv_w, m_w_out, m_ln1_g, m_ln1_b, m_w_ff1, m_w_ff2, m_ln2_g, m_ln2_b, v_w_in, v_lb_logits, v_gate_norm_w, v_conv_w, v_w_out, v_ln1_g, v_ln1_b, v_w_ff1, v_w_ff2, v_ln2_g, v_ln2_b):
    xs, tgt = x[0], loss_target[0]
    chip = 2 * lax.axis_index("x") + lax.axis_index("y")
    core = lax.axis_index("c").astype(jnp.int32).reshape(1)
    chip1 = chip.astype(jnp.int32).reshape(1)
    place = jnp.concatenate([chip1, core])

    conv4 = lax.dynamic_update_slice(jnp.zeros((N_CHIPS,) + conv_w.shape[1:], F32), conv_w, (chip, 0, 0))
    ici_in = _gather_w_in_over_ici(w_in[0], conv4)
    rest = (1, 2, 3)
    ici_rest = _gather_over_ici(rest, (_place_shard("place_w_out", w_out[0], chip1, False, after=ici_in.token),
                                       _place_shard("place_w_ff1", w_ff1[0], chip1, True, after=ici_in.token),
                                       _place_shard("place_w_ff2", w_ff2[0], chip1, False, after=ici_in.token)))
    near = ici_in.wait(ici_rest.token, (0, 1, 3, 4, 5), "_near")
    d2d_near = _gather_over_d2d((0,), near[:1], (0, 1), "_near")
    ici_in.arrays = (*d2d_near.arrays, near[1])
    far = ici_in.wait(d2d_near.token)
    d2d_far = _gather_over_d2d((0,), far[:1], (2,), "_far")
    d2d_near.arrays = d2d_far.arrays
    d2d_far.arrays = d2d_near.wait(d2d_far.token)
    wb_in, = d2d_far.wait(d2d_far.token)
    cv4 = far[1]
    conv_full = cv4.transpose(1, 0, 2).reshape(3, CONV_WIDTH)

    proj, bcu, xb, cat_c = _in_proj(xs, wb_in, conv_full, ici_rest.token)
    relay_rest = _relay_over_ici(rest, ici_rest.wait(proj))
    o, states = _hgrn_fwd(proj, lb_logits, relay_rest.token)
    d2d_rest = _gather_over_d2d(rest, relay_rest.wait(o))
    cat_h = _gate_fwd(proj, o, gate_norm_w, d2d_rest.token)
    wb_out, wb_ff1, wb_ff2 = d2d_rest.wait(cat_h)

    (h1b, r, da, dpre2b, dpre1, dcat, g_ln1_g, g_ln1_b, g_ln2_g, g_ln2_b, loss8, g_out_local) = _sublayers(
        cat_h, cat_c, xs, tgt, wb_out, wb_ff1, wb_ff2, ln1_g, ln1_b, ln2_g, ln2_b)

    names = ("w_in", "w_out", "w_ff1", "w_ff2")

    def named(prefix, kinds):
        return prefix + "".join("_" + names[k] for k in kinds)

    def add_halves(kinds, grads, lands):
        return _add_half(named("add_half", kinds), kinds, grads, lands, core)

    def sum_pieces(kinds, halves, lands, after):
        return _sum_pieces(named("sum_pieces", kinds), kinds, halves, lands, place, after)

    early = (1, 2, 3)
    g_ff2_local, dpc, g_conv = _dw_ff2(r, dpre2b, dcat, bcu, conv_full)
    swap_a = _swap_halves((1, 3), (g_out_local, g_ff2_local))
    g_ff1_local, do, dog, g_gnw = _dw_ff1(h1b, da, dcat, o, proj, gate_norm_w, swap_a.token)
    swap_b = _swap_halves((2,), (g_ff1_local,))
    swapped_a = swap_a.wait(swap_b.token)
    halves_a = add_halves((1, 3), swapped_a[:2], swapped_a[2:])
    swapped_b = swap_b.wait(halves_a[1])
    halves = (halves_a[0], *add_halves((2,), swapped_b[:1], swapped_b[1:]), halves_a[1])
    exch = _exchange_pieces(early, halves)
    dph, g_lbl = _hgrn_bwd(proj, do, states, lb_logits, exch.token)
    g_in_local, grad_x = _dw_in(xb, dph, dog, dpc, wb_in, dpre1, dph)

    late = (0,)
    swap = _swap_halves(late, (g_in_local,))
    exchanged = exch.wait(swap.token)
    pack = jnp.concatenate([
        g_ln1_g, g_ln1_b, g_ln2_g, g_ln2_b,
        jnp.concatenate([g_lbl[0:1], g_lbl[1:2]], axis=1),
        jnp.concatenate([g_gnw, g_conv[0:1]], axis=1),
        jnp.concatenate([g_conv[1:2], g_conv[2:3]], axis=1),
        jnp.concatenate([loss8[0:1], jnp.zeros((1, D_MODEL - LANES), F32)], axis=1)], axis=0)
    join_a = _join_halves((2,), sum_pieces((2,), exchanged[1:2], exchanged[4:5], swap.token))
    swapped = swap.wait(join_a.token)
    exch = _exchange_pieces(late, add_halves(late, swapped[:1], swapped[1:]), pack)
    join_b = _join_halves((1, 3), sum_pieces((1, 3), exchanged[0:3:2], exchanged[3:6:2], exch.token))
    g_w_ff1, = join_a.wait(join_b.token)
    (g_w_ff1, d_ff1, nm_ff1, nv_ff1), = _adamw("adamw_w_ff1", [(w_ff1[0], g_w_ff1, m_w_ff1[0], v_w_ff1[0])])
    g_w_out, g_w_ff2 = join_b.wait(d_ff1)
    (g_w_ff2, d_ff2, nm_ff2, nv_ff2), (g_w_out, d_out, nm_out, nv_out) = _adamw(
        "adamw_w_ff2_w_out", [(w_ff2[0], g_w_ff2, m_w_ff2[0], v_w_ff2[0]), (w_out[0], g_w_out, m_w_out[0], v_w_out[0])])
    shared = exch.wait(d_out, range(N_DEV - 1), "_pack")
    tot = _sum_shared(shared[2], shared[3], 2 * chip1 + core)
    exchanged = exch.wait(tot)
    join = _join_halves(late, sum_pieces(late, exchanged[:1], exchanged[1:2], tot))
    small = ("lb_logits", "gate_norm_w", "conv_w", "ln1_g", "ln1_b", "ln2_g", "ln2_b")
    small_out = _adamw_small(
        tot, chip1, (lb_logits, gate_norm_w, conv_w[0], ln1_g, ln1_b, ln2_g, ln2_b),
        (m_lb_logits, m_gate_norm_w, m_conv_w[0], m_ln1_g, m_ln1_b, m_ln2_g, m_ln2_b),
        (v_lb_logits, v_gate_norm_w, v_conv_w[0], v_ln1_g, v_ln1_b, v_ln2_g, v_ln2_b), join.token)
    g_w_in, = join.wait(small_out[0])
    (g_w_in, d_in, nm_in, nv_in), = _adamw("adamw_w_in", [(w_in[0], g_w_in, m_w_in[0], v_w_in[0])])
    loss = small_out[4 * len(small)][0, 0]

    def results(n_kind, large):
        out = dict(zip(small, small_out[n_kind * len(small):(n_kind + 1) * len(small)]))
        out["conv_w"] = out["conv_w"][None]
        out.update({name: a[None] for name, a in zip(("w_in", "w_out", "w_ff1", "w_ff2"), large)})
        return [out[name] for name in ("w_in", "lb_logits", "gate_norm_w", "conv_w", "w_out", "ln1_g", "ln1_b",
                                       "w_ff1", "w_ff2", "ln2_g", "ln2_b")]

    return (loss, grad_x[None], *results(0, (g_w_in, g_w_out, g_w_ff1, g_w_ff2)),
            *results(1, (d_in, d_out, d_ff1, d_ff2)), *results(2, (nm_in, nm_out, nm_ff1, nm_ff2)),
            *results(3, (nv_in, nv_out, nv_ff1, nv_ff2)))
```

```python
import jax
import jax.numpy as jnp
from jax import lax
from jax.experimental import pallas as pl
from jax.experimental.pallas import tpu as pltpu

F32 = jnp.float32
BF16 = jnp.bfloat16
MXU_DTYPE = jnp.bfloat16

D_MODEL = 1024
HGRN_WIDTH = 512
HEAD_DIM = 128
N_HEADS = 4
CONV_WIDTH = 512
CHUNK = 64
D_FF = 4096
IN_COLS = 3584
GROUP = 512
N_GROUPS = IN_COLS // GROUP
ALPHA = 2.0 ** 0.25
EPS = 1e-5
N_CHIPS = 4
ADAM_LR, ADAM_B1, ADAM_B2, ADAM_EPS, ADAM_WD, ADAM_STEP = 0.001, 0.9, 0.999, 1e-08, 0.01, 10

LANES = 128
SUBLANES = 8
VMEM_LIMIT = 56 * 1024 * 1024
FF_BLOCK = 1024
N_FF = D_FF // FF_BLOCK
GATE_STRIP = 64

NN = (((1,), (0,)), ((), ()))
NT = (((1,), (1,)), ((), ()))
TN = (((0,), (0,)), ((), ()))
MESH = pl.DeviceIdType.MESH
ANY = pl.BlockSpec(memory_space=pl.ANY)


def _dot(a, b, dims):
    return lax.dot_general(a.astype(MXU_DTYPE), b.astype(MXU_DTYPE), dims, preferred_element_type=F32)


def _dot_exact(ones, v):
    ones = ones.astype(jnp.bfloat16)
    hi = v.astype(jnp.bfloat16)
    rest = v - hi.astype(F32)
    mid = rest.astype(jnp.bfloat16)
    low = (rest - mid.astype(F32)).astype(jnp.bfloat16)
    return sum(lax.dot_general(ones, part, NN, preferred_element_type=F32) for part in (hi, mid, low))


def _params(*sem):
    return pltpu.CompilerParams(dimension_semantics=sem, vmem_limit_bytes=VMEM_LIMIT)


def _resident(shape):
    return pl.BlockSpec(shape, lambda *_: (0,) * len(shape), pipeline_mode=pl.Buffered(1))


def _sigmoid(v):
    return 1.0 / (1.0 + jnp.exp(-v))


def _lower_bound(lbl):
    m = jnp.max(lbl, axis=0, keepdims=True)
    e = jnp.exp(lbl - m)
    s = e / jnp.sum(e, axis=0, keepdims=True)
    return s[0:1, :], s[1:2, :]


def _heads(v):
    return [v[:, h * HEAD_DIM:(h + 1) * HEAD_DIM] for h in range(N_HEADS)]


def _per_head(fn, *arrays):
    return jnp.concatenate([fn(*parts) for parts in zip(*map(_heads, arrays))], axis=1)


def _in_proj(x, w_in, conv_w, after):
    t = x.shape[0]
    tm = min(t, 512)

    def body(x_ref, w_ref, cw_ref, after_ref, o_ref, bcu_ref, xb_ref, y_ref, zbuf):
        @pl.when(pl.program_id(0) == 0)
        def _():
            zbuf[tm:tm + SUBLANES, :] = jnp.zeros((SUBLANES, CONV_WIDTH), F32)

        xb = x_ref[...].astype(xb_ref.dtype)
        xb_ref[...] = xb
        group = lambda g: _dot(xb, w_ref[:, g * GROUP:(g + 1) * GROUP], NN)
        for g in range(4):
            o_ref[g] = group(g)
        b_gate, c_gate, u = group(4), group(5), group(6)
        for n, part in enumerate((b_gate, c_gate, u)):
            bcu_ref[n] = part.astype(bcu_ref.dtype)
        zbuf[0:SUBLANES, :] = zbuf[tm:tm + SUBLANES, :]
        zbuf[SUBLANES:SUBLANES + tm, :] = c_gate * u
        cw = cw_ref[...]
        at = lambda shift: zbuf[shift:shift + tm, :]
        conv = cw[2:3, :] * at(SUBLANES) + cw[1:2, :] * at(SUBLANES - 1) + cw[0:1, :] * at(SUBLANES - 2)
        y_ref[...] = (b_gate * conv).astype(y_ref.dtype)

    return pl.pallas_call(
        body, name="in_proj", grid=(t // tm,),
        in_specs=[pl.BlockSpec((tm, D_MODEL), lambda i: (i, 0)), _resident((D_MODEL, IN_COLS)),
                  pl.BlockSpec((3, CONV_WIDTH), lambda i: (0, 0)), ANY],
        out_specs=[pl.BlockSpec((4, tm, GROUP), lambda i: (0, i, 0)), pl.BlockSpec((3, tm, GROUP), lambda i: (0, i, 0)),
                   pl.BlockSpec((tm, D_MODEL), lambda i: (i, 0)), pl.BlockSpec((tm, CONV_WIDTH), lambda i: (i, 0))],
        out_shape=[jax.ShapeDtypeStruct((4, t, GROUP), F32), jax.ShapeDtypeStruct((3, t, GROUP), BF16),
                   jax.ShapeDtypeStruct((t, D_MODEL), BF16), jax.ShapeDtypeStruct((t, CONV_WIDTH), BF16)],
        scratch_shapes=[pltpu.VMEM((tm + SUBLANES, CONV_WIDTH), F32)],
        compiler_params=_params("arbitrary"),
    )(x, w_in, conv_w, after)


def _gates(fp, lb):
    sig = _sigmoid(fp)
    f = lb + (1.0 - lb) * sig
    return sig, f, jnp.log(f), 1.0 - f


def _chunk_masks():
    row = lax.broadcasted_iota(jnp.int32, (CHUNK, CHUNK), 0)
    col = lax.broadcasted_iota(jnp.int32, (CHUNK, CHUNK), 1)
    return row >= col, row <= col


def _hgrn_fwd(proj, lb_logits, after):
    t = proj.shape[1]
    tb = min(t, 512)
    ncb = tb // CHUNK

    def body(q_ref, f_ref, v_ref, lbl_ref, after_ref, o_ref, st_ref, s_scr):
        @pl.when(pl.program_id(0) == 0)
        def _():
            s_scr[...] = jnp.zeros_like(s_scr)

        lb, _ = _lower_bound(lbl_ref[...])
        causal, _ = _chunk_masks()

        every = range(ncb)
        rows = [slice(c * CHUNK, (c + 1) * CHUNK) for c in every]
        q, v = [q_ref[r, :] for r in rows], [v_ref[r, :] for r in rows]
        gates = [_gates(f_ref[r, :], lb) for r in rows]
        k = [gt[3] for gt in gates]
        b = [_dot_exact(causal, gt[2]) for gt in gates]
        mid, last = [x[CHUNK // 2:CHUNK // 2 + 1, :] for x in b], [x[CHUNK - 1:CHUNK, :] for x in b]
        qt = [q[c] * jnp.exp(b[c] - mid[c]) for c in every]
        kt = [k[c] * jnp.exp(mid[c] - b[c]) for c in every]
        qi = [q[c] * jnp.exp(b[c]) for c in every]
        ks = [k[c] * jnp.exp(last[c] - b[c]) for c in every]
        dec = [jnp.exp(x) for x in last]
        scores = [[jnp.where(causal, _dot(a, b_, NT), 0.0) for a, b_ in zip(_heads(qt[c]), _heads(kt[c]))] for c in every]
        intra = [[_dot(s, v_h, NN) for s, v_h in zip(scores[c], _heads(v[c]))] for c in every]
        update = [_per_head(lambda v_h, ks_h: _dot(v_h, ks_h, TN), v[c], ks[c]) for c in every]

        st = s_scr[...]
        states = []
        for c in every:
            states.append(st)
            st_ref[c] = st
            st = dec[c] * st + update[c]
        s_scr[...] = st

        o_ref[...] = jnp.concatenate(
            [jnp.concatenate([i_h + _dot(qi_h, st_h, NT) for i_h, qi_h, st_h in
                              zip(intra[c], _heads(qi[c]), _heads(states[c]))], axis=1) for c in every], axis=0)

    grp = lambda g: pl.BlockSpec((None, tb, GROUP), lambda i: (g, i, 0))
    return pl.pallas_call(
        body, name="hgrn_fwd", grid=(t // tb,),
        in_specs=[grp(0), grp(1), grp(2), pl.BlockSpec((2, HGRN_WIDTH), lambda i: (0, 0)), ANY],
        out_specs=[pl.BlockSpec((tb, HGRN_WIDTH), lambda i: (i, 0)),
                   pl.BlockSpec((ncb, HEAD_DIM, HGRN_WIDTH), lambda i: (i, 0, 0))],
        out_shape=[jax.ShapeDtypeStruct((t, HGRN_WIDTH), F32),
                   jax.ShapeDtypeStruct((t // CHUNK, HEAD_DIM, HGRN_WIDTH), F32)],
        scratch_shapes=[pltpu.VMEM((HEAD_DIM, HGRN_WIDTH), F32)],
        compiler_params=_params("arbitrary"),
    )(proj, proj, proj, lb_logits, after)


def _gate_fwd(proj, o, gate_norm_w, after):
    t = proj.shape[1]
    tb = min(t, 1024)

    def body(o_ref, og_ref, gnw_ref, after_ref, out_ref):
        gnw = gnw_ref[...]
        for s in range(tb // GATE_STRIP):
            rows = slice(s * GATE_STRIP, (s + 1) * GATE_STRIP)
            og = og_ref[rows, :]
            on = _per_head(lambda o_h: o_h * lax.rsqrt(jnp.mean(o_h * o_h, axis=-1, keepdims=True) + EPS), o_ref[rows, :])
            out_ref[rows, :] = (on * gnw * (og * _sigmoid(og))).astype(out_ref.dtype)

    tile = pl.BlockSpec((tb, GROUP), lambda i: (i, 0))
    return pl.pallas_call(
        body, name="gate_fwd", grid=(t // tb,),
        in_specs=[tile, pl.BlockSpec((None, tb, GROUP), lambda i: (3, i, 0)), pl.BlockSpec((1, GROUP), lambda i: (0, 0)), ANY],
        out_specs=tile,
        out_shape=jax.ShapeDtypeStruct((t, HGRN_WIDTH), BF16),
        compiler_params=_params("parallel"),
    )(o, proj, gate_norm_w, after)


def _ln_bwd(dy, xhat, rstd, g):
    dxhat = dy * g
    m1 = jnp.mean(dxhat, axis=-1, keepdims=True)
    m2 = jnp.mean(dxhat * xhat, axis=-1, keepdims=True)
    return rstd * (dxhat - m1 - xhat * m2)


def _layer_norm(pre):
    xc = pre - jnp.mean(pre, axis=-1, keepdims=True)
    rstd = lax.rsqrt(jnp.mean(xc * xc, axis=-1, keepdims=True) + EPS)
    return xc * rstd, rstd


def _sublayers(cat_h, cat_c, x, target, w_out, w_ff1, w_ff2, g1, b1, g2, b2):
    t = x.shape[0]
    tm = min(t, 256)

    def body(ch_ref, cc_ref, x_ref, tg_ref, wo_ref, w1_ref, w2_ref, g1_ref, b1_ref, g2_ref, b2_ref,
             h1_ref, r_ref, da_ref, dp2b_ref, dp1_ref, dcat_ref, dg1_ref, db1_ref, dg2_ref, db2_ref, loss_ref, gwo_ref,
             gwo_acc, gwo_narrow, sem):
        @pl.when(pl.program_id(0) == 0)
        def _():
            for ref in (dg1_ref, db1_ref, dg2_ref, db2_ref, loss_ref, gwo_acc):
                ref[...] = jnp.zeros_like(ref)

        mix = _dot(ch_ref[...], wo_ref[0:GROUP, :], NN) + _dot(cc_ref[...], wo_ref[GROUP:2 * GROUP, :], NN)
        xhat1, rstd1 = _layer_norm(ALPHA * x_ref[...] + mix)
        h1 = xhat1 * g1_ref[...] + b1_ref[...]
        h1b = h1.astype(h1_ref.dtype)
        h1_ref[...] = h1b
        mlp = jnp.zeros((tm, D_MODEL), F32)
        for j in range(N_FF):
            cols = slice(j * FF_BLOCK, (j + 1) * FF_BLOCK)
            r = jnp.square(jnp.maximum(_dot(h1b, w1_ref[:, cols], NN), 0.0)).astype(r_ref.dtype)
            r_ref[:, cols] = r
            mlp = mlp + _dot(r, w2_ref[cols, :], NN)
        xhat2, rstd2 = _layer_norm(ALPHA * h1 + mlp)
        err = xhat2 * g2_ref[...] + b2_ref[...] - tg_ref[...]
        loss_ref[...] += 0.5 * jnp.sum(jnp.mean(err * err, axis=-1, keepdims=True))
        dy = err * (1.0 / D_MODEL)
        dg2_ref[...] += jnp.sum(dy * xhat2, axis=0, keepdims=True)
        db2_ref[...] += jnp.sum(dy, axis=0, keepdims=True)
        dp2 = _ln_bwd(dy, xhat2, rstd2, g2_ref[...])
        dp2b = dp2.astype(dp2b_ref.dtype)
        dp2b_ref[...] = dp2b
        back = jnp.zeros((tm, D_MODEL), F32)
        for j in range(N_FF):
            cols = slice(j * FF_BLOCK, (j + 1) * FF_BLOCK)
            dr = _dot(dp2b, w2_ref[cols, :], NT)
            da = (dr * (2.0 * jnp.sqrt(r_ref[:, cols].astype(F32)))).astype(da_ref.dtype)
            da_ref[:, cols] = da
            back = back + _dot(da, w1_ref[:, cols], NT)
        dh1 = ALPHA * dp2 + back
        dg1_ref[...] += jnp.sum(dh1 * xhat1, axis=0, keepdims=True)
        db1_ref[...] += jnp.sum(dh1, axis=0, keepdims=True)
        dp1 = _ln_bwd(dh1, xhat1, rstd1, g1_ref[...])
        dp1b = dp1.astype(MXU_DTYPE)
        dp1_ref[...] = dp1
        dcat_ref[...] = _dot(dp1b, wo_ref[...], NT)
        gwo_acc[0:GROUP, :] += _dot(ch_ref[...], dp1b, TN)
        gwo_acc[GROUP:2 * GROUP, :] += _dot(cc_ref[...], dp1b, TN)

        @pl.when(pl.program_id(0) == pl.num_programs(0) - 1)
        def _():
            gwo_narrow[...] = gwo_acc[...].astype(gwo_narrow.dtype)
            copy = pltpu.make_async_copy(gwo_narrow, gwo_ref, sem.at[0])
            copy.start()
            copy.wait()

    row = pl.BlockSpec((tm, D_MODEL), lambda i: (i, 0))
    wide = pl.BlockSpec((tm, D_FF), lambda i: (i, 0))
    vec = pl.BlockSpec((1, D_MODEL), lambda i: (0, 0))
    narrow = lambda dtype: jax.ShapeDtypeStruct((t, D_MODEL), dtype)
    return pl.pallas_call(
        body, name="sublayers", grid=(t // tm,),
        in_specs=[pl.BlockSpec((tm, GROUP), lambda i: (i, 0)), pl.BlockSpec((tm, GROUP), lambda i: (i, 0)), row, row,
                  _resident((D_MODEL, D_MODEL)),
                  _resident((D_MODEL, D_FF)), _resident((D_FF, D_MODEL)), vec, vec, vec, vec],
        out_specs=[row, wide, wide, row, row, row, vec, vec, vec, vec,
                   pl.BlockSpec((SUBLANES, LANES), lambda i: (0, 0)), ANY],
        out_shape=[narrow(BF16), jax.ShapeDtypeStruct((t, D_FF), BF16), jax.ShapeDtypeStruct((t, D_FF), BF16),
                   narrow(BF16), narrow(F32), narrow(F32)]
                  + [jax.ShapeDtypeStruct((1, D_MODEL), F32)] * 4
                  + [jax.ShapeDtypeStruct((SUBLANES, LANES), F32), jax.ShapeDtypeStruct((D_MODEL, D_MODEL), BF16)],
        scratch_shapes=[pltpu.VMEM((D_MODEL, D_MODEL), F32), pltpu.VMEM((D_MODEL, D_MODEL), BF16),
                        pltpu.SemaphoreType.DMA((1,))],
        compiler_params=_params("arbitrary"),
    )(cat_h, cat_c, x, target, w_out, w_ff1, w_ff2, g1, b1, g2, b2)


def _hgrn_bwd(proj, do, states, lb_logits, after):
    t = proj.shape[1]
    tb = min(t, 512)
    ncb = tb // CHUNK
    nblk = t // tb

    def body(q_ref, f_ref, v_ref, do_ref, st_ref, lbl_ref, after_ref, dp_ref, dlbl_ref, ds_scr, dlb_scr):
        i = pl.program_id(0)

        @pl.when(i == 0)
        def _():
            ds_scr[...] = jnp.zeros_like(ds_scr)
            dlb_scr[...] = jnp.zeros_like(dlb_scr)

        lb, s1 = _lower_bound(lbl_ref[...])
        causal, anti = _chunk_masks()
        every = range(ncb)
        rows = [slice(c * CHUNK, (c + 1) * CHUNK) for c in every]
        q, v, do = ([ref[r, :] for r in rows] for ref in (q_ref, v_ref, do_ref))
        st = [st_ref[c] for c in every]
        gates = [_gates(f_ref[r, :], lb) for r in rows]
        sig, f, k = ([gt[n] for gt in gates] for n in (0, 1, 3))
        b = [_dot_exact(causal, gt[2]) for gt in gates]
        mid, last = [x[CHUNK // 2:CHUNK // 2 + 1, :] for x in b], [x[CHUNK - 1:CHUNK, :] for x in b]
        e_q = [jnp.exp(b[c] - mid[c]) for c in every]
        e_k = [jnp.exp(mid[c] - b[c]) for c in every]
        e_i = [jnp.exp(x) for x in b]
        e_s = [jnp.exp(last[c] - b[c]) for c in every]
        dec = [jnp.exp(x) for x in last]
        qt, kt, qi, ks = ([a[c] * e[c] for c in every] for a, e in ((q, e_q), (k, e_k), (q, e_i), (k, e_s)))

        def masked(a, b_):
            return [[jnp.where(causal, _dot(a_h, b_h, NT), 0.0) for a_h, b_h in zip(_heads(a[c]), _heads(b_[c]))]
                    for c in every]

        def with_scores(s, other, dims):
            return [jnp.concatenate([_dot(s_h, o_h, dims) for s_h, o_h in zip(s[c], _heads(other[c]))], axis=1)
                    for c in every]

        def per_head(dims, a, b_):
            return [_per_head(lambda a_h, b_h: _dot(a_h, b_h, dims), a[c], b_[c]) for c in every]

        scores, dscores = masked(qt, kt), masked(do, v)
        dqt, dkt, dv_intra = with_scores(dscores, kt, NN), with_scores(dscores, qt, TN), with_scores(scores, do, TN)
        dqi, update = per_head(NN, do, st), per_head(TN, do, qi)

        dst = ds_scr[...]
        dsts = [None] * ncb
        for c in reversed(every):
            dsts[c] = dst
            dst = dec[c] * dst + update[c]
        ds_scr[...] = dst

        dv_state, dks = per_head(NT, ks, dsts), per_head(NN, v, dsts)
        ddec = [jnp.sum(dsts[c] * st[c], axis=0, keepdims=True) for c in every]
        dq = [dqt[c] * e_q[c] + dqi[c] * e_i[c] for c in every]
        dk = [dkt[c] * e_k[c] + dks[c] * e_s[c] for c in every]
        db = [q[c] * dq[c] - k[c] * dk[c] for c in every]
        db_last = [jnp.sum(dks[c] * ks[c], axis=0, keepdims=True) + ddec[c] * dec[c] for c in every]
        dg = [_dot_exact(anti, db[c]) + db_last[c] for c in every]
        df = [dg[c] / f[c] - dk[c] for c in every]
        dlb_scr[...] += sum(jnp.sum(df[c] * (1.0 - sig[c]), axis=0, keepdims=True) for c in every)
        dfp = [df[c] * (1.0 - lb) * sig[c] * (1.0 - sig[c]) for c in every]
        dv = [dv_intra[c] + dv_state[c] for c in every]
        for n, parts in enumerate((dq, dfp, dv)):
            dp_ref[n] = jnp.concatenate(parts, axis=0).astype(dp_ref.dtype)

        @pl.when(i == nblk - 1)
        def _():
            dlb = dlb_scr[...]
            dlbl_ref[0:1, :] = dlb * lb * (1.0 - lb)
            dlbl_ref[1:2, :] = -dlb * lb * s1

    grp = lambda g: pl.BlockSpec((None, tb, GROUP), lambda i: (g, nblk - 1 - i, 0))
    vec = pl.BlockSpec((2, HGRN_WIDTH), lambda i: (0, 0))
    return pl.pallas_call(
        body, name="hgrn_bwd", grid=(nblk,),
        in_specs=[grp(0), grp(1), grp(2), pl.BlockSpec((tb, HGRN_WIDTH), lambda i: (nblk - 1 - i, 0)),
                  pl.BlockSpec((ncb, HEAD_DIM, HGRN_WIDTH), lambda i: (nblk - 1 - i, 0, 0)), vec, ANY],
        out_specs=[pl.BlockSpec((3, tb, HGRN_WIDTH), lambda i: (0, nblk - 1 - i, 0)), vec],
        out_shape=[jax.ShapeDtypeStruct((3, t, HGRN_WIDTH), BF16), jax.ShapeDtypeStruct((2, HGRN_WIDTH), F32)],
        scratch_shapes=[pltpu.VMEM((HEAD_DIM, HGRN_WIDTH), F32), pltpu.VMEM((1, HGRN_WIDTH), F32)],
        compiler_params=_params("arbitrary"),
    )(proj, proj, proj, do, states, lb_logits, after)


GRAD_TILE = 512
OUT_PARTS = 4


class _Side:
    def __init__(self, operands, in_specs, out_shape, out_specs, scratch, init, begin):
        self.operands, self.in_specs, self.out_shape, self.out_specs = operands, in_specs, out_shape, out_specs
        self.scratch, self.init, self.begin = scratch, init, begin


RING = 3


def _grad_w(name, operands, widths, shape, step, after=None, side=None, ringed=None):
    t = operands[0].shape[-2]
    tt = min(t, GRAD_TILE)
    n_in, n_steps = len(operands), t // tt
    in_specs = [ANY if n == ringed else pl.BlockSpec((tt, w), lambda k: (k, 0)) if a.ndim == 2 else
                pl.BlockSpec((a.shape[0], tt, w), lambda k: (0, k, 0)) for n, (a, w) in enumerate(zip(operands, widths))]
    extra = [] if after is None else [after]
    s_in, s_out = (len(side.operands), len(side.out_shape)) if side else (0, 0)
    first_out = n_in + s_in + len(extra)
    side_scratch = side.scratch if side else []
    ring_scratch = [] if ringed is None else [pltpu.VMEM((RING, tt, widths[ringed]), operands[ringed].dtype),
                                              pltpu.SemaphoreType.DMA((RING,))]

    def body(*refs):
        o_ref, side_outs = refs[first_out], refs[first_out + 1:first_out + 1 + s_out]
        acc, narrow, sem = refs[first_out + 1 + s_out:first_out + 4 + s_out]
        first_side = first_out + 4 + s_out
        k = pl.program_id(0)
        tiles = list(refs[:n_in])
        if ringed is not None:
            ring, ring_sems = refs[first_side + len(side_scratch):]

            def fetch(tile, slot):
                return pltpu.make_async_copy(refs[ringed].at[pl.ds(tile * tt, tt), :], ring.at[slot], ring_sems.at[slot])

            @pl.when(k == 0)
            def _():
                for first in range(min(RING - 1, n_steps)):
                    fetch(first, first).start()

            @pl.when(k + RING - 1 < n_steps)
            def _():
                fetch(k + RING - 1, lax.rem(k + RING - 1, RING)).start()

        @pl.when(k == 0)
        def _():
            acc[...] = jnp.zeros_like(acc)
            if side:
                side.init(side_outs)

        if ringed is not None:
            slot = lax.rem(k, RING)
            fetch(k, slot).wait()
            tiles[ringed] = ring.at[slot]

        tick = (side.begin(k, n_steps, refs[n_in:n_in + s_in], side_outs, refs[first_side:first_side + len(side_scratch)])
                if side else None)
        step(acc, *tiles, tick or (lambda j: None))

        @pl.when(k == n_steps - 1)
        def _():
            part = shape[0] // OUT_PARTS
            copies = []
            for p in range(OUT_PARTS):
                rows = pl.ds(p * part, part)
                narrow[rows, :] = acc[rows, :].astype(narrow.dtype)
                copies.append(pltpu.make_async_copy(narrow.at[rows, :], o_ref.at[rows, :], sem.at[p]))
                copies[-1].start()
            for cp in copies:
                cp.wait()

    outs = pl.pallas_call(
        body, name=name, grid=(n_steps,),
        in_specs=in_specs + (side.in_specs if side else []) + [ANY] * len(extra),
        out_specs=[ANY] + (side.out_specs if side else []),
        out_shape=[jax.ShapeDtypeStruct(shape, BF16)] + (side.out_shape if side else []),
        scratch_shapes=[pltpu.VMEM(shape, F32), pltpu.VMEM(shape, BF16), pltpu.SemaphoreType.DMA((OUT_PARTS,))]
                       + side_scratch + ring_scratch,
        compiler_params=_params("arbitrary"),
    )(*operands, *(side.operands if side else ()), *extra)
    return outs if side else outs[0]


def _dw_in(xb, dph, dog, dpc, w_in, dpre1, after):
    t = xb.shape[0]

    def step(acc, x_ref, dh_ref, dog_ref, dc_ref, tick):
        xv = x_ref[...]
        for g in range(N_GROUPS):
            part = dh_ref[g] if g < 3 else dog_ref[...] if g == 3 else dc_ref[g - 4]
            acc[:, g * GROUP:(g + 1) * GROUP] += _dot(xv, part, TN)
            tick(g, part)

    def begin(k, n_steps, ins, outs, scratch):
        w_ref, dp_ref = ins
        total = [ALPHA * dp_ref[...]]

        def tick(g, part):
            total[0] = total[0] + _dot(part, w_ref[:, g * GROUP:(g + 1) * GROUP], NT)
            if g == N_GROUPS - 1:
                outs[0][...] = total[0]

        return tick

    row = pl.BlockSpec((min(t, GRAD_TILE), D_MODEL), lambda k: (k, 0))
    side = _Side((w_in, dpre1), [_resident((D_MODEL, IN_COLS)), row], [jax.ShapeDtypeStruct((t, D_MODEL), F32)], [row],
                 [], lambda outs: None, begin)
    return _grad_w("dw_in", (xb, dph, dog, dpc), (D_MODEL, GROUP, GROUP, GROUP), (D_MODEL, IN_COLS), step, after, side)


def _strips_of(j, tt):
    per_tick = tt // GATE_STRIP // N_FF
    return [slice(s * GATE_STRIP, (s + 1) * GATE_STRIP) for s in range(j * per_tick, (j + 1) * per_tick)]


def _dw_ff1(h1b, da, dcat, o, proj, gate_norm_w, after):
    t = h1b.shape[0]
    tt = min(t, GRAD_TILE)

    def step(acc, h_ref, da_ref, tick):
        hv = h_ref[...]
        for j in range(N_FF):
            cols = slice(j * FF_BLOCK, (j + 1) * FF_BLOCK)
            acc[:, cols] += _dot(hv, da_ref[:, cols], TN)
            tick(j)

    def init(outs):
        outs[2][...] = jnp.zeros_like(outs[2])

    def begin(k, n_steps, ins, outs, scratch):
        do2_ref, o_ref, og_ref, gnw_ref = ins
        do_ref, dog_ref, dgnw_ref = outs
        total = [jnp.zeros((GATE_STRIP, GROUP), F32)]

        def tick(j):
            gnw = gnw_ref[...]
            for rows in _strips_of(j, tt):
                ov, og, do2 = o_ref[rows, :], og_ref[rows, :], do2_ref[rows, :]
                rs = _per_head(lambda o_h: jnp.broadcast_to(
                    lax.rsqrt(jnp.mean(o_h * o_h, axis=-1, keepdims=True) + EPS), o_h.shape), ov)
                on = ov * rs
                sg = _sigmoid(og)
                sil = og * sg
                don = do2 * gnw * sil
                total[0] = total[0] + do2 * on * sil
                dog_ref[rows, :] = (do2 * on * gnw * (sg * (1.0 + og * (1.0 - sg)))).astype(dog_ref.dtype)
                do_ref[rows, :] = rs * (don - on * _per_head(
                    lambda p_h: jnp.broadcast_to(jnp.mean(p_h, axis=-1, keepdims=True), p_h.shape), don * on))
            if j == N_FF - 1:
                dgnw_ref[...] += jnp.sum(total[0], axis=0, keepdims=True)

        return tick

    tile = pl.BlockSpec((tt, GROUP), lambda k: (k, 0))
    vec = pl.BlockSpec((1, GROUP), lambda k: (0, 0))
    side = _Side(
        (dcat, o, proj, gate_norm_w), [tile, tile, pl.BlockSpec((None, tt, GROUP), lambda k: (3, k, 0)), vec],
        [jax.ShapeDtypeStruct((t, HGRN_WIDTH), F32), jax.ShapeDtypeStruct((t, HGRN_WIDTH), BF16),
         jax.ShapeDtypeStruct((1, HGRN_WIDTH), F32)], [tile, tile, vec], [], init, begin)
    return _grad_w("dw_ff1", (h1b, da), (D_MODEL, D_FF), (D_MODEL, D_FF), step, after, side, ringed=1)


def _dw_ff2(r, dpre2b, dcat, bcu, conv_w):
    t = r.shape[0]
    tt = min(t, GRAD_TILE)
    hb = tt // SUBLANES
    halo = 2 * SUBLANES

    def step(acc, r_ref, d_ref, tick):
        dv = d_ref[...]
        for j in range(N_FF):
            rows = slice(j * FF_BLOCK, (j + 1) * FF_BLOCK)
            acc[rows, :] += _dot(r_ref[:, rows], dv, TN)
            tick(j)

    def init(outs):
        outs[1][...] = jnp.zeros_like(outs[1])

    def begin(k, n_steps, ins, outs, scratch):
        dy_ref, dyn_ref, b_ref, bn_ref, c_ref, u_ref, ch_ref, uh_ref, cw_ref = ins
        dp_ref, dcw_ref = outs
        zbuf, dbuf = scratch
        before = lambda ref: ref[SUBLANES:halo, :].astype(F32)
        zbuf[0:SUBLANES, :] = jnp.where(k > 0, before(ch_ref) * before(uh_ref), 0.0)
        zbuf[SUBLANES:SUBLANES + tt, :] = c_ref[...].astype(F32) * u_ref[...].astype(F32)
        dbuf[0:tt, :] = dy_ref[...] * b_ref[...].astype(F32)
        dbuf[tt:tt + SUBLANES, :] = jnp.where(k < n_steps - 1, dyn_ref[...] * bn_ref[0:SUBLANES, :].astype(F32), 0.0)
        totals = [jnp.zeros((GATE_STRIP, GROUP), F32) for _ in range(3)]

        def tick(j):
            cw = cw_ref[...]
            for rows in _strips_of(j, tt):
                at = lambda buf, shift: buf[shift + rows.start:shift + rows.stop, :]
                z, z1, z2 = at(zbuf, SUBLANES), at(zbuf, SUBLANES - 1), at(zbuf, SUBLANES - 2)
                dyc, d1, d2 = at(dbuf, 0), at(dbuf, 1), at(dbuf, 2)
                yc = cw[2:3, :] * z + cw[1:2, :] * z1 + cw[0:1, :] * z2
                dz = cw[2:3, :] * dyc + cw[1:2, :] * d1 + cw[0:1, :] * d2
                dp_ref[0, rows, :] = (dy_ref[rows, :] * yc).astype(dp_ref.dtype)
                dp_ref[1, rows, :] = (dz * u_ref[rows, :].astype(F32)).astype(dp_ref.dtype)
                dp_ref[2, rows, :] = (dz * c_ref[rows, :].astype(F32)).astype(dp_ref.dtype)
                for n, tap in enumerate((z2, z1, z)):
                    totals[n] = totals[n] + dyc * tap
            if j == N_FF - 1:
                for n in range(3):
                    dcw_ref[n:n + 1, :] += jnp.sum(totals[n], axis=0, keepdims=True)

        return tick

    grp = lambda g: pl.BlockSpec((None, tt, GROUP), lambda k: (g, k, 0))
    prev = lambda g: pl.BlockSpec((None, halo, GROUP), lambda k: (g, jnp.maximum(k * (tt // halo) - 1, 0), 0))
    nxt = lambda g: pl.BlockSpec((None, halo, GROUP), lambda k: (g, jnp.minimum((k + 1) * (tt // halo), t // halo - 1), 0))
    nxt_row = lambda k: jnp.minimum((k + 1) * hb, t // SUBLANES - 1)
    whole = pl.BlockSpec((3, CONV_WIDTH), lambda k: (0, 0))
    side = _Side(
        (dcat, dcat, bcu, bcu, bcu, bcu, bcu, bcu, conv_w),
        [pl.BlockSpec((tt, GROUP), lambda k: (k, 1)), pl.BlockSpec((SUBLANES, GROUP), lambda k: (nxt_row(k), 1)),
         grp(0), nxt(0), grp(1), grp(2), prev(1), prev(2), whole],
        [jax.ShapeDtypeStruct((3, t, CONV_WIDTH), BF16), jax.ShapeDtypeStruct((3, CONV_WIDTH), F32)],
        [pl.BlockSpec((3, tt, GROUP), lambda k: (0, k, 0)), whole],
        [pltpu.VMEM((tt + SUBLANES, GROUP), F32), pltpu.VMEM((tt + SUBLANES, GROUP), F32)], init, begin)
    return _grad_w("dw_ff2", (r, dpre2b), (D_FF, D_MODEL), (D_FF, D_MODEL), step, side=side, ringed=0)


def _place():
    x, y, c = lax.axis_index("x"), lax.axis_index("y"), lax.axis_index("c")
    return x, y, c, 2 * x + y


def _other_chips(x, y):
    return [(1 - x, y), (x, 1 - y), (1 - x, 1 - y)]


def _place_shard(name, w, chip, cols_sharded, after=None):
    rows, cols = w.shape
    tr = min(rows, 256)
    nb = rows // tr
    full = (rows, cols * N_CHIPS) if cols_sharded else (rows * N_CHIPS, cols)
    out_map = (lambda i, s: (i, s[0])) if cols_sharded else (lambda i, s: (s[0] * nb + i, 0))

    def body(s_ref, w_ref, *rest):
        rest[-1][...] = w_ref[...].astype(rest[-1].dtype)

    extra = [] if after is None else [after]
    return pl.pallas_call(
        body, name=name,
        grid_spec=pltpu.PrefetchScalarGridSpec(
            num_scalar_prefetch=1, grid=(nb,),
            in_specs=[pl.BlockSpec((tr, cols), lambda i, s: (i, 0))] + [ANY] * len(extra),
            out_specs=pl.BlockSpec((tr, cols), out_map)),
        out_shape=jax.ShapeDtypeStruct(full, BF16),
        compiler_params=_params("parallel"),
    )(chip, w, *extra)


HBM = pl.BlockSpec(memory_space=pltpu.HBM)
SEM = pl.BlockSpec(memory_space=pltpu.SEMAPHORE)
EFFECT = pltpu.SideEffectType.DATAFLOW_SIDE_EFFECTING


PEER_SETS = {
    "sibling": (0, lambda x, y, c: [(x, y, 1 - c)]),
    "chips": (1, lambda x, y, c: [(1 - x, y, c), (x, 1 - y, c), (1 - x, 1 - y, c)]),
    "neighbours": (2, lambda x, y, c: [(1 - x, y, c), (x, 1 - y, c)]),
}


class _Split:
    def __init__(self, name, arrays, plan, others=(), peers=None, prepare=None, sources=(), scratch=()):
        n_own, arrays = len(arrays), (*arrays, *others)
        n, n_copies, n_in = len(arrays), plan.count, len(arrays) + len(sources)
        self.name, self.plan, self.n = name, plan, n_own
        barrier_id, peer_ids = PEER_SETS[peers] if peers else (None, None)

        def body(*refs):
            send_sems, recv_sems, token = refs[n_in], refs[n_in + 1], refs[n_in + 2 + n]
            if peers:
                x, y, c, _ = _place()
                barrier = pltpu.get_barrier_semaphore()
                for peer in peer_ids(x, y, c):
                    pl.semaphore_signal(barrier, inc=1, device_id=peer, device_id_type=MESH)
            if prepare:
                prepare(refs[:n], refs[n:n_in], refs[n_in + 3 + n:])
            if peers:
                pl.semaphore_wait(barrier, len(peer_ids(0, 0, 0)))
            for k, (src, dst, to) in enumerate(plan(refs[:n])):
                pltpu.make_async_remote_copy(src_ref=src, dst_ref=dst, send_sem=send_sems.at[k], recv_sem=recv_sems.at[k],
                                             device_id=to, device_id_type=MESH).start()
            token[...] = jnp.zeros_like(token)

        outs = pl.pallas_call(
            body, name=name + "_start",
            out_shape=(pltpu.SemaphoreType.DMA((n_copies,)), pltpu.SemaphoreType.DMA((n_copies,)),
                       *[pltpu.HBM(a.shape, a.dtype) for a in arrays], jax.ShapeDtypeStruct((SUBLANES, LANES), F32)),
            in_specs=(HBM,) * n_in, out_specs=(SEM, SEM) + (HBM,) * n + (pl.BlockSpec(memory_space=pltpu.VMEM),),
            input_output_aliases={i: 2 + i for i in range(n)}, scratch_shapes=list(scratch),
            compiler_params=pltpu.CompilerParams(has_side_effects=EFFECT, collective_id=barrier_id),
        )(*[pltpu.with_memory_space_constraint(a, pltpu.HBM) for a in (*arrays, *sources)])
        self.sems, self.arrays, self.others, self.token = outs[:2], outs[2:2 + n_own], outs[2 + n_own:2 + n], outs[-1]
        self.waited = set()

    def wait(self, after, copies=None, part=""):
        n, plan = self.n, self.plan
        mine = set(range(plan.count) if copies is None else copies) - self.waited
        self.waited |= mine

        def body(*refs):
            send_sems, recv_sems = refs[n], refs[n + 1]
            for k, (src, dst, to) in enumerate(plan(refs[:n])):
                if k in mine:
                    cp = pltpu.make_async_remote_copy(src_ref=src, dst_ref=dst, send_sem=send_sems.at[k],
                                                      recv_sem=recv_sems.at[k], device_id=to, device_id_type=MESH)
                    cp.wait_send()
                    cp.wait_recv()

        self.arrays = pl.pallas_call(
            body, name=self.name + "_wait" + part, out_shape=tuple(pltpu.HBM(a.shape, a.dtype) for a in self.arrays),
            in_specs=(HBM,) * n + (SEM, SEM, ANY), out_specs=(HBM,) * n, input_output_aliases={i: i for i in range(n)},
            compiler_params=pltpu.CompilerParams(has_side_effects=EFFECT),
        )(*self.arrays, *self.sems, after)
        return self.arrays


COLS_SHARDED = (True, False, True, False)
HALF_SHAPES = [(D_MODEL // 2, IN_COLS), (D_MODEL, D_MODEL // 2), (D_MODEL // 2, D_FF), (D_FF, D_MODEL // 2)]
PIECE_SHAPES = [(D_MODEL // 2, IN_COLS // N_CHIPS), (D_MODEL // N_CHIPS, D_MODEL // 2),
                (D_MODEL // 2, D_FF // N_CHIPS), (D_FF // N_CHIPS, D_MODEL // 2)]


def _shard_view(kind, ref, chip):
    if COLS_SHARDED[kind]:
        n = ref.shape[1] // N_CHIPS
        return ref.at[:, pl.ds(chip * n, n)]
    n = ref.shape[0] // N_CHIPS
    return ref.at[pl.ds(chip * n, n), :]


def _half_view(kind, ref, h):
    if COLS_SHARDED[kind]:
        n = ref.shape[0] // 2
        return ref.at[pl.ds(h * n, n), :]
    n = ref.shape[1] // 2
    return ref.at[:, pl.ds(h * n, n)]


def _plan(count):
    def mark(fn):
        fn.count = count
        return fn
    return mark


def _shard_rows_view(kind, ref, chip, part, n_parts):
    if COLS_SHARDED[kind]:
        m, n = ref.shape[0] // n_parts, ref.shape[1] // N_CHIPS
        return ref.at[pl.ds(part * m, m), pl.ds(chip * n, n)]
    m = ref.shape[0] // N_CHIPS // n_parts
    return ref.at[pl.ds((n_parts * chip + part) * m, m), :]


def _shard_half_view(kind, ref, chip, h):
    return _shard_rows_view(kind, ref, chip, h, 2)


def _gather_over_ici(kinds, weights, others=()):
    @_plan(2 * len(kinds))
    def plan(refs):
        x, y, c, me = _place()
        mine = [_shard_half_view(kind, ref, me, c) for kind, ref in zip(kinds, refs)]
        return [(v, v, to) for v in mine for to in ((1 - x, y, c), (x, 1 - y, c))]

    return _Split("gather_ici_" + "".join(map(str, kinds)), tuple(weights), plan, others, peers="neighbours")


def _relay_over_ici(kinds, weights, others=()):
    @_plan(2 * len(kinds))
    def plan(refs):
        x, y, c, _ = _place()
        x_nbr, y_nbr = 2 * (1 - x) + y, 2 * x + (1 - y)
        out = []
        for kind, ref in zip(kinds, refs):
            first, second = (_shard_rows_view(kind, ref, chip, 2 * c + q, 4) for q, chip in ((0, x_nbr), (1, y_nbr)))
            out += [(first, first, (x, 1 - y, c)), (second, second, (1 - x, y, c))]
        return out

    return _Split("relay_ici_" + "".join(map(str, kinds)), tuple(weights), plan, others, peers="neighbours")


def _gather_w_in_over_ici(shard, conv4):
    rows, cols = shard.shape

    @_plan(6)
    def plan(refs):
        x, y, c, me = _place()
        half, conv = _shard_half_view(0, refs[0], me, c), refs[1].at[me]
        return [(v, v, (px, py, c)) for v in (half, conv) for px, py in _other_chips(x, y)]

    def prepare(refs, sources, scratch):
        wide, narrow, sems = scratch
        _, _, _, me = _place()
        load = pltpu.make_async_copy(sources[0], wide, sems.at[0])
        load.start()
        load.wait()
        narrow[...] = wide[...].astype(narrow.dtype)
        store = pltpu.make_async_copy(narrow, _shard_view(0, refs[0], me), sems.at[1])
        store.start()
        store.wait()

    return _Split("gather_w_in_ici", (lax.empty((rows, cols * N_CHIPS), BF16), conv4), plan, peers="chips",
                  prepare=prepare, sources=(shard,),
                  scratch=(pltpu.VMEM((rows, cols), F32), pltpu.VMEM((rows, cols), BF16), pltpu.SemaphoreType.DMA((2,))))


def _gather_over_d2d(kinds, weights, relations=(0, 1, 2), part=""):
    @_plan(len(relations) * len(kinds))
    def plan(refs):
        x, y, c, _ = _place()
        chips = [_other_chips(x, y)[n] for n in relations]
        got = [_shard_half_view(kind, ref, 2 * px + py, c) for kind, ref in zip(kinds, refs) for px, py in chips]
        return [(v, v, (x, y, 1 - c)) for v in got]

    return _Split("gather_d2d_" + "".join(map(str, kinds)) + part, tuple(weights), plan, peers="sibling")


def _swap_halves(kinds, grads):
    @_plan(len(kinds))
    def plan(refs):
        x, y, c, _ = _place()
        return [(_half_view(kind, g, 1 - c), land, (x, y, 1 - c))
                for kind, g, land in zip(kinds, refs[:len(kinds)], refs[len(kinds):])]

    lands = [lax.empty(HALF_SHAPES[kind], g.dtype) for kind, g in zip(kinds, grads)]
    return _Split("swap_halves_" + "".join(map(str, kinds)), (*grads, *lands), plan, peers="sibling")


def _block_rows(cols, elements):
    return 1 << ((elements // cols).bit_length() - 1)


def _grid_steps(shapes, elements):
    rows, cols = max(shapes, key=lambda shape: shape[0] * shape[1])
    return rows // min(rows, _block_rows(cols, elements))


def _add_half(name, kinds, grads, recvs, core):
    n = len(kinds)
    shapes = [recv.shape for recv in recvs]
    nb = _grid_steps(shapes, 1 << 20)

    def body(c_ref, *refs):
        for g_ref, r_ref, o_ref in zip(refs[:n], refs[n:2 * n], refs[2 * n:]):
            o_ref[...] = (g_ref[...].astype(F32) + r_ref[...].astype(F32)).astype(o_ref.dtype)

    own = [pl.BlockSpec((rows // nb, cols), (lambda i, c_ref: (c_ref[0] * nb + i, 0)) if COLS_SHARDED[k] else
                        (lambda i, c_ref: (i, c_ref[0]))) for k, (rows, cols) in zip(kinds, shapes)]
    blocks = [pl.BlockSpec((rows // nb, cols), lambda i, c_ref: (i, 0)) for rows, cols in shapes]
    return pl.pallas_call(
        body, name=name,
        grid_spec=pltpu.PrefetchScalarGridSpec(
            num_scalar_prefetch=1, grid=(nb,), in_specs=own + blocks, out_specs=blocks),
        out_shape=[jax.ShapeDtypeStruct(shape, BF16) for shape in shapes],
        compiler_params=_params("parallel"),
    )(core, *grads, *recvs)


def _exchange_pieces(kinds, halves, pack=None):
    n_p, n = N_CHIPS - 1, len(kinds)

    @_plan(n_p * n + (0 if pack is None else N_DEV - 1))
    def plan(refs):
        x, y, c, _ = _place()
        copies = []
        if pack is not None:
            me = 4 * x + 2 * y + c
            peers = [((1 - x) if m & 4 else x, (1 - y) if m & 2 else y, (1 - c) if m & 1 else c) for m in range(1, N_DEV)]
            copies += [(refs[2 * n], refs[2 * n + 1].at[me], peer) for peer in peers]
        return copies + [(_shard_view(kind, half, 2 * px + py), land.at[j], (px, py, c))
                         for j, (px, py) in enumerate(_other_chips(x, y))
                         for kind, half, land in zip(kinds, refs[:n], refs[n:2 * n])]

    lands = [lax.empty((n_p,) + PIECE_SHAPES[kind], BF16) for kind in kinds]
    small = () if pack is None else (pack, lax.empty((N_DEV,) + pack.shape, F32))
    return _Split("exchange_pieces_" + "".join(map(str, kinds)), (*halves, *lands, *small), plan,
                  peers="chips" if pack is None else None)


def _sum_pieces(name, kinds, halves, slots, place, after):
    n, n_p = len(kinds), N_CHIPS - 1
    shapes = [slot.shape[1:] for slot in slots]
    nb = _grid_steps(shapes, 1 << 18)

    def body(s_ref, *refs):
        for own_ref, slot_ref, o_ref in zip(refs[:n], refs[n:2 * n], refs[2 * n + 1:]):
            total = own_ref[...].astype(F32)
            for j in range(n_p):
                total = total + slot_ref[j].astype(F32)
            o_ref[...] = total

    own, out, shards = [], [], []
    for k, (rows, cols) in zip(kinds, shapes):
        if COLS_SHARDED[k]:
            own_map, out_map, shard = (lambda i, s: (i, s[0])), (lambda i, s: (s[1] * nb + i, 0)), (2 * rows, cols)
        else:
            own_map, out_map, shard = (lambda i, s: (s[0] * nb + i, 0)), (lambda i, s: (i, s[1])), (rows, 2 * cols)
        own.append(pl.BlockSpec((rows // nb, cols), own_map))
        out.append(pl.BlockSpec((rows // nb, cols), out_map))
        shards.append(jax.ShapeDtypeStruct(shard, F32))
    landed = [pl.BlockSpec((n_p, rows // nb, cols), lambda i, s: (0, i, 0)) for rows, cols in shapes]
    return pl.pallas_call(
        body, name=name,
        grid_spec=pltpu.PrefetchScalarGridSpec(
            num_scalar_prefetch=1, grid=(nb,), in_specs=own + landed + [ANY], out_specs=out),
        out_shape=shards,
        compiler_params=_params("parallel"),
    )(place, *halves, *slots, after)


def _join_halves(kinds, shards):
    @_plan(len(kinds))
    def plan(refs):
        x, y, c, _ = _place()
        return [(_half_view(kind, g, c), _half_view(kind, g, c), (x, y, 1 - c)) for kind, g in zip(kinds, refs)]

    return _Split("join_halves_" + "".join(map(str, kinds)), tuple(shards), plan, peers="sibling")


N_DEV = 8


def _sum_shared(pack, land, device):
    def body(d_ref, p_ref, l_ref, o_ref):
        me = d_ref[0]
        total = jnp.where(me == 0, p_ref[...], l_ref[0])
        for d in range(1, N_DEV):
            total = total + jnp.where(me == d, p_ref[...], l_ref[d])
        o_ref[...] = total

    return pl.pallas_call(
        body, name="sum_shared",
        grid_spec=pltpu.PrefetchScalarGridSpec(
            num_scalar_prefetch=1, grid=(1,),
            in_specs=[pl.BlockSpec(pack.shape, lambda i, d: (0, 0)), pl.BlockSpec(land.shape, lambda i, d: (0, 0, 0))],
            out_specs=pl.BlockSpec(pack.shape, lambda i, d: (0, 0))),
        out_shape=jax.ShapeDtypeStruct(pack.shape, F32),
    )(device, pack, land)


def _adamw(name, weights, after=None):
    shapes = [w.shape for w, _, _, _ in weights]
    nb = _grid_steps(shapes, 1 << 18)
    extra = [] if after is None else [after]
    n_in = 4 * len(weights) + len(extra)

    def body(*refs):
        for k in range(len(weights)):
            w_ref, g_ref, m_ref, v_ref = refs[4 * k:4 * k + 4]
            go_ref, d_ref, nm_ref, nv_ref = refs[n_in + 4 * k:n_in + 4 * k + 4]
            g = g_ref[...]
            go_ref[...] = g
            d_ref[...], nm_ref[...], nv_ref[...] = _adam_step(w_ref[...], g, m_ref[...], v_ref[...])

    blocks = [pl.BlockSpec((rows // nb, cols), lambda i: (i, 0)) for rows, cols in shapes for _ in range(4)]
    outs = pl.pallas_call(
        body, name=name, grid=(nb,), in_specs=blocks + [ANY] * len(extra), out_specs=blocks,
        out_shape=[jax.ShapeDtypeStruct(shape, F32) for shape in shapes for _ in range(4)],
        compiler_params=_params("parallel"),
    )(*[a for group in weights for a in group], *extra)
    return [outs[4 * k:4 * k + 4] for k in range(len(weights))]


def _adam_step(w, g, m, v):
    nm = ADAM_B1 * m + (1.0 - ADAM_B1) * g
    nv = ADAM_B2 * v + (1.0 - ADAM_B2) * jnp.square(g)
    m_hat = nm * (1.0 / (1.0 - ADAM_B1 ** ADAM_STEP))
    v_hat = nv * (1.0 / (1.0 - ADAM_B2 ** ADAM_STEP))
    return -ADAM_LR * (m_hat / (jnp.sqrt(v_hat) + ADAM_EPS) + ADAM_WD * w), nm, nv


def _adamw_small(tot, chip, weights, ms, vs, after):
    n, half = len(weights), D_MODEL // 2

    def body(chip_ref, tot_ref, *refs):
        ins, outs = refs[:3 * n], refs[3 * n + 1:]
        tot = tot_ref[...]
        conv_all = jnp.concatenate([tot[5:6, half:], tot[6:7, :half], tot[6:7, half:]], axis=0)
        conv = sum(jnp.where(chip_ref[0] == s, conv_all[:, s * LANES:(s + 1) * LANES], 0.0) for s in range(N_CHIPS))
        grads = [jnp.concatenate([tot[4:5, :half], tot[4:5, half:]], axis=0), tot[5:6, :half], conv,
                 tot[0:1], tot[1:2], tot[2:3], tot[3:4]]
        for k, g in enumerate(grads):
            delta, nm, nv = _adam_step(ins[k][...], g, ins[n + k][...], ins[2 * n + k][...])
            outs[k][...], outs[n + k][...], outs[2 * n + k][...], outs[3 * n + k][...] = g, delta, nm, nv
        outs[4 * n][...] = tot[7:8, 0:1]

    whole = lambda a: pl.BlockSpec(a.shape, lambda i, s: (0,) * a.ndim)
    arrays = (*weights, *ms, *vs)
    loss = jax.ShapeDtypeStruct((1, 1), F32)
    return pl.pallas_call(
        body, name="adamw_small",
        grid_spec=pltpu.PrefetchScalarGridSpec(
            num_scalar_prefetch=1, grid=(1,), in_specs=[whole(tot)] + [whole(a) for a in arrays] + [ANY],
            out_specs=[whole(a) for a in weights] * 4 + [whole(loss)]),
        out_shape=[jax.ShapeDtypeStruct(a.shape, F32) for a in weights] * 4 + [loss],
    )(chip, tot, *arrays, after)


def kernel(x, w_in, lb_logits, gate_norm_w, conv_w, w_out, ln1_g, ln1_b, w_ff1, w_ff2, ln2_g, ln2_b, loss_target, m_w_in, m_lb_logits, m_gate_norm_w, m_conv_w, m_w_out, m_ln1_g, m_ln1_b, m_w_ff1, m_w_ff2, m_ln2_g, m_ln2_b, v_w_in, v_lb_logits, v_gate_norm_w, v_conv_w, v_w_out, v_ln1_g, v_ln1_b, v_w_ff1, v_w_ff2, v_ln2_g, v_ln2_b):
    xs, tgt = x[0], loss_target[0]
    chip = 2 * lax.axis_index("x") + lax.axis_index("y")
    core = lax.axis_index("c").astype(jnp.int32).reshape(1)
    chip1 = chip.astype(jnp.int32).reshape(1)
    place = jnp.concatenate([chip1, core])

    conv4 = lax.dynamic_update_slice(jnp.zeros((N_CHIPS,) + conv_w.shape[1:], F32), conv_w, (chip, 0, 0))
    ici_in = _gather_w_in_over_ici(w_in[0], conv4)
    rest = (1, 2, 3)
    placed_out = _place_shard("place_w_out", w_out[0], chip1, False, after=ici_in.token)
    placed_ff1 = _place_shard("place_w_ff1", w_ff1[0], chip1, True, after=placed_out)
    placed_ff2 = _place_shard("place_w_ff2", w_ff2[0], chip1, False, after=placed_ff1)
    near = ici_in.wait(placed_ff2, (0, 1, 3, 4, 5), "_near")
    ici_rest = _gather_over_ici(rest, (placed_out, placed_ff1, placed_ff2), near[:1])
    d2d_near = _gather_over_d2d((0,), ici_rest.others, (0, 1), "_near")
    ici_in.arrays = (*d2d_near.arrays, near[1])
    far = ici_in.wait(d2d_near.token)
    d2d_far = _gather_over_d2d((0,), far[:1], (2,), "_far")
    d2d_near.arrays = d2d_far.arrays
    d2d_far.arrays = d2d_near.wait(d2d_far.token)
    wb_in, = d2d_far.wait(d2d_far.token)
    cv4 = far[1]
    conv_full = cv4.transpose(1, 0, 2).reshape(3, CONV_WIDTH)

    proj, bcu, xb, cat_c = _in_proj(xs, wb_in, conv_full, ici_rest.token)
    relay_rest = _relay_over_ici(rest, ici_rest.wait(proj))
    o, states = _hgrn_fwd(proj, lb_logits, relay_rest.token)
    d2d_rest = _gather_over_d2d(rest, relay_rest.wait(o))
    cat_h = _gate_fwd(proj, o, gate_norm_w, d2d_rest.token)
    wb_out, wb_ff1, wb_ff2 = d2d_rest.wait(cat_h)

    (h1b, r, da, dpre2b, dpre1, dcat, g_ln1_g, g_ln1_b, g_ln2_g, g_ln2_b, loss8, g_out_local) = _sublayers(
        cat_h, cat_c, xs, tgt, wb_out, wb_ff1, wb_ff2, ln1_g, ln1_b, ln2_g, ln2_b)

    names = ("w_in", "w_out", "w_ff1", "w_ff2")

    def named(prefix, kinds):
        return prefix + "".join("_" + names[k] for k in kinds)

    def add_halves(kinds, grads, lands):
        return _add_half(named("add_half", kinds), kinds, grads, lands, core)

    def sum_pieces(kinds, halves, lands, after):
        return _sum_pieces(named("sum_pieces", kinds), kinds, halves, lands, place, after)

    early = (1, 2, 3)
    g_ff2_local, dpc, g_conv = _dw_ff2(r, dpre2b, dcat, bcu, conv_full)
    swap_a = _swap_halves((1, 3), (g_out_local, g_ff2_local))
    g_ff1_local, do, dog, g_gnw = _dw_ff1(h1b, da, dcat, o, proj, gate_norm_w, swap_a.token)
    swap_b = _swap_halves((2,), (g_ff1_local,))
    swapped_a = swap_a.wait(swap_b.token)
    halves_a = add_halves((1, 3), swapped_a[:2], swapped_a[2:])
    swapped_b = swap_b.wait(halves_a[1])
    halves = (halves_a[0], *add_halves((2,), swapped_b[:1], swapped_b[1:]), halves_a[1])
    exch = _exchange_pieces(early, halves)
    dph, g_lbl = _hgrn_bwd(proj, do, states, lb_logits, exch.token)
    g_in_local, grad_x = _dw_in(xb, dph, dog, dpc, wb_in, dpre1, dph)

    late = (0,)
    swap = _swap_halves(late, (g_in_local,))
    exchanged = exch.wait(swap.token)
    pack = jnp.concatenate([
        g_ln1_g, g_ln1_b, g_ln2_g, g_ln2_b,
        jnp.concatenate([g_lbl[0:1], g_lbl[1:2]], axis=1),
        jnp.concatenate([g_gnw, g_conv[0:1]], axis=1),
        jnp.concatenate([g_conv[1:2], g_conv[2:3]], axis=1),
        jnp.concatenate([loss8[0:1], jnp.zeros((1, D_MODEL - LANES), F32)], axis=1)], axis=0)
    join_a = _join_halves((2,), sum_pieces((2,), exchanged[1:2], exchanged[4:5], swap.token))
    swapped = swap.wait(join_a.token)
    exch = _exchange_pieces(late, add_halves(late, swapped[:1], swapped[1:]), pack)
    join_b = _join_halves((1, 3), sum_pieces((1, 3), exchanged[0:3:2], exchanged[3:6:2], exch.token))
    g_w_ff1, = join_a.wait(join_b.token)
    (g_w_ff1, d_ff1, nm_ff1, nv_ff1), = _adamw("adamw_w_ff1", [(w_ff1[0], g_w_ff1, m_w_ff1[0], v_w_ff1[0])])
    g_w_out, g_w_ff2 = join_b.wait(d_ff1)
    (g_w_ff2, d_ff2, nm_ff2, nv_ff2), (g_w_out, d_out, nm_out, nv_out) = _adamw(
        "adamw_w_ff2_w_out", [(w_ff2[0], g_w_ff2, m_w_ff2[0], v_w_ff2[0]), (w_out[0], g_w_out, m_w_out[0], v_w_out[0])])
    shared = exch.wait(d_out, range(N_DEV - 1), "_pack")
    tot = _sum_shared(shared[2], shared[3], 2 * chip1 + core)
    exchanged = exch.wait(tot)
    join = _join_halves(late, sum_pieces(late, exchanged[:1], exchanged[1:2], tot))
    small = ("lb_logits", "gate_norm_w", "conv_w", "ln1_g", "ln1_b", "ln2_g", "ln2_b")
    small_out = _adamw_small(
        tot, chip1, (lb_logits, gate_norm_w, conv_w[0], ln1_g, ln1_b, ln2_g, ln2_b),
        (m_lb_logits, m_gate_norm_w, m_conv_w[0], m_ln1_g, m_ln1_b, m_ln2_g, m_ln2_b),
        (v_lb_logits, v_gate_norm_w, v_conv_w[0], v_ln1_g, v_ln1_b, v_ln2_g, v_ln2_b), join.token)
    g_w_in, = join.wait(small_out[0])
    (g_w_in, d_in, nm_in, nv_in), = _adamw("adamw_w_in", [(w_in[0], g_w_in, m_w_in[0], v_w_in[0])])
    loss = small_out[4 * len(small)][0, 0]

    def results(n_kind, large):
        out = dict(zip(small, small_out[n_kind * len(small):(n_kind + 1) * len(small)]))
        out["conv_w"] = out["conv_w"][None]
        out.update({name: a[None] for name, a in zip(("w_in", "w_out", "w_ff1", "w_ff2"), large)})
        return [out[name] for name in ("w_in", "lb_logits", "gate_norm_w", "conv_w", "w_out", "ln1_g", "ln1_b",
                                       "w_ff1", "w_ff2", "ln2_g", "ln2_b")]

    return (loss, grad_x[None], *results(0, (g_w_in, g_w_out, g_w_ff1, g_w_ff2)),
            *results(1, (d_in, d_out, d_ff1, d_ff2)), *results(2, (nm_in, nm_out, nm_ff1, nm_ff2)),
            *results(3, (nv_in, nv_out, nv_ff1, nv_ff2)))
```

```python
import jax
import jax.numpy as jnp
from jax import lax
from jax.experimental import pallas as pl
from jax.experimental.pallas import tpu as pltpu

F32 = jnp.float32
BF16 = jnp.bfloat16
MXU_DTYPE = jnp.bfloat16

D_MODEL = 1024
HGRN_WIDTH = 512
HEAD_DIM = 128
N_HEADS = 4
CONV_WIDTH = 512
CHUNK = 64
D_FF = 4096
IN_COLS = 3584
GROUP = 512
N_GROUPS = IN_COLS // GROUP
ALPHA = 2.0 ** 0.25
EPS = 1e-5
N_CHIPS = 4
ADAM_LR, ADAM_B1, ADAM_B2, ADAM_EPS, ADAM_WD, ADAM_STEP = 0.001, 0.9, 0.999, 1e-08, 0.01, 10

LANES = 128
SUBLANES = 8
VMEM_LIMIT = 56 * 1024 * 1024
FF_BLOCK = 1024
N_FF = D_FF // FF_BLOCK
GATE_STRIP = 64

NN = (((1,), (0,)), ((), ()))
NT = (((1,), (1,)), ((), ()))
TN = (((0,), (0,)), ((), ()))
MESH = pl.DeviceIdType.MESH
ANY = pl.BlockSpec(memory_space=pl.ANY)


def _dot(a, b, dims):
    return lax.dot_general(a.astype(MXU_DTYPE), b.astype(MXU_DTYPE), dims, preferred_element_type=F32)


def _dot_exact(ones, v):
    ones = ones.astype(jnp.bfloat16)
    hi = v.astype(jnp.bfloat16)
    rest = v - hi.astype(F32)
    mid = rest.astype(jnp.bfloat16)
    low = (rest - mid.astype(F32)).astype(jnp.bfloat16)
    return sum(lax.dot_general(ones, part, NN, preferred_element_type=F32) for part in (hi, mid, low))


def _params(*sem):
    return pltpu.CompilerParams(dimension_semantics=sem, vmem_limit_bytes=VMEM_LIMIT)


def _resident(shape):
    return pl.BlockSpec(shape, lambda *_: (0,) * len(shape), pipeline_mode=pl.Buffered(1))


def _sigmoid(v):
    return 1.0 / (1.0 + jnp.exp(-v))


def _lower_bound(lbl):
    m = jnp.max(lbl, axis=0, keepdims=True)
    e = jnp.exp(lbl - m)
    s = e / jnp.sum(e, axis=0, keepdims=True)
    return s[0:1, :], s[1:2, :]


def _heads(v):
    return [v[:, h * HEAD_DIM:(h + 1) * HEAD_DIM] for h in range(N_HEADS)]


def _per_head(fn, *arrays):
    return jnp.concatenate([fn(*parts) for parts in zip(*map(_heads, arrays))], axis=1)


def _in_proj(x, w_in, conv_w, after):
    t = x.shape[0]
    tm = min(t, 512)

    def body(x_ref, w_ref, cw_ref, after_ref, o_ref, bcu_ref, xb_ref, y_ref, zbuf):
        @pl.when(pl.program_id(0) == 0)
        def _():
            zbuf[tm:tm + SUBLANES, :] = jnp.zeros((SUBLANES, CONV_WIDTH), F32)

        xb = x_ref[...].astype(xb_ref.dtype)
        xb_ref[...] = xb
        group = lambda g: _dot(xb, w_ref[:, g * GROUP:(g + 1) * GROUP], NN)
        for g in range(4):
            o_ref[g] = group(g)
        b_gate, c_gate, u = group(4), group(5), group(6)
        for n, part in enumerate((b_gate, c_gate, u)):
            bcu_ref[n] = part.astype(bcu_ref.dtype)
        zbuf[0:SUBLANES, :] = zbuf[tm:tm + SUBLANES, :]
        zbuf[SUBLANES:SUBLANES + tm, :] = c_gate * u
        cw = cw_ref[...]
        at = lambda shift: zbuf[shift:shift + tm, :]
        conv = cw[2:3, :] * at(SUBLANES) + cw[1:2, :] * at(SUBLANES - 1) + cw[0:1, :] * at(SUBLANES - 2)
        y_ref[...] = (b_gate * conv).astype(y_ref.dtype)

    return pl.pallas_call(
        body, name="in_proj", grid=(t // tm,),
        in_specs=[pl.BlockSpec((tm, D_MODEL), lambda i: (i, 0)), _resident((D_MODEL, IN_COLS)),
                  pl.BlockSpec((3, CONV_WIDTH), lambda i: (0, 0)), ANY],
        out_specs=[pl.BlockSpec((4, tm, GROUP), lambda i: (0, i, 0)), pl.BlockSpec((3, tm, GROUP), lambda i: (0, i, 0)),
                   pl.BlockSpec((tm, D_MODEL), lambda i: (i, 0)), pl.BlockSpec((tm, CONV_WIDTH), lambda i: (i, 0))],
        out_shape=[jax.ShapeDtypeStruct((4, t, GROUP), F32), jax.ShapeDtypeStruct((3, t, GROUP), BF16),
                   jax.ShapeDtypeStruct((t, D_MODEL), BF16), jax.ShapeDtypeStruct((t, CONV_WIDTH), BF16)],
        scratch_shapes=[pltpu.VMEM((tm + SUBLANES, CONV_WIDTH), F32)],
        compiler_params=_params("arbitrary"),
    )(x, w_in, conv_w, after)


def _gates(fp, lb):
    sig = _sigmoid(fp)
    f = lb + (1.0 - lb) * sig
    return sig, f, jnp.log(f), 1.0 - f


def _chunk_masks():
    row = lax.broadcasted_iota(jnp.int32, (CHUNK, CHUNK), 0)
    col = lax.broadcasted_iota(jnp.int32, (CHUNK, CHUNK), 1)
    return row >= col, row <= col


def _hgrn_fwd(proj, lb_logits, after):
    t = proj.shape[1]
    tb = min(t, 512)
    ncb = tb // CHUNK

    def body(q_ref, f_ref, v_ref, lbl_ref, after_ref, o_ref, st_ref, s_scr):
        @pl.when(pl.program_id(0) == 0)
        def _():
            s_scr[...] = jnp.zeros_like(s_scr)

        lb, _ = _lower_bound(lbl_ref[...])
        causal, _ = _chunk_masks()

        every = range(ncb)
        rows = [slice(c * CHUNK, (c + 1) * CHUNK) for c in every]
        q, v = [q_ref[r, :] for r in rows], [v_ref[r, :] for r in rows]
        gates = [_gates(f_ref[r, :], lb) for r in rows]
        k = [gt[3] for gt in gates]
        b = [_dot_exact(causal, gt[2]) for gt in gates]
        mid, last = [x[CHUNK // 2:CHUNK // 2 + 1, :] for x in b], [x[CHUNK - 1:CHUNK, :] for x in b]
        qt = [q[c] * jnp.exp(b[c] - mid[c]) for c in every]
        kt = [k[c] * jnp.exp(mid[c] - b[c]) for c in every]
        qi = [q[c] * jnp.exp(b[c]) for c in every]
        ks = [k[c] * jnp.exp(last[c] - b[c]) for c in every]
        dec = [jnp.exp(x) for x in last]
        scores = [[jnp.where(causal, _dot(a, b_, NT), 0.0) for a, b_ in zip(_heads(qt[c]), _heads(kt[c]))] for c in every]
        intra = [[_dot(s, v_h, NN) for s, v_h in zip(scores[c], _heads(v[c]))] for c in every]
        update = [_per_head(lambda v_h, ks_h: _dot(v_h, ks_h, TN), v[c], ks[c]) for c in every]

        st = s_scr[...]
        states = []
        for c in every:
            states.append(st)
            st_ref[c] = st
            st = dec[c] * st + update[c]
        s_scr[...] = st

        o_ref[...] = jnp.concatenate(
            [jnp.concatenate([i_h + _dot(qi_h, st_h, NT) for i_h, qi_h, st_h in
                              zip(intra[c], _heads(qi[c]), _heads(states[c]))], axis=1) for c in every], axis=0)

    grp = lambda g: pl.BlockSpec((None, tb, GROUP), lambda i: (g, i, 0))
    return pl.pallas_call(
        body, name="hgrn_fwd", grid=(t // tb,),
        in_specs=[grp(0), grp(1), grp(2), pl.BlockSpec((2, HGRN_WIDTH), lambda i: (0, 0)), ANY],
        out_specs=[pl.BlockSpec((tb, HGRN_WIDTH), lambda i: (i, 0)),
                   pl.BlockSpec((ncb, HEAD_DIM, HGRN_WIDTH), lambda i: (i, 0, 0))],
        out_shape=[jax.ShapeDtypeStruct((t, HGRN_WIDTH), F32),
                   jax.ShapeDtypeStruct((t // CHUNK, HEAD_DIM, HGRN_WIDTH), F32)],
        scratch_shapes=[pltpu.VMEM((HEAD_DIM, HGRN_WIDTH), F32)],
        compiler_params=_params("arbitrary"),
    )(proj, proj, proj, lb_logits, after)


def _gate_fwd(proj, o, gate_norm_w, after):
    t = proj.shape[1]
    tb = min(t, 1024)

    def body(o_ref, og_ref, gnw_ref, after_ref, out_ref):
        gnw = gnw_ref[...]
        for s in range(tb // GATE_STRIP):
            rows = slice(s * GATE_STRIP, (s + 1) * GATE_STRIP)
            og = og_ref[rows, :]
            on = _per_head(lambda o_h: o_h * lax.rsqrt(jnp.mean(o_h * o_h, axis=-1, keepdims=True) + EPS), o_ref[rows, :])
            out_ref[rows, :] = (on * gnw * (og * _sigmoid(og))).astype(out_ref.dtype)

    tile = pl.BlockSpec((tb, GROUP), lambda i: (i, 0))
    return pl.pallas_call(
        body, name="gate_fwd", grid=(t // tb,),
        in_specs=[tile, pl.BlockSpec((None, tb, GROUP), lambda i: (3, i, 0)), pl.BlockSpec((1, GROUP), lambda i: (0, 0)), ANY],
        out_specs=tile,
        out_shape=jax.ShapeDtypeStruct((t, HGRN_WIDTH), BF16),
        compiler_params=_params("parallel"),
    )(o, proj, gate_norm_w, after)


def _ln_bwd(dy, xhat, rstd, g):
    dxhat = dy * g
    m1 = jnp.mean(dxhat, axis=-1, keepdims=True)
    m2 = jnp.mean(dxhat * xhat, axis=-1, keepdims=True)
    return rstd * (dxhat - m1 - xhat * m2)


def _layer_norm(pre):
    xc = pre - jnp.mean(pre, axis=-1, keepdims=True)
    rstd = lax.rsqrt(jnp.mean(xc * xc, axis=-1, keepdims=True) + EPS)
    return xc * rstd, rstd


def _sublayers(cat_h, cat_c, x, target, w_out, w_ff1, w_ff2, g1, b1, g2, b2):
    t = x.shape[0]
    tm = min(t, 256)

    def body(ch_ref, cc_ref, x_ref, tg_ref, wo_ref, w1_ref, w2_ref, g1_ref, b1_ref, g2_ref, b2_ref,
             h1_ref, r_ref, da_ref, dp2b_ref, dp1_ref, dcat_ref, dg1_ref, db1_ref, dg2_ref, db2_ref, loss_ref, gwo_ref,
             gwo_acc, gwo_narrow, sem):
        @pl.when(pl.program_id(0) == 0)
        def _():
            for ref in (dg1_ref, db1_ref, dg2_ref, db2_ref, loss_ref, gwo_acc):
                ref[...] = jnp.zeros_like(ref)

        mix = _dot(ch_ref[...], wo_ref[0:GROUP, :], NN) + _dot(cc_ref[...], wo_ref[GROUP:2 * GROUP, :], NN)
        xhat1, rstd1 = _layer_norm(ALPHA * x_ref[...] + mix)
        h1 = xhat1 * g1_ref[...] + b1_ref[...]
        h1b = h1.astype(h1_ref.dtype)
        h1_ref[...] = h1b
        mlp = jnp.zeros((tm, D_MODEL), F32)
        for j in range(N_FF):
            cols = slice(j * FF_BLOCK, (j + 1) * FF_BLOCK)
            r = jnp.square(jnp.maximum(_dot(h1b, w1_ref[:, cols], NN), 0.0)).astype(r_ref.dtype)
            r_ref[:, cols] = r
            mlp = mlp + _dot(r, w2_ref[cols, :], NN)
        xhat2, rstd2 = _layer_norm(ALPHA * h1 + mlp)
        err = xhat2 * g2_ref[...] + b2_ref[...] - tg_ref[...]
        loss_ref[...] += 0.5 * jnp.sum(jnp.mean(err * err, axis=-1, keepdims=True))
        dy = err * (1.0 / D_MODEL)
        dg2_ref[...] += jnp.sum(dy * xhat2, axis=0, keepdims=True)
        db2_ref[...] += jnp.sum(dy, axis=0, keepdims=True)
        dp2 = _ln_bwd(dy, xhat2, rstd2, g2_ref[...])
        dp2b = dp2.astype(dp2b_ref.dtype)
        dp2b_ref[...] = dp2b
        back = jnp.zeros((tm, D_MODEL), F32)
        for j in range(N_FF):
            cols = slice(j * FF_BLOCK, (j + 1) * FF_BLOCK)
            dr = _dot(dp2b, w2_ref[cols, :], NT)
            da = (dr * (2.0 * jnp.sqrt(r_ref[:, cols].astype(F32)))).astype(da_ref.dtype)
            da_ref[:, cols] = da
            back = back + _dot(da, w1_ref[:, cols], NT)
        dh1 = ALPHA * dp2 + back
        dg1_ref[...] += jnp.sum(dh1 * xhat1, axis=0, keepdims=True)
        db1_ref[...] += jnp.sum(dh1, axis=0, keepdims=True)
        dp1 = _ln_bwd(dh1, xhat1, rstd1, g1_ref[...])
        dp1b = dp1.astype(MXU_DTYPE)
        dp1_ref[...] = dp1
        dcat_ref[...] = _dot(dp1b, wo_ref[...], NT)
        gwo_acc[0:GROUP, :] += _dot(ch_ref[...], dp1b, TN)
        gwo_acc[GROUP:2 * GROUP, :] += _dot(cc_ref[...], dp1b, TN)

        @pl.when(pl.program_id(0) == pl.num_programs(0) - 1)
        def _():
            gwo_narrow[...] = gwo_acc[...].astype(gwo_narrow.dtype)
            copy = pltpu.make_async_copy(gwo_narrow, gwo_ref, sem.at[0])
            copy.start()
            copy.wait()

    row = pl.BlockSpec((tm, D_MODEL), lambda i: (i, 0))
    wide = pl.BlockSpec((tm, D_FF), lambda i: (i, 0))
    vec = pl.BlockSpec((1, D_MODEL), lambda i: (0, 0))
    narrow = lambda dtype: jax.ShapeDtypeStruct((t, D_MODEL), dtype)
    return pl.pallas_call(
        body, name="sublayers", grid=(t // tm,),
        in_specs=[pl.BlockSpec((tm, GROUP), lambda i: (i, 0)), pl.BlockSpec((tm, GROUP), lambda i: (i, 0)), row, row,
                  _resident((D_MODEL, D_MODEL)),
                  _resident((D_MODEL, D_FF)), _resident((D_FF, D_MODEL)), vec, vec, vec, vec],
        out_specs=[row, wide, wide, row, row, row, vec, vec, vec, vec,
                   pl.BlockSpec((SUBLANES, LANES), lambda i: (0, 0)), ANY],
        out_shape=[narrow(BF16), jax.ShapeDtypeStruct((t, D_FF), BF16), jax.ShapeDtypeStruct((t, D_FF), BF16),
                   narrow(BF16), narrow(F32), narrow(F32)]
                  + [jax.ShapeDtypeStruct((1, D_MODEL), F32)] * 4
                  + [jax.ShapeDtypeStruct((SUBLANES, LANES), F32), jax.ShapeDtypeStruct((D_MODEL, D_MODEL), BF16)],
        scratch_shapes=[pltpu.VMEM((D_MODEL, D_MODEL), F32), pltpu.VMEM((D_MODEL, D_MODEL), BF16),
                        pltpu.SemaphoreType.DMA((1,))],
        compiler_params=_params("arbitrary"),
    )(cat_h, cat_c, x, target, w_out, w_ff1, w_ff2, g1, b1, g2, b2)


def _hgrn_bwd(proj, do, states, lb_logits, after):
    t = proj.shape[1]
    tb = min(t, 512)
    ncb = tb // CHUNK
    nblk = t // tb

    def body(q_ref, f_ref, v_ref, do_ref, st_ref, lbl_ref, after_ref, dp_ref, dlbl_ref, ds_scr, dlb_scr):
        i = pl.program_id(0)

        @pl.when(i == 0)
        def _():
            ds_scr[...] = jnp.zeros_like(ds_scr)
            dlb_scr[...] = jnp.zeros_like(dlb_scr)

        lb, s1 = _lower_bound(lbl_ref[...])
        causal, anti = _chunk_masks()
        every = range(ncb)
        rows = [slice(c * CHUNK, (c + 1) * CHUNK) for c in every]
        q, v, do = ([ref[r, :] for r in rows] for ref in (q_ref, v_ref, do_ref))
        st = [st_ref[c] for c in every]
        gates = [_gates(f_ref[r, :], lb) for r in rows]
        sig, f, k = ([gt[n] for gt in gates] for n in (0, 1, 3))
        b = [_dot_exact(causal, gt[2]) for gt in gates]
        mid, last = [x[CHUNK // 2:CHUNK // 2 + 1, :] for x in b], [x[CHUNK - 1:CHUNK, :] for x in b]
        e_q = [jnp.exp(b[c] - mid[c]) for c in every]
        e_k = [jnp.exp(mid[c] - b[c]) for c in every]
        e_i = [jnp.exp(x) for x in b]
        e_s = [jnp.exp(last[c] - b[c]) for c in every]
        dec = [jnp.exp(x) for x in last]
        qt, kt, qi, ks = ([a[c] * e[c] for c in every] for a, e in ((q, e_q), (k, e_k), (q, e_i), (k, e_s)))

        def masked(a, b_):
            return [[jnp.where(causal, _dot(a_h, b_h, NT), 0.0) for a_h, b_h in zip(_heads(a[c]), _heads(b_[c]))]
                    for c in every]

        def with_scores(s, other, dims):
            return [jnp.concatenate([_dot(s_h, o_h, dims) for s_h, o_h in zip(s[c], _heads(other[c]))], axis=1)
                    for c in every]

        def per_head(dims, a, b_):
            return [_per_head(lambda a_h, b_h: _dot(a_h, b_h, dims), a[c], b_[c]) for c in every]

        scores, dscores = masked(qt, kt), masked(do, v)
        dqt, dkt, dv_intra = with_scores(dscores, kt, NN), with_scores(dscores, qt, TN), with_scores(scores, do, TN)
        dqi, update = per_head(NN, do, st), per_head(TN, do, qi)

        dst = ds_scr[...]
        dsts = [None] * ncb
        for c in reversed(every):
            dsts[c] = dst
            dst = dec[c] * dst + update[c]
        ds_scr[...] = dst

        dv_state, dks = per_head(NT, ks, dsts), per_head(NN, v, dsts)
        ddec = [jnp.sum(dsts[c] * st[c], axis=0, keepdims=True) for c in every]
        dq = [dqt[c] * e_q[c] + dqi[c] * e_i[c] for c in every]
        dk = [dkt[c] * e_k[c] + dks[c] * e_s[c] for c in every]
        db = [q[c] * dq[c] - k[c] * dk[c] for c in every]
        db_last = [jnp.sum(dks[c] * ks[c], axis=0, keepdims=True) + ddec[c] * dec[c] for c in every]
        dg = [_dot_exact(anti, db[c]) + db_last[c] for c in every]
        df = [dg[c] / f[c] - dk[c] for c in every]
        dlb_scr[...] += sum(jnp.sum(df[c] * (1.0 - sig[c]), axis=0, keepdims=True) for c in every)
        dfp = [df[c] * (1.0 - lb) * sig[c] * (1.0 - sig[c]) for c in every]
        dv = [dv_intra[c] + dv_state[c] for c in every]
        for n, parts in enumerate((dq, dfp, dv)):
            dp_ref[n] = jnp.concatenate(parts, axis=0).astype(dp_ref.dtype)

        @pl.when(i == nblk - 1)
        def _():
            dlb = dlb_scr[...]
            dlbl_ref[0:1, :] = dlb * lb * (1.0 - lb)
            dlbl_ref[1:2, :] = -dlb * lb * s1

    grp = lambda g: pl.BlockSpec((None, tb, GROUP), lambda i: (g, nblk - 1 - i, 0))
    vec = pl.BlockSpec((2, HGRN_WIDTH), lambda i: (0, 0))
    return pl.pallas_call(
        body, name="hgrn_bwd", grid=(nblk,),
        in_specs=[grp(0), grp(1), grp(2), pl.BlockSpec((tb, HGRN_WIDTH), lambda i: (nblk - 1 - i, 0)),
                  pl.BlockSpec((ncb, HEAD_DIM, HGRN_WIDTH), lambda i: (nblk - 1 - i, 0, 0)), vec, ANY],
        out_specs=[pl.BlockSpec((3, tb, HGRN_WIDTH), lambda i: (0, nblk - 1 - i, 0)), vec],
        out_shape=[jax.ShapeDtypeStruct((3, t, HGRN_WIDTH), BF16), jax.ShapeDtypeStruct((2, HGRN_WIDTH), F32)],
        scratch_shapes=[pltpu.VMEM((HEAD_DIM, HGRN_WIDTH), F32), pltpu.VMEM((1, HGRN_WIDTH), F32)],
        compiler_params=_params("arbitrary"),
    )(proj, proj, proj, do, states, lb_logits, after)


GRAD_TILE = 512
OUT_PARTS = 4


class _Side:
    def __init__(self, operands, in_specs, out_shape, out_specs, scratch, init, begin):
        self.operands, self.in_specs, self.out_shape, self.out_specs = operands, in_specs, out_shape, out_specs
        self.scratch, self.init, self.begin = scratch, init, begin


RING = 3


def _grad_w(name, operands, widths, shape, step, after=None, side=None, ringed=None):
    t = operands[0].shape[-2]
    tt = min(t, GRAD_TILE)
    n_in, n_steps = len(operands), t // tt
    in_specs = [ANY if n == ringed else pl.BlockSpec((tt, w), lambda k: (k, 0)) if a.ndim == 2 else
                pl.BlockSpec((a.shape[0], tt, w), lambda k: (0, k, 0)) for n, (a, w) in enumerate(zip(operands, widths))]
    extra = [] if after is None else [after]
    s_in, s_out = (len(side.operands), len(side.out_shape)) if side else (0, 0)
    first_out = n_in + s_in + len(extra)
    side_scratch = side.scratch if side else []
    ring_scratch = [] if ringed is None else [pltpu.VMEM((RING, tt, widths[ringed]), operands[ringed].dtype),
                                              pltpu.SemaphoreType.DMA((RING,))]

    def body(*refs):
        o_ref, side_outs = refs[first_out], refs[first_out + 1:first_out + 1 + s_out]
        acc, narrow, sem = refs[first_out + 1 + s_out:first_out + 4 + s_out]
        first_side = first_out + 4 + s_out
        k = pl.program_id(0)
        tiles = list(refs[:n_in])
        if ringed is not None:
            ring, ring_sems = refs[first_side + len(side_scratch):]

            def fetch(tile, slot):
                return pltpu.make_async_copy(refs[ringed].at[pl.ds(tile * tt, tt), :], ring.at[slot], ring_sems.at[slot])

            @pl.when(k == 0)
            def _():
                for first in range(min(RING - 1, n_steps)):
                    fetch(first, first).start(priority=1)

            @pl.when(k + RING - 1 < n_steps)
            def _():
                fetch(k + RING - 1, lax.rem(k + RING - 1, RING)).start(priority=1)

        @pl.when(k == 0)
        def _():
            acc[...] = jnp.zeros_like(acc)
            if side:
                side.init(side_outs)

        if ringed is not None:
            slot = lax.rem(k, RING)
            fetch(k, slot).wait()
            tiles[ringed] = ring.at[slot]

        tick = (side.begin(k, n_steps, refs[n_in:n_in + s_in], side_outs, refs[first_side:first_side + len(side_scratch)])
                if side else None)
        step(acc, *tiles, tick or (lambda j: None))

        @pl.when(k == n_steps - 1)
        def _():
            part = shape[0] // OUT_PARTS
            copies = []
            for p in range(OUT_PARTS):
                rows = pl.ds(p * part, part)
                narrow[rows, :] = acc[rows, :].astype(narrow.dtype)
                copies.append(pltpu.make_async_copy(narrow.at[rows, :], o_ref.at[rows, :], sem.at[p]))
                copies[-1].start()
            for cp in copies:
                cp.wait()

    outs = pl.pallas_call(
        body, name=name, grid=(n_steps,),
        in_specs=in_specs + (side.in_specs if side else []) + [ANY] * len(extra),
        out_specs=[ANY] + (side.out_specs if side else []),
        out_shape=[jax.ShapeDtypeStruct(shape, BF16)] + (side.out_shape if side else []),
        scratch_shapes=[pltpu.VMEM(shape, F32), pltpu.VMEM(shape, BF16), pltpu.SemaphoreType.DMA((OUT_PARTS,))]
                       + side_scratch + ring_scratch,
        compiler_params=_params("arbitrary"),
    )(*operands, *(side.operands if side else ()), *extra)
    return outs if side else outs[0]


def _dw_in(xb, dph, dog, dpc, w_in, dpre1, after):
    t = xb.shape[0]

    def step(acc, x_ref, dh_ref, dog_ref, dc_ref, tick):
        xv = x_ref[...]
        for g in range(N_GROUPS):
            part = dh_ref[g] if g < 3 else dog_ref[...] if g == 3 else dc_ref[g - 4]
            acc[:, g * GROUP:(g + 1) * GROUP] += _dot(xv, part, TN)
            tick(g, part)

    def begin(k, n_steps, ins, outs, scratch):
        w_ref, dp_ref = ins
        total = [ALPHA * dp_ref[...]]

        def tick(g, part):
            total[0] = total[0] + _dot(part, w_ref[:, g * GROUP:(g + 1) * GROUP], NT)
            if g == N_GROUPS - 1:
                outs[0][...] = total[0]

        return tick

    row = pl.BlockSpec((min(t, GRAD_TILE), D_MODEL), lambda k: (k, 0))
    side = _Side((w_in, dpre1), [_resident((D_MODEL, IN_COLS)), row], [jax.ShapeDtypeStruct((t, D_MODEL), F32)], [row],
                 [], lambda outs: None, begin)
    return _grad_w("dw_in", (xb, dph, dog, dpc), (D_MODEL, GROUP, GROUP, GROUP), (D_MODEL, IN_COLS), step, after, side)


def _strips_of(j, tt):
    per_tick = tt // GATE_STRIP // N_FF
    return [slice(s * GATE_STRIP, (s + 1) * GATE_STRIP) for s in range(j * per_tick, (j + 1) * per_tick)]


def _dw_ff1(h1b, da, dcat, o, proj, gate_norm_w, after):
    t = h1b.shape[0]
    tt = min(t, GRAD_TILE)

    def step(acc, h_ref, da_ref, tick):
        hv = h_ref[...]
        for j in range(N_FF):
            cols = slice(j * FF_BLOCK, (j + 1) * FF_BLOCK)
            acc[:, cols] += _dot(hv, da_ref[:, cols], TN)
            tick(j)

    def init(outs):
        outs[2][...] = jnp.zeros_like(outs[2])

    def begin(k, n_steps, ins, outs, scratch):
        do2_ref, o_ref, og_ref, gnw_ref = ins
        do_ref, dog_ref, dgnw_ref = outs
        total = [jnp.zeros((GATE_STRIP, GROUP), F32)]

        def tick(j):
            gnw = gnw_ref[...]
            for rows in _strips_of(j, tt):
                ov, og, do2 = o_ref[rows, :], og_ref[rows, :], do2_ref[rows, :]
                rs = _per_head(lambda o_h: jnp.broadcast_to(
                    lax.rsqrt(jnp.mean(o_h * o_h, axis=-1, keepdims=True) + EPS), o_h.shape), ov)
                on = ov * rs
                sg = _sigmoid(og)
                sil = og * sg
                don = do2 * gnw * sil
                total[0] = total[0] + do2 * on * sil
                dog_ref[rows, :] = (do2 * on * gnw * (sg * (1.0 + og * (1.0 - sg)))).astype(dog_ref.dtype)
                do_ref[rows, :] = rs * (don - on * _per_head(
                    lambda p_h: jnp.broadcast_to(jnp.mean(p_h, axis=-1, keepdims=True), p_h.shape), don * on))
            if j == N_FF - 1:
                dgnw_ref[...] += jnp.sum(total[0], axis=0, keepdims=True)

        return tick

    tile = pl.BlockSpec((tt, GROUP), lambda k: (k, 0))
    vec = pl.BlockSpec((1, GROUP), lambda k: (0, 0))
    side = _Side(
        (dcat, o, proj, gate_norm_w), [tile, tile, pl.BlockSpec((None, tt, GROUP), lambda k: (3, k, 0)), vec],
        [jax.ShapeDtypeStruct((t, HGRN_WIDTH), F32), jax.ShapeDtypeStruct((t, HGRN_WIDTH), BF16),
         jax.ShapeDtypeStruct((1, HGRN_WIDTH), F32)], [tile, tile, vec], [], init, begin)
    return _grad_w("dw_ff1", (h1b, da), (D_MODEL, D_FF), (D_MODEL, D_FF), step, after, side, ringed=1)


def _dw_ff2(r, dpre2b, dcat, bcu, conv_w):
    t = r.shape[0]
    tt = min(t, GRAD_TILE)
    hb = tt // SUBLANES
    halo = 2 * SUBLANES

    def step(acc, r_ref, d_ref, tick):
        dv = d_ref[...]
        for j in range(N_FF):
            rows = slice(j * FF_BLOCK, (j + 1) * FF_BLOCK)
            acc[rows, :] += _dot(r_ref[:, rows], dv, TN)
            tick(j)

    def init(outs):
        outs[1][...] = jnp.zeros_like(outs[1])

    def begin(k, n_steps, ins, outs, scratch):
        dy_ref, dyn_ref, b_ref, bn_ref, c_ref, u_ref, ch_ref, uh_ref, cw_ref = ins
        dp_ref, dcw_ref = outs
        zbuf, dbuf = scratch
        before = lambda ref: ref[SUBLANES:halo, :].astype(F32)
        zbuf[0:SUBLANES, :] = jnp.where(k > 0, before(ch_ref) * before(uh_ref), 0.0)
        zbuf[SUBLANES:SUBLANES + tt, :] = c_ref[...].astype(F32) * u_ref[...].astype(F32)
        dbuf[0:tt, :] = dy_ref[...] * b_ref[...].astype(F32)
        dbuf[tt:tt + SUBLANES, :] = jnp.where(k < n_steps - 1, dyn_ref[...] * bn_ref[0:SUBLANES, :].astype(F32), 0.0)
        totals = [jnp.zeros((GATE_STRIP, GROUP), F32) for _ in range(3)]

        def tick(j):
            cw = cw_ref[...]
            for rows in _strips_of(j, tt):
                at = lambda buf, shift: buf[shift + rows.start:shift + rows.stop, :]
                z, z1, z2 = at(zbuf, SUBLANES), at(zbuf, SUBLANES - 1), at(zbuf, SUBLANES - 2)
                dyc, d1, d2 = at(dbuf, 0), at(dbuf, 1), at(dbuf, 2)
                yc = cw[2:3, :] * z + cw[1:2, :] * z1 + cw[0:1, :] * z2
                dz = cw[2:3, :] * dyc + cw[1:2, :] * d1 + cw[0:1, :] * d2
                dp_ref[0, rows, :] = (dy_ref[rows, :] * yc).astype(dp_ref.dtype)
                dp_ref[1, rows, :] = (dz * u_ref[rows, :].astype(F32)).astype(dp_ref.dtype)
                dp_ref[2, rows, :] = (dz * c_ref[rows, :].astype(F32)).astype(dp_ref.dtype)
                for n, tap in enumerate((z2, z1, z)):
                    totals[n] = totals[n] + dyc * tap
            if j == N_FF - 1:
                for n in range(3):
                    dcw_ref[n:n + 1, :] += jnp.sum(totals[n], axis=0, keepdims=True)

        return tick

    grp = lambda g: pl.BlockSpec((None, tt, GROUP), lambda k: (g, k, 0))
    prev = lambda g: pl.BlockSpec((None, halo, GROUP), lambda k: (g, jnp.maximum(k * (tt // halo) - 1, 0), 0))
    nxt = lambda g: pl.BlockSpec((None, halo, GROUP), lambda k: (g, jnp.minimum((k + 1) * (tt // halo), t // halo - 1), 0))
    nxt_row = lambda k: jnp.minimum((k + 1) * hb, t // SUBLANES - 1)
    whole = pl.BlockSpec((3, CONV_WIDTH), lambda k: (0, 0))
    side = _Side(
        (dcat, dcat, bcu, bcu, bcu, bcu, bcu, bcu, conv_w),
        [pl.BlockSpec((tt, GROUP), lambda k: (k, 1)), pl.BlockSpec((SUBLANES, GROUP), lambda k: (nxt_row(k), 1)),
         grp(0), nxt(0), grp(1), grp(2), prev(1), prev(2), whole],
        [jax.ShapeDtypeStruct((3, t, CONV_WIDTH), BF16), jax.ShapeDtypeStruct((3, CONV_WIDTH), F32)],
        [pl.BlockSpec((3, tt, GROUP), lambda k: (0, k, 0)), whole],
        [pltpu.VMEM((tt + SUBLANES, GROUP), F32), pltpu.VMEM((tt + SUBLANES, GROUP), F32)], init, begin)
    return _grad_w("dw_ff2", (r, dpre2b), (D_FF, D_MODEL), (D_FF, D_MODEL), step, side=side, ringed=0)


def _place():
    x, y, c = lax.axis_index("x"), lax.axis_index("y"), lax.axis_index("c")
    return x, y, c, 2 * x + y


def _other_chips(x, y):
    return [(1 - x, y), (x, 1 - y), (1 - x, 1 - y)]


def _place_shard(name, w, chip, cols_sharded, after=None):
    rows, cols = w.shape
    tr = min(rows, 256)
    nb = rows // tr
    full = (rows, cols * N_CHIPS) if cols_sharded else (rows * N_CHIPS, cols)
    out_map = (lambda i, s: (i, s[0])) if cols_sharded else (lambda i, s: (s[0] * nb + i, 0))

    def body(s_ref, w_ref, *rest):
        rest[-1][...] = w_ref[...].astype(rest[-1].dtype)

    extra = [] if after is None else [after]
    return pl.pallas_call(
        body, name=name,
        grid_spec=pltpu.PrefetchScalarGridSpec(
            num_scalar_prefetch=1, grid=(nb,),
            in_specs=[pl.BlockSpec((tr, cols), lambda i, s: (i, 0))] + [ANY] * len(extra),
            out_specs=pl.BlockSpec((tr, cols), out_map)),
        out_shape=jax.ShapeDtypeStruct(full, BF16),
        compiler_params=_params("parallel"),
    )(chip, w, *extra)


HBM = pl.BlockSpec(memory_space=pltpu.HBM)
SEM = pl.BlockSpec(memory_space=pltpu.SEMAPHORE)
EFFECT = pltpu.SideEffectType.DATAFLOW_SIDE_EFFECTING


PEER_SETS = {
    "sibling": (0, lambda x, y, c: [(x, y, 1 - c)]),
    "chips": (1, lambda x, y, c: [(1 - x, y, c), (x, 1 - y, c), (1 - x, 1 - y, c)]),
    "neighbours": (2, lambda x, y, c: [(1 - x, y, c), (x, 1 - y, c)]),
}


class _Split:
    def __init__(self, name, arrays, plan, others=(), peers=None, prepare=None, sources=(), scratch=()):
        n_own, arrays = len(arrays), (*arrays, *others)
        n, n_copies, n_in = len(arrays), plan.count, len(arrays) + len(sources)
        self.name, self.plan, self.n = name, plan, n_own
        barrier_id, peer_ids = PEER_SETS[peers] if peers else (None, None)

        def body(*refs):
            send_sems, recv_sems, token = refs[n_in], refs[n_in + 1], refs[n_in + 2 + n]
            if peers:
                x, y, c, _ = _place()
                barrier = pltpu.get_barrier_semaphore()
                for peer in peer_ids(x, y, c):
                    pl.semaphore_signal(barrier, inc=1, device_id=peer, device_id_type=MESH)
            if prepare:
                prepare(refs[:n], refs[n:n_in], refs[n_in + 3 + n:])
            if peers:
                pl.semaphore_wait(barrier, len(peer_ids(0, 0, 0)))
            for k, (src, dst, to) in enumerate(plan(refs[:n])):
                pltpu.make_async_remote_copy(src_ref=src, dst_ref=dst, send_sem=send_sems.at[k], recv_sem=recv_sems.at[k],
                                             device_id=to, device_id_type=MESH).start()
            token[...] = jnp.zeros_like(token)

        outs = pl.pallas_call(
            body, name=name + "_start",
            out_shape=(pltpu.SemaphoreType.DMA((n_copies,)), pltpu.SemaphoreType.DMA((n_copies,)),
                       *[pltpu.HBM(a.shape, a.dtype) for a in arrays], jax.ShapeDtypeStruct((SUBLANES, LANES), F32)),
            in_specs=(HBM,) * n_in, out_specs=(SEM, SEM) + (HBM,) * n + (pl.BlockSpec(memory_space=pltpu.VMEM),),
            input_output_aliases={i: 2 + i for i in range(n)}, scratch_shapes=list(scratch),
            compiler_params=pltpu.CompilerParams(has_side_effects=EFFECT, collective_id=barrier_id),
        )(*[pltpu.with_memory_space_constraint(a, pltpu.HBM) for a in (*arrays, *sources)])
        self.sems, self.arrays, self.others, self.token = outs[:2], outs[2:2 + n_own], outs[2 + n_own:2 + n], outs[-1]
        self.waited = set()

    def wait(self, after, copies=None, part=""):
        n, plan = self.n, self.plan
        mine = set(range(plan.count) if copies is None else copies) - self.waited
        self.waited |= mine

        def body(*refs):
            send_sems, recv_sems = refs[n], refs[n + 1]
            for k, (src, dst, to) in enumerate(plan(refs[:n])):
                if k in mine:
                    cp = pltpu.make_async_remote_copy(src_ref=src, dst_ref=dst, send_sem=send_sems.at[k],
                                                      recv_sem=recv_sems.at[k], device_id=to, device_id_type=MESH)
                    cp.wait_send()
                    cp.wait_recv()

        self.arrays = pl.pallas_call(
            body, name=self.name + "_wait" + part, out_shape=tuple(pltpu.HBM(a.shape, a.dtype) for a in self.arrays),
            in_specs=(HBM,) * n + (SEM, SEM, ANY), out_specs=(HBM,) * n, input_output_aliases={i: i for i in range(n)},
            compiler_params=pltpu.CompilerParams(has_side_effects=EFFECT),
        )(*self.arrays, *self.sems, after)
        return self.arrays


COLS_SHARDED = (True, False, True, False)
HALF_SHAPES = [(D_MODEL // 2, IN_COLS), (D_MODEL, D_MODEL // 2), (D_MODEL // 2, D_FF), (D_FF, D_MODEL // 2)]
PIECE_SHAPES = [(D_MODEL // 2, IN_COLS // N_CHIPS), (D_MODEL // N_CHIPS, D_MODEL // 2),
                (D_MODEL // 2, D_FF // N_CHIPS), (D_FF // N_CHIPS, D_MODEL // 2)]


def _shard_view(kind, ref, chip):
    if COLS_SHARDED[kind]:
        n = ref.shape[1] // N_CHIPS
        return ref.at[:, pl.ds(chip * n, n)]
    n = ref.shape[0] // N_CHIPS
    return ref.at[pl.ds(chip * n, n), :]


def _half_view(kind, ref, h):
    if COLS_SHARDED[kind]:
        n = ref.shape[0] // 2
        return ref.at[pl.ds(h * n, n), :]
    n = ref.shape[1] // 2
    return ref.at[:, pl.ds(h * n, n)]


def _plan(count):
    def mark(fn):
        fn.count = count
        return fn
    return mark


def _shard_rows_view(kind, ref, chip, part, n_parts):
    if COLS_SHARDED[kind]:
        m, n = ref.shape[0] // n_parts, ref.shape[1] // N_CHIPS
        return ref.at[pl.ds(part * m, m), pl.ds(chip * n, n)]
    m = ref.shape[0] // N_CHIPS // n_parts
    return ref.at[pl.ds((n_parts * chip + part) * m, m), :]


def _shard_half_view(kind, ref, chip, h):
    return _shard_rows_view(kind, ref, chip, h, 2)


def _gather_over_ici(kinds, weights):
    @_plan(2 * len(kinds))
    def plan(refs):
        x, y, c, me = _place()
        mine = [_shard_half_view(kind, ref, me, c) for kind, ref in zip(kinds, refs)]
        return [(v, v, to) for v in mine for to in ((1 - x, y, c), (x, 1 - y, c))]

    return _Split("gather_ici_" + "".join(map(str, kinds)), tuple(weights), plan, peers="neighbours")


def _relay_over_ici(kinds, weights, others=()):
    @_plan(2 * len(kinds))
    def plan(refs):
        x, y, c, _ = _place()
        x_nbr, y_nbr = 2 * (1 - x) + y, 2 * x + (1 - y)
        out = []
        for kind, ref in zip(kinds, refs):
            first, second = (_shard_rows_view(kind, ref, chip, 2 * c + q, 4) for q, chip in ((0, x_nbr), (1, y_nbr)))
            out += [(first, first, (x, 1 - y, c)), (second, second, (1 - x, y, c))]
        return out

    return _Split("relay_ici_" + "".join(map(str, kinds)), tuple(weights), plan, others, peers="neighbours")


def _gather_w_in_over_ici(shard, conv4):
    rows, cols = shard.shape

    @_plan(6)
    def plan(refs):
        x, y, c, me = _place()
        half, conv = _shard_half_view(0, refs[0], me, c), refs[1].at[me]
        return [(v, v, (px, py, c)) for v in (half, conv) for px, py in _other_chips(x, y)]

    def prepare(refs, sources, scratch):
        wide, narrow, sems = scratch
        _, _, _, me = _place()
        load = pltpu.make_async_copy(sources[0], wide, sems.at[0])
        load.start()
        load.wait()
        narrow[...] = wide[...].astype(narrow.dtype)
        store = pltpu.make_async_copy(narrow, _shard_view(0, refs[0], me), sems.at[1])
        store.start()
        store.wait()

    return _Split("gather_w_in_ici", (lax.empty((rows, cols * N_CHIPS), BF16), conv4), plan, peers="chips",
                  prepare=prepare, sources=(shard,),
                  scratch=(pltpu.VMEM((rows, cols), F32), pltpu.VMEM((rows, cols), BF16), pltpu.SemaphoreType.DMA((2,))))


def _gather_over_d2d(kinds, weights, relations=(0, 1, 2), part=""):
    @_plan(len(relations) * len(kinds))
    def plan(refs):
        x, y, c, _ = _place()
        chips = [_other_chips(x, y)[n] for n in relations]
        got = [_shard_half_view(kind, ref, 2 * px + py, c) for kind, ref in zip(kinds, refs) for px, py in chips]
        return [(v, v, (x, y, 1 - c)) for v in got]

    return _Split("gather_d2d_" + "".join(map(str, kinds)) + part, tuple(weights), plan, peers="sibling")


def _swap_halves(kinds, grads):
    @_plan(len(kinds))
    def plan(refs):
        x, y, c, _ = _place()
        return [(_half_view(kind, g, 1 - c), land, (x, y, 1 - c))
                for kind, g, land in zip(kinds, refs[:len(kinds)], refs[len(kinds):])]

    lands = [lax.empty(HALF_SHAPES[kind], g.dtype) for kind, g in zip(kinds, grads)]
    return _Split("swap_halves_" + "".join(map(str, kinds)), (*grads, *lands), plan, peers="sibling")


def _block_rows(cols, elements):
    return 1 << ((elements // cols).bit_length() - 1)


def _grid_steps(shapes, elements):
    rows, cols = max(shapes, key=lambda shape: shape[0] * shape[1])
    return rows // min(rows, _block_rows(cols, elements))


def _add_half(name, kinds, grads, recvs, core):
    n = len(kinds)
    shapes = [recv.shape for recv in recvs]
    nb = _grid_steps(shapes, 1 << 20)

    def body(c_ref, *refs):
        for g_ref, r_ref, o_ref in zip(refs[:n], refs[n:2 * n], refs[2 * n:]):
            o_ref[...] = (g_ref[...].astype(F32) + r_ref[...].astype(F32)).astype(o_ref.dtype)

    own = [pl.BlockSpec((rows // nb, cols), (lambda i, c_ref: (c_ref[0] * nb + i, 0)) if COLS_SHARDED[k] else
                        (lambda i, c_ref: (i, c_ref[0]))) for k, (rows, cols) in zip(kinds, shapes)]
    blocks = [pl.BlockSpec((rows // nb, cols), lambda i, c_ref: (i, 0)) for rows, cols in shapes]
    return pl.pallas_call(
        body, name=name,
        grid_spec=pltpu.PrefetchScalarGridSpec(
            num_scalar_prefetch=1, grid=(nb,), in_specs=own + blocks, out_specs=blocks),
        out_shape=[jax.ShapeDtypeStruct(shape, BF16) for shape in shapes],
        compiler_params=_params("parallel"),
    )(core, *grads, *recvs)


def _exchange_pieces(kinds, halves, pack=None):
    n_p, n = N_CHIPS - 1, len(kinds)

    @_plan(n_p * n + (0 if pack is None else N_DEV - 1))
    def plan(refs):
        x, y, c, _ = _place()
        copies = []
        if pack is not None:
            me = 4 * x + 2 * y + c
            peers = [((1 - x) if m & 4 else x, (1 - y) if m & 2 else y, (1 - c) if m & 1 else c) for m in range(1, N_DEV)]
            copies += [(refs[2 * n], refs[2 * n + 1].at[me], peer) for peer in peers]
        return copies + [(_shard_view(kind, half, 2 * px + py), land.at[j], (px, py, c))
                         for j, (px, py) in enumerate(_other_chips(x, y))
                         for kind, half, land in zip(kinds, refs[:n], refs[n:2 * n])]

    lands = [lax.empty((n_p,) + PIECE_SHAPES[kind], BF16) for kind in kinds]
    small = () if pack is None else (pack, lax.empty((N_DEV,) + pack.shape, F32))
    return _Split("exchange_pieces_" + "".join(map(str, kinds)), (*halves, *lands, *small), plan,
                  peers="chips" if pack is None else None)


def _sum_pieces(name, kinds, halves, slots, place, after):
    n, n_p = len(kinds), N_CHIPS - 1
    shapes = [slot.shape[1:] for slot in slots]
    nb = _grid_steps(shapes, 1 << 18)

    def body(s_ref, *refs):
        for own_ref, slot_ref, o_ref in zip(refs[:n], refs[n:2 * n], refs[2 * n + 1:]):
            total = own_ref[...].astype(F32)
            for j in range(n_p):
                total = total + slot_ref[j].astype(F32)
            o_ref[...] = total

    own, out, shards = [], [], []
    for k, (rows, cols) in zip(kinds, shapes):
        if COLS_SHARDED[k]:
            own_map, out_map, shard = (lambda i, s: (i, s[0])), (lambda i, s: (s[1] * nb + i, 0)), (2 * rows, cols)
        else:
            own_map, out_map, shard = (lambda i, s: (s[0] * nb + i, 0)), (lambda i, s: (i, s[1])), (rows, 2 * cols)
        own.append(pl.BlockSpec((rows // nb, cols), own_map))
        out.append(pl.BlockSpec((rows // nb, cols), out_map))
        shards.append(jax.ShapeDtypeStruct(shard, F32))
    landed = [pl.BlockSpec((n_p, rows // nb, cols), lambda i, s: (0, i, 0)) for rows, cols in shapes]
    return pl.pallas_call(
        body, name=name,
        grid_spec=pltpu.PrefetchScalarGridSpec(
            num_scalar_prefetch=1, grid=(nb,), in_specs=own + landed + [ANY], out_specs=out),
        out_shape=shards,
        compiler_params=_params("parallel"),
    )(place, *halves, *slots, after)


def _join_halves(kinds, shards):
    @_plan(len(kinds))
    def plan(refs):
        x, y, c, _ = _place()
        return [(_half_view(kind, g, c), _half_view(kind, g, c), (x, y, 1 - c)) for kind, g in zip(kinds, refs)]

    return _Split("join_halves_" + "".join(map(str, kinds)), tuple(shards), plan, peers="sibling")


N_DEV = 8


def _sum_shared(pack, land, device):
    def body(d_ref, p_ref, l_ref, o_ref):
        me = d_ref[0]
        total = jnp.where(me == 0, p_ref[...], l_ref[0])
        for d in range(1, N_DEV):
            total = total + jnp.where(me == d, p_ref[...], l_ref[d])
        o_ref[...] = total

    return pl.pallas_call(
        body, name="sum_shared",
        grid_spec=pltpu.PrefetchScalarGridSpec(
            num_scalar_prefetch=1, grid=(1,),
            in_specs=[pl.BlockSpec(pack.shape, lambda i, d: (0, 0)), pl.BlockSpec(land.shape, lambda i, d: (0, 0, 0))],
            out_specs=pl.BlockSpec(pack.shape, lambda i, d: (0, 0))),
        out_shape=jax.ShapeDtypeStruct(pack.shape, F32),
    )(device, pack, land)


def _adamw(name, weights, after=None):
    shapes = [w.shape for w, _, _, _ in weights]
    nb = _grid_steps(shapes, 1 << 18)
    extra = [] if after is None else [after]
    n_in = 4 * len(weights) + len(extra)

    def body(*refs):
        for k in range(len(weights)):
            w_ref, g_ref, m_ref, v_ref = refs[4 * k:4 * k + 4]
            go_ref, d_ref, nm_ref, nv_ref = refs[n_in + 4 * k:n_in + 4 * k + 4]
            g = g_ref[...]
            go_ref[...] = g
            d_ref[...], nm_ref[...], nv_ref[...] = _adam_step(w_ref[...], g, m_ref[...], v_ref[...])

    blocks = [pl.BlockSpec((rows // nb, cols), lambda i: (i, 0)) for rows, cols in shapes for _ in range(4)]
    outs = pl.pallas_call(
        body, name=name, grid=(nb,), in_specs=blocks + [ANY] * len(extra), out_specs=blocks,
        out_shape=[jax.ShapeDtypeStruct(shape, F32) for shape in shapes for _ in range(4)],
        compiler_params=_params("parallel"),
    )(*[a for group in weights for a in group], *extra)
    return [outs[4 * k:4 * k + 4] for k in range(len(weights))]


def _adam_step(w, g, m, v):
    nm = ADAM_B1 * m + (1.0 - ADAM_B1) * g
    nv = ADAM_B2 * v + (1.0 - ADAM_B2) * jnp.square(g)
    m_hat = nm * (1.0 / (1.0 - ADAM_B1 ** ADAM_STEP))
    v_hat = nv * (1.0 / (1.0 - ADAM_B2 ** ADAM_STEP))
    return -ADAM_LR * (m_hat / (jnp.sqrt(v_hat) + ADAM_EPS) + ADAM_WD * w), nm, nv


def _adamw_small(tot, chip, weights, ms, vs, after):
    n, half = len(weights), D_MODEL // 2

    def body(chip_ref, tot_ref, *refs):
        ins, outs = refs[:3 * n], refs[3 * n + 1:]
        tot = tot_ref[...]
        conv_all = jnp.concatenate([tot[5:6, half:], tot[6:7, :half], tot[6:7, half:]], axis=0)
        conv = sum(jnp.where(chip_ref[0] == s, conv_all[:, s * LANES:(s + 1) * LANES], 0.0) for s in range(N_CHIPS))
        grads = [jnp.concatenate([tot[4:5, :half], tot[4:5, half:]], axis=0), tot[5:6, :half], conv,
                 tot[0:1], tot[1:2], tot[2:3], tot[3:4]]
        for k, g in enumerate(grads):
            delta, nm, nv = _adam_step(ins[k][...], g, ins[n + k][...], ins[2 * n + k][...])
            outs[k][...], outs[n + k][...], outs[2 * n + k][...], outs[3 * n + k][...] = g, delta, nm, nv
        outs[4 * n][...] = tot[7:8, 0:1]

    whole = lambda a: pl.BlockSpec(a.shape, lambda i, s: (0,) * a.ndim)
    arrays = (*weights, *ms, *vs)
    loss = jax.ShapeDtypeStruct((1, 1), F32)
    return pl.pallas_call(
        body, name="adamw_small",
        grid_spec=pltpu.PrefetchScalarGridSpec(
            num_scalar_prefetch=1, grid=(1,), in_specs=[whole(tot)] + [whole(a) for a in arrays] + [ANY],
            out_specs=[whole(a) for a in weights] * 4 + [whole(loss)]),
        out_shape=[jax.ShapeDtypeStruct(a.shape, F32) for a in weights] * 4 + [loss],
    )(chip, tot, *arrays, after)


def kernel(x, w_in, lb_logits, gate_norm_w, conv_w, w_out, ln1_g, ln1_b, w_ff1, w_ff2, ln2_g, ln2_b, loss_target, m_w_in, m_lb_logits, m_gate_norm_w, m_conv_w, m_w_out, m_ln1_g, m_ln1_b, m_w_ff1, m_w_ff2, m_ln2_g, m_ln2_b, v_w_in, v_lb_logits, v_gate_norm_w, v_conv_w, v_w_out, v_ln1_g, v_ln1_b, v_w_ff1, v_w_ff2, v_ln2_g, v_ln2_b):
    xs, tgt = x[0], loss_target[0]
    chip = 2 * lax.axis_index("x") + lax.axis_index("y")
    core = lax.axis_index("c").astype(jnp.int32).reshape(1)
    chip1 = chip.astype(jnp.int32).reshape(1)
    place = jnp.concatenate([chip1, core])

    conv4 = lax.dynamic_update_slice(jnp.zeros((N_CHIPS,) + conv_w.shape[1:], F32), conv_w, (chip, 0, 0))
    ici_in = _gather_w_in_over_ici(w_in[0], conv4)
    rest = (1, 2, 3)
    ici_rest = _gather_over_ici(rest, (_place_shard("place_w_out", w_out[0], chip1, False, after=ici_in.token),
                                       _place_shard("place_w_ff1", w_ff1[0], chip1, True, after=ici_in.token),
                                       _place_shard("place_w_ff2", w_ff2[0], chip1, False, after=ici_in.token)))
    near = ici_in.wait(ici_rest.token, (0, 1, 3, 4, 5), "_near")
    d2d_near = _gather_over_d2d((0,), near[:1], (0, 1), "_near")
    ici_in.arrays = (*d2d_near.arrays, near[1])
    far = ici_in.wait(d2d_near.token)
    d2d_far = _gather_over_d2d((0,), far[:1], (2,), "_far")
    d2d_near.arrays = d2d_far.arrays
    d2d_far.arrays = d2d_near.wait(d2d_far.token)
    wb_in, = d2d_far.wait(d2d_far.token)
    cv4 = far[1]
    conv_full = cv4.transpose(1, 0, 2).reshape(3, CONV_WIDTH)

    proj, bcu, xb, cat_c = _in_proj(xs, wb_in, conv_full, ici_rest.token)
    relay_rest = _relay_over_ici(rest, ici_rest.wait(proj))
    o, states = _hgrn_fwd(proj, lb_logits, relay_rest.token)
    d2d_rest = _gather_over_d2d(rest, relay_rest.wait(o))
    cat_h = _gate_fwd(proj, o, gate_norm_w, d2d_rest.token)
    wb_out, wb_ff1, wb_ff2 = d2d_rest.wait(cat_h)

    (h1b, r, da, dpre2b, dpre1, dcat, g_ln1_g, g_ln1_b, g_ln2_g, g_ln2_b, loss8, g_out_local) = _sublayers(
        cat_h, cat_c, xs, tgt, wb_out, wb_ff1, wb_ff2, ln1_g, ln1_b, ln2_g, ln2_b)

    names = ("w_in", "w_out", "w_ff1", "w_ff2")

    def named(prefix, kinds):
        return prefix + "".join("_" + names[k] for k in kinds)

    def add_halves(kinds, grads, lands):
        return _add_half(named("add_half", kinds), kinds, grads, lands, core)

    def sum_pieces(kinds, halves, lands, after):
        return _sum_pieces(named("sum_pieces", kinds), kinds, halves, lands, place, after)

    early = (1, 2, 3)
    g_ff2_local, dpc, g_conv = _dw_ff2(r, dpre2b, dcat, bcu, conv_full)
    swap_a = _swap_halves((1, 3), (g_out_local, g_ff2_local))
    g_ff1_local, do, dog, g_gnw = _dw_ff1(h1b, da, dcat, o, proj, gate_norm_w, swap_a.token)
    swap_b = _swap_halves((2,), (g_ff1_local,))
    swapped_a = swap_a.wait(swap_b.token)
    halves_a = add_halves((1, 3), swapped_a[:2], swapped_a[2:])
    swapped_b = swap_b.wait(halves_a[1])
    halves = (halves_a[0], *add_halves((2,), swapped_b[:1], swapped_b[1:]), halves_a[1])
    exch = _exchange_pieces(early, halves)
    dph, g_lbl = _hgrn_bwd(proj, do, states, lb_logits, exch.token)
    g_in_local, grad_x = _dw_in(xb, dph, dog, dpc, wb_in, dpre1, dph)

    late = (0,)
    swap = _swap_halves(late, (g_in_local,))
    exchanged = exch.wait(swap.token)
    pack = jnp.concatenate([
        g_ln1_g, g_ln1_b, g_ln2_g, g_ln2_b,
        jnp.concatenate([g_lbl[0:1], g_lbl[1:2]], axis=1),
        jnp.concatenate([g_gnw, g_conv[0:1]], axis=1),
        jnp.concatenate([g_conv[1:2], g_conv[2:3]], axis=1),
        jnp.concatenate([loss8[0:1], jnp.zeros((1, D_MODEL - LANES), F32)], axis=1)], axis=0)
    join_a = _join_halves((2,), sum_pieces((2,), exchanged[1:2], exchanged[4:5], swap.token))
    swapped = swap.wait(join_a.token)
    exch = _exchange_pieces(late, add_halves(late, swapped[:1], swapped[1:]), pack)
    join_b = _join_halves((1, 3), sum_pieces((1, 3), exchanged[0:3:2], exchanged[3:6:2], exch.token))
    g_w_ff1, = join_a.wait(join_b.token)
    (g_w_ff1, d_ff1, nm_ff1, nv_ff1), = _adamw("adamw_w_ff1", [(w_ff1[0], g_w_ff1, m_w_ff1[0], v_w_ff1[0])])
    g_w_out, g_w_ff2 = join_b.wait(d_ff1)
    (g_w_ff2, d_ff2, nm_ff2, nv_ff2), (g_w_out, d_out, nm_out, nv_out) = _adamw(
        "adamw_w_ff2_w_out", [(w_ff2[0], g_w_ff2, m_w_ff2[0], v_w_ff2[0]), (w_out[0], g_w_out, m_w_out[0], v_w_out[0])])
    shared = exch.wait(d_out, range(N_DEV - 1), "_pack")
    tot = _sum_shared(shared[2], shared[3], 2 * chip1 + core)
    exchanged = exch.wait(tot)
    join = _join_halves(late, sum_pieces(late, exchanged[:1], exchanged[1:2], tot))
    small = ("lb_logits", "gate_norm_w", "conv_w", "ln1_g", "ln1_b", "ln2_g", "ln2_b")
    small_out = _adamw_small(
        tot, chip1, (lb_logits, gate_norm_w, conv_w[0], ln1_g, ln1_b, ln2_g, ln2_b),
        (m_lb_logits, m_gate_norm_w, m_conv_w[0], m_ln1_g, m_ln1_b, m_ln2_g, m_ln2_b),
        (v_lb_logits, v_gate_norm_w, v_conv_w[0], v_ln1_g, v_ln1_b, v_ln2_g, v_ln2_b), join.token)
    g_w_in, = join.wait(small_out[0])
    (g_w_in, d_in, nm_in, nv_in), = _adamw("adamw_w_in", [(w_in[0], g_w_in, m_w_in[0], v_w_in[0])])
    loss = small_out[4 * len(small)][0, 0]

    def results(n_kind, large):
        out = dict(zip(small, small_out[n_kind * len(small):(n_kind + 1) * len(small)]))
        out["conv_w"] = out["conv_w"][None]
        out.update({name: a[None] for name, a in zip(("w_in", "w_out", "w_ff1", "w_ff2"), large)})
        return [out[name] for name in ("w_in", "lb_logits", "gate_norm_w", "conv_w", "w_out", "ln1_g", "ln1_b",
                                       "w_ff1", "w_ff2", "ln2_g", "ln2_b")]

    return (loss, grad_x[None], *results(0, (g_w_in, g_w_out, g_w_ff1, g_w_ff2)),
            *results(1, (d_in, d_out, d_ff1, d_ff2)), *results(2, (nm_in, nm_out, nm_ff1, nm_ff2)),
            *results(3, (nv_in, nv_out, nv_ff1, nv_ff2)))
```

```python
import jax
import jax.numpy as jnp
from jax import lax
from jax.experimental import pallas as pl
from jax.experimental.pallas import tpu as pltpu

F32 = jnp.float32
BF16 = jnp.bfloat16
MXU_DTYPE = jnp.bfloat16

D_MODEL = 1024
HGRN_WIDTH = 512
HEAD_DIM = 128
N_HEADS = 4
CONV_WIDTH = 512
CHUNK = 64
D_FF = 4096
IN_COLS = 3584
GROUP = 512
N_GROUPS = IN_COLS // GROUP
ALPHA = 2.0 ** 0.25
EPS = 1e-5
N_CHIPS = 4
ADAM_LR, ADAM_B1, ADAM_B2, ADAM_EPS, ADAM_WD, ADAM_STEP = 0.001, 0.9, 0.999, 1e-08, 0.01, 10

LANES = 128
SUBLANES = 8
VMEM_LIMIT = 56 * 1024 * 1024
FF_BLOCK = 1024
N_FF = D_FF // FF_BLOCK
GATE_STRIP = 64

NN = (((1,), (0,)), ((), ()))
NT = (((1,), (1,)), ((), ()))
TN = (((0,), (0,)), ((), ()))
MESH = pl.DeviceIdType.MESH
ANY = pl.BlockSpec(memory_space=pl.ANY)


def _dot(a, b, dims):
    return lax.dot_general(a.astype(MXU_DTYPE), b.astype(MXU_DTYPE), dims, preferred_element_type=F32)


def _dot_exact(ones, v):
    ones = ones.astype(jnp.bfloat16)
    hi = v.astype(jnp.bfloat16)
    rest = v - hi.astype(F32)
    mid = rest.astype(jnp.bfloat16)
    low = (rest - mid.astype(F32)).astype(jnp.bfloat16)
    return sum(lax.dot_general(ones, part, NN, preferred_element_type=F32) for part in (hi, mid, low))


def _params(*sem):
    return pltpu.CompilerParams(dimension_semantics=sem, vmem_limit_bytes=VMEM_LIMIT)


def _resident(shape):
    return pl.BlockSpec(shape, lambda *_: (0,) * len(shape), pipeline_mode=pl.Buffered(1))


def _sigmoid(v):
    return 1.0 / (1.0 + jnp.exp(-v))


def _lower_bound(lbl):
    m = jnp.max(lbl, axis=0, keepdims=True)
    e = jnp.exp(lbl - m)
    s = e / jnp.sum(e, axis=0, keepdims=True)
    return s[0:1, :], s[1:2, :]


def _heads(v):
    return [v[:, h * HEAD_DIM:(h + 1) * HEAD_DIM] for h in range(N_HEADS)]


def _per_head(fn, *arrays):
    return jnp.concatenate([fn(*parts) for parts in zip(*map(_heads, arrays))], axis=1)


def _in_proj(x, w_in, conv_w, after):
    t = x.shape[0]
    tm = min(t, 512)

    def body(x_ref, w_ref, cw_ref, after_ref, o_ref, bcu_ref, xb_ref, y_ref, zbuf):
        @pl.when(pl.program_id(0) == 0)
        def _():
            zbuf[tm:tm + SUBLANES, :] = jnp.zeros((SUBLANES, CONV_WIDTH), F32)

        xb = x_ref[...].astype(xb_ref.dtype)
        xb_ref[...] = xb
        group = lambda g: _dot(xb, w_ref[:, g * GROUP:(g + 1) * GROUP], NN)
        for g in range(4):
            o_ref[g] = group(g)
        b_gate, c_gate, u = group(4), group(5), group(6)
        for n, part in enumerate((b_gate, c_gate, u)):
            bcu_ref[n] = part.astype(bcu_ref.dtype)
        zbuf[0:SUBLANES, :] = zbuf[tm:tm + SUBLANES, :]
        zbuf[SUBLANES:SUBLANES + tm, :] = c_gate * u
        cw = cw_ref[...]
        at = lambda shift: zbuf[shift:shift + tm, :]
        conv = cw[2:3, :] * at(SUBLANES) + cw[1:2, :] * at(SUBLANES - 1) + cw[0:1, :] * at(SUBLANES - 2)
        y_ref[...] = (b_gate * conv).astype(y_ref.dtype)

    return pl.pallas_call(
        body, name="in_proj", grid=(t // tm,),
        in_specs=[pl.BlockSpec((tm, D_MODEL), lambda i: (i, 0)), _resident((D_MODEL, IN_COLS)),
                  pl.BlockSpec((3, CONV_WIDTH), lambda i: (0, 0)), ANY],
        out_specs=[pl.BlockSpec((4, tm, GROUP), lambda i: (0, i, 0)), pl.BlockSpec((3, tm, GROUP), lambda i: (0, i, 0)),
                   pl.BlockSpec((tm, D_MODEL), lambda i: (i, 0)), pl.BlockSpec((tm, CONV_WIDTH), lambda i: (i, 0))],
        out_shape=[jax.ShapeDtypeStruct((4, t, GROUP), F32), jax.ShapeDtypeStruct((3, t, GROUP), BF16),
                   jax.ShapeDtypeStruct((t, D_MODEL), BF16), jax.ShapeDtypeStruct((t, CONV_WIDTH), BF16)],
        scratch_shapes=[pltpu.VMEM((tm + SUBLANES, CONV_WIDTH), F32)],
        compiler_params=_params("arbitrary"),
    )(x, w_in, conv_w, after)


def _gates(fp, lb):
    sig = _sigmoid(fp)
    f = lb + (1.0 - lb) * sig
    return sig, f, jnp.log(f), 1.0 - f


def _chunk_masks():
    row = lax.broadcasted_iota(jnp.int32, (CHUNK, CHUNK), 0)
    col = lax.broadcasted_iota(jnp.int32, (CHUNK, CHUNK), 1)
    return row >= col, row <= col


def _hgrn_fwd(proj, lb_logits, after):
    t = proj.shape[1]
    tb = min(t, 512)
    ncb = tb // CHUNK

    def body(q_ref, f_ref, v_ref, lbl_ref, after_ref, o_ref, st_ref, s_scr):
        @pl.when(pl.program_id(0) == 0)
        def _():
            s_scr[...] = jnp.zeros_like(s_scr)

        lb, _ = _lower_bound(lbl_ref[...])
        causal, _ = _chunk_masks()

        every = range(ncb)
        rows = [slice(c * CHUNK, (c + 1) * CHUNK) for c in every]
        q, v = [q_ref[r, :] for r in rows], [v_ref[r, :] for r in rows]
        gates = [_gates(f_ref[r, :], lb) for r in rows]
        k = [gt[3] for gt in gates]
        b = [_dot_exact(causal, gt[2]) for gt in gates]
        mid, last = [x[CHUNK // 2:CHUNK // 2 + 1, :] for x in b], [x[CHUNK - 1:CHUNK, :] for x in b]
        qt = [q[c] * jnp.exp(b[c] - mid[c]) for c in every]
        kt = [k[c] * jnp.exp(mid[c] - b[c]) for c in every]
        qi = [q[c] * jnp.exp(b[c]) for c in every]
        ks = [k[c] * jnp.exp(last[c] - b[c]) for c in every]
        dec = [jnp.exp(x) for x in last]
        scores = [[jnp.where(causal, _dot(a, b_, NT), 0.0) for a, b_ in zip(_heads(qt[c]), _heads(kt[c]))] for c in every]
        intra = [[_dot(s, v_h, NN) for s, v_h in zip(scores[c], _heads(v[c]))] for c in every]
        update = [_per_head(lambda v_h, ks_h: _dot(v_h, ks_h, TN), v[c], ks[c]) for c in every]

        st = s_scr[...]
        states = []
        for c in every:
            states.append(st)
            st_ref[c] = st
            st = dec[c] * st + update[c]
        s_scr[...] = st

        o_ref[...] = jnp.concatenate(
            [jnp.concatenate([i_h + _dot(qi_h, st_h, NT) for i_h, qi_h, st_h in
                              zip(intra[c], _heads(qi[c]), _heads(states[c]))], axis=1) for c in every], axis=0)

    grp = lambda g: pl.BlockSpec((None, tb, GROUP), lambda i: (g, i, 0))
    return pl.pallas_call(
        body, name="hgrn_fwd", grid=(t // tb,),
        in_specs=[grp(0), grp(1), grp(2), pl.BlockSpec((2, HGRN_WIDTH), lambda i: (0, 0)), ANY],
        out_specs=[pl.BlockSpec((tb, HGRN_WIDTH), lambda i: (i, 0)),
                   pl.BlockSpec((ncb, HEAD_DIM, HGRN_WIDTH), lambda i: (i, 0, 0))],
        out_shape=[jax.ShapeDtypeStruct((t, HGRN_WIDTH), F32),
                   jax.ShapeDtypeStruct((t // CHUNK, HEAD_DIM, HGRN_WIDTH), F32)],
        scratch_shapes=[pltpu.VMEM((HEAD_DIM, HGRN_WIDTH), F32)],
        compiler_params=_params("arbitrary"),
    )(proj, proj, proj, lb_logits, after)


def _gate_fwd(proj, o, gate_norm_w, after):
    t = proj.shape[1]
    tb = min(t, 1024)

    def body(o_ref, og_ref, gnw_ref, after_ref, out_ref):
        gnw = gnw_ref[...]
        for s in range(tb // GATE_STRIP):
            rows = slice(s * GATE_STRIP, (s + 1) * GATE_STRIP)
            og = og_ref[rows, :]
            on = _per_head(lambda o_h: o_h * lax.rsqrt(jnp.mean(o_h * o_h, axis=-1, keepdims=True) + EPS), o_ref[rows, :])
            out_ref[rows, :] = (on * gnw * (og * _sigmoid(og))).astype(out_ref.dtype)

    tile = pl.BlockSpec((tb, GROUP), lambda i: (i, 0))
    return pl.pallas_call(
        body, name="gate_fwd", grid=(t // tb,),
        in_specs=[tile, pl.BlockSpec((None, tb, GROUP), lambda i: (3, i, 0)), pl.BlockSpec((1, GROUP), lambda i: (0, 0)), ANY],
        out_specs=tile,
        out_shape=jax.ShapeDtypeStruct((t, HGRN_WIDTH), BF16),
        compiler_params=_params("parallel"),
    )(o, proj, gate_norm_w, after)


def _ln_bwd(dy, xhat, rstd, g):
    dxhat = dy * g
    m1 = jnp.mean(dxhat, axis=-1, keepdims=True)
    m2 = jnp.mean(dxhat * xhat, axis=-1, keepdims=True)
    return rstd * (dxhat - m1 - xhat * m2)


def _layer_norm(pre):
    xc = pre - jnp.mean(pre, axis=-1, keepdims=True)
    rstd = lax.rsqrt(jnp.mean(xc * xc, axis=-1, keepdims=True) + EPS)
    return xc * rstd, rstd


def _sublayers(cat_h, cat_c, x, target, w_out, w_ff1, w_ff2, g1, b1, g2, b2):
    t = x.shape[0]
    tm = min(t, 256)

    def body(ch_ref, cc_ref, x_ref, tg_ref, wo_ref, w1_ref, w2_ref, g1_ref, b1_ref, g2_ref, b2_ref,
             h1_ref, r_ref, da_ref, dp2b_ref, dp1_ref, dcat_ref, dg1_ref, db1_ref, dg2_ref, db2_ref, loss_ref, gwo_ref,
             gwo_acc, gwo_narrow, sem):
        @pl.when(pl.program_id(0) == 0)
        def _():
            for ref in (dg1_ref, db1_ref, dg2_ref, db2_ref, loss_ref, gwo_acc):
                ref[...] = jnp.zeros_like(ref)

        mix = _dot(ch_ref[...], wo_ref[0:GROUP, :], NN) + _dot(cc_ref[...], wo_ref[GROUP:2 * GROUP, :], NN)
        xhat1, rstd1 = _layer_norm(ALPHA * x_ref[...] + mix)
        h1 = xhat1 * g1_ref[...] + b1_ref[...]
        h1b = h1.astype(h1_ref.dtype)
        h1_ref[...] = h1b
        mlp = jnp.zeros((tm, D_MODEL), F32)
        for j in range(N_FF):
            cols = slice(j * FF_BLOCK, (j + 1) * FF_BLOCK)
            r = jnp.square(jnp.maximum(_dot(h1b, w1_ref[:, cols], NN), 0.0)).astype(r_ref.dtype)
            r_ref[:, cols] = r
            mlp = mlp + _dot(r, w2_ref[cols, :], NN)
        xhat2, rstd2 = _layer_norm(ALPHA * h1 + mlp)
        err = xhat2 * g2_ref[...] + b2_ref[...] - tg_ref[...]
        loss_ref[...] += 0.5 * jnp.sum(jnp.mean(err * err, axis=-1, keepdims=True))
        dy = err * (1.0 / D_MODEL)
        dg2_ref[...] += jnp.sum(dy * xhat2, axis=0, keepdims=True)
        db2_ref[...] += jnp.sum(dy, axis=0, keepdims=True)
        dp2 = _ln_bwd(dy, xhat2, rstd2, g2_ref[...])
        dp2b = dp2.astype(dp2b_ref.dtype)
        dp2b_ref[...] = dp2b
        back = jnp.zeros((tm, D_MODEL), F32)
        for j in range(N_FF):
            cols = slice(j * FF_BLOCK, (j + 1) * FF_BLOCK)
            dr = _dot(dp2b, w2_ref[cols, :], NT)
            da = (dr * (2.0 * jnp.sqrt(r_ref[:, cols].astype(F32)))).astype(da_ref.dtype)
            da_ref[:, cols] = da
            back = back + _dot(da, w1_ref[:, cols], NT)
        dh1 = ALPHA * dp2 + back
        dg1_ref[...] += jnp.sum(dh1 * xhat1, axis=0, keepdims=True)
        db1_ref[...] += jnp.sum(dh1, axis=0, keepdims=True)
        dp1 = _ln_bwd(dh1, xhat1, rstd1, g1_ref[...])
        dp1b = dp1.astype(MXU_DTYPE)
        dp1_ref[...] = dp1
        dcat_ref[...] = _dot(dp1b, wo_ref[...], NT)
        gwo_acc[0:GROUP, :] += _dot(ch_ref[...], dp1b, TN)
        gwo_acc[GROUP:2 * GROUP, :] += _dot(cc_ref[...], dp1b, TN)

        @pl.when(pl.program_id(0) == pl.num_programs(0) - 1)
        def _():
            gwo_narrow[...] = gwo_acc[...].astype(gwo_narrow.dtype)
            copy = pltpu.make_async_copy(gwo_narrow, gwo_ref, sem.at[0])
            copy.start()
            copy.wait()

    row = pl.BlockSpec((tm, D_MODEL), lambda i: (i, 0))
    wide = pl.BlockSpec((tm, D_FF), lambda i: (i, 0))
    vec = pl.BlockSpec((1, D_MODEL), lambda i: (0, 0))
    narrow = lambda dtype: jax.ShapeDtypeStruct((t, D_MODEL), dtype)
    return pl.pallas_call(
        body, name="sublayers", grid=(t // tm,),
        in_specs=[pl.BlockSpec((tm, GROUP), lambda i: (i, 0)), pl.BlockSpec((tm, GROUP), lambda i: (i, 0)), row, row,
                  _resident((D_MODEL, D_MODEL)),
                  _resident((D_MODEL, D_FF)), _resident((D_FF, D_MODEL)), vec, vec, vec, vec],
        out_specs=[row, wide, wide, row, row, row, vec, vec, vec, vec,
                   pl.BlockSpec((SUBLANES, LANES), lambda i: (0, 0)), ANY],
        out_shape=[narrow(BF16), jax.ShapeDtypeStruct((t, D_FF), BF16), jax.ShapeDtypeStruct((t, D_FF), BF16),
                   narrow(BF16), narrow(F32), narrow(F32)]
                  + [jax.ShapeDtypeStruct((1, D_MODEL), F32)] * 4
                  + [jax.ShapeDtypeStruct((SUBLANES, LANES), F32), jax.ShapeDtypeStruct((D_MODEL, D_MODEL), BF16)],
        scratch_shapes=[pltpu.VMEM((D_MODEL, D_MODEL), F32), pltpu.VMEM((D_MODEL, D_MODEL), BF16),
                        pltpu.SemaphoreType.DMA((1,))],
        compiler_params=_params("arbitrary"),
    )(cat_h, cat_c, x, target, w_out, w_ff1, w_ff2, g1, b1, g2, b2)


def _hgrn_bwd(proj, do, states, lb_logits, after):
    t = proj.shape[1]
    tb = min(t, 512)
    ncb = tb // CHUNK
    nblk = t // tb

    def body(q_ref, f_ref, v_ref, do_ref, st_ref, lbl_ref, after_ref, dp_ref, dlbl_ref, ds_scr, dlb_scr):
        i = pl.program_id(0)

        @pl.when(i == 0)
        def _():
            ds_scr[...] = jnp.zeros_like(ds_scr)
            dlb_scr[...] = jnp.zeros_like(dlb_scr)

        lb, s1 = _lower_bound(lbl_ref[...])
        causal, anti = _chunk_masks()
        every = range(ncb)
        rows = [slice(c * CHUNK, (c + 1) * CHUNK) for c in every]
        q, v, do = ([ref[r, :] for r in rows] for ref in (q_ref, v_ref, do_ref))
        st = [st_ref[c] for c in every]
        gates = [_gates(f_ref[r, :], lb) for r in rows]
        sig, f, k = ([gt[n] for gt in gates] for n in (0, 1, 3))
        b = [_dot_exact(causal, gt[2]) for gt in gates]
        mid, last = [x[CHUNK // 2:CHUNK // 2 + 1, :] for x in b], [x[CHUNK - 1:CHUNK, :] for x in b]
        e_q = [jnp.exp(b[c] - mid[c]) for c in every]
        e_k = [jnp.exp(mid[c] - b[c]) for c in every]
        e_i = [jnp.exp(x) for x in b]
        e_s = [jnp.exp(last[c] - b[c]) for c in every]
        dec = [jnp.exp(x) for x in last]
        qt, kt, qi, ks = ([a[c] * e[c] for c in every] for a, e in ((q, e_q), (k, e_k), (q, e_i), (k, e_s)))

        def masked(a, b_):
            return [[jnp.where(causal, _dot(a_h, b_h, NT), 0.0) for a_h, b_h in zip(_heads(a[c]), _heads(b_[c]))]
                    for c in every]

        def with_scores(s, other, dims):
            return [jnp.concatenate([_dot(s_h, o_h, dims) for s_h, o_h in zip(s[c], _heads(other[c]))], axis=1)
                    for c in every]

        def per_head(dims, a, b_):
            return [_per_head(lambda a_h, b_h: _dot(a_h, b_h, dims), a[c], b_[c]) for c in every]

        scores, dscores = masked(qt, kt), masked(do, v)
        dqt, dkt, dv_intra = with_scores(dscores, kt, NN), with_scores(dscores, qt, TN), with_scores(scores, do, TN)
        dqi, update = per_head(NN, do, st), per_head(TN, do, qi)

        dst = ds_scr[...]
        dsts = [None] * ncb
        for c in reversed(every):
            dsts[c] = dst
            dst = dec[c] * dst + update[c]
        ds_scr[...] = dst

        dv_state, dks = per_head(NT, ks, dsts), per_head(NN, v, dsts)
        ddec = [jnp.sum(dsts[c] * st[c], axis=0, keepdims=True) for c in every]
        dq = [dqt[c] * e_q[c] + dqi[c] * e_i[c] for c in every]
        dk = [dkt[c] * e_k[c] + dks[c] * e_s[c] for c in every]
        db = [q[c] * dq[c] - k[c] * dk[c] for c in every]
        db_last = [jnp.sum(dks[c] * ks[c], axis=0, keepdims=True) + ddec[c] * dec[c] for c in every]
        dg = [_dot_exact(anti, db[c]) + db_last[c] for c in every]
        df = [dg[c] / f[c] - dk[c] for c in every]
        dlb_scr[...] += sum(jnp.sum(df[c] * (1.0 - sig[c]), axis=0, keepdims=True) for c in every)
        dfp = [df[c] * (1.0 - lb) * sig[c] * (1.0 - sig[c]) for c in every]
        dv = [dv_intra[c] + dv_state[c] for c in every]
        for n, parts in enumerate((dq, dfp, dv)):
            dp_ref[n] = jnp.concatenate(parts, axis=0).astype(dp_ref.dtype)

        @pl.when(i == nblk - 1)
        def _():
            dlb = dlb_scr[...]
            dlbl_ref[0:1, :] = dlb * lb * (1.0 - lb)
            dlbl_ref[1:2, :] = -dlb * lb * s1

    grp = lambda g: pl.BlockSpec((None, tb, GROUP), lambda i: (g, nblk - 1 - i, 0))
    vec = pl.BlockSpec((2, HGRN_WIDTH), lambda i: (0, 0))
    return pl.pallas_call(
        body, name="hgrn_bwd", grid=(nblk,),
        in_specs=[grp(0), grp(1), grp(2), pl.BlockSpec((tb, HGRN_WIDTH), lambda i: (nblk - 1 - i, 0)),
                  pl.BlockSpec((ncb, HEAD_DIM, HGRN_WIDTH), lambda i: (nblk - 1 - i, 0, 0)), vec, ANY],
        out_specs=[pl.BlockSpec((3, tb, HGRN_WIDTH), lambda i: (0, nblk - 1 - i, 0)), vec],
        out_shape=[jax.ShapeDtypeStruct((3, t, HGRN_WIDTH), BF16), jax.ShapeDtypeStruct((2, HGRN_WIDTH), F32)],
        scratch_shapes=[pltpu.VMEM((HEAD_DIM, HGRN_WIDTH), F32), pltpu.VMEM((1, HGRN_WIDTH), F32)],
        compiler_params=_params("arbitrary"),
    )(proj, proj, proj, do, states, lb_logits, after)


GRAD_TILE = 512
OUT_PARTS = 4


class _Side:
    def __init__(self, operands, in_specs, out_shape, out_specs, scratch, init, begin):
        self.operands, self.in_specs, self.out_shape, self.out_specs = operands, in_specs, out_shape, out_specs
        self.scratch, self.init, self.begin = scratch, init, begin


RING = 3


def _grad_w(name, operands, widths, shape, step, after=None, side=None, ringed=None):
    t = operands[0].shape[-2]
    tt = min(t, GRAD_TILE)
    n_in, n_steps = len(operands), t // tt
    in_specs = [ANY if n == ringed else pl.BlockSpec((tt, w), lambda k: (k, 0)) if a.ndim == 2 else
                pl.BlockSpec((a.shape[0], tt, w), lambda k: (0, k, 0)) for n, (a, w) in enumerate(zip(operands, widths))]
    extra = [] if after is None else [after]
    s_in, s_out = (len(side.operands), len(side.out_shape)) if side else (0, 0)
    first_out = n_in + s_in + len(extra)
    side_scratch = side.scratch if side else []
    ring_scratch = [] if ringed is None else [pltpu.VMEM((RING, tt, widths[ringed]), operands[ringed].dtype),
                                              pltpu.SemaphoreType.DMA((RING,))]

    def body(*refs):
        o_ref, side_outs = refs[first_out], refs[first_out + 1:first_out + 1 + s_out]
        acc, narrow, sem = refs[first_out + 1 + s_out:first_out + 4 + s_out]
        first_side = first_out + 4 + s_out
        k = pl.program_id(0)
        tiles = list(refs[:n_in])
        if ringed is not None:
            ring, ring_sems = refs[first_side + len(side_scratch):]

            def fetch(tile, slot):
                return pltpu.make_async_copy(refs[ringed].at[pl.ds(tile * tt, tt), :], ring.at[slot], ring_sems.at[slot])

            @pl.when(k == 0)
            def _():
                for first in range(min(RING - 1, n_steps)):
                    fetch(first, first).start()

            @pl.when(k + RING - 1 < n_steps)
            def _():
                fetch(k + RING - 1, lax.rem(k + RING - 1, RING)).start()

        @pl.when(k == 0)
        def _():
            acc[...] = jnp.zeros_like(acc)
            if side:
                side.init(side_outs)

        if ringed is not None:
            slot = lax.rem(k, RING)
            fetch(k, slot).wait()
            tiles[ringed] = ring.at[slot]

        tick = (side.begin(k, n_steps, refs[n_in:n_in + s_in], side_outs, refs[first_side:first_side + len(side_scratch)])
                if side else None)
        step(acc, *tiles, tick or (lambda j: None))

        @pl.when(k == n_steps - 1)
        def _():
            part = shape[0] // OUT_PARTS
            copies = []
            for p in range(OUT_PARTS):
                rows = pl.ds(p * part, part)
                narrow[rows, :] = acc[rows, :].astype(narrow.dtype)
                copies.append(pltpu.make_async_copy(narrow.at[rows, :], o_ref.at[rows, :], sem.at[p]))
                copies[-1].start()
            for cp in copies:
                cp.wait()

    outs = pl.pallas_call(
        body, name=name, grid=(n_steps,),
        in_specs=in_specs + (side.in_specs if side else []) + [ANY] * len(extra),
        out_specs=[ANY] + (side.out_specs if side else []),
        out_shape=[jax.ShapeDtypeStruct(shape, BF16)] + (side.out_shape if side else []),
        scratch_shapes=[pltpu.VMEM(shape, F32), pltpu.VMEM(shape, BF16), pltpu.SemaphoreType.DMA((OUT_PARTS,))]
                       + side_scratch + ring_scratch,
        compiler_params=_params("arbitrary"),
    )(*operands, *(side.operands if side else ()), *extra)
    return outs if side else outs[0]


def _dw_in(xb, dph, dog, dpc, w_in, dpre1, after):
    t = xb.shape[0]

    def step(acc, x_ref, dh_ref, dog_ref, dc_ref, tick):
        xv = x_ref[...]
        for g in range(N_GROUPS):
            part = dh_ref[g] if g < 3 else dog_ref[...] if g == 3 else dc_ref[g - 4]
            acc[:, g * GROUP:(g + 1) * GROUP] += _dot(xv, part, TN)
            tick(g, part)

    def begin(k, n_steps, ins, outs, scratch):
        w_ref, dp_ref = ins
        total = [ALPHA * dp_ref[...]]

        def tick(g, part):
            total[0] = total[0] + _dot(part, w_ref[:, g * GROUP:(g + 1) * GROUP], NT)
            if g == N_GROUPS - 1:
                outs[0][...] = total[0]

        return tick

    row = pl.BlockSpec((min(t, GRAD_TILE), D_MODEL), lambda k: (k, 0))
    side = _Side((w_in, dpre1), [_resident((D_MODEL, IN_COLS)), row], [jax.ShapeDtypeStruct((t, D_MODEL), F32)], [row],
                 [], lambda outs: None, begin)
    return _grad_w("dw_in", (xb, dph, dog, dpc), (D_MODEL, GROUP, GROUP, GROUP), (D_MODEL, IN_COLS), step, after, side)


def _strips_of(j, tt):
    per_tick = tt // GATE_STRIP // N_FF
    return [slice(s * GATE_STRIP, (s + 1) * GATE_STRIP) for s in range(j * per_tick, (j + 1) * per_tick)]


def _dw_ff1(h1b, da, dcat, o, proj, gate_norm_w, after):
    t = h1b.shape[0]
    tt = min(t, GRAD_TILE)

    def step(acc, h_ref, da_ref, tick):
        hv = h_ref[...]
        for j in range(N_FF):
            cols = slice(j * FF_BLOCK, (j + 1) * FF_BLOCK)
            acc[:, cols] += _dot(hv, da_ref[:, cols], TN)
            tick(j)

    def init(outs):
        outs[2][...] = jnp.zeros_like(outs[2])

    def begin(k, n_steps, ins, outs, scratch):
        do2_ref, o_ref, og_ref, gnw_ref = ins
        do_ref, dog_ref, dgnw_ref = outs
        total = [jnp.zeros((GATE_STRIP, GROUP), F32)]

        def tick(j):
            gnw = gnw_ref[...]
            for rows in _strips_of(j, tt):
                ov, og, do2 = o_ref[rows, :], og_ref[rows, :], do2_ref[rows, :]
                rs = _per_head(lambda o_h: jnp.broadcast_to(
                    lax.rsqrt(jnp.mean(o_h * o_h, axis=-1, keepdims=True) + EPS), o_h.shape), ov)
                on = ov * rs
                sg = _sigmoid(og)
                sil = og * sg
                don = do2 * gnw * sil
                total[0] = total[0] + do2 * on * sil
                dog_ref[rows, :] = (do2 * on * gnw * (sg * (1.0 + og * (1.0 - sg)))).astype(dog_ref.dtype)
                do_ref[rows, :] = rs * (don - on * _per_head(
                    lambda p_h: jnp.broadcast_to(jnp.mean(p_h, axis=-1, keepdims=True), p_h.shape), don * on))
            if j == N_FF - 1:
                dgnw_ref[...] += jnp.sum(total[0], axis=0, keepdims=True)

        return tick

    tile = pl.BlockSpec((tt, GROUP), lambda k: (k, 0))
    vec = pl.BlockSpec((1, GROUP), lambda k: (0, 0))
    side = _Side(
        (dcat, o, proj, gate_norm_w), [tile, tile, pl.BlockSpec((None, tt, GROUP), lambda k: (3, k, 0)), vec],
        [jax.ShapeDtypeStruct((t, HGRN_WIDTH), F32), jax.ShapeDtypeStruct((t, HGRN_WIDTH), BF16),
         jax.ShapeDtypeStruct((1, HGRN_WIDTH), F32)], [tile, tile, vec], [], init, begin)
    return _grad_w("dw_ff1", (h1b, da), (D_MODEL, D_FF), (D_MODEL, D_FF), step, after, side, ringed=1)


def _dw_ff2(r, dpre2b, dcat, bcu, conv_w):
    t = r.shape[0]
    tt = min(t, GRAD_TILE)
    hb = tt // SUBLANES
    halo = 2 * SUBLANES

    def step(acc, r_ref, d_ref, tick):
        dv = d_ref[...]
        for j in range(N_FF):
            rows = slice(j * FF_BLOCK, (j + 1) * FF_BLOCK)
            acc[rows, :] += _dot(r_ref[:, rows], dv, TN)
            tick(j)

    def init(outs):
        outs[1][...] = jnp.zeros_like(outs[1])

    def begin(k, n_steps, ins, outs, scratch):
        dy_ref, dyn_ref, b_ref, bn_ref, c_ref, u_ref, ch_ref, uh_ref, cw_ref = ins
        dp_ref, dcw_ref = outs
        zbuf, dbuf = scratch
        before = lambda ref: ref[SUBLANES:halo, :].astype(F32)
        zbuf[0:SUBLANES, :] = jnp.where(k > 0, before(ch_ref) * before(uh_ref), 0.0)
        zbuf[SUBLANES:SUBLANES + tt, :] = c_ref[...].astype(F32) * u_ref[...].astype(F32)
        dbuf[0:tt, :] = dy_ref[...] * b_ref[...].astype(F32)
        dbuf[tt:tt + SUBLANES, :] = jnp.where(k < n_steps - 1, dyn_ref[...] * bn_ref[0:SUBLANES, :].astype(F32), 0.0)
        totals = [jnp.zeros((GATE_STRIP, GROUP), F32) for _ in range(3)]

        def tick(j):
            cw = cw_ref[...]
            for rows in _strips_of(j, tt):
                at = lambda buf, shift: buf[shift + rows.start:shift + rows.stop, :]
                z, z1, z2 = at(zbuf, SUBLANES), at(zbuf, SUBLANES - 1), at(zbuf, SUBLANES - 2)
                dyc, d1, d2 = at(dbuf, 0), at(dbuf, 1), at(dbuf, 2)
                yc = cw[2:3, :] * z + cw[1:2, :] * z1 + cw[0:1, :] * z2
                dz = cw[2:3, :] * dyc + cw[1:2, :] * d1 + cw[0:1, :] * d2
                dp_ref[0, rows, :] = (dy_ref[rows, :] * yc).astype(dp_ref.dtype)
                dp_ref[1, rows, :] = (dz * u_ref[rows, :].astype(F32)).astype(dp_ref.dtype)
                dp_ref[2, rows, :] = (dz * c_ref[rows, :].astype(F32)).astype(dp_ref.dtype)
                for n, tap in enumerate((z2, z1, z)):
                    totals[n] = totals[n] + dyc * tap
            if j == N_FF - 1:
                for n in range(3):
                    dcw_ref[n:n + 1, :] += jnp.sum(totals[n], axis=0, keepdims=True)

        return tick

    grp = lambda g: pl.BlockSpec((None, tt, GROUP), lambda k: (g, k, 0))
    prev = lambda g: pl.BlockSpec((None, halo, GROUP), lambda k: (g, jnp.maximum(k * (tt // halo) - 1, 0), 0))
    nxt = lambda g: pl.BlockSpec((None, halo, GROUP), lambda k: (g, jnp.minimum((k + 1) * (tt // halo), t // halo - 1), 0))
    nxt_row = lambda k: jnp.minimum((k + 1) * hb, t // SUBLANES - 1)
    whole = pl.BlockSpec((3, CONV_WIDTH), lambda k: (0, 0))
    side = _Side(
        (dcat, dcat, bcu, bcu, bcu, bcu, bcu, bcu, conv_w),
        [pl.BlockSpec((tt, GROUP), lambda k: (k, 1)), pl.BlockSpec((SUBLANES, GROUP), lambda k: (nxt_row(k), 1)),
         grp(0), nxt(0), grp(1), grp(2), prev(1), prev(2), whole],
        [jax.ShapeDtypeStruct((3, t, CONV_WIDTH), BF16), jax.ShapeDtypeStruct((3, CONV_WIDTH), F32)],
        [pl.BlockSpec((3, tt, GROUP), lambda k: (0, k, 0)), whole],
        [pltpu.VMEM((tt + SUBLANES, GROUP), F32), pltpu.VMEM((tt + SUBLANES, GROUP), F32)], init, begin)
    return _grad_w("dw_ff2", (r, dpre2b), (D_FF, D_MODEL), (D_FF, D_MODEL), step, side=side, ringed=0)


def _place():
    x, y, c = lax.axis_index("x"), lax.axis_index("y"), lax.axis_index("c")
    return x, y, c, 2 * x + y


def _other_chips(x, y):
    return [(1 - x, y), (x, 1 - y), (1 - x, 1 - y)]


def _place_shard(name, w, chip, cols_sharded, after=None):
    rows, cols = w.shape
    tr = min(rows, 256)
    nb = rows // tr
    full = (rows, cols * N_CHIPS) if cols_sharded else (rows * N_CHIPS, cols)
    out_map = (lambda i, s: (i, s[0])) if cols_sharded else (lambda i, s: (s[0] * nb + i, 0))

    def body(s_ref, w_ref, *rest):
        rest[-1][...] = w_ref[...].astype(rest[-1].dtype)

    extra = [] if after is None else [after]
    return pl.pallas_call(
        body, name=name,
        grid_spec=pltpu.PrefetchScalarGridSpec(
            num_scalar_prefetch=1, grid=(nb,),
            in_specs=[pl.BlockSpec((tr, cols), lambda i, s: (i, 0))] + [ANY] * len(extra),
            out_specs=pl.BlockSpec((tr, cols), out_map)),
        out_shape=jax.ShapeDtypeStruct(full, BF16),
        compiler_params=_params("parallel"),
    )(chip, w, *extra)


HBM = pl.BlockSpec(memory_space=pltpu.HBM)
SEM = pl.BlockSpec(memory_space=pltpu.SEMAPHORE)
EFFECT = pltpu.SideEffectType.DATAFLOW_SIDE_EFFECTING


PEER_SETS = {
    "sibling": (0, lambda x, y, c: [(x, y, 1 - c)]),
    "chips": (1, lambda x, y, c: [(1 - x, y, c), (x, 1 - y, c), (1 - x, 1 - y, c)]),
    "neighbours": (2, lambda x, y, c: [(1 - x, y, c), (x, 1 - y, c)]),
}


class _Split:
    def __init__(self, name, arrays, plan, others=(), peers=None, prepare=None, sources=(), scratch=()):
        n_own, arrays = len(arrays), (*arrays, *others)
        n, n_copies, n_in = len(arrays), plan.count, len(arrays) + len(sources)
        self.name, self.plan, self.n = name, plan, n_own
        barrier_id, peer_ids = PEER_SETS[peers] if peers else (None, None)

        def body(*refs):
            send_sems, recv_sems, token = refs[n_in], refs[n_in + 1], refs[n_in + 2 + n]
            if peers:
                x, y, c, _ = _place()
                barrier = pltpu.get_barrier_semaphore()
                for peer in peer_ids(x, y, c):
                    pl.semaphore_signal(barrier, inc=1, device_id=peer, device_id_type=MESH)
            if prepare:
                prepare(refs[:n], refs[n:n_in], refs[n_in + 3 + n:])
            if peers:
                pl.semaphore_wait(barrier, len(peer_ids(0, 0, 0)))
            for k, (src, dst, to) in enumerate(plan(refs[:n])):
                pltpu.make_async_remote_copy(src_ref=src, dst_ref=dst, send_sem=send_sems.at[k], recv_sem=recv_sems.at[k],
                                             device_id=to, device_id_type=MESH).start()
            token[...] = jnp.zeros_like(token)

        outs = pl.pallas_call(
            body, name=name + "_start",
            out_shape=(pltpu.SemaphoreType.DMA((n_copies,)), pltpu.SemaphoreType.DMA((n_copies,)),
                       *[pltpu.HBM(a.shape, a.dtype) for a in arrays], jax.ShapeDtypeStruct((SUBLANES, LANES), F32)),
            in_specs=(HBM,) * n_in, out_specs=(SEM, SEM) + (HBM,) * n + (pl.BlockSpec(memory_space=pltpu.VMEM),),
            input_output_aliases={i: 2 + i for i in range(n)}, scratch_shapes=list(scratch),
            compiler_params=pltpu.CompilerParams(has_side_effects=EFFECT, collective_id=barrier_id),
        )(*[pltpu.with_memory_space_constraint(a, pltpu.HBM) for a in (*arrays, *sources)])
        self.sems, self.arrays, self.others, self.token = outs[:2], outs[2:2 + n_own], outs[2 + n_own:2 + n], outs[-1]
        self.waited = set()

    def wait(self, after, copies=None, part=""):
        n, plan = self.n, self.plan
        mine = set(range(plan.count) if copies is None else copies) - self.waited
        self.waited |= mine

        def body(*refs):
            send_sems, recv_sems = refs[n], refs[n + 1]
            for k, (src, dst, to) in enumerate(plan(refs[:n])):
                if k in mine:
                    cp = pltpu.make_async_remote_copy(src_ref=src, dst_ref=dst, send_sem=send_sems.at[k],
                                                      recv_sem=recv_sems.at[k], device_id=to, device_id_type=MESH)
                    cp.wait_send()
                    cp.wait_recv()

        self.arrays = pl.pallas_call(
            body, name=self.name + "_wait" + part, out_shape=tuple(pltpu.HBM(a.shape, a.dtype) for a in self.arrays),
            in_specs=(HBM,) * n + (SEM, SEM, ANY), out_specs=(HBM,) * n, input_output_aliases={i: i for i in range(n)},
            compiler_params=pltpu.CompilerParams(has_side_effects=EFFECT),
        )(*self.arrays, *self.sems, after)
        return self.arrays


COLS_SHARDED = (True, False, True, False)
HALF_SHAPES = [(D_MODEL // 2, IN_COLS), (D_MODEL, D_MODEL // 2), (D_MODEL // 2, D_FF), (D_FF, D_MODEL // 2)]
PIECE_SHAPES = [(D_MODEL // 2, IN_COLS // N_CHIPS), (D_MODEL // N_CHIPS, D_MODEL // 2),
                (D_MODEL // 2, D_FF // N_CHIPS), (D_FF // N_CHIPS, D_MODEL // 2)]


def _shard_view(kind, ref, chip):
    if COLS_SHARDED[kind]:
        n = ref.shape[1] // N_CHIPS
        return ref.at[:, pl.ds(chip * n, n)]
    n = ref.shape[0] // N_CHIPS
    return ref.at[pl.ds(chip * n, n), :]


def _half_view(kind, ref, h):
    if COLS_SHARDED[kind]:
        n = ref.shape[0] // 2
        return ref.at[pl.ds(h * n, n), :]
    n = ref.shape[1] // 2
    return ref.at[:, pl.ds(h * n, n)]


def _plan(count):
    def mark(fn):
        fn.count = count
        return fn
    return mark


def _shard_rows_view(kind, ref, chip, part, n_parts):
    if COLS_SHARDED[kind]:
        m, n = ref.shape[0] // n_parts, ref.shape[1] // N_CHIPS
        return ref.at[pl.ds(part * m, m), pl.ds(chip * n, n)]
    m = ref.shape[0] // N_CHIPS // n_parts
    return ref.at[pl.ds((n_parts * chip + part) * m, m), :]


def _shard_half_view(kind, ref, chip, h):
    return _shard_rows_view(kind, ref, chip, h, 2)


def _gather_over_ici(kinds, weights):
    @_plan(2 * len(kinds))
    def plan(refs):
        x, y, c, me = _place()
        mine = [_shard_half_view(kind, ref, me, c) for kind, ref in zip(kinds, refs)]
        return [(v, v, to) for v in mine for to in ((1 - x, y, c), (x, 1 - y, c))]

    return _Split("gather_ici_" + "".join(map(str, kinds)), tuple(weights), plan, peers="neighbours")


def _relay_over_ici(kinds, weights, others=()):
    @_plan(2 * len(kinds))
    def plan(refs):
        x, y, c, _ = _place()
        x_nbr, y_nbr = 2 * (1 - x) + y, 2 * x + (1 - y)
        out = []
        for kind, ref in zip(kinds, refs):
            first, second = (_shard_rows_view(kind, ref, chip, 2 * c + q, 4) for q, chip in ((0, x_nbr), (1, y_nbr)))
            out += [(first, first, (x, 1 - y, c)), (second, second, (1 - x, y, c))]
        return out

    return _Split("relay_ici_" + "".join(map(str, kinds)), tuple(weights), plan, others, peers="neighbours")


def _gather_w_in_over_ici(shard, conv4):
    rows, cols = shard.shape

    @_plan(6)
    def plan(refs):
        x, y, c, me = _place()
        half, conv = _shard_half_view(0, refs[0], me, c), refs[1].at[me]
        return [(v, v, (px, py, c)) for v in (half, conv) for px, py in _other_chips(x, y)]

    def prepare(refs, sources, scratch):
        wide, narrow, sems = scratch
        _, _, _, me = _place()
        load = pltpu.make_async_copy(sources[0], wide, sems.at[0])
        load.start()
        load.wait()
        narrow[...] = wide[...].astype(narrow.dtype)
        store = pltpu.make_async_copy(narrow, _shard_view(0, refs[0], me), sems.at[1])
        store.start()
        store.wait()

    return _Split("gather_w_in_ici", (lax.empty((rows, cols * N_CHIPS), BF16), conv4), plan, peers="chips",
                  prepare=prepare, sources=(shard,),
                  scratch=(pltpu.VMEM((rows, cols), F32), pltpu.VMEM((rows, cols), BF16), pltpu.SemaphoreType.DMA((2,))))


def _gather_over_d2d(kinds, weights, relations=(0, 1, 2), part=""):
    @_plan(len(relations) * len(kinds))
    def plan(refs):
        x, y, c, _ = _place()
        chips = [_other_chips(x, y)[n] for n in relations]
        got = [_shard_half_view(kind, ref, 2 * px + py, c) for kind, ref in zip(kinds, refs) for px, py in chips]
        return [(v, v, (x, y, 1 - c)) for v in got]

    return _Split("gather_d2d_" + "".join(map(str, kinds)) + part, tuple(weights), plan, peers="sibling")


def _swap_halves(kinds, grads):
    @_plan(len(kinds))
    def plan(refs):
        x, y, c, _ = _place()
        return [(_half_view(kind, g, 1 - c), land, (x, y, 1 - c))
                for kind, g, land in zip(kinds, refs[:len(kinds)], refs[len(kinds):])]

    lands = [lax.empty(HALF_SHAPES[kind], g.dtype) for kind, g in zip(kinds, grads)]
    return _Split("swap_halves_" + "".join(map(str, kinds)), (*grads, *lands), plan, peers="sibling")


def _block_rows(cols, elements):
    return 1 << ((elements // cols).bit_length() - 1)


def _grid_steps(shapes, elements):
    rows, cols = max(shapes, key=lambda shape: shape[0] * shape[1])
    return rows // min(rows, _block_rows(cols, elements))


def _add_half(name, kinds, grads, recvs, core):
    n = len(kinds)
    shapes = [recv.shape for recv in recvs]
    nb = _grid_steps(shapes, 1 << 20)

    def body(c_ref, *refs):
        for g_ref, r_ref, o_ref in zip(refs[:n], refs[n:2 * n], refs[2 * n:]):
            o_ref[...] = (g_ref[...].astype(F32) + r_ref[...].astype(F32)).astype(o_ref.dtype)

    own = [pl.BlockSpec((rows // nb, cols), (lambda i, c_ref: (c_ref[0] * nb + i, 0)) if COLS_SHARDED[k] else
                        (lambda i, c_ref: (i, c_ref[0]))) for k, (rows, cols) in zip(kinds, shapes)]
    blocks = [pl.BlockSpec((rows // nb, cols), lambda i, c_ref: (i, 0)) for rows, cols in shapes]
    return pl.pallas_call(
        body, name=name,
        grid_spec=pltpu.PrefetchScalarGridSpec(
            num_scalar_prefetch=1, grid=(nb,), in_specs=own + blocks, out_specs=blocks),
        out_shape=[jax.ShapeDtypeStruct(shape, BF16) for shape in shapes],
        compiler_params=_params("parallel"),
    )(core, *grads, *recvs)


def _exchange_pieces(kinds, halves, pack=None):
    n_p, n = N_CHIPS - 1, len(kinds)

    @_plan(n_p * n + (0 if pack is None else N_DEV - 1))
    def plan(refs):
        x, y, c, _ = _place()
        copies = []
        if pack is not None:
            me = 4 * x + 2 * y + c
            peers = [((1 - x) if m & 4 else x, (1 - y) if m & 2 else y, (1 - c) if m & 1 else c) for m in range(1, N_DEV)]
            copies += [(refs[2 * n], refs[2 * n + 1].at[me], peer) for peer in peers]
        return copies + [(_shard_view(kind, half, 2 * px + py), land.at[j], (px, py, c))
                         for j, (px, py) in enumerate(_other_chips(x, y))
                         for kind, half, land in zip(kinds, refs[:n], refs[n:2 * n])]

    lands = [lax.empty((n_p,) + PIECE_SHAPES[kind], BF16) for kind in kinds]
    small = () if pack is None else (pack, lax.empty((N_DEV,) + pack.shape, F32))
    return _Split("exchange_pieces_" + "".join(map(str, kinds)), (*halves, *lands, *small), plan,
                  peers="chips" if pack is None else None)


def _sum_pieces(name, kinds, halves, slots, place, after):
    n, n_p = len(kinds), N_CHIPS - 1
    shapes = [slot.shape[1:] for slot in slots]
    nb = _grid_steps(shapes, 1 << 18)

    def body(s_ref, *refs):
        for own_ref, slot_ref, o_ref in zip(refs[:n], refs[n:2 * n], refs[2 * n + 1:]):
            total = own_ref[...].astype(F32)
            for j in range(n_p):
                total = total + slot_ref[j].astype(F32)
            o_ref[...] = total

    own, out, shards = [], [], []
    for k, (rows, cols) in zip(kinds, shapes):
        if COLS_SHARDED[k]:
            own_map, out_map, shard = (lambda i, s: (i, s[0])), (lambda i, s: (s[1] * nb + i, 0)), (2 * rows, cols)
        else:
            own_map, out_map, shard = (lambda i, s: (s[0] * nb + i, 0)), (lambda i, s: (i, s[1])), (rows, 2 * cols)
        own.append(pl.BlockSpec((rows // nb, cols), own_map))
        out.append(pl.BlockSpec((rows // nb, cols), out_map))
        shards.append(jax.ShapeDtypeStruct(shard, F32))
    landed = [pl.BlockSpec((n_p, rows // nb, cols), lambda i, s: (0, i, 0)) for rows, cols in shapes]
    return pl.pallas_call(
        body, name=name,
        grid_spec=pltpu.PrefetchScalarGridSpec(
            num_scalar_prefetch=1, grid=(nb,), in_specs=own + landed + [ANY], out_specs=out),
        out_shape=shards,
        compiler_params=_params("parallel"),
    )(place, *halves, *slots, after)


def _join_halves(kinds, shards):
    @_plan(len(kinds))
    def plan(refs):
        x, y, c, _ = _place()
        return [(_half_view(kind, g, c), _half_view(kind, g, c), (x, y, 1 - c)) for kind, g in zip(kinds, refs)]

    return _Split("join_halves_" + "".join(map(str, kinds)), tuple(shards), plan, peers="sibling")


N_DEV = 8


def _sum_shared(pack, land, device):
    def body(d_ref, p_ref, l_ref, o_ref):
        me = d_ref[0]
        total = jnp.where(me == 0, p_ref[...], l_ref[0])
        for d in range(1, N_DEV):
            total = total + jnp.where(me == d, p_ref[...], l_ref[d])
        o_ref[...] = total

    return pl.pallas_call(
        body, name="sum_shared",
        grid_spec=pltpu.PrefetchScalarGridSpec(
            num_scalar_prefetch=1, grid=(1,),
            in_specs=[pl.BlockSpec(pack.shape, lambda i, d: (0, 0)), pl.BlockSpec(land.shape, lambda i, d: (0, 0, 0))],
            out_specs=pl.BlockSpec(pack.shape, lambda i, d: (0, 0))),
        out_shape=jax.ShapeDtypeStruct(pack.shape, F32),
    )(device, pack, land)


def _adamw(name, weights, after=None):
    shapes = [w.shape for w, _, _, _ in weights]
    nb = _grid_steps(shapes, 1 << 18)
    extra = [] if after is None else [after]
    n_in = 4 * len(weights) + len(extra)

    def body(*refs):
        for k in range(len(weights)):
            w_ref, g_ref, m_ref, v_ref = refs[4 * k:4 * k + 4]
            go_ref, d_ref, nm_ref, nv_ref = refs[n_in + 4 * k:n_in + 4 * k + 4]
            g = g_ref[...]
            go_ref[...] = g
            d_ref[...], nm_ref[...], nv_ref[...] = _adam_step(w_ref[...], g, m_ref[...], v_ref[...])

    blocks = [pl.BlockSpec((rows // nb, cols), lambda i: (i, 0)) for rows, cols in shapes for _ in range(4)]
    outs = pl.pallas_call(
        body, name=name, grid=(nb,), in_specs=blocks + [ANY] * len(extra), out_specs=blocks,
        out_shape=[jax.ShapeDtypeStruct(shape, F32) for shape in shapes for _ in range(4)],
        compiler_params=_params("parallel"),
    )(*[a for group in weights for a in group], *extra)
    return [outs[4 * k:4 * k + 4] for k in range(len(weights))]


def _adam_step(w, g, m, v):
    nm = ADAM_B1 * m + (1.0 - ADAM_B1) * g
    nv = ADAM_B2 * v + (1.0 - ADAM_B2) * jnp.square(g)
    m_hat = nm * (1.0 / (1.0 - ADAM_B1 ** ADAM_STEP))
    v_hat = nv * (1.0 / (1.0 - ADAM_B2 ** ADAM_STEP))
    return -ADAM_LR * (m_hat / (jnp.sqrt(v_hat) + ADAM_EPS) + ADAM_WD * w), nm, nv


def _adamw_small(tot, chip, weights, ms, vs, after):
    n, half = len(weights), D_MODEL // 2

    def body(chip_ref, tot_ref, *refs):
        ins, outs = refs[:3 * n], refs[3 * n + 1:]
        tot = tot_ref[...]
        conv_all = jnp.concatenate([tot[5:6, half:], tot[6:7, :half], tot[6:7, half:]], axis=0)
        conv = sum(jnp.where(chip_ref[0] == s, conv_all[:, s * LANES:(s + 1) * LANES], 0.0) for s in range(N_CHIPS))
        grads = [jnp.concatenate([tot[4:5, :half], tot[4:5, half:]], axis=0), tot[5:6, :half], conv,
                 tot[0:1], tot[1:2], tot[2:3], tot[3:4]]
        for k, g in enumerate(grads):
            delta, nm, nv = _adam_step(ins[k][...], g, ins[n + k][...], ins[2 * n + k][...])
            outs[k][...], outs[n + k][...], outs[2 * n + k][...], outs[3 * n + k][...] = g, delta, nm, nv
        outs[4 * n][...] = tot[7:8, 0:1]

    whole = lambda a: pl.BlockSpec(a.shape, lambda i, s: (0,) * a.ndim)
    arrays = (*weights, *ms, *vs)
    loss = jax.ShapeDtypeStruct((1, 1), F32)
    return pl.pallas_call(
        body, name="adamw_small",
        grid_spec=pltpu.PrefetchScalarGridSpec(
            num_scalar_prefetch=1, grid=(1,), in_specs=[whole(tot)] + [whole(a) for a in arrays] + [ANY],
            out_specs=[whole(a) for a in weights] * 4 + [whole(loss)]),
        out_shape=[jax.ShapeDtypeStruct(a.shape, F32) for a in weights] * 4 + [loss],
    )(chip, tot, *arrays, after)


def kernel(x, w_in, lb_logits, gate_norm_w, conv_w, w_out, ln1_g, ln1_b, w_ff1, w_ff2, ln2_g, ln2_b, loss_target, m_w_in, m_lb_logits, m_gate_norm_w, m_conv_w, m_w_out, m_ln1_g, m_ln1_b, m_w_ff1, m_w_ff2, m_ln2_g, m_ln2_b, v_w_in, v_lb_logits, v_gate_norm_w, v_conv_w, v_w_out, v_ln1_g, v_ln1_b, v_w_ff1, v_w_ff2, v_ln2_g, v_ln2_b):
    xs, tgt = x[0], loss_target[0]
    chip = 2 * lax.axis_index("x") + lax.axis_index("y")
    core = lax.axis_index("c").astype(jnp.int32).reshape(1)
    chip1 = chip.astype(jnp.int32).reshape(1)
    place = jnp.concatenate([chip1, core])

    conv4 = lax.dynamic_update_slice(jnp.zeros((N_CHIPS,) + conv_w.shape[1:], F32), conv_w, (chip, 0, 0))
    ici_in = _gather_w_in_over_ici(w_in[0], conv4)
    rest = (1, 2, 3)
    ici_rest = _gather_over_ici(rest, (_place_shard("place_w_out", w_out[0], chip1, False, after=ici_in.token),
                                       _place_shard("place_w_ff1", w_ff1[0], chip1, True, after=ici_in.token),
                                       _place_shard("place_w_ff2", w_ff2[0], chip1, False, after=ici_in.token)))
    near = ici_in.wait(ici_rest.token, (0, 1, 3, 4, 5), "_near")
    d2d_near = _gather_over_d2d((0,), near[:1], (0, 1), "_near")
    ici_in.arrays = (*d2d_near.arrays, near[1])
    far = ici_in.wait(d2d_near.token)
    d2d_far = _gather_over_d2d((0,), far[:1], (2,), "_far")
    d2d_near.arrays = d2d_far.arrays
    d2d_far.arrays = d2d_near.wait(d2d_far.token)
    wb_in, = d2d_far.wait(d2d_far.token)
    cv4 = far[1]
    conv_full = cv4.transpose(1, 0, 2).reshape(3, CONV_WIDTH)

    proj, bcu, xb, cat_c = _in_proj(xs, wb_in, conv_full, ici_rest.token)
    relay_rest = _relay_over_ici(rest, ici_rest.wait(proj))
    o, states = _hgrn_fwd(proj, lb_logits, relay_rest.token)
    d2d_rest = _gather_over_d2d(rest, relay_rest.wait(o))
    cat_h = _gate_fwd(proj, o, gate_norm_w, d2d_rest.token)
    wb_out, wb_ff1, wb_ff2 = d2d_rest.wait(cat_h)

    (h1b, r, da, dpre2b, dpre1, dcat, g_ln1_g, g_ln1_b, g_ln2_g, g_ln2_b, loss8, g_out_local) = _sublayers(
        cat_h, cat_c, xs, tgt, wb_out, wb_ff1, wb_ff2, ln1_g, ln1_b, ln2_g, ln2_b)

    names = ("w_in", "w_out", "w_ff1", "w_ff2")

    def named(prefix, kinds):
        return prefix + "".join("_" + names[k] for k in kinds)

    def add_halves(kinds, grads, lands):
        return _add_half(named("add_half", kinds), kinds, grads, lands, core)

    def sum_pieces(kinds, halves, lands, after):
        return _sum_pieces(named("sum_pieces", kinds), kinds, halves, lands, place, after)

    early = (1, 2, 3)
    g_ff2_local, dpc, g_conv = _dw_ff2(r, dpre2b, dcat, bcu, conv_full)
    swap_a = _swap_halves((1, 3), (g_out_local, g_ff2_local))
    g_ff1_local, do, dog, g_gnw = _dw_ff1(h1b, da, dcat, o, proj, gate_norm_w, swap_a.token)
    swap_b = _swap_halves((2,), (g_ff1_local,))
    swapped_a = swap_a.wait(swap_b.token)
    halves_a = add_halves((1, 3), swapped_a[:2], swapped_a[2:])
    swapped_b = swap_b.wait(halves_a[1])
    halves = (halves_a[0], *add_halves((2,), swapped_b[:1], swapped_b[1:]), halves_a[1])
    exch = _exchange_pieces(early, halves)
    dph, g_lbl = _hgrn_bwd(proj, do, states, lb_logits, exch.token)
    g_in_local, grad_x = _dw_in(xb, dph, dog, dpc, wb_in, dpre1, dph)

    late = (0,)
    swap = _swap_halves(late, (g_in_local,))
    exchanged = exch.wait(swap.token)
    pack = jnp.concatenate([
        g_ln1_g, g_ln1_b, g_ln2_g, g_ln2_b,
        jnp.concatenate([g_lbl[0:1], g_lbl[1:2]], axis=1),
        jnp.concatenate([g_gnw, g_conv[0:1]], axis=1),
        jnp.concatenate([g_conv[1:2], g_conv[2:3]], axis=1),
        jnp.concatenate([loss8[0:1], jnp.zeros((1, D_MODEL - LANES), F32)], axis=1)], axis=0)
    join_a = _join_halves((2,), sum_pieces((2,), exchanged[1:2], exchanged[4:5], swap.token))
    swapped = swap.wait(join_a.token)
    exch = _exchange_pieces(late, add_halves(late, swapped[:1], swapped[1:]), pack)
    join_b = _join_halves((1, 3), sum_pieces((1, 3), exchanged[0:3:2], exchanged[3:6:2], exch.token))
    g_w_ff1, = join_a.wait(join_b.token)
    (g_w_ff1, d_ff1, nm_ff1, nv_ff1), = _adamw("adamw_w_ff1", [(w_ff1[0], g_w_ff1, m_w_ff1[0], v_w_ff1[0])])
    g_w_out, g_w_ff2 = join_b.wait(d_ff1)
    (g_w_ff2, d_ff2, nm_ff2, nv_ff2), (g_w_out, d_out, nm_out, nv_out) = _adamw(
        "adamw_w_ff2_w_out", [(w_ff2[0], g_w_ff2, m_w_ff2[0], v_w_ff2[0]), (w_out[0], g_w_out, m_w_out[0], v_w_out[0])])
    shared = exch.wait(d_out, range(N_DEV - 1), "_pack")
    tot = _sum_shared(shared[2], shared[3], 2 * chip1 + core)
    small = ("lb_logits", "gate_norm_w", "conv_w", "ln1_g", "ln1_b", "ln2_g", "ln2_b")
    small_out = _adamw_small(
        tot, chip1, (lb_logits, gate_norm_w, conv_w[0], ln1_g, ln1_b, ln2_g, ln2_b),
        (m_lb_logits, m_gate_norm_w, m_conv_w[0], m_ln1_g, m_ln1_b, m_ln2_g, m_ln2_b),
        (v_lb_logits, v_gate_norm_w, v_conv_w[0], v_ln1_g, v_ln1_b, v_ln2_g, v_ln2_b), tot)
    exchanged = exch.wait(small_out[0])
    join = _join_halves(late, sum_pieces(late, exchanged[:1], exchanged[1:2], tot))
    g_w_in, = join.wait(join.token)
    (g_w_in, d_in, nm_in, nv_in), = _adamw("adamw_w_in", [(w_in[0], g_w_in, m_w_in[0], v_w_in[0])])
    loss = small_out[4 * len(small)][0, 0]

    def results(n_kind, large):
        out = dict(zip(small, small_out[n_kind * len(small):(n_kind + 1) * len(small)]))
        out["conv_w"] = out["conv_w"][None]
        out.update({name: a[None] for name, a in zip(("w_in", "w_out", "w_ff1", "w_ff2"), large)})
        return [out[name] for name in ("w_in", "lb_logits", "gate_norm_w", "conv_w", "w_out", "ln1_g", "ln1_b",
                                       "w_ff1", "w_ff2", "ln2_g", "ln2_b")]

    return (loss, grad_x[None], *results(0, (g_w_in, g_w_out, g_w_ff1, g_w_ff2)),
            *results(1, (d_in, d_out, d_ff1, d_ff2)), *results(2, (nm_in, nm_out, nm_ff1, nm_ff2)),
            *results(3, (nv_in, nv_out, nv_ff1, nv_ff2)))
```
